```python
import jax, jax.numpy as jnp
from jax import lax
import numpy as np

D_MODEL = 1024
BATCH = 8
SEQ = 4096
DEPTH = 1

ROPE_THETA = 500000.0
NORM_EPS = 1e-6
Q_BLOCK = 128

MLA_HEADS = 8
MLA_Q_RANK = 384
MLA_KV_RANK = 256
MLA_NOPE_DIM = 64
MLA_ROPE_DIM = 32
MLA_V_DIM = 64
MLA_WIDTH = MLA_HEADS * MLA_V_DIM

DIL_PATTERNS = ((128, 1), (512, 4), (2048, 16))
DIL_GROUPS = 3
DIL_HEADS_PER_GROUP = 8
DIL_HEADS = DIL_GROUPS * DIL_HEADS_PER_GROUP
DIL_HEAD_DIM = 64
DIL_ROPE_DIM = DIL_HEAD_DIM // 4
DIL_WIDTH = DIL_HEADS_PER_GROUP * DIL_HEAD_DIM

IN_SPLITS = (
    MLA_Q_RANK,
    MLA_KV_RANK,
    MLA_ROPE_DIM,
    3 * DIL_HEADS * DIL_HEAD_DIM,
    MLA_WIDTH,
    DIL_WIDTH,
    D_MODEL,
    D_MODEL,
)
IN_WIDTH = sum(IN_SPLITS)

kernel_name = "hybrid_mla_dilated_gated_parallel"


def rms_norm(x, g):
    xf = x.astype(jnp.float32)
    xf = xf * lax.rsqrt(jnp.mean(xf * xf, axis=-1, keepdims=True) + NORM_EPS)
    return (xf * g.astype(jnp.float32)).astype(x.dtype)


def rope_tables(positions, rot_dim):
    inv_freq = ROPE_THETA ** (-jnp.arange(0, rot_dim, 2, dtype=jnp.float32) / rot_dim)
    ang = positions.astype(jnp.float32)[..., None] * inv_freq
    return jnp.cos(ang), jnp.sin(ang)


def apply_rope(x, cos, sin):
    half = x.shape[-1] // 2
    xf = x.astype(jnp.float32)
    x1, x2 = xf[..., :half], xf[..., half:]
    return jnp.concatenate([x1 * cos - x2 * sin, x2 * cos + x1 * sin], axis=-1).astype(x.dtype)


def partial_rope(x, cos, sin, rot_dim):
    return jnp.concatenate([apply_rope(x[..., :rot_dim], cos, sin), x[..., rot_dim:]], axis=-1)


def mla_attention(c_q, c_kv, k_rope, q_norm_g, w_uq, kv_norm_g, w_ukv, cos, sin):
    B, S, _ = c_q.shape
    q = (rms_norm(c_q, q_norm_g) @ w_uq).reshape(B, S, MLA_HEADS, MLA_NOPE_DIM + MLA_ROPE_DIM)
    q_nope, q_rope = q[..., :MLA_NOPE_DIM], q[..., MLA_NOPE_DIM:]
    q_rope = apply_rope(q_rope, cos[:, :, None, :], sin[:, :, None, :])
    k_rope = apply_rope(k_rope, cos, sin)
    kv = (rms_norm(c_kv, kv_norm_g) @ w_ukv).reshape(B, S, MLA_HEADS, MLA_NOPE_DIM + MLA_V_DIM)
    k_nope, v = kv[..., :MLA_NOPE_DIM], kv[..., MLA_NOPE_DIM:]
    scale = (MLA_NOPE_DIM + MLA_ROPE_DIM) ** -0.5
    nb = S // Q_BLOCK
    qn_b = q_nope.reshape(B, nb, Q_BLOCK, MLA_HEADS, MLA_NOPE_DIM).transpose(1, 0, 2, 3, 4)
    qr_b = q_rope.reshape(B, nb, Q_BLOCK, MLA_HEADS, MLA_ROPE_DIM).transpose(1, 0, 2, 3, 4)
    key_idx = jnp.arange(S)

    def one_block(args):
        qn, qr, start = args
        s = (jnp.einsum('bqhd,bkhd->bhqk', qn, k_nope)
             + jnp.einsum('bqhr,bkr->bhqk', qr, k_rope)).astype(jnp.float32) * scale
        q_idx = start + jnp.arange(Q_BLOCK)
        s = jnp.where(key_idx[None, :] <= q_idx[:, None], s, -jnp.inf)
        p = jax.nn.softmax(s, axis=-1)
        return jnp.einsum('bhqk,bkhd->bqhd', p.astype(v.dtype), v)

    out = lax.map(one_block, (qn_b, qr_b, jnp.arange(nb) * Q_BLOCK))
    return out.transpose(1, 0, 2, 3, 4).reshape(B, S, MLA_WIDTH)


def dilated_group(q, k, v, window, dilation):
    B, S, H, Dh = q.shape
    L = S // dilation
    sub_window = window // dilation
    L_pad = -(-L // Q_BLOCK) * Q_BLOCK
    BD = B * dilation
    nb = L_pad // Q_BLOCK

    def to_sub(t):
        t = t.reshape(B, L, dilation, H, Dh).transpose(0, 2, 3, 1, 4).reshape(BD, H, L, Dh)
        return jnp.pad(t, ((0, 0), (0, 0), (0, L_pad - L), (0, 0)))

    qs, ks, vs = to_sub(q), to_sub(k), to_sub(v)
    qb = qs.reshape(BD, H, nb, Q_BLOCK, Dh)

    def band(t):
        tp = jnp.pad(t, ((0, 0), (0, 0), (Q_BLOCK, 0), (0, 0)))
        prev = tp[:, :, :L_pad].reshape(BD, H, nb, Q_BLOCK, Dh)
        cur = t.reshape(BD, H, nb, Q_BLOCK, Dh)
        return jnp.concatenate([prev, cur], axis=3)

    kb, vb = band(ks), band(vs)
    s = jnp.einsum('zhnqd,zhnkd->zhnqk', qb, kb).astype(jnp.float32) * (Dh ** -0.5)
    blk = jnp.arange(nb)[:, None, None] * Q_BLOCK
    qi = blk + jnp.arange(Q_BLOCK)[None, :, None]
    kj = blk - Q_BLOCK + jnp.arange(2 * Q_BLOCK)[None, None, :]
    dist = qi - kj
    mask = (dist >= 0) & (dist <= sub_window) & (kj >= 0)
    s = jnp.where(mask, s, -jnp.inf)
    m = jnp.max(s, axis=-1, keepdims=True)
    p = jnp.exp(s - m)
    denom = jnp.sum(p, axis=-1, keepdims=True)
    o = jnp.einsum('zhnqk,zhnkd->zhnqd', (p / denom).astype(v.dtype), vb)
    lse = (m + jnp.log(denom))[..., 0]
    o = o.reshape(BD, H, L_pad, Dh)[:, :, :L]
    o = o.reshape(B, dilation, H, L, Dh).transpose(0, 3, 1, 2, 4).reshape(B, S, H, Dh)
    lse = lse.reshape(BD, H, L_pad)[:, :, :L]
    lse = lse.reshape(B, dilation, H, L).transpose(0, 3, 1, 2).reshape(B, S, H)
    return o, lse


def dilated_attention(qkv, cos, sin):
    B, S, _ = qkv.shape
    qkv = qkv.reshape(B, S, 3, DIL_GROUPS, DIL_HEADS_PER_GROUP, DIL_HEAD_DIM)
    c, s_ = cos[:, :, None, None, :], sin[:, :, None, None, :]
    q = partial_rope(qkv[:, :, 0], c, s_, DIL_ROPE_DIM)
    k = partial_rope(qkv[:, :, 1], c, s_, DIL_ROPE_DIM)
    v = qkv[:, :, 2]
    outs, lses = [], []
    for g, (window, dilation) in enumerate(DIL_PATTERNS):
        o, lse = dilated_group(q[:, :, g], k[:, :, g], v[:, :, g], window, dilation)
        outs.append(o)
        lses.append(lse)
    alpha = jax.nn.softmax(jnp.stack(lses, axis=0), axis=0)
    out = jnp.sum(alpha[..., None].astype(v.dtype) * jnp.stack(outs, axis=0), axis=0)
    return out.reshape(B, S, DIL_WIDTH)


def _fwd_setup_inputs(seed: int = 0) -> dict:
    key = jax.random.key(seed)
    ks = jax.random.split(key, 13)
    f32 = jnp.float32

    def w(k, shape, fan_in):
        return jax.random.normal(k, shape, f32) * (fan_in ** -0.5)

    def gain(k, n):
        return 1.0 + 0.02 * jax.random.normal(k, (DEPTH, n), f32)

    x = jax.random.normal(ks[0], (BATCH, SEQ, D_MODEL), f32)
    start = jax.random.randint(ks[1], (BATCH, 1), 0, 4096, dtype=jnp.int32)
    positions = start + jnp.arange(SEQ, dtype=jnp.int32)[None, :]
    return {
        "x": x,
        "positions": positions,
        "pre_norm_g": gain(ks[2], D_MODEL),
        "w_in": w(ks[3], (DEPTH, D_MODEL, IN_WIDTH), D_MODEL),
        "q_norm_g": gain(ks[4], MLA_Q_RANK),
        "w_uq": w(ks[5], (DEPTH, MLA_Q_RANK, MLA_HEADS * (MLA_NOPE_DIM + MLA_ROPE_DIM)), MLA_Q_RANK),
        "kv_norm_g": gain(ks[6], MLA_KV_RANK),
        "w_ukv": w(ks[7], (DEPTH, MLA_KV_RANK, MLA_HEADS * (MLA_NOPE_DIM + MLA_V_DIM)), MLA_KV_RANK),
        "w_proj_mla": w(ks[8], (DEPTH, MLA_WIDTH, D_MODEL), MLA_WIDTH),
        "w_proj_dil": w(ks[9], (DEPTH, DIL_WIDTH, D_MODEL), DIL_WIDTH),
        "w_out": w(ks[10], (DEPTH, D_MODEL, D_MODEL), D_MODEL),
        "post_norm_g": gain(ks[11], D_MODEL),
    }


def _fwd_reference(x, positions, pre_norm_g, w_in, q_norm_g, w_uq, kv_norm_g, w_ukv,
              w_proj_mla, w_proj_dil, w_out, post_norm_g):
    cos_mla, sin_mla = rope_tables(positions, MLA_ROPE_DIM)
    cos_dil, sin_dil = rope_tables(positions, DIL_ROPE_DIM)
    split_at = [int(i) for i in np.cumsum(IN_SPLITS)[:-1]]
    for layer in range(DEPTH):
        h = rms_norm(x, pre_norm_g[layer])
        proj = h @ w_in[layer]
        c_q, c_kv, k_rope, qkv_dil, z_mla, z_dil, g_mla, g_dil = jnp.split(proj, split_at, axis=-1)
        y_mla = mla_attention(c_q, c_kv, k_rope, q_norm_g[layer], w_uq[layer],
                              kv_norm_g[layer], w_ukv[layer], cos_mla, sin_mla)
        y_dil = dilated_attention(qkv_dil, cos_dil, sin_dil)
        y_mla = (y_mla * jax.nn.silu(z_mla)) @ w_proj_mla[layer]
        y_dil = (y_dil * jax.nn.silu(z_dil)) @ w_proj_dil[layer]
        merged = jax.nn.sigmoid(g_mla) * y_mla + jax.nn.sigmoid(g_dil) * y_dil
        x = x + rms_norm(merged @ w_out[layer], post_norm_g[layer])
    return x


import jax as _jax
import jax.numpy as _jnp

TWIN_FORMAT = 'train_step'
FWD_PARAMS = ['x', 'positions', 'pre_norm_g', 'w_in', 'q_norm_g', 'w_uq', 'kv_norm_g', 'w_ukv', 'w_proj_mla', 'w_proj_dil', 'w_out', 'post_norm_g']
TWIN_WEIGHTS = ['pre_norm_g', 'w_in', 'q_norm_g', 'w_uq', 'kv_norm_g', 'w_ukv', 'w_proj_mla', 'w_proj_dil', 'w_out', 'post_norm_g']
TWIN_DIFF_INPUT = 'x'
TWIN_INPUTS = ['x', 'positions', 'pre_norm_g', 'w_in', 'q_norm_g', 'w_uq', 'kv_norm_g', 'w_ukv', 'w_proj_mla', 'w_proj_dil', 'w_out', 'post_norm_g', 'loss_target', 'm_pre_norm_g', 'm_w_in', 'm_q_norm_g', 'm_w_uq', 'm_kv_norm_g', 'm_w_ukv', 'm_w_proj_mla', 'm_w_proj_dil', 'm_w_out', 'm_post_norm_g', 'v_pre_norm_g', 'v_w_in', 'v_q_norm_g', 'v_w_uq', 'v_kv_norm_g', 'v_w_ukv', 'v_w_proj_mla', 'v_w_proj_dil', 'v_w_out', 'v_post_norm_g']
TWIN_OUTPUTS = ['loss', 'grad_x', 'grad_pre_norm_g', 'grad_w_in', 'grad_q_norm_g', 'grad_w_uq', 'grad_kv_norm_g', 'grad_w_ukv', 'grad_w_proj_mla', 'grad_w_proj_dil', 'grad_w_out', 'grad_post_norm_g', 'delta_pre_norm_g', 'delta_w_in', 'delta_q_norm_g', 'delta_w_uq', 'delta_kv_norm_g', 'delta_w_ukv', 'delta_w_proj_mla', 'delta_w_proj_dil', 'delta_w_out', 'delta_post_norm_g', 'new_m_pre_norm_g', 'new_m_w_in', 'new_m_q_norm_g', 'new_m_w_uq', 'new_m_kv_norm_g', 'new_m_w_ukv', 'new_m_w_proj_mla', 'new_m_w_proj_dil', 'new_m_w_out', 'new_m_post_norm_g', 'new_v_pre_norm_g', 'new_v_w_in', 'new_v_q_norm_g', 'new_v_w_uq', 'new_v_kv_norm_g', 'new_v_w_ukv', 'new_v_w_proj_mla', 'new_v_w_proj_dil', 'new_v_w_out', 'new_v_post_norm_g']
TWIN_LEAF_KINDS = {'loss': 'loss', 'grad_x': 'grad_x', 'grad_pre_norm_g': 'grad_w', 'grad_w_in': 'grad_w', 'grad_q_norm_g': 'grad_w', 'grad_w_uq': 'grad_w', 'grad_kv_norm_g': 'grad_w', 'grad_w_ukv': 'grad_w', 'grad_w_proj_mla': 'grad_w', 'grad_w_proj_dil': 'grad_w', 'grad_w_out': 'grad_w', 'grad_post_norm_g': 'grad_w', 'delta_pre_norm_g': 'delta_w', 'delta_w_in': 'delta_w', 'delta_q_norm_g': 'delta_w', 'delta_w_uq': 'delta_w', 'delta_kv_norm_g': 'delta_w', 'delta_w_ukv': 'delta_w', 'delta_w_proj_mla': 'delta_w', 'delta_w_proj_dil': 'delta_w', 'delta_w_out': 'delta_w', 'delta_post_norm_g': 'delta_w', 'new_m_pre_norm_g': 'new_m', 'new_m_w_in': 'new_m', 'new_m_q_norm_g': 'new_m', 'new_m_w_uq': 'new_m', 'new_m_kv_norm_g': 'new_m', 'new_m_w_ukv': 'new_m', 'new_m_w_proj_mla': 'new_m', 'new_m_w_proj_dil': 'new_m', 'new_m_w_out': 'new_m', 'new_m_post_norm_g': 'new_m', 'new_v_pre_norm_g': 'new_v', 'new_v_w_in': 'new_v', 'new_v_q_norm_g': 'new_v', 'new_v_w_uq': 'new_v', 'new_v_kv_norm_g': 'new_v', 'new_v_w_ukv': 'new_v', 'new_v_w_proj_mla': 'new_v', 'new_v_w_proj_dil': 'new_v', 'new_v_w_out': 'new_v', 'new_v_post_norm_g': 'new_v'}


def _forward(args):
    return _fwd_reference(*[args[k] for k in FWD_PARAMS])


def _output_shape():
    out = _jax.eval_shape(lambda: _forward(_fwd_setup_inputs(0)))
    return out.shape, out.dtype

N_MICROBATCH = 1
ADAM_LR = 0.001
ADAM_B1 = 0.9
ADAM_B2 = 0.999
ADAM_EPS = 1e-08
ADAM_WD = 0.01
ADAM_STEP = 10
PER_EXAMPLE_BATCH_AXIS = {'x': 0, 'positions': 0, 'loss_target': 0}
SHARED_INPUTS = []
_WEIGHT_DTYPES = {'pre_norm_g': _jnp.float32, 'w_in': _jnp.float32, 'q_norm_g': _jnp.float32, 'w_uq': _jnp.float32, 'kv_norm_g': _jnp.float32, 'w_ukv': _jnp.float32, 'w_proj_mla': _jnp.float32, 'w_proj_dil': _jnp.float32, 'w_out': _jnp.float32, 'post_norm_g': _jnp.float32}
MOMENT_SCALE = {'pre_norm_g': 5.182777e-01, 'w_in': 1.797889e-01, 'q_norm_g': 2.756399e-01, 'w_uq': 2.038051e-01, 'kv_norm_g': 6.281098e-01, 'w_ukv': 2.410579e-01, 'w_proj_mla': 1.912800e-01, 'w_proj_dil': 2.096063e-01, 'w_out': 2.689251e-01, 'post_norm_g': 3.200251e+01}


def _to_microbatches(a, axis):
    t = _jnp.moveaxis(a, axis, 0)
    t = t.reshape((N_MICROBATCH, t.shape[0] // N_MICROBATCH) + t.shape[1:])
    return _jnp.moveaxis(t, 1, axis + 1)


def setup_inputs(seed: int = 0) -> dict:
    inp = _fwd_setup_inputs(seed)
    key = _jax.random.fold_in(_jax.random.key(seed), 7919)
    shape, _ = _output_shape()
    out = dict(inp)
    out["loss_target"] = _jax.random.normal(_jax.random.fold_in(key, 0), shape, _jnp.float32)
    for i, name in enumerate(TWIN_WEIGHTS):
        w = inp[name].astype(_jnp.float32)
        if MOMENT_SCALE is None:
            s = _jnp.sqrt(_jnp.mean(_jnp.square(w)) + 1e-30)
        else:
            s = MOMENT_SCALE[name]
        km, kv = _jax.random.split(_jax.random.fold_in(key, i + 1))
        out[name] = w
        out["m_" + name] = s * _jax.random.normal(km, w.shape, _jnp.float32)
        out["v_" + name] = (s * s) * _jax.random.uniform(kv, w.shape, _jnp.float32, 0.5, 1.5)
    if N_MICROBATCH > 1:
        for name, axis in PER_EXAMPLE_BATCH_AXIS.items():
            out[name] = _to_microbatches(out[name], axis)
    return {'x': out['x'], 'positions': out['positions'], 'pre_norm_g': out['pre_norm_g'], 'w_in': out['w_in'], 'q_norm_g': out['q_norm_g'], 'w_uq': out['w_uq'], 'kv_norm_g': out['kv_norm_g'], 'w_ukv': out['w_ukv'], 'w_proj_mla': out['w_proj_mla'], 'w_proj_dil': out['w_proj_dil'], 'w_out': out['w_out'], 'post_norm_g': out['post_norm_g'], 'loss_target': out['loss_target'], 'm_pre_norm_g': out['m_pre_norm_g'], 'm_w_in': out['m_w_in'], 'm_q_norm_g': out['m_q_norm_g'], 'm_w_uq': out['m_w_uq'], 'm_kv_norm_g': out['m_kv_norm_g'], 'm_w_ukv': out['m_w_ukv'], 'm_w_proj_mla': out['m_w_proj_mla'], 'm_w_proj_dil': out['m_w_proj_dil'], 'm_w_out': out['m_w_out'], 'm_post_norm_g': out['m_post_norm_g'], 'v_pre_norm_g': out['v_pre_norm_g'], 'v_w_in': out['v_w_in'], 'v_q_norm_g': out['v_q_norm_g'], 'v_w_uq': out['v_w_uq'], 'v_kv_norm_g': out['v_kv_norm_g'], 'v_w_ukv': out['v_w_ukv'], 'v_w_proj_mla': out['v_w_proj_mla'], 'v_w_proj_dil': out['v_w_proj_dil'], 'v_w_out': out['v_w_out'], 'v_post_norm_g': out['v_post_norm_g']}


def _loss(weights, diff, rest, loss_target):
    with _jax.named_scope("forward"):
        args = {**rest, TWIN_DIFF_INPUT: diff, **{k: w.astype(_WEIGHT_DTYPES[k]) for k, w in weights.items()}}
        y = _forward(args)
    with _jax.named_scope("loss_head"):
        err = _jnp.square(y.astype(_jnp.float32) - loss_target)
        return 0.5 * _jnp.sum(_jnp.mean(err, axis=-1)) if err.ndim else 0.5 * err


def _adamw(w, g, m, v):
    m = ADAM_B1 * m + (1.0 - ADAM_B1) * g
    v = ADAM_B2 * v + (1.0 - ADAM_B2) * _jnp.square(g)
    m_hat = m / (1.0 - ADAM_B1 ** ADAM_STEP)
    v_hat = v / (1.0 - ADAM_B2 ** ADAM_STEP)
    delta = -ADAM_LR * (m_hat / (_jnp.sqrt(v_hat) + ADAM_EPS) + ADAM_WD * w)
    return delta, m, v


def reference(x, positions, pre_norm_g, w_in, q_norm_g, w_uq, kv_norm_g, w_ukv, w_proj_mla, w_proj_dil, w_out, post_norm_g, loss_target, m_pre_norm_g, m_w_in, m_q_norm_g, m_w_uq, m_kv_norm_g, m_w_ukv, m_w_proj_mla, m_w_proj_dil, m_w_out, m_post_norm_g, v_pre_norm_g, v_w_in, v_q_norm_g, v_w_uq, v_kv_norm_g, v_w_ukv, v_w_proj_mla, v_w_proj_dil, v_w_out, v_post_norm_g):
    given = dict(x=x, positions=positions, pre_norm_g=pre_norm_g, w_in=w_in, q_norm_g=q_norm_g, w_uq=w_uq, kv_norm_g=kv_norm_g, w_ukv=w_ukv, w_proj_mla=w_proj_mla, w_proj_dil=w_proj_dil, w_out=w_out, post_norm_g=post_norm_g, loss_target=loss_target, m_pre_norm_g=m_pre_norm_g, m_w_in=m_w_in, m_q_norm_g=m_q_norm_g, m_w_uq=m_w_uq, m_kv_norm_g=m_kv_norm_g, m_w_ukv=m_w_ukv, m_w_proj_mla=m_w_proj_mla, m_w_proj_dil=m_w_proj_dil, m_w_out=m_w_out, m_post_norm_g=m_post_norm_g, v_pre_norm_g=v_pre_norm_g, v_w_in=v_w_in, v_q_norm_g=v_q_norm_g, v_w_uq=v_w_uq, v_kv_norm_g=v_kv_norm_g, v_w_ukv=v_w_ukv, v_w_proj_mla=v_w_proj_mla, v_w_proj_dil=v_w_proj_dil, v_w_out=v_w_out, v_post_norm_g=v_post_norm_g)
    weights = {n: given[n] for n in TWIN_WEIGHTS}
    shared = {n: given[n] for n in SHARED_INPUTS}
    per_example = {n: given[n] for n in ['x', 'positions']}
    grad_fn = _jax.value_and_grad(_loss, argnums=(0, 1))

    def one_microbatch(ex, loss_target):
        ex = dict(ex)
        diff = ex.pop(TWIN_DIFF_INPUT)
        return grad_fn(weights, diff, {**shared, **ex}, loss_target)

    if N_MICROBATCH == 1:
        loss, (grad_w, grad_x) = one_microbatch(per_example, given["loss_target"])
    else:
        def body(carry, xs):
            loss_sum, grad_sum = carry
            l_k, (gw_k, gx_k) = one_microbatch(xs[0], xs[1])
            with _jax.named_scope("update"):
                return (loss_sum + l_k, _jax.tree.map(_jnp.add, grad_sum, gw_k)), gx_k

        init = (_jnp.zeros((), _jnp.float32), _jax.tree.map(_jnp.zeros_like, weights))
        (loss, grad_w), grad_x = _jax.lax.scan(body, init, (per_example, given["loss_target"]))
    with _jax.named_scope("update"):
        delta_w, new_m, new_v = {}, {}, {}
        for n in TWIN_WEIGHTS:
            delta_w[n], new_m[n], new_v[n] = _adamw(weights[n], grad_w[n], given["m_" + n], given["v_" + n])
    return (loss, grad_x, *[grad_w[n] for n in TWIN_WEIGHTS], *[delta_w[n] for n in TWIN_WEIGHTS],
            *[new_m[n] for n in TWIN_WEIGHTS], *[new_v[n] for n in TWIN_WEIGHTS])
```

```python
import functools

import jax
import jax.numpy as jnp
from jax import lax
from jax.experimental import pallas as pl
from jax.experimental.pallas import tpu as pltpu

F32 = jnp.float32
BF16 = jnp.bfloat16

SEQ = 4096
D_MODEL = 1024
EPS = 1e-6
ROPE_THETA = 500000.0
MLA_HEADS = 8
Q_RANK = 384
KV_RANK = 256
MLA_SCALE = 96.0 ** -0.5
MLA_ROPE_HALF = 16
DIL_DILATIONS = (1, 4, 16)
DIL_ROPE_HALF = 8
DIL_SCALE = 0.125
BAND = 128

N_GATE, N_Z, N_QKV, N_LAT = 2048, 1024, 4608, 768
COL_Z, COL_QKV, COL_LAT = 2048, 3072, 7680
N_PAD = 8448
IN_SPLITS = (384, 256, 32, 4608, 512, 512, 1024, 1024)
IN_WIDTH = 8352

SEG_SHAPES = ((1024, 2088), (384, 192), (256, 256), (512, 256), (512, 256), (256, 1024))
PACK_ROWS = 2752
HALF_ROWS = PACK_ROWS // 2
N_GAINS = 2688

ADAM_LR, ADAM_B1, ADAM_B2, ADAM_EPS, ADAM_WD, ADAM_STEP = 0.001, 0.9, 0.999, 1e-08, 0.01, 10

VMEM_LIMIT = 56 * 1024 * 1024
NEG = -1e30
MESH = pl.DeviceIdType.MESH


def _cparams(**kw):
    return pltpu.CompilerParams(vmem_limit_bytes=VMEM_LIMIT, **kw)


def _dot(a, b, dims):
    return lax.dot_general(a, b, (dims, ((), ())), preferred_element_type=F32)


def _nn(a, b):
    return _dot(a, b, ((1,), (0,)))


def _nt(a, b):
    return _dot(a, b, ((1,), (1,)))


def _tn(a, b):
    return _dot(a, b, ((0,), (0,)))


def _rope_fwd(x, c, s1, s2, half):
    return x * c + pltpu.roll(x, 128 - half, 1) * s1 + pltpu.roll(x, half, 1) * s2


def _rope_bwd(g, c, s1, s2, half):
    return g * c + pltpu.roll(g * s1, half, 1) + pltpu.roll(g * s2, 128 - half, 1)


def _sigmoid(x):
    return 1.0 / (1.0 + jnp.exp(-x))


def _matmul(a, b, mode, out_dtype, tm, tn, tk, name):
    if mode == "nn":
        (m, k), n = a.shape, b.shape[1]
        a_spec = pl.BlockSpec((tm, tk), lambda j, i, kk: (i, kk))
        b_spec = pl.BlockSpec((tk, tn), lambda j, i, kk: (kk, j))
        dot = _nn
    elif mode == "nt":
        (m, k), n = a.shape, b.shape[0]
        a_spec = pl.BlockSpec((tm, tk), lambda j, i, kk: (i, kk))
        b_spec = pl.BlockSpec((tn, tk), lambda j, i, kk: (j, kk))
        dot = _nt
    else:
        (k, m), n = a.shape, b.shape[1]
        a_spec = pl.BlockSpec((tk, tm), lambda j, i, kk: (kk, i))
        b_spec = pl.BlockSpec((tk, tn), lambda j, i, kk: (kk, j))
        dot = _tn
    assert m % tm == 0 and n % tn == 0 and k % tk == 0, (name, m, n, k, tm, tn, tk)
    nk = k // tk

    def body(a_ref, b_ref, o_ref, acc_ref):
        kk = pl.program_id(2)
        part = dot(a_ref[...], b_ref[...])

        @pl.when(kk == 0)
        def _():
            acc_ref[...] = part

        @pl.when(kk > 0)
        def _():
            acc_ref[...] += part

        @pl.when(kk == nk - 1)
        def _():
            o_ref[...] = acc_ref[...].astype(o_ref.dtype)

    return pl.pallas_call(
        body, name=name, grid=(n // tn, m // tm, nk),
        in_specs=[a_spec, b_spec],
        out_specs=pl.BlockSpec((tm, tn), lambda j, i, kk: (i, j)),
        out_shape=jax.ShapeDtypeStruct((m, n), out_dtype),
        scratch_shapes=[pltpu.VMEM((tm, tn), F32)],
        compiler_params=_cparams(),
    )(a, b)


def _prenorm_fwd(x, g):
    tm = 512

    def body(x_ref, g_ref, h_ref):
        xv = x_ref[...]
        r = lax.rsqrt(jnp.mean(xv * xv, axis=-1, keepdims=True) + EPS)
        h_ref[...] = (xv * r * g_ref[...]).astype(BF16)

    return pl.pallas_call(
        body, name="prenorm_fwd", grid=(SEQ // tm,),
        in_specs=[pl.BlockSpec((tm, D_MODEL), lambda i: (i, 0)), pl.BlockSpec((1, D_MODEL), lambda i: (0, 0))],
        out_specs=pl.BlockSpec((tm, D_MODEL), lambda i: (i, 0)),
        out_shape=jax.ShapeDtypeStruct((SEQ, D_MODEL), BF16),
    )(x, g)


def _prenorm_bwd(x, dh, dy, g):
    tm = 512

    def body(x_ref, dh_ref, dy_ref, g_ref, gx_ref, dg_ref):
        xv = x_ref[...]
        r = lax.rsqrt(jnp.mean(xv * xv, axis=-1, keepdims=True) + EPS)
        n = xv * r
        dhv = dh_ref[...]
        dn = dhv * g_ref[...]
        gx_ref[...] = dy_ref[...] + r * (dn - n * jnp.mean(dn * n, axis=-1, keepdims=True))
        part = jnp.sum(dhv * n, axis=0, keepdims=True)

        @pl.when(pl.program_id(0) == 0)
        def _():
            dg_ref[...] = part

        @pl.when(pl.program_id(0) > 0)
        def _():
            dg_ref[...] += part

    row = pl.BlockSpec((tm, D_MODEL), lambda i: (i, 0))
    vec = pl.BlockSpec((1, D_MODEL), lambda i: (0, 0))
    return pl.pallas_call(
        body, name="prenorm_bwd", grid=(SEQ // tm,),
        in_specs=[row, row, row, vec], out_specs=[row, vec],
        out_shape=[jax.ShapeDtypeStruct((SEQ, D_MODEL), F32), jax.ShapeDtypeStruct((1, D_MODEL), F32)],
        compiler_params=_cparams(),
    )(x, dh, dy, g)


def _mla_prep_fwd(p, qg, kvg, wuq, wk, wv, rc, rs1, rs2):
    tm = 512

    def body(lat_ref, qg_ref, kvg_ref, wuq_ref, wk_ref, wv_ref, c_ref, s1_ref, s2_ref,
             cqn_ref, ckvn_ref, q_ref, k_ref, v_ref):
        c, s1, s2 = c_ref[...], s1_ref[...], s2_ref[...]
        cq = lat_ref[:, 0:Q_RANK]
        r1 = lax.rsqrt(jnp.mean(cq * cq, axis=-1, keepdims=True) + EPS)
        cqn = (cq * r1 * qg_ref[...]).astype(BF16)
        cqn_ref[...] = cqn
        q = _nn(cqn, wuq_ref[...])
        for h in range(MLA_HEADS):
            sl = slice(h * 128, (h + 1) * 128)
            q_ref[:, sl] = _rope_fwd(q[:, sl], c, s1, s2, MLA_ROPE_HALF).astype(BF16)
        ckv = lat_ref[:, Q_RANK:Q_RANK + KV_RANK]
        r2 = lax.rsqrt(jnp.mean(ckv * ckv, axis=-1, keepdims=True) + EPS)
        ckvn = (ckv * r2 * kvg_ref[...]).astype(BF16)
        ckvn_ref[...] = ckvn
        krr = _rope_fwd(lat_ref[:, Q_RANK + KV_RANK:N_LAT], c, s1, s2, MLA_ROPE_HALF)
        kn = _nn(ckvn, wk_ref[...])
        for h in range(MLA_HEADS):
            sl = slice(h * 128, (h + 1) * 128)
            k_ref[:, sl] = (kn[:, sl] + krr).astype(BF16)
        v_ref[...] = _nn(ckvn, wv_ref[...]).astype(BF16)

    def full(shape):
        return pl.BlockSpec(shape, lambda i: (0, 0))

    def rows(w):
        return pl.BlockSpec((tm, w), lambda i: (i, 0))

    return pl.pallas_call(
        body, name="mla_prep_fwd", grid=(SEQ // tm,),
        in_specs=[pl.BlockSpec((tm, N_LAT), lambda i: (i, COL_LAT // N_LAT)),
                  full((1, Q_RANK)), full((1, KV_RANK)), full((Q_RANK, 1024)), full((KV_RANK, 1024)),
                  full((KV_RANK, 512)), rows(128), rows(128), rows(128)],
        out_specs=[rows(Q_RANK), rows(KV_RANK), rows(1024), rows(1024), rows(512)],
        out_shape=[jax.ShapeDtypeStruct((SEQ, Q_RANK), BF16), jax.ShapeDtypeStruct((SEQ, KV_RANK), BF16),
                   jax.ShapeDtypeStruct((SEQ, 1024), BF16), jax.ShapeDtypeStruct((SEQ, 1024), BF16),
                   jax.ShapeDtypeStruct((SEQ, 512), BF16)],
        compiler_params=_cparams(),
    )(p, qg, kvg, wuq, wk, wv, rc, rs1, rs2)


def _mla_prep_bwd(dp_in, p, dq, dk, dv, qg, kvg, wuq, wk, wv, rc, rs1, rs2):
    tm = 512

    def body(dp_any, lat_ref, dq_ref, dk_ref, dv_ref, qg_ref, kvg_ref, wuq_ref, wk_ref, wv_ref,
             c_ref, s1_ref, s2_ref, dp_ref, dqb_ref, dkb_ref, dvb_ref, dgq_ref, dgkv_ref):
        del dp_any
        c, s1, s2 = c_ref[...], s1_ref[...], s2_ref[...]
        lane = lax.broadcasted_iota(jnp.int32, (tm, 128), 1)
        dkr = jnp.zeros((tm, 128), F32)
        for h in range(MLA_HEADS):
            sl = slice(h * 128, (h + 1) * 128)
            dqb_ref[:, sl] = _rope_bwd(dq_ref[:, sl], c, s1, s2, MLA_ROPE_HALF).astype(BF16)
            dkh = dk_ref[:, sl]
            dkr = dkr + dkh
            dkb_ref[:, sl] = jnp.where(lane < 64, dkh, 0.0).astype(BF16)
        dkr = jnp.where((lane >= 64) & (lane < 96), dkr, 0.0)
        dkr = _rope_bwd(dkr, c, s1, s2, MLA_ROPE_HALF)
        dvb = dv_ref[...].astype(BF16)
        dvb_ref[...] = dvb

        cq = lat_ref[:, 0:Q_RANK]
        r1 = lax.rsqrt(jnp.mean(cq * cq, axis=-1, keepdims=True) + EPS)
        n1 = cq * r1
        dcqn = _nt(dqb_ref[...], wuq_ref[...])
        dn1 = dcqn * qg_ref[...]
        dcq = r1 * (dn1 - n1 * jnp.mean(dn1 * n1, axis=-1, keepdims=True))
        pq = jnp.sum(dcqn * n1, axis=0, keepdims=True)

        ckv = lat_ref[:, Q_RANK:Q_RANK + KV_RANK]
        r2 = lax.rsqrt(jnp.mean(ckv * ckv, axis=-1, keepdims=True) + EPS)
        n2 = ckv * r2
        dckvn = _nt(dkb_ref[...], wk_ref[...]) + _nt(dvb, wv_ref[...])
        dn2 = dckvn * kvg_ref[...]
        dckv = r2 * (dn2 - n2 * jnp.mean(dn2 * n2, axis=-1, keepdims=True))
        pkv = jnp.sum(dckvn * n2, axis=0, keepdims=True)

        dp_ref[:, 0:Q_RANK] = dcq.astype(BF16)
        dp_ref[:, Q_RANK:Q_RANK + KV_RANK] = dckv.astype(BF16)
        dp_ref[:, Q_RANK + KV_RANK:N_LAT] = dkr.astype(BF16)

        @pl.when(pl.program_id(0) == 0)
        def _():
            dgq_ref[...] = pq
            dgkv_ref[...] = pkv

        @pl.when(pl.program_id(0) > 0)
        def _():
            dgq_ref[...] += pq
            dgkv_ref[...] += pkv

    def full(shape):
        return pl.BlockSpec(shape, lambda i: (0, 0))

    def rows(w):
        return pl.BlockSpec((tm, w), lambda i: (i, 0))

    lat = pl.BlockSpec((tm, N_LAT), lambda i: (i, COL_LAT // N_LAT))
    outs = pl.pallas_call(
        body, name="mla_prep_bwd", grid=(SEQ // tm,),
        in_specs=[pl.BlockSpec(memory_space=pl.ANY), lat, rows(1024), rows(1024), rows(512),
                  full((1, Q_RANK)), full((1, KV_RANK)), full((Q_RANK, 1024)), full((KV_RANK, 1024)),
                  full((KV_RANK, 512)), rows(128), rows(128), rows(128)],
        out_specs=[lat, rows(1024), rows(1024), rows(512), full((1, Q_RANK)), full((1, KV_RANK))],
        out_shape=[jax.ShapeDtypeStruct((SEQ, N_PAD), BF16), jax.ShapeDtypeStruct((SEQ, 1024), BF16),
                   jax.ShapeDtypeStruct((SEQ, 1024), BF16), jax.ShapeDtypeStruct((SEQ, 512), BF16),
                   jax.ShapeDtypeStruct((1, Q_RANK), F32), jax.ShapeDtypeStruct((1, KV_RANK), F32)],
        input_output_aliases={0: 0},
        compiler_params=_cparams(),
    )(dp_in, p, dq, dk, dv, qg, kvg, wuq, wk, wv, rc, rs1, rs2)
    return outs


FLASH_T = 512


def _head_half(shape, hh):
    lane = lax.broadcasted_iota(jnp.int32, shape, 1)
    return (lane < 64) if hh == 0 else (lane >= 64)


def _causal_keep(t):
    row = lax.broadcasted_iota(jnp.int32, (t, t), 0)
    col = lax.broadcasted_iota(jnp.int32, (t, t), 1)
    return row >= col


def _mla_flash_fwd(q, k, v):
    t = FLASH_T
    nb = SEQ // t

    def body(q_ref, k_ref, v_ref, o_ref, lse_ref, m_scr, l_scr, acc_scr):
        i, kb = pl.program_id(1), pl.program_id(2)

        @pl.when(kb == 0)
        def _():
            m_scr[...] = jnp.full_like(m_scr, NEG)
            l_scr[...] = jnp.zeros_like(l_scr)
            acc_scr[...] = jnp.zeros_like(acc_scr)

        @pl.when(kb <= i)
        def _():
            keep = _causal_keep(t) | (kb < i)
            vv = v_ref[...]
            for hh in range(2):
                sl = slice(hh * 128, (hh + 1) * 128)
                s = _nt(q_ref[:, sl], k_ref[:, sl]) * MLA_SCALE
                s = jnp.where(keep, s, NEG)
                m_prev = m_scr[hh]
                m_new = jnp.maximum(m_prev, jnp.max(s, axis=-1, keepdims=True))
                pr = jnp.exp(s - m_new)
                alpha = jnp.exp(m_prev - m_new)
                l_scr[hh] = alpha * l_scr[hh] + jnp.sum(pr, axis=-1, keepdims=True)
                acc_scr[hh] = alpha * acc_scr[hh] + _nn(pr.astype(BF16), vv)
                m_scr[hh] = m_new

        @pl.when(kb == nb - 1)
        def _():
            o0 = acc_scr[0] / l_scr[0]
            o1 = acc_scr[1] / l_scr[1]
            o_ref[...] = jnp.where(_head_half((t, 128), 0), o0, o1)
            for hh in range(2):
                lse = m_scr[hh] + jnp.log(l_scr[hh])
                lse_ref[:, hh * 128:(hh + 1) * 128] = jnp.broadcast_to(lse, (t, 128))

    return pl.pallas_call(
        body, name="mla_flash_fwd", grid=(4, nb, nb),
        in_specs=[pl.BlockSpec((t, 256), lambda j, i, kb: (i, j)),
                  pl.BlockSpec((t, 256), lambda j, i, kb: (jnp.minimum(kb, i), j)),
                  pl.BlockSpec((t, 128), lambda j, i, kb: (jnp.minimum(kb, i), j))],
        out_specs=[pl.BlockSpec((t, 128), lambda j, i, kb: (i, j)),
                   pl.BlockSpec((t, 256), lambda j, i, kb: (i, j))],
        out_shape=[jax.ShapeDtypeStruct((SEQ, 512), F32), jax.ShapeDtypeStruct((SEQ, 1024), F32)],
        scratch_shapes=[pltpu.VMEM((2, t, 1), F32), pltpu.VMEM((2, t, 1), F32), pltpu.VMEM((2, t, 128), F32)],
        compiler_params=_cparams(),
    )(q, k, v)


def _mla_flash_bwd(q, k, v, o, do, lse):
    t = FLASH_T
    nb = SEQ // t

    def body(q_ref, k_ref, v_ref, o_ref, do_ref, lse_ref, dq_ref, dk_ref, dv_ref, dk_scr, dv_scr):
        kb, i = pl.program_id(1), pl.program_id(2)

        @pl.when((kb == 0) & (i == 0))
        def _():
            dq_ref[...] = jnp.zeros_like(dq_ref)

        @pl.when(i == 0)
        def _():
            dk_scr[...] = jnp.zeros_like(dk_scr)
            dv_scr[...] = jnp.zeros_like(dv_scr)

        @pl.when(i >= kb)
        def _():
            keep = _causal_keep(t) | (i > kb)
            vv = v_ref[...]
            ov = o_ref[...]
            dov = do_ref[...]
            rows = pl.ds(pl.multiple_of(i * t, t), t)
            for hh in range(2):
                sl = slice(hh * 128, (hh + 1) * 128)
                qh, kh = q_ref[:, sl], k_ref[:, sl]
                s = _nt(qh, kh) * MLA_SCALE
                pr = jnp.exp(jnp.where(keep, s, NEG) - lse_ref[:, hh * 128:hh * 128 + 1])
                dom = jnp.where(_head_half((t, 128), hh), dov, 0.0)
                domb = dom.astype(BF16)
                dv_scr[...] += _tn(pr.astype(BF16), domb)
                dpr = _nt(domb, vv)
                delta = jnp.sum(dom * ov, axis=-1, keepdims=True)
                ds = (pr * (dpr - delta) * MLA_SCALE).astype(BF16)
                dq_ref[rows, sl] += _nn(ds, kh)
                dk_scr[hh] += _tn(ds, qh)

        @pl.when(i == nb - 1)
        def _():
            dk_ref[:, 0:128] = dk_scr[0]
            dk_ref[:, 128:256] = dk_scr[1]
            dv_ref[...] = dv_scr[...]

    qi = lambda j, kb, i: (jnp.maximum(i, kb), j)
    ki = lambda j, kb, i: (kb, j)
    return pl.pallas_call(
        body, name="mla_flash_bwd", grid=(4, nb, nb),
        in_specs=[pl.BlockSpec((t, 256), qi), pl.BlockSpec((t, 256), ki), pl.BlockSpec((t, 128), ki),
                  pl.BlockSpec((t, 128), qi), pl.BlockSpec((t, 128), qi), pl.BlockSpec((t, 256), qi)],
        out_specs=[pl.BlockSpec((SEQ, 256), lambda j, kb, i: (0, j)), pl.BlockSpec((t, 256), ki),
                   pl.BlockSpec((t, 128), ki)],
        out_shape=[jax.ShapeDtypeStruct((SEQ, 1024), F32), jax.ShapeDtypeStruct((SEQ, 1024), F32),
                   jax.ShapeDtypeStruct((SEQ, 512), F32)],
        scratch_shapes=[pltpu.VMEM((2, t, 128), F32), pltpu.VMEM((t, 128), F32)],
        compiler_params=_cparams(),
    )(q, k, v, o, do, lse)


def _strided(start, size, d):
    return pl.ds(start, size) if d == 1 else pl.ds(start, size, stride=d)


def _dil_prep_fwd(p, rc, rs1, rs2, g):
    d = DIL_DILATIONS[g]
    sub_len = SEQ // d
    ch = min(sub_len, 512)

    def body(p_ref, c_ref, s1_ref, s2_ref, o_ref):
        tq = pl.program_id(0)
        is_v = tq == 2
        mult = jnp.where(tq == 0, DIL_SCALE, 1.0).astype(F32)
        for r in range(d):
            for c0 in range(0, sub_len, ch):
                rows = _strided(r + c0 * d, ch, d)
                xv = p_ref[rows, :]
                roped = _rope_fwd(xv, c_ref[rows, :], s1_ref[rows, :], s2_ref[rows, :], DIL_ROPE_HALF)
                o_ref[0, r * sub_len + c0:r * sub_len + c0 + ch, :] = (jnp.where(is_v, xv, roped) * mult).astype(BF16)

    tab = pl.BlockSpec((SEQ, 128), lambda tq, pr: (0, 0))
    return pl.pallas_call(
        body, name=f"dil_prep_fwd_g{g}", grid=(3, 4),
        in_specs=[pl.BlockSpec((SEQ, 128), lambda tq, pr: (0, COL_QKV // 128 + (tq * 3 + g) * 4 + pr)), tab, tab, tab],
        out_specs=pl.BlockSpec((1, SEQ, 128), lambda tq, pr: (tq, 0, pr)),
        out_shape=jax.ShapeDtypeStruct((3, SEQ, 512), BF16),
        compiler_params=_cparams(),
    )(p, rc, rs1, rs2)


def _band_masks():
    row = lax.broadcasted_iota(jnp.int32, (BAND, BAND), 0)
    col = lax.broadcasted_iota(jnp.int32, (BAND, BAND), 1)
    return row >= col, col >= row


def _dil_attn_fwd(qkv, g):
    d = DIL_DILATIONS[g]
    nbs = SEQ // d // BAND
    nblk = SEQ // BAND

    def body(q_ref, k_ref, v_ref, o_ref, l_ref):
        keep_c, keep_p = _band_masks()
        half0 = _head_half((BAND, 128), 0)

        def step(i, carry):
            cur = pl.ds(pl.multiple_of(i * BAND, BAND), BAND)
            prv = pl.ds(pl.multiple_of(jnp.maximum(i - 1, 0) * BAND, BAND), BAND)
            has_prev = (i % nbs) != 0
            qv = q_ref[0, cur, :]
            kc, kp = k_ref[0, cur, :], k_ref[0, prv, :]
            vc, vp = v_ref[0, cur, :], v_ref[0, prv, :]
            outs, lses = [], []
            for hh in range(2):
                qm = jnp.where(_head_half((BAND, 128), hh), qv, jnp.zeros_like(qv))
                sc = jnp.where(keep_c, _nt(qm, kc), NEG)
                sp = jnp.where(keep_p & has_prev, _nt(qm, kp), NEG)
                m = jnp.maximum(jnp.max(sc, axis=-1, keepdims=True), jnp.max(sp, axis=-1, keepdims=True))
                pc, pp = jnp.exp(sc - m), jnp.exp(sp - m)
                den = jnp.sum(pc, axis=-1, keepdims=True) + jnp.sum(pp, axis=-1, keepdims=True)
                o = (_nn(pc.astype(BF16), vc) + _nn(pp.astype(BF16), vp)) / den
                outs.append(o)
                lses.append(jnp.broadcast_to(m + jnp.log(den), (BAND, 128)))
            tok = _strided((i % nbs) * BAND * d + i // nbs, BAND, d)
            o_ref[tok, :] = jnp.where(half0, outs[0], outs[1])
            l_ref[tok, :] = jnp.where(half0, lses[0], lses[1])
            return carry

        lax.fori_loop(0, nblk, step, 0)

    def inp(tq):
        return pl.BlockSpec((1, SEQ, 128), lambda pr: (tq, 0, pr))

    out = pl.BlockSpec((SEQ, 128), lambda pr: (0, pr))
    return pl.pallas_call(
        body, name=f"dil_attn_fwd_g{g}", grid=(4,),
        in_specs=[inp(0), inp(1), inp(2)], out_specs=[out, out],
        out_shape=[jax.ShapeDtypeStruct((SEQ, 512), F32), jax.ShapeDtypeStruct((SEQ, 512), F32)],
        compiler_params=_cparams(),
    )(qkv, qkv, qkv)


def _dil_attn_bwd(qkv, dyd, yd, lse_all, g):
    d = DIL_DILATIONS[g]
    sub_len = SEQ // d
    nbs = sub_len // BAND
    nblk = SEQ // BAND

    def body(q_ref, k_ref, v_ref, do_ref, y_ref, l_ref, out_ref, dk_scr, dv_scr):
        keep_c, keep_p = _band_masks()
        dk_scr[...] = jnp.zeros_like(dk_scr)
        dv_scr[...] = jnp.zeros_like(dv_scr)

        def step(i, carry):
            cur = pl.ds(pl.multiple_of(i * BAND, BAND), BAND)
            prv = pl.ds(pl.multiple_of(jnp.maximum(i - 1, 0) * BAND, BAND), BAND)
            has_prev = (i % nbs) != 0
            tok = _strided((i % nbs) * BAND * d + i // nbs, BAND, d)
            qv = q_ref[0, cur, :]
            kc, kp = k_ref[0, cur, :], k_ref[0, prv, :]
            vc, vp = v_ref[0, cur, :], v_ref[0, prv, :]
            dov, yv, lv = do_ref[tok, :], y_ref[tok, :], l_ref[tok, :]
            dq = jnp.zeros((BAND, 128), F32)
            dkc = jnp.zeros((BAND, 128), F32)
            dkp = jnp.zeros((BAND, 128), F32)
            dvc = jnp.zeros((BAND, 128), F32)
            dvp = jnp.zeros((BAND, 128), F32)
            for hh in range(2):
                half = _head_half((BAND, 128), hh)
                qm = jnp.where(half, qv, jnp.zeros_like(qv))
                lcol = jnp.max(jnp.where(half, lv, NEG), axis=-1, keepdims=True)
                pc = jnp.exp(jnp.where(keep_c, _nt(qm, kc), NEG) - lcol)
                pp = jnp.exp(jnp.where(keep_p & has_prev, _nt(qm, kp), NEG) - lcol)
                dom = jnp.where(half, dov, 0.0)
                domb = dom.astype(BF16)
                delta = jnp.sum(dom * yv, axis=-1, keepdims=True)
                dsc = (pc * (_nt(domb, vc) - delta)).astype(BF16)
                dsp = (pp * (_nt(domb, vp) - delta)).astype(BF16)
                dvc = dvc + _tn(pc.astype(BF16), domb)
                dvp = dvp + _tn(pp.astype(BF16), domb)
                dq = dq + jnp.where(half, _nn(dsc, kc) + _nn(dsp, kp), 0.0)
                dkc = dkc + jnp.where(half, _tn(dsc, qv), 0.0)
                dkp = dkp + jnp.where(half, _tn(dsp, qv), 0.0)
            out_ref[pl.ds(0, 1), tok, :] = dq[None]
            dk_scr[cur, :] += dkc
            dk_scr[prv, :] += dkp
            dv_scr[cur, :] += dvc
            dv_scr[prv, :] += dvp
            return carry

        lax.fori_loop(0, nblk, step, 0)
        for r in range(d):
            rows = _strided(r, sub_len, d)
            out_ref[pl.ds(1, 1), rows, :] = dk_scr[r * sub_len:(r + 1) * sub_len, :][None]
            out_ref[pl.ds(2, 1), rows, :] = dv_scr[r * sub_len:(r + 1) * sub_len, :][None]

    def inp(tq):
        return pl.BlockSpec((1, SEQ, 128), lambda pr: (tq, 0, pr))

    tok_spec = pl.BlockSpec((SEQ, 128), lambda pr: (0, pr))
    return pl.pallas_call(
        body, name=f"dil_attn_bwd_g{g}", grid=(4,),
        in_specs=[inp(0), inp(1), inp(2), tok_spec, tok_spec, tok_spec],
        out_specs=pl.BlockSpec((3, SEQ, 128), lambda pr: (0, 0, pr)),
        out_shape=jax.ShapeDtypeStruct((3, SEQ, 512), F32),
        scratch_shapes=[pltpu.VMEM((SEQ, 128), F32), pltpu.VMEM((SEQ, 128), F32)],
        compiler_params=_cparams(),
    )(qkv, qkv, qkv, dyd, yd, lse_all)


def _dil_prep_bwd(dp_in, dqkv, rc, rs1, rs2, g):
    tm = 1024

    def body(dp_any, g_ref, c_ref, s1_ref, s2_ref, dp_ref):
        del dp_any
        tq = pl.program_id(0)
        gv = g_ref[0]
        mult = jnp.where(tq == 0, DIL_SCALE, 1.0).astype(F32)
        roped = _rope_bwd(gv, c_ref[...], s1_ref[...], s2_ref[...], DIL_ROPE_HALF)
        dp_ref[...] = (jnp.where(tq == 2, gv, roped) * mult).astype(BF16)

    tab = pl.BlockSpec((tm, 128), lambda tq, pr, i: (i, 0))
    return pl.pallas_call(
        body, name=f"dil_prep_bwd_g{g}", grid=(3, 4, SEQ // tm),
        in_specs=[pl.BlockSpec(memory_space=pl.ANY),
                  pl.BlockSpec((1, tm, 128), lambda tq, pr, i: (tq, i, pr)), tab, tab, tab],
        out_specs=pl.BlockSpec((tm, 128), lambda tq, pr, i: (i, COL_QKV // 128 + (tq * 3 + g) * 4 + pr)),
        out_shape=jax.ShapeDtypeStruct((SEQ, N_PAD), BF16),
        input_output_aliases={0: 0},
    )(dp_in, dqkv, rc, rs1, rs2)


TAIL_T = 128


def _tail(p, ya, o_g, l_g, x, target, wpm, wpd, wout, post_g):
    tm = TAIL_T

    def body(pgz_ref, ya_ref, o0_ref, o1_ref, o2_ref, l0_ref, l1_ref, l2_ref, x_ref, t_ref,
             wpm_ref, wpd_ref, wout_ref, pg_ref,
             dp_ref, dy_ref, mg_ref, dt_ref, ua_ref, dpa_ref, ud_ref, dpd_ref, dya_ref, dyd_ref,
             yd_ref, lse_ref, loss_ref, dgp_ref):
        l0, l1, l2 = l0_ref[...], l1_ref[...], l2_ref[...]
        mx = jnp.maximum(jnp.maximum(l0, l1), l2)
        e0, e1, e2 = jnp.exp(l0 - mx), jnp.exp(l1 - mx), jnp.exp(l2 - mx)
        den = e0 + e1 + e2
        yd = (e0 * o0_ref[...] + e1 * o1_ref[...] + e2 * o2_ref[...]) / den
        yd_ref[...] = yd
        lse_ref[...] = mx + jnp.log(den)
        ya = ya_ref[...]

        gm, gd = pgz_ref[:, 0:1024], pgz_ref[:, 1024:2048]
        zm, zd = pgz_ref[:, 2048:2560], pgz_ref[:, 2560:3072]
        szm, szd = _sigmoid(zm), _sigmoid(zd)
        sm, sd = zm * szm, zd * szd
        ua = (ya * sm).astype(BF16)
        ud = (yd * sd).astype(BF16)
        ua_ref[...] = ua
        ud_ref[...] = ud
        pa = _nn(ua, wpm_ref[...])
        pd = _nn(ud, wpd_ref[...])
        sgm, sgd = _sigmoid(gm), _sigmoid(gd)
        mg = (sgm * pa + sgd * pd).astype(BF16)
        mg_ref[...] = mg
        t = _nn(mg, wout_ref[...])
        r3 = lax.rsqrt(jnp.mean(t * t, axis=-1, keepdims=True) + EPS)
        n = t * r3
        pg = pg_ref[...]
        err = x_ref[...] + n * pg - t_ref[...]
        lpart = jnp.sum(err * err, axis=0, keepdims=True)

        dy = err * (1.0 / D_MODEL)
        dy_ref[...] = dy
        gpart = jnp.sum(dy * n, axis=0, keepdims=True)
        dn = dy * pg
        dt = (r3 * (dn - n * jnp.mean(dn * n, axis=-1, keepdims=True))).astype(BF16)
        dt_ref[...] = dt
        dmg = _nt(dt, wout_ref[...])
        dpa = (dmg * sgm).astype(BF16)
        dpd = (dmg * sgd).astype(BF16)
        dpa_ref[...] = dpa
        dpd_ref[...] = dpd
        dp_ref[:, 0:1024] = (dmg * pa * sgm * (1.0 - sgm)).astype(BF16)
        dp_ref[:, 1024:2048] = (dmg * pd * sgd * (1.0 - sgd)).astype(BF16)
        dua = _nt(dpa, wpm_ref[...])
        dud = _nt(dpd, wpd_ref[...])
        dya_ref[...] = dua * sm
        dyd_ref[...] = dud * sd
        dp_ref[:, 2048:2560] = (dua * ya * szm * (1.0 + zm * (1.0 - szm))).astype(BF16)
        dp_ref[:, 2560:3072] = (dud * yd * szd * (1.0 + zd * (1.0 - szd))).astype(BF16)

        @pl.when(pl.program_id(0) == 0)
        def _():
            loss_ref[...] = lpart
            dgp_ref[...] = gpart

        @pl.when(pl.program_id(0) > 0)
        def _():
            loss_ref[...] += lpart
            dgp_ref[...] += gpart

    def rows(w):
        return pl.BlockSpec((tm, w), lambda i: (i, 0))

    def full(shape):
        return pl.BlockSpec(shape, lambda i: (0, 0))

    def sds(w, dt):
        return jax.ShapeDtypeStruct((SEQ, w), dt)

    return pl.pallas_call(
        body, name="tail", grid=(SEQ // tm,),
        in_specs=[rows(3072), rows(512), rows(512), rows(512), rows(512), rows(512), rows(512), rows(512),
                  rows(1024), rows(1024), full((512, 1024)), full((512, 1024)), full((1024, 1024)), full((1, 1024))],
        out_specs=[rows(3072), rows(1024), rows(1024), rows(1024), rows(512), rows(1024), rows(512), rows(1024),
                   rows(512), rows(512), rows(512), rows(512), full((1, 1024)), full((1, 1024))],
        out_shape=[sds(N_PAD, BF16), sds(1024, F32), sds(1024, BF16), sds(1024, BF16), sds(512, BF16),
                   sds(1024, BF16), sds(512, BF16), sds(1024, BF16), sds(512, F32), sds(512, F32),
                   sds(512, F32), sds(512, F32),
                   jax.ShapeDtypeStruct((1, 1024), F32), jax.ShapeDtypeStruct((1, 1024), F32)],
        compiler_params=_cparams(),
    )(p, ya, o_g[0], o_g[1], o_g[2], l_g[0], l_g[1], l_g[2], x, target, wpm, wpd, wout, post_g)


def _sum_parts(parts, name):
    n, r, w = parts.shape
    tr = 32 if r % 32 == 0 else r

    def body(p_ref, o_ref):
        acc = p_ref[0].astype(F32)
        for s in range(1, n):
            acc = acc + p_ref[s].astype(F32)
        o_ref[...] = acc

    return pl.pallas_call(
        body, name=name, grid=(r // tr,),
        in_specs=[pl.BlockSpec((n, tr, w), lambda i: (0, i, 0))],
        out_specs=pl.BlockSpec((tr, w), lambda i: (i, 0)),
        out_shape=jax.ShapeDtypeStruct((r, w), F32),
    )(parts)


def _adamw(w, g, m, v, name):
    r, c = w.shape
    tr = 128 if r % 128 == 0 else r
    c1 = 1.0 - ADAM_B1 ** ADAM_STEP
    c2 = 1.0 - ADAM_B2 ** ADAM_STEP

    def body(w_ref, g_ref, m_ref, v_ref, d_ref, nm_ref, nv_ref):
        gv = g_ref[...]
        nm = ADAM_B1 * m_ref[...] + (1.0 - ADAM_B1) * gv
        nv = ADAM_B2 * v_ref[...] + (1.0 - ADAM_B2) * (gv * gv)
        nm_ref[...] = nm
        nv_ref[...] = nv
        d_ref[...] = -ADAM_LR * ((nm / c1) / (jnp.sqrt(nv / c2) + ADAM_EPS) + ADAM_WD * w_ref[...])

    spec = pl.BlockSpec((tr, c), lambda i: (i, 0))
    sd = jax.ShapeDtypeStruct((r, c), F32)
    return pl.pallas_call(
        body, name=name, grid=(r // tr,),
        in_specs=[spec] * 4, out_specs=[spec] * 3, out_shape=[sd] * 3,
    )(w, g, m, v)


ANY = pl.BlockSpec(memory_space=pl.ANY)


def _my_place():
    return lax.axis_index("x"), lax.axis_index("y"), lax.axis_index("c")


def _allgather_weights(wpk):
    def body(w_ref, out_ref, send_sems, recv_sems, local_sem):
        x, y, c = _my_place()
        sibling = (x, y, 1 - c)
        chips = [(1 - x, y), (x, 1 - y), (1 - x, 1 - y)]

        def half(shard, hc):
            return out_ref.at[shard, pl.ds(pl.multiple_of(hc * HALF_ROWS, 16), HALF_ROWS), :]

        def copy(k, src, dst, to):
            return pltpu.make_async_remote_copy(src_ref=src, dst_ref=dst, send_sem=send_sems.at[k],
                                                recv_sem=recv_sems.at[k], device_id=to, device_id_type=MESH)

        mine = pltpu.make_async_copy(w_ref, out_ref.at[2 * x + y], local_sem)
        mine.start()
        my_half = w_ref.at[pl.ds(pl.multiple_of(c * HALF_ROWS, 16), HALF_ROWS), :]
        first = [copy(j, my_half, half(2 * x + y, c), (cx, cy, c)) for j, (cx, cy) in enumerate(chips)]
        for cp in first:
            cp.start()
        passed = []
        for j, (cx, cy) in enumerate(chips):
            landed = half(2 * cx + cy, c)
            copy(j, landed, landed, (cx, cy, c)).wait_recv()
            fw = copy(3 + j, landed, landed, sibling)
            fw.start()
            passed.append(fw)
        for j, (cx, cy) in enumerate(chips):
            other = half(2 * cx + cy, 1 - c)
            copy(3 + j, other, other, sibling).wait_recv()
        for cp in first + passed:
            cp.wait_send()
        mine.wait()

    return pl.pallas_call(
        body, name="allgather_weights",
        in_specs=[ANY], out_specs=ANY,
        out_shape=jax.ShapeDtypeStruct((4, PACK_ROWS, 1024), BF16),
        scratch_shapes=[pltpu.SemaphoreType.DMA((6,)), pltpu.SemaphoreType.DMA((6,)), pltpu.SemaphoreType.DMA],
    )(wpk)


def _exchange_grads(parts, gvec):
    def body(p_ref, g_ref, rp_ref, rg_ref, send_sems, recv_sems, local_sems):
        x, y, c = _my_place()
        me = 4 * x + 2 * y + c
        lp = pltpu.make_async_copy(p_ref.at[me], rp_ref.at[me], local_sems.at[0])
        lg = pltpu.make_async_copy(g_ref, rg_ref.at[me], local_sems.at[1])
        lp.start()
        lg.start()
        sends = []
        for k in range(1, 8):
            px, py, pc = x ^ (k >> 2), y ^ ((k >> 1) & 1), c ^ (k & 1)
            peer = 4 * px + 2 * py + pc
            big = pltpu.make_async_remote_copy(
                src_ref=p_ref.at[peer], dst_ref=rp_ref.at[me], send_sem=send_sems.at[k - 1],
                recv_sem=recv_sems.at[k - 1], device_id=(px, py, pc), device_id_type=MESH)
            small = pltpu.make_async_remote_copy(
                src_ref=g_ref, dst_ref=rg_ref.at[me], send_sem=send_sems.at[6 + k],
                recv_sem=recv_sems.at[6 + k], device_id=(px, py, pc), device_id_type=MESH)
            big.start()
            small.start()
            sends += [big, small]
        for k in range(1, 8):
            px, py, pc = x ^ (k >> 2), y ^ ((k >> 1) & 1), c ^ (k & 1)
            peer = 4 * px + 2 * py + pc
            pltpu.make_async_remote_copy(
                src_ref=rp_ref.at[peer], dst_ref=rp_ref.at[peer], send_sem=send_sems.at[k - 1],
                recv_sem=recv_sems.at[k - 1], device_id=(px, py, pc), device_id_type=MESH).wait_recv()
            pltpu.make_async_remote_copy(
                src_ref=rg_ref.at[peer], dst_ref=rg_ref.at[peer], send_sem=send_sems.at[6 + k],
                recv_sem=recv_sems.at[6 + k], device_id=(px, py, pc), device_id_type=MESH).wait_recv()
        for cp in sends:
            cp.wait_send()
        lp.wait()
        lg.wait()

    return pl.pallas_call(
        body, name="exchange_grads",
        in_specs=[ANY, ANY], out_specs=[ANY, ANY],
        out_shape=[jax.ShapeDtypeStruct((8, HALF_ROWS, 1024), BF16), jax.ShapeDtypeStruct((8, 8, N_GAINS), F32)],
        scratch_shapes=[pltpu.SemaphoreType.DMA((14,)), pltpu.SemaphoreType.DMA((14,)), pltpu.SemaphoreType.DMA((2,))],
    )(parts, gvec)


def _swap_halves(g_half):
    def body(g_ref, out_ref, send_sem, recv_sem, local_sem):
        x, y, c = _my_place()
        lc = pltpu.make_async_copy(g_ref, out_ref.at[c], local_sem)
        lc.start()
        cp = pltpu.make_async_remote_copy(src_ref=g_ref, dst_ref=out_ref.at[c], send_sem=send_sem, recv_sem=recv_sem,
                                          device_id=(x, y, 1 - c), device_id_type=MESH)
        cp.start()
        pltpu.make_async_remote_copy(src_ref=out_ref.at[1 - c], dst_ref=out_ref.at[1 - c], send_sem=send_sem,
                                     recv_sem=recv_sem, device_id=(x, y, 1 - c), device_id_type=MESH).wait_recv()
        cp.wait_send()
        lc.wait()

    return pl.pallas_call(
        body, name="swap_halves",
        in_specs=[ANY], out_specs=ANY,
        out_shape=jax.ShapeDtypeStruct((2, HALF_ROWS, 1024), F32),
        scratch_shapes=[pltpu.SemaphoreType.DMA, pltpu.SemaphoreType.DMA, pltpu.SemaphoreType.DMA],
    )(g_half)


def _pack_shard(mats, dtype):
    flat = jnp.concatenate([m.reshape(-1).astype(dtype) for m in mats])
    flat = jnp.pad(flat, (0, PACK_ROWS * 1024 - flat.shape[0]))
    return flat.reshape(PACK_ROWS, 1024)


def _unpack_shard(pk):
    flat = pk.reshape(-1)
    out, off = [], 0
    for r, c in SEG_SHAPES:
        out.append(flat[off:off + r * c].reshape(r, c))
        off += r * c
    return out


def _full_weights(gathered):
    segs = [_unpack_shard(gathered[s]) for s in range(4)]
    w_in = jnp.concatenate([s[0] for s in segs], axis=1)
    w_uq = jnp.concatenate([s[1] for s in segs], axis=1)
    w_ukv = jnp.concatenate([s[2] for s in segs], axis=1)
    w_pm = jnp.concatenate([s[3] for s in segs], axis=1)
    w_pd = jnp.concatenate([s[4] for s in segs], axis=1)
    w_out = jnp.concatenate([s[5] for s in segs], axis=0)
    o = [0]
    for n in IN_SPLITS:
        o.append(o[-1] + n)
    z = lambda n: jnp.zeros((D_MODEL, n), w_in.dtype)
    w_pad = jnp.concatenate([w_in[:, o[6]:o[8]], w_in[:, o[4]:o[6]], w_in[:, o[3]:o[4]], w_in[:, o[0]:o[2]],
                             z(64), w_in[:, o[2]:o[3]], z(32)], axis=1)
    wuq_pad = jnp.pad(w_uq.reshape(Q_RANK, MLA_HEADS, 96), ((0, 0), (0, 0), (0, 32))).reshape(Q_RANK, 1024)
    kv = w_ukv.reshape(KV_RANK, MLA_HEADS, 128)
    wk_pad = jnp.pad(kv[:, :, :64], ((0, 0), (0, 0), (0, 64))).reshape(KV_RANK, 1024)
    wv = kv[:, :, 64:].reshape(KV_RANK, 512)
    return w_pad, wuq_pad, wk_pad, wv, w_pm, w_pd, w_out


def _grad_parts(dw_pad, dwuq_pad, dwk_pad, dwv, dwpm, dwpd, dwout):
    a = dw_pad
    d_in = jnp.concatenate([a[:, COL_LAT:COL_LAT + 640], a[:, COL_LAT + 704:COL_LAT + 736],
                            a[:, COL_QKV:COL_LAT], a[:, COL_Z:COL_QKV], a[:, 0:COL_Z]], axis=1)
    d_uq = dwuq_pad.reshape(Q_RANK, MLA_HEADS, 128)[:, :, :96].reshape(Q_RANK, 768)
    d_ukv = jnp.concatenate([dwk_pad.reshape(KV_RANK, MLA_HEADS, 128)[:, :, :64],
                             dwv.reshape(KV_RANK, MLA_HEADS, 64)], axis=2).reshape(KV_RANK, 1024)
    shards = []
    for s in range(4):
        mats = [d_in[:, s * 2088:(s + 1) * 2088], d_uq[:, s * 192:(s + 1) * 192], d_ukv[:, s * 256:(s + 1) * 256],
                dwpm[:, s * 256:(s + 1) * 256], dwpd[:, s * 256:(s + 1) * 256], dwout[s * 256:(s + 1) * 256, :]]
        shards.append(_pack_shard(mats, BF16))
    return jnp.stack(shards).reshape(8, HALF_ROWS, 1024)


def _rope_tables(positions):
    pos = positions.reshape(SEQ).astype(F32)

    def cs(rot):
        inv = ROPE_THETA ** (-jnp.arange(0, rot, 2, dtype=F32) / rot)
        ang = pos[:, None] * inv
        return jnp.cos(ang), jnp.sin(ang)

    cm, sm = cs(32)
    one, zero = jnp.ones((SEQ, 1), F32), jnp.zeros((SEQ, 1), F32)

    def lanes(parts):
        return jnp.concatenate([jnp.broadcast_to(a, (SEQ, n)) if a.shape[1] == 1 else a for a, n in parts], axis=1)

    mla = (lanes([(one, 64), (cm, 16), (cm, 16), (one, 32)]),
           lanes([(zero, 64), (-sm, 16), (zero, 48)]),
           lanes([(zero, 80), (sm, 16), (zero, 32)]))
    cd, sdl = cs(16)
    dil = (jnp.tile(lanes([(cd, 8), (cd, 8), (one, 48)]), (1, 2)),
           jnp.tile(lanes([(-sdl, 8), (zero, 56)]), (1, 2)),
           jnp.tile(lanes([(zero, 8), (sdl, 8), (zero, 48)]), (1, 2)))
    return mla, dil


def _device_grads(x, positions, target, gains, gathered):
    pre_g, q_g, kv_g, post_g = gains
    w_pad, wuq_pad, wk_pad, wv, w_pm, w_pd, w_out = _full_weights(gathered)
    (mc, ms1, ms2), (dc, ds1, ds2) = _rope_tables(positions)

    h = _prenorm_fwd(x, pre_g)
    p = _matmul(h, w_pad, "nn", F32, 512, 1408, 1024, "in_proj")
    cqn, ckvn, q, k, v = _mla_prep_fwd(p, q_g, kv_g, wuq_pad, wk_pad, wv, mc, ms1, ms2)
    ya, lse_m = _mla_flash_fwd(q, k, v)
    qkv = [_dil_prep_fwd(p, dc, ds1, ds2, g) for g in range(3)]
    o_g, l_g = zip(*[_dil_attn_fwd(qkv[g], g) for g in range(3)])
    (dp, dy, mg, dt, ua, dpa, ud, dpd, dya, dyd, yd, lse_d, loss_cols, dg_post) = _tail(
        p, ya, o_g, l_g, x, target, w_pm, w_pd, w_out, post_g)

    dq, dk, dv = _mla_flash_bwd(q, k, v, ya, dya, lse_m)
    dp, dqb, dkb, dvb, dg_q, dg_kv = _mla_prep_bwd(dp, p, dq, dk, dv, q_g, kv_g, wuq_pad, wk_pad, wv, mc, ms1, ms2)
    for g in range(3):
        dqkv = _dil_attn_bwd(qkv[g], dyd, yd, lse_d, g)
        dp = _dil_prep_bwd(dp, dqkv, dc, ds1, ds2, g)

    dw_pad = _matmul(h, dp, "tn", F32, 1024, 1408, 512, "dw_in")
    dwuq_pad = _matmul(cqn, dqb, "tn", F32, Q_RANK, 1024, 512, "dw_uq")
    dwk_pad = _matmul(ckvn, dkb, "tn", F32, KV_RANK, 1024, 512, "dw_k")
    dwv = _matmul(ckvn, dvb, "tn", F32, KV_RANK, 512, 512, "dw_v")
    dwpm = _matmul(ua, dpa, "tn", F32, 512, 1024, 512, "dw_proj_mla")
    dwpd = _matmul(ud, dpd, "tn", F32, 512, 1024, 512, "dw_proj_dil")
    dwout = _matmul(mg, dt, "tn", F32, 1024, 1024, 512, "dw_out")
    dh = _matmul(dp, w_pad, "nt", F32, 512, 1024, 1408, "dh")
    grad_x, dg_pre = _prenorm_bwd(x, dh, dy, pre_g)

    parts = _grad_parts(dw_pad, dwuq_pad, dwk_pad, dwv, dwpm, dwpd, dwout)
    gvec = jnp.concatenate([dg_pre, dg_q, dg_kv, dg_post], axis=1)
    return loss_cols, grad_x, parts, gvec


def kernel(x, positions, pre_norm_g, w_in, q_norm_g, w_uq, kv_norm_g, w_ukv, w_proj_mla, w_proj_dil, w_out, post_norm_g, loss_target, m_pre_norm_g, m_w_in, m_q_norm_g, m_w_uq, m_kv_norm_g, m_w_ukv, m_w_proj_mla, m_w_proj_dil, m_w_out, m_post_norm_g, v_pre_norm_g, v_w_in, v_q_norm_g, v_w_uq, v_kv_norm_g, v_w_ukv, v_w_proj_mla, v_w_proj_dil, v_w_out, v_post_norm_g):
    mats = [w_in[0], w_uq[0], w_ukv[0], w_proj_mla[0], w_proj_dil[0], w_out[0]]
    gathered = _allgather_weights(_pack_shard(mats, BF16))
    gains = (pre_norm_g, q_norm_g, kv_norm_g, post_norm_g)
    loss_cols, grad_x, parts, gvec = _device_grads(x[0], positions, loss_target[0], gains, gathered)

    loss = lax.psum(jnp.sum(loss_cols) * (0.5 / D_MODEL), ("x", "y", "c"))

    recv_parts, recv_gains = _exchange_grads(parts, jnp.pad(gvec, ((0, 7), (0, 0))))
    g_half = _sum_parts(recv_parts, "sum_grad_parts")
    g_shard = _swap_halves(g_half).reshape(PACK_ROWS, 1024)
    g_gains = _sum_parts(recv_gains, "sum_gain_parts")[0:1]

    g_mats = [g.reshape((1,) + g.shape) for g in _unpack_shard(g_shard)]
    off = [0, 1024, 1408, 1664, 2688]
    g_gain = [g_gains[:, off[i]:off[i + 1]] for i in range(4)]
    grads = [g_gain[0], g_mats[0], g_gain[1], g_mats[1], g_gain[2], g_mats[2], g_mats[3], g_mats[4], g_mats[5], g_gain[3]]
    ws = [pre_norm_g, w_in, q_norm_g, w_uq, kv_norm_g, w_ukv, w_proj_mla, w_proj_dil, w_out, post_norm_g]
    ms = [m_pre_norm_g, m_w_in, m_q_norm_g, m_w_uq, m_kv_norm_g, m_w_ukv, m_w_proj_mla, m_w_proj_dil, m_w_out, m_post_norm_g]
    vs = [v_pre_norm_g, v_w_in, v_q_norm_g, v_w_uq, v_kv_norm_g, v_w_ukv, v_w_proj_mla, v_w_proj_dil, v_w_out, v_post_norm_g]
    deltas, new_m, new_v = [], [], []
    for i, (w, g, m, v) in enumerate(zip(ws, grads, ms, vs)):
        shp = w.shape
        two_d = (shp[-2], shp[-1]) if w.ndim == 3 else shp
        d_, m_, v_ = _adamw(w.reshape(two_d), g.reshape(two_d), m.reshape(two_d), v.reshape(two_d), f"adamw_{i}")
        deltas.append(d_.reshape(shp))
        new_m.append(m_.reshape(shp))
        new_v.append(v_.reshape(shp))
    grads = [g.reshape(w.shape) for g, w in zip(grads, ws)]
    return (loss, grad_x.reshape(x.shape), *grads, *deltas, *new_m, *new_v)
```

```python
import jax
import jax.numpy as jnp
from jax import lax
from jax.experimental import pallas as pl
from jax.experimental.pallas import tpu as pltpu

F32 = jnp.float32
BF16 = jnp.bfloat16

SEQ = 4096
D_MODEL = 1024
EPS = 1e-6
ROPE_THETA = 500000.0
MLA_HEADS = 8
Q_RANK = 384
KV_RANK = 256
MLA_SCALE = 96.0 ** -0.5
MLA_ROPE_HALF = 16
DIL_DILATIONS = (1, 4, 16)
DIL_ROPE_HALF = 8
DIL_SCALE = 0.125
BAND = 128

N_LAT = 768
COL_Z, COL_QKV, COL_LAT = 2048, 3072, 7680
N_PAD = 8448
IN_SPLITS = (384, 256, 32, 4608, 512, 512, 1024, 1024)

SHARD_SHAPES = ((1024, 2088), (384, 192), (256, 256), (512, 256), (512, 256), (256, 1024))
N_MATS = len(SHARD_SHAPES)
N_GAINS = 2688

ADAM_LR, ADAM_B1, ADAM_B2, ADAM_EPS, ADAM_WD, ADAM_STEP = 0.001, 0.9, 0.999, 1e-08, 0.01, 10

VMEM_LIMIT = 56 * 1024 * 1024
NEG = -1e30
MESH = pl.DeviceIdType.MESH


def _cparams(**kw):
    return pltpu.CompilerParams(vmem_limit_bytes=VMEM_LIMIT, **kw)


def _dot(a, b, dims):
    return lax.dot_general(a, b, (dims, ((), ())), preferred_element_type=F32)


def _nn(a, b):
    return _dot(a, b, ((1,), (0,)))


def _nt(a, b):
    return _dot(a, b, ((1,), (1,)))


def _tn(a, b):
    return _dot(a, b, ((0,), (0,)))


def _rope_lanes(shape, half, period, first):
    lane = lax.broadcasted_iota(jnp.int32, shape, len(shape) - 1) % period
    return (lane >= first) & (lane < first + half), (lane >= first + half) & (lane < first + 2 * half)


def _rope_fwd(x, c, s, half, lanes):
    x1, _ = lanes
    return x * c + jnp.where(x1, pltpu.roll(x, 128 - half, 1), pltpu.roll(x, half, 1)) * s


def _rope_bwd(g, c, s, half, lanes):
    x1, x2 = lanes
    gs = g * s
    return g * c + jnp.where(x2, pltpu.roll(gs, half, 1), jnp.where(x1, pltpu.roll(gs, 128 - half, 1), 0.0))


def _sigmoid(x):
    return 1.0 / (1.0 + jnp.exp(-x))


def _matmul(a, b, mode, out_dtype, tm, tn, tk, name):
    if mode == "nn":
        (m, k), n = a.shape, b.shape[1]
        a_spec = pl.BlockSpec((tm, tk), lambda j, i, kk: (i, kk))
        b_spec = pl.BlockSpec((tk, tn), lambda j, i, kk: (kk, j))
        dot = _nn
    elif mode == "nt":
        (m, k), n = a.shape, b.shape[0]
        a_spec = pl.BlockSpec((tm, tk), lambda j, i, kk: (i, kk))
        b_spec = pl.BlockSpec((tn, tk), lambda j, i, kk: (j, kk))
        dot = _nt
    else:
        (k, m), n = a.shape, b.shape[1]
        a_spec = pl.BlockSpec((tk, tm), lambda j, i, kk: (kk, i))
        b_spec = pl.BlockSpec((tk, tn), lambda j, i, kk: (kk, j))
        dot = _tn
    assert m % tm == 0 and n % tn == 0 and k % tk == 0, (name, m, n, k, tm, tn, tk)
    nk = k // tk

    def body(a_ref, b_ref, o_ref, acc_ref):
        kk = pl.program_id(2)
        part = dot(a_ref[...], b_ref[...])

        @pl.when(kk == 0)
        def _():
            acc_ref[...] = part

        @pl.when(kk > 0)
        def _():
            acc_ref[...] += part

        @pl.when(kk == nk - 1)
        def _():
            o_ref[...] = acc_ref[...].astype(o_ref.dtype)

    return pl.pallas_call(
        body, name=name, grid=(n // tn, m // tm, nk),
        in_specs=[a_spec, b_spec],
        out_specs=pl.BlockSpec((tm, tn), lambda j, i, kk: (i, j)),
        out_shape=jax.ShapeDtypeStruct((m, n), out_dtype),
        scratch_shapes=[pltpu.VMEM((tm, tn), F32)],
        compiler_params=_cparams(),
    )(a, b)


def _prenorm_fwd(x, g):
    tm = 512

    def body(x_ref, g_ref, h_ref):
        xv = x_ref[...]
        r = lax.rsqrt(jnp.mean(xv * xv, axis=-1, keepdims=True) + EPS)
        h_ref[...] = (xv * r * g_ref[...]).astype(BF16)

    return pl.pallas_call(
        body, name="prenorm_fwd", grid=(SEQ // tm,),
        in_specs=[pl.BlockSpec((tm, D_MODEL), lambda i: (i, 0)), pl.BlockSpec((1, D_MODEL), lambda i: (0, 0))],
        out_specs=pl.BlockSpec((tm, D_MODEL), lambda i: (i, 0)),
        out_shape=jax.ShapeDtypeStruct((SEQ, D_MODEL), BF16),
    )(x, g)


def _prenorm_bwd(x, dh, dy, g):
    tm = 512

    def body(x_ref, dh_ref, dy_ref, g_ref, gx_ref, dg_ref):
        xv = x_ref[...]
        r = lax.rsqrt(jnp.mean(xv * xv, axis=-1, keepdims=True) + EPS)
        n = xv * r
        dhv = dh_ref[...]
        dn = dhv * g_ref[...]
        gx_ref[...] = dy_ref[...] + r * (dn - n * jnp.mean(dn * n, axis=-1, keepdims=True))
        part = jnp.sum(dhv * n, axis=0, keepdims=True)

        @pl.when(pl.program_id(0) == 0)
        def _():
            dg_ref[...] = part

        @pl.when(pl.program_id(0) > 0)
        def _():
            dg_ref[...] += part

    row = pl.BlockSpec((tm, D_MODEL), lambda i: (i, 0))
    vec = pl.BlockSpec((1, D_MODEL), lambda i: (0, 0))
    return pl.pallas_call(
        body, name="prenorm_bwd", grid=(SEQ // tm,),
        in_specs=[row, row, row, vec], out_specs=[row, vec],
        out_shape=[jax.ShapeDtypeStruct((SEQ, D_MODEL), F32), jax.ShapeDtypeStruct((1, D_MODEL), F32)],
        compiler_params=_cparams(),
    )(x, dh, dy, g)


def _mla_prep_fwd(p, qg, kvg, wuq, wk, wv, rc, rs):
    tm = 512

    def body(lat_ref, qg_ref, kvg_ref, wuq_ref, wk_ref, wv_ref, c_ref, s_ref,
             cqn_ref, ckvn_ref, q_ref, k_ref, v_ref):
        c, s = c_ref[...], s_ref[...]
        lanes = _rope_lanes((tm, 128), MLA_ROPE_HALF, 128, 64)
        cq = lat_ref[:, 0:Q_RANK]
        r1 = lax.rsqrt(jnp.mean(cq * cq, axis=-1, keepdims=True) + EPS)
        cqn = (cq * r1 * qg_ref[...]).astype(BF16)
        cqn_ref[...] = cqn
        q = _nn(cqn, wuq_ref[...])
        for h in range(MLA_HEADS):
            sl = slice(h * 128, (h + 1) * 128)
            q_ref[:, sl] = _rope_fwd(q[:, sl], c, s, MLA_ROPE_HALF, lanes).astype(BF16)
        ckv = lat_ref[:, Q_RANK:Q_RANK + KV_RANK]
        r2 = lax.rsqrt(jnp.mean(ckv * ckv, axis=-1, keepdims=True) + EPS)
        ckvn = (ckv * r2 * kvg_ref[...]).astype(BF16)
        ckvn_ref[...] = ckvn
        krr = _rope_fwd(lat_ref[:, Q_RANK + KV_RANK:N_LAT], c, s, MLA_ROPE_HALF, lanes)
        kn = _nn(ckvn, wk_ref[...])
        for h in range(MLA_HEADS):
            sl = slice(h * 128, (h + 1) * 128)
            k_ref[:, sl] = (kn[:, sl] + krr).astype(BF16)
        v_ref[...] = _nn(ckvn, wv_ref[...]).astype(BF16)

    def full(shape):
        return pl.BlockSpec(shape, lambda i: (0, 0))

    def rows(w):
        return pl.BlockSpec((tm, w), lambda i: (i, 0))

    return pl.pallas_call(
        body, name="mla_prep_fwd", grid=(SEQ // tm,),
        in_specs=[pl.BlockSpec((tm, N_LAT), lambda i: (i, COL_LAT // N_LAT)),
                  full((1, Q_RANK)), full((1, KV_RANK)), full((Q_RANK, 1024)), full((KV_RANK, 1024)),
                  full((KV_RANK, 512)), rows(128), rows(128)],
        out_specs=[rows(Q_RANK), rows(KV_RANK), rows(1024), rows(1024), rows(512)],
        out_shape=[jax.ShapeDtypeStruct((SEQ, Q_RANK), BF16), jax.ShapeDtypeStruct((SEQ, KV_RANK), BF16),
                   jax.ShapeDtypeStruct((SEQ, 1024), BF16), jax.ShapeDtypeStruct((SEQ, 1024), BF16),
                   jax.ShapeDtypeStruct((SEQ, 512), BF16)],
        compiler_params=_cparams(),
    )(p, qg, kvg, wuq, wk, wv, rc, rs)


def _mla_prep_bwd(dp_in, p, dq, dk, dv, qg, kvg, wuq, wk, wv, rc, rs):
    tm = 512

    def body(dp_any, lat_ref, dq_ref, dk_ref, dv_ref, qg_ref, kvg_ref, wuq_ref, wk_ref, wv_ref,
             c_ref, s_ref, dp_ref, dqb_ref, dkb_ref, dvb_ref, dgq_ref, dgkv_ref):
        del dp_any
        c, s = c_ref[...], s_ref[...]
        lanes = _rope_lanes((tm, 128), MLA_ROPE_HALF, 128, 64)
        lane = lax.broadcasted_iota(jnp.int32, (tm, 128), 1)
        dkr = jnp.zeros((tm, 128), F32)
        for h in range(MLA_HEADS):
            sl = slice(h * 128, (h + 1) * 128)
            dqb_ref[:, sl] = _rope_bwd(dq_ref[:, sl], c, s, MLA_ROPE_HALF, lanes).astype(BF16)
            dkh = dk_ref[:, sl]
            dkr = dkr + dkh
            dkb_ref[:, sl] = jnp.where(lane < 64, dkh, 0.0).astype(BF16)
        dkr = jnp.where((lane >= 64) & (lane < 96), dkr, 0.0)
        dkr = _rope_bwd(dkr, c, s, MLA_ROPE_HALF, lanes)
        dvb = dv_ref[...].astype(BF16)
        dvb_ref[...] = dvb

        cq = lat_ref[:, 0:Q_RANK]
        r1 = lax.rsqrt(jnp.mean(cq * cq, axis=-1, keepdims=True) + EPS)
        n1 = cq * r1
        dcqn = _nt(dqb_ref[...], wuq_ref[...])
        dn1 = dcqn * qg_ref[...]
        dcq = r1 * (dn1 - n1 * jnp.mean(dn1 * n1, axis=-1, keepdims=True))
        pq = jnp.sum(dcqn * n1, axis=0, keepdims=True)

        ckv = lat_ref[:, Q_RANK:Q_RANK + KV_RANK]
        r2 = lax.rsqrt(jnp.mean(ckv * ckv, axis=-1, keepdims=True) + EPS)
        n2 = ckv * r2
        dckvn = _nt(dkb_ref[...], wk_ref[...]) + _nt(dvb, wv_ref[...])
        dn2 = dckvn * kvg_ref[...]
        dckv = r2 * (dn2 - n2 * jnp.mean(dn2 * n2, axis=-1, keepdims=True))
        pkv = jnp.sum(dckvn * n2, axis=0, keepdims=True)

        dp_ref[:, 0:Q_RANK] = dcq.astype(BF16)
        dp_ref[:, Q_RANK:Q_RANK + KV_RANK] = dckv.astype(BF16)
        dp_ref[:, Q_RANK + KV_RANK:N_LAT] = dkr.astype(BF16)

        @pl.when(pl.program_id(0) == 0)
        def _():
            dgq_ref[...] = pq
            dgkv_ref[...] = pkv

        @pl.when(pl.program_id(0) > 0)
        def _():
            dgq_ref[...] += pq
            dgkv_ref[...] += pkv

    def full(shape):
        return pl.BlockSpec(shape, lambda i: (0, 0))

    def rows(w):
        return pl.BlockSpec((tm, w), lambda i: (i, 0))

    lat = pl.BlockSpec((tm, N_LAT), lambda i: (i, COL_LAT // N_LAT))
    return pl.pallas_call(
        body, name="mla_prep_bwd", grid=(SEQ // tm,),
        in_specs=[pl.BlockSpec(memory_space=pl.ANY), lat, rows(1024), rows(1024), rows(512),
                  full((1, Q_RANK)), full((1, KV_RANK)), full((Q_RANK, 1024)), full((KV_RANK, 1024)),
                  full((KV_RANK, 512)), rows(128), rows(128)],
        out_specs=[lat, rows(1024), rows(1024), rows(512), full((1, Q_RANK)), full((1, KV_RANK))],
        out_shape=[jax.ShapeDtypeStruct((SEQ, N_PAD), BF16), jax.ShapeDtypeStruct((SEQ, 1024), BF16),
                   jax.ShapeDtypeStruct((SEQ, 1024), BF16), jax.ShapeDtypeStruct((SEQ, 512), BF16),
                   jax.ShapeDtypeStruct((1, Q_RANK), F32), jax.ShapeDtypeStruct((1, KV_RANK), F32)],
        input_output_aliases={0: 0},
        compiler_params=_cparams(),
    )(dp_in, p, dq, dk, dv, qg, kvg, wuq, wk, wv, rc, rs)


FLASH_T = 512


def _head_half(shape, hh):
    lane = lax.broadcasted_iota(jnp.int32, shape, 1)
    return (lane < 64) if hh == 0 else (lane >= 64)


def _causal_keep(t):
    row = lax.broadcasted_iota(jnp.int32, (t, t), 0)
    col = lax.broadcasted_iota(jnp.int32, (t, t), 1)
    return row >= col


def _mla_flash_fwd(q, k, v):
    t = FLASH_T
    nb = SEQ // t

    def body(q_ref, k_ref, v_ref, o_ref, lse_ref, m_scr, l_scr, acc_scr):
        i, kb = pl.program_id(1), pl.program_id(2)

        @pl.when(kb == 0)
        def _():
            m_scr[...] = jnp.full_like(m_scr, NEG)
            l_scr[...] = jnp.zeros_like(l_scr)
            acc_scr[...] = jnp.zeros_like(acc_scr)

        @pl.when(kb <= i)
        def _():
            keep = _causal_keep(t) | (kb < i)
            vv = v_ref[...]
            for hh in range(2):
                sl = slice(hh * 128, (hh + 1) * 128)
                s = _nt(q_ref[:, sl], k_ref[:, sl]) * MLA_SCALE
                s = jnp.where(keep, s, NEG)
                m_prev = m_scr[hh]
                m_new = jnp.maximum(m_prev, jnp.max(s, axis=-1, keepdims=True))
                pr = jnp.exp(s - m_new)
                alpha = jnp.exp(m_prev - m_new)
                l_scr[hh] = alpha * l_scr[hh] + jnp.sum(pr, axis=-1, keepdims=True)
                acc_scr[hh] = alpha * acc_scr[hh] + _nn(pr.astype(BF16), vv)
                m_scr[hh] = m_new

        @pl.when(kb == nb - 1)
        def _():
            o0 = acc_scr[0] / l_scr[0]
            o1 = acc_scr[1] / l_scr[1]
            o_ref[...] = jnp.where(_head_half((t, 128), 0), o0, o1)
            for hh in range(2):
                lse = m_scr[hh] + jnp.log(l_scr[hh])
                lse_ref[:, hh * 128:(hh + 1) * 128] = jnp.broadcast_to(lse, (t, 128))

    return pl.pallas_call(
        body, name="mla_flash_fwd", grid=(4, nb, nb),
        in_specs=[pl.BlockSpec((t, 256), lambda j, i, kb: (i, j)),
                  pl.BlockSpec((t, 256), lambda j, i, kb: (jnp.minimum(kb, i), j)),
                  pl.BlockSpec((t, 128), lambda j, i, kb: (jnp.minimum(kb, i), j))],
        out_specs=[pl.BlockSpec((t, 128), lambda j, i, kb: (i, j)),
                   pl.BlockSpec((t, 256), lambda j, i, kb: (i, j))],
        out_shape=[jax.ShapeDtypeStruct((SEQ, 512), F32), jax.ShapeDtypeStruct((SEQ, 1024), F32)],
        scratch_shapes=[pltpu.VMEM((2, t, 1), F32), pltpu.VMEM((2, t, 1), F32), pltpu.VMEM((2, t, 128), F32)],
        compiler_params=_cparams(),
    )(q, k, v)


def _mla_flash_bwd(q, k, v, o, do, lse):
    t = FLASH_T
    nb = SEQ // t

    def body(q_ref, k_ref, v_ref, o_ref, do_ref, lse_ref, dq_ref, dk_ref, dv_ref, dk_scr, dv_scr):
        kb, i = pl.program_id(1), pl.program_id(2)

        @pl.when((kb == 0) & (i == 0))
        def _():
            dq_ref[...] = jnp.zeros_like(dq_ref)

        @pl.when(i == 0)
        def _():
            dk_scr[...] = jnp.zeros_like(dk_scr)
            dv_scr[...] = jnp.zeros_like(dv_scr)

        @pl.when(i >= kb)
        def _():
            keep = _causal_keep(t) | (i > kb)
            vv = v_ref[...]
            ov = o_ref[...]
            dov = do_ref[...]
            rows = pl.ds(pl.multiple_of(i * t, t), t)
            for hh in range(2):
                sl = slice(hh * 128, (hh + 1) * 128)
                qh, kh = q_ref[:, sl], k_ref[:, sl]
                s = _nt(qh, kh) * MLA_SCALE
                pr = jnp.exp(jnp.where(keep, s, NEG) - lse_ref[:, hh * 128:hh * 128 + 1])
                dom = jnp.where(_head_half((t, 128), hh), dov, 0.0)
                domb = dom.astype(BF16)
                dv_scr[...] += _tn(pr.astype(BF16), domb)
                dpr = _nt(domb, vv)
                delta = jnp.sum(dom * ov, axis=-1, keepdims=True)
                ds = (pr * (dpr - delta) * MLA_SCALE).astype(BF16)
                dq_ref[rows, sl] += _nn(ds, kh)
                dk_scr[hh] += _tn(ds, qh)

        @pl.when(i == nb - 1)
        def _():
            dk_ref[:, 0:128] = dk_scr[0]
            dk_ref[:, 128:256] = dk_scr[1]
            dv_ref[...] = dv_scr[...]

    qi = lambda j, kb, i: (jnp.maximum(i, kb), j)
    ki = lambda j, kb, i: (kb, j)
    return pl.pallas_call(
        body, name="mla_flash_bwd", grid=(4, nb, nb),
        in_specs=[pl.BlockSpec((t, 256), qi), pl.BlockSpec((t, 256), ki), pl.BlockSpec((t, 128), ki),
                  pl.BlockSpec((t, 128), qi), pl.BlockSpec((t, 128), qi), pl.BlockSpec((t, 256), qi)],
        out_specs=[pl.BlockSpec((SEQ, 256), lambda j, kb, i: (0, j)), pl.BlockSpec((t, 256), ki),
                   pl.BlockSpec((t, 128), ki)],
        out_shape=[jax.ShapeDtypeStruct((SEQ, 1024), F32), jax.ShapeDtypeStruct((SEQ, 1024), F32),
                   jax.ShapeDtypeStruct((SEQ, 512), F32)],
        scratch_shapes=[pltpu.VMEM((2, t, 128), F32), pltpu.VMEM((t, 128), F32)],
        compiler_params=_cparams(),
    )(q, k, v, o, do, lse)


DIL_UNROLL = 2


def _strided(start, size, d):
    return pl.ds(start, size) if d == 1 else pl.ds(start, size, stride=d)


def _dil_prep_fwd(p, rc, rs, g):
    d = DIL_DILATIONS[g]
    sub_len = SEQ // d
    ch = min(sub_len, 512)

    def body(p_ref, c_ref, s_ref, o_ref):
        tq = pl.program_id(0)
        is_v = tq == 2
        mult = jnp.where(tq == 0, DIL_SCALE, 1.0).astype(F32)
        lanes = _rope_lanes((ch, 128), DIL_ROPE_HALF, 64, 0)
        for r in range(d):
            for c0 in range(0, sub_len, ch):
                rows = _strided(r + c0 * d, ch, d)
                xv = p_ref[rows, :]
                roped = _rope_fwd(xv, c_ref[rows, :], s_ref[rows, :], DIL_ROPE_HALF, lanes)
                o_ref[0, r * sub_len + c0:r * sub_len + c0 + ch, :] = (jnp.where(is_v, xv, roped) * mult).astype(BF16)

    tab = pl.BlockSpec((SEQ, 128), lambda tq, pr: (0, 0))
    return pl.pallas_call(
        body, name=f"dil_prep_fwd_g{g}", grid=(3, 4),
        in_specs=[pl.BlockSpec((SEQ, 128), lambda tq, pr: (0, COL_QKV // 128 + (tq * 3 + g) * 4 + pr)), tab, tab],
        out_specs=pl.BlockSpec((1, SEQ, 128), lambda tq, pr: (tq, 0, pr)),
        out_shape=jax.ShapeDtypeStruct((3, SEQ, 512), BF16),
        compiler_params=_cparams(),
    )(p, rc, rs)


def _band_masks():
    row = lax.broadcasted_iota(jnp.int32, (BAND, BAND), 0)
    col = lax.broadcasted_iota(jnp.int32, (BAND, BAND), 1)
    return row >= col, col >= row


def _dil_attn_fwd(qkv, g):
    d = DIL_DILATIONS[g]
    nbs = SEQ // d // BAND
    nblk = SEQ // BAND

    def body(q_ref, k_ref, v_ref, o_ref, l_ref):
        keep_c, keep_p = _band_masks()
        half0 = _head_half((BAND, 128), 0)

        def step(i, carry):
            cur = pl.ds(pl.multiple_of(i * BAND, BAND), BAND)
            prv = pl.ds(pl.multiple_of(jnp.maximum(i - 1, 0) * BAND, BAND), BAND)
            has_prev = (i % nbs) != 0
            qv = q_ref[0, cur, :]
            kc, kp = k_ref[0, cur, :], k_ref[0, prv, :]
            vc, vp = v_ref[0, cur, :], v_ref[0, prv, :]
            outs, lses = [], []
            for hh in range(2):
                qm = jnp.where(_head_half((BAND, 128), hh), qv, jnp.zeros_like(qv))
                sc = jnp.where(keep_c, _nt(qm, kc), NEG)
                sp = jnp.where(keep_p & has_prev, _nt(qm, kp), NEG)
                m = jnp.maximum(jnp.max(sc, axis=-1, keepdims=True), jnp.max(sp, axis=-1, keepdims=True))
                pc, pp = jnp.exp(sc - m), jnp.exp(sp - m)
                den = jnp.sum(pc, axis=-1, keepdims=True) + jnp.sum(pp, axis=-1, keepdims=True)
                o = (_nn(pc.astype(BF16), vc) + _nn(pp.astype(BF16), vp)) / den
                outs.append(o)
                lses.append(jnp.broadcast_to(m + jnp.log(den), (BAND, 128)))
            tok = _strided((i % nbs) * BAND * d + i // nbs, BAND, d)
            o_ref[tok, :] = jnp.where(half0, outs[0], outs[1])
            l_ref[tok, :] = jnp.where(half0, lses[0], lses[1])
            return carry

        lax.fori_loop(0, nblk, step, 0, unroll=DIL_UNROLL)

    def inp(tq):
        return pl.BlockSpec((1, SEQ, 128), lambda pr: (tq, 0, pr))

    out = pl.BlockSpec((SEQ, 128), lambda pr: (0, pr))
    return pl.pallas_call(
        body, name=f"dil_attn_fwd_g{g}", grid=(4,),
        in_specs=[inp(0), inp(1), inp(2)], out_specs=[out, out],
        out_shape=[jax.ShapeDtypeStruct((SEQ, 512), F32), jax.ShapeDtypeStruct((SEQ, 512), F32)],
        compiler_params=_cparams(),
    )(qkv, qkv, qkv)


def _dil_attn_bwd(qkv, dyd, yd, lse_all, g):
    d = DIL_DILATIONS[g]
    sub_len = SEQ // d
    nbs = sub_len // BAND
    nblk = SEQ // BAND

    def body(q_ref, k_ref, v_ref, do_ref, y_ref, l_ref, out_ref, dk_scr, dv_scr):
        keep_c, keep_p = _band_masks()
        dk_scr[...] = jnp.zeros_like(dk_scr)
        dv_scr[...] = jnp.zeros_like(dv_scr)

        def step(i, carry):
            cur = pl.ds(pl.multiple_of(i * BAND, BAND), BAND)
            prv = pl.ds(pl.multiple_of(jnp.maximum(i - 1, 0) * BAND, BAND), BAND)
            has_prev = (i % nbs) != 0
            tok = _strided((i % nbs) * BAND * d + i // nbs, BAND, d)
            qv = q_ref[0, cur, :]
            kc, kp = k_ref[0, cur, :], k_ref[0, prv, :]
            vc, vp = v_ref[0, cur, :], v_ref[0, prv, :]
            dov, yv, lv = do_ref[tok, :], y_ref[tok, :], l_ref[tok, :]
            dq = jnp.zeros((BAND, 128), F32)
            dkc = jnp.zeros((BAND, 128), F32)
            dkp = jnp.zeros((BAND, 128), F32)
            dvc = jnp.zeros((BAND, 128), F32)
            dvp = jnp.zeros((BAND, 128), F32)
            for hh in range(2):
                half = _head_half((BAND, 128), hh)
                qm = jnp.where(half, qv, jnp.zeros_like(qv))
                lcol = jnp.max(jnp.where(half, lv, NEG), axis=-1, keepdims=True)
                pc = jnp.exp(jnp.where(keep_c, _nt(qm, kc), NEG) - lcol)
                pp = jnp.exp(jnp.where(keep_p & has_prev, _nt(qm, kp), NEG) - lcol)
                dom = jnp.where(half, dov, 0.0)
                domb = dom.astype(BF16)
                delta = jnp.sum(dom * yv, axis=-1, keepdims=True)
                dsc = (pc * (_nt(domb, vc) - delta)).astype(BF16)
                dsp = (pp * (_nt(domb, vp) - delta)).astype(BF16)
                dvc = dvc + _tn(pc.astype(BF16), domb)
                dvp = dvp + _tn(pp.astype(BF16), domb)
                dq = dq + jnp.where(half, _nn(dsc, kc) + _nn(dsp, kp), 0.0)
                dkc = dkc + jnp.where(half, _tn(dsc, qv), 0.0)
                dkp = dkp + jnp.where(half, _tn(dsp, qv), 0.0)
            out_ref[pl.ds(0, 1), tok, :] = dq[None]
            dk_scr[cur, :] += dkc
            dk_scr[prv, :] += dkp
            dv_scr[cur, :] += dvc
            dv_scr[prv, :] += dvp
            return carry

        lax.fori_loop(0, nblk, step, 0, unroll=DIL_UNROLL)
        for r in range(d):
            rows = _strided(r, sub_len, d)
            out_ref[pl.ds(1, 1), rows, :] = dk_scr[r * sub_len:(r + 1) * sub_len, :][None]
            out_ref[pl.ds(2, 1), rows, :] = dv_scr[r * sub_len:(r + 1) * sub_len, :][None]

    def inp(tq):
        return pl.BlockSpec((1, SEQ, 128), lambda pr: (tq, 0, pr))

    tok_spec = pl.BlockSpec((SEQ, 128), lambda pr: (0, pr))
    return pl.pallas_call(
        body, name=f"dil_attn_bwd_g{g}", grid=(4,),
        in_specs=[inp(0), inp(1), inp(2), tok_spec, tok_spec, tok_spec],
        out_specs=pl.BlockSpec((3, SEQ, 128), lambda pr: (0, 0, pr)),
        out_shape=jax.ShapeDtypeStruct((3, SEQ, 512), F32),
        scratch_shapes=[pltpu.VMEM((SEQ, 128), F32), pltpu.VMEM((SEQ, 128), F32)],
        compiler_params=_cparams(),
    )(qkv, qkv, qkv, dyd, yd, lse_all)


def _dil_prep_bwd(dp_in, dqkv, rc, rs, g):
    tm = 512

    def body(dp_any, g_ref, c_ref, s_ref, dp_ref):
        del dp_any
        tq = pl.program_id(0)
        mult = jnp.where(tq == 0, DIL_SCALE, 1.0).astype(F32)
        lanes = _rope_lanes((tm, 128), DIL_ROPE_HALF, 64, 0)
        cv, sv = c_ref[...], s_ref[...]
        for pr in range(4):
            gv = g_ref[0, :, pr * 128:(pr + 1) * 128]
            roped = _rope_bwd(gv, cv, sv, DIL_ROPE_HALF, lanes)
            dp_ref[:, pr * 128:(pr + 1) * 128] = (jnp.where(tq == 2, gv, roped) * mult).astype(BF16)

    tab = pl.BlockSpec((tm, 128), lambda tq, i: (i, 0))
    return pl.pallas_call(
        body, name=f"dil_prep_bwd_g{g}", grid=(3, SEQ // tm),
        in_specs=[pl.BlockSpec(memory_space=pl.ANY),
                  pl.BlockSpec((1, tm, 512), lambda tq, i: (tq, i, 0)), tab, tab],
        out_specs=pl.BlockSpec((tm, 512), lambda tq, i: (i, COL_QKV // 512 + tq * 3 + g)),
        out_shape=jax.ShapeDtypeStruct((SEQ, N_PAD), BF16),
        input_output_aliases={0: 0},
    )(dp_in, dqkv, rc, rs)


TAIL_T = 128


def _tail(p, ya, o_g, l_g, x, target, wpm, wpd, wout, post_g):
    tm = TAIL_T

    def body(pgz_ref, ya_ref, o0_ref, o1_ref, o2_ref, l0_ref, l1_ref, l2_ref, x_ref, t_ref,
             wpm_ref, wpd_ref, wout_ref, pg_ref,
             dp_ref, dy_ref, mg_ref, dt_ref, ua_ref, dpa_ref, ud_ref, dpd_ref, dya_ref, dyd_ref,
             yd_ref, lse_ref, loss_ref, dgp_ref):
        l0, l1, l2 = l0_ref[...], l1_ref[...], l2_ref[...]
        mx = jnp.maximum(jnp.maximum(l0, l1), l2)
        e0, e1, e2 = jnp.exp(l0 - mx), jnp.exp(l1 - mx), jnp.exp(l2 - mx)
        den = e0 + e1 + e2
        yd = (e0 * o0_ref[...] + e1 * o1_ref[...] + e2 * o2_ref[...]) / den
        yd_ref[...] = yd
        lse_ref[...] = mx + jnp.log(den)
        ya = ya_ref[...]

        gm, gd = pgz_ref[:, 0:1024], pgz_ref[:, 1024:2048]
        zm, zd = pgz_ref[:, 2048:2560], pgz_ref[:, 2560:3072]
        szm, szd = _sigmoid(zm), _sigmoid(zd)
        sm, sd = zm * szm, zd * szd
        ua = (ya * sm).astype(BF16)
        ud = (yd * sd).astype(BF16)
        ua_ref[...] = ua
        ud_ref[...] = ud
        pa = _nn(ua, wpm_ref[...])
        pd = _nn(ud, wpd_ref[...])
        sgm, sgd = _sigmoid(gm), _sigmoid(gd)
        mg = (sgm * pa + sgd * pd).astype(BF16)
        mg_ref[...] = mg
        t = _nn(mg, wout_ref[...])
        r3 = lax.rsqrt(jnp.mean(t * t, axis=-1, keepdims=True) + EPS)
        n = t * r3
        pg = pg_ref[...]
        err = x_ref[...] + n * pg - t_ref[...]
        lpart = jnp.sum(err * err, axis=0, keepdims=True)

        dy = err * (1.0 / D_MODEL)
        dy_ref[...] = dy
        gpart = jnp.sum(dy * n, axis=0, keepdims=True)
        dn = dy * pg
        dt = (r3 * (dn - n * jnp.mean(dn * n, axis=-1, keepdims=True))).astype(BF16)
        dt_ref[...] = dt
        dmg = _nt(dt, wout_ref[...])
        dpa = (dmg * sgm).astype(BF16)
        dpd = (dmg * sgd).astype(BF16)
        dpa_ref[...] = dpa
        dpd_ref[...] = dpd
        dp_ref[:, 0:1024] = (dmg * pa * sgm * (1.0 - sgm)).astype(BF16)
        dp_ref[:, 1024:2048] = (dmg * pd * sgd * (1.0 - sgd)).astype(BF16)
        dua = _nt(dpa, wpm_ref[...])
        dud = _nt(dpd, wpd_ref[...])
        dya_ref[...] = dua * sm
        dyd_ref[...] = dud * sd
        dp_ref[:, 2048:2560] = (dua * ya * szm * (1.0 + zm * (1.0 - szm))).astype(BF16)
        dp_ref[:, 2560:3072] = (dud * yd * szd * (1.0 + zd * (1.0 - szd))).astype(BF16)

        @pl.when(pl.program_id(0) == 0)
        def _():
            loss_ref[...] = lpart
            dgp_ref[...] = gpart

        @pl.when(pl.program_id(0) > 0)
        def _():
            loss_ref[...] += lpart
            dgp_ref[...] += gpart

    def rows(w):
        return pl.BlockSpec((tm, w), lambda i: (i, 0))

    def full(shape):
        return pl.BlockSpec(shape, lambda i: (0, 0))

    def sds(w, dt):
        return jax.ShapeDtypeStruct((SEQ, w), dt)

    return pl.pallas_call(
        body, name="tail", grid=(SEQ // tm,),
        in_specs=[rows(3072), rows(512), rows(512), rows(512), rows(512), rows(512), rows(512), rows(512),
                  rows(1024), rows(1024), full((512, 1024)), full((512, 1024)), full((1024, 1024)), full((1, 1024))],
        out_specs=[rows(3072), rows(1024), rows(1024), rows(1024), rows(512), rows(1024), rows(512), rows(1024),
                   rows(512), rows(512), rows(512), rows(512), full((1, 1024)), full((1, 1024))],
        out_shape=[sds(N_PAD, BF16), sds(1024, F32), sds(1024, BF16), sds(1024, BF16), sds(512, BF16),
                   sds(1024, BF16), sds(512, BF16), sds(1024, BF16), sds(512, F32), sds(512, F32),
                   sds(512, F32), sds(512, F32),
                   jax.ShapeDtypeStruct((1, 1024), F32), jax.ShapeDtypeStruct((1, 1024), F32)],
        compiler_params=_cparams(),
    )(p, ya, o_g[0], o_g[1], o_g[2], l_g[0], l_g[1], l_g[2], x, target, wpm, wpd, wout, post_g)


def _sum_parts(parts, tr, name):
    n, r, w = parts.shape

    def body(p_ref, o_ref):
        acc = p_ref[0].astype(F32)
        for s in range(1, n):
            acc = acc + p_ref[s].astype(F32)
        o_ref[...] = acc

    return pl.pallas_call(
        body, name=name, grid=(r // tr,),
        in_specs=[pl.BlockSpec((n, tr, w), lambda i: (0, i, 0))],
        out_specs=pl.BlockSpec((tr, w), lambda i: (i, 0)),
        out_shape=jax.ShapeDtypeStruct((r, w), F32),
    )(parts)


def _adamw(w, g, m, v, name):
    r, c = w.shape
    tr = 128 if r % 128 == 0 else r
    c1 = 1.0 - ADAM_B1 ** ADAM_STEP
    c2 = 1.0 - ADAM_B2 ** ADAM_STEP

    def body(w_ref, g_ref, m_ref, v_ref, d_ref, nm_ref, nv_ref):
        gv = g_ref[...]
        nm = ADAM_B1 * m_ref[...] + (1.0 - ADAM_B1) * gv
        nv = ADAM_B2 * v_ref[...] + (1.0 - ADAM_B2) * (gv * gv)
        nm_ref[...] = nm
        nv_ref[...] = nv
        d_ref[...] = -ADAM_LR * ((nm / c1) / (jnp.sqrt(nv / c2) + ADAM_EPS) + ADAM_WD * w_ref[...])

    spec = pl.BlockSpec((tr, c), lambda i: (i, 0))
    sd = jax.ShapeDtypeStruct((r, c), F32)
    return pl.pallas_call(
        body, name=name, grid=(r // tr,),
        in_specs=[spec] * 4, out_specs=[spec] * 3, out_shape=[sd] * 3,
    )(w, g, m, v)


ANY = pl.BlockSpec(memory_space=pl.ANY)


def _my_place():
    return lax.axis_index("x"), lax.axis_index("y"), lax.axis_index("c")


def _allgather_weights(mats):
    def body(*refs):
        w_refs, out_refs = refs[:N_MATS], refs[N_MATS:2 * N_MATS]
        send_sems, recv_sems = refs[2 * N_MATS:]
        x, y, c = _my_place()
        sibling = (x, y, 1 - c)
        chips = [(1 - x, y), (x, 1 - y), (1 - x, 1 - y)]

        def copy(k, src, dst, to):
            return pltpu.make_async_remote_copy(src_ref=src, dst_ref=dst, send_sem=send_sems.at[k],
                                                recv_sem=recv_sems.at[k], device_id=to, device_id_type=MESH)

        def half(mi, shard, hc):
            hr = SHARD_SHAPES[mi][0] // 2
            return out_refs[mi].at[shard, pl.ds(pl.multiple_of(hc * hr, 16), hr), :]

        started = []
        for mi in range(N_MATS):
            hr = SHARD_SHAPES[mi][0] // 2
            my_half = w_refs[mi].at[pl.ds(pl.multiple_of(c * hr, 16), hr), :]
            for j, (cx, cy) in enumerate(chips):
                cp = copy(mi * 6 + j, my_half, half(mi, 2 * x + y, c), (cx, cy, c))
                cp.start()
                started.append(cp)
        for mi in range(N_MATS):
            for j, (cx, cy) in enumerate(chips):
                landed = half(mi, 2 * cx + cy, c)
                copy(mi * 6 + j, landed, landed, (cx, cy, c)).wait_recv()
                fw = copy(mi * 6 + 3 + j, landed, landed, sibling)
                fw.start()
                started.append(fw)
        for mi in range(N_MATS):
            for j, (cx, cy) in enumerate(chips):
                other = half(mi, 2 * cx + cy, 1 - c)
                copy(mi * 6 + 3 + j, other, other, sibling).wait_recv()
        for cp in started:
            cp.wait_send()

    return pl.pallas_call(
        body, name="allgather_weights",
        in_specs=[ANY] * N_MATS, out_specs=[ANY] * N_MATS,
        out_shape=[jax.ShapeDtypeStruct((4, r, c), BF16) for r, c in SHARD_SHAPES],
        scratch_shapes=[pltpu.SemaphoreType.DMA((6 * N_MATS,)), pltpu.SemaphoreType.DMA((6 * N_MATS,))],
    )(*mats)


def _exchange_grads(parts, gvec):
    n = N_MATS + 1

    def body(*refs):
        p_refs, r_refs = refs[:n], refs[n:2 * n]
        send_sems, recv_sems = refs[2 * n:]
        x, y, c = _my_place()
        me = 4 * x + 2 * y + c
        sends = []
        for k in range(1, 8):
            px, py, pc = x ^ (k >> 2), y ^ ((k >> 1) & 1), c ^ (k & 1)
            peer = 4 * px + 2 * py + pc
            for mi in range(n):
                src = p_refs[mi].at[peer] if mi < N_MATS else p_refs[mi]
                cp = pltpu.make_async_remote_copy(
                    src_ref=src, dst_ref=r_refs[mi].at[me], send_sem=send_sems.at[(k - 1) * n + mi],
                    recv_sem=recv_sems.at[(k - 1) * n + mi], device_id=(px, py, pc), device_id_type=MESH)
                cp.start()
                sends.append(cp)
        for k in range(1, 8):
            px, py, pc = x ^ (k >> 2), y ^ ((k >> 1) & 1), c ^ (k & 1)
            peer = 4 * px + 2 * py + pc
            for mi in range(n):
                slot = r_refs[mi].at[peer]
                pltpu.make_async_remote_copy(
                    src_ref=slot, dst_ref=slot, send_sem=send_sems.at[(k - 1) * n + mi],
                    recv_sem=recv_sems.at[(k - 1) * n + mi], device_id=(px, py, pc), device_id_type=MESH).wait_recv()
        for cp in sends:
            cp.wait_send()

    return pl.pallas_call(
        body, name="exchange_grads",
        in_specs=[ANY] * n, out_specs=[ANY] * n,
        out_shape=[jax.ShapeDtypeStruct((8, r // 2, c), BF16) for r, c in SHARD_SHAPES]
        + [jax.ShapeDtypeStruct((8, 8, N_GAINS), F32)],
        scratch_shapes=[pltpu.SemaphoreType.DMA((7 * n,)), pltpu.SemaphoreType.DMA((7 * n,))],
    )(*parts, gvec)


def _swap_halves(halves):
    def body(*refs):
        g_refs, out_refs = refs[:N_MATS], refs[N_MATS:2 * N_MATS]
        send_sems, recv_sems = refs[2 * N_MATS:]
        x, y, c = _my_place()
        sends = []
        for mi in range(N_MATS):
            cp = pltpu.make_async_remote_copy(src_ref=g_refs[mi], dst_ref=out_refs[mi].at[c], send_sem=send_sems.at[mi],
                                              recv_sem=recv_sems.at[mi], device_id=(x, y, 1 - c), device_id_type=MESH)
            cp.start()
            sends.append(cp)
        for mi in range(N_MATS):
            got = out_refs[mi].at[1 - c]
            pltpu.make_async_remote_copy(src_ref=got, dst_ref=got, send_sem=send_sems.at[mi], recv_sem=recv_sems.at[mi],
                                         device_id=(x, y, 1 - c), device_id_type=MESH).wait_recv()
        for cp in sends:
            cp.wait_send()

    return pl.pallas_call(
        body, name="swap_halves",
        in_specs=[ANY] * N_MATS, out_specs=[ANY] * N_MATS,
        out_shape=[jax.ShapeDtypeStruct((2, r // 2, c), F32) for r, c in SHARD_SHAPES],
        scratch_shapes=[pltpu.SemaphoreType.DMA((N_MATS,)), pltpu.SemaphoreType.DMA((N_MATS,))],
    )(*halves)


def _set_slot(arr, block, idx):
    return lax.dynamic_update_slice(arr, block[None], (idx,) + (0,) * block.ndim)


PAD_RUNS = ((6304, 8352, 0), (5280, 6304, COL_Z), (672, 5280, COL_QKV), (0, 640, COL_LAT), (640, 672, COL_LAT + 704))
W_IN_SHARD = 2088


def _full_weights(gathered):
    def cols(a):
        return jnp.concatenate([a[s] for s in range(4)], axis=1)

    w_uq, w_ukv, w_pm, w_pd = [cols(a) for a in gathered[1:5]]
    w_out = gathered[5].reshape(D_MODEL, D_MODEL)
    g_in = gathered[0]
    pieces, at = [], 0
    for lo, hi, pad_lo in sorted(PAD_RUNS, key=lambda t: t[2]):
        if pad_lo > at:
            pieces.append(jnp.zeros((D_MODEL, pad_lo - at), g_in.dtype))
        for s in range(4):
            a_, b_ = max(lo, s * W_IN_SHARD), min(hi, (s + 1) * W_IN_SHARD)
            if a_ < b_:
                pieces.append(g_in[s][:, a_ - s * W_IN_SHARD:b_ - s * W_IN_SHARD])
        at = pad_lo + hi - lo
    pieces.append(jnp.zeros((D_MODEL, N_PAD - at), g_in.dtype))
    w_pad = jnp.concatenate(pieces, axis=1)
    z32 = jnp.zeros((Q_RANK, 32), w_uq.dtype)
    wuq_pad = jnp.concatenate([t for h in range(MLA_HEADS) for t in (w_uq[:, h * 96:(h + 1) * 96], z32)], axis=1)
    z64 = jnp.zeros((KV_RANK, 64), w_ukv.dtype)
    wk_pad = jnp.concatenate([t for h in range(MLA_HEADS) for t in (w_ukv[:, h * 128:h * 128 + 64], z64)], axis=1)
    wv = jnp.concatenate([w_ukv[:, h * 128 + 64:(h + 1) * 128] for h in range(MLA_HEADS)], axis=1)
    return w_pad, wuq_pad, wk_pad, wv, w_pm, w_pd, w_out


def _grad_parts(dw_pad, dwuq_pad, dwk_pad, dwv, dwpm, dwpd, dwout):
    def in_block(s, h):
        rows = slice(h * 512, (h + 1) * 512)
        out = []
        for lo, hi, pad_lo in sorted(PAD_RUNS):
            a_, b_ = max(lo, s * W_IN_SHARD), min(hi, (s + 1) * W_IN_SHARD)
            if a_ < b_:
                out.append(dw_pad[rows, pad_lo + a_ - lo:pad_lo + b_ - lo])
        return jnp.concatenate(out, axis=1).astype(BF16)

    d_uq = jnp.concatenate([dwuq_pad[:, h * 128:h * 128 + 96] for h in range(MLA_HEADS)], axis=1)
    d_ukv = jnp.concatenate([t for h in range(MLA_HEADS) for t in (dwk_pad[:, h * 128:h * 128 + 64], dwv[:, h * 64:(h + 1) * 64])],
                            axis=1)

    def col_blocks(m):
        r, c = m.shape[0] // 2, m.shape[1] // 4
        return jnp.stack([m[h * r:(h + 1) * r, s * c:(s + 1) * c].astype(BF16) for s in range(4) for h in range(2)])

    return [jnp.stack([in_block(s, h) for s in range(4) for h in range(2)]), col_blocks(d_uq), col_blocks(d_ukv),
            col_blocks(dwpm), col_blocks(dwpd), dwout.astype(BF16).reshape(8, 128, D_MODEL)]


def _rope_tables(positions):
    pos = positions.reshape(SEQ).astype(F32)
    lane = jnp.arange(128)

    def table(rot, first, period):
        inv = ROPE_THETA ** (-jnp.arange(0, rot, 2, dtype=F32) / rot)
        half = rot // 2
        off = lane % period - first
        in1, in2 = (off >= 0) & (off < half), (off >= half) & (off < rot)
        inv_lane = jnp.where(in1 | in2, inv[jnp.clip(off % half, 0, half - 1)], 0.0)
        sign = jnp.where(in1, -1.0, 1.0).astype(F32)
        ang = pos[:, None] * inv_lane[None, :]
        return jnp.cos(ang), jnp.sin(ang) * sign[None, :]

    return table(32, 64, 128), table(16, 0, 64)


def _device_grads(x, positions, target, gains, gathered):
    pre_g, q_g, kv_g, post_g = gains
    w_pad, wuq_pad, wk_pad, wv, w_pm, w_pd, w_out = _full_weights(gathered)
    (mc, ms), (dc, ds) = _rope_tables(positions)

    h = _prenorm_fwd(x, pre_g)
    p = _matmul(h, w_pad, "nn", F32, 512, 1408, 1024, "in_proj")
    cqn, ckvn, q, k, v = _mla_prep_fwd(p, q_g, kv_g, wuq_pad, wk_pad, wv, mc, ms)
    ya, lse_m = _mla_flash_fwd(q, k, v)
    qkv = [_dil_prep_fwd(p, dc, ds, g) for g in range(3)]
    o_g, l_g = zip(*[_dil_attn_fwd(qkv[g], g) for g in range(3)])
    (dp, dy, mg, dt, ua, dpa, ud, dpd, dya, dyd, yd, lse_d, loss_cols, dg_post) = _tail(
        p, ya, o_g, l_g, x, target, w_pm, w_pd, w_out, post_g)

    dq, dk, dv = _mla_flash_bwd(q, k, v, ya, dya, lse_m)
    dp, dqb, dkb, dvb, dg_q, dg_kv = _mla_prep_bwd(dp, p, dq, dk, dv, q_g, kv_g, wuq_pad, wk_pad, wv, mc, ms)
    for g in range(3):
        dqkv = _dil_attn_bwd(qkv[g], dyd, yd, lse_d, g)
        dp = _dil_prep_bwd(dp, dqkv, dc, ds, g)

    dw_pad = _matmul(h, dp, "tn", F32, 1024, 1408, 512, "dw_in")
    dwuq_pad = _matmul(cqn, dqb, "tn", F32, Q_RANK, 1024, 512, "dw_uq")
    dwk_pad = _matmul(ckvn, dkb, "tn", F32, KV_RANK, 1024, 512, "dw_k")
    dwv = _matmul(ckvn, dvb, "tn", F32, KV_RANK, 512, 512, "dw_v")
    dwpm = _matmul(ua, dpa, "tn", F32, 512, 1024, 512, "dw_proj_mla")
    dwpd = _matmul(ud, dpd, "tn", F32, 512, 1024, 512, "dw_proj_dil")
    dwout = _matmul(mg, dt, "tn", F32, 1024, 1024, 512, "dw_out")
    dh = _matmul(dp, w_pad, "nt", F32, 512, 1024, 1408, "dh")
    grad_x, dg_pre = _prenorm_bwd(x, dh, dy, pre_g)

    parts = _grad_parts(dw_pad, dwuq_pad, dwk_pad, dwv, dwpm, dwpd, dwout)
    gvec = jnp.concatenate([dg_pre, dg_q, dg_kv, dg_post], axis=1)
    return loss_cols, grad_x, parts, gvec


def kernel(x, positions, pre_norm_g, w_in, q_norm_g, w_uq, kv_norm_g, w_ukv, w_proj_mla, w_proj_dil, w_out, post_norm_g, loss_target, m_pre_norm_g, m_w_in, m_q_norm_g, m_w_uq, m_kv_norm_g, m_w_ukv, m_w_proj_mla, m_w_proj_dil, m_w_out, m_post_norm_g, v_pre_norm_g, v_w_in, v_q_norm_g, v_w_uq, v_kv_norm_g, v_w_ukv, v_w_proj_mla, v_w_proj_dil, v_w_out, v_post_norm_g):
    xi, yi, ci = _my_place()
    chip, me = 2 * xi + yi, 4 * xi + 2 * yi + ci
    mats = [w.reshape(w.shape[1:]).astype(BF16) for w in (w_in, w_uq, w_ukv, w_proj_mla, w_proj_dil, w_out)]
    gathered = [_set_slot(g, m, chip) for g, m in zip(_allgather_weights(mats), mats)]
    gains = (pre_norm_g, q_norm_g, kv_norm_g, post_norm_g)
    loss_cols, grad_x, parts, gvec = _device_grads(x[0], positions, loss_target[0], gains, gathered)

    loss = lax.psum(jnp.sum(loss_cols) * (0.5 / D_MODEL), ("x", "y", "c"))

    gvec8 = jnp.pad(gvec, ((0, 7), (0, 0)))
    recv = _exchange_grads(parts, gvec8)
    own = [lax.dynamic_index_in_dim(p, me, 0, keepdims=False) for p in parts] + [gvec8]
    recv = [_set_slot(r, o, me) for r, o in zip(recv, own)]
    halves = [_sum_parts(recv[mi], 64, f"sum_grad_{mi}") for mi in range(N_MATS)]
    g_gains = _sum_parts(recv[N_MATS], 8, "sum_gain_parts")[0:1]
    g_mats = [_set_slot(s, hf, ci).reshape((1,) + shp) for s, hf, shp in zip(_swap_halves(halves), halves, SHARD_SHAPES)]

    off = [0, 1024, 1408, 1664, 2688]
    g_gain = [g_gains[:, off[i]:off[i + 1]] for i in range(4)]
    grads = [g_gain[0], g_mats[0], g_gain[1], g_mats[1], g_gain[2], g_mats[2], g_mats[3], g_mats[4], g_mats[5], g_gain[3]]
    ws = [pre_norm_g, w_in, q_norm_g, w_uq, kv_norm_g, w_ukv, w_proj_mla, w_proj_dil, w_out, post_norm_g]
    ms = [m_pre_norm_g, m_w_in, m_q_norm_g, m_w_uq, m_kv_norm_g, m_w_ukv, m_w_proj_mla, m_w_proj_dil, m_w_out, m_post_norm_g]
    vs = [v_pre_norm_g, v_w_in, v_q_norm_g, v_w_uq, v_kv_norm_g, v_w_ukv, v_w_proj_mla, v_w_proj_dil, v_w_out, v_post_norm_g]
    deltas, new_m, new_v = [], [], []
    for i, (w, g, m, v) in enumerate(zip(ws, grads, ms, vs)):
        shp = w.shape
        two_d = shp[-2:]
        d_, m_, v_ = _adamw(w.reshape(two_d), g.reshape(two_d), m.reshape(two_d), v.reshape(two_d), f"adamw_{i}")
        deltas.append(d_.reshape(shp))
        new_m.append(m_.reshape(shp))
        new_v.append(v_.reshape(shp))
    return (loss, grad_x.reshape(x.shape), *grads, *deltas, *new_m, *new_v)
```

```python
import jax
import jax.numpy as jnp
from jax import lax
from jax.experimental import pallas as pl
from jax.experimental.pallas import tpu as pltpu

F32 = jnp.float32
BF16 = jnp.bfloat16

SEQ = 4096
D_MODEL = 1024
EPS = 1e-6
ROPE_THETA = 500000.0
MLA_HEADS = 8
Q_RANK = 384
KV_RANK = 256
MLA_SCALE = 96.0 ** -0.5
MLA_ROPE_HALF = 16
DIL_DILATIONS = (1, 4, 16)
DIL_ROPE_HALF = 8
DIL_SCALE = 0.125
BAND = 128

N_LAT = 768
COL_Z, COL_QKV, COL_LAT = 2048, 3072, 7680
N_PAD = 8448
IN_SPLITS = (384, 256, 32, 4608, 512, 512, 1024, 1024)

SHARD_SHAPES = ((1024, 2088), (384, 192), (256, 256), (512, 256), (512, 256), (256, 1024))
N_MATS = len(SHARD_SHAPES)
N_GAINS = 2688

ADAM_LR, ADAM_B1, ADAM_B2, ADAM_EPS, ADAM_WD, ADAM_STEP = 0.001, 0.9, 0.999, 1e-08, 0.01, 10

VMEM_LIMIT = 56 * 1024 * 1024
NEG = -1e30
MESH = pl.DeviceIdType.MESH


def _cparams(**kw):
    return pltpu.CompilerParams(vmem_limit_bytes=VMEM_LIMIT, **kw)


def _dot(a, b, dims):
    return lax.dot_general(a, b, (dims, ((), ())), preferred_element_type=F32)


def _nn(a, b):
    return _dot(a, b, ((1,), (0,)))


def _nt(a, b):
    return _dot(a, b, ((1,), (1,)))


def _tn(a, b):
    return _dot(a, b, ((0,), (0,)))


def _rope_lanes(shape, half, period, first):
    lane = lax.broadcasted_iota(jnp.int32, shape, len(shape) - 1) % period
    return (lane >= first) & (lane < first + half), (lane >= first + half) & (lane < first + 2 * half)


def _rope_fwd(x, c, s, half, lanes):
    x1, _ = lanes
    return x * c + jnp.where(x1, pltpu.roll(x, 128 - half, 1), pltpu.roll(x, half, 1)) * s


def _rope_bwd(g, c, s, half, lanes):
    x1, x2 = lanes
    gs = g * s
    return g * c + jnp.where(x2, pltpu.roll(gs, half, 1), jnp.where(x1, pltpu.roll(gs, 128 - half, 1), 0.0))


def _sigmoid(x):
    return 1.0 / (1.0 + jnp.exp(-x))


def _matmul(a, b, mode, out_dtype, tm, tn, tk, name):
    if mode == "nn":
        (m, k), n = a.shape, b.shape[1]
        a_spec = pl.BlockSpec((tm, tk), lambda j, i, kk: (i, kk))
        b_spec = pl.BlockSpec((tk, tn), lambda j, i, kk: (kk, j))
        dot = _nn
    elif mode == "nt":
        (m, k), n = a.shape, b.shape[0]
        a_spec = pl.BlockSpec((tm, tk), lambda j, i, kk: (i, kk))
        b_spec = pl.BlockSpec((tn, tk), lambda j, i, kk: (j, kk))
        dot = _nt
    else:
        (k, m), n = a.shape, b.shape[1]
        a_spec = pl.BlockSpec((tk, tm), lambda j, i, kk: (kk, i))
        b_spec = pl.BlockSpec((tk, tn), lambda j, i, kk: (kk, j))
        dot = _tn
    assert m % tm == 0 and n % tn == 0 and k % tk == 0, (name, m, n, k, tm, tn, tk)
    nk = k // tk

    def body(a_ref, b_ref, o_ref, acc_ref):
        kk = pl.program_id(2)
        part = dot(a_ref[...], b_ref[...])

        @pl.when(kk == 0)
        def _():
            acc_ref[...] = part

        @pl.when(kk > 0)
        def _():
            acc_ref[...] += part

        @pl.when(kk == nk - 1)
        def _():
            o_ref[...] = acc_ref[...].astype(o_ref.dtype)

    return pl.pallas_call(
        body, name=name, grid=(n // tn, m // tm, nk),
        in_specs=[a_spec, b_spec],
        out_specs=pl.BlockSpec((tm, tn), lambda j, i, kk: (i, j)),
        out_shape=jax.ShapeDtypeStruct((m, n), out_dtype),
        scratch_shapes=[pltpu.VMEM((tm, tn), F32)],
        compiler_params=_cparams(),
    )(a, b)


def _prenorm_fwd(x, g):
    tm = 512

    def body(x_ref, g_ref, h_ref):
        xv = x_ref[...]
        r = lax.rsqrt(jnp.mean(xv * xv, axis=-1, keepdims=True) + EPS)
        h_ref[...] = (xv * r * g_ref[...]).astype(BF16)

    return pl.pallas_call(
        body, name="prenorm_fwd", grid=(SEQ // tm,),
        in_specs=[pl.BlockSpec((tm, D_MODEL), lambda i: (i, 0)), pl.BlockSpec((1, D_MODEL), lambda i: (0, 0))],
        out_specs=pl.BlockSpec((tm, D_MODEL), lambda i: (i, 0)),
        out_shape=jax.ShapeDtypeStruct((SEQ, D_MODEL), BF16),
    )(x, g)


def _prenorm_bwd(x, dh, dy, g):
    tm = 512

    def body(x_ref, dh_ref, dy_ref, g_ref, gx_ref, dg_ref):
        xv = x_ref[...]
        r = lax.rsqrt(jnp.mean(xv * xv, axis=-1, keepdims=True) + EPS)
        n = xv * r
        dhv = dh_ref[...]
        dn = dhv * g_ref[...]
        gx_ref[...] = dy_ref[...] + r * (dn - n * jnp.mean(dn * n, axis=-1, keepdims=True))
        part = jnp.sum(dhv * n, axis=0, keepdims=True)

        @pl.when(pl.program_id(0) == 0)
        def _():
            dg_ref[...] = part

        @pl.when(pl.program_id(0) > 0)
        def _():
            dg_ref[...] += part

    row = pl.BlockSpec((tm, D_MODEL), lambda i: (i, 0))
    vec = pl.BlockSpec((1, D_MODEL), lambda i: (0, 0))
    return pl.pallas_call(
        body, name="prenorm_bwd", grid=(SEQ // tm,),
        in_specs=[row, row, row, vec], out_specs=[row, vec],
        out_shape=[jax.ShapeDtypeStruct((SEQ, D_MODEL), F32), jax.ShapeDtypeStruct((1, D_MODEL), F32)],
        compiler_params=_cparams(),
    )(x, dh, dy, g)


def _mla_prep_fwd(p, qg, kvg, wuq, wk, wv, rc, rs):
    tm = 512

    def body(lat_ref, qg_ref, kvg_ref, wuq_ref, wk_ref, wv_ref, c_ref, s_ref,
             cqn_ref, ckvn_ref, q_ref, k_ref, v_ref):
        c, s = c_ref[...], s_ref[...]
        lanes = _rope_lanes((tm, 128), MLA_ROPE_HALF, 128, 64)
        cq = lat_ref[:, 0:Q_RANK]
        r1 = lax.rsqrt(jnp.mean(cq * cq, axis=-1, keepdims=True) + EPS)
        cqn = (cq * r1 * qg_ref[...]).astype(BF16)
        cqn_ref[...] = cqn
        q = _nn(cqn, wuq_ref[...])
        for h in range(MLA_HEADS):
            sl = slice(h * 128, (h + 1) * 128)
            q_ref[:, sl] = (_rope_fwd(q[:, sl], c, s, MLA_ROPE_HALF, lanes) * MLA_SCALE).astype(BF16)
        ckv = lat_ref[:, Q_RANK:Q_RANK + KV_RANK]
        r2 = lax.rsqrt(jnp.mean(ckv * ckv, axis=-1, keepdims=True) + EPS)
        ckvn = (ckv * r2 * kvg_ref[...]).astype(BF16)
        ckvn_ref[...] = ckvn
        krr = _rope_fwd(lat_ref[:, Q_RANK + KV_RANK:N_LAT], c, s, MLA_ROPE_HALF, lanes)
        kn = _nn(ckvn, wk_ref[...])
        for h in range(MLA_HEADS):
            sl = slice(h * 128, (h + 1) * 128)
            k_ref[:, sl] = (kn[:, sl] + krr).astype(BF16)
        v_ref[...] = _nn(ckvn, wv_ref[...]).astype(BF16)

    def full(shape):
        return pl.BlockSpec(shape, lambda i: (0, 0))

    def rows(w):
        return pl.BlockSpec((tm, w), lambda i: (i, 0))

    return pl.pallas_call(
        body, name="mla_prep_fwd", grid=(SEQ // tm,),
        in_specs=[pl.BlockSpec((tm, N_LAT), lambda i: (i, COL_LAT // N_LAT)),
                  full((1, Q_RANK)), full((1, KV_RANK)), full((Q_RANK, 1024)), full((KV_RANK, 1024)),
                  full((KV_RANK, 512)), rows(128), rows(128)],
        out_specs=[rows(Q_RANK), rows(KV_RANK), rows(1024), rows(1024), rows(512)],
        out_shape=[jax.ShapeDtypeStruct((SEQ, Q_RANK), BF16), jax.ShapeDtypeStruct((SEQ, KV_RANK), BF16),
                   jax.ShapeDtypeStruct((SEQ, 1024), BF16), jax.ShapeDtypeStruct((SEQ, 1024), BF16),
                   jax.ShapeDtypeStruct((SEQ, 512), BF16)],
        compiler_params=_cparams(),
    )(p, qg, kvg, wuq, wk, wv, rc, rs)


def _mla_prep_bwd(dp_in, p, dq, dk, dv, qg, kvg, wuq, wk, wv, rc, rs):
    tm = 512

    def body(dp_any, lat_ref, dq_ref, dk_ref, dv_ref, qg_ref, kvg_ref, wuq_ref, wk_ref, wv_ref,
             c_ref, s_ref, dp_ref, dqb_ref, dkb_ref, dvb_ref, dgq_ref, dgkv_ref):
        del dp_any
        c, s = c_ref[...], s_ref[...]
        lanes = _rope_lanes((tm, 128), MLA_ROPE_HALF, 128, 64)
        lane = lax.broadcasted_iota(jnp.int32, (tm, 128), 1)
        dkr = jnp.zeros((tm, 128), F32)
        for h in range(MLA_HEADS):
            sl = slice(h * 128, (h + 1) * 128)
            dqb_ref[:, sl] = _rope_bwd(dq_ref[:, sl] * MLA_SCALE, c, s, MLA_ROPE_HALF, lanes).astype(BF16)
            dkh = dk_ref[:, sl]
            dkr = dkr + dkh
            dkb_ref[:, sl] = jnp.where(lane < 64, dkh, 0.0).astype(BF16)
        dkr = jnp.where((lane >= 64) & (lane < 96), dkr, 0.0)
        dkr = _rope_bwd(dkr, c, s, MLA_ROPE_HALF, lanes)
        dvb = dv_ref[...].astype(BF16)
        dvb_ref[...] = dvb

        cq = lat_ref[:, 0:Q_RANK]
        r1 = lax.rsqrt(jnp.mean(cq * cq, axis=-1, keepdims=True) + EPS)
        n1 = cq * r1
        dcqn = _nt(dqb_ref[...], wuq_ref[...])
        dn1 = dcqn * qg_ref[...]
        dcq = r1 * (dn1 - n1 * jnp.mean(dn1 * n1, axis=-1, keepdims=True))
        pq = jnp.sum(dcqn * n1, axis=0, keepdims=True)

        ckv = lat_ref[:, Q_RANK:Q_RANK + KV_RANK]
        r2 = lax.rsqrt(jnp.mean(ckv * ckv, axis=-1, keepdims=True) + EPS)
        n2 = ckv * r2
        dckvn = _nt(dkb_ref[...], wk_ref[...]) + _nt(dvb, wv_ref[...])
        dn2 = dckvn * kvg_ref[...]
        dckv = r2 * (dn2 - n2 * jnp.mean(dn2 * n2, axis=-1, keepdims=True))
        pkv = jnp.sum(dckvn * n2, axis=0, keepdims=True)

        dp_ref[:, 0:Q_RANK] = dcq.astype(BF16)
        dp_ref[:, Q_RANK:Q_RANK + KV_RANK] = dckv.astype(BF16)
        dp_ref[:, Q_RANK + KV_RANK:N_LAT] = dkr.astype(BF16)

        @pl.when(pl.program_id(0) == 0)
        def _():
            dgq_ref[...] = pq
            dgkv_ref[...] = pkv

        @pl.when(pl.program_id(0) > 0)
        def _():
            dgq_ref[...] += pq
            dgkv_ref[...] += pkv

    def full(shape):
        return pl.BlockSpec(shape, lambda i: (0, 0))

    def rows(w):
        return pl.BlockSpec((tm, w), lambda i: (i, 0))

    lat = pl.BlockSpec((tm, N_LAT), lambda i: (i, COL_LAT // N_LAT))
    return pl.pallas_call(
        body, name="mla_prep_bwd", grid=(SEQ // tm,),
        in_specs=[pl.BlockSpec(memory_space=pl.ANY), lat, rows(1024), rows(1024), rows(512),
                  full((1, Q_RANK)), full((1, KV_RANK)), full((Q_RANK, 1024)), full((KV_RANK, 1024)),
                  full((KV_RANK, 512)), rows(128), rows(128)],
        out_specs=[lat, rows(1024), rows(1024), rows(512), full((1, Q_RANK)), full((1, KV_RANK))],
        out_shape=[jax.ShapeDtypeStruct((SEQ, N_PAD), BF16), jax.ShapeDtypeStruct((SEQ, 1024), BF16),
                   jax.ShapeDtypeStruct((SEQ, 1024), BF16), jax.ShapeDtypeStruct((SEQ, 512), BF16),
                   jax.ShapeDtypeStruct((1, Q_RANK), F32), jax.ShapeDtypeStruct((1, KV_RANK), F32)],
        input_output_aliases={0: 0},
        compiler_params=_cparams(),
    )(dp_in, p, dq, dk, dv, qg, kvg, wuq, wk, wv, rc, rs)


FLASH_T = 512


def _head_half(shape, hh):
    lane = lax.broadcasted_iota(jnp.int32, shape, 1)
    return (lane < 64) if hh == 0 else (lane >= 64)


def _causal_keep(t):
    row = lax.broadcasted_iota(jnp.int32, (t, t), 0)
    col = lax.broadcasted_iota(jnp.int32, (t, t), 1)
    return row >= col


def _tri_steps(nb, q_major):
    if q_major:
        pairs = [(i, kb) for i in range(nb) for kb in range(i + 1)]
    else:
        pairs = [(i, kb) for kb in range(nb) for i in range(kb, nb)]
    return jnp.asarray([p[0] for p in pairs], jnp.int32), jnp.asarray([p[1] for p in pairs], jnp.int32)


def _mla_flash_fwd(q, k, v):
    t = FLASH_T
    nb = SEQ // t
    qtab, ktab = _tri_steps(nb, True)

    def body(qi_ref, ki_ref, q_ref, k_ref, v_ref, o_ref, lse_ref, m_scr, l_scr, acc_scr):
        step = pl.program_id(1)
        i, kb = qi_ref[step], ki_ref[step]

        @pl.when(kb == 0)
        def _():
            m_scr[...] = jnp.full_like(m_scr, NEG)
            l_scr[...] = jnp.zeros_like(l_scr)
            acc_scr[...] = jnp.zeros_like(acc_scr)

        def update(masked):
            vv = v_ref[...]
            for hh in range(2):
                sl = slice(hh * 128, (hh + 1) * 128)
                s = _nt(q_ref[:, sl], k_ref[:, sl])
                if masked:
                    s = jnp.where(_causal_keep(t), s, NEG)
                m_prev = m_scr[hh]
                m_new = jnp.maximum(m_prev, jnp.max(s, axis=-1, keepdims=True))
                pr = jnp.exp(s - jnp.tile(m_new, (1, t // 128)))
                alpha = jnp.exp(m_prev - m_new)
                l_scr[hh] = alpha * l_scr[hh] + jnp.sum(pr, axis=-1, keepdims=True)
                acc_scr[hh] = alpha * acc_scr[hh] + _nn(pr.astype(BF16), vv)
                m_scr[hh] = m_new

        @pl.when(kb < i)
        def _():
            update(False)

        @pl.when(kb == i)
        def _():
            update(True)
            o0 = acc_scr[0] / l_scr[0]
            o1 = acc_scr[1] / l_scr[1]
            o_ref[...] = jnp.where(_head_half((t, 128), 0), o0, o1)
            for hh in range(2):
                lse_ref[:, hh * 128:(hh + 1) * 128] = m_scr[hh] + jnp.log(l_scr[hh])

    grid_spec = pltpu.PrefetchScalarGridSpec(
        num_scalar_prefetch=2, grid=(4, qtab.shape[0]),
        in_specs=[pl.BlockSpec((t, 256), lambda j, s, qi, ki: (qi[s], j)),
                  pl.BlockSpec((t, 256), lambda j, s, qi, ki: (ki[s], j)),
                  pl.BlockSpec((t, 128), lambda j, s, qi, ki: (ki[s], j))],
        out_specs=[pl.BlockSpec((t, 128), lambda j, s, qi, ki: (qi[s], j)),
                   pl.BlockSpec((t, 256), lambda j, s, qi, ki: (qi[s], j))],
        scratch_shapes=[pltpu.VMEM((2, t, 128), F32), pltpu.VMEM((2, t, 128), F32), pltpu.VMEM((2, t, 128), F32)])
    return pl.pallas_call(
        body, name="mla_flash_fwd", grid_spec=grid_spec,
        out_shape=[jax.ShapeDtypeStruct((SEQ, 512), F32), jax.ShapeDtypeStruct((SEQ, 1024), F32)],
        compiler_params=_cparams(),
    )(qtab, ktab, q, k, v)


def _mla_flash_bwd(q, k, v, o, do, lse):
    t = FLASH_T
    nb = SEQ // t
    qtab, ktab = _tri_steps(nb, False)

    def body(qi_ref, ki_ref, q_ref, k_ref, v_ref, o_ref, do_ref, lse_ref, dq_ref, dk_ref, dv_ref, dk_scr, dv_scr):
        step = pl.program_id(1)
        i, kb = qi_ref[step], ki_ref[step]

        @pl.when(step == 0)
        def _():
            dq_ref[...] = jnp.zeros_like(dq_ref)

        @pl.when(i == kb)
        def _():
            dk_scr[...] = jnp.zeros_like(dk_scr)
            dv_scr[...] = jnp.zeros_like(dv_scr)

        def update(masked):
            vv = v_ref[...]
            ov = o_ref[...]
            dov = do_ref[...]
            rows = pl.ds(pl.multiple_of(i * t, t), t)
            for hh in range(2):
                sl = slice(hh * 128, (hh + 1) * 128)
                qh, kh = q_ref[:, sl], k_ref[:, sl]
                s = _nt(qh, kh)
                if masked:
                    s = jnp.where(_causal_keep(t), s, NEG)
                pr = jnp.exp(s - jnp.tile(lse_ref[:, sl], (1, t // 128)))
                dom = jnp.where(_head_half((t, 128), hh), dov, 0.0)
                domb = dom.astype(BF16)
                dv_scr[...] += _tn(pr.astype(BF16), domb)
                dpr = _nt(domb, vv)
                delta = jnp.sum(dom * ov, axis=-1, keepdims=True)
                ds = (pr * (dpr - delta)).astype(BF16)
                dq_ref[rows, sl] += _nn(ds, kh)
                dk_scr[hh] += _tn(ds, qh)

        @pl.when(i > kb)
        def _():
            update(False)

        @pl.when(i == kb)
        def _():
            update(True)

        @pl.when(i == nb - 1)
        def _():
            dk_ref[:, 0:128] = dk_scr[0]
            dk_ref[:, 128:256] = dk_scr[1]
            dv_ref[...] = dv_scr[...]

    qi_map = lambda j, s, qi, ki: (qi[s], j)
    ki_map = lambda j, s, qi, ki: (ki[s], j)
    grid_spec = pltpu.PrefetchScalarGridSpec(
        num_scalar_prefetch=2, grid=(4, qtab.shape[0]),
        in_specs=[pl.BlockSpec((t, 256), qi_map), pl.BlockSpec((t, 256), ki_map), pl.BlockSpec((t, 128), ki_map),
                  pl.BlockSpec((t, 128), qi_map), pl.BlockSpec((t, 128), qi_map), pl.BlockSpec((t, 256), qi_map)],
        out_specs=[pl.BlockSpec((SEQ, 256), lambda j, s, qi, ki: (0, j)), pl.BlockSpec((t, 256), ki_map),
                   pl.BlockSpec((t, 128), ki_map)],
        scratch_shapes=[pltpu.VMEM((2, t, 128), F32), pltpu.VMEM((t, 128), F32)])
    return pl.pallas_call(
        body, name="mla_flash_bwd", grid_spec=grid_spec,
        out_shape=[jax.ShapeDtypeStruct((SEQ, 1024), F32), jax.ShapeDtypeStruct((SEQ, 1024), F32),
                   jax.ShapeDtypeStruct((SEQ, 512), F32)],
        compiler_params=_cparams(),
    )(qtab, ktab, q, k, v, o, do, lse)


DIL_UNROLL = 4


def _strided(start, size, d):
    return pl.ds(start, size) if d == 1 else pl.ds(start, size, stride=d)


def _dil_prep_fwd(p, rc, rs, g):
    d = DIL_DILATIONS[g]
    sub_len = SEQ // d
    ch = min(sub_len, 512)

    def body(p_ref, c_ref, s_ref, o_ref):
        tq = pl.program_id(0)
        is_v = tq == 2
        mult = jnp.where(tq == 0, DIL_SCALE, 1.0).astype(F32)
        lanes = _rope_lanes((ch, 128), DIL_ROPE_HALF, 64, 0)
        for r in range(d):
            for c0 in range(0, sub_len, ch):
                rows = _strided(r + c0 * d, ch, d)
                xv = p_ref[rows, :]
                roped = _rope_fwd(xv, c_ref[rows, :], s_ref[rows, :], DIL_ROPE_HALF, lanes)
                o_ref[0, r * sub_len + c0:r * sub_len + c0 + ch, :] = (jnp.where(is_v, xv, roped) * mult).astype(BF16)

    tab = pl.BlockSpec((SEQ, 128), lambda tq, pr: (0, 0))
    return pl.pallas_call(
        body, name=f"dil_prep_fwd_g{g}", grid=(3, 4),
        in_specs=[pl.BlockSpec((SEQ, 128), lambda tq, pr: (0, COL_QKV // 128 + (tq * 3 + g) * 4 + pr)), tab, tab],
        out_specs=pl.BlockSpec((1, SEQ, 128), lambda tq, pr: (tq, 0, pr)),
        out_shape=jax.ShapeDtypeStruct((3, SEQ, 512), BF16),
        compiler_params=_cparams(),
    )(p, rc, rs)


def _band_masks():
    row = lax.broadcasted_iota(jnp.int32, (BAND, BAND), 0)
    col = lax.broadcasted_iota(jnp.int32, (BAND, BAND), 1)
    return row >= col, col >= row


def _dil_attn_fwd(qkv, g):
    d = DIL_DILATIONS[g]
    nbs = SEQ // d // BAND
    nblk = SEQ // BAND

    def body(q_ref, k_ref, v_ref, o_ref, l_ref):
        keep_c, keep_p = _band_masks()
        half0 = _head_half((BAND, 128), 0)

        def step(i, carry):
            cur = pl.ds(pl.multiple_of(i * BAND, BAND), BAND)
            prv = pl.ds(pl.multiple_of(jnp.maximum(i - 1, 0) * BAND, BAND), BAND)
            has_prev = (i % nbs) != 0
            qv = q_ref[0, cur, :]
            kc, kp = k_ref[0, cur, :], k_ref[0, prv, :]
            vc, vp = v_ref[0, cur, :], v_ref[0, prv, :]
            outs, lses = [], []
            for hh in range(2):
                qm = jnp.where(_head_half((BAND, 128), hh), qv, jnp.zeros_like(qv))
                sc = jnp.where(keep_c, _nt(qm, kc), NEG)
                sp = jnp.where(keep_p & has_prev, _nt(qm, kp), NEG)
                m = jnp.maximum(jnp.max(sc, axis=-1, keepdims=True), jnp.max(sp, axis=-1, keepdims=True))
                pc, pp = jnp.exp(sc - m), jnp.exp(sp - m)
                den = jnp.sum(pc, axis=-1, keepdims=True) + jnp.sum(pp, axis=-1, keepdims=True)
                o = (_nn(pc.astype(BF16), vc) + _nn(pp.astype(BF16), vp)) / den
                outs.append(o)
                lses.append(jnp.broadcast_to(m + jnp.log(den), (BAND, 128)))
            tok = _strided((i % nbs) * BAND * d + i // nbs, BAND, d)
            o_ref[tok, :] = jnp.where(half0, outs[0], outs[1])
            l_ref[tok, :] = jnp.where(half0, lses[0], lses[1])
            return carry

        lax.fori_loop(0, nblk, step, 0, unroll=DIL_UNROLL)

    def inp(tq):
        return pl.BlockSpec((1, SEQ, 128), lambda pr: (tq, 0, pr))

    out = pl.BlockSpec((SEQ, 128), lambda pr: (0, pr))
    return pl.pallas_call(
        body, name=f"dil_attn_fwd_g{g}", grid=(4,),
        in_specs=[inp(0), inp(1), inp(2)], out_specs=[out, out],
        out_shape=[jax.ShapeDtypeStruct((SEQ, 512), F32), jax.ShapeDtypeStruct((SEQ, 512), F32)],
        compiler_params=_cparams(),
    )(qkv, qkv, qkv)


def _dil_attn_bwd(qkv, dyd, yd, lse_all, g):
    d = DIL_DILATIONS[g]
    sub_len = SEQ // d
    nbs = sub_len // BAND
    nblk = SEQ // BAND

    def body(q_ref, k_ref, v_ref, do_ref, y_ref, l_ref, out_ref, dk_scr, dv_scr):
        keep_c, keep_p = _band_masks()
        dk_scr[...] = jnp.zeros_like(dk_scr)
        dv_scr[...] = jnp.zeros_like(dv_scr)

        def step(i, carry):
            cur = pl.ds(pl.multiple_of(i * BAND, BAND), BAND)
            prv = pl.ds(pl.multiple_of(jnp.maximum(i - 1, 0) * BAND, BAND), BAND)
            has_prev = (i % nbs) != 0
            tok = _strided((i % nbs) * BAND * d + i // nbs, BAND, d)
            qv = q_ref[0, cur, :]
            kc, kp = k_ref[0, cur, :], k_ref[0, prv, :]
            vc, vp = v_ref[0, cur, :], v_ref[0, prv, :]
            dov, yv, lv = do_ref[tok, :], y_ref[tok, :], l_ref[tok, :]
            dq = jnp.zeros((BAND, 128), F32)
            dkc = jnp.zeros((BAND, 128), F32)
            dkp = jnp.zeros((BAND, 128), F32)
            dvc = jnp.zeros((BAND, 128), F32)
            dvp = jnp.zeros((BAND, 128), F32)
            for hh in range(2):
                half = _head_half((BAND, 128), hh)
                qm = jnp.where(half, qv, jnp.zeros_like(qv))
                lcol = jnp.max(jnp.where(half, lv, NEG), axis=-1, keepdims=True)
                pc = jnp.exp(jnp.where(keep_c, _nt(qm, kc), NEG) - lcol)
                pp = jnp.exp(jnp.where(keep_p & has_prev, _nt(qm, kp), NEG) - lcol)
                dom = jnp.where(half, dov, 0.0)
                domb = dom.astype(BF16)
                delta = jnp.sum(dom * yv, axis=-1, keepdims=True)
                dsc = (pc * (_nt(domb, vc) - delta)).astype(BF16)
                dsp = (pp * (_nt(domb, vp) - delta)).astype(BF16)
                dvc = dvc + _tn(pc.astype(BF16), domb)
                dvp = dvp + _tn(pp.astype(BF16), domb)
                dq = dq + jnp.where(half, _nn(dsc, kc) + _nn(dsp, kp), 0.0)
                dkc = dkc + jnp.where(half, _tn(dsc, qv), 0.0)
                dkp = dkp + jnp.where(half, _tn(dsp, qv), 0.0)
            out_ref[pl.ds(0, 1), tok, :] = dq[None]
            dk_scr[cur, :] += dkc
            dk_scr[prv, :] += dkp
            dv_scr[cur, :] += dvc
            dv_scr[prv, :] += dvp
            return carry

        lax.fori_loop(0, nblk, step, 0, unroll=DIL_UNROLL)
        for r in range(d):
            rows = _strided(r, sub_len, d)
            out_ref[pl.ds(1, 1), rows, :] = dk_scr[r * sub_len:(r + 1) * sub_len, :][None]
            out_ref[pl.ds(2, 1), rows, :] = dv_scr[r * sub_len:(r + 1) * sub_len, :][None]

    def inp(tq):
        return pl.BlockSpec((1, SEQ, 128), lambda pr: (tq, 0, pr))

    tok_spec = pl.BlockSpec((SEQ, 128), lambda pr: (0, pr))
    return pl.pallas_call(
        body, name=f"dil_attn_bwd_g{g}", grid=(4,),
        in_specs=[inp(0), inp(1), inp(2), tok_spec, tok_spec, tok_spec],
        out_specs=pl.BlockSpec((3, SEQ, 128), lambda pr: (0, 0, pr)),
        out_shape=jax.ShapeDtypeStruct((3, SEQ, 512), F32),
        scratch_shapes=[pltpu.VMEM((SEQ, 128), F32), pltpu.VMEM((SEQ, 128), F32)],
        compiler_params=_cparams(),
    )(qkv, qkv, qkv, dyd, yd, lse_all)


def _dil_prep_bwd(dp_in, dqkv, rc, rs, g):
    tm = 512

    def body(dp_any, g_ref, c_ref, s_ref, dp_ref):
        del dp_any
        tq = pl.program_id(0)
        mult = jnp.where(tq == 0, DIL_SCALE, 1.0).astype(F32)
        lanes = _rope_lanes((tm, 128), DIL_ROPE_HALF, 64, 0)
        cv, sv = c_ref[...], s_ref[...]
        for pr in range(4):
            gv = g_ref[0, :, pr * 128:(pr + 1) * 128]
            roped = _rope_bwd(gv, cv, sv, DIL_ROPE_HALF, lanes)
            dp_ref[:, pr * 128:(pr + 1) * 128] = (jnp.where(tq == 2, gv, roped) * mult).astype(BF16)

    tab = pl.BlockSpec((tm, 128), lambda tq, i: (i, 0))
    return pl.pallas_call(
        body, name=f"dil_prep_bwd_g{g}", grid=(3, SEQ // tm),
        in_specs=[pl.BlockSpec(memory_space=pl.ANY),
                  pl.BlockSpec((1, tm, 512), lambda tq, i: (tq, i, 0)), tab, tab],
        out_specs=pl.BlockSpec((tm, 512), lambda tq, i: (i, COL_QKV // 512 + tq * 3 + g)),
        out_shape=jax.ShapeDtypeStruct((SEQ, N_PAD), BF16),
        input_output_aliases={0: 0},
    )(dp_in, dqkv, rc, rs)


TAIL_T = 128


def _tail(p, ya, o_g, l_g, x, target, wpm, wpd, wout, post_g):
    tm = TAIL_T

    def body(pgz_ref, ya_ref, o0_ref, o1_ref, o2_ref, l0_ref, l1_ref, l2_ref, x_ref, t_ref,
             wpm_ref, wpd_ref, wout_ref, pg_ref,
             dp_ref, dy_ref, mg_ref, dt_ref, ua_ref, dpa_ref, ud_ref, dpd_ref, dya_ref, dyd_ref,
             yd_ref, lse_ref, loss_ref, dgp_ref):
        l0, l1, l2 = l0_ref[...], l1_ref[...], l2_ref[...]
        mx = jnp.maximum(jnp.maximum(l0, l1), l2)
        e0, e1, e2 = jnp.exp(l0 - mx), jnp.exp(l1 - mx), jnp.exp(l2 - mx)
        den = e0 + e1 + e2
        yd = (e0 * o0_ref[...] + e1 * o1_ref[...] + e2 * o2_ref[...]) / den
        yd_ref[...] = yd
        lse_ref[...] = mx + jnp.log(den)
        ya = ya_ref[...]

        gm, gd = pgz_ref[:, 0:1024], pgz_ref[:, 1024:2048]
        zm, zd = pgz_ref[:, 2048:2560], pgz_ref[:, 2560:3072]
        szm, szd = _sigmoid(zm), _sigmoid(zd)
        sm, sd = zm * szm, zd * szd
        ua = (ya * sm).astype(BF16)
        ud = (yd * sd).astype(BF16)
        ua_ref[...] = ua
        ud_ref[...] = ud
        pa = _nn(ua, wpm_ref[...])
        pd = _nn(ud, wpd_ref[...])
        sgm, sgd = _sigmoid(gm), _sigmoid(gd)
        mg = (sgm * pa + sgd * pd).astype(BF16)
        mg_ref[...] = mg
        t = _nn(mg, wout_ref[...])
        r3 = lax.rsqrt(jnp.mean(t * t, axis=-1, keepdims=True) + EPS)
        n = t * r3
        pg = pg_ref[...]
        err = x_ref[...] + n * pg - t_ref[...]
        lpart = jnp.sum(err * err, axis=0, keepdims=True)

        dy = err * (1.0 / D_MODEL)
        dy_ref[...] = dy
        gpart = jnp.sum(dy * n, axis=0, keepdims=True)
        dn = dy * pg
        dt = (r3 * (dn - n * jnp.mean(dn * n, axis=-1, keepdims=True))).astype(BF16)
        dt_ref[...] = dt
        dmg = _nt(dt, wout_ref[...])
        dpa = (dmg * sgm).astype(BF16)
        dpd = (dmg * sgd).astype(BF16)
        dpa_ref[...] = dpa
        dpd_ref[...] = dpd
        dp_ref[:, 0:1024] = (dmg * pa * sgm * (1.0 - sgm)).astype(BF16)
        dp_ref[:, 1024:2048] = (dmg * pd * sgd * (1.0 - sgd)).astype(BF16)
        dua = _nt(dpa, wpm_ref[...])
        dud = _nt(dpd, wpd_ref[...])
        dya_ref[...] = dua * sm
        dyd_ref[...] = dud * sd
        dp_ref[:, 2048:2560] = (dua * ya * szm * (1.0 + zm * (1.0 - szm))).astype(BF16)
        dp_ref[:, 2560:3072] = (dud * yd * szd * (1.0 + zd * (1.0 - szd))).astype(BF16)

        @pl.when(pl.program_id(0) == 0)
        def _():
            loss_ref[...] = lpart
            dgp_ref[...] = gpart

        @pl.when(pl.program_id(0) > 0)
        def _():
            loss_ref[...] += lpart
            dgp_ref[...] += gpart

    def rows(w):
        return pl.BlockSpec((tm, w), lambda i: (i, 0))

    def full(shape):
        return pl.BlockSpec(shape, lambda i: (0, 0))

    def sds(w, dt):
        return jax.ShapeDtypeStruct((SEQ, w), dt)

    return pl.pallas_call(
        body, name="tail", grid=(SEQ // tm,),
        in_specs=[rows(3072), rows(512), rows(512), rows(512), rows(512), rows(512), rows(512), rows(512),
                  rows(1024), rows(1024), full((512, 1024)), full((512, 1024)), full((1024, 1024)), full((1, 1024))],
        out_specs=[rows(3072), rows(1024), rows(1024), rows(1024), rows(512), rows(1024), rows(512), rows(1024),
                   rows(512), rows(512), rows(512), rows(512), full((1, 1024)), full((1, 1024))],
        out_shape=[sds(N_PAD, BF16), sds(1024, F32), sds(1024, BF16), sds(1024, BF16), sds(512, BF16),
                   sds(1024, BF16), sds(512, BF16), sds(1024, BF16), sds(512, F32), sds(512, F32),
                   sds(512, F32), sds(512, F32),
                   jax.ShapeDtypeStruct((1, 1024), F32), jax.ShapeDtypeStruct((1, 1024), F32)],
        compiler_params=_cparams(),
    )(p, ya, o_g[0], o_g[1], o_g[2], l_g[0], l_g[1], l_g[2], x, target, wpm, wpd, wout, post_g)


def _sum_parts(parts, tr, name):
    n, r, w = parts.shape

    def body(p_ref, o_ref):
        acc = p_ref[0].astype(F32)
        for s in range(1, n):
            acc = acc + p_ref[s].astype(F32)
        o_ref[...] = acc

    return pl.pallas_call(
        body, name=name, grid=(r // tr,),
        in_specs=[pl.BlockSpec((n, tr, w), lambda i: (0, i, 0))],
        out_specs=pl.BlockSpec((tr, w), lambda i: (i, 0)),
        out_shape=jax.ShapeDtypeStruct((r, w), F32),
    )(parts)


def _adamw(w, g, m, v, name):
    r, c = w.shape
    tr = 128 if r % 128 == 0 else r
    c1 = 1.0 - ADAM_B1 ** ADAM_STEP
    c2 = 1.0 - ADAM_B2 ** ADAM_STEP

    def body(w_ref, g_ref, m_ref, v_ref, d_ref, nm_ref, nv_ref):
        gv = g_ref[...]
        nm = ADAM_B1 * m_ref[...] + (1.0 - ADAM_B1) * gv
        nv = ADAM_B2 * v_ref[...] + (1.0 - ADAM_B2) * (gv * gv)
        nm_ref[...] = nm
        nv_ref[...] = nv
        d_ref[...] = -ADAM_LR * ((nm / c1) / (jnp.sqrt(nv / c2) + ADAM_EPS) + ADAM_WD * w_ref[...])

    spec = pl.BlockSpec((tr, c), lambda i: (i, 0))
    sd = jax.ShapeDtypeStruct((r, c), F32)
    return pl.pallas_call(
        body, name=name, grid=(r // tr,),
        in_specs=[spec] * 4, out_specs=[spec] * 3, out_shape=[sd] * 3,
    )(w, g, m, v)


ANY = pl.BlockSpec(memory_space=pl.ANY)


def _my_place():
    return lax.axis_index("x"), lax.axis_index("y"), lax.axis_index("c")


def _allgather_weights(mats):
    def body(*refs):
        w_refs, out_refs = refs[:N_MATS], refs[N_MATS:2 * N_MATS]
        send_sems, recv_sems = refs[2 * N_MATS:]
        x, y, c = _my_place()
        sibling = (x, y, 1 - c)
        chips = [(1 - x, y), (x, 1 - y), (1 - x, 1 - y)]

        def copy(k, src, dst, to):
            return pltpu.make_async_remote_copy(src_ref=src, dst_ref=dst, send_sem=send_sems.at[k],
                                                recv_sem=recv_sems.at[k], device_id=to, device_id_type=MESH)

        def half(mi, shard, hc):
            hr = SHARD_SHAPES[mi][0] // 2
            return out_refs[mi].at[shard, pl.ds(pl.multiple_of(hc * hr, 16), hr), :]

        started = []
        for mi in range(N_MATS):
            hr = SHARD_SHAPES[mi][0] // 2
            my_half = w_refs[mi].at[pl.ds(pl.multiple_of(c * hr, 16), hr), :]
            for j, (cx, cy) in enumerate(chips):
                cp = copy(mi * 6 + j, my_half, half(mi, 2 * x + y, c), (cx, cy, c))
                cp.start()
                started.append(cp)
        for mi in range(N_MATS):
            for j, (cx, cy) in enumerate(chips):
                landed = half(mi, 2 * cx + cy, c)
                copy(mi * 6 + j, landed, landed, (cx, cy, c)).wait_recv()
                fw = copy(mi * 6 + 3 + j, landed, landed, sibling)
                fw.start()
                started.append(fw)
        for mi in range(N_MATS):
            for j, (cx, cy) in enumerate(chips):
                other = half(mi, 2 * cx + cy, 1 - c)
                copy(mi * 6 + 3 + j, other, other, sibling).wait_recv()
        for cp in started:
            cp.wait_send()

    return pl.pallas_call(
        body, name="allgather_weights",
        in_specs=[ANY] * N_MATS, out_specs=[ANY] * N_MATS,
        out_shape=[jax.ShapeDtypeStruct((4, r, c), BF16) for r, c in SHARD_SHAPES],
        scratch_shapes=[pltpu.SemaphoreType.DMA((6 * N_MATS,)), pltpu.SemaphoreType.DMA((6 * N_MATS,))],
    )(*mats)


def _exchange_grads(parts, gvec):
    n = N_MATS + 1

    def body(*refs):
        p_refs, r_refs = refs[:n], refs[n:2 * n]
        send_sems, recv_sems = refs[2 * n:]
        x, y, c = _my_place()
        me = 4 * x + 2 * y + c
        sends = []
        for k in range(1, 8):
            px, py, pc = x ^ (k >> 2), y ^ ((k >> 1) & 1), c ^ (k & 1)
            peer = 4 * px + 2 * py + pc
            for mi in range(n):
                src = p_refs[mi].at[peer] if mi < N_MATS else p_refs[mi]
                cp = pltpu.make_async_remote_copy(
                    src_ref=src, dst_ref=r_refs[mi].at[me], send_sem=send_sems.at[(k - 1) * n + mi],
                    recv_sem=recv_sems.at[(k - 1) * n + mi], device_id=(px, py, pc), device_id_type=MESH)
                cp.start()
                sends.append(cp)
        for k in range(1, 8):
            px, py, pc = x ^ (k >> 2), y ^ ((k >> 1) & 1), c ^ (k & 1)
            peer = 4 * px + 2 * py + pc
            for mi in range(n):
                slot = r_refs[mi].at[peer]
                pltpu.make_async_remote_copy(
                    src_ref=slot, dst_ref=slot, send_sem=send_sems.at[(k - 1) * n + mi],
                    recv_sem=recv_sems.at[(k - 1) * n + mi], device_id=(px, py, pc), device_id_type=MESH).wait_recv()
        for cp in sends:
            cp.wait_send()

    return pl.pallas_call(
        body, name="exchange_grads",
        in_specs=[ANY] * n, out_specs=[ANY] * n,
        out_shape=[jax.ShapeDtypeStruct((8, r // 2, c), BF16) for r, c in SHARD_SHAPES]
        + [jax.ShapeDtypeStruct((8, 8, N_GAINS), F32)],
        scratch_shapes=[pltpu.SemaphoreType.DMA((7 * n,)), pltpu.SemaphoreType.DMA((7 * n,))],
    )(*parts, gvec)


def _swap_halves(halves):
    def body(*refs):
        g_refs, out_refs = refs[:N_MATS], refs[N_MATS:2 * N_MATS]
        send_sems, recv_sems = refs[2 * N_MATS:]
        x, y, c = _my_place()
        sends = []
        for mi in range(N_MATS):
            cp = pltpu.make_async_remote_copy(src_ref=g_refs[mi], dst_ref=out_refs[mi].at[c], send_sem=send_sems.at[mi],
                                              recv_sem=recv_sems.at[mi], device_id=(x, y, 1 - c), device_id_type=MESH)
            cp.start()
            sends.append(cp)
        for mi in range(N_MATS):
            got = out_refs[mi].at[1 - c]
            pltpu.make_async_remote_copy(src_ref=got, dst_ref=got, send_sem=send_sems.at[mi], recv_sem=recv_sems.at[mi],
                                         device_id=(x, y, 1 - c), device_id_type=MESH).wait_recv()
        for cp in sends:
            cp.wait_send()

    return pl.pallas_call(
        body, name="swap_halves",
        in_specs=[ANY] * N_MATS, out_specs=[ANY] * N_MATS,
        out_shape=[jax.ShapeDtypeStruct((2, r // 2, c), F32) for r, c in SHARD_SHAPES],
        scratch_shapes=[pltpu.SemaphoreType.DMA((N_MATS,)), pltpu.SemaphoreType.DMA((N_MATS,))],
    )(*halves)


def _set_slot(arr, block, idx):
    return lax.dynamic_update_slice(arr, block[None], (idx,) + (0,) * block.ndim)


PAD_RUNS = ((6304, 8352, 0), (5280, 6304, COL_Z), (672, 5280, COL_QKV), (0, 640, COL_LAT), (640, 672, COL_LAT + 704))
W_IN_SHARD = 2088


def _full_weights(gathered):
    def cols(a):
        return jnp.concatenate([a[s] for s in range(4)], axis=1)

    w_uq, w_ukv, w_pm, w_pd = [cols(a) for a in gathered[1:5]]
    w_out = gathered[5].reshape(D_MODEL, D_MODEL)
    g_in = gathered[0]
    pieces, at = [], 0
    for lo, hi, pad_lo in sorted(PAD_RUNS, key=lambda t: t[2]):
        if pad_lo > at:
            pieces.append(jnp.zeros((D_MODEL, pad_lo - at), g_in.dtype))
        for s in range(4):
            a_, b_ = max(lo, s * W_IN_SHARD), min(hi, (s + 1) * W_IN_SHARD)
            if a_ < b_:
                pieces.append(g_in[s][:, a_ - s * W_IN_SHARD:b_ - s * W_IN_SHARD])
        at = pad_lo + hi - lo
    pieces.append(jnp.zeros((D_MODEL, N_PAD - at), g_in.dtype))
    w_pad = jnp.concatenate(pieces, axis=1)
    z32 = jnp.zeros((Q_RANK, 32), w_uq.dtype)
    wuq_pad = jnp.concatenate([t for h in range(MLA_HEADS) for t in (w_uq[:, h * 96:(h + 1) * 96], z32)], axis=1)
    z64 = jnp.zeros((KV_RANK, 64), w_ukv.dtype)
    wk_pad = jnp.concatenate([t for h in range(MLA_HEADS) for t in (w_ukv[:, h * 128:h * 128 + 64], z64)], axis=1)
    wv = jnp.concatenate([w_ukv[:, h * 128 + 64:(h + 1) * 128] for h in range(MLA_HEADS)], axis=1)
    return w_pad, wuq_pad, wk_pad, wv, w_pm, w_pd, w_out


def _grad_parts(dw_pad, dwuq_pad, dwk_pad, dwv, dwpm, dwpd, dwout):
    def in_block(s, h):
        rows = slice(h * 512, (h + 1) * 512)
        out = []
        for lo, hi, pad_lo in sorted(PAD_RUNS):
            a_, b_ = max(lo, s * W_IN_SHARD), min(hi, (s + 1) * W_IN_SHARD)
            if a_ < b_:
                out.append(dw_pad[rows, pad_lo + a_ - lo:pad_lo + b_ - lo])
        return jnp.concatenate(out, axis=1).astype(BF16)

    d_uq = jnp.concatenate([dwuq_pad[:, h * 128:h * 128 + 96] for h in range(MLA_HEADS)], axis=1)
    d_ukv = jnp.concatenate([t for h in range(MLA_HEADS) for t in (dwk_pad[:, h * 128:h * 128 + 64], dwv[:, h * 64:(h + 1) * 64])],
                            axis=1)

    def col_blocks(m):
        r, c = m.shape[0] // 2, m.shape[1] // 4
        return jnp.stack([m[h * r:(h + 1) * r, s * c:(s + 1) * c].astype(BF16) for s in range(4) for h in range(2)])

    return [jnp.stack([in_block(s, h) for s in range(4) for h in range(2)]), col_blocks(d_uq), col_blocks(d_ukv),
            col_blocks(dwpm), col_blocks(dwpd), dwout.astype(BF16).reshape(8, 128, D_MODEL)]


def _rope_tables(positions):
    pos = positions.reshape(SEQ).astype(F32)
    lane = jnp.arange(128)

    def table(rot, first, period):
        inv = ROPE_THETA ** (-jnp.arange(0, rot, 2, dtype=F32) / rot)
        half = rot // 2
        off = lane % period - first
        in1, in2 = (off >= 0) & (off < half), (off >= half) & (off < rot)
        inv_lane = jnp.where(in1 | in2, inv[jnp.clip(off % half, 0, half - 1)], 0.0)
        sign = jnp.where(in1, -1.0, 1.0).astype(F32)
        ang = pos[:, None] * inv_lane[None, :]
        return jnp.cos(ang), jnp.sin(ang) * sign[None, :]

    return table(32, 64, 128), table(16, 0, 64)


def _device_grads(x, positions, target, gains, gathered):
    pre_g, q_g, kv_g, post_g = gains
    w_pad, wuq_pad, wk_pad, wv, w_pm, w_pd, w_out = _full_weights(gathered)
    (mc, ms), (dc, ds) = _rope_tables(positions)

    h = _prenorm_fwd(x, pre_g)
    p = _matmul(h, w_pad, "nn", F32, 512, 1408, 1024, "in_proj")
    cqn, ckvn, q, k, v = _mla_prep_fwd(p, q_g, kv_g, wuq_pad, wk_pad, wv, mc, ms)
    ya, lse_m = _mla_flash_fwd(q, k, v)
    qkv = [_dil_prep_fwd(p, dc, ds, g) for g in range(3)]
    o_g, l_g = zip(*[_dil_attn_fwd(qkv[g], g) for g in range(3)])
    (dp, dy, mg, dt, ua, dpa, ud, dpd, dya, dyd, yd, lse_d, loss_cols, dg_post) = _tail(
        p, ya, o_g, l_g, x, target, w_pm, w_pd, w_out, post_g)

    dq, dk, dv = _mla_flash_bwd(q, k, v, ya, dya, lse_m)
    dp, dqb, dkb, dvb, dg_q, dg_kv = _mla_prep_bwd(dp, p, dq, dk, dv, q_g, kv_g, wuq_pad, wk_pad, wv, mc, ms)
    for g in range(3):
        dqkv = _dil_attn_bwd(qkv[g], dyd, yd, lse_d, g)
        dp = _dil_prep_bwd(dp, dqkv, dc, ds, g)

    dw_pad = _matmul(h, dp, "tn", F32, 1024, 1408, 512, "dw_in")
    dwuq_pad = _matmul(cqn, dqb, "tn", F32, Q_RANK, 1024, 512, "dw_uq")
    dwk_pad = _matmul(ckvn, dkb, "tn", F32, KV_RANK, 1024, 512, "dw_k")
    dwv = _matmul(ckvn, dvb, "tn", F32, KV_RANK, 512, 512, "dw_v")
    dwpm = _matmul(ua, dpa, "tn", F32, 512, 1024, 512, "dw_proj_mla")
    dwpd = _matmul(ud, dpd, "tn", F32, 512, 1024, 512, "dw_proj_dil")
    dwout = _matmul(mg, dt, "tn", F32, 1024, 1024, 512, "dw_out")
    dh = _matmul(dp, w_pad, "nt", F32, 512, 1024, 1408, "dh")
    grad_x, dg_pre = _prenorm_bwd(x, dh, dy, pre_g)

    parts = _grad_parts(dw_pad, dwuq_pad, dwk_pad, dwv, dwpm, dwpd, dwout)
    gvec = jnp.concatenate([dg_pre, dg_q, dg_kv, dg_post], axis=1)
    return loss_cols, grad_x, parts, gvec


def kernel(x, positions, pre_norm_g, w_in, q_norm_g, w_uq, kv_norm_g, w_ukv, w_proj_mla, w_proj_dil, w_out, post_norm_g, loss_target, m_pre_norm_g, m_w_in, m_q_norm_g, m_w_uq, m_kv_norm_g, m_w_ukv, m_w_proj_mla, m_w_proj_dil, m_w_out, m_post_norm_g, v_pre_norm_g, v_w_in, v_q_norm_g, v_w_uq, v_kv_norm_g, v_w_ukv, v_w_proj_mla, v_w_proj_dil, v_w_out, v_post_norm_g):
    xi, yi, ci = _my_place()
    chip, me = 2 * xi + yi, 4 * xi + 2 * yi + ci
    mats = [w.reshape(w.shape[1:]).astype(BF16) for w in (w_in, w_uq, w_ukv, w_proj_mla, w_proj_dil, w_out)]
    gathered = [_set_slot(g, m, chip) for g, m in zip(_allgather_weights(mats), mats)]
    gains = (pre_norm_g, q_norm_g, kv_norm_g, post_norm_g)
    loss_cols, grad_x, parts, gvec = _device_grads(x[0], positions, loss_target[0], gains, gathered)

    loss = lax.psum(jnp.sum(loss_cols) * (0.5 / D_MODEL), ("x", "y", "c"))

    gvec8 = jnp.pad(gvec, ((0, 7), (0, 0)))
    recv = _exchange_grads(parts, gvec8)
    own = [lax.dynamic_index_in_dim(p, me, 0, keepdims=False) for p in parts] + [gvec8]
    recv = [_set_slot(r, o, me) for r, o in zip(recv, own)]
    halves = [_sum_parts(recv[mi], 64, f"sum_grad_{mi}") for mi in range(N_MATS)]
    g_gains = _sum_parts(recv[N_MATS], 8, "sum_gain_parts")[0:1]
    g_mats = [_set_slot(s, hf, ci).reshape((1,) + shp) for s, hf, shp in zip(_swap_halves(halves), halves, SHARD_SHAPES)]

    off = [0, 1024, 1408, 1664, 2688]
    g_gain = [g_gains[:, off[i]:off[i + 1]] for i in range(4)]
    grads = [g_gain[0], g_mats[0], g_gain[1], g_mats[1], g_gain[2], g_mats[2], g_mats[3], g_mats[4], g_mats[5], g_gain[3]]
    ws = [pre_norm_g, w_in, q_norm_g, w_uq, kv_norm_g, w_ukv, w_proj_mla, w_proj_dil, w_out, post_norm_g]
    ms = [m_pre_norm_g, m_w_in, m_q_norm_g, m_w_uq, m_kv_norm_g, m_w_ukv, m_w_proj_mla, m_w_proj_dil, m_w_out, m_post_norm_g]
    vs = [v_pre_norm_g, v_w_in, v_q_norm_g, v_w_uq, v_kv_norm_g, v_w_ukv, v_w_proj_mla, v_w_proj_dil, v_w_out, v_post_norm_g]
    deltas, new_m, new_v = [], [], []
    for i, (w, g, m, v) in enumerate(zip(ws, grads, ms, vs)):
        shp = w.shape
        two_d = shp[-2:]
        d_, m_, v_ = _adamw(w.reshape(two_d), g.reshape(two_d), m.reshape(two_d), v.reshape(two_d), f"adamw_{i}")
        deltas.append(d_.reshape(shp))
        new_m.append(m_.reshape(shp))
        new_v.append(v_.reshape(shp))
    return (loss, grad_x.reshape(x.shape), *grads, *deltas, *new_m, *new_v)
```

```python
import jax
import jax.numpy as jnp
from jax import lax
from jax.experimental import pallas as pl
from jax.experimental.pallas import tpu as pltpu

F32 = jnp.float32
BF16 = jnp.bfloat16

SEQ = 4096
D_MODEL = 1024
EPS = 1e-6
ROPE_THETA = 500000.0
MLA_HEADS = 8
Q_RANK = 384
KV_RANK = 256
MLA_SCALE = 96.0 ** -0.5
MLA_ROPE_HALF = 16
DIL_DILATIONS = (1, 4, 16)
DIL_ROPE_HALF = 8
DIL_SCALE = 0.125
BAND = 128

N_LAT = 768
COL_Z, COL_QKV, COL_LAT = 2048, 3072, 7680
N_PAD = 8448
IN_SPLITS = (384, 256, 32, 4608, 512, 512, 1024, 1024)

SHARD_SHAPES = ((1024, 2088), (384, 192), (256, 256), (512, 256), (512, 256), (256, 1024))
N_MATS = len(SHARD_SHAPES)
N_GAINS = 2688

ADAM_LR, ADAM_B1, ADAM_B2, ADAM_EPS, ADAM_WD, ADAM_STEP = 0.001, 0.9, 0.999, 1e-08, 0.01, 10

VMEM_LIMIT = 56 * 1024 * 1024
NEG = -1e30
MESH = pl.DeviceIdType.MESH


def _cparams(**kw):
    return pltpu.CompilerParams(vmem_limit_bytes=VMEM_LIMIT, **kw)


def _dot(a, b, dims):
    return lax.dot_general(a, b, (dims, ((), ())), preferred_element_type=F32)


def _nn(a, b):
    return _dot(a, b, ((1,), (0,)))


def _nt(a, b):
    return _dot(a, b, ((1,), (1,)))


def _tn(a, b):
    return _dot(a, b, ((0,), (0,)))


def _rope_lanes(shape, half, period, first):
    lane = lax.broadcasted_iota(jnp.int32, shape, len(shape) - 1) % period
    return (lane >= first) & (lane < first + half), (lane >= first + half) & (lane < first + 2 * half)


def _rope_fwd(x, c, s, half, lanes):
    x1, _ = lanes
    return x * c + jnp.where(x1, pltpu.roll(x, 128 - half, 1), pltpu.roll(x, half, 1)) * s


def _rope_bwd(g, c, s, half, lanes):
    x1, x2 = lanes
    gs = g * s
    return g * c + jnp.where(x2, pltpu.roll(gs, half, 1), jnp.where(x1, pltpu.roll(gs, 128 - half, 1), 0.0))


def _sigmoid(x):
    return 1.0 / (1.0 + jnp.exp(-x))


def _matmul(a, b, mode, out_dtype, tm, tn, tk, name):
    if mode == "nn":
        (m, k), n = a.shape, b.shape[1]
        a_spec = pl.BlockSpec((tm, tk), lambda j, i, kk: (i, kk))
        b_spec = pl.BlockSpec((tk, tn), lambda j, i, kk: (kk, j))
        dot = _nn
    elif mode == "nt":
        (m, k), n = a.shape, b.shape[0]
        a_spec = pl.BlockSpec((tm, tk), lambda j, i, kk: (i, kk))
        b_spec = pl.BlockSpec((tn, tk), lambda j, i, kk: (j, kk))
        dot = _nt
    else:
        (k, m), n = a.shape, b.shape[1]
        a_spec = pl.BlockSpec((tk, tm), lambda j, i, kk: (kk, i))
        b_spec = pl.BlockSpec((tk, tn), lambda j, i, kk: (kk, j))
        dot = _tn
    assert m % tm == 0 and n % tn == 0 and k % tk == 0, (name, m, n, k, tm, tn, tk)
    nk = k // tk

    def body(a_ref, b_ref, o_ref, acc_ref):
        kk = pl.program_id(2)
        part = dot(a_ref[...], b_ref[...])

        @pl.when(kk == 0)
        def _():
            acc_ref[...] = part

        @pl.when(kk > 0)
        def _():
            acc_ref[...] += part

        @pl.when(kk == nk - 1)
        def _():
            o_ref[...] = acc_ref[...].astype(o_ref.dtype)

    return pl.pallas_call(
        body, name=name, grid=(n // tn, m // tm, nk),
        in_specs=[a_spec, b_spec],
        out_specs=pl.BlockSpec((tm, tn), lambda j, i, kk: (i, j)),
        out_shape=jax.ShapeDtypeStruct((m, n), out_dtype),
        scratch_shapes=[pltpu.VMEM((tm, tn), F32)],
        compiler_params=_cparams(),
    )(a, b)


def _prenorm_fwd(x, g):
    tm = 512

    def body(x_ref, g_ref, h_ref):
        xv = x_ref[...]
        r = lax.rsqrt(jnp.mean(xv * xv, axis=-1, keepdims=True) + EPS)
        h_ref[...] = (xv * r * g_ref[...]).astype(BF16)

    return pl.pallas_call(
        body, name="prenorm_fwd", grid=(SEQ // tm,),
        in_specs=[pl.BlockSpec((tm, D_MODEL), lambda i: (i, 0)), pl.BlockSpec((1, D_MODEL), lambda i: (0, 0))],
        out_specs=pl.BlockSpec((tm, D_MODEL), lambda i: (i, 0)),
        out_shape=jax.ShapeDtypeStruct((SEQ, D_MODEL), BF16),
    )(x, g)


def _prenorm_bwd(x, dh, dy, g):
    tm = 512

    def body(x_ref, dh_ref, dy_ref, g_ref, gx_ref, dg_ref):
        xv = x_ref[...]
        r = lax.rsqrt(jnp.mean(xv * xv, axis=-1, keepdims=True) + EPS)
        n = xv * r
        dhv = dh_ref[...]
        dn = dhv * g_ref[...]
        gx_ref[...] = dy_ref[...] + r * (dn - n * jnp.mean(dn * n, axis=-1, keepdims=True))
        part = jnp.sum(dhv * n, axis=0, keepdims=True)

        @pl.when(pl.program_id(0) == 0)
        def _():
            dg_ref[...] = part

        @pl.when(pl.program_id(0) > 0)
        def _():
            dg_ref[...] += part

    row = pl.BlockSpec((tm, D_MODEL), lambda i: (i, 0))
    vec = pl.BlockSpec((1, D_MODEL), lambda i: (0, 0))
    return pl.pallas_call(
        body, name="prenorm_bwd", grid=(SEQ // tm,),
        in_specs=[row, row, row, vec], out_specs=[row, vec],
        out_shape=[jax.ShapeDtypeStruct((SEQ, D_MODEL), F32), jax.ShapeDtypeStruct((1, D_MODEL), F32)],
        compiler_params=_cparams(),
    )(x, dh, dy, g)


def _mla_prep_fwd(p, qg, kvg, wuq, wk, wv, rc, rs):
    tm = 512

    def body(lat_ref, qg_ref, kvg_ref, wuq_ref, wk_ref, wv_ref, c_ref, s_ref,
             cqn_ref, ckvn_ref, q_ref, k_ref, v_ref):
        c, s = c_ref[...], s_ref[...]
        lanes = _rope_lanes((tm, 128), MLA_ROPE_HALF, 128, 64)
        cq = lat_ref[:, 0:Q_RANK]
        r1 = lax.rsqrt(jnp.mean(cq * cq, axis=-1, keepdims=True) + EPS)
        cqn = (cq * r1 * qg_ref[...]).astype(BF16)
        cqn_ref[...] = cqn
        q = _nn(cqn, wuq_ref[...])
        for h in range(MLA_HEADS):
            sl = slice(h * 128, (h + 1) * 128)
            q_ref[:, sl] = (_rope_fwd(q[:, sl], c, s, MLA_ROPE_HALF, lanes) * MLA_SCALE).astype(BF16)
        ckv = lat_ref[:, Q_RANK:Q_RANK + KV_RANK]
        r2 = lax.rsqrt(jnp.mean(ckv * ckv, axis=-1, keepdims=True) + EPS)
        ckvn = (ckv * r2 * kvg_ref[...]).astype(BF16)
        ckvn_ref[...] = ckvn
        krr = _rope_fwd(lat_ref[:, Q_RANK + KV_RANK:N_LAT], c, s, MLA_ROPE_HALF, lanes)
        kn = _nn(ckvn, wk_ref[...])
        for h in range(MLA_HEADS):
            sl = slice(h * 128, (h + 1) * 128)
            k_ref[:, sl] = (kn[:, sl] + krr).astype(BF16)
        v_ref[...] = _nn(ckvn, wv_ref[...]).astype(BF16)

    def full(shape):
        return pl.BlockSpec(shape, lambda i: (0, 0))

    def rows(w):
        return pl.BlockSpec((tm, w), lambda i: (i, 0))

    return pl.pallas_call(
        body, name="mla_prep_fwd", grid=(SEQ // tm,),
        in_specs=[pl.BlockSpec((tm, N_LAT), lambda i: (i, COL_LAT // N_LAT)),
                  full((1, Q_RANK)), full((1, KV_RANK)), full((Q_RANK, 1024)), full((KV_RANK, 1024)),
                  full((KV_RANK, 512)), rows(128), rows(128)],
        out_specs=[rows(Q_RANK), rows(KV_RANK), rows(1024), rows(1024), rows(512)],
        out_shape=[jax.ShapeDtypeStruct((SEQ, Q_RANK), BF16), jax.ShapeDtypeStruct((SEQ, KV_RANK), BF16),
                   jax.ShapeDtypeStruct((SEQ, 1024), BF16), jax.ShapeDtypeStruct((SEQ, 1024), BF16),
                   jax.ShapeDtypeStruct((SEQ, 512), BF16)],
        compiler_params=_cparams(),
    )(p, qg, kvg, wuq, wk, wv, rc, rs)


def _mla_prep_bwd(dp_in, p, dq, dk, dv, qg, kvg, wuq, wk, wv, rc, rs):
    tm = 512

    def body(dp_any, lat_ref, dq_ref, dk_ref, dv_ref, qg_ref, kvg_ref, wuq_ref, wk_ref, wv_ref,
             c_ref, s_ref, dp_ref, dqb_ref, dkb_ref, dvb_ref, dgq_ref, dgkv_ref):
        del dp_any
        c, s = c_ref[...], s_ref[...]
        lanes = _rope_lanes((tm, 128), MLA_ROPE_HALF, 128, 64)
        lane = lax.broadcasted_iota(jnp.int32, (tm, 128), 1)
        dkr = jnp.zeros((tm, 128), F32)
        for h in range(MLA_HEADS):
            sl = slice(h * 128, (h + 1) * 128)
            dqb_ref[:, sl] = _rope_bwd(dq_ref[:, sl] * MLA_SCALE, c, s, MLA_ROPE_HALF, lanes).astype(BF16)
            dkh = dk_ref[:, sl]
            dkr = dkr + dkh
            dkb_ref[:, sl] = jnp.where(lane < 64, dkh, 0.0).astype(BF16)
        dkr = jnp.where((lane >= 64) & (lane < 96), dkr, 0.0)
        dkr = _rope_bwd(dkr, c, s, MLA_ROPE_HALF, lanes)
        dvb = dv_ref[...].astype(BF16)
        dvb_ref[...] = dvb

        cq = lat_ref[:, 0:Q_RANK]
        r1 = lax.rsqrt(jnp.mean(cq * cq, axis=-1, keepdims=True) + EPS)
        n1 = cq * r1
        dcqn = _nt(dqb_ref[...], wuq_ref[...])
        dn1 = dcqn * qg_ref[...]
        dcq = r1 * (dn1 - n1 * jnp.mean(dn1 * n1, axis=-1, keepdims=True))
        pq = jnp.sum(dcqn * n1, axis=0, keepdims=True)

        ckv = lat_ref[:, Q_RANK:Q_RANK + KV_RANK]
        r2 = lax.rsqrt(jnp.mean(ckv * ckv, axis=-1, keepdims=True) + EPS)
        n2 = ckv * r2
        dckvn = _nt(dkb_ref[...], wk_ref[...]) + _nt(dvb, wv_ref[...])
        dn2 = dckvn * kvg_ref[...]
        dckv = r2 * (dn2 - n2 * jnp.mean(dn2 * n2, axis=-1, keepdims=True))
        pkv = jnp.sum(dckvn * n2, axis=0, keepdims=True)

        dp_ref[:, 0:Q_RANK] = dcq.astype(BF16)
        dp_ref[:, Q_RANK:Q_RANK + KV_RANK] = dckv.astype(BF16)
        dp_ref[:, Q_RANK + KV_RANK:N_LAT] = dkr.astype(BF16)

        @pl.when(pl.program_id(0) == 0)
        def _():
            dgq_ref[...] = pq
            dgkv_ref[...] = pkv

        @pl.when(pl.program_id(0) > 0)
        def _():
            dgq_ref[...] += pq
            dgkv_ref[...] += pkv

    def full(shape):
        return pl.BlockSpec(shape, lambda i: (0, 0))

    def rows(w):
        return pl.BlockSpec((tm, w), lambda i: (i, 0))

    lat = pl.BlockSpec((tm, N_LAT), lambda i: (i, COL_LAT // N_LAT))
    return pl.pallas_call(
        body, name="mla_prep_bwd", grid=(SEQ // tm,),
        in_specs=[pl.BlockSpec(memory_space=pl.ANY), lat, rows(1024), rows(1024), rows(512),
                  full((1, Q_RANK)), full((1, KV_RANK)), full((Q_RANK, 1024)), full((KV_RANK, 1024)),
                  full((KV_RANK, 512)), rows(128), rows(128)],
        out_specs=[lat, rows(1024), rows(1024), rows(512), full((1, Q_RANK)), full((1, KV_RANK))],
        out_shape=[jax.ShapeDtypeStruct((SEQ, N_PAD), BF16), jax.ShapeDtypeStruct((SEQ, 1024), BF16),
                   jax.ShapeDtypeStruct((SEQ, 1024), BF16), jax.ShapeDtypeStruct((SEQ, 512), BF16),
                   jax.ShapeDtypeStruct((1, Q_RANK), F32), jax.ShapeDtypeStruct((1, KV_RANK), F32)],
        input_output_aliases={0: 0},
        compiler_params=_cparams(),
    )(dp_in, p, dq, dk, dv, qg, kvg, wuq, wk, wv, rc, rs)


FLASH_T = 512


def _head_half(shape, hh):
    lane = lax.broadcasted_iota(jnp.int32, shape, 1)
    return (lane < 64) if hh == 0 else (lane >= 64)


def _causal_keep(t):
    row = lax.broadcasted_iota(jnp.int32, (t, t), 0)
    col = lax.broadcasted_iota(jnp.int32, (t, t), 1)
    return row >= col


def _tri_steps(nb, q_major):
    if q_major:
        pairs = [(i, kb) for i in range(nb) for kb in range(i + 1)]
    else:
        pairs = [(i, kb) for kb in range(nb) for i in range(kb, nb)]
    return jnp.asarray([p[0] for p in pairs], jnp.int32), jnp.asarray([p[1] for p in pairs], jnp.int32)


def _mla_flash_fwd(q, k, v):
    t = FLASH_T
    nb = SEQ // t
    qtab, ktab = _tri_steps(nb, True)

    def body(qi_ref, ki_ref, q_ref, k_ref, v_ref, o_ref, lse_ref, m_scr, l_scr, acc_scr):
        step = pl.program_id(1)
        i, kb = qi_ref[step], ki_ref[step]

        @pl.when(kb == 0)
        def _():
            m_scr[...] = jnp.full_like(m_scr, NEG)
            l_scr[...] = jnp.zeros_like(l_scr)
            acc_scr[...] = jnp.zeros_like(acc_scr)

        def update(masked):
            vv = v_ref[...]
            for hh in range(2):
                sl = slice(hh * 128, (hh + 1) * 128)
                s = _nt(q_ref[:, sl], k_ref[:, sl])
                if masked:
                    s = jnp.where(_causal_keep(t), s, NEG)
                m_prev = m_scr[hh]
                m_new = jnp.maximum(m_prev, jnp.max(s, axis=-1, keepdims=True))
                pr = jnp.exp(s - jnp.tile(m_new, (1, t // 128)))
                alpha = jnp.exp(m_prev - m_new)
                l_scr[hh] = alpha * l_scr[hh] + jnp.sum(pr, axis=-1, keepdims=True)
                acc_scr[hh] = alpha * acc_scr[hh] + _nn(pr.astype(BF16), vv)
                m_scr[hh] = m_new

        @pl.when(kb < i)
        def _():
            update(False)

        @pl.when(kb == i)
        def _():
            update(True)
            o0 = acc_scr[0] / l_scr[0]
            o1 = acc_scr[1] / l_scr[1]
            o_ref[...] = jnp.where(_head_half((t, 128), 0), o0, o1)
            for hh in range(2):
                lse_ref[:, hh * 128:(hh + 1) * 128] = m_scr[hh] + jnp.log(l_scr[hh])

    grid_spec = pltpu.PrefetchScalarGridSpec(
        num_scalar_prefetch=2, grid=(4, qtab.shape[0]),
        in_specs=[pl.BlockSpec((t, 256), lambda j, s, qi, ki: (qi[s], j)),
                  pl.BlockSpec((t, 256), lambda j, s, qi, ki: (ki[s], j)),
                  pl.BlockSpec((t, 128), lambda j, s, qi, ki: (ki[s], j))],
        out_specs=[pl.BlockSpec((t, 128), lambda j, s, qi, ki: (qi[s], j)),
                   pl.BlockSpec((t, 256), lambda j, s, qi, ki: (qi[s], j))],
        scratch_shapes=[pltpu.VMEM((2, t, 128), F32), pltpu.VMEM((2, t, 128), F32), pltpu.VMEM((2, t, 128), F32)])
    return pl.pallas_call(
        body, name="mla_flash_fwd", grid_spec=grid_spec,
        out_shape=[jax.ShapeDtypeStruct((SEQ, 512), F32), jax.ShapeDtypeStruct((SEQ, 1024), F32)],
        compiler_params=_cparams(),
    )(qtab, ktab, q, k, v)


def _mla_flash_bwd(q, k, v, o, do, lse):
    t = FLASH_T
    nb = SEQ // t
    qtab, ktab = _tri_steps(nb, False)

    def body(qi_ref, ki_ref, q_ref, k_ref, v_ref, o_ref, do_ref, lse_ref, dq_ref, dk_ref, dv_ref, dk_scr, dv_scr):
        step = pl.program_id(1)
        i, kb = qi_ref[step], ki_ref[step]

        @pl.when(step == 0)
        def _():
            dq_ref[...] = jnp.zeros_like(dq_ref)

        @pl.when(i == kb)
        def _():
            dk_scr[...] = jnp.zeros_like(dk_scr)
            dv_scr[...] = jnp.zeros_like(dv_scr)

        def update(masked):
            vv = v_ref[...]
            ov = o_ref[...]
            dov = do_ref[...]
            rows = pl.ds(pl.multiple_of(i * t, t), t)
            for hh in range(2):
                sl = slice(hh * 128, (hh + 1) * 128)
                qh, kh = q_ref[:, sl], k_ref[:, sl]
                s = _nt(qh, kh)
                if masked:
                    s = jnp.where(_causal_keep(t), s, NEG)
                pr = jnp.exp(s - jnp.tile(lse_ref[:, sl], (1, t // 128)))
                dom = jnp.where(_head_half((t, 128), hh), dov, 0.0)
                domb = dom.astype(BF16)
                dv_scr[...] += _tn(pr.astype(BF16), domb)
                dpr = _nt(domb, vv)
                delta = jnp.sum(dom * ov, axis=-1, keepdims=True)
                ds = (pr * (dpr - delta)).astype(BF16)
                dq_ref[rows, sl] += _nn(ds, kh)
                dk_scr[hh] += _tn(ds, qh)

        @pl.when(i > kb)
        def _():
            update(False)

        @pl.when(i == kb)
        def _():
            update(True)

        @pl.when(i == nb - 1)
        def _():
            dk_ref[:, 0:128] = dk_scr[0]
            dk_ref[:, 128:256] = dk_scr[1]
            dv_ref[...] = dv_scr[...]

    qi_map = lambda j, s, qi, ki: (qi[s], j)
    ki_map = lambda j, s, qi, ki: (ki[s], j)
    grid_spec = pltpu.PrefetchScalarGridSpec(
        num_scalar_prefetch=2, grid=(4, qtab.shape[0]),
        in_specs=[pl.BlockSpec((t, 256), qi_map), pl.BlockSpec((t, 256), ki_map), pl.BlockSpec((t, 128), ki_map),
                  pl.BlockSpec((t, 128), qi_map), pl.BlockSpec((t, 128), qi_map), pl.BlockSpec((t, 256), qi_map)],
        out_specs=[pl.BlockSpec((SEQ, 256), lambda j, s, qi, ki: (0, j)), pl.BlockSpec((t, 256), ki_map),
                   pl.BlockSpec((t, 128), ki_map)],
        scratch_shapes=[pltpu.VMEM((2, t, 128), F32), pltpu.VMEM((t, 128), F32)])
    return pl.pallas_call(
        body, name="mla_flash_bwd", grid_spec=grid_spec,
        out_shape=[jax.ShapeDtypeStruct((SEQ, 1024), F32), jax.ShapeDtypeStruct((SEQ, 1024), F32),
                   jax.ShapeDtypeStruct((SEQ, 512), F32)],
        compiler_params=_cparams(),
    )(qtab, ktab, q, k, v, o, do, lse)


DIL_UNROLL = 4


def _strided(start, size, d):
    return pl.ds(start, size) if d == 1 else pl.ds(start, size, stride=d)


def _dil_prep_fwd(p, rc, rs, g):
    d = DIL_DILATIONS[g]
    sub_len = SEQ // d
    ch = min(sub_len, 512)

    def body(p_ref, c_ref, s_ref, o_ref):
        tq = pl.program_id(0)
        is_v = tq == 2
        mult = jnp.where(tq == 0, DIL_SCALE, 1.0).astype(F32)
        lanes = _rope_lanes((ch, 128), DIL_ROPE_HALF, 64, 0)
        o_ref[0, 0:BAND, :] = jnp.zeros((BAND, 128), BF16)
        for r in range(d):
            for c0 in range(0, sub_len, ch):
                rows = _strided(r + c0 * d, ch, d)
                xv = p_ref[rows, :]
                roped = _rope_fwd(xv, c_ref[rows, :], s_ref[rows, :], DIL_ROPE_HALF, lanes)
                at = BAND + r * sub_len + c0
                o_ref[0, at:at + ch, :] = (jnp.where(is_v, xv, roped) * mult).astype(BF16)

    tab = pl.BlockSpec((SEQ, 128), lambda tq, pr: (0, 0))
    return pl.pallas_call(
        body, name=f"dil_prep_fwd_g{g}", grid=(3, 4),
        in_specs=[pl.BlockSpec((SEQ, 128), lambda tq, pr: (0, COL_QKV // 128 + (tq * 3 + g) * 4 + pr)), tab, tab],
        out_specs=pl.BlockSpec((1, BAND + SEQ, 128), lambda tq, pr: (tq, 0, pr)),
        out_shape=jax.ShapeDtypeStruct((3, BAND + SEQ, 512), BF16),
        compiler_params=_cparams(),
    )(p, rc, rs)


DIL_ST = 1024
DIL_NB = DIL_ST // BAND


def _band_keep(g, b, t):
    nbs = SEQ // DIL_DILATIONS[g] // BAND
    row = lax.broadcasted_iota(jnp.int32, (BAND, 2 * BAND), 0)
    col = lax.broadcasted_iota(jnp.int32, (BAND, 2 * BAND), 1)
    cur = (col >= BAND) & (row >= col - BAND)
    prev = (col < BAND) & (col >= row)
    if nbs >= DIL_NB:
        if b > 0:
            return cur | prev
        return cur | (prev & ((t * DIL_NB) % nbs != 0))
    return cur | prev if b % nbs else cur


def _dil_tok(g, b, t):
    d = DIL_DILATIONS[g]
    nbs = SEQ // d // BAND
    gb = t * DIL_NB + b
    return _strided((gb % nbs) * BAND * d + gb // nbs, BAND, d)


def _dil_attn_fwd2(qkv, g):
    def body(q_ref, k_ref, v_ref, o_ref, l_ref, s_scr, p_scr, o_scr):
        t = pl.program_id(1)
        base = t * DIL_ST
        half0 = _head_half((DIL_ST, 128), 0)
        lse_h = []
        for hh in range(2):
            half = _head_half((BAND, 128), hh)
            for b in range(DIL_NB):
                qv = q_ref[0, pl.ds(pl.multiple_of(base + (b + 1) * BAND, BAND), BAND), :]
                k2 = k_ref[0, pl.ds(pl.multiple_of(base + b * BAND, BAND), 2 * BAND), :]
                sb = _nt(jnp.where(half, qv, jnp.zeros_like(qv)), k2)
                s_scr[b * BAND:(b + 1) * BAND, :] = jnp.where(_band_keep(g, b, t), sb, NEG)
            s = s_scr[...]
            m = jnp.max(s, axis=-1, keepdims=True)
            pr = jnp.exp(s - m)
            den = jnp.sum(pr, axis=-1, keepdims=True)
            p_scr[...] = pr.astype(BF16)
            for b in range(DIL_NB):
                v2 = v_ref[0, pl.ds(pl.multiple_of(base + b * BAND, BAND), 2 * BAND), :]
                o_scr[hh, b * BAND:(b + 1) * BAND, :] = _nn(p_scr[b * BAND:(b + 1) * BAND, :], v2)
            o_scr[hh] = o_scr[hh] / den
            lse_h.append(m + jnp.log(den))
        out = jnp.where(half0, o_scr[0], o_scr[1])
        lse = jnp.where(half0, lse_h[0], lse_h[1])
        for b in range(DIL_NB):
            tok = _dil_tok(g, b, t)
            o_ref[tok, :] = out[b * BAND:(b + 1) * BAND, :]
            l_ref[tok, :] = lse[b * BAND:(b + 1) * BAND, :]

    def inp(tq):
        return pl.BlockSpec((1, BAND + SEQ, 128), lambda pr, t: (tq, 0, pr))

    out = pl.BlockSpec((SEQ, 128), lambda pr, t: (0, pr))
    return pl.pallas_call(
        body, name=f"dil_attn_fwd_g{g}", grid=(4, SEQ // DIL_ST),
        in_specs=[inp(0), inp(1), inp(2)], out_specs=[out, out],
        out_shape=[jax.ShapeDtypeStruct((SEQ, 512), F32), jax.ShapeDtypeStruct((SEQ, 512), F32)],
        scratch_shapes=[pltpu.VMEM((DIL_ST, 2 * BAND), F32), pltpu.VMEM((DIL_ST, 2 * BAND), BF16),
                        pltpu.VMEM((2, DIL_ST, 128), F32)],
        compiler_params=_cparams(),
    )(qkv, qkv, qkv)


def _dil_attn_bwd2(qkv, dyd, yd, lse_all, g):
    d = DIL_DILATIONS[g]
    sub_len = SEQ // d
    nst = SEQ // DIL_ST

    def body(q_ref, k_ref, v_ref, do_ref, y_ref, l_ref, out_ref,
             dk_scr, dv_scr, s_scr, dp_scr, p_scr, ds_scr, do_scr, y_scr, l_scr, dq_scr):
        t = pl.program_id(1)
        base = t * DIL_ST

        @pl.when(t == 0)
        def _():
            dk_scr[...] = jnp.zeros_like(dk_scr)
            dv_scr[...] = jnp.zeros_like(dv_scr)

        for b in range(DIL_NB):
            tok = _dil_tok(g, b, t)
            do_scr[b * BAND:(b + 1) * BAND, :] = do_ref[tok, :]
            y_scr[b * BAND:(b + 1) * BAND, :] = y_ref[tok, :]
            l_scr[b * BAND:(b + 1) * BAND, :] = l_ref[tok, :]
        for hh in range(2):
            half = _head_half((BAND, 128), hh)
            half_st = _head_half((DIL_ST, 128), hh)
            dom = jnp.where(half_st, do_scr[...], 0.0)
            delta = jnp.sum(dom * y_scr[...], axis=-1, keepdims=True)
            lcol = jnp.max(jnp.where(half_st, l_scr[...], NEG), axis=-1, keepdims=True)
            for b in range(DIL_NB):
                rows = slice(b * BAND, (b + 1) * BAND)
                qv = q_ref[0, pl.ds(pl.multiple_of(base + (b + 1) * BAND, BAND), BAND), :]
                band = pl.ds(pl.multiple_of(base + b * BAND, BAND), 2 * BAND)
                sb = _nt(jnp.where(half, qv, jnp.zeros_like(qv)), k_ref[0, band, :])
                s_scr[rows, :] = jnp.where(_band_keep(g, b, t), sb, NEG)
                dp_scr[rows, :] = _nt(dom[rows, :].astype(BF16), v_ref[0, band, :])
            pr = jnp.exp(s_scr[...] - lcol)
            p_scr[...] = pr.astype(BF16)
            ds_scr[...] = (pr * (dp_scr[...] - delta)).astype(BF16)
            for b in range(DIL_NB):
                rows = slice(b * BAND, (b + 1) * BAND)
                qv = q_ref[0, pl.ds(pl.multiple_of(base + (b + 1) * BAND, BAND), BAND), :]
                band = pl.ds(pl.multiple_of(base + b * BAND, BAND), 2 * BAND)
                dqb = jnp.where(half, _nn(ds_scr[rows, :], k_ref[0, band, :]), 0.0)
                if hh == 0:
                    dq_scr[rows, :] = dqb
                else:
                    dq_scr[rows, :] += dqb
                half2 = _head_half((2 * BAND, 128), hh)
                dk_scr[band, :] += jnp.where(half2, _tn(ds_scr[rows, :], qv), 0.0)
                dv_scr[band, :] += _tn(p_scr[rows, :], dom[rows, :].astype(BF16))
        for b in range(DIL_NB):
            out_ref[pl.ds(0, 1), _dil_tok(g, b, t), :] = dq_scr[b * BAND:(b + 1) * BAND, :][None]

        @pl.when(t == nst - 1)
        def _():
            for r in range(d):
                rows = _strided(r, sub_len, d)
                out_ref[pl.ds(1, 1), rows, :] = dk_scr[BAND + r * sub_len:BAND + (r + 1) * sub_len, :][None]
                out_ref[pl.ds(2, 1), rows, :] = dv_scr[BAND + r * sub_len:BAND + (r + 1) * sub_len, :][None]

    def inp(tq):
        return pl.BlockSpec((1, BAND + SEQ, 128), lambda pr, t: (tq, 0, pr))

    tok_spec = pl.BlockSpec((SEQ, 128), lambda pr, t: (0, pr))
    st = (DIL_ST, 2 * BAND)
    return pl.pallas_call(
        body, name=f"dil_attn_bwd_g{g}", grid=(4, nst),
        in_specs=[inp(0), inp(1), inp(2), tok_spec, tok_spec, tok_spec],
        out_specs=pl.BlockSpec((3, SEQ, 128), lambda pr, t: (0, 0, pr)),
        out_shape=jax.ShapeDtypeStruct((3, SEQ, 512), F32),
        scratch_shapes=[pltpu.VMEM((BAND + SEQ, 128), F32), pltpu.VMEM((BAND + SEQ, 128), F32),
                        pltpu.VMEM(st, F32), pltpu.VMEM(st, F32), pltpu.VMEM(st, BF16), pltpu.VMEM(st, BF16),
                        pltpu.VMEM((DIL_ST, 128), F32), pltpu.VMEM((DIL_ST, 128), F32), pltpu.VMEM((DIL_ST, 128), F32),
                        pltpu.VMEM((DIL_ST, 128), F32)],
        compiler_params=_cparams(),
    )(qkv, qkv, qkv, dyd, yd, lse_all)


def _band_masks():
    row = lax.broadcasted_iota(jnp.int32, (BAND, BAND), 0)
    col = lax.broadcasted_iota(jnp.int32, (BAND, BAND), 1)
    return row >= col, col >= row


def _dil_attn_fwd(qkv, g):
    d = DIL_DILATIONS[g]
    nbs = SEQ // d // BAND
    nblk = SEQ // BAND

    def body(q_ref, k_ref, v_ref, o_ref, l_ref):
        keep_c, keep_p = _band_masks()
        half0 = _head_half((BAND, 128), 0)

        def step(i, carry):
            cur = pl.ds(pl.multiple_of(i * BAND, BAND), BAND)
            prv = pl.ds(pl.multiple_of(jnp.maximum(i - 1, 0) * BAND, BAND), BAND)
            has_prev = (i % nbs) != 0
            qv = q_ref[0, cur, :]
            kc, kp = k_ref[0, cur, :], k_ref[0, prv, :]
            vc, vp = v_ref[0, cur, :], v_ref[0, prv, :]
            outs, lses = [], []
            for hh in range(2):
                qm = jnp.where(_head_half((BAND, 128), hh), qv, jnp.zeros_like(qv))
                sc = jnp.where(keep_c, _nt(qm, kc), NEG)
                sp = jnp.where(keep_p & has_prev, _nt(qm, kp), NEG)
                m = jnp.maximum(jnp.max(sc, axis=-1, keepdims=True), jnp.max(sp, axis=-1, keepdims=True))
                pc, pp = jnp.exp(sc - m), jnp.exp(sp - m)
                den = jnp.sum(pc, axis=-1, keepdims=True) + jnp.sum(pp, axis=-1, keepdims=True)
                o = (_nn(pc.astype(BF16), vc) + _nn(pp.astype(BF16), vp)) / den
                outs.append(o)
                lses.append(jnp.broadcast_to(m + jnp.log(den), (BAND, 128)))
            tok = _strided((i % nbs) * BAND * d + i // nbs, BAND, d)
            o_ref[tok, :] = jnp.where(half0, outs[0], outs[1])
            l_ref[tok, :] = jnp.where(half0, lses[0], lses[1])
            return carry

        lax.fori_loop(0, nblk, step, 0, unroll=DIL_UNROLL)

    def inp(tq):
        return pl.BlockSpec((1, SEQ, 128), lambda pr: (tq, 0, pr))

    out = pl.BlockSpec((SEQ, 128), lambda pr: (0, pr))
    return pl.pallas_call(
        body, name=f"dil_attn_fwd_g{g}", grid=(4,),
        in_specs=[inp(0), inp(1), inp(2)], out_specs=[out, out],
        out_shape=[jax.ShapeDtypeStruct((SEQ, 512), F32), jax.ShapeDtypeStruct((SEQ, 512), F32)],
        compiler_params=_cparams(),
    )(qkv, qkv, qkv)


def _dil_attn_bwd(qkv, dyd, yd, lse_all, g):
    d = DIL_DILATIONS[g]
    sub_len = SEQ // d
    nbs = sub_len // BAND
    nblk = SEQ // BAND

    def body(q_ref, k_ref, v_ref, do_ref, y_ref, l_ref, out_ref, dk_scr, dv_scr):
        keep_c, keep_p = _band_masks()
        dk_scr[...] = jnp.zeros_like(dk_scr)
        dv_scr[...] = jnp.zeros_like(dv_scr)

        def step(i, carry):
            cur = pl.ds(pl.multiple_of(i * BAND, BAND), BAND)
            prv = pl.ds(pl.multiple_of(jnp.maximum(i - 1, 0) * BAND, BAND), BAND)
            has_prev = (i % nbs) != 0
            tok = _strided((i % nbs) * BAND * d + i // nbs, BAND, d)
            qv = q_ref[0, cur, :]
            kc, kp = k_ref[0, cur, :], k_ref[0, prv, :]
            vc, vp = v_ref[0, cur, :], v_ref[0, prv, :]
            dov, yv, lv = do_ref[tok, :], y_ref[tok, :], l_ref[tok, :]
            dq = jnp.zeros((BAND, 128), F32)
            dkc = jnp.zeros((BAND, 128), F32)
            dkp = jnp.zeros((BAND, 128), F32)
            dvc = jnp.zeros((BAND, 128), F32)
            dvp = jnp.zeros((BAND, 128), F32)
            for hh in range(2):
                half = _head_half((BAND, 128), hh)
                qm = jnp.where(half, qv, jnp.zeros_like(qv))
                lcol = jnp.max(jnp.where(half, lv, NEG), axis=-1, keepdims=True)
                pc = jnp.exp(jnp.where(keep_c, _nt(qm, kc), NEG) - lcol)
                pp = jnp.exp(jnp.where(keep_p & has_prev, _nt(qm, kp), NEG) - lcol)
                dom = jnp.where(half, dov, 0.0)
                domb = dom.astype(BF16)
                delta = jnp.sum(dom * yv, axis=-1, keepdims=True)
                dsc = (pc * (_nt(domb, vc) - delta)).astype(BF16)
                dsp = (pp * (_nt(domb, vp) - delta)).astype(BF16)
                dvc = dvc + _tn(pc.astype(BF16), domb)
                dvp = dvp + _tn(pp.astype(BF16), domb)
                dq = dq + jnp.where(half, _nn(dsc, kc) + _nn(dsp, kp), 0.0)
                dkc = dkc + jnp.where(half, _tn(dsc, qv), 0.0)
                dkp = dkp + jnp.where(half, _tn(dsp, qv), 0.0)
            out_ref[pl.ds(0, 1), tok, :] = dq[None]
            dk_scr[cur, :] += dkc
            dk_scr[prv, :] += dkp
            dv_scr[cur, :] += dvc
            dv_scr[prv, :] += dvp
            return carry

        lax.fori_loop(0, nblk, step, 0, unroll=DIL_UNROLL)
        for r in range(d):
            rows = _strided(r, sub_len, d)
            out_ref[pl.ds(1, 1), rows, :] = dk_scr[r * sub_len:(r + 1) * sub_len, :][None]
            out_ref[pl.ds(2, 1), rows, :] = dv_scr[r * sub_len:(r + 1) * sub_len, :][None]

    def inp(tq):
        return pl.BlockSpec((1, SEQ, 128), lambda pr: (tq, 0, pr))

    tok_spec = pl.BlockSpec((SEQ, 128), lambda pr: (0, pr))
    return pl.pallas_call(
        body, name=f"dil_attn_bwd_g{g}", grid=(4,),
        in_specs=[inp(0), inp(1), inp(2), tok_spec, tok_spec, tok_spec],
        out_specs=pl.BlockSpec((3, SEQ, 128), lambda pr: (0, 0, pr)),
        out_shape=jax.ShapeDtypeStruct((3, SEQ, 512), F32),
        scratch_shapes=[pltpu.VMEM((SEQ, 128), F32), pltpu.VMEM((SEQ, 128), F32)],
        compiler_params=_cparams(),
    )(qkv, qkv, qkv, dyd, yd, lse_all)


def _dil_prep_bwd(dp_in, dqkv, rc, rs, g):
    tm = 512

    def body(dp_any, g_ref, c_ref, s_ref, dp_ref):
        del dp_any
        tq = pl.program_id(0)
        mult = jnp.where(tq == 0, DIL_SCALE, 1.0).astype(F32)
        lanes = _rope_lanes((tm, 128), DIL_ROPE_HALF, 64, 0)
        cv, sv = c_ref[...], s_ref[...]
        for pr in range(4):
            gv = g_ref[0, :, pr * 128:(pr + 1) * 128]
            roped = _rope_bwd(gv, cv, sv, DIL_ROPE_HALF, lanes)
            dp_ref[:, pr * 128:(pr + 1) * 128] = (jnp.where(tq == 2, gv, roped) * mult).astype(BF16)

    tab = pl.BlockSpec((tm, 128), lambda tq, i: (i, 0))
    return pl.pallas_call(
        body, name=f"dil_prep_bwd_g{g}", grid=(3, SEQ // tm),
        in_specs=[pl.BlockSpec(memory_space=pl.ANY),
                  pl.BlockSpec((1, tm, 512), lambda tq, i: (tq, i, 0)), tab, tab],
        out_specs=pl.BlockSpec((tm, 512), lambda tq, i: (i, COL_QKV // 512 + tq * 3 + g)),
        out_shape=jax.ShapeDtypeStruct((SEQ, N_PAD), BF16),
        input_output_aliases={0: 0},
    )(dp_in, dqkv, rc, rs)


TAIL_T = 128


def _tail(p, ya, o_g, l_g, x, target, wpm, wpd, wout, post_g):
    tm = TAIL_T

    def body(pgz_ref, ya_ref, o0_ref, o1_ref, o2_ref, l0_ref, l1_ref, l2_ref, x_ref, t_ref,
             wpm_ref, wpd_ref, wout_ref, pg_ref,
             dp_ref, dy_ref, mg_ref, dt_ref, ua_ref, dpa_ref, ud_ref, dpd_ref, dya_ref, dyd_ref,
             yd_ref, lse_ref, loss_ref, dgp_ref):
        l0, l1, l2 = l0_ref[...], l1_ref[...], l2_ref[...]
        mx = jnp.maximum(jnp.maximum(l0, l1), l2)
        e0, e1, e2 = jnp.exp(l0 - mx), jnp.exp(l1 - mx), jnp.exp(l2 - mx)
        den = e0 + e1 + e2
        yd = (e0 * o0_ref[...] + e1 * o1_ref[...] + e2 * o2_ref[...]) / den
        yd_ref[...] = yd
        lse_ref[...] = mx + jnp.log(den)
        ya = ya_ref[...]

        gm, gd = pgz_ref[:, 0:1024], pgz_ref[:, 1024:2048]
        zm, zd = pgz_ref[:, 2048:2560], pgz_ref[:, 2560:3072]
        szm, szd = _sigmoid(zm), _sigmoid(zd)
        sm, sd = zm * szm, zd * szd
        ua = (ya * sm).astype(BF16)
        ud = (yd * sd).astype(BF16)
        ua_ref[...] = ua
        ud_ref[...] = ud
        pa = _nn(ua, wpm_ref[...])
        pd = _nn(ud, wpd_ref[...])
        sgm, sgd = _sigmoid(gm), _sigmoid(gd)
        mg = (sgm * pa + sgd * pd).astype(BF16)
        mg_ref[...] = mg
        t = _nn(mg, wout_ref[...])
        r3 = lax.rsqrt(jnp.mean(t * t, axis=-1, keepdims=True) + EPS)
        n = t * r3
        pg = pg_ref[...]
        err = x_ref[...] + n * pg - t_ref[...]
        lpart = jnp.sum(err * err, axis=0, keepdims=True)

        dy = err * (1.0 / D_MODEL)
        dy_ref[...] = dy
        gpart = jnp.sum(dy * n, axis=0, keepdims=True)
        dn = dy * pg
        dt = (r3 * (dn - n * jnp.mean(dn * n, axis=-1, keepdims=True))).astype(BF16)
        dt_ref[...] = dt
        dmg = _nt(dt, wout_ref[...])
        dpa = (dmg * sgm).astype(BF16)
        dpd = (dmg * sgd).astype(BF16)
        dpa_ref[...] = dpa
        dpd_ref[...] = dpd
        dp_ref[:, 0:1024] = (dmg * pa * sgm * (1.0 - sgm)).astype(BF16)
        dp_ref[:, 1024:2048] = (dmg * pd * sgd * (1.0 - sgd)).astype(BF16)
        dua = _nt(dpa, wpm_ref[...])
        dud = _nt(dpd, wpd_ref[...])
        dya_ref[...] = dua * sm
        dyd_ref[...] = dud * sd
        dp_ref[:, 2048:2560] = (dua * ya * szm * (1.0 + zm * (1.0 - szm))).astype(BF16)
        dp_ref[:, 2560:3072] = (dud * yd * szd * (1.0 + zd * (1.0 - szd))).astype(BF16)

        @pl.when(pl.program_id(0) == 0)
        def _():
            loss_ref[...] = lpart
            dgp_ref[...] = gpart

        @pl.when(pl.program_id(0) > 0)
        def _():
            loss_ref[...] += lpart
            dgp_ref[...] += gpart

    def rows(w):
        return pl.BlockSpec((tm, w), lambda i: (i, 0))

    def full(shape):
        return pl.BlockSpec(shape, lambda i: (0, 0))

    def sds(w, dt):
        return jax.ShapeDtypeStruct((SEQ, w), dt)

    return pl.pallas_call(
        body, name="tail", grid=(SEQ // tm,),
        in_specs=[rows(3072), rows(512), rows(512), rows(512), rows(512), rows(512), rows(512), rows(512),
                  rows(1024), rows(1024), full((512, 1024)), full((512, 1024)), full((1024, 1024)), full((1, 1024))],
        out_specs=[rows(3072), rows(1024), rows(1024), rows(1024), rows(512), rows(1024), rows(512), rows(1024),
                   rows(512), rows(512), rows(512), rows(512), full((1, 1024)), full((1, 1024))],
        out_shape=[sds(N_PAD, BF16), sds(1024, F32), sds(1024, BF16), sds(1024, BF16), sds(512, BF16),
                   sds(1024, BF16), sds(512, BF16), sds(1024, BF16), sds(512, F32), sds(512, F32),
                   sds(512, F32), sds(512, F32),
                   jax.ShapeDtypeStruct((1, 1024), F32), jax.ShapeDtypeStruct((1, 1024), F32)],
        compiler_params=_cparams(),
    )(p, ya, o_g[0], o_g[1], o_g[2], l_g[0], l_g[1], l_g[2], x, target, wpm, wpd, wout, post_g)


def _sum_parts(parts, tr, name):
    n, r, w = parts.shape

    def body(p_ref, o_ref):
        acc = p_ref[0].astype(F32)
        for s in range(1, n):
            acc = acc + p_ref[s].astype(F32)
        o_ref[...] = acc

    return pl.pallas_call(
        body, name=name, grid=(r // tr,),
        in_specs=[pl.BlockSpec((n, tr, w), lambda i: (0, i, 0))],
        out_specs=pl.BlockSpec((tr, w), lambda i: (i, 0)),
        out_shape=jax.ShapeDtypeStruct((r, w), F32),
    )(parts)


def _adamw(w, g, m, v, name):
    r, c = w.shape
    tr = 128 if r % 128 == 0 else r
    c1 = 1.0 - ADAM_B1 ** ADAM_STEP
    c2 = 1.0 - ADAM_B2 ** ADAM_STEP

    def body(w_ref, g_ref, m_ref, v_ref, d_ref, nm_ref, nv_ref):
        gv = g_ref[...]
        nm = ADAM_B1 * m_ref[...] + (1.0 - ADAM_B1) * gv
        nv = ADAM_B2 * v_ref[...] + (1.0 - ADAM_B2) * (gv * gv)
        nm_ref[...] = nm
        nv_ref[...] = nv
        d_ref[...] = -ADAM_LR * ((nm / c1) / (jnp.sqrt(nv / c2) + ADAM_EPS) + ADAM_WD * w_ref[...])

    spec = pl.BlockSpec((tr, c), lambda i: (i, 0))
    sd = jax.ShapeDtypeStruct((r, c), F32)
    return pl.pallas_call(
        body, name=name, grid=(r // tr,),
        in_specs=[spec] * 4, out_specs=[spec] * 3, out_shape=[sd] * 3,
    )(w, g, m, v)


ANY = pl.BlockSpec(memory_space=pl.ANY)


def _my_place():
    return lax.axis_index("x"), lax.axis_index("y"), lax.axis_index("c")


def _allgather_weights(mats):
    def body(*refs):
        w_refs, out_refs = refs[:N_MATS], refs[N_MATS:2 * N_MATS]
        send_sems, recv_sems = refs[2 * N_MATS:]
        x, y, c = _my_place()
        sibling = (x, y, 1 - c)
        chips = [(1 - x, y), (x, 1 - y), (1 - x, 1 - y)]

        def copy(k, src, dst, to):
            return pltpu.make_async_remote_copy(src_ref=src, dst_ref=dst, send_sem=send_sems.at[k],
                                                recv_sem=recv_sems.at[k], device_id=to, device_id_type=MESH)

        def half(mi, shard, hc):
            hr = SHARD_SHAPES[mi][0] // 2
            return out_refs[mi].at[shard, pl.ds(pl.multiple_of(hc * hr, 16), hr), :]

        started = []
        for mi in range(N_MATS):
            hr = SHARD_SHAPES[mi][0] // 2
            my_half = w_refs[mi].at[pl.ds(pl.multiple_of(c * hr, 16), hr), :]
            for j, (cx, cy) in enumerate(chips):
                cp = copy(mi * 6 + j, my_half, half(mi, 2 * x + y, c), (cx, cy, c))
                cp.start()
                started.append(cp)
        for mi in range(N_MATS):
            for j, (cx, cy) in enumerate(chips):
                landed = half(mi, 2 * cx + cy, c)
                copy(mi * 6 + j, landed, landed, (cx, cy, c)).wait_recv()
                fw = copy(mi * 6 + 3 + j, landed, landed, sibling)
                fw.start()
                started.append(fw)
        for mi in range(N_MATS):
            for j, (cx, cy) in enumerate(chips):
                other = half(mi, 2 * cx + cy, 1 - c)
                copy(mi * 6 + 3 + j, other, other, sibling).wait_recv()
        for cp in started:
            cp.wait_send()

    return pl.pallas_call(
        body, name="allgather_weights",
        in_specs=[ANY] * N_MATS, out_specs=[ANY] * N_MATS,
        out_shape=[jax.ShapeDtypeStruct((4, r, c), BF16) for r, c in SHARD_SHAPES],
        scratch_shapes=[pltpu.SemaphoreType.DMA((6 * N_MATS,)), pltpu.SemaphoreType.DMA((6 * N_MATS,))],
    )(*mats)


def _exchange_grads(parts, gvec):
    n = N_MATS + 1

    def body(*refs):
        p_refs, r_refs = refs[:n], refs[n:2 * n]
        send_sems, recv_sems = refs[2 * n:]
        x, y, c = _my_place()
        me = 4 * x + 2 * y + c
        sends = []
        for k in range(1, 8):
            px, py, pc = x ^ (k >> 2), y ^ ((k >> 1) & 1), c ^ (k & 1)
            peer = 4 * px + 2 * py + pc
            for mi in range(n):
                src = p_refs[mi].at[peer] if mi < N_MATS else p_refs[mi]
                cp = pltpu.make_async_remote_copy(
                    src_ref=src, dst_ref=r_refs[mi].at[me], send_sem=send_sems.at[(k - 1) * n + mi],
                    recv_sem=recv_sems.at[(k - 1) * n + mi], device_id=(px, py, pc), device_id_type=MESH)
                cp.start()
                sends.append(cp)
        for k in range(1, 8):
            px, py, pc = x ^ (k >> 2), y ^ ((k >> 1) & 1), c ^ (k & 1)
            peer = 4 * px + 2 * py + pc
            for mi in range(n):
                slot = r_refs[mi].at[peer]
                pltpu.make_async_remote_copy(
                    src_ref=slot, dst_ref=slot, send_sem=send_sems.at[(k - 1) * n + mi],
                    recv_sem=recv_sems.at[(k - 1) * n + mi], device_id=(px, py, pc), device_id_type=MESH).wait_recv()
        for cp in sends:
            cp.wait_send()

    return pl.pallas_call(
        body, name="exchange_grads",
        in_specs=[ANY] * n, out_specs=[ANY] * n,
        out_shape=[jax.ShapeDtypeStruct((8, r // 2, c), BF16) for r, c in SHARD_SHAPES]
        + [jax.ShapeDtypeStruct((8, 8, N_GAINS), F32)],
        scratch_shapes=[pltpu.SemaphoreType.DMA((7 * n,)), pltpu.SemaphoreType.DMA((7 * n,))],
    )(*parts, gvec)


def _swap_halves(halves):
    def body(*refs):
        g_refs, out_refs = refs[:N_MATS], refs[N_MATS:2 * N_MATS]
        send_sems, recv_sems = refs[2 * N_MATS:]
        x, y, c = _my_place()
        sends = []
        for mi in range(N_MATS):
            cp = pltpu.make_async_remote_copy(src_ref=g_refs[mi], dst_ref=out_refs[mi].at[c], send_sem=send_sems.at[mi],
                                              recv_sem=recv_sems.at[mi], device_id=(x, y, 1 - c), device_id_type=MESH)
            cp.start()
            sends.append(cp)
        for mi in range(N_MATS):
            got = out_refs[mi].at[1 - c]
            pltpu.make_async_remote_copy(src_ref=got, dst_ref=got, send_sem=send_sems.at[mi], recv_sem=recv_sems.at[mi],
                                         device_id=(x, y, 1 - c), device_id_type=MESH).wait_recv()
        for cp in sends:
            cp.wait_send()

    return pl.pallas_call(
        body, name="swap_halves",
        in_specs=[ANY] * N_MATS, out_specs=[ANY] * N_MATS,
        out_shape=[jax.ShapeDtypeStruct((2, r // 2, c), F32) for r, c in SHARD_SHAPES],
        scratch_shapes=[pltpu.SemaphoreType.DMA((N_MATS,)), pltpu.SemaphoreType.DMA((N_MATS,))],
    )(*halves)


def _set_slot(arr, block, idx):
    return lax.dynamic_update_slice(arr, block[None], (idx,) + (0,) * block.ndim)


PAD_RUNS = ((6304, 8352, 0), (5280, 6304, COL_Z), (672, 5280, COL_QKV), (0, 640, COL_LAT), (640, 672, COL_LAT + 704))
W_IN_SHARD = 2088


def _full_weights(gathered):
    def cols(a):
        return jnp.concatenate([a[s] for s in range(4)], axis=1)

    w_uq, w_ukv, w_pm, w_pd = [cols(a) for a in gathered[1:5]]
    w_out = gathered[5].reshape(D_MODEL, D_MODEL)
    g_in = gathered[0]
    pieces, at = [], 0
    for lo, hi, pad_lo in sorted(PAD_RUNS, key=lambda t: t[2]):
        if pad_lo > at:
            pieces.append(jnp.zeros((D_MODEL, pad_lo - at), g_in.dtype))
        for s in range(4):
            a_, b_ = max(lo, s * W_IN_SHARD), min(hi, (s + 1) * W_IN_SHARD)
            if a_ < b_:
                pieces.append(g_in[s][:, a_ - s * W_IN_SHARD:b_ - s * W_IN_SHARD])
        at = pad_lo + hi - lo
    pieces.append(jnp.zeros((D_MODEL, N_PAD - at), g_in.dtype))
    w_pad = jnp.concatenate(pieces, axis=1)
    z32 = jnp.zeros((Q_RANK, 32), w_uq.dtype)
    wuq_pad = jnp.concatenate([t for h in range(MLA_HEADS) for t in (w_uq[:, h * 96:(h + 1) * 96], z32)], axis=1)
    z64 = jnp.zeros((KV_RANK, 64), w_ukv.dtype)
    wk_pad = jnp.concatenate([t for h in range(MLA_HEADS) for t in (w_ukv[:, h * 128:h * 128 + 64], z64)], axis=1)
    wv = jnp.concatenate([w_ukv[:, h * 128 + 64:(h + 1) * 128] for h in range(MLA_HEADS)], axis=1)
    return w_pad, wuq_pad, wk_pad, wv, w_pm, w_pd, w_out


def _grad_parts(dw_pad, dwuq_pad, dwk_pad, dwv, dwpm, dwpd, dwout):
    def in_block(s, h):
        rows = slice(h * 512, (h + 1) * 512)
        out = []
        for lo, hi, pad_lo in sorted(PAD_RUNS):
            a_, b_ = max(lo, s * W_IN_SHARD), min(hi, (s + 1) * W_IN_SHARD)
            if a_ < b_:
                out.append(dw_pad[rows, pad_lo + a_ - lo:pad_lo + b_ - lo])
        return jnp.concatenate(out, axis=1).astype(BF16)

    d_uq = jnp.concatenate([dwuq_pad[:, h * 128:h * 128 + 96] for h in range(MLA_HEADS)], axis=1)
    d_ukv = jnp.concatenate([t for h in range(MLA_HEADS) for t in (dwk_pad[:, h * 128:h * 128 + 64], dwv[:, h * 64:(h + 1) * 64])],
                            axis=1)

    def col_blocks(m):
        r, c = m.shape[0] // 2, m.shape[1] // 4
        return jnp.stack([m[h * r:(h + 1) * r, s * c:(s + 1) * c].astype(BF16) for s in range(4) for h in range(2)])

    return [jnp.stack([in_block(s, h) for s in range(4) for h in range(2)]), col_blocks(d_uq), col_blocks(d_ukv),
            col_blocks(dwpm), col_blocks(dwpd), dwout.astype(BF16).reshape(8, 128, D_MODEL)]


def _rope_tables(positions):
    pos = positions.reshape(SEQ).astype(F32)
    lane = jnp.arange(128)

    def table(rot, first, period):
        inv = ROPE_THETA ** (-jnp.arange(0, rot, 2, dtype=F32) / rot)
        half = rot // 2
        off = lane % period - first
        in1, in2 = (off >= 0) & (off < half), (off >= half) & (off < rot)
        inv_lane = jnp.where(in1 | in2, inv[jnp.clip(off % half, 0, half - 1)], 0.0)
        sign = jnp.where(in1, -1.0, 1.0).astype(F32)
        ang = pos[:, None] * inv_lane[None, :]
        return jnp.cos(ang), jnp.sin(ang) * sign[None, :]

    return table(32, 64, 128), table(16, 0, 64)


def _device_grads(x, positions, target, gains, gathered):
    pre_g, q_g, kv_g, post_g = gains
    w_pad, wuq_pad, wk_pad, wv, w_pm, w_pd, w_out = _full_weights(gathered)
    (mc, ms), (dc, ds) = _rope_tables(positions)

    h = _prenorm_fwd(x, pre_g)
    p = _matmul(h, w_pad, "nn", F32, 512, 1408, 1024, "in_proj")
    cqn, ckvn, q, k, v = _mla_prep_fwd(p, q_g, kv_g, wuq_pad, wk_pad, wv, mc, ms)
    ya, lse_m = _mla_flash_fwd(q, k, v)
    qkv = [_dil_prep_fwd(p, dc, ds, g) for g in range(3)]
    o_g, l_g = zip(*[_dil_attn_fwd2(qkv[g], g) for g in range(3)])
    (dp, dy, mg, dt, ua, dpa, ud, dpd, dya, dyd, yd, lse_d, loss_cols, dg_post) = _tail(
        p, ya, o_g, l_g, x, target, w_pm, w_pd, w_out, post_g)

    dq, dk, dv = _mla_flash_bwd(q, k, v, ya, dya, lse_m)
    dp, dqb, dkb, dvb, dg_q, dg_kv = _mla_prep_bwd(dp, p, dq, dk, dv, q_g, kv_g, wuq_pad, wk_pad, wv, mc, ms)
    for g in range(3):
        dqkv = _dil_attn_bwd2(qkv[g], dyd, yd, lse_d, g)
        dp = _dil_prep_bwd(dp, dqkv, dc, ds, g)

    dw_pad = _matmul(h, dp, "tn", F32, 1024, 1408, 512, "dw_in")
    dwuq_pad = _matmul(cqn, dqb, "tn", F32, Q_RANK, 1024, 512, "dw_uq")
    dwk_pad = _matmul(ckvn, dkb, "tn", F32, KV_RANK, 1024, 512, "dw_k")
    dwv = _matmul(ckvn, dvb, "tn", F32, KV_RANK, 512, 512, "dw_v")
    dwpm = _matmul(ua, dpa, "tn", F32, 512, 1024, 512, "dw_proj_mla")
    dwpd = _matmul(ud, dpd, "tn", F32, 512, 1024, 512, "dw_proj_dil")
    dwout = _matmul(mg, dt, "tn", F32, 1024, 1024, 512, "dw_out")
    dh = _matmul(dp, w_pad, "nt", F32, 512, 1024, 1408, "dh")
    grad_x, dg_pre = _prenorm_bwd(x, dh, dy, pre_g)

    parts = _grad_parts(dw_pad, dwuq_pad, dwk_pad, dwv, dwpm, dwpd, dwout)
    gvec = jnp.concatenate([dg_pre, dg_q, dg_kv, dg_post], axis=1)
    return loss_cols, grad_x, parts, gvec


def kernel(x, positions, pre_norm_g, w_in, q_norm_g, w_uq, kv_norm_g, w_ukv, w_proj_mla, w_proj_dil, w_out, post_norm_g, loss_target, m_pre_norm_g, m_w_in, m_q_norm_g, m_w_uq, m_kv_norm_g, m_w_ukv, m_w_proj_mla, m_w_proj_dil, m_w_out, m_post_norm_g, v_pre_norm_g, v_w_in, v_q_norm_g, v_w_uq, v_kv_norm_g, v_w_ukv, v_w_proj_mla, v_w_proj_dil, v_w_out, v_post_norm_g):
    xi, yi, ci = _my_place()
    chip, me = 2 * xi + yi, 4 * xi + 2 * yi + ci
    mats = [w.reshape(w.shape[1:]).astype(BF16) for w in (w_in, w_uq, w_ukv, w_proj_mla, w_proj_dil, w_out)]
    gathered = [_set_slot(g, m, chip) for g, m in zip(_allgather_weights(mats), mats)]
    gains = (pre_norm_g, q_norm_g, kv_norm_g, post_norm_g)
    loss_cols, grad_x, parts, gvec = _device_grads(x[0], positions, loss_target[0], gains, gathered)

    loss = lax.psum(jnp.sum(loss_cols) * (0.5 / D_MODEL), ("x", "y", "c"))

    gvec8 = jnp.pad(gvec, ((0, 7), (0, 0)))
    recv = _exchange_grads(parts, gvec8)
    own = [lax.dynamic_index_in_dim(p, me, 0, keepdims=False) for p in parts] + [gvec8]
    recv = [_set_slot(r, o, me) for r, o in zip(recv, own)]
    halves = [_sum_parts(recv[mi], 64, f"sum_grad_{mi}") for mi in range(N_MATS)]
    g_gains = _sum_parts(recv[N_MATS], 8, "sum_gain_parts")[0:1]
    g_mats = [_set_slot(s, hf, ci).reshape((1,) + shp) for s, hf, shp in zip(_swap_halves(halves), halves, SHARD_SHAPES)]

    off = [0, 1024, 1408, 1664, 2688]
    g_gain = [g_gains[:, off[i]:off[i + 1]] for i in range(4)]
    grads = [g_gain[0], g_mats[0], g_gain[1], g_mats[1], g_gain[2], g_mats[2], g_mats[3], g_mats[4], g_mats[5], g_gain[3]]
    ws = [pre_norm_g, w_in, q_norm_g, w_uq, kv_norm_g, w_ukv, w_proj_mla, w_proj_dil, w_out, post_norm_g]
    ms = [m_pre_norm_g, m_w_in, m_q_norm_g, m_w_uq, m_kv_norm_g, m_w_ukv, m_w_proj_mla, m_w_proj_dil, m_w_out, m_post_norm_g]
    vs = [v_pre_norm_g, v_w_in, v_q_norm_g, v_w_uq, v_kv_norm_g, v_w_ukv, v_w_proj_mla, v_w_proj_dil, v_w_out, v_post_norm_g]
    deltas, new_m, new_v = [], [], []
    for i, (w, g, m, v) in enumerate(zip(ws, grads, ms, vs)):
        shp = w.shape
        two_d = shp[-2:]
        d_, m_, v_ = _adamw(w.reshape(two_d), g.reshape(two_d), m.reshape(two_d), v.reshape(two_d), f"adamw_{i}")
        deltas.append(d_.reshape(shp))
        new_m.append(m_.reshape(shp))
        new_v.append(v_.reshape(shp))
    return (loss, grad_x.reshape(x.shape), *grads, *deltas, *new_m, *new_v)
```

```python
import jax
import jax.numpy as jnp
from jax import lax
from jax.experimental import pallas as pl
from jax.experimental.pallas import tpu as pltpu

F32 = jnp.float32
BF16 = jnp.bfloat16

SEQ = 4096
D_MODEL = 1024
EPS = 1e-6
ROPE_THETA = 500000.0
MLA_HEADS = 8
Q_RANK = 384
KV_RANK = 256
MLA_SCALE = 96.0 ** -0.5
MLA_ROPE_HALF = 16
DIL_DILATIONS = (1, 4, 16)
DIL_ROPE_HALF = 8
DIL_SCALE = 0.125
BAND = 128

N_LAT = 768
COL_Z, COL_QKV, COL_LAT = 2048, 3072, 7680
N_PAD = 8448
IN_SPLITS = (384, 256, 32, 4608, 512, 512, 1024, 1024)

SHARD_SHAPES = ((1024, 2088), (384, 192), (256, 256), (512, 256), (512, 256), (256, 1024))
N_MATS = len(SHARD_SHAPES)
N_GAINS = 2688

ADAM_LR, ADAM_B1, ADAM_B2, ADAM_EPS, ADAM_WD, ADAM_STEP = 0.001, 0.9, 0.999, 1e-08, 0.01, 10

VMEM_LIMIT = 56 * 1024 * 1024
NEG = -1e30
MESH = pl.DeviceIdType.MESH


def _cparams(**kw):
    return pltpu.CompilerParams(vmem_limit_bytes=VMEM_LIMIT, **kw)


def _dot(a, b, dims):
    return lax.dot_general(a, b, (dims, ((), ())), preferred_element_type=F32)


def _nn(a, b):
    return _dot(a, b, ((1,), (0,)))


def _nt(a, b):
    return _dot(a, b, ((1,), (1,)))


def _tn(a, b):
    return _dot(a, b, ((0,), (0,)))


def _rope_lanes(shape, half, period, first):
    lane = lax.broadcasted_iota(jnp.int32, shape, len(shape) - 1) % period
    return (lane >= first) & (lane < first + half), (lane >= first + half) & (lane < first + 2 * half)


def _rope_fwd(x, c, s, half, lanes):
    x1, _ = lanes
    return x * c + jnp.where(x1, pltpu.roll(x, 128 - half, 1), pltpu.roll(x, half, 1)) * s


def _rope_bwd(g, c, s, half, lanes):
    x1, x2 = lanes
    gs = g * s
    return g * c + jnp.where(x2, pltpu.roll(gs, half, 1), jnp.where(x1, pltpu.roll(gs, 128 - half, 1), 0.0))


def _sigmoid(x):
    return 1.0 / (1.0 + jnp.exp(-x))


def _after(token):
    return ([], []) if token is None else ([token], [pl.BlockSpec(memory_space=pl.ANY)])


def _matmul(a, b, mode, out_dtype, tm, tn, tk, name, token=None):
    after, after_specs = _after(token)
    if mode == "nn":
        (m, k), n = a.shape, b.shape[1]
        a_spec = pl.BlockSpec((tm, tk), lambda j, i, kk: (i, kk))
        b_spec = pl.BlockSpec((tk, tn), lambda j, i, kk: (kk, j))
        dot = _nn
    elif mode == "nt":
        (m, k), n = a.shape, b.shape[0]
        a_spec = pl.BlockSpec((tm, tk), lambda j, i, kk: (i, kk))
        b_spec = pl.BlockSpec((tn, tk), lambda j, i, kk: (j, kk))
        dot = _nt
    else:
        (k, m), n = a.shape, b.shape[1]
        a_spec = pl.BlockSpec((tk, tm), lambda j, i, kk: (kk, i))
        b_spec = pl.BlockSpec((tk, tn), lambda j, i, kk: (kk, j))
        dot = _tn
    assert m % tm == 0 and n % tn == 0 and k % tk == 0, (name, m, n, k, tm, tn, tk)
    nk = k // tk

    def body(a_ref, b_ref, *rest):
        o_ref, acc_ref = rest[-2:]
        kk = pl.program_id(2)
        part = dot(a_ref[...], b_ref[...])

        @pl.when(kk == 0)
        def _():
            acc_ref[...] = part

        @pl.when(kk > 0)
        def _():
            acc_ref[...] += part

        @pl.when(kk == nk - 1)
        def _():
            o_ref[...] = acc_ref[...].astype(o_ref.dtype)

    return pl.pallas_call(
        body, name=name, grid=(n // tn, m // tm, nk),
        in_specs=[a_spec, b_spec] + after_specs,
        out_specs=pl.BlockSpec((tm, tn), lambda j, i, kk: (i, j)),
        out_shape=jax.ShapeDtypeStruct((m, n), out_dtype),
        scratch_shapes=[pltpu.VMEM((tm, tn), F32)],
        compiler_params=_cparams(),
    )(a, b, *after)


def _prenorm_fwd(x, g):
    tm = 512

    def body(x_ref, g_ref, h_ref):
        xv = x_ref[...]
        r = lax.rsqrt(jnp.mean(xv * xv, axis=-1, keepdims=True) + EPS)
        h_ref[...] = (xv * r * g_ref[...]).astype(BF16)

    return pl.pallas_call(
        body, name="prenorm_fwd", grid=(SEQ // tm,),
        in_specs=[pl.BlockSpec((tm, D_MODEL), lambda i: (i, 0)), pl.BlockSpec((1, D_MODEL), lambda i: (0, 0))],
        out_specs=pl.BlockSpec((tm, D_MODEL), lambda i: (i, 0)),
        out_shape=jax.ShapeDtypeStruct((SEQ, D_MODEL), BF16),
    )(x, g)


def _prenorm_bwd(x, dh, dy, g):
    tm = 512

    def body(x_ref, dh_ref, dy_ref, g_ref, gx_ref, dg_ref):
        xv = x_ref[...]
        r = lax.rsqrt(jnp.mean(xv * xv, axis=-1, keepdims=True) + EPS)
        n = xv * r
        dhv = dh_ref[...]
        dn = dhv * g_ref[...]
        gx_ref[...] = dy_ref[...] + r * (dn - n * jnp.mean(dn * n, axis=-1, keepdims=True))
        part = jnp.sum(dhv * n, axis=0, keepdims=True)

        @pl.when(pl.program_id(0) == 0)
        def _():
            dg_ref[...] = part

        @pl.when(pl.program_id(0) > 0)
        def _():
            dg_ref[...] += part

    row = pl.BlockSpec((tm, D_MODEL), lambda i: (i, 0))
    vec = pl.BlockSpec((1, D_MODEL), lambda i: (0, 0))
    return pl.pallas_call(
        body, name="prenorm_bwd", grid=(SEQ // tm,),
        in_specs=[row, row, row, vec], out_specs=[row, vec],
        out_shape=[jax.ShapeDtypeStruct((SEQ, D_MODEL), F32), jax.ShapeDtypeStruct((1, D_MODEL), F32)],
        compiler_params=_cparams(),
    )(x, dh, dy, g)


def _mla_prep_fwd(p, qg, kvg, wuq, wk, wv, rc, rs):
    tm = 512

    def body(lat_ref, qg_ref, kvg_ref, wuq_ref, wk_ref, wv_ref, c_ref, s_ref,
             cqn_ref, ckvn_ref, q_ref, k_ref, v_ref):
        c, s = c_ref[...], s_ref[...]
        lanes = _rope_lanes((tm, 128), MLA_ROPE_HALF, 128, 64)
        cq = lat_ref[:, 0:Q_RANK]
        r1 = lax.rsqrt(jnp.mean(cq * cq, axis=-1, keepdims=True) + EPS)
        cqn = (cq * r1 * qg_ref[...]).astype(BF16)
        cqn_ref[...] = cqn
        q = _nn(cqn, wuq_ref[...])
        for h in range(MLA_HEADS):
            sl = slice(h * 128, (h + 1) * 128)
            q_ref[:, sl] = (_rope_fwd(q[:, sl], c, s, MLA_ROPE_HALF, lanes) * MLA_SCALE).astype(BF16)
        ckv = lat_ref[:, Q_RANK:Q_RANK + KV_RANK]
        r2 = lax.rsqrt(jnp.mean(ckv * ckv, axis=-1, keepdims=True) + EPS)
        ckvn = (ckv * r2 * kvg_ref[...]).astype(BF16)
        ckvn_ref[...] = ckvn
        krr = _rope_fwd(lat_ref[:, Q_RANK + KV_RANK:N_LAT], c, s, MLA_ROPE_HALF, lanes)
        kn = _nn(ckvn, wk_ref[...])
        for h in range(MLA_HEADS):
            sl = slice(h * 128, (h + 1) * 128)
            k_ref[:, sl] = (kn[:, sl] + krr).astype(BF16)
        v_ref[...] = _nn(ckvn, wv_ref[...]).astype(BF16)

    def full(shape):
        return pl.BlockSpec(shape, lambda i: (0, 0))

    def rows(w):
        return pl.BlockSpec((tm, w), lambda i: (i, 0))

    return pl.pallas_call(
        body, name="mla_prep_fwd", grid=(SEQ // tm,),
        in_specs=[pl.BlockSpec((tm, N_LAT), lambda i: (i, COL_LAT // N_LAT)),
                  full((1, Q_RANK)), full((1, KV_RANK)), full((Q_RANK, 1024)), full((KV_RANK, 1024)),
                  full((KV_RANK, 512)), rows(128), rows(128)],
        out_specs=[rows(Q_RANK), rows(KV_RANK), rows(1024), rows(1024), rows(512)],
        out_shape=[jax.ShapeDtypeStruct((SEQ, Q_RANK), BF16), jax.ShapeDtypeStruct((SEQ, KV_RANK), BF16),
                   jax.ShapeDtypeStruct((SEQ, 1024), BF16), jax.ShapeDtypeStruct((SEQ, 1024), BF16),
                   jax.ShapeDtypeStruct((SEQ, 512), BF16)],
        compiler_params=_cparams(),
    )(p, qg, kvg, wuq, wk, wv, rc, rs)


def _mla_prep_bwd(dp_in, p, dq, dk, dv, qg, kvg, wuq, wk, wv, rc, rs):
    tm = 512

    def body(dp_any, lat_ref, dq_ref, dk_ref, dv_ref, qg_ref, kvg_ref, wuq_ref, wk_ref, wv_ref,
             c_ref, s_ref, dp_ref, dqb_ref, dkb_ref, dvb_ref, dgq_ref, dgkv_ref):
        del dp_any
        c, s = c_ref[...], s_ref[...]
        lanes = _rope_lanes((tm, 128), MLA_ROPE_HALF, 128, 64)
        lane = lax.broadcasted_iota(jnp.int32, (tm, 128), 1)
        dkr = jnp.zeros((tm, 128), F32)
        for h in range(MLA_HEADS):
            sl = slice(h * 128, (h + 1) * 128)
            dqb_ref[:, sl] = _rope_bwd(dq_ref[:, sl] * MLA_SCALE, c, s, MLA_ROPE_HALF, lanes).astype(BF16)
            dkh = dk_ref[:, sl]
            dkr = dkr + dkh
            dkb_ref[:, sl] = jnp.where(lane < 64, dkh, 0.0).astype(BF16)
        dkr = jnp.where((lane >= 64) & (lane < 96), dkr, 0.0)
        dkr = _rope_bwd(dkr, c, s, MLA_ROPE_HALF, lanes)
        dvb = dv_ref[...].astype(BF16)
        dvb_ref[...] = dvb

        cq = lat_ref[:, 0:Q_RANK]
        r1 = lax.rsqrt(jnp.mean(cq * cq, axis=-1, keepdims=True) + EPS)
        n1 = cq * r1
        dcqn = _nt(dqb_ref[...], wuq_ref[...])
        dn1 = dcqn * qg_ref[...]
        dcq = r1 * (dn1 - n1 * jnp.mean(dn1 * n1, axis=-1, keepdims=True))
        pq = jnp.sum(dcqn * n1, axis=0, keepdims=True)

        ckv = lat_ref[:, Q_RANK:Q_RANK + KV_RANK]
        r2 = lax.rsqrt(jnp.mean(ckv * ckv, axis=-1, keepdims=True) + EPS)
        n2 = ckv * r2
        dckvn = _nt(dkb_ref[...], wk_ref[...]) + _nt(dvb, wv_ref[...])
        dn2 = dckvn * kvg_ref[...]
        dckv = r2 * (dn2 - n2 * jnp.mean(dn2 * n2, axis=-1, keepdims=True))
        pkv = jnp.sum(dckvn * n2, axis=0, keepdims=True)

        dp_ref[:, 0:Q_RANK] = dcq.astype(BF16)
        dp_ref[:, Q_RANK:Q_RANK + KV_RANK] = dckv.astype(BF16)
        dp_ref[:, Q_RANK + KV_RANK:N_LAT] = dkr.astype(BF16)

        @pl.when(pl.program_id(0) == 0)
        def _():
            dgq_ref[...] = pq
            dgkv_ref[...] = pkv

        @pl.when(pl.program_id(0) > 0)
        def _():
            dgq_ref[...] += pq
            dgkv_ref[...] += pkv

    def full(shape):
        return pl.BlockSpec(shape, lambda i: (0, 0))

    def rows(w):
        return pl.BlockSpec((tm, w), lambda i: (i, 0))

    lat = pl.BlockSpec((tm, N_LAT), lambda i: (i, COL_LAT // N_LAT))
    return pl.pallas_call(
        body, name="mla_prep_bwd", grid=(SEQ // tm,),
        in_specs=[pl.BlockSpec(memory_space=pl.ANY), lat, rows(1024), rows(1024), rows(512),
                  full((1, Q_RANK)), full((1, KV_RANK)), full((Q_RANK, 1024)), full((KV_RANK, 1024)),
                  full((KV_RANK, 512)), rows(128), rows(128)],
        out_specs=[lat, rows(1024), rows(1024), rows(512), full((1, Q_RANK)), full((1, KV_RANK))],
        out_shape=[jax.ShapeDtypeStruct((SEQ, N_PAD), BF16), jax.ShapeDtypeStruct((SEQ, 1024), BF16),
                   jax.ShapeDtypeStruct((SEQ, 1024), BF16), jax.ShapeDtypeStruct((SEQ, 512), BF16),
                   jax.ShapeDtypeStruct((1, Q_RANK), F32), jax.ShapeDtypeStruct((1, KV_RANK), F32)],
        input_output_aliases={0: 0},
        compiler_params=_cparams(),
    )(dp_in, p, dq, dk, dv, qg, kvg, wuq, wk, wv, rc, rs)


FLASH_T = 512


def _head_half(shape, hh):
    lane = lax.broadcasted_iota(jnp.int32, shape, 1)
    return (lane < 64) if hh == 0 else (lane >= 64)


def _causal_keep(t):
    row = lax.broadcasted_iota(jnp.int32, (t, t), 0)
    col = lax.broadcasted_iota(jnp.int32, (t, t), 1)
    return row >= col


def _tri_steps(nb, q_major):
    if q_major:
        pairs = [(i, kb) for i in range(nb) for kb in range(i + 1)]
    else:
        pairs = [(i, kb) for kb in range(nb) for i in range(kb, nb)]
    return jnp.asarray([p[0] for p in pairs], jnp.int32), jnp.asarray([p[1] for p in pairs], jnp.int32)


def _mla_flash_fwd(q, k, v):
    t = FLASH_T
    nb = SEQ // t
    qtab, ktab = _tri_steps(nb, True)

    def body(qi_ref, ki_ref, q_ref, k_ref, v_ref, o_ref, lse_ref, m_scr, l_scr, acc_scr):
        step = pl.program_id(1)
        i, kb = qi_ref[step], ki_ref[step]

        @pl.when(kb == 0)
        def _():
            m_scr[...] = jnp.full_like(m_scr, NEG)
            l_scr[...] = jnp.zeros_like(l_scr)
            acc_scr[...] = jnp.zeros_like(acc_scr)

        def update(masked):
            vv = v_ref[...]
            for hh in range(2):
                sl = slice(hh * 128, (hh + 1) * 128)
                s = _nt(q_ref[:, sl], k_ref[:, sl])
                if masked:
                    s = jnp.where(_causal_keep(t), s, NEG)
                m_prev = m_scr[hh]
                m_new = jnp.maximum(m_prev, jnp.max(s, axis=-1, keepdims=True))
                pr = jnp.exp(s - jnp.tile(m_new, (1, t // 128)))
                alpha = jnp.exp(m_prev - m_new)
                l_scr[hh] = alpha * l_scr[hh] + jnp.sum(pr, axis=-1, keepdims=True)
                acc_scr[hh] = alpha * acc_scr[hh] + _nn(pr.astype(BF16), vv)
                m_scr[hh] = m_new

        @pl.when(kb < i)
        def _():
            update(False)

        @pl.when(kb == i)
        def _():
            update(True)
            o0 = acc_scr[0] / l_scr[0]
            o1 = acc_scr[1] / l_scr[1]
            o_ref[...] = jnp.where(_head_half((t, 128), 0), o0, o1)
            for hh in range(2):
                lse_ref[:, hh * 128:(hh + 1) * 128] = m_scr[hh] + jnp.log(l_scr[hh])

    grid_spec = pltpu.PrefetchScalarGridSpec(
        num_scalar_prefetch=2, grid=(4, qtab.shape[0]),
        in_specs=[pl.BlockSpec((t, 256), lambda j, s, qi, ki: (qi[s], j)),
                  pl.BlockSpec((t, 256), lambda j, s, qi, ki: (ki[s], j)),
                  pl.BlockSpec((t, 128), lambda j, s, qi, ki: (ki[s], j))],
        out_specs=[pl.BlockSpec((t, 128), lambda j, s, qi, ki: (qi[s], j)),
                   pl.BlockSpec((t, 256), lambda j, s, qi, ki: (qi[s], j))],
        scratch_shapes=[pltpu.VMEM((2, t, 128), F32), pltpu.VMEM((2, t, 128), F32), pltpu.VMEM((2, t, 128), F32)])
    return pl.pallas_call(
        body, name="mla_flash_fwd", grid_spec=grid_spec,
        out_shape=[jax.ShapeDtypeStruct((SEQ, 512), F32), jax.ShapeDtypeStruct((SEQ, 1024), F32)],
        compiler_params=_cparams(),
    )(qtab, ktab, q, k, v)


def _mla_flash_bwd(q, k, v, o, do, lse):
    t = FLASH_T
    nb = SEQ // t
    qtab, ktab = _tri_steps(nb, False)

    def body(qi_ref, ki_ref, q_ref, k_ref, v_ref, o_ref, do_ref, lse_ref, dq_ref, dk_ref, dv_ref, dk_scr, dv_scr):
        step = pl.program_id(1)
        i, kb = qi_ref[step], ki_ref[step]

        @pl.when(step == 0)
        def _():
            dq_ref[...] = jnp.zeros_like(dq_ref)

        @pl.when(i == kb)
        def _():
            dk_scr[...] = jnp.zeros_like(dk_scr)
            dv_scr[...] = jnp.zeros_like(dv_scr)

        def update(masked):
            vv = v_ref[...]
            ov = o_ref[...]
            dov = do_ref[...]
            rows = pl.ds(pl.multiple_of(i * t, t), t)
            for hh in range(2):
                sl = slice(hh * 128, (hh + 1) * 128)
                qh, kh = q_ref[:, sl], k_ref[:, sl]
                s = _nt(qh, kh)
                if masked:
                    s = jnp.where(_causal_keep(t), s, NEG)
                pr = jnp.exp(s - jnp.tile(lse_ref[:, sl], (1, t // 128)))
                dom = jnp.where(_head_half((t, 128), hh), dov, 0.0)
                domb = dom.astype(BF16)
                dv_scr[...] += _tn(pr.astype(BF16), domb)
                dpr = _nt(domb, vv)
                delta = jnp.sum(dom * ov, axis=-1, keepdims=True)
                ds = (pr * (dpr - delta)).astype(BF16)
                dq_ref[rows, sl] += _nn(ds, kh)
                dk_scr[hh] += _tn(ds, qh)

        @pl.when(i > kb)
        def _():
            update(False)

        @pl.when(i == kb)
        def _():
            update(True)

        @pl.when(i == nb - 1)
        def _():
            dk_ref[:, 0:128] = dk_scr[0]
            dk_ref[:, 128:256] = dk_scr[1]
            dv_ref[...] = dv_scr[...]

    qi_map = lambda j, s, qi, ki: (qi[s], j)
    ki_map = lambda j, s, qi, ki: (ki[s], j)
    grid_spec = pltpu.PrefetchScalarGridSpec(
        num_scalar_prefetch=2, grid=(4, qtab.shape[0]),
        in_specs=[pl.BlockSpec((t, 256), qi_map), pl.BlockSpec((t, 256), ki_map), pl.BlockSpec((t, 128), ki_map),
                  pl.BlockSpec((t, 128), qi_map), pl.BlockSpec((t, 128), qi_map), pl.BlockSpec((t, 256), qi_map)],
        out_specs=[pl.BlockSpec((SEQ, 256), lambda j, s, qi, ki: (0, j)), pl.BlockSpec((t, 256), ki_map),
                   pl.BlockSpec((t, 128), ki_map)],
        scratch_shapes=[pltpu.VMEM((2, t, 128), F32), pltpu.VMEM((t, 128), F32)])
    return pl.pallas_call(
        body, name="mla_flash_bwd", grid_spec=grid_spec,
        out_shape=[jax.ShapeDtypeStruct((SEQ, 1024), F32), jax.ShapeDtypeStruct((SEQ, 1024), F32),
                   jax.ShapeDtypeStruct((SEQ, 512), F32)],
        compiler_params=_cparams(),
    )(qtab, ktab, q, k, v, o, do, lse)


DIL_UNROLL = 4


def _strided(start, size, d):
    return pl.ds(start, size) if d == 1 else pl.ds(start, size, stride=d)


def _dil_prep_fwd(p, rc, rs, g):
    d = DIL_DILATIONS[g]
    sub_len = SEQ // d
    ch = min(sub_len, 512)

    def body(p_ref, c_ref, s_ref, o_ref):
        tq = pl.program_id(0)
        is_v = tq == 2
        mult = jnp.where(tq == 0, DIL_SCALE, 1.0).astype(F32)
        lanes = _rope_lanes((ch, 128), DIL_ROPE_HALF, 64, 0)
        o_ref[0, 0:BAND, :] = jnp.zeros((BAND, 128), BF16)
        for r in range(d):
            for c0 in range(0, sub_len, ch):
                rows = _strided(r + c0 * d, ch, d)
                xv = p_ref[rows, :]
                roped = _rope_fwd(xv, c_ref[rows, :], s_ref[rows, :], DIL_ROPE_HALF, lanes)
                at = BAND + r * sub_len + c0
                o_ref[0, at:at + ch, :] = (jnp.where(is_v, xv, roped) * mult).astype(BF16)

    tab = pl.BlockSpec((SEQ, 128), lambda tq, pr: (0, 0))
    return pl.pallas_call(
        body, name=f"dil_prep_fwd_g{g}", grid=(3, 4),
        in_specs=[pl.BlockSpec((SEQ, 128), lambda tq, pr: (0, COL_QKV // 128 + (tq * 3 + g) * 4 + pr)), tab, tab],
        out_specs=pl.BlockSpec((1, BAND + SEQ, 128), lambda tq, pr: (tq, 0, pr)),
        out_shape=jax.ShapeDtypeStruct((3, BAND + SEQ, 512), BF16),
        compiler_params=_cparams(),
    )(p, rc, rs)


DIL_ST = 1024
DIL_NB = DIL_ST // BAND


def _band_keep(g, b, t):
    nbs = SEQ // DIL_DILATIONS[g] // BAND
    row = lax.broadcasted_iota(jnp.int32, (BAND, 2 * BAND), 0)
    col = lax.broadcasted_iota(jnp.int32, (BAND, 2 * BAND), 1)
    cur = (col >= BAND) & (row >= col - BAND)
    prev = (col < BAND) & (col >= row)
    if nbs >= DIL_NB:
        if b > 0:
            return cur | prev
        return cur | (prev & ((t * DIL_NB) % nbs != 0))
    return cur | prev if b % nbs else cur


def _dil_tok(g, b, t):
    d = DIL_DILATIONS[g]
    nbs = SEQ // d // BAND
    gb = t * DIL_NB + b
    return _strided((gb % nbs) * BAND * d + gb // nbs, BAND, d)


def _dil_attn_fwd2(qkv, g):
    def body(q_ref, k_ref, v_ref, o_ref, l_ref, s_scr, p_scr, o_scr):
        t = pl.program_id(1)
        base = t * DIL_ST
        half0 = _head_half((DIL_ST, 128), 0)
        lse_h = []
        for hh in range(2):
            half = _head_half((BAND, 128), hh)
            for b in range(DIL_NB):
                qv = q_ref[0, pl.ds(pl.multiple_of(base + (b + 1) * BAND, BAND), BAND), :]
                k2 = k_ref[0, pl.ds(pl.multiple_of(base + b * BAND, BAND), 2 * BAND), :]
                sb = _nt(jnp.where(half, qv, jnp.zeros_like(qv)), k2)
                s_scr[b * BAND:(b + 1) * BAND, :] = jnp.where(_band_keep(g, b, t), sb, NEG)
            s = s_scr[...]
            m = jnp.max(s, axis=-1, keepdims=True)
            pr = jnp.exp(s - m)
            den = jnp.sum(pr, axis=-1, keepdims=True)
            p_scr[...] = pr.astype(BF16)
            for b in range(DIL_NB):
                v2 = v_ref[0, pl.ds(pl.multiple_of(base + b * BAND, BAND), 2 * BAND), :]
                o_scr[hh, b * BAND:(b + 1) * BAND, :] = _nn(p_scr[b * BAND:(b + 1) * BAND, :], v2)
            o_scr[hh] = o_scr[hh] / den
            lse_h.append(m + jnp.log(den))
        out = jnp.where(half0, o_scr[0], o_scr[1])
        lse = jnp.where(half0, lse_h[0], lse_h[1])
        for b in range(DIL_NB):
            tok = _dil_tok(g, b, t)
            o_ref[tok, :] = out[b * BAND:(b + 1) * BAND, :]
            l_ref[tok, :] = lse[b * BAND:(b + 1) * BAND, :]

    def inp(tq):
        return pl.BlockSpec((1, BAND + SEQ, 128), lambda pr, t: (tq, 0, pr))

    out = pl.BlockSpec((SEQ, 128), lambda pr, t: (0, pr))
    return pl.pallas_call(
        body, name=f"dil_attn_fwd_g{g}", grid=(4, SEQ // DIL_ST),
        in_specs=[inp(0), inp(1), inp(2)], out_specs=[out, out],
        out_shape=[jax.ShapeDtypeStruct((SEQ, 512), F32), jax.ShapeDtypeStruct((SEQ, 512), F32)],
        scratch_shapes=[pltpu.VMEM((DIL_ST, 2 * BAND), F32), pltpu.VMEM((DIL_ST, 2 * BAND), BF16),
                        pltpu.VMEM((2, DIL_ST, 128), F32)],
        compiler_params=_cparams(),
    )(qkv, qkv, qkv)


def _dil_attn_bwd2(qkv, dyd, yd, lse_all, g, token=None):
    d = DIL_DILATIONS[g]
    sub_len = SEQ // d
    nst = SEQ // DIL_ST
    after, after_specs = _after(token)

    def body(q_ref, k_ref, v_ref, do_ref, y_ref, l_ref, *rest):
        out_ref, dk_scr, dv_scr, s_scr, dp_scr, p_scr, ds_scr, do_scr, y_scr, l_scr, dq_scr = rest[-11:]
        t = pl.program_id(1)
        base = t * DIL_ST

        @pl.when(t == 0)
        def _():
            dk_scr[...] = jnp.zeros_like(dk_scr)
            dv_scr[...] = jnp.zeros_like(dv_scr)

        for b in range(DIL_NB):
            tok = _dil_tok(g, b, t)
            do_scr[b * BAND:(b + 1) * BAND, :] = do_ref[tok, :]
            y_scr[b * BAND:(b + 1) * BAND, :] = y_ref[tok, :]
            l_scr[b * BAND:(b + 1) * BAND, :] = l_ref[tok, :]
        for hh in range(2):
            half = _head_half((BAND, 128), hh)
            half_st = _head_half((DIL_ST, 128), hh)
            dom = jnp.where(half_st, do_scr[...], 0.0)
            delta = jnp.sum(dom * y_scr[...], axis=-1, keepdims=True)
            lcol = jnp.max(jnp.where(half_st, l_scr[...], NEG), axis=-1, keepdims=True)
            for b in range(DIL_NB):
                rows = slice(b * BAND, (b + 1) * BAND)
                qv = q_ref[0, pl.ds(pl.multiple_of(base + (b + 1) * BAND, BAND), BAND), :]
                band = pl.ds(pl.multiple_of(base + b * BAND, BAND), 2 * BAND)
                sb = _nt(jnp.where(half, qv, jnp.zeros_like(qv)), k_ref[0, band, :])
                s_scr[rows, :] = jnp.where(_band_keep(g, b, t), sb, NEG)
                dp_scr[rows, :] = _nt(dom[rows, :].astype(BF16), v_ref[0, band, :])
            pr = jnp.exp(s_scr[...] - lcol)
            p_scr[...] = pr.astype(BF16)
            ds_scr[...] = (pr * (dp_scr[...] - delta)).astype(BF16)
            for b in range(DIL_NB):
                rows = slice(b * BAND, (b + 1) * BAND)
                qv = q_ref[0, pl.ds(pl.multiple_of(base + (b + 1) * BAND, BAND), BAND), :]
                band = pl.ds(pl.multiple_of(base + b * BAND, BAND), 2 * BAND)
                dqb = jnp.where(half, _nn(ds_scr[rows, :], k_ref[0, band, :]), 0.0)
                if hh == 0:
                    dq_scr[rows, :] = dqb
                else:
                    dq_scr[rows, :] += dqb
                half2 = _head_half((2 * BAND, 128), hh)
                dk_scr[band, :] += jnp.where(half2, _tn(ds_scr[rows, :], qv), 0.0)
                dv_scr[band, :] += _tn(p_scr[rows, :], dom[rows, :].astype(BF16))
        for b in range(DIL_NB):
            out_ref[pl.ds(0, 1), _dil_tok(g, b, t), :] = dq_scr[b * BAND:(b + 1) * BAND, :][None]

        @pl.when(t == nst - 1)
        def _():
            for r in range(d):
                rows = _strided(r, sub_len, d)
                out_ref[pl.ds(1, 1), rows, :] = dk_scr[BAND + r * sub_len:BAND + (r + 1) * sub_len, :][None]
                out_ref[pl.ds(2, 1), rows, :] = dv_scr[BAND + r * sub_len:BAND + (r + 1) * sub_len, :][None]

    def inp(tq):
        return pl.BlockSpec((1, BAND + SEQ, 128), lambda pr, t: (tq, 0, pr))

    tok_spec = pl.BlockSpec((SEQ, 128), lambda pr, t: (0, pr))
    st = (DIL_ST, 2 * BAND)
    return pl.pallas_call(
        body, name=f"dil_attn_bwd_g{g}", grid=(4, nst),
        in_specs=[inp(0), inp(1), inp(2), tok_spec, tok_spec, tok_spec] + after_specs,
        out_specs=pl.BlockSpec((3, SEQ, 128), lambda pr, t: (0, 0, pr)),
        out_shape=jax.ShapeDtypeStruct((3, SEQ, 512), F32),
        scratch_shapes=[pltpu.VMEM((BAND + SEQ, 128), F32), pltpu.VMEM((BAND + SEQ, 128), F32),
                        pltpu.VMEM(st, F32), pltpu.VMEM(st, F32), pltpu.VMEM(st, BF16), pltpu.VMEM(st, BF16),
                        pltpu.VMEM((DIL_ST, 128), F32), pltpu.VMEM((DIL_ST, 128), F32), pltpu.VMEM((DIL_ST, 128), F32),
                        pltpu.VMEM((DIL_ST, 128), F32)],
        compiler_params=_cparams(),
    )(qkv, qkv, qkv, dyd, yd, lse_all, *after)


def _band_masks():
    row = lax.broadcasted_iota(jnp.int32, (BAND, BAND), 0)
    col = lax.broadcasted_iota(jnp.int32, (BAND, BAND), 1)
    return row >= col, col >= row


def _dil_attn_fwd(qkv, g):
    d = DIL_DILATIONS[g]
    nbs = SEQ // d // BAND
    nblk = SEQ // BAND

    def body(q_ref, k_ref, v_ref, o_ref, l_ref):
        keep_c, keep_p = _band_masks()
        half0 = _head_half((BAND, 128), 0)

        def step(i, carry):
            cur = pl.ds(pl.multiple_of(i * BAND, BAND), BAND)
            prv = pl.ds(pl.multiple_of(jnp.maximum(i - 1, 0) * BAND, BAND), BAND)
            has_prev = (i % nbs) != 0
            qv = q_ref[0, cur, :]
            kc, kp = k_ref[0, cur, :], k_ref[0, prv, :]
            vc, vp = v_ref[0, cur, :], v_ref[0, prv, :]
            outs, lses = [], []
            for hh in range(2):
                qm = jnp.where(_head_half((BAND, 128), hh), qv, jnp.zeros_like(qv))
                sc = jnp.where(keep_c, _nt(qm, kc), NEG)
                sp = jnp.where(keep_p & has_prev, _nt(qm, kp), NEG)
                m = jnp.maximum(jnp.max(sc, axis=-1, keepdims=True), jnp.max(sp, axis=-1, keepdims=True))
                pc, pp = jnp.exp(sc - m), jnp.exp(sp - m)
                den = jnp.sum(pc, axis=-1, keepdims=True) + jnp.sum(pp, axis=-1, keepdims=True)
                o = (_nn(pc.astype(BF16), vc) + _nn(pp.astype(BF16), vp)) / den
                outs.append(o)
                lses.append(jnp.broadcast_to(m + jnp.log(den), (BAND, 128)))
            tok = _strided((i % nbs) * BAND * d + i // nbs, BAND, d)
            o_ref[tok, :] = jnp.where(half0, outs[0], outs[1])
            l_ref[tok, :] = jnp.where(half0, lses[0], lses[1])
            return carry

        lax.fori_loop(0, nblk, step, 0, unroll=DIL_UNROLL)

    def inp(tq):
        return pl.BlockSpec((1, SEQ, 128), lambda pr: (tq, 0, pr))

    out = pl.BlockSpec((SEQ, 128), lambda pr: (0, pr))
    return pl.pallas_call(
        body, name=f"dil_attn_fwd_g{g}", grid=(4,),
        in_specs=[inp(0), inp(1), inp(2)], out_specs=[out, out],
        out_shape=[jax.ShapeDtypeStruct((SEQ, 512), F32), jax.ShapeDtypeStruct((SEQ, 512), F32)],
        compiler_params=_cparams(),
    )(qkv, qkv, qkv)


def _dil_attn_bwd(qkv, dyd, yd, lse_all, g):
    d = DIL_DILATIONS[g]
    sub_len = SEQ // d
    nbs = sub_len // BAND
    nblk = SEQ // BAND

    def body(q_ref, k_ref, v_ref, do_ref, y_ref, l_ref, out_ref, dk_scr, dv_scr):
        keep_c, keep_p = _band_masks()
        dk_scr[...] = jnp.zeros_like(dk_scr)
        dv_scr[...] = jnp.zeros_like(dv_scr)

        def step(i, carry):
            cur = pl.ds(pl.multiple_of(i * BAND, BAND), BAND)
            prv = pl.ds(pl.multiple_of(jnp.maximum(i - 1, 0) * BAND, BAND), BAND)
            has_prev = (i % nbs) != 0
            tok = _strided((i % nbs) * BAND * d + i // nbs, BAND, d)
            qv = q_ref[0, cur, :]
            kc, kp = k_ref[0, cur, :], k_ref[0, prv, :]
            vc, vp = v_ref[0, cur, :], v_ref[0, prv, :]
            dov, yv, lv = do_ref[tok, :], y_ref[tok, :], l_ref[tok, :]
            dq = jnp.zeros((BAND, 128), F32)
            dkc = jnp.zeros((BAND, 128), F32)
            dkp = jnp.zeros((BAND, 128), F32)
            dvc = jnp.zeros((BAND, 128), F32)
            dvp = jnp.zeros((BAND, 128), F32)
            for hh in range(2):
                half = _head_half((BAND, 128), hh)
                qm = jnp.where(half, qv, jnp.zeros_like(qv))
                lcol = jnp.max(jnp.where(half, lv, NEG), axis=-1, keepdims=True)
                pc = jnp.exp(jnp.where(keep_c, _nt(qm, kc), NEG) - lcol)
                pp = jnp.exp(jnp.where(keep_p & has_prev, _nt(qm, kp), NEG) - lcol)
                dom = jnp.where(half, dov, 0.0)
                domb = dom.astype(BF16)
                delta = jnp.sum(dom * yv, axis=-1, keepdims=True)
                dsc = (pc * (_nt(domb, vc) - delta)).astype(BF16)
                dsp = (pp * (_nt(domb, vp) - delta)).astype(BF16)
                dvc = dvc + _tn(pc.astype(BF16), domb)
                dvp = dvp + _tn(pp.astype(BF16), domb)
                dq = dq + jnp.where(half, _nn(dsc, kc) + _nn(dsp, kp), 0.0)
                dkc = dkc + jnp.where(half, _tn(dsc, qv), 0.0)
                dkp = dkp + jnp.where(half, _tn(dsp, qv), 0.0)
            out_ref[pl.ds(0, 1), tok, :] = dq[None]
            dk_scr[cur, :] += dkc
            dk_scr[prv, :] += dkp
            dv_scr[cur, :] += dvc
            dv_scr[prv, :] += dvp
            return carry

        lax.fori_loop(0, nblk, step, 0, unroll=DIL_UNROLL)
        for r in range(d):
            rows = _strided(r, sub_len, d)
            out_ref[pl.ds(1, 1), rows, :] = dk_scr[r * sub_len:(r + 1) * sub_len, :][None]
            out_ref[pl.ds(2, 1), rows, :] = dv_scr[r * sub_len:(r + 1) * sub_len, :][None]

    def inp(tq):
        return pl.BlockSpec((1, SEQ, 128), lambda pr: (tq, 0, pr))

    tok_spec = pl.BlockSpec((SEQ, 128), lambda pr: (0, pr))
    return pl.pallas_call(
        body, name=f"dil_attn_bwd_g{g}", grid=(4,),
        in_specs=[inp(0), inp(1), inp(2), tok_spec, tok_spec, tok_spec],
        out_specs=pl.BlockSpec((3, SEQ, 128), lambda pr: (0, 0, pr)),
        out_shape=jax.ShapeDtypeStruct((3, SEQ, 512), F32),
        scratch_shapes=[pltpu.VMEM((SEQ, 128), F32), pltpu.VMEM((SEQ, 128), F32)],
        compiler_params=_cparams(),
    )(qkv, qkv, qkv, dyd, yd, lse_all)


def _dil_prep_bwd(dp_in, dqkv, rc, rs, g):
    tm = 512

    def body(dp_any, g_ref, c_ref, s_ref, dp_ref):
        del dp_any
        tq = pl.program_id(0)
        mult = jnp.where(tq == 0, DIL_SCALE, 1.0).astype(F32)
        lanes = _rope_lanes((tm, 128), DIL_ROPE_HALF, 64, 0)
        cv, sv = c_ref[...], s_ref[...]
        for pr in range(4):
            gv = g_ref[0, :, pr * 128:(pr + 1) * 128]
            roped = _rope_bwd(gv, cv, sv, DIL_ROPE_HALF, lanes)
            dp_ref[:, pr * 128:(pr + 1) * 128] = (jnp.where(tq == 2, gv, roped) * mult).astype(BF16)

    tab = pl.BlockSpec((tm, 128), lambda tq, i: (i, 0))
    return pl.pallas_call(
        body, name=f"dil_prep_bwd_g{g}", grid=(3, SEQ // tm),
        in_specs=[pl.BlockSpec(memory_space=pl.ANY),
                  pl.BlockSpec((1, tm, 512), lambda tq, i: (tq, i, 0)), tab, tab],
        out_specs=pl.BlockSpec((tm, 512), lambda tq, i: (i, COL_QKV // 512 + tq * 3 + g)),
        out_shape=jax.ShapeDtypeStruct((SEQ, N_PAD), BF16),
        input_output_aliases={0: 0},
    )(dp_in, dqkv, rc, rs)


TAIL_T = 128


def _tail(p, ya, o_g, l_g, x, target, wpm, wpd, wout, post_g):
    tm = TAIL_T

    def body(pgz_ref, ya_ref, o0_ref, o1_ref, o2_ref, l0_ref, l1_ref, l2_ref, x_ref, t_ref,
             wpm_ref, wpd_ref, wout_ref, pg_ref,
             dp_ref, dy_ref, mg_ref, dt_ref, ua_ref, dpa_ref, ud_ref, dpd_ref, dya_ref, dyd_ref,
             yd_ref, lse_ref, loss_ref, dgp_ref):
        l0, l1, l2 = l0_ref[...], l1_ref[...], l2_ref[...]
        mx = jnp.maximum(jnp.maximum(l0, l1), l2)
        e0, e1, e2 = jnp.exp(l0 - mx), jnp.exp(l1 - mx), jnp.exp(l2 - mx)
        den = e0 + e1 + e2
        yd = (e0 * o0_ref[...] + e1 * o1_ref[...] + e2 * o2_ref[...]) / den
        yd_ref[...] = yd
        lse_ref[...] = mx + jnp.log(den)
        ya = ya_ref[...]

        gm, gd = pgz_ref[:, 0:1024], pgz_ref[:, 1024:2048]
        zm, zd = pgz_ref[:, 2048:2560], pgz_ref[:, 2560:3072]
        szm, szd = _sigmoid(zm), _sigmoid(zd)
        sm, sd = zm * szm, zd * szd
        ua = (ya * sm).astype(BF16)
        ud = (yd * sd).astype(BF16)
        ua_ref[...] = ua
        ud_ref[...] = ud
        pa = _nn(ua, wpm_ref[...])
        pd = _nn(ud, wpd_ref[...])
        sgm, sgd = _sigmoid(gm), _sigmoid(gd)
        mg = (sgm * pa + sgd * pd).astype(BF16)
        mg_ref[...] = mg
        t = _nn(mg, wout_ref[...])
        r3 = lax.rsqrt(jnp.mean(t * t, axis=-1, keepdims=True) + EPS)
        n = t * r3
        pg = pg_ref[...]
        err = x_ref[...] + n * pg - t_ref[...]
        lpart = jnp.sum(err * err, axis=0, keepdims=True)

        dy = err * (1.0 / D_MODEL)
        dy_ref[...] = dy
        gpart = jnp.sum(dy * n, axis=0, keepdims=True)
        dn = dy * pg
        dt = (r3 * (dn - n * jnp.mean(dn * n, axis=-1, keepdims=True))).astype(BF16)
        dt_ref[...] = dt
        dmg = _nt(dt, wout_ref[...])
        dpa = (dmg * sgm).astype(BF16)
        dpd = (dmg * sgd).astype(BF16)
        dpa_ref[...] = dpa
        dpd_ref[...] = dpd
        dp_ref[:, 0:1024] = (dmg * pa * sgm * (1.0 - sgm)).astype(BF16)
        dp_ref[:, 1024:2048] = (dmg * pd * sgd * (1.0 - sgd)).astype(BF16)
        dua = _nt(dpa, wpm_ref[...])
        dud = _nt(dpd, wpd_ref[...])
        dya_ref[...] = dua * sm
        dyd_ref[...] = dud * sd
        dp_ref[:, 2048:2560] = (dua * ya * szm * (1.0 + zm * (1.0 - szm))).astype(BF16)
        dp_ref[:, 2560:3072] = (dud * yd * szd * (1.0 + zd * (1.0 - szd))).astype(BF16)

        @pl.when(pl.program_id(0) == 0)
        def _():
            loss_ref[...] = lpart
            dgp_ref[...] = gpart

        @pl.when(pl.program_id(0) > 0)
        def _():
            loss_ref[...] += lpart
            dgp_ref[...] += gpart

    def rows(w):
        return pl.BlockSpec((tm, w), lambda i: (i, 0))

    def full(shape):
        return pl.BlockSpec(shape, lambda i: (0, 0))

    def sds(w, dt):
        return jax.ShapeDtypeStruct((SEQ, w), dt)

    return pl.pallas_call(
        body, name="tail", grid=(SEQ // tm,),
        in_specs=[rows(3072), rows(512), rows(512), rows(512), rows(512), rows(512), rows(512), rows(512),
                  rows(1024), rows(1024), full((512, 1024)), full((512, 1024)), full((1024, 1024)), full((1, 1024))],
        out_specs=[rows(3072), rows(1024), rows(1024), rows(1024), rows(512), rows(1024), rows(512), rows(1024),
                   rows(512), rows(512), rows(512), rows(512), full((1, 1024)), full((1, 1024))],
        out_shape=[sds(N_PAD, BF16), sds(1024, F32), sds(1024, BF16), sds(1024, BF16), sds(512, BF16),
                   sds(1024, BF16), sds(512, BF16), sds(1024, BF16), sds(512, F32), sds(512, F32),
                   sds(512, F32), sds(512, F32),
                   jax.ShapeDtypeStruct((1, 1024), F32), jax.ShapeDtypeStruct((1, 1024), F32)],
        compiler_params=_cparams(),
    )(p, ya, o_g[0], o_g[1], o_g[2], l_g[0], l_g[1], l_g[2], x, target, wpm, wpd, wout, post_g)


def _sum_parts(parts, tr, name):
    n, r, w = parts.shape

    def body(p_ref, o_ref):
        acc = p_ref[0].astype(F32)
        for s in range(1, n):
            acc = acc + p_ref[s].astype(F32)
        o_ref[...] = acc

    return pl.pallas_call(
        body, name=name, grid=(r // tr,),
        in_specs=[pl.BlockSpec((n, tr, w), lambda i: (0, i, 0))],
        out_specs=pl.BlockSpec((tr, w), lambda i: (i, 0)),
        out_shape=jax.ShapeDtypeStruct((r, w), F32),
    )(parts)


def _adamw(w, g, m, v, name):
    r, c = w.shape
    tr = 128 if r % 128 == 0 else r
    c1 = 1.0 - ADAM_B1 ** ADAM_STEP
    c2 = 1.0 - ADAM_B2 ** ADAM_STEP

    def body(w_ref, g_ref, m_ref, v_ref, d_ref, nm_ref, nv_ref):
        gv = g_ref[...]
        nm = ADAM_B1 * m_ref[...] + (1.0 - ADAM_B1) * gv
        nv = ADAM_B2 * v_ref[...] + (1.0 - ADAM_B2) * (gv * gv)
        nm_ref[...] = nm
        nv_ref[...] = nv
        d_ref[...] = -ADAM_LR * ((nm / c1) / (jnp.sqrt(nv / c2) + ADAM_EPS) + ADAM_WD * w_ref[...])

    spec = pl.BlockSpec((tr, c), lambda i: (i, 0))
    sd = jax.ShapeDtypeStruct((r, c), F32)
    return pl.pallas_call(
        body, name=name, grid=(r // tr,),
        in_specs=[spec] * 4, out_specs=[spec] * 3, out_shape=[sd] * 3,
    )(w, g, m, v)


ANY = pl.BlockSpec(memory_space=pl.ANY)


def _my_place():
    return lax.axis_index("x"), lax.axis_index("y"), lax.axis_index("c")


def _allgather_weights(mats):
    def body(*refs):
        w_refs, out_refs = refs[:N_MATS], refs[N_MATS:2 * N_MATS]
        send_sems, recv_sems = refs[2 * N_MATS:]
        x, y, c = _my_place()
        sibling = (x, y, 1 - c)
        chips = [(1 - x, y), (x, 1 - y), (1 - x, 1 - y)]

        def copy(k, src, dst, to):
            return pltpu.make_async_remote_copy(src_ref=src, dst_ref=dst, send_sem=send_sems.at[k],
                                                recv_sem=recv_sems.at[k], device_id=to, device_id_type=MESH)

        def half(mi, shard, hc):
            hr = SHARD_SHAPES[mi][0] // 2
            return out_refs[mi].at[shard, pl.ds(pl.multiple_of(hc * hr, 16), hr), :]

        started = []
        for mi in range(N_MATS):
            hr = SHARD_SHAPES[mi][0] // 2
            my_half = w_refs[mi].at[pl.ds(pl.multiple_of(c * hr, 16), hr), :]
            for j, (cx, cy) in enumerate(chips):
                cp = copy(mi * 6 + j, my_half, half(mi, 2 * x + y, c), (cx, cy, c))
                cp.start()
                started.append(cp)
        for mi in range(N_MATS):
            for j, (cx, cy) in enumerate(chips):
                landed = half(mi, 2 * cx + cy, c)
                copy(mi * 6 + j, landed, landed, (cx, cy, c)).wait_recv()
                fw = copy(mi * 6 + 3 + j, landed, landed, sibling)
                fw.start()
                started.append(fw)
        for mi in range(N_MATS):
            for j, (cx, cy) in enumerate(chips):
                other = half(mi, 2 * cx + cy, 1 - c)
                copy(mi * 6 + 3 + j, other, other, sibling).wait_recv()
        for cp in started:
            cp.wait_send()

    return pl.pallas_call(
        body, name="allgather_weights",
        in_specs=[ANY] * N_MATS, out_specs=[ANY] * N_MATS,
        out_shape=[jax.ShapeDtypeStruct((4, r, c), BF16) for r, c in SHARD_SHAPES],
        scratch_shapes=[pltpu.SemaphoreType.DMA((6 * N_MATS,)), pltpu.SemaphoreType.DMA((6 * N_MATS,))],
    )(*mats)


HBM = pl.BlockSpec(memory_space=pltpu.HBM)
SEM = pl.BlockSpec(memory_space=pltpu.SEMAPHORE)
DATAFLOW = pltpu.SideEffectType.DATAFLOW_SIDE_EFFECTING


def _peers(x, y, c):
    out = []
    for k in range(1, 8):
        px, py, pc = x ^ (k >> 2), y ^ ((k >> 1) & 1), c ^ (k & 1)
        out.append((k - 1, (px, py, pc), 4 * px + 2 * py + pc))
    return out


def _exchange_start(parts, name):
    n = len(parts)

    def body(*refs):
        p_refs, land_refs = refs[:n], refs[n:2 * n]
        send_sems, recv_sems, token = refs[2 * n], refs[2 * n + 1], refs[-1]
        x, y, c = _my_place()
        me = 4 * x + 2 * y + c
        for k, dev, peer in _peers(x, y, c):
            for mi in range(n):
                pltpu.make_async_remote_copy(
                    src_ref=p_refs[mi].at[peer], dst_ref=land_refs[mi].at[me], send_sem=send_sems.at[k * n + mi],
                    recv_sem=recv_sems.at[k * n + mi], device_id=dev, device_id_type=MESH).start()
        token[...] = jnp.zeros_like(token)

    hbm = [pltpu.HBM(p.shape, p.dtype) for p in parts]
    outs = pl.pallas_call(
        body, name=name + "_start",
        out_shape=(pltpu.SemaphoreType.DMA((7 * n,)), pltpu.SemaphoreType.DMA((7 * n,)), *hbm, *hbm,
                   jax.ShapeDtypeStruct((8, 128), F32)),
        in_specs=[HBM] * (2 * n), out_specs=(SEM, SEM, *[HBM] * (2 * n), pl.BlockSpec(memory_space=pltpu.VMEM)),
        input_output_aliases={i: 2 + i for i in range(2 * n)},
        compiler_params=pltpu.CompilerParams(has_side_effects=DATAFLOW),
    )(*[pltpu.with_memory_space_constraint(p, pltpu.HBM) for p in parts],
      *[pltpu.with_memory_space_constraint(lax.empty(p.shape, p.dtype), pltpu.HBM) for p in parts])
    return (name, outs[:-1]), outs[-1]


def _exchange_wait(handle, after):
    name, outs = handle
    n = (len(outs) - 2) // 2

    def body(*refs):
        p_refs, land_refs = refs[:n], refs[n:2 * n]
        send_sems, recv_sems = refs[2 * n], refs[2 * n + 1]
        x, y, c = _my_place()
        me = 4 * x + 2 * y + c
        for k, dev, peer in _peers(x, y, c):
            for mi in range(n):
                pltpu.make_async_remote_copy(
                    src_ref=p_refs[mi].at[peer], dst_ref=land_refs[mi].at[me], send_sem=send_sems.at[k * n + mi],
                    recv_sem=recv_sems.at[k * n + mi], device_id=dev, device_id_type=MESH).wait_send()
                slot = land_refs[mi].at[peer]
                pltpu.make_async_remote_copy(
                    src_ref=slot, dst_ref=slot, send_sem=send_sems.at[k * n + mi],
                    recv_sem=recv_sems.at[k * n + mi], device_id=dev, device_id_type=MESH).wait_recv()

    bufs = outs[2:]
    res = pl.pallas_call(
        body, name=name + "_wait", out_shape=tuple(pltpu.HBM(b.shape, b.dtype) for b in bufs),
        in_specs=[HBM] * (2 * n) + [SEM, SEM, ANY], out_specs=tuple([HBM] * (2 * n)),
        input_output_aliases={i: i for i in range(2 * n)},
        compiler_params=pltpu.CompilerParams(has_side_effects=DATAFLOW),
    )(*bufs, outs[0], outs[1], after)
    return list(res[n:])


def _swap_halves(halves, gvec):
    def body(*refs):
        g_refs, gv_ref = refs[:N_MATS], refs[N_MATS]
        out_refs, rg_ref = refs[N_MATS + 1:2 * N_MATS + 1], refs[2 * N_MATS + 1]
        send_sems, recv_sems = refs[2 * N_MATS + 2:]
        x, y, c = _my_place()
        me = 4 * x + 2 * y + c
        sends = []
        for mi in range(N_MATS):
            cp = pltpu.make_async_remote_copy(src_ref=g_refs[mi], dst_ref=out_refs[mi].at[c], send_sem=send_sems.at[mi],
                                              recv_sem=recv_sems.at[mi], device_id=(x, y, 1 - c), device_id_type=MESH)
            cp.start()
            sends.append(cp)
        for k, dev, peer in _peers(x, y, c):
            cp = pltpu.make_async_remote_copy(src_ref=gv_ref, dst_ref=rg_ref.at[me], send_sem=send_sems.at[N_MATS + k],
                                              recv_sem=recv_sems.at[N_MATS + k], device_id=dev, device_id_type=MESH)
            cp.start()
            sends.append(cp)
        for mi in range(N_MATS):
            got = out_refs[mi].at[1 - c]
            pltpu.make_async_remote_copy(src_ref=got, dst_ref=got, send_sem=send_sems.at[mi], recv_sem=recv_sems.at[mi],
                                         device_id=(x, y, 1 - c), device_id_type=MESH).wait_recv()
        for k, dev, peer in _peers(x, y, c):
            got = rg_ref.at[peer]
            pltpu.make_async_remote_copy(src_ref=got, dst_ref=got, send_sem=send_sems.at[N_MATS + k],
                                         recv_sem=recv_sems.at[N_MATS + k], device_id=dev, device_id_type=MESH).wait_recv()
        for cp in sends:
            cp.wait_send()

    outs = pl.pallas_call(
        body, name="swap_halves",
        in_specs=[ANY] * (N_MATS + 1), out_specs=[ANY] * (N_MATS + 1),
        out_shape=[jax.ShapeDtypeStruct((2, r // 2, c), F32) for r, c in SHARD_SHAPES]
        + [jax.ShapeDtypeStruct((8, 8, N_GAINS), F32)],
        scratch_shapes=[pltpu.SemaphoreType.DMA((N_MATS + 7,)), pltpu.SemaphoreType.DMA((N_MATS + 7,))],
    )(*halves, gvec)
    return outs[:N_MATS], outs[N_MATS]


def _set_slot(arr, block, idx):
    return lax.dynamic_update_slice(arr, block[None], (idx,) + (0,) * block.ndim)


PAD_RUNS = ((6304, 8352, 0), (5280, 6304, COL_Z), (672, 5280, COL_QKV), (0, 640, COL_LAT), (640, 672, COL_LAT + 704))
W_IN_SHARD = 2088


def _full_weights(gathered):
    def cols(a):
        return jnp.concatenate([a[s] for s in range(4)], axis=1)

    w_uq, w_ukv, w_pm, w_pd = [cols(a) for a in gathered[1:5]]
    w_out = gathered[5].reshape(D_MODEL, D_MODEL)
    g_in = gathered[0]
    pieces, at = [], 0
    for lo, hi, pad_lo in sorted(PAD_RUNS, key=lambda t: t[2]):
        if pad_lo > at:
            pieces.append(jnp.zeros((D_MODEL, pad_lo - at), g_in.dtype))
        for s in range(4):
            a_, b_ = max(lo, s * W_IN_SHARD), min(hi, (s + 1) * W_IN_SHARD)
            if a_ < b_:
                pieces.append(g_in[s][:, a_ - s * W_IN_SHARD:b_ - s * W_IN_SHARD])
        at = pad_lo + hi - lo
    pieces.append(jnp.zeros((D_MODEL, N_PAD - at), g_in.dtype))
    w_pad = jnp.concatenate(pieces, axis=1)
    z32 = jnp.zeros((Q_RANK, 32), w_uq.dtype)
    wuq_pad = jnp.concatenate([t for h in range(MLA_HEADS) for t in (w_uq[:, h * 96:(h + 1) * 96], z32)], axis=1)
    z64 = jnp.zeros((KV_RANK, 64), w_ukv.dtype)
    wk_pad = jnp.concatenate([t for h in range(MLA_HEADS) for t in (w_ukv[:, h * 128:h * 128 + 64], z64)], axis=1)
    wv = jnp.concatenate([w_ukv[:, h * 128 + 64:(h + 1) * 128] for h in range(MLA_HEADS)], axis=1)
    return w_pad, wuq_pad, wk_pad, wv, w_pm, w_pd, w_out


def _grad_parts_in(dw_pad):
    def in_block(s, h):
        rows = slice(h * 512, (h + 1) * 512)
        out = []
        for lo, hi, pad_lo in sorted(PAD_RUNS):
            a_, b_ = max(lo, s * W_IN_SHARD), min(hi, (s + 1) * W_IN_SHARD)
            if a_ < b_:
                out.append(dw_pad[rows, pad_lo + a_ - lo:pad_lo + b_ - lo])
        return jnp.concatenate(out, axis=1).astype(BF16)

    return jnp.stack([in_block(s, h) for s in range(4) for h in range(2)])


def _grad_parts_small(dwuq_pad, dwk_pad, dwv, dwpm, dwpd, dwout):
    d_uq = jnp.concatenate([dwuq_pad[:, h * 128:h * 128 + 96] for h in range(MLA_HEADS)], axis=1)
    d_ukv = jnp.concatenate([t for h in range(MLA_HEADS) for t in (dwk_pad[:, h * 128:h * 128 + 64], dwv[:, h * 64:(h + 1) * 64])],
                            axis=1)

    def col_blocks(m):
        r, c = m.shape[0] // 2, m.shape[1] // 4
        return jnp.stack([m[h * r:(h + 1) * r, s * c:(s + 1) * c].astype(BF16) for s in range(4) for h in range(2)])

    return [col_blocks(d_uq), col_blocks(d_ukv), col_blocks(dwpm), col_blocks(dwpd),
            dwout.astype(BF16).reshape(8, 128, D_MODEL)]


def _rope_tables(positions):
    pos = positions.reshape(SEQ).astype(F32)
    lane = jnp.arange(128)

    def table(rot, first, period):
        inv = ROPE_THETA ** (-jnp.arange(0, rot, 2, dtype=F32) / rot)
        half = rot // 2
        off = lane % period - first
        in1, in2 = (off >= 0) & (off < half), (off >= half) & (off < rot)
        inv_lane = jnp.where(in1 | in2, inv[jnp.clip(off % half, 0, half - 1)], 0.0)
        sign = jnp.where(in1, -1.0, 1.0).astype(F32)
        ang = pos[:, None] * inv_lane[None, :]
        return jnp.cos(ang), jnp.sin(ang) * sign[None, :]

    return table(32, 64, 128), table(16, 0, 64)


def _device_grads(x, positions, target, gains, gathered, send):
    pre_g, q_g, kv_g, post_g = gains
    w_pad, wuq_pad, wk_pad, wv, w_pm, w_pd, w_out = _full_weights(gathered)
    (mc, ms), (dc, ds) = _rope_tables(positions)

    h = _prenorm_fwd(x, pre_g)
    p = _matmul(h, w_pad, "nn", F32, 512, 1408, 1024, "in_proj")
    cqn, ckvn, q, k, v = _mla_prep_fwd(p, q_g, kv_g, wuq_pad, wk_pad, wv, mc, ms)
    ya, lse_m = _mla_flash_fwd(q, k, v)
    qkv = [_dil_prep_fwd(p, dc, ds, g) for g in range(3)]
    o_g, l_g = zip(*[_dil_attn_fwd2(qkv[g], g) for g in range(3)])
    (dp, dy, mg, dt, ua, dpa, ud, dpd, dya, dyd, yd, lse_d, loss_cols, dg_post) = _tail(
        p, ya, o_g, l_g, x, target, w_pm, w_pd, w_out, post_g)

    dq, dk, dv = _mla_flash_bwd(q, k, v, ya, dya, lse_m)
    dp, dqb, dkb, dvb, dg_q, dg_kv = _mla_prep_bwd(dp, p, dq, dk, dv, q_g, kv_g, wuq_pad, wk_pad, wv, mc, ms)
    dwuq_pad = _matmul(cqn, dqb, "tn", F32, Q_RANK, 1024, 512, "dw_uq")
    dwk_pad = _matmul(ckvn, dkb, "tn", F32, KV_RANK, 1024, 512, "dw_k")
    dwv = _matmul(ckvn, dvb, "tn", F32, KV_RANK, 512, 512, "dw_v")
    dwpm = _matmul(ua, dpa, "tn", F32, 512, 1024, 512, "dw_proj_mla")
    dwpd = _matmul(ud, dpd, "tn", F32, 512, 1024, 512, "dw_proj_dil")
    dwout = _matmul(mg, dt, "tn", F32, 1024, 1024, 512, "dw_out")
    small, token = send(_grad_parts_small(dwuq_pad, dwk_pad, dwv, dwpm, dwpd, dwout), "exchange_small")
    for g in range(3):
        dqkv = _dil_attn_bwd2(qkv[g], dyd, yd, lse_d, g, token if g == 0 else None)
        dp = _dil_prep_bwd(dp, dqkv, dc, ds, g)

    dw_pad = _matmul(h, dp, "tn", F32, 1024, 1408, 512, "dw_in")
    big, token = send([_grad_parts_in(dw_pad)], "exchange_in")
    dh = _matmul(dp, w_pad, "nt", F32, 512, 1024, 1408, "dh", token)
    grad_x, dg_pre = _prenorm_bwd(x, dh, dy, pre_g)

    gvec = jnp.concatenate([dg_pre, dg_q, dg_kv, dg_post], axis=1)
    return loss_cols, grad_x, (big, small), gvec


def kernel(x, positions, pre_norm_g, w_in, q_norm_g, w_uq, kv_norm_g, w_ukv, w_proj_mla, w_proj_dil, w_out, post_norm_g, loss_target, m_pre_norm_g, m_w_in, m_q_norm_g, m_w_uq, m_kv_norm_g, m_w_ukv, m_w_proj_mla, m_w_proj_dil, m_w_out, m_post_norm_g, v_pre_norm_g, v_w_in, v_q_norm_g, v_w_uq, v_kv_norm_g, v_w_ukv, v_w_proj_mla, v_w_proj_dil, v_w_out, v_post_norm_g):
    xi, yi, ci = _my_place()
    chip, me = 2 * xi + yi, 4 * xi + 2 * yi + ci
    mats = [w.reshape(w.shape[1:]).astype(BF16) for w in (w_in, w_uq, w_ukv, w_proj_mla, w_proj_dil, w_out)]
    gathered = [_set_slot(g, m, chip) for g, m in zip(_allgather_weights(mats), mats)]
    gains = (pre_norm_g, q_norm_g, kv_norm_g, post_norm_g)
    sent = {}

    def send(blocks, name):
        sent[name] = blocks
        return _exchange_start(blocks, name)

    loss_cols, grad_x, (big, small), gvec = _device_grads(x[0], positions, loss_target[0], gains, gathered, send)

    loss = lax.psum(jnp.sum(loss_cols) * (0.5 / D_MODEL), ("x", "y", "c"))

    recv = _exchange_wait(big, grad_x) + _exchange_wait(small, grad_x)
    own = [lax.dynamic_index_in_dim(p, me, 0, keepdims=False) for p in sent["exchange_in"] + sent["exchange_small"]]
    recv = [_set_slot(r, o, me) for r, o in zip(recv, own)]
    halves = [_sum_parts(recv[mi], 64, f"sum_grad_{mi}") for mi in range(N_MATS)]
    gvec8 = jnp.pad(gvec, ((0, 7), (0, 0)))
    swapped, recv_gains = _swap_halves(halves, gvec8)
    g_gains = _sum_parts(_set_slot(recv_gains, gvec8, me), 8, "sum_gain_parts")[0:1]
    g_mats = [_set_slot(s, hf, ci).reshape((1,) + shp) for s, hf, shp in zip(swapped, halves, SHARD_SHAPES)]

    off = [0, 1024, 1408, 1664, 2688]
    g_gain = [g_gains[:, off[i]:off[i + 1]] for i in range(4)]
    grads = [g_gain[0], g_mats[0], g_gain[1], g_mats[1], g_gain[2], g_mats[2], g_mats[3], g_mats[4], g_mats[5], g_gain[3]]
    ws = [pre_norm_g, w_in, q_norm_g, w_uq, kv_norm_g, w_ukv, w_proj_mla, w_proj_dil, w_out, post_norm_g]
    ms = [m_pre_norm_g, m_w_in, m_q_norm_g, m_w_uq, m_kv_norm_g, m_w_ukv, m_w_proj_mla, m_w_proj_dil, m_w_out, m_post_norm_g]
    vs = [v_pre_norm_g, v_w_in, v_q_norm_g, v_w_uq, v_kv_norm_g, v_w_ukv, v_w_proj_mla, v_w_proj_dil, v_w_out, v_post_norm_g]
    deltas, new_m, new_v = [], [], []
    for i, (w, g, m, v) in enumerate(zip(ws, grads, ms, vs)):
        shp = w.shape
        two_d = shp[-2:]
        d_, m_, v_ = _adamw(w.reshape(two_d), g.reshape(two_d), m.reshape(two_d), v.reshape(two_d), f"adamw_{i}")
        deltas.append(d_.reshape(shp))
        new_m.append(m_.reshape(shp))
        new_v.append(v_.reshape(shp))
    return (loss, grad_x.reshape(x.shape), *grads, *deltas, *new_m, *new_v)
```

```python
import jax
import jax.numpy as jnp
from jax import lax
from jax.experimental import pallas as pl
from jax.experimental.pallas import tpu as pltpu

F32 = jnp.float32
BF16 = jnp.bfloat16

SEQ = 4096
D_MODEL = 1024
EPS = 1e-6
ROPE_THETA = 500000.0
MLA_HEADS = 8
Q_RANK = 384
KV_RANK = 256
MLA_SCALE = 96.0 ** -0.5
MLA_ROPE_HALF = 16
DIL_DILATIONS = (1, 4, 16)
DIL_ROPE_HALF = 8
DIL_SCALE = 0.125
BAND = 128

N_LAT = 768
COL_Z, COL_QKV, COL_LAT = 2048, 3072, 7680
N_PAD = 8448
IN_SPLITS = (384, 256, 32, 4608, 512, 512, 1024, 1024)

SHARD_SHAPES = ((1024, 2088), (384, 192), (256, 256), (512, 256), (512, 256), (256, 1024))
N_MATS = len(SHARD_SHAPES)
N_GAINS = 2688

ADAM_LR, ADAM_B1, ADAM_B2, ADAM_EPS, ADAM_WD, ADAM_STEP = 0.001, 0.9, 0.999, 1e-08, 0.01, 10

VMEM_LIMIT = 56 * 1024 * 1024
NEG = -1e30
MESH = pl.DeviceIdType.MESH


def _cparams(**kw):
    return pltpu.CompilerParams(vmem_limit_bytes=VMEM_LIMIT, **kw)


def _dot(a, b, dims):
    return lax.dot_general(a, b, (dims, ((), ())), preferred_element_type=F32)


def _nn(a, b):
    return _dot(a, b, ((1,), (0,)))


def _nt(a, b):
    return _dot(a, b, ((1,), (1,)))


def _tn(a, b):
    return _dot(a, b, ((0,), (0,)))


def _rope_lanes(shape, half, period, first):
    lane = lax.broadcasted_iota(jnp.int32, shape, len(shape) - 1) % period
    return (lane >= first) & (lane < first + half), (lane >= first + half) & (lane < first + 2 * half)


def _rope_fwd(x, c, s, half, lanes):
    x1, _ = lanes
    return x * c + jnp.where(x1, pltpu.roll(x, 128 - half, 1), pltpu.roll(x, half, 1)) * s


def _rope_bwd(g, c, s, half, lanes):
    x1, x2 = lanes
    gs = g * s
    return g * c + jnp.where(x2, pltpu.roll(gs, half, 1), jnp.where(x1, pltpu.roll(gs, 128 - half, 1), 0.0))


def _sigmoid(x):
    return 1.0 / (1.0 + jnp.exp(-x))


def _after(token):
    return ([], []) if token is None else ([token], [pl.BlockSpec(memory_space=pl.ANY)])


def _matmul(a, b, mode, out_dtype, tm, tn, tk, name, token=None):
    after, after_specs = _after(token)
    if mode == "nn":
        (m, k), n = a.shape, b.shape[1]
        a_spec = pl.BlockSpec((tm, tk), lambda j, i, kk: (i, kk))
        b_spec = pl.BlockSpec((tk, tn), lambda j, i, kk: (kk, j))
        dot = _nn
    elif mode == "nt":
        (m, k), n = a.shape, b.shape[0]
        a_spec = pl.BlockSpec((tm, tk), lambda j, i, kk: (i, kk))
        b_spec = pl.BlockSpec((tn, tk), lambda j, i, kk: (j, kk))
        dot = _nt
    else:
        (k, m), n = a.shape, b.shape[1]
        a_spec = pl.BlockSpec((tk, tm), lambda j, i, kk: (kk, i))
        b_spec = pl.BlockSpec((tk, tn), lambda j, i, kk: (kk, j))
        dot = _tn
    assert m % tm == 0 and n % tn == 0 and k % tk == 0, (name, m, n, k, tm, tn, tk)
    nk = k // tk

    def body(a_ref, b_ref, *rest):
        o_ref, acc_ref = rest[-2:]
        kk = pl.program_id(2)
        part = dot(a_ref[...], b_ref[...])

        @pl.when(kk == 0)
        def _():
            acc_ref[...] = part

        @pl.when(kk > 0)
        def _():
            acc_ref[...] += part

        @pl.when(kk == nk - 1)
        def _():
            o_ref[...] = acc_ref[...].astype(o_ref.dtype)

    return pl.pallas_call(
        body, name=name, grid=(n // tn, m // tm, nk),
        in_specs=[a_spec, b_spec] + after_specs,
        out_specs=pl.BlockSpec((tm, tn), lambda j, i, kk: (i, j)),
        out_shape=jax.ShapeDtypeStruct((m, n), out_dtype),
        scratch_shapes=[pltpu.VMEM((tm, tn), F32)],
        compiler_params=_cparams(),
    )(a, b, *after)


def _prenorm_fwd(x, g):
    tm = 512

    def body(x_ref, g_ref, h_ref):
        xv = x_ref[...]
        r = lax.rsqrt(jnp.mean(xv * xv, axis=-1, keepdims=True) + EPS)
        h_ref[...] = (xv * r * g_ref[...]).astype(BF16)

    return pl.pallas_call(
        body, name="prenorm_fwd", grid=(SEQ // tm,),
        in_specs=[pl.BlockSpec((tm, D_MODEL), lambda i: (i, 0)), pl.BlockSpec((1, D_MODEL), lambda i: (0, 0))],
        out_specs=pl.BlockSpec((tm, D_MODEL), lambda i: (i, 0)),
        out_shape=jax.ShapeDtypeStruct((SEQ, D_MODEL), BF16),
    )(x, g)


def _prenorm_bwd(x, dh, dy, g):
    tm = 512

    def body(x_ref, dh_ref, dy_ref, g_ref, gx_ref, dg_ref):
        xv = x_ref[...]
        r = lax.rsqrt(jnp.mean(xv * xv, axis=-1, keepdims=True) + EPS)
        n = xv * r
        dhv = dh_ref[...]
        dn = dhv * g_ref[...]
        gx_ref[...] = dy_ref[...] + r * (dn - n * jnp.mean(dn * n, axis=-1, keepdims=True))
        part = jnp.sum(dhv * n, axis=0, keepdims=True)

        @pl.when(pl.program_id(0) == 0)
        def _():
            dg_ref[...] = part

        @pl.when(pl.program_id(0) > 0)
        def _():
            dg_ref[...] += part

    row = pl.BlockSpec((tm, D_MODEL), lambda i: (i, 0))
    vec = pl.BlockSpec((1, D_MODEL), lambda i: (0, 0))
    return pl.pallas_call(
        body, name="prenorm_bwd", grid=(SEQ // tm,),
        in_specs=[row, row, row, vec], out_specs=[row, vec],
        out_shape=[jax.ShapeDtypeStruct((SEQ, D_MODEL), F32), jax.ShapeDtypeStruct((1, D_MODEL), F32)],
        compiler_params=_cparams(),
    )(x, dh, dy, g)


def _mla_prep_fwd(p, qg, kvg, wuq, wk, wv, rc, rs):
    tm = 512

    def body(lat_ref, qg_ref, kvg_ref, wuq_ref, wk_ref, wv_ref, c_ref, s_ref,
             cqn_ref, ckvn_ref, q_ref, k_ref, v_ref):
        c, s = c_ref[...], s_ref[...]
        lanes = _rope_lanes((tm, 128), MLA_ROPE_HALF, 128, 64)
        cq = lat_ref[:, 0:Q_RANK]
        r1 = lax.rsqrt(jnp.mean(cq * cq, axis=-1, keepdims=True) + EPS)
        cqn = (cq * r1 * qg_ref[...]).astype(BF16)
        cqn_ref[...] = cqn
        q = _nn(cqn, wuq_ref[...])
        for h in range(MLA_HEADS):
            sl = slice(h * 128, (h + 1) * 128)
            q_ref[:, sl] = (_rope_fwd(q[:, sl], c, s, MLA_ROPE_HALF, lanes) * MLA_SCALE).astype(BF16)
        ckv = lat_ref[:, Q_RANK:Q_RANK + KV_RANK]
        r2 = lax.rsqrt(jnp.mean(ckv * ckv, axis=-1, keepdims=True) + EPS)
        ckvn = (ckv * r2 * kvg_ref[...]).astype(BF16)
        ckvn_ref[...] = ckvn
        krr = _rope_fwd(lat_ref[:, Q_RANK + KV_RANK:N_LAT], c, s, MLA_ROPE_HALF, lanes)
        kn = _nn(ckvn, wk_ref[...])
        for h in range(MLA_HEADS):
            sl = slice(h * 128, (h + 1) * 128)
            k_ref[:, sl] = (kn[:, sl] + krr).astype(BF16)
        v_ref[...] = _nn(ckvn, wv_ref[...]).astype(BF16)

    def full(shape):
        return pl.BlockSpec(shape, lambda i: (0, 0))

    def rows(w):
        return pl.BlockSpec((tm, w), lambda i: (i, 0))

    return pl.pallas_call(
        body, name="mla_prep_fwd", grid=(SEQ // tm,),
        in_specs=[pl.BlockSpec((tm, N_LAT), lambda i: (i, COL_LAT // N_LAT)),
                  full((1, Q_RANK)), full((1, KV_RANK)), full((Q_RANK, 1024)), full((KV_RANK, 1024)),
                  full((KV_RANK, 512)), rows(128), rows(128)],
        out_specs=[rows(Q_RANK), rows(KV_RANK), rows(1024), rows(1024), rows(512)],
        out_shape=[jax.ShapeDtypeStruct((SEQ, Q_RANK), BF16), jax.ShapeDtypeStruct((SEQ, KV_RANK), BF16),
                   jax.ShapeDtypeStruct((SEQ, 1024), BF16), jax.ShapeDtypeStruct((SEQ, 1024), BF16),
                   jax.ShapeDtypeStruct((SEQ, 512), BF16)],
        compiler_params=_cparams(),
    )(p, qg, kvg, wuq, wk, wv, rc, rs)


def _mla_prep_bwd(dp_in, p, dq, dk, dv, qg, kvg, wuq, wk, wv, rc, rs):
    tm = 512

    def body(dp_any, lat_ref, dq_ref, dk_ref, dv_ref, qg_ref, kvg_ref, wuq_ref, wk_ref, wv_ref,
             c_ref, s_ref, dp_ref, dqb_ref, dkb_ref, dvb_ref, dgq_ref, dgkv_ref):
        del dp_any
        c, s = c_ref[...], s_ref[...]
        lanes = _rope_lanes((tm, 128), MLA_ROPE_HALF, 128, 64)
        lane = lax.broadcasted_iota(jnp.int32, (tm, 128), 1)
        dkr = jnp.zeros((tm, 128), F32)
        for h in range(MLA_HEADS):
            sl = slice(h * 128, (h + 1) * 128)
            dqb_ref[:, sl] = _rope_bwd(dq_ref[:, sl] * MLA_SCALE, c, s, MLA_ROPE_HALF, lanes).astype(BF16)
            dkh = dk_ref[:, sl]
            dkr = dkr + dkh
            dkb_ref[:, sl] = jnp.where(lane < 64, dkh, 0.0).astype(BF16)
        dkr = jnp.where((lane >= 64) & (lane < 96), dkr, 0.0)
        dkr = _rope_bwd(dkr, c, s, MLA_ROPE_HALF, lanes)
        dvb = dv_ref[...].astype(BF16)
        dvb_ref[...] = dvb

        cq = lat_ref[:, 0:Q_RANK]
        r1 = lax.rsqrt(jnp.mean(cq * cq, axis=-1, keepdims=True) + EPS)
        n1 = cq * r1
        dcqn = _nt(dqb_ref[...], wuq_ref[...])
        dn1 = dcqn * qg_ref[...]
        dcq = r1 * (dn1 - n1 * jnp.mean(dn1 * n1, axis=-1, keepdims=True))
        pq = jnp.sum(dcqn * n1, axis=0, keepdims=True)

        ckv = lat_ref[:, Q_RANK:Q_RANK + KV_RANK]
        r2 = lax.rsqrt(jnp.mean(ckv * ckv, axis=-1, keepdims=True) + EPS)
        n2 = ckv * r2
        dckvn = _nt(dkb_ref[...], wk_ref[...]) + _nt(dvb, wv_ref[...])
        dn2 = dckvn * kvg_ref[...]
        dckv = r2 * (dn2 - n2 * jnp.mean(dn2 * n2, axis=-1, keepdims=True))
        pkv = jnp.sum(dckvn * n2, axis=0, keepdims=True)

        dp_ref[:, 0:Q_RANK] = dcq.astype(BF16)
        dp_ref[:, Q_RANK:Q_RANK + KV_RANK] = dckv.astype(BF16)
        dp_ref[:, Q_RANK + KV_RANK:N_LAT] = dkr.astype(BF16)

        @pl.when(pl.program_id(0) == 0)
        def _():
            dgq_ref[...] = pq
            dgkv_ref[...] = pkv

        @pl.when(pl.program_id(0) > 0)
        def _():
            dgq_ref[...] += pq
            dgkv_ref[...] += pkv

    def full(shape):
        return pl.BlockSpec(shape, lambda i: (0, 0))

    def rows(w):
        return pl.BlockSpec((tm, w), lambda i: (i, 0))

    lat = pl.BlockSpec((tm, N_LAT), lambda i: (i, COL_LAT // N_LAT))
    return pl.pallas_call(
        body, name="mla_prep_bwd", grid=(SEQ // tm,),
        in_specs=[pl.BlockSpec(memory_space=pl.ANY), lat, rows(1024), rows(1024), rows(512),
                  full((1, Q_RANK)), full((1, KV_RANK)), full((Q_RANK, 1024)), full((KV_RANK, 1024)),
                  full((KV_RANK, 512)), rows(128), rows(128)],
        out_specs=[lat, rows(1024), rows(1024), rows(512), full((1, Q_RANK)), full((1, KV_RANK))],
        out_shape=[jax.ShapeDtypeStruct((SEQ, N_PAD), BF16), jax.ShapeDtypeStruct((SEQ, 1024), BF16),
                   jax.ShapeDtypeStruct((SEQ, 1024), BF16), jax.ShapeDtypeStruct((SEQ, 512), BF16),
                   jax.ShapeDtypeStruct((1, Q_RANK), F32), jax.ShapeDtypeStruct((1, KV_RANK), F32)],
        input_output_aliases={0: 0},
        compiler_params=_cparams(),
    )(dp_in, p, dq, dk, dv, qg, kvg, wuq, wk, wv, rc, rs)


FLASH_T = 512


def _head_half(shape, hh):
    lane = lax.broadcasted_iota(jnp.int32, shape, 1)
    return (lane < 64) if hh == 0 else (lane >= 64)


def _causal_keep(t):
    row = lax.broadcasted_iota(jnp.int32, (t, t), 0)
    col = lax.broadcasted_iota(jnp.int32, (t, t), 1)
    return row >= col


def _tri_steps(nb, q_major):
    if q_major:
        pairs = [(i, kb) for i in range(nb) for kb in range(i + 1)]
    else:
        pairs = [(i, kb) for kb in range(nb) for i in range(kb, nb)]
    return jnp.asarray([p[0] for p in pairs], jnp.int32), jnp.asarray([p[1] for p in pairs], jnp.int32)


def _mla_flash_fwd(q, k, v):
    t = FLASH_T
    nb = SEQ // t
    qtab, ktab = _tri_steps(nb, True)

    def body(qi_ref, ki_ref, q_ref, k_ref, v_ref, o_ref, lse_ref, m_scr, l_scr, acc_scr):
        step = pl.program_id(1)
        i, kb = qi_ref[step], ki_ref[step]

        @pl.when(kb == 0)
        def _():
            m_scr[...] = jnp.full_like(m_scr, NEG)
            l_scr[...] = jnp.zeros_like(l_scr)
            acc_scr[...] = jnp.zeros_like(acc_scr)

        def update(masked):
            vv = v_ref[...]
            for hh in range(2):
                sl = slice(hh * 128, (hh + 1) * 128)
                s = _nt(q_ref[:, sl], k_ref[:, sl])
                if masked:
                    s = jnp.where(_causal_keep(t), s, NEG)
                m_prev = m_scr[hh]
                m_new = jnp.maximum(m_prev, jnp.max(s, axis=-1, keepdims=True))
                pr = jnp.exp(s - jnp.tile(m_new, (1, t // 128)))
                alpha = jnp.exp(m_prev - m_new)
                l_scr[hh] = alpha * l_scr[hh] + jnp.sum(pr, axis=-1, keepdims=True)
                acc_scr[hh] = alpha * acc_scr[hh] + _nn(pr.astype(BF16), vv)
                m_scr[hh] = m_new

        @pl.when(kb < i)
        def _():
            update(False)

        @pl.when(kb == i)
        def _():
            update(True)
            o0 = acc_scr[0] / l_scr[0]
            o1 = acc_scr[1] / l_scr[1]
            o_ref[...] = jnp.where(_head_half((t, 128), 0), o0, o1)
            for hh in range(2):
                lse_ref[:, hh * 128:(hh + 1) * 128] = m_scr[hh] + jnp.log(l_scr[hh])

    grid_spec = pltpu.PrefetchScalarGridSpec(
        num_scalar_prefetch=2, grid=(4, qtab.shape[0]),
        in_specs=[pl.BlockSpec((t, 256), lambda j, s, qi, ki: (qi[s], j)),
                  pl.BlockSpec((t, 256), lambda j, s, qi, ki: (ki[s], j)),
                  pl.BlockSpec((t, 128), lambda j, s, qi, ki: (ki[s], j))],
        out_specs=[pl.BlockSpec((t, 128), lambda j, s, qi, ki: (qi[s], j)),
                   pl.BlockSpec((t, 256), lambda j, s, qi, ki: (qi[s], j))],
        scratch_shapes=[pltpu.VMEM((2, t, 128), F32), pltpu.VMEM((2, t, 128), F32), pltpu.VMEM((2, t, 128), F32)])
    return pl.pallas_call(
        body, name="mla_flash_fwd", grid_spec=grid_spec,
        out_shape=[jax.ShapeDtypeStruct((SEQ, 512), F32), jax.ShapeDtypeStruct((SEQ, 1024), F32)],
        compiler_params=_cparams(),
    )(qtab, ktab, q, k, v)


def _mla_flash_bwd(q, k, v, o, do, lse):
    t = FLASH_T
    nb = SEQ // t
    qtab, ktab = _tri_steps(nb, False)

    def body(qi_ref, ki_ref, q_ref, k_ref, v_ref, o_ref, do_ref, lse_ref, dq_ref, dk_ref, dv_ref, dk_scr, dv_scr):
        step = pl.program_id(1)
        i, kb = qi_ref[step], ki_ref[step]

        @pl.when(step == 0)
        def _():
            dq_ref[...] = jnp.zeros_like(dq_ref)

        @pl.when(i == kb)
        def _():
            dk_scr[...] = jnp.zeros_like(dk_scr)
            dv_scr[...] = jnp.zeros_like(dv_scr)

        def update(masked):
            vv = v_ref[...]
            ov = o_ref[...]
            dov = do_ref[...]
            rows = pl.ds(pl.multiple_of(i * t, t), t)
            for hh in range(2):
                sl = slice(hh * 128, (hh + 1) * 128)
                qh, kh = q_ref[:, sl], k_ref[:, sl]
                s = _nt(qh, kh)
                if masked:
                    s = jnp.where(_causal_keep(t), s, NEG)
                pr = jnp.exp(s - jnp.tile(lse_ref[:, sl], (1, t // 128)))
                dom = jnp.where(_head_half((t, 128), hh), dov, 0.0)
                domb = dom.astype(BF16)
                dv_scr[...] += _tn(pr.astype(BF16), domb)
                dpr = _nt(domb, vv)
                delta = jnp.sum(dom * ov, axis=-1, keepdims=True)
                ds = (pr * (dpr - delta)).astype(BF16)
                dq_ref[rows, sl] += _nn(ds, kh)
                dk_scr[hh] += _tn(ds, qh)

        @pl.when(i > kb)
        def _():
            update(False)

        @pl.when(i == kb)
        def _():
            update(True)

        @pl.when(i == nb - 1)
        def _():
            dk_ref[:, 0:128] = dk_scr[0]
            dk_ref[:, 128:256] = dk_scr[1]
            dv_ref[...] = dv_scr[...]

    qi_map = lambda j, s, qi, ki: (qi[s], j)
    ki_map = lambda j, s, qi, ki: (ki[s], j)
    grid_spec = pltpu.PrefetchScalarGridSpec(
        num_scalar_prefetch=2, grid=(4, qtab.shape[0]),
        in_specs=[pl.BlockSpec((t, 256), qi_map), pl.BlockSpec((t, 256), ki_map), pl.BlockSpec((t, 128), ki_map),
                  pl.BlockSpec((t, 128), qi_map), pl.BlockSpec((t, 128), qi_map), pl.BlockSpec((t, 256), qi_map)],
        out_specs=[pl.BlockSpec((SEQ, 256), lambda j, s, qi, ki: (0, j)), pl.BlockSpec((t, 256), ki_map),
                   pl.BlockSpec((t, 128), ki_map)],
        scratch_shapes=[pltpu.VMEM((2, t, 128), F32), pltpu.VMEM((t, 128), F32)])
    return pl.pallas_call(
        body, name="mla_flash_bwd", grid_spec=grid_spec,
        out_shape=[jax.ShapeDtypeStruct((SEQ, 1024), F32), jax.ShapeDtypeStruct((SEQ, 1024), F32),
                   jax.ShapeDtypeStruct((SEQ, 512), F32)],
        compiler_params=_cparams(),
    )(qtab, ktab, q, k, v, o, do, lse)


DIL_UNROLL = 4


def _strided(start, size, d):
    return pl.ds(start, size) if d == 1 else pl.ds(start, size, stride=d)


def _dil_prep_fwd(p, rc, rs, g):
    d = DIL_DILATIONS[g]
    sub_len = SEQ // d
    ch = min(sub_len, 512)

    def body(p_ref, c_ref, s_ref, o_ref):
        tq = pl.program_id(0)
        is_v = tq == 2
        mult = jnp.where(tq == 0, DIL_SCALE, 1.0).astype(F32)
        lanes = _rope_lanes((ch, 128), DIL_ROPE_HALF, 64, 0)
        o_ref[0, 0:BAND, :] = jnp.zeros((BAND, 128), BF16)
        for r in range(d):
            for c0 in range(0, sub_len, ch):
                rows = _strided(r + c0 * d, ch, d)
                xv = p_ref[rows, :]
                roped = _rope_fwd(xv, c_ref[rows, :], s_ref[rows, :], DIL_ROPE_HALF, lanes)
                at = BAND + r * sub_len + c0
                o_ref[0, at:at + ch, :] = (jnp.where(is_v, xv, roped) * mult).astype(BF16)

    tab = pl.BlockSpec((SEQ, 128), lambda tq, pr: (0, 0))
    return pl.pallas_call(
        body, name=f"dil_prep_fwd_g{g}", grid=(3, 4),
        in_specs=[pl.BlockSpec((SEQ, 128), lambda tq, pr: (0, COL_QKV // 128 + (tq * 3 + g) * 4 + pr)), tab, tab],
        out_specs=pl.BlockSpec((1, BAND + SEQ, 128), lambda tq, pr: (tq, 0, pr)),
        out_shape=jax.ShapeDtypeStruct((3, BAND + SEQ, 512), BF16),
        compiler_params=_cparams(),
    )(p, rc, rs)


DIL_ST = 1024
DIL_NB = DIL_ST // BAND


def _band_keep(g, b, t):
    nbs = SEQ // DIL_DILATIONS[g] // BAND
    row = lax.broadcasted_iota(jnp.int32, (BAND, 2 * BAND), 0)
    col = lax.broadcasted_iota(jnp.int32, (BAND, 2 * BAND), 1)
    cur = (col >= BAND) & (row >= col - BAND)
    prev = (col < BAND) & (col >= row)
    if nbs >= DIL_NB:
        if b > 0:
            return cur | prev
        return cur | (prev & ((t * DIL_NB) % nbs != 0))
    return cur | prev if b % nbs else cur


def _dil_tok(g, b, t):
    d = DIL_DILATIONS[g]
    nbs = SEQ // d // BAND
    gb = t * DIL_NB + b
    return _strided((gb % nbs) * BAND * d + gb // nbs, BAND, d)


def _dil_attn_fwd2(qkv, g):
    def body(q_ref, k_ref, v_ref, o_ref, l_ref, s_scr, p_scr, o_scr):
        t = pl.program_id(1)
        base = t * DIL_ST
        half0 = _head_half((DIL_ST, 128), 0)
        lse_h = []
        for hh in range(2):
            half = _head_half((BAND, 128), hh)
            for b in range(DIL_NB):
                qv = q_ref[0, pl.ds(pl.multiple_of(base + (b + 1) * BAND, BAND), BAND), :]
                k2 = k_ref[0, pl.ds(pl.multiple_of(base + b * BAND, BAND), 2 * BAND), :]
                sb = _nt(jnp.where(half, qv, jnp.zeros_like(qv)), k2)
                s_scr[b * BAND:(b + 1) * BAND, :] = jnp.where(_band_keep(g, b, t), sb, NEG)
            s = s_scr[...]
            m = jnp.max(s, axis=-1, keepdims=True)
            pr = jnp.exp(s - m)
            den = jnp.sum(pr, axis=-1, keepdims=True)
            p_scr[...] = pr.astype(BF16)
            for b in range(DIL_NB):
                v2 = v_ref[0, pl.ds(pl.multiple_of(base + b * BAND, BAND), 2 * BAND), :]
                o_scr[hh, b * BAND:(b + 1) * BAND, :] = _nn(p_scr[b * BAND:(b + 1) * BAND, :], v2)
            o_scr[hh] = o_scr[hh] / den
            lse_h.append(m + jnp.log(den))
        out = jnp.where(half0, o_scr[0], o_scr[1])
        lse = jnp.where(half0, lse_h[0], lse_h[1])
        for b in range(DIL_NB):
            tok = _dil_tok(g, b, t)
            o_ref[tok, :] = out[b * BAND:(b + 1) * BAND, :]
            l_ref[tok, :] = lse[b * BAND:(b + 1) * BAND, :]

    def inp(tq):
        return pl.BlockSpec((1, BAND + SEQ, 128), lambda pr, t: (tq, 0, pr))

    out = pl.BlockSpec((SEQ, 128), lambda pr, t: (0, pr))
    return pl.pallas_call(
        body, name=f"dil_attn_fwd_g{g}", grid=(4, SEQ // DIL_ST),
        in_specs=[inp(0), inp(1), inp(2)], out_specs=[out, out],
        out_shape=[jax.ShapeDtypeStruct((SEQ, 512), F32), jax.ShapeDtypeStruct((SEQ, 512), F32)],
        scratch_shapes=[pltpu.VMEM((DIL_ST, 2 * BAND), F32), pltpu.VMEM((DIL_ST, 2 * BAND), BF16),
                        pltpu.VMEM((2, DIL_ST, 128), F32)],
        compiler_params=_cparams(),
    )(qkv, qkv, qkv)


def _dil_attn_bwd2(qkv, dyd, yd, lse_all, g, token=None):
    d = DIL_DILATIONS[g]
    sub_len = SEQ // d
    nst = SEQ // DIL_ST
    after, after_specs = _after(token)

    def body(q_ref, k_ref, v_ref, do_ref, y_ref, l_ref, *rest):
        out_ref, dk_scr, dv_scr, s_scr, dp_scr, p_scr, ds_scr, do_scr, y_scr, l_scr, dq_scr = rest[-11:]
        t = pl.program_id(1)
        base = t * DIL_ST

        @pl.when(t == 0)
        def _():
            dk_scr[...] = jnp.zeros_like(dk_scr)
            dv_scr[...] = jnp.zeros_like(dv_scr)

        for b in range(DIL_NB):
            tok = _dil_tok(g, b, t)
            do_scr[b * BAND:(b + 1) * BAND, :] = do_ref[tok, :]
            y_scr[b * BAND:(b + 1) * BAND, :] = y_ref[tok, :]
            l_scr[b * BAND:(b + 1) * BAND, :] = l_ref[tok, :]
        for hh in range(2):
            half = _head_half((BAND, 128), hh)
            half_st = _head_half((DIL_ST, 128), hh)
            dom = jnp.where(half_st, do_scr[...], 0.0)
            delta = jnp.sum(dom * y_scr[...], axis=-1, keepdims=True)
            lcol = jnp.max(jnp.where(half_st, l_scr[...], NEG), axis=-1, keepdims=True)
            for b in range(DIL_NB):
                rows = slice(b * BAND, (b + 1) * BAND)
                qv = q_ref[0, pl.ds(pl.multiple_of(base + (b + 1) * BAND, BAND), BAND), :]
                band = pl.ds(pl.multiple_of(base + b * BAND, BAND), 2 * BAND)
                sb = _nt(jnp.where(half, qv, jnp.zeros_like(qv)), k_ref[0, band, :])
                s_scr[rows, :] = jnp.where(_band_keep(g, b, t), sb, NEG)
                dp_scr[rows, :] = _nt(dom[rows, :].astype(BF16), v_ref[0, band, :])
            pr = jnp.exp(s_scr[...] - lcol)
            p_scr[...] = pr.astype(BF16)
            ds_scr[...] = (pr * (dp_scr[...] - delta)).astype(BF16)
            for b in range(DIL_NB):
                rows = slice(b * BAND, (b + 1) * BAND)
                qv = q_ref[0, pl.ds(pl.multiple_of(base + (b + 1) * BAND, BAND), BAND), :]
                band = pl.ds(pl.multiple_of(base + b * BAND, BAND), 2 * BAND)
                dqb = jnp.where(half, _nn(ds_scr[rows, :], k_ref[0, band, :]), 0.0)
                if hh == 0:
                    dq_scr[rows, :] = dqb
                else:
                    dq_scr[rows, :] += dqb
                half2 = _head_half((2 * BAND, 128), hh)
                dk_scr[band, :] += jnp.where(half2, _tn(ds_scr[rows, :], qv), 0.0)
                dv_scr[band, :] += _tn(p_scr[rows, :], dom[rows, :].astype(BF16))
        for b in range(DIL_NB):
            out_ref[pl.ds(0, 1), _dil_tok(g, b, t), :] = dq_scr[b * BAND:(b + 1) * BAND, :][None]

        @pl.when(t == nst - 1)
        def _():
            for r in range(d):
                rows = _strided(r, sub_len, d)
                out_ref[pl.ds(1, 1), rows, :] = dk_scr[BAND + r * sub_len:BAND + (r + 1) * sub_len, :][None]
                out_ref[pl.ds(2, 1), rows, :] = dv_scr[BAND + r * sub_len:BAND + (r + 1) * sub_len, :][None]

    def inp(tq):
        return pl.BlockSpec((1, BAND + SEQ, 128), lambda pr, t: (tq, 0, pr))

    tok_spec = pl.BlockSpec((SEQ, 128), lambda pr, t: (0, pr))
    st = (DIL_ST, 2 * BAND)
    return pl.pallas_call(
        body, name=f"dil_attn_bwd_g{g}", grid=(4, nst),
        in_specs=[inp(0), inp(1), inp(2), tok_spec, tok_spec, tok_spec] + after_specs,
        out_specs=pl.BlockSpec((3, SEQ, 128), lambda pr, t: (0, 0, pr)),
        out_shape=jax.ShapeDtypeStruct((3, SEQ, 512), F32),
        scratch_shapes=[pltpu.VMEM((BAND + SEQ, 128), F32), pltpu.VMEM((BAND + SEQ, 128), F32),
                        pltpu.VMEM(st, F32), pltpu.VMEM(st, F32), pltpu.VMEM(st, BF16), pltpu.VMEM(st, BF16),
                        pltpu.VMEM((DIL_ST, 128), F32), pltpu.VMEM((DIL_ST, 128), F32), pltpu.VMEM((DIL_ST, 128), F32),
                        pltpu.VMEM((DIL_ST, 128), F32)],
        compiler_params=_cparams(),
    )(qkv, qkv, qkv, dyd, yd, lse_all, *after)


def _band_masks():
    row = lax.broadcasted_iota(jnp.int32, (BAND, BAND), 0)
    col = lax.broadcasted_iota(jnp.int32, (BAND, BAND), 1)
    return row >= col, col >= row


def _dil_attn_fwd(qkv, g):
    d = DIL_DILATIONS[g]
    nbs = SEQ // d // BAND
    nblk = SEQ // BAND

    def body(q_ref, k_ref, v_ref, o_ref, l_ref):
        keep_c, keep_p = _band_masks()
        half0 = _head_half((BAND, 128), 0)

        def step(i, carry):
            cur = pl.ds(pl.multiple_of(i * BAND, BAND), BAND)
            prv = pl.ds(pl.multiple_of(jnp.maximum(i - 1, 0) * BAND, BAND), BAND)
            has_prev = (i % nbs) != 0
            qv = q_ref[0, cur, :]
            kc, kp = k_ref[0, cur, :], k_ref[0, prv, :]
            vc, vp = v_ref[0, cur, :], v_ref[0, prv, :]
            outs, lses = [], []
            for hh in range(2):
                qm = jnp.where(_head_half((BAND, 128), hh), qv, jnp.zeros_like(qv))
                sc = jnp.where(keep_c, _nt(qm, kc), NEG)
                sp = jnp.where(keep_p & has_prev, _nt(qm, kp), NEG)
                m = jnp.maximum(jnp.max(sc, axis=-1, keepdims=True), jnp.max(sp, axis=-1, keepdims=True))
                pc, pp = jnp.exp(sc - m), jnp.exp(sp - m)
                den = jnp.sum(pc, axis=-1, keepdims=True) + jnp.sum(pp, axis=-1, keepdims=True)
                o = (_nn(pc.astype(BF16), vc) + _nn(pp.astype(BF16), vp)) / den
                outs.append(o)
                lses.append(jnp.broadcast_to(m + jnp.log(den), (BAND, 128)))
            tok = _strided((i % nbs) * BAND * d + i // nbs, BAND, d)
            o_ref[tok, :] = jnp.where(half0, outs[0], outs[1])
            l_ref[tok, :] = jnp.where(half0, lses[0], lses[1])
            return carry

        lax.fori_loop(0, nblk, step, 0, unroll=DIL_UNROLL)

    def inp(tq):
        return pl.BlockSpec((1, SEQ, 128), lambda pr: (tq, 0, pr))

    out = pl.BlockSpec((SEQ, 128), lambda pr: (0, pr))
    return pl.pallas_call(
        body, name=f"dil_attn_fwd_g{g}", grid=(4,),
        in_specs=[inp(0), inp(1), inp(2)], out_specs=[out, out],
        out_shape=[jax.ShapeDtypeStruct((SEQ, 512), F32), jax.ShapeDtypeStruct((SEQ, 512), F32)],
        compiler_params=_cparams(),
    )(qkv, qkv, qkv)


def _dil_attn_bwd(qkv, dyd, yd, lse_all, g):
    d = DIL_DILATIONS[g]
    sub_len = SEQ // d
    nbs = sub_len // BAND
    nblk = SEQ // BAND

    def body(q_ref, k_ref, v_ref, do_ref, y_ref, l_ref, out_ref, dk_scr, dv_scr):
        keep_c, keep_p = _band_masks()
        dk_scr[...] = jnp.zeros_like(dk_scr)
        dv_scr[...] = jnp.zeros_like(dv_scr)

        def step(i, carry):
            cur = pl.ds(pl.multiple_of(i * BAND, BAND), BAND)
            prv = pl.ds(pl.multiple_of(jnp.maximum(i - 1, 0) * BAND, BAND), BAND)
            has_prev = (i % nbs) != 0
            tok = _strided((i % nbs) * BAND * d + i // nbs, BAND, d)
            qv = q_ref[0, cur, :]
            kc, kp = k_ref[0, cur, :], k_ref[0, prv, :]
            vc, vp = v_ref[0, cur, :], v_ref[0, prv, :]
            dov, yv, lv = do_ref[tok, :], y_ref[tok, :], l_ref[tok, :]
            dq = jnp.zeros((BAND, 128), F32)
            dkc = jnp.zeros((BAND, 128), F32)
            dkp = jnp.zeros((BAND, 128), F32)
            dvc = jnp.zeros((BAND, 128), F32)
            dvp = jnp.zeros((BAND, 128), F32)
            for hh in range(2):
                half = _head_half((BAND, 128), hh)
                qm = jnp.where(half, qv, jnp.zeros_like(qv))
                lcol = jnp.max(jnp.where(half, lv, NEG), axis=-1, keepdims=True)
                pc = jnp.exp(jnp.where(keep_c, _nt(qm, kc), NEG) - lcol)
                pp = jnp.exp(jnp.where(keep_p & has_prev, _nt(qm, kp), NEG) - lcol)
                dom = jnp.where(half, dov, 0.0)
                domb = dom.astype(BF16)
                delta = jnp.sum(dom * yv, axis=-1, keepdims=True)
                dsc = (pc * (_nt(domb, vc) - delta)).astype(BF16)
                dsp = (pp * (_nt(domb, vp) - delta)).astype(BF16)
                dvc = dvc + _tn(pc.astype(BF16), domb)
                dvp = dvp + _tn(pp.astype(BF16), domb)
                dq = dq + jnp.where(half, _nn(dsc, kc) + _nn(dsp, kp), 0.0)
                dkc = dkc + jnp.where(half, _tn(dsc, qv), 0.0)
                dkp = dkp + jnp.where(half, _tn(dsp, qv), 0.0)
            out_ref[pl.ds(0, 1), tok, :] = dq[None]
            dk_scr[cur, :] += dkc
            dk_scr[prv, :] += dkp
            dv_scr[cur, :] += dvc
            dv_scr[prv, :] += dvp
            return carry

        lax.fori_loop(0, nblk, step, 0, unroll=DIL_UNROLL)
        for r in range(d):
            rows = _strided(r, sub_len, d)
            out_ref[pl.ds(1, 1), rows, :] = dk_scr[r * sub_len:(r + 1) * sub_len, :][None]
            out_ref[pl.ds(2, 1), rows, :] = dv_scr[r * sub_len:(r + 1) * sub_len, :][None]

    def inp(tq):
        return pl.BlockSpec((1, SEQ, 128), lambda pr: (tq, 0, pr))

    tok_spec = pl.BlockSpec((SEQ, 128), lambda pr: (0, pr))
    return pl.pallas_call(
        body, name=f"dil_attn_bwd_g{g}", grid=(4,),
        in_specs=[inp(0), inp(1), inp(2), tok_spec, tok_spec, tok_spec],
        out_specs=pl.BlockSpec((3, SEQ, 128), lambda pr: (0, 0, pr)),
        out_shape=jax.ShapeDtypeStruct((3, SEQ, 512), F32),
        scratch_shapes=[pltpu.VMEM((SEQ, 128), F32), pltpu.VMEM((SEQ, 128), F32)],
        compiler_params=_cparams(),
    )(qkv, qkv, qkv, dyd, yd, lse_all)


def _dil_prep_bwd(dp_in, dqkv, rc, rs, g):
    tm = 512

    def body(dp_any, g_ref, c_ref, s_ref, dp_ref):
        del dp_any
        tq = pl.program_id(0)
        mult = jnp.where(tq == 0, DIL_SCALE, 1.0).astype(F32)
        lanes = _rope_lanes((tm, 128), DIL_ROPE_HALF, 64, 0)
        cv, sv = c_ref[...], s_ref[...]
        for pr in range(4):
            gv = g_ref[0, :, pr * 128:(pr + 1) * 128]
            roped = _rope_bwd(gv, cv, sv, DIL_ROPE_HALF, lanes)
            dp_ref[:, pr * 128:(pr + 1) * 128] = (jnp.where(tq == 2, gv, roped) * mult).astype(BF16)

    tab = pl.BlockSpec((tm, 128), lambda tq, i: (i, 0))
    return pl.pallas_call(
        body, name=f"dil_prep_bwd_g{g}", grid=(3, SEQ // tm),
        in_specs=[pl.BlockSpec(memory_space=pl.ANY),
                  pl.BlockSpec((1, tm, 512), lambda tq, i: (tq, i, 0)), tab, tab],
        out_specs=pl.BlockSpec((tm, 512), lambda tq, i: (i, COL_QKV // 512 + tq * 3 + g)),
        out_shape=jax.ShapeDtypeStruct((SEQ, N_PAD), BF16),
        input_output_aliases={0: 0},
    )(dp_in, dqkv, rc, rs)


TAIL_T = 256


def _tail(p, ya, o_g, l_g, x, target, wpm, wpd, wout, post_g):
    tm = TAIL_T

    def body(pgz_ref, ya_ref, o0_ref, o1_ref, o2_ref, l0_ref, l1_ref, l2_ref, x_ref, t_ref,
             wpm_ref, wpd_ref, wout_ref, pg_ref,
             dp_ref, dy_ref, mg_ref, dt_ref, ua_ref, dpa_ref, ud_ref, dpd_ref, dya_ref, dyd_ref,
             yd_ref, lse_ref, loss_ref, dgp_ref):
        l0, l1, l2 = l0_ref[...], l1_ref[...], l2_ref[...]
        mx = jnp.maximum(jnp.maximum(l0, l1), l2)
        e0, e1, e2 = jnp.exp(l0 - mx), jnp.exp(l1 - mx), jnp.exp(l2 - mx)
        den = e0 + e1 + e2
        yd = (e0 * o0_ref[...] + e1 * o1_ref[...] + e2 * o2_ref[...]) / den
        yd_ref[...] = yd
        lse_ref[...] = mx + jnp.log(den)
        ya = ya_ref[...]

        gm, gd = pgz_ref[:, 0:1024], pgz_ref[:, 1024:2048]
        zm, zd = pgz_ref[:, 2048:2560], pgz_ref[:, 2560:3072]
        szm, szd = _sigmoid(zm), _sigmoid(zd)
        sm, sd = zm * szm, zd * szd
        ua = (ya * sm).astype(BF16)
        ud = (yd * sd).astype(BF16)
        ua_ref[...] = ua
        ud_ref[...] = ud
        pa = _nn(ua, wpm_ref[...])
        pd = _nn(ud, wpd_ref[...])
        sgm, sgd = _sigmoid(gm), _sigmoid(gd)
        mg = (sgm * pa + sgd * pd).astype(BF16)
        mg_ref[...] = mg
        t = _nn(mg, wout_ref[...])
        r3 = lax.rsqrt(jnp.mean(t * t, axis=-1, keepdims=True) + EPS)
        n = t * r3
        pg = pg_ref[...]
        err = x_ref[...] + n * pg - t_ref[...]
        lpart = jnp.sum(err * err, axis=0, keepdims=True)

        dy = err * (1.0 / D_MODEL)
        dy_ref[...] = dy
        gpart = jnp.sum(dy * n, axis=0, keepdims=True)
        dn = dy * pg
        dt = (r3 * (dn - n * jnp.mean(dn * n, axis=-1, keepdims=True))).astype(BF16)
        dt_ref[...] = dt
        dmg = _nt(dt, wout_ref[...])
        dpa = (dmg * sgm).astype(BF16)
        dpd = (dmg * sgd).astype(BF16)
        dpa_ref[...] = dpa
        dpd_ref[...] = dpd
        dp_ref[:, 0:1024] = (dmg * pa * sgm * (1.0 - sgm)).astype(BF16)
        dp_ref[:, 1024:2048] = (dmg * pd * sgd * (1.0 - sgd)).astype(BF16)
        dua = _nt(dpa, wpm_ref[...])
        dud = _nt(dpd, wpd_ref[...])
        dya_ref[...] = dua * sm
        dyd_ref[...] = dud * sd
        dp_ref[:, 2048:2560] = (dua * ya * szm * (1.0 + zm * (1.0 - szm))).astype(BF16)
        dp_ref[:, 2560:3072] = (dud * yd * szd * (1.0 + zd * (1.0 - szd))).astype(BF16)

        @pl.when(pl.program_id(0) == 0)
        def _():
            loss_ref[...] = lpart
            dgp_ref[...] = gpart

        @pl.when(pl.program_id(0) > 0)
        def _():
            loss_ref[...] += lpart
            dgp_ref[...] += gpart

    def rows(w):
        return pl.BlockSpec((tm, w), lambda i: (i, 0))

    def full(shape):
        return pl.BlockSpec(shape, lambda i: (0, 0))

    def sds(w, dt):
        return jax.ShapeDtypeStruct((SEQ, w), dt)

    return pl.pallas_call(
        body, name="tail", grid=(SEQ // tm,),
        in_specs=[rows(3072), rows(512), rows(512), rows(512), rows(512), rows(512), rows(512), rows(512),
                  rows(1024), rows(1024), full((512, 1024)), full((512, 1024)), full((1024, 1024)), full((1, 1024))],
        out_specs=[rows(3072), rows(1024), rows(1024), rows(1024), rows(512), rows(1024), rows(512), rows(1024),
                   rows(512), rows(512), rows(512), rows(512), full((1, 1024)), full((1, 1024))],
        out_shape=[sds(N_PAD, BF16), sds(1024, F32), sds(1024, BF16), sds(1024, BF16), sds(512, BF16),
                   sds(1024, BF16), sds(512, BF16), sds(1024, BF16), sds(512, F32), sds(512, F32),
                   sds(512, F32), sds(512, F32),
                   jax.ShapeDtypeStruct((1, 1024), F32), jax.ShapeDtypeStruct((1, 1024), F32)],
        compiler_params=_cparams(),
    )(p, ya, o_g[0], o_g[1], o_g[2], l_g[0], l_g[1], l_g[2], x, target, wpm, wpd, wout, post_g)


def _sum_parts(parts, tr, name):
    n, r, w = parts.shape

    def body(p_ref, o_ref):
        acc = p_ref[0].astype(F32)
        for s in range(1, n):
            acc = acc + p_ref[s].astype(F32)
        o_ref[...] = acc

    return pl.pallas_call(
        body, name=name, grid=(r // tr,),
        in_specs=[pl.BlockSpec((n, tr, w), lambda i: (0, i, 0))],
        out_specs=pl.BlockSpec((tr, w), lambda i: (i, 0)),
        out_shape=jax.ShapeDtypeStruct((r, w), F32),
    )(parts)


def _adamw(w, g, m, v, name):
    r, c = w.shape
    tr = 128 if r % 128 == 0 else r
    c1 = 1.0 - ADAM_B1 ** ADAM_STEP
    c2 = 1.0 - ADAM_B2 ** ADAM_STEP

    def body(w_ref, g_ref, m_ref, v_ref, d_ref, nm_ref, nv_ref):
        gv = g_ref[...]
        nm = ADAM_B1 * m_ref[...] + (1.0 - ADAM_B1) * gv
        nv = ADAM_B2 * v_ref[...] + (1.0 - ADAM_B2) * (gv * gv)
        nm_ref[...] = nm
        nv_ref[...] = nv
        d_ref[...] = -ADAM_LR * ((nm / c1) / (jnp.sqrt(nv / c2) + ADAM_EPS) + ADAM_WD * w_ref[...])

    spec = pl.BlockSpec((tr, c), lambda i: (i, 0))
    sd = jax.ShapeDtypeStruct((r, c), F32)
    return pl.pallas_call(
        body, name=name, grid=(r // tr,),
        in_specs=[spec] * 4, out_specs=[spec] * 3, out_shape=[sd] * 3,
    )(w, g, m, v)


ANY = pl.BlockSpec(memory_space=pl.ANY)


def _my_place():
    return lax.axis_index("x"), lax.axis_index("y"), lax.axis_index("c")


def _allgather_weights(mats):
    def body(*refs):
        w_refs, out_refs = refs[:N_MATS], refs[N_MATS:2 * N_MATS]
        send_sems, recv_sems = refs[2 * N_MATS:]
        x, y, c = _my_place()
        sibling = (x, y, 1 - c)
        chips = [(1 - x, y), (x, 1 - y), (1 - x, 1 - y)]

        def copy(k, src, dst, to):
            return pltpu.make_async_remote_copy(src_ref=src, dst_ref=dst, send_sem=send_sems.at[k],
                                                recv_sem=recv_sems.at[k], device_id=to, device_id_type=MESH)

        def half(mi, shard, hc):
            hr = SHARD_SHAPES[mi][0] // 2
            return out_refs[mi].at[shard, pl.ds(pl.multiple_of(hc * hr, 16), hr), :]

        started = []
        for mi in range(N_MATS):
            hr = SHARD_SHAPES[mi][0] // 2
            my_half = w_refs[mi].at[pl.ds(pl.multiple_of(c * hr, 16), hr), :]
            for j, (cx, cy) in enumerate(chips):
                cp = copy(mi * 6 + j, my_half, half(mi, 2 * x + y, c), (cx, cy, c))
                cp.start()
                started.append(cp)
        for mi in range(N_MATS):
            for j, (cx, cy) in enumerate(chips):
                landed = half(mi, 2 * cx + cy, c)
                copy(mi * 6 + j, landed, landed, (cx, cy, c)).wait_recv()
                fw = copy(mi * 6 + 3 + j, landed, landed, sibling)
                fw.start()
                started.append(fw)
        for mi in range(N_MATS):
            for j, (cx, cy) in enumerate(chips):
                other = half(mi, 2 * cx + cy, 1 - c)
                copy(mi * 6 + 3 + j, other, other, sibling).wait_recv()
        for cp in started:
            cp.wait_send()

    return pl.pallas_call(
        body, name="allgather_weights",
        in_specs=[ANY] * N_MATS, out_specs=[ANY] * N_MATS,
        out_shape=[jax.ShapeDtypeStruct((4, r, c), BF16) for r, c in SHARD_SHAPES],
        scratch_shapes=[pltpu.SemaphoreType.DMA((6 * N_MATS,)), pltpu.SemaphoreType.DMA((6 * N_MATS,))],
    )(*mats)


HBM = pl.BlockSpec(memory_space=pltpu.HBM)
SEM = pl.BlockSpec(memory_space=pltpu.SEMAPHORE)
DATAFLOW = pltpu.SideEffectType.DATAFLOW_SIDE_EFFECTING


def _peers(x, y, c):
    out = []
    for k in range(1, 8):
        px, py, pc = x ^ (k >> 2), y ^ ((k >> 1) & 1), c ^ (k & 1)
        out.append((k - 1, (px, py, pc), 4 * px + 2 * py + pc))
    return out


def _exchange_start(parts, name):
    n = len(parts)

    def body(*refs):
        p_refs, land_refs = refs[:n], refs[n:2 * n]
        send_sems, recv_sems, token = refs[2 * n], refs[2 * n + 1], refs[-1]
        x, y, c = _my_place()
        me = 4 * x + 2 * y + c
        for k, dev, peer in _peers(x, y, c):
            for mi in range(n):
                pltpu.make_async_remote_copy(
                    src_ref=p_refs[mi].at[peer], dst_ref=land_refs[mi].at[me], send_sem=send_sems.at[k * n + mi],
                    recv_sem=recv_sems.at[k * n + mi], device_id=dev, device_id_type=MESH).start()
        token[...] = jnp.zeros_like(token)

    hbm = [pltpu.HBM(p.shape, p.dtype) for p in parts]
    outs = pl.pallas_call(
        body, name=name + "_start",
        out_shape=(pltpu.SemaphoreType.DMA((7 * n,)), pltpu.SemaphoreType.DMA((7 * n,)), *hbm, *hbm,
                   jax.ShapeDtypeStruct((8, 128), F32)),
        in_specs=[HBM] * (2 * n), out_specs=(SEM, SEM, *[HBM] * (2 * n), pl.BlockSpec(memory_space=pltpu.VMEM)),
        input_output_aliases={i: 2 + i for i in range(2 * n)},
        compiler_params=pltpu.CompilerParams(has_side_effects=DATAFLOW),
    )(*[pltpu.with_memory_space_constraint(p, pltpu.HBM) for p in parts],
      *[pltpu.with_memory_space_constraint(lax.empty(p.shape, p.dtype), pltpu.HBM) for p in parts])
    return (name, outs[:-1]), outs[-1]


def _exchange_wait(handle, after):
    name, outs = handle
    n = (len(outs) - 2) // 2

    def body(*refs):
        p_refs, land_refs = refs[:n], refs[n:2 * n]
        send_sems, recv_sems = refs[2 * n], refs[2 * n + 1]
        x, y, c = _my_place()
        me = 4 * x + 2 * y + c
        for k, dev, peer in _peers(x, y, c):
            for mi in range(n):
                pltpu.make_async_remote_copy(
                    src_ref=p_refs[mi].at[peer], dst_ref=land_refs[mi].at[me], send_sem=send_sems.at[k * n + mi],
                    recv_sem=recv_sems.at[k * n + mi], device_id=dev, device_id_type=MESH).wait_send()
                slot = land_refs[mi].at[peer]
                pltpu.make_async_remote_copy(
                    src_ref=slot, dst_ref=slot, send_sem=send_sems.at[k * n + mi],
                    recv_sem=recv_sems.at[k * n + mi], device_id=dev, device_id_type=MESH).wait_recv()

    bufs = outs[2:]
    res = pl.pallas_call(
        body, name=name + "_wait", out_shape=tuple(pltpu.HBM(b.shape, b.dtype) for b in bufs),
        in_specs=[HBM] * (2 * n) + [SEM, SEM, ANY], out_specs=tuple([HBM] * (2 * n)),
        input_output_aliases={i: i for i in range(2 * n)},
        compiler_params=pltpu.CompilerParams(has_side_effects=DATAFLOW),
    )(*bufs, outs[0], outs[1], after)
    return list(res[n:])


def _swap_halves(halves, gvec):
    def body(*refs):
        g_refs, gv_ref = refs[:N_MATS], refs[N_MATS]
        out_refs, rg_ref = refs[N_MATS + 1:2 * N_MATS + 1], refs[2 * N_MATS + 1]
        send_sems, recv_sems = refs[2 * N_MATS + 2:]
        x, y, c = _my_place()
        me = 4 * x + 2 * y + c
        sends = []
        for mi in range(N_MATS):
            cp = pltpu.make_async_remote_copy(src_ref=g_refs[mi], dst_ref=out_refs[mi].at[c], send_sem=send_sems.at[mi],
                                              recv_sem=recv_sems.at[mi], device_id=(x, y, 1 - c), device_id_type=MESH)
            cp.start()
            sends.append(cp)
        for k, dev, peer in _peers(x, y, c):
            cp = pltpu.make_async_remote_copy(src_ref=gv_ref, dst_ref=rg_ref.at[me], send_sem=send_sems.at[N_MATS + k],
                                              recv_sem=recv_sems.at[N_MATS + k], device_id=dev, device_id_type=MESH)
            cp.start()
            sends.append(cp)
        for mi in range(N_MATS):
            got = out_refs[mi].at[1 - c]
            pltpu.make_async_remote_copy(src_ref=got, dst_ref=got, send_sem=send_sems.at[mi], recv_sem=recv_sems.at[mi],
                                         device_id=(x, y, 1 - c), device_id_type=MESH).wait_recv()
        for k, dev, peer in _peers(x, y, c):
            got = rg_ref.at[peer]
            pltpu.make_async_remote_copy(src_ref=got, dst_ref=got, send_sem=send_sems.at[N_MATS + k],
                                         recv_sem=recv_sems.at[N_MATS + k], device_id=dev, device_id_type=MESH).wait_recv()
        for cp in sends:
            cp.wait_send()

    outs = pl.pallas_call(
        body, name="swap_halves",
        in_specs=[ANY] * (N_MATS + 1), out_specs=[ANY] * (N_MATS + 1),
        out_shape=[jax.ShapeDtypeStruct((2, r // 2, c), F32) for r, c in SHARD_SHAPES]
        + [jax.ShapeDtypeStruct((8, 8, N_GAINS), F32)],
        scratch_shapes=[pltpu.SemaphoreType.DMA((N_MATS + 7,)), pltpu.SemaphoreType.DMA((N_MATS + 7,))],
    )(*halves, gvec)
    return outs[:N_MATS], outs[N_MATS]


def _set_slot(arr, block, idx):
    return lax.dynamic_update_slice(arr, block[None], (idx,) + (0,) * block.ndim)


PAD_RUNS = ((6304, 8352, 0), (5280, 6304, COL_Z), (672, 5280, COL_QKV), (0, 640, COL_LAT), (640, 672, COL_LAT + 704))
W_IN_SHARD = 2088


def _full_weights(gathered):
    def cols(a):
        return jnp.concatenate([a[s] for s in range(4)], axis=1)

    w_uq, w_ukv, w_pm, w_pd = [cols(a) for a in gathered[1:5]]
    w_out = gathered[5].reshape(D_MODEL, D_MODEL)
    g_in = gathered[0]
    pieces, at = [], 0
    for lo, hi, pad_lo in sorted(PAD_RUNS, key=lambda t: t[2]):
        if pad_lo > at:
            pieces.append(jnp.zeros((D_MODEL, pad_lo - at), g_in.dtype))
        for s in range(4):
            a_, b_ = max(lo, s * W_IN_SHARD), min(hi, (s + 1) * W_IN_SHARD)
            if a_ < b_:
                pieces.append(g_in[s][:, a_ - s * W_IN_SHARD:b_ - s * W_IN_SHARD])
        at = pad_lo + hi - lo
    pieces.append(jnp.zeros((D_MODEL, N_PAD - at), g_in.dtype))
    w_pad = jnp.concatenate(pieces, axis=1)
    z32 = jnp.zeros((Q_RANK, 32), w_uq.dtype)
    wuq_pad = jnp.concatenate([t for h in range(MLA_HEADS) for t in (w_uq[:, h * 96:(h + 1) * 96], z32)], axis=1)
    z64 = jnp.zeros((KV_RANK, 64), w_ukv.dtype)
    wk_pad = jnp.concatenate([t for h in range(MLA_HEADS) for t in (w_ukv[:, h * 128:h * 128 + 64], z64)], axis=1)
    wv = jnp.concatenate([w_ukv[:, h * 128 + 64:(h + 1) * 128] for h in range(MLA_HEADS)], axis=1)
    return w_pad, wuq_pad, wk_pad, wv, w_pm, w_pd, w_out


def _grad_parts_in(dw_pad):
    def in_block(s, h):
        rows = slice(h * 512, (h + 1) * 512)
        out = []
        for lo, hi, pad_lo in sorted(PAD_RUNS):
            a_, b_ = max(lo, s * W_IN_SHARD), min(hi, (s + 1) * W_IN_SHARD)
            if a_ < b_:
                out.append(dw_pad[rows, pad_lo + a_ - lo:pad_lo + b_ - lo])
        return jnp.concatenate(out, axis=1).astype(BF16)

    return jnp.stack([in_block(s, h) for s in range(4) for h in range(2)])


def _grad_parts_small(dwuq_pad, dwk_pad, dwv, dwpm, dwpd, dwout):
    d_uq = jnp.concatenate([dwuq_pad[:, h * 128:h * 128 + 96] for h in range(MLA_HEADS)], axis=1)
    d_ukv = jnp.concatenate([t for h in range(MLA_HEADS) for t in (dwk_pad[:, h * 128:h * 128 + 64], dwv[:, h * 64:(h + 1) * 64])],
                            axis=1)

    def col_blocks(m):
        r, c = m.shape[0] // 2, m.shape[1] // 4
        return jnp.stack([m[h * r:(h + 1) * r, s * c:(s + 1) * c].astype(BF16) for s in range(4) for h in range(2)])

    return [col_blocks(d_uq), col_blocks(d_ukv), col_blocks(dwpm), col_blocks(dwpd),
            dwout.astype(BF16).reshape(8, 128, D_MODEL)]


def _rope_tables(positions):
    pos = positions.reshape(SEQ).astype(F32)
    lane = jnp.arange(128)

    def table(rot, first, period):
        inv = ROPE_THETA ** (-jnp.arange(0, rot, 2, dtype=F32) / rot)
        half = rot // 2
        off = lane % period - first
        in1, in2 = (off >= 0) & (off < half), (off >= half) & (off < rot)
        inv_lane = jnp.where(in1 | in2, inv[jnp.clip(off % half, 0, half - 1)], 0.0)
        sign = jnp.where(in1, -1.0, 1.0).astype(F32)
        ang = pos[:, None] * inv_lane[None, :]
        return jnp.cos(ang), jnp.sin(ang) * sign[None, :]

    return table(32, 64, 128), table(16, 0, 64)


def _device_grads(x, positions, target, gains, gathered, send):
    pre_g, q_g, kv_g, post_g = gains
    w_pad, wuq_pad, wk_pad, wv, w_pm, w_pd, w_out = _full_weights(gathered)
    (mc, ms), (dc, ds) = _rope_tables(positions)

    h = _prenorm_fwd(x, pre_g)
    p = _matmul(h, w_pad, "nn", F32, 1024, 1408, 1024, "in_proj")
    cqn, ckvn, q, k, v = _mla_prep_fwd(p, q_g, kv_g, wuq_pad, wk_pad, wv, mc, ms)
    ya, lse_m = _mla_flash_fwd(q, k, v)
    qkv = [_dil_prep_fwd(p, dc, ds, g) for g in range(3)]
    o_g, l_g = zip(*[_dil_attn_fwd2(qkv[g], g) for g in range(3)])
    (dp, dy, mg, dt, ua, dpa, ud, dpd, dya, dyd, yd, lse_d, loss_cols, dg_post) = _tail(
        p, ya, o_g, l_g, x, target, w_pm, w_pd, w_out, post_g)

    dq, dk, dv = _mla_flash_bwd(q, k, v, ya, dya, lse_m)
    dp, dqb, dkb, dvb, dg_q, dg_kv = _mla_prep_bwd(dp, p, dq, dk, dv, q_g, kv_g, wuq_pad, wk_pad, wv, mc, ms)
    dwuq_pad = _matmul(cqn, dqb, "tn", F32, Q_RANK, 1024, 512, "dw_uq")
    dwk_pad = _matmul(ckvn, dkb, "tn", F32, KV_RANK, 1024, 512, "dw_k")
    dwv = _matmul(ckvn, dvb, "tn", F32, KV_RANK, 512, 512, "dw_v")
    dwpm = _matmul(ua, dpa, "tn", F32, 512, 1024, 512, "dw_proj_mla")
    dwpd = _matmul(ud, dpd, "tn", F32, 512, 1024, 512, "dw_proj_dil")
    dwout = _matmul(mg, dt, "tn", F32, 1024, 1024, 512, "dw_out")
    small, token = send(_grad_parts_small(dwuq_pad, dwk_pad, dwv, dwpm, dwpd, dwout), "exchange_small")
    for g in range(3):
        dqkv = _dil_attn_bwd2(qkv[g], dyd, yd, lse_d, g, token if g == 0 else None)
        dp = _dil_prep_bwd(dp, dqkv, dc, ds, g)

    dw_pad = _matmul(h, dp, "tn", F32, 1024, 1408, 2048, "dw_in")
    big, token = send([_grad_parts_in(dw_pad)], "exchange_in")
    dh = _matmul(dp, w_pad, "nt", F32, 1024, 1024, 1408, "dh", token)
    grad_x, dg_pre = _prenorm_bwd(x, dh, dy, pre_g)

    gvec = jnp.concatenate([dg_pre, dg_q, dg_kv, dg_post], axis=1)
    return loss_cols, grad_x, (big, small), gvec


def kernel(x, positions, pre_norm_g, w_in, q_norm_g, w_uq, kv_norm_g, w_ukv, w_proj_mla, w_proj_dil, w_out, post_norm_g, loss_target, m_pre_norm_g, m_w_in, m_q_norm_g, m_w_uq, m_kv_norm_g, m_w_ukv, m_w_proj_mla, m_w_proj_dil, m_w_out, m_post_norm_g, v_pre_norm_g, v_w_in, v_q_norm_g, v_w_uq, v_kv_norm_g, v_w_ukv, v_w_proj_mla, v_w_proj_dil, v_w_out, v_post_norm_g):
    xi, yi, ci = _my_place()
    chip, me = 2 * xi + yi, 4 * xi + 2 * yi + ci
    mats = [w.reshape(w.shape[1:]).astype(BF16) for w in (w_in, w_uq, w_ukv, w_proj_mla, w_proj_dil, w_out)]
    gathered = [_set_slot(g, m, chip) for g, m in zip(_allgather_weights(mats), mats)]
    gains = (pre_norm_g, q_norm_g, kv_norm_g, post_norm_g)
    sent = {}

    def send(blocks, name):
        sent[name] = blocks
        return _exchange_start(blocks, name)

    loss_cols, grad_x, (big, small), gvec = _device_grads(x[0], positions, loss_target[0], gains, gathered, send)

    loss = lax.psum(jnp.sum(loss_cols) * (0.5 / D_MODEL), ("x", "y", "c"))

    recv = _exchange_wait(big, grad_x) + _exchange_wait(small, grad_x)
    own = [lax.dynamic_index_in_dim(p, me, 0, keepdims=False) for p in sent["exchange_in"] + sent["exchange_small"]]
    recv = [_set_slot(r, o, me) for r, o in zip(recv, own)]
    halves = [_sum_parts(recv[mi], 64, f"sum_grad_{mi}") for mi in range(N_MATS)]
    gvec8 = jnp.pad(gvec, ((0, 7), (0, 0)))
    swapped, recv_gains = _swap_halves(halves, gvec8)
    g_gains = _sum_parts(_set_slot(recv_gains, gvec8, me), 8, "sum_gain_parts")[0:1]
    g_mats = [_set_slot(s, hf, ci).reshape((1,) + shp) for s, hf, shp in zip(swapped, halves, SHARD_SHAPES)]

    off = [0, 1024, 1408, 1664, 2688]
    g_gain = [g_gains[:, off[i]:off[i + 1]] for i in range(4)]
    grads = [g_gain[0], g_mats[0], g_gain[1], g_mats[1], g_gain[2], g_mats[2], g_mats[3], g_mats[4], g_mats[5], g_gain[3]]
    ws = [pre_norm_g, w_in, q_norm_g, w_uq, kv_norm_g, w_ukv, w_proj_mla, w_proj_dil, w_out, post_norm_g]
    ms = [m_pre_norm_g, m_w_in, m_q_norm_g, m_w_uq, m_kv_norm_g, m_w_ukv, m_w_proj_mla, m_w_proj_dil, m_w_out, m_post_norm_g]
    vs = [v_pre_norm_g, v_w_in, v_q_norm_g, v_w_uq, v_kv_norm_g, v_w_ukv, v_w_proj_mla, v_w_proj_dil, v_w_out, v_post_norm_g]
    deltas, new_m, new_v = [], [], []
    for i, (w, g, m, v) in enumerate(zip(ws, grads, ms, vs)):
        shp = w.shape
        two_d = shp[-2:]
        d_, m_, v_ = _adamw(w.reshape(two_d), g.reshape(two_d), m.reshape(two_d), v.reshape(two_d), f"adamw_{i}")
        deltas.append(d_.reshape(shp))
        new_m.append(m_.reshape(shp))
        new_v.append(v_.reshape(shp))
    return (loss, grad_x.reshape(x.shape), *grads, *deltas, *new_m, *new_v)
```

```python
import jax
import jax.numpy as jnp
from jax import lax
from jax.experimental import pallas as pl
from jax.experimental.pallas import tpu as pltpu

F32 = jnp.float32
BF16 = jnp.bfloat16

SEQ = 4096
D_MODEL = 1024
EPS = 1e-6
ROPE_THETA = 500000.0
MLA_HEADS = 8
Q_RANK = 384
KV_RANK = 256
MLA_SCALE = 96.0 ** -0.5
MLA_ROPE_HALF = 16
DIL_DILATIONS = (1, 4, 16)
DIL_ROPE_HALF = 8
DIL_SCALE = 0.125
BAND = 128

N_LAT = 768
COL_Z, COL_QKV, COL_LAT = 2048, 3072, 7680
N_PAD = 8448
IN_SPLITS = (384, 256, 32, 4608, 512, 512, 1024, 1024)

SHARD_SHAPES = ((1024, 2088), (384, 192), (256, 256), (512, 256), (512, 256), (256, 1024))
N_MATS = len(SHARD_SHAPES)
N_GAINS = 2688

ADAM_LR, ADAM_B1, ADAM_B2, ADAM_EPS, ADAM_WD, ADAM_STEP = 0.001, 0.9, 0.999, 1e-08, 0.01, 10

VMEM_LIMIT = 56 * 1024 * 1024
NEG = -1e30
MESH = pl.DeviceIdType.MESH


def _cparams(**kw):
    return pltpu.CompilerParams(vmem_limit_bytes=VMEM_LIMIT, **kw)


def _dot(a, b, dims):
    return lax.dot_general(a, b, (dims, ((), ())), preferred_element_type=F32)


def _nn(a, b):
    return _dot(a, b, ((1,), (0,)))


def _nt(a, b):
    return _dot(a, b, ((1,), (1,)))


def _tn(a, b):
    return _dot(a, b, ((0,), (0,)))


def _rope_lanes(shape, half, period, first):
    lane = lax.broadcasted_iota(jnp.int32, shape, len(shape) - 1) % period
    return (lane >= first) & (lane < first + half), (lane >= first + half) & (lane < first + 2 * half)


def _rope_fwd(x, c, s, half, lanes):
    x1, _ = lanes
    return x * c + jnp.where(x1, pltpu.roll(x, 128 - half, 1), pltpu.roll(x, half, 1)) * s


def _rope_bwd(g, c, s, half, lanes):
    x1, x2 = lanes
    gs = g * s
    return g * c + jnp.where(x2, pltpu.roll(gs, half, 1), jnp.where(x1, pltpu.roll(gs, 128 - half, 1), 0.0))


def _sigmoid(x):
    return 1.0 / (1.0 + jnp.exp(-x))


def _after(token):
    return ([], []) if token is None else ([token], [pl.BlockSpec(memory_space=pl.ANY)])


def _matmul(a, b, mode, out_dtype, tm, tn, tk, name, token=None):
    after, after_specs = _after(token)
    if mode == "nn":
        (m, k), n = a.shape, b.shape[1]
        a_spec = pl.BlockSpec((tm, tk), lambda j, i, kk: (i, kk))
        b_spec = pl.BlockSpec((tk, tn), lambda j, i, kk: (kk, j))
        dot = _nn
    elif mode == "nt":
        (m, k), n = a.shape, b.shape[0]
        a_spec = pl.BlockSpec((tm, tk), lambda j, i, kk: (i, kk))
        b_spec = pl.BlockSpec((tn, tk), lambda j, i, kk: (j, kk))
        dot = _nt
    else:
        (k, m), n = a.shape, b.shape[1]
        a_spec = pl.BlockSpec((tk, tm), lambda j, i, kk: (kk, i))
        b_spec = pl.BlockSpec((tk, tn), lambda j, i, kk: (kk, j))
        dot = _tn
    assert m % tm == 0 and n % tn == 0 and k % tk == 0, (name, m, n, k, tm, tn, tk)
    nk = k // tk

    def body(a_ref, b_ref, *rest):
        o_ref, acc_ref = rest[-2:]
        kk = pl.program_id(2)
        part = dot(a_ref[...], b_ref[...])

        @pl.when(kk == 0)
        def _():
            acc_ref[...] = part

        @pl.when(kk > 0)
        def _():
            acc_ref[...] += part

        @pl.when(kk == nk - 1)
        def _():
            o_ref[...] = acc_ref[...].astype(o_ref.dtype)

    return pl.pallas_call(
        body, name=name, grid=(n // tn, m // tm, nk),
        in_specs=[a_spec, b_spec] + after_specs,
        out_specs=pl.BlockSpec((tm, tn), lambda j, i, kk: (i, j)),
        out_shape=jax.ShapeDtypeStruct((m, n), out_dtype),
        scratch_shapes=[pltpu.VMEM((tm, tn), F32)],
        compiler_params=_cparams(),
    )(a, b, *after)


def _prenorm_fwd(x, g):
    tm = 512

    def body(x_ref, g_ref, h_ref, ht_ref):
        xv = x_ref[...]
        r = lax.rsqrt(jnp.mean(xv * xv, axis=-1, keepdims=True) + EPS)
        hv = (xv * r * g_ref[...]).astype(BF16)
        h_ref[...] = hv
        ht_ref[...] = hv.T

    return pl.pallas_call(
        body, name="prenorm_fwd", grid=(SEQ // tm,),
        in_specs=[pl.BlockSpec((tm, D_MODEL), lambda i: (i, 0)), pl.BlockSpec((1, D_MODEL), lambda i: (0, 0))],
        out_specs=[pl.BlockSpec((tm, D_MODEL), lambda i: (i, 0)), pl.BlockSpec((D_MODEL, tm), lambda i: (0, i))],
        out_shape=[jax.ShapeDtypeStruct((SEQ, D_MODEL), BF16), jax.ShapeDtypeStruct((D_MODEL, SEQ), BF16)],
    )(x, g)


def _prenorm_bwd(x, dh, dy, g):
    tm = 512

    def body(x_ref, dh_ref, dy_ref, g_ref, gx_ref, dg_ref):
        xv = x_ref[...]
        r = lax.rsqrt(jnp.mean(xv * xv, axis=-1, keepdims=True) + EPS)
        n = xv * r
        dhv = dh_ref[...]
        dn = dhv * g_ref[...]
        gx_ref[...] = dy_ref[...] + r * (dn - n * jnp.mean(dn * n, axis=-1, keepdims=True))
        part = jnp.sum(dhv * n, axis=0, keepdims=True)

        @pl.when(pl.program_id(0) == 0)
        def _():
            dg_ref[...] = part

        @pl.when(pl.program_id(0) > 0)
        def _():
            dg_ref[...] += part

    row = pl.BlockSpec((tm, D_MODEL), lambda i: (i, 0))
    vec = pl.BlockSpec((1, D_MODEL), lambda i: (0, 0))
    return pl.pallas_call(
        body, name="prenorm_bwd", grid=(SEQ // tm,),
        in_specs=[row, row, row, vec], out_specs=[row, vec],
        out_shape=[jax.ShapeDtypeStruct((SEQ, D_MODEL), F32), jax.ShapeDtypeStruct((1, D_MODEL), F32)],
        compiler_params=_cparams(),
    )(x, dh, dy, g)


def _mla_prep_fwd(p, qg, kvg, wuq, wk, wv, rc, rs):
    tm = 512

    def body(lat_ref, qg_ref, kvg_ref, wuq_ref, wk_ref, wv_ref, c_ref, s_ref,
             cqn_ref, ckvn_ref, q_ref, k_ref, v_ref):
        c, s = c_ref[...], s_ref[...]
        lanes = _rope_lanes((tm, 128), MLA_ROPE_HALF, 128, 64)
        cq = lat_ref[:, 0:Q_RANK]
        r1 = lax.rsqrt(jnp.mean(cq * cq, axis=-1, keepdims=True) + EPS)
        cqn = (cq * r1 * qg_ref[...]).astype(BF16)
        cqn_ref[...] = cqn
        q = _nn(cqn, wuq_ref[...])
        for h in range(MLA_HEADS):
            sl = slice(h * 128, (h + 1) * 128)
            q_ref[:, sl] = (_rope_fwd(q[:, sl], c, s, MLA_ROPE_HALF, lanes) * MLA_SCALE).astype(BF16)
        ckv = lat_ref[:, Q_RANK:Q_RANK + KV_RANK]
        r2 = lax.rsqrt(jnp.mean(ckv * ckv, axis=-1, keepdims=True) + EPS)
        ckvn = (ckv * r2 * kvg_ref[...]).astype(BF16)
        ckvn_ref[...] = ckvn
        krr = _rope_fwd(lat_ref[:, Q_RANK + KV_RANK:N_LAT], c, s, MLA_ROPE_HALF, lanes)
        kn = _nn(ckvn, wk_ref[...])
        for h in range(MLA_HEADS):
            sl = slice(h * 128, (h + 1) * 128)
            k_ref[:, sl] = (kn[:, sl] + krr).astype(BF16)
        v_ref[...] = _nn(ckvn, wv_ref[...]).astype(BF16)

    def full(shape):
        return pl.BlockSpec(shape, lambda i: (0, 0))

    def rows(w):
        return pl.BlockSpec((tm, w), lambda i: (i, 0))

    return pl.pallas_call(
        body, name="mla_prep_fwd", grid=(SEQ // tm,),
        in_specs=[pl.BlockSpec((tm, N_LAT), lambda i: (i, COL_LAT // N_LAT)),
                  full((1, Q_RANK)), full((1, KV_RANK)), full((Q_RANK, 1024)), full((KV_RANK, 1024)),
                  full((KV_RANK, 512)), rows(128), rows(128)],
        out_specs=[rows(Q_RANK), rows(KV_RANK), rows(1024), rows(1024), rows(512)],
        out_shape=[jax.ShapeDtypeStruct((SEQ, Q_RANK), BF16), jax.ShapeDtypeStruct((SEQ, KV_RANK), BF16),
                   jax.ShapeDtypeStruct((SEQ, 1024), BF16), jax.ShapeDtypeStruct((SEQ, 1024), BF16),
                   jax.ShapeDtypeStruct((SEQ, 512), BF16)],
        compiler_params=_cparams(),
    )(p, qg, kvg, wuq, wk, wv, rc, rs)


def _mla_prep_bwd(dp_in, p, dq, dk, dv, qg, kvg, wuq, wk, wv, rc, rs):
    tm = 512

    def body(dp_any, lat_ref, dq_ref, dk_ref, dv_ref, qg_ref, kvg_ref, wuq_ref, wk_ref, wv_ref,
             c_ref, s_ref, dp_ref, dqb_ref, dkb_ref, dvb_ref, dgq_ref, dgkv_ref):
        del dp_any
        c, s = c_ref[...], s_ref[...]
        lanes = _rope_lanes((tm, 128), MLA_ROPE_HALF, 128, 64)
        lane = lax.broadcasted_iota(jnp.int32, (tm, 128), 1)
        dkr = jnp.zeros((tm, 128), F32)
        for h in range(MLA_HEADS):
            sl = slice(h * 128, (h + 1) * 128)
            dqb_ref[:, sl] = _rope_bwd(dq_ref[:, sl] * MLA_SCALE, c, s, MLA_ROPE_HALF, lanes).astype(BF16)
            dkh = dk_ref[:, sl]
            dkr = dkr + dkh
            dkb_ref[:, sl] = jnp.where(lane < 64, dkh, 0.0).astype(BF16)
        dkr = jnp.where((lane >= 64) & (lane < 96), dkr, 0.0)
        dkr = _rope_bwd(dkr, c, s, MLA_ROPE_HALF, lanes)
        dvb = dv_ref[...].astype(BF16)
        dvb_ref[...] = dvb

        cq = lat_ref[:, 0:Q_RANK]
        r1 = lax.rsqrt(jnp.mean(cq * cq, axis=-1, keepdims=True) + EPS)
        n1 = cq * r1
        dcqn = _nt(dqb_ref[...], wuq_ref[...])
        dn1 = dcqn * qg_ref[...]
        dcq = r1 * (dn1 - n1 * jnp.mean(dn1 * n1, axis=-1, keepdims=True))
        pq = jnp.sum(dcqn * n1, axis=0, keepdims=True)

        ckv = lat_ref[:, Q_RANK:Q_RANK + KV_RANK]
        r2 = lax.rsqrt(jnp.mean(ckv * ckv, axis=-1, keepdims=True) + EPS)
        n2 = ckv * r2
        dckvn = _nt(dkb_ref[...], wk_ref[...]) + _nt(dvb, wv_ref[...])
        dn2 = dckvn * kvg_ref[...]
        dckv = r2 * (dn2 - n2 * jnp.mean(dn2 * n2, axis=-1, keepdims=True))
        pkv = jnp.sum(dckvn * n2, axis=0, keepdims=True)

        dp_ref[:, 0:Q_RANK] = dcq.astype(BF16)
        dp_ref[:, Q_RANK:Q_RANK + KV_RANK] = dckv.astype(BF16)
        dp_ref[:, Q_RANK + KV_RANK:N_LAT] = dkr.astype(BF16)

        @pl.when(pl.program_id(0) == 0)
        def _():
            dgq_ref[...] = pq
            dgkv_ref[...] = pkv

        @pl.when(pl.program_id(0) > 0)
        def _():
            dgq_ref[...] += pq
            dgkv_ref[...] += pkv

    def full(shape):
        return pl.BlockSpec(shape, lambda i: (0, 0))

    def rows(w):
        return pl.BlockSpec((tm, w), lambda i: (i, 0))

    lat = pl.BlockSpec((tm, N_LAT), lambda i: (i, COL_LAT // N_LAT))
    return pl.pallas_call(
        body, name="mla_prep_bwd", grid=(SEQ // tm,),
        in_specs=[pl.BlockSpec(memory_space=pl.ANY), lat, rows(1024), rows(1024), rows(512),
                  full((1, Q_RANK)), full((1, KV_RANK)), full((Q_RANK, 1024)), full((KV_RANK, 1024)),
                  full((KV_RANK, 512)), rows(128), rows(128)],
        out_specs=[lat, rows(1024), rows(1024), rows(512), full((1, Q_RANK)), full((1, KV_RANK))],
        out_shape=[jax.ShapeDtypeStruct((SEQ, N_PAD), BF16), jax.ShapeDtypeStruct((SEQ, 1024), BF16),
                   jax.ShapeDtypeStruct((SEQ, 1024), BF16), jax.ShapeDtypeStruct((SEQ, 512), BF16),
                   jax.ShapeDtypeStruct((1, Q_RANK), F32), jax.ShapeDtypeStruct((1, KV_RANK), F32)],
        input_output_aliases={0: 0},
        compiler_params=_cparams(),
    )(dp_in, p, dq, dk, dv, qg, kvg, wuq, wk, wv, rc, rs)


FLASH_T = 512


def _head_half(shape, hh):
    lane = lax.broadcasted_iota(jnp.int32, shape, 1)
    return (lane < 64) if hh == 0 else (lane >= 64)


def _causal_keep(t):
    row = lax.broadcasted_iota(jnp.int32, (t, t), 0)
    col = lax.broadcasted_iota(jnp.int32, (t, t), 1)
    return row >= col


def _tri_steps(nb, q_major):
    if q_major:
        pairs = [(i, kb) for i in range(nb) for kb in range(i + 1)]
    else:
        pairs = [(i, kb) for kb in range(nb) for i in range(kb, nb)]
    return jnp.asarray([p[0] for p in pairs], jnp.int32), jnp.asarray([p[1] for p in pairs], jnp.int32)


def _mla_flash_fwd(q, k, v):
    t = FLASH_T
    nb = SEQ // t
    qtab, ktab = _tri_steps(nb, True)

    def body(qi_ref, ki_ref, q_ref, k_ref, v_ref, o_ref, lse_ref, m_scr, l_scr, acc_scr):
        step = pl.program_id(1)
        i, kb = qi_ref[step], ki_ref[step]

        @pl.when(kb == 0)
        def _():
            m_scr[...] = jnp.full_like(m_scr, NEG)
            l_scr[...] = jnp.zeros_like(l_scr)
            acc_scr[...] = jnp.zeros_like(acc_scr)

        def update(masked):
            vv = v_ref[...]
            for hh in range(2):
                sl = slice(hh * 128, (hh + 1) * 128)
                s = _nt(q_ref[:, sl], k_ref[:, sl])
                if masked:
                    s = jnp.where(_causal_keep(t), s, NEG)
                m_prev = m_scr[hh]
                m_new = jnp.maximum(m_prev, jnp.max(s, axis=-1, keepdims=True))
                pr = jnp.exp(s - jnp.tile(m_new, (1, t // 128)))
                alpha = jnp.exp(m_prev - m_new)
                l_scr[hh] = alpha * l_scr[hh] + jnp.sum(pr, axis=-1, keepdims=True)
                acc_scr[hh] = alpha * acc_scr[hh] + _nn(pr.astype(BF16), vv)
                m_scr[hh] = m_new

        @pl.when(kb < i)
        def _():
            update(False)

        @pl.when(kb == i)
        def _():
            update(True)
            o0 = acc_scr[0] / l_scr[0]
            o1 = acc_scr[1] / l_scr[1]
            o_ref[...] = jnp.where(_head_half((t, 128), 0), o0, o1)
            for hh in range(2):
                lse_ref[:, hh * 128:(hh + 1) * 128] = m_scr[hh] + jnp.log(l_scr[hh])

    grid_spec = pltpu.PrefetchScalarGridSpec(
        num_scalar_prefetch=2, grid=(4, qtab.shape[0]),
        in_specs=[pl.BlockSpec((t, 256), lambda j, s, qi, ki: (qi[s], j)),
                  pl.BlockSpec((t, 256), lambda j, s, qi, ki: (ki[s], j)),
                  pl.BlockSpec((t, 128), lambda j, s, qi, ki: (ki[s], j))],
        out_specs=[pl.BlockSpec((t, 128), lambda j, s, qi, ki: (qi[s], j)),
                   pl.BlockSpec((t, 256), lambda j, s, qi, ki: (qi[s], j))],
        scratch_shapes=[pltpu.VMEM((2, t, 128), F32), pltpu.VMEM((2, t, 128), F32), pltpu.VMEM((2, t, 128), F32)])
    return pl.pallas_call(
        body, name="mla_flash_fwd", grid_spec=grid_spec,
        out_shape=[jax.ShapeDtypeStruct((SEQ, 512), F32), jax.ShapeDtypeStruct((SEQ, 1024), F32)],
        compiler_params=_cparams(),
    )(qtab, ktab, q, k, v)


def _mla_flash_bwd(q, k, v, o, do, lse):
    t = FLASH_T
    nb = SEQ // t
    qtab, ktab = _tri_steps(nb, False)

    def body(qi_ref, ki_ref, q_ref, k_ref, v_ref, o_ref, do_ref, lse_ref, dq_ref, dk_ref, dv_ref, dk_scr, dv_scr):
        step = pl.program_id(1)
        i, kb = qi_ref[step], ki_ref[step]

        @pl.when(step == 0)
        def _():
            dq_ref[...] = jnp.zeros_like(dq_ref)

        @pl.when(i == kb)
        def _():
            dk_scr[...] = jnp.zeros_like(dk_scr)
            dv_scr[...] = jnp.zeros_like(dv_scr)

        def update(masked):
            vv = v_ref[...]
            ov = o_ref[...]
            dov = do_ref[...]
            rows = pl.ds(pl.multiple_of(i * t, t), t)
            for hh in range(2):
                sl = slice(hh * 128, (hh + 1) * 128)
                qh, kh = q_ref[:, sl], k_ref[:, sl]
                s = _nt(qh, kh)
                if masked:
                    s = jnp.where(_causal_keep(t), s, NEG)
                pr = jnp.exp(s - jnp.tile(lse_ref[:, sl], (1, t // 128)))
                dom = jnp.where(_head_half((t, 128), hh), dov, 0.0)
                domb = dom.astype(BF16)
                dv_scr[...] += _tn(pr.astype(BF16), domb)
                dpr = _nt(domb, vv)
                delta = jnp.sum(dom * ov, axis=-1, keepdims=True)
                ds = (pr * (dpr - delta)).astype(BF16)
                dq_ref[rows, sl] += _nn(ds, kh)
                dk_scr[hh] += _tn(ds, qh)

        @pl.when(i > kb)
        def _():
            update(False)

        @pl.when(i == kb)
        def _():
            update(True)

        @pl.when(i == nb - 1)
        def _():
            dk_ref[:, 0:128] = dk_scr[0]
            dk_ref[:, 128:256] = dk_scr[1]
            dv_ref[...] = dv_scr[...]

    qi_map = lambda j, s, qi, ki: (qi[s], j)
    ki_map = lambda j, s, qi, ki: (ki[s], j)
    grid_spec = pltpu.PrefetchScalarGridSpec(
        num_scalar_prefetch=2, grid=(4, qtab.shape[0]),
        in_specs=[pl.BlockSpec((t, 256), qi_map), pl.BlockSpec((t, 256), ki_map), pl.BlockSpec((t, 128), ki_map),
                  pl.BlockSpec((t, 128), qi_map), pl.BlockSpec((t, 128), qi_map), pl.BlockSpec((t, 256), qi_map)],
        out_specs=[pl.BlockSpec((SEQ, 256), lambda j, s, qi, ki: (0, j)), pl.BlockSpec((t, 256), ki_map),
                   pl.BlockSpec((t, 128), ki_map)],
        scratch_shapes=[pltpu.VMEM((2, t, 128), F32), pltpu.VMEM((t, 128), F32)])
    return pl.pallas_call(
        body, name="mla_flash_bwd", grid_spec=grid_spec,
        out_shape=[jax.ShapeDtypeStruct((SEQ, 1024), F32), jax.ShapeDtypeStruct((SEQ, 1024), F32),
                   jax.ShapeDtypeStruct((SEQ, 512), F32)],
        compiler_params=_cparams(),
    )(qtab, ktab, q, k, v, o, do, lse)


DIL_UNROLL = 4


def _strided(start, size, d):
    return pl.ds(start, size) if d == 1 else pl.ds(start, size, stride=d)


def _dil_prep_fwd(p, rc, rs, g):
    d = DIL_DILATIONS[g]
    sub_len = SEQ // d
    ch = min(sub_len, 512)

    def body(p_ref, c_ref, s_ref, o_ref):
        tq = pl.program_id(0)
        is_v = tq == 2
        mult = jnp.where(tq == 0, DIL_SCALE, 1.0).astype(F32)
        lanes = _rope_lanes((ch, 128), DIL_ROPE_HALF, 64, 0)
        o_ref[0, 0:BAND, :] = jnp.zeros((BAND, 128), BF16)
        for r in range(d):
            for c0 in range(0, sub_len, ch):
                rows = _strided(r + c0 * d, ch, d)
                xv = p_ref[rows, :]
                roped = _rope_fwd(xv, c_ref[rows, :], s_ref[rows, :], DIL_ROPE_HALF, lanes)
                at = BAND + r * sub_len + c0
                o_ref[0, at:at + ch, :] = (jnp.where(is_v, xv, roped) * mult).astype(BF16)

    tab = pl.BlockSpec((SEQ, 128), lambda tq, pr: (0, 0))
    return pl.pallas_call(
        body, name=f"dil_prep_fwd_g{g}", grid=(3, 4),
        in_specs=[pl.BlockSpec((SEQ, 128), lambda tq, pr: (0, COL_QKV // 128 + (tq * 3 + g) * 4 + pr)), tab, tab],
        out_specs=pl.BlockSpec((1, BAND + SEQ, 128), lambda tq, pr: (tq, 0, pr)),
        out_shape=jax.ShapeDtypeStruct((3, BAND + SEQ, 512), BF16),
        compiler_params=_cparams(),
    )(p, rc, rs)


DIL_ST = 1024
DIL_NB = DIL_ST // BAND


def _band_keep(g, b, t):
    nbs = SEQ // DIL_DILATIONS[g] // BAND
    row = lax.broadcasted_iota(jnp.int32, (BAND, 2 * BAND), 0)
    col = lax.broadcasted_iota(jnp.int32, (BAND, 2 * BAND), 1)
    cur = (col >= BAND) & (row >= col - BAND)
    prev = (col < BAND) & (col >= row)
    if nbs >= DIL_NB:
        if b > 0:
            return cur | prev
        return cur | (prev & ((t * DIL_NB) % nbs != 0))
    return cur | prev if b % nbs else cur


def _dil_tok(g, b, t):
    d = DIL_DILATIONS[g]
    nbs = SEQ // d // BAND
    gb = t * DIL_NB + b
    return _strided((gb % nbs) * BAND * d + gb // nbs, BAND, d)


def _dil_attn_fwd2(qkv, g):
    def body(q_ref, k_ref, v_ref, o_ref, l_ref, s_scr, p_scr, o_scr):
        t = pl.program_id(1)
        base = t * DIL_ST
        half0 = _head_half((DIL_ST, 128), 0)
        lse_h = []
        for hh in range(2):
            half = _head_half((BAND, 128), hh)
            for b in range(DIL_NB):
                qv = q_ref[0, pl.ds(pl.multiple_of(base + (b + 1) * BAND, BAND), BAND), :]
                k2 = k_ref[0, pl.ds(pl.multiple_of(base + b * BAND, BAND), 2 * BAND), :]
                sb = _nt(jnp.where(half, qv, jnp.zeros_like(qv)), k2)
                s_scr[b * BAND:(b + 1) * BAND, :] = jnp.where(_band_keep(g, b, t), sb, NEG)
            s = s_scr[...]
            m = jnp.max(s, axis=-1, keepdims=True)
            pr = jnp.exp(s - m)
            den = jnp.sum(pr, axis=-1, keepdims=True)
            p_scr[...] = pr.astype(BF16)
            for b in range(DIL_NB):
                v2 = v_ref[0, pl.ds(pl.multiple_of(base + b * BAND, BAND), 2 * BAND), :]
                o_scr[hh, b * BAND:(b + 1) * BAND, :] = _nn(p_scr[b * BAND:(b + 1) * BAND, :], v2)
            o_scr[hh] = o_scr[hh] / den
            lse_h.append(m + jnp.log(den))
        out = jnp.where(half0, o_scr[0], o_scr[1])
        lse = jnp.where(half0, lse_h[0], lse_h[1])
        for b in range(DIL_NB):
            tok = _dil_tok(g, b, t)
            o_ref[tok, :] = out[b * BAND:(b + 1) * BAND, :]
            l_ref[tok, :] = lse[b * BAND:(b + 1) * BAND, :]

    def inp(tq):
        return pl.BlockSpec((1, BAND + SEQ, 128), lambda pr, t: (tq, 0, pr))

    out = pl.BlockSpec((SEQ, 128), lambda pr, t: (0, pr))
    return pl.pallas_call(
        body, name=f"dil_attn_fwd_g{g}", grid=(4, SEQ // DIL_ST),
        in_specs=[inp(0), inp(1), inp(2)], out_specs=[out, out],
        out_shape=[jax.ShapeDtypeStruct((SEQ, 512), F32), jax.ShapeDtypeStruct((SEQ, 512), F32)],
        scratch_shapes=[pltpu.VMEM((DIL_ST, 2 * BAND), F32), pltpu.VMEM((DIL_ST, 2 * BAND), BF16),
                        pltpu.VMEM((2, DIL_ST, 128), F32)],
        compiler_params=_cparams(),
    )(qkv, qkv, qkv)


def _dil_attn_bwd2(qkv, dyd, yd, lse_all, g, token=None):
    d = DIL_DILATIONS[g]
    sub_len = SEQ // d
    nst = SEQ // DIL_ST
    after, after_specs = _after(token)

    def body(q_ref, k_ref, v_ref, do_ref, y_ref, l_ref, *rest):
        out_ref, dk_scr, dv_scr, s_scr, dp_scr, p_scr, ds_scr, do_scr, y_scr, l_scr, dq_scr = rest[-11:]
        t = pl.program_id(1)
        base = t * DIL_ST

        @pl.when(t == 0)
        def _():
            dk_scr[...] = jnp.zeros_like(dk_scr)
            dv_scr[...] = jnp.zeros_like(dv_scr)

        for b in range(DIL_NB):
            tok = _dil_tok(g, b, t)
            do_scr[b * BAND:(b + 1) * BAND, :] = do_ref[tok, :]
            y_scr[b * BAND:(b + 1) * BAND, :] = y_ref[tok, :]
            l_scr[b * BAND:(b + 1) * BAND, :] = l_ref[tok, :]
        for hh in range(2):
            half = _head_half((BAND, 128), hh)
            half_st = _head_half((DIL_ST, 128), hh)
            dom = jnp.where(half_st, do_scr[...], 0.0)
            delta = jnp.sum(dom * y_scr[...], axis=-1, keepdims=True)
            lcol = jnp.max(jnp.where(half_st, l_scr[...], NEG), axis=-1, keepdims=True)
            for b in range(DIL_NB):
                rows = slice(b * BAND, (b + 1) * BAND)
                qv = q_ref[0, pl.ds(pl.multiple_of(base + (b + 1) * BAND, BAND), BAND), :]
                band = pl.ds(pl.multiple_of(base + b * BAND, BAND), 2 * BAND)
                sb = _nt(jnp.where(half, qv, jnp.zeros_like(qv)), k_ref[0, band, :])
                s_scr[rows, :] = jnp.where(_band_keep(g, b, t), sb, NEG)
                dp_scr[rows, :] = _nt(dom[rows, :].astype(BF16), v_ref[0, band, :])
            pr = jnp.exp(s_scr[...] - lcol)
            p_scr[...] = pr.astype(BF16)
            ds_scr[...] = (pr * (dp_scr[...] - delta)).astype(BF16)
            for b in range(DIL_NB):
                rows = slice(b * BAND, (b + 1) * BAND)
                qv = q_ref[0, pl.ds(pl.multiple_of(base + (b + 1) * BAND, BAND), BAND), :]
                band = pl.ds(pl.multiple_of(base + b * BAND, BAND), 2 * BAND)
                dqb = jnp.where(half, _nn(ds_scr[rows, :], k_ref[0, band, :]), 0.0)
                if hh == 0:
                    dq_scr[rows, :] = dqb
                else:
                    dq_scr[rows, :] += dqb
                half2 = _head_half((2 * BAND, 128), hh)
                dk_scr[band, :] += jnp.where(half2, _tn(ds_scr[rows, :], qv), 0.0)
                dv_scr[band, :] += _tn(p_scr[rows, :], dom[rows, :].astype(BF16))
        for b in range(DIL_NB):
            out_ref[pl.ds(0, 1), _dil_tok(g, b, t), :] = dq_scr[b * BAND:(b + 1) * BAND, :][None]

        @pl.when(t == nst - 1)
        def _():
            for r in range(d):
                rows = _strided(r, sub_len, d)
                out_ref[pl.ds(1, 1), rows, :] = dk_scr[BAND + r * sub_len:BAND + (r + 1) * sub_len, :][None]
                out_ref[pl.ds(2, 1), rows, :] = dv_scr[BAND + r * sub_len:BAND + (r + 1) * sub_len, :][None]

    def inp(tq):
        return pl.BlockSpec((1, BAND + SEQ, 128), lambda pr, t: (tq, 0, pr))

    tok_spec = pl.BlockSpec((SEQ, 128), lambda pr, t: (0, pr))
    st = (DIL_ST, 2 * BAND)
    return pl.pallas_call(
        body, name=f"dil_attn_bwd_g{g}", grid=(4, nst),
        in_specs=[inp(0), inp(1), inp(2), tok_spec, tok_spec, tok_spec] + after_specs,
        out_specs=pl.BlockSpec((3, SEQ, 128), lambda pr, t: (0, 0, pr)),
        out_shape=jax.ShapeDtypeStruct((3, SEQ, 512), F32),
        scratch_shapes=[pltpu.VMEM((BAND + SEQ, 128), F32), pltpu.VMEM((BAND + SEQ, 128), F32),
                        pltpu.VMEM(st, F32), pltpu.VMEM(st, F32), pltpu.VMEM(st, BF16), pltpu.VMEM(st, BF16),
                        pltpu.VMEM((DIL_ST, 128), F32), pltpu.VMEM((DIL_ST, 128), F32), pltpu.VMEM((DIL_ST, 128), F32),
                        pltpu.VMEM((DIL_ST, 128), F32)],
        compiler_params=_cparams(),
    )(qkv, qkv, qkv, dyd, yd, lse_all, *after)


def _band_masks():
    row = lax.broadcasted_iota(jnp.int32, (BAND, BAND), 0)
    col = lax.broadcasted_iota(jnp.int32, (BAND, BAND), 1)
    return row >= col, col >= row


def _dil_attn_fwd(qkv, g):
    d = DIL_DILATIONS[g]
    nbs = SEQ // d // BAND
    nblk = SEQ // BAND

    def body(q_ref, k_ref, v_ref, o_ref, l_ref):
        keep_c, keep_p = _band_masks()
        half0 = _head_half((BAND, 128), 0)

        def step(i, carry):
            cur = pl.ds(pl.multiple_of(i * BAND, BAND), BAND)
            prv = pl.ds(pl.multiple_of(jnp.maximum(i - 1, 0) * BAND, BAND), BAND)
            has_prev = (i % nbs) != 0
            qv = q_ref[0, cur, :]
            kc, kp = k_ref[0, cur, :], k_ref[0, prv, :]
            vc, vp = v_ref[0, cur, :], v_ref[0, prv, :]
            outs, lses = [], []
            for hh in range(2):
                qm = jnp.where(_head_half((BAND, 128), hh), qv, jnp.zeros_like(qv))
                sc = jnp.where(keep_c, _nt(qm, kc), NEG)
                sp = jnp.where(keep_p & has_prev, _nt(qm, kp), NEG)
                m = jnp.maximum(jnp.max(sc, axis=-1, keepdims=True), jnp.max(sp, axis=-1, keepdims=True))
                pc, pp = jnp.exp(sc - m), jnp.exp(sp - m)
                den = jnp.sum(pc, axis=-1, keepdims=True) + jnp.sum(pp, axis=-1, keepdims=True)
                o = (_nn(pc.astype(BF16), vc) + _nn(pp.astype(BF16), vp)) / den
                outs.append(o)
                lses.append(jnp.broadcast_to(m + jnp.log(den), (BAND, 128)))
            tok = _strided((i % nbs) * BAND * d + i // nbs, BAND, d)
            o_ref[tok, :] = jnp.where(half0, outs[0], outs[1])
            l_ref[tok, :] = jnp.where(half0, lses[0], lses[1])
            return carry

        lax.fori_loop(0, nblk, step, 0, unroll=DIL_UNROLL)

    def inp(tq):
        return pl.BlockSpec((1, SEQ, 128), lambda pr: (tq, 0, pr))

    out = pl.BlockSpec((SEQ, 128), lambda pr: (0, pr))
    return pl.pallas_call(
        body, name=f"dil_attn_fwd_g{g}", grid=(4,),
        in_specs=[inp(0), inp(1), inp(2)], out_specs=[out, out],
        out_shape=[jax.ShapeDtypeStruct((SEQ, 512), F32), jax.ShapeDtypeStruct((SEQ, 512), F32)],
        compiler_params=_cparams(),
    )(qkv, qkv, qkv)


def _dil_attn_bwd(qkv, dyd, yd, lse_all, g):
    d = DIL_DILATIONS[g]
    sub_len = SEQ // d
    nbs = sub_len // BAND
    nblk = SEQ // BAND

    def body(q_ref, k_ref, v_ref, do_ref, y_ref, l_ref, out_ref, dk_scr, dv_scr):
        keep_c, keep_p = _band_masks()
        dk_scr[...] = jnp.zeros_like(dk_scr)
        dv_scr[...] = jnp.zeros_like(dv_scr)

        def step(i, carry):
            cur = pl.ds(pl.multiple_of(i * BAND, BAND), BAND)
            prv = pl.ds(pl.multiple_of(jnp.maximum(i - 1, 0) * BAND, BAND), BAND)
            has_prev = (i % nbs) != 0
            tok = _strided((i % nbs) * BAND * d + i // nbs, BAND, d)
            qv = q_ref[0, cur, :]
            kc, kp = k_ref[0, cur, :], k_ref[0, prv, :]
            vc, vp = v_ref[0, cur, :], v_ref[0, prv, :]
            dov, yv, lv = do_ref[tok, :], y_ref[tok, :], l_ref[tok, :]
            dq = jnp.zeros((BAND, 128), F32)
            dkc = jnp.zeros((BAND, 128), F32)
            dkp = jnp.zeros((BAND, 128), F32)
            dvc = jnp.zeros((BAND, 128), F32)
            dvp = jnp.zeros((BAND, 128), F32)
            for hh in range(2):
                half = _head_half((BAND, 128), hh)
                qm = jnp.where(half, qv, jnp.zeros_like(qv))
                lcol = jnp.max(jnp.where(half, lv, NEG), axis=-1, keepdims=True)
                pc = jnp.exp(jnp.where(keep_c, _nt(qm, kc), NEG) - lcol)
                pp = jnp.exp(jnp.where(keep_p & has_prev, _nt(qm, kp), NEG) - lcol)
                dom = jnp.where(half, dov, 0.0)
                domb = dom.astype(BF16)
                delta = jnp.sum(dom * yv, axis=-1, keepdims=True)
                dsc = (pc * (_nt(domb, vc) - delta)).astype(BF16)
                dsp = (pp * (_nt(domb, vp) - delta)).astype(BF16)
                dvc = dvc + _tn(pc.astype(BF16), domb)
                dvp = dvp + _tn(pp.astype(BF16), domb)
                dq = dq + jnp.where(half, _nn(dsc, kc) + _nn(dsp, kp), 0.0)
                dkc = dkc + jnp.where(half, _tn(dsc, qv), 0.0)
                dkp = dkp + jnp.where(half, _tn(dsp, qv), 0.0)
            out_ref[pl.ds(0, 1), tok, :] = dq[None]
            dk_scr[cur, :] += dkc
            dk_scr[prv, :] += dkp
            dv_scr[cur, :] += dvc
            dv_scr[prv, :] += dvp
            return carry

        lax.fori_loop(0, nblk, step, 0, unroll=DIL_UNROLL)
        for r in range(d):
            rows = _strided(r, sub_len, d)
            out_ref[pl.ds(1, 1), rows, :] = dk_scr[r * sub_len:(r + 1) * sub_len, :][None]
            out_ref[pl.ds(2, 1), rows, :] = dv_scr[r * sub_len:(r + 1) * sub_len, :][None]

    def inp(tq):
        return pl.BlockSpec((1, SEQ, 128), lambda pr: (tq, 0, pr))

    tok_spec = pl.BlockSpec((SEQ, 128), lambda pr: (0, pr))
    return pl.pallas_call(
        body, name=f"dil_attn_bwd_g{g}", grid=(4,),
        in_specs=[inp(0), inp(1), inp(2), tok_spec, tok_spec, tok_spec],
        out_specs=pl.BlockSpec((3, SEQ, 128), lambda pr: (0, 0, pr)),
        out_shape=jax.ShapeDtypeStruct((3, SEQ, 512), F32),
        scratch_shapes=[pltpu.VMEM((SEQ, 128), F32), pltpu.VMEM((SEQ, 128), F32)],
        compiler_params=_cparams(),
    )(qkv, qkv, qkv, dyd, yd, lse_all)


def _dil_prep_bwd(dp_in, dqkv, rc, rs, g):
    tm = 1024

    def body(dp_any, g_ref, c_ref, s_ref, dp_ref):
        del dp_any
        tq = pl.program_id(0)

        @pl.when(tq == 2)
        def _():
            dp_ref[...] = g_ref[0].astype(BF16)

        @pl.when(tq < 2)
        def _():
            mult = jnp.where(tq == 0, DIL_SCALE, 1.0).astype(F32)
            lanes = _rope_lanes((tm, 128), DIL_ROPE_HALF, 64, 0)
            cv, sv = c_ref[...], s_ref[...] * mult
            cv = cv * mult
            for pr in range(4):
                gv = g_ref[0, :, pr * 128:(pr + 1) * 128]
                dp_ref[:, pr * 128:(pr + 1) * 128] = _rope_bwd(gv, cv, sv, DIL_ROPE_HALF, lanes).astype(BF16)

    tab = pl.BlockSpec((tm, 128), lambda tq, i: (i, 0))
    return pl.pallas_call(
        body, name=f"dil_prep_bwd_g{g}", grid=(3, SEQ // tm),
        in_specs=[pl.BlockSpec(memory_space=pl.ANY),
                  pl.BlockSpec((1, tm, 512), lambda tq, i: (tq, i, 0)), tab, tab],
        out_specs=pl.BlockSpec((tm, 512), lambda tq, i: (i, COL_QKV // 512 + tq * 3 + g)),
        out_shape=jax.ShapeDtypeStruct((SEQ, N_PAD), BF16),
        input_output_aliases={0: 0},
    )(dp_in, dqkv, rc, rs)


TAIL_T = 256


def _tail(p, ya, o_g, l_g, x, target, wpm, wpd, wout, post_g):
    tm = TAIL_T

    def body(pgz_ref, ya_ref, o0_ref, o1_ref, o2_ref, l0_ref, l1_ref, l2_ref, x_ref, t_ref,
             wpm_ref, wpd_ref, wout_ref, pg_ref,
             dp_ref, dy_ref, mg_ref, dt_ref, ua_ref, dpa_ref, ud_ref, dpd_ref, dya_ref, dyd_ref,
             yd_ref, lse_ref, loss_ref, dgp_ref):
        l0, l1, l2 = l0_ref[...], l1_ref[...], l2_ref[...]
        mx = jnp.maximum(jnp.maximum(l0, l1), l2)
        e0, e1, e2 = jnp.exp(l0 - mx), jnp.exp(l1 - mx), jnp.exp(l2 - mx)
        den = e0 + e1 + e2
        yd = (e0 * o0_ref[...] + e1 * o1_ref[...] + e2 * o2_ref[...]) / den
        yd_ref[...] = yd
        lse_ref[...] = mx + jnp.log(den)
        ya = ya_ref[...]

        gm, gd = pgz_ref[:, 0:1024], pgz_ref[:, 1024:2048]
        zm, zd = pgz_ref[:, 2048:2560], pgz_ref[:, 2560:3072]
        szm, szd = _sigmoid(zm), _sigmoid(zd)
        sm, sd = zm * szm, zd * szd
        ua = (ya * sm).astype(BF16)
        ud = (yd * sd).astype(BF16)
        ua_ref[...] = ua
        ud_ref[...] = ud
        pa = _nn(ua, wpm_ref[...])
        pd = _nn(ud, wpd_ref[...])
        sgm, sgd = _sigmoid(gm), _sigmoid(gd)
        mg = (sgm * pa + sgd * pd).astype(BF16)
        mg_ref[...] = mg
        t = _nn(mg, wout_ref[...])
        r3 = lax.rsqrt(jnp.mean(t * t, axis=-1, keepdims=True) + EPS)
        n = t * r3
        pg = pg_ref[...]
        err = x_ref[...] + n * pg - t_ref[...]
        lpart = jnp.sum(err * err, axis=0, keepdims=True)

        dy = err * (1.0 / D_MODEL)
        dy_ref[...] = dy
        gpart = jnp.sum(dy * n, axis=0, keepdims=True)
        dn = dy * pg
        dt = (r3 * (dn - n * jnp.mean(dn * n, axis=-1, keepdims=True))).astype(BF16)
        dt_ref[...] = dt
        dmg = _nt(dt, wout_ref[...])
        dpa = (dmg * sgm).astype(BF16)
        dpd = (dmg * sgd).astype(BF16)
        dpa_ref[...] = dpa
        dpd_ref[...] = dpd
        dp_ref[:, 0:1024] = (dmg * pa * sgm * (1.0 - sgm)).astype(BF16)
        dp_ref[:, 1024:2048] = (dmg * pd * sgd * (1.0 - sgd)).astype(BF16)
        dua = _nt(dpa, wpm_ref[...])
        dud = _nt(dpd, wpd_ref[...])
        dya_ref[...] = dua * sm
        dyd_ref[...] = dud * sd
        dp_ref[:, 2048:2560] = (dua * ya * szm * (1.0 + zm * (1.0 - szm))).astype(BF16)
        dp_ref[:, 2560:3072] = (dud * yd * szd * (1.0 + zd * (1.0 - szd))).astype(BF16)

        @pl.when(pl.program_id(0) == 0)
        def _():
            loss_ref[...] = lpart
            dgp_ref[...] = gpart

        @pl.when(pl.program_id(0) > 0)
        def _():
            loss_ref[...] += lpart
            dgp_ref[...] += gpart

    def rows(w):
        return pl.BlockSpec((tm, w), lambda i: (i, 0))

    def full(shape):
        return pl.BlockSpec(shape, lambda i: (0, 0))

    def sds(w, dt):
        return jax.ShapeDtypeStruct((SEQ, w), dt)

    return pl.pallas_call(
        body, name="tail", grid=(SEQ // tm,),
        in_specs=[rows(3072), rows(512), rows(512), rows(512), rows(512), rows(512), rows(512), rows(512),
                  rows(1024), rows(1024), full((512, 1024)), full((512, 1024)), full((1024, 1024)), full((1, 1024))],
        out_specs=[rows(3072), rows(1024), rows(1024), rows(1024), rows(512), rows(1024), rows(512), rows(1024),
                   rows(512), rows(512), rows(512), rows(512), full((1, 1024)), full((1, 1024))],
        out_shape=[sds(N_PAD, BF16), sds(1024, F32), sds(1024, BF16), sds(1024, BF16), sds(512, BF16),
                   sds(1024, BF16), sds(512, BF16), sds(1024, BF16), sds(512, F32), sds(512, F32),
                   sds(512, F32), sds(512, F32),
                   jax.ShapeDtypeStruct((1, 1024), F32), jax.ShapeDtypeStruct((1, 1024), F32)],
        compiler_params=_cparams(),
    )(p, ya, o_g[0], o_g[1], o_g[2], l_g[0], l_g[1], l_g[2], x, target, wpm, wpd, wout, post_g)


def _sum_parts(parts, tr, name):
    n, r, w = parts.shape

    def body(p_ref, o_ref):
        acc = p_ref[0].astype(F32)
        for s in range(1, n):
            acc = acc + p_ref[s].astype(F32)
        o_ref[...] = acc

    return pl.pallas_call(
        body, name=name, grid=(r // tr,),
        in_specs=[pl.BlockSpec((n, tr, w), lambda i: (0, i, 0))],
        out_specs=pl.BlockSpec((tr, w), lambda i: (i, 0)),
        out_shape=jax.ShapeDtypeStruct((r, w), F32),
    )(parts)


def _adamw(w, g, m, v, name):
    lead = w.shape[:-2]
    r, c = w.shape[-2:]
    tr = 128 if r % 128 == 0 else r
    c1 = 1.0 - ADAM_B1 ** ADAM_STEP
    c2 = 1.0 - ADAM_B2 ** ADAM_STEP

    def body(w_ref, g_ref, m_ref, v_ref, d_ref, nm_ref, nv_ref):
        gv = g_ref[...]
        nm = ADAM_B1 * m_ref[...] + (1.0 - ADAM_B1) * gv
        nv = ADAM_B2 * v_ref[...] + (1.0 - ADAM_B2) * (gv * gv)
        nm_ref[...] = nm
        nv_ref[...] = nv
        d_ref[...] = -ADAM_LR * ((nm / c1) / (jnp.sqrt(nv / c2) + ADAM_EPS) + ADAM_WD * w_ref[...])

    zeros = (0,) * len(lead)
    spec = pl.BlockSpec((1,) * len(lead) + (tr, c), lambda i: zeros + (i, 0))
    sd = jax.ShapeDtypeStruct(w.shape, F32)
    return pl.pallas_call(
        body, name=name, grid=(r // tr,),
        in_specs=[spec] * 4, out_specs=[spec] * 3, out_shape=[sd] * 3,
    )(w, g, m, v)


ANY = pl.BlockSpec(memory_space=pl.ANY)


def _my_place():
    return lax.axis_index("x"), lax.axis_index("y"), lax.axis_index("c")


def _allgather_weights(mats):
    def body(*refs):
        w_refs, out_refs = refs[:N_MATS], refs[N_MATS:2 * N_MATS]
        send_sems, recv_sems = refs[2 * N_MATS:]
        x, y, c = _my_place()
        sibling = (x, y, 1 - c)
        chips = [(1 - x, y), (x, 1 - y), (1 - x, 1 - y)]

        def copy(k, src, dst, to):
            return pltpu.make_async_remote_copy(src_ref=src, dst_ref=dst, send_sem=send_sems.at[k],
                                                recv_sem=recv_sems.at[k], device_id=to, device_id_type=MESH)

        def half(mi, shard, hc):
            hr = SHARD_SHAPES[mi][0] // 2
            return out_refs[mi].at[shard, pl.ds(pl.multiple_of(hc * hr, 16), hr), :]

        started = []
        for mi in range(N_MATS):
            hr = SHARD_SHAPES[mi][0] // 2
            my_half = w_refs[mi].at[pl.ds(pl.multiple_of(c * hr, 16), hr), :]
            for j, (cx, cy) in enumerate(chips):
                cp = copy(mi * 6 + j, my_half, half(mi, 2 * x + y, c), (cx, cy, c))
                cp.start()
                started.append(cp)
        for mi in range(N_MATS):
            for j, (cx, cy) in enumerate(chips):
                landed = half(mi, 2 * cx + cy, c)
                copy(mi * 6 + j, landed, landed, (cx, cy, c)).wait_recv()
                fw = copy(mi * 6 + 3 + j, landed, landed, sibling)
                fw.start()
                started.append(fw)
        for mi in range(N_MATS):
            for j, (cx, cy) in enumerate(chips):
                other = half(mi, 2 * cx + cy, 1 - c)
                copy(mi * 6 + 3 + j, other, other, sibling).wait_recv()
        for cp in started:
            cp.wait_send()

    return pl.pallas_call(
        body, name="allgather_weights",
        in_specs=[ANY] * N_MATS, out_specs=[ANY] * N_MATS,
        out_shape=[jax.ShapeDtypeStruct((4, r, c), BF16) for r, c in SHARD_SHAPES],
        scratch_shapes=[pltpu.SemaphoreType.DMA((6 * N_MATS,)), pltpu.SemaphoreType.DMA((6 * N_MATS,))],
    )(*mats)


HBM = pl.BlockSpec(memory_space=pltpu.HBM)
SEM = pl.BlockSpec(memory_space=pltpu.SEMAPHORE)
DATAFLOW = pltpu.SideEffectType.DATAFLOW_SIDE_EFFECTING


def _peers(x, y, c):
    out = []
    for k in range(1, 8):
        px, py, pc = x ^ (k >> 2), y ^ ((k >> 1) & 1), c ^ (k & 1)
        out.append((k - 1, (px, py, pc), 4 * px + 2 * py + pc))
    return out


def _exchange_start(parts, name):
    n = len(parts)

    def body(*refs):
        p_refs, land_refs = refs[:n], refs[n:2 * n]
        send_sems, recv_sems, token = refs[2 * n], refs[2 * n + 1], refs[-1]
        x, y, c = _my_place()
        me = 4 * x + 2 * y + c
        for k, dev, peer in _peers(x, y, c):
            for mi in range(n):
                pltpu.make_async_remote_copy(
                    src_ref=p_refs[mi].at[peer], dst_ref=land_refs[mi].at[me], send_sem=send_sems.at[k * n + mi],
                    recv_sem=recv_sems.at[k * n + mi], device_id=dev, device_id_type=MESH).start()
        token[...] = jnp.zeros_like(token)

    hbm = [pltpu.HBM(p.shape, p.dtype) for p in parts]
    outs = pl.pallas_call(
        body, name=name + "_start",
        out_shape=(pltpu.SemaphoreType.DMA((7 * n,)), pltpu.SemaphoreType.DMA((7 * n,)), *hbm, *hbm,
                   jax.ShapeDtypeStruct((8, 128), F32)),
        in_specs=[HBM] * (2 * n), out_specs=(SEM, SEM, *[HBM] * (2 * n), pl.BlockSpec(memory_space=pltpu.VMEM)),
        input_output_aliases={i: 2 + i for i in range(2 * n)},
        compiler_params=pltpu.CompilerParams(has_side_effects=DATAFLOW),
    )(*[pltpu.with_memory_space_constraint(p, pltpu.HBM) for p in parts],
      *[pltpu.with_memory_space_constraint(lax.empty(p.shape, p.dtype), pltpu.HBM) for p in parts])
    return (name, outs[:-1]), outs[-1]


def _exchange_wait(handle, after):
    name, outs = handle
    n = (len(outs) - 2) // 2

    def body(*refs):
        p_refs, land_refs = refs[:n], refs[n:2 * n]
        send_sems, recv_sems = refs[2 * n], refs[2 * n + 1]
        x, y, c = _my_place()
        me = 4 * x + 2 * y + c
        for k, dev, peer in _peers(x, y, c):
            for mi in range(n):
                pltpu.make_async_remote_copy(
                    src_ref=p_refs[mi].at[peer], dst_ref=land_refs[mi].at[me], send_sem=send_sems.at[k * n + mi],
                    recv_sem=recv_sems.at[k * n + mi], device_id=dev, device_id_type=MESH).wait_send()
                slot = land_refs[mi].at[peer]
                pltpu.make_async_remote_copy(
                    src_ref=slot, dst_ref=slot, send_sem=send_sems.at[k * n + mi],
                    recv_sem=recv_sems.at[k * n + mi], device_id=dev, device_id_type=MESH).wait_recv()

    bufs = outs[2:]
    res = pl.pallas_call(
        body, name=name + "_wait", out_shape=tuple(pltpu.HBM(b.shape, b.dtype) for b in bufs),
        in_specs=[HBM] * (2 * n) + [SEM, SEM, ANY], out_specs=tuple([HBM] * (2 * n)),
        input_output_aliases={i: i for i in range(2 * n)},
        compiler_params=pltpu.CompilerParams(has_side_effects=DATAFLOW),
    )(*bufs, outs[0], outs[1], after)
    return list(res[n:])


def _swap_halves(halves, gvec):
    def body(*refs):
        g_refs, gv_ref = refs[:N_MATS], refs[N_MATS]
        out_refs, rg_ref = refs[N_MATS + 1:2 * N_MATS + 1], refs[2 * N_MATS + 1]
        send_sems, recv_sems = refs[2 * N_MATS + 2:]
        x, y, c = _my_place()
        me = 4 * x + 2 * y + c
        sends = []
        for mi in range(N_MATS):
            cp = pltpu.make_async_remote_copy(src_ref=g_refs[mi], dst_ref=out_refs[mi].at[c], send_sem=send_sems.at[mi],
                                              recv_sem=recv_sems.at[mi], device_id=(x, y, 1 - c), device_id_type=MESH)
            cp.start()
            sends.append(cp)
        for k, dev, peer in _peers(x, y, c):
            cp = pltpu.make_async_remote_copy(src_ref=gv_ref, dst_ref=rg_ref.at[me], send_sem=send_sems.at[N_MATS + k],
                                              recv_sem=recv_sems.at[N_MATS + k], device_id=dev, device_id_type=MESH)
            cp.start()
            sends.append(cp)
        for mi in range(N_MATS):
            got = out_refs[mi].at[1 - c]
            pltpu.make_async_remote_copy(src_ref=got, dst_ref=got, send_sem=send_sems.at[mi], recv_sem=recv_sems.at[mi],
                                         device_id=(x, y, 1 - c), device_id_type=MESH).wait_recv()
        for k, dev, peer in _peers(x, y, c):
            got = rg_ref.at[peer]
            pltpu.make_async_remote_copy(src_ref=got, dst_ref=got, send_sem=send_sems.at[N_MATS + k],
                                         recv_sem=recv_sems.at[N_MATS + k], device_id=dev, device_id_type=MESH).wait_recv()
        for cp in sends:
            cp.wait_send()

    outs = pl.pallas_call(
        body, name="swap_halves",
        in_specs=[ANY] * (N_MATS + 1), out_specs=[ANY] * (N_MATS + 1),
        out_shape=[jax.ShapeDtypeStruct((2, r // 2, c), F32) for r, c in SHARD_SHAPES]
        + [jax.ShapeDtypeStruct((8, 8, N_GAINS), F32)],
        scratch_shapes=[pltpu.SemaphoreType.DMA((N_MATS + 7,)), pltpu.SemaphoreType.DMA((N_MATS + 7,))],
    )(*halves, gvec)
    return outs[:N_MATS], outs[N_MATS]


def _set_slot(arr, block, idx):
    return lax.dynamic_update_slice(arr, block[None], (idx,) + (0,) * block.ndim)


PAD_RUNS = ((6304, 8352, 0), (5280, 6304, COL_Z), (672, 5280, COL_QKV), (0, 640, COL_LAT), (640, 672, COL_LAT + 704))
W_IN_SHARD = 2088


def _full_weights(gathered):
    def cols(a):
        return jnp.concatenate([a[s] for s in range(4)], axis=1)

    w_uq, w_ukv, w_pm, w_pd = [cols(a) for a in gathered[1:5]]
    w_out = gathered[5].reshape(D_MODEL, D_MODEL)
    g_in = gathered[0]
    pieces, at = [], 0
    for lo, hi, pad_lo in sorted(PAD_RUNS, key=lambda t: t[2]):
        if pad_lo > at:
            pieces.append(jnp.zeros((D_MODEL, pad_lo - at), g_in.dtype))
        for s in range(4):
            a_, b_ = max(lo, s * W_IN_SHARD), min(hi, (s + 1) * W_IN_SHARD)
            if a_ < b_:
                pieces.append(g_in[s][:, a_ - s * W_IN_SHARD:b_ - s * W_IN_SHARD])
        at = pad_lo + hi - lo
    pieces.append(jnp.zeros((D_MODEL, N_PAD - at), g_in.dtype))
    w_pad = jnp.concatenate(pieces, axis=1)
    z32 = jnp.zeros((Q_RANK, 32), w_uq.dtype)
    wuq_pad = jnp.concatenate([t for h in range(MLA_HEADS) for t in (w_uq[:, h * 96:(h + 1) * 96], z32)], axis=1)
    z64 = jnp.zeros((KV_RANK, 64), w_ukv.dtype)
    wk_pad = jnp.concatenate([t for h in range(MLA_HEADS) for t in (w_ukv[:, h * 128:h * 128 + 64], z64)], axis=1)
    wv = jnp.concatenate([w_ukv[:, h * 128 + 64:(h + 1) * 128] for h in range(MLA_HEADS)], axis=1)
    return w_pad, wuq_pad, wk_pad, wv, w_pm, w_pd, w_out


def _grad_parts_in(dw_pad):
    def in_block(s, h):
        rows = slice(h * 512, (h + 1) * 512)
        out = []
        for lo, hi, pad_lo in sorted(PAD_RUNS):
            a_, b_ = max(lo, s * W_IN_SHARD), min(hi, (s + 1) * W_IN_SHARD)
            if a_ < b_:
                out.append(dw_pad[rows, pad_lo + a_ - lo:pad_lo + b_ - lo])
        return jnp.concatenate(out, axis=1)

    return jnp.stack([in_block(s, h) for s in range(4) for h in range(2)])


def _grad_parts_small(dwuq_pad, dwk_pad, dwv, dwpm, dwpd, dwout):
    d_uq = jnp.concatenate([dwuq_pad[:, h * 128:h * 128 + 96] for h in range(MLA_HEADS)], axis=1)
    d_ukv = jnp.concatenate([t for h in range(MLA_HEADS) for t in (dwk_pad[:, h * 128:h * 128 + 64], dwv[:, h * 64:(h + 1) * 64])],
                            axis=1)

    def col_blocks(m):
        r, c = m.shape[0] // 2, m.shape[1] // 4
        return jnp.stack([m[h * r:(h + 1) * r, s * c:(s + 1) * c] for s in range(4) for h in range(2)])

    return [col_blocks(d_uq), col_blocks(d_ukv), col_blocks(dwpm), col_blocks(dwpd),
            dwout.reshape(8, 128, D_MODEL)]


def _rope_tables(positions):
    pos = positions.reshape(SEQ).astype(F32)
    lane = jnp.arange(128)

    def table(rot, first, period):
        inv = ROPE_THETA ** (-jnp.arange(0, rot, 2, dtype=F32) / rot)
        half = rot // 2
        off = lane % period - first
        in1, in2 = (off >= 0) & (off < half), (off >= half) & (off < rot)
        inv_lane = jnp.where(in1 | in2, inv[jnp.clip(off % half, 0, half - 1)], 0.0)
        sign = jnp.where(in1, -1.0, 1.0).astype(F32)
        ang = pos[:, None] * inv_lane[None, :]
        return jnp.cos(ang), jnp.sin(ang) * sign[None, :]

    return table(32, 64, 128), table(16, 0, 64)


def _device_grads(x, positions, target, gains, gathered, send):
    pre_g, q_g, kv_g, post_g = gains
    w_pad, wuq_pad, wk_pad, wv, w_pm, w_pd, w_out = _full_weights(gathered)
    (mc, ms), (dc, ds) = _rope_tables(positions)

    h, h_t = _prenorm_fwd(x, pre_g)
    p = _matmul(h, w_pad, "nn", F32, 1024, 1408, 1024, "in_proj")
    cqn, ckvn, q, k, v = _mla_prep_fwd(p, q_g, kv_g, wuq_pad, wk_pad, wv, mc, ms)
    ya, lse_m = _mla_flash_fwd(q, k, v)
    qkv = [_dil_prep_fwd(p, dc, ds, g) for g in range(3)]
    o_g, l_g = zip(*[_dil_attn_fwd2(qkv[g], g) for g in range(3)])
    (dp, dy, mg, dt, ua, dpa, ud, dpd, dya, dyd, yd, lse_d, loss_cols, dg_post) = _tail(
        p, ya, o_g, l_g, x, target, w_pm, w_pd, w_out, post_g)

    dq, dk, dv = _mla_flash_bwd(q, k, v, ya, dya, lse_m)
    dp, dqb, dkb, dvb, dg_q, dg_kv = _mla_prep_bwd(dp, p, dq, dk, dv, q_g, kv_g, wuq_pad, wk_pad, wv, mc, ms)
    dwuq_pad = _matmul(cqn, dqb, "tn", BF16, Q_RANK, 1024, 512, "dw_uq")
    dwk_pad = _matmul(ckvn, dkb, "tn", BF16, KV_RANK, 1024, 512, "dw_k")
    dwv = _matmul(ckvn, dvb, "tn", BF16, KV_RANK, 512, 512, "dw_v")
    dwpm = _matmul(ua, dpa, "tn", BF16, 512, 1024, 512, "dw_proj_mla")
    dwpd = _matmul(ud, dpd, "tn", BF16, 512, 1024, 512, "dw_proj_dil")
    dwout = _matmul(mg, dt, "tn", BF16, 1024, 1024, 512, "dw_out")
    small, token = send(_grad_parts_small(dwuq_pad, dwk_pad, dwv, dwpm, dwpd, dwout), "exchange_small")
    for g in range(3):
        dqkv = _dil_attn_bwd2(qkv[g], dyd, yd, lse_d, g, token if g == 0 else None)
        dp = _dil_prep_bwd(dp, dqkv, dc, ds, g)

    dw_pad = _matmul(h_t, dp, "nn", BF16, 1024, 1408, 2048, "dw_in")
    big, token = send([_grad_parts_in(dw_pad)], "exchange_in")
    dh = _matmul(dp, w_pad, "nt", F32, 1024, 1024, 1408, "dh", token)
    grad_x, dg_pre = _prenorm_bwd(x, dh, dy, pre_g)

    gvec = jnp.concatenate([dg_pre, dg_q, dg_kv, dg_post], axis=1)
    return loss_cols, grad_x, (big, small), gvec


def kernel(x, positions, pre_norm_g, w_in, q_norm_g, w_uq, kv_norm_g, w_ukv, w_proj_mla, w_proj_dil, w_out, post_norm_g, loss_target, m_pre_norm_g, m_w_in, m_q_norm_g, m_w_uq, m_kv_norm_g, m_w_ukv, m_w_proj_mla, m_w_proj_dil, m_w_out, m_post_norm_g, v_pre_norm_g, v_w_in, v_q_norm_g, v_w_uq, v_kv_norm_g, v_w_ukv, v_w_proj_mla, v_w_proj_dil, v_w_out, v_post_norm_g):
    xi, yi, ci = _my_place()
    chip, me = 2 * xi + yi, 4 * xi + 2 * yi + ci
    mats = [w.reshape(w.shape[1:]).astype(BF16) for w in (w_in, w_uq, w_ukv, w_proj_mla, w_proj_dil, w_out)]
    gathered = [_set_slot(g, m, chip) for g, m in zip(_allgather_weights(mats), mats)]
    gains = (pre_norm_g, q_norm_g, kv_norm_g, post_norm_g)
    sent = {}

    def send(blocks, name):
        sent[name] = blocks
        return _exchange_start(blocks, name)

    loss_cols, grad_x, (big, small), gvec = _device_grads(x[0], positions, loss_target[0], gains, gathered, send)

    loss = lax.psum(jnp.sum(loss_cols) * (0.5 / D_MODEL), ("x", "y", "c"))

    recv = _exchange_wait(big, grad_x) + _exchange_wait(small, grad_x)
    own = [lax.dynamic_index_in_dim(p, me, 0, keepdims=False) for p in sent["exchange_in"] + sent["exchange_small"]]
    recv = [_set_slot(r, o, me) for r, o in zip(recv, own)]
    halves = [_sum_parts(recv[mi], 64, f"sum_grad_{mi}") for mi in range(N_MATS)]
    gvec8 = jnp.pad(gvec, ((0, 7), (0, 0)))
    swapped, recv_gains = _swap_halves(halves, gvec8)
    g_gains = _sum_parts(_set_slot(recv_gains, gvec8, me), 8, "sum_gain_parts")[0:1]
    g_mats = [_set_slot(s, hf, ci).reshape((1,) + shp) for s, hf, shp in zip(swapped, halves, SHARD_SHAPES)]

    off = [0, 1024, 1408, 1664, 2688]
    g_gain = [g_gains[:, off[i]:off[i + 1]] for i in range(4)]
    grads = [g_gain[0], g_mats[0], g_gain[1], g_mats[1], g_gain[2], g_mats[2], g_mats[3], g_mats[4], g_mats[5], g_gain[3]]
    ws = [pre_norm_g, w_in, q_norm_g, w_uq, kv_norm_g, w_ukv, w_proj_mla, w_proj_dil, w_out, post_norm_g]
    ms = [m_pre_norm_g, m_w_in, m_q_norm_g, m_w_uq, m_kv_norm_g, m_w_ukv, m_w_proj_mla, m_w_proj_dil, m_w_out, m_post_norm_g]
    vs = [v_pre_norm_g, v_w_in, v_q_norm_g, v_w_uq, v_kv_norm_g, v_w_ukv, v_w_proj_mla, v_w_proj_dil, v_w_out, v_post_norm_g]
    deltas, new_m, new_v = [], [], []
    for i, (w, g, m, v) in enumerate(zip(ws, grads, ms, vs)):
        d_, m_, v_ = _adamw(w, g, m, v, f"adamw_{i}")
        deltas.append(d_)
        new_m.append(m_)
        new_v.append(v_)
    return (loss, grad_x.reshape(x.shape), *grads, *deltas, *new_m, *new_v)
```

```python
import jax
import jax.numpy as jnp
from jax import lax
from jax.experimental import pallas as pl
from jax.experimental.pallas import tpu as pltpu

F32 = jnp.float32
BF16 = jnp.bfloat16

SEQ = 4096
D_MODEL = 1024
EPS = 1e-6
ROPE_THETA = 500000.0
MLA_HEADS = 8
Q_RANK = 384
KV_RANK = 256
MLA_SCALE = 96.0 ** -0.5
MLA_ROPE_HALF = 16
DIL_DILATIONS = (1, 4, 16)
DIL_ROPE_HALF = 8
DIL_SCALE = 0.125
BAND = 128

N_LAT = 768
COL_Z, COL_QKV, COL_LAT = 2048, 3072, 7680
N_PAD = 8448
IN_SPLITS = (384, 256, 32, 4608, 512, 512, 1024, 1024)

SHARD_SHAPES = ((1024, 2088), (384, 192), (256, 256), (512, 256), (512, 256), (256, 1024))
N_MATS = len(SHARD_SHAPES)
N_GAINS = 2688

ADAM_LR, ADAM_B1, ADAM_B2, ADAM_EPS, ADAM_WD, ADAM_STEP = 0.001, 0.9, 0.999, 1e-08, 0.01, 10

VMEM_LIMIT = 56 * 1024 * 1024
NEG = -1e30
MESH = pl.DeviceIdType.MESH


def _cparams(**kw):
    return pltpu.CompilerParams(vmem_limit_bytes=VMEM_LIMIT, **kw)


def _dot(a, b, dims):
    return lax.dot_general(a, b, (dims, ((), ())), preferred_element_type=F32)


def _nn(a, b):
    return _dot(a, b, ((1,), (0,)))


def _nt(a, b):
    return _dot(a, b, ((1,), (1,)))


def _tn(a, b):
    return _dot(a, b, ((0,), (0,)))


def _rope_lanes(shape, half, period, first):
    lane = lax.broadcasted_iota(jnp.int32, shape, len(shape) - 1) % period
    return (lane >= first) & (lane < first + half), (lane >= first + half) & (lane < first + 2 * half)


def _rope_fwd(x, c, s, half, lanes):
    x1, _ = lanes
    return x * c + jnp.where(x1, pltpu.roll(x, 128 - half, 1), pltpu.roll(x, half, 1)) * s


def _rope_bwd(g, c, s, half, lanes):
    x1, x2 = lanes
    gs = g * s
    return g * c + jnp.where(x2, pltpu.roll(gs, half, 1), jnp.where(x1, pltpu.roll(gs, 128 - half, 1), 0.0))


def _sigmoid(x):
    return 1.0 / (1.0 + jnp.exp(-x))


def _after(token):
    return ([], []) if token is None else ([token], [pl.BlockSpec(memory_space=pl.ANY)])


def _matmul(a, b, mode, out_dtype, tm, tn, tk, name, token=None):
    after, after_specs = _after(token)
    if mode == "nn":
        (m, k), n = a.shape, b.shape[1]
        a_spec = pl.BlockSpec((tm, tk), lambda j, i, kk: (i, kk))
        b_spec = pl.BlockSpec((tk, tn), lambda j, i, kk: (kk, j))
        dot = _nn
    elif mode == "nt":
        (m, k), n = a.shape, b.shape[0]
        a_spec = pl.BlockSpec((tm, tk), lambda j, i, kk: (i, kk))
        b_spec = pl.BlockSpec((tn, tk), lambda j, i, kk: (j, kk))
        dot = _nt
    else:
        (k, m), n = a.shape, b.shape[1]
        a_spec = pl.BlockSpec((tk, tm), lambda j, i, kk: (kk, i))
        b_spec = pl.BlockSpec((tk, tn), lambda j, i, kk: (kk, j))
        dot = _tn
    assert m % tm == 0 and n % tn == 0 and k % tk == 0, (name, m, n, k, tm, tn, tk)
    nk = k // tk

    def body(a_ref, b_ref, *rest):
        o_ref, acc_ref = rest[-2:]
        kk = pl.program_id(2)
        part = dot(a_ref[...], b_ref[...])

        @pl.when(kk == 0)
        def _():
            acc_ref[...] = part

        @pl.when(kk > 0)
        def _():
            acc_ref[...] += part

        @pl.when(kk == nk - 1)
        def _():
            o_ref[...] = acc_ref[...].astype(o_ref.dtype)

    return pl.pallas_call(
        body, name=name, grid=(n // tn, m // tm, nk),
        in_specs=[a_spec, b_spec] + after_specs,
        out_specs=pl.BlockSpec((tm, tn), lambda j, i, kk: (i, j)),
        out_shape=jax.ShapeDtypeStruct((m, n), out_dtype),
        scratch_shapes=[pltpu.VMEM((tm, tn), F32)],
        compiler_params=_cparams(),
    )(a, b, *after)


def _prenorm_fwd(x, g):
    tm = 512

    def body(x_ref, g_ref, h_ref, ht_ref):
        xv = x_ref[...]
        r = lax.rsqrt(jnp.mean(xv * xv, axis=-1, keepdims=True) + EPS)
        hv = (xv * r * g_ref[...]).astype(BF16)
        h_ref[...] = hv
        ht_ref[...] = hv.T

    return pl.pallas_call(
        body, name="prenorm_fwd", grid=(SEQ // tm,),
        in_specs=[pl.BlockSpec((tm, D_MODEL), lambda i: (i, 0)), pl.BlockSpec((1, D_MODEL), lambda i: (0, 0))],
        out_specs=[pl.BlockSpec((tm, D_MODEL), lambda i: (i, 0)), pl.BlockSpec((D_MODEL, tm), lambda i: (0, i))],
        out_shape=[jax.ShapeDtypeStruct((SEQ, D_MODEL), BF16), jax.ShapeDtypeStruct((D_MODEL, SEQ), BF16)],
    )(x, g)


def _prenorm_bwd(x, dh, dy, g):
    tm = 512

    def body(x_ref, dh_ref, dy_ref, g_ref, gx_ref, dg_ref):
        xv = x_ref[...]
        r = lax.rsqrt(jnp.mean(xv * xv, axis=-1, keepdims=True) + EPS)
        n = xv * r
        dhv = dh_ref[...]
        dn = dhv * g_ref[...]
        gx_ref[...] = dy_ref[...] + r * (dn - n * jnp.mean(dn * n, axis=-1, keepdims=True))
        part = jnp.sum(dhv * n, axis=0, keepdims=True)

        @pl.when(pl.program_id(0) == 0)
        def _():
            dg_ref[...] = part

        @pl.when(pl.program_id(0) > 0)
        def _():
            dg_ref[...] += part

    row = pl.BlockSpec((tm, D_MODEL), lambda i: (i, 0))
    vec = pl.BlockSpec((1, D_MODEL), lambda i: (0, 0))
    return pl.pallas_call(
        body, name="prenorm_bwd", grid=(SEQ // tm,),
        in_specs=[row, row, row, vec], out_specs=[row, vec],
        out_shape=[jax.ShapeDtypeStruct((SEQ, D_MODEL), F32), jax.ShapeDtypeStruct((1, D_MODEL), F32)],
        compiler_params=_cparams(),
    )(x, dh, dy, g)


def _mla_prep_fwd(p, qg, kvg, wuq, wk, wv, rc, rs):
    tm = 512

    def body(lat_ref, qg_ref, kvg_ref, wuq_ref, wk_ref, wv_ref, c_ref, s_ref,
             cqn_ref, ckvn_ref, q_ref, k_ref, v_ref):
        c, s = c_ref[...], s_ref[...]
        lanes = _rope_lanes((tm, 128), MLA_ROPE_HALF, 128, 64)
        cq = lat_ref[:, 0:Q_RANK]
        r1 = lax.rsqrt(jnp.mean(cq * cq, axis=-1, keepdims=True) + EPS)
        cqn = (cq * r1 * qg_ref[...]).astype(BF16)
        cqn_ref[...] = cqn
        q = _nn(cqn, wuq_ref[...])
        for h in range(MLA_HEADS):
            sl = slice(h * 128, (h + 1) * 128)
            q_ref[:, sl] = (_rope_fwd(q[:, sl], c, s, MLA_ROPE_HALF, lanes) * MLA_SCALE).astype(BF16)
        ckv = lat_ref[:, Q_RANK:Q_RANK + KV_RANK]
        r2 = lax.rsqrt(jnp.mean(ckv * ckv, axis=-1, keepdims=True) + EPS)
        ckvn = (ckv * r2 * kvg_ref[...]).astype(BF16)
        ckvn_ref[...] = ckvn
        krr = _rope_fwd(lat_ref[:, Q_RANK + KV_RANK:N_LAT], c, s, MLA_ROPE_HALF, lanes)
        kn = _nn(ckvn, wk_ref[...])
        for h in range(MLA_HEADS):
            sl = slice(h * 128, (h + 1) * 128)
            k_ref[:, sl] = (kn[:, sl] + krr).astype(BF16)
        v_ref[...] = _nn(ckvn, wv_ref[...]).astype(BF16)

    def full(shape):
        return pl.BlockSpec(shape, lambda i: (0, 0))

    def rows(w):
        return pl.BlockSpec((tm, w), lambda i: (i, 0))

    return pl.pallas_call(
        body, name="mla_prep_fwd", grid=(SEQ // tm,),
        in_specs=[pl.BlockSpec((tm, N_LAT), lambda i: (i, COL_LAT // N_LAT)),
                  full((1, Q_RANK)), full((1, KV_RANK)), full((Q_RANK, 1024)), full((KV_RANK, 1024)),
                  full((KV_RANK, 512)), rows(128), rows(128)],
        out_specs=[rows(Q_RANK), rows(KV_RANK), rows(1024), rows(1024), rows(512)],
        out_shape=[jax.ShapeDtypeStruct((SEQ, Q_RANK), BF16), jax.ShapeDtypeStruct((SEQ, KV_RANK), BF16),
                   jax.ShapeDtypeStruct((SEQ, 1024), BF16), jax.ShapeDtypeStruct((SEQ, 1024), BF16),
                   jax.ShapeDtypeStruct((SEQ, 512), BF16)],
        compiler_params=_cparams(),
    )(p, qg, kvg, wuq, wk, wv, rc, rs)


def _mla_prep_bwd(dp_in, p, dq, dk, dv, qg, kvg, wuq, wk, wv, rc, rs):
    tm = 512

    def body(dp_any, lat_ref, dq_ref, dk_ref, dv_ref, qg_ref, kvg_ref, wuq_ref, wk_ref, wv_ref,
             c_ref, s_ref, dp_ref, dqb_ref, dkb_ref, dvb_ref, dgq_ref, dgkv_ref):
        del dp_any
        c, s = c_ref[...], s_ref[...]
        lanes = _rope_lanes((tm, 128), MLA_ROPE_HALF, 128, 64)
        lane = lax.broadcasted_iota(jnp.int32, (tm, 128), 1)
        dkr = jnp.zeros((tm, 128), F32)
        for h in range(MLA_HEADS):
            sl = slice(h * 128, (h + 1) * 128)
            dqb_ref[:, sl] = _rope_bwd(dq_ref[:, sl] * MLA_SCALE, c, s, MLA_ROPE_HALF, lanes).astype(BF16)
            dkh = dk_ref[:, sl]
            dkr = dkr + dkh
            dkb_ref[:, sl] = jnp.where(lane < 64, dkh, 0.0).astype(BF16)
        dkr = jnp.where((lane >= 64) & (lane < 96), dkr, 0.0)
        dkr = _rope_bwd(dkr, c, s, MLA_ROPE_HALF, lanes)
        dvb = dv_ref[...].astype(BF16)
        dvb_ref[...] = dvb

        cq = lat_ref[:, 0:Q_RANK]
        r1 = lax.rsqrt(jnp.mean(cq * cq, axis=-1, keepdims=True) + EPS)
        n1 = cq * r1
        dcqn = _nt(dqb_ref[...], wuq_ref[...])
        dn1 = dcqn * qg_ref[...]
        dcq = r1 * (dn1 - n1 * jnp.mean(dn1 * n1, axis=-1, keepdims=True))
        pq = jnp.sum(dcqn * n1, axis=0, keepdims=True)

        ckv = lat_ref[:, Q_RANK:Q_RANK + KV_RANK]
        r2 = lax.rsqrt(jnp.mean(ckv * ckv, axis=-1, keepdims=True) + EPS)
        n2 = ckv * r2
        dckvn = _nt(dkb_ref[...], wk_ref[...]) + _nt(dvb, wv_ref[...])
        dn2 = dckvn * kvg_ref[...]
        dckv = r2 * (dn2 - n2 * jnp.mean(dn2 * n2, axis=-1, keepdims=True))
        pkv = jnp.sum(dckvn * n2, axis=0, keepdims=True)

        dp_ref[:, 0:Q_RANK] = dcq.astype(BF16)
        dp_ref[:, Q_RANK:Q_RANK + KV_RANK] = dckv.astype(BF16)
        dp_ref[:, Q_RANK + KV_RANK:N_LAT] = dkr.astype(BF16)

        @pl.when(pl.program_id(0) == 0)
        def _():
            dgq_ref[...] = pq
            dgkv_ref[...] = pkv

        @pl.when(pl.program_id(0) > 0)
        def _():
            dgq_ref[...] += pq
            dgkv_ref[...] += pkv

    def full(shape):
        return pl.BlockSpec(shape, lambda i: (0, 0))

    def rows(w):
        return pl.BlockSpec((tm, w), lambda i: (i, 0))

    lat = pl.BlockSpec((tm, N_LAT), lambda i: (i, COL_LAT // N_LAT))
    return pl.pallas_call(
        body, name="mla_prep_bwd", grid=(SEQ // tm,),
        in_specs=[pl.BlockSpec(memory_space=pl.ANY), lat, rows(1024), rows(1024), rows(512),
                  full((1, Q_RANK)), full((1, KV_RANK)), full((Q_RANK, 1024)), full((KV_RANK, 1024)),
                  full((KV_RANK, 512)), rows(128), rows(128)],
        out_specs=[lat, rows(1024), rows(1024), rows(512), full((1, Q_RANK)), full((1, KV_RANK))],
        out_shape=[jax.ShapeDtypeStruct((SEQ, N_PAD), BF16), jax.ShapeDtypeStruct((SEQ, 1024), BF16),
                   jax.ShapeDtypeStruct((SEQ, 1024), BF16), jax.ShapeDtypeStruct((SEQ, 512), BF16),
                   jax.ShapeDtypeStruct((1, Q_RANK), F32), jax.ShapeDtypeStruct((1, KV_RANK), F32)],
        input_output_aliases={0: 0},
        compiler_params=_cparams(),
    )(dp_in, p, dq, dk, dv, qg, kvg, wuq, wk, wv, rc, rs)


FLASH_T = 512


def _head_half(shape, hh):
    lane = lax.broadcasted_iota(jnp.int32, shape, 1)
    return (lane < 64) if hh == 0 else (lane >= 64)


def _causal_keep(t):
    row = lax.broadcasted_iota(jnp.int32, (t, t), 0)
    col = lax.broadcasted_iota(jnp.int32, (t, t), 1)
    return row >= col


def _tri_steps(nb, q_major):
    if q_major:
        pairs = [(i, kb) for i in range(nb) for kb in range(i + 1)]
    else:
        pairs = [(i, kb) for kb in range(nb) for i in range(kb, nb)]
    return jnp.asarray([p[0] for p in pairs], jnp.int32), jnp.asarray([p[1] for p in pairs], jnp.int32)


def _mla_flash_fwd(q, k, v):
    t = FLASH_T
    nb = SEQ // t
    qtab, ktab = _tri_steps(nb, True)

    def body(qi_ref, ki_ref, q_ref, k_ref, v_ref, o_ref, lse_ref, m_scr, l_scr, acc_scr):
        step = pl.program_id(1)
        i, kb = qi_ref[step], ki_ref[step]

        @pl.when(kb == 0)
        def _():
            m_scr[...] = jnp.full_like(m_scr, NEG)
            l_scr[...] = jnp.zeros_like(l_scr)
            acc_scr[...] = jnp.zeros_like(acc_scr)

        def update(masked):
            vv = v_ref[...]
            for hh in range(2):
                sl = slice(hh * 128, (hh + 1) * 128)
                s = _nt(q_ref[:, sl], k_ref[:, sl])
                if masked:
                    s = jnp.where(_causal_keep(t), s, NEG)
                m_prev = m_scr[hh]
                m_new = jnp.maximum(m_prev, jnp.max(s, axis=-1, keepdims=True))
                pr = jnp.exp(s - jnp.tile(m_new, (1, t // 128)))
                alpha = jnp.exp(m_prev - m_new)
                l_scr[hh] = alpha * l_scr[hh] + jnp.sum(pr, axis=-1, keepdims=True)
                acc_scr[hh] = alpha * acc_scr[hh] + _nn(pr.astype(BF16), vv)
                m_scr[hh] = m_new

        @pl.when(kb < i)
        def _():
            update(False)

        @pl.when(kb == i)
        def _():
            update(True)
            o0 = acc_scr[0] / l_scr[0]
            o1 = acc_scr[1] / l_scr[1]
            o_ref[...] = jnp.where(_head_half((t, 128), 0), o0, o1)
            for hh in range(2):
                lse_ref[:, hh * 128:(hh + 1) * 128] = m_scr[hh] + jnp.log(l_scr[hh])

    grid_spec = pltpu.PrefetchScalarGridSpec(
        num_scalar_prefetch=2, grid=(4, qtab.shape[0]),
        in_specs=[pl.BlockSpec((t, 256), lambda j, s, qi, ki: (qi[s], j)),
                  pl.BlockSpec((t, 256), lambda j, s, qi, ki: (ki[s], j)),
                  pl.BlockSpec((t, 128), lambda j, s, qi, ki: (ki[s], j))],
        out_specs=[pl.BlockSpec((t, 128), lambda j, s, qi, ki: (qi[s], j)),
                   pl.BlockSpec((t, 256), lambda j, s, qi, ki: (qi[s], j))],
        scratch_shapes=[pltpu.VMEM((2, t, 128), F32), pltpu.VMEM((2, t, 128), F32), pltpu.VMEM((2, t, 128), F32)])
    return pl.pallas_call(
        body, name="mla_flash_fwd", grid_spec=grid_spec,
        out_shape=[jax.ShapeDtypeStruct((SEQ, 512), F32), jax.ShapeDtypeStruct((SEQ, 1024), F32)],
        compiler_params=_cparams(),
    )(qtab, ktab, q, k, v)


def _mla_flash_bwd(q, k, v, o, do, lse):
    t = FLASH_T
    nb = SEQ // t
    qtab, ktab = _tri_steps(nb, False)

    def body(qi_ref, ki_ref, q_ref, k_ref, v_ref, o_ref, do_ref, lse_ref, dq_ref, dk_ref, dv_ref, dk_scr, dv_scr):
        step = pl.program_id(1)
        i, kb = qi_ref[step], ki_ref[step]

        @pl.when(step == 0)
        def _():
            dq_ref[...] = jnp.zeros_like(dq_ref)

        @pl.when(i == kb)
        def _():
            dk_scr[...] = jnp.zeros_like(dk_scr)
            dv_scr[...] = jnp.zeros_like(dv_scr)

        def update(masked):
            vv = v_ref[...]
            ov = o_ref[...]
            dov = do_ref[...]
            rows = pl.ds(pl.multiple_of(i * t, t), t)
            for hh in range(2):
                sl = slice(hh * 128, (hh + 1) * 128)
                qh, kh = q_ref[:, sl], k_ref[:, sl]
                s = _nt(qh, kh)
                if masked:
                    s = jnp.where(_causal_keep(t), s, NEG)
                pr = jnp.exp(s - jnp.tile(lse_ref[:, sl], (1, t // 128)))
                dom = jnp.where(_head_half((t, 128), hh), dov, 0.0)
                domb = dom.astype(BF16)
                dv_scr[...] += _tn(pr.astype(BF16), domb)
                dpr = _nt(domb, vv)
                delta = jnp.sum(dom * ov, axis=-1, keepdims=True)
                ds = (pr * (dpr - delta)).astype(BF16)
                dq_ref[rows, sl] += _nn(ds, kh)
                dk_scr[hh] += _tn(ds, qh)

        @pl.when(i > kb)
        def _():
            update(False)

        @pl.when(i == kb)
        def _():
            update(True)

        @pl.when(i == nb - 1)
        def _():
            dk_ref[:, 0:128] = dk_scr[0]
            dk_ref[:, 128:256] = dk_scr[1]
            dv_ref[...] = dv_scr[...]

    qi_map = lambda j, s, qi, ki: (qi[s], j)
    ki_map = lambda j, s, qi, ki: (ki[s], j)
    grid_spec = pltpu.PrefetchScalarGridSpec(
        num_scalar_prefetch=2, grid=(4, qtab.shape[0]),
        in_specs=[pl.BlockSpec((t, 256), qi_map), pl.BlockSpec((t, 256), ki_map), pl.BlockSpec((t, 128), ki_map),
                  pl.BlockSpec((t, 128), qi_map), pl.BlockSpec((t, 128), qi_map), pl.BlockSpec((t, 256), qi_map)],
        out_specs=[pl.BlockSpec((SEQ, 256), lambda j, s, qi, ki: (0, j)), pl.BlockSpec((t, 256), ki_map),
                   pl.BlockSpec((t, 128), ki_map)],
        scratch_shapes=[pltpu.VMEM((2, t, 128), F32), pltpu.VMEM((t, 128), F32)])
    return pl.pallas_call(
        body, name="mla_flash_bwd", grid_spec=grid_spec,
        out_shape=[jax.ShapeDtypeStruct((SEQ, 1024), F32), jax.ShapeDtypeStruct((SEQ, 1024), F32),
                   jax.ShapeDtypeStruct((SEQ, 512), F32)],
        compiler_params=_cparams(),
    )(qtab, ktab, q, k, v, o, do, lse)


DIL_UNROLL = 4


def _strided(start, size, d):
    return pl.ds(start, size) if d == 1 else pl.ds(start, size, stride=d)


def _dil_prep_fwd(p, rc, rs, g):
    d = DIL_DILATIONS[g]
    sub_len = SEQ // d
    ch = min(sub_len, 512)

    def body(p_ref, c_ref, s_ref, o_ref):
        tq = pl.program_id(0)
        is_v = tq == 2
        mult = jnp.where(tq == 0, DIL_SCALE, 1.0).astype(F32)
        lanes = _rope_lanes((ch, 128), DIL_ROPE_HALF, 64, 0)
        o_ref[0, 0:BAND, :] = jnp.zeros((BAND, 128), BF16)
        for r in range(d):
            for c0 in range(0, sub_len, ch):
                rows = _strided(r + c0 * d, ch, d)
                xv = p_ref[rows, :]
                roped = _rope_fwd(xv, c_ref[rows, :], s_ref[rows, :], DIL_ROPE_HALF, lanes)
                at = BAND + r * sub_len + c0
                o_ref[0, at:at + ch, :] = (jnp.where(is_v, xv, roped) * mult).astype(BF16)

    tab = pl.BlockSpec((SEQ, 128), lambda tq, pr: (0, 0))
    return pl.pallas_call(
        body, name=f"dil_prep_fwd_g{g}", grid=(3, 4),
        in_specs=[pl.BlockSpec((SEQ, 128), lambda tq, pr: (0, COL_QKV // 128 + (tq * 3 + g) * 4 + pr)), tab, tab],
        out_specs=pl.BlockSpec((1, BAND + SEQ, 128), lambda tq, pr: (tq, 0, pr)),
        out_shape=jax.ShapeDtypeStruct((3, BAND + SEQ, 512), BF16),
        compiler_params=_cparams(),
    )(p, rc, rs)


DIL_ST = 1024
DIL_NB = DIL_ST // BAND


def _band_keep(g, b, t):
    nbs = SEQ // DIL_DILATIONS[g] // BAND
    row = lax.broadcasted_iota(jnp.int32, (BAND, 2 * BAND), 0)
    col = lax.broadcasted_iota(jnp.int32, (BAND, 2 * BAND), 1)
    cur = (col >= BAND) & (row >= col - BAND)
    prev = (col < BAND) & (col >= row)
    if nbs >= DIL_NB:
        if b > 0:
            return cur | prev
        return cur | (prev & ((t * DIL_NB) % nbs != 0))
    return cur | prev if b % nbs else cur


def _dil_tok(g, b, t):
    d = DIL_DILATIONS[g]
    nbs = SEQ // d // BAND
    gb = t * DIL_NB + b
    return _strided((gb % nbs) * BAND * d + gb // nbs, BAND, d)


def _dil_attn_fwd2(qkv, g):
    def body(q_ref, k_ref, v_ref, o_ref, l_ref, s_scr, p_scr, o_scr):
        t = pl.program_id(1)
        base = t * DIL_ST
        half0 = _head_half((DIL_ST, 128), 0)
        lse_h = []
        for hh in range(2):
            half = _head_half((BAND, 128), hh)
            for b in range(DIL_NB):
                qv = q_ref[0, pl.ds(pl.multiple_of(base + (b + 1) * BAND, BAND), BAND), :]
                k2 = k_ref[0, pl.ds(pl.multiple_of(base + b * BAND, BAND), 2 * BAND), :]
                sb = _nt(jnp.where(half, qv, jnp.zeros_like(qv)), k2)
                s_scr[b * BAND:(b + 1) * BAND, :] = jnp.where(_band_keep(g, b, t), sb, NEG)
            s = s_scr[...]
            m = jnp.max(s, axis=-1, keepdims=True)
            pr = jnp.exp(s - m)
            den = jnp.sum(pr, axis=-1, keepdims=True)
            p_scr[...] = pr.astype(BF16)
            for b in range(DIL_NB):
                v2 = v_ref[0, pl.ds(pl.multiple_of(base + b * BAND, BAND), 2 * BAND), :]
                o_scr[hh, b * BAND:(b + 1) * BAND, :] = _nn(p_scr[b * BAND:(b + 1) * BAND, :], v2)
            o_scr[hh] = o_scr[hh] / den
            lse_h.append(m + jnp.log(den))
        out = jnp.where(half0, o_scr[0], o_scr[1])
        lse = jnp.where(half0, lse_h[0], lse_h[1])
        for b in range(DIL_NB):
            tok = _dil_tok(g, b, t)
            o_ref[tok, :] = out[b * BAND:(b + 1) * BAND, :]
            l_ref[tok, :] = lse[b * BAND:(b + 1) * BAND, :]

    def inp(tq):
        return pl.BlockSpec((1, BAND + SEQ, 128), lambda pr, t: (tq, 0, pr))

    out = pl.BlockSpec((SEQ, 128), lambda pr, t: (0, pr))
    return pl.pallas_call(
        body, name=f"dil_attn_fwd_g{g}", grid=(4, SEQ // DIL_ST),
        in_specs=[inp(0), inp(1), inp(2)], out_specs=[out, out],
        out_shape=[jax.ShapeDtypeStruct((SEQ, 512), F32), jax.ShapeDtypeStruct((SEQ, 512), F32)],
        scratch_shapes=[pltpu.VMEM((DIL_ST, 2 * BAND), F32), pltpu.VMEM((DIL_ST, 2 * BAND), BF16),
                        pltpu.VMEM((2, DIL_ST, 128), F32)],
        compiler_params=_cparams(),
    )(qkv, qkv, qkv)


def _dil_attn_bwd2(qkv, dyd, yd, lse_all, g, token=None):
    d = DIL_DILATIONS[g]
    sub_len = SEQ // d
    nst = SEQ // DIL_ST
    after, after_specs = _after(token)

    def body(q_ref, k_ref, v_ref, do_ref, y_ref, l_ref, *rest):
        out_ref, dk_scr, dv_scr, s_scr, dp_scr, p_scr, ds_scr, do_scr, y_scr, l_scr, dq_scr = rest[-11:]
        t = pl.program_id(1)
        base = t * DIL_ST

        @pl.when(t == 0)
        def _():
            dk_scr[...] = jnp.zeros_like(dk_scr)
            dv_scr[...] = jnp.zeros_like(dv_scr)

        for b in range(DIL_NB):
            tok = _dil_tok(g, b, t)
            do_scr[b * BAND:(b + 1) * BAND, :] = do_ref[tok, :]
            y_scr[b * BAND:(b + 1) * BAND, :] = y_ref[tok, :]
            l_scr[b * BAND:(b + 1) * BAND, :] = l_ref[tok, :]
        for hh in range(2):
            half = _head_half((BAND, 128), hh)
            half_st = _head_half((DIL_ST, 128), hh)
            dom = jnp.where(half_st, do_scr[...], 0.0)
            delta = jnp.sum(dom * y_scr[...], axis=-1, keepdims=True)
            lcol = jnp.max(jnp.where(half_st, l_scr[...], NEG), axis=-1, keepdims=True)
            for b in range(DIL_NB):
                rows = slice(b * BAND, (b + 1) * BAND)
                qv = q_ref[0, pl.ds(pl.multiple_of(base + (b + 1) * BAND, BAND), BAND), :]
                band = pl.ds(pl.multiple_of(base + b * BAND, BAND), 2 * BAND)
                sb = _nt(jnp.where(half, qv, jnp.zeros_like(qv)), k_ref[0, band, :])
                s_scr[rows, :] = jnp.where(_band_keep(g, b, t), sb, NEG)
                dp_scr[rows, :] = _nt(dom[rows, :].astype(BF16), v_ref[0, band, :])
            pr = jnp.exp(s_scr[...] - lcol)
            p_scr[...] = pr.astype(BF16)
            ds_scr[...] = (pr * (dp_scr[...] - delta)).astype(BF16)
            for b in range(DIL_NB):
                rows = slice(b * BAND, (b + 1) * BAND)
                qv = q_ref[0, pl.ds(pl.multiple_of(base + (b + 1) * BAND, BAND), BAND), :]
                band = pl.ds(pl.multiple_of(base + b * BAND, BAND), 2 * BAND)
                dqb = jnp.where(half, _nn(ds_scr[rows, :], k_ref[0, band, :]), 0.0)
                if hh == 0:
                    dq_scr[rows, :] = dqb
                else:
                    dq_scr[rows, :] += dqb
                half2 = _head_half((2 * BAND, 128), hh)
                dk_scr[band, :] += jnp.where(half2, _tn(ds_scr[rows, :], qv), 0.0)
                dv_scr[band, :] += _tn(p_scr[rows, :], dom[rows, :].astype(BF16))
        for b in range(DIL_NB):
            out_ref[pl.ds(0, 1), _dil_tok(g, b, t), :] = dq_scr[b * BAND:(b + 1) * BAND, :][None]

        @pl.when(t == nst - 1)
        def _():
            for r in range(d):
                rows = _strided(r, sub_len, d)
                out_ref[pl.ds(1, 1), rows, :] = dk_scr[BAND + r * sub_len:BAND + (r + 1) * sub_len, :][None]
                out_ref[pl.ds(2, 1), rows, :] = dv_scr[BAND + r * sub_len:BAND + (r + 1) * sub_len, :][None]

    def inp(tq):
        return pl.BlockSpec((1, BAND + SEQ, 128), lambda pr, t: (tq, 0, pr))

    tok_spec = pl.BlockSpec((SEQ, 128), lambda pr, t: (0, pr))
    st = (DIL_ST, 2 * BAND)
    return pl.pallas_call(
        body, name=f"dil_attn_bwd_g{g}", grid=(4, nst),
        in_specs=[inp(0), inp(1), inp(2), tok_spec, tok_spec, tok_spec] + after_specs,
        out_specs=pl.BlockSpec((3, SEQ, 128), lambda pr, t: (0, 0, pr)),
        out_shape=jax.ShapeDtypeStruct((3, SEQ, 512), F32),
        scratch_shapes=[pltpu.VMEM((BAND + SEQ, 128), F32), pltpu.VMEM((BAND + SEQ, 128), F32),
                        pltpu.VMEM(st, F32), pltpu.VMEM(st, F32), pltpu.VMEM(st, BF16), pltpu.VMEM(st, BF16),
                        pltpu.VMEM((DIL_ST, 128), F32), pltpu.VMEM((DIL_ST, 128), F32), pltpu.VMEM((DIL_ST, 128), F32),
                        pltpu.VMEM((DIL_ST, 128), F32)],
        compiler_params=_cparams(),
    )(qkv, qkv, qkv, dyd, yd, lse_all, *after)


def _band_masks():
    row = lax.broadcasted_iota(jnp.int32, (BAND, BAND), 0)
    col = lax.broadcasted_iota(jnp.int32, (BAND, BAND), 1)
    return row >= col, col >= row


def _dil_attn_fwd(qkv, g):
    d = DIL_DILATIONS[g]
    nbs = SEQ // d // BAND
    nblk = SEQ // BAND

    def body(q_ref, k_ref, v_ref, o_ref, l_ref):
        keep_c, keep_p = _band_masks()
        half0 = _head_half((BAND, 128), 0)

        def step(i, carry):
            cur = pl.ds(pl.multiple_of(i * BAND, BAND), BAND)
            prv = pl.ds(pl.multiple_of(jnp.maximum(i - 1, 0) * BAND, BAND), BAND)
            has_prev = (i % nbs) != 0
            qv = q_ref[0, cur, :]
            kc, kp = k_ref[0, cur, :], k_ref[0, prv, :]
            vc, vp = v_ref[0, cur, :], v_ref[0, prv, :]
            outs, lses = [], []
            for hh in range(2):
                qm = jnp.where(_head_half((BAND, 128), hh), qv, jnp.zeros_like(qv))
                sc = jnp.where(keep_c, _nt(qm, kc), NEG)
                sp = jnp.where(keep_p & has_prev, _nt(qm, kp), NEG)
                m = jnp.maximum(jnp.max(sc, axis=-1, keepdims=True), jnp.max(sp, axis=-1, keepdims=True))
                pc, pp = jnp.exp(sc - m), jnp.exp(sp - m)
                den = jnp.sum(pc, axis=-1, keepdims=True) + jnp.sum(pp, axis=-1, keepdims=True)
                o = (_nn(pc.astype(BF16), vc) + _nn(pp.astype(BF16), vp)) / den
                outs.append(o)
                lses.append(jnp.broadcast_to(m + jnp.log(den), (BAND, 128)))
            tok = _strided((i % nbs) * BAND * d + i // nbs, BAND, d)
            o_ref[tok, :] = jnp.where(half0, outs[0], outs[1])
            l_ref[tok, :] = jnp.where(half0, lses[0], lses[1])
            return carry

        lax.fori_loop(0, nblk, step, 0, unroll=DIL_UNROLL)

    def inp(tq):
        return pl.BlockSpec((1, SEQ, 128), lambda pr: (tq, 0, pr))

    out = pl.BlockSpec((SEQ, 128), lambda pr: (0, pr))
    return pl.pallas_call(
        body, name=f"dil_attn_fwd_g{g}", grid=(4,),
        in_specs=[inp(0), inp(1), inp(2)], out_specs=[out, out],
        out_shape=[jax.ShapeDtypeStruct((SEQ, 512), F32), jax.ShapeDtypeStruct((SEQ, 512), F32)],
        compiler_params=_cparams(),
    )(qkv, qkv, qkv)


def _dil_attn_bwd(qkv, dyd, yd, lse_all, g):
    d = DIL_DILATIONS[g]
    sub_len = SEQ // d
    nbs = sub_len // BAND
    nblk = SEQ // BAND

    def body(q_ref, k_ref, v_ref, do_ref, y_ref, l_ref, out_ref, dk_scr, dv_scr):
        keep_c, keep_p = _band_masks()
        dk_scr[...] = jnp.zeros_like(dk_scr)
        dv_scr[...] = jnp.zeros_like(dv_scr)

        def step(i, carry):
            cur = pl.ds(pl.multiple_of(i * BAND, BAND), BAND)
            prv = pl.ds(pl.multiple_of(jnp.maximum(i - 1, 0) * BAND, BAND), BAND)
            has_prev = (i % nbs) != 0
            tok = _strided((i % nbs) * BAND * d + i // nbs, BAND, d)
            qv = q_ref[0, cur, :]
            kc, kp = k_ref[0, cur, :], k_ref[0, prv, :]
            vc, vp = v_ref[0, cur, :], v_ref[0, prv, :]
            dov, yv, lv = do_ref[tok, :], y_ref[tok, :], l_ref[tok, :]
            dq = jnp.zeros((BAND, 128), F32)
            dkc = jnp.zeros((BAND, 128), F32)
            dkp = jnp.zeros((BAND, 128), F32)
            dvc = jnp.zeros((BAND, 128), F32)
            dvp = jnp.zeros((BAND, 128), F32)
            for hh in range(2):
                half = _head_half((BAND, 128), hh)
                qm = jnp.where(half, qv, jnp.zeros_like(qv))
                lcol = jnp.max(jnp.where(half, lv, NEG), axis=-1, keepdims=True)
                pc = jnp.exp(jnp.where(keep_c, _nt(qm, kc), NEG) - lcol)
                pp = jnp.exp(jnp.where(keep_p & has_prev, _nt(qm, kp), NEG) - lcol)
                dom = jnp.where(half, dov, 0.0)
                domb = dom.astype(BF16)
                delta = jnp.sum(dom * yv, axis=-1, keepdims=True)
                dsc = (pc * (_nt(domb, vc) - delta)).astype(BF16)
                dsp = (pp * (_nt(domb, vp) - delta)).astype(BF16)
                dvc = dvc + _tn(pc.astype(BF16), domb)
                dvp = dvp + _tn(pp.astype(BF16), domb)
                dq = dq + jnp.where(half, _nn(dsc, kc) + _nn(dsp, kp), 0.0)
                dkc = dkc + jnp.where(half, _tn(dsc, qv), 0.0)
                dkp = dkp + jnp.where(half, _tn(dsp, qv), 0.0)
            out_ref[pl.ds(0, 1), tok, :] = dq[None]
            dk_scr[cur, :] += dkc
            dk_scr[prv, :] += dkp
            dv_scr[cur, :] += dvc
            dv_scr[prv, :] += dvp
            return carry

        lax.fori_loop(0, nblk, step, 0, unroll=DIL_UNROLL)
        for r in range(d):
            rows = _strided(r, sub_len, d)
            out_ref[pl.ds(1, 1), rows, :] = dk_scr[r * sub_len:(r + 1) * sub_len, :][None]
            out_ref[pl.ds(2, 1), rows, :] = dv_scr[r * sub_len:(r + 1) * sub_len, :][None]

    def inp(tq):
        return pl.BlockSpec((1, SEQ, 128), lambda pr: (tq, 0, pr))

    tok_spec = pl.BlockSpec((SEQ, 128), lambda pr: (0, pr))
    return pl.pallas_call(
        body, name=f"dil_attn_bwd_g{g}", grid=(4,),
        in_specs=[inp(0), inp(1), inp(2), tok_spec, tok_spec, tok_spec],
        out_specs=pl.BlockSpec((3, SEQ, 128), lambda pr: (0, 0, pr)),
        out_shape=jax.ShapeDtypeStruct((3, SEQ, 512), F32),
        scratch_shapes=[pltpu.VMEM((SEQ, 128), F32), pltpu.VMEM((SEQ, 128), F32)],
        compiler_params=_cparams(),
    )(qkv, qkv, qkv, dyd, yd, lse_all)


def _dil_prep_bwd(dp_in, dqkv, rc, rs, g):
    tm = 1024

    def body(dp_any, g_ref, c_ref, s_ref, dp_ref):
        del dp_any
        tq = pl.program_id(0)

        @pl.when(tq == 2)
        def _():
            dp_ref[...] = g_ref[0].astype(BF16)

        @pl.when(tq < 2)
        def _():
            mult = jnp.where(tq == 0, DIL_SCALE, 1.0).astype(F32)
            lanes = _rope_lanes((tm, 128), DIL_ROPE_HALF, 64, 0)
            cv, sv = c_ref[...], s_ref[...] * mult
            cv = cv * mult
            for pr in range(4):
                gv = g_ref[0, :, pr * 128:(pr + 1) * 128]
                dp_ref[:, pr * 128:(pr + 1) * 128] = _rope_bwd(gv, cv, sv, DIL_ROPE_HALF, lanes).astype(BF16)

    tab = pl.BlockSpec((tm, 128), lambda tq, i: (i, 0))
    return pl.pallas_call(
        body, name=f"dil_prep_bwd_g{g}", grid=(3, SEQ // tm),
        in_specs=[pl.BlockSpec(memory_space=pl.ANY),
                  pl.BlockSpec((1, tm, 512), lambda tq, i: (tq, i, 0)), tab, tab],
        out_specs=pl.BlockSpec((tm, 512), lambda tq, i: (i, COL_QKV // 512 + tq * 3 + g)),
        out_shape=jax.ShapeDtypeStruct((SEQ, N_PAD), BF16),
        input_output_aliases={0: 0},
    )(dp_in, dqkv, rc, rs)


TAIL_T = 256


def _tail(p, ya, o_g, l_g, x, target, wpm, wpd, wout, post_g):
    tm = TAIL_T

    def body(pgz_ref, ya_ref, o0_ref, o1_ref, o2_ref, l0_ref, l1_ref, l2_ref, x_ref, t_ref,
             wpm_ref, wpd_ref, wout_ref, pg_ref,
             dp_ref, dy_ref, mg_ref, dt_ref, ua_ref, dpa_ref, ud_ref, dpd_ref, dya_ref, dyd_ref,
             yd_ref, lse_ref, loss_ref, dgp_ref):
        l0, l1, l2 = l0_ref[...], l1_ref[...], l2_ref[...]
        mx = jnp.maximum(jnp.maximum(l0, l1), l2)
        e0, e1, e2 = jnp.exp(l0 - mx), jnp.exp(l1 - mx), jnp.exp(l2 - mx)
        den = e0 + e1 + e2
        yd = (e0 * o0_ref[...] + e1 * o1_ref[...] + e2 * o2_ref[...]) / den
        yd_ref[...] = yd
        lse_ref[...] = mx + jnp.log(den)
        ya = ya_ref[...]

        gm, gd = pgz_ref[:, 0:1024], pgz_ref[:, 1024:2048]
        zm, zd = pgz_ref[:, 2048:2560], pgz_ref[:, 2560:3072]
        szm, szd = _sigmoid(zm), _sigmoid(zd)
        sm, sd = zm * szm, zd * szd
        ua = (ya * sm).astype(BF16)
        ud = (yd * sd).astype(BF16)
        ua_ref[...] = ua
        ud_ref[...] = ud
        pa = _nn(ua, wpm_ref[...])
        pd = _nn(ud, wpd_ref[...])
        sgm, sgd = _sigmoid(gm), _sigmoid(gd)
        mg = (sgm * pa + sgd * pd).astype(BF16)
        mg_ref[...] = mg
        t = _nn(mg, wout_ref[...])
        r3 = lax.rsqrt(jnp.mean(t * t, axis=-1, keepdims=True) + EPS)
        n = t * r3
        pg = pg_ref[...]
        err = x_ref[...] + n * pg - t_ref[...]
        lpart = jnp.sum(err * err, axis=0, keepdims=True)

        dy = err * (1.0 / D_MODEL)
        dy_ref[...] = dy
        gpart = jnp.sum(dy * n, axis=0, keepdims=True)
        dn = dy * pg
        dt = (r3 * (dn - n * jnp.mean(dn * n, axis=-1, keepdims=True))).astype(BF16)
        dt_ref[...] = dt
        dmg = _nt(dt, wout_ref[...])
        dpa = (dmg * sgm).astype(BF16)
        dpd = (dmg * sgd).astype(BF16)
        dpa_ref[...] = dpa
        dpd_ref[...] = dpd
        dp_ref[:, 0:1024] = (dmg * pa * sgm * (1.0 - sgm)).astype(BF16)
        dp_ref[:, 1024:2048] = (dmg * pd * sgd * (1.0 - sgd)).astype(BF16)
        dua = _nt(dpa, wpm_ref[...])
        dud = _nt(dpd, wpd_ref[...])
        dya_ref[...] = dua * sm
        dyd_ref[...] = dud * sd
        dp_ref[:, 2048:2560] = (dua * ya * szm * (1.0 + zm * (1.0 - szm))).astype(BF16)
        dp_ref[:, 2560:3072] = (dud * yd * szd * (1.0 + zd * (1.0 - szd))).astype(BF16)

        @pl.when(pl.program_id(0) == 0)
        def _():
            loss_ref[...] = lpart
            dgp_ref[...] = gpart

        @pl.when(pl.program_id(0) > 0)
        def _():
            loss_ref[...] += lpart
            dgp_ref[...] += gpart

    def rows(w):
        return pl.BlockSpec((tm, w), lambda i: (i, 0))

    def full(shape):
        return pl.BlockSpec(shape, lambda i: (0, 0))

    def sds(w, dt):
        return jax.ShapeDtypeStruct((SEQ, w), dt)

    return pl.pallas_call(
        body, name="tail", grid=(SEQ // tm,),
        in_specs=[rows(3072), rows(512), rows(512), rows(512), rows(512), rows(512), rows(512), rows(512),
                  rows(1024), rows(1024), full((512, 1024)), full((512, 1024)), full((1024, 1024)), full((1, 1024))],
        out_specs=[rows(3072), rows(1024), rows(1024), rows(1024), rows(512), rows(1024), rows(512), rows(1024),
                   rows(512), rows(512), rows(512), rows(512), full((1, 1024)), full((1, 1024))],
        out_shape=[sds(N_PAD, BF16), sds(1024, F32), sds(1024, BF16), sds(1024, BF16), sds(512, BF16),
                   sds(1024, BF16), sds(512, BF16), sds(1024, BF16), sds(512, F32), sds(512, F32),
                   sds(512, F32), sds(512, F32),
                   jax.ShapeDtypeStruct((1, 1024), F32), jax.ShapeDtypeStruct((1, 1024), F32)],
        compiler_params=_cparams(),
    )(p, ya, o_g[0], o_g[1], o_g[2], l_g[0], l_g[1], l_g[2], x, target, wpm, wpd, wout, post_g)


def _sum_parts(recv, own, me, tr, name):
    n, r, w = recv.shape
    own_spec = (pl.BlockSpec((tr, w), lambda i, me_ref: (i, 0)) if own.ndim == 2
                else pl.BlockSpec((None, tr, w), lambda i, me_ref: (me_ref[0], i, 0)))

    def body(me_ref, p_ref, own_ref, o_ref):
        mine = own_ref[...].astype(F32)
        acc = jnp.zeros((tr, w), F32)
        for s in range(n):
            acc = acc + jnp.where(me_ref[0] == s, mine, p_ref[s].astype(F32))
        o_ref[...] = acc

    return pl.pallas_call(
        body, name=name,
        grid_spec=pltpu.PrefetchScalarGridSpec(
            num_scalar_prefetch=1, grid=(r // tr,),
            in_specs=[pl.BlockSpec((n, tr, w), lambda i, me_ref: (0, i, 0)), own_spec],
            out_specs=pl.BlockSpec((tr, w), lambda i, me_ref: (i, 0))),
        out_shape=jax.ShapeDtypeStruct((r, w), F32),
    )(me.reshape(1), recv, own)


def _adamw(w, g, m, v, name):
    lead = w.shape[:-2]
    r, c = w.shape[-2:]
    tr = max([t for t in range(8, 257, 8) if r % t == 0], default=r)
    c1 = 1.0 - ADAM_B1 ** ADAM_STEP
    c2 = 1.0 - ADAM_B2 ** ADAM_STEP

    def body(w_ref, g_ref, m_ref, v_ref, d_ref, nm_ref, nv_ref):
        gv = g_ref[...]
        nm = ADAM_B1 * m_ref[...] + (1.0 - ADAM_B1) * gv
        nv = ADAM_B2 * v_ref[...] + (1.0 - ADAM_B2) * (gv * gv)
        nm_ref[...] = nm
        nv_ref[...] = nv
        d_ref[...] = -ADAM_LR * ((nm / c1) / (jnp.sqrt(nv / c2) + ADAM_EPS) + ADAM_WD * w_ref[...])

    zeros = (0,) * len(lead)
    spec = pl.BlockSpec((1,) * len(lead) + (tr, c), lambda i: zeros + (i, 0))
    sd = jax.ShapeDtypeStruct(w.shape, F32)
    return pl.pallas_call(
        body, name=name, grid=(r // tr,),
        in_specs=[spec] * 4, out_specs=[spec] * 3, out_shape=[sd] * 3,
    )(w, g, m, v)


ANY = pl.BlockSpec(memory_space=pl.ANY)


def _my_place():
    return lax.axis_index("x"), lax.axis_index("y"), lax.axis_index("c")


def _allgather_weights(mats):
    def body(*refs):
        w_refs, out_refs = refs[:N_MATS], refs[N_MATS:2 * N_MATS]
        send_sems, recv_sems = refs[2 * N_MATS:]
        x, y, c = _my_place()
        sibling = (x, y, 1 - c)
        chips = [(1 - x, y), (x, 1 - y), (1 - x, 1 - y)]

        def copy(k, src, dst, to):
            return pltpu.make_async_remote_copy(src_ref=src, dst_ref=dst, send_sem=send_sems.at[k],
                                                recv_sem=recv_sems.at[k], device_id=to, device_id_type=MESH)

        def half(mi, shard, hc):
            hr = SHARD_SHAPES[mi][0] // 2
            return out_refs[mi].at[shard, pl.ds(pl.multiple_of(hc * hr, 16), hr), :]

        started = []
        for mi in range(N_MATS):
            hr = SHARD_SHAPES[mi][0] // 2
            my_half = w_refs[mi].at[pl.ds(pl.multiple_of(c * hr, 16), hr), :]
            for j, (cx, cy) in enumerate(chips):
                cp = copy(mi * 6 + j, my_half, half(mi, 2 * x + y, c), (cx, cy, c))
                cp.start()
                started.append(cp)
        for mi in range(N_MATS):
            for j, (cx, cy) in enumerate(chips):
                landed = half(mi, 2 * cx + cy, c)
                copy(mi * 6 + j, landed, landed, (cx, cy, c)).wait_recv()
                fw = copy(mi * 6 + 3 + j, landed, landed, sibling)
                fw.start()
                started.append(fw)
        for mi in range(N_MATS):
            for j, (cx, cy) in enumerate(chips):
                other = half(mi, 2 * cx + cy, 1 - c)
                copy(mi * 6 + 3 + j, other, other, sibling).wait_recv()
        for cp in started:
            cp.wait_send()

    return pl.pallas_call(
        body, name="allgather_weights",
        in_specs=[ANY] * N_MATS, out_specs=[ANY] * N_MATS,
        out_shape=[jax.ShapeDtypeStruct((4, r, c), BF16) for r, c in SHARD_SHAPES],
        scratch_shapes=[pltpu.SemaphoreType.DMA((6 * N_MATS,)), pltpu.SemaphoreType.DMA((6 * N_MATS,))],
    )(*mats)


HBM = pl.BlockSpec(memory_space=pltpu.HBM)
SEM = pl.BlockSpec(memory_space=pltpu.SEMAPHORE)
DATAFLOW = pltpu.SideEffectType.DATAFLOW_SIDE_EFFECTING


def _peers(x, y, c):
    out = []
    for k in range(1, 8):
        px, py, pc = x ^ (k >> 2), y ^ ((k >> 1) & 1), c ^ (k & 1)
        out.append((k - 1, (px, py, pc), 4 * px + 2 * py + pc))
    return out


def _exchange_start(parts, name):
    n = len(parts)

    def body(*refs):
        p_refs, land_refs = refs[:n], refs[n:2 * n]
        send_sems, recv_sems, token = refs[2 * n], refs[2 * n + 1], refs[-1]
        x, y, c = _my_place()
        me = 4 * x + 2 * y + c
        for k, dev, peer in _peers(x, y, c):
            for mi in range(n):
                pltpu.make_async_remote_copy(
                    src_ref=p_refs[mi].at[peer], dst_ref=land_refs[mi].at[me], send_sem=send_sems.at[k * n + mi],
                    recv_sem=recv_sems.at[k * n + mi], device_id=dev, device_id_type=MESH).start()
        token[...] = jnp.zeros_like(token)

    hbm = [pltpu.HBM(p.shape, p.dtype) for p in parts]
    outs = pl.pallas_call(
        body, name=name + "_start",
        out_shape=(pltpu.SemaphoreType.DMA((7 * n,)), pltpu.SemaphoreType.DMA((7 * n,)), *hbm, *hbm,
                   jax.ShapeDtypeStruct((8, 128), F32)),
        in_specs=[HBM] * (2 * n), out_specs=(SEM, SEM, *[HBM] * (2 * n), pl.BlockSpec(memory_space=pltpu.VMEM)),
        input_output_aliases={i: 2 + i for i in range(2 * n)},
        compiler_params=pltpu.CompilerParams(has_side_effects=DATAFLOW),
    )(*[pltpu.with_memory_space_constraint(p, pltpu.HBM) for p in parts],
      *[pltpu.with_memory_space_constraint(lax.empty(p.shape, p.dtype), pltpu.HBM) for p in parts])
    return (name, outs[:-1]), outs[-1]


def _exchange_wait(handle, after):
    name, outs = handle
    n = (len(outs) - 2) // 2

    def body(*refs):
        p_refs, land_refs = refs[:n], refs[n:2 * n]
        send_sems, recv_sems = refs[2 * n], refs[2 * n + 1]
        x, y, c = _my_place()
        me = 4 * x + 2 * y + c
        for k, dev, peer in _peers(x, y, c):
            for mi in range(n):
                pltpu.make_async_remote_copy(
                    src_ref=p_refs[mi].at[peer], dst_ref=land_refs[mi].at[me], send_sem=send_sems.at[k * n + mi],
                    recv_sem=recv_sems.at[k * n + mi], device_id=dev, device_id_type=MESH).wait_send()
                slot = land_refs[mi].at[peer]
                pltpu.make_async_remote_copy(
                    src_ref=slot, dst_ref=slot, send_sem=send_sems.at[k * n + mi],
                    recv_sem=recv_sems.at[k * n + mi], device_id=dev, device_id_type=MESH).wait_recv()

    bufs = outs[2:]
    res = pl.pallas_call(
        body, name=name + "_wait", out_shape=tuple(pltpu.HBM(b.shape, b.dtype) for b in bufs),
        in_specs=[HBM] * (2 * n) + [SEM, SEM, ANY], out_specs=tuple([HBM] * (2 * n)),
        input_output_aliases={i: i for i in range(2 * n)},
        compiler_params=pltpu.CompilerParams(has_side_effects=DATAFLOW),
    )(*bufs, outs[0], outs[1], after)
    return list(res[n:])


def _swap_halves(halves, gvec):
    def body(*refs):
        g_refs, gv_ref = refs[:N_MATS], refs[N_MATS]
        out_refs, rg_ref = refs[N_MATS + 1:2 * N_MATS + 1], refs[2 * N_MATS + 1]
        send_sems, recv_sems = refs[2 * N_MATS + 2:]
        x, y, c = _my_place()
        me = 4 * x + 2 * y + c
        sends = []
        for mi in range(N_MATS):
            cp = pltpu.make_async_remote_copy(src_ref=g_refs[mi], dst_ref=out_refs[mi].at[c], send_sem=send_sems.at[mi],
                                              recv_sem=recv_sems.at[mi], device_id=(x, y, 1 - c), device_id_type=MESH)
            cp.start()
            sends.append(cp)
        for k, dev, peer in _peers(x, y, c):
            cp = pltpu.make_async_remote_copy(src_ref=gv_ref, dst_ref=rg_ref.at[me], send_sem=send_sems.at[N_MATS + k],
                                              recv_sem=recv_sems.at[N_MATS + k], device_id=dev, device_id_type=MESH)
            cp.start()
            sends.append(cp)
        for mi in range(N_MATS):
            got = out_refs[mi].at[1 - c]
            pltpu.make_async_remote_copy(src_ref=got, dst_ref=got, send_sem=send_sems.at[mi], recv_sem=recv_sems.at[mi],
                                         device_id=(x, y, 1 - c), device_id_type=MESH).wait_recv()
        for k, dev, peer in _peers(x, y, c):
            got = rg_ref.at[peer]
            pltpu.make_async_remote_copy(src_ref=got, dst_ref=got, send_sem=send_sems.at[N_MATS + k],
                                         recv_sem=recv_sems.at[N_MATS + k], device_id=dev, device_id_type=MESH).wait_recv()
        for cp in sends:
            cp.wait_send()

    outs = pl.pallas_call(
        body, name="swap_halves",
        in_specs=[ANY] * (N_MATS + 1), out_specs=[ANY] * (N_MATS + 1),
        out_shape=[jax.ShapeDtypeStruct((2, r // 2, c), F32) for r, c in SHARD_SHAPES]
        + [jax.ShapeDtypeStruct((8, 8, N_GAINS), F32)],
        scratch_shapes=[pltpu.SemaphoreType.DMA((N_MATS + 7,)), pltpu.SemaphoreType.DMA((N_MATS + 7,))],
    )(*halves, gvec)
    return outs[:N_MATS], outs[N_MATS]


def _set_slot(arr, block, idx):
    return lax.dynamic_update_slice(arr, block[None], (idx,) + (0,) * block.ndim)


PAD_RUNS = ((6304, 8352, 0), (5280, 6304, COL_Z), (672, 5280, COL_QKV), (0, 640, COL_LAT), (640, 672, COL_LAT + 704))
W_IN_SHARD = 2088


def _full_weights(gathered):
    def cols(a):
        return jnp.concatenate([a[s] for s in range(4)], axis=1)

    w_uq, w_ukv, w_pm, w_pd = [cols(a) for a in gathered[1:5]]
    w_out = gathered[5].reshape(D_MODEL, D_MODEL)
    g_in = gathered[0]
    pieces, at = [], 0
    for lo, hi, pad_lo in sorted(PAD_RUNS, key=lambda t: t[2]):
        if pad_lo > at:
            pieces.append(jnp.zeros((D_MODEL, pad_lo - at), g_in.dtype))
        for s in range(4):
            a_, b_ = max(lo, s * W_IN_SHARD), min(hi, (s + 1) * W_IN_SHARD)
            if a_ < b_:
                pieces.append(g_in[s][:, a_ - s * W_IN_SHARD:b_ - s * W_IN_SHARD])
        at = pad_lo + hi - lo
    pieces.append(jnp.zeros((D_MODEL, N_PAD - at), g_in.dtype))
    w_pad = jnp.concatenate(pieces, axis=1)
    z32 = jnp.zeros((Q_RANK, 32), w_uq.dtype)
    wuq_pad = jnp.concatenate([t for h in range(MLA_HEADS) for t in (w_uq[:, h * 96:(h + 1) * 96], z32)], axis=1)
    z64 = jnp.zeros((KV_RANK, 64), w_ukv.dtype)
    wk_pad = jnp.concatenate([t for h in range(MLA_HEADS) for t in (w_ukv[:, h * 128:h * 128 + 64], z64)], axis=1)
    wv = jnp.concatenate([w_ukv[:, h * 128 + 64:(h + 1) * 128] for h in range(MLA_HEADS)], axis=1)
    return w_pad, wuq_pad, wk_pad, wv, w_pm, w_pd, w_out


def _grad_parts_in(dw_pad):
    def in_block(s, h):
        rows = slice(h * 512, (h + 1) * 512)
        out = []
        for lo, hi, pad_lo in sorted(PAD_RUNS):
            a_, b_ = max(lo, s * W_IN_SHARD), min(hi, (s + 1) * W_IN_SHARD)
            if a_ < b_:
                out.append(dw_pad[rows, pad_lo + a_ - lo:pad_lo + b_ - lo])
        return jnp.concatenate(out, axis=1)

    return jnp.stack([in_block(s, h) for s in range(4) for h in range(2)])


def _grad_parts_small(dwuq_pad, dwk_pad, dwv, dwpm, dwpd, dwout):
    d_uq = jnp.concatenate([dwuq_pad[:, h * 128:h * 128 + 96] for h in range(MLA_HEADS)], axis=1)
    d_ukv = jnp.concatenate([t for h in range(MLA_HEADS) for t in (dwk_pad[:, h * 128:h * 128 + 64], dwv[:, h * 64:(h + 1) * 64])],
                            axis=1)

    def col_blocks(m):
        r, c = m.shape[0] // 2, m.shape[1] // 4
        return jnp.stack([m[h * r:(h + 1) * r, s * c:(s + 1) * c] for s in range(4) for h in range(2)])

    return [col_blocks(d_uq), col_blocks(d_ukv), col_blocks(dwpm), col_blocks(dwpd),
            dwout.reshape(8, 128, D_MODEL)]


def _rope_tables(positions):
    pos = positions.reshape(SEQ).astype(F32)
    lane = jnp.arange(128)

    def table(rot, first, period):
        inv = ROPE_THETA ** (-jnp.arange(0, rot, 2, dtype=F32) / rot)
        half = rot // 2
        off = lane % period - first
        in1, in2 = (off >= 0) & (off < half), (off >= half) & (off < rot)
        inv_lane = jnp.where(in1 | in2, inv[jnp.clip(off % half, 0, half - 1)], 0.0)
        sign = jnp.where(in1, -1.0, 1.0).astype(F32)
        ang = pos[:, None] * inv_lane[None, :]
        return jnp.cos(ang), jnp.sin(ang) * sign[None, :]

    return table(32, 64, 128), table(16, 0, 64)


def _device_grads(x, positions, target, gains, gathered, send):
    pre_g, q_g, kv_g, post_g = gains
    w_pad, wuq_pad, wk_pad, wv, w_pm, w_pd, w_out = _full_weights(gathered)
    (mc, ms), (dc, ds) = _rope_tables(positions)

    h, h_t = _prenorm_fwd(x, pre_g)
    p = _matmul(h, w_pad, "nn", F32, 1024, 1408, 1024, "in_proj")
    cqn, ckvn, q, k, v = _mla_prep_fwd(p, q_g, kv_g, wuq_pad, wk_pad, wv, mc, ms)
    ya, lse_m = _mla_flash_fwd(q, k, v)
    qkv = [_dil_prep_fwd(p, dc, ds, g) for g in range(3)]
    o_g, l_g = zip(*[_dil_attn_fwd2(qkv[g], g) for g in range(3)])
    (dp, dy, mg, dt, ua, dpa, ud, dpd, dya, dyd, yd, lse_d, loss_cols, dg_post) = _tail(
        p, ya, o_g, l_g, x, target, w_pm, w_pd, w_out, post_g)

    dq, dk, dv = _mla_flash_bwd(q, k, v, ya, dya, lse_m)
    dp, dqb, dkb, dvb, dg_q, dg_kv = _mla_prep_bwd(dp, p, dq, dk, dv, q_g, kv_g, wuq_pad, wk_pad, wv, mc, ms)
    dwuq_pad = _matmul(cqn, dqb, "tn", BF16, Q_RANK, 1024, 512, "dw_uq")
    dwk_pad = _matmul(ckvn, dkb, "tn", BF16, KV_RANK, 1024, 512, "dw_k")
    dwv = _matmul(ckvn, dvb, "tn", BF16, KV_RANK, 512, 512, "dw_v")
    dwpm = _matmul(ua, dpa, "tn", BF16, 512, 1024, 512, "dw_proj_mla")
    dwpd = _matmul(ud, dpd, "tn", BF16, 512, 1024, 512, "dw_proj_dil")
    dwout = _matmul(mg, dt, "tn", BF16, 1024, 1024, 512, "dw_out")
    small, token = send(_grad_parts_small(dwuq_pad, dwk_pad, dwv, dwpm, dwpd, dwout), "exchange_small")
    for g in range(3):
        dqkv = _dil_attn_bwd2(qkv[g], dyd, yd, lse_d, g, token if g == 0 else None)
        dp = _dil_prep_bwd(dp, dqkv, dc, ds, g)

    dw_pad = _matmul(h_t, dp, "nn", BF16, 1024, 1408, 2048, "dw_in")
    big, token = send([_grad_parts_in(dw_pad)], "exchange_in")
    dh = _matmul(dp, w_pad, "nt", F32, 1024, 1024, 1408, "dh", token)
    grad_x, dg_pre = _prenorm_bwd(x, dh, dy, pre_g)

    gvec = jnp.concatenate([dg_pre, dg_q, dg_kv, dg_post], axis=1)
    return loss_cols, grad_x, (big, small), gvec


def kernel(x, positions, pre_norm_g, w_in, q_norm_g, w_uq, kv_norm_g, w_ukv, w_proj_mla, w_proj_dil, w_out, post_norm_g, loss_target, m_pre_norm_g, m_w_in, m_q_norm_g, m_w_uq, m_kv_norm_g, m_w_ukv, m_w_proj_mla, m_w_proj_dil, m_w_out, m_post_norm_g, v_pre_norm_g, v_w_in, v_q_norm_g, v_w_uq, v_kv_norm_g, v_w_ukv, v_w_proj_mla, v_w_proj_dil, v_w_out, v_post_norm_g):
    xi, yi, ci = _my_place()
    chip, me = 2 * xi + yi, 4 * xi + 2 * yi + ci
    mats = [w.reshape(w.shape[1:]).astype(BF16) for w in (w_in, w_uq, w_ukv, w_proj_mla, w_proj_dil, w_out)]
    gathered = [_set_slot(g, m, chip) for g, m in zip(_allgather_weights(mats), mats)]
    gains = (pre_norm_g, q_norm_g, kv_norm_g, post_norm_g)
    sent = {}

    def send(blocks, name):
        sent[name] = blocks
        return _exchange_start(blocks, name)

    loss_cols, grad_x, (big, small), gvec = _device_grads(x[0], positions, loss_target[0], gains, gathered, send)

    loss = lax.psum(jnp.sum(loss_cols) * (0.5 / D_MODEL), ("x", "y", "c"))

    recv = _exchange_wait(big, grad_x) + _exchange_wait(small, grad_x)
    own = sent["exchange_in"] + sent["exchange_small"]
    halves = [_sum_parts(recv[mi], own[mi], me, 64, f"sum_grad_{mi}") for mi in range(N_MATS)]
    gvec8 = jnp.pad(gvec, ((0, 7), (0, 0)))
    swapped, recv_gains = _swap_halves(halves, gvec8)
    g_gains = _sum_parts(recv_gains, gvec8, me, 8, "sum_gain_parts")[0:1]
    g_mats = [_set_slot(s, hf, ci).reshape((1,) + shp) for s, hf, shp in zip(swapped, halves, SHARD_SHAPES)]

    off = [0, 1024, 1408, 1664, 2688]
    g_gain = [g_gains[:, off[i]:off[i + 1]] for i in range(4)]
    grads = [g_gain[0], g_mats[0], g_gain[1], g_mats[1], g_gain[2], g_mats[2], g_mats[3], g_mats[4], g_mats[5], g_gain[3]]
    ws = [pre_norm_g, w_in, q_norm_g, w_uq, kv_norm_g, w_ukv, w_proj_mla, w_proj_dil, w_out, post_norm_g]
    ms = [m_pre_norm_g, m_w_in, m_q_norm_g, m_w_uq, m_kv_norm_g, m_w_ukv, m_w_proj_mla, m_w_proj_dil, m_w_out, m_post_norm_g]
    vs = [v_pre_norm_g, v_w_in, v_q_norm_g, v_w_uq, v_kv_norm_g, v_w_ukv, v_w_proj_mla, v_w_proj_dil, v_w_out, v_post_norm_g]
    deltas, new_m, new_v = [], [], []
    for i, (w, g, m, v) in enumerate(zip(ws, grads, ms, vs)):
        if w.shape[-1] % 128 and w.shape[-2] % 128 == 0:
            g = jnp.swapaxes(g, 1, 2)
            grads[i] = jnp.swapaxes(g, 1, 2)
            d_, m_, v_ = [jnp.swapaxes(o, 1, 2) for o in
                          _adamw(jnp.swapaxes(w, 1, 2), g, jnp.swapaxes(m, 1, 2), jnp.swapaxes(v, 1, 2), f"adamw_{i}")]
        else:
            d_, m_, v_ = _adamw(w, g, m, v, f"adamw_{i}")
        deltas.append(d_)
        new_m.append(m_)
        new_v.append(v_)
    return (loss, grad_x.reshape(x.shape), *grads, *deltas, *new_m, *new_v)
```

```python
import jax
import jax.numpy as jnp
from jax import lax
from jax.experimental import pallas as pl
from jax.experimental.pallas import tpu as pltpu

F32 = jnp.float32
BF16 = jnp.bfloat16

SEQ = 4096
D_MODEL = 1024
EPS = 1e-6
ROPE_THETA = 500000.0
MLA_HEADS = 8
Q_RANK = 384
KV_RANK = 256
MLA_SCALE = 96.0 ** -0.5
MLA_ROPE_HALF = 16
DIL_DILATIONS = (1, 4, 16)
DIL_ROPE_HALF = 8
DIL_SCALE = 0.125
BAND = 128

N_LAT = 768
COL_Z, COL_QKV, COL_LAT = 2048, 3072, 7680
N_PAD = 8448
IN_SPLITS = (384, 256, 32, 4608, 512, 512, 1024, 1024)

SHARD_SHAPES = ((1024, 2088), (384, 192), (256, 256), (512, 256), (512, 256), (256, 1024))
N_MATS = len(SHARD_SHAPES)
N_GAINS = 2688

ADAM_LR, ADAM_B1, ADAM_B2, ADAM_EPS, ADAM_WD, ADAM_STEP = 0.001, 0.9, 0.999, 1e-08, 0.01, 10

VMEM_LIMIT = 56 * 1024 * 1024
NEG = -1e30
MESH = pl.DeviceIdType.MESH


def _cparams(**kw):
    return pltpu.CompilerParams(vmem_limit_bytes=VMEM_LIMIT, **kw)


def _dot(a, b, dims):
    return lax.dot_general(a, b, (dims, ((), ())), preferred_element_type=F32)


def _nn(a, b):
    return _dot(a, b, ((1,), (0,)))


def _nt(a, b):
    return _dot(a, b, ((1,), (1,)))


def _tn(a, b):
    return _dot(a, b, ((0,), (0,)))


def _rope_lanes(shape, half, period, first):
    lane = lax.broadcasted_iota(jnp.int32, shape, len(shape) - 1) % period
    return (lane >= first) & (lane < first + half), (lane >= first + half) & (lane < first + 2 * half)


def _rope_fwd(x, c, s, half, lanes):
    x1, _ = lanes
    return x * c + jnp.where(x1, pltpu.roll(x, 128 - half, 1), pltpu.roll(x, half, 1)) * s


def _rope_bwd(g, c, s, half, lanes):
    x1, x2 = lanes
    gs = g * s
    return g * c + jnp.where(x2, pltpu.roll(gs, half, 1), jnp.where(x1, pltpu.roll(gs, 128 - half, 1), 0.0))


def _sigmoid(x):
    return 1.0 / (1.0 + jnp.exp(-x))


def _after(token):
    return ([], []) if token is None else ([token], [pl.BlockSpec(memory_space=pl.ANY)])


def _matmul(a, b, mode, out_dtype, tm, tn, tk, name, token=None, b_cols=None):
    after, after_specs = _after(token)
    if mode == "nn":
        (m, k), n = a.shape, b.shape[1]
        first = 0
        if b_cols is not None:
            first, n = b_cols[0], b_cols[1] * tn
        a_spec = pl.BlockSpec((tm, tk), lambda j, i, kk: (i, kk))
        b_spec = pl.BlockSpec((tk, tn), lambda j, i, kk: (kk, j + first))
        dot = _nn
    elif mode == "nt":
        (m, k), n = a.shape, b.shape[0]
        a_spec = pl.BlockSpec((tm, tk), lambda j, i, kk: (i, kk))
        b_spec = pl.BlockSpec((tn, tk), lambda j, i, kk: (j, kk))
        dot = _nt
    else:
        (k, m), n = a.shape, b.shape[1]
        a_spec = pl.BlockSpec((tk, tm), lambda j, i, kk: (kk, i))
        b_spec = pl.BlockSpec((tk, tn), lambda j, i, kk: (kk, j))
        dot = _tn
    assert m % tm == 0 and n % tn == 0 and k % tk == 0, (name, m, n, k, tm, tn, tk)
    nk = k // tk

    def body(a_ref, b_ref, *rest):
        o_ref, acc_ref = rest[-2:]
        kk = pl.program_id(2)
        part = dot(a_ref[...], b_ref[...])

        @pl.when(kk == 0)
        def _():
            acc_ref[...] = part

        @pl.when(kk > 0)
        def _():
            acc_ref[...] += part

        @pl.when(kk == nk - 1)
        def _():
            o_ref[...] = acc_ref[...].astype(o_ref.dtype)

    return pl.pallas_call(
        body, name=name, grid=(n // tn, m // tm, nk),
        in_specs=[a_spec, b_spec] + after_specs,
        out_specs=pl.BlockSpec((tm, tn), lambda j, i, kk: (i, j)),
        out_shape=jax.ShapeDtypeStruct((m, n), out_dtype),
        scratch_shapes=[pltpu.VMEM((tm, tn), F32)],
        compiler_params=_cparams(),
    )(a, b, *after)


def _prenorm_fwd(x, g):
    tm = 512

    def body(x_ref, g_ref, h_ref, ht_ref):
        xv = x_ref[...]
        r = lax.rsqrt(jnp.mean(xv * xv, axis=-1, keepdims=True) + EPS)
        hv = (xv * r * g_ref[...]).astype(BF16)
        h_ref[...] = hv
        ht_ref[...] = hv.T

    return pl.pallas_call(
        body, name="prenorm_fwd", grid=(SEQ // tm,),
        in_specs=[pl.BlockSpec((tm, D_MODEL), lambda i: (i, 0)), pl.BlockSpec((1, D_MODEL), lambda i: (0, 0))],
        out_specs=[pl.BlockSpec((tm, D_MODEL), lambda i: (i, 0)), pl.BlockSpec((D_MODEL, tm), lambda i: (0, i))],
        out_shape=[jax.ShapeDtypeStruct((SEQ, D_MODEL), BF16), jax.ShapeDtypeStruct((D_MODEL, SEQ), BF16)],
    )(x, g)


def _prenorm_bwd(x, dh, dy, g):
    tm = 512

    def body(x_ref, dh_ref, dy_ref, g_ref, gx_ref, dg_ref):
        xv = x_ref[...]
        r = lax.rsqrt(jnp.mean(xv * xv, axis=-1, keepdims=True) + EPS)
        n = xv * r
        dhv = dh_ref[...]
        dn = dhv * g_ref[...]
        gx_ref[...] = dy_ref[...] + r * (dn - n * jnp.mean(dn * n, axis=-1, keepdims=True))
        part = jnp.sum(dhv * n, axis=0, keepdims=True)

        @pl.when(pl.program_id(0) == 0)
        def _():
            dg_ref[...] = part

        @pl.when(pl.program_id(0) > 0)
        def _():
            dg_ref[...] += part

    row = pl.BlockSpec((tm, D_MODEL), lambda i: (i, 0))
    vec = pl.BlockSpec((1, D_MODEL), lambda i: (0, 0))
    return pl.pallas_call(
        body, name="prenorm_bwd", grid=(SEQ // tm,),
        in_specs=[row, row, row, vec], out_specs=[row, vec],
        out_shape=[jax.ShapeDtypeStruct((SEQ, D_MODEL), F32), jax.ShapeDtypeStruct((1, D_MODEL), F32)],
        compiler_params=_cparams(),
    )(x, dh, dy, g)


def _mla_prep_fwd(p, qg, kvg, wuq, wk, wv, rc, rs):
    tm = 512

    def body(lat_ref, qg_ref, kvg_ref, wuq_ref, wk_ref, wv_ref, c_ref, s_ref,
             cqn_ref, ckvn_ref, q_ref, k_ref, v_ref):
        c, s = c_ref[...], s_ref[...]
        lanes = _rope_lanes((tm, 128), MLA_ROPE_HALF, 128, 64)
        cq = lat_ref[:, 0:Q_RANK]
        r1 = lax.rsqrt(jnp.mean(cq * cq, axis=-1, keepdims=True) + EPS)
        cqn = (cq * r1 * qg_ref[...]).astype(BF16)
        cqn_ref[...] = cqn
        q = _nn(cqn, wuq_ref[...])
        for h in range(MLA_HEADS):
            sl = slice(h * 128, (h + 1) * 128)
            q_ref[:, sl] = (_rope_fwd(q[:, sl], c, s, MLA_ROPE_HALF, lanes) * MLA_SCALE).astype(BF16)
        ckv = lat_ref[:, Q_RANK:Q_RANK + KV_RANK]
        r2 = lax.rsqrt(jnp.mean(ckv * ckv, axis=-1, keepdims=True) + EPS)
        ckvn = (ckv * r2 * kvg_ref[...]).astype(BF16)
        ckvn_ref[...] = ckvn
        krr = _rope_fwd(lat_ref[:, Q_RANK + KV_RANK:N_LAT], c, s, MLA_ROPE_HALF, lanes)
        kn = _nn(ckvn, wk_ref[...])
        for h in range(MLA_HEADS):
            sl = slice(h * 128, (h + 1) * 128)
            k_ref[:, sl] = (kn[:, sl] + krr).astype(BF16)
        v_ref[...] = _nn(ckvn, wv_ref[...]).astype(BF16)

    def full(shape):
        return pl.BlockSpec(shape, lambda i: (0, 0))

    def rows(w):
        return pl.BlockSpec((tm, w), lambda i: (i, 0))

    return pl.pallas_call(
        body, name="mla_prep_fwd", grid=(SEQ // tm,),
        in_specs=[pl.BlockSpec((tm, N_LAT), lambda i: (i, COL_LAT // N_LAT)),
                  full((1, Q_RANK)), full((1, KV_RANK)), full((Q_RANK, 1024)), full((KV_RANK, 1024)),
                  full((KV_RANK, 512)), rows(128), rows(128)],
        out_specs=[rows(Q_RANK), rows(KV_RANK), rows(1024), rows(1024), rows(512)],
        out_shape=[jax.ShapeDtypeStruct((SEQ, Q_RANK), BF16), jax.ShapeDtypeStruct((SEQ, KV_RANK), BF16),
                   jax.ShapeDtypeStruct((SEQ, 1024), BF16), jax.ShapeDtypeStruct((SEQ, 1024), BF16),
                   jax.ShapeDtypeStruct((SEQ, 512), BF16)],
        compiler_params=_cparams(),
    )(p, qg, kvg, wuq, wk, wv, rc, rs)


def _mla_prep_bwd(dp_in, p, dq, dk, dv, qg, kvg, wuq, wk, wv, rc, rs):
    tm = 512

    def body(dp_any, lat_ref, dq_ref, dk_ref, dv_ref, qg_ref, kvg_ref, wuq_ref, wk_ref, wv_ref,
             c_ref, s_ref, dp_ref, dqb_ref, dkb_ref, dvb_ref, dgq_ref, dgkv_ref):
        del dp_any
        c, s = c_ref[...], s_ref[...]
        lanes = _rope_lanes((tm, 128), MLA_ROPE_HALF, 128, 64)
        lane = lax.broadcasted_iota(jnp.int32, (tm, 128), 1)
        dkr = jnp.zeros((tm, 128), F32)
        for h in range(MLA_HEADS):
            sl = slice(h * 128, (h + 1) * 128)
            dqb_ref[:, sl] = _rope_bwd(dq_ref[:, sl] * MLA_SCALE, c, s, MLA_ROPE_HALF, lanes).astype(BF16)
            dkh = dk_ref[:, sl]
            dkr = dkr + dkh
            dkb_ref[:, sl] = jnp.where(lane < 64, dkh, 0.0).astype(BF16)
        dkr = jnp.where((lane >= 64) & (lane < 96), dkr, 0.0)
        dkr = _rope_bwd(dkr, c, s, MLA_ROPE_HALF, lanes)
        dvb = dv_ref[...].astype(BF16)
        dvb_ref[...] = dvb

        cq = lat_ref[:, 0:Q_RANK]
        r1 = lax.rsqrt(jnp.mean(cq * cq, axis=-1, keepdims=True) + EPS)
        n1 = cq * r1
        dcqn = _nt(dqb_ref[...], wuq_ref[...])
        dn1 = dcqn * qg_ref[...]
        dcq = r1 * (dn1 - n1 * jnp.mean(dn1 * n1, axis=-1, keepdims=True))
        pq = jnp.sum(dcqn * n1, axis=0, keepdims=True)

        ckv = lat_ref[:, Q_RANK:Q_RANK + KV_RANK]
        r2 = lax.rsqrt(jnp.mean(ckv * ckv, axis=-1, keepdims=True) + EPS)
        n2 = ckv * r2
        dckvn = _nt(dkb_ref[...], wk_ref[...]) + _nt(dvb, wv_ref[...])
        dn2 = dckvn * kvg_ref[...]
        dckv = r2 * (dn2 - n2 * jnp.mean(dn2 * n2, axis=-1, keepdims=True))
        pkv = jnp.sum(dckvn * n2, axis=0, keepdims=True)

        dp_ref[:, 0:Q_RANK] = dcq.astype(BF16)
        dp_ref[:, Q_RANK:Q_RANK + KV_RANK] = dckv.astype(BF16)
        dp_ref[:, Q_RANK + KV_RANK:N_LAT] = dkr.astype(BF16)

        @pl.when(pl.program_id(0) == 0)
        def _():
            dgq_ref[...] = pq
            dgkv_ref[...] = pkv

        @pl.when(pl.program_id(0) > 0)
        def _():
            dgq_ref[...] += pq
            dgkv_ref[...] += pkv

    def full(shape):
        return pl.BlockSpec(shape, lambda i: (0, 0))

    def rows(w):
        return pl.BlockSpec((tm, w), lambda i: (i, 0))

    lat = pl.BlockSpec((tm, N_LAT), lambda i: (i, COL_LAT // N_LAT))
    return pl.pallas_call(
        body, name="mla_prep_bwd", grid=(SEQ // tm,),
        in_specs=[pl.BlockSpec(memory_space=pl.ANY), lat, rows(1024), rows(1024), rows(512),
                  full((1, Q_RANK)), full((1, KV_RANK)), full((Q_RANK, 1024)), full((KV_RANK, 1024)),
                  full((KV_RANK, 512)), rows(128), rows(128)],
        out_specs=[lat, rows(1024), rows(1024), rows(512), full((1, Q_RANK)), full((1, KV_RANK))],
        out_shape=[jax.ShapeDtypeStruct((SEQ, N_PAD), BF16), jax.ShapeDtypeStruct((SEQ, 1024), BF16),
                   jax.ShapeDtypeStruct((SEQ, 1024), BF16), jax.ShapeDtypeStruct((SEQ, 512), BF16),
                   jax.ShapeDtypeStruct((1, Q_RANK), F32), jax.ShapeDtypeStruct((1, KV_RANK), F32)],
        input_output_aliases={0: 0},
        compiler_params=_cparams(),
    )(dp_in, p, dq, dk, dv, qg, kvg, wuq, wk, wv, rc, rs)


FLASH_T = 512


def _head_half(shape, hh):
    lane = lax.broadcasted_iota(jnp.int32, shape, 1)
    return (lane < 64) if hh == 0 else (lane >= 64)


def _causal_keep(t):
    row = lax.broadcasted_iota(jnp.int32, (t, t), 0)
    col = lax.broadcasted_iota(jnp.int32, (t, t), 1)
    return row >= col


def _tri_steps(nb, q_major):
    if q_major:
        pairs = [(i, kb) for i in range(nb) for kb in range(i + 1)]
    else:
        pairs = [(i, kb) for kb in range(nb) for i in range(kb, nb)]
    return jnp.asarray([p[0] for p in pairs], jnp.int32), jnp.asarray([p[1] for p in pairs], jnp.int32)


def _mla_flash_fwd(q, k, v):
    t = FLASH_T
    nb = SEQ // t
    qtab, ktab = _tri_steps(nb, True)

    def body(qi_ref, ki_ref, q_ref, k_ref, v_ref, o_ref, lse_ref, m_scr, l_scr, acc_scr):
        step = pl.program_id(1)
        i, kb = qi_ref[step], ki_ref[step]

        @pl.when(kb == 0)
        def _():
            m_scr[...] = jnp.full_like(m_scr, NEG)
            l_scr[...] = jnp.zeros_like(l_scr)
            acc_scr[...] = jnp.zeros_like(acc_scr)

        def update(masked):
            vv = v_ref[...]
            for hh in range(2):
                sl = slice(hh * 128, (hh + 1) * 128)
                s = _nt(q_ref[:, sl], k_ref[:, sl])
                if masked:
                    s = jnp.where(_causal_keep(t), s, NEG)
                m_prev = m_scr[hh]
                m_new = jnp.maximum(m_prev, jnp.max(s, axis=-1, keepdims=True))
                pr = jnp.exp(s - jnp.tile(m_new, (1, t // 128)))
                alpha = jnp.exp(m_prev - m_new)
                l_scr[hh] = alpha * l_scr[hh] + jnp.sum(pr, axis=-1, keepdims=True)
                acc_scr[hh] = alpha * acc_scr[hh] + _nn(pr.astype(BF16), vv)
                m_scr[hh] = m_new

        @pl.when(kb < i)
        def _():
            update(False)

        @pl.when(kb == i)
        def _():
            update(True)
            o0 = acc_scr[0] / l_scr[0]
            o1 = acc_scr[1] / l_scr[1]
            o_ref[...] = jnp.where(_head_half((t, 128), 0), o0, o1)
            for hh in range(2):
                lse_ref[:, hh * 128:(hh + 1) * 128] = m_scr[hh] + jnp.log(l_scr[hh])

    grid_spec = pltpu.PrefetchScalarGridSpec(
        num_scalar_prefetch=2, grid=(4, qtab.shape[0]),
        in_specs=[pl.BlockSpec((t, 256), lambda j, s, qi, ki: (qi[s], j)),
                  pl.BlockSpec((t, 256), lambda j, s, qi, ki: (ki[s], j)),
                  pl.BlockSpec((t, 128), lambda j, s, qi, ki: (ki[s], j))],
        out_specs=[pl.BlockSpec((t, 128), lambda j, s, qi, ki: (qi[s], j)),
                   pl.BlockSpec((t, 256), lambda j, s, qi, ki: (qi[s], j))],
        scratch_shapes=[pltpu.VMEM((2, t, 128), F32), pltpu.VMEM((2, t, 128), F32), pltpu.VMEM((2, t, 128), F32)])
    return pl.pallas_call(
        body, name="mla_flash_fwd", grid_spec=grid_spec,
        out_shape=[jax.ShapeDtypeStruct((SEQ, 512), F32), jax.ShapeDtypeStruct((SEQ, 1024), F32)],
        compiler_params=_cparams(),
    )(qtab, ktab, q, k, v)


def _mla_flash_bwd(q, k, v, o, do, lse, token=None):
    t = FLASH_T
    nb = SEQ // t
    qtab, ktab = _tri_steps(nb, False)
    after, after_specs = _after(token)

    def body(qi_ref, ki_ref, q_ref, k_ref, v_ref, o_ref, do_ref, lse_ref, *rest):
        dq_ref, dk_ref, dv_ref, dk_scr, dv_scr = rest[-5:]
        step = pl.program_id(1)
        i, kb = qi_ref[step], ki_ref[step]

        @pl.when(step == 0)
        def _():
            dq_ref[...] = jnp.zeros_like(dq_ref)

        @pl.when(i == kb)
        def _():
            dk_scr[...] = jnp.zeros_like(dk_scr)
            dv_scr[...] = jnp.zeros_like(dv_scr)

        def update(masked):
            vv = v_ref[...]
            ov = o_ref[...]
            dov = do_ref[...]
            rows = pl.ds(pl.multiple_of(i * t, t), t)
            for hh in range(2):
                sl = slice(hh * 128, (hh + 1) * 128)
                qh, kh = q_ref[:, sl], k_ref[:, sl]
                s = _nt(qh, kh)
                if masked:
                    s = jnp.where(_causal_keep(t), s, NEG)
                pr = jnp.exp(s - jnp.tile(lse_ref[:, sl], (1, t // 128)))
                dom = jnp.where(_head_half((t, 128), hh), dov, 0.0)
                domb = dom.astype(BF16)
                dv_scr[...] += _tn(pr.astype(BF16), domb)
                dpr = _nt(domb, vv)
                delta = jnp.sum(dom * ov, axis=-1, keepdims=True)
                ds = (pr * (dpr - delta)).astype(BF16)
                dq_ref[rows, sl] += _nn(ds, kh)
                dk_scr[hh] += _tn(ds, qh)

        @pl.when(i > kb)
        def _():
            update(False)

        @pl.when(i == kb)
        def _():
            update(True)

        @pl.when(i == nb - 1)
        def _():
            dk_ref[:, 0:128] = dk_scr[0]
            dk_ref[:, 128:256] = dk_scr[1]
            dv_ref[...] = dv_scr[...]

    qi_map = lambda j, s, qi, ki: (qi[s], j)
    ki_map = lambda j, s, qi, ki: (ki[s], j)
    grid_spec = pltpu.PrefetchScalarGridSpec(
        num_scalar_prefetch=2, grid=(4, qtab.shape[0]),
        in_specs=[pl.BlockSpec((t, 256), qi_map), pl.BlockSpec((t, 256), ki_map), pl.BlockSpec((t, 128), ki_map),
                  pl.BlockSpec((t, 128), qi_map), pl.BlockSpec((t, 128), qi_map), pl.BlockSpec((t, 256), qi_map)]
        + after_specs,
        out_specs=[pl.BlockSpec((SEQ, 256), lambda j, s, qi, ki: (0, j)), pl.BlockSpec((t, 256), ki_map),
                   pl.BlockSpec((t, 128), ki_map)],
        scratch_shapes=[pltpu.VMEM((2, t, 128), F32), pltpu.VMEM((t, 128), F32)])
    return pl.pallas_call(
        body, name="mla_flash_bwd", grid_spec=grid_spec,
        out_shape=[jax.ShapeDtypeStruct((SEQ, 1024), F32), jax.ShapeDtypeStruct((SEQ, 1024), F32),
                   jax.ShapeDtypeStruct((SEQ, 512), F32)],
        compiler_params=_cparams(),
    )(qtab, ktab, q, k, v, o, do, lse, *after)


DIL_UNROLL = 4


def _strided(start, size, d):
    return pl.ds(start, size) if d == 1 else pl.ds(start, size, stride=d)


def _dil_prep_fwd(p, rc, rs, g):
    d = DIL_DILATIONS[g]
    sub_len = SEQ // d
    ch = min(sub_len, 512)

    def body(p_ref, c_ref, s_ref, o_ref):
        tq = pl.program_id(0)
        is_v = tq == 2
        mult = jnp.where(tq == 0, DIL_SCALE, 1.0).astype(F32)
        lanes = _rope_lanes((ch, 128), DIL_ROPE_HALF, 64, 0)
        o_ref[0, 0:BAND, :] = jnp.zeros((BAND, 128), BF16)
        for r in range(d):
            for c0 in range(0, sub_len, ch):
                rows = _strided(r + c0 * d, ch, d)
                xv = p_ref[rows, :]
                roped = _rope_fwd(xv, c_ref[rows, :], s_ref[rows, :], DIL_ROPE_HALF, lanes)
                at = BAND + r * sub_len + c0
                o_ref[0, at:at + ch, :] = (jnp.where(is_v, xv, roped) * mult).astype(BF16)

    tab = pl.BlockSpec((SEQ, 128), lambda tq, pr: (0, 0))
    return pl.pallas_call(
        body, name=f"dil_prep_fwd_g{g}", grid=(3, 4),
        in_specs=[pl.BlockSpec((SEQ, 128), lambda tq, pr: (0, COL_QKV // 128 + (tq * 3 + g) * 4 + pr)), tab, tab],
        out_specs=pl.BlockSpec((1, BAND + SEQ, 128), lambda tq, pr: (tq, 0, pr)),
        out_shape=jax.ShapeDtypeStruct((3, BAND + SEQ, 512), BF16),
        compiler_params=_cparams(),
    )(p, rc, rs)


DIL_ST = 1024
DIL_NB = DIL_ST // BAND


def _band_keep(g, b, t):
    nbs = SEQ // DIL_DILATIONS[g] // BAND
    row = lax.broadcasted_iota(jnp.int32, (BAND, 2 * BAND), 0)
    col = lax.broadcasted_iota(jnp.int32, (BAND, 2 * BAND), 1)
    cur = (col >= BAND) & (row >= col - BAND)
    prev = (col < BAND) & (col >= row)
    if nbs >= DIL_NB:
        if b > 0:
            return cur | prev
        return cur | (prev & ((t * DIL_NB) % nbs != 0))
    return cur | prev if b % nbs else cur


def _dil_tok(g, b, t):
    d = DIL_DILATIONS[g]
    nbs = SEQ // d // BAND
    gb = t * DIL_NB + b
    return _strided((gb % nbs) * BAND * d + gb // nbs, BAND, d)


def _dil_attn_fwd2(qkv, g):
    def body(q_ref, k_ref, v_ref, o_ref, l_ref, s_scr, p_scr, o_scr):
        t = pl.program_id(1)
        base = t * DIL_ST
        half0 = _head_half((DIL_ST, 128), 0)
        lse_h = []
        for hh in range(2):
            half = _head_half((BAND, 128), hh)
            for b in range(DIL_NB):
                qv = q_ref[0, pl.ds(pl.multiple_of(base + (b + 1) * BAND, BAND), BAND), :]
                k2 = k_ref[0, pl.ds(pl.multiple_of(base + b * BAND, BAND), 2 * BAND), :]
                sb = _nt(jnp.where(half, qv, jnp.zeros_like(qv)), k2)
                s_scr[b * BAND:(b + 1) * BAND, :] = jnp.where(_band_keep(g, b, t), sb, NEG)
            s = s_scr[...]
            m = jnp.max(s, axis=-1, keepdims=True)
            pr = jnp.exp(s - m)
            den = jnp.sum(pr, axis=-1, keepdims=True)
            p_scr[...] = pr.astype(BF16)
            for b in range(DIL_NB):
                v2 = v_ref[0, pl.ds(pl.multiple_of(base + b * BAND, BAND), 2 * BAND), :]
                o_scr[hh, b * BAND:(b + 1) * BAND, :] = _nn(p_scr[b * BAND:(b + 1) * BAND, :], v2)
            o_scr[hh] = o_scr[hh] / den
            lse_h.append(m + jnp.log(den))
        out = jnp.where(half0, o_scr[0], o_scr[1])
        lse = jnp.where(half0, lse_h[0], lse_h[1])
        for b in range(DIL_NB):
            tok = _dil_tok(g, b, t)
            o_ref[tok, :] = out[b * BAND:(b + 1) * BAND, :]
            l_ref[tok, :] = lse[b * BAND:(b + 1) * BAND, :]

    def inp(tq):
        return pl.BlockSpec((1, BAND + SEQ, 128), lambda pr, t: (tq, 0, pr))

    out = pl.BlockSpec((SEQ, 128), lambda pr, t: (0, pr))
    return pl.pallas_call(
        body, name=f"dil_attn_fwd_g{g}", grid=(4, SEQ // DIL_ST),
        in_specs=[inp(0), inp(1), inp(2)], out_specs=[out, out],
        out_shape=[jax.ShapeDtypeStruct((SEQ, 512), F32), jax.ShapeDtypeStruct((SEQ, 512), F32)],
        scratch_shapes=[pltpu.VMEM((DIL_ST, 2 * BAND), F32), pltpu.VMEM((DIL_ST, 2 * BAND), BF16),
                        pltpu.VMEM((2, DIL_ST, 128), F32)],
        compiler_params=_cparams(),
    )(qkv, qkv, qkv)


def _dil_attn_bwd2(qkv, dyd, yd, lse_all, g, token=None):
    d = DIL_DILATIONS[g]
    sub_len = SEQ // d
    nst = SEQ // DIL_ST
    after, after_specs = _after(token)

    def body(q_ref, k_ref, v_ref, do_ref, y_ref, l_ref, *rest):
        out_ref, dk_scr, dv_scr, s_scr, dp_scr, p_scr, ds_scr, do_scr, y_scr, l_scr, dq_scr = rest[-11:]
        t = pl.program_id(1)
        base = t * DIL_ST

        @pl.when(t == 0)
        def _():
            dk_scr[...] = jnp.zeros_like(dk_scr)
            dv_scr[...] = jnp.zeros_like(dv_scr)

        for b in range(DIL_NB):
            tok = _dil_tok(g, b, t)
            do_scr[b * BAND:(b + 1) * BAND, :] = do_ref[tok, :]
            y_scr[b * BAND:(b + 1) * BAND, :] = y_ref[tok, :]
            l_scr[b * BAND:(b + 1) * BAND, :] = l_ref[tok, :]
        for hh in range(2):
            half = _head_half((BAND, 128), hh)
            half_st = _head_half((DIL_ST, 128), hh)
            dom = jnp.where(half_st, do_scr[...], 0.0)
            delta = jnp.sum(dom * y_scr[...], axis=-1, keepdims=True)
            lcol = jnp.max(jnp.where(half_st, l_scr[...], NEG), axis=-1, keepdims=True)
            for b in range(DIL_NB):
                rows = slice(b * BAND, (b + 1) * BAND)
                qv = q_ref[0, pl.ds(pl.multiple_of(base + (b + 1) * BAND, BAND), BAND), :]
                band = pl.ds(pl.multiple_of(base + b * BAND, BAND), 2 * BAND)
                sb = _nt(jnp.where(half, qv, jnp.zeros_like(qv)), k_ref[0, band, :])
                s_scr[rows, :] = jnp.where(_band_keep(g, b, t), sb, NEG)
                dp_scr[rows, :] = _nt(dom[rows, :].astype(BF16), v_ref[0, band, :])
            pr = jnp.exp(s_scr[...] - lcol)
            p_scr[...] = pr.astype(BF16)
            ds_scr[...] = (pr * (dp_scr[...] - delta)).astype(BF16)
            for b in range(DIL_NB):
                rows = slice(b * BAND, (b + 1) * BAND)
                qv = q_ref[0, pl.ds(pl.multiple_of(base + (b + 1) * BAND, BAND), BAND), :]
                band = pl.ds(pl.multiple_of(base + b * BAND, BAND), 2 * BAND)
                dqb = jnp.where(half, _nn(ds_scr[rows, :], k_ref[0, band, :]), 0.0)
                if hh == 0:
                    dq_scr[rows, :] = dqb
                else:
                    dq_scr[rows, :] += dqb
                half2 = _head_half((2 * BAND, 128), hh)
                dk_scr[band, :] += jnp.where(half2, _tn(ds_scr[rows, :], qv), 0.0)
                dv_scr[band, :] += _tn(p_scr[rows, :], dom[rows, :].astype(BF16))
        for b in range(DIL_NB):
            out_ref[pl.ds(0, 1), _dil_tok(g, b, t), :] = dq_scr[b * BAND:(b + 1) * BAND, :][None]

        @pl.when(t == nst - 1)
        def _():
            for r in range(d):
                rows = _strided(r, sub_len, d)
                out_ref[pl.ds(1, 1), rows, :] = dk_scr[BAND + r * sub_len:BAND + (r + 1) * sub_len, :][None]
                out_ref[pl.ds(2, 1), rows, :] = dv_scr[BAND + r * sub_len:BAND + (r + 1) * sub_len, :][None]

    def inp(tq):
        return pl.BlockSpec((1, BAND + SEQ, 128), lambda pr, t: (tq, 0, pr))

    tok_spec = pl.BlockSpec((SEQ, 128), lambda pr, t: (0, pr))
    st = (DIL_ST, 2 * BAND)
    return pl.pallas_call(
        body, name=f"dil_attn_bwd_g{g}", grid=(4, nst),
        in_specs=[inp(0), inp(1), inp(2), tok_spec, tok_spec, tok_spec] + after_specs,
        out_specs=pl.BlockSpec((3, SEQ, 128), lambda pr, t: (0, 0, pr)),
        out_shape=jax.ShapeDtypeStruct((3, SEQ, 512), F32),
        scratch_shapes=[pltpu.VMEM((BAND + SEQ, 128), F32), pltpu.VMEM((BAND + SEQ, 128), F32),
                        pltpu.VMEM(st, F32), pltpu.VMEM(st, F32), pltpu.VMEM(st, BF16), pltpu.VMEM(st, BF16),
                        pltpu.VMEM((DIL_ST, 128), F32), pltpu.VMEM((DIL_ST, 128), F32), pltpu.VMEM((DIL_ST, 128), F32),
                        pltpu.VMEM((DIL_ST, 128), F32)],
        compiler_params=_cparams(),
    )(qkv, qkv, qkv, dyd, yd, lse_all, *after)


def _band_masks():
    row = lax.broadcasted_iota(jnp.int32, (BAND, BAND), 0)
    col = lax.broadcasted_iota(jnp.int32, (BAND, BAND), 1)
    return row >= col, col >= row


def _dil_attn_fwd(qkv, g):
    d = DIL_DILATIONS[g]
    nbs = SEQ // d // BAND
    nblk = SEQ // BAND

    def body(q_ref, k_ref, v_ref, o_ref, l_ref):
        keep_c, keep_p = _band_masks()
        half0 = _head_half((BAND, 128), 0)

        def step(i, carry):
            cur = pl.ds(pl.multiple_of(i * BAND, BAND), BAND)
            prv = pl.ds(pl.multiple_of(jnp.maximum(i - 1, 0) * BAND, BAND), BAND)
            has_prev = (i % nbs) != 0
            qv = q_ref[0, cur, :]
            kc, kp = k_ref[0, cur, :], k_ref[0, prv, :]
            vc, vp = v_ref[0, cur, :], v_ref[0, prv, :]
            outs, lses = [], []
            for hh in range(2):
                qm = jnp.where(_head_half((BAND, 128), hh), qv, jnp.zeros_like(qv))
                sc = jnp.where(keep_c, _nt(qm, kc), NEG)
                sp = jnp.where(keep_p & has_prev, _nt(qm, kp), NEG)
                m = jnp.maximum(jnp.max(sc, axis=-1, keepdims=True), jnp.max(sp, axis=-1, keepdims=True))
                pc, pp = jnp.exp(sc - m), jnp.exp(sp - m)
                den = jnp.sum(pc, axis=-1, keepdims=True) + jnp.sum(pp, axis=-1, keepdims=True)
                o = (_nn(pc.astype(BF16), vc) + _nn(pp.astype(BF16), vp)) / den
                outs.append(o)
                lses.append(jnp.broadcast_to(m + jnp.log(den), (BAND, 128)))
            tok = _strided((i % nbs) * BAND * d + i // nbs, BAND, d)
            o_ref[tok, :] = jnp.where(half0, outs[0], outs[1])
            l_ref[tok, :] = jnp.where(half0, lses[0], lses[1])
            return carry

        lax.fori_loop(0, nblk, step, 0, unroll=DIL_UNROLL)

    def inp(tq):
        return pl.BlockSpec((1, SEQ, 128), lambda pr: (tq, 0, pr))

    out = pl.BlockSpec((SEQ, 128), lambda pr: (0, pr))
    return pl.pallas_call(
        body, name=f"dil_attn_fwd_g{g}", grid=(4,),
        in_specs=[inp(0), inp(1), inp(2)], out_specs=[out, out],
        out_shape=[jax.ShapeDtypeStruct((SEQ, 512), F32), jax.ShapeDtypeStruct((SEQ, 512), F32)],
        compiler_params=_cparams(),
    )(qkv, qkv, qkv)


def _dil_attn_bwd(qkv, dyd, yd, lse_all, g):
    d = DIL_DILATIONS[g]
    sub_len = SEQ // d
    nbs = sub_len // BAND
    nblk = SEQ // BAND

    def body(q_ref, k_ref, v_ref, do_ref, y_ref, l_ref, out_ref, dk_scr, dv_scr):
        keep_c, keep_p = _band_masks()
        dk_scr[...] = jnp.zeros_like(dk_scr)
        dv_scr[...] = jnp.zeros_like(dv_scr)

        def step(i, carry):
            cur = pl.ds(pl.multiple_of(i * BAND, BAND), BAND)
            prv = pl.ds(pl.multiple_of(jnp.maximum(i - 1, 0) * BAND, BAND), BAND)
            has_prev = (i % nbs) != 0
            tok = _strided((i % nbs) * BAND * d + i // nbs, BAND, d)
            qv = q_ref[0, cur, :]
            kc, kp = k_ref[0, cur, :], k_ref[0, prv, :]
            vc, vp = v_ref[0, cur, :], v_ref[0, prv, :]
            dov, yv, lv = do_ref[tok, :], y_ref[tok, :], l_ref[tok, :]
            dq = jnp.zeros((BAND, 128), F32)
            dkc = jnp.zeros((BAND, 128), F32)
            dkp = jnp.zeros((BAND, 128), F32)
            dvc = jnp.zeros((BAND, 128), F32)
            dvp = jnp.zeros((BAND, 128), F32)
            for hh in range(2):
                half = _head_half((BAND, 128), hh)
                qm = jnp.where(half, qv, jnp.zeros_like(qv))
                lcol = jnp.max(jnp.where(half, lv, NEG), axis=-1, keepdims=True)
                pc = jnp.exp(jnp.where(keep_c, _nt(qm, kc), NEG) - lcol)
                pp = jnp.exp(jnp.where(keep_p & has_prev, _nt(qm, kp), NEG) - lcol)
                dom = jnp.where(half, dov, 0.0)
                domb = dom.astype(BF16)
                delta = jnp.sum(dom * yv, axis=-1, keepdims=True)
                dsc = (pc * (_nt(domb, vc) - delta)).astype(BF16)
                dsp = (pp * (_nt(domb, vp) - delta)).astype(BF16)
                dvc = dvc + _tn(pc.astype(BF16), domb)
                dvp = dvp + _tn(pp.astype(BF16), domb)
                dq = dq + jnp.where(half, _nn(dsc, kc) + _nn(dsp, kp), 0.0)
                dkc = dkc + jnp.where(half, _tn(dsc, qv), 0.0)
                dkp = dkp + jnp.where(half, _tn(dsp, qv), 0.0)
            out_ref[pl.ds(0, 1), tok, :] = dq[None]
            dk_scr[cur, :] += dkc
            dk_scr[prv, :] += dkp
            dv_scr[cur, :] += dvc
            dv_scr[prv, :] += dvp
            return carry

        lax.fori_loop(0, nblk, step, 0, unroll=DIL_UNROLL)
        for r in range(d):
            rows = _strided(r, sub_len, d)
            out_ref[pl.ds(1, 1), rows, :] = dk_scr[r * sub_len:(r + 1) * sub_len, :][None]
            out_ref[pl.ds(2, 1), rows, :] = dv_scr[r * sub_len:(r + 1) * sub_len, :][None]

    def inp(tq):
        return pl.BlockSpec((1, SEQ, 128), lambda pr: (tq, 0, pr))

    tok_spec = pl.BlockSpec((SEQ, 128), lambda pr: (0, pr))
    return pl.pallas_call(
        body, name=f"dil_attn_bwd_g{g}", grid=(4,),
        in_specs=[inp(0), inp(1), inp(2), tok_spec, tok_spec, tok_spec],
        out_specs=pl.BlockSpec((3, SEQ, 128), lambda pr: (0, 0, pr)),
        out_shape=jax.ShapeDtypeStruct((3, SEQ, 512), F32),
        scratch_shapes=[pltpu.VMEM((SEQ, 128), F32), pltpu.VMEM((SEQ, 128), F32)],
        compiler_params=_cparams(),
    )(qkv, qkv, qkv, dyd, yd, lse_all)


def _dil_prep_bwd(dp_in, dqkv, rc, rs, g):
    tm = 1024

    def body(dp_any, g_ref, c_ref, s_ref, dp_ref):
        del dp_any
        tq = pl.program_id(0)

        @pl.when(tq == 2)
        def _():
            dp_ref[...] = g_ref[0].astype(BF16)

        @pl.when(tq < 2)
        def _():
            mult = jnp.where(tq == 0, DIL_SCALE, 1.0).astype(F32)
            lanes = _rope_lanes((tm, 128), DIL_ROPE_HALF, 64, 0)
            cv, sv = c_ref[...], s_ref[...] * mult
            cv = cv * mult
            for pr in range(4):
                gv = g_ref[0, :, pr * 128:(pr + 1) * 128]
                dp_ref[:, pr * 128:(pr + 1) * 128] = _rope_bwd(gv, cv, sv, DIL_ROPE_HALF, lanes).astype(BF16)

    tab = pl.BlockSpec((tm, 128), lambda tq, i: (i, 0))
    return pl.pallas_call(
        body, name=f"dil_prep_bwd_g{g}", grid=(3, SEQ // tm),
        in_specs=[pl.BlockSpec(memory_space=pl.ANY),
                  pl.BlockSpec((1, tm, 512), lambda tq, i: (tq, i, 0)), tab, tab],
        out_specs=pl.BlockSpec((tm, 512), lambda tq, i: (i, COL_QKV // 512 + tq * 3 + g)),
        out_shape=jax.ShapeDtypeStruct((SEQ, N_PAD), BF16),
        input_output_aliases={0: 0},
    )(dp_in, dqkv, rc, rs)


TAIL_T = 256


def _tail(p, ya, o_g, l_g, x, target, wpm, wpd, wout, post_g):
    tm = TAIL_T

    def body(pgz_ref, ya_ref, o0_ref, o1_ref, o2_ref, l0_ref, l1_ref, l2_ref, x_ref, t_ref,
             wpm_ref, wpd_ref, wout_ref, pg_ref,
             dp_ref, dy_ref, mg_ref, dt_ref, ua_ref, dpa_ref, ud_ref, dpd_ref, dya_ref, dyd_ref,
             yd_ref, lse_ref, loss_ref, dgp_ref):
        l0, l1, l2 = l0_ref[...], l1_ref[...], l2_ref[...]
        mx = jnp.maximum(jnp.maximum(l0, l1), l2)
        e0, e1, e2 = jnp.exp(l0 - mx), jnp.exp(l1 - mx), jnp.exp(l2 - mx)
        den = e0 + e1 + e2
        yd = (e0 * o0_ref[...] + e1 * o1_ref[...] + e2 * o2_ref[...]) / den
        yd_ref[...] = yd
        lse_ref[...] = mx + jnp.log(den)
        ya = ya_ref[...]

        gm, gd = pgz_ref[:, 0:1024], pgz_ref[:, 1024:2048]
        zm, zd = pgz_ref[:, 2048:2560], pgz_ref[:, 2560:3072]
        szm, szd = _sigmoid(zm), _sigmoid(zd)
        sm, sd = zm * szm, zd * szd
        ua = (ya * sm).astype(BF16)
        ud = (yd * sd).astype(BF16)
        ua_ref[...] = ua
        ud_ref[...] = ud
        pa = _nn(ua, wpm_ref[...])
        pd = _nn(ud, wpd_ref[...])
        sgm, sgd = _sigmoid(gm), _sigmoid(gd)
        mg = (sgm * pa + sgd * pd).astype(BF16)
        mg_ref[...] = mg
        t = _nn(mg, wout_ref[...])
        r3 = lax.rsqrt(jnp.mean(t * t, axis=-1, keepdims=True) + EPS)
        n = t * r3
        pg = pg_ref[...]
        err = x_ref[...] + n * pg - t_ref[...]
        lpart = jnp.sum(err * err, axis=0, keepdims=True)

        dy = err * (1.0 / D_MODEL)
        dy_ref[...] = dy
        gpart = jnp.sum(dy * n, axis=0, keepdims=True)
        dn = dy * pg
        dt = (r3 * (dn - n * jnp.mean(dn * n, axis=-1, keepdims=True))).astype(BF16)
        dt_ref[...] = dt
        dmg = _nt(dt, wout_ref[...])
        dpa = (dmg * sgm).astype(BF16)
        dpd = (dmg * sgd).astype(BF16)
        dpa_ref[...] = dpa
        dpd_ref[...] = dpd
        dp_ref[:, 0:1024] = (dmg * pa * sgm * (1.0 - sgm)).astype(BF16)
        dp_ref[:, 1024:2048] = (dmg * pd * sgd * (1.0 - sgd)).astype(BF16)
        dua = _nt(dpa, wpm_ref[...])
        dud = _nt(dpd, wpd_ref[...])
        dya_ref[...] = dua * sm
        dyd_ref[...] = dud * sd
        dp_ref[:, 2048:2560] = (dua * ya * szm * (1.0 + zm * (1.0 - szm))).astype(BF16)
        dp_ref[:, 2560:3072] = (dud * yd * szd * (1.0 + zd * (1.0 - szd))).astype(BF16)

        @pl.when(pl.program_id(0) == 0)
        def _():
            loss_ref[...] = lpart
            dgp_ref[...] = gpart

        @pl.when(pl.program_id(0) > 0)
        def _():
            loss_ref[...] += lpart
            dgp_ref[...] += gpart

    def rows(w):
        return pl.BlockSpec((tm, w), lambda i: (i, 0))

    def full(shape):
        return pl.BlockSpec(shape, lambda i: (0, 0))

    def sds(w, dt):
        return jax.ShapeDtypeStruct((SEQ, w), dt)

    return pl.pallas_call(
        body, name="tail", grid=(SEQ // tm,),
        in_specs=[rows(3072), rows(512), rows(512), rows(512), rows(512), rows(512), rows(512), rows(512),
                  rows(1024), rows(1024), full((512, 1024)), full((512, 1024)), full((1024, 1024)), full((1, 1024))],
        out_specs=[rows(3072), rows(1024), rows(1024), rows(1024), rows(512), rows(1024), rows(512), rows(1024),
                   rows(512), rows(512), rows(512), rows(512), full((1, 1024)), full((1, 1024))],
        out_shape=[sds(N_PAD, BF16), sds(1024, F32), sds(1024, BF16), sds(1024, BF16), sds(512, BF16),
                   sds(1024, BF16), sds(512, BF16), sds(1024, BF16), sds(512, F32), sds(512, F32),
                   sds(512, F32), sds(512, F32),
                   jax.ShapeDtypeStruct((1, 1024), F32), jax.ShapeDtypeStruct((1, 1024), F32)],
        compiler_params=_cparams(),
    )(p, ya, o_g[0], o_g[1], o_g[2], l_g[0], l_g[1], l_g[2], x, target, wpm, wpd, wout, post_g)


def _sum_parts(recv, own, me, tr, name):
    n, r, w = recv.shape
    own_spec = (pl.BlockSpec((tr, w), lambda i, me_ref: (i, 0)) if own.ndim == 2
                else pl.BlockSpec((None, tr, w), lambda i, me_ref: (me_ref[0], i, 0)))

    def body(me_ref, p_ref, own_ref, o_ref):
        mine = own_ref[...].astype(F32)
        acc = jnp.zeros((tr, w), F32)
        for s in range(n):
            acc = acc + jnp.where(me_ref[0] == s, mine, p_ref[s].astype(F32))
        o_ref[...] = acc

    return pl.pallas_call(
        body, name=name,
        grid_spec=pltpu.PrefetchScalarGridSpec(
            num_scalar_prefetch=1, grid=(r // tr,),
            in_specs=[pl.BlockSpec((n, tr, w), lambda i, me_ref: (0, i, 0)), own_spec],
            out_specs=pl.BlockSpec((tr, w), lambda i, me_ref: (i, 0))),
        out_shape=jax.ShapeDtypeStruct((r, w), F32),
    )(me.reshape(1), recv, own)


def _adamw(w, g, m, v, name):
    lead = w.shape[:-2]
    r, c = w.shape[-2:]
    tr = max([t for t in range(8, 257, 8) if r % t == 0], default=r)
    c1 = 1.0 - ADAM_B1 ** ADAM_STEP
    c2 = 1.0 - ADAM_B2 ** ADAM_STEP

    def body(w_ref, g_ref, m_ref, v_ref, d_ref, nm_ref, nv_ref):
        gv = g_ref[...]
        nm = ADAM_B1 * m_ref[...] + (1.0 - ADAM_B1) * gv
        nv = ADAM_B2 * v_ref[...] + (1.0 - ADAM_B2) * (gv * gv)
        nm_ref[...] = nm
        nv_ref[...] = nv
        d_ref[...] = -ADAM_LR * ((nm / c1) / (jnp.sqrt(nv / c2) + ADAM_EPS) + ADAM_WD * w_ref[...])

    zeros = (0,) * len(lead)
    spec = pl.BlockSpec((1,) * len(lead) + (tr, c), lambda i: zeros + (i, 0))
    sd = jax.ShapeDtypeStruct(w.shape, F32)
    return pl.pallas_call(
        body, name=name, grid=(r // tr,),
        in_specs=[spec] * 4, out_specs=[spec] * 3, out_shape=[sd] * 3,
    )(w, g, m, v)


ANY = pl.BlockSpec(memory_space=pl.ANY)


def _my_place():
    return lax.axis_index("x"), lax.axis_index("y"), lax.axis_index("c")


def _allgather_weights(mats):
    def body(*refs):
        w_refs, out_refs = refs[:N_MATS], refs[N_MATS:2 * N_MATS]
        send_sems, recv_sems = refs[2 * N_MATS:]
        x, y, c = _my_place()
        sibling = (x, y, 1 - c)
        chips = [(1 - x, y), (x, 1 - y), (1 - x, 1 - y)]

        def copy(k, src, dst, to):
            return pltpu.make_async_remote_copy(src_ref=src, dst_ref=dst, send_sem=send_sems.at[k],
                                                recv_sem=recv_sems.at[k], device_id=to, device_id_type=MESH)

        def half(mi, shard, hc):
            hr = SHARD_SHAPES[mi][0] // 2
            return out_refs[mi].at[shard, pl.ds(pl.multiple_of(hc * hr, 16), hr), :]

        started = []
        for mi in range(N_MATS):
            hr = SHARD_SHAPES[mi][0] // 2
            my_half = w_refs[mi].at[pl.ds(pl.multiple_of(c * hr, 16), hr), :]
            for j, (cx, cy) in enumerate(chips):
                cp = copy(mi * 6 + j, my_half, half(mi, 2 * x + y, c), (cx, cy, c))
                cp.start()
                started.append(cp)
        for mi in range(N_MATS):
            for j, (cx, cy) in enumerate(chips):
                landed = half(mi, 2 * cx + cy, c)
                copy(mi * 6 + j, landed, landed, (cx, cy, c)).wait_recv()
                fw = copy(mi * 6 + 3 + j, landed, landed, sibling)
                fw.start()
                started.append(fw)
        for mi in range(N_MATS):
            for j, (cx, cy) in enumerate(chips):
                other = half(mi, 2 * cx + cy, 1 - c)
                copy(mi * 6 + 3 + j, other, other, sibling).wait_recv()
        for cp in started:
            cp.wait_send()

    return pl.pallas_call(
        body, name="allgather_weights",
        in_specs=[ANY] * N_MATS, out_specs=[ANY] * N_MATS,
        out_shape=[jax.ShapeDtypeStruct((4, r, c), BF16) for r, c in SHARD_SHAPES],
        scratch_shapes=[pltpu.SemaphoreType.DMA((6 * N_MATS,)), pltpu.SemaphoreType.DMA((6 * N_MATS,))],
    )(*mats)


HBM = pl.BlockSpec(memory_space=pltpu.HBM)
SEM = pl.BlockSpec(memory_space=pltpu.SEMAPHORE)
DATAFLOW = pltpu.SideEffectType.DATAFLOW_SIDE_EFFECTING


def _peers(x, y, c):
    out = []
    for k in range(1, 8):
        px, py, pc = x ^ (k >> 2), y ^ ((k >> 1) & 1), c ^ (k & 1)
        out.append((k - 1, (px, py, pc), 4 * px + 2 * py + pc))
    return out


def _exchange_start(parts, name):
    n = len(parts)

    def body(*refs):
        p_refs, land_refs = refs[:n], refs[n:2 * n]
        send_sems, recv_sems, token = refs[2 * n], refs[2 * n + 1], refs[-1]
        x, y, c = _my_place()
        me = 4 * x + 2 * y + c
        for k, dev, peer in _peers(x, y, c):
            for mi in range(n):
                pltpu.make_async_remote_copy(
                    src_ref=p_refs[mi].at[peer], dst_ref=land_refs[mi].at[me], send_sem=send_sems.at[k * n + mi],
                    recv_sem=recv_sems.at[k * n + mi], device_id=dev, device_id_type=MESH).start()
        token[...] = jnp.zeros_like(token)

    hbm = [pltpu.HBM(p.shape, p.dtype) for p in parts]
    outs = pl.pallas_call(
        body, name=name + "_start",
        out_shape=(pltpu.SemaphoreType.DMA((7 * n,)), pltpu.SemaphoreType.DMA((7 * n,)), *hbm, *hbm,
                   jax.ShapeDtypeStruct((8, 128), F32)),
        in_specs=[HBM] * (2 * n), out_specs=(SEM, SEM, *[HBM] * (2 * n), pl.BlockSpec(memory_space=pltpu.VMEM)),
        input_output_aliases={i: 2 + i for i in range(2 * n)},
        compiler_params=pltpu.CompilerParams(has_side_effects=DATAFLOW),
    )(*[pltpu.with_memory_space_constraint(p, pltpu.HBM) for p in parts],
      *[pltpu.with_memory_space_constraint(lax.empty(p.shape, p.dtype), pltpu.HBM) for p in parts])
    return (name, outs[:-1]), outs[-1]


def _exchange_wait(handle, after):
    name, outs = handle
    n = (len(outs) - 2) // 2

    def body(*refs):
        p_refs, land_refs = refs[:n], refs[n:2 * n]
        send_sems, recv_sems = refs[2 * n], refs[2 * n + 1]
        x, y, c = _my_place()
        me = 4 * x + 2 * y + c
        for k, dev, peer in _peers(x, y, c):
            for mi in range(n):
                pltpu.make_async_remote_copy(
                    src_ref=p_refs[mi].at[peer], dst_ref=land_refs[mi].at[me], send_sem=send_sems.at[k * n + mi],
                    recv_sem=recv_sems.at[k * n + mi], device_id=dev, device_id_type=MESH).wait_send()
                slot = land_refs[mi].at[peer]
                pltpu.make_async_remote_copy(
                    src_ref=slot, dst_ref=slot, send_sem=send_sems.at[k * n + mi],
                    recv_sem=recv_sems.at[k * n + mi], device_id=dev, device_id_type=MESH).wait_recv()

    bufs = outs[2:]
    res = pl.pallas_call(
        body, name=name + "_wait", out_shape=tuple(pltpu.HBM(b.shape, b.dtype) for b in bufs),
        in_specs=[HBM] * (2 * n) + [SEM, SEM, ANY], out_specs=tuple([HBM] * (2 * n)),
        input_output_aliases={i: i for i in range(2 * n)},
        compiler_params=pltpu.CompilerParams(has_side_effects=DATAFLOW),
    )(*bufs, outs[0], outs[1], after)
    return list(res[n:])


def _swap_halves(halves, gvec):
    def body(*refs):
        g_refs, gv_ref = refs[:N_MATS], refs[N_MATS]
        out_refs, rg_ref = refs[N_MATS + 1:2 * N_MATS + 1], refs[2 * N_MATS + 1]
        send_sems, recv_sems = refs[2 * N_MATS + 2:]
        x, y, c = _my_place()
        me = 4 * x + 2 * y + c
        sends = []
        for mi in range(N_MATS):
            cp = pltpu.make_async_remote_copy(src_ref=g_refs[mi], dst_ref=out_refs[mi].at[c], send_sem=send_sems.at[mi],
                                              recv_sem=recv_sems.at[mi], device_id=(x, y, 1 - c), device_id_type=MESH)
            cp.start()
            sends.append(cp)
        for k, dev, peer in _peers(x, y, c):
            cp = pltpu.make_async_remote_copy(src_ref=gv_ref, dst_ref=rg_ref.at[me], send_sem=send_sems.at[N_MATS + k],
                                              recv_sem=recv_sems.at[N_MATS + k], device_id=dev, device_id_type=MESH)
            cp.start()
            sends.append(cp)
        for mi in range(N_MATS):
            got = out_refs[mi].at[1 - c]
            pltpu.make_async_remote_copy(src_ref=got, dst_ref=got, send_sem=send_sems.at[mi], recv_sem=recv_sems.at[mi],
                                         device_id=(x, y, 1 - c), device_id_type=MESH).wait_recv()
        for k, dev, peer in _peers(x, y, c):
            got = rg_ref.at[peer]
            pltpu.make_async_remote_copy(src_ref=got, dst_ref=got, send_sem=send_sems.at[N_MATS + k],
                                         recv_sem=recv_sems.at[N_MATS + k], device_id=dev, device_id_type=MESH).wait_recv()
        for cp in sends:
            cp.wait_send()

    outs = pl.pallas_call(
        body, name="swap_halves",
        in_specs=[ANY] * (N_MATS + 1), out_specs=[ANY] * (N_MATS + 1),
        out_shape=[jax.ShapeDtypeStruct((2, r // 2, c), F32) for r, c in SHARD_SHAPES]
        + [jax.ShapeDtypeStruct((8, 8, N_GAINS), F32)],
        scratch_shapes=[pltpu.SemaphoreType.DMA((N_MATS + 7,)), pltpu.SemaphoreType.DMA((N_MATS + 7,))],
    )(*halves, gvec)
    return outs[:N_MATS], outs[N_MATS]


def _set_slot(arr, block, idx):
    return lax.dynamic_update_slice(arr, block[None], (idx,) + (0,) * block.ndim)


PAD_RUNS = ((6304, 8352, 0), (5280, 6304, COL_Z), (672, 5280, COL_QKV), (0, 640, COL_LAT), (640, 672, COL_LAT + 704))
W_IN_SHARD = 2088


def _full_weights(gathered):
    def cols(a):
        return jnp.concatenate([a[s] for s in range(4)], axis=1)

    w_uq, w_ukv, w_pm, w_pd = [cols(a) for a in gathered[1:5]]
    w_out = gathered[5].reshape(D_MODEL, D_MODEL)
    g_in = gathered[0]
    pieces, at = [], 0
    for lo, hi, pad_lo in sorted(PAD_RUNS, key=lambda t: t[2]):
        if pad_lo > at:
            pieces.append(jnp.zeros((D_MODEL, pad_lo - at), g_in.dtype))
        for s in range(4):
            a_, b_ = max(lo, s * W_IN_SHARD), min(hi, (s + 1) * W_IN_SHARD)
            if a_ < b_:
                pieces.append(g_in[s][:, a_ - s * W_IN_SHARD:b_ - s * W_IN_SHARD])
        at = pad_lo + hi - lo
    pieces.append(jnp.zeros((D_MODEL, N_PAD - at), g_in.dtype))
    w_pad = jnp.concatenate(pieces, axis=1)
    z32 = jnp.zeros((Q_RANK, 32), w_uq.dtype)
    wuq_pad = jnp.concatenate([t for h in range(MLA_HEADS) for t in (w_uq[:, h * 96:(h + 1) * 96], z32)], axis=1)
    z64 = jnp.zeros((KV_RANK, 64), w_ukv.dtype)
    wk_pad = jnp.concatenate([t for h in range(MLA_HEADS) for t in (w_ukv[:, h * 128:h * 128 + 64], z64)], axis=1)
    wv = jnp.concatenate([w_ukv[:, h * 128 + 64:(h + 1) * 128] for h in range(MLA_HEADS)], axis=1)
    return w_pad, wuq_pad, wk_pad, wv, w_pm, w_pd, w_out


W_IN_LAT = 672


def _grad_parts_in_early(dw_early):
    def in_block(s, h):
        rows = slice(h * 512, (h + 1) * 512)
        out = []
        for lo, hi, pad_lo in sorted(PAD_RUNS):
            a_, b_ = max(lo, s * W_IN_SHARD), min(hi, (s + 1) * W_IN_SHARD)
            if a_ < b_:
                out.append(jnp.zeros((512, b_ - a_), dw_early.dtype) if pad_lo >= COL_LAT
                           else dw_early[rows, pad_lo + a_ - lo:pad_lo + b_ - lo])
        return jnp.concatenate(out, axis=1)

    return jnp.stack([in_block(s, h) for s in range(4) for h in range(2)])


def _grad_parts_in_late(dw_late):
    cols = jnp.concatenate([dw_late[:, 0:640], dw_late[:, 704:736]], axis=1)
    zero = jnp.zeros((512, W_IN_LAT), dw_late.dtype)
    return jnp.stack([cols[0:512], cols[512:1024]] + [zero] * 6)


def _col_blocks(m):
    r, c = m.shape[0] // 2, m.shape[1] // 4
    return jnp.stack([m[h * r:(h + 1) * r, s * c:(s + 1) * c] for s in range(4) for h in range(2)])


def _grad_parts_mla(dwuq_pad, dwk_pad, dwv):
    d_uq = jnp.concatenate([dwuq_pad[:, h * 128:h * 128 + 96] for h in range(MLA_HEADS)], axis=1)
    d_ukv = jnp.concatenate([t for h in range(MLA_HEADS) for t in (dwk_pad[:, h * 128:h * 128 + 64], dwv[:, h * 64:(h + 1) * 64])],
                            axis=1)
    return [_col_blocks(d_uq), _col_blocks(d_ukv)]


def _rope_tables(positions):
    pos = positions.reshape(SEQ).astype(F32)
    lane = jnp.arange(128)

    def table(rot, first, period):
        inv = ROPE_THETA ** (-jnp.arange(0, rot, 2, dtype=F32) / rot)
        half = rot // 2
        off = lane % period - first
        in1, in2 = (off >= 0) & (off < half), (off >= half) & (off < rot)
        inv_lane = jnp.where(in1 | in2, inv[jnp.clip(off % half, 0, half - 1)], 0.0)
        sign = jnp.where(in1, -1.0, 1.0).astype(F32)
        ang = pos[:, None] * inv_lane[None, :]
        return jnp.cos(ang), jnp.sin(ang) * sign[None, :]

    return table(32, 64, 128), table(16, 0, 64)


def _device_grads(x, positions, target, gains, gathered, send):
    pre_g, q_g, kv_g, post_g = gains
    w_pad, wuq_pad, wk_pad, wv, w_pm, w_pd, w_out = _full_weights(gathered)
    (mc, ms), (dc, ds) = _rope_tables(positions)

    h, h_t = _prenorm_fwd(x, pre_g)
    p = _matmul(h, w_pad, "nn", F32, 1024, 1408, 1024, "in_proj")
    cqn, ckvn, q, k, v = _mla_prep_fwd(p, q_g, kv_g, wuq_pad, wk_pad, wv, mc, ms)
    ya, lse_m = _mla_flash_fwd(q, k, v)
    qkv = [_dil_prep_fwd(p, dc, ds, g) for g in range(3)]
    o_g, l_g = zip(*[_dil_attn_fwd2(qkv[g], g) for g in range(3)])
    (dp, dy, mg, dt, ua, dpa, ud, dpd, dya, dyd, yd, lse_d, loss_cols, dg_post) = _tail(
        p, ya, o_g, l_g, x, target, w_pm, w_pd, w_out, post_g)

    for g in range(3):
        dqkv = _dil_attn_bwd2(qkv[g], dyd, yd, lse_d, g)
        dp = _dil_prep_bwd(dp, dqkv, dc, ds, g)
    dw_early = _matmul(h_t, dp, "nn", BF16, 1024, 1536, 2048, "dw_in_early", b_cols=(0, COL_LAT // 1536))
    dwpm = _matmul(ua, dpa, "tn", BF16, 512, 1024, 512, "dw_proj_mla")
    dwpd = _matmul(ud, dpd, "tn", BF16, 512, 1024, 512, "dw_proj_dil")
    dwout = _matmul(mg, dt, "tn", BF16, 1024, 1024, 512, "dw_out")
    early, token = send([_grad_parts_in_early(dw_early), _col_blocks(dwpm), _col_blocks(dwpd),
                         dwout.reshape(8, 128, D_MODEL)], "exchange_early")

    dq, dk, dv = _mla_flash_bwd(q, k, v, ya, dya, lse_m, token)
    dp, dqb, dkb, dvb, dg_q, dg_kv = _mla_prep_bwd(dp, p, dq, dk, dv, q_g, kv_g, wuq_pad, wk_pad, wv, mc, ms)
    dwuq_pad = _matmul(cqn, dqb, "tn", BF16, Q_RANK, 1024, 512, "dw_uq")
    dwk_pad = _matmul(ckvn, dkb, "tn", BF16, KV_RANK, 1024, 512, "dw_k")
    dwv = _matmul(ckvn, dvb, "tn", BF16, KV_RANK, 512, 512, "dw_v")
    dw_late = _matmul(h_t, dp, "nn", BF16, 1024, N_LAT, 2048, "dw_in_late", b_cols=(COL_LAT // N_LAT, 1))
    late, token = send([_grad_parts_in_late(dw_late)] + _grad_parts_mla(dwuq_pad, dwk_pad, dwv), "exchange_late")

    dh = _matmul(dp, w_pad, "nt", F32, 1024, 1024, 1408, "dh", token)
    grad_x, dg_pre = _prenorm_bwd(x, dh, dy, pre_g)

    gvec = jnp.concatenate([dg_pre, dg_q, dg_kv, dg_post], axis=1)
    return loss_cols, grad_x, (early, late), gvec


def kernel(x, positions, pre_norm_g, w_in, q_norm_g, w_uq, kv_norm_g, w_ukv, w_proj_mla, w_proj_dil, w_out, post_norm_g, loss_target, m_pre_norm_g, m_w_in, m_q_norm_g, m_w_uq, m_kv_norm_g, m_w_ukv, m_w_proj_mla, m_w_proj_dil, m_w_out, m_post_norm_g, v_pre_norm_g, v_w_in, v_q_norm_g, v_w_uq, v_kv_norm_g, v_w_ukv, v_w_proj_mla, v_w_proj_dil, v_w_out, v_post_norm_g):
    xi, yi, ci = _my_place()
    chip, me = 2 * xi + yi, 4 * xi + 2 * yi + ci
    mats = [w.reshape(w.shape[1:]).astype(BF16) for w in (w_in, w_uq, w_ukv, w_proj_mla, w_proj_dil, w_out)]
    gathered = [_set_slot(g, m, chip) for g, m in zip(_allgather_weights(mats), mats)]
    gains = (pre_norm_g, q_norm_g, kv_norm_g, post_norm_g)
    sent = {}

    def send(blocks, name):
        sent[name] = blocks
        return _exchange_start(blocks, name)

    loss_cols, grad_x, (early, late), gvec = _device_grads(x[0], positions, loss_target[0], gains, gathered, send)

    loss = lax.psum(jnp.sum(loss_cols) * (0.5 / D_MODEL), ("x", "y", "c"))

    recv_e, recv_l = _exchange_wait(early, grad_x), _exchange_wait(late, grad_x)
    own_e, own_l = sent["exchange_early"], sent["exchange_late"]
    in_e = _sum_parts(recv_e[0], own_e[0], me, 64, "sum_grad_in_early")
    in_l = _sum_parts(recv_l[0], own_l[0], me, 64, "sum_grad_in_late")
    half_in = jnp.concatenate([in_e[:, :W_IN_LAT] + jnp.where(chip == 0, in_l, 0.0), in_e[:, W_IN_LAT:]], axis=1)
    halves = [half_in,
              _sum_parts(recv_l[1], own_l[1], me, 64, "sum_grad_uq"), _sum_parts(recv_l[2], own_l[2], me, 64, "sum_grad_ukv"),
              _sum_parts(recv_e[1], own_e[1], me, 64, "sum_grad_pm"), _sum_parts(recv_e[2], own_e[2], me, 64, "sum_grad_pd"),
              _sum_parts(recv_e[3], own_e[3], me, 64, "sum_grad_out")]
    gvec8 = jnp.pad(gvec, ((0, 7), (0, 0)))
    swapped, recv_gains = _swap_halves(halves, gvec8)
    g_gains = _sum_parts(recv_gains, gvec8, me, 8, "sum_gain_parts")[0:1]
    g_mats = [_set_slot(s, hf, ci).reshape((1,) + shp) for s, hf, shp in zip(swapped, halves, SHARD_SHAPES)]

    off = [0, 1024, 1408, 1664, 2688]
    g_gain = [g_gains[:, off[i]:off[i + 1]] for i in range(4)]
    grads = [g_gain[0], g_mats[0], g_gain[1], g_mats[1], g_gain[2], g_mats[2], g_mats[3], g_mats[4], g_mats[5], g_gain[3]]
    ws = [pre_norm_g, w_in, q_norm_g, w_uq, kv_norm_g, w_ukv, w_proj_mla, w_proj_dil, w_out, post_norm_g]
    ms = [m_pre_norm_g, m_w_in, m_q_norm_g, m_w_uq, m_kv_norm_g, m_w_ukv, m_w_proj_mla, m_w_proj_dil, m_w_out, m_post_norm_g]
    vs = [v_pre_norm_g, v_w_in, v_q_norm_g, v_w_uq, v_kv_norm_g, v_w_ukv, v_w_proj_mla, v_w_proj_dil, v_w_out, v_post_norm_g]
    deltas, new_m, new_v = [], [], []
    for i, (w, g, m, v) in enumerate(zip(ws, grads, ms, vs)):
        if w.shape[-1] % 128 and w.shape[-2] % 128 == 0:
            g = jnp.swapaxes(g, 1, 2)
            grads[i] = jnp.swapaxes(g, 1, 2)
            d_, m_, v_ = [jnp.swapaxes(o, 1, 2) for o in
                          _adamw(jnp.swapaxes(w, 1, 2), g, jnp.swapaxes(m, 1, 2), jnp.swapaxes(v, 1, 2), f"adamw_{i}")]
        else:
            d_, m_, v_ = _adamw(w, g, m, v, f"adamw_{i}")
        deltas.append(d_)
        new_m.append(m_)
        new_v.append(v_)
    return (loss, grad_x.reshape(x.shape), *grads, *deltas, *new_m, *new_v)
```

```python
import jax
import jax.numpy as jnp
from jax import lax
from jax.experimental import pallas as pl
from jax.experimental.pallas import tpu as pltpu

F32 = jnp.float32
BF16 = jnp.bfloat16

SEQ = 4096
D_MODEL = 1024
EPS = 1e-6
ROPE_THETA = 500000.0
MLA_HEADS = 8
Q_RANK = 384
KV_RANK = 256
MLA_SCALE = 96.0 ** -0.5
MLA_ROPE_HALF = 16
DIL_DILATIONS = (1, 4, 16)
DIL_ROPE_HALF = 8
DIL_SCALE = 0.125
BAND = 128

N_LAT = 768
COL_Z, COL_QKV, COL_LAT = 2048, 3072, 7680
N_PAD = 8448
IN_SPLITS = (384, 256, 32, 4608, 512, 512, 1024, 1024)

SHARD_SHAPES = ((1024, 2088), (384, 192), (256, 256), (512, 256), (512, 256), (256, 1024))
N_MATS = len(SHARD_SHAPES)
N_GAINS = 2688

ADAM_LR, ADAM_B1, ADAM_B2, ADAM_EPS, ADAM_WD, ADAM_STEP = 0.001, 0.9, 0.999, 1e-08, 0.01, 10

VMEM_LIMIT = 56 * 1024 * 1024
NEG = -1e30
MESH = pl.DeviceIdType.MESH


def _cparams(**kw):
    return pltpu.CompilerParams(vmem_limit_bytes=VMEM_LIMIT, **kw)


def _dot(a, b, dims):
    return lax.dot_general(a, b, (dims, ((), ())), preferred_element_type=F32)


def _nn(a, b):
    return _dot(a, b, ((1,), (0,)))


def _nt(a, b):
    return _dot(a, b, ((1,), (1,)))


def _tn(a, b):
    return _dot(a, b, ((0,), (0,)))


def _rope_lanes(shape, half, period, first):
    lane = lax.broadcasted_iota(jnp.int32, shape, len(shape) - 1) % period
    return (lane >= first) & (lane < first + half), (lane >= first + half) & (lane < first + 2 * half)


def _rope_fwd(x, c, s, half, lanes):
    x1, _ = lanes
    return x * c + jnp.where(x1, pltpu.roll(x, 128 - half, 1), pltpu.roll(x, half, 1)) * s


def _rope_bwd(g, c, s, half, lanes):
    x1, x2 = lanes
    gs = g * s
    return g * c + jnp.where(x2, pltpu.roll(gs, half, 1), jnp.where(x1, pltpu.roll(gs, 128 - half, 1), 0.0))


def _sigmoid(x):
    return 1.0 / (1.0 + jnp.exp(-x))


def _after(token):
    return ([], []) if token is None else ([token], [pl.BlockSpec(memory_space=pl.ANY)])


def _matmul(a, b, mode, out_dtype, tm, tn, tk, name, token=None, b_cols=None):
    after, after_specs = _after(token)
    if mode == "nn":
        (m, k), n = a.shape, b.shape[1]
        first = 0
        if b_cols is not None:
            first, n = b_cols[0], b_cols[1] * tn
        a_spec = pl.BlockSpec((tm, tk), lambda j, i, kk: (i, kk))
        b_spec = pl.BlockSpec((tk, tn), lambda j, i, kk: (kk, j + first))
        dot = _nn
    elif mode == "nt":
        (m, k), n = a.shape, b.shape[0]
        a_spec = pl.BlockSpec((tm, tk), lambda j, i, kk: (i, kk))
        b_spec = pl.BlockSpec((tn, tk), lambda j, i, kk: (j, kk))
        dot = _nt
    else:
        (k, m), n = a.shape, b.shape[1]
        a_spec = pl.BlockSpec((tk, tm), lambda j, i, kk: (kk, i))
        b_spec = pl.BlockSpec((tk, tn), lambda j, i, kk: (kk, j))
        dot = _tn
    assert m % tm == 0 and n % tn == 0 and k % tk == 0, (name, m, n, k, tm, tn, tk)
    nk = k // tk

    def body(a_ref, b_ref, *rest):
        o_ref, acc_ref = rest[-2:]
        kk = pl.program_id(2)
        part = dot(a_ref[...], b_ref[...])

        @pl.when(kk == 0)
        def _():
            acc_ref[...] = part

        @pl.when(kk > 0)
        def _():
            acc_ref[...] += part

        @pl.when(kk == nk - 1)
        def _():
            o_ref[...] = acc_ref[...].astype(o_ref.dtype)

    return pl.pallas_call(
        body, name=name, grid=(n // tn, m // tm, nk),
        in_specs=[a_spec, b_spec] + after_specs,
        out_specs=pl.BlockSpec((tm, tn), lambda j, i, kk: (i, j)),
        out_shape=jax.ShapeDtypeStruct((m, n), out_dtype),
        scratch_shapes=[pltpu.VMEM((tm, tn), F32)],
        compiler_params=_cparams(),
    )(a, b, *after)


def _prenorm_fwd(x, g):
    tm = 512

    def body(x_ref, g_ref, h_ref, ht_ref):
        xv = x_ref[...]
        r = lax.rsqrt(jnp.mean(xv * xv, axis=-1, keepdims=True) + EPS)
        hv = (xv * r * g_ref[...]).astype(BF16)
        h_ref[...] = hv
        ht_ref[...] = hv.T

    return pl.pallas_call(
        body, name="prenorm_fwd", grid=(SEQ // tm,),
        in_specs=[pl.BlockSpec((tm, D_MODEL), lambda i: (i, 0)), pl.BlockSpec((1, D_MODEL), lambda i: (0, 0))],
        out_specs=[pl.BlockSpec((tm, D_MODEL), lambda i: (i, 0)), pl.BlockSpec((D_MODEL, tm), lambda i: (0, i))],
        out_shape=[jax.ShapeDtypeStruct((SEQ, D_MODEL), BF16), jax.ShapeDtypeStruct((D_MODEL, SEQ), BF16)],
    )(x, g)


def _prenorm_bwd(x, dh, dy, g):
    tm = 512

    def body(x_ref, dh_ref, dy_ref, g_ref, gx_ref, dg_ref):
        xv = x_ref[...]
        r = lax.rsqrt(jnp.mean(xv * xv, axis=-1, keepdims=True) + EPS)
        n = xv * r
        dhv = dh_ref[...]
        dn = dhv * g_ref[...]
        gx_ref[...] = dy_ref[...] + r * (dn - n * jnp.mean(dn * n, axis=-1, keepdims=True))
        part = jnp.sum(dhv * n, axis=0, keepdims=True)

        @pl.when(pl.program_id(0) == 0)
        def _():
            dg_ref[...] = part

        @pl.when(pl.program_id(0) > 0)
        def _():
            dg_ref[...] += part

    row = pl.BlockSpec((tm, D_MODEL), lambda i: (i, 0))
    vec = pl.BlockSpec((1, D_MODEL), lambda i: (0, 0))
    return pl.pallas_call(
        body, name="prenorm_bwd", grid=(SEQ // tm,),
        in_specs=[row, row, row, vec], out_specs=[row, vec],
        out_shape=[jax.ShapeDtypeStruct((SEQ, D_MODEL), F32), jax.ShapeDtypeStruct((1, D_MODEL), F32)],
        compiler_params=_cparams(),
    )(x, dh, dy, g)


def _mla_prep_fwd(p, qg, kvg, wuq, wk, wv, rc, rs):
    tm = 512

    def body(lat_ref, qg_ref, kvg_ref, wuq_ref, wk_ref, wv_ref, c_ref, s_ref,
             cqn_ref, ckvn_ref, q_ref, k_ref, v_ref):
        c, s = c_ref[...], s_ref[...]
        lanes = _rope_lanes((tm, 128), MLA_ROPE_HALF, 128, 64)
        cq = lat_ref[:, 0:Q_RANK]
        r1 = lax.rsqrt(jnp.mean(cq * cq, axis=-1, keepdims=True) + EPS)
        cqn = (cq * r1 * qg_ref[...]).astype(BF16)
        cqn_ref[...] = cqn
        q = _nn(cqn, wuq_ref[...])
        for h in range(MLA_HEADS):
            sl = slice(h * 128, (h + 1) * 128)
            q_ref[:, sl] = (_rope_fwd(q[:, sl], c, s, MLA_ROPE_HALF, lanes) * MLA_SCALE).astype(BF16)
        ckv = lat_ref[:, Q_RANK:Q_RANK + KV_RANK]
        r2 = lax.rsqrt(jnp.mean(ckv * ckv, axis=-1, keepdims=True) + EPS)
        ckvn = (ckv * r2 * kvg_ref[...]).astype(BF16)
        ckvn_ref[...] = ckvn
        krr = _rope_fwd(lat_ref[:, Q_RANK + KV_RANK:N_LAT], c, s, MLA_ROPE_HALF, lanes)
        kn = _nn(ckvn, wk_ref[...])
        for h in range(MLA_HEADS):
            sl = slice(h * 128, (h + 1) * 128)
            k_ref[:, sl] = (kn[:, sl] + krr).astype(BF16)
        v_ref[...] = _nn(ckvn, wv_ref[...]).astype(BF16)

    def full(shape):
        return pl.BlockSpec(shape, lambda i: (0, 0))

    def rows(w):
        return pl.BlockSpec((tm, w), lambda i: (i, 0))

    return pl.pallas_call(
        body, name="mla_prep_fwd", grid=(SEQ // tm,),
        in_specs=[pl.BlockSpec((tm, N_LAT), lambda i: (i, COL_LAT // N_LAT)),
                  full((1, Q_RANK)), full((1, KV_RANK)), full((Q_RANK, 1024)), full((KV_RANK, 1024)),
                  full((KV_RANK, 512)), rows(128), rows(128)],
        out_specs=[rows(Q_RANK), rows(KV_RANK), rows(1024), rows(1024), rows(512)],
        out_shape=[jax.ShapeDtypeStruct((SEQ, Q_RANK), BF16), jax.ShapeDtypeStruct((SEQ, KV_RANK), BF16),
                   jax.ShapeDtypeStruct((SEQ, 1024), BF16), jax.ShapeDtypeStruct((SEQ, 1024), BF16),
                   jax.ShapeDtypeStruct((SEQ, 512), BF16)],
        compiler_params=_cparams(),
    )(p, qg, kvg, wuq, wk, wv, rc, rs)


def _mla_prep_bwd(dp_in, p, dq, dk, dv, qg, kvg, wuq, wk, wv, rc, rs):
    tm = 512

    def body(dp_any, lat_ref, dq_ref, dk_ref, dv_ref, qg_ref, kvg_ref, wuq_ref, wk_ref, wv_ref,
             c_ref, s_ref, dp_ref, dqb_ref, dkb_ref, dvb_ref, dgq_ref, dgkv_ref):
        del dp_any
        c, s = c_ref[...], s_ref[...]
        lanes = _rope_lanes((tm, 128), MLA_ROPE_HALF, 128, 64)
        lane = lax.broadcasted_iota(jnp.int32, (tm, 128), 1)
        dkr = jnp.zeros((tm, 128), F32)
        for h in range(MLA_HEADS):
            sl = slice(h * 128, (h + 1) * 128)
            dqb_ref[:, sl] = _rope_bwd(dq_ref[:, sl] * MLA_SCALE, c, s, MLA_ROPE_HALF, lanes).astype(BF16)
            dkh = dk_ref[:, sl]
            dkr = dkr + dkh
            dkb_ref[:, sl] = jnp.where(lane < 64, dkh, 0.0).astype(BF16)
        dkr = jnp.where((lane >= 64) & (lane < 96), dkr, 0.0)
        dkr = _rope_bwd(dkr, c, s, MLA_ROPE_HALF, lanes)
        dvb = dv_ref[...].astype(BF16)
        dvb_ref[...] = dvb

        cq = lat_ref[:, 0:Q_RANK]
        r1 = lax.rsqrt(jnp.mean(cq * cq, axis=-1, keepdims=True) + EPS)
        n1 = cq * r1
        dcqn = _nt(dqb_ref[...], wuq_ref[...])
        dn1 = dcqn * qg_ref[...]
        dcq = r1 * (dn1 - n1 * jnp.mean(dn1 * n1, axis=-1, keepdims=True))
        pq = jnp.sum(dcqn * n1, axis=0, keepdims=True)

        ckv = lat_ref[:, Q_RANK:Q_RANK + KV_RANK]
        r2 = lax.rsqrt(jnp.mean(ckv * ckv, axis=-1, keepdims=True) + EPS)
        n2 = ckv * r2
        dckvn = _nt(dkb_ref[...], wk_ref[...]) + _nt(dvb, wv_ref[...])
        dn2 = dckvn * kvg_ref[...]
        dckv = r2 * (dn2 - n2 * jnp.mean(dn2 * n2, axis=-1, keepdims=True))
        pkv = jnp.sum(dckvn * n2, axis=0, keepdims=True)

        dp_ref[:, 0:Q_RANK] = dcq.astype(BF16)
        dp_ref[:, Q_RANK:Q_RANK + KV_RANK] = dckv.astype(BF16)
        dp_ref[:, Q_RANK + KV_RANK:N_LAT] = dkr.astype(BF16)

        @pl.when(pl.program_id(0) == 0)
        def _():
            dgq_ref[...] = pq
            dgkv_ref[...] = pkv

        @pl.when(pl.program_id(0) > 0)
        def _():
            dgq_ref[...] += pq
            dgkv_ref[...] += pkv

    def full(shape):
        return pl.BlockSpec(shape, lambda i: (0, 0))

    def rows(w):
        return pl.BlockSpec((tm, w), lambda i: (i, 0))

    lat = pl.BlockSpec((tm, N_LAT), lambda i: (i, COL_LAT // N_LAT))
    return pl.pallas_call(
        body, name="mla_prep_bwd", grid=(SEQ // tm,),
        in_specs=[pl.BlockSpec(memory_space=pl.ANY), lat, rows(1024), rows(1024), rows(512),
                  full((1, Q_RANK)), full((1, KV_RANK)), full((Q_RANK, 1024)), full((KV_RANK, 1024)),
                  full((KV_RANK, 512)), rows(128), rows(128)],
        out_specs=[lat, rows(1024), rows(1024), rows(512), full((1, Q_RANK)), full((1, KV_RANK))],
        out_shape=[jax.ShapeDtypeStruct((SEQ, N_PAD), BF16), jax.ShapeDtypeStruct((SEQ, 1024), BF16),
                   jax.ShapeDtypeStruct((SEQ, 1024), BF16), jax.ShapeDtypeStruct((SEQ, 512), BF16),
                   jax.ShapeDtypeStruct((1, Q_RANK), F32), jax.ShapeDtypeStruct((1, KV_RANK), F32)],
        input_output_aliases={0: 0},
        compiler_params=_cparams(),
    )(dp_in, p, dq, dk, dv, qg, kvg, wuq, wk, wv, rc, rs)


FLASH_T = 512


def _head_half(shape, hh):
    lane = lax.broadcasted_iota(jnp.int32, shape, 1)
    return (lane < 64) if hh == 0 else (lane >= 64)


def _causal_keep(t):
    row = lax.broadcasted_iota(jnp.int32, (t, t), 0)
    col = lax.broadcasted_iota(jnp.int32, (t, t), 1)
    return row >= col


def _tri_steps(nb, q_major):
    if q_major:
        pairs = [(i, kb) for i in range(nb) for kb in range(i + 1)]
    else:
        pairs = [(i, kb) for kb in range(nb) for i in range(kb, nb)]
    return jnp.asarray([p[0] for p in pairs], jnp.int32), jnp.asarray([p[1] for p in pairs], jnp.int32)


def _mla_flash_fwd(q, k, v):
    t = FLASH_T
    nb = SEQ // t
    qtab, ktab = _tri_steps(nb, True)

    def body(qi_ref, ki_ref, q_ref, k_ref, v_ref, o_ref, lse_ref, m_scr, l_scr, acc_scr):
        step = pl.program_id(1)
        i, kb = qi_ref[step], ki_ref[step]

        @pl.when(kb == 0)
        def _():
            m_scr[...] = jnp.full_like(m_scr, NEG)
            l_scr[...] = jnp.zeros_like(l_scr)
            acc_scr[...] = jnp.zeros_like(acc_scr)

        def update(masked):
            vv = v_ref[...]
            for hh in range(2):
                sl = slice(hh * 128, (hh + 1) * 128)
                s = _nt(q_ref[:, sl], k_ref[:, sl])
                if masked:
                    s = jnp.where(_causal_keep(t), s, NEG)
                m_prev = m_scr[hh]
                m_new = jnp.maximum(m_prev, jnp.max(s, axis=-1, keepdims=True))
                pr = jnp.exp(s - jnp.tile(m_new, (1, t // 128)))
                alpha = jnp.exp(m_prev - m_new)
                l_scr[hh] = alpha * l_scr[hh] + jnp.sum(pr, axis=-1, keepdims=True)
                acc_scr[hh] = alpha * acc_scr[hh] + _nn(pr.astype(BF16), vv)
                m_scr[hh] = m_new

        @pl.when(kb < i)
        def _():
            update(False)

        @pl.when(kb == i)
        def _():
            update(True)
            o0 = acc_scr[0] / l_scr[0]
            o1 = acc_scr[1] / l_scr[1]
            o_ref[...] = jnp.where(_head_half((t, 128), 0), o0, o1)
            for hh in range(2):
                lse_ref[:, hh * 128:(hh + 1) * 128] = m_scr[hh] + jnp.log(l_scr[hh])

    grid_spec = pltpu.PrefetchScalarGridSpec(
        num_scalar_prefetch=2, grid=(4, qtab.shape[0]),
        in_specs=[pl.BlockSpec((t, 256), lambda j, s, qi, ki: (qi[s], j)),
                  pl.BlockSpec((t, 256), lambda j, s, qi, ki: (ki[s], j)),
                  pl.BlockSpec((t, 128), lambda j, s, qi, ki: (ki[s], j))],
        out_specs=[pl.BlockSpec((t, 128), lambda j, s, qi, ki: (qi[s], j)),
                   pl.BlockSpec((t, 256), lambda j, s, qi, ki: (qi[s], j))],
        scratch_shapes=[pltpu.VMEM((2, t, 128), F32), pltpu.VMEM((2, t, 128), F32), pltpu.VMEM((2, t, 128), F32)])
    return pl.pallas_call(
        body, name="mla_flash_fwd", grid_spec=grid_spec,
        out_shape=[jax.ShapeDtypeStruct((SEQ, 512), F32), jax.ShapeDtypeStruct((SEQ, 1024), F32)],
        compiler_params=_cparams(),
    )(qtab, ktab, q, k, v)


def _mla_flash_bwd(q, k, v, o, do, lse, token=None):
    t = FLASH_T
    nb = SEQ // t
    qtab, ktab = _tri_steps(nb, False)
    after, after_specs = _after(token)

    def body(qi_ref, ki_ref, q_ref, k_ref, v_ref, o_ref, do_ref, lse_ref, *rest):
        dq_ref, dk_ref, dv_ref, dk_scr, dv_scr = rest[-5:]
        step = pl.program_id(1)
        i, kb = qi_ref[step], ki_ref[step]

        @pl.when(step == 0)
        def _():
            dq_ref[...] = jnp.zeros_like(dq_ref)

        @pl.when(i == kb)
        def _():
            dk_scr[...] = jnp.zeros_like(dk_scr)
            dv_scr[...] = jnp.zeros_like(dv_scr)

        def update(masked):
            vv = v_ref[...]
            ov = o_ref[...]
            dov = do_ref[...]
            rows = pl.ds(pl.multiple_of(i * t, t), t)
            for hh in range(2):
                sl = slice(hh * 128, (hh + 1) * 128)
                qh, kh = q_ref[:, sl], k_ref[:, sl]
                s = _nt(qh, kh)
                if masked:
                    s = jnp.where(_causal_keep(t), s, NEG)
                pr = jnp.exp(s - jnp.tile(lse_ref[:, sl], (1, t // 128)))
                dom = jnp.where(_head_half((t, 128), hh), dov, 0.0)
                domb = dom.astype(BF16)
                dv_scr[...] += _tn(pr.astype(BF16), domb)
                dpr = _nt(domb, vv)
                delta = jnp.sum(dom * ov, axis=-1, keepdims=True)
                ds = (pr * (dpr - delta)).astype(BF16)
                dq_ref[rows, sl] += _nn(ds, kh)
                dk_scr[hh] += _tn(ds, qh)

        @pl.when(i > kb)
        def _():
            update(False)

        @pl.when(i == kb)
        def _():
            update(True)

        @pl.when(i == nb - 1)
        def _():
            dk_ref[:, 0:128] = dk_scr[0]
            dk_ref[:, 128:256] = dk_scr[1]
            dv_ref[...] = dv_scr[...]

    qi_map = lambda j, s, qi, ki: (qi[s], j)
    ki_map = lambda j, s, qi, ki: (ki[s], j)
    grid_spec = pltpu.PrefetchScalarGridSpec(
        num_scalar_prefetch=2, grid=(4, qtab.shape[0]),
        in_specs=[pl.BlockSpec((t, 256), qi_map), pl.BlockSpec((t, 256), ki_map), pl.BlockSpec((t, 128), ki_map),
                  pl.BlockSpec((t, 128), qi_map), pl.BlockSpec((t, 128), qi_map), pl.BlockSpec((t, 256), qi_map)]
        + after_specs,
        out_specs=[pl.BlockSpec((SEQ, 256), lambda j, s, qi, ki: (0, j)), pl.BlockSpec((t, 256), ki_map),
                   pl.BlockSpec((t, 128), ki_map)],
        scratch_shapes=[pltpu.VMEM((2, t, 128), F32), pltpu.VMEM((t, 128), F32)])
    return pl.pallas_call(
        body, name="mla_flash_bwd", grid_spec=grid_spec,
        out_shape=[jax.ShapeDtypeStruct((SEQ, 1024), F32), jax.ShapeDtypeStruct((SEQ, 1024), F32),
                   jax.ShapeDtypeStruct((SEQ, 512), F32)],
        compiler_params=_cparams(),
    )(qtab, ktab, q, k, v, o, do, lse, *after)


DIL_UNROLL = 4


def _strided(start, size, d):
    return pl.ds(start, size) if d == 1 else pl.ds(start, size, stride=d)


def _dil_prep_fwd(p, rc, rs, g):
    d = DIL_DILATIONS[g]
    sub_len = SEQ // d
    ch = min(sub_len, 512)

    def body(p_ref, c_ref, s_ref, o_ref, x_scr):
        tq = pl.program_id(0)
        lanes = _rope_lanes((ch, 128), DIL_ROPE_HALF, 64, 0)
        o_ref[0, 0:BAND, :] = jnp.zeros((BAND, 128), BF16)

        @pl.when(tq < 2)
        def _():
            mult = jnp.where(tq == 0, DIL_SCALE, 1.0).astype(F32)
            for c0 in range(0, SEQ, ch):
                rows = pl.ds(c0, ch)
                x_scr[rows, :] = _rope_fwd(p_ref[rows, :], c_ref[rows, :] * mult, s_ref[rows, :] * mult, DIL_ROPE_HALF, lanes)

        def gather(src):
            for r in range(d):
                for c0 in range(0, sub_len, ch):
                    at = BAND + r * sub_len + c0
                    o_ref[0, at:at + ch, :] = src[_strided(r + c0 * d, ch, d), :].astype(BF16)

        @pl.when(tq < 2)
        def _():
            gather(x_scr)

        @pl.when(tq == 2)
        def _():
            gather(p_ref)

    tab = pl.BlockSpec((SEQ, 128), lambda tq, pr: (0, 0))
    return pl.pallas_call(
        body, name=f"dil_prep_fwd_g{g}", grid=(3, 4),
        in_specs=[pl.BlockSpec((SEQ, 128), lambda tq, pr: (0, COL_QKV // 128 + (tq * 3 + g) * 4 + pr)), tab, tab],
        out_specs=pl.BlockSpec((1, BAND + SEQ, 128), lambda tq, pr: (tq, 0, pr)),
        out_shape=jax.ShapeDtypeStruct((3, BAND + SEQ, 512), BF16),
        scratch_shapes=[pltpu.VMEM((SEQ, 128), F32)],
        compiler_params=_cparams(),
    )(p, rc, rs)


DIL_ST = 1024
DIL_NB = DIL_ST // BAND


def _band_keep(g, b, t):
    nbs = SEQ // DIL_DILATIONS[g] // BAND
    row = lax.broadcasted_iota(jnp.int32, (BAND, 2 * BAND), 0)
    col = lax.broadcasted_iota(jnp.int32, (BAND, 2 * BAND), 1)
    cur = (col >= BAND) & (row >= col - BAND)
    prev = (col < BAND) & (col >= row)
    if nbs >= DIL_NB:
        if b > 0:
            return cur | prev
        return cur | (prev & ((t * DIL_NB) % nbs != 0))
    return cur | prev if b % nbs else cur


def _dil_tok(g, b, t):
    d = DIL_DILATIONS[g]
    nbs = SEQ // d // BAND
    gb = t * DIL_NB + b
    return _strided((gb % nbs) * BAND * d + gb // nbs, BAND, d)


def _dil_attn_fwd2(qkv, g):
    def body(q_ref, k_ref, v_ref, o_ref, l_ref, s_scr, p_scr, o_scr):
        t = pl.program_id(1)
        base = t * DIL_ST
        half0 = _head_half((DIL_ST, 128), 0)
        lse_h = []
        for hh in range(2):
            half = _head_half((BAND, 128), hh)
            for b in range(DIL_NB):
                qv = q_ref[0, pl.ds(pl.multiple_of(base + (b + 1) * BAND, BAND), BAND), :]
                k2 = k_ref[0, pl.ds(pl.multiple_of(base + b * BAND, BAND), 2 * BAND), :]
                sb = _nt(jnp.where(half, qv, jnp.zeros_like(qv)), k2)
                s_scr[b * BAND:(b + 1) * BAND, :] = jnp.where(_band_keep(g, b, t), sb, NEG)
            s = s_scr[...]
            m = jnp.max(s, axis=-1, keepdims=True)
            pr = jnp.exp(s - m)
            den = jnp.sum(pr, axis=-1, keepdims=True)
            p_scr[...] = pr.astype(BF16)
            for b in range(DIL_NB):
                v2 = v_ref[0, pl.ds(pl.multiple_of(base + b * BAND, BAND), 2 * BAND), :]
                o_scr[hh, b * BAND:(b + 1) * BAND, :] = _nn(p_scr[b * BAND:(b + 1) * BAND, :], v2)
            o_scr[hh] = o_scr[hh] / den
            lse_h.append(m + jnp.log(den))
        out = jnp.where(half0, o_scr[0], o_scr[1])
        lse = jnp.where(half0, lse_h[0], lse_h[1])
        for b in range(DIL_NB):
            tok = _dil_tok(g, b, t)
            o_ref[tok, :] = out[b * BAND:(b + 1) * BAND, :]
            l_ref[tok, :] = lse[b * BAND:(b + 1) * BAND, :]

    def inp(tq):
        return pl.BlockSpec((1, BAND + SEQ, 128), lambda pr, t: (tq, 0, pr))

    out = pl.BlockSpec((SEQ, 128), lambda pr, t: (0, pr))
    return pl.pallas_call(
        body, name=f"dil_attn_fwd_g{g}", grid=(4, SEQ // DIL_ST),
        in_specs=[inp(0), inp(1), inp(2)], out_specs=[out, out],
        out_shape=[jax.ShapeDtypeStruct((SEQ, 512), F32), jax.ShapeDtypeStruct((SEQ, 512), F32)],
        scratch_shapes=[pltpu.VMEM((DIL_ST, 2 * BAND), F32), pltpu.VMEM((DIL_ST, 2 * BAND), BF16),
                        pltpu.VMEM((2, DIL_ST, 128), F32)],
        compiler_params=_cparams(),
    )(qkv, qkv, qkv)


def _dil_attn_bwd2(qkv, dyd, yd, lse_all, g, token=None):
    d = DIL_DILATIONS[g]
    sub_len = SEQ // d
    nst = SEQ // DIL_ST
    after, after_specs = _after(token)

    def body(q_ref, k_ref, v_ref, do_ref, y_ref, l_ref, *rest):
        out_ref, dk_scr, dv_scr, s_scr, dp_scr, p_scr, ds_scr, do_scr, y_scr, l_scr, dq_scr = rest[-11:]
        t = pl.program_id(1)
        base = t * DIL_ST

        @pl.when(t == 0)
        def _():
            dk_scr[...] = jnp.zeros_like(dk_scr)
            dv_scr[...] = jnp.zeros_like(dv_scr)

        for b in range(DIL_NB):
            tok = _dil_tok(g, b, t)
            do_scr[b * BAND:(b + 1) * BAND, :] = do_ref[tok, :]
            y_scr[b * BAND:(b + 1) * BAND, :] = y_ref[tok, :]
            l_scr[b * BAND:(b + 1) * BAND, :] = l_ref[tok, :]
        for hh in range(2):
            half = _head_half((BAND, 128), hh)
            half_st = _head_half((DIL_ST, 128), hh)
            dom = jnp.where(half_st, do_scr[...], 0.0)
            delta = jnp.sum(dom * y_scr[...], axis=-1, keepdims=True)
            lcol = jnp.max(jnp.where(half_st, l_scr[...], NEG), axis=-1, keepdims=True)
            for b in range(DIL_NB):
                rows = slice(b * BAND, (b + 1) * BAND)
                qv = q_ref[0, pl.ds(pl.multiple_of(base + (b + 1) * BAND, BAND), BAND), :]
                band = pl.ds(pl.multiple_of(base + b * BAND, BAND), 2 * BAND)
                sb = _nt(jnp.where(half, qv, jnp.zeros_like(qv)), k_ref[0, band, :])
                s_scr[rows, :] = jnp.where(_band_keep(g, b, t), sb, NEG)
                dp_scr[rows, :] = _nt(dom[rows, :].astype(BF16), v_ref[0, band, :])
            pr = jnp.exp(s_scr[...] - lcol)
            p_scr[...] = pr.astype(BF16)
            ds_scr[...] = (pr * (dp_scr[...] - delta)).astype(BF16)
            for b in range(DIL_NB):
                rows = slice(b * BAND, (b + 1) * BAND)
                qv = q_ref[0, pl.ds(pl.multiple_of(base + (b + 1) * BAND, BAND), BAND), :]
                band = pl.ds(pl.multiple_of(base + b * BAND, BAND), 2 * BAND)
                dqb = jnp.where(half, _nn(ds_scr[rows, :], k_ref[0, band, :]), 0.0)
                if hh == 0:
                    dq_scr[rows, :] = dqb
                else:
                    dq_scr[rows, :] += dqb
                half2 = _head_half((2 * BAND, 128), hh)
                dk_scr[band, :] += jnp.where(half2, _tn(ds_scr[rows, :], qv), 0.0)
                dv_scr[band, :] += _tn(p_scr[rows, :], dom[rows, :].astype(BF16))
        for b in range(DIL_NB):
            out_ref[pl.ds(0, 1), _dil_tok(g, b, t), :] = dq_scr[b * BAND:(b + 1) * BAND, :][None]

        @pl.when(t == nst - 1)
        def _():
            for r in range(d):
                rows = _strided(r, sub_len, d)
                out_ref[pl.ds(1, 1), rows, :] = dk_scr[BAND + r * sub_len:BAND + (r + 1) * sub_len, :][None]
                out_ref[pl.ds(2, 1), rows, :] = dv_scr[BAND + r * sub_len:BAND + (r + 1) * sub_len, :][None]

    def inp(tq):
        return pl.BlockSpec((1, BAND + SEQ, 128), lambda pr, t: (tq, 0, pr))

    tok_spec = pl.BlockSpec((SEQ, 128), lambda pr, t: (0, pr))
    st = (DIL_ST, 2 * BAND)
    return pl.pallas_call(
        body, name=f"dil_attn_bwd_g{g}", grid=(4, nst),
        in_specs=[inp(0), inp(1), inp(2), tok_spec, tok_spec, tok_spec] + after_specs,
        out_specs=pl.BlockSpec((3, SEQ, 128), lambda pr, t: (0, 0, pr)),
        out_shape=jax.ShapeDtypeStruct((3, SEQ, 512), F32),
        scratch_shapes=[pltpu.VMEM((BAND + SEQ, 128), F32), pltpu.VMEM((BAND + SEQ, 128), F32),
                        pltpu.VMEM(st, F32), pltpu.VMEM(st, F32), pltpu.VMEM(st, BF16), pltpu.VMEM(st, BF16),
                        pltpu.VMEM((DIL_ST, 128), F32), pltpu.VMEM((DIL_ST, 128), F32), pltpu.VMEM((DIL_ST, 128), F32),
                        pltpu.VMEM((DIL_ST, 128), F32)],
        compiler_params=_cparams(),
    )(qkv, qkv, qkv, dyd, yd, lse_all, *after)


def _band_masks():
    row = lax.broadcasted_iota(jnp.int32, (BAND, BAND), 0)
    col = lax.broadcasted_iota(jnp.int32, (BAND, BAND), 1)
    return row >= col, col >= row


def _dil_attn_fwd(qkv, g):
    d = DIL_DILATIONS[g]
    nbs = SEQ // d // BAND
    nblk = SEQ // BAND

    def body(q_ref, k_ref, v_ref, o_ref, l_ref):
        keep_c, keep_p = _band_masks()
        half0 = _head_half((BAND, 128), 0)

        def step(i, carry):
            cur = pl.ds(pl.multiple_of(i * BAND, BAND), BAND)
            prv = pl.ds(pl.multiple_of(jnp.maximum(i - 1, 0) * BAND, BAND), BAND)
            has_prev = (i % nbs) != 0
            qv = q_ref[0, cur, :]
            kc, kp = k_ref[0, cur, :], k_ref[0, prv, :]
            vc, vp = v_ref[0, cur, :], v_ref[0, prv, :]
            outs, lses = [], []
            for hh in range(2):
                qm = jnp.where(_head_half((BAND, 128), hh), qv, jnp.zeros_like(qv))
                sc = jnp.where(keep_c, _nt(qm, kc), NEG)
                sp = jnp.where(keep_p & has_prev, _nt(qm, kp), NEG)
                m = jnp.maximum(jnp.max(sc, axis=-1, keepdims=True), jnp.max(sp, axis=-1, keepdims=True))
                pc, pp = jnp.exp(sc - m), jnp.exp(sp - m)
                den = jnp.sum(pc, axis=-1, keepdims=True) + jnp.sum(pp, axis=-1, keepdims=True)
                o = (_nn(pc.astype(BF16), vc) + _nn(pp.astype(BF16), vp)) / den
                outs.append(o)
                lses.append(jnp.broadcast_to(m + jnp.log(den), (BAND, 128)))
            tok = _strided((i % nbs) * BAND * d + i // nbs, BAND, d)
            o_ref[tok, :] = jnp.where(half0, outs[0], outs[1])
            l_ref[tok, :] = jnp.where(half0, lses[0], lses[1])
            return carry

        lax.fori_loop(0, nblk, step, 0, unroll=DIL_UNROLL)

    def inp(tq):
        return pl.BlockSpec((1, SEQ, 128), lambda pr: (tq, 0, pr))

    out = pl.BlockSpec((SEQ, 128), lambda pr: (0, pr))
    return pl.pallas_call(
        body, name=f"dil_attn_fwd_g{g}", grid=(4,),
        in_specs=[inp(0), inp(1), inp(2)], out_specs=[out, out],
        out_shape=[jax.ShapeDtypeStruct((SEQ, 512), F32), jax.ShapeDtypeStruct((SEQ, 512), F32)],
        compiler_params=_cparams(),
    )(qkv, qkv, qkv)


def _dil_attn_bwd(qkv, dyd, yd, lse_all, g):
    d = DIL_DILATIONS[g]
    sub_len = SEQ // d
    nbs = sub_len // BAND
    nblk = SEQ // BAND

    def body(q_ref, k_ref, v_ref, do_ref, y_ref, l_ref, out_ref, dk_scr, dv_scr):
        keep_c, keep_p = _band_masks()
        dk_scr[...] = jnp.zeros_like(dk_scr)
        dv_scr[...] = jnp.zeros_like(dv_scr)

        def step(i, carry):
            cur = pl.ds(pl.multiple_of(i * BAND, BAND), BAND)
            prv = pl.ds(pl.multiple_of(jnp.maximum(i - 1, 0) * BAND, BAND), BAND)
            has_prev = (i % nbs) != 0
            tok = _strided((i % nbs) * BAND * d + i // nbs, BAND, d)
            qv = q_ref[0, cur, :]
            kc, kp = k_ref[0, cur, :], k_ref[0, prv, :]
            vc, vp = v_ref[0, cur, :], v_ref[0, prv, :]
            dov, yv, lv = do_ref[tok, :], y_ref[tok, :], l_ref[tok, :]
            dq = jnp.zeros((BAND, 128), F32)
            dkc = jnp.zeros((BAND, 128), F32)
            dkp = jnp.zeros((BAND, 128), F32)
            dvc = jnp.zeros((BAND, 128), F32)
            dvp = jnp.zeros((BAND, 128), F32)
            for hh in range(2):
                half = _head_half((BAND, 128), hh)
                qm = jnp.where(half, qv, jnp.zeros_like(qv))
                lcol = jnp.max(jnp.where(half, lv, NEG), axis=-1, keepdims=True)
                pc = jnp.exp(jnp.where(keep_c, _nt(qm, kc), NEG) - lcol)
                pp = jnp.exp(jnp.where(keep_p & has_prev, _nt(qm, kp), NEG) - lcol)
                dom = jnp.where(half, dov, 0.0)
                domb = dom.astype(BF16)
                delta = jnp.sum(dom * yv, axis=-1, keepdims=True)
                dsc = (pc * (_nt(domb, vc) - delta)).astype(BF16)
                dsp = (pp * (_nt(domb, vp) - delta)).astype(BF16)
                dvc = dvc + _tn(pc.astype(BF16), domb)
                dvp = dvp + _tn(pp.astype(BF16), domb)
                dq = dq + jnp.where(half, _nn(dsc, kc) + _nn(dsp, kp), 0.0)
                dkc = dkc + jnp.where(half, _tn(dsc, qv), 0.0)
                dkp = dkp + jnp.where(half, _tn(dsp, qv), 0.0)
            out_ref[pl.ds(0, 1), tok, :] = dq[None]
            dk_scr[cur, :] += dkc
            dk_scr[prv, :] += dkp
            dv_scr[cur, :] += dvc
            dv_scr[prv, :] += dvp
            return carry

        lax.fori_loop(0, nblk, step, 0, unroll=DIL_UNROLL)
        for r in range(d):
            rows = _strided(r, sub_len, d)
            out_ref[pl.ds(1, 1), rows, :] = dk_scr[r * sub_len:(r + 1) * sub_len, :][None]
            out_ref[pl.ds(2, 1), rows, :] = dv_scr[r * sub_len:(r + 1) * sub_len, :][None]

    def inp(tq):
        return pl.BlockSpec((1, SEQ, 128), lambda pr: (tq, 0, pr))

    tok_spec = pl.BlockSpec((SEQ, 128), lambda pr: (0, pr))
    return pl.pallas_call(
        body, name=f"dil_attn_bwd_g{g}", grid=(4,),
        in_specs=[inp(0), inp(1), inp(2), tok_spec, tok_spec, tok_spec],
        out_specs=pl.BlockSpec((3, SEQ, 128), lambda pr: (0, 0, pr)),
        out_shape=jax.ShapeDtypeStruct((3, SEQ, 512), F32),
        scratch_shapes=[pltpu.VMEM((SEQ, 128), F32), pltpu.VMEM((SEQ, 128), F32)],
        compiler_params=_cparams(),
    )(qkv, qkv, qkv, dyd, yd, lse_all)


def _dil_prep_bwd(dp_in, dqkv, rc, rs, g):
    tm = 1024

    def body(dp_any, g_ref, c_ref, s_ref, dp_ref):
        del dp_any
        tq = pl.program_id(0)

        @pl.when(tq == 2)
        def _():
            dp_ref[...] = g_ref[0].astype(BF16)

        @pl.when(tq < 2)
        def _():
            mult = jnp.where(tq == 0, DIL_SCALE, 1.0).astype(F32)
            lanes = _rope_lanes((tm, 128), DIL_ROPE_HALF, 64, 0)
            cv, sv = c_ref[...], s_ref[...] * mult
            cv = cv * mult
            for pr in range(4):
                gv = g_ref[0, :, pr * 128:(pr + 1) * 128]
                dp_ref[:, pr * 128:(pr + 1) * 128] = _rope_bwd(gv, cv, sv, DIL_ROPE_HALF, lanes).astype(BF16)

    tab = pl.BlockSpec((tm, 128), lambda tq, i: (i, 0))
    return pl.pallas_call(
        body, name=f"dil_prep_bwd_g{g}", grid=(3, SEQ // tm),
        in_specs=[pl.BlockSpec(memory_space=pl.ANY),
                  pl.BlockSpec((1, tm, 512), lambda tq, i: (tq, i, 0)), tab, tab],
        out_specs=pl.BlockSpec((tm, 512), lambda tq, i: (i, COL_QKV // 512 + tq * 3 + g)),
        out_shape=jax.ShapeDtypeStruct((SEQ, N_PAD), BF16),
        input_output_aliases={0: 0},
    )(dp_in, dqkv, rc, rs)


TAIL_T = 256


def _tail(p, ya, o_g, l_g, x, target, wpm, wpd, wout, post_g):
    tm = TAIL_T

    def body(pgz_ref, ya_ref, o0_ref, o1_ref, o2_ref, l0_ref, l1_ref, l2_ref, x_ref, t_ref,
             wpm_ref, wpd_ref, wout_ref, pg_ref,
             dp_ref, dy_ref, mg_ref, dt_ref, ua_ref, dpa_ref, ud_ref, dpd_ref, dya_ref, dyd_ref,
             yd_ref, lse_ref, loss_ref, dgp_ref):
        l0, l1, l2 = l0_ref[...], l1_ref[...], l2_ref[...]
        mx = jnp.maximum(jnp.maximum(l0, l1), l2)
        e0, e1, e2 = jnp.exp(l0 - mx), jnp.exp(l1 - mx), jnp.exp(l2 - mx)
        den = e0 + e1 + e2
        yd = (e0 * o0_ref[...] + e1 * o1_ref[...] + e2 * o2_ref[...]) / den
        yd_ref[...] = yd
        lse_ref[...] = mx + jnp.log(den)
        ya = ya_ref[...]

        gm, gd = pgz_ref[:, 0:1024], pgz_ref[:, 1024:2048]
        zm, zd = pgz_ref[:, 2048:2560], pgz_ref[:, 2560:3072]
        szm, szd = _sigmoid(zm), _sigmoid(zd)
        sm, sd = zm * szm, zd * szd
        ua = (ya * sm).astype(BF16)
        ud = (yd * sd).astype(BF16)
        ua_ref[...] = ua
        ud_ref[...] = ud
        pa = _nn(ua, wpm_ref[...])
        pd = _nn(ud, wpd_ref[...])
        sgm, sgd = _sigmoid(gm), _sigmoid(gd)
        mg = (sgm * pa + sgd * pd).astype(BF16)
        mg_ref[...] = mg
        t = _nn(mg, wout_ref[...])
        r3 = lax.rsqrt(jnp.mean(t * t, axis=-1, keepdims=True) + EPS)
        n = t * r3
        pg = pg_ref[...]
        err = x_ref[...] + n * pg - t_ref[...]
        lpart = jnp.sum(err * err, axis=0, keepdims=True)

        dy = err * (1.0 / D_MODEL)
        dy_ref[...] = dy
        gpart = jnp.sum(dy * n, axis=0, keepdims=True)
        dn = dy * pg
        dt = (r3 * (dn - n * jnp.mean(dn * n, axis=-1, keepdims=True))).astype(BF16)
        dt_ref[...] = dt
        dmg = _nt(dt, wout_ref[...])
        dpa = (dmg * sgm).astype(BF16)
        dpd = (dmg * sgd).astype(BF16)
        dpa_ref[...] = dpa
        dpd_ref[...] = dpd
        dp_ref[:, 0:1024] = (dmg * pa * sgm * (1.0 - sgm)).astype(BF16)
        dp_ref[:, 1024:2048] = (dmg * pd * sgd * (1.0 - sgd)).astype(BF16)
        dua = _nt(dpa, wpm_ref[...])
        dud = _nt(dpd, wpd_ref[...])
        dya_ref[...] = dua * sm
        dyd_ref[...] = dud * sd
        dp_ref[:, 2048:2560] = (dua * ya * szm * (1.0 + zm * (1.0 - szm))).astype(BF16)
        dp_ref[:, 2560:3072] = (dud * yd * szd * (1.0 + zd * (1.0 - szd))).astype(BF16)

        @pl.when(pl.program_id(0) == 0)
        def _():
            loss_ref[...] = lpart
            dgp_ref[...] = gpart

        @pl.when(pl.program_id(0) > 0)
        def _():
            loss_ref[...] += lpart
            dgp_ref[...] += gpart

    def rows(w):
        return pl.BlockSpec((tm, w), lambda i: (i, 0))

    def full(shape):
        return pl.BlockSpec(shape, lambda i: (0, 0))

    def sds(w, dt):
        return jax.ShapeDtypeStruct((SEQ, w), dt)

    return pl.pallas_call(
        body, name="tail", grid=(SEQ // tm,),
        in_specs=[rows(3072), rows(512), rows(512), rows(512), rows(512), rows(512), rows(512), rows(512),
                  rows(1024), rows(1024), full((512, 1024)), full((512, 1024)), full((1024, 1024)), full((1, 1024))],
        out_specs=[rows(3072), rows(1024), rows(1024), rows(1024), rows(512), rows(1024), rows(512), rows(1024),
                   rows(512), rows(512), rows(512), rows(512), full((1, 1024)), full((1, 1024))],
        out_shape=[sds(N_PAD, BF16), sds(1024, F32), sds(1024, BF16), sds(1024, BF16), sds(512, BF16),
                   sds(1024, BF16), sds(512, BF16), sds(1024, BF16), sds(512, F32), sds(512, F32),
                   sds(512, F32), sds(512, F32),
                   jax.ShapeDtypeStruct((1, 1024), F32), jax.ShapeDtypeStruct((1, 1024), F32)],
        compiler_params=_cparams(),
    )(p, ya, o_g[0], o_g[1], o_g[2], l_g[0], l_g[1], l_g[2], x, target, wpm, wpd, wout, post_g)


def _sum_parts(recv, own, me, tr, name):
    n, r, w = recv.shape
    own_spec = (pl.BlockSpec((tr, w), lambda i, me_ref: (i, 0)) if own.ndim == 2
                else pl.BlockSpec((None, tr, w), lambda i, me_ref: (me_ref[0], i, 0)))

    def body(me_ref, p_ref, own_ref, o_ref):
        mine = own_ref[...].astype(F32)
        acc = jnp.zeros((tr, w), F32)
        for s in range(n):
            acc = acc + jnp.where(me_ref[0] == s, mine, p_ref[s].astype(F32))
        o_ref[...] = acc

    return pl.pallas_call(
        body, name=name,
        grid_spec=pltpu.PrefetchScalarGridSpec(
            num_scalar_prefetch=1, grid=(r // tr,),
            in_specs=[pl.BlockSpec((n, tr, w), lambda i, me_ref: (0, i, 0)), own_spec],
            out_specs=pl.BlockSpec((tr, w), lambda i, me_ref: (i, 0))),
        out_shape=jax.ShapeDtypeStruct((r, w), F32),
    )(me.reshape(1), recv, own)


def _adamw(w, g, m, v, name):
    lead = w.shape[:-2]
    r, c = w.shape[-2:]
    tr = max([t for t in range(8, 257, 8) if r % t == 0], default=r)
    c1 = 1.0 - ADAM_B1 ** ADAM_STEP
    c2 = 1.0 - ADAM_B2 ** ADAM_STEP

    def body(w_ref, g_ref, m_ref, v_ref, d_ref, nm_ref, nv_ref):
        gv = g_ref[...]
        nm = ADAM_B1 * m_ref[...] + (1.0 - ADAM_B1) * gv
        nv = ADAM_B2 * v_ref[...] + (1.0 - ADAM_B2) * (gv * gv)
        nm_ref[...] = nm
        nv_ref[...] = nv
        d_ref[...] = -ADAM_LR * ((nm / c1) / (jnp.sqrt(nv / c2) + ADAM_EPS) + ADAM_WD * w_ref[...])

    zeros = (0,) * len(lead)
    spec = pl.BlockSpec((1,) * len(lead) + (tr, c), lambda i: zeros + (i, 0))
    sd = jax.ShapeDtypeStruct(w.shape, F32)
    return pl.pallas_call(
        body, name=name, grid=(r // tr,),
        in_specs=[spec] * 4, out_specs=[spec] * 3, out_shape=[sd] * 3,
    )(w, g, m, v)


ANY = pl.BlockSpec(memory_space=pl.ANY)


def _my_place():
    return lax.axis_index("x"), lax.axis_index("y"), lax.axis_index("c")


def _allgather_weights(mats):
    def body(*refs):
        w_refs, out_refs = refs[:N_MATS], refs[N_MATS:2 * N_MATS]
        send_sems, recv_sems = refs[2 * N_MATS:]
        x, y, c = _my_place()
        sibling = (x, y, 1 - c)
        chips = [(1 - x, y), (x, 1 - y), (1 - x, 1 - y)]

        def copy(k, src, dst, to):
            return pltpu.make_async_remote_copy(src_ref=src, dst_ref=dst, send_sem=send_sems.at[k],
                                                recv_sem=recv_sems.at[k], device_id=to, device_id_type=MESH)

        def half(mi, shard, hc):
            hr = SHARD_SHAPES[mi][0] // 2
            return out_refs[mi].at[shard, pl.ds(pl.multiple_of(hc * hr, 16), hr), :]

        started = []
        for mi in range(N_MATS):
            hr = SHARD_SHAPES[mi][0] // 2
            my_half = w_refs[mi].at[pl.ds(pl.multiple_of(c * hr, 16), hr), :]
            for j, (cx, cy) in enumerate(chips):
                cp = copy(mi * 6 + j, my_half, half(mi, 2 * x + y, c), (cx, cy, c))
                cp.start()
                started.append(cp)
        for mi in range(N_MATS):
            for j, (cx, cy) in enumerate(chips):
                landed = half(mi, 2 * cx + cy, c)
                copy(mi * 6 + j, landed, landed, (cx, cy, c)).wait_recv()
                fw = copy(mi * 6 + 3 + j, landed, landed, sibling)
                fw.start()
                started.append(fw)
        for mi in range(N_MATS):
            for j, (cx, cy) in enumerate(chips):
                other = half(mi, 2 * cx + cy, 1 - c)
                copy(mi * 6 + 3 + j, other, other, sibling).wait_recv()
        for cp in started:
            cp.wait_send()

    return pl.pallas_call(
        body, name="allgather_weights",
        in_specs=[ANY] * N_MATS, out_specs=[ANY] * N_MATS,
        out_shape=[jax.ShapeDtypeStruct((4, r, c), BF16) for r, c in SHARD_SHAPES],
        scratch_shapes=[pltpu.SemaphoreType.DMA((6 * N_MATS,)), pltpu.SemaphoreType.DMA((6 * N_MATS,))],
    )(*mats)


HBM = pl.BlockSpec(memory_space=pltpu.HBM)
SEM = pl.BlockSpec(memory_space=pltpu.SEMAPHORE)
DATAFLOW = pltpu.SideEffectType.DATAFLOW_SIDE_EFFECTING


def _peers(x, y, c):
    out = []
    for k in range(1, 8):
        px, py, pc = x ^ (k >> 2), y ^ ((k >> 1) & 1), c ^ (k & 1)
        out.append((k - 1, (px, py, pc), 4 * px + 2 * py + pc))
    return out


def _exchange_start(parts, name):
    n = len(parts)

    def body(*refs):
        p_refs, land_refs = refs[:n], refs[n:2 * n]
        send_sems, recv_sems, token = refs[2 * n], refs[2 * n + 1], refs[-1]
        x, y, c = _my_place()
        me = 4 * x + 2 * y + c
        for k, dev, peer in _peers(x, y, c):
            for mi in range(n):
                pltpu.make_async_remote_copy(
                    src_ref=p_refs[mi].at[peer], dst_ref=land_refs[mi].at[me], send_sem=send_sems.at[k * n + mi],
                    recv_sem=recv_sems.at[k * n + mi], device_id=dev, device_id_type=MESH).start()
        token[...] = jnp.zeros_like(token)

    hbm = [pltpu.HBM(p.shape, p.dtype) for p in parts]
    outs = pl.pallas_call(
        body, name=name + "_start",
        out_shape=(pltpu.SemaphoreType.DMA((7 * n,)), pltpu.SemaphoreType.DMA((7 * n,)), *hbm, *hbm,
                   jax.ShapeDtypeStruct((8, 128), F32)),
        in_specs=[HBM] * (2 * n), out_specs=(SEM, SEM, *[HBM] * (2 * n), pl.BlockSpec(memory_space=pltpu.VMEM)),
        input_output_aliases={i: 2 + i for i in range(2 * n)},
        compiler_params=pltpu.CompilerParams(has_side_effects=DATAFLOW),
    )(*[pltpu.with_memory_space_constraint(p, pltpu.HBM) for p in parts],
      *[pltpu.with_memory_space_constraint(lax.empty(p.shape, p.dtype), pltpu.HBM) for p in parts])
    return (name, outs[:-1]), outs[-1]


def _exchange_wait(handle, after):
    name, outs = handle
    n = (len(outs) - 2) // 2

    def body(*refs):
        p_refs, land_refs = refs[:n], refs[n:2 * n]
        send_sems, recv_sems = refs[2 * n], refs[2 * n + 1]
        x, y, c = _my_place()
        me = 4 * x + 2 * y + c
        for k, dev, peer in _peers(x, y, c):
            for mi in range(n):
                pltpu.make_async_remote_copy(
                    src_ref=p_refs[mi].at[peer], dst_ref=land_refs[mi].at[me], send_sem=send_sems.at[k * n + mi],
                    recv_sem=recv_sems.at[k * n + mi], device_id=dev, device_id_type=MESH).wait_send()
                slot = land_refs[mi].at[peer]
                pltpu.make_async_remote_copy(
                    src_ref=slot, dst_ref=slot, send_sem=send_sems.at[k * n + mi],
                    recv_sem=recv_sems.at[k * n + mi], device_id=dev, device_id_type=MESH).wait_recv()

    bufs = outs[2:]
    res = pl.pallas_call(
        body, name=name + "_wait", out_shape=tuple(pltpu.HBM(b.shape, b.dtype) for b in bufs),
        in_specs=[HBM] * (2 * n) + [SEM, SEM, ANY], out_specs=tuple([HBM] * (2 * n)),
        input_output_aliases={i: i for i in range(2 * n)},
        compiler_params=pltpu.CompilerParams(has_side_effects=DATAFLOW),
    )(*bufs, outs[0], outs[1], after)
    return list(res[n:])


def _swap_halves(halves, gvec):
    def body(*refs):
        g_refs, gv_ref = refs[:N_MATS], refs[N_MATS]
        out_refs, rg_ref = refs[N_MATS + 1:2 * N_MATS + 1], refs[2 * N_MATS + 1]
        send_sems, recv_sems = refs[2 * N_MATS + 2:]
        x, y, c = _my_place()
        me = 4 * x + 2 * y + c
        sends = []
        for mi in range(N_MATS):
            cp = pltpu.make_async_remote_copy(src_ref=g_refs[mi], dst_ref=out_refs[mi].at[c], send_sem=send_sems.at[mi],
                                              recv_sem=recv_sems.at[mi], device_id=(x, y, 1 - c), device_id_type=MESH)
            cp.start()
            sends.append(cp)
        for k, dev, peer in _peers(x, y, c):
            cp = pltpu.make_async_remote_copy(src_ref=gv_ref, dst_ref=rg_ref.at[me], send_sem=send_sems.at[N_MATS + k],
                                              recv_sem=recv_sems.at[N_MATS + k], device_id=dev, device_id_type=MESH)
            cp.start()
            sends.append(cp)
        for mi in range(N_MATS):
            got = out_refs[mi].at[1 - c]
            pltpu.make_async_remote_copy(src_ref=got, dst_ref=got, send_sem=send_sems.at[mi], recv_sem=recv_sems.at[mi],
                                         device_id=(x, y, 1 - c), device_id_type=MESH).wait_recv()
        for k, dev, peer in _peers(x, y, c):
            got = rg_ref.at[peer]
            pltpu.make_async_remote_copy(src_ref=got, dst_ref=got, send_sem=send_sems.at[N_MATS + k],
                                         recv_sem=recv_sems.at[N_MATS + k], device_id=dev, device_id_type=MESH).wait_recv()
        for cp in sends:
            cp.wait_send()

    outs = pl.pallas_call(
        body, name="swap_halves",
        in_specs=[ANY] * (N_MATS + 1), out_specs=[ANY] * (N_MATS + 1),
        out_shape=[jax.ShapeDtypeStruct((2, r // 2, c), F32) for r, c in SHARD_SHAPES]
        + [jax.ShapeDtypeStruct((8, 8, N_GAINS), F32)],
        scratch_shapes=[pltpu.SemaphoreType.DMA((N_MATS + 7,)), pltpu.SemaphoreType.DMA((N_MATS + 7,))],
    )(*halves, gvec)
    return outs[:N_MATS], outs[N_MATS]


def _set_slot(arr, block, idx):
    return lax.dynamic_update_slice(arr, block[None], (idx,) + (0,) * block.ndim)


PAD_RUNS = ((6304, 8352, 0), (5280, 6304, COL_Z), (672, 5280, COL_QKV), (0, 640, COL_LAT), (640, 672, COL_LAT + 704))
W_IN_SHARD = 2088


def _full_weights(gathered):
    def cols(a):
        return jnp.concatenate([a[s] for s in range(4)], axis=1)

    w_uq, w_ukv, w_pm, w_pd = [cols(a) for a in gathered[1:5]]
    w_out = gathered[5].reshape(D_MODEL, D_MODEL)
    g_in = gathered[0]
    pieces, at = [], 0
    for lo, hi, pad_lo in sorted(PAD_RUNS, key=lambda t: t[2]):
        if pad_lo > at:
            pieces.append(jnp.zeros((D_MODEL, pad_lo - at), g_in.dtype))
        for s in range(4):
            a_, b_ = max(lo, s * W_IN_SHARD), min(hi, (s + 1) * W_IN_SHARD)
            if a_ < b_:
                pieces.append(g_in[s][:, a_ - s * W_IN_SHARD:b_ - s * W_IN_SHARD])
        at = pad_lo + hi - lo
    pieces.append(jnp.zeros((D_MODEL, N_PAD - at), g_in.dtype))
    w_pad = jnp.concatenate(pieces, axis=1)
    z32 = jnp.zeros((Q_RANK, 32), w_uq.dtype)
    wuq_pad = jnp.concatenate([t for h in range(MLA_HEADS) for t in (w_uq[:, h * 96:(h + 1) * 96], z32)], axis=1)
    z64 = jnp.zeros((KV_RANK, 64), w_ukv.dtype)
    wk_pad = jnp.concatenate([t for h in range(MLA_HEADS) for t in (w_ukv[:, h * 128:h * 128 + 64], z64)], axis=1)
    wv = jnp.concatenate([w_ukv[:, h * 128 + 64:(h + 1) * 128] for h in range(MLA_HEADS)], axis=1)
    return w_pad, wuq_pad, wk_pad, wv, w_pm, w_pd, w_out


W_IN_LAT = 672


def _grad_parts_in_early(dw_early):
    def in_block(s, h):
        rows = slice(h * 512, (h + 1) * 512)
        out = []
        for lo, hi, pad_lo in sorted(PAD_RUNS):
            a_, b_ = max(lo, s * W_IN_SHARD), min(hi, (s + 1) * W_IN_SHARD)
            if a_ < b_:
                out.append(jnp.zeros((512, b_ - a_), dw_early.dtype) if pad_lo >= COL_LAT
                           else dw_early[rows, pad_lo + a_ - lo:pad_lo + b_ - lo])
        return jnp.concatenate(out, axis=1)

    return jnp.stack([in_block(s, h) for s in range(4) for h in range(2)])


def _grad_parts_in_late(dw_late):
    cols = jnp.concatenate([dw_late[:, 0:640], dw_late[:, 704:736]], axis=1)
    zero = jnp.zeros((512, W_IN_LAT), dw_late.dtype)
    return jnp.stack([cols[0:512], cols[512:1024]] + [zero] * 6)


def _col_blocks(m):
    r, c = m.shape[0] // 2, m.shape[1] // 4
    return jnp.stack([m[h * r:(h + 1) * r, s * c:(s + 1) * c] for s in range(4) for h in range(2)])


def _grad_parts_mla(dwuq_pad, dwk_pad, dwv):
    d_uq = jnp.concatenate([dwuq_pad[:, h * 128:h * 128 + 96] for h in range(MLA_HEADS)], axis=1)
    d_ukv = jnp.concatenate([t for h in range(MLA_HEADS) for t in (dwk_pad[:, h * 128:h * 128 + 64], dwv[:, h * 64:(h + 1) * 64])],
                            axis=1)
    return [_col_blocks(d_uq), _col_blocks(d_ukv)]


def _rope_tables(positions):
    pos = positions.reshape(SEQ).astype(F32)
    lane = jnp.arange(128)

    def table(rot, first, period):
        inv = ROPE_THETA ** (-jnp.arange(0, rot, 2, dtype=F32) / rot)
        half = rot // 2
        off = lane % period - first
        in1, in2 = (off >= 0) & (off < half), (off >= half) & (off < rot)
        inv_lane = jnp.where(in1 | in2, inv[jnp.clip(off % half, 0, half - 1)], 0.0)
        sign = jnp.where(in1, -1.0, 1.0).astype(F32)
        ang = pos[:, None] * inv_lane[None, :]
        return jnp.cos(ang), jnp.sin(ang) * sign[None, :]

    return table(32, 64, 128), table(16, 0, 64)


def _device_grads(x, positions, target, gains, gathered, send):
    pre_g, q_g, kv_g, post_g = gains
    w_pad, wuq_pad, wk_pad, wv, w_pm, w_pd, w_out = _full_weights(gathered)
    (mc, ms), (dc, ds) = _rope_tables(positions)

    h, h_t = _prenorm_fwd(x, pre_g)
    p = _matmul(h, w_pad, "nn", F32, 1024, 1408, 1024, "in_proj")
    cqn, ckvn, q, k, v = _mla_prep_fwd(p, q_g, kv_g, wuq_pad, wk_pad, wv, mc, ms)
    ya, lse_m = _mla_flash_fwd(q, k, v)
    qkv = [_dil_prep_fwd(p, dc, ds, g) for g in range(3)]
    o_g, l_g = zip(*[_dil_attn_fwd2(qkv[g], g) for g in range(3)])
    (dp, dy, mg, dt, ua, dpa, ud, dpd, dya, dyd, yd, lse_d, loss_cols, dg_post) = _tail(
        p, ya, o_g, l_g, x, target, w_pm, w_pd, w_out, post_g)

    for g in range(3):
        dqkv = _dil_attn_bwd2(qkv[g], dyd, yd, lse_d, g)
        dp = _dil_prep_bwd(dp, dqkv, dc, ds, g)
    dw_early = _matmul(h_t, dp, "nn", BF16, 1024, 1536, 2048, "dw_in_early", b_cols=(0, COL_LAT // 1536))
    dwpm = _matmul(ua, dpa, "tn", BF16, 512, 1024, 512, "dw_proj_mla")
    dwpd = _matmul(ud, dpd, "tn", BF16, 512, 1024, 512, "dw_proj_dil")
    dwout = _matmul(mg, dt, "tn", BF16, 1024, 1024, 512, "dw_out")
    early, token = send([_grad_parts_in_early(dw_early), _col_blocks(dwpm), _col_blocks(dwpd),
                         dwout.reshape(8, 128, D_MODEL)], "exchange_early")

    dq, dk, dv = _mla_flash_bwd(q, k, v, ya, dya, lse_m, token)
    dp, dqb, dkb, dvb, dg_q, dg_kv = _mla_prep_bwd(dp, p, dq, dk, dv, q_g, kv_g, wuq_pad, wk_pad, wv, mc, ms)
    dwuq_pad = _matmul(cqn, dqb, "tn", BF16, Q_RANK, 1024, 512, "dw_uq")
    dwk_pad = _matmul(ckvn, dkb, "tn", BF16, KV_RANK, 1024, 512, "dw_k")
    dwv = _matmul(ckvn, dvb, "tn", BF16, KV_RANK, 512, 512, "dw_v")
    dw_late = _matmul(h_t, dp, "nn", BF16, 1024, N_LAT, 2048, "dw_in_late", b_cols=(COL_LAT // N_LAT, 1))
    late, token = send([_grad_parts_in_late(dw_late)] + _grad_parts_mla(dwuq_pad, dwk_pad, dwv), "exchange_late")

    dh = _matmul(dp, w_pad, "nt", F32, 1024, 1024, 1408, "dh", token)
    grad_x, dg_pre = _prenorm_bwd(x, dh, dy, pre_g)

    gvec = jnp.concatenate([dg_pre, dg_q, dg_kv, dg_post], axis=1)
    return loss_cols, grad_x, (early, late), gvec


def kernel(x, positions, pre_norm_g, w_in, q_norm_g, w_uq, kv_norm_g, w_ukv, w_proj_mla, w_proj_dil, w_out, post_norm_g, loss_target, m_pre_norm_g, m_w_in, m_q_norm_g, m_w_uq, m_kv_norm_g, m_w_ukv, m_w_proj_mla, m_w_proj_dil, m_w_out, m_post_norm_g, v_pre_norm_g, v_w_in, v_q_norm_g, v_w_uq, v_kv_norm_g, v_w_ukv, v_w_proj_mla, v_w_proj_dil, v_w_out, v_post_norm_g):
    xi, yi, ci = _my_place()
    chip, me = 2 * xi + yi, 4 * xi + 2 * yi + ci
    mats = [w.reshape(w.shape[1:]).astype(BF16) for w in (w_in, w_uq, w_ukv, w_proj_mla, w_proj_dil, w_out)]
    gathered = [_set_slot(g, m, chip) for g, m in zip(_allgather_weights(mats), mats)]
    gains = (pre_norm_g, q_norm_g, kv_norm_g, post_norm_g)
    sent = {}

    def send(blocks, name):
        sent[name] = blocks
        return _exchange_start(blocks, name)

    loss_cols, grad_x, (early, late), gvec = _device_grads(x[0], positions, loss_target[0], gains, gathered, send)

    loss = lax.psum(jnp.sum(loss_cols) * (0.5 / D_MODEL), ("x", "y", "c"))

    recv_e, recv_l = _exchange_wait(early, grad_x), _exchange_wait(late, grad_x)
    own_e, own_l = sent["exchange_early"], sent["exchange_late"]
    in_e = _sum_parts(recv_e[0], own_e[0], me, 64, "sum_grad_in_early")
    in_l = _sum_parts(recv_l[0], own_l[0], me, 64, "sum_grad_in_late")
    half_in = jnp.concatenate([in_e[:, :W_IN_LAT] + jnp.where(chip == 0, in_l, 0.0), in_e[:, W_IN_LAT:]], axis=1)
    halves = [half_in,
              _sum_parts(recv_l[1], own_l[1], me, 64, "sum_grad_uq"), _sum_parts(recv_l[2], own_l[2], me, 64, "sum_grad_ukv"),
              _sum_parts(recv_e[1], own_e[1], me, 64, "sum_grad_pm"), _sum_parts(recv_e[2], own_e[2], me, 64, "sum_grad_pd"),
              _sum_parts(recv_e[3], own_e[3], me, 64, "sum_grad_out")]
    gvec8 = jnp.pad(gvec, ((0, 7), (0, 0)))
    swapped, recv_gains = _swap_halves(halves, gvec8)
    g_gains = _sum_parts(recv_gains, gvec8, me, 8, "sum_gain_parts")[0:1]
    g_mats = [_set_slot(s, hf, ci).reshape((1,) + shp) for s, hf, shp in zip(swapped, halves, SHARD_SHAPES)]

    off = [0, 1024, 1408, 1664, 2688]
    g_gain = [g_gains[:, off[i]:off[i + 1]] for i in range(4)]
    grads = [g_gain[0], g_mats[0], g_gain[1], g_mats[1], g_gain[2], g_mats[2], g_mats[3], g_mats[4], g_mats[5], g_gain[3]]
    ws = [pre_norm_g, w_in, q_norm_g, w_uq, kv_norm_g, w_ukv, w_proj_mla, w_proj_dil, w_out, post_norm_g]
    ms = [m_pre_norm_g, m_w_in, m_q_norm_g, m_w_uq, m_kv_norm_g, m_w_ukv, m_w_proj_mla, m_w_proj_dil, m_w_out, m_post_norm_g]
    vs = [v_pre_norm_g, v_w_in, v_q_norm_g, v_w_uq, v_kv_norm_g, v_w_ukv, v_w_proj_mla, v_w_proj_dil, v_w_out, v_post_norm_g]
    deltas, new_m, new_v = [], [], []
    for i, (w, g, m, v) in enumerate(zip(ws, grads, ms, vs)):
        if w.shape[-1] % 128 and w.shape[-2] % 128 == 0:
            g = jnp.swapaxes(g, 1, 2)
            grads[i] = jnp.swapaxes(g, 1, 2)
            d_, m_, v_ = [jnp.swapaxes(o, 1, 2) for o in
                          _adamw(jnp.swapaxes(w, 1, 2), g, jnp.swapaxes(m, 1, 2), jnp.swapaxes(v, 1, 2), f"adamw_{i}")]
        else:
            d_, m_, v_ = _adamw(w, g, m, v, f"adamw_{i}")
        deltas.append(d_)
        new_m.append(m_)
        new_v.append(v_)
    return (loss, grad_x.reshape(x.shape), *grads, *deltas, *new_m, *new_v)
```

```python
import jax
import jax.numpy as jnp
from jax import lax
from jax.experimental import pallas as pl
from jax.experimental.pallas import tpu as pltpu

F32 = jnp.float32
BF16 = jnp.bfloat16

SEQ = 4096
D_MODEL = 1024
EPS = 1e-6
ROPE_THETA = 500000.0
MLA_HEADS = 8
Q_RANK = 384
KV_RANK = 256
MLA_SCALE = 96.0 ** -0.5
MLA_ROPE_HALF = 16
DIL_DILATIONS = (1, 4, 16)
DIL_ROPE_HALF = 8
DIL_SCALE = 0.125
BAND = 128

N_LAT = 768
COL_Z, COL_QKV, COL_LAT = 2048, 3072, 7680
N_PAD = 8448
IN_SPLITS = (384, 256, 32, 4608, 512, 512, 1024, 1024)

SHARD_SHAPES = ((1024, 2088), (384, 192), (256, 256), (512, 256), (512, 256), (256, 1024))
N_MATS = len(SHARD_SHAPES)
N_GAINS = 2688

ADAM_LR, ADAM_B1, ADAM_B2, ADAM_EPS, ADAM_WD, ADAM_STEP = 0.001, 0.9, 0.999, 1e-08, 0.01, 10

VMEM_LIMIT = 56 * 1024 * 1024
NEG = -1e30
MESH = pl.DeviceIdType.MESH


def _cparams(**kw):
    return pltpu.CompilerParams(vmem_limit_bytes=VMEM_LIMIT, **kw)


def _dot(a, b, dims):
    return lax.dot_general(a, b, (dims, ((), ())), preferred_element_type=F32)


def _nn(a, b):
    return _dot(a, b, ((1,), (0,)))


def _nt(a, b):
    return _dot(a, b, ((1,), (1,)))


def _tn(a, b):
    return _dot(a, b, ((0,), (0,)))


def _rope_lanes(shape, half, period, first):
    lane = lax.broadcasted_iota(jnp.int32, shape, len(shape) - 1) % period
    return (lane >= first) & (lane < first + half), (lane >= first + half) & (lane < first + 2 * half)


def _rope_fwd(x, c, s, half, lanes):
    x1, _ = lanes
    return x * c + jnp.where(x1, pltpu.roll(x, 128 - half, 1), pltpu.roll(x, half, 1)) * s


def _rope_bwd(g, c, s, half, lanes):
    x1, x2 = lanes
    gs = g * s
    return g * c + jnp.where(x2, pltpu.roll(gs, half, 1), jnp.where(x1, pltpu.roll(gs, 128 - half, 1), 0.0))


def _sigmoid(x):
    return 1.0 / (1.0 + jnp.exp(-x))


def _after(token):
    return ([], []) if token is None else ([token], [pl.BlockSpec(memory_space=pl.ANY)])


def _matmul(a, b, mode, out_dtype, tm, tn, tk, name, token=None, b_cols=None):
    after, after_specs = _after(token)
    if mode == "nn":
        (m, k), n = a.shape, b.shape[1]
        first = 0
        if b_cols is not None:
            first, n = b_cols[0], b_cols[1] * tn
        a_spec = pl.BlockSpec((tm, tk), lambda j, i, kk: (i, kk))
        b_spec = pl.BlockSpec((tk, tn), lambda j, i, kk: (kk, j + first))
        dot = _nn
    elif mode == "nt":
        (m, k), n = a.shape, b.shape[0]
        a_spec = pl.BlockSpec((tm, tk), lambda j, i, kk: (i, kk))
        b_spec = pl.BlockSpec((tn, tk), lambda j, i, kk: (j, kk))
        dot = _nt
    else:
        (k, m), n = a.shape, b.shape[1]
        a_spec = pl.BlockSpec((tk, tm), lambda j, i, kk: (kk, i))
        b_spec = pl.BlockSpec((tk, tn), lambda j, i, kk: (kk, j))
        dot = _tn
    assert m % tm == 0 and n % tn == 0 and k % tk == 0, (name, m, n, k, tm, tn, tk)
    nk = k // tk

    def body(a_ref, b_ref, *rest):
        o_ref, acc_ref = rest[-2:]
        kk = pl.program_id(2)
        part = dot(a_ref[...], b_ref[...])

        @pl.when(kk == 0)
        def _():
            acc_ref[...] = part

        @pl.when(kk > 0)
        def _():
            acc_ref[...] += part

        @pl.when(kk == nk - 1)
        def _():
            o_ref[...] = acc_ref[...].astype(o_ref.dtype)

    return pl.pallas_call(
        body, name=name, grid=(n // tn, m // tm, nk),
        in_specs=[a_spec, b_spec] + after_specs,
        out_specs=pl.BlockSpec((tm, tn), lambda j, i, kk: (i, j)),
        out_shape=jax.ShapeDtypeStruct((m, n), out_dtype),
        scratch_shapes=[pltpu.VMEM((tm, tn), F32)],
        compiler_params=_cparams(),
    )(a, b, *after)


def _prenorm_fwd(x, g):
    tm = 512

    def body(x_ref, g_ref, h_ref, ht_ref):
        xv = x_ref[...]
        r = lax.rsqrt(jnp.mean(xv * xv, axis=-1, keepdims=True) + EPS)
        hv = (xv * r * g_ref[...]).astype(BF16)
        h_ref[...] = hv
        ht_ref[...] = hv.T

    return pl.pallas_call(
        body, name="prenorm_fwd", grid=(SEQ // tm,),
        in_specs=[pl.BlockSpec((tm, D_MODEL), lambda i: (i, 0)), pl.BlockSpec((1, D_MODEL), lambda i: (0, 0))],
        out_specs=[pl.BlockSpec((tm, D_MODEL), lambda i: (i, 0)), pl.BlockSpec((D_MODEL, tm), lambda i: (0, i))],
        out_shape=[jax.ShapeDtypeStruct((SEQ, D_MODEL), BF16), jax.ShapeDtypeStruct((D_MODEL, SEQ), BF16)],
    )(x, g)


def _prenorm_bwd(x, dh, dy, g):
    tm = 512

    def body(x_ref, dh_ref, dy_ref, g_ref, gx_ref, dg_ref):
        xv = x_ref[...]
        r = lax.rsqrt(jnp.mean(xv * xv, axis=-1, keepdims=True) + EPS)
        n = xv * r
        dhv = dh_ref[...]
        dn = dhv * g_ref[...]
        gx_ref[...] = dy_ref[...] + r * (dn - n * jnp.mean(dn * n, axis=-1, keepdims=True))
        part = jnp.sum(dhv * n, axis=0, keepdims=True)

        @pl.when(pl.program_id(0) == 0)
        def _():
            dg_ref[...] = part

        @pl.when(pl.program_id(0) > 0)
        def _():
            dg_ref[...] += part

    row = pl.BlockSpec((tm, D_MODEL), lambda i: (i, 0))
    vec = pl.BlockSpec((1, D_MODEL), lambda i: (0, 0))
    return pl.pallas_call(
        body, name="prenorm_bwd", grid=(SEQ // tm,),
        in_specs=[row, row, row, vec], out_specs=[row, vec],
        out_shape=[jax.ShapeDtypeStruct((SEQ, D_MODEL), F32), jax.ShapeDtypeStruct((1, D_MODEL), F32)],
        compiler_params=_cparams(),
    )(x, dh, dy, g)


def _mla_prep_fwd(p, qg, kvg, wuq, wk, wv, rc, rs):
    tm = 512

    def body(lat_ref, qg_ref, kvg_ref, wuq_ref, wk_ref, wv_ref, c_ref, s_ref,
             cqn_ref, ckvn_ref, q_ref, k_ref, v_ref):
        c, s = c_ref[...], s_ref[...]
        lanes = _rope_lanes((tm, 128), MLA_ROPE_HALF, 128, 64)
        cq = lat_ref[:, 0:Q_RANK]
        r1 = lax.rsqrt(jnp.mean(cq * cq, axis=-1, keepdims=True) + EPS)
        cqn = (cq * r1 * qg_ref[...]).astype(BF16)
        cqn_ref[...] = cqn
        q = _nn(cqn, wuq_ref[...])
        for h in range(MLA_HEADS):
            sl = slice(h * 128, (h + 1) * 128)
            q_ref[:, sl] = (_rope_fwd(q[:, sl], c, s, MLA_ROPE_HALF, lanes) * MLA_SCALE).astype(BF16)
        ckv = lat_ref[:, Q_RANK:Q_RANK + KV_RANK]
        r2 = lax.rsqrt(jnp.mean(ckv * ckv, axis=-1, keepdims=True) + EPS)
        ckvn = (ckv * r2 * kvg_ref[...]).astype(BF16)
        ckvn_ref[...] = ckvn
        krr = _rope_fwd(lat_ref[:, Q_RANK + KV_RANK:N_LAT], c, s, MLA_ROPE_HALF, lanes)
        kn = _nn(ckvn, wk_ref[...])
        for h in range(MLA_HEADS):
            sl = slice(h * 128, (h + 1) * 128)
            k_ref[:, sl] = (kn[:, sl] + krr).astype(BF16)
        v_ref[...] = _nn(ckvn, wv_ref[...]).astype(BF16)

    def full(shape):
        return pl.BlockSpec(shape, lambda i: (0, 0))

    def rows(w):
        return pl.BlockSpec((tm, w), lambda i: (i, 0))

    return pl.pallas_call(
        body, name="mla_prep_fwd", grid=(SEQ // tm,),
        in_specs=[pl.BlockSpec((tm, N_LAT), lambda i: (i, COL_LAT // N_LAT)),
                  full((1, Q_RANK)), full((1, KV_RANK)), full((Q_RANK, 1024)), full((KV_RANK, 1024)),
                  full((KV_RANK, 512)), rows(128), rows(128)],
        out_specs=[rows(Q_RANK), rows(KV_RANK), rows(1024), rows(1024), rows(512)],
        out_shape=[jax.ShapeDtypeStruct((SEQ, Q_RANK), BF16), jax.ShapeDtypeStruct((SEQ, KV_RANK), BF16),
                   jax.ShapeDtypeStruct((SEQ, 1024), BF16), jax.ShapeDtypeStruct((SEQ, 1024), BF16),
                   jax.ShapeDtypeStruct((SEQ, 512), BF16)],
        compiler_params=_cparams(),
    )(p, qg, kvg, wuq, wk, wv, rc, rs)


def _mla_prep_bwd(dp_in, p, dq, dk, dv, qg, kvg, wuq, wk, wv, rc, rs):
    tm = 512

    def body(dp_any, lat_ref, dq_ref, dk_ref, dv_ref, qg_ref, kvg_ref, wuq_ref, wk_ref, wv_ref,
             c_ref, s_ref, dp_ref, dqb_ref, dkb_ref, dvb_ref, dgq_ref, dgkv_ref):
        del dp_any
        c, s = c_ref[...], s_ref[...]
        lanes = _rope_lanes((tm, 128), MLA_ROPE_HALF, 128, 64)
        lane = lax.broadcasted_iota(jnp.int32, (tm, 128), 1)
        dkr = jnp.zeros((tm, 128), F32)
        for h in range(MLA_HEADS):
            sl = slice(h * 128, (h + 1) * 128)
            dqb_ref[:, sl] = _rope_bwd(dq_ref[:, sl] * MLA_SCALE, c, s, MLA_ROPE_HALF, lanes).astype(BF16)
            dkh = dk_ref[:, sl]
            dkr = dkr + dkh
            dkb_ref[:, sl] = jnp.where(lane < 64, dkh, 0.0).astype(BF16)
        dkr = jnp.where((lane >= 64) & (lane < 96), dkr, 0.0)
        dkr = _rope_bwd(dkr, c, s, MLA_ROPE_HALF, lanes)
        dvb = dv_ref[...].astype(BF16)
        dvb_ref[...] = dvb

        cq = lat_ref[:, 0:Q_RANK]
        r1 = lax.rsqrt(jnp.mean(cq * cq, axis=-1, keepdims=True) + EPS)
        n1 = cq * r1
        dcqn = _nt(dqb_ref[...], wuq_ref[...])
        dn1 = dcqn * qg_ref[...]
        dcq = r1 * (dn1 - n1 * jnp.mean(dn1 * n1, axis=-1, keepdims=True))
        pq = jnp.sum(dcqn * n1, axis=0, keepdims=True)

        ckv = lat_ref[:, Q_RANK:Q_RANK + KV_RANK]
        r2 = lax.rsqrt(jnp.mean(ckv * ckv, axis=-1, keepdims=True) + EPS)
        n2 = ckv * r2
        dckvn = _nt(dkb_ref[...], wk_ref[...]) + _nt(dvb, wv_ref[...])
        dn2 = dckvn * kvg_ref[...]
        dckv = r2 * (dn2 - n2 * jnp.mean(dn2 * n2, axis=-1, keepdims=True))
        pkv = jnp.sum(dckvn * n2, axis=0, keepdims=True)

        dp_ref[:, 0:Q_RANK] = dcq.astype(BF16)
        dp_ref[:, Q_RANK:Q_RANK + KV_RANK] = dckv.astype(BF16)
        dp_ref[:, Q_RANK + KV_RANK:N_LAT] = dkr.astype(BF16)

        @pl.when(pl.program_id(0) == 0)
        def _():
            dgq_ref[...] = pq
            dgkv_ref[...] = pkv

        @pl.when(pl.program_id(0) > 0)
        def _():
            dgq_ref[...] += pq
            dgkv_ref[...] += pkv

    def full(shape):
        return pl.BlockSpec(shape, lambda i: (0, 0))

    def rows(w):
        return pl.BlockSpec((tm, w), lambda i: (i, 0))

    lat = pl.BlockSpec((tm, N_LAT), lambda i: (i, COL_LAT // N_LAT))
    return pl.pallas_call(
        body, name="mla_prep_bwd", grid=(SEQ // tm,),
        in_specs=[pl.BlockSpec(memory_space=pl.ANY), lat, rows(1024), rows(1024), rows(512),
                  full((1, Q_RANK)), full((1, KV_RANK)), full((Q_RANK, 1024)), full((KV_RANK, 1024)),
                  full((KV_RANK, 512)), rows(128), rows(128)],
        out_specs=[lat, rows(1024), rows(1024), rows(512), full((1, Q_RANK)), full((1, KV_RANK))],
        out_shape=[jax.ShapeDtypeStruct((SEQ, N_PAD), BF16), jax.ShapeDtypeStruct((SEQ, 1024), BF16),
                   jax.ShapeDtypeStruct((SEQ, 1024), BF16), jax.ShapeDtypeStruct((SEQ, 512), BF16),
                   jax.ShapeDtypeStruct((1, Q_RANK), F32), jax.ShapeDtypeStruct((1, KV_RANK), F32)],
        input_output_aliases={0: 0},
        compiler_params=_cparams(),
    )(dp_in, p, dq, dk, dv, qg, kvg, wuq, wk, wv, rc, rs)


FLASH_T = 1024


def _head_half(shape, hh):
    lane = lax.broadcasted_iota(jnp.int32, shape, 1)
    return (lane < 64) if hh == 0 else (lane >= 64)


def _causal_keep(t):
    row = lax.broadcasted_iota(jnp.int32, (t, t), 0)
    col = lax.broadcasted_iota(jnp.int32, (t, t), 1)
    return row >= col


def _tri_steps(nb, q_major):
    if q_major:
        pairs = [(i, kb) for i in range(nb) for kb in range(i + 1)]
    else:
        pairs = [(i, kb) for kb in range(nb) for i in range(kb, nb)]
    return jnp.asarray([p[0] for p in pairs], jnp.int32), jnp.asarray([p[1] for p in pairs], jnp.int32)


def _mla_flash_fwd(q, k, v):
    t = FLASH_T
    nb = SEQ // t
    qtab, ktab = _tri_steps(nb, True)

    def body(qi_ref, ki_ref, q_ref, k_ref, v_ref, o_ref, lse_ref, m_scr, l_scr, acc_scr):
        step = pl.program_id(1)
        i, kb = qi_ref[step], ki_ref[step]

        @pl.when(kb == 0)
        def _():
            m_scr[...] = jnp.full_like(m_scr, NEG)
            l_scr[...] = jnp.zeros_like(l_scr)
            acc_scr[...] = jnp.zeros_like(acc_scr)

        def update(masked):
            vv = v_ref[...]
            for hh in range(2):
                sl = slice(hh * 128, (hh + 1) * 128)
                s = _nt(q_ref[:, sl], k_ref[:, sl])
                if masked:
                    s = jnp.where(_causal_keep(t), s, NEG)
                m_prev = m_scr[hh]
                m_new = jnp.maximum(m_prev, jnp.max(s, axis=-1, keepdims=True))
                pr = jnp.exp(s - jnp.tile(m_new, (1, t // 128)))
                alpha = jnp.exp(m_prev - m_new)
                l_scr[hh] = alpha * l_scr[hh] + jnp.sum(pr, axis=-1, keepdims=True)
                acc_scr[hh] = alpha * acc_scr[hh] + _nn(pr.astype(BF16), vv)
                m_scr[hh] = m_new

        @pl.when(kb < i)
        def _():
            update(False)

        @pl.when(kb == i)
        def _():
            update(True)
            o0 = acc_scr[0] / l_scr[0]
            o1 = acc_scr[1] / l_scr[1]
            o_ref[...] = jnp.where(_head_half((t, 128), 0), o0, o1)
            for hh in range(2):
                lse_ref[:, hh * 128:(hh + 1) * 128] = m_scr[hh] + jnp.log(l_scr[hh])

    grid_spec = pltpu.PrefetchScalarGridSpec(
        num_scalar_prefetch=2, grid=(4, qtab.shape[0]),
        in_specs=[pl.BlockSpec((t, 256), lambda j, s, qi, ki: (qi[s], j)),
                  pl.BlockSpec((t, 256), lambda j, s, qi, ki: (ki[s], j)),
                  pl.BlockSpec((t, 128), lambda j, s, qi, ki: (ki[s], j))],
        out_specs=[pl.BlockSpec((t, 128), lambda j, s, qi, ki: (qi[s], j)),
                   pl.BlockSpec((t, 256), lambda j, s, qi, ki: (qi[s], j))],
        scratch_shapes=[pltpu.VMEM((2, t, 128), F32), pltpu.VMEM((2, t, 128), F32), pltpu.VMEM((2, t, 128), F32)])
    return pl.pallas_call(
        body, name="mla_flash_fwd", grid_spec=grid_spec,
        out_shape=[jax.ShapeDtypeStruct((SEQ, 512), F32), jax.ShapeDtypeStruct((SEQ, 1024), F32)],
        compiler_params=_cparams(),
    )(qtab, ktab, q, k, v)


def _mla_flash_bwd(q, k, v, o, do, lse, token=None):
    t = FLASH_T
    nb = SEQ // t
    qtab, ktab = _tri_steps(nb, False)
    after, after_specs = _after(token)

    def body(qi_ref, ki_ref, q_ref, k_ref, v_ref, o_ref, do_ref, lse_ref, *rest):
        dq_ref, dk_ref, dv_ref, dk_scr, dv_scr = rest[-5:]
        step = pl.program_id(1)
        i, kb = qi_ref[step], ki_ref[step]

        @pl.when(step == 0)
        def _():
            dq_ref[...] = jnp.zeros_like(dq_ref)

        @pl.when(i == kb)
        def _():
            dk_scr[...] = jnp.zeros_like(dk_scr)
            dv_scr[...] = jnp.zeros_like(dv_scr)

        def update(masked):
            vv = v_ref[...]
            ov = o_ref[...]
            dov = do_ref[...]
            rows = pl.ds(pl.multiple_of(i * t, t), t)
            for hh in range(2):
                sl = slice(hh * 128, (hh + 1) * 128)
                qh, kh = q_ref[:, sl], k_ref[:, sl]
                s = _nt(qh, kh)
                if masked:
                    s = jnp.where(_causal_keep(t), s, NEG)
                pr = jnp.exp(s - jnp.tile(lse_ref[:, sl], (1, t // 128)))
                dom = jnp.where(_head_half((t, 128), hh), dov, 0.0)
                domb = dom.astype(BF16)
                dv_scr[...] += _tn(pr.astype(BF16), domb)
                dpr = _nt(domb, vv)
                delta = jnp.sum(dom * ov, axis=-1, keepdims=True)
                ds = (pr * (dpr - delta)).astype(BF16)
                dq_ref[rows, sl] += _nn(ds, kh)
                dk_scr[hh] += _tn(ds, qh)

        @pl.when(i > kb)
        def _():
            update(False)

        @pl.when(i == kb)
        def _():
            update(True)

        @pl.when(i == nb - 1)
        def _():
            dk_ref[:, 0:128] = dk_scr[0]
            dk_ref[:, 128:256] = dk_scr[1]
            dv_ref[...] = dv_scr[...]

    qi_map = lambda j, s, qi, ki: (qi[s], j)
    ki_map = lambda j, s, qi, ki: (ki[s], j)
    grid_spec = pltpu.PrefetchScalarGridSpec(
        num_scalar_prefetch=2, grid=(4, qtab.shape[0]),
        in_specs=[pl.BlockSpec((t, 256), qi_map), pl.BlockSpec((t, 256), ki_map), pl.BlockSpec((t, 128), ki_map),
                  pl.BlockSpec((t, 128), qi_map), pl.BlockSpec((t, 128), qi_map), pl.BlockSpec((t, 256), qi_map)]
        + after_specs,
        out_specs=[pl.BlockSpec((SEQ, 256), lambda j, s, qi, ki: (0, j)), pl.BlockSpec((t, 256), ki_map),
                   pl.BlockSpec((t, 128), ki_map)],
        scratch_shapes=[pltpu.VMEM((2, t, 128), F32), pltpu.VMEM((t, 128), F32)])
    return pl.pallas_call(
        body, name="mla_flash_bwd", grid_spec=grid_spec,
        out_shape=[jax.ShapeDtypeStruct((SEQ, 1024), F32), jax.ShapeDtypeStruct((SEQ, 1024), F32),
                   jax.ShapeDtypeStruct((SEQ, 512), F32)],
        compiler_params=_cparams(),
    )(qtab, ktab, q, k, v, o, do, lse, *after)


DIL_UNROLL = 4


def _strided(start, size, d):
    return pl.ds(start, size) if d == 1 else pl.ds(start, size, stride=d)


def _dil_prep_fwd(p, rc, rs, g):
    d = DIL_DILATIONS[g]
    sub_len = SEQ // d
    ch = min(sub_len, 512)

    def body(p_ref, c_ref, s_ref, o_ref, x_scr):
        tq = pl.program_id(0)
        lanes = _rope_lanes((ch, 128), DIL_ROPE_HALF, 64, 0)
        o_ref[0, 0:BAND, :] = jnp.zeros((BAND, 128), BF16)

        @pl.when(tq < 2)
        def _():
            mult = jnp.where(tq == 0, DIL_SCALE, 1.0).astype(F32)
            for c0 in range(0, SEQ, ch):
                rows = pl.ds(c0, ch)
                x_scr[rows, :] = _rope_fwd(p_ref[rows, :], c_ref[rows, :] * mult, s_ref[rows, :] * mult, DIL_ROPE_HALF, lanes)

        def gather(src):
            for r in range(d):
                for c0 in range(0, sub_len, ch):
                    at = BAND + r * sub_len + c0
                    o_ref[0, at:at + ch, :] = src[_strided(r + c0 * d, ch, d), :].astype(BF16)

        @pl.when(tq < 2)
        def _():
            gather(x_scr)

        @pl.when(tq == 2)
        def _():
            gather(p_ref)

    tab = pl.BlockSpec((SEQ, 128), lambda tq, pr: (0, 0))
    return pl.pallas_call(
        body, name=f"dil_prep_fwd_g{g}", grid=(3, 4),
        in_specs=[pl.BlockSpec((SEQ, 128), lambda tq, pr: (0, COL_QKV // 128 + (tq * 3 + g) * 4 + pr)), tab, tab],
        out_specs=pl.BlockSpec((1, BAND + SEQ, 128), lambda tq, pr: (tq, 0, pr)),
        out_shape=jax.ShapeDtypeStruct((3, BAND + SEQ, 512), BF16),
        scratch_shapes=[pltpu.VMEM((SEQ, 128), F32)],
        compiler_params=_cparams(),
    )(p, rc, rs)


DIL_ST = 1024
DIL_NB = DIL_ST // BAND


def _band_keep(g, b, t):
    nbs = SEQ // DIL_DILATIONS[g] // BAND
    row = lax.broadcasted_iota(jnp.int32, (BAND, 2 * BAND), 0)
    col = lax.broadcasted_iota(jnp.int32, (BAND, 2 * BAND), 1)
    cur = (col >= BAND) & (row >= col - BAND)
    prev = (col < BAND) & (col >= row)
    if nbs >= DIL_NB:
        if b > 0:
            return cur | prev
        return cur | (prev & ((t * DIL_NB) % nbs != 0))
    return cur | prev if b % nbs else cur


def _dil_tok(g, b, t):
    d = DIL_DILATIONS[g]
    nbs = SEQ // d // BAND
    gb = t * DIL_NB + b
    return _strided((gb % nbs) * BAND * d + gb // nbs, BAND, d)


def _dil_attn_fwd2(qkv, g):
    def body(q_ref, k_ref, v_ref, o_ref, l_ref, s_scr, p_scr, o_scr):
        t = pl.program_id(1)
        base = t * DIL_ST
        half0 = _head_half((DIL_ST, 128), 0)
        lse_h = []
        for hh in range(2):
            half = _head_half((BAND, 128), hh)
            for b in range(DIL_NB):
                qv = q_ref[0, pl.ds(pl.multiple_of(base + (b + 1) * BAND, BAND), BAND), :]
                k2 = k_ref[0, pl.ds(pl.multiple_of(base + b * BAND, BAND), 2 * BAND), :]
                sb = _nt(jnp.where(half, qv, jnp.zeros_like(qv)), k2)
                s_scr[b * BAND:(b + 1) * BAND, :] = jnp.where(_band_keep(g, b, t), sb, NEG)
            s = s_scr[...]
            m = jnp.max(s, axis=-1, keepdims=True)
            pr = jnp.exp(s - m)
            den = jnp.sum(pr, axis=-1, keepdims=True)
            p_scr[...] = pr.astype(BF16)
            for b in range(DIL_NB):
                v2 = v_ref[0, pl.ds(pl.multiple_of(base + b * BAND, BAND), 2 * BAND), :]
                o_scr[hh, b * BAND:(b + 1) * BAND, :] = _nn(p_scr[b * BAND:(b + 1) * BAND, :], v2)
            o_scr[hh] = o_scr[hh] / den
            lse_h.append(m + jnp.log(den))
        out = jnp.where(half0, o_scr[0], o_scr[1])
        lse = jnp.where(half0, lse_h[0], lse_h[1])
        for b in range(DIL_NB):
            tok = _dil_tok(g, b, t)
            o_ref[tok, :] = out[b * BAND:(b + 1) * BAND, :]
            l_ref[tok, :] = lse[b * BAND:(b + 1) * BAND, :]

    def inp(tq):
        return pl.BlockSpec((1, BAND + SEQ, 128), lambda pr, t: (tq, 0, pr))

    out = pl.BlockSpec((SEQ, 128), lambda pr, t: (0, pr))
    return pl.pallas_call(
        body, name=f"dil_attn_fwd_g{g}", grid=(4, SEQ // DIL_ST),
        in_specs=[inp(0), inp(1), inp(2)], out_specs=[out, out],
        out_shape=[jax.ShapeDtypeStruct((SEQ, 512), F32), jax.ShapeDtypeStruct((SEQ, 512), F32)],
        scratch_shapes=[pltpu.VMEM((DIL_ST, 2 * BAND), F32), pltpu.VMEM((DIL_ST, 2 * BAND), BF16),
                        pltpu.VMEM((2, DIL_ST, 128), F32)],
        compiler_params=_cparams(),
    )(qkv, qkv, qkv)


def _dil_attn_bwd2(qkv, dyd, yd, lse_all, g, token=None):
    d = DIL_DILATIONS[g]
    sub_len = SEQ // d
    nst = SEQ // DIL_ST
    after, after_specs = _after(token)

    def body(q_ref, k_ref, v_ref, do_ref, y_ref, l_ref, *rest):
        out_ref, dk_scr, dv_scr, s_scr, dp_scr, p_scr, ds_scr, do_scr, y_scr, l_scr, dq_scr = rest[-11:]
        t = pl.program_id(1)
        base = t * DIL_ST

        @pl.when(t == 0)
        def _():
            dk_scr[...] = jnp.zeros_like(dk_scr)
            dv_scr[...] = jnp.zeros_like(dv_scr)

        for b in range(DIL_NB):
            tok = _dil_tok(g, b, t)
            do_scr[b * BAND:(b + 1) * BAND, :] = do_ref[tok, :]
            y_scr[b * BAND:(b + 1) * BAND, :] = y_ref[tok, :]
            l_scr[b * BAND:(b + 1) * BAND, :] = l_ref[tok, :]
        for hh in range(2):
            half = _head_half((BAND, 128), hh)
            half_st = _head_half((DIL_ST, 128), hh)
            dom = jnp.where(half_st, do_scr[...], 0.0)
            delta = jnp.sum(dom * y_scr[...], axis=-1, keepdims=True)
            lcol = jnp.max(jnp.where(half_st, l_scr[...], NEG), axis=-1, keepdims=True)
            for b in range(DIL_NB):
                rows = slice(b * BAND, (b + 1) * BAND)
                qv = q_ref[0, pl.ds(pl.multiple_of(base + (b + 1) * BAND, BAND), BAND), :]
                band = pl.ds(pl.multiple_of(base + b * BAND, BAND), 2 * BAND)
                sb = _nt(jnp.where(half, qv, jnp.zeros_like(qv)), k_ref[0, band, :])
                s_scr[rows, :] = jnp.where(_band_keep(g, b, t), sb, NEG)
                dp_scr[rows, :] = _nt(dom[rows, :].astype(BF16), v_ref[0, band, :])
            pr = jnp.exp(s_scr[...] - lcol)
            p_scr[...] = pr.astype(BF16)
            ds_scr[...] = (pr * (dp_scr[...] - delta)).astype(BF16)
            for b in range(DIL_NB):
                rows = slice(b * BAND, (b + 1) * BAND)
                qv = q_ref[0, pl.ds(pl.multiple_of(base + (b + 1) * BAND, BAND), BAND), :]
                band = pl.ds(pl.multiple_of(base + b * BAND, BAND), 2 * BAND)
                dqb = jnp.where(half, _nn(ds_scr[rows, :], k_ref[0, band, :]), 0.0)
                if hh == 0:
                    dq_scr[rows, :] = dqb
                else:
                    dq_scr[rows, :] += dqb
                half2 = _head_half((2 * BAND, 128), hh)
                dk_scr[band, :] += jnp.where(half2, _tn(ds_scr[rows, :], qv), 0.0)
                dv_scr[band, :] += _tn(p_scr[rows, :], dom[rows, :].astype(BF16))
        for b in range(DIL_NB):
            out_ref[pl.ds(0, 1), _dil_tok(g, b, t), :] = dq_scr[b * BAND:(b + 1) * BAND, :][None]

        @pl.when(t == nst - 1)
        def _():
            for r in range(d):
                rows = _strided(r, sub_len, d)
                out_ref[pl.ds(1, 1), rows, :] = dk_scr[BAND + r * sub_len:BAND + (r + 1) * sub_len, :][None]
                out_ref[pl.ds(2, 1), rows, :] = dv_scr[BAND + r * sub_len:BAND + (r + 1) * sub_len, :][None]

    def inp(tq):
        return pl.BlockSpec((1, BAND + SEQ, 128), lambda pr, t: (tq, 0, pr))

    tok_spec = pl.BlockSpec((SEQ, 128), lambda pr, t: (0, pr))
    st = (DIL_ST, 2 * BAND)
    return pl.pallas_call(
        body, name=f"dil_attn_bwd_g{g}", grid=(4, nst),
        in_specs=[inp(0), inp(1), inp(2), tok_spec, tok_spec, tok_spec] + after_specs,
        out_specs=pl.BlockSpec((3, SEQ, 128), lambda pr, t: (0, 0, pr)),
        out_shape=jax.ShapeDtypeStruct((3, SEQ, 512), F32),
        scratch_shapes=[pltpu.VMEM((BAND + SEQ, 128), F32), pltpu.VMEM((BAND + SEQ, 128), F32),
                        pltpu.VMEM(st, F32), pltpu.VMEM(st, F32), pltpu.VMEM(st, BF16), pltpu.VMEM(st, BF16),
                        pltpu.VMEM((DIL_ST, 128), F32), pltpu.VMEM((DIL_ST, 128), F32), pltpu.VMEM((DIL_ST, 128), F32),
                        pltpu.VMEM((DIL_ST, 128), F32)],
        compiler_params=_cparams(),
    )(qkv, qkv, qkv, dyd, yd, lse_all, *after)


def _band_masks():
    row = lax.broadcasted_iota(jnp.int32, (BAND, BAND), 0)
    col = lax.broadcasted_iota(jnp.int32, (BAND, BAND), 1)
    return row >= col, col >= row


def _dil_attn_fwd(qkv, g):
    d = DIL_DILATIONS[g]
    nbs = SEQ // d // BAND
    nblk = SEQ // BAND

    def body(q_ref, k_ref, v_ref, o_ref, l_ref):
        keep_c, keep_p = _band_masks()
        half0 = _head_half((BAND, 128), 0)

        def step(i, carry):
            cur = pl.ds(pl.multiple_of(i * BAND, BAND), BAND)
            prv = pl.ds(pl.multiple_of(jnp.maximum(i - 1, 0) * BAND, BAND), BAND)
            has_prev = (i % nbs) != 0
            qv = q_ref[0, cur, :]
            kc, kp = k_ref[0, cur, :], k_ref[0, prv, :]
            vc, vp = v_ref[0, cur, :], v_ref[0, prv, :]
            outs, lses = [], []
            for hh in range(2):
                qm = jnp.where(_head_half((BAND, 128), hh), qv, jnp.zeros_like(qv))
                sc = jnp.where(keep_c, _nt(qm, kc), NEG)
                sp = jnp.where(keep_p & has_prev, _nt(qm, kp), NEG)
                m = jnp.maximum(jnp.max(sc, axis=-1, keepdims=True), jnp.max(sp, axis=-1, keepdims=True))
                pc, pp = jnp.exp(sc - m), jnp.exp(sp - m)
                den = jnp.sum(pc, axis=-1, keepdims=True) + jnp.sum(pp, axis=-1, keepdims=True)
                o = (_nn(pc.astype(BF16), vc) + _nn(pp.astype(BF16), vp)) / den
                outs.append(o)
                lses.append(jnp.broadcast_to(m + jnp.log(den), (BAND, 128)))
            tok = _strided((i % nbs) * BAND * d + i // nbs, BAND, d)
            o_ref[tok, :] = jnp.where(half0, outs[0], outs[1])
            l_ref[tok, :] = jnp.where(half0, lses[0], lses[1])
            return carry

        lax.fori_loop(0, nblk, step, 0, unroll=DIL_UNROLL)

    def inp(tq):
        return pl.BlockSpec((1, SEQ, 128), lambda pr: (tq, 0, pr))

    out = pl.BlockSpec((SEQ, 128), lambda pr: (0, pr))
    return pl.pallas_call(
        body, name=f"dil_attn_fwd_g{g}", grid=(4,),
        in_specs=[inp(0), inp(1), inp(2)], out_specs=[out, out],
        out_shape=[jax.ShapeDtypeStruct((SEQ, 512), F32), jax.ShapeDtypeStruct((SEQ, 512), F32)],
        compiler_params=_cparams(),
    )(qkv, qkv, qkv)


def _dil_attn_bwd(qkv, dyd, yd, lse_all, g):
    d = DIL_DILATIONS[g]
    sub_len = SEQ // d
    nbs = sub_len // BAND
    nblk = SEQ // BAND

    def body(q_ref, k_ref, v_ref, do_ref, y_ref, l_ref, out_ref, dk_scr, dv_scr):
        keep_c, keep_p = _band_masks()
        dk_scr[...] = jnp.zeros_like(dk_scr)
        dv_scr[...] = jnp.zeros_like(dv_scr)

        def step(i, carry):
            cur = pl.ds(pl.multiple_of(i * BAND, BAND), BAND)
            prv = pl.ds(pl.multiple_of(jnp.maximum(i - 1, 0) * BAND, BAND), BAND)
            has_prev = (i % nbs) != 0
            tok = _strided((i % nbs) * BAND * d + i // nbs, BAND, d)
            qv = q_ref[0, cur, :]
            kc, kp = k_ref[0, cur, :], k_ref[0, prv, :]
            vc, vp = v_ref[0, cur, :], v_ref[0, prv, :]
            dov, yv, lv = do_ref[tok, :], y_ref[tok, :], l_ref[tok, :]
            dq = jnp.zeros((BAND, 128), F32)
            dkc = jnp.zeros((BAND, 128), F32)
            dkp = jnp.zeros((BAND, 128), F32)
            dvc = jnp.zeros((BAND, 128), F32)
            dvp = jnp.zeros((BAND, 128), F32)
            for hh in range(2):
                half = _head_half((BAND, 128), hh)
                qm = jnp.where(half, qv, jnp.zeros_like(qv))
                lcol = jnp.max(jnp.where(half, lv, NEG), axis=-1, keepdims=True)
                pc = jnp.exp(jnp.where(keep_c, _nt(qm, kc), NEG) - lcol)
                pp = jnp.exp(jnp.where(keep_p & has_prev, _nt(qm, kp), NEG) - lcol)
                dom = jnp.where(half, dov, 0.0)
                domb = dom.astype(BF16)
                delta = jnp.sum(dom * yv, axis=-1, keepdims=True)
                dsc = (pc * (_nt(domb, vc) - delta)).astype(BF16)
                dsp = (pp * (_nt(domb, vp) - delta)).astype(BF16)
                dvc = dvc + _tn(pc.astype(BF16), domb)
                dvp = dvp + _tn(pp.astype(BF16), domb)
                dq = dq + jnp.where(half, _nn(dsc, kc) + _nn(dsp, kp), 0.0)
                dkc = dkc + jnp.where(half, _tn(dsc, qv), 0.0)
                dkp = dkp + jnp.where(half, _tn(dsp, qv), 0.0)
            out_ref[pl.ds(0, 1), tok, :] = dq[None]
            dk_scr[cur, :] += dkc
            dk_scr[prv, :] += dkp
            dv_scr[cur, :] += dvc
            dv_scr[prv, :] += dvp
            return carry

        lax.fori_loop(0, nblk, step, 0, unroll=DIL_UNROLL)
        for r in range(d):
            rows = _strided(r, sub_len, d)
            out_ref[pl.ds(1, 1), rows, :] = dk_scr[r * sub_len:(r + 1) * sub_len, :][None]
            out_ref[pl.ds(2, 1), rows, :] = dv_scr[r * sub_len:(r + 1) * sub_len, :][None]

    def inp(tq):
        return pl.BlockSpec((1, SEQ, 128), lambda pr: (tq, 0, pr))

    tok_spec = pl.BlockSpec((SEQ, 128), lambda pr: (0, pr))
    return pl.pallas_call(
        body, name=f"dil_attn_bwd_g{g}", grid=(4,),
        in_specs=[inp(0), inp(1), inp(2), tok_spec, tok_spec, tok_spec],
        out_specs=pl.BlockSpec((3, SEQ, 128), lambda pr: (0, 0, pr)),
        out_shape=jax.ShapeDtypeStruct((3, SEQ, 512), F32),
        scratch_shapes=[pltpu.VMEM((SEQ, 128), F32), pltpu.VMEM((SEQ, 128), F32)],
        compiler_params=_cparams(),
    )(qkv, qkv, qkv, dyd, yd, lse_all)


def _dil_prep_bwd(dp_in, dqkv, rc, rs, g):
    tm = 1024

    def body(dp_any, g_ref, c_ref, s_ref, dp_ref):
        del dp_any
        tq = pl.program_id(0)

        @pl.when(tq == 2)
        def _():
            dp_ref[...] = g_ref[0].astype(BF16)

        @pl.when(tq < 2)
        def _():
            mult = jnp.where(tq == 0, DIL_SCALE, 1.0).astype(F32)
            lanes = _rope_lanes((tm, 128), DIL_ROPE_HALF, 64, 0)
            cv, sv = c_ref[...], s_ref[...] * mult
            cv = cv * mult
            for pr in range(4):
                gv = g_ref[0, :, pr * 128:(pr + 1) * 128]
                dp_ref[:, pr * 128:(pr + 1) * 128] = _rope_bwd(gv, cv, sv, DIL_ROPE_HALF, lanes).astype(BF16)

    tab = pl.BlockSpec((tm, 128), lambda tq, i: (i, 0))
    return pl.pallas_call(
        body, name=f"dil_prep_bwd_g{g}", grid=(3, SEQ // tm),
        in_specs=[pl.BlockSpec(memory_space=pl.ANY),
                  pl.BlockSpec((1, tm, 512), lambda tq, i: (tq, i, 0)), tab, tab],
        out_specs=pl.BlockSpec((tm, 512), lambda tq, i: (i, COL_QKV // 512 + tq * 3 + g)),
        out_shape=jax.ShapeDtypeStruct((SEQ, N_PAD), BF16),
        input_output_aliases={0: 0},
    )(dp_in, dqkv, rc, rs)


TAIL_T = 256


def _tail(p, ya, o_g, l_g, x, target, wpm, wpd, wout, post_g):
    tm = TAIL_T

    def body(pgz_ref, ya_ref, o0_ref, o1_ref, o2_ref, l0_ref, l1_ref, l2_ref, x_ref, t_ref,
             wpm_ref, wpd_ref, wout_ref, pg_ref,
             dp_ref, dy_ref, mg_ref, dt_ref, ua_ref, dpa_ref, ud_ref, dpd_ref, dya_ref, dyd_ref,
             yd_ref, lse_ref, loss_ref, dgp_ref):
        l0, l1, l2 = l0_ref[...], l1_ref[...], l2_ref[...]
        mx = jnp.maximum(jnp.maximum(l0, l1), l2)
        e0, e1, e2 = jnp.exp(l0 - mx), jnp.exp(l1 - mx), jnp.exp(l2 - mx)
        den = e0 + e1 + e2
        yd = (e0 * o0_ref[...] + e1 * o1_ref[...] + e2 * o2_ref[...]) / den
        yd_ref[...] = yd
        lse_ref[...] = mx + jnp.log(den)
        ya = ya_ref[...]

        gm, gd = pgz_ref[:, 0:1024], pgz_ref[:, 1024:2048]
        zm, zd = pgz_ref[:, 2048:2560], pgz_ref[:, 2560:3072]
        szm, szd = _sigmoid(zm), _sigmoid(zd)
        sm, sd = zm * szm, zd * szd
        ua = (ya * sm).astype(BF16)
        ud = (yd * sd).astype(BF16)
        ua_ref[...] = ua
        ud_ref[...] = ud
        pa = _nn(ua, wpm_ref[...])
        pd = _nn(ud, wpd_ref[...])
        sgm, sgd = _sigmoid(gm), _sigmoid(gd)
        mg = (sgm * pa + sgd * pd).astype(BF16)
        mg_ref[...] = mg
        t = _nn(mg, wout_ref[...])
        r3 = lax.rsqrt(jnp.mean(t * t, axis=-1, keepdims=True) + EPS)
        n = t * r3
        pg = pg_ref[...]
        err = x_ref[...] + n * pg - t_ref[...]
        lpart = jnp.sum(err * err, axis=0, keepdims=True)

        dy = err * (1.0 / D_MODEL)
        dy_ref[...] = dy
        gpart = jnp.sum(dy * n, axis=0, keepdims=True)
        dn = dy * pg
        dt = (r3 * (dn - n * jnp.mean(dn * n, axis=-1, keepdims=True))).astype(BF16)
        dt_ref[...] = dt
        dmg = _nt(dt, wout_ref[...])
        dpa = (dmg * sgm).astype(BF16)
        dpd = (dmg * sgd).astype(BF16)
        dpa_ref[...] = dpa
        dpd_ref[...] = dpd
        dp_ref[:, 0:1024] = (dmg * pa * sgm * (1.0 - sgm)).astype(BF16)
        dp_ref[:, 1024:2048] = (dmg * pd * sgd * (1.0 - sgd)).astype(BF16)
        dua = _nt(dpa, wpm_ref[...])
        dud = _nt(dpd, wpd_ref[...])
        dya_ref[...] = dua * sm
        dyd_ref[...] = dud * sd
        dp_ref[:, 2048:2560] = (dua * ya * szm * (1.0 + zm * (1.0 - szm))).astype(BF16)
        dp_ref[:, 2560:3072] = (dud * yd * szd * (1.0 + zd * (1.0 - szd))).astype(BF16)

        @pl.when(pl.program_id(0) == 0)
        def _():
            loss_ref[...] = lpart
            dgp_ref[...] = gpart

        @pl.when(pl.program_id(0) > 0)
        def _():
            loss_ref[...] += lpart
            dgp_ref[...] += gpart

    def rows(w):
        return pl.BlockSpec((tm, w), lambda i: (i, 0))

    def full(shape):
        return pl.BlockSpec(shape, lambda i: (0, 0))

    def sds(w, dt):
        return jax.ShapeDtypeStruct((SEQ, w), dt)

    return pl.pallas_call(
        body, name="tail", grid=(SEQ // tm,),
        in_specs=[rows(3072), rows(512), rows(512), rows(512), rows(512), rows(512), rows(512), rows(512),
                  rows(1024), rows(1024), full((512, 1024)), full((512, 1024)), full((1024, 1024)), full((1, 1024))],
        out_specs=[rows(3072), rows(1024), rows(1024), rows(1024), rows(512), rows(1024), rows(512), rows(1024),
                   rows(512), rows(512), rows(512), rows(512), full((1, 1024)), full((1, 1024))],
        out_shape=[sds(N_PAD, BF16), sds(1024, F32), sds(1024, BF16), sds(1024, BF16), sds(512, BF16),
                   sds(1024, BF16), sds(512, BF16), sds(1024, BF16), sds(512, F32), sds(512, F32),
                   sds(512, F32), sds(512, F32),
                   jax.ShapeDtypeStruct((1, 1024), F32), jax.ShapeDtypeStruct((1, 1024), F32)],
        compiler_params=_cparams(),
    )(p, ya, o_g[0], o_g[1], o_g[2], l_g[0], l_g[1], l_g[2], x, target, wpm, wpd, wout, post_g)


def _sum_parts(recv, own, me, tr, name):
    n, r, w = recv.shape
    own_spec = (pl.BlockSpec((tr, w), lambda i, me_ref: (i, 0)) if own.ndim == 2
                else pl.BlockSpec((None, tr, w), lambda i, me_ref: (me_ref[0], i, 0)))

    def body(me_ref, p_ref, own_ref, o_ref):
        mine = own_ref[...].astype(F32)
        acc = jnp.zeros((tr, w), F32)
        for s in range(n):
            acc = acc + jnp.where(me_ref[0] == s, mine, p_ref[s].astype(F32))
        o_ref[...] = acc

    return pl.pallas_call(
        body, name=name,
        grid_spec=pltpu.PrefetchScalarGridSpec(
            num_scalar_prefetch=1, grid=(r // tr,),
            in_specs=[pl.BlockSpec((n, tr, w), lambda i, me_ref: (0, i, 0)), own_spec],
            out_specs=pl.BlockSpec((tr, w), lambda i, me_ref: (i, 0))),
        out_shape=jax.ShapeDtypeStruct((r, w), F32),
    )(me.reshape(1), recv, own)


def _adamw(w, g, m, v, name):
    lead = w.shape[:-2]
    r, c = w.shape[-2:]
    tr = max([t for t in range(8, 257, 8) if r % t == 0], default=r)
    c1 = 1.0 - ADAM_B1 ** ADAM_STEP
    c2 = 1.0 - ADAM_B2 ** ADAM_STEP

    def body(w_ref, g_ref, m_ref, v_ref, d_ref, nm_ref, nv_ref):
        gv = g_ref[...]
        nm = ADAM_B1 * m_ref[...] + (1.0 - ADAM_B1) * gv
        nv = ADAM_B2 * v_ref[...] + (1.0 - ADAM_B2) * (gv * gv)
        nm_ref[...] = nm
        nv_ref[...] = nv
        d_ref[...] = -ADAM_LR * ((nm / c1) / (jnp.sqrt(nv / c2) + ADAM_EPS) + ADAM_WD * w_ref[...])

    zeros = (0,) * len(lead)
    spec = pl.BlockSpec((1,) * len(lead) + (tr, c), lambda i: zeros + (i, 0))
    sd = jax.ShapeDtypeStruct(w.shape, F32)
    return pl.pallas_call(
        body, name=name, grid=(r // tr,),
        in_specs=[spec] * 4, out_specs=[spec] * 3, out_shape=[sd] * 3,
    )(w, g, m, v)


ANY = pl.BlockSpec(memory_space=pl.ANY)


def _my_place():
    return lax.axis_index("x"), lax.axis_index("y"), lax.axis_index("c")


def _allgather_weights(mats):
    def body(*refs):
        w_refs, out_refs = refs[:N_MATS], refs[N_MATS:2 * N_MATS]
        send_sems, recv_sems = refs[2 * N_MATS:]
        x, y, c = _my_place()
        sibling = (x, y, 1 - c)
        chips = [(1 - x, y), (x, 1 - y), (1 - x, 1 - y)]

        def copy(k, src, dst, to):
            return pltpu.make_async_remote_copy(src_ref=src, dst_ref=dst, send_sem=send_sems.at[k],
                                                recv_sem=recv_sems.at[k], device_id=to, device_id_type=MESH)

        def half(mi, shard, hc):
            hr = SHARD_SHAPES[mi][0] // 2
            return out_refs[mi].at[shard, pl.ds(pl.multiple_of(hc * hr, 16), hr), :]

        started = []
        for mi in range(N_MATS):
            hr = SHARD_SHAPES[mi][0] // 2
            my_half = w_refs[mi].at[pl.ds(pl.multiple_of(c * hr, 16), hr), :]
            for j, (cx, cy) in enumerate(chips):
                cp = copy(mi * 6 + j, my_half, half(mi, 2 * x + y, c), (cx, cy, c))
                cp.start()
                started.append(cp)
        for mi in range(N_MATS):
            for j, (cx, cy) in enumerate(chips):
                landed = half(mi, 2 * cx + cy, c)
                copy(mi * 6 + j, landed, landed, (cx, cy, c)).wait_recv()
                fw = copy(mi * 6 + 3 + j, landed, landed, sibling)
                fw.start()
                started.append(fw)
        for mi in range(N_MATS):
            for j, (cx, cy) in enumerate(chips):
                other = half(mi, 2 * cx + cy, 1 - c)
                copy(mi * 6 + 3 + j, other, other, sibling).wait_recv()
        for cp in started:
            cp.wait_send()

    return pl.pallas_call(
        body, name="allgather_weights",
        in_specs=[ANY] * N_MATS, out_specs=[ANY] * N_MATS,
        out_shape=[jax.ShapeDtypeStruct((4, r, c), BF16) for r, c in SHARD_SHAPES],
        scratch_shapes=[pltpu.SemaphoreType.DMA((6 * N_MATS,)), pltpu.SemaphoreType.DMA((6 * N_MATS,))],
    )(*mats)


HBM = pl.BlockSpec(memory_space=pltpu.HBM)
SEM = pl.BlockSpec(memory_space=pltpu.SEMAPHORE)
DATAFLOW = pltpu.SideEffectType.DATAFLOW_SIDE_EFFECTING


def _peers(x, y, c):
    out = []
    for k in range(1, 8):
        px, py, pc = x ^ (k >> 2), y ^ ((k >> 1) & 1), c ^ (k & 1)
        out.append((k - 1, (px, py, pc), 4 * px + 2 * py + pc))
    return out


def _exchange_start(parts, name):
    n = len(parts)

    def body(*refs):
        p_refs, land_refs = refs[:n], refs[n:2 * n]
        send_sems, recv_sems, token = refs[2 * n], refs[2 * n + 1], refs[-1]
        x, y, c = _my_place()
        me = 4 * x + 2 * y + c
        for k, dev, peer in _peers(x, y, c):
            for mi in range(n):
                pltpu.make_async_remote_copy(
                    src_ref=p_refs[mi].at[peer], dst_ref=land_refs[mi].at[me], send_sem=send_sems.at[k * n + mi],
                    recv_sem=recv_sems.at[k * n + mi], device_id=dev, device_id_type=MESH).start()
        token[...] = jnp.zeros_like(token)

    hbm = [pltpu.HBM(p.shape, p.dtype) for p in parts]
    outs = pl.pallas_call(
        body, name=name + "_start",
        out_shape=(pltpu.SemaphoreType.DMA((7 * n,)), pltpu.SemaphoreType.DMA((7 * n,)), *hbm, *hbm,
                   jax.ShapeDtypeStruct((8, 128), F32)),
        in_specs=[HBM] * (2 * n), out_specs=(SEM, SEM, *[HBM] * (2 * n), pl.BlockSpec(memory_space=pltpu.VMEM)),
        input_output_aliases={i: 2 + i for i in range(2 * n)},
        compiler_params=pltpu.CompilerParams(has_side_effects=DATAFLOW),
    )(*[pltpu.with_memory_space_constraint(p, pltpu.HBM) for p in parts],
      *[pltpu.with_memory_space_constraint(lax.empty(p.shape, p.dtype), pltpu.HBM) for p in parts])
    return (name, outs[:-1]), outs[-1]


def _exchange_wait(handle, after):
    name, outs = handle
    n = (len(outs) - 2) // 2

    def body(*refs):
        p_refs, land_refs = refs[:n], refs[n:2 * n]
        send_sems, recv_sems = refs[2 * n], refs[2 * n + 1]
        x, y, c = _my_place()
        me = 4 * x + 2 * y + c
        for k, dev, peer in _peers(x, y, c):
            for mi in range(n):
                pltpu.make_async_remote_copy(
                    src_ref=p_refs[mi].at[peer], dst_ref=land_refs[mi].at[me], send_sem=send_sems.at[k * n + mi],
                    recv_sem=recv_sems.at[k * n + mi], device_id=dev, device_id_type=MESH).wait_send()
                slot = land_refs[mi].at[peer]
                pltpu.make_async_remote_copy(
                    src_ref=slot, dst_ref=slot, send_sem=send_sems.at[k * n + mi],
                    recv_sem=recv_sems.at[k * n + mi], device_id=dev, device_id_type=MESH).wait_recv()

    bufs = outs[2:]
    res = pl.pallas_call(
        body, name=name + "_wait", out_shape=tuple(pltpu.HBM(b.shape, b.dtype) for b in bufs),
        in_specs=[HBM] * (2 * n) + [SEM, SEM, ANY], out_specs=tuple([HBM] * (2 * n)),
        input_output_aliases={i: i for i in range(2 * n)},
        compiler_params=pltpu.CompilerParams(has_side_effects=DATAFLOW),
    )(*bufs, outs[0], outs[1], after)
    return list(res[n:])


def _swap_halves(halves, gvec):
    def body(*refs):
        g_refs, gv_ref = refs[:N_MATS], refs[N_MATS]
        out_refs, rg_ref = refs[N_MATS + 1:2 * N_MATS + 1], refs[2 * N_MATS + 1]
        send_sems, recv_sems = refs[2 * N_MATS + 2:]
        x, y, c = _my_place()
        me = 4 * x + 2 * y + c
        sends = []
        for mi in range(N_MATS):
            cp = pltpu.make_async_remote_copy(src_ref=g_refs[mi], dst_ref=out_refs[mi].at[c], send_sem=send_sems.at[mi],
                                              recv_sem=recv_sems.at[mi], device_id=(x, y, 1 - c), device_id_type=MESH)
            cp.start()
            sends.append(cp)
        for k, dev, peer in _peers(x, y, c):
            cp = pltpu.make_async_remote_copy(src_ref=gv_ref, dst_ref=rg_ref.at[me], send_sem=send_sems.at[N_MATS + k],
                                              recv_sem=recv_sems.at[N_MATS + k], device_id=dev, device_id_type=MESH)
            cp.start()
            sends.append(cp)
        for mi in range(N_MATS):
            got = out_refs[mi].at[1 - c]
            pltpu.make_async_remote_copy(src_ref=got, dst_ref=got, send_sem=send_sems.at[mi], recv_sem=recv_sems.at[mi],
                                         device_id=(x, y, 1 - c), device_id_type=MESH).wait_recv()
        for k, dev, peer in _peers(x, y, c):
            got = rg_ref.at[peer]
            pltpu.make_async_remote_copy(src_ref=got, dst_ref=got, send_sem=send_sems.at[N_MATS + k],
                                         recv_sem=recv_sems.at[N_MATS + k], device_id=dev, device_id_type=MESH).wait_recv()
        for cp in sends:
            cp.wait_send()

    outs = pl.pallas_call(
        body, name="swap_halves",
        in_specs=[ANY] * (N_MATS + 1), out_specs=[ANY] * (N_MATS + 1),
        out_shape=[jax.ShapeDtypeStruct((2, r // 2, c), F32) for r, c in SHARD_SHAPES]
        + [jax.ShapeDtypeStruct((8, 8, N_GAINS), F32)],
        scratch_shapes=[pltpu.SemaphoreType.DMA((N_MATS + 7,)), pltpu.SemaphoreType.DMA((N_MATS + 7,))],
    )(*halves, gvec)
    return outs[:N_MATS], outs[N_MATS]


def _set_slot(arr, block, idx):
    return lax.dynamic_update_slice(arr, block[None], (idx,) + (0,) * block.ndim)


PAD_RUNS = ((6304, 8352, 0), (5280, 6304, COL_Z), (672, 5280, COL_QKV), (0, 640, COL_LAT), (640, 672, COL_LAT + 704))
W_IN_SHARD = 2088


def _full_weights(gathered):
    def cols(a):
        return jnp.concatenate([a[s] for s in range(4)], axis=1)

    w_uq, w_ukv, w_pm, w_pd = [cols(a) for a in gathered[1:5]]
    w_out = gathered[5].reshape(D_MODEL, D_MODEL)
    g_in = gathered[0]
    pieces, at = [], 0
    for lo, hi, pad_lo in sorted(PAD_RUNS, key=lambda t: t[2]):
        if pad_lo > at:
            pieces.append(jnp.zeros((D_MODEL, pad_lo - at), g_in.dtype))
        for s in range(4):
            a_, b_ = max(lo, s * W_IN_SHARD), min(hi, (s + 1) * W_IN_SHARD)
            if a_ < b_:
                pieces.append(g_in[s][:, a_ - s * W_IN_SHARD:b_ - s * W_IN_SHARD])
        at = pad_lo + hi - lo
    pieces.append(jnp.zeros((D_MODEL, N_PAD - at), g_in.dtype))
    w_pad = jnp.concatenate(pieces, axis=1)
    z32 = jnp.zeros((Q_RANK, 32), w_uq.dtype)
    wuq_pad = jnp.concatenate([t for h in range(MLA_HEADS) for t in (w_uq[:, h * 96:(h + 1) * 96], z32)], axis=1)
    z64 = jnp.zeros((KV_RANK, 64), w_ukv.dtype)
    wk_pad = jnp.concatenate([t for h in range(MLA_HEADS) for t in (w_ukv[:, h * 128:h * 128 + 64], z64)], axis=1)
    wv = jnp.concatenate([w_ukv[:, h * 128 + 64:(h + 1) * 128] for h in range(MLA_HEADS)], axis=1)
    return w_pad, wuq_pad, wk_pad, wv, w_pm, w_pd, w_out


W_IN_LAT = 672


def _grad_parts_in_early(dw_early):
    def in_block(s, h):
        rows = slice(h * 512, (h + 1) * 512)
        out = []
        for lo, hi, pad_lo in sorted(PAD_RUNS):
            a_, b_ = max(lo, s * W_IN_SHARD), min(hi, (s + 1) * W_IN_SHARD)
            if a_ < b_:
                out.append(jnp.zeros((512, b_ - a_), dw_early.dtype) if pad_lo >= COL_LAT
                           else dw_early[rows, pad_lo + a_ - lo:pad_lo + b_ - lo])
        return jnp.concatenate(out, axis=1)

    return jnp.stack([in_block(s, h) for s in range(4) for h in range(2)])


def _grad_parts_in_late(dw_late):
    cols = jnp.concatenate([dw_late[:, 0:640], dw_late[:, 704:736]], axis=1)
    zero = jnp.zeros((512, W_IN_LAT), dw_late.dtype)
    return jnp.stack([cols[0:512], cols[512:1024]] + [zero] * 6)


def _col_blocks(m):
    r, c = m.shape[0] // 2, m.shape[1] // 4
    return jnp.stack([m[h * r:(h + 1) * r, s * c:(s + 1) * c] for s in range(4) for h in range(2)])


def _grad_parts_mla(dwuq_pad, dwk_pad, dwv):
    d_uq = jnp.concatenate([dwuq_pad[:, h * 128:h * 128 + 96] for h in range(MLA_HEADS)], axis=1)
    d_ukv = jnp.concatenate([t for h in range(MLA_HEADS) for t in (dwk_pad[:, h * 128:h * 128 + 64], dwv[:, h * 64:(h + 1) * 64])],
                            axis=1)
    return [_col_blocks(d_uq), _col_blocks(d_ukv)]


def _rope_tables(positions):
    pos = positions.reshape(SEQ).astype(F32)
    lane = jnp.arange(128)

    def table(rot, first, period):
        inv = ROPE_THETA ** (-jnp.arange(0, rot, 2, dtype=F32) / rot)
        half = rot // 2
        off = lane % period - first
        in1, in2 = (off >= 0) & (off < half), (off >= half) & (off < rot)
        inv_lane = jnp.where(in1 | in2, inv[jnp.clip(off % half, 0, half - 1)], 0.0)
        sign = jnp.where(in1, -1.0, 1.0).astype(F32)
        ang = pos[:, None] * inv_lane[None, :]
        return jnp.cos(ang), jnp.sin(ang) * sign[None, :]

    return table(32, 64, 128), table(16, 0, 64)


def _device_grads(x, positions, target, gains, gathered, send):
    pre_g, q_g, kv_g, post_g = gains
    w_pad, wuq_pad, wk_pad, wv, w_pm, w_pd, w_out = _full_weights(gathered)
    (mc, ms), (dc, ds) = _rope_tables(positions)

    h, h_t = _prenorm_fwd(x, pre_g)
    p = _matmul(h, w_pad, "nn", F32, 1024, 1408, 1024, "in_proj")
    cqn, ckvn, q, k, v = _mla_prep_fwd(p, q_g, kv_g, wuq_pad, wk_pad, wv, mc, ms)
    ya, lse_m = _mla_flash_fwd(q, k, v)
    qkv = [_dil_prep_fwd(p, dc, ds, g) for g in range(3)]
    o_g, l_g = zip(*[_dil_attn_fwd2(qkv[g], g) for g in range(3)])
    (dp, dy, mg, dt, ua, dpa, ud, dpd, dya, dyd, yd, lse_d, loss_cols, dg_post) = _tail(
        p, ya, o_g, l_g, x, target, w_pm, w_pd, w_out, post_g)

    for g in range(3):
        dqkv = _dil_attn_bwd2(qkv[g], dyd, yd, lse_d, g)
        dp = _dil_prep_bwd(dp, dqkv, dc, ds, g)
    dw_early = _matmul(h_t, dp, "nn", BF16, 1024, 1536, 2048, "dw_in_early", b_cols=(0, COL_LAT // 1536))
    dwpm = _matmul(ua, dpa, "tn", BF16, 512, 1024, 512, "dw_proj_mla")
    dwpd = _matmul(ud, dpd, "tn", BF16, 512, 1024, 512, "dw_proj_dil")
    dwout = _matmul(mg, dt, "tn", BF16, 1024, 1024, 512, "dw_out")
    early, token = send([_grad_parts_in_early(dw_early), _col_blocks(dwpm), _col_blocks(dwpd),
                         dwout.reshape(8, 128, D_MODEL)], "exchange_early")

    dq, dk, dv = _mla_flash_bwd(q, k, v, ya, dya, lse_m, token)
    dp, dqb, dkb, dvb, dg_q, dg_kv = _mla_prep_bwd(dp, p, dq, dk, dv, q_g, kv_g, wuq_pad, wk_pad, wv, mc, ms)
    dwuq_pad = _matmul(cqn, dqb, "tn", BF16, Q_RANK, 1024, 512, "dw_uq")
    dwk_pad = _matmul(ckvn, dkb, "tn", BF16, KV_RANK, 1024, 512, "dw_k")
    dwv = _matmul(ckvn, dvb, "tn", BF16, KV_RANK, 512, 512, "dw_v")
    dw_late = _matmul(h_t, dp, "nn", BF16, 1024, N_LAT, 2048, "dw_in_late", b_cols=(COL_LAT // N_LAT, 1))
    late, token = send([_grad_parts_in_late(dw_late)] + _grad_parts_mla(dwuq_pad, dwk_pad, dwv), "exchange_late")

    dh = _matmul(dp, w_pad, "nt", F32, 1024, 1024, 1408, "dh", token)
    grad_x, dg_pre = _prenorm_bwd(x, dh, dy, pre_g)

    gvec = jnp.concatenate([dg_pre, dg_q, dg_kv, dg_post], axis=1)
    return loss_cols, grad_x, (early, late), gvec


def kernel(x, positions, pre_norm_g, w_in, q_norm_g, w_uq, kv_norm_g, w_ukv, w_proj_mla, w_proj_dil, w_out, post_norm_g, loss_target, m_pre_norm_g, m_w_in, m_q_norm_g, m_w_uq, m_kv_norm_g, m_w_ukv, m_w_proj_mla, m_w_proj_dil, m_w_out, m_post_norm_g, v_pre_norm_g, v_w_in, v_q_norm_g, v_w_uq, v_kv_norm_g, v_w_ukv, v_w_proj_mla, v_w_proj_dil, v_w_out, v_post_norm_g):
    xi, yi, ci = _my_place()
    chip, me = 2 * xi + yi, 4 * xi + 2 * yi + ci
    mats = [w.reshape(w.shape[1:]).astype(BF16) for w in (w_in, w_uq, w_ukv, w_proj_mla, w_proj_dil, w_out)]
    gathered = [_set_slot(g, m, chip) for g, m in zip(_allgather_weights(mats), mats)]
    gains = (pre_norm_g, q_norm_g, kv_norm_g, post_norm_g)
    sent = {}

    def send(blocks, name):
        sent[name] = blocks
        return _exchange_start(blocks, name)

    loss_cols, grad_x, (early, late), gvec = _device_grads(x[0], positions, loss_target[0], gains, gathered, send)

    loss = lax.psum(jnp.sum(loss_cols) * (0.5 / D_MODEL), ("x", "y", "c"))

    recv_e, recv_l = _exchange_wait(early, grad_x), _exchange_wait(late, grad_x)
    own_e, own_l = sent["exchange_early"], sent["exchange_late"]
    in_e = _sum_parts(recv_e[0], own_e[0], me, 64, "sum_grad_in_early")
    in_l = _sum_parts(recv_l[0], own_l[0], me, 64, "sum_grad_in_late")
    half_in = jnp.concatenate([in_e[:, :W_IN_LAT] + jnp.where(chip == 0, in_l, 0.0), in_e[:, W_IN_LAT:]], axis=1)
    halves = [half_in,
              _sum_parts(recv_l[1], own_l[1], me, 64, "sum_grad_uq"), _sum_parts(recv_l[2], own_l[2], me, 64, "sum_grad_ukv"),
              _sum_parts(recv_e[1], own_e[1], me, 64, "sum_grad_pm"), _sum_parts(recv_e[2], own_e[2], me, 64, "sum_grad_pd"),
              _sum_parts(recv_e[3], own_e[3], me, 64, "sum_grad_out")]
    gvec8 = jnp.pad(gvec, ((0, 7), (0, 0)))
    swapped, recv_gains = _swap_halves(halves, gvec8)
    g_gains = _sum_parts(recv_gains, gvec8, me, 8, "sum_gain_parts")[0:1]
    g_mats = [_set_slot(s, hf, ci).reshape((1,) + shp) for s, hf, shp in zip(swapped, halves, SHARD_SHAPES)]

    off = [0, 1024, 1408, 1664, 2688]
    g_gain = [g_gains[:, off[i]:off[i + 1]] for i in range(4)]
    grads = [g_gain[0], g_mats[0], g_gain[1], g_mats[1], g_gain[2], g_mats[2], g_mats[3], g_mats[4], g_mats[5], g_gain[3]]
    ws = [pre_norm_g, w_in, q_norm_g, w_uq, kv_norm_g, w_ukv, w_proj_mla, w_proj_dil, w_out, post_norm_g]
    ms = [m_pre_norm_g, m_w_in, m_q_norm_g, m_w_uq, m_kv_norm_g, m_w_ukv, m_w_proj_mla, m_w_proj_dil, m_w_out, m_post_norm_g]
    vs = [v_pre_norm_g, v_w_in, v_q_norm_g, v_w_uq, v_kv_norm_g, v_w_ukv, v_w_proj_mla, v_w_proj_dil, v_w_out, v_post_norm_g]
    deltas, new_m, new_v = [], [], []
    for i, (w, g, m, v) in enumerate(zip(ws, grads, ms, vs)):
        if w.shape[-1] % 128 and w.shape[-2] % 128 == 0:
            g = jnp.swapaxes(g, 1, 2)
            grads[i] = jnp.swapaxes(g, 1, 2)
            d_, m_, v_ = [jnp.swapaxes(o, 1, 2) for o in
                          _adamw(jnp.swapaxes(w, 1, 2), g, jnp.swapaxes(m, 1, 2), jnp.swapaxes(v, 1, 2), f"adamw_{i}")]
        else:
            d_, m_, v_ = _adamw(w, g, m, v, f"adamw_{i}")
        deltas.append(d_)
        new_m.append(m_)
        new_v.append(v_)
    return (loss, grad_x.reshape(x.shape), *grads, *deltas, *new_m, *new_v)
```

```python
import jax
import jax.numpy as jnp
from jax import lax
from jax.experimental import pallas as pl
from jax.experimental.pallas import tpu as pltpu

F32 = jnp.float32
BF16 = jnp.bfloat16

SEQ = 4096
D_MODEL = 1024
EPS = 1e-6
ROPE_THETA = 500000.0
MLA_HEADS = 8
Q_RANK = 384
KV_RANK = 256
MLA_SCALE = 96.0 ** -0.5
MLA_ROPE_HALF = 16
DIL_DILATIONS = (1, 4, 16)
DIL_ROPE_HALF = 8
DIL_SCALE = 0.125
BAND = 128

N_LAT = 768
COL_Z, COL_QKV, COL_LAT = 2048, 3072, 7680
N_PAD = 8448
IN_SPLITS = (384, 256, 32, 4608, 512, 512, 1024, 1024)

SHARD_SHAPES = ((1024, 2088), (384, 192), (256, 256), (512, 256), (512, 256), (256, 1024))
N_MATS = len(SHARD_SHAPES)
N_GAINS = 2688

ADAM_LR, ADAM_B1, ADAM_B2, ADAM_EPS, ADAM_WD, ADAM_STEP = 0.001, 0.9, 0.999, 1e-08, 0.01, 10

VMEM_LIMIT = 56 * 1024 * 1024
NEG = -1e30
MESH = pl.DeviceIdType.MESH


def _cparams(**kw):
    return pltpu.CompilerParams(vmem_limit_bytes=VMEM_LIMIT, **kw)


def _dot(a, b, dims):
    return lax.dot_general(a, b, (dims, ((), ())), preferred_element_type=F32)


def _nn(a, b):
    return _dot(a, b, ((1,), (0,)))


def _nt(a, b):
    return _dot(a, b, ((1,), (1,)))


def _tn(a, b):
    return _dot(a, b, ((0,), (0,)))


def _rope_lanes(shape, half, period, first):
    lane = lax.broadcasted_iota(jnp.int32, shape, len(shape) - 1) % period
    return (lane >= first) & (lane < first + half), (lane >= first + half) & (lane < first + 2 * half)


def _rope_fwd(x, c, s, half, lanes):
    x1, _ = lanes
    return x * c + jnp.where(x1, pltpu.roll(x, 128 - half, 1), pltpu.roll(x, half, 1)) * s


def _rope_bwd(g, c, s, half, lanes):
    x1, x2 = lanes
    gs = g * s
    return g * c + jnp.where(x2, pltpu.roll(gs, half, 1), jnp.where(x1, pltpu.roll(gs, 128 - half, 1), 0.0))


def _sigmoid(x):
    return 1.0 / (1.0 + jnp.exp(-x))


def _after(token):
    return ([], []) if token is None else ([token], [pl.BlockSpec(memory_space=pl.ANY)])


def _matmul(a, b, mode, out_dtype, tm, tn, tk, name, token=None, b_cols=None):
    after, after_specs = _after(token)
    if mode == "nn":
        (m, k), n = a.shape, b.shape[1]
        first = 0
        if b_cols is not None:
            first, n = b_cols[0], b_cols[1] * tn
        a_spec = pl.BlockSpec((tm, tk), lambda j, i, kk: (i, kk))
        b_spec = pl.BlockSpec((tk, tn), lambda j, i, kk: (kk, j + first))
        dot = _nn
    elif mode == "nt":
        (m, k), n = a.shape, b.shape[0]
        a_spec = pl.BlockSpec((tm, tk), lambda j, i, kk: (i, kk))
        b_spec = pl.BlockSpec((tn, tk), lambda j, i, kk: (j, kk))
        dot = _nt
    else:
        (k, m), n = a.shape, b.shape[1]
        a_spec = pl.BlockSpec((tk, tm), lambda j, i, kk: (kk, i))
        b_spec = pl.BlockSpec((tk, tn), lambda j, i, kk: (kk, j))
        dot = _tn
    assert m % tm == 0 and n % tn == 0 and k % tk == 0, (name, m, n, k, tm, tn, tk)
    nk = k // tk

    def body(a_ref, b_ref, *rest):
        o_ref, acc_ref = rest[-2:]
        kk = pl.program_id(2)
        part = dot(a_ref[...], b_ref[...])

        @pl.when(kk == 0)
        def _():
            acc_ref[...] = part

        @pl.when(kk > 0)
        def _():
            acc_ref[...] += part

        @pl.when(kk == nk - 1)
        def _():
            o_ref[...] = acc_ref[...].astype(o_ref.dtype)

    return pl.pallas_call(
        body, name=name, grid=(n // tn, m // tm, nk),
        in_specs=[a_spec, b_spec] + after_specs,
        out_specs=pl.BlockSpec((tm, tn), lambda j, i, kk: (i, j)),
        out_shape=jax.ShapeDtypeStruct((m, n), out_dtype),
        scratch_shapes=[pltpu.VMEM((tm, tn), F32)],
        compiler_params=_cparams(),
    )(a, b, *after)


def _prenorm_fwd(x, g):
    tm = 512

    def body(x_ref, g_ref, h_ref, ht_ref):
        xv = x_ref[...]
        r = lax.rsqrt(jnp.mean(xv * xv, axis=-1, keepdims=True) + EPS)
        hv = (xv * r * g_ref[...]).astype(BF16)
        h_ref[...] = hv
        ht_ref[...] = hv.T

    return pl.pallas_call(
        body, name="prenorm_fwd", grid=(SEQ // tm,),
        in_specs=[pl.BlockSpec((tm, D_MODEL), lambda i: (i, 0)), pl.BlockSpec((1, D_MODEL), lambda i: (0, 0))],
        out_specs=[pl.BlockSpec((tm, D_MODEL), lambda i: (i, 0)), pl.BlockSpec((D_MODEL, tm), lambda i: (0, i))],
        out_shape=[jax.ShapeDtypeStruct((SEQ, D_MODEL), BF16), jax.ShapeDtypeStruct((D_MODEL, SEQ), BF16)],
    )(x, g)


def _prenorm_bwd(x, dh, dy, g):
    tm = 512

    def body(x_ref, dh_ref, dy_ref, g_ref, gx_ref, dg_ref):
        xv = x_ref[...]
        r = lax.rsqrt(jnp.mean(xv * xv, axis=-1, keepdims=True) + EPS)
        n = xv * r
        dhv = dh_ref[...]
        dn = dhv * g_ref[...]
        gx_ref[...] = dy_ref[...] + r * (dn - n * jnp.mean(dn * n, axis=-1, keepdims=True))
        part = jnp.sum(dhv * n, axis=0, keepdims=True)

        @pl.when(pl.program_id(0) == 0)
        def _():
            dg_ref[...] = part

        @pl.when(pl.program_id(0) > 0)
        def _():
            dg_ref[...] += part

    row = pl.BlockSpec((tm, D_MODEL), lambda i: (i, 0))
    vec = pl.BlockSpec((1, D_MODEL), lambda i: (0, 0))
    return pl.pallas_call(
        body, name="prenorm_bwd", grid=(SEQ // tm,),
        in_specs=[row, row, row, vec], out_specs=[row, vec],
        out_shape=[jax.ShapeDtypeStruct((SEQ, D_MODEL), F32), jax.ShapeDtypeStruct((1, D_MODEL), F32)],
        compiler_params=_cparams(),
    )(x, dh, dy, g)


def _mla_prep_fwd(p, qg, kvg, wuq, wk, wv, rc, rs):
    tm = 512

    def body(lat_ref, qg_ref, kvg_ref, wuq_ref, wk_ref, wv_ref, c_ref, s_ref,
             cqn_ref, ckvn_ref, q_ref, k_ref, v_ref):
        c, s = c_ref[...], s_ref[...]
        lanes = _rope_lanes((tm, 128), MLA_ROPE_HALF, 128, 64)
        cq = lat_ref[:, 0:Q_RANK]
        r1 = lax.rsqrt(jnp.mean(cq * cq, axis=-1, keepdims=True) + EPS)
        cqn = (cq * r1 * qg_ref[...]).astype(BF16)
        cqn_ref[...] = cqn
        q = _nn(cqn, wuq_ref[...])
        for h in range(MLA_HEADS):
            sl = slice(h * 128, (h + 1) * 128)
            q_ref[:, sl] = (_rope_fwd(q[:, sl], c, s, MLA_ROPE_HALF, lanes) * MLA_SCALE).astype(BF16)
        ckv = lat_ref[:, Q_RANK:Q_RANK + KV_RANK]
        r2 = lax.rsqrt(jnp.mean(ckv * ckv, axis=-1, keepdims=True) + EPS)
        ckvn = (ckv * r2 * kvg_ref[...]).astype(BF16)
        ckvn_ref[...] = ckvn
        krr = _rope_fwd(lat_ref[:, Q_RANK + KV_RANK:N_LAT], c, s, MLA_ROPE_HALF, lanes)
        kn = _nn(ckvn, wk_ref[...])
        for h in range(MLA_HEADS):
            sl = slice(h * 128, (h + 1) * 128)
            k_ref[:, sl] = (kn[:, sl] + krr).astype(BF16)
        v_ref[...] = _nn(ckvn, wv_ref[...]).astype(BF16)

    def full(shape):
        return pl.BlockSpec(shape, lambda i: (0, 0))

    def rows(w):
        return pl.BlockSpec((tm, w), lambda i: (i, 0))

    return pl.pallas_call(
        body, name="mla_prep_fwd", grid=(SEQ // tm,),
        in_specs=[pl.BlockSpec((tm, N_LAT), lambda i: (i, COL_LAT // N_LAT)),
                  full((1, Q_RANK)), full((1, KV_RANK)), full((Q_RANK, 1024)), full((KV_RANK, 1024)),
                  full((KV_RANK, 512)), rows(128), rows(128)],
        out_specs=[rows(Q_RANK), rows(KV_RANK), rows(1024), rows(1024), rows(512)],
        out_shape=[jax.ShapeDtypeStruct((SEQ, Q_RANK), BF16), jax.ShapeDtypeStruct((SEQ, KV_RANK), BF16),
                   jax.ShapeDtypeStruct((SEQ, 1024), BF16), jax.ShapeDtypeStruct((SEQ, 1024), BF16),
                   jax.ShapeDtypeStruct((SEQ, 512), BF16)],
        compiler_params=_cparams(),
    )(p, qg, kvg, wuq, wk, wv, rc, rs)


def _mla_prep_bwd(dp_in, p, dq, dk, dv, qg, kvg, wuq, wk, wv, rc, rs):
    tm = 512

    def body(dp_any, lat_ref, dq_ref, dk_ref, dv_ref, qg_ref, kvg_ref, wuq_ref, wk_ref, wv_ref,
             c_ref, s_ref, dp_ref, dqb_ref, dkb_ref, dvb_ref, dgq_ref, dgkv_ref):
        del dp_any
        c, s = c_ref[...], s_ref[...]
        lanes = _rope_lanes((tm, 128), MLA_ROPE_HALF, 128, 64)
        lane = lax.broadcasted_iota(jnp.int32, (tm, 128), 1)
        dkr = jnp.zeros((tm, 128), F32)
        for h in range(MLA_HEADS):
            sl = slice(h * 128, (h + 1) * 128)
            dqb_ref[:, sl] = _rope_bwd(dq_ref[:, sl] * MLA_SCALE, c, s, MLA_ROPE_HALF, lanes).astype(BF16)
            dkh = dk_ref[:, sl]
            dkr = dkr + dkh
            dkb_ref[:, sl] = jnp.where(lane < 64, dkh, 0.0).astype(BF16)
        dkr = jnp.where((lane >= 64) & (lane < 96), dkr, 0.0)
        dkr = _rope_bwd(dkr, c, s, MLA_ROPE_HALF, lanes)
        dvb = dv_ref[...].astype(BF16)
        dvb_ref[...] = dvb

        cq = lat_ref[:, 0:Q_RANK]
        r1 = lax.rsqrt(jnp.mean(cq * cq, axis=-1, keepdims=True) + EPS)
        n1 = cq * r1
        dcqn = _nt(dqb_ref[...], wuq_ref[...])
        dn1 = dcqn * qg_ref[...]
        dcq = r1 * (dn1 - n1 * jnp.mean(dn1 * n1, axis=-1, keepdims=True))
        pq = jnp.sum(dcqn * n1, axis=0, keepdims=True)

        ckv = lat_ref[:, Q_RANK:Q_RANK + KV_RANK]
        r2 = lax.rsqrt(jnp.mean(ckv * ckv, axis=-1, keepdims=True) + EPS)
        n2 = ckv * r2
        dckvn = _nt(dkb_ref[...], wk_ref[...]) + _nt(dvb, wv_ref[...])
        dn2 = dckvn * kvg_ref[...]
        dckv = r2 * (dn2 - n2 * jnp.mean(dn2 * n2, axis=-1, keepdims=True))
        pkv = jnp.sum(dckvn * n2, axis=0, keepdims=True)

        dp_ref[:, 0:Q_RANK] = dcq.astype(BF16)
        dp_ref[:, Q_RANK:Q_RANK + KV_RANK] = dckv.astype(BF16)
        dp_ref[:, Q_RANK + KV_RANK:N_LAT] = dkr.astype(BF16)

        @pl.when(pl.program_id(0) == 0)
        def _():
            dgq_ref[...] = pq
            dgkv_ref[...] = pkv

        @pl.when(pl.program_id(0) > 0)
        def _():
            dgq_ref[...] += pq
            dgkv_ref[...] += pkv

    def full(shape):
        return pl.BlockSpec(shape, lambda i: (0, 0))

    def rows(w):
        return pl.BlockSpec((tm, w), lambda i: (i, 0))

    lat = pl.BlockSpec((tm, N_LAT), lambda i: (i, COL_LAT // N_LAT))
    return pl.pallas_call(
        body, name="mla_prep_bwd", grid=(SEQ // tm,),
        in_specs=[pl.BlockSpec(memory_space=pl.ANY), lat, rows(1024), rows(1024), rows(512),
                  full((1, Q_RANK)), full((1, KV_RANK)), full((Q_RANK, 1024)), full((KV_RANK, 1024)),
                  full((KV_RANK, 512)), rows(128), rows(128)],
        out_specs=[lat, rows(1024), rows(1024), rows(512), full((1, Q_RANK)), full((1, KV_RANK))],
        out_shape=[jax.ShapeDtypeStruct((SEQ, N_PAD), BF16), jax.ShapeDtypeStruct((SEQ, 1024), BF16),
                   jax.ShapeDtypeStruct((SEQ, 1024), BF16), jax.ShapeDtypeStruct((SEQ, 512), BF16),
                   jax.ShapeDtypeStruct((1, Q_RANK), F32), jax.ShapeDtypeStruct((1, KV_RANK), F32)],
        input_output_aliases={0: 0},
        compiler_params=_cparams(),
    )(dp_in, p, dq, dk, dv, qg, kvg, wuq, wk, wv, rc, rs)


FLASH_T = 1024


def _head_half(shape, hh):
    lane = lax.broadcasted_iota(jnp.int32, shape, 1)
    return (lane < 64) if hh == 0 else (lane >= 64)


def _diag_keep(nr, nk):
    row = lax.broadcasted_iota(jnp.int32, (nr, nk), 0)
    col = lax.broadcasted_iota(jnp.int32, (nr, nk), 1)
    return row + (nk - nr) >= col


def _tri_steps(nb, q_major):
    if q_major:
        pairs = [(i, kb) for i in range(nb) for kb in range(i + 1)]
    else:
        pairs = [(i, kb) for kb in range(nb) for i in range(kb, nb)]
    return jnp.asarray([p[0] for p in pairs], jnp.int32), jnp.asarray([p[1] for p in pairs], jnp.int32)


def _mla_flash_fwd(q, k, v):
    t = FLASH_T
    nb = SEQ // t
    qtab, ktab = _tri_steps(nb, True)

    def body(qi_ref, ki_ref, q_ref, k_ref, v_ref, o_ref, lse_ref, m_scr, l_scr, acc_scr):
        step = pl.program_id(1)
        i, kb = qi_ref[step], ki_ref[step]

        @pl.when(kb == 0)
        def _():
            m_scr[...] = jnp.full_like(m_scr, NEG)
            l_scr[...] = jnp.zeros_like(l_scr)
            acc_scr[...] = jnp.zeros_like(acc_scr)

        def update(r0, nr, nk, diagonal):
            rs = slice(r0, r0 + nr)
            vv = v_ref[0:nk, :]
            for hh in range(2):
                sl = slice(hh * 128, (hh + 1) * 128)
                s = _nt(q_ref[rs, sl], k_ref[0:nk, sl])
                if diagonal:
                    s = jnp.where(_diag_keep(nr, nk), s, NEG)
                m_prev = m_scr[hh, rs, :]
                m_new = jnp.maximum(m_prev, jnp.max(s, axis=-1, keepdims=True))
                pr = jnp.exp(s - jnp.tile(m_new, (1, nk // 128)))
                alpha = jnp.exp(m_prev - m_new)
                l_scr[hh, rs, :] = alpha * l_scr[hh, rs, :] + jnp.sum(pr, axis=-1, keepdims=True)
                acc_scr[hh, rs, :] = alpha * acc_scr[hh, rs, :] + _nn(pr.astype(BF16), vv)
                m_scr[hh, rs, :] = m_new

        @pl.when(kb < i)
        def _():
            update(0, t, t, False)

        @pl.when(kb == i)
        def _():
            update(0, t // 2, t // 2, True)
            update(t // 2, t // 2, t, True)
            o0 = acc_scr[0] / l_scr[0]
            o1 = acc_scr[1] / l_scr[1]
            o_ref[...] = jnp.where(_head_half((t, 128), 0), o0, o1)
            for hh in range(2):
                lse_ref[:, hh * 128:(hh + 1) * 128] = m_scr[hh] + jnp.log(l_scr[hh])

    grid_spec = pltpu.PrefetchScalarGridSpec(
        num_scalar_prefetch=2, grid=(4, qtab.shape[0]),
        in_specs=[pl.BlockSpec((t, 256), lambda j, s, qi, ki: (qi[s], j)),
                  pl.BlockSpec((t, 256), lambda j, s, qi, ki: (ki[s], j)),
                  pl.BlockSpec((t, 128), lambda j, s, qi, ki: (ki[s], j))],
        out_specs=[pl.BlockSpec((t, 128), lambda j, s, qi, ki: (qi[s], j)),
                   pl.BlockSpec((t, 256), lambda j, s, qi, ki: (qi[s], j))],
        scratch_shapes=[pltpu.VMEM((2, t, 128), F32), pltpu.VMEM((2, t, 128), F32), pltpu.VMEM((2, t, 128), F32)])
    return pl.pallas_call(
        body, name="mla_flash_fwd", grid_spec=grid_spec,
        out_shape=[jax.ShapeDtypeStruct((SEQ, 512), F32), jax.ShapeDtypeStruct((SEQ, 1024), F32)],
        compiler_params=_cparams(),
    )(qtab, ktab, q, k, v)


def _mla_flash_bwd(q, k, v, o, do, lse, token=None):
    t = FLASH_T
    nb = SEQ // t
    qtab, ktab = _tri_steps(nb, False)
    after, after_specs = _after(token)

    def body(qi_ref, ki_ref, q_ref, k_ref, v_ref, o_ref, do_ref, lse_ref, *rest):
        dq_ref, dk_ref, dv_ref, dk_scr, dv_scr = rest[-5:]
        step = pl.program_id(1)
        i, kb = qi_ref[step], ki_ref[step]

        @pl.when(step == 0)
        def _():
            dq_ref[...] = jnp.zeros_like(dq_ref)

        @pl.when(i == kb)
        def _():
            dk_scr[...] = jnp.zeros_like(dk_scr)
            dv_scr[...] = jnp.zeros_like(dv_scr)

        def update(r0, nr, nk, diagonal):
            rs = slice(r0, r0 + nr)
            vv = v_ref[0:nk, :]
            ov = o_ref[rs, :]
            dov = do_ref[rs, :]
            rows = pl.ds(pl.multiple_of(i * t + r0, t // 2), nr)
            for hh in range(2):
                sl = slice(hh * 128, (hh + 1) * 128)
                qh, kh = q_ref[rs, sl], k_ref[0:nk, sl]
                s = _nt(qh, kh)
                if diagonal:
                    s = jnp.where(_diag_keep(nr, nk), s, NEG)
                pr = jnp.exp(s - jnp.tile(lse_ref[rs, sl], (1, nk // 128)))
                dom = jnp.where(_head_half((nr, 128), hh), dov, 0.0)
                domb = dom.astype(BF16)
                dv_scr[0:nk, :] += _tn(pr.astype(BF16), domb)
                dpr = _nt(domb, vv)
                delta = jnp.sum(dom * ov, axis=-1, keepdims=True)
                ds = (pr * (dpr - delta)).astype(BF16)
                dq_ref[rows, sl] += _nn(ds, kh)
                dk_scr[hh, 0:nk, :] += _tn(ds, qh)

        @pl.when(i > kb)
        def _():
            update(0, t, t, False)

        @pl.when(i == kb)
        def _():
            update(0, t // 2, t // 2, True)
            update(t // 2, t // 2, t, True)

        @pl.when(i == nb - 1)
        def _():
            dk_ref[:, 0:128] = dk_scr[0]
            dk_ref[:, 128:256] = dk_scr[1]
            dv_ref[...] = dv_scr[...]

    qi_map = lambda j, s, qi, ki: (qi[s], j)
    ki_map = lambda j, s, qi, ki: (ki[s], j)
    grid_spec = pltpu.PrefetchScalarGridSpec(
        num_scalar_prefetch=2, grid=(4, qtab.shape[0]),
        in_specs=[pl.BlockSpec((t, 256), qi_map), pl.BlockSpec((t, 256), ki_map), pl.BlockSpec((t, 128), ki_map),
                  pl.BlockSpec((t, 128), qi_map), pl.BlockSpec((t, 128), qi_map), pl.BlockSpec((t, 256), qi_map)]
        + after_specs,
        out_specs=[pl.BlockSpec((SEQ, 256), lambda j, s, qi, ki: (0, j)), pl.BlockSpec((t, 256), ki_map),
                   pl.BlockSpec((t, 128), ki_map)],
        scratch_shapes=[pltpu.VMEM((2, t, 128), F32), pltpu.VMEM((t, 128), F32)])
    return pl.pallas_call(
        body, name="mla_flash_bwd", grid_spec=grid_spec,
        out_shape=[jax.ShapeDtypeStruct((SEQ, 1024), F32), jax.ShapeDtypeStruct((SEQ, 1024), F32),
                   jax.ShapeDtypeStruct((SEQ, 512), F32)],
        compiler_params=_cparams(),
    )(qtab, ktab, q, k, v, o, do, lse, *after)


DIL_UNROLL = 4


def _strided(start, size, d):
    return pl.ds(start, size) if d == 1 else pl.ds(start, size, stride=d)


def _dil_prep_fwd(p, rc, rs, g):
    d = DIL_DILATIONS[g]
    sub_len = SEQ // d
    ch = min(sub_len, 512)

    def body(p_ref, c_ref, s_ref, o_ref, x_scr):
        tq = pl.program_id(0)
        lanes = _rope_lanes((ch, 128), DIL_ROPE_HALF, 64, 0)
        o_ref[0, 0:BAND, :] = jnp.zeros((BAND, 128), BF16)

        @pl.when(tq < 2)
        def _():
            mult = jnp.where(tq == 0, DIL_SCALE, 1.0).astype(F32)
            for c0 in range(0, SEQ, ch):
                rows = pl.ds(c0, ch)
                x_scr[rows, :] = _rope_fwd(p_ref[rows, :], c_ref[rows, :] * mult, s_ref[rows, :] * mult, DIL_ROPE_HALF, lanes)

        def gather(src):
            for r in range(d):
                for c0 in range(0, sub_len, ch):
                    at = BAND + r * sub_len + c0
                    o_ref[0, at:at + ch, :] = src[_strided(r + c0 * d, ch, d), :].astype(BF16)

        @pl.when(tq < 2)
        def _():
            gather(x_scr)

        @pl.when(tq == 2)
        def _():
            gather(p_ref)

    tab = pl.BlockSpec((SEQ, 128), lambda tq, pr: (0, 0))
    return pl.pallas_call(
        body, name=f"dil_prep_fwd_g{g}", grid=(3, 4),
        in_specs=[pl.BlockSpec((SEQ, 128), lambda tq, pr: (0, COL_QKV // 128 + (tq * 3 + g) * 4 + pr)), tab, tab],
        out_specs=pl.BlockSpec((1, BAND + SEQ, 128), lambda tq, pr: (tq, 0, pr)),
        out_shape=jax.ShapeDtypeStruct((3, BAND + SEQ, 512), BF16),
        scratch_shapes=[pltpu.VMEM((SEQ, 128), F32)],
        compiler_params=_cparams(),
    )(p, rc, rs)


DIL_ST = 1024
DIL_NB = DIL_ST // BAND


def _band_keep(g, b, t):
    nbs = SEQ // DIL_DILATIONS[g] // BAND
    row = lax.broadcasted_iota(jnp.int32, (BAND, 2 * BAND), 0)
    col = lax.broadcasted_iota(jnp.int32, (BAND, 2 * BAND), 1)
    cur = (col >= BAND) & (row >= col - BAND)
    prev = (col < BAND) & (col >= row)
    if nbs >= DIL_NB:
        if b > 0:
            return cur | prev
        return cur | (prev & ((t * DIL_NB) % nbs != 0))
    return cur | prev if b % nbs else cur


def _dil_tok(g, b, t):
    d = DIL_DILATIONS[g]
    nbs = SEQ // d // BAND
    gb = t * DIL_NB + b
    return _strided((gb % nbs) * BAND * d + gb // nbs, BAND, d)


def _dil_attn_fwd2(qkv, g):
    def body(q_ref, k_ref, v_ref, o_ref, l_ref, s_scr, p_scr, o_scr):
        t = pl.program_id(1)
        base = t * DIL_ST
        half0 = _head_half((DIL_ST, 128), 0)
        lse_h = []
        for hh in range(2):
            half = _head_half((BAND, 128), hh)
            for b in range(DIL_NB):
                qv = q_ref[0, pl.ds(pl.multiple_of(base + (b + 1) * BAND, BAND), BAND), :]
                k2 = k_ref[0, pl.ds(pl.multiple_of(base + b * BAND, BAND), 2 * BAND), :]
                sb = _nt(jnp.where(half, qv, jnp.zeros_like(qv)), k2)
                s_scr[b * BAND:(b + 1) * BAND, :] = jnp.where(_band_keep(g, b, t), sb, NEG)
            s = s_scr[...]
            m = jnp.max(s, axis=-1, keepdims=True)
            pr = jnp.exp(s - m)
            den = jnp.sum(pr, axis=-1, keepdims=True)
            p_scr[...] = pr.astype(BF16)
            for b in range(DIL_NB):
                v2 = v_ref[0, pl.ds(pl.multiple_of(base + b * BAND, BAND), 2 * BAND), :]
                o_scr[hh, b * BAND:(b + 1) * BAND, :] = _nn(p_scr[b * BAND:(b + 1) * BAND, :], v2)
            o_scr[hh] = o_scr[hh] / den
            lse_h.append(m + jnp.log(den))
        out = jnp.where(half0, o_scr[0], o_scr[1])
        lse = jnp.where(half0, lse_h[0], lse_h[1])
        for b in range(DIL_NB):
            tok = _dil_tok(g, b, t)
            o_ref[tok, :] = out[b * BAND:(b + 1) * BAND, :]
            l_ref[tok, :] = lse[b * BAND:(b + 1) * BAND, :]

    def inp(tq):
        return pl.BlockSpec((1, BAND + SEQ, 128), lambda pr, t: (tq, 0, pr))

    out = pl.BlockSpec((SEQ, 128), lambda pr, t: (0, pr))
    return pl.pallas_call(
        body, name=f"dil_attn_fwd_g{g}", grid=(4, SEQ // DIL_ST),
        in_specs=[inp(0), inp(1), inp(2)], out_specs=[out, out],
        out_shape=[jax.ShapeDtypeStruct((SEQ, 512), F32), jax.ShapeDtypeStruct((SEQ, 512), F32)],
        scratch_shapes=[pltpu.VMEM((DIL_ST, 2 * BAND), F32), pltpu.VMEM((DIL_ST, 2 * BAND), BF16),
                        pltpu.VMEM((2, DIL_ST, 128), F32)],
        compiler_params=_cparams(),
    )(qkv, qkv, qkv)


def _dil_attn_bwd2(qkv, dyd, yd, lse_all, g, token=None):
    d = DIL_DILATIONS[g]
    sub_len = SEQ // d
    nst = SEQ // DIL_ST
    after, after_specs = _after(token)

    def body(q_ref, k_ref, v_ref, do_ref, y_ref, l_ref, *rest):
        out_ref, dk_scr, dv_scr, s_scr, dp_scr, p_scr, ds_scr, do_scr, y_scr, l_scr, dq_scr = rest[-11:]
        t = pl.program_id(1)
        base = t * DIL_ST

        @pl.when(t == 0)
        def _():
            dk_scr[...] = jnp.zeros_like(dk_scr)
            dv_scr[...] = jnp.zeros_like(dv_scr)

        for b in range(DIL_NB):
            tok = _dil_tok(g, b, t)
            do_scr[b * BAND:(b + 1) * BAND, :] = do_ref[tok, :]
            y_scr[b * BAND:(b + 1) * BAND, :] = y_ref[tok, :]
            l_scr[b * BAND:(b + 1) * BAND, :] = l_ref[tok, :]
        for hh in range(2):
            half = _head_half((BAND, 128), hh)
            half_st = _head_half((DIL_ST, 128), hh)
            dom = jnp.where(half_st, do_scr[...], 0.0)
            delta = jnp.sum(dom * y_scr[...], axis=-1, keepdims=True)
            lcol = jnp.max(jnp.where(half_st, l_scr[...], NEG), axis=-1, keepdims=True)
            for b in range(DIL_NB):
                rows = slice(b * BAND, (b + 1) * BAND)
                qv = q_ref[0, pl.ds(pl.multiple_of(base + (b + 1) * BAND, BAND), BAND), :]
                band = pl.ds(pl.multiple_of(base + b * BAND, BAND), 2 * BAND)
                sb = _nt(jnp.where(half, qv, jnp.zeros_like(qv)), k_ref[0, band, :])
                s_scr[rows, :] = jnp.where(_band_keep(g, b, t), sb, NEG)
                dp_scr[rows, :] = _nt(dom[rows, :].astype(BF16), v_ref[0, band, :])
            pr = jnp.exp(s_scr[...] - lcol)
            p_scr[...] = pr.astype(BF16)
            ds_scr[...] = (pr * (dp_scr[...] - delta)).astype(BF16)
            for b in range(DIL_NB):
                rows = slice(b * BAND, (b + 1) * BAND)
                qv = q_ref[0, pl.ds(pl.multiple_of(base + (b + 1) * BAND, BAND), BAND), :]
                band = pl.ds(pl.multiple_of(base + b * BAND, BAND), 2 * BAND)
                dqb = jnp.where(half, _nn(ds_scr[rows, :], k_ref[0, band, :]), 0.0)
                if hh == 0:
                    dq_scr[rows, :] = dqb
                else:
                    dq_scr[rows, :] += dqb
                half2 = _head_half((2 * BAND, 128), hh)
                dk_scr[band, :] += jnp.where(half2, _tn(ds_scr[rows, :], qv), 0.0)
                dv_scr[band, :] += _tn(p_scr[rows, :], dom[rows, :].astype(BF16))
        for b in range(DIL_NB):
            out_ref[pl.ds(0, 1), _dil_tok(g, b, t), :] = dq_scr[b * BAND:(b + 1) * BAND, :][None]

        @pl.when(t == nst - 1)
        def _():
            for r in range(d):
                rows = _strided(r, sub_len, d)
                out_ref[pl.ds(1, 1), rows, :] = dk_scr[BAND + r * sub_len:BAND + (r + 1) * sub_len, :][None]
                out_ref[pl.ds(2, 1), rows, :] = dv_scr[BAND + r * sub_len:BAND + (r + 1) * sub_len, :][None]

    def inp(tq):
        return pl.BlockSpec((1, BAND + SEQ, 128), lambda pr, t: (tq, 0, pr))

    tok_spec = pl.BlockSpec((SEQ, 128), lambda pr, t: (0, pr))
    st = (DIL_ST, 2 * BAND)
    return pl.pallas_call(
        body, name=f"dil_attn_bwd_g{g}", grid=(4, nst),
        in_specs=[inp(0), inp(1), inp(2), tok_spec, tok_spec, tok_spec] + after_specs,
        out_specs=pl.BlockSpec((3, SEQ, 128), lambda pr, t: (0, 0, pr)),
        out_shape=jax.ShapeDtypeStruct((3, SEQ, 512), F32),
        scratch_shapes=[pltpu.VMEM((BAND + SEQ, 128), F32), pltpu.VMEM((BAND + SEQ, 128), F32),
                        pltpu.VMEM(st, F32), pltpu.VMEM(st, F32), pltpu.VMEM(st, BF16), pltpu.VMEM(st, BF16),
                        pltpu.VMEM((DIL_ST, 128), F32), pltpu.VMEM((DIL_ST, 128), F32), pltpu.VMEM((DIL_ST, 128), F32),
                        pltpu.VMEM((DIL_ST, 128), F32)],
        compiler_params=_cparams(),
    )(qkv, qkv, qkv, dyd, yd, lse_all, *after)


def _band_masks():
    row = lax.broadcasted_iota(jnp.int32, (BAND, BAND), 0)
    col = lax.broadcasted_iota(jnp.int32, (BAND, BAND), 1)
    return row >= col, col >= row


def _dil_attn_fwd(qkv, g):
    d = DIL_DILATIONS[g]
    nbs = SEQ // d // BAND
    nblk = SEQ // BAND

    def body(q_ref, k_ref, v_ref, o_ref, l_ref):
        keep_c, keep_p = _band_masks()
        half0 = _head_half((BAND, 128), 0)

        def step(i, carry):
            cur = pl.ds(pl.multiple_of(i * BAND, BAND), BAND)
            prv = pl.ds(pl.multiple_of(jnp.maximum(i - 1, 0) * BAND, BAND), BAND)
            has_prev = (i % nbs) != 0
            qv = q_ref[0, cur, :]
            kc, kp = k_ref[0, cur, :], k_ref[0, prv, :]
            vc, vp = v_ref[0, cur, :], v_ref[0, prv, :]
            outs, lses = [], []
            for hh in range(2):
                qm = jnp.where(_head_half((BAND, 128), hh), qv, jnp.zeros_like(qv))
                sc = jnp.where(keep_c, _nt(qm, kc), NEG)
                sp = jnp.where(keep_p & has_prev, _nt(qm, kp), NEG)
                m = jnp.maximum(jnp.max(sc, axis=-1, keepdims=True), jnp.max(sp, axis=-1, keepdims=True))
                pc, pp = jnp.exp(sc - m), jnp.exp(sp - m)
                den = jnp.sum(pc, axis=-1, keepdims=True) + jnp.sum(pp, axis=-1, keepdims=True)
                o = (_nn(pc.astype(BF16), vc) + _nn(pp.astype(BF16), vp)) / den
                outs.append(o)
                lses.append(jnp.broadcast_to(m + jnp.log(den), (BAND, 128)))
            tok = _strided((i % nbs) * BAND * d + i // nbs, BAND, d)
            o_ref[tok, :] = jnp.where(half0, outs[0], outs[1])
            l_ref[tok, :] = jnp.where(half0, lses[0], lses[1])
            return carry

        lax.fori_loop(0, nblk, step, 0, unroll=DIL_UNROLL)

    def inp(tq):
        return pl.BlockSpec((1, SEQ, 128), lambda pr: (tq, 0, pr))

    out = pl.BlockSpec((SEQ, 128), lambda pr: (0, pr))
    return pl.pallas_call(
        body, name=f"dil_attn_fwd_g{g}", grid=(4,),
        in_specs=[inp(0), inp(1), inp(2)], out_specs=[out, out],
        out_shape=[jax.ShapeDtypeStruct((SEQ, 512), F32), jax.ShapeDtypeStruct((SEQ, 512), F32)],
        compiler_params=_cparams(),
    )(qkv, qkv, qkv)


def _dil_attn_bwd(qkv, dyd, yd, lse_all, g):
    d = DIL_DILATIONS[g]
    sub_len = SEQ // d
    nbs = sub_len // BAND
    nblk = SEQ // BAND

    def body(q_ref, k_ref, v_ref, do_ref, y_ref, l_ref, out_ref, dk_scr, dv_scr):
        keep_c, keep_p = _band_masks()
        dk_scr[...] = jnp.zeros_like(dk_scr)
        dv_scr[...] = jnp.zeros_like(dv_scr)

        def step(i, carry):
            cur = pl.ds(pl.multiple_of(i * BAND, BAND), BAND)
            prv = pl.ds(pl.multiple_of(jnp.maximum(i - 1, 0) * BAND, BAND), BAND)
            has_prev = (i % nbs) != 0
            tok = _strided((i % nbs) * BAND * d + i // nbs, BAND, d)
            qv = q_ref[0, cur, :]
            kc, kp = k_ref[0, cur, :], k_ref[0, prv, :]
            vc, vp = v_ref[0, cur, :], v_ref[0, prv, :]
            dov, yv, lv = do_ref[tok, :], y_ref[tok, :], l_ref[tok, :]
            dq = jnp.zeros((BAND, 128), F32)
            dkc = jnp.zeros((BAND, 128), F32)
            dkp = jnp.zeros((BAND, 128), F32)
            dvc = jnp.zeros((BAND, 128), F32)
            dvp = jnp.zeros((BAND, 128), F32)
            for hh in range(2):
                half = _head_half((BAND, 128), hh)
                qm = jnp.where(half, qv, jnp.zeros_like(qv))
                lcol = jnp.max(jnp.where(half, lv, NEG), axis=-1, keepdims=True)
                pc = jnp.exp(jnp.where(keep_c, _nt(qm, kc), NEG) - lcol)
                pp = jnp.exp(jnp.where(keep_p & has_prev, _nt(qm, kp), NEG) - lcol)
                dom = jnp.where(half, dov, 0.0)
                domb = dom.astype(BF16)
                delta = jnp.sum(dom * yv, axis=-1, keepdims=True)
                dsc = (pc * (_nt(domb, vc) - delta)).astype(BF16)
                dsp = (pp * (_nt(domb, vp) - delta)).astype(BF16)
                dvc = dvc + _tn(pc.astype(BF16), domb)
                dvp = dvp + _tn(pp.astype(BF16), domb)
                dq = dq + jnp.where(half, _nn(dsc, kc) + _nn(dsp, kp), 0.0)
                dkc = dkc + jnp.where(half, _tn(dsc, qv), 0.0)
                dkp = dkp + jnp.where(half, _tn(dsp, qv), 0.0)
            out_ref[pl.ds(0, 1), tok, :] = dq[None]
            dk_scr[cur, :] += dkc
            dk_scr[prv, :] += dkp
            dv_scr[cur, :] += dvc
            dv_scr[prv, :] += dvp
            return carry

        lax.fori_loop(0, nblk, step, 0, unroll=DIL_UNROLL)
        for r in range(d):
            rows = _strided(r, sub_len, d)
            out_ref[pl.ds(1, 1), rows, :] = dk_scr[r * sub_len:(r + 1) * sub_len, :][None]
            out_ref[pl.ds(2, 1), rows, :] = dv_scr[r * sub_len:(r + 1) * sub_len, :][None]

    def inp(tq):
        return pl.BlockSpec((1, SEQ, 128), lambda pr: (tq, 0, pr))

    tok_spec = pl.BlockSpec((SEQ, 128), lambda pr: (0, pr))
    return pl.pallas_call(
        body, name=f"dil_attn_bwd_g{g}", grid=(4,),
        in_specs=[inp(0), inp(1), inp(2), tok_spec, tok_spec, tok_spec],
        out_specs=pl.BlockSpec((3, SEQ, 128), lambda pr: (0, 0, pr)),
        out_shape=jax.ShapeDtypeStruct((3, SEQ, 512), F32),
        scratch_shapes=[pltpu.VMEM((SEQ, 128), F32), pltpu.VMEM((SEQ, 128), F32)],
        compiler_params=_cparams(),
    )(qkv, qkv, qkv, dyd, yd, lse_all)


def _dil_prep_bwd(dp_in, dqkv, rc, rs, g):
    tm = 1024

    def body(dp_any, g_ref, c_ref, s_ref, dp_ref):
        del dp_any
        tq = pl.program_id(0)

        @pl.when(tq == 2)
        def _():
            dp_ref[...] = g_ref[0].astype(BF16)

        @pl.when(tq < 2)
        def _():
            mult = jnp.where(tq == 0, DIL_SCALE, 1.0).astype(F32)
            lanes = _rope_lanes((tm, 128), DIL_ROPE_HALF, 64, 0)
            cv, sv = c_ref[...], s_ref[...] * mult
            cv = cv * mult
            for pr in range(4):
                gv = g_ref[0, :, pr * 128:(pr + 1) * 128]
                dp_ref[:, pr * 128:(pr + 1) * 128] = _rope_bwd(gv, cv, sv, DIL_ROPE_HALF, lanes).astype(BF16)

    tab = pl.BlockSpec((tm, 128), lambda tq, i: (i, 0))
    return pl.pallas_call(
        body, name=f"dil_prep_bwd_g{g}", grid=(3, SEQ // tm),
        in_specs=[pl.BlockSpec(memory_space=pl.ANY),
                  pl.BlockSpec((1, tm, 512), lambda tq, i: (tq, i, 0)), tab, tab],
        out_specs=pl.BlockSpec((tm, 512), lambda tq, i: (i, COL_QKV // 512 + tq * 3 + g)),
        out_shape=jax.ShapeDtypeStruct((SEQ, N_PAD), BF16),
        input_output_aliases={0: 0},
    )(dp_in, dqkv, rc, rs)


TAIL_T = 256


def _tail(p, ya, o_g, l_g, x, target, wpm, wpd, wout, post_g):
    tm = TAIL_T

    def body(pgz_ref, ya_ref, o0_ref, o1_ref, o2_ref, l0_ref, l1_ref, l2_ref, x_ref, t_ref,
             wpm_ref, wpd_ref, wout_ref, pg_ref,
             dp_ref, dy_ref, mg_ref, dt_ref, ua_ref, dpa_ref, ud_ref, dpd_ref, dya_ref, dyd_ref,
             yd_ref, lse_ref, loss_ref, dgp_ref):
        l0, l1, l2 = l0_ref[...], l1_ref[...], l2_ref[...]
        mx = jnp.maximum(jnp.maximum(l0, l1), l2)
        e0, e1, e2 = jnp.exp(l0 - mx), jnp.exp(l1 - mx), jnp.exp(l2 - mx)
        den = e0 + e1 + e2
        yd = (e0 * o0_ref[...] + e1 * o1_ref[...] + e2 * o2_ref[...]) / den
        yd_ref[...] = yd
        lse_ref[...] = mx + jnp.log(den)
        ya = ya_ref[...]

        gm, gd = pgz_ref[:, 0:1024], pgz_ref[:, 1024:2048]
        zm, zd = pgz_ref[:, 2048:2560], pgz_ref[:, 2560:3072]
        szm, szd = _sigmoid(zm), _sigmoid(zd)
        sm, sd = zm * szm, zd * szd
        ua = (ya * sm).astype(BF16)
        ud = (yd * sd).astype(BF16)
        ua_ref[...] = ua
        ud_ref[...] = ud
        pa = _nn(ua, wpm_ref[...])
        pd = _nn(ud, wpd_ref[...])
        sgm, sgd = _sigmoid(gm), _sigmoid(gd)
        mg = (sgm * pa + sgd * pd).astype(BF16)
        mg_ref[...] = mg
        t = _nn(mg, wout_ref[...])
        r3 = lax.rsqrt(jnp.mean(t * t, axis=-1, keepdims=True) + EPS)
        n = t * r3
        pg = pg_ref[...]
        err = x_ref[...] + n * pg - t_ref[...]
        lpart = jnp.sum(err * err, axis=0, keepdims=True)

        dy = err * (1.0 / D_MODEL)
        dy_ref[...] = dy
        gpart = jnp.sum(dy * n, axis=0, keepdims=True)
        dn = dy * pg
        dt = (r3 * (dn - n * jnp.mean(dn * n, axis=-1, keepdims=True))).astype(BF16)
        dt_ref[...] = dt
        dmg = _nt(dt, wout_ref[...])
        dpa = (dmg * sgm).astype(BF16)
        dpd = (dmg * sgd).astype(BF16)
        dpa_ref[...] = dpa
        dpd_ref[...] = dpd
        dp_ref[:, 0:1024] = (dmg * pa * sgm * (1.0 - sgm)).astype(BF16)
        dp_ref[:, 1024:2048] = (dmg * pd * sgd * (1.0 - sgd)).astype(BF16)
        dua = _nt(dpa, wpm_ref[...])
        dud = _nt(dpd, wpd_ref[...])
        dya_ref[...] = dua * sm
        dyd_ref[...] = dud * sd
        dp_ref[:, 2048:2560] = (dua * ya * szm * (1.0 + zm * (1.0 - szm))).astype(BF16)
        dp_ref[:, 2560:3072] = (dud * yd * szd * (1.0 + zd * (1.0 - szd))).astype(BF16)

        @pl.when(pl.program_id(0) == 0)
        def _():
            loss_ref[...] = lpart
            dgp_ref[...] = gpart

        @pl.when(pl.program_id(0) > 0)
        def _():
            loss_ref[...] += lpart
            dgp_ref[...] += gpart

    def rows(w):
        return pl.BlockSpec((tm, w), lambda i: (i, 0))

    def full(shape):
        return pl.BlockSpec(shape, lambda i: (0, 0))

    def sds(w, dt):
        return jax.ShapeDtypeStruct((SEQ, w), dt)

    return pl.pallas_call(
        body, name="tail", grid=(SEQ // tm,),
        in_specs=[rows(3072), rows(512), rows(512), rows(512), rows(512), rows(512), rows(512), rows(512),
                  rows(1024), rows(1024), full((512, 1024)), full((512, 1024)), full((1024, 1024)), full((1, 1024))],
        out_specs=[rows(3072), rows(1024), rows(1024), rows(1024), rows(512), rows(1024), rows(512), rows(1024),
                   rows(512), rows(512), rows(512), rows(512), full((1, 1024)), full((1, 1024))],
        out_shape=[sds(N_PAD, BF16), sds(1024, F32), sds(1024, BF16), sds(1024, BF16), sds(512, BF16),
                   sds(1024, BF16), sds(512, BF16), sds(1024, BF16), sds(512, F32), sds(512, F32),
                   sds(512, F32), sds(512, F32),
                   jax.ShapeDtypeStruct((1, 1024), F32), jax.ShapeDtypeStruct((1, 1024), F32)],
        compiler_params=_cparams(),
    )(p, ya, o_g[0], o_g[1], o_g[2], l_g[0], l_g[1], l_g[2], x, target, wpm, wpd, wout, post_g)


def _sum_parts(recv, own, me, tr, name):
    n, r, w = recv.shape
    own_spec = (pl.BlockSpec((tr, w), lambda i, me_ref: (i, 0)) if own.ndim == 2
                else pl.BlockSpec((None, tr, w), lambda i, me_ref: (me_ref[0], i, 0)))

    def body(me_ref, p_ref, own_ref, o_ref):
        mine = own_ref[...].astype(F32)
        acc = jnp.zeros((tr, w), F32)
        for s in range(n):
            acc = acc + jnp.where(me_ref[0] == s, mine, p_ref[s].astype(F32))
        o_ref[...] = acc

    return pl.pallas_call(
        body, name=name,
        grid_spec=pltpu.PrefetchScalarGridSpec(
            num_scalar_prefetch=1, grid=(r // tr,),
            in_specs=[pl.BlockSpec((n, tr, w), lambda i, me_ref: (0, i, 0)), own_spec],
            out_specs=pl.BlockSpec((tr, w), lambda i, me_ref: (i, 0))),
        out_shape=jax.ShapeDtypeStruct((r, w), F32),
    )(me.reshape(1), recv, own)


def _adamw(w, g, m, v, name):
    lead = w.shape[:-2]
    r, c = w.shape[-2:]
    tr = max([t for t in range(8, 257, 8) if r % t == 0], default=r)
    c1 = 1.0 - ADAM_B1 ** ADAM_STEP
    c2 = 1.0 - ADAM_B2 ** ADAM_STEP

    def body(w_ref, g_ref, m_ref, v_ref, d_ref, nm_ref, nv_ref):
        gv = g_ref[...]
        nm = ADAM_B1 * m_ref[...] + (1.0 - ADAM_B1) * gv
        nv = ADAM_B2 * v_ref[...] + (1.0 - ADAM_B2) * (gv * gv)
        nm_ref[...] = nm
        nv_ref[...] = nv
        d_ref[...] = -ADAM_LR * ((nm / c1) / (jnp.sqrt(nv / c2) + ADAM_EPS) + ADAM_WD * w_ref[...])

    zeros = (0,) * len(lead)
    spec = pl.BlockSpec((1,) * len(lead) + (tr, c), lambda i: zeros + (i, 0))
    sd = jax.ShapeDtypeStruct(w.shape, F32)
    return pl.pallas_call(
        body, name=name, grid=(r // tr,),
        in_specs=[spec] * 4, out_specs=[spec] * 3, out_shape=[sd] * 3,
    )(w, g, m, v)


ANY = pl.BlockSpec(memory_space=pl.ANY)


def _my_place():
    return lax.axis_index("x"), lax.axis_index("y"), lax.axis_index("c")


def _allgather_weights(mats):
    def body(*refs):
        w_refs, out_refs = refs[:N_MATS], refs[N_MATS:2 * N_MATS]
        send_sems, recv_sems = refs[2 * N_MATS:]
        x, y, c = _my_place()
        sibling = (x, y, 1 - c)
        chips = [(1 - x, y), (x, 1 - y), (1 - x, 1 - y)]

        def copy(k, src, dst, to):
            return pltpu.make_async_remote_copy(src_ref=src, dst_ref=dst, send_sem=send_sems.at[k],
                                                recv_sem=recv_sems.at[k], device_id=to, device_id_type=MESH)

        def half(mi, shard, hc):
            hr = SHARD_SHAPES[mi][0] // 2
            return out_refs[mi].at[shard, pl.ds(pl.multiple_of(hc * hr, 16), hr), :]

        started = []
        for mi in range(N_MATS):
            hr = SHARD_SHAPES[mi][0] // 2
            my_half = w_refs[mi].at[pl.ds(pl.multiple_of(c * hr, 16), hr), :]
            for j, (cx, cy) in enumerate(chips):
                cp = copy(mi * 6 + j, my_half, half(mi, 2 * x + y, c), (cx, cy, c))
                cp.start()
                started.append(cp)
        for mi in range(N_MATS):
            for j, (cx, cy) in enumerate(chips):
                landed = half(mi, 2 * cx + cy, c)
                copy(mi * 6 + j, landed, landed, (cx, cy, c)).wait_recv()
                fw = copy(mi * 6 + 3 + j, landed, landed, sibling)
                fw.start()
                started.append(fw)
        for mi in range(N_MATS):
            for j, (cx, cy) in enumerate(chips):
                other = half(mi, 2 * cx + cy, 1 - c)
                copy(mi * 6 + 3 + j, other, other, sibling).wait_recv()
        for cp in started:
            cp.wait_send()

    return pl.pallas_call(
        body, name="allgather_weights",
        in_specs=[ANY] * N_MATS, out_specs=[ANY] * N_MATS,
        out_shape=[jax.ShapeDtypeStruct((4, r, c), BF16) for r, c in SHARD_SHAPES],
        scratch_shapes=[pltpu.SemaphoreType.DMA((6 * N_MATS,)), pltpu.SemaphoreType.DMA((6 * N_MATS,))],
    )(*mats)


HBM = pl.BlockSpec(memory_space=pltpu.HBM)
SEM = pl.BlockSpec(memory_space=pltpu.SEMAPHORE)
DATAFLOW = pltpu.SideEffectType.DATAFLOW_SIDE_EFFECTING


def _peers(x, y, c):
    out = []
    for k in range(1, 8):
        px, py, pc = x ^ (k >> 2), y ^ ((k >> 1) & 1), c ^ (k & 1)
        out.append((k - 1, (px, py, pc), 4 * px + 2 * py + pc))
    return out


def _exchange_start(parts, name):
    n = len(parts)

    def body(*refs):
        p_refs, land_refs = refs[:n], refs[n:2 * n]
        send_sems, recv_sems, token = refs[2 * n], refs[2 * n + 1], refs[-1]
        x, y, c = _my_place()
        me = 4 * x + 2 * y + c
        for k, dev, peer in _peers(x, y, c):
            for mi in range(n):
                pltpu.make_async_remote_copy(
                    src_ref=p_refs[mi].at[peer], dst_ref=land_refs[mi].at[me], send_sem=send_sems.at[k * n + mi],
                    recv_sem=recv_sems.at[k * n + mi], device_id=dev, device_id_type=MESH).start()
        token[...] = jnp.zeros_like(token)

    hbm = [pltpu.HBM(p.shape, p.dtype) for p in parts]
    outs = pl.pallas_call(
        body, name=name + "_start",
        out_shape=(pltpu.SemaphoreType.DMA((7 * n,)), pltpu.SemaphoreType.DMA((7 * n,)), *hbm, *hbm,
                   jax.ShapeDtypeStruct((8, 128), F32)),
        in_specs=[HBM] * (2 * n), out_specs=(SEM, SEM, *[HBM] * (2 * n), pl.BlockSpec(memory_space=pltpu.VMEM)),
        input_output_aliases={i: 2 + i for i in range(2 * n)},
        compiler_params=pltpu.CompilerParams(has_side_effects=DATAFLOW),
    )(*[pltpu.with_memory_space_constraint(p, pltpu.HBM) for p in parts],
      *[pltpu.with_memory_space_constraint(lax.empty(p.shape, p.dtype), pltpu.HBM) for p in parts])
    return (name, outs[:-1]), outs[-1]


def _exchange_wait(handle, after):
    name, outs = handle
    n = (len(outs) - 2) // 2

    def body(*refs):
        p_refs, land_refs = refs[:n], refs[n:2 * n]
        send_sems, recv_sems = refs[2 * n], refs[2 * n + 1]
        x, y, c = _my_place()
        me = 4 * x + 2 * y + c
        for k, dev, peer in _peers(x, y, c):
            for mi in range(n):
                pltpu.make_async_remote_copy(
                    src_ref=p_refs[mi].at[peer], dst_ref=land_refs[mi].at[me], send_sem=send_sems.at[k * n + mi],
                    recv_sem=recv_sems.at[k * n + mi], device_id=dev, device_id_type=MESH).wait_send()
                slot = land_refs[mi].at[peer]
                pltpu.make_async_remote_copy(
                    src_ref=slot, dst_ref=slot, send_sem=send_sems.at[k * n + mi],
                    recv_sem=recv_sems.at[k * n + mi], device_id=dev, device_id_type=MESH).wait_recv()

    bufs = outs[2:]
    res = pl.pallas_call(
        body, name=name + "_wait", out_shape=tuple(pltpu.HBM(b.shape, b.dtype) for b in bufs),
        in_specs=[HBM] * (2 * n) + [SEM, SEM, ANY], out_specs=tuple([HBM] * (2 * n)),
        input_output_aliases={i: i for i in range(2 * n)},
        compiler_params=pltpu.CompilerParams(has_side_effects=DATAFLOW),
    )(*bufs, outs[0], outs[1], after)
    return list(res[n:])


def _swap_halves(halves, gvec):
    def body(*refs):
        g_refs, gv_ref = refs[:N_MATS], refs[N_MATS]
        out_refs, rg_ref = refs[N_MATS + 1:2 * N_MATS + 1], refs[2 * N_MATS + 1]
        send_sems, recv_sems = refs[2 * N_MATS + 2:]
        x, y, c = _my_place()
        me = 4 * x + 2 * y + c
        sends = []
        for mi in range(N_MATS):
            cp = pltpu.make_async_remote_copy(src_ref=g_refs[mi], dst_ref=out_refs[mi].at[c], send_sem=send_sems.at[mi],
                                              recv_sem=recv_sems.at[mi], device_id=(x, y, 1 - c), device_id_type=MESH)
            cp.start()
            sends.append(cp)
        for k, dev, peer in _peers(x, y, c):
            cp = pltpu.make_async_remote_copy(src_ref=gv_ref, dst_ref=rg_ref.at[me], send_sem=send_sems.at[N_MATS + k],
                                              recv_sem=recv_sems.at[N_MATS + k], device_id=dev, device_id_type=MESH)
            cp.start()
            sends.append(cp)
        for mi in range(N_MATS):
            got = out_refs[mi].at[1 - c]
            pltpu.make_async_remote_copy(src_ref=got, dst_ref=got, send_sem=send_sems.at[mi], recv_sem=recv_sems.at[mi],
                                         device_id=(x, y, 1 - c), device_id_type=MESH).wait_recv()
        for k, dev, peer in _peers(x, y, c):
            got = rg_ref.at[peer]
            pltpu.make_async_remote_copy(src_ref=got, dst_ref=got, send_sem=send_sems.at[N_MATS + k],
                                         recv_sem=recv_sems.at[N_MATS + k], device_id=dev, device_id_type=MESH).wait_recv()
        for cp in sends:
            cp.wait_send()

    outs = pl.pallas_call(
        body, name="swap_halves",
        in_specs=[ANY] * (N_MATS + 1), out_specs=[ANY] * (N_MATS + 1),
        out_shape=[jax.ShapeDtypeStruct((2, r // 2, c), F32) for r, c in SHARD_SHAPES]
        + [jax.ShapeDtypeStruct((8, 8, N_GAINS), F32)],
        scratch_shapes=[pltpu.SemaphoreType.DMA((N_MATS + 7,)), pltpu.SemaphoreType.DMA((N_MATS + 7,))],
    )(*halves, gvec)
    return outs[:N_MATS], outs[N_MATS]


def _set_slot(arr, block, idx):
    return lax.dynamic_update_slice(arr, block[None], (idx,) + (0,) * block.ndim)


PAD_RUNS = ((6304, 8352, 0), (5280, 6304, COL_Z), (672, 5280, COL_QKV), (0, 640, COL_LAT), (640, 672, COL_LAT + 704))
W_IN_SHARD = 2088


def _full_weights(gathered):
    def cols(a):
        return jnp.concatenate([a[s] for s in range(4)], axis=1)

    w_uq, w_ukv, w_pm, w_pd = [cols(a) for a in gathered[1:5]]
    w_out = gathered[5].reshape(D_MODEL, D_MODEL)
    g_in = gathered[0]
    pieces, at = [], 0
    for lo, hi, pad_lo in sorted(PAD_RUNS, key=lambda t: t[2]):
        if pad_lo > at:
            pieces.append(jnp.zeros((D_MODEL, pad_lo - at), g_in.dtype))
        for s in range(4):
            a_, b_ = max(lo, s * W_IN_SHARD), min(hi, (s + 1) * W_IN_SHARD)
            if a_ < b_:
                pieces.append(g_in[s][:, a_ - s * W_IN_SHARD:b_ - s * W_IN_SHARD])
        at = pad_lo + hi - lo
    pieces.append(jnp.zeros((D_MODEL, N_PAD - at), g_in.dtype))
    w_pad = jnp.concatenate(pieces, axis=1)
    z32 = jnp.zeros((Q_RANK, 32), w_uq.dtype)
    wuq_pad = jnp.concatenate([t for h in range(MLA_HEADS) for t in (w_uq[:, h * 96:(h + 1) * 96], z32)], axis=1)
    z64 = jnp.zeros((KV_RANK, 64), w_ukv.dtype)
    wk_pad = jnp.concatenate([t for h in range(MLA_HEADS) for t in (w_ukv[:, h * 128:h * 128 + 64], z64)], axis=1)
    wv = jnp.concatenate([w_ukv[:, h * 128 + 64:(h + 1) * 128] for h in range(MLA_HEADS)], axis=1)
    return w_pad, wuq_pad, wk_pad, wv, w_pm, w_pd, w_out


W_IN_LAT = 672


def _grad_parts_in_early(dw_early):
    def in_block(s, h):
        rows = slice(h * 512, (h + 1) * 512)
        out = []
        for lo, hi, pad_lo in sorted(PAD_RUNS):
            a_, b_ = max(lo, s * W_IN_SHARD), min(hi, (s + 1) * W_IN_SHARD)
            if a_ < b_:
                out.append(jnp.zeros((512, b_ - a_), dw_early.dtype) if pad_lo >= COL_LAT
                           else dw_early[rows, pad_lo + a_ - lo:pad_lo + b_ - lo])
        return jnp.concatenate(out, axis=1)

    return jnp.stack([in_block(s, h) for s in range(4) for h in range(2)])


def _grad_parts_in_late(dw_late):
    cols = jnp.concatenate([dw_late[:, 0:640], dw_late[:, 704:736]], axis=1)
    zero = jnp.zeros((512, W_IN_LAT), dw_late.dtype)
    return jnp.stack([cols[0:512], cols[512:1024]] + [zero] * 6)


def _col_blocks(m):
    r, c = m.shape[0] // 2, m.shape[1] // 4
    return jnp.stack([m[h * r:(h + 1) * r, s * c:(s + 1) * c] for s in range(4) for h in range(2)])


def _grad_parts_mla(dwuq_pad, dwk_pad, dwv):
    d_uq = jnp.concatenate([dwuq_pad[:, h * 128:h * 128 + 96] for h in range(MLA_HEADS)], axis=1)
    d_ukv = jnp.concatenate([t for h in range(MLA_HEADS) for t in (dwk_pad[:, h * 128:h * 128 + 64], dwv[:, h * 64:(h + 1) * 64])],
                            axis=1)
    return [_col_blocks(d_uq), _col_blocks(d_ukv)]


def _rope_tables(positions):
    pos = positions.reshape(SEQ).astype(F32)
    lane = jnp.arange(128)

    def table(rot, first, period):
        inv = ROPE_THETA ** (-jnp.arange(0, rot, 2, dtype=F32) / rot)
        half = rot // 2
        off = lane % period - first
        in1, in2 = (off >= 0) & (off < half), (off >= half) & (off < rot)
        inv_lane = jnp.where(in1 | in2, inv[jnp.clip(off % half, 0, half - 1)], 0.0)
        sign = jnp.where(in1, -1.0, 1.0).astype(F32)
        ang = pos[:, None] * inv_lane[None, :]
        return jnp.cos(ang), jnp.sin(ang) * sign[None, :]

    return table(32, 64, 128), table(16, 0, 64)


def _device_grads(x, positions, target, gains, gathered, send):
    pre_g, q_g, kv_g, post_g = gains
    w_pad, wuq_pad, wk_pad, wv, w_pm, w_pd, w_out = _full_weights(gathered)
    (mc, ms), (dc, ds) = _rope_tables(positions)

    h, h_t = _prenorm_fwd(x, pre_g)
    p = _matmul(h, w_pad, "nn", F32, 1024, 1408, 1024, "in_proj")
    cqn, ckvn, q, k, v = _mla_prep_fwd(p, q_g, kv_g, wuq_pad, wk_pad, wv, mc, ms)
    ya, lse_m = _mla_flash_fwd(q, k, v)
    qkv = [_dil_prep_fwd(p, dc, ds, g) for g in range(3)]
    o_g, l_g = zip(*[_dil_attn_fwd2(qkv[g], g) for g in range(3)])
    (dp, dy, mg, dt, ua, dpa, ud, dpd, dya, dyd, yd, lse_d, loss_cols, dg_post) = _tail(
        p, ya, o_g, l_g, x, target, w_pm, w_pd, w_out, post_g)

    for g in range(3):
        dqkv = _dil_attn_bwd2(qkv[g], dyd, yd, lse_d, g)
        dp = _dil_prep_bwd(dp, dqkv, dc, ds, g)
    dw_early = _matmul(h_t, dp, "nn", BF16, 1024, 1536, 2048, "dw_in_early", b_cols=(0, COL_LAT // 1536))
    dwpm = _matmul(ua, dpa, "tn", BF16, 512, 1024, 512, "dw_proj_mla")
    dwpd = _matmul(ud, dpd, "tn", BF16, 512, 1024, 512, "dw_proj_dil")
    dwout = _matmul(mg, dt, "tn", BF16, 1024, 1024, 512, "dw_out")
    early, token = send([_grad_parts_in_early(dw_early), _col_blocks(dwpm), _col_blocks(dwpd),
                         dwout.reshape(8, 128, D_MODEL)], "exchange_early")

    dq, dk, dv = _mla_flash_bwd(q, k, v, ya, dya, lse_m, token)
    dp, dqb, dkb, dvb, dg_q, dg_kv = _mla_prep_bwd(dp, p, dq, dk, dv, q_g, kv_g, wuq_pad, wk_pad, wv, mc, ms)
    dwuq_pad = _matmul(cqn, dqb, "tn", BF16, Q_RANK, 1024, 512, "dw_uq")
    dwk_pad = _matmul(ckvn, dkb, "tn", BF16, KV_RANK, 1024, 512, "dw_k")
    dwv = _matmul(ckvn, dvb, "tn", BF16, KV_RANK, 512, 512, "dw_v")
    dw_late = _matmul(h_t, dp, "nn", BF16, 1024, N_LAT, 2048, "dw_in_late", b_cols=(COL_LAT // N_LAT, 1))
    late, token = send([_grad_parts_in_late(dw_late)] + _grad_parts_mla(dwuq_pad, dwk_pad, dwv), "exchange_late")

    dh = _matmul(dp, w_pad, "nt", F32, 1024, 1024, 1408, "dh", token)
    grad_x, dg_pre = _prenorm_bwd(x, dh, dy, pre_g)

    gvec = jnp.concatenate([dg_pre, dg_q, dg_kv, dg_post], axis=1)
    return loss_cols, grad_x, (early, late), gvec


def kernel(x, positions, pre_norm_g, w_in, q_norm_g, w_uq, kv_norm_g, w_ukv, w_proj_mla, w_proj_dil, w_out, post_norm_g, loss_target, m_pre_norm_g, m_w_in, m_q_norm_g, m_w_uq, m_kv_norm_g, m_w_ukv, m_w_proj_mla, m_w_proj_dil, m_w_out, m_post_norm_g, v_pre_norm_g, v_w_in, v_q_norm_g, v_w_uq, v_kv_norm_g, v_w_ukv, v_w_proj_mla, v_w_proj_dil, v_w_out, v_post_norm_g):
    xi, yi, ci = _my_place()
    chip, me = 2 * xi + yi, 4 * xi + 2 * yi + ci
    mats = [w.reshape(w.shape[1:]).astype(BF16) for w in (w_in, w_uq, w_ukv, w_proj_mla, w_proj_dil, w_out)]
    gathered = [_set_slot(g, m, chip) for g, m in zip(_allgather_weights(mats), mats)]
    gains = (pre_norm_g, q_norm_g, kv_norm_g, post_norm_g)
    sent = {}

    def send(blocks, name):
        sent[name] = blocks
        return _exchange_start(blocks, name)

    loss_cols, grad_x, (early, late), gvec = _device_grads(x[0], positions, loss_target[0], gains, gathered, send)

    loss = lax.psum(jnp.sum(loss_cols) * (0.5 / D_MODEL), ("x", "y", "c"))

    recv_e, recv_l = _exchange_wait(early, grad_x), _exchange_wait(late, grad_x)
    own_e, own_l = sent["exchange_early"], sent["exchange_late"]
    in_e = _sum_parts(recv_e[0], own_e[0], me, 64, "sum_grad_in_early")
    in_l = _sum_parts(recv_l[0], own_l[0], me, 64, "sum_grad_in_late")
    half_in = jnp.concatenate([in_e[:, :W_IN_LAT] + jnp.where(chip == 0, in_l, 0.0), in_e[:, W_IN_LAT:]], axis=1)
    halves = [half_in,
              _sum_parts(recv_l[1], own_l[1], me, 64, "sum_grad_uq"), _sum_parts(recv_l[2], own_l[2], me, 64, "sum_grad_ukv"),
              _sum_parts(recv_e[1], own_e[1], me, 64, "sum_grad_pm"), _sum_parts(recv_e[2], own_e[2], me, 64, "sum_grad_pd"),
              _sum_parts(recv_e[3], own_e[3], me, 64, "sum_grad_out")]
    gvec8 = jnp.pad(gvec, ((0, 7), (0, 0)))
    swapped, recv_gains = _swap_halves(halves, gvec8)
    g_gains = _sum_parts(recv_gains, gvec8, me, 8, "sum_gain_parts")[0:1]
    g_mats = [_set_slot(s, hf, ci).reshape((1,) + shp) for s, hf, shp in zip(swapped, halves, SHARD_SHAPES)]

    off = [0, 1024, 1408, 1664, 2688]
    g_gain = [g_gains[:, off[i]:off[i + 1]] for i in range(4)]
    grads = [g_gain[0], g_mats[0], g_gain[1], g_mats[1], g_gain[2], g_mats[2], g_mats[3], g_mats[4], g_mats[5], g_gain[3]]
    ws = [pre_norm_g, w_in, q_norm_g, w_uq, kv_norm_g, w_ukv, w_proj_mla, w_proj_dil, w_out, post_norm_g]
    ms = [m_pre_norm_g, m_w_in, m_q_norm_g, m_w_uq, m_kv_norm_g, m_w_ukv, m_w_proj_mla, m_w_proj_dil, m_w_out, m_post_norm_g]
    vs = [v_pre_norm_g, v_w_in, v_q_norm_g, v_w_uq, v_kv_norm_g, v_w_ukv, v_w_proj_mla, v_w_proj_dil, v_w_out, v_post_norm_g]
    deltas, new_m, new_v = [], [], []
    for i, (w, g, m, v) in enumerate(zip(ws, grads, ms, vs)):
        if w.shape[-1] % 128 and w.shape[-2] % 128 == 0:
            g = jnp.swapaxes(g, 1, 2)
            grads[i] = jnp.swapaxes(g, 1, 2)
            d_, m_, v_ = [jnp.swapaxes(o, 1, 2) for o in
                          _adamw(jnp.swapaxes(w, 1, 2), g, jnp.swapaxes(m, 1, 2), jnp.swapaxes(v, 1, 2), f"adamw_{i}")]
        else:
            d_, m_, v_ = _adamw(w, g, m, v, f"adamw_{i}")
        deltas.append(d_)
        new_m.append(m_)
        new_v.append(v_)
    return (loss, grad_x.reshape(x.shape), *grads, *deltas, *new_m, *new_v)
```

```python
import jax
import jax.numpy as jnp
from jax import lax
from jax.experimental import pallas as pl
from jax.experimental.pallas import tpu as pltpu

F32 = jnp.float32
BF16 = jnp.bfloat16

SEQ = 4096
D_MODEL = 1024
EPS = 1e-6
ROPE_THETA = 500000.0
MLA_HEADS = 8
Q_RANK = 384
KV_RANK = 256
MLA_SCALE = 96.0 ** -0.5
MLA_ROPE_HALF = 16
DIL_DILATIONS = (1, 4, 16)
DIL_ROPE_HALF = 8
DIL_SCALE = 0.125
BAND = 128

N_LAT = 768
COL_Z, COL_QKV, COL_LAT = 2048, 3072, 7680
N_PAD = 8448
IN_SPLITS = (384, 256, 32, 4608, 512, 512, 1024, 1024)

SHARD_SHAPES = ((1024, 2088), (384, 192), (256, 256), (512, 256), (512, 256), (256, 1024))
N_MATS = len(SHARD_SHAPES)
N_GAINS = 2688

ADAM_LR, ADAM_B1, ADAM_B2, ADAM_EPS, ADAM_WD, ADAM_STEP = 0.001, 0.9, 0.999, 1e-08, 0.01, 10

VMEM_LIMIT = 56 * 1024 * 1024
NEG = -1e30
MESH = pl.DeviceIdType.MESH


def _cparams(**kw):
    return pltpu.CompilerParams(vmem_limit_bytes=VMEM_LIMIT, **kw)


def _dot(a, b, dims):
    return lax.dot_general(a, b, (dims, ((), ())), preferred_element_type=F32)


def _nn(a, b):
    return _dot(a, b, ((1,), (0,)))


def _nt(a, b):
    return _dot(a, b, ((1,), (1,)))


def _tn(a, b):
    return _dot(a, b, ((0,), (0,)))


def _rope_lanes(shape, half, period, first):
    lane = lax.broadcasted_iota(jnp.int32, shape, len(shape) - 1) % period
    return (lane >= first) & (lane < first + half), (lane >= first + half) & (lane < first + 2 * half)


def _rope_fwd(x, c, s, half, lanes):
    x1, _ = lanes
    return x * c + jnp.where(x1, pltpu.roll(x, 128 - half, 1), pltpu.roll(x, half, 1)) * s


def _rope_bwd(g, c, s, half, lanes):
    x1, x2 = lanes
    gs = g * s
    return g * c + jnp.where(x2, pltpu.roll(gs, half, 1), jnp.where(x1, pltpu.roll(gs, 128 - half, 1), 0.0))


def _sigmoid(x):
    return 1.0 / (1.0 + jnp.exp(-x))


def _after(token):
    tokens = [t for t in (token if isinstance(token, (tuple, list)) else [token]) if t is not None]
    return tokens, [pl.BlockSpec(memory_space=pl.ANY)] * len(tokens)


def _matmul(a, b, mode, out_dtype, tm, tn, tk, name, token=None, b_cols=None):
    after, after_specs = _after(token)
    if mode == "nn":
        (m, k), n = a.shape, b.shape[1]
        first = 0
        if b_cols is not None:
            first, n = b_cols[0], b_cols[1] * tn
        a_spec = pl.BlockSpec((tm, tk), lambda j, i, kk: (i, kk))
        b_spec = pl.BlockSpec((tk, tn), lambda j, i, kk: (kk, j + first))
        dot = _nn
    elif mode == "nt":
        (m, k), n = a.shape, b.shape[0]
        a_spec = pl.BlockSpec((tm, tk), lambda j, i, kk: (i, kk))
        b_spec = pl.BlockSpec((tn, tk), lambda j, i, kk: (j, kk))
        dot = _nt
    else:
        (k, m), n = a.shape, b.shape[1]
        a_spec = pl.BlockSpec((tk, tm), lambda j, i, kk: (kk, i))
        b_spec = pl.BlockSpec((tk, tn), lambda j, i, kk: (kk, j))
        dot = _tn
    assert m % tm == 0 and n % tn == 0 and k % tk == 0, (name, m, n, k, tm, tn, tk)
    nk = k // tk

    def body(a_ref, b_ref, *rest):
        o_ref, acc_ref = rest[-2:]
        kk = pl.program_id(2)
        part = dot(a_ref[...], b_ref[...])

        @pl.when(kk == 0)
        def _():
            acc_ref[...] = part

        @pl.when(kk > 0)
        def _():
            acc_ref[...] += part

        @pl.when(kk == nk - 1)
        def _():
            o_ref[...] = acc_ref[...].astype(o_ref.dtype)

    return pl.pallas_call(
        body, name=name, grid=(n // tn, m // tm, nk),
        in_specs=[a_spec, b_spec] + after_specs,
        out_specs=pl.BlockSpec((tm, tn), lambda j, i, kk: (i, j)),
        out_shape=jax.ShapeDtypeStruct((m, n), out_dtype),
        scratch_shapes=[pltpu.VMEM((tm, tn), F32)],
        compiler_params=_cparams(),
    )(a, b, *after)


def _prenorm_fwd(x, g, token=None):
    tm = 512
    after, after_specs = _after(token)

    def body(x_ref, g_ref, *rest):
        h_ref, ht_ref = rest[-2:]
        xv = x_ref[...]
        r = lax.rsqrt(jnp.mean(xv * xv, axis=-1, keepdims=True) + EPS)
        hv = (xv * r * g_ref[...]).astype(BF16)
        h_ref[...] = hv
        ht_ref[...] = hv.T

    return pl.pallas_call(
        body, name="prenorm_fwd", grid=(SEQ // tm,),
        in_specs=[pl.BlockSpec((tm, D_MODEL), lambda i: (i, 0)), pl.BlockSpec((1, D_MODEL), lambda i: (0, 0))] + after_specs,
        out_specs=[pl.BlockSpec((tm, D_MODEL), lambda i: (i, 0)), pl.BlockSpec((D_MODEL, tm), lambda i: (0, i))],
        out_shape=[jax.ShapeDtypeStruct((SEQ, D_MODEL), BF16), jax.ShapeDtypeStruct((D_MODEL, SEQ), BF16)],
    )(x, g, *after)


def _prenorm_bwd(x, dh, dy, g):
    tm = 512

    def body(x_ref, dh_ref, dy_ref, g_ref, gx_ref, dg_ref):
        xv = x_ref[...]
        r = lax.rsqrt(jnp.mean(xv * xv, axis=-1, keepdims=True) + EPS)
        n = xv * r
        dhv = dh_ref[...]
        dn = dhv * g_ref[...]
        gx_ref[...] = dy_ref[...] + r * (dn - n * jnp.mean(dn * n, axis=-1, keepdims=True))
        part = jnp.sum(dhv * n, axis=0, keepdims=True)

        @pl.when(pl.program_id(0) == 0)
        def _():
            dg_ref[...] = part

        @pl.when(pl.program_id(0) > 0)
        def _():
            dg_ref[...] += part

    row = pl.BlockSpec((tm, D_MODEL), lambda i: (i, 0))
    vec = pl.BlockSpec((1, D_MODEL), lambda i: (0, 0))
    return pl.pallas_call(
        body, name="prenorm_bwd", grid=(SEQ // tm,),
        in_specs=[row, row, row, vec], out_specs=[row, vec],
        out_shape=[jax.ShapeDtypeStruct((SEQ, D_MODEL), F32), jax.ShapeDtypeStruct((1, D_MODEL), F32)],
        compiler_params=_cparams(),
    )(x, dh, dy, g)


def _mla_prep_fwd(p, qg, kvg, wuq, wk, wv, rc, rs):
    tm = 512

    def body(lat_ref, qg_ref, kvg_ref, wuq_ref, wk_ref, wv_ref, c_ref, s_ref,
             cqn_ref, ckvn_ref, q_ref, k_ref, v_ref):
        c, s = c_ref[...], s_ref[...]
        lanes = _rope_lanes((tm, 128), MLA_ROPE_HALF, 128, 64)
        cq = lat_ref[:, 0:Q_RANK]
        r1 = lax.rsqrt(jnp.mean(cq * cq, axis=-1, keepdims=True) + EPS)
        cqn = (cq * r1 * qg_ref[...]).astype(BF16)
        cqn_ref[...] = cqn
        q = _nn(cqn, wuq_ref[...])
        for h in range(MLA_HEADS):
            sl = slice(h * 128, (h + 1) * 128)
            q_ref[:, sl] = (_rope_fwd(q[:, sl], c, s, MLA_ROPE_HALF, lanes) * MLA_SCALE).astype(BF16)
        ckv = lat_ref[:, Q_RANK:Q_RANK + KV_RANK]
        r2 = lax.rsqrt(jnp.mean(ckv * ckv, axis=-1, keepdims=True) + EPS)
        ckvn = (ckv * r2 * kvg_ref[...]).astype(BF16)
        ckvn_ref[...] = ckvn
        krr = _rope_fwd(lat_ref[:, Q_RANK + KV_RANK:N_LAT], c, s, MLA_ROPE_HALF, lanes)
        kn = _nn(ckvn, wk_ref[...])
        for h in range(MLA_HEADS):
            sl = slice(h * 128, (h + 1) * 128)
            k_ref[:, sl] = (kn[:, sl] + krr).astype(BF16)
        v_ref[...] = _nn(ckvn, wv_ref[...]).astype(BF16)

    def full(shape):
        return pl.BlockSpec(shape, lambda i: (0, 0))

    def rows(w):
        return pl.BlockSpec((tm, w), lambda i: (i, 0))

    return pl.pallas_call(
        body, name="mla_prep_fwd", grid=(SEQ // tm,),
        in_specs=[pl.BlockSpec((tm, N_LAT), lambda i: (i, COL_LAT // N_LAT)),
                  full((1, Q_RANK)), full((1, KV_RANK)), full((Q_RANK, 1024)), full((KV_RANK, 1024)),
                  full((KV_RANK, 512)), rows(128), rows(128)],
        out_specs=[rows(Q_RANK), rows(KV_RANK), rows(1024), rows(1024), rows(512)],
        out_shape=[jax.ShapeDtypeStruct((SEQ, Q_RANK), BF16), jax.ShapeDtypeStruct((SEQ, KV_RANK), BF16),
                   jax.ShapeDtypeStruct((SEQ, 1024), BF16), jax.ShapeDtypeStruct((SEQ, 1024), BF16),
                   jax.ShapeDtypeStruct((SEQ, 512), BF16)],
        compiler_params=_cparams(),
    )(p, qg, kvg, wuq, wk, wv, rc, rs)


def _mla_prep_bwd(dp_in, p, dq, dk, dv, qg, kvg, wuq, wk, wv, rc, rs):
    tm = 512

    def body(dp_any, lat_ref, dq_ref, dk_ref, dv_ref, qg_ref, kvg_ref, wuq_ref, wk_ref, wv_ref,
             c_ref, s_ref, dp_ref, dqb_ref, dkb_ref, dvb_ref, dgq_ref, dgkv_ref):
        del dp_any
        c, s = c_ref[...], s_ref[...]
        lanes = _rope_lanes((tm, 128), MLA_ROPE_HALF, 128, 64)
        lane = lax.broadcasted_iota(jnp.int32, (tm, 128), 1)
        dkr = jnp.zeros((tm, 128), F32)
        for h in range(MLA_HEADS):
            sl = slice(h * 128, (h + 1) * 128)
            dqb_ref[:, sl] = _rope_bwd(dq_ref[:, sl] * MLA_SCALE, c, s, MLA_ROPE_HALF, lanes).astype(BF16)
            dkh = dk_ref[:, sl]
            dkr = dkr + dkh
            dkb_ref[:, sl] = jnp.where(lane < 64, dkh, 0.0).astype(BF16)
        dkr = jnp.where((lane >= 64) & (lane < 96), dkr, 0.0)
        dkr = _rope_bwd(dkr, c, s, MLA_ROPE_HALF, lanes)
        dvb = dv_ref[...].astype(BF16)
        dvb_ref[...] = dvb

        cq = lat_ref[:, 0:Q_RANK]
        r1 = lax.rsqrt(jnp.mean(cq * cq, axis=-1, keepdims=True) + EPS)
        n1 = cq * r1
        dcqn = _nt(dqb_ref[...], wuq_ref[...])
        dn1 = dcqn * qg_ref[...]
        dcq = r1 * (dn1 - n1 * jnp.mean(dn1 * n1, axis=-1, keepdims=True))
        pq = jnp.sum(dcqn * n1, axis=0, keepdims=True)

        ckv = lat_ref[:, Q_RANK:Q_RANK + KV_RANK]
        r2 = lax.rsqrt(jnp.mean(ckv * ckv, axis=-1, keepdims=True) + EPS)
        n2 = ckv * r2
        dckvn = _nt(dkb_ref[...], wk_ref[...]) + _nt(dvb, wv_ref[...])
        dn2 = dckvn * kvg_ref[...]
        dckv = r2 * (dn2 - n2 * jnp.mean(dn2 * n2, axis=-1, keepdims=True))
        pkv = jnp.sum(dckvn * n2, axis=0, keepdims=True)

        dp_ref[:, 0:Q_RANK] = dcq.astype(BF16)
        dp_ref[:, Q_RANK:Q_RANK + KV_RANK] = dckv.astype(BF16)
        dp_ref[:, Q_RANK + KV_RANK:N_LAT] = dkr.astype(BF16)

        @pl.when(pl.program_id(0) == 0)
        def _():
            dgq_ref[...] = pq
            dgkv_ref[...] = pkv

        @pl.when(pl.program_id(0) > 0)
        def _():
            dgq_ref[...] += pq
            dgkv_ref[...] += pkv

    def full(shape):
        return pl.BlockSpec(shape, lambda i: (0, 0))

    def rows(w):
        return pl.BlockSpec((tm, w), lambda i: (i, 0))

    lat = pl.BlockSpec((tm, N_LAT), lambda i: (i, COL_LAT // N_LAT))
    return pl.pallas_call(
        body, name="mla_prep_bwd", grid=(SEQ // tm,),
        in_specs=[pl.BlockSpec(memory_space=pl.ANY), lat, rows(1024), rows(1024), rows(512),
                  full((1, Q_RANK)), full((1, KV_RANK)), full((Q_RANK, 1024)), full((KV_RANK, 1024)),
                  full((KV_RANK, 512)), rows(128), rows(128)],
        out_specs=[lat, rows(1024), rows(1024), rows(512), full((1, Q_RANK)), full((1, KV_RANK))],
        out_shape=[jax.ShapeDtypeStruct((SEQ, N_PAD), BF16), jax.ShapeDtypeStruct((SEQ, 1024), BF16),
                   jax.ShapeDtypeStruct((SEQ, 1024), BF16), jax.ShapeDtypeStruct((SEQ, 512), BF16),
                   jax.ShapeDtypeStruct((1, Q_RANK), F32), jax.ShapeDtypeStruct((1, KV_RANK), F32)],
        input_output_aliases={0: 0},
        compiler_params=_cparams(),
    )(dp_in, p, dq, dk, dv, qg, kvg, wuq, wk, wv, rc, rs)


FLASH_T = 1024


def _head_half(shape, hh):
    lane = lax.broadcasted_iota(jnp.int32, shape, 1)
    return (lane < 64) if hh == 0 else (lane >= 64)


def _diag_keep(nr, nk):
    row = lax.broadcasted_iota(jnp.int32, (nr, nk), 0)
    col = lax.broadcasted_iota(jnp.int32, (nr, nk), 1)
    return row + (nk - nr) >= col


def _tri_steps(nb, q_major):
    if q_major:
        pairs = [(i, kb) for i in range(nb) for kb in range(i + 1)]
    else:
        pairs = [(i, kb) for kb in range(nb) for i in range(kb, nb)]
    return jnp.asarray([p[0] for p in pairs], jnp.int32), jnp.asarray([p[1] for p in pairs], jnp.int32)


def _mla_flash_fwd(q, k, v):
    t = FLASH_T
    nb = SEQ // t
    qtab, ktab = _tri_steps(nb, True)

    def body(qi_ref, ki_ref, q_ref, k_ref, v_ref, o_ref, lse_ref, m_scr, l_scr, acc_scr):
        step = pl.program_id(1)
        i, kb = qi_ref[step], ki_ref[step]

        @pl.when(kb == 0)
        def _():
            m_scr[...] = jnp.full_like(m_scr, NEG)
            l_scr[...] = jnp.zeros_like(l_scr)
            acc_scr[...] = jnp.zeros_like(acc_scr)

        def update(r0, nr, nk, diagonal):
            rs = slice(r0, r0 + nr)
            vv = v_ref[0:nk, :]
            for hh in range(2):
                sl = slice(hh * 128, (hh + 1) * 128)
                s = _nt(q_ref[rs, sl], k_ref[0:nk, sl])
                if diagonal:
                    s = jnp.where(_diag_keep(nr, nk), s, NEG)
                m_prev = m_scr[hh, rs, :]
                m_new = jnp.maximum(m_prev, jnp.max(s, axis=-1, keepdims=True))
                pr = jnp.exp(s - jnp.tile(m_new, (1, nk // 128)))
                alpha = jnp.exp(m_prev - m_new)
                l_scr[hh, rs, :] = alpha * l_scr[hh, rs, :] + jnp.sum(pr, axis=-1, keepdims=True)
                acc_scr[hh, rs, :] = alpha * acc_scr[hh, rs, :] + _nn(pr.astype(BF16), vv)
                m_scr[hh, rs, :] = m_new

        @pl.when(kb < i)
        def _():
            update(0, t, t, False)

        @pl.when(kb == i)
        def _():
            update(0, t // 2, t // 2, True)
            update(t // 2, t // 2, t, True)
            o0 = acc_scr[0] / l_scr[0]
            o1 = acc_scr[1] / l_scr[1]
            o_ref[...] = jnp.where(_head_half((t, 128), 0), o0, o1)
            for hh in range(2):
                lse_ref[:, hh * 128:(hh + 1) * 128] = m_scr[hh] + jnp.log(l_scr[hh])

    grid_spec = pltpu.PrefetchScalarGridSpec(
        num_scalar_prefetch=2, grid=(4, qtab.shape[0]),
        in_specs=[pl.BlockSpec((t, 256), lambda j, s, qi, ki: (qi[s], j)),
                  pl.BlockSpec((t, 256), lambda j, s, qi, ki: (ki[s], j)),
                  pl.BlockSpec((t, 128), lambda j, s, qi, ki: (ki[s], j))],
        out_specs=[pl.BlockSpec((t, 128), lambda j, s, qi, ki: (qi[s], j)),
                   pl.BlockSpec((t, 256), lambda j, s, qi, ki: (qi[s], j))],
        scratch_shapes=[pltpu.VMEM((2, t, 128), F32), pltpu.VMEM((2, t, 128), F32), pltpu.VMEM((2, t, 128), F32)])
    return pl.pallas_call(
        body, name="mla_flash_fwd", grid_spec=grid_spec,
        out_shape=[jax.ShapeDtypeStruct((SEQ, 512), F32), jax.ShapeDtypeStruct((SEQ, 1024), F32)],
        compiler_params=_cparams(),
    )(qtab, ktab, q, k, v)


def _mla_flash_bwd(q, k, v, o, do, lse, token=None):
    t = FLASH_T
    nb = SEQ // t
    qtab, ktab = _tri_steps(nb, False)
    after, after_specs = _after(token)

    def body(qi_ref, ki_ref, q_ref, k_ref, v_ref, o_ref, do_ref, lse_ref, *rest):
        dq_ref, dk_ref, dv_ref, dk_scr, dv_scr = rest[-5:]
        step = pl.program_id(1)
        i, kb = qi_ref[step], ki_ref[step]

        @pl.when(step == 0)
        def _():
            dq_ref[...] = jnp.zeros_like(dq_ref)

        @pl.when(i == kb)
        def _():
            dk_scr[...] = jnp.zeros_like(dk_scr)
            dv_scr[...] = jnp.zeros_like(dv_scr)

        def update(r0, nr, nk, diagonal):
            rs = slice(r0, r0 + nr)
            vv = v_ref[0:nk, :]
            ov = o_ref[rs, :]
            dov = do_ref[rs, :]
            rows = pl.ds(pl.multiple_of(i * t + r0, t // 2), nr)
            for hh in range(2):
                sl = slice(hh * 128, (hh + 1) * 128)
                qh, kh = q_ref[rs, sl], k_ref[0:nk, sl]
                s = _nt(qh, kh)
                if diagonal:
                    s = jnp.where(_diag_keep(nr, nk), s, NEG)
                pr = jnp.exp(s - jnp.tile(lse_ref[rs, sl], (1, nk // 128)))
                dom = jnp.where(_head_half((nr, 128), hh), dov, 0.0)
                domb = dom.astype(BF16)
                dv_scr[0:nk, :] += _tn(pr.astype(BF16), domb)
                dpr = _nt(domb, vv)
                delta = jnp.sum(dom * ov, axis=-1, keepdims=True)
                ds = (pr * (dpr - delta)).astype(BF16)
                dq_ref[rows, sl] += _nn(ds, kh)
                dk_scr[hh, 0:nk, :] += _tn(ds, qh)

        @pl.when(i > kb)
        def _():
            update(0, t, t, False)

        @pl.when(i == kb)
        def _():
            update(0, t // 2, t // 2, True)
            update(t // 2, t // 2, t, True)

        @pl.when(i == nb - 1)
        def _():
            dk_ref[:, 0:128] = dk_scr[0]
            dk_ref[:, 128:256] = dk_scr[1]
            dv_ref[...] = dv_scr[...]

    qi_map = lambda j, s, qi, ki: (qi[s], j)
    ki_map = lambda j, s, qi, ki: (ki[s], j)
    grid_spec = pltpu.PrefetchScalarGridSpec(
        num_scalar_prefetch=2, grid=(4, qtab.shape[0]),
        in_specs=[pl.BlockSpec((t, 256), qi_map), pl.BlockSpec((t, 256), ki_map), pl.BlockSpec((t, 128), ki_map),
                  pl.BlockSpec((t, 128), qi_map), pl.BlockSpec((t, 128), qi_map), pl.BlockSpec((t, 256), qi_map)]
        + after_specs,
        out_specs=[pl.BlockSpec((SEQ, 256), lambda j, s, qi, ki: (0, j)), pl.BlockSpec((t, 256), ki_map),
                   pl.BlockSpec((t, 128), ki_map)],
        scratch_shapes=[pltpu.VMEM((2, t, 128), F32), pltpu.VMEM((t, 128), F32)])
    return pl.pallas_call(
        body, name="mla_flash_bwd", grid_spec=grid_spec,
        out_shape=[jax.ShapeDtypeStruct((SEQ, 1024), F32), jax.ShapeDtypeStruct((SEQ, 1024), F32),
                   jax.ShapeDtypeStruct((SEQ, 512), F32)],
        compiler_params=_cparams(),
    )(qtab, ktab, q, k, v, o, do, lse, *after)


DIL_UNROLL = 4


def _strided(start, size, d):
    return pl.ds(start, size) if d == 1 else pl.ds(start, size, stride=d)


def _dil_prep_fwd(p, rc, rs, g):
    d = DIL_DILATIONS[g]
    sub_len = SEQ // d
    ch = min(sub_len, 512)

    def body(p_ref, c_ref, s_ref, o_ref, x_scr):
        tq = pl.program_id(0)
        lanes = _rope_lanes((ch, 128), DIL_ROPE_HALF, 64, 0)
        o_ref[0, 0:BAND, :] = jnp.zeros((BAND, 128), BF16)

        @pl.when(tq < 2)
        def _():
            mult = jnp.where(tq == 0, DIL_SCALE, 1.0).astype(F32)
            for c0 in range(0, SEQ, ch):
                rows = pl.ds(c0, ch)
                x_scr[rows, :] = _rope_fwd(p_ref[rows, :], c_ref[rows, :] * mult, s_ref[rows, :] * mult, DIL_ROPE_HALF, lanes)

        def gather(src):
            for r in range(d):
                for c0 in range(0, sub_len, ch):
                    at = BAND + r * sub_len + c0
                    o_ref[0, at:at + ch, :] = src[_strided(r + c0 * d, ch, d), :].astype(BF16)

        @pl.when(tq < 2)
        def _():
            gather(x_scr)

        @pl.when(tq == 2)
        def _():
            gather(p_ref)

    tab = pl.BlockSpec((SEQ, 128), lambda tq, pr: (0, 0))
    return pl.pallas_call(
        body, name=f"dil_prep_fwd_g{g}", grid=(3, 4),
        in_specs=[pl.BlockSpec((SEQ, 128), lambda tq, pr: (0, COL_QKV // 128 + (tq * 3 + g) * 4 + pr)), tab, tab],
        out_specs=pl.BlockSpec((1, BAND + SEQ, 128), lambda tq, pr: (tq, 0, pr)),
        out_shape=jax.ShapeDtypeStruct((3, BAND + SEQ, 512), BF16),
        scratch_shapes=[pltpu.VMEM((SEQ, 128), F32)],
        compiler_params=_cparams(),
    )(p, rc, rs)


DIL_ST = 1024
DIL_NB = DIL_ST // BAND


def _band_keep(g, b, t):
    nbs = SEQ // DIL_DILATIONS[g] // BAND
    row = lax.broadcasted_iota(jnp.int32, (BAND, 2 * BAND), 0)
    col = lax.broadcasted_iota(jnp.int32, (BAND, 2 * BAND), 1)
    cur = (col >= BAND) & (row >= col - BAND)
    prev = (col < BAND) & (col >= row)
    if nbs >= DIL_NB:
        if b > 0:
            return cur | prev
        return cur | (prev & ((t * DIL_NB) % nbs != 0))
    return cur | prev if b % nbs else cur


def _dil_tok(g, b, t):
    d = DIL_DILATIONS[g]
    nbs = SEQ // d // BAND
    gb = t * DIL_NB + b
    return _strided((gb % nbs) * BAND * d + gb // nbs, BAND, d)


def _dil_attn_fwd2(qkv, g):
    def body(q_ref, k_ref, v_ref, o_ref, l_ref, s_scr, p_scr, o_scr):
        t = pl.program_id(1)
        base = t * DIL_ST
        half0 = _head_half((DIL_ST, 128), 0)
        lse_h = []
        for hh in range(2):
            half = _head_half((BAND, 128), hh)
            for b in range(DIL_NB):
                qv = q_ref[0, pl.ds(pl.multiple_of(base + (b + 1) * BAND, BAND), BAND), :]
                k2 = k_ref[0, pl.ds(pl.multiple_of(base + b * BAND, BAND), 2 * BAND), :]
                sb = _nt(jnp.where(half, qv, jnp.zeros_like(qv)), k2)
                s_scr[b * BAND:(b + 1) * BAND, :] = jnp.where(_band_keep(g, b, t), sb, NEG)
            s = s_scr[...]
            m = jnp.max(s, axis=-1, keepdims=True)
            pr = jnp.exp(s - m)
            den = jnp.sum(pr, axis=-1, keepdims=True)
            p_scr[...] = pr.astype(BF16)
            for b in range(DIL_NB):
                v2 = v_ref[0, pl.ds(pl.multiple_of(base + b * BAND, BAND), 2 * BAND), :]
                o_scr[hh, b * BAND:(b + 1) * BAND, :] = _nn(p_scr[b * BAND:(b + 1) * BAND, :], v2)
            o_scr[hh] = o_scr[hh] / den
            lse_h.append(m + jnp.log(den))
        out = jnp.where(half0, o_scr[0], o_scr[1])
        lse = jnp.where(half0, lse_h[0], lse_h[1])
        for b in range(DIL_NB):
            tok = _dil_tok(g, b, t)
            o_ref[tok, :] = out[b * BAND:(b + 1) * BAND, :]
            l_ref[tok, :] = lse[b * BAND:(b + 1) * BAND, :]

    def inp(tq):
        return pl.BlockSpec((1, BAND + SEQ, 128), lambda pr, t: (tq, 0, pr))

    out = pl.BlockSpec((SEQ, 128), lambda pr, t: (0, pr))
    return pl.pallas_call(
        body, name=f"dil_attn_fwd_g{g}", grid=(4, SEQ // DIL_ST),
        in_specs=[inp(0), inp(1), inp(2)], out_specs=[out, out],
        out_shape=[jax.ShapeDtypeStruct((SEQ, 512), F32), jax.ShapeDtypeStruct((SEQ, 512), F32)],
        scratch_shapes=[pltpu.VMEM((DIL_ST, 2 * BAND), F32), pltpu.VMEM((DIL_ST, 2 * BAND), BF16),
                        pltpu.VMEM((2, DIL_ST, 128), F32)],
        compiler_params=_cparams(),
    )(qkv, qkv, qkv)


def _dil_attn_bwd2(qkv, dyd, yd, lse_all, g, token=None):
    d = DIL_DILATIONS[g]
    sub_len = SEQ // d
    nst = SEQ // DIL_ST
    after, after_specs = _after(token)

    def body(q_ref, k_ref, v_ref, do_ref, y_ref, l_ref, *rest):
        out_ref, dk_scr, dv_scr, s_scr, dp_scr, p_scr, ds_scr, do_scr, y_scr, l_scr, dq_scr = rest[-11:]
        t = pl.program_id(1)
        base = t * DIL_ST

        @pl.when(t == 0)
        def _():
            dk_scr[...] = jnp.zeros_like(dk_scr)
            dv_scr[...] = jnp.zeros_like(dv_scr)

        for b in range(DIL_NB):
            tok = _dil_tok(g, b, t)
            do_scr[b * BAND:(b + 1) * BAND, :] = do_ref[tok, :]
            y_scr[b * BAND:(b + 1) * BAND, :] = y_ref[tok, :]
            l_scr[b * BAND:(b + 1) * BAND, :] = l_ref[tok, :]
        for hh in range(2):
            half = _head_half((BAND, 128), hh)
            half_st = _head_half((DIL_ST, 128), hh)
            dom = jnp.where(half_st, do_scr[...], 0.0)
            delta = jnp.sum(dom * y_scr[...], axis=-1, keepdims=True)
            lcol = jnp.max(jnp.where(half_st, l_scr[...], NEG), axis=-1, keepdims=True)
            for b in range(DIL_NB):
                rows = slice(b * BAND, (b + 1) * BAND)
                qv = q_ref[0, pl.ds(pl.multiple_of(base + (b + 1) * BAND, BAND), BAND), :]
                band = pl.ds(pl.multiple_of(base + b * BAND, BAND), 2 * BAND)
                sb = _nt(jnp.where(half, qv, jnp.zeros_like(qv)), k_ref[0, band, :])
                s_scr[rows, :] = jnp.where(_band_keep(g, b, t), sb, NEG)
                dp_scr[rows, :] = _nt(dom[rows, :].astype(BF16), v_ref[0, band, :])
            pr = jnp.exp(s_scr[...] - lcol)
            p_scr[...] = pr.astype(BF16)
            ds_scr[...] = (pr * (dp_scr[...] - delta)).astype(BF16)
            for b in range(DIL_NB):
                rows = slice(b * BAND, (b + 1) * BAND)
                qv = q_ref[0, pl.ds(pl.multiple_of(base + (b + 1) * BAND, BAND), BAND), :]
                band = pl.ds(pl.multiple_of(base + b * BAND, BAND), 2 * BAND)
                dqb = jnp.where(half, _nn(ds_scr[rows, :], k_ref[0, band, :]), 0.0)
                if hh == 0:
                    dq_scr[rows, :] = dqb
                else:
                    dq_scr[rows, :] += dqb
                half2 = _head_half((2 * BAND, 128), hh)
                dk_scr[band, :] += jnp.where(half2, _tn(ds_scr[rows, :], qv), 0.0)
                dv_scr[band, :] += _tn(p_scr[rows, :], dom[rows, :].astype(BF16))
        for b in range(DIL_NB):
            out_ref[pl.ds(0, 1), _dil_tok(g, b, t), :] = dq_scr[b * BAND:(b + 1) * BAND, :][None]

        @pl.when(t == nst - 1)
        def _():
            for r in range(d):
                rows = _strided(r, sub_len, d)
                out_ref[pl.ds(1, 1), rows, :] = dk_scr[BAND + r * sub_len:BAND + (r + 1) * sub_len, :][None]
                out_ref[pl.ds(2, 1), rows, :] = dv_scr[BAND + r * sub_len:BAND + (r + 1) * sub_len, :][None]

    def inp(tq):
        return pl.BlockSpec((1, BAND + SEQ, 128), lambda pr, t: (tq, 0, pr))

    tok_spec = pl.BlockSpec((SEQ, 128), lambda pr, t: (0, pr))
    st = (DIL_ST, 2 * BAND)
    return pl.pallas_call(
        body, name=f"dil_attn_bwd_g{g}", grid=(4, nst),
        in_specs=[inp(0), inp(1), inp(2), tok_spec, tok_spec, tok_spec] + after_specs,
        out_specs=pl.BlockSpec((3, SEQ, 128), lambda pr, t: (0, 0, pr)),
        out_shape=jax.ShapeDtypeStruct((3, SEQ, 512), F32),
        scratch_shapes=[pltpu.VMEM((BAND + SEQ, 128), F32), pltpu.VMEM((BAND + SEQ, 128), F32),
                        pltpu.VMEM(st, F32), pltpu.VMEM(st, F32), pltpu.VMEM(st, BF16), pltpu.VMEM(st, BF16),
                        pltpu.VMEM((DIL_ST, 128), F32), pltpu.VMEM((DIL_ST, 128), F32), pltpu.VMEM((DIL_ST, 128), F32),
                        pltpu.VMEM((DIL_ST, 128), F32)],
        compiler_params=_cparams(),
    )(qkv, qkv, qkv, dyd, yd, lse_all, *after)


def _band_masks():
    row = lax.broadcasted_iota(jnp.int32, (BAND, BAND), 0)
    col = lax.broadcasted_iota(jnp.int32, (BAND, BAND), 1)
    return row >= col, col >= row


def _dil_attn_fwd(qkv, g):
    d = DIL_DILATIONS[g]
    nbs = SEQ // d // BAND
    nblk = SEQ // BAND

    def body(q_ref, k_ref, v_ref, o_ref, l_ref):
        keep_c, keep_p = _band_masks()
        half0 = _head_half((BAND, 128), 0)

        def step(i, carry):
            cur = pl.ds(pl.multiple_of(i * BAND, BAND), BAND)
            prv = pl.ds(pl.multiple_of(jnp.maximum(i - 1, 0) * BAND, BAND), BAND)
            has_prev = (i % nbs) != 0
            qv = q_ref[0, cur, :]
            kc, kp = k_ref[0, cur, :], k_ref[0, prv, :]
            vc, vp = v_ref[0, cur, :], v_ref[0, prv, :]
            outs, lses = [], []
            for hh in range(2):
                qm = jnp.where(_head_half((BAND, 128), hh), qv, jnp.zeros_like(qv))
                sc = jnp.where(keep_c, _nt(qm, kc), NEG)
                sp = jnp.where(keep_p & has_prev, _nt(qm, kp), NEG)
                m = jnp.maximum(jnp.max(sc, axis=-1, keepdims=True), jnp.max(sp, axis=-1, keepdims=True))
                pc, pp = jnp.exp(sc - m), jnp.exp(sp - m)
                den = jnp.sum(pc, axis=-1, keepdims=True) + jnp.sum(pp, axis=-1, keepdims=True)
                o = (_nn(pc.astype(BF16), vc) + _nn(pp.astype(BF16), vp)) / den
                outs.append(o)
                lses.append(jnp.broadcast_to(m + jnp.log(den), (BAND, 128)))
            tok = _strided((i % nbs) * BAND * d + i // nbs, BAND, d)
            o_ref[tok, :] = jnp.where(half0, outs[0], outs[1])
            l_ref[tok, :] = jnp.where(half0, lses[0], lses[1])
            return carry

        lax.fori_loop(0, nblk, step, 0, unroll=DIL_UNROLL)

    def inp(tq):
        return pl.BlockSpec((1, SEQ, 128), lambda pr: (tq, 0, pr))

    out = pl.BlockSpec((SEQ, 128), lambda pr: (0, pr))
    return pl.pallas_call(
        body, name=f"dil_attn_fwd_g{g}", grid=(4,),
        in_specs=[inp(0), inp(1), inp(2)], out_specs=[out, out],
        out_shape=[jax.ShapeDtypeStruct((SEQ, 512), F32), jax.ShapeDtypeStruct((SEQ, 512), F32)],
        compiler_params=_cparams(),
    )(qkv, qkv, qkv)


def _dil_attn_bwd(qkv, dyd, yd, lse_all, g):
    d = DIL_DILATIONS[g]
    sub_len = SEQ // d
    nbs = sub_len // BAND
    nblk = SEQ // BAND

    def body(q_ref, k_ref, v_ref, do_ref, y_ref, l_ref, out_ref, dk_scr, dv_scr):
        keep_c, keep_p = _band_masks()
        dk_scr[...] = jnp.zeros_like(dk_scr)
        dv_scr[...] = jnp.zeros_like(dv_scr)

        def step(i, carry):
            cur = pl.ds(pl.multiple_of(i * BAND, BAND), BAND)
            prv = pl.ds(pl.multiple_of(jnp.maximum(i - 1, 0) * BAND, BAND), BAND)
            has_prev = (i % nbs) != 0
            tok = _strided((i % nbs) * BAND * d + i // nbs, BAND, d)
            qv = q_ref[0, cur, :]
            kc, kp = k_ref[0, cur, :], k_ref[0, prv, :]
            vc, vp = v_ref[0, cur, :], v_ref[0, prv, :]
            dov, yv, lv = do_ref[tok, :], y_ref[tok, :], l_ref[tok, :]
            dq = jnp.zeros((BAND, 128), F32)
            dkc = jnp.zeros((BAND, 128), F32)
            dkp = jnp.zeros((BAND, 128), F32)
            dvc = jnp.zeros((BAND, 128), F32)
            dvp = jnp.zeros((BAND, 128), F32)
            for hh in range(2):
                half = _head_half((BAND, 128), hh)
                qm = jnp.where(half, qv, jnp.zeros_like(qv))
                lcol = jnp.max(jnp.where(half, lv, NEG), axis=-1, keepdims=True)
                pc = jnp.exp(jnp.where(keep_c, _nt(qm, kc), NEG) - lcol)
                pp = jnp.exp(jnp.where(keep_p & has_prev, _nt(qm, kp), NEG) - lcol)
                dom = jnp.where(half, dov, 0.0)
                domb = dom.astype(BF16)
                delta = jnp.sum(dom * yv, axis=-1, keepdims=True)
                dsc = (pc * (_nt(domb, vc) - delta)).astype(BF16)
                dsp = (pp * (_nt(domb, vp) - delta)).astype(BF16)
                dvc = dvc + _tn(pc.astype(BF16), domb)
                dvp = dvp + _tn(pp.astype(BF16), domb)
                dq = dq + jnp.where(half, _nn(dsc, kc) + _nn(dsp, kp), 0.0)
                dkc = dkc + jnp.where(half, _tn(dsc, qv), 0.0)
                dkp = dkp + jnp.where(half, _tn(dsp, qv), 0.0)
            out_ref[pl.ds(0, 1), tok, :] = dq[None]
            dk_scr[cur, :] += dkc
            dk_scr[prv, :] += dkp
            dv_scr[cur, :] += dvc
            dv_scr[prv, :] += dvp
            return carry

        lax.fori_loop(0, nblk, step, 0, unroll=DIL_UNROLL)
        for r in range(d):
            rows = _strided(r, sub_len, d)
            out_ref[pl.ds(1, 1), rows, :] = dk_scr[r * sub_len:(r + 1) * sub_len, :][None]
            out_ref[pl.ds(2, 1), rows, :] = dv_scr[r * sub_len:(r + 1) * sub_len, :][None]

    def inp(tq):
        return pl.BlockSpec((1, SEQ, 128), lambda pr: (tq, 0, pr))

    tok_spec = pl.BlockSpec((SEQ, 128), lambda pr: (0, pr))
    return pl.pallas_call(
        body, name=f"dil_attn_bwd_g{g}", grid=(4,),
        in_specs=[inp(0), inp(1), inp(2), tok_spec, tok_spec, tok_spec],
        out_specs=pl.BlockSpec((3, SEQ, 128), lambda pr: (0, 0, pr)),
        out_shape=jax.ShapeDtypeStruct((3, SEQ, 512), F32),
        scratch_shapes=[pltpu.VMEM((SEQ, 128), F32), pltpu.VMEM((SEQ, 128), F32)],
        compiler_params=_cparams(),
    )(qkv, qkv, qkv, dyd, yd, lse_all)


def _dil_prep_bwd(dp_in, dqkv, rc, rs, g):
    tm = 1024

    def body(dp_any, g_ref, c_ref, s_ref, dp_ref):
        del dp_any
        tq = pl.program_id(0)

        @pl.when(tq == 2)
        def _():
            dp_ref[...] = g_ref[0].astype(BF16)

        @pl.when(tq < 2)
        def _():
            mult = jnp.where(tq == 0, DIL_SCALE, 1.0).astype(F32)
            lanes = _rope_lanes((tm, 128), DIL_ROPE_HALF, 64, 0)
            cv, sv = c_ref[...], s_ref[...] * mult
            cv = cv * mult
            for pr in range(4):
                gv = g_ref[0, :, pr * 128:(pr + 1) * 128]
                dp_ref[:, pr * 128:(pr + 1) * 128] = _rope_bwd(gv, cv, sv, DIL_ROPE_HALF, lanes).astype(BF16)

    tab = pl.BlockSpec((tm, 128), lambda tq, i: (i, 0))
    return pl.pallas_call(
        body, name=f"dil_prep_bwd_g{g}", grid=(3, SEQ // tm),
        in_specs=[pl.BlockSpec(memory_space=pl.ANY),
                  pl.BlockSpec((1, tm, 512), lambda tq, i: (tq, i, 0)), tab, tab],
        out_specs=pl.BlockSpec((tm, 512), lambda tq, i: (i, COL_QKV // 512 + tq * 3 + g)),
        out_shape=jax.ShapeDtypeStruct((SEQ, N_PAD), BF16),
        input_output_aliases={0: 0},
    )(dp_in, dqkv, rc, rs)


TAIL_T = 256


def _tail(p, ya, o_g, l_g, x, target, wpm, wpd, wout, post_g):
    tm = TAIL_T

    def body(pgz_ref, ya_ref, o0_ref, o1_ref, o2_ref, l0_ref, l1_ref, l2_ref, x_ref, t_ref,
             wpm_ref, wpd_ref, wout_ref, pg_ref,
             dp_ref, dy_ref, mg_ref, dt_ref, ua_ref, dpa_ref, ud_ref, dpd_ref, dya_ref, dyd_ref,
             yd_ref, lse_ref, loss_ref, dgp_ref):
        l0, l1, l2 = l0_ref[...], l1_ref[...], l2_ref[...]
        mx = jnp.maximum(jnp.maximum(l0, l1), l2)
        e0, e1, e2 = jnp.exp(l0 - mx), jnp.exp(l1 - mx), jnp.exp(l2 - mx)
        den = e0 + e1 + e2
        yd = (e0 * o0_ref[...] + e1 * o1_ref[...] + e2 * o2_ref[...]) / den
        yd_ref[...] = yd
        lse_ref[...] = mx + jnp.log(den)
        ya = ya_ref[...]

        gm, gd = pgz_ref[:, 0:1024], pgz_ref[:, 1024:2048]
        zm, zd = pgz_ref[:, 2048:2560], pgz_ref[:, 2560:3072]
        szm, szd = _sigmoid(zm), _sigmoid(zd)
        sm, sd = zm * szm, zd * szd
        ua = (ya * sm).astype(BF16)
        ud = (yd * sd).astype(BF16)
        ua_ref[...] = ua
        ud_ref[...] = ud
        pa = _nn(ua, wpm_ref[...])
        pd = _nn(ud, wpd_ref[...])
        sgm, sgd = _sigmoid(gm), _sigmoid(gd)
        mg = (sgm * pa + sgd * pd).astype(BF16)
        mg_ref[...] = mg
        t = _nn(mg, wout_ref[...])
        r3 = lax.rsqrt(jnp.mean(t * t, axis=-1, keepdims=True) + EPS)
        n = t * r3
        pg = pg_ref[...]
        err = x_ref[...] + n * pg - t_ref[...]
        lpart = jnp.sum(err * err, axis=0, keepdims=True)

        dy = err * (1.0 / D_MODEL)
        dy_ref[...] = dy
        gpart = jnp.sum(dy * n, axis=0, keepdims=True)
        dn = dy * pg
        dt = (r3 * (dn - n * jnp.mean(dn * n, axis=-1, keepdims=True))).astype(BF16)
        dt_ref[...] = dt
        dmg = _nt(dt, wout_ref[...])
        dpa = (dmg * sgm).astype(BF16)
        dpd = (dmg * sgd).astype(BF16)
        dpa_ref[...] = dpa
        dpd_ref[...] = dpd
        dp_ref[:, 0:1024] = (dmg * pa * sgm * (1.0 - sgm)).astype(BF16)
        dp_ref[:, 1024:2048] = (dmg * pd * sgd * (1.0 - sgd)).astype(BF16)
        dua = _nt(dpa, wpm_ref[...])
        dud = _nt(dpd, wpd_ref[...])
        dya_ref[...] = dua * sm
        dyd_ref[...] = dud * sd
        dp_ref[:, 2048:2560] = (dua * ya * szm * (1.0 + zm * (1.0 - szm))).astype(BF16)
        dp_ref[:, 2560:3072] = (dud * yd * szd * (1.0 + zd * (1.0 - szd))).astype(BF16)

        @pl.when(pl.program_id(0) == 0)
        def _():
            loss_ref[...] = lpart
            dgp_ref[...] = gpart

        @pl.when(pl.program_id(0) > 0)
        def _():
            loss_ref[...] += lpart
            dgp_ref[...] += gpart

    def rows(w):
        return pl.BlockSpec((tm, w), lambda i: (i, 0))

    def full(shape):
        return pl.BlockSpec(shape, lambda i: (0, 0))

    def sds(w, dt):
        return jax.ShapeDtypeStruct((SEQ, w), dt)

    return pl.pallas_call(
        body, name="tail", grid=(SEQ // tm,),
        in_specs=[rows(3072), rows(512), rows(512), rows(512), rows(512), rows(512), rows(512), rows(512),
                  rows(1024), rows(1024), full((512, 1024)), full((512, 1024)), full((1024, 1024)), full((1, 1024))],
        out_specs=[rows(3072), rows(1024), rows(1024), rows(1024), rows(512), rows(1024), rows(512), rows(1024),
                   rows(512), rows(512), rows(512), rows(512), full((1, 1024)), full((1, 1024))],
        out_shape=[sds(N_PAD, BF16), sds(1024, F32), sds(1024, BF16), sds(1024, BF16), sds(512, BF16),
                   sds(1024, BF16), sds(512, BF16), sds(1024, BF16), sds(512, F32), sds(512, F32),
                   sds(512, F32), sds(512, F32),
                   jax.ShapeDtypeStruct((1, 1024), F32), jax.ShapeDtypeStruct((1, 1024), F32)],
        compiler_params=_cparams(),
    )(p, ya, o_g[0], o_g[1], o_g[2], l_g[0], l_g[1], l_g[2], x, target, wpm, wpd, wout, post_g)


def _sum_parts(recv, own, me, tr, name):
    n, r, w = recv.shape
    own_spec = (pl.BlockSpec((tr, w), lambda i, me_ref: (i, 0)) if own.ndim == 2
                else pl.BlockSpec((None, tr, w), lambda i, me_ref: (me_ref[0], i, 0)))

    def body(me_ref, p_ref, own_ref, o_ref):
        mine = own_ref[...].astype(F32)
        acc = jnp.zeros((tr, w), F32)
        for s in range(n):
            acc = acc + jnp.where(me_ref[0] == s, mine, p_ref[s].astype(F32))
        o_ref[...] = acc

    return pl.pallas_call(
        body, name=name,
        grid_spec=pltpu.PrefetchScalarGridSpec(
            num_scalar_prefetch=1, grid=(r // tr,),
            in_specs=[pl.BlockSpec((n, tr, w), lambda i, me_ref: (0, i, 0)), own_spec],
            out_specs=pl.BlockSpec((tr, w), lambda i, me_ref: (i, 0))),
        out_shape=jax.ShapeDtypeStruct((r, w), F32),
    )(me.reshape(1), recv, own)


def _adamw(w, g, m, v, name):
    lead = w.shape[:-2]
    r, c = w.shape[-2:]
    tr = max([t for t in range(8, 257, 8) if r % t == 0], default=r)
    c1 = 1.0 - ADAM_B1 ** ADAM_STEP
    c2 = 1.0 - ADAM_B2 ** ADAM_STEP

    def body(w_ref, g_ref, m_ref, v_ref, d_ref, nm_ref, nv_ref):
        gv = g_ref[...]
        nm = ADAM_B1 * m_ref[...] + (1.0 - ADAM_B1) * gv
        nv = ADAM_B2 * v_ref[...] + (1.0 - ADAM_B2) * (gv * gv)
        nm_ref[...] = nm
        nv_ref[...] = nv
        d_ref[...] = -ADAM_LR * ((nm / c1) / (jnp.sqrt(nv / c2) + ADAM_EPS) + ADAM_WD * w_ref[...])

    zeros = (0,) * len(lead)
    spec = pl.BlockSpec((1,) * len(lead) + (tr, c), lambda i: zeros + (i, 0))
    sd = jax.ShapeDtypeStruct(w.shape, F32)
    return pl.pallas_call(
        body, name=name, grid=(r // tr,),
        in_specs=[spec] * 4, out_specs=[spec] * 3, out_shape=[sd] * 3,
    )(w, g, m, v)


ANY = pl.BlockSpec(memory_space=pl.ANY)


def _my_place():
    return lax.axis_index("x"), lax.axis_index("y"), lax.axis_index("c")


HBM = pl.BlockSpec(memory_space=pltpu.HBM)
SEM = pl.BlockSpec(memory_space=pltpu.SEMAPHORE)
DATAFLOW = pltpu.SideEffectType.DATAFLOW_SIDE_EFFECTING


def _other_chips(x, y):
    return [(1 - x, y), (x, 1 - y), (1 - x, 1 - y)]


def _half_rows(mi, hc):
    hr = SHARD_SHAPES[mi][0] // 2
    return pl.ds(pl.multiple_of(hc * hr, 16), hr)


def _gather_start(mats, landing):
    n = N_MATS

    def body(*refs):
        m_refs, land_refs = refs[:n], refs[n:2 * n]
        send_sems, recv_sems, token = refs[2 * n], refs[2 * n + 1], refs[-1]
        x, y, c = _my_place()
        for mi in range(n):
            rows = _half_rows(mi, c)
            for j, (cx, cy) in enumerate(_other_chips(x, y)):
                pltpu.make_async_remote_copy(
                    src_ref=m_refs[mi].at[rows, :], dst_ref=land_refs[mi].at[2 * x + y, rows, :],
                    send_sem=send_sems.at[mi * 3 + j], recv_sem=recv_sems.at[mi * 3 + j],
                    device_id=(cx, cy, c), device_id_type=MESH).start()
        token[...] = jnp.zeros_like(token)

    hbm = [pltpu.HBM(a.shape, a.dtype) for a in list(mats) + list(landing)]
    outs = pl.pallas_call(
        body, name="gather_start",
        out_shape=(pltpu.SemaphoreType.DMA((3 * n,)), pltpu.SemaphoreType.DMA((3 * n,)), *hbm,
                   jax.ShapeDtypeStruct((8, 128), F32)),
        in_specs=[HBM] * (2 * n), out_specs=(SEM, SEM, *[HBM] * (2 * n), pl.BlockSpec(memory_space=pltpu.VMEM)),
        input_output_aliases={i: 2 + i for i in range(2 * n)},
        compiler_params=pltpu.CompilerParams(has_side_effects=DATAFLOW),
    )(*[pltpu.with_memory_space_constraint(a, pltpu.HBM) for a in list(mats) + list(landing)])
    return outs[:-1], outs[-1]


def _gather_wait(handle, after):
    n = N_MATS

    def body(*refs):
        m_refs, land_refs = refs[:n], refs[n:2 * n]
        send_sems, recv_sems = refs[2 * n], refs[2 * n + 1]
        x, y, c = _my_place()
        for mi in range(n):
            rows = _half_rows(mi, c)
            for j, (cx, cy) in enumerate(_other_chips(x, y)):
                pltpu.make_async_remote_copy(
                    src_ref=m_refs[mi].at[rows, :], dst_ref=land_refs[mi].at[2 * x + y, rows, :],
                    send_sem=send_sems.at[mi * 3 + j], recv_sem=recv_sems.at[mi * 3 + j],
                    device_id=(cx, cy, c), device_id_type=MESH).wait_send()
                got = land_refs[mi].at[2 * cx + cy, rows, :]
                pltpu.make_async_remote_copy(
                    src_ref=got, dst_ref=got, send_sem=send_sems.at[mi * 3 + j], recv_sem=recv_sems.at[mi * 3 + j],
                    device_id=(cx, cy, c), device_id_type=MESH).wait_recv()

    bufs = handle[2:]
    res = pl.pallas_call(
        body, name="gather_wait", out_shape=tuple(pltpu.HBM(b.shape, b.dtype) for b in bufs),
        in_specs=[HBM] * (2 * n) + [SEM, SEM, ANY], out_specs=tuple([HBM] * (2 * n)),
        input_output_aliases={i: i for i in range(2 * n)},
        compiler_params=pltpu.CompilerParams(has_side_effects=DATAFLOW),
    )(*bufs, handle[0], handle[1], after)
    return list(res[n:])


def _share_halves(gathered):
    n = N_MATS

    def body(*refs):
        out_refs = refs[n:2 * n]
        send_sems, recv_sems = refs[2 * n:]
        x, y, c = _my_place()
        sibling = (x, y, 1 - c)
        sends = []
        for mi in range(n):
            for j, (cx, cy) in enumerate(_other_chips(x, y)):
                mine = out_refs[mi].at[2 * cx + cy, _half_rows(mi, c), :]
                cp = pltpu.make_async_remote_copy(src_ref=mine, dst_ref=mine, send_sem=send_sems.at[mi * 3 + j],
                                                  recv_sem=recv_sems.at[mi * 3 + j], device_id=sibling, device_id_type=MESH)
                cp.start()
                sends.append(cp)
        for mi in range(n):
            for j, (cx, cy) in enumerate(_other_chips(x, y)):
                theirs = out_refs[mi].at[2 * cx + cy, _half_rows(mi, 1 - c), :]
                pltpu.make_async_remote_copy(src_ref=theirs, dst_ref=theirs, send_sem=send_sems.at[mi * 3 + j],
                                             recv_sem=recv_sems.at[mi * 3 + j], device_id=sibling, device_id_type=MESH).wait_recv()
        for cp in sends:
            cp.wait_send()

    return pl.pallas_call(
        body, name="share_halves",
        in_specs=[ANY] * n, out_specs=[ANY] * n,
        out_shape=[jax.ShapeDtypeStruct(g.shape, g.dtype) for g in gathered],
        input_output_aliases={i: i for i in range(n)},
        scratch_shapes=[pltpu.SemaphoreType.DMA((3 * n,)), pltpu.SemaphoreType.DMA((3 * n,))],
    )(*gathered)


def _peers(x, y, c):
    out = []
    for k in range(1, 8):
        px, py, pc = x ^ (k >> 2), y ^ ((k >> 1) & 1), c ^ (k & 1)
        out.append((k - 1, (px, py, pc), 4 * px + 2 * py + pc))
    return out


def _exchange_start(parts, name):
    n = len(parts)

    def body(*refs):
        p_refs, land_refs = refs[:n], refs[n:2 * n]
        send_sems, recv_sems, token = refs[2 * n], refs[2 * n + 1], refs[-1]
        x, y, c = _my_place()
        me = 4 * x + 2 * y + c
        for k, dev, peer in _peers(x, y, c):
            for mi in range(n):
                pltpu.make_async_remote_copy(
                    src_ref=p_refs[mi].at[peer], dst_ref=land_refs[mi].at[me], send_sem=send_sems.at[k * n + mi],
                    recv_sem=recv_sems.at[k * n + mi], device_id=dev, device_id_type=MESH).start()
        token[...] = jnp.zeros_like(token)

    hbm = [pltpu.HBM(p.shape, p.dtype) for p in parts]
    outs = pl.pallas_call(
        body, name=name + "_start",
        out_shape=(pltpu.SemaphoreType.DMA((7 * n,)), pltpu.SemaphoreType.DMA((7 * n,)), *hbm, *hbm,
                   jax.ShapeDtypeStruct((8, 128), F32)),
        in_specs=[HBM] * (2 * n), out_specs=(SEM, SEM, *[HBM] * (2 * n), pl.BlockSpec(memory_space=pltpu.VMEM)),
        input_output_aliases={i: 2 + i for i in range(2 * n)},
        compiler_params=pltpu.CompilerParams(has_side_effects=DATAFLOW),
    )(*[pltpu.with_memory_space_constraint(p, pltpu.HBM) for p in parts],
      *[pltpu.with_memory_space_constraint(lax.empty(p.shape, p.dtype), pltpu.HBM) for p in parts])
    return (name, outs[:-1]), outs[-1]


def _exchange_wait(handle, after):
    name, outs = handle
    n = (len(outs) - 2) // 2

    def body(*refs):
        p_refs, land_refs = refs[:n], refs[n:2 * n]
        send_sems, recv_sems = refs[2 * n], refs[2 * n + 1]
        x, y, c = _my_place()
        me = 4 * x + 2 * y + c
        for k, dev, peer in _peers(x, y, c):
            for mi in range(n):
                pltpu.make_async_remote_copy(
                    src_ref=p_refs[mi].at[peer], dst_ref=land_refs[mi].at[me], send_sem=send_sems.at[k * n + mi],
                    recv_sem=recv_sems.at[k * n + mi], device_id=dev, device_id_type=MESH).wait_send()
                slot = land_refs[mi].at[peer]
                pltpu.make_async_remote_copy(
                    src_ref=slot, dst_ref=slot, send_sem=send_sems.at[k * n + mi],
                    recv_sem=recv_sems.at[k * n + mi], device_id=dev, device_id_type=MESH).wait_recv()

    bufs = outs[2:]
    res = pl.pallas_call(
        body, name=name + "_wait", out_shape=tuple(pltpu.HBM(b.shape, b.dtype) for b in bufs),
        in_specs=[HBM] * (2 * n) + [SEM, SEM, ANY], out_specs=tuple([HBM] * (2 * n)),
        input_output_aliases={i: i for i in range(2 * n)},
        compiler_params=pltpu.CompilerParams(has_side_effects=DATAFLOW),
    )(*bufs, outs[0], outs[1], after)
    return list(res[n:])


def _swap_halves(halves, gvec):
    def body(*refs):
        g_refs, gv_ref = refs[:N_MATS], refs[N_MATS]
        out_refs, rg_ref = refs[N_MATS + 1:2 * N_MATS + 1], refs[2 * N_MATS + 1]
        send_sems, recv_sems = refs[2 * N_MATS + 2:]
        x, y, c = _my_place()
        me = 4 * x + 2 * y + c
        sends = []
        for mi in range(N_MATS):
            cp = pltpu.make_async_remote_copy(src_ref=g_refs[mi], dst_ref=out_refs[mi].at[c], send_sem=send_sems.at[mi],
                                              recv_sem=recv_sems.at[mi], device_id=(x, y, 1 - c), device_id_type=MESH)
            cp.start()
            sends.append(cp)
        for k, dev, peer in _peers(x, y, c):
            cp = pltpu.make_async_remote_copy(src_ref=gv_ref, dst_ref=rg_ref.at[me], send_sem=send_sems.at[N_MATS + k],
                                              recv_sem=recv_sems.at[N_MATS + k], device_id=dev, device_id_type=MESH)
            cp.start()
            sends.append(cp)
        for mi in range(N_MATS):
            got = out_refs[mi].at[1 - c]
            pltpu.make_async_remote_copy(src_ref=got, dst_ref=got, send_sem=send_sems.at[mi], recv_sem=recv_sems.at[mi],
                                         device_id=(x, y, 1 - c), device_id_type=MESH).wait_recv()
        for k, dev, peer in _peers(x, y, c):
            got = rg_ref.at[peer]
            pltpu.make_async_remote_copy(src_ref=got, dst_ref=got, send_sem=send_sems.at[N_MATS + k],
                                         recv_sem=recv_sems.at[N_MATS + k], device_id=dev, device_id_type=MESH).wait_recv()
        for cp in sends:
            cp.wait_send()

    outs = pl.pallas_call(
        body, name="swap_halves",
        in_specs=[ANY] * (N_MATS + 1), out_specs=[ANY] * (N_MATS + 1),
        out_shape=[jax.ShapeDtypeStruct((2, r // 2, c), F32) for r, c in SHARD_SHAPES]
        + [jax.ShapeDtypeStruct((8, 8, N_GAINS), F32)],
        scratch_shapes=[pltpu.SemaphoreType.DMA((N_MATS + 7,)), pltpu.SemaphoreType.DMA((N_MATS + 7,))],
    )(*halves, gvec)
    return outs[:N_MATS], outs[N_MATS]


def _set_slot(arr, block, idx):
    return lax.dynamic_update_slice(arr, block[None], (idx,) + (0,) * block.ndim)


PAD_RUNS = ((6304, 8352, 0), (5280, 6304, COL_Z), (672, 5280, COL_QKV), (0, 640, COL_LAT), (640, 672, COL_LAT + 704))
W_IN_SHARD = 2088


def _full_weights(gathered):
    def cols(a):
        return jnp.concatenate([a[s] for s in range(4)], axis=1)

    w_uq, w_ukv, w_pm, w_pd = [cols(a) for a in gathered[1:5]]
    w_out = gathered[5].reshape(D_MODEL, D_MODEL)
    g_in = gathered[0]
    pieces, at = [], 0
    for lo, hi, pad_lo in sorted(PAD_RUNS, key=lambda t: t[2]):
        if pad_lo > at:
            pieces.append(jnp.zeros((D_MODEL, pad_lo - at), g_in.dtype))
        for s in range(4):
            a_, b_ = max(lo, s * W_IN_SHARD), min(hi, (s + 1) * W_IN_SHARD)
            if a_ < b_:
                pieces.append(g_in[s][:, a_ - s * W_IN_SHARD:b_ - s * W_IN_SHARD])
        at = pad_lo + hi - lo
    pieces.append(jnp.zeros((D_MODEL, N_PAD - at), g_in.dtype))
    w_pad = jnp.concatenate(pieces, axis=1)
    z32 = jnp.zeros((Q_RANK, 32), w_uq.dtype)
    wuq_pad = jnp.concatenate([t for h in range(MLA_HEADS) for t in (w_uq[:, h * 96:(h + 1) * 96], z32)], axis=1)
    z64 = jnp.zeros((KV_RANK, 64), w_ukv.dtype)
    wk_pad = jnp.concatenate([t for h in range(MLA_HEADS) for t in (w_ukv[:, h * 128:h * 128 + 64], z64)], axis=1)
    wv = jnp.concatenate([w_ukv[:, h * 128 + 64:(h + 1) * 128] for h in range(MLA_HEADS)], axis=1)
    return w_pad, wuq_pad, wk_pad, wv, w_pm, w_pd, w_out


W_IN_LAT = 672


def _grad_parts_in_early(dw_early):
    def in_block(s, h):
        rows = slice(h * 512, (h + 1) * 512)
        out = []
        for lo, hi, pad_lo in sorted(PAD_RUNS):
            a_, b_ = max(lo, s * W_IN_SHARD), min(hi, (s + 1) * W_IN_SHARD)
            if a_ < b_:
                out.append(jnp.zeros((512, b_ - a_), dw_early.dtype) if pad_lo >= COL_LAT
                           else dw_early[rows, pad_lo + a_ - lo:pad_lo + b_ - lo])
        return jnp.concatenate(out, axis=1)

    return jnp.stack([in_block(s, h) for s in range(4) for h in range(2)])


def _grad_parts_in_late(dw_late):
    cols = jnp.concatenate([dw_late[:, 0:640], dw_late[:, 704:736]], axis=1)
    zero = jnp.zeros((512, W_IN_LAT), dw_late.dtype)
    return jnp.stack([cols[0:512], cols[512:1024]] + [zero] * 6)


def _col_blocks(m):
    r, c = m.shape[0] // 2, m.shape[1] // 4
    return jnp.stack([m[h * r:(h + 1) * r, s * c:(s + 1) * c] for s in range(4) for h in range(2)])


def _grad_parts_mla(dwuq_pad, dwk_pad, dwv):
    d_uq = jnp.concatenate([dwuq_pad[:, h * 128:h * 128 + 96] for h in range(MLA_HEADS)], axis=1)
    d_ukv = jnp.concatenate([t for h in range(MLA_HEADS) for t in (dwk_pad[:, h * 128:h * 128 + 64], dwv[:, h * 64:(h + 1) * 64])],
                            axis=1)
    return [_col_blocks(d_uq), _col_blocks(d_ukv)]


def _rope_tables(positions, token=None):
    pos = positions.reshape(SEQ).astype(F32)
    if token is not None:
        pos = pos + token[0, 0]
    lane = jnp.arange(128)

    def table(rot, first, period):
        inv = ROPE_THETA ** (-jnp.arange(0, rot, 2, dtype=F32) / rot)
        half = rot // 2
        off = lane % period - first
        in1, in2 = (off >= 0) & (off < half), (off >= half) & (off < rot)
        inv_lane = jnp.where(in1 | in2, inv[jnp.clip(off % half, 0, half - 1)], 0.0)
        sign = jnp.where(in1, -1.0, 1.0).astype(F32)
        ang = pos[:, None] * inv_lane[None, :]
        return jnp.cos(ang), jnp.sin(ang) * sign[None, :]

    return table(32, 64, 128), table(16, 0, 64)


class _Links:
    def __init__(self, mats, chip, me):
        landing = [_set_slot(lax.empty((4,) + m.shape, m.dtype), m, chip) for m in mats]
        self.gather, self.token = _gather_start(mats, landing)
        self.me, self.sent, self.handles, self.sums = me, {}, {}, {}

    def weights(self, after):
        return _share_halves(_gather_wait(self.gather, after))

    def send(self, blocks, name):
        self.sent[name] = blocks
        self.handles[name], token = _exchange_start(blocks, name)
        return token

    def collect(self, name, after, parts):
        recv = _exchange_wait(self.handles[name], after)
        for r, own, part in zip(recv, self.sent[name], parts):
            self.sums[part] = _sum_parts(r, own, self.me, 64, "sum_grad_" + part)
        return tuple(self.sums[part] for part in parts)


def _device_grads(x, positions, target, gains, links):
    pre_g, q_g, kv_g, post_g = gains
    (mc, ms), (dc, ds) = _rope_tables(positions, links.token)
    h, h_t = _prenorm_fwd(x, pre_g, links.token)
    w_pad, wuq_pad, wk_pad, wv, w_pm, w_pd, w_out = _full_weights(links.weights(h))

    p = _matmul(h, w_pad, "nn", F32, 1024, 1408, 1024, "in_proj")
    cqn, ckvn, q, k, v = _mla_prep_fwd(p, q_g, kv_g, wuq_pad, wk_pad, wv, mc, ms)
    ya, lse_m = _mla_flash_fwd(q, k, v)
    qkv = [_dil_prep_fwd(p, dc, ds, g) for g in range(3)]
    o_g, l_g = zip(*[_dil_attn_fwd2(qkv[g], g) for g in range(3)])
    (dp, dy, mg, dt, ua, dpa, ud, dpd, dya, dyd, yd, lse_d, loss_cols, dg_post) = _tail(
        p, ya, o_g, l_g, x, target, w_pm, w_pd, w_out, post_g)

    for g in range(3):
        dqkv = _dil_attn_bwd2(qkv[g], dyd, yd, lse_d, g)
        dp = _dil_prep_bwd(dp, dqkv, dc, ds, g)
    dw_early = _matmul(h_t, dp, "nn", BF16, 1024, 1536, 2048, "dw_in_early", b_cols=(0, COL_LAT // 1536))
    dwpm = _matmul(ua, dpa, "tn", BF16, 512, 1024, 512, "dw_proj_mla")
    dwpd = _matmul(ud, dpd, "tn", BF16, 512, 1024, 512, "dw_proj_dil")
    dwout = _matmul(mg, dt, "tn", BF16, 1024, 1024, 512, "dw_out")
    token = links.send([_grad_parts_in_early(dw_early), _col_blocks(dwpm), _col_blocks(dwpd),
                        dwout.reshape(8, 128, D_MODEL)], "exchange_early")

    dq, dk, dv = _mla_flash_bwd(q, k, v, ya, dya, lse_m, token)
    dp, dqb, dkb, dvb, dg_q, dg_kv = _mla_prep_bwd(dp, p, dq, dk, dv, q_g, kv_g, wuq_pad, wk_pad, wv, mc, ms)
    dwuq_pad = _matmul(cqn, dqb, "tn", BF16, Q_RANK, 1024, 512, "dw_uq")
    dwk_pad = _matmul(ckvn, dkb, "tn", BF16, KV_RANK, 1024, 512, "dw_k")
    dwv = _matmul(ckvn, dvb, "tn", BF16, KV_RANK, 512, 512, "dw_v")
    dw_late = _matmul(h_t, dp, "nn", BF16, 1024, N_LAT, 2048, "dw_in_late", b_cols=(COL_LAT // N_LAT, 1))
    token = links.send([_grad_parts_in_late(dw_late)] + _grad_parts_mla(dwuq_pad, dwk_pad, dwv), "exchange_late")
    early = links.collect("exchange_early", dw_late, ("in_early", "pm", "pd", "out"))

    dh = _matmul(dp, w_pad, "nt", F32, 1024, 1024, 1408, "dh", (token,) + tuple(early))
    grad_x, dg_pre = _prenorm_bwd(x, dh, dy, pre_g)
    links.collect("exchange_late", grad_x, ("in_late", "uq", "ukv"))

    gvec = jnp.concatenate([dg_pre, dg_q, dg_kv, dg_post], axis=1)
    return loss_cols, grad_x, gvec


def kernel(x, positions, pre_norm_g, w_in, q_norm_g, w_uq, kv_norm_g, w_ukv, w_proj_mla, w_proj_dil, w_out, post_norm_g, loss_target, m_pre_norm_g, m_w_in, m_q_norm_g, m_w_uq, m_kv_norm_g, m_w_ukv, m_w_proj_mla, m_w_proj_dil, m_w_out, m_post_norm_g, v_pre_norm_g, v_w_in, v_q_norm_g, v_w_uq, v_kv_norm_g, v_w_ukv, v_w_proj_mla, v_w_proj_dil, v_w_out, v_post_norm_g):
    xi, yi, ci = _my_place()
    chip, me = 2 * xi + yi, 4 * xi + 2 * yi + ci
    mats = [w.reshape(w.shape[1:]).astype(BF16) for w in (w_in, w_uq, w_ukv, w_proj_mla, w_proj_dil, w_out)]
    links = _Links(mats, chip, me)
    gains = (pre_norm_g, q_norm_g, kv_norm_g, post_norm_g)
    loss_cols, grad_x, gvec = _device_grads(x[0], positions, loss_target[0], gains, links)

    loss = lax.psum(jnp.sum(loss_cols) * (0.5 / D_MODEL), ("x", "y", "c"))

    sums = links.sums
    in_e = sums["in_early"]
    half_in = jnp.concatenate([in_e[:, :W_IN_LAT] + jnp.where(chip == 0, sums["in_late"], 0.0), in_e[:, W_IN_LAT:]], axis=1)
    halves = [half_in, sums["uq"], sums["ukv"], sums["pm"], sums["pd"], sums["out"]]
    gvec8 = jnp.pad(gvec, ((0, 7), (0, 0)))
    swapped, recv_gains = _swap_halves(halves, gvec8)
    g_gains = _sum_parts(recv_gains, gvec8, me, 8, "sum_gain_parts")[0:1]
    g_mats = [_set_slot(s, hf, ci).reshape((1,) + shp) for s, hf, shp in zip(swapped, halves, SHARD_SHAPES)]

    off = [0, 1024, 1408, 1664, 2688]
    g_gain = [g_gains[:, off[i]:off[i + 1]] for i in range(4)]
    grads = [g_gain[0], g_mats[0], g_gain[1], g_mats[1], g_gain[2], g_mats[2], g_mats[3], g_mats[4], g_mats[5], g_gain[3]]
    ws = [pre_norm_g, w_in, q_norm_g, w_uq, kv_norm_g, w_ukv, w_proj_mla, w_proj_dil, w_out, post_norm_g]
    ms = [m_pre_norm_g, m_w_in, m_q_norm_g, m_w_uq, m_kv_norm_g, m_w_ukv, m_w_proj_mla, m_w_proj_dil, m_w_out, m_post_norm_g]
    vs = [v_pre_norm_g, v_w_in, v_q_norm_g, v_w_uq, v_kv_norm_g, v_w_ukv, v_w_proj_mla, v_w_proj_dil, v_w_out, v_post_norm_g]
    deltas, new_m, new_v = [], [], []
    for i, (w, g, m, v) in enumerate(zip(ws, grads, ms, vs)):
        if w.shape[-1] % 128 and w.shape[-2] % 128 == 0:
            g = jnp.swapaxes(g, 1, 2)
            grads[i] = jnp.swapaxes(g, 1, 2)
            d_, m_, v_ = [jnp.swapaxes(o, 1, 2) for o in
                          _adamw(jnp.swapaxes(w, 1, 2), g, jnp.swapaxes(m, 1, 2), jnp.swapaxes(v, 1, 2), f"adamw_{i}")]
        else:
            d_, m_, v_ = _adamw(w, g, m, v, f"adamw_{i}")
        deltas.append(d_)
        new_m.append(m_)
        new_v.append(v_)
    return (loss, grad_x.reshape(x.shape), *grads, *deltas, *new_m, *new_v)
```

```python
import jax
import jax.numpy as jnp
from jax import lax
from jax.experimental import pallas as pl
from jax.experimental.pallas import tpu as pltpu

F32 = jnp.float32
BF16 = jnp.bfloat16

SEQ = 4096
D_MODEL = 1024
EPS = 1e-6
ROPE_THETA = 500000.0
MLA_HEADS = 8
Q_RANK = 384
KV_RANK = 256
MLA_SCALE = 96.0 ** -0.5
MLA_ROPE_HALF = 16
DIL_DILATIONS = (1, 4, 16)
DIL_ROPE_HALF = 8
DIL_SCALE = 0.125
BAND = 128

N_LAT = 768
COL_Z, COL_QKV, COL_LAT = 2048, 3072, 7680
N_PAD = 8448
IN_SPLITS = (384, 256, 32, 4608, 512, 512, 1024, 1024)

SHARD_SHAPES = ((2088, 1024), (384, 192), (256, 256), (512, 256), (512, 256), (256, 1024))
N_MATS = len(SHARD_SHAPES)
N_GAINS = 2688
N_GVEC = N_GAINS + 128

ADAM_LR, ADAM_B1, ADAM_B2, ADAM_EPS, ADAM_WD, ADAM_STEP = 0.001, 0.9, 0.999, 1e-08, 0.01, 10

VMEM_LIMIT = 56 * 1024 * 1024
NEG = -1e30
MESH = pl.DeviceIdType.MESH


def _cparams(**kw):
    return pltpu.CompilerParams(vmem_limit_bytes=VMEM_LIMIT, **kw)


def _dot(a, b, dims):
    return lax.dot_general(a, b, (dims, ((), ())), preferred_element_type=F32)


def _nn(a, b):
    return _dot(a, b, ((1,), (0,)))


def _nt(a, b):
    return _dot(a, b, ((1,), (1,)))


def _tn(a, b):
    return _dot(a, b, ((0,), (0,)))


def _rope_lanes(shape, half, period, first):
    lane = lax.broadcasted_iota(jnp.int32, shape, len(shape) - 1) % period
    return (lane >= first) & (lane < first + half), (lane >= first + half) & (lane < first + 2 * half)


def _rope_fwd(x, c, s, half, lanes):
    x1, _ = lanes
    return x * c + jnp.where(x1, pltpu.roll(x, 128 - half, 1), pltpu.roll(x, half, 1)) * s


def _rope_bwd(g, c, s, half, lanes):
    x1, x2 = lanes
    gs = g * s
    return g * c + jnp.where(x2, pltpu.roll(gs, half, 1), jnp.where(x1, pltpu.roll(gs, 128 - half, 1), 0.0))


def _sigmoid(x):
    return 1.0 / (1.0 + jnp.exp(-x))


def _after(token):
    tokens = [t for t in (token if isinstance(token, (tuple, list)) else [token]) if t is not None]
    return tokens, [pl.BlockSpec(memory_space=pl.ANY)] * len(tokens)


def _matmul(a, b, mode, out_dtype, tm, tn, tk, name, token=None, b_cols=None, a_cols=None):
    after, after_specs = _after(token)
    if mode == "nn":
        (m, k), n = a.shape, b.shape[1]
        first = 0
        if b_cols is not None:
            first, n = b_cols[0], b_cols[1] * tn
        a_spec = pl.BlockSpec((tm, tk), lambda j, i, kk: (i, kk))
        b_spec = pl.BlockSpec((tk, tn), lambda j, i, kk: (kk, j + first))
        dot = _nn
    elif mode == "nt":
        (m, k), n = a.shape, b.shape[0]
        a_spec = pl.BlockSpec((tm, tk), lambda j, i, kk: (i, kk))
        b_spec = pl.BlockSpec((tn, tk), lambda j, i, kk: (j, kk))
        dot = _nt
    else:
        (k, m), n = a.shape, b.shape[1]
        first = 0
        if a_cols is not None:
            first, m = a_cols[0], a_cols[1] * tm
        a_spec = pl.BlockSpec((tk, tm), lambda j, i, kk: (kk, i + first))
        b_spec = pl.BlockSpec((tk, tn), lambda j, i, kk: (kk, j))
        dot = _tn
    assert m % tm == 0 and n % tn == 0 and k % tk == 0, (name, m, n, k, tm, tn, tk)
    nk = k // tk

    def body(a_ref, b_ref, *rest):
        o_ref, acc_ref = rest[-2:]
        kk = pl.program_id(2)
        part = dot(a_ref[...], b_ref[...])

        @pl.when(kk == 0)
        def _():
            acc_ref[...] = part

        @pl.when(kk > 0)
        def _():
            acc_ref[...] += part

        @pl.when(kk == nk - 1)
        def _():
            o_ref[...] = acc_ref[...].astype(o_ref.dtype)

    return pl.pallas_call(
        body, name=name, grid=(n // tn, m // tm, nk),
        in_specs=[a_spec, b_spec] + after_specs,
        out_specs=pl.BlockSpec((tm, tn), lambda j, i, kk: (i, j)),
        out_shape=jax.ShapeDtypeStruct((m, n), out_dtype),
        scratch_shapes=[pltpu.VMEM((tm, tn), F32)],
        compiler_params=_cparams(),
    )(a, b, *after)


def _prenorm_fwd(x, g, token=None):
    tm = 512
    after, after_specs = _after(token)

    def body(x_ref, g_ref, *rest):
        xv = x_ref[...]
        r = lax.rsqrt(jnp.mean(xv * xv, axis=-1, keepdims=True) + EPS)
        rest[-1][...] = (xv * r * g_ref[...]).astype(BF16)

    return pl.pallas_call(
        body, name="prenorm_fwd", grid=(SEQ // tm,),
        in_specs=[pl.BlockSpec((tm, D_MODEL), lambda i: (i, 0)), pl.BlockSpec((1, D_MODEL), lambda i: (0, 0))] + after_specs,
        out_specs=pl.BlockSpec((tm, D_MODEL), lambda i: (i, 0)),
        out_shape=jax.ShapeDtypeStruct((SEQ, D_MODEL), BF16),
    )(x, g, *after)


def _prenorm_bwd(x, dh, dy, g):
    tm = 512

    def body(x_ref, dh_ref, dy_ref, g_ref, gx_ref, dg_ref):
        xv = x_ref[...]
        r = lax.rsqrt(jnp.mean(xv * xv, axis=-1, keepdims=True) + EPS)
        n = xv * r
        dhv = dh_ref[...]
        dn = dhv * g_ref[...]
        gx_ref[...] = dy_ref[...] + r * (dn - n * jnp.mean(dn * n, axis=-1, keepdims=True))
        part = jnp.sum(dhv * n, axis=0, keepdims=True)

        @pl.when(pl.program_id(0) == 0)
        def _():
            dg_ref[...] = part

        @pl.when(pl.program_id(0) > 0)
        def _():
            dg_ref[...] += part

    row = pl.BlockSpec((tm, D_MODEL), lambda i: (i, 0))
    vec = pl.BlockSpec((1, D_MODEL), lambda i: (0, 0))
    return pl.pallas_call(
        body, name="prenorm_bwd", grid=(SEQ // tm,),
        in_specs=[row, row, row, vec], out_specs=[row, vec],
        out_shape=[jax.ShapeDtypeStruct((SEQ, D_MODEL), F32), jax.ShapeDtypeStruct((1, D_MODEL), F32)],
        compiler_params=_cparams(),
    )(x, dh, dy, g)


def _mla_prep_fwd(p, qg, kvg, wuq, wk, wv, rc, rs):
    tm = 512

    def body(lat_ref, qg_ref, kvg_ref, wuq_ref, wk_ref, wv_ref, c_ref, s_ref,
             cqn_ref, ckvn_ref, q_ref, k_ref, v_ref):
        c, s = c_ref[...], s_ref[...]
        lanes = _rope_lanes((tm, 128), MLA_ROPE_HALF, 128, 64)
        cq = lat_ref[:, 0:Q_RANK]
        r1 = lax.rsqrt(jnp.mean(cq * cq, axis=-1, keepdims=True) + EPS)
        cqn = (cq * r1 * qg_ref[...]).astype(BF16)
        cqn_ref[...] = cqn
        q = _nn(cqn, wuq_ref[...])
        for h in range(MLA_HEADS):
            sl = slice(h * 128, (h + 1) * 128)
            q_ref[:, sl] = (_rope_fwd(q[:, sl], c, s, MLA_ROPE_HALF, lanes) * MLA_SCALE).astype(BF16)
        ckv = lat_ref[:, Q_RANK:Q_RANK + KV_RANK]
        r2 = lax.rsqrt(jnp.mean(ckv * ckv, axis=-1, keepdims=True) + EPS)
        ckvn = (ckv * r2 * kvg_ref[...]).astype(BF16)
        ckvn_ref[...] = ckvn
        krr = _rope_fwd(lat_ref[:, Q_RANK + KV_RANK:N_LAT], c, s, MLA_ROPE_HALF, lanes)
        kn = _nn(ckvn, wk_ref[...])
        for h in range(MLA_HEADS):
            sl = slice(h * 128, (h + 1) * 128)
            k_ref[:, sl] = (kn[:, sl] + krr).astype(BF16)
        v_ref[...] = _nn(ckvn, wv_ref[...]).astype(BF16)

    def full(shape):
        return pl.BlockSpec(shape, lambda i: (0, 0))

    def rows(w):
        return pl.BlockSpec((tm, w), lambda i: (i, 0))

    return pl.pallas_call(
        body, name="mla_prep_fwd", grid=(SEQ // tm,),
        in_specs=[pl.BlockSpec((tm, N_LAT), lambda i: (i, COL_LAT // N_LAT)),
                  full((1, Q_RANK)), full((1, KV_RANK)), full((Q_RANK, 1024)), full((KV_RANK, 1024)),
                  full((KV_RANK, 512)), rows(128), rows(128)],
        out_specs=[rows(Q_RANK), rows(KV_RANK), rows(1024), rows(1024), rows(512)],
        out_shape=[jax.ShapeDtypeStruct((SEQ, Q_RANK), BF16), jax.ShapeDtypeStruct((SEQ, KV_RANK), BF16),
                   jax.ShapeDtypeStruct((SEQ, 1024), BF16), jax.ShapeDtypeStruct((SEQ, 1024), BF16),
                   jax.ShapeDtypeStruct((SEQ, 512), BF16)],
        compiler_params=_cparams(),
    )(p, qg, kvg, wuq, wk, wv, rc, rs)


def _mla_prep_bwd(dp_in, p, dq, dk, dv, qg, kvg, wuq, wk, wv, rc, rs):
    tm = 512

    def body(dp_any, lat_ref, dq_ref, dk_ref, dv_ref, qg_ref, kvg_ref, wuq_ref, wk_ref, wv_ref,
             c_ref, s_ref, dp_ref, dqb_ref, dkb_ref, dvb_ref, dgq_ref, dgkv_ref):
        del dp_any
        c, s = c_ref[...], s_ref[...]
        lanes = _rope_lanes((tm, 128), MLA_ROPE_HALF, 128, 64)
        lane = lax.broadcasted_iota(jnp.int32, (tm, 128), 1)
        dkr = jnp.zeros((tm, 128), F32)
        for h in range(MLA_HEADS):
            sl = slice(h * 128, (h + 1) * 128)
            dqb_ref[:, sl] = _rope_bwd(dq_ref[:, sl] * MLA_SCALE, c, s, MLA_ROPE_HALF, lanes).astype(BF16)
            dkh = dk_ref[:, sl]
            dkr = dkr + dkh
            dkb_ref[:, sl] = jnp.where(lane < 64, dkh, 0.0).astype(BF16)
        dkr = jnp.where((lane >= 64) & (lane < 96), dkr, 0.0)
        dkr = _rope_bwd(dkr, c, s, MLA_ROPE_HALF, lanes)
        dvb = dv_ref[...].astype(BF16)
        dvb_ref[...] = dvb

        cq = lat_ref[:, 0:Q_RANK]
        r1 = lax.rsqrt(jnp.mean(cq * cq, axis=-1, keepdims=True) + EPS)
        n1 = cq * r1
        dcqn = _nt(dqb_ref[...], wuq_ref[...])
        dn1 = dcqn * qg_ref[...]
        dcq = r1 * (dn1 - n1 * jnp.mean(dn1 * n1, axis=-1, keepdims=True))
        pq = jnp.sum(dcqn * n1, axis=0, keepdims=True)

        ckv = lat_ref[:, Q_RANK:Q_RANK + KV_RANK]
        r2 = lax.rsqrt(jnp.mean(ckv * ckv, axis=-1, keepdims=True) + EPS)
        n2 = ckv * r2
        dckvn = _nt(dkb_ref[...], wk_ref[...]) + _nt(dvb, wv_ref[...])
        dn2 = dckvn * kvg_ref[...]
        dckv = r2 * (dn2 - n2 * jnp.mean(dn2 * n2, axis=-1, keepdims=True))
        pkv = jnp.sum(dckvn * n2, axis=0, keepdims=True)

        dp_ref[:, 0:Q_RANK] = dcq.astype(BF16)
        dp_ref[:, Q_RANK:Q_RANK + KV_RANK] = dckv.astype(BF16)
        dp_ref[:, Q_RANK + KV_RANK:N_LAT] = dkr.astype(BF16)

        @pl.when(pl.program_id(0) == 0)
        def _():
            dgq_ref[...] = pq
            dgkv_ref[...] = pkv

        @pl.when(pl.program_id(0) > 0)
        def _():
            dgq_ref[...] += pq
            dgkv_ref[...] += pkv

    def full(shape):
        return pl.BlockSpec(shape, lambda i: (0, 0))

    def rows(w):
        return pl.BlockSpec((tm, w), lambda i: (i, 0))

    lat = pl.BlockSpec((tm, N_LAT), lambda i: (i, COL_LAT // N_LAT))
    return pl.pallas_call(
        body, name="mla_prep_bwd", grid=(SEQ // tm,),
        in_specs=[pl.BlockSpec(memory_space=pl.ANY), lat, rows(1024), rows(1024), rows(512),
                  full((1, Q_RANK)), full((1, KV_RANK)), full((Q_RANK, 1024)), full((KV_RANK, 1024)),
                  full((KV_RANK, 512)), rows(128), rows(128)],
        out_specs=[lat, rows(1024), rows(1024), rows(512), full((1, Q_RANK)), full((1, KV_RANK))],
        out_shape=[jax.ShapeDtypeStruct((SEQ, N_PAD), BF16), jax.ShapeDtypeStruct((SEQ, 1024), BF16),
                   jax.ShapeDtypeStruct((SEQ, 1024), BF16), jax.ShapeDtypeStruct((SEQ, 512), BF16),
                   jax.ShapeDtypeStruct((1, Q_RANK), F32), jax.ShapeDtypeStruct((1, KV_RANK), F32)],
        input_output_aliases={0: 0},
        compiler_params=_cparams(),
    )(dp_in, p, dq, dk, dv, qg, kvg, wuq, wk, wv, rc, rs)


FLASH_T = 1024


def _head_half(shape, hh):
    lane = lax.broadcasted_iota(jnp.int32, shape, 1)
    return (lane < 64) if hh == 0 else (lane >= 64)


def _diag_keep(nr, nk):
    row = lax.broadcasted_iota(jnp.int32, (nr, nk), 0)
    col = lax.broadcasted_iota(jnp.int32, (nr, nk), 1)
    return row + (nk - nr) >= col


def _tri_steps(nb, q_major):
    if q_major:
        pairs = [(i, kb) for i in range(nb) for kb in range(i + 1)]
    else:
        pairs = [(i, kb) for kb in range(nb) for i in range(kb, nb)]
    return jnp.asarray([p[0] for p in pairs], jnp.int32), jnp.asarray([p[1] for p in pairs], jnp.int32)


def _mla_flash_fwd(q, k, v):
    t = FLASH_T
    nb = SEQ // t
    qtab, ktab = _tri_steps(nb, True)

    def body(qi_ref, ki_ref, q_ref, k_ref, v_ref, o_ref, lse_ref, m_scr, l_scr, acc_scr):
        step = pl.program_id(1)
        i, kb = qi_ref[step], ki_ref[step]

        @pl.when(kb == 0)
        def _():
            m_scr[...] = jnp.full_like(m_scr, NEG)
            l_scr[...] = jnp.zeros_like(l_scr)
            acc_scr[...] = jnp.zeros_like(acc_scr)

        def update(r0, nr, nk, diagonal):
            rs = slice(r0, r0 + nr)
            vv = v_ref[0:nk, :]
            for hh in range(2):
                sl = slice(hh * 128, (hh + 1) * 128)
                s = _nt(q_ref[rs, sl], k_ref[0:nk, sl])
                if diagonal:
                    s = jnp.where(_diag_keep(nr, nk), s, NEG)
                m_prev = m_scr[hh, rs, :]
                m_new = jnp.maximum(m_prev, jnp.max(s, axis=-1, keepdims=True))
                pr = jnp.exp(s - jnp.tile(m_new, (1, nk // 128)))
                alpha = jnp.exp(m_prev - m_new)
                l_scr[hh, rs, :] = alpha * l_scr[hh, rs, :] + jnp.sum(pr, axis=-1, keepdims=True)
                acc_scr[hh, rs, :] = alpha * acc_scr[hh, rs, :] + _nn(pr.astype(BF16), vv)
                m_scr[hh, rs, :] = m_new

        @pl.when(kb < i)
        def _():
            update(0, t, t, False)

        @pl.when(kb == i)
        def _():
            update(0, t // 2, t // 2, True)
            update(t // 2, t // 2, t, True)
            o0 = acc_scr[0] / l_scr[0]
            o1 = acc_scr[1] / l_scr[1]
            o_ref[...] = jnp.where(_head_half((t, 128), 0), o0, o1)
            for hh in range(2):
                lse_ref[:, hh * 128:(hh + 1) * 128] = m_scr[hh] + jnp.log(l_scr[hh])

    grid_spec = pltpu.PrefetchScalarGridSpec(
        num_scalar_prefetch=2, grid=(4, qtab.shape[0]),
        in_specs=[pl.BlockSpec((t, 256), lambda j, s, qi, ki: (qi[s], j)),
                  pl.BlockSpec((t, 256), lambda j, s, qi, ki: (ki[s], j)),
                  pl.BlockSpec((t, 128), lambda j, s, qi, ki: (ki[s], j))],
        out_specs=[pl.BlockSpec((t, 128), lambda j, s, qi, ki: (qi[s], j)),
                   pl.BlockSpec((t, 256), lambda j, s, qi, ki: (qi[s], j))],
        scratch_shapes=[pltpu.VMEM((2, t, 128), F32), pltpu.VMEM((2, t, 128), F32), pltpu.VMEM((2, t, 128), F32)])
    return pl.pallas_call(
        body, name="mla_flash_fwd", grid_spec=grid_spec,
        out_shape=[jax.ShapeDtypeStruct((SEQ, 512), F32), jax.ShapeDtypeStruct((SEQ, 1024), F32)],
        compiler_params=_cparams(),
    )(qtab, ktab, q, k, v)


def _mla_flash_bwd(q, k, v, o, do, lse, token=None):
    t = FLASH_T
    nb = SEQ // t
    qtab, ktab = _tri_steps(nb, False)
    after, after_specs = _after(token)

    def body(qi_ref, ki_ref, q_ref, k_ref, v_ref, o_ref, do_ref, lse_ref, *rest):
        dq_ref, dk_ref, dv_ref, dk_scr, dv_scr = rest[-5:]
        step = pl.program_id(1)
        i, kb = qi_ref[step], ki_ref[step]

        @pl.when(step == 0)
        def _():
            dq_ref[...] = jnp.zeros_like(dq_ref)

        @pl.when(i == kb)
        def _():
            dk_scr[...] = jnp.zeros_like(dk_scr)
            dv_scr[...] = jnp.zeros_like(dv_scr)

        def update(r0, nr, nk, diagonal):
            rs = slice(r0, r0 + nr)
            vv = v_ref[0:nk, :]
            ov = o_ref[rs, :]
            dov = do_ref[rs, :]
            rows = pl.ds(pl.multiple_of(i * t + r0, t // 2), nr)
            for hh in range(2):
                sl = slice(hh * 128, (hh + 1) * 128)
                qh, kh = q_ref[rs, sl], k_ref[0:nk, sl]
                s = _nt(qh, kh)
                if diagonal:
                    s = jnp.where(_diag_keep(nr, nk), s, NEG)
                pr = jnp.exp(s - jnp.tile(lse_ref[rs, sl], (1, nk // 128)))
                dom = jnp.where(_head_half((nr, 128), hh), dov, 0.0)
                domb = dom.astype(BF16)
                dv_scr[0:nk, :] += _tn(pr.astype(BF16), domb)
                dpr = _nt(domb, vv)
                delta = jnp.sum(dom * ov, axis=-1, keepdims=True)
                ds = (pr * (dpr - delta)).astype(BF16)
                dq_ref[rows, sl] += _nn(ds, kh)
                dk_scr[hh, 0:nk, :] += _tn(ds, qh)

        @pl.when(i > kb)
        def _():
            update(0, t, t, False)

        @pl.when(i == kb)
        def _():
            update(0, t // 2, t // 2, True)
            update(t // 2, t // 2, t, True)

        @pl.when(i == nb - 1)
        def _():
            dk_ref[:, 0:128] = dk_scr[0]
            dk_ref[:, 128:256] = dk_scr[1]
            dv_ref[...] = dv_scr[...]

    qi_map = lambda j, s, qi, ki: (qi[s], j)
    ki_map = lambda j, s, qi, ki: (ki[s], j)
    grid_spec = pltpu.PrefetchScalarGridSpec(
        num_scalar_prefetch=2, grid=(4, qtab.shape[0]),
        in_specs=[pl.BlockSpec((t, 256), qi_map), pl.BlockSpec((t, 256), ki_map), pl.BlockSpec((t, 128), ki_map),
                  pl.BlockSpec((t, 128), qi_map), pl.BlockSpec((t, 128), qi_map), pl.BlockSpec((t, 256), qi_map)]
        + after_specs,
        out_specs=[pl.BlockSpec((SEQ, 256), lambda j, s, qi, ki: (0, j)), pl.BlockSpec((t, 256), ki_map),
                   pl.BlockSpec((t, 128), ki_map)],
        scratch_shapes=[pltpu.VMEM((2, t, 128), F32), pltpu.VMEM((t, 128), F32)])
    return pl.pallas_call(
        body, name="mla_flash_bwd", grid_spec=grid_spec,
        out_shape=[jax.ShapeDtypeStruct((SEQ, 1024), F32), jax.ShapeDtypeStruct((SEQ, 1024), F32),
                   jax.ShapeDtypeStruct((SEQ, 512), F32)],
        compiler_params=_cparams(),
    )(qtab, ktab, q, k, v, o, do, lse, *after)


DIL_UNROLL = 4


def _strided(start, size, d):
    return pl.ds(start, size) if d == 1 else pl.ds(start, size, stride=d)


def _dil_prep_fwd(p, rc, rs, g):
    d = DIL_DILATIONS[g]
    sub_len = SEQ // d
    ch = min(sub_len, 512)

    def body(p_ref, c_ref, s_ref, o_ref, x_scr):
        tq = pl.program_id(0)
        lanes = _rope_lanes((ch, 128), DIL_ROPE_HALF, 64, 0)
        o_ref[0, 0:BAND, :] = jnp.zeros((BAND, 128), BF16)

        @pl.when(tq < 2)
        def _():
            mult = jnp.where(tq == 0, DIL_SCALE, 1.0).astype(F32)
            for c0 in range(0, SEQ, ch):
                rows = pl.ds(c0, ch)
                x_scr[rows, :] = _rope_fwd(p_ref[rows, :], c_ref[rows, :] * mult, s_ref[rows, :] * mult, DIL_ROPE_HALF, lanes)

        def gather(src):
            for r in range(d):
                for c0 in range(0, sub_len, ch):
                    at = BAND + r * sub_len + c0
                    o_ref[0, at:at + ch, :] = src[_strided(r + c0 * d, ch, d), :].astype(BF16)

        @pl.when(tq < 2)
        def _():
            gather(x_scr)

        @pl.when(tq == 2)
        def _():
            gather(p_ref)

    tab = pl.BlockSpec((SEQ, 128), lambda tq, pr: (0, 0))
    return pl.pallas_call(
        body, name=f"dil_prep_fwd_g{g}", grid=(3, 4),
        in_specs=[pl.BlockSpec((SEQ, 128), lambda tq, pr: (0, COL_QKV // 128 + (tq * 3 + g) * 4 + pr)), tab, tab],
        out_specs=pl.BlockSpec((1, BAND + SEQ, 128), lambda tq, pr: (tq, 0, pr)),
        out_shape=jax.ShapeDtypeStruct((3, BAND + SEQ, 512), BF16),
        scratch_shapes=[pltpu.VMEM((SEQ, 128), F32)],
        compiler_params=_cparams(),
    )(p, rc, rs)


DIL_ST = 1024
DIL_NB = DIL_ST // BAND


def _band_keep(g, b, t):
    nbs = SEQ // DIL_DILATIONS[g] // BAND
    row = lax.broadcasted_iota(jnp.int32, (BAND, 2 * BAND), 0)
    col = lax.broadcasted_iota(jnp.int32, (BAND, 2 * BAND), 1)
    cur = (col >= BAND) & (row >= col - BAND)
    prev = (col < BAND) & (col >= row)
    if nbs >= DIL_NB:
        if b > 0:
            return cur | prev
        return cur | (prev & ((t * DIL_NB) % nbs != 0))
    return cur | prev if b % nbs else cur


def _dil_tok(g, b, t):
    d = DIL_DILATIONS[g]
    nbs = SEQ // d // BAND
    gb = t * DIL_NB + b
    return _strided((gb % nbs) * BAND * d + gb // nbs, BAND, d)


def _dil_attn_fwd2(qkv, g):
    def body(q_ref, k_ref, v_ref, o_ref, l_ref, s_scr, p_scr, o_scr):
        t = pl.program_id(1)
        base = t * DIL_ST
        half0 = _head_half((DIL_ST, 128), 0)
        lse_h = []
        for hh in range(2):
            half = _head_half((BAND, 128), hh)
            for b in range(DIL_NB):
                qv = q_ref[0, pl.ds(pl.multiple_of(base + (b + 1) * BAND, BAND), BAND), :]
                k2 = k_ref[0, pl.ds(pl.multiple_of(base + b * BAND, BAND), 2 * BAND), :]
                sb = _nt(jnp.where(half, qv, jnp.zeros_like(qv)), k2)
                s_scr[b * BAND:(b + 1) * BAND, :] = jnp.where(_band_keep(g, b, t), sb, NEG)
            s = s_scr[...]
            m = jnp.max(s, axis=-1, keepdims=True)
            pr = jnp.exp(s - m)
            den = jnp.sum(pr, axis=-1, keepdims=True)
            p_scr[...] = pr.astype(BF16)
            for b in range(DIL_NB):
                v2 = v_ref[0, pl.ds(pl.multiple_of(base + b * BAND, BAND), 2 * BAND), :]
                o_scr[hh, b * BAND:(b + 1) * BAND, :] = _nn(p_scr[b * BAND:(b + 1) * BAND, :], v2)
            o_scr[hh] = o_scr[hh] / den
            lse_h.append(m + jnp.log(den))
        out = jnp.where(half0, o_scr[0], o_scr[1])
        lse = jnp.where(half0, lse_h[0], lse_h[1])
        for b in range(DIL_NB):
            tok = _dil_tok(g, b, t)
            o_ref[tok, :] = out[b * BAND:(b + 1) * BAND, :]
            l_ref[tok, :] = lse[b * BAND:(b + 1) * BAND, :]

    def inp(tq):
        return pl.BlockSpec((1, BAND + SEQ, 128), lambda pr, t: (tq, 0, pr))

    out = pl.BlockSpec((SEQ, 128), lambda pr, t: (0, pr))
    return pl.pallas_call(
        body, name=f"dil_attn_fwd_g{g}", grid=(4, SEQ // DIL_ST),
        in_specs=[inp(0), inp(1), inp(2)], out_specs=[out, out],
        out_shape=[jax.ShapeDtypeStruct((SEQ, 512), F32), jax.ShapeDtypeStruct((SEQ, 512), F32)],
        scratch_shapes=[pltpu.VMEM((DIL_ST, 2 * BAND), F32), pltpu.VMEM((DIL_ST, 2 * BAND), BF16),
                        pltpu.VMEM((2, DIL_ST, 128), F32)],
        compiler_params=_cparams(),
    )(qkv, qkv, qkv)


def _dil_attn_bwd2(qkv, dyd, yd, lse_all, g, token=None):
    d = DIL_DILATIONS[g]
    sub_len = SEQ // d
    nst = SEQ // DIL_ST
    after, after_specs = _after(token)

    def body(q_ref, k_ref, v_ref, do_ref, y_ref, l_ref, *rest):
        out_ref, dk_scr, dv_scr, s_scr, dp_scr, p_scr, ds_scr, do_scr, y_scr, l_scr, dq_scr = rest[-11:]
        t = pl.program_id(1)
        base = t * DIL_ST

        @pl.when(t == 0)
        def _():
            dk_scr[...] = jnp.zeros_like(dk_scr)
            dv_scr[...] = jnp.zeros_like(dv_scr)

        for b in range(DIL_NB):
            tok = _dil_tok(g, b, t)
            do_scr[b * BAND:(b + 1) * BAND, :] = do_ref[tok, :]
            y_scr[b * BAND:(b + 1) * BAND, :] = y_ref[tok, :]
            l_scr[b * BAND:(b + 1) * BAND, :] = l_ref[tok, :]
        for hh in range(2):
            half = _head_half((BAND, 128), hh)
            half_st = _head_half((DIL_ST, 128), hh)
            dom = jnp.where(half_st, do_scr[...], 0.0)
            delta = jnp.sum(dom * y_scr[...], axis=-1, keepdims=True)
            lcol = jnp.max(jnp.where(half_st, l_scr[...], NEG), axis=-1, keepdims=True)
            for b in range(DIL_NB):
                rows = slice(b * BAND, (b + 1) * BAND)
                qv = q_ref[0, pl.ds(pl.multiple_of(base + (b + 1) * BAND, BAND), BAND), :]
                band = pl.ds(pl.multiple_of(base + b * BAND, BAND), 2 * BAND)
                sb = _nt(jnp.where(half, qv, jnp.zeros_like(qv)), k_ref[0, band, :])
                s_scr[rows, :] = jnp.where(_band_keep(g, b, t), sb, NEG)
                dp_scr[rows, :] = _nt(dom[rows, :].astype(BF16), v_ref[0, band, :])
            pr = jnp.exp(s_scr[...] - lcol)
            p_scr[...] = pr.astype(BF16)
            ds_scr[...] = (pr * (dp_scr[...] - delta)).astype(BF16)
            for b in range(DIL_NB):
                rows = slice(b * BAND, (b + 1) * BAND)
                qv = q_ref[0, pl.ds(pl.multiple_of(base + (b + 1) * BAND, BAND), BAND), :]
                band = pl.ds(pl.multiple_of(base + b * BAND, BAND), 2 * BAND)
                dqb = jnp.where(half, _nn(ds_scr[rows, :], k_ref[0, band, :]), 0.0)
                if hh == 0:
                    dq_scr[rows, :] = dqb
                else:
                    dq_scr[rows, :] += dqb
                half2 = _head_half((2 * BAND, 128), hh)
                dk_scr[band, :] += jnp.where(half2, _tn(ds_scr[rows, :], qv), 0.0)
                dv_scr[band, :] += _tn(p_scr[rows, :], dom[rows, :].astype(BF16))
        for b in range(DIL_NB):
            out_ref[pl.ds(0, 1), _dil_tok(g, b, t), :] = dq_scr[b * BAND:(b + 1) * BAND, :][None]

        @pl.when(t == nst - 1)
        def _():
            for r in range(d):
                rows = _strided(r, sub_len, d)
                out_ref[pl.ds(1, 1), rows, :] = dk_scr[BAND + r * sub_len:BAND + (r + 1) * sub_len, :][None]
                out_ref[pl.ds(2, 1), rows, :] = dv_scr[BAND + r * sub_len:BAND + (r + 1) * sub_len, :][None]

    def inp(tq):
        return pl.BlockSpec((1, BAND + SEQ, 128), lambda pr, t: (tq, 0, pr))

    tok_spec = pl.BlockSpec((SEQ, 128), lambda pr, t: (0, pr))
    st = (DIL_ST, 2 * BAND)
    return pl.pallas_call(
        body, name=f"dil_attn_bwd_g{g}", grid=(4, nst),
        in_specs=[inp(0), inp(1), inp(2), tok_spec, tok_spec, tok_spec] + after_specs,
        out_specs=pl.BlockSpec((3, SEQ, 128), lambda pr, t: (0, 0, pr)),
        out_shape=jax.ShapeDtypeStruct((3, SEQ, 512), F32),
        scratch_shapes=[pltpu.VMEM((BAND + SEQ, 128), F32), pltpu.VMEM((BAND + SEQ, 128), F32),
                        pltpu.VMEM(st, F32), pltpu.VMEM(st, F32), pltpu.VMEM(st, BF16), pltpu.VMEM(st, BF16),
                        pltpu.VMEM((DIL_ST, 128), F32), pltpu.VMEM((DIL_ST, 128), F32), pltpu.VMEM((DIL_ST, 128), F32),
                        pltpu.VMEM((DIL_ST, 128), F32)],
        compiler_params=_cparams(),
    )(qkv, qkv, qkv, dyd, yd, lse_all, *after)


def _band_masks():
    row = lax.broadcasted_iota(jnp.int32, (BAND, BAND), 0)
    col = lax.broadcasted_iota(jnp.int32, (BAND, BAND), 1)
    return row >= col, col >= row


def _dil_attn_fwd(qkv, g):
    d = DIL_DILATIONS[g]
    nbs = SEQ // d // BAND
    nblk = SEQ // BAND

    def body(q_ref, k_ref, v_ref, o_ref, l_ref):
        keep_c, keep_p = _band_masks()
        half0 = _head_half((BAND, 128), 0)

        def step(i, carry):
            cur = pl.ds(pl.multiple_of(i * BAND, BAND), BAND)
            prv = pl.ds(pl.multiple_of(jnp.maximum(i - 1, 0) * BAND, BAND), BAND)
            has_prev = (i % nbs) != 0
            qv = q_ref[0, cur, :]
            kc, kp = k_ref[0, cur, :], k_ref[0, prv, :]
            vc, vp = v_ref[0, cur, :], v_ref[0, prv, :]
            outs, lses = [], []
            for hh in range(2):
                qm = jnp.where(_head_half((BAND, 128), hh), qv, jnp.zeros_like(qv))
                sc = jnp.where(keep_c, _nt(qm, kc), NEG)
                sp = jnp.where(keep_p & has_prev, _nt(qm, kp), NEG)
                m = jnp.maximum(jnp.max(sc, axis=-1, keepdims=True), jnp.max(sp, axis=-1, keepdims=True))
                pc, pp = jnp.exp(sc - m), jnp.exp(sp - m)
                den = jnp.sum(pc, axis=-1, keepdims=True) + jnp.sum(pp, axis=-1, keepdims=True)
                o = (_nn(pc.astype(BF16), vc) + _nn(pp.astype(BF16), vp)) / den
                outs.append(o)
                lses.append(jnp.broadcast_to(m + jnp.log(den), (BAND, 128)))
            tok = _strided((i % nbs) * BAND * d + i // nbs, BAND, d)
            o_ref[tok, :] = jnp.where(half0, outs[0], outs[1])
            l_ref[tok, :] = jnp.where(half0, lses[0], lses[1])
            return carry

        lax.fori_loop(0, nblk, step, 0, unroll=DIL_UNROLL)

    def inp(tq):
        return pl.BlockSpec((1, SEQ, 128), lambda pr: (tq, 0, pr))

    out = pl.BlockSpec((SEQ, 128), lambda pr: (0, pr))
    return pl.pallas_call(
        body, name=f"dil_attn_fwd_g{g}", grid=(4,),
        in_specs=[inp(0), inp(1), inp(2)], out_specs=[out, out],
        out_shape=[jax.ShapeDtypeStruct((SEQ, 512), F32), jax.ShapeDtypeStruct((SEQ, 512), F32)],
        compiler_params=_cparams(),
    )(qkv, qkv, qkv)


def _dil_attn_bwd(qkv, dyd, yd, lse_all, g):
    d = DIL_DILATIONS[g]
    sub_len = SEQ // d
    nbs = sub_len // BAND
    nblk = SEQ // BAND

    def body(q_ref, k_ref, v_ref, do_ref, y_ref, l_ref, out_ref, dk_scr, dv_scr):
        keep_c, keep_p = _band_masks()
        dk_scr[...] = jnp.zeros_like(dk_scr)
        dv_scr[...] = jnp.zeros_like(dv_scr)

        def step(i, carry):
            cur = pl.ds(pl.multiple_of(i * BAND, BAND), BAND)
            prv = pl.ds(pl.multiple_of(jnp.maximum(i - 1, 0) * BAND, BAND), BAND)
            has_prev = (i % nbs) != 0
            tok = _strided((i % nbs) * BAND * d + i // nbs, BAND, d)
            qv = q_ref[0, cur, :]
            kc, kp = k_ref[0, cur, :], k_ref[0, prv, :]
            vc, vp = v_ref[0, cur, :], v_ref[0, prv, :]
            dov, yv, lv = do_ref[tok, :], y_ref[tok, :], l_ref[tok, :]
            dq = jnp.zeros((BAND, 128), F32)
            dkc = jnp.zeros((BAND, 128), F32)
            dkp = jnp.zeros((BAND, 128), F32)
            dvc = jnp.zeros((BAND, 128), F32)
            dvp = jnp.zeros((BAND, 128), F32)
            for hh in range(2):
                half = _head_half((BAND, 128), hh)
                qm = jnp.where(half, qv, jnp.zeros_like(qv))
                lcol = jnp.max(jnp.where(half, lv, NEG), axis=-1, keepdims=True)
                pc = jnp.exp(jnp.where(keep_c, _nt(qm, kc), NEG) - lcol)
                pp = jnp.exp(jnp.where(keep_p & has_prev, _nt(qm, kp), NEG) - lcol)
                dom = jnp.where(half, dov, 0.0)
                domb = dom.astype(BF16)
                delta = jnp.sum(dom * yv, axis=-1, keepdims=True)
                dsc = (pc * (_nt(domb, vc) - delta)).astype(BF16)
                dsp = (pp * (_nt(domb, vp) - delta)).astype(BF16)
                dvc = dvc + _tn(pc.astype(BF16), domb)
                dvp = dvp + _tn(pp.astype(BF16), domb)
                dq = dq + jnp.where(half, _nn(dsc, kc) + _nn(dsp, kp), 0.0)
                dkc = dkc + jnp.where(half, _tn(dsc, qv), 0.0)
                dkp = dkp + jnp.where(half, _tn(dsp, qv), 0.0)
            out_ref[pl.ds(0, 1), tok, :] = dq[None]
            dk_scr[cur, :] += dkc
            dk_scr[prv, :] += dkp
            dv_scr[cur, :] += dvc
            dv_scr[prv, :] += dvp
            return carry

        lax.fori_loop(0, nblk, step, 0, unroll=DIL_UNROLL)
        for r in range(d):
            rows = _strided(r, sub_len, d)
            out_ref[pl.ds(1, 1), rows, :] = dk_scr[r * sub_len:(r + 1) * sub_len, :][None]
            out_ref[pl.ds(2, 1), rows, :] = dv_scr[r * sub_len:(r + 1) * sub_len, :][None]

    def inp(tq):
        return pl.BlockSpec((1, SEQ, 128), lambda pr: (tq, 0, pr))

    tok_spec = pl.BlockSpec((SEQ, 128), lambda pr: (0, pr))
    return pl.pallas_call(
        body, name=f"dil_attn_bwd_g{g}", grid=(4,),
        in_specs=[inp(0), inp(1), inp(2), tok_spec, tok_spec, tok_spec],
        out_specs=pl.BlockSpec((3, SEQ, 128), lambda pr: (0, 0, pr)),
        out_shape=jax.ShapeDtypeStruct((3, SEQ, 512), F32),
        scratch_shapes=[pltpu.VMEM((SEQ, 128), F32), pltpu.VMEM((SEQ, 128), F32)],
        compiler_params=_cparams(),
    )(qkv, qkv, qkv, dyd, yd, lse_all)


def _dil_prep_bwd(dp_in, dqkv, rc, rs, g):
    tm = 1024

    def body(dp_any, g_ref, c_ref, s_ref, dp_ref):
        del dp_any
        tq = pl.program_id(0)

        @pl.when(tq == 2)
        def _():
            dp_ref[...] = g_ref[0].astype(BF16)

        @pl.when(tq < 2)
        def _():
            mult = jnp.where(tq == 0, DIL_SCALE, 1.0).astype(F32)
            lanes = _rope_lanes((tm, 128), DIL_ROPE_HALF, 64, 0)
            cv, sv = c_ref[...], s_ref[...] * mult
            cv = cv * mult
            for pr in range(4):
                gv = g_ref[0, :, pr * 128:(pr + 1) * 128]
                dp_ref[:, pr * 128:(pr + 1) * 128] = _rope_bwd(gv, cv, sv, DIL_ROPE_HALF, lanes).astype(BF16)

    tab = pl.BlockSpec((tm, 128), lambda tq, i: (i, 0))
    return pl.pallas_call(
        body, name=f"dil_prep_bwd_g{g}", grid=(3, SEQ // tm),
        in_specs=[pl.BlockSpec(memory_space=pl.ANY),
                  pl.BlockSpec((1, tm, 512), lambda tq, i: (tq, i, 0)), tab, tab],
        out_specs=pl.BlockSpec((tm, 512), lambda tq, i: (i, COL_QKV // 512 + tq * 3 + g)),
        out_shape=jax.ShapeDtypeStruct((SEQ, N_PAD), BF16),
        input_output_aliases={0: 0},
    )(dp_in, dqkv, rc, rs)


TAIL_T = 256


def _tail(p, ya, o_g, l_g, x, target, wpm, wpd, wout, post_g):
    tm = TAIL_T

    def body(pgz_ref, ya_ref, o0_ref, o1_ref, o2_ref, l0_ref, l1_ref, l2_ref, x_ref, t_ref,
             wpm_ref, wpd_ref, wout_ref, pg_ref,
             dp_ref, dy_ref, mg_ref, dt_ref, ua_ref, dpa_ref, ud_ref, dpd_ref, dya_ref, dyd_ref,
             yd_ref, lse_ref, loss_ref, dgp_ref):
        l0, l1, l2 = l0_ref[...], l1_ref[...], l2_ref[...]
        mx = jnp.maximum(jnp.maximum(l0, l1), l2)
        e0, e1, e2 = jnp.exp(l0 - mx), jnp.exp(l1 - mx), jnp.exp(l2 - mx)
        den = e0 + e1 + e2
        yd = (e0 * o0_ref[...] + e1 * o1_ref[...] + e2 * o2_ref[...]) / den
        yd_ref[...] = yd
        lse_ref[...] = mx + jnp.log(den)
        ya = ya_ref[...]

        gm, gd = pgz_ref[:, 0:1024], pgz_ref[:, 1024:2048]
        zm, zd = pgz_ref[:, 2048:2560], pgz_ref[:, 2560:3072]
        szm, szd = _sigmoid(zm), _sigmoid(zd)
        sm, sd = zm * szm, zd * szd
        ua = (ya * sm).astype(BF16)
        ud = (yd * sd).astype(BF16)
        ua_ref[...] = ua
        ud_ref[...] = ud
        pa = _nn(ua, wpm_ref[...])
        pd = _nn(ud, wpd_ref[...])
        sgm, sgd = _sigmoid(gm), _sigmoid(gd)
        mg = (sgm * pa + sgd * pd).astype(BF16)
        mg_ref[...] = mg
        t = _nn(mg, wout_ref[...])
        r3 = lax.rsqrt(jnp.mean(t * t, axis=-1, keepdims=True) + EPS)
        n = t * r3
        pg = pg_ref[...]
        err = x_ref[...] + n * pg - t_ref[...]
        lpart = jnp.sum(err * err, axis=0, keepdims=True)

        dy = err * (1.0 / D_MODEL)
        dy_ref[...] = dy
        gpart = jnp.sum(dy * n, axis=0, keepdims=True)
        dn = dy * pg
        dt = (r3 * (dn - n * jnp.mean(dn * n, axis=-1, keepdims=True))).astype(BF16)
        dt_ref[...] = dt
        dmg = _nt(dt, wout_ref[...])
        dpa = (dmg * sgm).astype(BF16)
        dpd = (dmg * sgd).astype(BF16)
        dpa_ref[...] = dpa
        dpd_ref[...] = dpd
        dp_ref[:, 0:1024] = (dmg * pa * sgm * (1.0 - sgm)).astype(BF16)
        dp_ref[:, 1024:2048] = (dmg * pd * sgd * (1.0 - sgd)).astype(BF16)
        dua = _nt(dpa, wpm_ref[...])
        dud = _nt(dpd, wpd_ref[...])
        dya_ref[...] = dua * sm
        dyd_ref[...] = dud * sd
        dp_ref[:, 2048:2560] = (dua * ya * szm * (1.0 + zm * (1.0 - szm))).astype(BF16)
        dp_ref[:, 2560:3072] = (dud * yd * szd * (1.0 + zd * (1.0 - szd))).astype(BF16)

        @pl.when(pl.program_id(0) == 0)
        def _():
            loss_ref[...] = lpart
            dgp_ref[...] = gpart

        @pl.when(pl.program_id(0) > 0)
        def _():
            loss_ref[...] += lpart
            dgp_ref[...] += gpart

    def rows(w):
        return pl.BlockSpec((tm, w), lambda i: (i, 0))

    def full(shape):
        return pl.BlockSpec(shape, lambda i: (0, 0))

    def sds(w, dt):
        return jax.ShapeDtypeStruct((SEQ, w), dt)

    return pl.pallas_call(
        body, name="tail", grid=(SEQ // tm,),
        in_specs=[rows(3072), rows(512), rows(512), rows(512), rows(512), rows(512), rows(512), rows(512),
                  rows(1024), rows(1024), full((512, 1024)), full((512, 1024)), full((1024, 1024)), full((1, 1024))],
        out_specs=[rows(3072), rows(1024), rows(1024), rows(1024), rows(512), rows(1024), rows(512), rows(1024),
                   rows(512), rows(512), rows(512), rows(512), full((1, 1024)), full((1, 1024))],
        out_shape=[sds(N_PAD, BF16), sds(1024, F32), sds(1024, BF16), sds(1024, BF16), sds(512, BF16),
                   sds(1024, BF16), sds(512, BF16), sds(1024, BF16), sds(512, F32), sds(512, F32),
                   sds(512, F32), sds(512, F32),
                   jax.ShapeDtypeStruct((1, 1024), F32), jax.ShapeDtypeStruct((1, 1024), F32)],
        compiler_params=_cparams(),
    )(p, ya, o_g[0], o_g[1], o_g[2], l_g[0], l_g[1], l_g[2], x, target, wpm, wpd, wout, post_g)


def _sum_parts(recv, own, me, tr, name):
    n, r, w = recv.shape
    if r % tr:
        return _sum_parts_cols(recv, own, me, name)
    own_spec = (pl.BlockSpec((tr, w), lambda i, me_ref: (i, 0)) if own.ndim == 2
                else pl.BlockSpec((None, tr, w), lambda i, me_ref: (me_ref[0], i, 0)))

    def body(me_ref, p_ref, own_ref, o_ref):
        mine = own_ref[...].astype(F32)
        acc = jnp.zeros((tr, w), F32)
        for s in range(n):
            acc = acc + jnp.where(me_ref[0] == s, mine, p_ref[s].astype(F32))
        o_ref[...] = acc

    return pl.pallas_call(
        body, name=name,
        grid_spec=pltpu.PrefetchScalarGridSpec(
            num_scalar_prefetch=1, grid=(r // tr,),
            in_specs=[pl.BlockSpec((n, tr, w), lambda i, me_ref: (0, i, 0)), own_spec],
            out_specs=pl.BlockSpec((tr, w), lambda i, me_ref: (i, 0))),
        out_shape=jax.ShapeDtypeStruct((r, w), F32),
    )(me.reshape(1), recv, own)


def _sum_parts_cols(recv, own, me, name):
    n, r, w = recv.shape
    tc = 128

    def body(me_ref, p_ref, own_ref, o_ref):
        mine = own_ref[...].astype(F32)
        acc = jnp.zeros((r, tc), F32)
        for s in range(n):
            acc = acc + jnp.where(me_ref[0] == s, mine, p_ref[s].astype(F32))
        o_ref[...] = acc

    return pl.pallas_call(
        body, name=name,
        grid_spec=pltpu.PrefetchScalarGridSpec(
            num_scalar_prefetch=1, grid=(w // tc,),
            in_specs=[pl.BlockSpec((n, r, tc), lambda i, me_ref: (0, 0, i)),
                      pl.BlockSpec((None, r, tc), lambda i, me_ref: (me_ref[0], 0, i))],
            out_specs=pl.BlockSpec((r, tc), lambda i, me_ref: (0, i))),
        out_shape=jax.ShapeDtypeStruct((r, w), F32),
    )(me.reshape(1), recv, own)


def _adamw(w, g, m, v, name):
    lead = w.shape[:-2]
    r, c = w.shape[-2:]
    tr = max([t for t in range(8, 257, 8) if r % t == 0], default=r)
    c1 = 1.0 - ADAM_B1 ** ADAM_STEP
    c2 = 1.0 - ADAM_B2 ** ADAM_STEP

    def body(w_ref, g_ref, m_ref, v_ref, d_ref, nm_ref, nv_ref):
        gv = g_ref[...]
        nm = ADAM_B1 * m_ref[...] + (1.0 - ADAM_B1) * gv
        nv = ADAM_B2 * v_ref[...] + (1.0 - ADAM_B2) * (gv * gv)
        nm_ref[...] = nm
        nv_ref[...] = nv
        d_ref[...] = -ADAM_LR * ((nm / c1) / (jnp.sqrt(nv / c2) + ADAM_EPS) + ADAM_WD * w_ref[...])

    zeros = (0,) * len(lead)
    spec = pl.BlockSpec((1,) * len(lead) + (tr, c), lambda i: zeros + (i, 0))
    sd = jax.ShapeDtypeStruct(w.shape, F32)
    return pl.pallas_call(
        body, name=name, grid=(r // tr,),
        in_specs=[spec] * 4, out_specs=[spec] * 3, out_shape=[sd] * 3,
    )(w, g, m, v)


ANY = pl.BlockSpec(memory_space=pl.ANY)


def _my_place():
    return lax.axis_index("x"), lax.axis_index("y"), lax.axis_index("c")


HBM = pl.BlockSpec(memory_space=pltpu.HBM)
SEM = pl.BlockSpec(memory_space=pltpu.SEMAPHORE)
DATAFLOW = pltpu.SideEffectType.DATAFLOW_SIDE_EFFECTING


def _other_chips(x, y):
    return [(1 - x, y), (x, 1 - y), (1 - x, 1 - y)]


def _half(mi, hc):
    r, c = SHARD_SHAPES[mi]
    if mi == 0:
        return pl.ds(0, r), pl.ds(pl.multiple_of(hc * (c // 2), 128), c // 2)
    return pl.ds(pl.multiple_of(hc * (r // 2), 16), r // 2), pl.ds(0, c)


def _gather_start(mats, landing):
    n = N_MATS

    def body(*refs):
        m_refs, land_refs = refs[:n], refs[n:2 * n]
        send_sems, recv_sems, token = refs[2 * n], refs[2 * n + 1], refs[-1]
        x, y, c = _my_place()
        for mi in range(n):
            rows, cols = _half(mi, c)
            for j, (cx, cy) in enumerate(_other_chips(x, y)):
                pltpu.make_async_remote_copy(
                    src_ref=m_refs[mi].at[rows, cols], dst_ref=land_refs[mi].at[2 * x + y, rows, cols],
                    send_sem=send_sems.at[mi * 3 + j], recv_sem=recv_sems.at[mi * 3 + j],
                    device_id=(cx, cy, c), device_id_type=MESH).start()
        token[...] = jnp.zeros_like(token)

    hbm = [pltpu.HBM(a.shape, a.dtype) for a in list(mats) + list(landing)]
    outs = pl.pallas_call(
        body, name="gather_start",
        out_shape=(pltpu.SemaphoreType.DMA((3 * n,)), pltpu.SemaphoreType.DMA((3 * n,)), *hbm,
                   jax.ShapeDtypeStruct((8, 128), F32)),
        in_specs=[HBM] * (2 * n), out_specs=(SEM, SEM, *[HBM] * (2 * n), pl.BlockSpec(memory_space=pltpu.VMEM)),
        input_output_aliases={i: 2 + i for i in range(2 * n)},
        compiler_params=pltpu.CompilerParams(has_side_effects=DATAFLOW),
    )(*[pltpu.with_memory_space_constraint(a, pltpu.HBM) for a in list(mats) + list(landing)])
    return outs[:-1], outs[-1]


def _gather_wait(handle, after):
    n = N_MATS

    def body(*refs):
        m_refs, land_refs = refs[:n], refs[n:2 * n]
        send_sems, recv_sems = refs[2 * n], refs[2 * n + 1]
        x, y, c = _my_place()
        for mi in range(n):
            rows, cols = _half(mi, c)
            for j, (cx, cy) in enumerate(_other_chips(x, y)):
                pltpu.make_async_remote_copy(
                    src_ref=m_refs[mi].at[rows, cols], dst_ref=land_refs[mi].at[2 * x + y, rows, cols],
                    send_sem=send_sems.at[mi * 3 + j], recv_sem=recv_sems.at[mi * 3 + j],
                    device_id=(cx, cy, c), device_id_type=MESH).wait_send()
                got = land_refs[mi].at[2 * cx + cy, rows, cols]
                pltpu.make_async_remote_copy(
                    src_ref=got, dst_ref=got, send_sem=send_sems.at[mi * 3 + j], recv_sem=recv_sems.at[mi * 3 + j],
                    device_id=(cx, cy, c), device_id_type=MESH).wait_recv()

    bufs = handle[2:]
    res = pl.pallas_call(
        body, name="gather_wait", out_shape=tuple(pltpu.HBM(b.shape, b.dtype) for b in bufs),
        in_specs=[HBM] * (2 * n) + [SEM, SEM, ANY], out_specs=tuple([HBM] * (2 * n)),
        input_output_aliases={i: i for i in range(2 * n)},
        compiler_params=pltpu.CompilerParams(has_side_effects=DATAFLOW),
    )(*bufs, handle[0], handle[1], after)
    return list(res[n:])


def _share_halves(gathered):
    n = N_MATS

    def body(*refs):
        out_refs = refs[n:2 * n]
        send_sems, recv_sems = refs[2 * n:]
        x, y, c = _my_place()
        sibling = (x, y, 1 - c)
        sends = []
        for mi in range(n):
            for j, (cx, cy) in enumerate(_other_chips(x, y)):
                mine = out_refs[mi].at[(2 * cx + cy,) + _half(mi, c)]
                cp = pltpu.make_async_remote_copy(src_ref=mine, dst_ref=mine, send_sem=send_sems.at[mi * 3 + j],
                                                  recv_sem=recv_sems.at[mi * 3 + j], device_id=sibling, device_id_type=MESH)
                cp.start()
                sends.append(cp)
        for mi in range(n):
            for j, (cx, cy) in enumerate(_other_chips(x, y)):
                theirs = out_refs[mi].at[(2 * cx + cy,) + _half(mi, 1 - c)]
                pltpu.make_async_remote_copy(src_ref=theirs, dst_ref=theirs, send_sem=send_sems.at[mi * 3 + j],
                                             recv_sem=recv_sems.at[mi * 3 + j], device_id=sibling, device_id_type=MESH).wait_recv()
        for cp in sends:
            cp.wait_send()

    return pl.pallas_call(
        body, name="share_halves",
        in_specs=[ANY] * n, out_specs=[ANY] * n,
        out_shape=[jax.ShapeDtypeStruct(g.shape, g.dtype) for g in gathered],
        input_output_aliases={i: i for i in range(n)},
        scratch_shapes=[pltpu.SemaphoreType.DMA((3 * n,)), pltpu.SemaphoreType.DMA((3 * n,))],
    )(*gathered)


def _peers(x, y, c):
    out = []
    for k in range(1, 8):
        px, py, pc = x ^ (k >> 2), y ^ ((k >> 1) & 1), c ^ (k & 1)
        out.append((k - 1, (px, py, pc), 4 * px + 2 * py + pc))
    return out


def _exchange_start(parts, name):
    n = len(parts)

    def body(*refs):
        p_refs, land_refs = refs[:n], refs[n:2 * n]
        send_sems, recv_sems, token = refs[2 * n], refs[2 * n + 1], refs[-1]
        x, y, c = _my_place()
        me = 4 * x + 2 * y + c
        for k, dev, peer in _peers(x, y, c):
            for mi in range(n):
                pltpu.make_async_remote_copy(
                    src_ref=p_refs[mi].at[peer], dst_ref=land_refs[mi].at[me], send_sem=send_sems.at[k * n + mi],
                    recv_sem=recv_sems.at[k * n + mi], device_id=dev, device_id_type=MESH).start()
        token[...] = jnp.zeros_like(token)

    hbm = [pltpu.HBM(p.shape, p.dtype) for p in parts]
    outs = pl.pallas_call(
        body, name=name + "_start",
        out_shape=(pltpu.SemaphoreType.DMA((7 * n,)), pltpu.SemaphoreType.DMA((7 * n,)), *hbm, *hbm,
                   jax.ShapeDtypeStruct((8, 128), F32)),
        in_specs=[HBM] * (2 * n), out_specs=(SEM, SEM, *[HBM] * (2 * n), pl.BlockSpec(memory_space=pltpu.VMEM)),
        input_output_aliases={i: 2 + i for i in range(2 * n)},
        compiler_params=pltpu.CompilerParams(has_side_effects=DATAFLOW),
    )(*[pltpu.with_memory_space_constraint(p, pltpu.HBM) for p in parts],
      *[pltpu.with_memory_space_constraint(lax.empty(p.shape, p.dtype), pltpu.HBM) for p in parts])
    return (name, outs[:-1]), outs[-1]


def _exchange_wait(handle, after):
    name, outs = handle
    n = (len(outs) - 2) // 2

    def body(*refs):
        p_refs, land_refs = refs[:n], refs[n:2 * n]
        send_sems, recv_sems = refs[2 * n], refs[2 * n + 1]
        x, y, c = _my_place()
        me = 4 * x + 2 * y + c
        for k, dev, peer in _peers(x, y, c):
            for mi in range(n):
                pltpu.make_async_remote_copy(
                    src_ref=p_refs[mi].at[peer], dst_ref=land_refs[mi].at[me], send_sem=send_sems.at[k * n + mi],
                    recv_sem=recv_sems.at[k * n + mi], device_id=dev, device_id_type=MESH).wait_send()
                slot = land_refs[mi].at[peer]
                pltpu.make_async_remote_copy(
                    src_ref=slot, dst_ref=slot, send_sem=send_sems.at[k * n + mi],
                    recv_sem=recv_sems.at[k * n + mi], device_id=dev, device_id_type=MESH).wait_recv()

    bufs = outs[2:]
    res = pl.pallas_call(
        body, name=name + "_wait", out_shape=tuple(pltpu.HBM(b.shape, b.dtype) for b in bufs),
        in_specs=[HBM] * (2 * n) + [SEM, SEM, ANY], out_specs=tuple([HBM] * (2 * n)),
        input_output_aliases={i: i for i in range(2 * n)},
        compiler_params=pltpu.CompilerParams(has_side_effects=DATAFLOW),
    )(*bufs, outs[0], outs[1], after)
    return list(res[n:])


def _swap_halves(halves, gvec):
    def place(ref, mi, hc):
        return ref.at[:, pl.ds(pl.multiple_of(hc * 512, 128), 512)] if mi == 0 else ref.at[hc]

    def body(*refs):
        g_refs, gv_ref = refs[:N_MATS], refs[N_MATS]
        out_refs, rg_ref = refs[N_MATS + 1:2 * N_MATS + 1], refs[2 * N_MATS + 1]
        send_sems, recv_sems = refs[2 * N_MATS + 2:]
        x, y, c = _my_place()
        me = 4 * x + 2 * y + c
        sends = []
        for mi in range(N_MATS):
            cp = pltpu.make_async_remote_copy(src_ref=g_refs[mi], dst_ref=place(out_refs[mi], mi, c), send_sem=send_sems.at[mi],
                                              recv_sem=recv_sems.at[mi], device_id=(x, y, 1 - c), device_id_type=MESH)
            cp.start()
            sends.append(cp)
        for k, dev, peer in _peers(x, y, c):
            cp = pltpu.make_async_remote_copy(src_ref=gv_ref, dst_ref=rg_ref.at[me], send_sem=send_sems.at[N_MATS + k],
                                              recv_sem=recv_sems.at[N_MATS + k], device_id=dev, device_id_type=MESH)
            cp.start()
            sends.append(cp)
        for mi in range(N_MATS):
            got = place(out_refs[mi], mi, 1 - c)
            pltpu.make_async_remote_copy(src_ref=got, dst_ref=got, send_sem=send_sems.at[mi], recv_sem=recv_sems.at[mi],
                                         device_id=(x, y, 1 - c), device_id_type=MESH).wait_recv()
        for k, dev, peer in _peers(x, y, c):
            got = rg_ref.at[peer]
            pltpu.make_async_remote_copy(src_ref=got, dst_ref=got, send_sem=send_sems.at[N_MATS + k],
                                         recv_sem=recv_sems.at[N_MATS + k], device_id=dev, device_id_type=MESH).wait_recv()
        for cp in sends:
            cp.wait_send()

    outs = pl.pallas_call(
        body, name="swap_halves",
        in_specs=[ANY] * (N_MATS + 1), out_specs=[ANY] * (N_MATS + 1),
        out_shape=[jax.ShapeDtypeStruct(SHARD_SHAPES[0], F32)]
        + [jax.ShapeDtypeStruct((2, r // 2, c), F32) for r, c in SHARD_SHAPES[1:]]
        + [jax.ShapeDtypeStruct((8, 8, N_GVEC), F32)],
        scratch_shapes=[pltpu.SemaphoreType.DMA((N_MATS + 7,)), pltpu.SemaphoreType.DMA((N_MATS + 7,))],
    )(*halves, gvec)
    return outs[:N_MATS], outs[N_MATS]


def _set_slot(arr, block, idx):
    return lax.dynamic_update_slice(arr, block[None], (idx,) + (0,) * block.ndim)


PAD_RUNS = ((6304, 8352, 0), (5280, 6304, COL_Z), (672, 5280, COL_QKV), (0, 640, COL_LAT), (640, 672, COL_LAT + 704))
W_IN_SHARD = 2088


def _full_weights(gathered):
    def cols(a):
        return jnp.concatenate([a[s] for s in range(4)], axis=1)

    w_uq, w_ukv, w_pm, w_pd = [cols(a) for a in gathered[1:5]]
    w_out = gathered[5].reshape(D_MODEL, D_MODEL)
    w_in_t = gathered[0].reshape(4 * W_IN_SHARD, D_MODEL)
    pieces, at = [], 0
    for lo, hi, pad_lo in sorted(PAD_RUNS, key=lambda t: t[2]):
        if pad_lo > at:
            pieces.append(jnp.zeros((pad_lo - at, D_MODEL), w_in_t.dtype))
        pieces.append(w_in_t[lo:hi])
        at = pad_lo + hi - lo
    pieces.append(jnp.zeros((N_PAD - at, D_MODEL), w_in_t.dtype))
    w_pad_t = jnp.concatenate(pieces, axis=0)
    z32 = jnp.zeros((Q_RANK, 32), w_uq.dtype)
    wuq_pad = jnp.concatenate([t for h in range(MLA_HEADS) for t in (w_uq[:, h * 96:(h + 1) * 96], z32)], axis=1)
    z64 = jnp.zeros((KV_RANK, 64), w_ukv.dtype)
    wk_pad = jnp.concatenate([t for h in range(MLA_HEADS) for t in (w_ukv[:, h * 128:h * 128 + 64], z64)], axis=1)
    wv = jnp.concatenate([w_ukv[:, h * 128 + 64:(h + 1) * 128] for h in range(MLA_HEADS)], axis=1)
    return w_pad_t.T, w_pad_t, wuq_pad, wk_pad, wv, w_pm, w_pd, w_out


W_IN_LAT = 672


def _grad_parts_in_early(dwt_early):
    def in_block(s, h):
        cols = slice(h * 512, (h + 1) * 512)
        out = []
        for lo, hi, pad_lo in sorted(PAD_RUNS):
            a_, b_ = max(lo, s * W_IN_SHARD), min(hi, (s + 1) * W_IN_SHARD)
            if a_ < b_:
                out.append(jnp.zeros((b_ - a_, 512), dwt_early.dtype) if pad_lo >= COL_LAT
                           else dwt_early[pad_lo + a_ - lo:pad_lo + b_ - lo, cols])
        return jnp.concatenate(out, axis=0)

    return jnp.stack([in_block(s, h) for s in range(4) for h in range(2)])


def _grad_parts_in_late(dwt_late):
    rows = jnp.concatenate([dwt_late[0:640], dwt_late[704:736]], axis=0)
    zero = jnp.zeros((W_IN_LAT, 512), dwt_late.dtype)
    return jnp.stack([rows[:, 0:512], rows[:, 512:1024]] + [zero] * 6)


def _col_blocks(m):
    r, c = m.shape[0] // 2, m.shape[1] // 4
    return jnp.stack([m[h * r:(h + 1) * r, s * c:(s + 1) * c] for s in range(4) for h in range(2)])


def _grad_parts_mla(dwuq_pad, dwk_pad, dwv):
    d_uq = jnp.concatenate([dwuq_pad[:, h * 128:h * 128 + 96] for h in range(MLA_HEADS)], axis=1)
    d_ukv = jnp.concatenate([t for h in range(MLA_HEADS) for t in (dwk_pad[:, h * 128:h * 128 + 64], dwv[:, h * 64:(h + 1) * 64])],
                            axis=1)
    return [_col_blocks(d_uq), _col_blocks(d_ukv)]


def _rope_tables(positions, token=None):
    pos = positions.reshape(SEQ).astype(F32)
    if token is not None:
        pos = pos + token[0, 0]
    lane = jnp.arange(128)

    def table(rot, first, period):
        inv = ROPE_THETA ** (-jnp.arange(0, rot, 2, dtype=F32) / rot)
        half = rot // 2
        off = lane % period - first
        in1, in2 = (off >= 0) & (off < half), (off >= half) & (off < rot)
        inv_lane = jnp.where(in1 | in2, inv[jnp.clip(off % half, 0, half - 1)], 0.0)
        sign = jnp.where(in1, -1.0, 1.0).astype(F32)
        ang = pos[:, None] * inv_lane[None, :]
        return jnp.cos(ang), jnp.sin(ang) * sign[None, :]

    return table(32, 64, 128), table(16, 0, 64)


class _Links:
    def __init__(self, mats, chip, me):
        landing = [_set_slot(lax.empty((4,) + m.shape, m.dtype), m, chip) for m in mats]
        self.gather, self.token = _gather_start(mats, landing)
        self.me, self.sent, self.handles, self.sums = me, {}, {}, {}

    def weights(self, after):
        return _share_halves(_gather_wait(self.gather, after))

    def send(self, blocks, name):
        self.sent[name] = blocks
        self.handles[name], token = _exchange_start(blocks, name)
        return token

    def collect(self, name, after, parts):
        recv = _exchange_wait(self.handles[name], after)
        for r, own, part in zip(recv, self.sent[name], parts):
            self.sums[part] = _sum_parts(r, own, self.me, 64, "sum_grad_" + part)
        return tuple(self.sums[part] for part in parts)


def _device_grads(x, positions, target, gains, links):
    pre_g, q_g, kv_g, post_g = gains
    (mc, ms), (dc, ds) = _rope_tables(positions, links.token)
    h = _prenorm_fwd(x, pre_g, links.token)
    w_pad, w_pad_t, wuq_pad, wk_pad, wv, w_pm, w_pd, w_out = _full_weights(links.weights(h))

    p = _matmul(h, w_pad, "nn", F32, 1024, 1408, 1024, "in_proj")
    cqn, ckvn, q, k, v = _mla_prep_fwd(p, q_g, kv_g, wuq_pad, wk_pad, wv, mc, ms)
    ya, lse_m = _mla_flash_fwd(q, k, v)
    qkv = [_dil_prep_fwd(p, dc, ds, g) for g in range(3)]
    o_g, l_g = zip(*[_dil_attn_fwd2(qkv[g], g) for g in range(3)])
    (dp, dy, mg, dt, ua, dpa, ud, dpd, dya, dyd, yd, lse_d, loss_cols, dg_post) = _tail(
        p, ya, o_g, l_g, x, target, w_pm, w_pd, w_out, post_g)

    for g in range(3):
        dqkv = _dil_attn_bwd2(qkv[g], dyd, yd, lse_d, g)
        dp = _dil_prep_bwd(dp, dqkv, dc, ds, g)
    dw_early = _matmul(dp, h, "tn", BF16, 1536, 1024, 2048, "dw_in_early", a_cols=(0, COL_LAT // 1536))
    dwpm = _matmul(ua, dpa, "tn", BF16, 512, 1024, 512, "dw_proj_mla")
    dwpd = _matmul(ud, dpd, "tn", BF16, 512, 1024, 512, "dw_proj_dil")
    dwout = _matmul(mg, dt, "tn", BF16, 1024, 1024, 512, "dw_out")
    token = links.send([_grad_parts_in_early(dw_early), _col_blocks(dwpm), _col_blocks(dwpd),
                        dwout.reshape(8, 128, D_MODEL)], "exchange_early")

    dq, dk, dv = _mla_flash_bwd(q, k, v, ya, dya, lse_m, token)
    dp, dqb, dkb, dvb, dg_q, dg_kv = _mla_prep_bwd(dp, p, dq, dk, dv, q_g, kv_g, wuq_pad, wk_pad, wv, mc, ms)
    dwuq_pad = _matmul(cqn, dqb, "tn", BF16, Q_RANK, 1024, 512, "dw_uq")
    dwk_pad = _matmul(ckvn, dkb, "tn", BF16, KV_RANK, 1024, 512, "dw_k")
    dwv = _matmul(ckvn, dvb, "tn", BF16, KV_RANK, 512, 512, "dw_v")
    dw_late = _matmul(dp, h, "tn", BF16, N_LAT, 1024, 2048, "dw_in_late", a_cols=(COL_LAT // N_LAT, 1))
    token = links.send([_grad_parts_in_late(dw_late)] + _grad_parts_mla(dwuq_pad, dwk_pad, dwv), "exchange_late")
    early = links.collect("exchange_early", dw_late, ("in_early", "pm", "pd", "out"))

    dh = _matmul(dp, w_pad_t, "nn", F32, 1024, 1024, 1408, "dh", (token,) + tuple(early))
    grad_x, dg_pre = _prenorm_bwd(x, dh, dy, pre_g)
    links.collect("exchange_late", grad_x, ("in_late", "uq", "ukv"))

    loss_part = jnp.pad((jnp.sum(loss_cols) * (0.5 / D_MODEL)).reshape(1, 1), ((0, 0), (0, N_GVEC - N_GAINS - 1)))
    gvec = jnp.concatenate([dg_pre, dg_q, dg_kv, dg_post, loss_part], axis=1)
    return grad_x, gvec


def kernel(x, positions, pre_norm_g, w_in, q_norm_g, w_uq, kv_norm_g, w_ukv, w_proj_mla, w_proj_dil, w_out, post_norm_g, loss_target, m_pre_norm_g, m_w_in, m_q_norm_g, m_w_uq, m_kv_norm_g, m_w_ukv, m_w_proj_mla, m_w_proj_dil, m_w_out, m_post_norm_g, v_pre_norm_g, v_w_in, v_q_norm_g, v_w_uq, v_kv_norm_g, v_w_ukv, v_w_proj_mla, v_w_proj_dil, v_w_out, v_post_norm_g):
    xi, yi, ci = _my_place()
    chip, me = 2 * xi + yi, 4 * xi + 2 * yi + ci
    mats = [jnp.swapaxes(w_in, 1, 2)] + [w_uq, w_ukv, w_proj_mla, w_proj_dil, w_out]
    mats = [w.reshape(w.shape[1:]).astype(BF16) for w in mats]
    links = _Links(mats, chip, me)
    gains = (pre_norm_g, q_norm_g, kv_norm_g, post_norm_g)
    grad_x, gvec = _device_grads(x[0], positions, loss_target[0], gains, links)

    sums = links.sums
    in_e = sums["in_early"]
    half_in = jnp.concatenate([in_e[:W_IN_LAT] + jnp.where(chip == 0, sums["in_late"], 0.0), in_e[W_IN_LAT:]], axis=0)
    halves = [half_in, sums["uq"], sums["ukv"], sums["pm"], sums["pd"], sums["out"]]
    gvec8 = jnp.pad(gvec, ((0, 7), (0, 0)))
    swapped, recv_gains = _swap_halves(halves, gvec8)
    g_gains = _sum_parts(recv_gains, gvec8, me, 8, "sum_gain_parts")[0:1]
    loss = g_gains[0, N_GAINS]
    g_in_t = lax.dynamic_update_slice(swapped[0], half_in, (0, ci * 512))[None]
    g_mats = [jnp.swapaxes(g_in_t, 1, 2)] + [_set_slot(s, hf, ci).reshape((1,) + shp)
                                             for s, hf, shp in zip(swapped[1:], halves[1:], SHARD_SHAPES[1:])]

    off = [0, 1024, 1408, 1664, 2688]
    g_gain = [g_gains[:, off[i]:off[i + 1]] for i in range(4)]
    grads = [g_gain[0], g_mats[0], g_gain[1], g_mats[1], g_gain[2], g_mats[2], g_mats[3], g_mats[4], g_mats[5], g_gain[3]]
    ws = [pre_norm_g, w_in, q_norm_g, w_uq, kv_norm_g, w_ukv, w_proj_mla, w_proj_dil, w_out, post_norm_g]
    ms = [m_pre_norm_g, m_w_in, m_q_norm_g, m_w_uq, m_kv_norm_g, m_w_ukv, m_w_proj_mla, m_w_proj_dil, m_w_out, m_post_norm_g]
    vs = [v_pre_norm_g, v_w_in, v_q_norm_g, v_w_uq, v_kv_norm_g, v_w_ukv, v_w_proj_mla, v_w_proj_dil, v_w_out, v_post_norm_g]
    deltas, new_m, new_v = [], [], []
    for i, (w, g, m, v) in enumerate(zip(ws, grads, ms, vs)):
        if w.shape[-1] % 128 and w.shape[-2] % 128 == 0:
            g = g_in_t if w is w_in else jnp.swapaxes(g, 1, 2)
            grads[i] = jnp.swapaxes(g, 1, 2)
            d_, m_, v_ = [jnp.swapaxes(o, 1, 2) for o in
                          _adamw(jnp.swapaxes(w, 1, 2), g, jnp.swapaxes(m, 1, 2), jnp.swapaxes(v, 1, 2), f"adamw_{i}")]
        else:
            d_, m_, v_ = _adamw(w, g, m, v, f"adamw_{i}")
        deltas.append(d_)
        new_m.append(m_)
        new_v.append(v_)
    return (loss, grad_x.reshape(x.shape), *grads, *deltas, *new_m, *new_v)
```

```python
import jax
import jax.numpy as jnp
from jax import lax
from jax.experimental import pallas as pl
from jax.experimental.pallas import tpu as pltpu

F32 = jnp.float32
BF16 = jnp.bfloat16

SEQ = 4096
D_MODEL = 1024
EPS = 1e-6
ROPE_THETA = 500000.0
MLA_HEADS = 8
Q_RANK = 384
KV_RANK = 256
MLA_SCALE = 96.0 ** -0.5
MLA_ROPE_HALF = 16
DIL_DILATIONS = (1, 4, 16)
DIL_ROPE_HALF = 8
DIL_SCALE = 0.125
BAND = 128

N_LAT = 768
COL_Z, COL_QKV, COL_LAT = 2048, 3072, 7680
N_PAD = 8448
IN_SPLITS = (384, 256, 32, 4608, 512, 512, 1024, 1024)

SHARD_SHAPES = ((2088, 1024), (384, 192), (256, 256), (512, 256), (512, 256), (256, 1024))
N_MATS = len(SHARD_SHAPES)
N_GAINS = 2688
N_GVEC = N_GAINS + 128

ADAM_LR, ADAM_B1, ADAM_B2, ADAM_EPS, ADAM_WD, ADAM_STEP = 0.001, 0.9, 0.999, 1e-08, 0.01, 10

VMEM_LIMIT = 56 * 1024 * 1024
NEG = -1e30
MESH = pl.DeviceIdType.MESH


def _cparams(**kw):
    return pltpu.CompilerParams(vmem_limit_bytes=VMEM_LIMIT, **kw)


def _dot(a, b, dims):
    return lax.dot_general(a, b, (dims, ((), ())), preferred_element_type=F32)


def _nn(a, b):
    return _dot(a, b, ((1,), (0,)))


def _nt(a, b):
    return _dot(a, b, ((1,), (1,)))


def _tn(a, b):
    return _dot(a, b, ((0,), (0,)))


def _rope_lanes(shape, half, period, first):
    lane = lax.broadcasted_iota(jnp.int32, shape, len(shape) - 1) % period
    return (lane >= first) & (lane < first + half), (lane >= first + half) & (lane < first + 2 * half)


def _rope_fwd(x, c, s, half, lanes):
    x1, _ = lanes
    return x * c + jnp.where(x1, pltpu.roll(x, 128 - half, 1), pltpu.roll(x, half, 1)) * s


def _rope_bwd(g, c, s, half, lanes):
    x1, x2 = lanes
    gs = g * s
    return g * c + jnp.where(x2, pltpu.roll(gs, half, 1), jnp.where(x1, pltpu.roll(gs, 128 - half, 1), 0.0))


def _sigmoid(x):
    return 1.0 / (1.0 + jnp.exp(-x))


def _after(token):
    tokens = [t for t in (token if isinstance(token, (tuple, list)) else [token]) if t is not None]
    return tokens, [pl.BlockSpec(memory_space=pl.ANY)] * len(tokens)


def _matmul(a, b, mode, out_dtype, tm, tn, tk, name, token=None, b_cols=None, a_cols=None):
    after, after_specs = _after(token)
    if mode == "nn":
        (m, k), n = a.shape, b.shape[1]
        first = 0
        if b_cols is not None:
            first, n = b_cols[0], b_cols[1] * tn
        a_spec = pl.BlockSpec((tm, tk), lambda j, i, kk: (i, kk))
        b_spec = pl.BlockSpec((tk, tn), lambda j, i, kk: (kk, j + first))
        dot = _nn
    elif mode == "nt":
        (m, k), n = a.shape, b.shape[0]
        a_spec = pl.BlockSpec((tm, tk), lambda j, i, kk: (i, kk))
        b_spec = pl.BlockSpec((tn, tk), lambda j, i, kk: (j, kk))
        dot = _nt
    else:
        (k, m), n = a.shape, b.shape[1]
        first = 0
        if a_cols is not None:
            first, m = a_cols[0], a_cols[1] * tm
        a_spec = pl.BlockSpec((tk, tm), lambda j, i, kk: (kk, i + first))
        b_spec = pl.BlockSpec((tk, tn), lambda j, i, kk: (kk, j))
        dot = _tn
    assert m % tm == 0 and n % tn == 0 and k % tk == 0, (name, m, n, k, tm, tn, tk)
    nk = k // tk

    def body(a_ref, b_ref, *rest):
        o_ref, acc_ref = rest[-2:]
        kk = pl.program_id(2)
        part = dot(a_ref[...], b_ref[...])

        @pl.when(kk == 0)
        def _():
            acc_ref[...] = part

        @pl.when(kk > 0)
        def _():
            acc_ref[...] += part

        @pl.when(kk == nk - 1)
        def _():
            o_ref[...] = acc_ref[...].astype(o_ref.dtype)

    return pl.pallas_call(
        body, name=name, grid=(n // tn, m // tm, nk),
        in_specs=[a_spec, b_spec] + after_specs,
        out_specs=pl.BlockSpec((tm, tn), lambda j, i, kk: (i, j)),
        out_shape=jax.ShapeDtypeStruct((m, n), out_dtype),
        scratch_shapes=[pltpu.VMEM((tm, tn), F32)],
        compiler_params=_cparams(),
    )(a, b, *after)


def _prenorm_fwd(x, g, token=None):
    tm = 512
    after, after_specs = _after(token)

    def body(x_ref, g_ref, *rest):
        xv = x_ref[...]
        r = lax.rsqrt(jnp.mean(xv * xv, axis=-1, keepdims=True) + EPS)
        rest[-1][...] = (xv * r * g_ref[...]).astype(BF16)

    return pl.pallas_call(
        body, name="prenorm_fwd", grid=(SEQ // tm,),
        in_specs=[pl.BlockSpec((tm, D_MODEL), lambda i: (i, 0)), pl.BlockSpec((1, D_MODEL), lambda i: (0, 0))] + after_specs,
        out_specs=pl.BlockSpec((tm, D_MODEL), lambda i: (i, 0)),
        out_shape=jax.ShapeDtypeStruct((SEQ, D_MODEL), BF16),
    )(x, g, *after)


def _prenorm_bwd(x, dh, dy, g):
    tm = 512

    def body(x_ref, dh_ref, dy_ref, g_ref, gx_ref, dg_ref):
        xv = x_ref[...]
        r = lax.rsqrt(jnp.mean(xv * xv, axis=-1, keepdims=True) + EPS)
        n = xv * r
        dhv = dh_ref[...]
        dn = dhv * g_ref[...]
        gx_ref[...] = dy_ref[...] + r * (dn - n * jnp.mean(dn * n, axis=-1, keepdims=True))
        part = jnp.sum(dhv * n, axis=0, keepdims=True)

        @pl.when(pl.program_id(0) == 0)
        def _():
            dg_ref[...] = part

        @pl.when(pl.program_id(0) > 0)
        def _():
            dg_ref[...] += part

    row = pl.BlockSpec((tm, D_MODEL), lambda i: (i, 0))
    vec = pl.BlockSpec((1, D_MODEL), lambda i: (0, 0))
    return pl.pallas_call(
        body, name="prenorm_bwd", grid=(SEQ // tm,),
        in_specs=[row, row, row, vec], out_specs=[row, vec],
        out_shape=[jax.ShapeDtypeStruct((SEQ, D_MODEL), F32), jax.ShapeDtypeStruct((1, D_MODEL), F32)],
        compiler_params=_cparams(),
    )(x, dh, dy, g)


def _mla_prep_fwd(p, qg, kvg, wuq, wk, wv, rc, rs):
    tm = 512

    def body(lat_ref, qg_ref, kvg_ref, wuq_ref, wk_ref, wv_ref, c_ref, s_ref,
             cqn_ref, ckvn_ref, q_ref, k_ref, v_ref):
        c, s = c_ref[...], s_ref[...]
        lanes = _rope_lanes((tm, 128), MLA_ROPE_HALF, 128, 64)
        cq = lat_ref[:, 0:Q_RANK]
        r1 = lax.rsqrt(jnp.mean(cq * cq, axis=-1, keepdims=True) + EPS)
        cqn = (cq * r1 * qg_ref[...]).astype(BF16)
        cqn_ref[...] = cqn
        q = _nn(cqn, wuq_ref[...])
        for h in range(MLA_HEADS):
            sl = slice(h * 128, (h + 1) * 128)
            q_ref[:, sl] = (_rope_fwd(q[:, sl], c, s, MLA_ROPE_HALF, lanes) * MLA_SCALE).astype(BF16)
        ckv = lat_ref[:, Q_RANK:Q_RANK + KV_RANK]
        r2 = lax.rsqrt(jnp.mean(ckv * ckv, axis=-1, keepdims=True) + EPS)
        ckvn = (ckv * r2 * kvg_ref[...]).astype(BF16)
        ckvn_ref[...] = ckvn
        krr = _rope_fwd(lat_ref[:, Q_RANK + KV_RANK:N_LAT], c, s, MLA_ROPE_HALF, lanes)
        kn = _nn(ckvn, wk_ref[...])
        for h in range(MLA_HEADS):
            sl = slice(h * 128, (h + 1) * 128)
            k_ref[:, sl] = (kn[:, sl] + krr).astype(BF16)
        v_ref[...] = _nn(ckvn, wv_ref[...]).astype(BF16)

    def full(shape):
        return pl.BlockSpec(shape, lambda i: (0, 0))

    def rows(w):
        return pl.BlockSpec((tm, w), lambda i: (i, 0))

    return pl.pallas_call(
        body, name="mla_prep_fwd", grid=(SEQ // tm,),
        in_specs=[pl.BlockSpec((tm, N_LAT), lambda i: (i, COL_LAT // N_LAT)),
                  full((1, Q_RANK)), full((1, KV_RANK)), full((Q_RANK, 1024)), full((KV_RANK, 1024)),
                  full((KV_RANK, 512)), rows(128), rows(128)],
        out_specs=[rows(Q_RANK), rows(KV_RANK), rows(1024), rows(1024), rows(512)],
        out_shape=[jax.ShapeDtypeStruct((SEQ, Q_RANK), BF16), jax.ShapeDtypeStruct((SEQ, KV_RANK), BF16),
                   jax.ShapeDtypeStruct((SEQ, 1024), BF16), jax.ShapeDtypeStruct((SEQ, 1024), BF16),
                   jax.ShapeDtypeStruct((SEQ, 512), BF16)],
        compiler_params=_cparams(),
    )(p, qg, kvg, wuq, wk, wv, rc, rs)


def _mla_prep_bwd(dp_in, p, dq, dk, dv, qg, kvg, wuq, wk, wv, rc, rs):
    tm = 512

    def body(dp_any, lat_ref, dq_ref, dk_ref, dv_ref, qg_ref, kvg_ref, wuq_ref, wk_ref, wv_ref,
             c_ref, s_ref, dp_ref, dqb_ref, dkb_ref, dvb_ref, dgq_ref, dgkv_ref):
        del dp_any
        c, s = c_ref[...], s_ref[...]
        lanes = _rope_lanes((tm, 128), MLA_ROPE_HALF, 128, 64)
        lane = lax.broadcasted_iota(jnp.int32, (tm, 128), 1)
        dkr = jnp.zeros((tm, 128), F32)
        for h in range(MLA_HEADS):
            sl = slice(h * 128, (h + 1) * 128)
            dqb_ref[:, sl] = _rope_bwd(dq_ref[:, sl] * MLA_SCALE, c, s, MLA_ROPE_HALF, lanes).astype(BF16)
            dkh = dk_ref[:, sl]
            dkr = dkr + dkh
            dkb_ref[:, sl] = jnp.where(lane < 64, dkh, 0.0).astype(BF16)
        dkr = jnp.where((lane >= 64) & (lane < 96), dkr, 0.0)
        dkr = _rope_bwd(dkr, c, s, MLA_ROPE_HALF, lanes)
        dvb = dv_ref[...].astype(BF16)
        dvb_ref[...] = dvb

        cq = lat_ref[:, 0:Q_RANK]
        r1 = lax.rsqrt(jnp.mean(cq * cq, axis=-1, keepdims=True) + EPS)
        n1 = cq * r1
        dcqn = _nt(dqb_ref[...], wuq_ref[...])
        dn1 = dcqn * qg_ref[...]
        dcq = r1 * (dn1 - n1 * jnp.mean(dn1 * n1, axis=-1, keepdims=True))
        pq = jnp.sum(dcqn * n1, axis=0, keepdims=True)

        ckv = lat_ref[:, Q_RANK:Q_RANK + KV_RANK]
        r2 = lax.rsqrt(jnp.mean(ckv * ckv, axis=-1, keepdims=True) + EPS)
        n2 = ckv * r2
        dckvn = _nt(dkb_ref[...], wk_ref[...]) + _nt(dvb, wv_ref[...])
        dn2 = dckvn * kvg_ref[...]
        dckv = r2 * (dn2 - n2 * jnp.mean(dn2 * n2, axis=-1, keepdims=True))
        pkv = jnp.sum(dckvn * n2, axis=0, keepdims=True)

        dp_ref[:, 0:Q_RANK] = dcq.astype(BF16)
        dp_ref[:, Q_RANK:Q_RANK + KV_RANK] = dckv.astype(BF16)
        dp_ref[:, Q_RANK + KV_RANK:N_LAT] = dkr.astype(BF16)

        @pl.when(pl.program_id(0) == 0)
        def _():
            dgq_ref[...] = pq
            dgkv_ref[...] = pkv

        @pl.when(pl.program_id(0) > 0)
        def _():
            dgq_ref[...] += pq
            dgkv_ref[...] += pkv

    def full(shape):
        return pl.BlockSpec(shape, lambda i: (0, 0))

    def rows(w):
        return pl.BlockSpec((tm, w), lambda i: (i, 0))

    lat = pl.BlockSpec((tm, N_LAT), lambda i: (i, COL_LAT // N_LAT))
    return pl.pallas_call(
        body, name="mla_prep_bwd", grid=(SEQ // tm,),
        in_specs=[pl.BlockSpec(memory_space=pl.ANY), lat, rows(1024), rows(1024), rows(512),
                  full((1, Q_RANK)), full((1, KV_RANK)), full((Q_RANK, 1024)), full((KV_RANK, 1024)),
                  full((KV_RANK, 512)), rows(128), rows(128)],
        out_specs=[lat, rows(1024), rows(1024), rows(512), full((1, Q_RANK)), full((1, KV_RANK))],
        out_shape=[jax.ShapeDtypeStruct((SEQ, N_PAD), BF16), jax.ShapeDtypeStruct((SEQ, 1024), BF16),
                   jax.ShapeDtypeStruct((SEQ, 1024), BF16), jax.ShapeDtypeStruct((SEQ, 512), BF16),
                   jax.ShapeDtypeStruct((1, Q_RANK), F32), jax.ShapeDtypeStruct((1, KV_RANK), F32)],
        input_output_aliases={0: 0},
        compiler_params=_cparams(),
    )(dp_in, p, dq, dk, dv, qg, kvg, wuq, wk, wv, rc, rs)


FLASH_T = 1024


def _head_half(shape, hh):
    lane = lax.broadcasted_iota(jnp.int32, shape, 1)
    return (lane < 64) if hh == 0 else (lane >= 64)


def _diag_keep(nr, nk):
    row = lax.broadcasted_iota(jnp.int32, (nr, nk), 0)
    col = lax.broadcasted_iota(jnp.int32, (nr, nk), 1)
    return row + (nk - nr) >= col


def _tri_steps(nb, q_major):
    if q_major:
        pairs = [(i, kb) for i in range(nb) for kb in range(i + 1)]
    else:
        pairs = [(i, kb) for kb in range(nb) for i in range(kb, nb)]
    return jnp.asarray([p[0] for p in pairs], jnp.int32), jnp.asarray([p[1] for p in pairs], jnp.int32)


def _mla_flash_fwd(q, k, v):
    t = FLASH_T
    nb = SEQ // t
    qtab, ktab = _tri_steps(nb, True)

    def body(qi_ref, ki_ref, q_ref, k_ref, v_ref, o_ref, lse_ref, m_scr, l_scr, acc_scr):
        step = pl.program_id(1)
        i, kb = qi_ref[step], ki_ref[step]

        @pl.when(kb == 0)
        def _():
            m_scr[...] = jnp.full_like(m_scr, NEG)
            l_scr[...] = jnp.zeros_like(l_scr)
            acc_scr[...] = jnp.zeros_like(acc_scr)

        def update(r0, nr, nk, diagonal):
            rs = slice(r0, r0 + nr)
            vv = v_ref[0:nk, :]
            for hh in range(2):
                sl = slice(hh * 128, (hh + 1) * 128)
                s = _nt(q_ref[rs, sl], k_ref[0:nk, sl])
                if diagonal:
                    s = jnp.where(_diag_keep(nr, nk), s, NEG)
                m_prev = m_scr[hh, rs, :]
                m_new = jnp.maximum(m_prev, jnp.max(s, axis=-1, keepdims=True))
                pr = jnp.exp(s - jnp.tile(m_new, (1, nk // 128)))
                alpha = jnp.exp(m_prev - m_new)
                l_scr[hh, rs, :] = alpha * l_scr[hh, rs, :] + jnp.sum(pr, axis=-1, keepdims=True)
                acc_scr[hh, rs, :] = alpha * acc_scr[hh, rs, :] + _nn(pr.astype(BF16), vv)
                m_scr[hh, rs, :] = m_new

        @pl.when(kb < i)
        def _():
            update(0, t, t, False)

        @pl.when(kb == i)
        def _():
            update(0, t // 2, t // 2, True)
            update(t // 2, t // 2, t, True)
            o0 = acc_scr[0] / l_scr[0]
            o1 = acc_scr[1] / l_scr[1]
            o_ref[...] = jnp.where(_head_half((t, 128), 0), o0, o1)
            for hh in range(2):
                lse_ref[:, hh * 128:(hh + 1) * 128] = m_scr[hh] + jnp.log(l_scr[hh])

    grid_spec = pltpu.PrefetchScalarGridSpec(
        num_scalar_prefetch=2, grid=(4, qtab.shape[0]),
        in_specs=[pl.BlockSpec((t, 256), lambda j, s, qi, ki: (qi[s], j)),
                  pl.BlockSpec((t, 256), lambda j, s, qi, ki: (ki[s], j)),
                  pl.BlockSpec((t, 128), lambda j, s, qi, ki: (ki[s], j))],
        out_specs=[pl.BlockSpec((t, 128), lambda j, s, qi, ki: (qi[s], j)),
                   pl.BlockSpec((t, 256), lambda j, s, qi, ki: (qi[s], j))],
        scratch_shapes=[pltpu.VMEM((2, t, 128), F32), pltpu.VMEM((2, t, 128), F32), pltpu.VMEM((2, t, 128), F32)])
    return pl.pallas_call(
        body, name="mla_flash_fwd", grid_spec=grid_spec,
        out_shape=[jax.ShapeDtypeStruct((SEQ, 512), F32), jax.ShapeDtypeStruct((SEQ, 1024), F32)],
        compiler_params=_cparams(),
    )(qtab, ktab, q, k, v)


def _mla_flash_bwd(q, k, v, o, do, lse, token=None):
    t = FLASH_T
    nb = SEQ // t
    qtab, ktab = _tri_steps(nb, False)
    after, after_specs = _after(token)

    def body(qi_ref, ki_ref, q_ref, k_ref, v_ref, o_ref, do_ref, lse_ref, *rest):
        dq_ref, dk_ref, dv_ref, dk_scr, dv_scr = rest[-5:]
        step = pl.program_id(1)
        i, kb = qi_ref[step], ki_ref[step]

        @pl.when(step == 0)
        def _():
            dq_ref[...] = jnp.zeros_like(dq_ref)

        @pl.when(i == kb)
        def _():
            dk_scr[...] = jnp.zeros_like(dk_scr)
            dv_scr[...] = jnp.zeros_like(dv_scr)

        def update(r0, nr, nk, diagonal):
            rs = slice(r0, r0 + nr)
            vv = v_ref[0:nk, :]
            ov = o_ref[rs, :]
            dov = do_ref[rs, :]
            rows = pl.ds(pl.multiple_of(i * t + r0, t // 2), nr)
            for hh in range(2):
                sl = slice(hh * 128, (hh + 1) * 128)
                qh, kh = q_ref[rs, sl], k_ref[0:nk, sl]
                s = _nt(qh, kh)
                if diagonal:
                    s = jnp.where(_diag_keep(nr, nk), s, NEG)
                pr = jnp.exp(s - jnp.tile(lse_ref[rs, sl], (1, nk // 128)))
                dom = jnp.where(_head_half((nr, 128), hh), dov, 0.0)
                domb = dom.astype(BF16)
                dv_scr[0:nk, :] += _tn(pr.astype(BF16), domb)
                dpr = _nt(domb, vv)
                delta = jnp.sum(dom * ov, axis=-1, keepdims=True)
                ds = (pr * (dpr - delta)).astype(BF16)
                dq_ref[rows, sl] += _nn(ds, kh)
                dk_scr[hh, 0:nk, :] += _tn(ds, qh)

        @pl.when(i > kb)
        def _():
            update(0, t, t, False)

        @pl.when(i == kb)
        def _():
            update(0, t // 2, t // 2, True)
            update(t // 2, t // 2, t, True)

        @pl.when(i == nb - 1)
        def _():
            dk_ref[:, 0:128] = dk_scr[0]
            dk_ref[:, 128:256] = dk_scr[1]
            dv_ref[...] = dv_scr[...]

    qi_map = lambda j, s, qi, ki: (qi[s], j)
    ki_map = lambda j, s, qi, ki: (ki[s], j)
    grid_spec = pltpu.PrefetchScalarGridSpec(
        num_scalar_prefetch=2, grid=(4, qtab.shape[0]),
        in_specs=[pl.BlockSpec((t, 256), qi_map), pl.BlockSpec((t, 256), ki_map), pl.BlockSpec((t, 128), ki_map),
                  pl.BlockSpec((t, 128), qi_map), pl.BlockSpec((t, 128), qi_map), pl.BlockSpec((t, 256), qi_map)]
        + after_specs,
        out_specs=[pl.BlockSpec((SEQ, 256), lambda j, s, qi, ki: (0, j)), pl.BlockSpec((t, 256), ki_map),
                   pl.BlockSpec((t, 128), ki_map)],
        scratch_shapes=[pltpu.VMEM((2, t, 128), F32), pltpu.VMEM((t, 128), F32)])
    return pl.pallas_call(
        body, name="mla_flash_bwd", grid_spec=grid_spec,
        out_shape=[jax.ShapeDtypeStruct((SEQ, 1024), F32), jax.ShapeDtypeStruct((SEQ, 1024), F32),
                   jax.ShapeDtypeStruct((SEQ, 512), F32)],
        compiler_params=_cparams(),
    )(qtab, ktab, q, k, v, o, do, lse, *after)


DIL_UNROLL = 4


def _strided(start, size, d):
    return pl.ds(start, size) if d == 1 else pl.ds(start, size, stride=d)


def _dil_prep_fwd(p, rc, rs, g):
    d = DIL_DILATIONS[g]
    sub_len = SEQ // d
    ch = min(sub_len, 512)

    def body(p_ref, c_ref, s_ref, o_ref, x_scr):
        tq = pl.program_id(0)
        lanes = _rope_lanes((ch, 128), DIL_ROPE_HALF, 64, 0)
        o_ref[0, 0:BAND, :] = jnp.zeros((BAND, 128), BF16)

        @pl.when(tq < 2)
        def _():
            mult = jnp.where(tq == 0, DIL_SCALE, 1.0).astype(F32)
            for c0 in range(0, SEQ, ch):
                rows = pl.ds(c0, ch)
                x_scr[rows, :] = _rope_fwd(p_ref[rows, :], c_ref[rows, :] * mult, s_ref[rows, :] * mult, DIL_ROPE_HALF, lanes)

        def gather(src):
            for r in range(d):
                for c0 in range(0, sub_len, ch):
                    at = BAND + r * sub_len + c0
                    o_ref[0, at:at + ch, :] = src[_strided(r + c0 * d, ch, d), :].astype(BF16)

        @pl.when(tq < 2)
        def _():
            gather(x_scr)

        @pl.when(tq == 2)
        def _():
            gather(p_ref)

    tab = pl.BlockSpec((SEQ, 128), lambda tq, pr: (0, 0))
    return pl.pallas_call(
        body, name=f"dil_prep_fwd_g{g}", grid=(3, 4),
        in_specs=[pl.BlockSpec((SEQ, 128), lambda tq, pr: (0, COL_QKV // 128 + (tq * 3 + g) * 4 + pr)), tab, tab],
        out_specs=pl.BlockSpec((1, BAND + SEQ, 128), lambda tq, pr: (tq, 0, pr)),
        out_shape=jax.ShapeDtypeStruct((3, BAND + SEQ, 512), BF16),
        scratch_shapes=[pltpu.VMEM((SEQ, 128), F32)],
        compiler_params=_cparams(),
    )(p, rc, rs)


DIL_ST = 1024
DIL_NB = DIL_ST // BAND


def _band_keep(g, b, t):
    nbs = SEQ // DIL_DILATIONS[g] // BAND
    row = lax.broadcasted_iota(jnp.int32, (BAND, 2 * BAND), 0)
    col = lax.broadcasted_iota(jnp.int32, (BAND, 2 * BAND), 1)
    cur = (col >= BAND) & (row >= col - BAND)
    prev = (col < BAND) & (col >= row)
    if nbs >= DIL_NB:
        if b > 0:
            return cur | prev
        return cur | (prev & ((t * DIL_NB) % nbs != 0))
    return cur | prev if b % nbs else cur


def _dil_tok(g, b, t):
    d = DIL_DILATIONS[g]
    nbs = SEQ // d // BAND
    gb = t * DIL_NB + b
    return _strided((gb % nbs) * BAND * d + gb // nbs, BAND, d)


def _dil_attn_fwd2(qkv, g):
    def body(q_ref, k_ref, v_ref, o_ref, l_ref, s_scr, p_scr, o_scr):
        t = pl.program_id(1)
        base = t * DIL_ST
        half0 = _head_half((DIL_ST, 128), 0)
        lse_h = []
        for hh in range(2):
            half = _head_half((BAND, 128), hh)
            for b in range(DIL_NB):
                qv = q_ref[0, pl.ds(pl.multiple_of(base + (b + 1) * BAND, BAND), BAND), :]
                k2 = k_ref[0, pl.ds(pl.multiple_of(base + b * BAND, BAND), 2 * BAND), :]
                sb = _nt(jnp.where(half, qv, jnp.zeros_like(qv)), k2)
                s_scr[b * BAND:(b + 1) * BAND, :] = jnp.where(_band_keep(g, b, t), sb, NEG)
            s = s_scr[...]
            m = jnp.max(s, axis=-1, keepdims=True)
            pr = jnp.exp(s - m)
            den = jnp.sum(pr, axis=-1, keepdims=True)
            p_scr[...] = pr.astype(BF16)
            for b in range(DIL_NB):
                v2 = v_ref[0, pl.ds(pl.multiple_of(base + b * BAND, BAND), 2 * BAND), :]
                o_scr[hh, b * BAND:(b + 1) * BAND, :] = _nn(p_scr[b * BAND:(b + 1) * BAND, :], v2)
            o_scr[hh] = o_scr[hh] / den
            lse_h.append(m + jnp.log(den))
        out = jnp.where(half0, o_scr[0], o_scr[1])
        lse = jnp.where(half0, lse_h[0], lse_h[1])
        for b in range(DIL_NB):
            tok = _dil_tok(g, b, t)
            o_ref[tok, :] = out[b * BAND:(b + 1) * BAND, :]
            l_ref[tok, :] = lse[b * BAND:(b + 1) * BAND, :]

    def inp(tq):
        return pl.BlockSpec((1, BAND + SEQ, 128), lambda pr, t: (tq, 0, pr))

    out = pl.BlockSpec((SEQ, 128), lambda pr, t: (0, pr))
    return pl.pallas_call(
        body, name=f"dil_attn_fwd_g{g}", grid=(4, SEQ // DIL_ST),
        in_specs=[inp(0), inp(1), inp(2)], out_specs=[out, out],
        out_shape=[jax.ShapeDtypeStruct((SEQ, 512), F32), jax.ShapeDtypeStruct((SEQ, 512), F32)],
        scratch_shapes=[pltpu.VMEM((DIL_ST, 2 * BAND), F32), pltpu.VMEM((DIL_ST, 2 * BAND), BF16),
                        pltpu.VMEM((2, DIL_ST, 128), F32)],
        compiler_params=_cparams(),
    )(qkv, qkv, qkv)


def _dil_attn_bwd2(qkv, dyd, yd, lse_all, g, token=None):
    d = DIL_DILATIONS[g]
    sub_len = SEQ // d
    nst = SEQ // DIL_ST
    after, after_specs = _after(token)

    def body(q_ref, k_ref, v_ref, do_ref, y_ref, l_ref, *rest):
        out_ref, dk_scr, dv_scr, s_scr, dp_scr, p_scr, ds_scr, do_scr, y_scr, l_scr, dq_scr = rest[-11:]
        t = pl.program_id(1)
        base = t * DIL_ST

        @pl.when(t == 0)
        def _():
            dk_scr[...] = jnp.zeros_like(dk_scr)
            dv_scr[...] = jnp.zeros_like(dv_scr)

        for b in range(DIL_NB):
            tok = _dil_tok(g, b, t)
            do_scr[b * BAND:(b + 1) * BAND, :] = do_ref[tok, :]
            y_scr[b * BAND:(b + 1) * BAND, :] = y_ref[tok, :]
            l_scr[b * BAND:(b + 1) * BAND, :] = l_ref[tok, :]
        for hh in range(2):
            half = _head_half((BAND, 128), hh)
            half_st = _head_half((DIL_ST, 128), hh)
            dom = jnp.where(half_st, do_scr[...], 0.0)
            delta = jnp.sum(dom * y_scr[...], axis=-1, keepdims=True)
            lcol = jnp.max(jnp.where(half_st, l_scr[...], NEG), axis=-1, keepdims=True)
            for b in range(DIL_NB):
                rows = slice(b * BAND, (b + 1) * BAND)
                qv = q_ref[0, pl.ds(pl.multiple_of(base + (b + 1) * BAND, BAND), BAND), :]
                band = pl.ds(pl.multiple_of(base + b * BAND, BAND), 2 * BAND)
                sb = _nt(jnp.where(half, qv, jnp.zeros_like(qv)), k_ref[0, band, :])
                s_scr[rows, :] = jnp.where(_band_keep(g, b, t), sb, NEG)
                dp_scr[rows, :] = _nt(dom[rows, :].astype(BF16), v_ref[0, band, :])
            pr = jnp.exp(s_scr[...] - lcol)
            p_scr[...] = pr.astype(BF16)
            ds_scr[...] = (pr * (dp_scr[...] - delta)).astype(BF16)
            for b in range(DIL_NB):
                rows = slice(b * BAND, (b + 1) * BAND)
                qv = q_ref[0, pl.ds(pl.multiple_of(base + (b + 1) * BAND, BAND), BAND), :]
                band = pl.ds(pl.multiple_of(base + b * BAND, BAND), 2 * BAND)
                dqb = jnp.where(half, _nn(ds_scr[rows, :], k_ref[0, band, :]), 0.0)
                if hh == 0:
                    dq_scr[rows, :] = dqb
                else:
                    dq_scr[rows, :] += dqb
                half2 = _head_half((2 * BAND, 128), hh)
                dk_scr[band, :] += jnp.where(half2, _tn(ds_scr[rows, :], qv), 0.0)
                dv_scr[band, :] += _tn(p_scr[rows, :], dom[rows, :].astype(BF16))
        for b in range(DIL_NB):
            out_ref[pl.ds(0, 1), _dil_tok(g, b, t), :] = dq_scr[b * BAND:(b + 1) * BAND, :][None]

        @pl.when(t == nst - 1)
        def _():
            for r in range(d):
                rows = _strided(r, sub_len, d)
                out_ref[pl.ds(1, 1), rows, :] = dk_scr[BAND + r * sub_len:BAND + (r + 1) * sub_len, :][None]
                out_ref[pl.ds(2, 1), rows, :] = dv_scr[BAND + r * sub_len:BAND + (r + 1) * sub_len, :][None]

    def inp(tq):
        return pl.BlockSpec((1, BAND + SEQ, 128), lambda pr, t: (tq, 0, pr))

    tok_spec = pl.BlockSpec((SEQ, 128), lambda pr, t: (0, pr))
    st = (DIL_ST, 2 * BAND)
    return pl.pallas_call(
        body, name=f"dil_attn_bwd_g{g}", grid=(4, nst),
        in_specs=[inp(0), inp(1), inp(2), tok_spec, tok_spec, tok_spec] + after_specs,
        out_specs=pl.BlockSpec((3, SEQ, 128), lambda pr, t: (0, 0, pr)),
        out_shape=jax.ShapeDtypeStruct((3, SEQ, 512), F32),
        scratch_shapes=[pltpu.VMEM((BAND + SEQ, 128), F32), pltpu.VMEM((BAND + SEQ, 128), F32),
                        pltpu.VMEM(st, F32), pltpu.VMEM(st, F32), pltpu.VMEM(st, BF16), pltpu.VMEM(st, BF16),
                        pltpu.VMEM((DIL_ST, 128), F32), pltpu.VMEM((DIL_ST, 128), F32), pltpu.VMEM((DIL_ST, 128), F32),
                        pltpu.VMEM((DIL_ST, 128), F32)],
        compiler_params=_cparams(),
    )(qkv, qkv, qkv, dyd, yd, lse_all, *after)


def _band_masks():
    row = lax.broadcasted_iota(jnp.int32, (BAND, BAND), 0)
    col = lax.broadcasted_iota(jnp.int32, (BAND, BAND), 1)
    return row >= col, col >= row


def _dil_attn_fwd(qkv, g):
    d = DIL_DILATIONS[g]
    nbs = SEQ // d // BAND
    nblk = SEQ // BAND

    def body(q_ref, k_ref, v_ref, o_ref, l_ref):
        keep_c, keep_p = _band_masks()
        half0 = _head_half((BAND, 128), 0)

        def step(i, carry):
            cur = pl.ds(pl.multiple_of(i * BAND, BAND), BAND)
            prv = pl.ds(pl.multiple_of(jnp.maximum(i - 1, 0) * BAND, BAND), BAND)
            has_prev = (i % nbs) != 0
            qv = q_ref[0, cur, :]
            kc, kp = k_ref[0, cur, :], k_ref[0, prv, :]
            vc, vp = v_ref[0, cur, :], v_ref[0, prv, :]
            outs, lses = [], []
            for hh in range(2):
                qm = jnp.where(_head_half((BAND, 128), hh), qv, jnp.zeros_like(qv))
                sc = jnp.where(keep_c, _nt(qm, kc), NEG)
                sp = jnp.where(keep_p & has_prev, _nt(qm, kp), NEG)
                m = jnp.maximum(jnp.max(sc, axis=-1, keepdims=True), jnp.max(sp, axis=-1, keepdims=True))
                pc, pp = jnp.exp(sc - m), jnp.exp(sp - m)
                den = jnp.sum(pc, axis=-1, keepdims=True) + jnp.sum(pp, axis=-1, keepdims=True)
                o = (_nn(pc.astype(BF16), vc) + _nn(pp.astype(BF16), vp)) / den
                outs.append(o)
                lses.append(jnp.broadcast_to(m + jnp.log(den), (BAND, 128)))
            tok = _strided((i % nbs) * BAND * d + i // nbs, BAND, d)
            o_ref[tok, :] = jnp.where(half0, outs[0], outs[1])
            l_ref[tok, :] = jnp.where(half0, lses[0], lses[1])
            return carry

        lax.fori_loop(0, nblk, step, 0, unroll=DIL_UNROLL)

    def inp(tq):
        return pl.BlockSpec((1, SEQ, 128), lambda pr: (tq, 0, pr))

    out = pl.BlockSpec((SEQ, 128), lambda pr: (0, pr))
    return pl.pallas_call(
        body, name=f"dil_attn_fwd_g{g}", grid=(4,),
        in_specs=[inp(0), inp(1), inp(2)], out_specs=[out, out],
        out_shape=[jax.ShapeDtypeStruct((SEQ, 512), F32), jax.ShapeDtypeStruct((SEQ, 512), F32)],
        compiler_params=_cparams(),
    )(qkv, qkv, qkv)


def _dil_attn_bwd(qkv, dyd, yd, lse_all, g):
    d = DIL_DILATIONS[g]
    sub_len = SEQ // d
    nbs = sub_len // BAND
    nblk = SEQ // BAND

    def body(q_ref, k_ref, v_ref, do_ref, y_ref, l_ref, out_ref, dk_scr, dv_scr):
        keep_c, keep_p = _band_masks()
        dk_scr[...] = jnp.zeros_like(dk_scr)
        dv_scr[...] = jnp.zeros_like(dv_scr)

        def step(i, carry):
            cur = pl.ds(pl.multiple_of(i * BAND, BAND), BAND)
            prv = pl.ds(pl.multiple_of(jnp.maximum(i - 1, 0) * BAND, BAND), BAND)
            has_prev = (i % nbs) != 0
            tok = _strided((i % nbs) * BAND * d + i // nbs, BAND, d)
            qv = q_ref[0, cur, :]
            kc, kp = k_ref[0, cur, :], k_ref[0, prv, :]
            vc, vp = v_ref[0, cur, :], v_ref[0, prv, :]
            dov, yv, lv = do_ref[tok, :], y_ref[tok, :], l_ref[tok, :]
            dq = jnp.zeros((BAND, 128), F32)
            dkc = jnp.zeros((BAND, 128), F32)
            dkp = jnp.zeros((BAND, 128), F32)
            dvc = jnp.zeros((BAND, 128), F32)
            dvp = jnp.zeros((BAND, 128), F32)
            for hh in range(2):
                half = _head_half((BAND, 128), hh)
                qm = jnp.where(half, qv, jnp.zeros_like(qv))
                lcol = jnp.max(jnp.where(half, lv, NEG), axis=-1, keepdims=True)
                pc = jnp.exp(jnp.where(keep_c, _nt(qm, kc), NEG) - lcol)
                pp = jnp.exp(jnp.where(keep_p & has_prev, _nt(qm, kp), NEG) - lcol)
                dom = jnp.where(half, dov, 0.0)
                domb = dom.astype(BF16)
                delta = jnp.sum(dom * yv, axis=-1, keepdims=True)
                dsc = (pc * (_nt(domb, vc) - delta)).astype(BF16)
                dsp = (pp * (_nt(domb, vp) - delta)).astype(BF16)
                dvc = dvc + _tn(pc.astype(BF16), domb)
                dvp = dvp + _tn(pp.astype(BF16), domb)
                dq = dq + jnp.where(half, _nn(dsc, kc) + _nn(dsp, kp), 0.0)
                dkc = dkc + jnp.where(half, _tn(dsc, qv), 0.0)
                dkp = dkp + jnp.where(half, _tn(dsp, qv), 0.0)
            out_ref[pl.ds(0, 1), tok, :] = dq[None]
            dk_scr[cur, :] += dkc
            dk_scr[prv, :] += dkp
            dv_scr[cur, :] += dvc
            dv_scr[prv, :] += dvp
            return carry

        lax.fori_loop(0, nblk, step, 0, unroll=DIL_UNROLL)
        for r in range(d):
            rows = _strided(r, sub_len, d)
            out_ref[pl.ds(1, 1), rows, :] = dk_scr[r * sub_len:(r + 1) * sub_len, :][None]
            out_ref[pl.ds(2, 1), rows, :] = dv_scr[r * sub_len:(r + 1) * sub_len, :][None]

    def inp(tq):
        return pl.BlockSpec((1, SEQ, 128), lambda pr: (tq, 0, pr))

    tok_spec = pl.BlockSpec((SEQ, 128), lambda pr: (0, pr))
    return pl.pallas_call(
        body, name=f"dil_attn_bwd_g{g}", grid=(4,),
        in_specs=[inp(0), inp(1), inp(2), tok_spec, tok_spec, tok_spec],
        out_specs=pl.BlockSpec((3, SEQ, 128), lambda pr: (0, 0, pr)),
        out_shape=jax.ShapeDtypeStruct((3, SEQ, 512), F32),
        scratch_shapes=[pltpu.VMEM((SEQ, 128), F32), pltpu.VMEM((SEQ, 128), F32)],
        compiler_params=_cparams(),
    )(qkv, qkv, qkv, dyd, yd, lse_all)


def _dil_prep_bwd(dp_in, dqkv, rc, rs, g):
    tm = 1024

    def body(dp_any, g_ref, c_ref, s_ref, dp_ref):
        del dp_any
        tq = pl.program_id(0)

        @pl.when(tq == 2)
        def _():
            dp_ref[...] = g_ref[0].astype(BF16)

        @pl.when(tq < 2)
        def _():
            mult = jnp.where(tq == 0, DIL_SCALE, 1.0).astype(F32)
            lanes = _rope_lanes((tm, 128), DIL_ROPE_HALF, 64, 0)
            cv, sv = c_ref[...], s_ref[...] * mult
            cv = cv * mult
            for pr in range(4):
                gv = g_ref[0, :, pr * 128:(pr + 1) * 128]
                dp_ref[:, pr * 128:(pr + 1) * 128] = _rope_bwd(gv, cv, sv, DIL_ROPE_HALF, lanes).astype(BF16)

    tab = pl.BlockSpec((tm, 128), lambda tq, i: (i, 0))
    return pl.pallas_call(
        body, name=f"dil_prep_bwd_g{g}", grid=(3, SEQ // tm),
        in_specs=[pl.BlockSpec(memory_space=pl.ANY),
                  pl.BlockSpec((1, tm, 512), lambda tq, i: (tq, i, 0)), tab, tab],
        out_specs=pl.BlockSpec((tm, 512), lambda tq, i: (i, COL_QKV // 512 + tq * 3 + g)),
        out_shape=jax.ShapeDtypeStruct((SEQ, N_PAD), BF16),
        input_output_aliases={0: 0},
    )(dp_in, dqkv, rc, rs)


TAIL_T = 256


def _tail(p, ya, o_g, l_g, x, target, wpm, wpd, wout, post_g):
    tm = TAIL_T

    def body(pgz_ref, ya_ref, o0_ref, o1_ref, o2_ref, l0_ref, l1_ref, l2_ref, x_ref, t_ref,
             wpm_ref, wpd_ref, wout_ref, pg_ref,
             dp_ref, dy_ref, mg_ref, dt_ref, ua_ref, dpa_ref, ud_ref, dpd_ref, dya_ref, dyd_ref,
             yd_ref, lse_ref, loss_ref, dgp_ref):
        l0, l1, l2 = l0_ref[...], l1_ref[...], l2_ref[...]
        mx = jnp.maximum(jnp.maximum(l0, l1), l2)
        e0, e1, e2 = jnp.exp(l0 - mx), jnp.exp(l1 - mx), jnp.exp(l2 - mx)
        den = e0 + e1 + e2
        yd = (e0 * o0_ref[...] + e1 * o1_ref[...] + e2 * o2_ref[...]) / den
        yd_ref[...] = yd
        lse_ref[...] = mx + jnp.log(den)
        ya = ya_ref[...]

        gm, gd = pgz_ref[:, 0:1024], pgz_ref[:, 1024:2048]
        zm, zd = pgz_ref[:, 2048:2560], pgz_ref[:, 2560:3072]
        szm, szd = _sigmoid(zm), _sigmoid(zd)
        sm, sd = zm * szm, zd * szd
        ua = (ya * sm).astype(BF16)
        ud = (yd * sd).astype(BF16)
        ua_ref[...] = ua
        ud_ref[...] = ud
        pa = _nn(ua, wpm_ref[...])
        pd = _nn(ud, wpd_ref[...])
        sgm, sgd = _sigmoid(gm), _sigmoid(gd)
        mg = (sgm * pa + sgd * pd).astype(BF16)
        mg_ref[...] = mg
        t = _nn(mg, wout_ref[...])
        r3 = lax.rsqrt(jnp.mean(t * t, axis=-1, keepdims=True) + EPS)
        n = t * r3
        pg = pg_ref[...]
        err = x_ref[...] + n * pg - t_ref[...]
        lpart = jnp.sum(err * err, axis=0, keepdims=True)

        dy = err * (1.0 / D_MODEL)
        dy_ref[...] = dy
        gpart = jnp.sum(dy * n, axis=0, keepdims=True)
        dn = dy * pg
        dt = (r3 * (dn - n * jnp.mean(dn * n, axis=-1, keepdims=True))).astype(BF16)
        dt_ref[...] = dt
        dmg = _nt(dt, wout_ref[...])
        dpa = (dmg * sgm).astype(BF16)
        dpd = (dmg * sgd).astype(BF16)
        dpa_ref[...] = dpa
        dpd_ref[...] = dpd
        dp_ref[:, 0:1024] = (dmg * pa * sgm * (1.0 - sgm)).astype(BF16)
        dp_ref[:, 1024:2048] = (dmg * pd * sgd * (1.0 - sgd)).astype(BF16)
        dua = _nt(dpa, wpm_ref[...])
        dud = _nt(dpd, wpd_ref[...])
        dya_ref[...] = dua * sm
        dyd_ref[...] = dud * sd
        dp_ref[:, 2048:2560] = (dua * ya * szm * (1.0 + zm * (1.0 - szm))).astype(BF16)
        dp_ref[:, 2560:3072] = (dud * yd * szd * (1.0 + zd * (1.0 - szd))).astype(BF16)

        @pl.when(pl.program_id(0) == 0)
        def _():
            loss_ref[...] = lpart
            dgp_ref[...] = gpart

        @pl.when(pl.program_id(0) > 0)
        def _():
            loss_ref[...] += lpart
            dgp_ref[...] += gpart

    def rows(w):
        return pl.BlockSpec((tm, w), lambda i: (i, 0))

    def full(shape):
        return pl.BlockSpec(shape, lambda i: (0, 0))

    def sds(w, dt):
        return jax.ShapeDtypeStruct((SEQ, w), dt)

    return pl.pallas_call(
        body, name="tail", grid=(SEQ // tm,),
        in_specs=[rows(3072), rows(512), rows(512), rows(512), rows(512), rows(512), rows(512), rows(512),
                  rows(1024), rows(1024), full((512, 1024)), full((512, 1024)), full((1024, 1024)), full((1, 1024))],
        out_specs=[rows(3072), rows(1024), rows(1024), rows(1024), rows(512), rows(1024), rows(512), rows(1024),
                   rows(512), rows(512), rows(512), rows(512), full((1, 1024)), full((1, 1024))],
        out_shape=[sds(N_PAD, BF16), sds(1024, F32), sds(1024, BF16), sds(1024, BF16), sds(512, BF16),
                   sds(1024, BF16), sds(512, BF16), sds(1024, BF16), sds(512, F32), sds(512, F32),
                   sds(512, F32), sds(512, F32),
                   jax.ShapeDtypeStruct((1, 1024), F32), jax.ShapeDtypeStruct((1, 1024), F32)],
        compiler_params=_cparams(),
    )(p, ya, o_g[0], o_g[1], o_g[2], l_g[0], l_g[1], l_g[2], x, target, wpm, wpd, wout, post_g)


def _sum_parts(recv, own, me, tr, name):
    n, r, w = recv.shape
    if r % tr:
        return _sum_parts_cols(recv, own, me, name)
    own_spec = (pl.BlockSpec((tr, w), lambda i, me_ref: (i, 0)) if own.ndim == 2
                else pl.BlockSpec((None, tr, w), lambda i, me_ref: (me_ref[0], i, 0)))

    def body(me_ref, p_ref, own_ref, o_ref):
        mine = own_ref[...].astype(F32)
        acc = jnp.zeros((tr, w), F32)
        for s in range(n):
            acc = acc + jnp.where(me_ref[0] == s, mine, p_ref[s].astype(F32))
        o_ref[...] = acc

    return pl.pallas_call(
        body, name=name,
        grid_spec=pltpu.PrefetchScalarGridSpec(
            num_scalar_prefetch=1, grid=(r // tr,),
            in_specs=[pl.BlockSpec((n, tr, w), lambda i, me_ref: (0, i, 0)), own_spec],
            out_specs=pl.BlockSpec((tr, w), lambda i, me_ref: (i, 0))),
        out_shape=jax.ShapeDtypeStruct((r, w), F32),
    )(me.reshape(1), recv, own)


def _sum_parts_cols(recv, own, me, name):
    n, r, w = recv.shape
    tc = 128

    def body(me_ref, p_ref, own_ref, o_ref):
        mine = own_ref[...].astype(F32)
        acc = jnp.zeros((r, tc), F32)
        for s in range(n):
            acc = acc + jnp.where(me_ref[0] == s, mine, p_ref[s].astype(F32))
        o_ref[...] = acc

    return pl.pallas_call(
        body, name=name,
        grid_spec=pltpu.PrefetchScalarGridSpec(
            num_scalar_prefetch=1, grid=(w // tc,),
            in_specs=[pl.BlockSpec((n, r, tc), lambda i, me_ref: (0, 0, i)),
                      pl.BlockSpec((None, r, tc), lambda i, me_ref: (me_ref[0], 0, i))],
            out_specs=pl.BlockSpec((r, tc), lambda i, me_ref: (0, i))),
        out_shape=jax.ShapeDtypeStruct((r, w), F32),
    )(me.reshape(1), recv, own)


def _adamw(w, g, m, v, name):
    lead = w.shape[:-2]
    r, c = w.shape[-2:]
    tr = max([t for t in range(8, 257, 8) if r % t == 0], default=r)
    c1 = 1.0 - ADAM_B1 ** ADAM_STEP
    c2 = 1.0 - ADAM_B2 ** ADAM_STEP

    def body(w_ref, g_ref, m_ref, v_ref, d_ref, nm_ref, nv_ref):
        gv = g_ref[...]
        nm = ADAM_B1 * m_ref[...] + (1.0 - ADAM_B1) * gv
        nv = ADAM_B2 * v_ref[...] + (1.0 - ADAM_B2) * (gv * gv)
        nm_ref[...] = nm
        nv_ref[...] = nv
        d_ref[...] = -ADAM_LR * ((nm / c1) / (jnp.sqrt(nv / c2) + ADAM_EPS) + ADAM_WD * w_ref[...])

    zeros = (0,) * len(lead)
    spec = pl.BlockSpec((1,) * len(lead) + (tr, c), lambda i: zeros + (i, 0))
    sd = jax.ShapeDtypeStruct(w.shape, F32)
    return pl.pallas_call(
        body, name=name, grid=(r // tr,),
        in_specs=[spec] * 4, out_specs=[spec] * 3, out_shape=[sd] * 3,
    )(w, g, m, v)


def _adamw_in(w_t, m_t, v_t, own_half, swapped, core):
    r, c = SHARD_SHAPES[0]
    tr = max(t for t in range(8, 257, 8) if r % t == 0)
    c1 = 1.0 - ADAM_B1 ** ADAM_STEP
    c2 = 1.0 - ADAM_B2 ** ADAM_STEP

    def body(core_ref, w_ref, m_ref, v_ref, own_ref, sw_ref, d_ref, nm_ref, nv_ref, g_ref):
        own = own_ref[...]
        col_half = lax.broadcasted_iota(jnp.int32, (tr, c), 1) // (c // 2)
        gv = jnp.where(col_half == core_ref[0], jnp.concatenate([own, own], axis=1), sw_ref[...])
        g_ref[0] = gv
        nm = ADAM_B1 * m_ref[0] + (1.0 - ADAM_B1) * gv
        nv = ADAM_B2 * v_ref[0] + (1.0 - ADAM_B2) * (gv * gv)
        nm_ref[0] = nm
        nv_ref[0] = nv
        d_ref[0] = -ADAM_LR * ((nm / c1) / (jnp.sqrt(nv / c2) + ADAM_EPS) + ADAM_WD * w_ref[0])

    full = pl.BlockSpec((1, tr, c), lambda i, core_ref: (0, i, 0))
    sd = jax.ShapeDtypeStruct((1, r, c), F32)
    return pl.pallas_call(
        body, name="adamw_in",
        grid_spec=pltpu.PrefetchScalarGridSpec(
            num_scalar_prefetch=1, grid=(r // tr,),
            in_specs=[full, full, full, pl.BlockSpec((tr, c // 2), lambda i, core_ref: (i, 0)),
                      pl.BlockSpec((tr, c), lambda i, core_ref: (i, 0))],
            out_specs=[full] * 4),
        out_shape=[sd] * 4,
    )(core.reshape(1), w_t, m_t, v_t, own_half, swapped)


ANY = pl.BlockSpec(memory_space=pl.ANY)


def _my_place():
    return lax.axis_index("x"), lax.axis_index("y"), lax.axis_index("c")


HBM = pl.BlockSpec(memory_space=pltpu.HBM)
SEM = pl.BlockSpec(memory_space=pltpu.SEMAPHORE)
DATAFLOW = pltpu.SideEffectType.DATAFLOW_SIDE_EFFECTING


def _other_chips(x, y):
    return [(1 - x, y), (x, 1 - y), (1 - x, 1 - y)]


def _half(mi, hc):
    r, c = SHARD_SHAPES[mi]
    if mi == 0:
        return pl.ds(0, r), pl.ds(pl.multiple_of(hc * (c // 2), 128), c // 2)
    return pl.ds(pl.multiple_of(hc * (r // 2), 16), r // 2), pl.ds(0, c)


def _gather_start(mats, landing):
    n = N_MATS

    def body(*refs):
        m_refs, land_refs = refs[:n], refs[n:2 * n]
        send_sems, recv_sems, token = refs[2 * n], refs[2 * n + 1], refs[-1]
        x, y, c = _my_place()
        for mi in range(n):
            rows, cols = _half(mi, c)
            for j, (cx, cy) in enumerate(_other_chips(x, y)):
                pltpu.make_async_remote_copy(
                    src_ref=m_refs[mi].at[rows, cols], dst_ref=land_refs[mi].at[2 * x + y, rows, cols],
                    send_sem=send_sems.at[mi * 3 + j], recv_sem=recv_sems.at[mi * 3 + j],
                    device_id=(cx, cy, c), device_id_type=MESH).start()
        token[...] = jnp.zeros_like(token)

    hbm = [pltpu.HBM(a.shape, a.dtype) for a in list(mats) + list(landing)]
    outs = pl.pallas_call(
        body, name="gather_start",
        out_shape=(pltpu.SemaphoreType.DMA((3 * n,)), pltpu.SemaphoreType.DMA((3 * n,)), *hbm,
                   jax.ShapeDtypeStruct((8, 128), F32)),
        in_specs=[HBM] * (2 * n), out_specs=(SEM, SEM, *[HBM] * (2 * n), pl.BlockSpec(memory_space=pltpu.VMEM)),
        input_output_aliases={i: 2 + i for i in range(2 * n)},
        compiler_params=pltpu.CompilerParams(has_side_effects=DATAFLOW),
    )(*[pltpu.with_memory_space_constraint(a, pltpu.HBM) for a in list(mats) + list(landing)])
    return outs[:-1], outs[-1]


def _gather_wait(handle, after):
    n = N_MATS

    def body(*refs):
        m_refs, land_refs = refs[:n], refs[n:2 * n]
        send_sems, recv_sems = refs[2 * n], refs[2 * n + 1]
        x, y, c = _my_place()
        for mi in range(n):
            rows, cols = _half(mi, c)
            for j, (cx, cy) in enumerate(_other_chips(x, y)):
                pltpu.make_async_remote_copy(
                    src_ref=m_refs[mi].at[rows, cols], dst_ref=land_refs[mi].at[2 * x + y, rows, cols],
                    send_sem=send_sems.at[mi * 3 + j], recv_sem=recv_sems.at[mi * 3 + j],
                    device_id=(cx, cy, c), device_id_type=MESH).wait_send()
                got = land_refs[mi].at[2 * cx + cy, rows, cols]
                pltpu.make_async_remote_copy(
                    src_ref=got, dst_ref=got, send_sem=send_sems.at[mi * 3 + j], recv_sem=recv_sems.at[mi * 3 + j],
                    device_id=(cx, cy, c), device_id_type=MESH).wait_recv()

    bufs = handle[2:]
    res = pl.pallas_call(
        body, name="gather_wait", out_shape=tuple(pltpu.HBM(b.shape, b.dtype) for b in bufs),
        in_specs=[HBM] * (2 * n) + [SEM, SEM, ANY], out_specs=tuple([HBM] * (2 * n)),
        input_output_aliases={i: i for i in range(2 * n)},
        compiler_params=pltpu.CompilerParams(has_side_effects=DATAFLOW),
    )(*bufs, handle[0], handle[1], after)
    return list(res[n:])


def _share_halves(gathered):
    n = N_MATS

    def body(*refs):
        out_refs = refs[n:2 * n]
        send_sems, recv_sems = refs[2 * n:]
        x, y, c = _my_place()
        sibling = (x, y, 1 - c)
        sends = []
        for mi in range(n):
            for j, (cx, cy) in enumerate(_other_chips(x, y)):
                mine = out_refs[mi].at[(2 * cx + cy,) + _half(mi, c)]
                cp = pltpu.make_async_remote_copy(src_ref=mine, dst_ref=mine, send_sem=send_sems.at[mi * 3 + j],
                                                  recv_sem=recv_sems.at[mi * 3 + j], device_id=sibling, device_id_type=MESH)
                cp.start()
                sends.append(cp)
        for mi in range(n):
            for j, (cx, cy) in enumerate(_other_chips(x, y)):
                theirs = out_refs[mi].at[(2 * cx + cy,) + _half(mi, 1 - c)]
                pltpu.make_async_remote_copy(src_ref=theirs, dst_ref=theirs, send_sem=send_sems.at[mi * 3 + j],
                                             recv_sem=recv_sems.at[mi * 3 + j], device_id=sibling, device_id_type=MESH).wait_recv()
        for cp in sends:
            cp.wait_send()

    return pl.pallas_call(
        body, name="share_halves",
        in_specs=[ANY] * n, out_specs=[ANY] * n,
        out_shape=[jax.ShapeDtypeStruct(g.shape, g.dtype) for g in gathered],
        input_output_aliases={i: i for i in range(n)},
        scratch_shapes=[pltpu.SemaphoreType.DMA((3 * n,)), pltpu.SemaphoreType.DMA((3 * n,))],
    )(*gathered)


def _peers(x, y, c):
    out = []
    for k in range(1, 8):
        px, py, pc = x ^ (k >> 2), y ^ ((k >> 1) & 1), c ^ (k & 1)
        out.append((k - 1, (px, py, pc), 4 * px + 2 * py + pc))
    return out


def _exchange_start(parts, name):
    n = len(parts)

    def body(*refs):
        p_refs, land_refs = refs[:n], refs[n:2 * n]
        send_sems, recv_sems, token = refs[2 * n], refs[2 * n + 1], refs[-1]
        x, y, c = _my_place()
        me = 4 * x + 2 * y + c
        for k, dev, peer in _peers(x, y, c):
            for mi in range(n):
                pltpu.make_async_remote_copy(
                    src_ref=p_refs[mi].at[peer], dst_ref=land_refs[mi].at[me], send_sem=send_sems.at[k * n + mi],
                    recv_sem=recv_sems.at[k * n + mi], device_id=dev, device_id_type=MESH).start()
        token[...] = jnp.zeros_like(token)

    hbm = [pltpu.HBM(p.shape, p.dtype) for p in parts]
    outs = pl.pallas_call(
        body, name=name + "_start",
        out_shape=(pltpu.SemaphoreType.DMA((7 * n,)), pltpu.SemaphoreType.DMA((7 * n,)), *hbm, *hbm,
                   jax.ShapeDtypeStruct((8, 128), F32)),
        in_specs=[HBM] * (2 * n), out_specs=(SEM, SEM, *[HBM] * (2 * n), pl.BlockSpec(memory_space=pltpu.VMEM)),
        input_output_aliases={i: 2 + i for i in range(2 * n)},
        compiler_params=pltpu.CompilerParams(has_side_effects=DATAFLOW),
    )(*[pltpu.with_memory_space_constraint(p, pltpu.HBM) for p in parts],
      *[pltpu.with_memory_space_constraint(lax.empty(p.shape, p.dtype), pltpu.HBM) for p in parts])
    return (name, outs[:-1]), outs[-1]


def _exchange_wait(handle, after):
    name, outs = handle
    n = (len(outs) - 2) // 2

    def body(*refs):
        p_refs, land_refs = refs[:n], refs[n:2 * n]
        send_sems, recv_sems = refs[2 * n], refs[2 * n + 1]
        x, y, c = _my_place()
        me = 4 * x + 2 * y + c
        for k, dev, peer in _peers(x, y, c):
            for mi in range(n):
                pltpu.make_async_remote_copy(
                    src_ref=p_refs[mi].at[peer], dst_ref=land_refs[mi].at[me], send_sem=send_sems.at[k * n + mi],
                    recv_sem=recv_sems.at[k * n + mi], device_id=dev, device_id_type=MESH).wait_send()
                slot = land_refs[mi].at[peer]
                pltpu.make_async_remote_copy(
                    src_ref=slot, dst_ref=slot, send_sem=send_sems.at[k * n + mi],
                    recv_sem=recv_sems.at[k * n + mi], device_id=dev, device_id_type=MESH).wait_recv()

    bufs = outs[2:]
    res = pl.pallas_call(
        body, name=name + "_wait", out_shape=tuple(pltpu.HBM(b.shape, b.dtype) for b in bufs),
        in_specs=[HBM] * (2 * n) + [SEM, SEM, ANY], out_specs=tuple([HBM] * (2 * n)),
        input_output_aliases={i: i for i in range(2 * n)},
        compiler_params=pltpu.CompilerParams(has_side_effects=DATAFLOW),
    )(*bufs, outs[0], outs[1], after)
    return list(res[n:])


def _swap_halves(halves, gvec):
    def place(ref, mi, hc):
        return ref.at[:, pl.ds(pl.multiple_of(hc * 512, 128), 512)] if mi == 0 else ref.at[hc]

    def body(*refs):
        g_refs, gv_ref = refs[:N_MATS], refs[N_MATS]
        out_refs, rg_ref = refs[N_MATS + 1:2 * N_MATS + 1], refs[2 * N_MATS + 1]
        send_sems, recv_sems = refs[2 * N_MATS + 2:]
        x, y, c = _my_place()
        me = 4 * x + 2 * y + c
        sends = []
        for mi in range(N_MATS):
            cp = pltpu.make_async_remote_copy(src_ref=g_refs[mi], dst_ref=place(out_refs[mi], mi, c), send_sem=send_sems.at[mi],
                                              recv_sem=recv_sems.at[mi], device_id=(x, y, 1 - c), device_id_type=MESH)
            cp.start()
            sends.append(cp)
        for k, dev, peer in _peers(x, y, c):
            cp = pltpu.make_async_remote_copy(src_ref=gv_ref, dst_ref=rg_ref.at[me], send_sem=send_sems.at[N_MATS + k],
                                              recv_sem=recv_sems.at[N_MATS + k], device_id=dev, device_id_type=MESH)
            cp.start()
            sends.append(cp)
        for mi in range(N_MATS):
            got = place(out_refs[mi], mi, 1 - c)
            pltpu.make_async_remote_copy(src_ref=got, dst_ref=got, send_sem=send_sems.at[mi], recv_sem=recv_sems.at[mi],
                                         device_id=(x, y, 1 - c), device_id_type=MESH).wait_recv()
        for k, dev, peer in _peers(x, y, c):
            got = rg_ref.at[peer]
            pltpu.make_async_remote_copy(src_ref=got, dst_ref=got, send_sem=send_sems.at[N_MATS + k],
                                         recv_sem=recv_sems.at[N_MATS + k], device_id=dev, device_id_type=MESH).wait_recv()
        for cp in sends:
            cp.wait_send()

    outs = pl.pallas_call(
        body, name="swap_halves",
        in_specs=[ANY] * (N_MATS + 1), out_specs=[ANY] * (N_MATS + 1),
        out_shape=[jax.ShapeDtypeStruct(SHARD_SHAPES[0], F32)]
        + [jax.ShapeDtypeStruct((2, r // 2, c), F32) for r, c in SHARD_SHAPES[1:]]
        + [jax.ShapeDtypeStruct((8, 8, N_GVEC), F32)],
        scratch_shapes=[pltpu.SemaphoreType.DMA((N_MATS + 7,)), pltpu.SemaphoreType.DMA((N_MATS + 7,))],
    )(*halves, gvec)
    return outs[:N_MATS], outs[N_MATS]


def _set_slot(arr, block, idx):
    return lax.dynamic_update_slice(arr, block[None], (idx,) + (0,) * block.ndim)


PAD_RUNS = ((6304, 8352, 0), (5280, 6304, COL_Z), (672, 5280, COL_QKV), (0, 640, COL_LAT), (640, 672, COL_LAT + 704))
W_IN_SHARD = 2088


def _full_weights(gathered):
    def cols(a):
        return jnp.concatenate([a[s] for s in range(4)], axis=1)

    w_uq, w_ukv, w_pm, w_pd = [cols(a) for a in gathered[1:5]]
    w_out = gathered[5].reshape(D_MODEL, D_MODEL)
    w_in_t = gathered[0].reshape(4 * W_IN_SHARD, D_MODEL)
    pieces, at = [], 0
    for lo, hi, pad_lo in sorted(PAD_RUNS, key=lambda t: t[2]):
        if pad_lo > at:
            pieces.append(jnp.zeros((pad_lo - at, D_MODEL), w_in_t.dtype))
        pieces.append(w_in_t[lo:hi])
        at = pad_lo + hi - lo
    pieces.append(jnp.zeros((N_PAD - at, D_MODEL), w_in_t.dtype))
    w_pad_t = jnp.concatenate(pieces, axis=0)
    z32 = jnp.zeros((Q_RANK, 32), w_uq.dtype)
    wuq_pad = jnp.concatenate([t for h in range(MLA_HEADS) for t in (w_uq[:, h * 96:(h + 1) * 96], z32)], axis=1)
    z64 = jnp.zeros((KV_RANK, 64), w_ukv.dtype)
    wk_pad = jnp.concatenate([t for h in range(MLA_HEADS) for t in (w_ukv[:, h * 128:h * 128 + 64], z64)], axis=1)
    wv = jnp.concatenate([w_ukv[:, h * 128 + 64:(h + 1) * 128] for h in range(MLA_HEADS)], axis=1)
    return w_pad_t.T, w_pad_t, wuq_pad, wk_pad, wv, w_pm, w_pd, w_out


W_IN_LAT = 672


def _grad_parts_in_early(dwt_early):
    def in_block(s, h):
        cols = slice(h * 512, (h + 1) * 512)
        out = []
        for lo, hi, pad_lo in sorted(PAD_RUNS):
            a_, b_ = max(lo, s * W_IN_SHARD), min(hi, (s + 1) * W_IN_SHARD)
            if a_ < b_:
                out.append(jnp.zeros((b_ - a_, 512), dwt_early.dtype) if pad_lo >= COL_LAT
                           else dwt_early[pad_lo + a_ - lo:pad_lo + b_ - lo, cols])
        return jnp.concatenate(out, axis=0)

    return jnp.stack([in_block(s, h) for s in range(4) for h in range(2)])


def _grad_parts_in_late(dwt_late):
    rows = jnp.concatenate([dwt_late[0:640], dwt_late[704:736]], axis=0)
    zero = jnp.zeros((W_IN_LAT, 512), dwt_late.dtype)
    return jnp.stack([rows[:, 0:512], rows[:, 512:1024]] + [zero] * 6)


def _col_blocks(m):
    r, c = m.shape[0] // 2, m.shape[1] // 4
    return jnp.stack([m[h * r:(h + 1) * r, s * c:(s + 1) * c] for s in range(4) for h in range(2)])


def _grad_parts_mla(dwuq_pad, dwk_pad, dwv):
    d_uq = jnp.concatenate([dwuq_pad[:, h * 128:h * 128 + 96] for h in range(MLA_HEADS)], axis=1)
    d_ukv = jnp.concatenate([t for h in range(MLA_HEADS) for t in (dwk_pad[:, h * 128:h * 128 + 64], dwv[:, h * 64:(h + 1) * 64])],
                            axis=1)
    return [_col_blocks(d_uq), _col_blocks(d_ukv)]


def _rope_tables(positions, token=None):
    pos = positions.reshape(SEQ).astype(F32)
    if token is not None:
        pos = pos + token[0, 0]
    lane = jnp.arange(128)

    def table(rot, first, period):
        inv = ROPE_THETA ** (-jnp.arange(0, rot, 2, dtype=F32) / rot)
        half = rot // 2
        off = lane % period - first
        in1, in2 = (off >= 0) & (off < half), (off >= half) & (off < rot)
        inv_lane = jnp.where(in1 | in2, inv[jnp.clip(off % half, 0, half - 1)], 0.0)
        sign = jnp.where(in1, -1.0, 1.0).astype(F32)
        ang = pos[:, None] * inv_lane[None, :]
        return jnp.cos(ang), jnp.sin(ang) * sign[None, :]

    return table(32, 64, 128), table(16, 0, 64)


class _Links:
    def __init__(self, mats, chip, me):
        landing = [_set_slot(lax.empty((4,) + m.shape, m.dtype), m, chip) for m in mats]
        self.gather, self.token = _gather_start(mats, landing)
        self.me, self.sent, self.handles, self.sums = me, {}, {}, {}

    def weights(self, after):
        return _share_halves(_gather_wait(self.gather, after))

    def send(self, blocks, name):
        self.sent[name] = blocks
        self.handles[name], token = _exchange_start(blocks, name)
        return token

    def collect(self, name, after, parts):
        recv = _exchange_wait(self.handles[name], after)
        for r, own, part in zip(recv, self.sent[name], parts):
            self.sums[part] = _sum_parts(r, own, self.me, 64, "sum_grad_" + part)
        return tuple(self.sums[part] for part in parts)


def _device_grads(x, positions, target, gains, links):
    pre_g, q_g, kv_g, post_g = gains
    (mc, ms), (dc, ds) = _rope_tables(positions, links.token)
    h = _prenorm_fwd(x, pre_g, links.token)
    w_pad, w_pad_t, wuq_pad, wk_pad, wv, w_pm, w_pd, w_out = _full_weights(links.weights(h))

    p = _matmul(h, w_pad, "nn", F32, 1024, 1408, 1024, "in_proj")
    cqn, ckvn, q, k, v = _mla_prep_fwd(p, q_g, kv_g, wuq_pad, wk_pad, wv, mc, ms)
    ya, lse_m = _mla_flash_fwd(q, k, v)
    qkv = [_dil_prep_fwd(p, dc, ds, g) for g in range(3)]
    o_g, l_g = zip(*[_dil_attn_fwd2(qkv[g], g) for g in range(3)])
    (dp, dy, mg, dt, ua, dpa, ud, dpd, dya, dyd, yd, lse_d, loss_cols, dg_post) = _tail(
        p, ya, o_g, l_g, x, target, w_pm, w_pd, w_out, post_g)

    for g in range(3):
        dqkv = _dil_attn_bwd2(qkv[g], dyd, yd, lse_d, g)
        dp = _dil_prep_bwd(dp, dqkv, dc, ds, g)
    dw_early = _matmul(dp, h, "tn", BF16, 1536, 1024, 2048, "dw_in_early", a_cols=(0, COL_LAT // 1536))
    dwpm = _matmul(ua, dpa, "tn", BF16, 512, 1024, 512, "dw_proj_mla")
    dwpd = _matmul(ud, dpd, "tn", BF16, 512, 1024, 512, "dw_proj_dil")
    dwout = _matmul(mg, dt, "tn", BF16, 1024, 1024, 512, "dw_out")
    token = links.send([_grad_parts_in_early(dw_early), _col_blocks(dwpm), _col_blocks(dwpd),
                        dwout.reshape(8, 128, D_MODEL)], "exchange_early")

    dq, dk, dv = _mla_flash_bwd(q, k, v, ya, dya, lse_m, token)
    dp, dqb, dkb, dvb, dg_q, dg_kv = _mla_prep_bwd(dp, p, dq, dk, dv, q_g, kv_g, wuq_pad, wk_pad, wv, mc, ms)
    dwuq_pad = _matmul(cqn, dqb, "tn", BF16, Q_RANK, 1024, 512, "dw_uq")
    dwk_pad = _matmul(ckvn, dkb, "tn", BF16, KV_RANK, 1024, 512, "dw_k")
    dwv = _matmul(ckvn, dvb, "tn", BF16, KV_RANK, 512, 512, "dw_v")
    dw_late = _matmul(dp, h, "tn", BF16, N_LAT, 1024, 2048, "dw_in_late", a_cols=(COL_LAT // N_LAT, 1))
    token = links.send([_grad_parts_in_late(dw_late)] + _grad_parts_mla(dwuq_pad, dwk_pad, dwv), "exchange_late")
    early = links.collect("exchange_early", dw_late, ("in_early", "pm", "pd", "out"))

    dh = _matmul(dp, w_pad_t, "nn", F32, 1024, 1024, 1408, "dh", (token,) + tuple(early))
    grad_x, dg_pre = _prenorm_bwd(x, dh, dy, pre_g)
    links.collect("exchange_late", grad_x, ("in_late", "uq", "ukv"))

    loss_part = jnp.pad((jnp.sum(loss_cols) * (0.5 / D_MODEL)).reshape(1, 1), ((0, 0), (0, N_GVEC - N_GAINS - 1)))
    gvec = jnp.concatenate([dg_pre, dg_q, dg_kv, dg_post, loss_part], axis=1)
    return grad_x, gvec


def kernel(x, positions, pre_norm_g, w_in, q_norm_g, w_uq, kv_norm_g, w_ukv, w_proj_mla, w_proj_dil, w_out, post_norm_g, loss_target, m_pre_norm_g, m_w_in, m_q_norm_g, m_w_uq, m_kv_norm_g, m_w_ukv, m_w_proj_mla, m_w_proj_dil, m_w_out, m_post_norm_g, v_pre_norm_g, v_w_in, v_q_norm_g, v_w_uq, v_kv_norm_g, v_w_ukv, v_w_proj_mla, v_w_proj_dil, v_w_out, v_post_norm_g):
    xi, yi, ci = _my_place()
    chip, me = 2 * xi + yi, 4 * xi + 2 * yi + ci
    mats = [jnp.swapaxes(w_in, 1, 2)] + [w_uq, w_ukv, w_proj_mla, w_proj_dil, w_out]
    mats = [w.reshape(w.shape[1:]).astype(BF16) for w in mats]
    links = _Links(mats, chip, me)
    gains = (pre_norm_g, q_norm_g, kv_norm_g, post_norm_g)
    grad_x, gvec = _device_grads(x[0], positions, loss_target[0], gains, links)

    sums = links.sums
    in_e = sums["in_early"]
    half_in = jnp.concatenate([in_e[:W_IN_LAT] + jnp.where(chip == 0, sums["in_late"], 0.0), in_e[W_IN_LAT:]], axis=0)
    halves = [half_in, sums["uq"], sums["ukv"], sums["pm"], sums["pd"], sums["out"]]
    gvec8 = jnp.pad(gvec, ((0, 7), (0, 0)))
    swapped, recv_gains = _swap_halves(halves, gvec8)
    g_gains = _sum_parts(recv_gains, gvec8, me, 8, "sum_gain_parts")[0:1]
    loss = g_gains[0, N_GAINS]
    sw = lambda a: jnp.swapaxes(a, 1, 2)
    d_in, m_in, v_in, g_in = [sw(o) for o in _adamw_in(sw(w_in), sw(m_w_in), sw(v_w_in), half_in, swapped[0], ci)]
    g_mats = [g_in] + [_set_slot(s, hf, ci).reshape((1,) + shp)
                       for s, hf, shp in zip(swapped[1:], halves[1:], SHARD_SHAPES[1:])]

    off = [0, 1024, 1408, 1664, 2688]
    g_gain = [g_gains[:, off[i]:off[i + 1]] for i in range(4)]
    grads = [g_gain[0], g_mats[0], g_gain[1], g_mats[1], g_gain[2], g_mats[2], g_mats[3], g_mats[4], g_mats[5], g_gain[3]]
    ws = [pre_norm_g, w_in, q_norm_g, w_uq, kv_norm_g, w_ukv, w_proj_mla, w_proj_dil, w_out, post_norm_g]
    ms = [m_pre_norm_g, m_w_in, m_q_norm_g, m_w_uq, m_kv_norm_g, m_w_ukv, m_w_proj_mla, m_w_proj_dil, m_w_out, m_post_norm_g]
    vs = [v_pre_norm_g, v_w_in, v_q_norm_g, v_w_uq, v_kv_norm_g, v_w_ukv, v_w_proj_mla, v_w_proj_dil, v_w_out, v_post_norm_g]
    deltas, new_m, new_v = [], [], []
    for i, (w, g, m, v) in enumerate(zip(ws, grads, ms, vs)):
        if w is w_in:
            d_, m_, v_ = d_in, m_in, v_in
        elif w.shape[-1] % 128 and w.shape[-2] % 128 == 0:
            g = jnp.swapaxes(g, 1, 2)
            grads[i] = jnp.swapaxes(g, 1, 2)
            d_, m_, v_ = [jnp.swapaxes(o, 1, 2) for o in
                          _adamw(jnp.swapaxes(w, 1, 2), g, jnp.swapaxes(m, 1, 2), jnp.swapaxes(v, 1, 2), f"adamw_{i}")]
        else:
            d_, m_, v_ = _adamw(w, g, m, v, f"adamw_{i}")
        deltas.append(d_)
        new_m.append(m_)
        new_v.append(v_)
    return (loss, grad_x.reshape(x.shape), *grads, *deltas, *new_m, *new_v)
```

```python
import jax
import jax.numpy as jnp
from jax import lax
from jax.experimental import pallas as pl
from jax.experimental.pallas import tpu as pltpu

F32 = jnp.float32
BF16 = jnp.bfloat16

SEQ = 4096
D_MODEL = 1024
EPS = 1e-6
ROPE_THETA = 500000.0
MLA_HEADS = 8
Q_RANK = 384
KV_RANK = 256
MLA_SCALE = 96.0 ** -0.5
MLA_ROPE_HALF = 16
DIL_DILATIONS = (1, 4, 16)
DIL_ROPE_HALF = 8
DIL_SCALE = 0.125
BAND = 128

N_LAT = 768
COL_Z, COL_QKV, COL_LAT = 2048, 3072, 7680
N_PAD = 8448
IN_SPLITS = (384, 256, 32, 4608, 512, 512, 1024, 1024)

SHARD_SHAPES = ((2088, 1024), (384, 192), (256, 256), (512, 256), (512, 256), (256, 1024))
N_MATS = len(SHARD_SHAPES)
N_GAINS = 2688
N_GVEC = N_GAINS + 128

ADAM_LR, ADAM_B1, ADAM_B2, ADAM_EPS, ADAM_WD, ADAM_STEP = 0.001, 0.9, 0.999, 1e-08, 0.01, 10

VMEM_LIMIT = 56 * 1024 * 1024
NEG = -1e30
MESH = pl.DeviceIdType.MESH


def _cparams(**kw):
    return pltpu.CompilerParams(vmem_limit_bytes=VMEM_LIMIT, **kw)


def _dot(a, b, dims):
    return lax.dot_general(a, b, (dims, ((), ())), preferred_element_type=F32)


def _nn(a, b):
    return _dot(a, b, ((1,), (0,)))


def _nt(a, b):
    return _dot(a, b, ((1,), (1,)))


def _tn(a, b):
    return _dot(a, b, ((0,), (0,)))


def _rope_lanes(shape, half, period, first):
    lane = lax.broadcasted_iota(jnp.int32, shape, len(shape) - 1) % period
    return (lane >= first) & (lane < first + half), (lane >= first + half) & (lane < first + 2 * half)


def _rope_fwd(x, c, s, half, lanes):
    x1, _ = lanes
    return x * c + jnp.where(x1, pltpu.roll(x, 128 - half, 1), pltpu.roll(x, half, 1)) * s


def _rope_bwd(g, c, s, half, lanes):
    x1, x2 = lanes
    gs = g * s
    return g * c + jnp.where(x2, pltpu.roll(gs, half, 1), jnp.where(x1, pltpu.roll(gs, 128 - half, 1), 0.0))


def _sigmoid(x):
    return 1.0 / (1.0 + jnp.exp(-x))


def _after(token):
    tokens = [t for t in (token if isinstance(token, (tuple, list)) else [token]) if t is not None]
    return tokens, [pl.BlockSpec(memory_space=pl.ANY)] * len(tokens)


def _matmul(a, b, mode, out_dtype, tm, tn, tk, name, token=None, b_cols=None, a_cols=None):
    after, after_specs = _after(token)
    if mode == "nn":
        (m, k), n = a.shape, b.shape[1]
        first = 0
        if b_cols is not None:
            first, n = b_cols[0], b_cols[1] * tn
        a_spec = pl.BlockSpec((tm, tk), lambda j, i, kk: (i, kk))
        b_spec = pl.BlockSpec((tk, tn), lambda j, i, kk: (kk, j + first))
        dot = _nn
    elif mode == "nt":
        (m, k), n = a.shape, b.shape[0]
        a_spec = pl.BlockSpec((tm, tk), lambda j, i, kk: (i, kk))
        b_spec = pl.BlockSpec((tn, tk), lambda j, i, kk: (j, kk))
        dot = _nt
    else:
        (k, m), n = a.shape, b.shape[1]
        first = 0
        if a_cols is not None:
            first, m = a_cols[0], a_cols[1] * tm
        a_spec = pl.BlockSpec((tk, tm), lambda j, i, kk: (kk, i + first))
        b_spec = pl.BlockSpec((tk, tn), lambda j, i, kk: (kk, j))
        dot = _tn
    assert m % tm == 0 and n % tn == 0 and k % tk == 0, (name, m, n, k, tm, tn, tk)
    nk = k // tk

    def body(a_ref, b_ref, *rest):
        o_ref, acc_ref = rest[-2:]
        kk = pl.program_id(2)
        part = dot(a_ref[...], b_ref[...])

        @pl.when(kk == 0)
        def _():
            acc_ref[...] = part

        @pl.when(kk > 0)
        def _():
            acc_ref[...] += part

        @pl.when(kk == nk - 1)
        def _():
            o_ref[...] = acc_ref[...].astype(o_ref.dtype)

    return pl.pallas_call(
        body, name=name, grid=(n // tn, m // tm, nk),
        in_specs=[a_spec, b_spec] + after_specs,
        out_specs=pl.BlockSpec((tm, tn), lambda j, i, kk: (i, j)),
        out_shape=jax.ShapeDtypeStruct((m, n), out_dtype),
        scratch_shapes=[pltpu.VMEM((tm, tn), F32)],
        compiler_params=_cparams(),
    )(a, b, *after)


def _prenorm_fwd(x, g, token=None):
    tm = 512
    after, after_specs = _after(token)

    def body(x_ref, g_ref, *rest):
        xv = x_ref[...]
        r = lax.rsqrt(jnp.mean(xv * xv, axis=-1, keepdims=True) + EPS)
        rest[-1][...] = (xv * r * g_ref[...]).astype(BF16)

    return pl.pallas_call(
        body, name="prenorm_fwd", grid=(SEQ // tm,),
        in_specs=[pl.BlockSpec((tm, D_MODEL), lambda i: (i, 0)), pl.BlockSpec((1, D_MODEL), lambda i: (0, 0))] + after_specs,
        out_specs=pl.BlockSpec((tm, D_MODEL), lambda i: (i, 0)),
        out_shape=jax.ShapeDtypeStruct((SEQ, D_MODEL), BF16),
    )(x, g, *after)


def _prenorm_bwd(x, dh, dy, g):
    tm = 512

    def body(x_ref, dh_ref, dy_ref, g_ref, gx_ref, dg_ref):
        xv = x_ref[...]
        r = lax.rsqrt(jnp.mean(xv * xv, axis=-1, keepdims=True) + EPS)
        n = xv * r
        dhv = dh_ref[...]
        dn = dhv * g_ref[...]
        gx_ref[...] = dy_ref[...] + r * (dn - n * jnp.mean(dn * n, axis=-1, keepdims=True))
        part = jnp.sum(dhv * n, axis=0, keepdims=True)

        @pl.when(pl.program_id(0) == 0)
        def _():
            dg_ref[...] = part

        @pl.when(pl.program_id(0) > 0)
        def _():
            dg_ref[...] += part

    row = pl.BlockSpec((tm, D_MODEL), lambda i: (i, 0))
    vec = pl.BlockSpec((1, D_MODEL), lambda i: (0, 0))
    return pl.pallas_call(
        body, name="prenorm_bwd", grid=(SEQ // tm,),
        in_specs=[row, row, row, vec], out_specs=[row, vec],
        out_shape=[jax.ShapeDtypeStruct((SEQ, D_MODEL), F32), jax.ShapeDtypeStruct((1, D_MODEL), F32)],
        compiler_params=_cparams(),
    )(x, dh, dy, g)


def _mla_prep_fwd(p, qg, kvg, wuq, wk, wv, rc, rs):
    tm = 512

    def body(lat_ref, qg_ref, kvg_ref, wuq_ref, wk_ref, wv_ref, c_ref, s_ref,
             cqn_ref, ckvn_ref, q_ref, k_ref, v_ref):
        c, s = c_ref[...], s_ref[...]
        lanes = _rope_lanes((tm, 128), MLA_ROPE_HALF, 128, 64)
        cq = lat_ref[:, 0:Q_RANK]
        r1 = lax.rsqrt(jnp.mean(cq * cq, axis=-1, keepdims=True) + EPS)
        cqn = (cq * r1 * qg_ref[...]).astype(BF16)
        cqn_ref[...] = cqn
        q = _nn(cqn, wuq_ref[...])
        for h in range(MLA_HEADS):
            sl = slice(h * 128, (h + 1) * 128)
            q_ref[:, sl] = (_rope_fwd(q[:, sl], c, s, MLA_ROPE_HALF, lanes) * MLA_SCALE).astype(BF16)
        ckv = lat_ref[:, Q_RANK:Q_RANK + KV_RANK]
        r2 = lax.rsqrt(jnp.mean(ckv * ckv, axis=-1, keepdims=True) + EPS)
        ckvn = (ckv * r2 * kvg_ref[...]).astype(BF16)
        ckvn_ref[...] = ckvn
        krr = _rope_fwd(lat_ref[:, Q_RANK + KV_RANK:N_LAT], c, s, MLA_ROPE_HALF, lanes)
        kn = _nn(ckvn, wk_ref[...])
        for h in range(MLA_HEADS):
            sl = slice(h * 128, (h + 1) * 128)
            k_ref[:, sl] = (kn[:, sl] + krr).astype(BF16)
        v_ref[...] = _nn(ckvn, wv_ref[...]).astype(BF16)

    def full(shape):
        return pl.BlockSpec(shape, lambda i: (0, 0))

    def rows(w):
        return pl.BlockSpec((tm, w), lambda i: (i, 0))

    return pl.pallas_call(
        body, name="mla_prep_fwd", grid=(SEQ // tm,),
        in_specs=[pl.BlockSpec((tm, N_LAT), lambda i: (i, COL_LAT // N_LAT)),
                  full((1, Q_RANK)), full((1, KV_RANK)), full((Q_RANK, 1024)), full((KV_RANK, 1024)),
                  full((KV_RANK, 512)), rows(128), rows(128)],
        out_specs=[rows(Q_RANK), rows(KV_RANK), rows(1024), rows(1024), rows(512)],
        out_shape=[jax.ShapeDtypeStruct((SEQ, Q_RANK), BF16), jax.ShapeDtypeStruct((SEQ, KV_RANK), BF16),
                   jax.ShapeDtypeStruct((SEQ, 1024), BF16), jax.ShapeDtypeStruct((SEQ, 1024), BF16),
                   jax.ShapeDtypeStruct((SEQ, 512), BF16)],
        compiler_params=_cparams(),
    )(p, qg, kvg, wuq, wk, wv, rc, rs)


def _mla_prep_bwd(dp_in, p, dq, dk, dv, qg, kvg, wuq, wk, wv, rc, rs):
    tm = 512

    def body(dp_any, lat_ref, dq_ref, dk_ref, dv_ref, qg_ref, kvg_ref, wuq_ref, wk_ref, wv_ref,
             c_ref, s_ref, dp_ref, dqb_ref, dkb_ref, dvb_ref, dgq_ref, dgkv_ref):
        del dp_any
        c, s = c_ref[...], s_ref[...]
        lanes = _rope_lanes((tm, 128), MLA_ROPE_HALF, 128, 64)
        lane = lax.broadcasted_iota(jnp.int32, (tm, 128), 1)
        dkr = jnp.zeros((tm, 128), F32)
        for h in range(MLA_HEADS):
            sl = slice(h * 128, (h + 1) * 128)
            dqb_ref[:, sl] = _rope_bwd(dq_ref[:, sl] * MLA_SCALE, c, s, MLA_ROPE_HALF, lanes).astype(BF16)
            dkh = dk_ref[:, sl]
            dkr = dkr + dkh
            dkb_ref[:, sl] = jnp.where(lane < 64, dkh, 0.0).astype(BF16)
        dkr = jnp.where((lane >= 64) & (lane < 96), dkr, 0.0)
        dkr = _rope_bwd(dkr, c, s, MLA_ROPE_HALF, lanes)
        dvb = dv_ref[...].astype(BF16)
        dvb_ref[...] = dvb

        cq = lat_ref[:, 0:Q_RANK]
        r1 = lax.rsqrt(jnp.mean(cq * cq, axis=-1, keepdims=True) + EPS)
        n1 = cq * r1
        dcqn = _nt(dqb_ref[...], wuq_ref[...])
        dn1 = dcqn * qg_ref[...]
        dcq = r1 * (dn1 - n1 * jnp.mean(dn1 * n1, axis=-1, keepdims=True))
        pq = jnp.sum(dcqn * n1, axis=0, keepdims=True)

        ckv = lat_ref[:, Q_RANK:Q_RANK + KV_RANK]
        r2 = lax.rsqrt(jnp.mean(ckv * ckv, axis=-1, keepdims=True) + EPS)
        n2 = ckv * r2
        dckvn = _nt(dkb_ref[...], wk_ref[...]) + _nt(dvb, wv_ref[...])
        dn2 = dckvn * kvg_ref[...]
        dckv = r2 * (dn2 - n2 * jnp.mean(dn2 * n2, axis=-1, keepdims=True))
        pkv = jnp.sum(dckvn * n2, axis=0, keepdims=True)

        dp_ref[:, 0:Q_RANK] = dcq.astype(BF16)
        dp_ref[:, Q_RANK:Q_RANK + KV_RANK] = dckv.astype(BF16)
        dp_ref[:, Q_RANK + KV_RANK:N_LAT] = dkr.astype(BF16)

        @pl.when(pl.program_id(0) == 0)
        def _():
            dgq_ref[...] = pq
            dgkv_ref[...] = pkv

        @pl.when(pl.program_id(0) > 0)
        def _():
            dgq_ref[...] += pq
            dgkv_ref[...] += pkv

    def full(shape):
        return pl.BlockSpec(shape, lambda i: (0, 0))

    def rows(w):
        return pl.BlockSpec((tm, w), lambda i: (i, 0))

    lat = pl.BlockSpec((tm, N_LAT), lambda i: (i, COL_LAT // N_LAT))
    return pl.pallas_call(
        body, name="mla_prep_bwd", grid=(SEQ // tm,),
        in_specs=[pl.BlockSpec(memory_space=pl.ANY), lat, rows(1024), rows(1024), rows(512),
                  full((1, Q_RANK)), full((1, KV_RANK)), full((Q_RANK, 1024)), full((KV_RANK, 1024)),
                  full((KV_RANK, 512)), rows(128), rows(128)],
        out_specs=[lat, rows(1024), rows(1024), rows(512), full((1, Q_RANK)), full((1, KV_RANK))],
        out_shape=[jax.ShapeDtypeStruct((SEQ, N_PAD), BF16), jax.ShapeDtypeStruct((SEQ, 1024), BF16),
                   jax.ShapeDtypeStruct((SEQ, 1024), BF16), jax.ShapeDtypeStruct((SEQ, 512), BF16),
                   jax.ShapeDtypeStruct((1, Q_RANK), F32), jax.ShapeDtypeStruct((1, KV_RANK), F32)],
        input_output_aliases={0: 0},
        compiler_params=_cparams(),
    )(dp_in, p, dq, dk, dv, qg, kvg, wuq, wk, wv, rc, rs)


FLASH_T = 1024


def _head_half(shape, hh):
    lane = lax.broadcasted_iota(jnp.int32, shape, 1)
    return (lane < 64) if hh == 0 else (lane >= 64)


def _diag_keep(nr, nk):
    row = lax.broadcasted_iota(jnp.int32, (nr, nk), 0)
    col = lax.broadcasted_iota(jnp.int32, (nr, nk), 1)
    return row + (nk - nr) >= col


def _tri_steps(nb, q_major):
    if q_major:
        pairs = [(i, kb) for i in range(nb) for kb in range(i + 1)]
    else:
        pairs = [(i, kb) for kb in range(nb) for i in range(kb, nb)]
    return jnp.asarray([p[0] for p in pairs], jnp.int32), jnp.asarray([p[1] for p in pairs], jnp.int32)


def _mla_flash_fwd(q, k, v):
    t = FLASH_T
    nb = SEQ // t
    qtab, ktab = _tri_steps(nb, True)

    def body(qi_ref, ki_ref, q_ref, k_ref, v_ref, o_ref, lse_ref, m_scr, l_scr, acc_scr):
        step = pl.program_id(1)
        i, kb = qi_ref[step], ki_ref[step]

        @pl.when(kb == 0)
        def _():
            m_scr[...] = jnp.full_like(m_scr, NEG)
            l_scr[...] = jnp.zeros_like(l_scr)
            acc_scr[...] = jnp.zeros_like(acc_scr)

        def update(r0, nr, nk, diagonal):
            rs = slice(r0, r0 + nr)
            vv = v_ref[0:nk, :]
            for hh in range(2):
                sl = slice(hh * 128, (hh + 1) * 128)
                s = _nt(q_ref[rs, sl], k_ref[0:nk, sl])
                if diagonal:
                    s = jnp.where(_diag_keep(nr, nk), s, NEG)
                m_prev = m_scr[hh, rs, :]
                m_new = jnp.maximum(m_prev, jnp.max(s, axis=-1, keepdims=True))
                pr = jnp.exp(s - jnp.tile(m_new, (1, nk // 128)))
                alpha = jnp.exp(m_prev - m_new)
                l_scr[hh, rs, :] = alpha * l_scr[hh, rs, :] + jnp.sum(pr, axis=-1, keepdims=True)
                acc_scr[hh, rs, :] = alpha * acc_scr[hh, rs, :] + _nn(pr.astype(BF16), vv)
                m_scr[hh, rs, :] = m_new

        @pl.when(kb < i)
        def _():
            update(0, t, t, False)

        @pl.when(kb == i)
        def _():
            update(0, t // 2, t // 2, True)
            update(t // 2, t // 2, t, True)
            o0 = acc_scr[0] / l_scr[0]
            o1 = acc_scr[1] / l_scr[1]
            o_ref[...] = jnp.where(_head_half((t, 128), 0), o0, o1)
            for hh in range(2):
                lse_ref[:, hh * 128:(hh + 1) * 128] = m_scr[hh] + jnp.log(l_scr[hh])

    grid_spec = pltpu.PrefetchScalarGridSpec(
        num_scalar_prefetch=2, grid=(4, qtab.shape[0]),
        in_specs=[pl.BlockSpec((t, 256), lambda j, s, qi, ki: (qi[s], j)),
                  pl.BlockSpec((t, 256), lambda j, s, qi, ki: (ki[s], j)),
                  pl.BlockSpec((t, 128), lambda j, s, qi, ki: (ki[s], j))],
        out_specs=[pl.BlockSpec((t, 128), lambda j, s, qi, ki: (qi[s], j)),
                   pl.BlockSpec((t, 256), lambda j, s, qi, ki: (qi[s], j))],
        scratch_shapes=[pltpu.VMEM((2, t, 128), F32), pltpu.VMEM((2, t, 128), F32), pltpu.VMEM((2, t, 128), F32)])
    return pl.pallas_call(
        body, name="mla_flash_fwd", grid_spec=grid_spec,
        out_shape=[jax.ShapeDtypeStruct((SEQ, 512), F32), jax.ShapeDtypeStruct((SEQ, 1024), F32)],
        compiler_params=_cparams(),
    )(qtab, ktab, q, k, v)


def _mla_flash_bwd(q, k, v, o, do, lse, token=None):
    t = FLASH_T
    nb = SEQ // t
    qtab, ktab = _tri_steps(nb, False)
    after, after_specs = _after(token)

    def body(qi_ref, ki_ref, q_ref, k_ref, v_ref, o_ref, do_ref, lse_ref, *rest):
        dq_ref, dk_ref, dv_ref, dk_scr, dv_scr = rest[-5:]
        step = pl.program_id(1)
        i, kb = qi_ref[step], ki_ref[step]

        @pl.when(step == 0)
        def _():
            dq_ref[...] = jnp.zeros_like(dq_ref)

        @pl.when(i == kb)
        def _():
            dk_scr[...] = jnp.zeros_like(dk_scr)
            dv_scr[...] = jnp.zeros_like(dv_scr)

        def update(r0, nr, nk, diagonal):
            rs = slice(r0, r0 + nr)
            vv = v_ref[0:nk, :]
            ov = o_ref[rs, :]
            dov = do_ref[rs, :]
            rows = pl.ds(pl.multiple_of(i * t + r0, t // 2), nr)
            for hh in range(2):
                sl = slice(hh * 128, (hh + 1) * 128)
                qh, kh = q_ref[rs, sl], k_ref[0:nk, sl]
                s = _nt(qh, kh)
                if diagonal:
                    s = jnp.where(_diag_keep(nr, nk), s, NEG)
                pr = jnp.exp(s - jnp.tile(lse_ref[rs, sl], (1, nk // 128)))
                dom = jnp.where(_head_half((nr, 128), hh), dov, 0.0)
                domb = dom.astype(BF16)
                dv_scr[0:nk, :] += _tn(pr.astype(BF16), domb)
                dpr = _nt(domb, vv)
                delta = jnp.sum(dom * ov, axis=-1, keepdims=True)
                ds = (pr * (dpr - delta)).astype(BF16)
                dq_ref[rows, sl] += _nn(ds, kh)
                dk_scr[hh, 0:nk, :] += _tn(ds, qh)

        @pl.when(i > kb)
        def _():
            update(0, t, t, False)

        @pl.when(i == kb)
        def _():
            update(0, t // 2, t // 2, True)
            update(t // 2, t // 2, t, True)

        @pl.when(i == nb - 1)
        def _():
            dk_ref[:, 0:128] = dk_scr[0]
            dk_ref[:, 128:256] = dk_scr[1]
            dv_ref[...] = dv_scr[...]

    qi_map = lambda j, s, qi, ki: (qi[s], j)
    ki_map = lambda j, s, qi, ki: (ki[s], j)
    grid_spec = pltpu.PrefetchScalarGridSpec(
        num_scalar_prefetch=2, grid=(4, qtab.shape[0]),
        in_specs=[pl.BlockSpec((t, 256), qi_map), pl.BlockSpec((t, 256), ki_map), pl.BlockSpec((t, 128), ki_map),
                  pl.BlockSpec((t, 128), qi_map), pl.BlockSpec((t, 128), qi_map), pl.BlockSpec((t, 256), qi_map)]
        + after_specs,
        out_specs=[pl.BlockSpec((SEQ, 256), lambda j, s, qi, ki: (0, j)), pl.BlockSpec((t, 256), ki_map),
                   pl.BlockSpec((t, 128), ki_map)],
        scratch_shapes=[pltpu.VMEM((2, t, 128), F32), pltpu.VMEM((t, 128), F32)])
    return pl.pallas_call(
        body, name="mla_flash_bwd", grid_spec=grid_spec,
        out_shape=[jax.ShapeDtypeStruct((SEQ, 1024), F32), jax.ShapeDtypeStruct((SEQ, 1024), F32),
                   jax.ShapeDtypeStruct((SEQ, 512), F32)],
        compiler_params=_cparams(),
    )(qtab, ktab, q, k, v, o, do, lse, *after)


DIL_UNROLL = 4


def _strided(start, size, d):
    return pl.ds(start, size) if d == 1 else pl.ds(start, size, stride=d)


def _dil_prep_fwd(p, rc, rs, g):
    d = DIL_DILATIONS[g]
    sub_len = SEQ // d
    ch = min(sub_len, 512)

    def body(p_ref, c_ref, s_ref, o_ref, x_scr):
        tq = pl.program_id(0)
        lanes = _rope_lanes((ch, 128), DIL_ROPE_HALF, 64, 0)
        o_ref[0, 0:BAND, :] = jnp.zeros((BAND, 128), BF16)

        @pl.when(tq < 2)
        def _():
            mult = jnp.where(tq == 0, DIL_SCALE, 1.0).astype(F32)
            for c0 in range(0, SEQ, ch):
                rows = pl.ds(c0, ch)
                x_scr[rows, :] = _rope_fwd(p_ref[rows, :], c_ref[rows, :] * mult, s_ref[rows, :] * mult, DIL_ROPE_HALF, lanes)

        def gather(src):
            for r in range(d):
                for c0 in range(0, sub_len, ch):
                    at = BAND + r * sub_len + c0
                    o_ref[0, at:at + ch, :] = src[_strided(r + c0 * d, ch, d), :].astype(BF16)

        @pl.when(tq < 2)
        def _():
            gather(x_scr)

        @pl.when(tq == 2)
        def _():
            gather(p_ref)

    tab = pl.BlockSpec((SEQ, 128), lambda tq, pr: (0, 0))
    return pl.pallas_call(
        body, name=f"dil_prep_fwd_g{g}", grid=(3, 4),
        in_specs=[pl.BlockSpec((SEQ, 128), lambda tq, pr: (0, COL_QKV // 128 + (tq * 3 + g) * 4 + pr)), tab, tab],
        out_specs=pl.BlockSpec((1, BAND + SEQ, 128), lambda tq, pr: (tq, 0, pr)),
        out_shape=jax.ShapeDtypeStruct((3, BAND + SEQ, 512), BF16),
        scratch_shapes=[pltpu.VMEM((SEQ, 128), F32)],
        compiler_params=_cparams(),
    )(p, rc, rs)


DIL_ST = 1024
DIL_NB = DIL_ST // BAND


def _band_keep(g, b, t):
    nbs = SEQ // DIL_DILATIONS[g] // BAND
    row = lax.broadcasted_iota(jnp.int32, (BAND, 2 * BAND), 0)
    col = lax.broadcasted_iota(jnp.int32, (BAND, 2 * BAND), 1)
    cur = (col >= BAND) & (row >= col - BAND)
    prev = (col < BAND) & (col >= row)
    if nbs >= DIL_NB:
        if b > 0:
            return cur | prev
        return cur | (prev & ((t * DIL_NB) % nbs != 0))
    return cur | prev if b % nbs else cur


def _dil_tok(g, b, t):
    d = DIL_DILATIONS[g]
    nbs = SEQ // d // BAND
    gb = t * DIL_NB + b
    return _strided((gb % nbs) * BAND * d + gb // nbs, BAND, d)


def _dil_attn_fwd2(qkv, g):
    def body(q_ref, k_ref, v_ref, o_ref, l_ref, s_scr, p_scr, o_scr):
        t = pl.program_id(1)
        base = t * DIL_ST
        half0 = _head_half((DIL_ST, 128), 0)
        lse_h = []
        for hh in range(2):
            half = _head_half((BAND, 128), hh)
            for b in range(DIL_NB):
                qv = q_ref[0, pl.ds(pl.multiple_of(base + (b + 1) * BAND, BAND), BAND), :]
                k2 = k_ref[0, pl.ds(pl.multiple_of(base + b * BAND, BAND), 2 * BAND), :]
                sb = _nt(jnp.where(half, qv, jnp.zeros_like(qv)), k2)
                s_scr[b * BAND:(b + 1) * BAND, :] = jnp.where(_band_keep(g, b, t), sb, NEG)
            s = s_scr[...]
            m = jnp.max(s, axis=-1, keepdims=True)
            pr = jnp.exp(s - m)
            den = jnp.sum(pr, axis=-1, keepdims=True)
            p_scr[...] = pr.astype(BF16)
            for b in range(DIL_NB):
                v2 = v_ref[0, pl.ds(pl.multiple_of(base + b * BAND, BAND), 2 * BAND), :]
                o_scr[hh, b * BAND:(b + 1) * BAND, :] = _nn(p_scr[b * BAND:(b + 1) * BAND, :], v2)
            o_scr[hh] = o_scr[hh] / den
            lse_h.append(m + jnp.log(den))
        out = jnp.where(half0, o_scr[0], o_scr[1])
        lse = jnp.where(half0, lse_h[0], lse_h[1])
        for b in range(DIL_NB):
            tok = _dil_tok(g, b, t)
            o_ref[tok, :] = out[b * BAND:(b + 1) * BAND, :]
            l_ref[tok, :] = lse[b * BAND:(b + 1) * BAND, :]

    def inp(tq):
        return pl.BlockSpec((1, BAND + SEQ, 128), lambda pr, t: (tq, 0, pr))

    out = pl.BlockSpec((SEQ, 128), lambda pr, t: (0, pr))
    return pl.pallas_call(
        body, name=f"dil_attn_fwd_g{g}", grid=(4, SEQ // DIL_ST),
        in_specs=[inp(0), inp(1), inp(2)], out_specs=[out, out],
        out_shape=[jax.ShapeDtypeStruct((SEQ, 512), F32), jax.ShapeDtypeStruct((SEQ, 512), F32)],
        scratch_shapes=[pltpu.VMEM((DIL_ST, 2 * BAND), F32), pltpu.VMEM((DIL_ST, 2 * BAND), BF16),
                        pltpu.VMEM((2, DIL_ST, 128), F32)],
        compiler_params=_cparams(),
    )(qkv, qkv, qkv)


def _dil_attn_bwd2(qkv, dyd, yd, lse_all, g, token=None):
    d = DIL_DILATIONS[g]
    sub_len = SEQ // d
    nst = SEQ // DIL_ST
    after, after_specs = _after(token)

    def body(q_ref, k_ref, v_ref, do_ref, y_ref, l_ref, *rest):
        out_ref, dk_scr, dv_scr, s_scr, dp_scr, p_scr, ds_scr, do_scr, y_scr, l_scr, dq_scr = rest[-11:]
        t = pl.program_id(1)
        base = t * DIL_ST

        @pl.when(t == 0)
        def _():
            dk_scr[...] = jnp.zeros_like(dk_scr)
            dv_scr[...] = jnp.zeros_like(dv_scr)

        for b in range(DIL_NB):
            tok = _dil_tok(g, b, t)
            do_scr[b * BAND:(b + 1) * BAND, :] = do_ref[tok, :]
            y_scr[b * BAND:(b + 1) * BAND, :] = y_ref[tok, :]
            l_scr[b * BAND:(b + 1) * BAND, :] = l_ref[tok, :]
        for hh in range(2):
            half = _head_half((BAND, 128), hh)
            half_st = _head_half((DIL_ST, 128), hh)
            dom = jnp.where(half_st, do_scr[...], 0.0)
            delta = jnp.sum(dom * y_scr[...], axis=-1, keepdims=True)
            lcol = jnp.max(jnp.where(half_st, l_scr[...], NEG), axis=-1, keepdims=True)
            for b in range(DIL_NB):
                rows = slice(b * BAND, (b + 1) * BAND)
                qv = q_ref[0, pl.ds(pl.multiple_of(base + (b + 1) * BAND, BAND), BAND), :]
                band = pl.ds(pl.multiple_of(base + b * BAND, BAND), 2 * BAND)
                sb = _nt(jnp.where(half, qv, jnp.zeros_like(qv)), k_ref[0, band, :])
                s_scr[rows, :] = jnp.where(_band_keep(g, b, t), sb, NEG)
                dp_scr[rows, :] = _nt(dom[rows, :].astype(BF16), v_ref[0, band, :])
            pr = jnp.exp(s_scr[...] - lcol)
            p_scr[...] = pr.astype(BF16)
            ds_scr[...] = (pr * (dp_scr[...] - delta)).astype(BF16)
            for b in range(DIL_NB):
                rows = slice(b * BAND, (b + 1) * BAND)
                qv = q_ref[0, pl.ds(pl.multiple_of(base + (b + 1) * BAND, BAND), BAND), :]
                band = pl.ds(pl.multiple_of(base + b * BAND, BAND), 2 * BAND)
                dqb = jnp.where(half, _nn(ds_scr[rows, :], k_ref[0, band, :]), 0.0)
                if hh == 0:
                    dq_scr[rows, :] = dqb
                else:
                    dq_scr[rows, :] += dqb
                half2 = _head_half((2 * BAND, 128), hh)
                dk_scr[band, :] += jnp.where(half2, _tn(ds_scr[rows, :], qv), 0.0)
                dv_scr[band, :] += _tn(p_scr[rows, :], dom[rows, :].astype(BF16))
        for b in range(DIL_NB):
            out_ref[pl.ds(0, 1), _dil_tok(g, b, t), :] = dq_scr[b * BAND:(b + 1) * BAND, :][None]

        @pl.when(t == nst - 1)
        def _():
            for r in range(d):
                rows = _strided(r, sub_len, d)
                out_ref[pl.ds(1, 1), rows, :] = dk_scr[BAND + r * sub_len:BAND + (r + 1) * sub_len, :][None]
                out_ref[pl.ds(2, 1), rows, :] = dv_scr[BAND + r * sub_len:BAND + (r + 1) * sub_len, :][None]

    def inp(tq):
        return pl.BlockSpec((1, BAND + SEQ, 128), lambda pr, t: (tq, 0, pr))

    tok_spec = pl.BlockSpec((SEQ, 128), lambda pr, t: (0, pr))
    st = (DIL_ST, 2 * BAND)
    return pl.pallas_call(
        body, name=f"dil_attn_bwd_g{g}", grid=(4, nst),
        in_specs=[inp(0), inp(1), inp(2), tok_spec, tok_spec, tok_spec] + after_specs,
        out_specs=pl.BlockSpec((3, SEQ, 128), lambda pr, t: (0, 0, pr)),
        out_shape=jax.ShapeDtypeStruct((3, SEQ, 512), F32),
        scratch_shapes=[pltpu.VMEM((BAND + SEQ, 128), F32), pltpu.VMEM((BAND + SEQ, 128), F32),
                        pltpu.VMEM(st, F32), pltpu.VMEM(st, F32), pltpu.VMEM(st, BF16), pltpu.VMEM(st, BF16),
                        pltpu.VMEM((DIL_ST, 128), F32), pltpu.VMEM((DIL_ST, 128), F32), pltpu.VMEM((DIL_ST, 128), F32),
                        pltpu.VMEM((DIL_ST, 128), F32)],
        compiler_params=_cparams(),
    )(qkv, qkv, qkv, dyd, yd, lse_all, *after)


def _band_masks():
    row = lax.broadcasted_iota(jnp.int32, (BAND, BAND), 0)
    col = lax.broadcasted_iota(jnp.int32, (BAND, BAND), 1)
    return row >= col, col >= row


def _dil_attn_fwd(qkv, g):
    d = DIL_DILATIONS[g]
    nbs = SEQ // d // BAND
    nblk = SEQ // BAND

    def body(q_ref, k_ref, v_ref, o_ref, l_ref):
        keep_c, keep_p = _band_masks()
        half0 = _head_half((BAND, 128), 0)

        def step(i, carry):
            cur = pl.ds(pl.multiple_of(i * BAND, BAND), BAND)
            prv = pl.ds(pl.multiple_of(jnp.maximum(i - 1, 0) * BAND, BAND), BAND)
            has_prev = (i % nbs) != 0
            qv = q_ref[0, cur, :]
            kc, kp = k_ref[0, cur, :], k_ref[0, prv, :]
            vc, vp = v_ref[0, cur, :], v_ref[0, prv, :]
            outs, lses = [], []
            for hh in range(2):
                qm = jnp.where(_head_half((BAND, 128), hh), qv, jnp.zeros_like(qv))
                sc = jnp.where(keep_c, _nt(qm, kc), NEG)
                sp = jnp.where(keep_p & has_prev, _nt(qm, kp), NEG)
                m = jnp.maximum(jnp.max(sc, axis=-1, keepdims=True), jnp.max(sp, axis=-1, keepdims=True))
                pc, pp = jnp.exp(sc - m), jnp.exp(sp - m)
                den = jnp.sum(pc, axis=-1, keepdims=True) + jnp.sum(pp, axis=-1, keepdims=True)
                o = (_nn(pc.astype(BF16), vc) + _nn(pp.astype(BF16), vp)) / den
                outs.append(o)
                lses.append(jnp.broadcast_to(m + jnp.log(den), (BAND, 128)))
            tok = _strided((i % nbs) * BAND * d + i // nbs, BAND, d)
            o_ref[tok, :] = jnp.where(half0, outs[0], outs[1])
            l_ref[tok, :] = jnp.where(half0, lses[0], lses[1])
            return carry

        lax.fori_loop(0, nblk, step, 0, unroll=DIL_UNROLL)

    def inp(tq):
        return pl.BlockSpec((1, SEQ, 128), lambda pr: (tq, 0, pr))

    out = pl.BlockSpec((SEQ, 128), lambda pr: (0, pr))
    return pl.pallas_call(
        body, name=f"dil_attn_fwd_g{g}", grid=(4,),
        in_specs=[inp(0), inp(1), inp(2)], out_specs=[out, out],
        out_shape=[jax.ShapeDtypeStruct((SEQ, 512), F32), jax.ShapeDtypeStruct((SEQ, 512), F32)],
        compiler_params=_cparams(),
    )(qkv, qkv, qkv)


def _dil_attn_bwd(qkv, dyd, yd, lse_all, g):
    d = DIL_DILATIONS[g]
    sub_len = SEQ // d
    nbs = sub_len // BAND
    nblk = SEQ // BAND

    def body(q_ref, k_ref, v_ref, do_ref, y_ref, l_ref, out_ref, dk_scr, dv_scr):
        keep_c, keep_p = _band_masks()
        dk_scr[...] = jnp.zeros_like(dk_scr)
        dv_scr[...] = jnp.zeros_like(dv_scr)

        def step(i, carry):
            cur = pl.ds(pl.multiple_of(i * BAND, BAND), BAND)
            prv = pl.ds(pl.multiple_of(jnp.maximum(i - 1, 0) * BAND, BAND), BAND)
            has_prev = (i % nbs) != 0
            tok = _strided((i % nbs) * BAND * d + i // nbs, BAND, d)
            qv = q_ref[0, cur, :]
            kc, kp = k_ref[0, cur, :], k_ref[0, prv, :]
            vc, vp = v_ref[0, cur, :], v_ref[0, prv, :]
            dov, yv, lv = do_ref[tok, :], y_ref[tok, :], l_ref[tok, :]
            dq = jnp.zeros((BAND, 128), F32)
            dkc = jnp.zeros((BAND, 128), F32)
            dkp = jnp.zeros((BAND, 128), F32)
            dvc = jnp.zeros((BAND, 128), F32)
            dvp = jnp.zeros((BAND, 128), F32)
            for hh in range(2):
                half = _head_half((BAND, 128), hh)
                qm = jnp.where(half, qv, jnp.zeros_like(qv))
                lcol = jnp.max(jnp.where(half, lv, NEG), axis=-1, keepdims=True)
                pc = jnp.exp(jnp.where(keep_c, _nt(qm, kc), NEG) - lcol)
                pp = jnp.exp(jnp.where(keep_p & has_prev, _nt(qm, kp), NEG) - lcol)
                dom = jnp.where(half, dov, 0.0)
                domb = dom.astype(BF16)
                delta = jnp.sum(dom * yv, axis=-1, keepdims=True)
                dsc = (pc * (_nt(domb, vc) - delta)).astype(BF16)
                dsp = (pp * (_nt(domb, vp) - delta)).astype(BF16)
                dvc = dvc + _tn(pc.astype(BF16), domb)
                dvp = dvp + _tn(pp.astype(BF16), domb)
                dq = dq + jnp.where(half, _nn(dsc, kc) + _nn(dsp, kp), 0.0)
                dkc = dkc + jnp.where(half, _tn(dsc, qv), 0.0)
                dkp = dkp + jnp.where(half, _tn(dsp, qv), 0.0)
            out_ref[pl.ds(0, 1), tok, :] = dq[None]
            dk_scr[cur, :] += dkc
            dk_scr[prv, :] += dkp
            dv_scr[cur, :] += dvc
            dv_scr[prv, :] += dvp
            return carry

        lax.fori_loop(0, nblk, step, 0, unroll=DIL_UNROLL)
        for r in range(d):
            rows = _strided(r, sub_len, d)
            out_ref[pl.ds(1, 1), rows, :] = dk_scr[r * sub_len:(r + 1) * sub_len, :][None]
            out_ref[pl.ds(2, 1), rows, :] = dv_scr[r * sub_len:(r + 1) * sub_len, :][None]

    def inp(tq):
        return pl.BlockSpec((1, SEQ, 128), lambda pr: (tq, 0, pr))

    tok_spec = pl.BlockSpec((SEQ, 128), lambda pr: (0, pr))
    return pl.pallas_call(
        body, name=f"dil_attn_bwd_g{g}", grid=(4,),
        in_specs=[inp(0), inp(1), inp(2), tok_spec, tok_spec, tok_spec],
        out_specs=pl.BlockSpec((3, SEQ, 128), lambda pr: (0, 0, pr)),
        out_shape=jax.ShapeDtypeStruct((3, SEQ, 512), F32),
        scratch_shapes=[pltpu.VMEM((SEQ, 128), F32), pltpu.VMEM((SEQ, 128), F32)],
        compiler_params=_cparams(),
    )(qkv, qkv, qkv, dyd, yd, lse_all)


def _dil_prep_bwd(dp_in, dqkv, rc, rs, g):
    tm = 1024

    def body(dp_any, g_ref, c_ref, s_ref, dp_ref):
        del dp_any
        tq = pl.program_id(0)

        @pl.when(tq == 2)
        def _():
            dp_ref[...] = g_ref[0].astype(BF16)

        @pl.when(tq < 2)
        def _():
            mult = jnp.where(tq == 0, DIL_SCALE, 1.0).astype(F32)
            lanes = _rope_lanes((tm, 128), DIL_ROPE_HALF, 64, 0)
            cv, sv = c_ref[...], s_ref[...] * mult
            cv = cv * mult
            for pr in range(4):
                gv = g_ref[0, :, pr * 128:(pr + 1) * 128]
                dp_ref[:, pr * 128:(pr + 1) * 128] = _rope_bwd(gv, cv, sv, DIL_ROPE_HALF, lanes).astype(BF16)

    tab = pl.BlockSpec((tm, 128), lambda tq, i: (i, 0))
    return pl.pallas_call(
        body, name=f"dil_prep_bwd_g{g}", grid=(3, SEQ // tm),
        in_specs=[pl.BlockSpec(memory_space=pl.ANY),
                  pl.BlockSpec((1, tm, 512), lambda tq, i: (tq, i, 0)), tab, tab],
        out_specs=pl.BlockSpec((tm, 512), lambda tq, i: (i, COL_QKV // 512 + tq * 3 + g)),
        out_shape=jax.ShapeDtypeStruct((SEQ, N_PAD), BF16),
        input_output_aliases={0: 0},
    )(dp_in, dqkv, rc, rs)


TAIL_T = 256


def _tail(p, ya, o_g, l_g, x, target, wpm, wpd, wout, post_g):
    tm = TAIL_T

    def body(pgz_ref, ya_ref, o0_ref, o1_ref, o2_ref, l0_ref, l1_ref, l2_ref, x_ref, t_ref,
             wpm_ref, wpd_ref, wout_ref, pg_ref,
             dp_ref, dy_ref, mg_ref, dt_ref, ua_ref, dpa_ref, ud_ref, dpd_ref, dya_ref, dyd_ref,
             yd_ref, lse_ref, loss_ref, dgp_ref):
        l0, l1, l2 = l0_ref[...], l1_ref[...], l2_ref[...]
        mx = jnp.maximum(jnp.maximum(l0, l1), l2)
        e0, e1, e2 = jnp.exp(l0 - mx), jnp.exp(l1 - mx), jnp.exp(l2 - mx)
        den = e0 + e1 + e2
        yd = (e0 * o0_ref[...] + e1 * o1_ref[...] + e2 * o2_ref[...]) / den
        yd_ref[...] = yd
        lse_ref[...] = mx + jnp.log(den)
        ya = ya_ref[...]

        gm, gd = pgz_ref[:, 0:1024], pgz_ref[:, 1024:2048]
        zm, zd = pgz_ref[:, 2048:2560], pgz_ref[:, 2560:3072]
        szm, szd = _sigmoid(zm), _sigmoid(zd)
        sm, sd = zm * szm, zd * szd
        ua = (ya * sm).astype(BF16)
        ud = (yd * sd).astype(BF16)
        ua_ref[...] = ua
        ud_ref[...] = ud
        pa = _nn(ua, wpm_ref[...])
        pd = _nn(ud, wpd_ref[...])
        sgm, sgd = _sigmoid(gm), _sigmoid(gd)
        mg = (sgm * pa + sgd * pd).astype(BF16)
        mg_ref[...] = mg
        t = _nn(mg, wout_ref[...])
        r3 = lax.rsqrt(jnp.mean(t * t, axis=-1, keepdims=True) + EPS)
        n = t * r3
        pg = pg_ref[...]
        err = x_ref[...] + n * pg - t_ref[...]
        lpart = jnp.sum(err * err, axis=0, keepdims=True)

        dy = err * (1.0 / D_MODEL)
        dy_ref[...] = dy
        gpart = jnp.sum(dy * n, axis=0, keepdims=True)
        dn = dy * pg
        dt = (r3 * (dn - n * jnp.mean(dn * n, axis=-1, keepdims=True))).astype(BF16)
        dt_ref[...] = dt
        dmg = _nt(dt, wout_ref[...])
        dpa = (dmg * sgm).astype(BF16)
        dpd = (dmg * sgd).astype(BF16)
        dpa_ref[...] = dpa
        dpd_ref[...] = dpd
        dp_ref[:, 0:1024] = (dmg * pa * sgm * (1.0 - sgm)).astype(BF16)
        dp_ref[:, 1024:2048] = (dmg * pd * sgd * (1.0 - sgd)).astype(BF16)
        dua = _nt(dpa, wpm_ref[...])
        dud = _nt(dpd, wpd_ref[...])
        dya_ref[...] = dua * sm
        dyd_ref[...] = dud * sd
        dp_ref[:, 2048:2560] = (dua * ya * szm * (1.0 + zm * (1.0 - szm))).astype(BF16)
        dp_ref[:, 2560:3072] = (dud * yd * szd * (1.0 + zd * (1.0 - szd))).astype(BF16)

        @pl.when(pl.program_id(0) == 0)
        def _():
            loss_ref[...] = lpart
            dgp_ref[...] = gpart

        @pl.when(pl.program_id(0) > 0)
        def _():
            loss_ref[...] += lpart
            dgp_ref[...] += gpart

    def rows(w):
        return pl.BlockSpec((tm, w), lambda i: (i, 0))

    def full(shape):
        return pl.BlockSpec(shape, lambda i: (0, 0))

    def sds(w, dt):
        return jax.ShapeDtypeStruct((SEQ, w), dt)

    return pl.pallas_call(
        body, name="tail", grid=(SEQ // tm,),
        in_specs=[rows(3072), rows(512), rows(512), rows(512), rows(512), rows(512), rows(512), rows(512),
                  rows(1024), rows(1024), full((512, 1024)), full((512, 1024)), full((1024, 1024)), full((1, 1024))],
        out_specs=[rows(3072), rows(1024), rows(1024), rows(1024), rows(512), rows(1024), rows(512), rows(1024),
                   rows(512), rows(512), rows(512), rows(512), full((1, 1024)), full((1, 1024))],
        out_shape=[sds(N_PAD, BF16), sds(1024, F32), sds(1024, BF16), sds(1024, BF16), sds(512, BF16),
                   sds(1024, BF16), sds(512, BF16), sds(1024, BF16), sds(512, F32), sds(512, F32),
                   sds(512, F32), sds(512, F32),
                   jax.ShapeDtypeStruct((1, 1024), F32), jax.ShapeDtypeStruct((1, 1024), F32)],
        compiler_params=_cparams(),
    )(p, ya, o_g[0], o_g[1], o_g[2], l_g[0], l_g[1], l_g[2], x, target, wpm, wpd, wout, post_g)


def _sum_parts(recv, own, me, tr, name):
    n, r, w = recv.shape
    if r % tr:
        return _sum_parts_cols(recv, own, me, name)
    own_spec = (pl.BlockSpec((tr, w), lambda i, me_ref: (i, 0)) if own.ndim == 2
                else pl.BlockSpec((None, tr, w), lambda i, me_ref: (me_ref[0], i, 0)))

    def body(me_ref, p_ref, own_ref, o_ref):
        mine = own_ref[...].astype(F32)
        acc = jnp.zeros((tr, w), F32)
        for s in range(n):
            acc = acc + jnp.where(me_ref[0] == s, mine, p_ref[s].astype(F32))
        o_ref[...] = acc

    return pl.pallas_call(
        body, name=name,
        grid_spec=pltpu.PrefetchScalarGridSpec(
            num_scalar_prefetch=1, grid=(r // tr,),
            in_specs=[pl.BlockSpec((n, tr, w), lambda i, me_ref: (0, i, 0)), own_spec],
            out_specs=pl.BlockSpec((tr, w), lambda i, me_ref: (i, 0))),
        out_shape=jax.ShapeDtypeStruct((r, w), F32),
    )(me.reshape(1), recv, own)


def _sum_parts_cols(recv, own, me, name):
    n, r, w = recv.shape
    tc = 128

    def body(me_ref, p_ref, own_ref, o_ref):
        mine = own_ref[...].astype(F32)
        acc = jnp.zeros((r, tc), F32)
        for s in range(n):
            acc = acc + jnp.where(me_ref[0] == s, mine, p_ref[s].astype(F32))
        o_ref[...] = acc

    return pl.pallas_call(
        body, name=name,
        grid_spec=pltpu.PrefetchScalarGridSpec(
            num_scalar_prefetch=1, grid=(w // tc,),
            in_specs=[pl.BlockSpec((n, r, tc), lambda i, me_ref: (0, 0, i)),
                      pl.BlockSpec((None, r, tc), lambda i, me_ref: (me_ref[0], 0, i))],
            out_specs=pl.BlockSpec((r, tc), lambda i, me_ref: (0, i))),
        out_shape=jax.ShapeDtypeStruct((r, w), F32),
    )(me.reshape(1), recv, own)


def _adamw(w, g, m, v, name):
    lead = w.shape[:-2]
    r, c = w.shape[-2:]
    tr = max([t for t in range(8, 257, 8) if r % t == 0], default=r)
    c1 = 1.0 - ADAM_B1 ** ADAM_STEP
    c2 = 1.0 - ADAM_B2 ** ADAM_STEP

    def body(w_ref, g_ref, m_ref, v_ref, d_ref, nm_ref, nv_ref):
        gv = g_ref[...]
        nm = ADAM_B1 * m_ref[...] + (1.0 - ADAM_B1) * gv
        nv = ADAM_B2 * v_ref[...] + (1.0 - ADAM_B2) * (gv * gv)
        nm_ref[...] = nm
        nv_ref[...] = nv
        d_ref[...] = -ADAM_LR * ((nm / c1) / (jnp.sqrt(nv / c2) + ADAM_EPS) + ADAM_WD * w_ref[...])

    zeros = (0,) * len(lead)
    spec = pl.BlockSpec((1,) * len(lead) + (tr, c), lambda i: zeros + (i, 0))
    sd = jax.ShapeDtypeStruct(w.shape, F32)
    return pl.pallas_call(
        body, name=name, grid=(r // tr,),
        in_specs=[spec] * 4, out_specs=[spec] * 3, out_shape=[sd] * 3,
    )(w, g, m, v)


def _adamw_in(w_t, m_t, v_t, own_half, swapped, core):
    r, c = SHARD_SHAPES[0]
    tr = max(t for t in range(8, 257, 8) if r % t == 0)
    c1 = 1.0 - ADAM_B1 ** ADAM_STEP
    c2 = 1.0 - ADAM_B2 ** ADAM_STEP

    def body(core_ref, w_ref, m_ref, v_ref, own_ref, sw_ref, d_ref, nm_ref, nv_ref, g_ref):
        own = own_ref[...]
        col_half = lax.broadcasted_iota(jnp.int32, (tr, c), 1) // (c // 2)
        gv = jnp.where(col_half == core_ref[0], jnp.concatenate([own, own], axis=1), sw_ref[...])
        g_ref[0] = gv
        nm = ADAM_B1 * m_ref[0] + (1.0 - ADAM_B1) * gv
        nv = ADAM_B2 * v_ref[0] + (1.0 - ADAM_B2) * (gv * gv)
        nm_ref[0] = nm
        nv_ref[0] = nv
        d_ref[0] = -ADAM_LR * ((nm / c1) / (jnp.sqrt(nv / c2) + ADAM_EPS) + ADAM_WD * w_ref[0])

    full = pl.BlockSpec((1, tr, c), lambda i, core_ref: (0, i, 0))
    sd = jax.ShapeDtypeStruct((1, r, c), F32)
    return pl.pallas_call(
        body, name="adamw_in",
        grid_spec=pltpu.PrefetchScalarGridSpec(
            num_scalar_prefetch=1, grid=(r // tr,),
            in_specs=[full, full, full, pl.BlockSpec((tr, c // 2), lambda i, core_ref: (i, 0)),
                      pl.BlockSpec((tr, c), lambda i, core_ref: (i, 0))],
            out_specs=[full] * 4),
        out_shape=[sd] * 4,
    )(core.reshape(1), w_t, m_t, v_t, own_half, swapped)


ANY = pl.BlockSpec(memory_space=pl.ANY)


def _my_place():
    return lax.axis_index("x"), lax.axis_index("y"), lax.axis_index("c")


HBM = pl.BlockSpec(memory_space=pltpu.HBM)
SEM = pl.BlockSpec(memory_space=pltpu.SEMAPHORE)
DATAFLOW = pltpu.SideEffectType.DATAFLOW_SIDE_EFFECTING


def _near_chips(x, y):
    return [(1 - x, y), (x, 1 - y)]


def _half(mi, hc):
    r, c = SHARD_SHAPES[mi]
    if mi == 0:
        return pl.ds(0, r), pl.ds(pl.multiple_of(hc * (c // 2), 128), c // 2)
    return pl.ds(pl.multiple_of(hc * (r // 2), 16), r // 2), pl.ds(0, c)


def _gather_copies(m_refs, land_refs, send_sems, recv_sems):
    x, y, c = _my_place()
    out, back = [], []
    for mi in range(N_MATS):
        rows, cols = _half(mi, c)
        for j, (cx, cy) in enumerate(_near_chips(x, y)):
            sems = dict(send_sem=send_sems.at[mi * 2 + j], recv_sem=recv_sems.at[mi * 2 + j],
                        device_id=(cx, cy, c), device_id_type=MESH)
            out.append(pltpu.make_async_remote_copy(src_ref=m_refs[mi].at[rows, cols],
                                                    dst_ref=land_refs[mi].at[2 * x + y, rows, cols], **sems))
            got = land_refs[mi].at[2 * cx + cy, rows, cols]
            back.append(pltpu.make_async_remote_copy(src_ref=got, dst_ref=got, **sems))
    return out, back


def _gather_start(mats, landing):
    n = N_MATS

    def body(*refs):
        out, _ = _gather_copies(refs[:n], refs[n:2 * n], refs[2 * n], refs[2 * n + 1])
        for cp in out:
            cp.start()
        refs[-1][...] = jnp.zeros_like(refs[-1])

    hbm = [pltpu.HBM(a.shape, a.dtype) for a in list(mats) + list(landing)]
    outs = pl.pallas_call(
        body, name="gather_start",
        out_shape=(pltpu.SemaphoreType.DMA((2 * n,)), pltpu.SemaphoreType.DMA((2 * n,)), *hbm,
                   jax.ShapeDtypeStruct((8, 128), F32)),
        in_specs=[HBM] * (2 * n), out_specs=(SEM, SEM, *[HBM] * (2 * n), pl.BlockSpec(memory_space=pltpu.VMEM)),
        input_output_aliases={i: 2 + i for i in range(2 * n)},
        compiler_params=pltpu.CompilerParams(has_side_effects=DATAFLOW),
    )(*[pltpu.with_memory_space_constraint(a, pltpu.HBM) for a in list(mats) + list(landing)])
    return outs[:-1], outs[-1]


def _gather_wait(handle, after):
    n = N_MATS

    def body(*refs):
        out, back = _gather_copies(refs[:n], refs[n:2 * n], refs[2 * n], refs[2 * n + 1])
        for cp, arrival in zip(out, back):
            cp.wait_send()
            arrival.wait_recv()

    bufs = handle[2:]
    after, after_specs = _after(after)
    res = pl.pallas_call(
        body, name="gather_wait", out_shape=tuple(pltpu.HBM(b.shape, b.dtype) for b in bufs),
        in_specs=[HBM] * (2 * n) + [SEM, SEM] + after_specs, out_specs=tuple([HBM] * (2 * n)),
        input_output_aliases={i: i for i in range(2 * n)},
        compiler_params=pltpu.CompilerParams(has_side_effects=DATAFLOW),
    )(*bufs, handle[0], handle[1], *after)
    return list(res[n:])


def _relay_share(gathered):
    n = N_MATS

    def body(*refs):
        out_refs = refs[n:2 * n]
        send_sems, recv_sems = refs[2 * n:]
        x, y, c = _my_place()
        sibling = (x, y, 1 - c)
        relayed = 2 * (x ^ (1 - c)) + (y ^ c)
        relay_to = (x ^ c, y ^ (1 - c), c)
        far = 2 * (1 - x) + (1 - y)
        near = [2 * (1 - x) + y, 2 * x + (1 - y)]

        def copy(k, mi, shard, hc, to):
            blk = out_refs[mi].at[(shard,) + _half(mi, hc)]
            return pltpu.make_async_remote_copy(src_ref=blk, dst_ref=blk, send_sem=send_sems.at[mi * 4 + k],
                                                recv_sem=recv_sems.at[mi * 4 + k], device_id=to, device_id_type=MESH)

        sends = []
        for mi in range(n):
            sends.append(copy(0, mi, relayed, c, relay_to))
            sends += [copy(1 + j, mi, near[j], c, sibling) for j in range(2)]
        for cp in sends:
            cp.start()
        for mi in range(n):
            copy(0, mi, far, c, relay_to).wait_recv()
            cp = copy(3, mi, far, c, sibling)
            cp.start()
            sends.append(cp)
        for mi in range(n):
            for j in range(2):
                copy(1 + j, mi, near[j], 1 - c, sibling).wait_recv()
            copy(3, mi, far, 1 - c, sibling).wait_recv()
        for cp in sends:
            cp.wait_send()

    return pl.pallas_call(
        body, name="relay_share",
        in_specs=[ANY] * n, out_specs=[ANY] * n,
        out_shape=[jax.ShapeDtypeStruct(g.shape, g.dtype) for g in gathered],
        input_output_aliases={i: i for i in range(n)},
        scratch_shapes=[pltpu.SemaphoreType.DMA((4 * n,)), pltpu.SemaphoreType.DMA((4 * n,))],
    )(*gathered)


def _peers(x, y, c):
    out = []
    for k in range(1, 8):
        px, py, pc = x ^ (k >> 2), y ^ ((k >> 1) & 1), c ^ (k & 1)
        out.append((k - 1, (px, py, pc), 4 * px + 2 * py + pc))
    return out


def _exchange_start(parts, name):
    n = len(parts)

    def body(*refs):
        p_refs, land_refs = refs[:n], refs[n:2 * n]
        send_sems, recv_sems, token = refs[2 * n], refs[2 * n + 1], refs[-1]
        x, y, c = _my_place()
        me = 4 * x + 2 * y + c
        for k, dev, peer in _peers(x, y, c):
            for mi in range(n):
                pltpu.make_async_remote_copy(
                    src_ref=p_refs[mi].at[peer], dst_ref=land_refs[mi].at[me], send_sem=send_sems.at[k * n + mi],
                    recv_sem=recv_sems.at[k * n + mi], device_id=dev, device_id_type=MESH).start()
        token[...] = jnp.zeros_like(token)

    hbm = [pltpu.HBM(p.shape, p.dtype) for p in parts]
    outs = pl.pallas_call(
        body, name=name + "_start",
        out_shape=(pltpu.SemaphoreType.DMA((7 * n,)), pltpu.SemaphoreType.DMA((7 * n,)), *hbm, *hbm,
                   jax.ShapeDtypeStruct((8, 128), F32)),
        in_specs=[HBM] * (2 * n), out_specs=(SEM, SEM, *[HBM] * (2 * n), pl.BlockSpec(memory_space=pltpu.VMEM)),
        input_output_aliases={i: 2 + i for i in range(2 * n)},
        compiler_params=pltpu.CompilerParams(has_side_effects=DATAFLOW),
    )(*[pltpu.with_memory_space_constraint(p, pltpu.HBM) for p in parts],
      *[pltpu.with_memory_space_constraint(lax.empty(p.shape, p.dtype), pltpu.HBM) for p in parts])
    return (name, outs[:-1]), outs[-1]


def _exchange_wait(handle, after):
    name, outs = handle
    n = (len(outs) - 2) // 2

    def body(*refs):
        p_refs, land_refs = refs[:n], refs[n:2 * n]
        send_sems, recv_sems = refs[2 * n], refs[2 * n + 1]
        x, y, c = _my_place()
        me = 4 * x + 2 * y + c
        for k, dev, peer in _peers(x, y, c):
            for mi in range(n):
                pltpu.make_async_remote_copy(
                    src_ref=p_refs[mi].at[peer], dst_ref=land_refs[mi].at[me], send_sem=send_sems.at[k * n + mi],
                    recv_sem=recv_sems.at[k * n + mi], device_id=dev, device_id_type=MESH).wait_send()
                slot = land_refs[mi].at[peer]
                pltpu.make_async_remote_copy(
                    src_ref=slot, dst_ref=slot, send_sem=send_sems.at[k * n + mi],
                    recv_sem=recv_sems.at[k * n + mi], device_id=dev, device_id_type=MESH).wait_recv()

    bufs = outs[2:]
    res = pl.pallas_call(
        body, name=name + "_wait", out_shape=tuple(pltpu.HBM(b.shape, b.dtype) for b in bufs),
        in_specs=[HBM] * (2 * n) + [SEM, SEM, ANY], out_specs=tuple([HBM] * (2 * n)),
        input_output_aliases={i: i for i in range(2 * n)},
        compiler_params=pltpu.CompilerParams(has_side_effects=DATAFLOW),
    )(*bufs, outs[0], outs[1], after)
    return list(res[n:])


def _swap_halves(halves, gvec):
    def place(ref, mi, hc):
        return ref.at[:, pl.ds(pl.multiple_of(hc * 512, 128), 512)] if mi == 0 else ref.at[hc]

    def body(*refs):
        g_refs, gv_ref = refs[:N_MATS], refs[N_MATS]
        out_refs, rg_ref = refs[N_MATS + 1:2 * N_MATS + 1], refs[2 * N_MATS + 1]
        send_sems, recv_sems = refs[2 * N_MATS + 2:]
        x, y, c = _my_place()
        me = 4 * x + 2 * y + c
        sends = []
        for mi in range(N_MATS):
            cp = pltpu.make_async_remote_copy(src_ref=g_refs[mi], dst_ref=place(out_refs[mi], mi, c), send_sem=send_sems.at[mi],
                                              recv_sem=recv_sems.at[mi], device_id=(x, y, 1 - c), device_id_type=MESH)
            cp.start()
            sends.append(cp)
        for k, dev, peer in _peers(x, y, c):
            cp = pltpu.make_async_remote_copy(src_ref=gv_ref, dst_ref=rg_ref.at[me], send_sem=send_sems.at[N_MATS + k],
                                              recv_sem=recv_sems.at[N_MATS + k], device_id=dev, device_id_type=MESH)
            cp.start()
            sends.append(cp)
        for mi in range(N_MATS):
            got = place(out_refs[mi], mi, 1 - c)
            pltpu.make_async_remote_copy(src_ref=got, dst_ref=got, send_sem=send_sems.at[mi], recv_sem=recv_sems.at[mi],
                                         device_id=(x, y, 1 - c), device_id_type=MESH).wait_recv()
        for k, dev, peer in _peers(x, y, c):
            got = rg_ref.at[peer]
            pltpu.make_async_remote_copy(src_ref=got, dst_ref=got, send_sem=send_sems.at[N_MATS + k],
                                         recv_sem=recv_sems.at[N_MATS + k], device_id=dev, device_id_type=MESH).wait_recv()
        for cp in sends:
            cp.wait_send()

    outs = pl.pallas_call(
        body, name="swap_halves",
        in_specs=[ANY] * (N_MATS + 1), out_specs=[ANY] * (N_MATS + 1),
        out_shape=[jax.ShapeDtypeStruct(SHARD_SHAPES[0], F32)]
        + [jax.ShapeDtypeStruct((2, r // 2, c), F32) for r, c in SHARD_SHAPES[1:]]
        + [jax.ShapeDtypeStruct((8, 8, N_GVEC), F32)],
        scratch_shapes=[pltpu.SemaphoreType.DMA((N_MATS + 7,)), pltpu.SemaphoreType.DMA((N_MATS + 7,))],
    )(*halves, gvec)
    return outs[:N_MATS], outs[N_MATS]


def _set_slot(arr, block, idx):
    return lax.dynamic_update_slice(arr, block[None], (idx,) + (0,) * block.ndim)


PAD_RUNS = ((6304, 8352, 0), (5280, 6304, COL_Z), (672, 5280, COL_QKV), (0, 640, COL_LAT), (640, 672, COL_LAT + 704))
W_IN_SHARD = 2088


def _full_weights(gathered):
    def cols(a):
        return jnp.concatenate([a[s] for s in range(4)], axis=1)

    w_uq, w_ukv, w_pm, w_pd = [cols(a) for a in gathered[1:5]]
    w_out = gathered[5].reshape(D_MODEL, D_MODEL)
    w_in_t = gathered[0].reshape(4 * W_IN_SHARD, D_MODEL)
    pieces, at = [], 0
    for lo, hi, pad_lo in sorted(PAD_RUNS, key=lambda t: t[2]):
        if pad_lo > at:
            pieces.append(jnp.zeros((pad_lo - at, D_MODEL), w_in_t.dtype))
        pieces.append(w_in_t[lo:hi])
        at = pad_lo + hi - lo
    pieces.append(jnp.zeros((N_PAD - at, D_MODEL), w_in_t.dtype))
    w_pad_t = jnp.concatenate(pieces, axis=0)
    z32 = jnp.zeros((Q_RANK, 32), w_uq.dtype)
    wuq_pad = jnp.concatenate([t for h in range(MLA_HEADS) for t in (w_uq[:, h * 96:(h + 1) * 96], z32)], axis=1)
    z64 = jnp.zeros((KV_RANK, 64), w_ukv.dtype)
    wk_pad = jnp.concatenate([t for h in range(MLA_HEADS) for t in (w_ukv[:, h * 128:h * 128 + 64], z64)], axis=1)
    wv = jnp.concatenate([w_ukv[:, h * 128 + 64:(h + 1) * 128] for h in range(MLA_HEADS)], axis=1)
    return w_pad_t.T, w_pad_t, wuq_pad, wk_pad, wv, w_pm, w_pd, w_out


W_IN_LAT = 672


def _grad_parts_in_early(dwt_early):
    def in_block(s, h):
        cols = slice(h * 512, (h + 1) * 512)
        out = []
        for lo, hi, pad_lo in sorted(PAD_RUNS):
            a_, b_ = max(lo, s * W_IN_SHARD), min(hi, (s + 1) * W_IN_SHARD)
            if a_ < b_:
                out.append(jnp.zeros((b_ - a_, 512), dwt_early.dtype) if pad_lo >= COL_LAT
                           else dwt_early[pad_lo + a_ - lo:pad_lo + b_ - lo, cols])
        return jnp.concatenate(out, axis=0)

    return jnp.stack([in_block(s, h) for s in range(4) for h in range(2)])


def _grad_parts_in_late(dwt_late):
    rows = jnp.concatenate([dwt_late[0:640], dwt_late[704:736]], axis=0)
    zero = jnp.zeros((W_IN_LAT, 512), dwt_late.dtype)
    return jnp.stack([rows[:, 0:512], rows[:, 512:1024]] + [zero] * 6)


def _col_blocks(m):
    r, c = m.shape[0] // 2, m.shape[1] // 4
    return jnp.stack([m[h * r:(h + 1) * r, s * c:(s + 1) * c] for s in range(4) for h in range(2)])


def _grad_parts_mla(dwuq_pad, dwk_pad, dwv):
    d_uq = jnp.concatenate([dwuq_pad[:, h * 128:h * 128 + 96] for h in range(MLA_HEADS)], axis=1)
    d_ukv = jnp.concatenate([t for h in range(MLA_HEADS) for t in (dwk_pad[:, h * 128:h * 128 + 64], dwv[:, h * 64:(h + 1) * 64])],
                            axis=1)
    return [_col_blocks(d_uq), _col_blocks(d_ukv)]


def _rope_tables(positions, token=None):
    pos = positions.reshape(SEQ).astype(F32)
    if token is not None:
        pos = pos + token[0, 0]
    lane = jnp.arange(128)

    def table(rot, first, period):
        inv = ROPE_THETA ** (-jnp.arange(0, rot, 2, dtype=F32) / rot)
        half = rot // 2
        off = lane % period - first
        in1, in2 = (off >= 0) & (off < half), (off >= half) & (off < rot)
        inv_lane = jnp.where(in1 | in2, inv[jnp.clip(off % half, 0, half - 1)], 0.0)
        sign = jnp.where(in1, -1.0, 1.0).astype(F32)
        ang = pos[:, None] * inv_lane[None, :]
        return jnp.cos(ang), jnp.sin(ang) * sign[None, :]

    return table(32, 64, 128), table(16, 0, 64)


class _Links:
    def __init__(self, mats, chip, me):
        landing = [_set_slot(lax.empty((4,) + m.shape, m.dtype), m, chip) for m in mats]
        self.gather, self.token = _gather_start(mats, landing)
        self.me, self.sent, self.handles, self.sums = me, {}, {}, {}

    def weights(self, after):
        return _relay_share(_gather_wait(self.gather, after))

    def send(self, blocks, name):
        self.sent[name] = blocks
        self.handles[name], token = _exchange_start(blocks, name)
        return token

    def collect(self, name, after, parts):
        recv = _exchange_wait(self.handles[name], after)
        for r, own, part in zip(recv, self.sent[name], parts):
            self.sums[part] = _sum_parts(r, own, self.me, 64, "sum_grad_" + part)
        return tuple(self.sums[part] for part in parts)


def _device_grads(x, positions, target, gains, links):
    pre_g, q_g, kv_g, post_g = gains
    (mc, ms), (dc, ds) = _rope_tables(positions, links.token)
    h = _prenorm_fwd(x, pre_g, links.token)
    w_pad, w_pad_t, wuq_pad, wk_pad, wv, w_pm, w_pd, w_out = _full_weights(links.weights((h, mc, ms, dc, ds)))

    p = _matmul(h, w_pad, "nn", F32, 1024, 1408, 1024, "in_proj")
    cqn, ckvn, q, k, v = _mla_prep_fwd(p, q_g, kv_g, wuq_pad, wk_pad, wv, mc, ms)
    ya, lse_m = _mla_flash_fwd(q, k, v)
    qkv = [_dil_prep_fwd(p, dc, ds, g) for g in range(3)]
    o_g, l_g = zip(*[_dil_attn_fwd2(qkv[g], g) for g in range(3)])
    (dp, dy, mg, dt, ua, dpa, ud, dpd, dya, dyd, yd, lse_d, loss_cols, dg_post) = _tail(
        p, ya, o_g, l_g, x, target, w_pm, w_pd, w_out, post_g)

    for g in range(3):
        dqkv = _dil_attn_bwd2(qkv[g], dyd, yd, lse_d, g)
        dp = _dil_prep_bwd(dp, dqkv, dc, ds, g)
    dw_early = _matmul(dp, h, "tn", BF16, 1536, 1024, 2048, "dw_in_early", a_cols=(0, COL_LAT // 1536))
    dwpm = _matmul(ua, dpa, "tn", BF16, 512, 1024, 512, "dw_proj_mla")
    dwpd = _matmul(ud, dpd, "tn", BF16, 512, 1024, 512, "dw_proj_dil")
    dwout = _matmul(mg, dt, "tn", BF16, 1024, 1024, 512, "dw_out")
    token = links.send([_grad_parts_in_early(dw_early), _col_blocks(dwpm), _col_blocks(dwpd),
                        dwout.reshape(8, 128, D_MODEL)], "exchange_early")

    dq, dk, dv = _mla_flash_bwd(q, k, v, ya, dya, lse_m, token)
    dp, dqb, dkb, dvb, dg_q, dg_kv = _mla_prep_bwd(dp, p, dq, dk, dv, q_g, kv_g, wuq_pad, wk_pad, wv, mc, ms)
    dwuq_pad = _matmul(cqn, dqb, "tn", BF16, Q_RANK, 1024, 512, "dw_uq")
    dwk_pad = _matmul(ckvn, dkb, "tn", BF16, KV_RANK, 1024, 512, "dw_k")
    dwv = _matmul(ckvn, dvb, "tn", BF16, KV_RANK, 512, 512, "dw_v")
    dw_late = _matmul(dp, h, "tn", BF16, N_LAT, 1024, 2048, "dw_in_late", a_cols=(COL_LAT // N_LAT, 1))
    token = links.send([_grad_parts_in_late(dw_late)] + _grad_parts_mla(dwuq_pad, dwk_pad, dwv), "exchange_late")
    early = links.collect("exchange_early", dw_late, ("in_early", "pm", "pd", "out"))

    dh = _matmul(dp, w_pad_t, "nn", F32, 1024, 1024, 1408, "dh", (token,) + tuple(early))
    grad_x, dg_pre = _prenorm_bwd(x, dh, dy, pre_g)
    links.collect("exchange_late", grad_x, ("in_late", "uq", "ukv"))

    loss_part = jnp.pad((jnp.sum(loss_cols) * (0.5 / D_MODEL)).reshape(1, 1), ((0, 0), (0, N_GVEC - N_GAINS - 1)))
    gvec = jnp.concatenate([dg_pre, dg_q, dg_kv, dg_post, loss_part], axis=1)
    return grad_x, gvec


def kernel(x, positions, pre_norm_g, w_in, q_norm_g, w_uq, kv_norm_g, w_ukv, w_proj_mla, w_proj_dil, w_out, post_norm_g, loss_target, m_pre_norm_g, m_w_in, m_q_norm_g, m_w_uq, m_kv_norm_g, m_w_ukv, m_w_proj_mla, m_w_proj_dil, m_w_out, m_post_norm_g, v_pre_norm_g, v_w_in, v_q_norm_g, v_w_uq, v_kv_norm_g, v_w_ukv, v_w_proj_mla, v_w_proj_dil, v_w_out, v_post_norm_g):
    xi, yi, ci = _my_place()
    chip, me = 2 * xi + yi, 4 * xi + 2 * yi + ci
    mats = [jnp.swapaxes(w_in, 1, 2)] + [w_uq, w_ukv, w_proj_mla, w_proj_dil, w_out]
    mats = [w.reshape(w.shape[1:]).astype(BF16) for w in mats]
    links = _Links(mats, chip, me)
    gains = (pre_norm_g, q_norm_g, kv_norm_g, post_norm_g)
    grad_x, gvec = _device_grads(x[0], positions, loss_target[0], gains, links)

    sums = links.sums
    in_e = sums["in_early"]
    half_in = jnp.concatenate([in_e[:W_IN_LAT] + jnp.where(chip == 0, sums["in_late"], 0.0), in_e[W_IN_LAT:]], axis=0)
    halves = [half_in, sums["uq"], sums["ukv"], sums["pm"], sums["pd"], sums["out"]]
    gvec8 = jnp.pad(gvec, ((0, 7), (0, 0)))
    swapped, recv_gains = _swap_halves(halves, gvec8)
    g_gains = _sum_parts(recv_gains, gvec8, me, 8, "sum_gain_parts")[0:1]
    loss = g_gains[0, N_GAINS]
    sw = lambda a: jnp.swapaxes(a, 1, 2)
    d_in, m_in, v_in, g_in = [sw(o) for o in _adamw_in(sw(w_in), sw(m_w_in), sw(v_w_in), half_in, swapped[0], ci)]
    g_mats = [g_in] + [_set_slot(s, hf, ci).reshape((1,) + shp)
                       for s, hf, shp in zip(swapped[1:], halves[1:], SHARD_SHAPES[1:])]

    off = [0, 1024, 1408, 1664, 2688]
    g_gain = [g_gains[:, off[i]:off[i + 1]] for i in range(4)]
    grads = [g_gain[0], g_mats[0], g_gain[1], g_mats[1], g_gain[2], g_mats[2], g_mats[3], g_mats[4], g_mats[5], g_gain[3]]
    ws = [pre_norm_g, w_in, q_norm_g, w_uq, kv_norm_g, w_ukv, w_proj_mla, w_proj_dil, w_out, post_norm_g]
    ms = [m_pre_norm_g, m_w_in, m_q_norm_g, m_w_uq, m_kv_norm_g, m_w_ukv, m_w_proj_mla, m_w_proj_dil, m_w_out, m_post_norm_g]
    vs = [v_pre_norm_g, v_w_in, v_q_norm_g, v_w_uq, v_kv_norm_g, v_w_ukv, v_w_proj_mla, v_w_proj_dil, v_w_out, v_post_norm_g]
    deltas, new_m, new_v = [], [], []
    for i, (w, g, m, v) in enumerate(zip(ws, grads, ms, vs)):
        if w is w_in:
            d_, m_, v_ = d_in, m_in, v_in
        elif w.shape[-1] % 128 and w.shape[-2] % 128 == 0:
            g = jnp.swapaxes(g, 1, 2)
            grads[i] = jnp.swapaxes(g, 1, 2)
            d_, m_, v_ = [jnp.swapaxes(o, 1, 2) for o in
                          _adamw(jnp.swapaxes(w, 1, 2), g, jnp.swapaxes(m, 1, 2), jnp.swapaxes(v, 1, 2), f"adamw_{i}")]
        else:
            d_, m_, v_ = _adamw(w, g, m, v, f"adamw_{i}")
        deltas.append(d_)
        new_m.append(m_)
        new_v.append(v_)
    return (loss, grad_x.reshape(x.shape), *grads, *deltas, *new_m, *new_v)
```

```python
import jax
import jax.numpy as jnp
from jax import lax
from jax.experimental import pallas as pl
from jax.experimental.pallas import tpu as pltpu

F32 = jnp.float32
BF16 = jnp.bfloat16

SEQ = 4096
D_MODEL = 1024
EPS = 1e-6
ROPE_THETA = 500000.0
MLA_HEADS = 8
Q_RANK = 384
KV_RANK = 256
MLA_SCALE = 96.0 ** -0.5
MLA_ROPE_HALF = 16
DIL_DILATIONS = (1, 4, 16)
DIL_ROPE_HALF = 8
DIL_SCALE = 0.125
BAND = 128

N_LAT = 768
COL_Z, COL_QKV, COL_LAT = 2048, 3072, 7680
N_PAD = 8448
IN_SPLITS = (384, 256, 32, 4608, 512, 512, 1024, 1024)

SHARD_SHAPES = ((2088, 1024), (384, 192), (256, 256), (512, 256), (512, 256), (256, 1024))
N_MATS = len(SHARD_SHAPES)
N_GAINS = 2688
N_GVEC = N_GAINS + 128

ADAM_LR, ADAM_B1, ADAM_B2, ADAM_EPS, ADAM_WD, ADAM_STEP = 0.001, 0.9, 0.999, 1e-08, 0.01, 10

VMEM_LIMIT = 56 * 1024 * 1024
NEG = -1e30
MESH = pl.DeviceIdType.MESH


def _cparams(**kw):
    return pltpu.CompilerParams(vmem_limit_bytes=VMEM_LIMIT, **kw)


def _dot(a, b, dims):
    return lax.dot_general(a, b, (dims, ((), ())), preferred_element_type=F32)


def _nn(a, b):
    return _dot(a, b, ((1,), (0,)))


def _nt(a, b):
    return _dot(a, b, ((1,), (1,)))


def _tn(a, b):
    return _dot(a, b, ((0,), (0,)))


def _rope_lanes(shape, half, period, first):
    lane = lax.broadcasted_iota(jnp.int32, shape, len(shape) - 1) % period
    return (lane >= first) & (lane < first + half), (lane >= first + half) & (lane < first + 2 * half)


def _rope_fwd(x, c, s, half, lanes):
    x1, _ = lanes
    return x * c + jnp.where(x1, pltpu.roll(x, 128 - half, 1), pltpu.roll(x, half, 1)) * s


def _rope_bwd(g, c, s, half, lanes):
    x1, x2 = lanes
    gs = g * s
    return g * c + jnp.where(x2, pltpu.roll(gs, half, 1), jnp.where(x1, pltpu.roll(gs, 128 - half, 1), 0.0))


def _sigmoid(x):
    return 1.0 / (1.0 + jnp.exp(-x))


def _after(token):
    tokens = [t for t in (token if isinstance(token, (tuple, list)) else [token]) if t is not None]
    return tokens, [pl.BlockSpec(memory_space=pl.ANY)] * len(tokens)


def _matmul(a, b, mode, out_dtype, tm, tn, tk, name, token=None, b_cols=None, a_cols=None):
    after, after_specs = _after(token)
    if mode == "nn":
        (m, k), n = a.shape, b.shape[1]
        first = 0
        if b_cols is not None:
            first, n = b_cols[0], b_cols[1] * tn
        a_spec = pl.BlockSpec((tm, tk), lambda j, i, kk: (i, kk))
        b_spec = pl.BlockSpec((tk, tn), lambda j, i, kk: (kk, j + first))
        dot = _nn
    elif mode == "nt":
        (m, k), n = a.shape, b.shape[0]
        a_spec = pl.BlockSpec((tm, tk), lambda j, i, kk: (i, kk))
        b_spec = pl.BlockSpec((tn, tk), lambda j, i, kk: (j, kk))
        dot = _nt
    else:
        (k, m), n = a.shape, b.shape[1]
        first = 0
        if a_cols is not None:
            first, m = a_cols[0], a_cols[1] * tm
        a_spec = pl.BlockSpec((tk, tm), lambda j, i, kk: (kk, i + first))
        b_spec = pl.BlockSpec((tk, tn), lambda j, i, kk: (kk, j))
        dot = _tn
    assert m % tm == 0 and n % tn == 0 and k % tk == 0, (name, m, n, k, tm, tn, tk)
    nk = k // tk

    def body(a_ref, b_ref, *rest):
        o_ref, acc_ref = rest[-2:]
        kk = pl.program_id(2)
        part = dot(a_ref[...], b_ref[...])

        @pl.when(kk == 0)
        def _():
            acc_ref[...] = part

        @pl.when(kk > 0)
        def _():
            acc_ref[...] += part

        @pl.when(kk == nk - 1)
        def _():
            o_ref[...] = acc_ref[...].astype(o_ref.dtype)

    return pl.pallas_call(
        body, name=name, grid=(n // tn, m // tm, nk),
        in_specs=[a_spec, b_spec] + after_specs,
        out_specs=pl.BlockSpec((tm, tn), lambda j, i, kk: (i, j)),
        out_shape=jax.ShapeDtypeStruct((m, n), out_dtype),
        scratch_shapes=[pltpu.VMEM((tm, tn), F32)],
        compiler_params=_cparams(),
    )(a, b, *after)


def _prenorm_fwd(x, g, token=None):
    tm = 512
    after, after_specs = _after(token)

    def body(x_ref, g_ref, *rest):
        xv = x_ref[...]
        r = lax.rsqrt(jnp.mean(xv * xv, axis=-1, keepdims=True) + EPS)
        rest[-1][...] = (xv * r * g_ref[...]).astype(BF16)

    return pl.pallas_call(
        body, name="prenorm_fwd", grid=(SEQ // tm,),
        in_specs=[pl.BlockSpec((tm, D_MODEL), lambda i: (i, 0)), pl.BlockSpec((1, D_MODEL), lambda i: (0, 0))] + after_specs,
        out_specs=pl.BlockSpec((tm, D_MODEL), lambda i: (i, 0)),
        out_shape=jax.ShapeDtypeStruct((SEQ, D_MODEL), BF16),
    )(x, g, *after)


def _prenorm_bwd(x, dh, dy, g):
    tm = 512

    def body(x_ref, dh_ref, dy_ref, g_ref, gx_ref, dg_ref):
        xv = x_ref[...]
        r = lax.rsqrt(jnp.mean(xv * xv, axis=-1, keepdims=True) + EPS)
        n = xv * r
        dhv = dh_ref[...]
        dn = dhv * g_ref[...]
        gx_ref[...] = dy_ref[...] + r * (dn - n * jnp.mean(dn * n, axis=-1, keepdims=True))
        part = jnp.sum(dhv * n, axis=0, keepdims=True)

        @pl.when(pl.program_id(0) == 0)
        def _():
            dg_ref[...] = part

        @pl.when(pl.program_id(0) > 0)
        def _():
            dg_ref[...] += part

    row = pl.BlockSpec((tm, D_MODEL), lambda i: (i, 0))
    vec = pl.BlockSpec((1, D_MODEL), lambda i: (0, 0))
    return pl.pallas_call(
        body, name="prenorm_bwd", grid=(SEQ // tm,),
        in_specs=[row, row, row, vec], out_specs=[row, vec],
        out_shape=[jax.ShapeDtypeStruct((SEQ, D_MODEL), F32), jax.ShapeDtypeStruct((1, D_MODEL), F32)],
        compiler_params=_cparams(),
    )(x, dh, dy, g)


def _mla_prep_fwd(p, qg, kvg, wuq, wk, wv, rc, rs):
    tm = 512

    def body(lat_ref, qg_ref, kvg_ref, wuq_ref, wk_ref, wv_ref, c_ref, s_ref,
             cqn_ref, ckvn_ref, q_ref, k_ref, v_ref):
        c, s = c_ref[...], s_ref[...]
        lanes = _rope_lanes((tm, 128), MLA_ROPE_HALF, 128, 64)
        cq = lat_ref[:, 0:Q_RANK]
        r1 = lax.rsqrt(jnp.mean(cq * cq, axis=-1, keepdims=True) + EPS)
        cqn = (cq * r1 * qg_ref[...]).astype(BF16)
        cqn_ref[...] = cqn
        q = _nn(cqn, wuq_ref[...])
        for h in range(MLA_HEADS):
            sl = slice(h * 128, (h + 1) * 128)
            q_ref[:, sl] = (_rope_fwd(q[:, sl], c, s, MLA_ROPE_HALF, lanes) * MLA_SCALE).astype(BF16)
        ckv = lat_ref[:, Q_RANK:Q_RANK + KV_RANK]
        r2 = lax.rsqrt(jnp.mean(ckv * ckv, axis=-1, keepdims=True) + EPS)
        ckvn = (ckv * r2 * kvg_ref[...]).astype(BF16)
        ckvn_ref[...] = ckvn
        krr = _rope_fwd(lat_ref[:, Q_RANK + KV_RANK:N_LAT], c, s, MLA_ROPE_HALF, lanes)
        kn = _nn(ckvn, wk_ref[...])
        for h in range(MLA_HEADS):
            sl = slice(h * 128, (h + 1) * 128)
            k_ref[:, sl] = (kn[:, sl] + krr).astype(BF16)
        v_ref[...] = _nn(ckvn, wv_ref[...]).astype(BF16)

    def full(shape):
        return pl.BlockSpec(shape, lambda i: (0, 0))

    def rows(w):
        return pl.BlockSpec((tm, w), lambda i: (i, 0))

    return pl.pallas_call(
        body, name="mla_prep_fwd", grid=(SEQ // tm,),
        in_specs=[pl.BlockSpec((tm, N_LAT), lambda i: (i, COL_LAT // N_LAT)),
                  full((1, Q_RANK)), full((1, KV_RANK)), full((Q_RANK, 1024)), full((KV_RANK, 1024)),
                  full((KV_RANK, 512)), rows(128), rows(128)],
        out_specs=[rows(Q_RANK), rows(KV_RANK), rows(1024), rows(1024), rows(512)],
        out_shape=[jax.ShapeDtypeStruct((SEQ, Q_RANK), BF16), jax.ShapeDtypeStruct((SEQ, KV_RANK), BF16),
                   jax.ShapeDtypeStruct((SEQ, 1024), BF16), jax.ShapeDtypeStruct((SEQ, 1024), BF16),
                   jax.ShapeDtypeStruct((SEQ, 512), BF16)],
        compiler_params=_cparams(),
    )(p, qg, kvg, wuq, wk, wv, rc, rs)


def _mla_prep_bwd(dp_in, p, dq, dk, dv, qg, kvg, wuq, wk, wv, rc, rs):
    tm = 512

    def body(dp_any, lat_ref, dq_ref, dk_ref, dv_ref, qg_ref, kvg_ref, wuq_ref, wk_ref, wv_ref,
             c_ref, s_ref, dp_ref, dqb_ref, dkb_ref, dvb_ref, dgq_ref, dgkv_ref):
        del dp_any
        c, s = c_ref[...], s_ref[...]
        lanes = _rope_lanes((tm, 128), MLA_ROPE_HALF, 128, 64)
        lane = lax.broadcasted_iota(jnp.int32, (tm, 128), 1)
        dkr = jnp.zeros((tm, 128), F32)
        for h in range(MLA_HEADS):
            sl = slice(h * 128, (h + 1) * 128)
            dqb_ref[:, sl] = _rope_bwd(dq_ref[:, sl] * MLA_SCALE, c, s, MLA_ROPE_HALF, lanes).astype(BF16)
            dkh = dk_ref[:, sl]
            dkr = dkr + dkh
            dkb_ref[:, sl] = jnp.where(lane < 64, dkh, 0.0).astype(BF16)
        dkr = jnp.where((lane >= 64) & (lane < 96), dkr, 0.0)
        dkr = _rope_bwd(dkr, c, s, MLA_ROPE_HALF, lanes)
        dvb = dv_ref[...].astype(BF16)
        dvb_ref[...] = dvb

        cq = lat_ref[:, 0:Q_RANK]
        r1 = lax.rsqrt(jnp.mean(cq * cq, axis=-1, keepdims=True) + EPS)
        n1 = cq * r1
        dcqn = _nt(dqb_ref[...], wuq_ref[...])
        dn1 = dcqn * qg_ref[...]
        dcq = r1 * (dn1 - n1 * jnp.mean(dn1 * n1, axis=-1, keepdims=True))
        pq = jnp.sum(dcqn * n1, axis=0, keepdims=True)

        ckv = lat_ref[:, Q_RANK:Q_RANK + KV_RANK]
        r2 = lax.rsqrt(jnp.mean(ckv * ckv, axis=-1, keepdims=True) + EPS)
        n2 = ckv * r2
        dckvn = _nt(dkb_ref[...], wk_ref[...]) + _nt(dvb, wv_ref[...])
        dn2 = dckvn * kvg_ref[...]
        dckv = r2 * (dn2 - n2 * jnp.mean(dn2 * n2, axis=-1, keepdims=True))
        pkv = jnp.sum(dckvn * n2, axis=0, keepdims=True)

        dp_ref[:, 0:Q_RANK] = dcq.astype(BF16)
        dp_ref[:, Q_RANK:Q_RANK + KV_RANK] = dckv.astype(BF16)
        dp_ref[:, Q_RANK + KV_RANK:N_LAT] = dkr.astype(BF16)

        @pl.when(pl.program_id(0) == 0)
        def _():
            dgq_ref[...] = pq
            dgkv_ref[...] = pkv

        @pl.when(pl.program_id(0) > 0)
        def _():
            dgq_ref[...] += pq
            dgkv_ref[...] += pkv

    def full(shape):
        return pl.BlockSpec(shape, lambda i: (0, 0))

    def rows(w):
        return pl.BlockSpec((tm, w), lambda i: (i, 0))

    lat = pl.BlockSpec((tm, N_LAT), lambda i: (i, COL_LAT // N_LAT))
    return pl.pallas_call(
        body, name="mla_prep_bwd", grid=(SEQ // tm,),
        in_specs=[pl.BlockSpec(memory_space=pl.ANY), lat, rows(1024), rows(1024), rows(512),
                  full((1, Q_RANK)), full((1, KV_RANK)), full((Q_RANK, 1024)), full((KV_RANK, 1024)),
                  full((KV_RANK, 512)), rows(128), rows(128)],
        out_specs=[lat, rows(1024), rows(1024), rows(512), full((1, Q_RANK)), full((1, KV_RANK))],
        out_shape=[jax.ShapeDtypeStruct((SEQ, N_PAD), BF16), jax.ShapeDtypeStruct((SEQ, 1024), BF16),
                   jax.ShapeDtypeStruct((SEQ, 1024), BF16), jax.ShapeDtypeStruct((SEQ, 512), BF16),
                   jax.ShapeDtypeStruct((1, Q_RANK), F32), jax.ShapeDtypeStruct((1, KV_RANK), F32)],
        input_output_aliases={0: 0},
        compiler_params=_cparams(),
    )(dp_in, p, dq, dk, dv, qg, kvg, wuq, wk, wv, rc, rs)


FLASH_T = 1024


def _head_half(shape, hh):
    lane = lax.broadcasted_iota(jnp.int32, shape, 1)
    return (lane < 64) if hh == 0 else (lane >= 64)


def _diag_keep(nr, nk):
    row = lax.broadcasted_iota(jnp.int32, (nr, nk), 0)
    col = lax.broadcasted_iota(jnp.int32, (nr, nk), 1)
    return row + (nk - nr) >= col


def _tri_steps(nb, q_major):
    if q_major:
        pairs = [(i, kb) for i in range(nb) for kb in range(i + 1)]
    else:
        pairs = [(i, kb) for kb in range(nb) for i in range(kb, nb)]
    return jnp.asarray([p[0] for p in pairs], jnp.int32), jnp.asarray([p[1] for p in pairs], jnp.int32)


def _mla_flash_fwd(q, k, v):
    t = FLASH_T
    nb = SEQ // t
    qtab, ktab = _tri_steps(nb, True)

    def body(qi_ref, ki_ref, q_ref, k_ref, v_ref, o_ref, lse_ref, m_scr, l_scr, acc_scr):
        step = pl.program_id(1)
        i, kb = qi_ref[step], ki_ref[step]

        @pl.when(kb == 0)
        def _():
            m_scr[...] = jnp.full_like(m_scr, NEG)
            l_scr[...] = jnp.zeros_like(l_scr)
            acc_scr[...] = jnp.zeros_like(acc_scr)

        def update(r0, nr, nk, diagonal):
            rs = slice(r0, r0 + nr)
            vv = v_ref[0:nk, :]
            for hh in range(2):
                sl = slice(hh * 128, (hh + 1) * 128)
                s = _nt(q_ref[rs, sl], k_ref[0:nk, sl])
                if diagonal:
                    s = jnp.where(_diag_keep(nr, nk), s, NEG)
                m_prev = m_scr[hh, rs, :]
                m_new = jnp.maximum(m_prev, jnp.max(s, axis=-1, keepdims=True))
                pr = jnp.exp(s - jnp.tile(m_new, (1, nk // 128)))
                alpha = jnp.exp(m_prev - m_new)
                l_scr[hh, rs, :] = alpha * l_scr[hh, rs, :] + jnp.sum(pr, axis=-1, keepdims=True)
                acc_scr[hh, rs, :] = alpha * acc_scr[hh, rs, :] + _nn(pr.astype(BF16), vv)
                m_scr[hh, rs, :] = m_new

        @pl.when(kb < i)
        def _():
            update(0, t, t, False)

        @pl.when(kb == i)
        def _():
            update(0, t // 2, t // 2, True)
            update(t // 2, t // 2, t, True)
            o0 = acc_scr[0] / l_scr[0]
            o1 = acc_scr[1] / l_scr[1]
            o_ref[...] = jnp.where(_head_half((t, 128), 0), o0, o1)
            for hh in range(2):
                lse_ref[:, hh * 128:(hh + 1) * 128] = m_scr[hh] + jnp.log(l_scr[hh])

    grid_spec = pltpu.PrefetchScalarGridSpec(
        num_scalar_prefetch=2, grid=(4, qtab.shape[0]),
        in_specs=[pl.BlockSpec((t, 256), lambda j, s, qi, ki: (qi[s], j)),
                  pl.BlockSpec((t, 256), lambda j, s, qi, ki: (ki[s], j)),
                  pl.BlockSpec((t, 128), lambda j, s, qi, ki: (ki[s], j))],
        out_specs=[pl.BlockSpec((t, 128), lambda j, s, qi, ki: (qi[s], j)),
                   pl.BlockSpec((t, 256), lambda j, s, qi, ki: (qi[s], j))],
        scratch_shapes=[pltpu.VMEM((2, t, 128), F32), pltpu.VMEM((2, t, 128), F32), pltpu.VMEM((2, t, 128), F32)])
    return pl.pallas_call(
        body, name="mla_flash_fwd", grid_spec=grid_spec,
        out_shape=[jax.ShapeDtypeStruct((SEQ, 512), F32), jax.ShapeDtypeStruct((SEQ, 1024), F32)],
        compiler_params=_cparams(),
    )(qtab, ktab, q, k, v)


def _mla_flash_bwd(q, k, v, o, do, lse, token=None):
    t = FLASH_T
    nb = SEQ // t
    qtab, ktab = _tri_steps(nb, False)
    after, after_specs = _after(token)

    def body(qi_ref, ki_ref, q_ref, k_ref, v_ref, o_ref, do_ref, lse_ref, *rest):
        dq_ref, dk_ref, dv_ref, dk_scr, dv_scr = rest[-5:]
        step = pl.program_id(1)
        i, kb = qi_ref[step], ki_ref[step]

        @pl.when(step == 0)
        def _():
            dq_ref[...] = jnp.zeros_like(dq_ref)

        @pl.when(i == kb)
        def _():
            dk_scr[...] = jnp.zeros_like(dk_scr)
            dv_scr[...] = jnp.zeros_like(dv_scr)

        def update(r0, nr, nk, diagonal):
            rs = slice(r0, r0 + nr)
            vv = v_ref[0:nk, :]
            ov = o_ref[rs, :]
            dov = do_ref[rs, :]
            rows = pl.ds(pl.multiple_of(i * t + r0, t // 2), nr)
            for hh in range(2):
                sl = slice(hh * 128, (hh + 1) * 128)
                qh, kh = q_ref[rs, sl], k_ref[0:nk, sl]
                s = _nt(qh, kh)
                if diagonal:
                    s = jnp.where(_diag_keep(nr, nk), s, NEG)
                pr = jnp.exp(s - jnp.tile(lse_ref[rs, sl], (1, nk // 128)))
                dom = jnp.where(_head_half((nr, 128), hh), dov, 0.0)
                domb = dom.astype(BF16)
                dv_scr[0:nk, :] += _tn(pr.astype(BF16), domb)
                dpr = _nt(domb, vv)
                delta = jnp.sum(dom * ov, axis=-1, keepdims=True)
                ds = (pr * (dpr - delta)).astype(BF16)
                dq_ref[rows, sl] += _nn(ds, kh)
                dk_scr[hh, 0:nk, :] += _tn(ds, qh)

        @pl.when(i > kb)
        def _():
            update(0, t, t, False)

        @pl.when(i == kb)
        def _():
            update(0, t // 2, t // 2, True)
            update(t // 2, t // 2, t, True)

        @pl.when(i == nb - 1)
        def _():
            dk_ref[:, 0:128] = dk_scr[0]
            dk_ref[:, 128:256] = dk_scr[1]
            dv_ref[...] = dv_scr[...]

    qi_map = lambda j, s, qi, ki: (qi[s], j)
    ki_map = lambda j, s, qi, ki: (ki[s], j)
    grid_spec = pltpu.PrefetchScalarGridSpec(
        num_scalar_prefetch=2, grid=(4, qtab.shape[0]),
        in_specs=[pl.BlockSpec((t, 256), qi_map), pl.BlockSpec((t, 256), ki_map), pl.BlockSpec((t, 128), ki_map),
                  pl.BlockSpec((t, 128), qi_map), pl.BlockSpec((t, 128), qi_map), pl.BlockSpec((t, 256), qi_map)]
        + after_specs,
        out_specs=[pl.BlockSpec((SEQ, 256), lambda j, s, qi, ki: (0, j)), pl.BlockSpec((t, 256), ki_map),
                   pl.BlockSpec((t, 128), ki_map)],
        scratch_shapes=[pltpu.VMEM((2, t, 128), F32), pltpu.VMEM((t, 128), F32)])
    return pl.pallas_call(
        body, name="mla_flash_bwd", grid_spec=grid_spec,
        out_shape=[jax.ShapeDtypeStruct((SEQ, 1024), F32), jax.ShapeDtypeStruct((SEQ, 1024), F32),
                   jax.ShapeDtypeStruct((SEQ, 512), F32)],
        compiler_params=_cparams(),
    )(qtab, ktab, q, k, v, o, do, lse, *after)


DIL_UNROLL = 4


def _strided(start, size, d):
    return pl.ds(start, size) if d == 1 else pl.ds(start, size, stride=d)


def _dil_prep_fwd(p, rc, rs, g):
    d = DIL_DILATIONS[g]
    sub_len = SEQ // d
    ch = min(sub_len, 512)

    def body(p_ref, c_ref, s_ref, o_ref, x_scr):
        tq = pl.program_id(0)
        lanes = _rope_lanes((ch, 128), DIL_ROPE_HALF, 64, 0)
        o_ref[0, 0:BAND, :] = jnp.zeros((BAND, 128), BF16)

        @pl.when(tq < 2)
        def _():
            mult = jnp.where(tq == 0, DIL_SCALE, 1.0).astype(F32)
            for c0 in range(0, SEQ, ch):
                rows = pl.ds(c0, ch)
                x_scr[rows, :] = _rope_fwd(p_ref[rows, :], c_ref[rows, :] * mult, s_ref[rows, :] * mult, DIL_ROPE_HALF, lanes)

        def gather(src):
            for r in range(d):
                for c0 in range(0, sub_len, ch):
                    at = BAND + r * sub_len + c0
                    o_ref[0, at:at + ch, :] = src[_strided(r + c0 * d, ch, d), :].astype(BF16)

        @pl.when(tq < 2)
        def _():
            gather(x_scr)

        @pl.when(tq == 2)
        def _():
            gather(p_ref)

    tab = pl.BlockSpec((SEQ, 128), lambda tq, pr: (0, 0))
    return pl.pallas_call(
        body, name=f"dil_prep_fwd_g{g}", grid=(3, 4),
        in_specs=[pl.BlockSpec((SEQ, 128), lambda tq, pr: (0, COL_QKV // 128 + (tq * 3 + g) * 4 + pr)), tab, tab],
        out_specs=pl.BlockSpec((1, BAND + SEQ, 128), lambda tq, pr: (tq, 0, pr)),
        out_shape=jax.ShapeDtypeStruct((3, BAND + SEQ, 512), BF16),
        scratch_shapes=[pltpu.VMEM((SEQ, 128), F32)],
        compiler_params=_cparams(),
    )(p, rc, rs)


DIL_ST_FWD, DIL_ST_BWD = 1024, 2048


def _band_keep(g, b, t, nb):
    nbs = SEQ // DIL_DILATIONS[g] // BAND
    row = lax.broadcasted_iota(jnp.int32, (BAND, 2 * BAND), 0)
    col = lax.broadcasted_iota(jnp.int32, (BAND, 2 * BAND), 1)
    cur = (col >= BAND) & (row >= col - BAND)
    prev = (col < BAND) & (col >= row)
    if nbs >= nb:
        if b > 0:
            return cur | prev
        return cur | (prev & ((t * nb) % nbs != 0))
    return cur | prev if b % nbs else cur


def _dil_tok(g, b, t, nb):
    d = DIL_DILATIONS[g]
    nbs = SEQ // d // BAND
    gb = t * nb + b
    return _strided((gb % nbs) * BAND * d + gb // nbs, BAND, d)


def _dil_attn_fwd2(qkv, g):
    DIL_ST, DIL_NB = DIL_ST_FWD, DIL_ST_FWD // BAND

    def body(q_ref, k_ref, v_ref, o_ref, l_ref, s_scr, p_scr, o_scr):
        t = pl.program_id(1)
        base = t * DIL_ST
        half0 = _head_half((DIL_ST, 128), 0)
        lse_h = []
        for hh in range(2):
            half = _head_half((BAND, 128), hh)
            for b in range(DIL_NB):
                qv = q_ref[0, pl.ds(pl.multiple_of(base + (b + 1) * BAND, BAND), BAND), :]
                k2 = k_ref[0, pl.ds(pl.multiple_of(base + b * BAND, BAND), 2 * BAND), :]
                sb = _nt(jnp.where(half, qv, jnp.zeros_like(qv)), k2)
                s_scr[b * BAND:(b + 1) * BAND, :] = jnp.where(_band_keep(g, b, t, DIL_NB), sb, NEG)
            s = s_scr[...]
            m = jnp.max(s, axis=-1, keepdims=True)
            pr = jnp.exp(s - m)
            den = jnp.sum(pr, axis=-1, keepdims=True)
            p_scr[...] = pr.astype(BF16)
            for b in range(DIL_NB):
                v2 = v_ref[0, pl.ds(pl.multiple_of(base + b * BAND, BAND), 2 * BAND), :]
                o_scr[hh, b * BAND:(b + 1) * BAND, :] = _nn(p_scr[b * BAND:(b + 1) * BAND, :], v2)
            o_scr[hh] = o_scr[hh] / den
            lse_h.append(m + jnp.log(den))
        out = jnp.where(half0, o_scr[0], o_scr[1])
        lse = jnp.where(half0, lse_h[0], lse_h[1])
        for b in range(DIL_NB):
            tok = _dil_tok(g, b, t, DIL_NB)
            o_ref[tok, :] = out[b * BAND:(b + 1) * BAND, :]
            l_ref[tok, :] = lse[b * BAND:(b + 1) * BAND, :]

    def inp(tq):
        return pl.BlockSpec((1, BAND + SEQ, 128), lambda pr, t: (tq, 0, pr))

    out = pl.BlockSpec((SEQ, 128), lambda pr, t: (0, pr))
    return pl.pallas_call(
        body, name=f"dil_attn_fwd_g{g}", grid=(4, SEQ // DIL_ST),
        in_specs=[inp(0), inp(1), inp(2)], out_specs=[out, out],
        out_shape=[jax.ShapeDtypeStruct((SEQ, 512), F32), jax.ShapeDtypeStruct((SEQ, 512), F32)],
        scratch_shapes=[pltpu.VMEM((DIL_ST, 2 * BAND), F32), pltpu.VMEM((DIL_ST, 2 * BAND), BF16),
                        pltpu.VMEM((2, DIL_ST, 128), F32)],
        compiler_params=_cparams(),
    )(qkv, qkv, qkv)


def _dil_attn_bwd2(qkv, dyd, yd, lse_all, g, token=None):
    d = DIL_DILATIONS[g]
    sub_len = SEQ // d
    DIL_ST, DIL_NB = DIL_ST_BWD, DIL_ST_BWD // BAND
    nst = SEQ // DIL_ST
    after, after_specs = _after(token)

    def body(q_ref, k_ref, v_ref, do_ref, y_ref, l_ref, *rest):
        out_ref, dk_scr, dv_scr, s_scr, dp_scr, p_scr, ds_scr, do_scr, y_scr, l_scr, dq_scr = rest[-11:]
        t = pl.program_id(1)
        base = t * DIL_ST

        @pl.when(t == 0)
        def _():
            dk_scr[...] = jnp.zeros_like(dk_scr)
            dv_scr[...] = jnp.zeros_like(dv_scr)

        for b in range(DIL_NB):
            tok = _dil_tok(g, b, t, DIL_NB)
            do_scr[b * BAND:(b + 1) * BAND, :] = do_ref[tok, :]
            y_scr[b * BAND:(b + 1) * BAND, :] = y_ref[tok, :]
            l_scr[b * BAND:(b + 1) * BAND, :] = l_ref[tok, :]
        for hh in range(2):
            half = _head_half((BAND, 128), hh)
            half_st = _head_half((DIL_ST, 128), hh)
            dom = jnp.where(half_st, do_scr[...], 0.0)
            delta = jnp.sum(dom * y_scr[...], axis=-1, keepdims=True)
            lcol = jnp.max(jnp.where(half_st, l_scr[...], NEG), axis=-1, keepdims=True)
            for b in range(DIL_NB):
                rows = slice(b * BAND, (b + 1) * BAND)
                qv = q_ref[0, pl.ds(pl.multiple_of(base + (b + 1) * BAND, BAND), BAND), :]
                band = pl.ds(pl.multiple_of(base + b * BAND, BAND), 2 * BAND)
                sb = _nt(jnp.where(half, qv, jnp.zeros_like(qv)), k_ref[0, band, :])
                s_scr[rows, :] = jnp.where(_band_keep(g, b, t, DIL_NB), sb, NEG)
                dp_scr[rows, :] = _nt(dom[rows, :].astype(BF16), v_ref[0, band, :])
            pr = jnp.exp(s_scr[...] - lcol)
            p_scr[...] = pr.astype(BF16)
            ds_scr[...] = (pr * (dp_scr[...] - delta)).astype(BF16)
            for b in range(DIL_NB):
                rows = slice(b * BAND, (b + 1) * BAND)
                qv = q_ref[0, pl.ds(pl.multiple_of(base + (b + 1) * BAND, BAND), BAND), :]
                band = pl.ds(pl.multiple_of(base + b * BAND, BAND), 2 * BAND)
                dqb = jnp.where(half, _nn(ds_scr[rows, :], k_ref[0, band, :]), 0.0)
                if hh == 0:
                    dq_scr[rows, :] = dqb
                else:
                    dq_scr[rows, :] += dqb
                half2 = _head_half((2 * BAND, 128), hh)
                dk_scr[band, :] += jnp.where(half2, _tn(ds_scr[rows, :], qv), 0.0)
                dv_scr[band, :] += _tn(p_scr[rows, :], dom[rows, :].astype(BF16))
        for b in range(DIL_NB):
            out_ref[pl.ds(0, 1), _dil_tok(g, b, t, DIL_NB), :] = dq_scr[b * BAND:(b + 1) * BAND, :][None]

        @pl.when(t == nst - 1)
        def _():
            for r in range(d):
                rows = _strided(r, sub_len, d)
                out_ref[pl.ds(1, 1), rows, :] = dk_scr[BAND + r * sub_len:BAND + (r + 1) * sub_len, :][None]
                out_ref[pl.ds(2, 1), rows, :] = dv_scr[BAND + r * sub_len:BAND + (r + 1) * sub_len, :][None]

    def inp(tq):
        return pl.BlockSpec((1, BAND + SEQ, 128), lambda pr, t: (tq, 0, pr))

    tok_spec = pl.BlockSpec((SEQ, 128), lambda pr, t: (0, pr))
    st = (DIL_ST, 2 * BAND)
    return pl.pallas_call(
        body, name=f"dil_attn_bwd_g{g}", grid=(4, nst),
        in_specs=[inp(0), inp(1), inp(2), tok_spec, tok_spec, tok_spec] + after_specs,
        out_specs=pl.BlockSpec((3, SEQ, 128), lambda pr, t: (0, 0, pr)),
        out_shape=jax.ShapeDtypeStruct((3, SEQ, 512), F32),
        scratch_shapes=[pltpu.VMEM((BAND + SEQ, 128), F32), pltpu.VMEM((BAND + SEQ, 128), F32),
                        pltpu.VMEM(st, F32), pltpu.VMEM(st, F32), pltpu.VMEM(st, BF16), pltpu.VMEM(st, BF16),
                        pltpu.VMEM((DIL_ST, 128), F32), pltpu.VMEM((DIL_ST, 128), F32), pltpu.VMEM((DIL_ST, 128), F32),
                        pltpu.VMEM((DIL_ST, 128), F32)],
        compiler_params=_cparams(),
    )(qkv, qkv, qkv, dyd, yd, lse_all, *after)


def _band_masks():
    row = lax.broadcasted_iota(jnp.int32, (BAND, BAND), 0)
    col = lax.broadcasted_iota(jnp.int32, (BAND, BAND), 1)
    return row >= col, col >= row


def _dil_attn_fwd(qkv, g):
    d = DIL_DILATIONS[g]
    nbs = SEQ // d // BAND
    nblk = SEQ // BAND

    def body(q_ref, k_ref, v_ref, o_ref, l_ref):
        keep_c, keep_p = _band_masks()
        half0 = _head_half((BAND, 128), 0)

        def step(i, carry):
            cur = pl.ds(pl.multiple_of(i * BAND, BAND), BAND)
            prv = pl.ds(pl.multiple_of(jnp.maximum(i - 1, 0) * BAND, BAND), BAND)
            has_prev = (i % nbs) != 0
            qv = q_ref[0, cur, :]
            kc, kp = k_ref[0, cur, :], k_ref[0, prv, :]
            vc, vp = v_ref[0, cur, :], v_ref[0, prv, :]
            outs, lses = [], []
            for hh in range(2):
                qm = jnp.where(_head_half((BAND, 128), hh), qv, jnp.zeros_like(qv))
                sc = jnp.where(keep_c, _nt(qm, kc), NEG)
                sp = jnp.where(keep_p & has_prev, _nt(qm, kp), NEG)
                m = jnp.maximum(jnp.max(sc, axis=-1, keepdims=True), jnp.max(sp, axis=-1, keepdims=True))
                pc, pp = jnp.exp(sc - m), jnp.exp(sp - m)
                den = jnp.sum(pc, axis=-1, keepdims=True) + jnp.sum(pp, axis=-1, keepdims=True)
                o = (_nn(pc.astype(BF16), vc) + _nn(pp.astype(BF16), vp)) / den
                outs.append(o)
                lses.append(jnp.broadcast_to(m + jnp.log(den), (BAND, 128)))
            tok = _strided((i % nbs) * BAND * d + i // nbs, BAND, d)
            o_ref[tok, :] = jnp.where(half0, outs[0], outs[1])
            l_ref[tok, :] = jnp.where(half0, lses[0], lses[1])
            return carry

        lax.fori_loop(0, nblk, step, 0, unroll=DIL_UNROLL)

    def inp(tq):
        return pl.BlockSpec((1, SEQ, 128), lambda pr: (tq, 0, pr))

    out = pl.BlockSpec((SEQ, 128), lambda pr: (0, pr))
    return pl.pallas_call(
        body, name=f"dil_attn_fwd_g{g}", grid=(4,),
        in_specs=[inp(0), inp(1), inp(2)], out_specs=[out, out],
        out_shape=[jax.ShapeDtypeStruct((SEQ, 512), F32), jax.ShapeDtypeStruct((SEQ, 512), F32)],
        compiler_params=_cparams(),
    )(qkv, qkv, qkv)


def _dil_attn_bwd(qkv, dyd, yd, lse_all, g):
    d = DIL_DILATIONS[g]
    sub_len = SEQ // d
    nbs = sub_len // BAND
    nblk = SEQ // BAND

    def body(q_ref, k_ref, v_ref, do_ref, y_ref, l_ref, out_ref, dk_scr, dv_scr):
        keep_c, keep_p = _band_masks()
        dk_scr[...] = jnp.zeros_like(dk_scr)
        dv_scr[...] = jnp.zeros_like(dv_scr)

        def step(i, carry):
            cur = pl.ds(pl.multiple_of(i * BAND, BAND), BAND)
            prv = pl.ds(pl.multiple_of(jnp.maximum(i - 1, 0) * BAND, BAND), BAND)
            has_prev = (i % nbs) != 0
            tok = _strided((i % nbs) * BAND * d + i // nbs, BAND, d)
            qv = q_ref[0, cur, :]
            kc, kp = k_ref[0, cur, :], k_ref[0, prv, :]
            vc, vp = v_ref[0, cur, :], v_ref[0, prv, :]
            dov, yv, lv = do_ref[tok, :], y_ref[tok, :], l_ref[tok, :]
            dq = jnp.zeros((BAND, 128), F32)
            dkc = jnp.zeros((BAND, 128), F32)
            dkp = jnp.zeros((BAND, 128), F32)
            dvc = jnp.zeros((BAND, 128), F32)
            dvp = jnp.zeros((BAND, 128), F32)
            for hh in range(2):
                half = _head_half((BAND, 128), hh)
                qm = jnp.where(half, qv, jnp.zeros_like(qv))
                lcol = jnp.max(jnp.where(half, lv, NEG), axis=-1, keepdims=True)
                pc = jnp.exp(jnp.where(keep_c, _nt(qm, kc), NEG) - lcol)
                pp = jnp.exp(jnp.where(keep_p & has_prev, _nt(qm, kp), NEG) - lcol)
                dom = jnp.where(half, dov, 0.0)
                domb = dom.astype(BF16)
                delta = jnp.sum(dom * yv, axis=-1, keepdims=True)
                dsc = (pc * (_nt(domb, vc) - delta)).astype(BF16)
                dsp = (pp * (_nt(domb, vp) - delta)).astype(BF16)
                dvc = dvc + _tn(pc.astype(BF16), domb)
                dvp = dvp + _tn(pp.astype(BF16), domb)
                dq = dq + jnp.where(half, _nn(dsc, kc) + _nn(dsp, kp), 0.0)
                dkc = dkc + jnp.where(half, _tn(dsc, qv), 0.0)
                dkp = dkp + jnp.where(half, _tn(dsp, qv), 0.0)
            out_ref[pl.ds(0, 1), tok, :] = dq[None]
            dk_scr[cur, :] += dkc
            dk_scr[prv, :] += dkp
            dv_scr[cur, :] += dvc
            dv_scr[prv, :] += dvp
            return carry

        lax.fori_loop(0, nblk, step, 0, unroll=DIL_UNROLL)
        for r in range(d):
            rows = _strided(r, sub_len, d)
            out_ref[pl.ds(1, 1), rows, :] = dk_scr[r * sub_len:(r + 1) * sub_len, :][None]
            out_ref[pl.ds(2, 1), rows, :] = dv_scr[r * sub_len:(r + 1) * sub_len, :][None]

    def inp(tq):
        return pl.BlockSpec((1, SEQ, 128), lambda pr: (tq, 0, pr))

    tok_spec = pl.BlockSpec((SEQ, 128), lambda pr: (0, pr))
    return pl.pallas_call(
        body, name=f"dil_attn_bwd_g{g}", grid=(4,),
        in_specs=[inp(0), inp(1), inp(2), tok_spec, tok_spec, tok_spec],
        out_specs=pl.BlockSpec((3, SEQ, 128), lambda pr: (0, 0, pr)),
        out_shape=jax.ShapeDtypeStruct((3, SEQ, 512), F32),
        scratch_shapes=[pltpu.VMEM((SEQ, 128), F32), pltpu.VMEM((SEQ, 128), F32)],
        compiler_params=_cparams(),
    )(qkv, qkv, qkv, dyd, yd, lse_all)


def _dil_prep_bwd(dp_in, dqkv, rc, rs, g):
    tm = 1024

    def body(dp_any, g_ref, c_ref, s_ref, dp_ref):
        del dp_any
        tq = pl.program_id(0)

        @pl.when(tq == 2)
        def _():
            dp_ref[...] = g_ref[0].astype(BF16)

        @pl.when(tq < 2)
        def _():
            mult = jnp.where(tq == 0, DIL_SCALE, 1.0).astype(F32)
            lanes = _rope_lanes((tm, 128), DIL_ROPE_HALF, 64, 0)
            cv, sv = c_ref[...], s_ref[...] * mult
            cv = cv * mult
            for pr in range(4):
                gv = g_ref[0, :, pr * 128:(pr + 1) * 128]
                dp_ref[:, pr * 128:(pr + 1) * 128] = _rope_bwd(gv, cv, sv, DIL_ROPE_HALF, lanes).astype(BF16)

    tab = pl.BlockSpec((tm, 128), lambda tq, i: (i, 0))
    return pl.pallas_call(
        body, name=f"dil_prep_bwd_g{g}", grid=(3, SEQ // tm),
        in_specs=[pl.BlockSpec(memory_space=pl.ANY),
                  pl.BlockSpec((1, tm, 512), lambda tq, i: (tq, i, 0)), tab, tab],
        out_specs=pl.BlockSpec((tm, 512), lambda tq, i: (i, COL_QKV // 512 + tq * 3 + g)),
        out_shape=jax.ShapeDtypeStruct((SEQ, N_PAD), BF16),
        input_output_aliases={0: 0},
    )(dp_in, dqkv, rc, rs)


TAIL_T = 256


def _tail(p, ya, o_g, l_g, x, target, wpm, wpd, wout, post_g):
    tm = TAIL_T

    def body(pgz_ref, ya_ref, o0_ref, o1_ref, o2_ref, l0_ref, l1_ref, l2_ref, x_ref, t_ref,
             wpm_ref, wpd_ref, wout_ref, pg_ref,
             dp_ref, dy_ref, mg_ref, dt_ref, ua_ref, dpa_ref, ud_ref, dpd_ref, dya_ref, dyd_ref,
             yd_ref, lse_ref, loss_ref, dgp_ref):
        l0, l1, l2 = l0_ref[...], l1_ref[...], l2_ref[...]
        mx = jnp.maximum(jnp.maximum(l0, l1), l2)
        e0, e1, e2 = jnp.exp(l0 - mx), jnp.exp(l1 - mx), jnp.exp(l2 - mx)
        den = e0 + e1 + e2
        yd = (e0 * o0_ref[...] + e1 * o1_ref[...] + e2 * o2_ref[...]) / den
        yd_ref[...] = yd
        lse_ref[...] = mx + jnp.log(den)
        ya = ya_ref[...]

        gm, gd = pgz_ref[:, 0:1024], pgz_ref[:, 1024:2048]
        zm, zd = pgz_ref[:, 2048:2560], pgz_ref[:, 2560:3072]
        szm, szd = _sigmoid(zm), _sigmoid(zd)
        sm, sd = zm * szm, zd * szd
        ua = (ya * sm).astype(BF16)
        ud = (yd * sd).astype(BF16)
        ua_ref[...] = ua
        ud_ref[...] = ud
        pa = _nn(ua, wpm_ref[...])
        pd = _nn(ud, wpd_ref[...])
        sgm, sgd = _sigmoid(gm), _sigmoid(gd)
        mg = (sgm * pa + sgd * pd).astype(BF16)
        mg_ref[...] = mg
        t = _nn(mg, wout_ref[...])
        r3 = lax.rsqrt(jnp.mean(t * t, axis=-1, keepdims=True) + EPS)
        n = t * r3
        pg = pg_ref[...]
        err = x_ref[...] + n * pg - t_ref[...]
        lpart = jnp.sum(err * err, axis=0, keepdims=True)

        dy = err * (1.0 / D_MODEL)
        dy_ref[...] = dy
        gpart = jnp.sum(dy * n, axis=0, keepdims=True)
        dn = dy * pg
        dt = (r3 * (dn - n * jnp.mean(dn * n, axis=-1, keepdims=True))).astype(BF16)
        dt_ref[...] = dt
        dmg = _nt(dt, wout_ref[...])
        dpa = (dmg * sgm).astype(BF16)
        dpd = (dmg * sgd).astype(BF16)
        dpa_ref[...] = dpa
        dpd_ref[...] = dpd
        dp_ref[:, 0:1024] = (dmg * pa * sgm * (1.0 - sgm)).astype(BF16)
        dp_ref[:, 1024:2048] = (dmg * pd * sgd * (1.0 - sgd)).astype(BF16)
        dua = _nt(dpa, wpm_ref[...])
        dud = _nt(dpd, wpd_ref[...])
        dya_ref[...] = dua * sm
        dyd_ref[...] = dud * sd
        dp_ref[:, 2048:2560] = (dua * ya * szm * (1.0 + zm * (1.0 - szm))).astype(BF16)
        dp_ref[:, 2560:3072] = (dud * yd * szd * (1.0 + zd * (1.0 - szd))).astype(BF16)

        @pl.when(pl.program_id(0) == 0)
        def _():
            loss_ref[...] = lpart
            dgp_ref[...] = gpart

        @pl.when(pl.program_id(0) > 0)
        def _():
            loss_ref[...] += lpart
            dgp_ref[...] += gpart

    def rows(w):
        return pl.BlockSpec((tm, w), lambda i: (i, 0))

    def full(shape):
        return pl.BlockSpec(shape, lambda i: (0, 0))

    def sds(w, dt):
        return jax.ShapeDtypeStruct((SEQ, w), dt)

    return pl.pallas_call(
        body, name="tail", grid=(SEQ // tm,),
        in_specs=[rows(3072), rows(512), rows(512), rows(512), rows(512), rows(512), rows(512), rows(512),
                  rows(1024), rows(1024), full((512, 1024)), full((512, 1024)), full((1024, 1024)), full((1, 1024))],
        out_specs=[rows(3072), rows(1024), rows(1024), rows(1024), rows(512), rows(1024), rows(512), rows(1024),
                   rows(512), rows(512), rows(512), rows(512), full((1, 1024)), full((1, 1024))],
        out_shape=[sds(N_PAD, BF16), sds(1024, F32), sds(1024, BF16), sds(1024, BF16), sds(512, BF16),
                   sds(1024, BF16), sds(512, BF16), sds(1024, BF16), sds(512, F32), sds(512, F32),
                   sds(512, F32), sds(512, F32),
                   jax.ShapeDtypeStruct((1, 1024), F32), jax.ShapeDtypeStruct((1, 1024), F32)],
        compiler_params=_cparams(),
    )(p, ya, o_g[0], o_g[1], o_g[2], l_g[0], l_g[1], l_g[2], x, target, wpm, wpd, wout, post_g)


def _sum_parts(recv, own, me, tr, name):
    n, r, w = recv.shape
    if r % tr:
        return _sum_parts_cols(recv, own, me, name)
    own_spec = (pl.BlockSpec((tr, w), lambda i, me_ref: (i, 0)) if own.ndim == 2
                else pl.BlockSpec((None, tr, w), lambda i, me_ref: (me_ref[0], i, 0)))

    def body(me_ref, p_ref, own_ref, o_ref):
        mine = own_ref[...].astype(F32)
        acc = jnp.zeros((tr, w), F32)
        for s in range(n):
            acc = acc + jnp.where(me_ref[0] == s, mine, p_ref[s].astype(F32))
        o_ref[...] = acc

    return pl.pallas_call(
        body, name=name,
        grid_spec=pltpu.PrefetchScalarGridSpec(
            num_scalar_prefetch=1, grid=(r // tr,),
            in_specs=[pl.BlockSpec((n, tr, w), lambda i, me_ref: (0, i, 0)), own_spec],
            out_specs=pl.BlockSpec((tr, w), lambda i, me_ref: (i, 0))),
        out_shape=jax.ShapeDtypeStruct((r, w), F32),
    )(me.reshape(1), recv, own)


def _sum_parts_cols(recv, own, me, name):
    n, r, w = recv.shape
    tc = 128

    def body(me_ref, p_ref, own_ref, o_ref):
        mine = own_ref[...].astype(F32)
        acc = jnp.zeros((r, tc), F32)
        for s in range(n):
            acc = acc + jnp.where(me_ref[0] == s, mine, p_ref[s].astype(F32))
        o_ref[...] = acc

    return pl.pallas_call(
        body, name=name,
        grid_spec=pltpu.PrefetchScalarGridSpec(
            num_scalar_prefetch=1, grid=(w // tc,),
            in_specs=[pl.BlockSpec((n, r, tc), lambda i, me_ref: (0, 0, i)),
                      pl.BlockSpec((None, r, tc), lambda i, me_ref: (me_ref[0], 0, i))],
            out_specs=pl.BlockSpec((r, tc), lambda i, me_ref: (0, i))),
        out_shape=jax.ShapeDtypeStruct((r, w), F32),
    )(me.reshape(1), recv, own)


def _adamw(w, g, m, v, name):
    lead = w.shape[:-2]
    r, c = w.shape[-2:]
    tr = max([t for t in range(8, 257, 8) if r % t == 0], default=r)
    c1 = 1.0 - ADAM_B1 ** ADAM_STEP
    c2 = 1.0 - ADAM_B2 ** ADAM_STEP

    def body(w_ref, g_ref, m_ref, v_ref, d_ref, nm_ref, nv_ref):
        gv = g_ref[...]
        nm = ADAM_B1 * m_ref[...] + (1.0 - ADAM_B1) * gv
        nv = ADAM_B2 * v_ref[...] + (1.0 - ADAM_B2) * (gv * gv)
        nm_ref[...] = nm
        nv_ref[...] = nv
        d_ref[...] = -ADAM_LR * ((nm / c1) / (jnp.sqrt(nv / c2) + ADAM_EPS) + ADAM_WD * w_ref[...])

    zeros = (0,) * len(lead)
    spec = pl.BlockSpec((1,) * len(lead) + (tr, c), lambda i: zeros + (i, 0))
    sd = jax.ShapeDtypeStruct(w.shape, F32)
    return pl.pallas_call(
        body, name=name, grid=(r // tr,),
        in_specs=[spec] * 4, out_specs=[spec] * 3, out_shape=[sd] * 3,
    )(w, g, m, v)


def _adamw_in(w_t, m_t, v_t, own_half, swapped, core):
    r, c = SHARD_SHAPES[0]
    tr = max(t for t in range(8, 257, 8) if r % t == 0)
    c1 = 1.0 - ADAM_B1 ** ADAM_STEP
    c2 = 1.0 - ADAM_B2 ** ADAM_STEP

    def body(core_ref, w_ref, m_ref, v_ref, own_ref, sw_ref, d_ref, nm_ref, nv_ref, g_ref):
        own = own_ref[...]
        col_half = lax.broadcasted_iota(jnp.int32, (tr, c), 1) // (c // 2)
        gv = jnp.where(col_half == core_ref[0], jnp.concatenate([own, own], axis=1), sw_ref[...])
        g_ref[0] = gv
        nm = ADAM_B1 * m_ref[0] + (1.0 - ADAM_B1) * gv
        nv = ADAM_B2 * v_ref[0] + (1.0 - ADAM_B2) * (gv * gv)
        nm_ref[0] = nm
        nv_ref[0] = nv
        d_ref[0] = -ADAM_LR * ((nm / c1) / (jnp.sqrt(nv / c2) + ADAM_EPS) + ADAM_WD * w_ref[0])

    full = pl.BlockSpec((1, tr, c), lambda i, core_ref: (0, i, 0))
    sd = jax.ShapeDtypeStruct((1, r, c), F32)
    return pl.pallas_call(
        body, name="adamw_in",
        grid_spec=pltpu.PrefetchScalarGridSpec(
            num_scalar_prefetch=1, grid=(r // tr,),
            in_specs=[full, full, full, pl.BlockSpec((tr, c // 2), lambda i, core_ref: (i, 0)),
                      pl.BlockSpec((tr, c), lambda i, core_ref: (i, 0))],
            out_specs=[full] * 4),
        out_shape=[sd] * 4,
    )(core.reshape(1), w_t, m_t, v_t, own_half, swapped)


ANY = pl.BlockSpec(memory_space=pl.ANY)


def _my_place():
    return lax.axis_index("x"), lax.axis_index("y"), lax.axis_index("c")


HBM = pl.BlockSpec(memory_space=pltpu.HBM)
SEM = pl.BlockSpec(memory_space=pltpu.SEMAPHORE)
DATAFLOW = pltpu.SideEffectType.DATAFLOW_SIDE_EFFECTING


def _near_chips(x, y):
    return [(1 - x, y), (x, 1 - y)]


def _half(mi, hc):
    r, c = SHARD_SHAPES[mi]
    if mi == 0:
        return pl.ds(0, r), pl.ds(pl.multiple_of(hc * (c // 2), 128), c // 2)
    return pl.ds(pl.multiple_of(hc * (r // 2), 16), r // 2), pl.ds(0, c)


def _gather_copies(m_refs, land_refs, send_sems, recv_sems):
    x, y, c = _my_place()
    out, back = [], []
    for mi in range(N_MATS):
        rows, cols = _half(mi, c)
        for j, (cx, cy) in enumerate(_near_chips(x, y)):
            sems = dict(send_sem=send_sems.at[mi * 2 + j], recv_sem=recv_sems.at[mi * 2 + j],
                        device_id=(cx, cy, c), device_id_type=MESH)
            out.append(pltpu.make_async_remote_copy(src_ref=m_refs[mi].at[rows, cols],
                                                    dst_ref=land_refs[mi].at[2 * x + y, rows, cols], **sems))
            got = land_refs[mi].at[2 * cx + cy, rows, cols]
            back.append(pltpu.make_async_remote_copy(src_ref=got, dst_ref=got, **sems))
    return out, back


def _gather_start(mats, landing):
    n = N_MATS

    def body(*refs):
        out, _ = _gather_copies(refs[:n], refs[n:2 * n], refs[2 * n], refs[2 * n + 1])
        for cp in out:
            cp.start()
        refs[-1][...] = jnp.zeros_like(refs[-1])

    hbm = [pltpu.HBM(a.shape, a.dtype) for a in list(mats) + list(landing)]
    outs = pl.pallas_call(
        body, name="gather_start",
        out_shape=(pltpu.SemaphoreType.DMA((2 * n,)), pltpu.SemaphoreType.DMA((2 * n,)), *hbm,
                   jax.ShapeDtypeStruct((8, 128), F32)),
        in_specs=[HBM] * (2 * n), out_specs=(SEM, SEM, *[HBM] * (2 * n), pl.BlockSpec(memory_space=pltpu.VMEM)),
        input_output_aliases={i: 2 + i for i in range(2 * n)},
        compiler_params=pltpu.CompilerParams(has_side_effects=DATAFLOW),
    )(*[pltpu.with_memory_space_constraint(a, pltpu.HBM) for a in list(mats) + list(landing)])
    return outs[:-1], outs[-1]


def _gather_wait(handle, after):
    n = N_MATS

    def body(*refs):
        out, back = _gather_copies(refs[:n], refs[n:2 * n], refs[2 * n], refs[2 * n + 1])
        for cp, arrival in zip(out, back):
            cp.wait_send()
            arrival.wait_recv()

    bufs = handle[2:]
    after, after_specs = _after(after)
    res = pl.pallas_call(
        body, name="gather_wait", out_shape=tuple(pltpu.HBM(b.shape, b.dtype) for b in bufs),
        in_specs=[HBM] * (2 * n) + [SEM, SEM] + after_specs, out_specs=tuple([HBM] * (2 * n)),
        input_output_aliases={i: i for i in range(2 * n)},
        compiler_params=pltpu.CompilerParams(has_side_effects=DATAFLOW),
    )(*bufs, handle[0], handle[1], *after)
    return list(res[n:])


def _relay_share(gathered):
    n = N_MATS

    def body(*refs):
        out_refs = refs[n:2 * n]
        send_sems, recv_sems = refs[2 * n:]
        x, y, c = _my_place()
        sibling = (x, y, 1 - c)
        relayed = 2 * (x ^ (1 - c)) + (y ^ c)
        relay_to = (x ^ c, y ^ (1 - c), c)
        far = 2 * (1 - x) + (1 - y)
        near = [2 * (1 - x) + y, 2 * x + (1 - y)]

        def copy(k, mi, shard, hc, to):
            blk = out_refs[mi].at[(shard,) + _half(mi, hc)]
            return pltpu.make_async_remote_copy(src_ref=blk, dst_ref=blk, send_sem=send_sems.at[mi * 4 + k],
                                                recv_sem=recv_sems.at[mi * 4 + k], device_id=to, device_id_type=MESH)

        sends = []
        for mi in range(n):
            sends.append(copy(0, mi, relayed, c, relay_to))
            sends += [copy(1 + j, mi, near[j], c, sibling) for j in range(2)]
        for cp in sends:
            cp.start()
        for mi in range(n):
            copy(0, mi, far, c, relay_to).wait_recv()
            cp = copy(3, mi, far, c, sibling)
            cp.start()
            sends.append(cp)
        for mi in range(n):
            for j in range(2):
                copy(1 + j, mi, near[j], 1 - c, sibling).wait_recv()
            copy(3, mi, far, 1 - c, sibling).wait_recv()
        for cp in sends:
            cp.wait_send()

    return pl.pallas_call(
        body, name="relay_share",
        in_specs=[ANY] * n, out_specs=[ANY] * n,
        out_shape=[jax.ShapeDtypeStruct(g.shape, g.dtype) for g in gathered],
        input_output_aliases={i: i for i in range(n)},
        scratch_shapes=[pltpu.SemaphoreType.DMA((4 * n,)), pltpu.SemaphoreType.DMA((4 * n,))],
    )(*gathered)


def _peers(x, y, c):
    out = []
    for k in range(1, 8):
        px, py, pc = x ^ (k >> 2), y ^ ((k >> 1) & 1), c ^ (k & 1)
        out.append((k - 1, (px, py, pc), 4 * px + 2 * py + pc))
    return out


def _exchange_start(parts, name):
    n = len(parts)

    def body(*refs):
        p_refs, land_refs = refs[:n], refs[n:2 * n]
        send_sems, recv_sems, token = refs[2 * n], refs[2 * n + 1], refs[-1]
        x, y, c = _my_place()
        me = 4 * x + 2 * y + c
        for k, dev, peer in _peers(x, y, c):
            for mi in range(n):
                pltpu.make_async_remote_copy(
                    src_ref=p_refs[mi].at[peer], dst_ref=land_refs[mi].at[me], send_sem=send_sems.at[k * n + mi],
                    recv_sem=recv_sems.at[k * n + mi], device_id=dev, device_id_type=MESH).start()
        token[...] = jnp.zeros_like(token)

    hbm = [pltpu.HBM(p.shape, p.dtype) for p in parts]
    outs = pl.pallas_call(
        body, name=name + "_start",
        out_shape=(pltpu.SemaphoreType.DMA((7 * n,)), pltpu.SemaphoreType.DMA((7 * n,)), *hbm, *hbm,
                   jax.ShapeDtypeStruct((8, 128), F32)),
        in_specs=[HBM] * (2 * n), out_specs=(SEM, SEM, *[HBM] * (2 * n), pl.BlockSpec(memory_space=pltpu.VMEM)),
        input_output_aliases={i: 2 + i for i in range(2 * n)},
        compiler_params=pltpu.CompilerParams(has_side_effects=DATAFLOW),
    )(*[pltpu.with_memory_space_constraint(p, pltpu.HBM) for p in parts],
      *[pltpu.with_memory_space_constraint(lax.empty(p.shape, p.dtype), pltpu.HBM) for p in parts])
    return (name, outs[:-1]), outs[-1]


def _exchange_wait(handle, after):
    name, outs = handle
    n = (len(outs) - 2) // 2

    def body(*refs):
        p_refs, land_refs = refs[:n], refs[n:2 * n]
        send_sems, recv_sems = refs[2 * n], refs[2 * n + 1]
        x, y, c = _my_place()
        me = 4 * x + 2 * y + c
        for k, dev, peer in _peers(x, y, c):
            for mi in range(n):
                pltpu.make_async_remote_copy(
                    src_ref=p_refs[mi].at[peer], dst_ref=land_refs[mi].at[me], send_sem=send_sems.at[k * n + mi],
                    recv_sem=recv_sems.at[k * n + mi], device_id=dev, device_id_type=MESH).wait_send()
                slot = land_refs[mi].at[peer]
                pltpu.make_async_remote_copy(
                    src_ref=slot, dst_ref=slot, send_sem=send_sems.at[k * n + mi],
                    recv_sem=recv_sems.at[k * n + mi], device_id=dev, device_id_type=MESH).wait_recv()

    bufs = outs[2:]
    res = pl.pallas_call(
        body, name=name + "_wait", out_shape=tuple(pltpu.HBM(b.shape, b.dtype) for b in bufs),
        in_specs=[HBM] * (2 * n) + [SEM, SEM, ANY], out_specs=tuple([HBM] * (2 * n)),
        input_output_aliases={i: i for i in range(2 * n)},
        compiler_params=pltpu.CompilerParams(has_side_effects=DATAFLOW),
    )(*bufs, outs[0], outs[1], after)
    return list(res[n:])


def _swap_halves(halves, gvec):
    def place(ref, mi, hc):
        return ref.at[:, pl.ds(pl.multiple_of(hc * 512, 128), 512)] if mi == 0 else ref.at[hc]

    def body(*refs):
        g_refs, gv_ref = refs[:N_MATS], refs[N_MATS]
        out_refs, rg_ref = refs[N_MATS + 1:2 * N_MATS + 1], refs[2 * N_MATS + 1]
        send_sems, recv_sems = refs[2 * N_MATS + 2:]
        x, y, c = _my_place()
        me = 4 * x + 2 * y + c
        sends = []
        for mi in range(N_MATS):
            cp = pltpu.make_async_remote_copy(src_ref=g_refs[mi], dst_ref=place(out_refs[mi], mi, c), send_sem=send_sems.at[mi],
                                              recv_sem=recv_sems.at[mi], device_id=(x, y, 1 - c), device_id_type=MESH)
            cp.start()
            sends.append(cp)
        for k, dev, peer in _peers(x, y, c):
            cp = pltpu.make_async_remote_copy(src_ref=gv_ref, dst_ref=rg_ref.at[me], send_sem=send_sems.at[N_MATS + k],
                                              recv_sem=recv_sems.at[N_MATS + k], device_id=dev, device_id_type=MESH)
            cp.start()
            sends.append(cp)
        for mi in range(N_MATS):
            got = place(out_refs[mi], mi, 1 - c)
            pltpu.make_async_remote_copy(src_ref=got, dst_ref=got, send_sem=send_sems.at[mi], recv_sem=recv_sems.at[mi],
                                         device_id=(x, y, 1 - c), device_id_type=MESH).wait_recv()
        for k, dev, peer in _peers(x, y, c):
            got = rg_ref.at[peer]
            pltpu.make_async_remote_copy(src_ref=got, dst_ref=got, send_sem=send_sems.at[N_MATS + k],
                                         recv_sem=recv_sems.at[N_MATS + k], device_id=dev, device_id_type=MESH).wait_recv()
        for cp in sends:
            cp.wait_send()

    outs = pl.pallas_call(
        body, name="swap_halves",
        in_specs=[ANY] * (N_MATS + 1), out_specs=[ANY] * (N_MATS + 1),
        out_shape=[jax.ShapeDtypeStruct(SHARD_SHAPES[0], F32)]
        + [jax.ShapeDtypeStruct((2, r // 2, c), F32) for r, c in SHARD_SHAPES[1:]]
        + [jax.ShapeDtypeStruct((8, 8, N_GVEC), F32)],
        scratch_shapes=[pltpu.SemaphoreType.DMA((N_MATS + 7,)), pltpu.SemaphoreType.DMA((N_MATS + 7,))],
    )(*halves, gvec)
    return outs[:N_MATS], outs[N_MATS]


def _set_slot(arr, block, idx):
    return lax.dynamic_update_slice(arr, block[None], (idx,) + (0,) * block.ndim)


PAD_RUNS = ((6304, 8352, 0), (5280, 6304, COL_Z), (672, 5280, COL_QKV), (0, 640, COL_LAT), (640, 672, COL_LAT + 704))
W_IN_SHARD = 2088


def _full_weights(gathered):
    def cols(a):
        return jnp.concatenate([a[s] for s in range(4)], axis=1)

    w_uq, w_ukv, w_pm, w_pd = [cols(a) for a in gathered[1:5]]
    w_out = gathered[5].reshape(D_MODEL, D_MODEL)
    w_in_t = gathered[0].reshape(4 * W_IN_SHARD, D_MODEL)
    pieces, at = [], 0
    for lo, hi, pad_lo in sorted(PAD_RUNS, key=lambda t: t[2]):
        if pad_lo > at:
            pieces.append(jnp.zeros((pad_lo - at, D_MODEL), w_in_t.dtype))
        pieces.append(w_in_t[lo:hi])
        at = pad_lo + hi - lo
    pieces.append(jnp.zeros((N_PAD - at, D_MODEL), w_in_t.dtype))
    w_pad_t = jnp.concatenate(pieces, axis=0)
    z32 = jnp.zeros((Q_RANK, 32), w_uq.dtype)
    wuq_pad = jnp.concatenate([t for h in range(MLA_HEADS) for t in (w_uq[:, h * 96:(h + 1) * 96], z32)], axis=1)
    z64 = jnp.zeros((KV_RANK, 64), w_ukv.dtype)
    wk_pad = jnp.concatenate([t for h in range(MLA_HEADS) for t in (w_ukv[:, h * 128:h * 128 + 64], z64)], axis=1)
    wv = jnp.concatenate([w_ukv[:, h * 128 + 64:(h + 1) * 128] for h in range(MLA_HEADS)], axis=1)
    return w_pad_t.T, w_pad_t, wuq_pad, wk_pad, wv, w_pm, w_pd, w_out


W_IN_LAT = 672


def _grad_parts_in_early(dwt_early):
    def in_block(s, h):
        cols = slice(h * 512, (h + 1) * 512)
        out = []
        for lo, hi, pad_lo in sorted(PAD_RUNS):
            a_, b_ = max(lo, s * W_IN_SHARD), min(hi, (s + 1) * W_IN_SHARD)
            if a_ < b_:
                out.append(jnp.zeros((b_ - a_, 512), dwt_early.dtype) if pad_lo >= COL_LAT
                           else dwt_early[pad_lo + a_ - lo:pad_lo + b_ - lo, cols])
        return jnp.concatenate(out, axis=0)

    return jnp.stack([in_block(s, h) for s in range(4) for h in range(2)])


def _grad_parts_in_late(dwt_late):
    rows = jnp.concatenate([dwt_late[0:640], dwt_late[704:736]], axis=0)
    zero = jnp.zeros((W_IN_LAT, 512), dwt_late.dtype)
    return jnp.stack([rows[:, 0:512], rows[:, 512:1024]] + [zero] * 6)


def _col_blocks(m):
    r, c = m.shape[0] // 2, m.shape[1] // 4
    return jnp.stack([m[h * r:(h + 1) * r, s * c:(s + 1) * c] for s in range(4) for h in range(2)])


def _grad_parts_mla(dwuq_pad, dwk_pad, dwv):
    d_uq = jnp.concatenate([dwuq_pad[:, h * 128:h * 128 + 96] for h in range(MLA_HEADS)], axis=1)
    d_ukv = jnp.concatenate([t for h in range(MLA_HEADS) for t in (dwk_pad[:, h * 128:h * 128 + 64], dwv[:, h * 64:(h + 1) * 64])],
                            axis=1)
    return [_col_blocks(d_uq), _col_blocks(d_ukv)]


def _rope_tables(positions, token=None):
    pos = positions.reshape(SEQ).astype(F32)
    if token is not None:
        pos = pos + token[0, 0]
    lane = jnp.arange(128)

    def table(rot, first, period):
        inv = ROPE_THETA ** (-jnp.arange(0, rot, 2, dtype=F32) / rot)
        half = rot // 2
        off = lane % period - first
        in1, in2 = (off >= 0) & (off < half), (off >= half) & (off < rot)
        inv_lane = jnp.where(in1 | in2, inv[jnp.clip(off % half, 0, half - 1)], 0.0)
        sign = jnp.where(in1, -1.0, 1.0).astype(F32)
        ang = pos[:, None] * inv_lane[None, :]
        return jnp.cos(ang), jnp.sin(ang) * sign[None, :]

    return table(32, 64, 128), table(16, 0, 64)


class _Links:
    def __init__(self, mats, chip, me):
        landing = [_set_slot(lax.empty((4,) + m.shape, m.dtype), m, chip) for m in mats]
        self.gather, self.token = _gather_start(mats, landing)
        self.me, self.sent, self.handles, self.sums = me, {}, {}, {}

    def weights(self, after):
        return _relay_share(_gather_wait(self.gather, after))

    def send(self, blocks, name):
        self.sent[name] = blocks
        self.handles[name], token = _exchange_start(blocks, name)
        return token

    def collect(self, name, after, parts):
        recv = _exchange_wait(self.handles[name], after)
        for r, own, part in zip(recv, self.sent[name], parts):
            self.sums[part] = _sum_parts(r, own, self.me, 64, "sum_grad_" + part)
        return tuple(self.sums[part] for part in parts)


def _device_grads(x, positions, target, gains, links):
    pre_g, q_g, kv_g, post_g = gains
    (mc, ms), (dc, ds) = _rope_tables(positions, links.token)
    h = _prenorm_fwd(x, pre_g, links.token)
    w_pad, w_pad_t, wuq_pad, wk_pad, wv, w_pm, w_pd, w_out = _full_weights(links.weights((h, mc, ms, dc, ds)))

    p = _matmul(h, w_pad, "nn", F32, 1024, 1408, 1024, "in_proj")
    cqn, ckvn, q, k, v = _mla_prep_fwd(p, q_g, kv_g, wuq_pad, wk_pad, wv, mc, ms)
    ya, lse_m = _mla_flash_fwd(q, k, v)
    qkv = [_dil_prep_fwd(p, dc, ds, g) for g in range(3)]
    o_g, l_g = zip(*[_dil_attn_fwd2(qkv[g], g) for g in range(3)])
    (dp, dy, mg, dt, ua, dpa, ud, dpd, dya, dyd, yd, lse_d, loss_cols, dg_post) = _tail(
        p, ya, o_g, l_g, x, target, w_pm, w_pd, w_out, post_g)

    for g in range(3):
        dqkv = _dil_attn_bwd2(qkv[g], dyd, yd, lse_d, g)
        dp = _dil_prep_bwd(dp, dqkv, dc, ds, g)
    dw_early = _matmul(dp, h, "tn", BF16, 1536, 1024, 2048, "dw_in_early", a_cols=(0, COL_LAT // 1536))
    dwpm = _matmul(ua, dpa, "tn", BF16, 512, 1024, 512, "dw_proj_mla")
    dwpd = _matmul(ud, dpd, "tn", BF16, 512, 1024, 512, "dw_proj_dil")
    dwout = _matmul(mg, dt, "tn", BF16, 1024, 1024, 512, "dw_out")
    token = links.send([_grad_parts_in_early(dw_early), _col_blocks(dwpm), _col_blocks(dwpd),
                        dwout.reshape(8, 128, D_MODEL)], "exchange_early")

    dq, dk, dv = _mla_flash_bwd(q, k, v, ya, dya, lse_m, token)
    dp, dqb, dkb, dvb, dg_q, dg_kv = _mla_prep_bwd(dp, p, dq, dk, dv, q_g, kv_g, wuq_pad, wk_pad, wv, mc, ms)
    dwuq_pad = _matmul(cqn, dqb, "tn", BF16, Q_RANK, 1024, 512, "dw_uq")
    dwk_pad = _matmul(ckvn, dkb, "tn", BF16, KV_RANK, 1024, 512, "dw_k")
    dwv = _matmul(ckvn, dvb, "tn", BF16, KV_RANK, 512, 512, "dw_v")
    dw_late = _matmul(dp, h, "tn", BF16, N_LAT, 1024, 2048, "dw_in_late", a_cols=(COL_LAT // N_LAT, 1))
    token = links.send([_grad_parts_in_late(dw_late)] + _grad_parts_mla(dwuq_pad, dwk_pad, dwv), "exchange_late")
    early = links.collect("exchange_early", dw_late, ("in_early", "pm", "pd", "out"))

    dh = _matmul(dp, w_pad_t, "nn", F32, 1024, 1024, 1408, "dh", (token,) + tuple(early))
    grad_x, dg_pre = _prenorm_bwd(x, dh, dy, pre_g)
    links.collect("exchange_late", grad_x, ("in_late", "uq", "ukv"))

    loss_part = jnp.pad((jnp.sum(loss_cols) * (0.5 / D_MODEL)).reshape(1, 1), ((0, 0), (0, N_GVEC - N_GAINS - 1)))
    gvec = jnp.concatenate([dg_pre, dg_q, dg_kv, dg_post, loss_part], axis=1)
    return grad_x, gvec


def kernel(x, positions, pre_norm_g, w_in, q_norm_g, w_uq, kv_norm_g, w_ukv, w_proj_mla, w_proj_dil, w_out, post_norm_g, loss_target, m_pre_norm_g, m_w_in, m_q_norm_g, m_w_uq, m_kv_norm_g, m_w_ukv, m_w_proj_mla, m_w_proj_dil, m_w_out, m_post_norm_g, v_pre_norm_g, v_w_in, v_q_norm_g, v_w_uq, v_kv_norm_g, v_w_ukv, v_w_proj_mla, v_w_proj_dil, v_w_out, v_post_norm_g):
    xi, yi, ci = _my_place()
    chip, me = 2 * xi + yi, 4 * xi + 2 * yi + ci
    mats = [jnp.swapaxes(w_in, 1, 2)] + [w_uq, w_ukv, w_proj_mla, w_proj_dil, w_out]
    mats = [w.reshape(w.shape[1:]).astype(BF16) for w in mats]
    links = _Links(mats, chip, me)
    gains = (pre_norm_g, q_norm_g, kv_norm_g, post_norm_g)
    grad_x, gvec = _device_grads(x[0], positions, loss_target[0], gains, links)

    sums = links.sums
    in_e = sums["in_early"]
    half_in = jnp.concatenate([in_e[:W_IN_LAT] + jnp.where(chip == 0, sums["in_late"], 0.0), in_e[W_IN_LAT:]], axis=0)
    halves = [half_in, sums["uq"], sums["ukv"], sums["pm"], sums["pd"], sums["out"]]
    gvec8 = jnp.pad(gvec, ((0, 7), (0, 0)))
    swapped, recv_gains = _swap_halves(halves, gvec8)
    g_gains = _sum_parts(recv_gains, gvec8, me, 8, "sum_gain_parts")[0:1]
    loss = g_gains[0, N_GAINS]
    sw = lambda a: jnp.swapaxes(a, 1, 2)
    d_in, m_in, v_in, g_in = [sw(o) for o in _adamw_in(sw(w_in), sw(m_w_in), sw(v_w_in), half_in, swapped[0], ci)]
    g_mats = [g_in] + [_set_slot(s, hf, ci).reshape((1,) + shp)
                       for s, hf, shp in zip(swapped[1:], halves[1:], SHARD_SHAPES[1:])]

    off = [0, 1024, 1408, 1664, 2688]
    g_gain = [g_gains[:, off[i]:off[i + 1]] for i in range(4)]
    grads = [g_gain[0], g_mats[0], g_gain[1], g_mats[1], g_gain[2], g_mats[2], g_mats[3], g_mats[4], g_mats[5], g_gain[3]]
    ws = [pre_norm_g, w_in, q_norm_g, w_uq, kv_norm_g, w_ukv, w_proj_mla, w_proj_dil, w_out, post_norm_g]
    ms = [m_pre_norm_g, m_w_in, m_q_norm_g, m_w_uq, m_kv_norm_g, m_w_ukv, m_w_proj_mla, m_w_proj_dil, m_w_out, m_post_norm_g]
    vs = [v_pre_norm_g, v_w_in, v_q_norm_g, v_w_uq, v_kv_norm_g, v_w_ukv, v_w_proj_mla, v_w_proj_dil, v_w_out, v_post_norm_g]
    deltas, new_m, new_v = [], [], []
    for i, (w, g, m, v) in enumerate(zip(ws, grads, ms, vs)):
        if w is w_in:
            d_, m_, v_ = d_in, m_in, v_in
        elif w.shape[-1] % 128 and w.shape[-2] % 128 == 0:
            g = jnp.swapaxes(g, 1, 2)
            grads[i] = jnp.swapaxes(g, 1, 2)
            d_, m_, v_ = [jnp.swapaxes(o, 1, 2) for o in
                          _adamw(jnp.swapaxes(w, 1, 2), g, jnp.swapaxes(m, 1, 2), jnp.swapaxes(v, 1, 2), f"adamw_{i}")]
        else:
            d_, m_, v_ = _adamw(w, g, m, v, f"adamw_{i}")
        deltas.append(d_)
        new_m.append(m_)
        new_v.append(v_)
    return (loss, grad_x.reshape(x.shape), *grads, *deltas, *new_m, *new_v)
```

```python
import jax
import jax.numpy as jnp
from jax import lax
from jax.experimental import pallas as pl
from jax.experimental.pallas import tpu as pltpu

F32 = jnp.float32
BF16 = jnp.bfloat16

SEQ = 4096
D_MODEL = 1024
EPS = 1e-6
ROPE_THETA = 500000.0
MLA_HEADS = 8
Q_RANK = 384
KV_RANK = 256
MLA_SCALE = 96.0 ** -0.5
MLA_ROPE_HALF = 16
DIL_DILATIONS = (1, 4, 16)
DIL_ROPE_HALF = 8
DIL_SCALE = 0.125
BAND = 128

N_LAT = 768
COL_Z, COL_QKV, COL_LAT = 2048, 3072, 7680
N_PAD = 8448
IN_SPLITS = (384, 256, 32, 4608, 512, 512, 1024, 1024)

SHARD_SHAPES = ((2088, 1024), (384, 192), (256, 256), (512, 256), (512, 256), (256, 1024))
N_MATS = len(SHARD_SHAPES)
N_GAINS = 2688
N_GVEC = N_GAINS + 128

ADAM_LR, ADAM_B1, ADAM_B2, ADAM_EPS, ADAM_WD, ADAM_STEP = 0.001, 0.9, 0.999, 1e-08, 0.01, 10

VMEM_LIMIT = 56 * 1024 * 1024
NEG = -1e30
MESH = pl.DeviceIdType.MESH


def _cparams(**kw):
    return pltpu.CompilerParams(vmem_limit_bytes=VMEM_LIMIT, **kw)


def _dot(a, b, dims):
    return lax.dot_general(a, b, (dims, ((), ())), preferred_element_type=F32)


def _nn(a, b):
    return _dot(a, b, ((1,), (0,)))


def _nt(a, b):
    return _dot(a, b, ((1,), (1,)))


def _tn(a, b):
    return _dot(a, b, ((0,), (0,)))


def _rope_lanes(shape, half, period, first):
    lane = lax.broadcasted_iota(jnp.int32, shape, len(shape) - 1) % period
    return (lane >= first) & (lane < first + half), (lane >= first + half) & (lane < first + 2 * half)


def _rope_fwd(x, c, s, half, lanes):
    x1, _ = lanes
    return x * c + jnp.where(x1, pltpu.roll(x, 128 - half, 1), pltpu.roll(x, half, 1)) * s


def _rope_bwd(g, c, s, half, lanes):
    x1, x2 = lanes
    gs = g * s
    return g * c + jnp.where(x2, pltpu.roll(gs, half, 1), jnp.where(x1, pltpu.roll(gs, 128 - half, 1), 0.0))


def _sigmoid(x):
    return 1.0 / (1.0 + jnp.exp(-x))


def _after(token):
    tokens = [t for t in (token if isinstance(token, (tuple, list)) else [token]) if t is not None]
    return tokens, [pl.BlockSpec(memory_space=pl.ANY)] * len(tokens)


def _matmul(a, b, mode, out_dtype, tm, tn, tk, name, token=None, b_cols=None, a_cols=None):
    after, after_specs = _after(token)
    if mode == "nn":
        (m, k), n = a.shape, b.shape[1]
        first = 0
        if b_cols is not None:
            first, n = b_cols[0], b_cols[1] * tn
        a_spec = pl.BlockSpec((tm, tk), lambda j, i, kk: (i, kk))
        b_spec = pl.BlockSpec((tk, tn), lambda j, i, kk: (kk, j + first))
        dot = _nn
    elif mode == "nt":
        (m, k), n = a.shape, b.shape[0]
        a_spec = pl.BlockSpec((tm, tk), lambda j, i, kk: (i, kk))
        b_spec = pl.BlockSpec((tn, tk), lambda j, i, kk: (j, kk))
        dot = _nt
    else:
        (k, m), n = a.shape, b.shape[1]
        first = 0
        if a_cols is not None:
            first, m = a_cols[0], a_cols[1] * tm
        a_spec = pl.BlockSpec((tk, tm), lambda j, i, kk: (kk, i + first))
        b_spec = pl.BlockSpec((tk, tn), lambda j, i, kk: (kk, j))
        dot = _tn
    assert m % tm == 0 and n % tn == 0 and k % tk == 0, (name, m, n, k, tm, tn, tk)
    nk = k // tk

    def body(a_ref, b_ref, *rest):
        o_ref, acc_ref = rest[-2:]
        kk = pl.program_id(2)
        part = dot(a_ref[...], b_ref[...])

        @pl.when(kk == 0)
        def _():
            acc_ref[...] = part

        @pl.when(kk > 0)
        def _():
            acc_ref[...] += part

        @pl.when(kk == nk - 1)
        def _():
            o_ref[...] = acc_ref[...].astype(o_ref.dtype)

    return pl.pallas_call(
        body, name=name, grid=(n // tn, m // tm, nk),
        in_specs=[a_spec, b_spec] + after_specs,
        out_specs=pl.BlockSpec((tm, tn), lambda j, i, kk: (i, j)),
        out_shape=jax.ShapeDtypeStruct((m, n), out_dtype),
        scratch_shapes=[pltpu.VMEM((tm, tn), F32)],
        compiler_params=_cparams(),
    )(a, b, *after)


def _prenorm_fwd(x, g, token=None):
    tm = 512
    after, after_specs = _after(token)

    def body(x_ref, g_ref, *rest):
        xv = x_ref[...]
        r = lax.rsqrt(jnp.mean(xv * xv, axis=-1, keepdims=True) + EPS)
        rest[-1][...] = (xv * r * g_ref[...]).astype(BF16)

    return pl.pallas_call(
        body, name="prenorm_fwd", grid=(SEQ // tm,),
        in_specs=[pl.BlockSpec((tm, D_MODEL), lambda i: (i, 0)), pl.BlockSpec((1, D_MODEL), lambda i: (0, 0))] + after_specs,
        out_specs=pl.BlockSpec((tm, D_MODEL), lambda i: (i, 0)),
        out_shape=jax.ShapeDtypeStruct((SEQ, D_MODEL), BF16),
    )(x, g, *after)


def _prenorm_bwd(x, dh, dy, g):
    tm = 512

    def body(x_ref, dh_ref, dy_ref, g_ref, gx_ref, dg_ref):
        xv = x_ref[...]
        r = lax.rsqrt(jnp.mean(xv * xv, axis=-1, keepdims=True) + EPS)
        n = xv * r
        dhv = dh_ref[...]
        dn = dhv * g_ref[...]
        gx_ref[...] = dy_ref[...] + r * (dn - n * jnp.mean(dn * n, axis=-1, keepdims=True))
        part = jnp.sum(dhv * n, axis=0, keepdims=True)

        @pl.when(pl.program_id(0) == 0)
        def _():
            dg_ref[...] = part

        @pl.when(pl.program_id(0) > 0)
        def _():
            dg_ref[...] += part

    row = pl.BlockSpec((tm, D_MODEL), lambda i: (i, 0))
    vec = pl.BlockSpec((1, D_MODEL), lambda i: (0, 0))
    return pl.pallas_call(
        body, name="prenorm_bwd", grid=(SEQ // tm,),
        in_specs=[row, row, row, vec], out_specs=[row, vec],
        out_shape=[jax.ShapeDtypeStruct((SEQ, D_MODEL), F32), jax.ShapeDtypeStruct((1, D_MODEL), F32)],
        compiler_params=_cparams(),
    )(x, dh, dy, g)


def _dh_prenorm_bwd(dp, w_pad_t, x, dy, g, token=None):
    tm, tk = 1024, 1408
    nk = N_PAD // tk
    after, after_specs = _after(token)

    def body(a_ref, b_ref, x_ref, dy_ref, g_ref, *rest):
        gx_ref, dg_ref, acc_ref = rest[-3:]
        i, kk = pl.program_id(0), pl.program_id(1)
        part = _nn(a_ref[...], b_ref[...])

        @pl.when(kk == 0)
        def _():
            acc_ref[...] = part

        @pl.when(kk > 0)
        def _():
            acc_ref[...] += part

        @pl.when(kk == nk - 1)
        def _():
            xv = x_ref[...]
            r = lax.rsqrt(jnp.mean(xv * xv, axis=-1, keepdims=True) + EPS)
            n = xv * r
            dhv = acc_ref[...]
            dn = dhv * g_ref[...]
            gx_ref[...] = dy_ref[...] + r * (dn - n * jnp.mean(dn * n, axis=-1, keepdims=True))
            cols = jnp.sum(dhv * n, axis=0, keepdims=True)

            @pl.when(i == 0)
            def _():
                dg_ref[...] = cols

            @pl.when(i > 0)
            def _():
                dg_ref[...] += cols

    row = pl.BlockSpec((tm, D_MODEL), lambda i, kk: (i, 0))
    vec = pl.BlockSpec((1, D_MODEL), lambda i, kk: (0, 0))
    return pl.pallas_call(
        body, name="dh_prenorm_bwd", grid=(SEQ // tm, nk),
        in_specs=[pl.BlockSpec((tm, tk), lambda i, kk: (i, kk)), pl.BlockSpec((tk, D_MODEL), lambda i, kk: (kk, 0)),
                  row, row, vec] + after_specs,
        out_specs=[row, vec],
        out_shape=[jax.ShapeDtypeStruct((SEQ, D_MODEL), F32), jax.ShapeDtypeStruct((1, D_MODEL), F32)],
        scratch_shapes=[pltpu.VMEM((tm, D_MODEL), F32)],
        compiler_params=_cparams(),
    )(dp, w_pad_t, x, dy, g, *after)


def _mla_prep_fwd(p, qg, kvg, wuq, wk, wv, rc, rs):
    tm = 512

    def body(lat_ref, qg_ref, kvg_ref, wuq_ref, wk_ref, wv_ref, c_ref, s_ref,
             cqn_ref, ckvn_ref, q_ref, k_ref, v_ref):
        c, s = c_ref[...], s_ref[...]
        lanes = _rope_lanes((tm, 128), MLA_ROPE_HALF, 128, 64)
        cq = lat_ref[:, 0:Q_RANK]
        r1 = lax.rsqrt(jnp.mean(cq * cq, axis=-1, keepdims=True) + EPS)
        cqn = (cq * r1 * qg_ref[...]).astype(BF16)
        cqn_ref[...] = cqn
        q = _nn(cqn, wuq_ref[...])
        for h in range(MLA_HEADS):
            sl = slice(h * 128, (h + 1) * 128)
            q_ref[:, sl] = (_rope_fwd(q[:, sl], c, s, MLA_ROPE_HALF, lanes) * MLA_SCALE).astype(BF16)
        ckv = lat_ref[:, Q_RANK:Q_RANK + KV_RANK]
        r2 = lax.rsqrt(jnp.mean(ckv * ckv, axis=-1, keepdims=True) + EPS)
        ckvn = (ckv * r2 * kvg_ref[...]).astype(BF16)
        ckvn_ref[...] = ckvn
        krr = _rope_fwd(lat_ref[:, Q_RANK + KV_RANK:N_LAT], c, s, MLA_ROPE_HALF, lanes)
        kn = _nn(ckvn, wk_ref[...])
        for h in range(MLA_HEADS):
            sl = slice(h * 128, (h + 1) * 128)
            k_ref[:, sl] = (kn[:, sl] + krr).astype(BF16)
        v_ref[...] = _nn(ckvn, wv_ref[...]).astype(BF16)

    def full(shape):
        return pl.BlockSpec(shape, lambda i: (0, 0))

    def rows(w):
        return pl.BlockSpec((tm, w), lambda i: (i, 0))

    return pl.pallas_call(
        body, name="mla_prep_fwd", grid=(SEQ // tm,),
        in_specs=[pl.BlockSpec((tm, N_LAT), lambda i: (i, COL_LAT // N_LAT)),
                  full((1, Q_RANK)), full((1, KV_RANK)), full((Q_RANK, 1024)), full((KV_RANK, 1024)),
                  full((KV_RANK, 512)), rows(128), rows(128)],
        out_specs=[rows(Q_RANK), rows(KV_RANK), rows(1024), rows(1024), rows(512)],
        out_shape=[jax.ShapeDtypeStruct((SEQ, Q_RANK), BF16), jax.ShapeDtypeStruct((SEQ, KV_RANK), BF16),
                   jax.ShapeDtypeStruct((SEQ, 1024), BF16), jax.ShapeDtypeStruct((SEQ, 1024), BF16),
                   jax.ShapeDtypeStruct((SEQ, 512), BF16)],
        compiler_params=_cparams(),
    )(p, qg, kvg, wuq, wk, wv, rc, rs)


def _mla_prep_bwd(dp_in, p, dq, dk, dv, qg, kvg, wuq, wk, wv, rc, rs):
    tm = 512

    def body(dp_any, lat_ref, dq_ref, dk_ref, dv_ref, qg_ref, kvg_ref, wuq_ref, wk_ref, wv_ref,
             c_ref, s_ref, dp_ref, dqb_ref, dkb_ref, dvb_ref, dgq_ref, dgkv_ref):
        del dp_any
        c, s = c_ref[...], s_ref[...]
        lanes = _rope_lanes((tm, 128), MLA_ROPE_HALF, 128, 64)
        lane = lax.broadcasted_iota(jnp.int32, (tm, 128), 1)
        dkr = jnp.zeros((tm, 128), F32)
        for h in range(MLA_HEADS):
            sl = slice(h * 128, (h + 1) * 128)
            dqb_ref[:, sl] = _rope_bwd(dq_ref[:, sl] * MLA_SCALE, c, s, MLA_ROPE_HALF, lanes).astype(BF16)
            dkh = dk_ref[:, sl]
            dkr = dkr + dkh
            dkb_ref[:, sl] = jnp.where(lane < 64, dkh, 0.0).astype(BF16)
        dkr = jnp.where((lane >= 64) & (lane < 96), dkr, 0.0)
        dkr = _rope_bwd(dkr, c, s, MLA_ROPE_HALF, lanes)
        dvb = dv_ref[...].astype(BF16)
        dvb_ref[...] = dvb

        cq = lat_ref[:, 0:Q_RANK]
        r1 = lax.rsqrt(jnp.mean(cq * cq, axis=-1, keepdims=True) + EPS)
        n1 = cq * r1
        dcqn = _nt(dqb_ref[...], wuq_ref[...])
        dn1 = dcqn * qg_ref[...]
        dcq = r1 * (dn1 - n1 * jnp.mean(dn1 * n1, axis=-1, keepdims=True))
        pq = jnp.sum(dcqn * n1, axis=0, keepdims=True)

        ckv = lat_ref[:, Q_RANK:Q_RANK + KV_RANK]
        r2 = lax.rsqrt(jnp.mean(ckv * ckv, axis=-1, keepdims=True) + EPS)
        n2 = ckv * r2
        dckvn = _nt(dkb_ref[...], wk_ref[...]) + _nt(dvb, wv_ref[...])
        dn2 = dckvn * kvg_ref[...]
        dckv = r2 * (dn2 - n2 * jnp.mean(dn2 * n2, axis=-1, keepdims=True))
        pkv = jnp.sum(dckvn * n2, axis=0, keepdims=True)

        dp_ref[:, 0:Q_RANK] = dcq.astype(BF16)
        dp_ref[:, Q_RANK:Q_RANK + KV_RANK] = dckv.astype(BF16)
        dp_ref[:, Q_RANK + KV_RANK:N_LAT] = dkr.astype(BF16)

        @pl.when(pl.program_id(0) == 0)
        def _():
            dgq_ref[...] = pq
            dgkv_ref[...] = pkv

        @pl.when(pl.program_id(0) > 0)
        def _():
            dgq_ref[...] += pq
            dgkv_ref[...] += pkv

    def full(shape):
        return pl.BlockSpec(shape, lambda i: (0, 0))

    def rows(w):
        return pl.BlockSpec((tm, w), lambda i: (i, 0))

    lat = pl.BlockSpec((tm, N_LAT), lambda i: (i, COL_LAT // N_LAT))
    return pl.pallas_call(
        body, name="mla_prep_bwd", grid=(SEQ // tm,),
        in_specs=[pl.BlockSpec(memory_space=pl.ANY), lat, rows(1024), rows(1024), rows(512),
                  full((1, Q_RANK)), full((1, KV_RANK)), full((Q_RANK, 1024)), full((KV_RANK, 1024)),
                  full((KV_RANK, 512)), rows(128), rows(128)],
        out_specs=[lat, rows(1024), rows(1024), rows(512), full((1, Q_RANK)), full((1, KV_RANK))],
        out_shape=[jax.ShapeDtypeStruct((SEQ, N_PAD), BF16), jax.ShapeDtypeStruct((SEQ, 1024), BF16),
                   jax.ShapeDtypeStruct((SEQ, 1024), BF16), jax.ShapeDtypeStruct((SEQ, 512), BF16),
                   jax.ShapeDtypeStruct((1, Q_RANK), F32), jax.ShapeDtypeStruct((1, KV_RANK), F32)],
        input_output_aliases={0: 0},
        compiler_params=_cparams(),
    )(dp_in, p, dq, dk, dv, qg, kvg, wuq, wk, wv, rc, rs)


FLASH_T = 1024


def _head_half(shape, hh):
    lane = lax.broadcasted_iota(jnp.int32, shape, 1)
    return (lane < 64) if hh == 0 else (lane >= 64)


def _diag_keep(nr, nk):
    row = lax.broadcasted_iota(jnp.int32, (nr, nk), 0)
    col = lax.broadcasted_iota(jnp.int32, (nr, nk), 1)
    return row + (nk - nr) >= col


def _tri_steps(nb, q_major):
    if q_major:
        pairs = [(i, kb) for i in range(nb) for kb in range(i + 1)]
    else:
        pairs = [(i, kb) for kb in range(nb) for i in range(kb, nb)]
    return jnp.asarray([p[0] for p in pairs], jnp.int32), jnp.asarray([p[1] for p in pairs], jnp.int32)


def _mla_flash_fwd(q, k, v):
    t = FLASH_T
    nb = SEQ // t
    qtab, ktab = _tri_steps(nb, True)

    def body(qi_ref, ki_ref, q_ref, k_ref, v_ref, o_ref, lse_ref, m_scr, l_scr, acc_scr):
        step = pl.program_id(1)
        i, kb = qi_ref[step], ki_ref[step]

        @pl.when(kb == 0)
        def _():
            m_scr[...] = jnp.full_like(m_scr, NEG)
            l_scr[...] = jnp.zeros_like(l_scr)
            acc_scr[...] = jnp.zeros_like(acc_scr)

        def update(r0, nr, nk, diagonal):
            rs = slice(r0, r0 + nr)
            vv = v_ref[0:nk, :]
            for hh in range(2):
                sl = slice(hh * 128, (hh + 1) * 128)
                s = _nt(q_ref[rs, sl], k_ref[0:nk, sl])
                if diagonal:
                    s = jnp.where(_diag_keep(nr, nk), s, NEG)
                m_prev = m_scr[hh, rs, :]
                m_new = jnp.maximum(m_prev, jnp.max(s, axis=-1, keepdims=True))
                pr = jnp.exp(s - jnp.tile(m_new, (1, nk // 128)))
                alpha = jnp.exp(m_prev - m_new)
                l_scr[hh, rs, :] = alpha * l_scr[hh, rs, :] + jnp.sum(pr, axis=-1, keepdims=True)
                acc_scr[hh, rs, :] = alpha * acc_scr[hh, rs, :] + _nn(pr.astype(BF16), vv)
                m_scr[hh, rs, :] = m_new

        @pl.when(kb < i)
        def _():
            update(0, t, t, False)

        @pl.when(kb == i)
        def _():
            update(0, t // 2, t // 2, True)
            update(t // 2, t // 2, t, True)
            o0 = acc_scr[0] / l_scr[0]
            o1 = acc_scr[1] / l_scr[1]
            o_ref[...] = jnp.where(_head_half((t, 128), 0), o0, o1)
            for hh in range(2):
                lse_ref[:, hh * 128:(hh + 1) * 128] = m_scr[hh] + jnp.log(l_scr[hh])

    grid_spec = pltpu.PrefetchScalarGridSpec(
        num_scalar_prefetch=2, grid=(4, qtab.shape[0]),
        in_specs=[pl.BlockSpec((t, 256), lambda j, s, qi, ki: (qi[s], j)),
                  pl.BlockSpec((t, 256), lambda j, s, qi, ki: (ki[s], j)),
                  pl.BlockSpec((t, 128), lambda j, s, qi, ki: (ki[s], j))],
        out_specs=[pl.BlockSpec((t, 128), lambda j, s, qi, ki: (qi[s], j)),
                   pl.BlockSpec((t, 256), lambda j, s, qi, ki: (qi[s], j))],
        scratch_shapes=[pltpu.VMEM((2, t, 128), F32), pltpu.VMEM((2, t, 128), F32), pltpu.VMEM((2, t, 128), F32)])
    return pl.pallas_call(
        body, name="mla_flash_fwd", grid_spec=grid_spec,
        out_shape=[jax.ShapeDtypeStruct((SEQ, 512), F32), jax.ShapeDtypeStruct((SEQ, 1024), F32)],
        compiler_params=_cparams(),
    )(qtab, ktab, q, k, v)


def _mla_flash_bwd(q, k, v, o, do, lse, token=None):
    t = FLASH_T
    nb = SEQ // t
    qtab, ktab = _tri_steps(nb, False)
    after, after_specs = _after(token)

    def body(qi_ref, ki_ref, q_ref, k_ref, v_ref, o_ref, do_ref, lse_ref, *rest):
        dq_ref, dk_ref, dv_ref, dk_scr, dv_scr = rest[-5:]
        step = pl.program_id(1)
        i, kb = qi_ref[step], ki_ref[step]

        @pl.when(step == 0)
        def _():
            dq_ref[...] = jnp.zeros_like(dq_ref)

        @pl.when(i == kb)
        def _():
            dk_scr[...] = jnp.zeros_like(dk_scr)
            dv_scr[...] = jnp.zeros_like(dv_scr)

        def update(r0, nr, nk, diagonal):
            rs = slice(r0, r0 + nr)
            vv = v_ref[0:nk, :]
            ov = o_ref[rs, :]
            dov = do_ref[rs, :]
            rows = pl.ds(pl.multiple_of(i * t + r0, t // 2), nr)
            for hh in range(2):
                sl = slice(hh * 128, (hh + 1) * 128)
                qh, kh = q_ref[rs, sl], k_ref[0:nk, sl]
                s = _nt(qh, kh)
                if diagonal:
                    s = jnp.where(_diag_keep(nr, nk), s, NEG)
                pr = jnp.exp(s - jnp.tile(lse_ref[rs, sl], (1, nk // 128)))
                dom = jnp.where(_head_half((nr, 128), hh), dov, 0.0)
                domb = dom.astype(BF16)
                dv_scr[0:nk, :] += _tn(pr.astype(BF16), domb)
                dpr = _nt(domb, vv)
                delta = jnp.sum(dom * ov, axis=-1, keepdims=True)
                ds = (pr * (dpr - delta)).astype(BF16)
                dq_ref[rows, sl] += _nn(ds, kh)
                dk_scr[hh, 0:nk, :] += _tn(ds, qh)

        @pl.when(i > kb)
        def _():
            update(0, t, t, False)

        @pl.when(i == kb)
        def _():
            update(0, t // 2, t // 2, True)
            update(t // 2, t // 2, t, True)

        @pl.when(i == nb - 1)
        def _():
            dk_ref[:, 0:128] = dk_scr[0]
            dk_ref[:, 128:256] = dk_scr[1]
            dv_ref[...] = dv_scr[...]

    qi_map = lambda j, s, qi, ki: (qi[s], j)
    ki_map = lambda j, s, qi, ki: (ki[s], j)
    grid_spec = pltpu.PrefetchScalarGridSpec(
        num_scalar_prefetch=2, grid=(4, qtab.shape[0]),
        in_specs=[pl.BlockSpec((t, 256), qi_map), pl.BlockSpec((t, 256), ki_map), pl.BlockSpec((t, 128), ki_map),
                  pl.BlockSpec((t, 128), qi_map), pl.BlockSpec((t, 128), qi_map), pl.BlockSpec((t, 256), qi_map)]
        + after_specs,
        out_specs=[pl.BlockSpec((SEQ, 256), lambda j, s, qi, ki: (0, j)), pl.BlockSpec((t, 256), ki_map),
                   pl.BlockSpec((t, 128), ki_map)],
        scratch_shapes=[pltpu.VMEM((2, t, 128), F32), pltpu.VMEM((t, 128), F32)])
    return pl.pallas_call(
        body, name="mla_flash_bwd", grid_spec=grid_spec,
        out_shape=[jax.ShapeDtypeStruct((SEQ, 1024), F32), jax.ShapeDtypeStruct((SEQ, 1024), F32),
                   jax.ShapeDtypeStruct((SEQ, 512), F32)],
        compiler_params=_cparams(),
    )(qtab, ktab, q, k, v, o, do, lse, *after)


DIL_UNROLL = 4


def _strided(start, size, d):
    return pl.ds(start, size) if d == 1 else pl.ds(start, size, stride=d)


def _dil_prep_fwd(p, rc, rs, g):
    d = DIL_DILATIONS[g]
    sub_len = SEQ // d
    ch = min(sub_len, 512)

    def body(p_ref, c_ref, s_ref, o_ref, x_scr):
        tq = pl.program_id(0)
        lanes = _rope_lanes((ch, 128), DIL_ROPE_HALF, 64, 0)
        o_ref[0, 0:BAND, :] = jnp.zeros((BAND, 128), BF16)

        @pl.when(tq < 2)
        def _():
            mult = jnp.where(tq == 0, DIL_SCALE, 1.0).astype(F32)
            for c0 in range(0, SEQ, ch):
                rows = pl.ds(c0, ch)
                x_scr[rows, :] = _rope_fwd(p_ref[rows, :], c_ref[rows, :] * mult, s_ref[rows, :] * mult, DIL_ROPE_HALF, lanes)

        def gather(src):
            for r in range(d):
                for c0 in range(0, sub_len, ch):
                    at = BAND + r * sub_len + c0
                    o_ref[0, at:at + ch, :] = src[_strided(r + c0 * d, ch, d), :].astype(BF16)

        @pl.when(tq < 2)
        def _():
            gather(x_scr)

        @pl.when(tq == 2)
        def _():
            gather(p_ref)

    tab = pl.BlockSpec((SEQ, 128), lambda tq, pr: (0, 0))
    return pl.pallas_call(
        body, name=f"dil_prep_fwd_g{g}", grid=(3, 4),
        in_specs=[pl.BlockSpec((SEQ, 128), lambda tq, pr: (0, COL_QKV // 128 + (tq * 3 + g) * 4 + pr)), tab, tab],
        out_specs=pl.BlockSpec((1, BAND + SEQ, 128), lambda tq, pr: (tq, 0, pr)),
        out_shape=jax.ShapeDtypeStruct((3, BAND + SEQ, 512), BF16),
        scratch_shapes=[pltpu.VMEM((SEQ, 128), F32)],
        compiler_params=_cparams(),
    )(p, rc, rs)


DIL_ST_FWD, DIL_ST_BWD = 1024, 2048


def _band_keep(g, b, t, nb):
    nbs = SEQ // DIL_DILATIONS[g] // BAND
    row = lax.broadcasted_iota(jnp.int32, (BAND, 2 * BAND), 0)
    col = lax.broadcasted_iota(jnp.int32, (BAND, 2 * BAND), 1)
    cur = (col >= BAND) & (row >= col - BAND)
    prev = (col < BAND) & (col >= row)
    if nbs >= nb:
        if b > 0:
            return cur | prev
        return cur | (prev & ((t * nb) % nbs != 0))
    return cur | prev if b % nbs else cur


def _dil_tok(g, b, t, nb):
    d = DIL_DILATIONS[g]
    nbs = SEQ // d // BAND
    gb = t * nb + b
    return _strided((gb % nbs) * BAND * d + gb // nbs, BAND, d)


def _dil_attn_fwd2(qkv, g):
    DIL_ST, DIL_NB = DIL_ST_FWD, DIL_ST_FWD // BAND

    def body(q_ref, k_ref, v_ref, o_ref, l_ref, s_scr, p_scr, o_scr):
        t = pl.program_id(1)
        base = t * DIL_ST
        half0 = _head_half((DIL_ST, 128), 0)
        lse_h = []
        for hh in range(2):
            half = _head_half((BAND, 128), hh)
            for b in range(DIL_NB):
                qv = q_ref[0, pl.ds(pl.multiple_of(base + (b + 1) * BAND, BAND), BAND), :]
                k2 = k_ref[0, pl.ds(pl.multiple_of(base + b * BAND, BAND), 2 * BAND), :]
                sb = _nt(jnp.where(half, qv, jnp.zeros_like(qv)), k2)
                s_scr[b * BAND:(b + 1) * BAND, :] = jnp.where(_band_keep(g, b, t, DIL_NB), sb, NEG)
            s = s_scr[...]
            m = jnp.max(s, axis=-1, keepdims=True)
            pr = jnp.exp(s - m)
            den = jnp.sum(pr, axis=-1, keepdims=True)
            p_scr[...] = pr.astype(BF16)
            for b in range(DIL_NB):
                v2 = v_ref[0, pl.ds(pl.multiple_of(base + b * BAND, BAND), 2 * BAND), :]
                o_scr[hh, b * BAND:(b + 1) * BAND, :] = _nn(p_scr[b * BAND:(b + 1) * BAND, :], v2)
            o_scr[hh] = o_scr[hh] / den
            lse_h.append(m + jnp.log(den))
        out = jnp.where(half0, o_scr[0], o_scr[1])
        lse = jnp.where(half0, lse_h[0], lse_h[1])
        for b in range(DIL_NB):
            tok = _dil_tok(g, b, t, DIL_NB)
            o_ref[tok, :] = out[b * BAND:(b + 1) * BAND, :]
            l_ref[tok, :] = lse[b * BAND:(b + 1) * BAND, :]

    def inp(tq):
        return pl.BlockSpec((1, BAND + SEQ, 128), lambda pr, t: (tq, 0, pr))

    out = pl.BlockSpec((SEQ, 128), lambda pr, t: (0, pr))
    return pl.pallas_call(
        body, name=f"dil_attn_fwd_g{g}", grid=(4, SEQ // DIL_ST),
        in_specs=[inp(0), inp(1), inp(2)], out_specs=[out, out],
        out_shape=[jax.ShapeDtypeStruct((SEQ, 512), F32), jax.ShapeDtypeStruct((SEQ, 512), F32)],
        scratch_shapes=[pltpu.VMEM((DIL_ST, 2 * BAND), F32), pltpu.VMEM((DIL_ST, 2 * BAND), BF16),
                        pltpu.VMEM((2, DIL_ST, 128), F32)],
        compiler_params=_cparams(),
    )(qkv, qkv, qkv)


def _dil_attn_bwd2(qkv, dyd, yd, lse_all, g, token=None):
    d = DIL_DILATIONS[g]
    sub_len = SEQ // d
    DIL_ST, DIL_NB = DIL_ST_BWD, DIL_ST_BWD // BAND
    nst = SEQ // DIL_ST
    after, after_specs = _after(token)

    def body(q_ref, k_ref, v_ref, do_ref, y_ref, l_ref, *rest):
        out_ref, dk_scr, dv_scr, s_scr, dp_scr, p_scr, ds_scr, do_scr, y_scr, l_scr, dq_scr = rest[-11:]
        t = pl.program_id(1)
        base = t * DIL_ST

        @pl.when(t == 0)
        def _():
            dk_scr[...] = jnp.zeros_like(dk_scr)
            dv_scr[...] = jnp.zeros_like(dv_scr)

        for b in range(DIL_NB):
            tok = _dil_tok(g, b, t, DIL_NB)
            do_scr[b * BAND:(b + 1) * BAND, :] = do_ref[tok, :]
            y_scr[b * BAND:(b + 1) * BAND, :] = y_ref[tok, :]
            l_scr[b * BAND:(b + 1) * BAND, :] = l_ref[tok, :]
        for hh in range(2):
            half = _head_half((BAND, 128), hh)
            half_st = _head_half((DIL_ST, 128), hh)
            dom = jnp.where(half_st, do_scr[...], 0.0)
            delta = jnp.sum(dom * y_scr[...], axis=-1, keepdims=True)
            lcol = jnp.max(jnp.where(half_st, l_scr[...], NEG), axis=-1, keepdims=True)
            for b in range(DIL_NB):
                rows = slice(b * BAND, (b + 1) * BAND)
                qv = q_ref[0, pl.ds(pl.multiple_of(base + (b + 1) * BAND, BAND), BAND), :]
                band = pl.ds(pl.multiple_of(base + b * BAND, BAND), 2 * BAND)
                sb = _nt(jnp.where(half, qv, jnp.zeros_like(qv)), k_ref[0, band, :])
                s_scr[rows, :] = jnp.where(_band_keep(g, b, t, DIL_NB), sb, NEG)
                dp_scr[rows, :] = _nt(dom[rows, :].astype(BF16), v_ref[0, band, :])
            pr = jnp.exp(s_scr[...] - lcol)
            p_scr[...] = pr.astype(BF16)
            ds_scr[...] = (pr * (dp_scr[...] - delta)).astype(BF16)
            for b in range(DIL_NB):
                rows = slice(b * BAND, (b + 1) * BAND)
                qv = q_ref[0, pl.ds(pl.multiple_of(base + (b + 1) * BAND, BAND), BAND), :]
                band = pl.ds(pl.multiple_of(base + b * BAND, BAND), 2 * BAND)
                dqb = jnp.where(half, _nn(ds_scr[rows, :], k_ref[0, band, :]), 0.0)
                if hh == 0:
                    dq_scr[rows, :] = dqb
                else:
                    dq_scr[rows, :] += dqb
                half2 = _head_half((2 * BAND, 128), hh)
                dk_scr[band, :] += jnp.where(half2, _tn(ds_scr[rows, :], qv), 0.0)
                dv_scr[band, :] += _tn(p_scr[rows, :], dom[rows, :].astype(BF16))
        for b in range(DIL_NB):
            out_ref[pl.ds(0, 1), _dil_tok(g, b, t, DIL_NB), :] = dq_scr[b * BAND:(b + 1) * BAND, :][None]

        @pl.when(t == nst - 1)
        def _():
            for r in range(d):
                rows = _strided(r, sub_len, d)
                out_ref[pl.ds(1, 1), rows, :] = dk_scr[BAND + r * sub_len:BAND + (r + 1) * sub_len, :][None]
                out_ref[pl.ds(2, 1), rows, :] = dv_scr[BAND + r * sub_len:BAND + (r + 1) * sub_len, :][None]

    def inp(tq):
        return pl.BlockSpec((1, BAND + SEQ, 128), lambda pr, t: (tq, 0, pr))

    tok_spec = pl.BlockSpec((SEQ, 128), lambda pr, t: (0, pr))
    st = (DIL_ST, 2 * BAND)
    return pl.pallas_call(
        body, name=f"dil_attn_bwd_g{g}", grid=(4, nst),
        in_specs=[inp(0), inp(1), inp(2), tok_spec, tok_spec, tok_spec] + after_specs,
        out_specs=pl.BlockSpec((3, SEQ, 128), lambda pr, t: (0, 0, pr)),
        out_shape=jax.ShapeDtypeStruct((3, SEQ, 512), F32),
        scratch_shapes=[pltpu.VMEM((BAND + SEQ, 128), F32), pltpu.VMEM((BAND + SEQ, 128), F32),
                        pltpu.VMEM(st, F32), pltpu.VMEM(st, F32), pltpu.VMEM(st, BF16), pltpu.VMEM(st, BF16),
                        pltpu.VMEM((DIL_ST, 128), F32), pltpu.VMEM((DIL_ST, 128), F32), pltpu.VMEM((DIL_ST, 128), F32),
                        pltpu.VMEM((DIL_ST, 128), F32)],
        compiler_params=_cparams(),
    )(qkv, qkv, qkv, dyd, yd, lse_all, *after)


def _band_masks():
    row = lax.broadcasted_iota(jnp.int32, (BAND, BAND), 0)
    col = lax.broadcasted_iota(jnp.int32, (BAND, BAND), 1)
    return row >= col, col >= row


def _dil_attn_fwd(qkv, g):
    d = DIL_DILATIONS[g]
    nbs = SEQ // d // BAND
    nblk = SEQ // BAND

    def body(q_ref, k_ref, v_ref, o_ref, l_ref):
        keep_c, keep_p = _band_masks()
        half0 = _head_half((BAND, 128), 0)

        def step(i, carry):
            cur = pl.ds(pl.multiple_of(i * BAND, BAND), BAND)
            prv = pl.ds(pl.multiple_of(jnp.maximum(i - 1, 0) * BAND, BAND), BAND)
            has_prev = (i % nbs) != 0
            qv = q_ref[0, cur, :]
            kc, kp = k_ref[0, cur, :], k_ref[0, prv, :]
            vc, vp = v_ref[0, cur, :], v_ref[0, prv, :]
            outs, lses = [], []
            for hh in range(2):
                qm = jnp.where(_head_half((BAND, 128), hh), qv, jnp.zeros_like(qv))
                sc = jnp.where(keep_c, _nt(qm, kc), NEG)
                sp = jnp.where(keep_p & has_prev, _nt(qm, kp), NEG)
                m = jnp.maximum(jnp.max(sc, axis=-1, keepdims=True), jnp.max(sp, axis=-1, keepdims=True))
                pc, pp = jnp.exp(sc - m), jnp.exp(sp - m)
                den = jnp.sum(pc, axis=-1, keepdims=True) + jnp.sum(pp, axis=-1, keepdims=True)
                o = (_nn(pc.astype(BF16), vc) + _nn(pp.astype(BF16), vp)) / den
                outs.append(o)
                lses.append(jnp.broadcast_to(m + jnp.log(den), (BAND, 128)))
            tok = _strided((i % nbs) * BAND * d + i // nbs, BAND, d)
            o_ref[tok, :] = jnp.where(half0, outs[0], outs[1])
            l_ref[tok, :] = jnp.where(half0, lses[0], lses[1])
            return carry

        lax.fori_loop(0, nblk, step, 0, unroll=DIL_UNROLL)

    def inp(tq):
        return pl.BlockSpec((1, SEQ, 128), lambda pr: (tq, 0, pr))

    out = pl.BlockSpec((SEQ, 128), lambda pr: (0, pr))
    return pl.pallas_call(
        body, name=f"dil_attn_fwd_g{g}", grid=(4,),
        in_specs=[inp(0), inp(1), inp(2)], out_specs=[out, out],
        out_shape=[jax.ShapeDtypeStruct((SEQ, 512), F32), jax.ShapeDtypeStruct((SEQ, 512), F32)],
        compiler_params=_cparams(),
    )(qkv, qkv, qkv)


def _dil_attn_bwd(qkv, dyd, yd, lse_all, g):
    d = DIL_DILATIONS[g]
    sub_len = SEQ // d
    nbs = sub_len // BAND
    nblk = SEQ // BAND

    def body(q_ref, k_ref, v_ref, do_ref, y_ref, l_ref, out_ref, dk_scr, dv_scr):
        keep_c, keep_p = _band_masks()
        dk_scr[...] = jnp.zeros_like(dk_scr)
        dv_scr[...] = jnp.zeros_like(dv_scr)

        def step(i, carry):
            cur = pl.ds(pl.multiple_of(i * BAND, BAND), BAND)
            prv = pl.ds(pl.multiple_of(jnp.maximum(i - 1, 0) * BAND, BAND), BAND)
            has_prev = (i % nbs) != 0
            tok = _strided((i % nbs) * BAND * d + i // nbs, BAND, d)
            qv = q_ref[0, cur, :]
            kc, kp = k_ref[0, cur, :], k_ref[0, prv, :]
            vc, vp = v_ref[0, cur, :], v_ref[0, prv, :]
            dov, yv, lv = do_ref[tok, :], y_ref[tok, :], l_ref[tok, :]
            dq = jnp.zeros((BAND, 128), F32)
            dkc = jnp.zeros((BAND, 128), F32)
            dkp = jnp.zeros((BAND, 128), F32)
            dvc = jnp.zeros((BAND, 128), F32)
            dvp = jnp.zeros((BAND, 128), F32)
            for hh in range(2):
                half = _head_half((BAND, 128), hh)
                qm = jnp.where(half, qv, jnp.zeros_like(qv))
                lcol = jnp.max(jnp.where(half, lv, NEG), axis=-1, keepdims=True)
                pc = jnp.exp(jnp.where(keep_c, _nt(qm, kc), NEG) - lcol)
                pp = jnp.exp(jnp.where(keep_p & has_prev, _nt(qm, kp), NEG) - lcol)
                dom = jnp.where(half, dov, 0.0)
                domb = dom.astype(BF16)
                delta = jnp.sum(dom * yv, axis=-1, keepdims=True)
                dsc = (pc * (_nt(domb, vc) - delta)).astype(BF16)
                dsp = (pp * (_nt(domb, vp) - delta)).astype(BF16)
                dvc = dvc + _tn(pc.astype(BF16), domb)
                dvp = dvp + _tn(pp.astype(BF16), domb)
                dq = dq + jnp.where(half, _nn(dsc, kc) + _nn(dsp, kp), 0.0)
                dkc = dkc + jnp.where(half, _tn(dsc, qv), 0.0)
                dkp = dkp + jnp.where(half, _tn(dsp, qv), 0.0)
            out_ref[pl.ds(0, 1), tok, :] = dq[None]
            dk_scr[cur, :] += dkc
            dk_scr[prv, :] += dkp
            dv_scr[cur, :] += dvc
            dv_scr[prv, :] += dvp
            return carry

        lax.fori_loop(0, nblk, step, 0, unroll=DIL_UNROLL)
        for r in range(d):
            rows = _strided(r, sub_len, d)
            out_ref[pl.ds(1, 1), rows, :] = dk_scr[r * sub_len:(r + 1) * sub_len, :][None]
            out_ref[pl.ds(2, 1), rows, :] = dv_scr[r * sub_len:(r + 1) * sub_len, :][None]

    def inp(tq):
        return pl.BlockSpec((1, SEQ, 128), lambda pr: (tq, 0, pr))

    tok_spec = pl.BlockSpec((SEQ, 128), lambda pr: (0, pr))
    return pl.pallas_call(
        body, name=f"dil_attn_bwd_g{g}", grid=(4,),
        in_specs=[inp(0), inp(1), inp(2), tok_spec, tok_spec, tok_spec],
        out_specs=pl.BlockSpec((3, SEQ, 128), lambda pr: (0, 0, pr)),
        out_shape=jax.ShapeDtypeStruct((3, SEQ, 512), F32),
        scratch_shapes=[pltpu.VMEM((SEQ, 128), F32), pltpu.VMEM((SEQ, 128), F32)],
        compiler_params=_cparams(),
    )(qkv, qkv, qkv, dyd, yd, lse_all)


def _dil_prep_bwd(dp_in, dqkv, rc, rs, g):
    tm = 1024

    def body(dp_any, g_ref, c_ref, s_ref, dp_ref):
        del dp_any
        tq = pl.program_id(0)

        @pl.when(tq == 2)
        def _():
            dp_ref[...] = g_ref[0].astype(BF16)

        @pl.when(tq < 2)
        def _():
            mult = jnp.where(tq == 0, DIL_SCALE, 1.0).astype(F32)
            lanes = _rope_lanes((tm, 128), DIL_ROPE_HALF, 64, 0)
            cv, sv = c_ref[...], s_ref[...] * mult
            cv = cv * mult
            for pr in range(4):
                gv = g_ref[0, :, pr * 128:(pr + 1) * 128]
                dp_ref[:, pr * 128:(pr + 1) * 128] = _rope_bwd(gv, cv, sv, DIL_ROPE_HALF, lanes).astype(BF16)

    tab = pl.BlockSpec((tm, 128), lambda tq, i: (i, 0))
    return pl.pallas_call(
        body, name=f"dil_prep_bwd_g{g}", grid=(3, SEQ // tm),
        in_specs=[pl.BlockSpec(memory_space=pl.ANY),
                  pl.BlockSpec((1, tm, 512), lambda tq, i: (tq, i, 0)), tab, tab],
        out_specs=pl.BlockSpec((tm, 512), lambda tq, i: (i, COL_QKV // 512 + tq * 3 + g)),
        out_shape=jax.ShapeDtypeStruct((SEQ, N_PAD), BF16),
        input_output_aliases={0: 0},
    )(dp_in, dqkv, rc, rs)


TAIL_T = 256


def _tail(p, ya, o_g, l_g, x, target, wpm, wpd, wout, post_g):
    tm = TAIL_T

    def body(pgz_ref, ya_ref, o0_ref, o1_ref, o2_ref, l0_ref, l1_ref, l2_ref, x_ref, t_ref,
             wpm_ref, wpd_ref, wout_ref, pg_ref,
             dp_ref, dy_ref, mg_ref, dt_ref, ua_ref, dpa_ref, ud_ref, dpd_ref, dya_ref, dyd_ref,
             yd_ref, lse_ref, loss_ref, dgp_ref):
        l0, l1, l2 = l0_ref[...], l1_ref[...], l2_ref[...]
        mx = jnp.maximum(jnp.maximum(l0, l1), l2)
        e0, e1, e2 = jnp.exp(l0 - mx), jnp.exp(l1 - mx), jnp.exp(l2 - mx)
        den = e0 + e1 + e2
        yd = (e0 * o0_ref[...] + e1 * o1_ref[...] + e2 * o2_ref[...]) / den
        yd_ref[...] = yd
        lse_ref[...] = mx + jnp.log(den)
        ya = ya_ref[...]

        gm, gd = pgz_ref[:, 0:1024], pgz_ref[:, 1024:2048]
        zm, zd = pgz_ref[:, 2048:2560], pgz_ref[:, 2560:3072]
        szm, szd = _sigmoid(zm), _sigmoid(zd)
        sm, sd = zm * szm, zd * szd
        ua = (ya * sm).astype(BF16)
        ud = (yd * sd).astype(BF16)
        ua_ref[...] = ua
        ud_ref[...] = ud
        pa = _nn(ua, wpm_ref[...])
        pd = _nn(ud, wpd_ref[...])
        sgm, sgd = _sigmoid(gm), _sigmoid(gd)
        mg = (sgm * pa + sgd * pd).astype(BF16)
        mg_ref[...] = mg
        t = _nn(mg, wout_ref[...])
        r3 = lax.rsqrt(jnp.mean(t * t, axis=-1, keepdims=True) + EPS)
        n = t * r3
        pg = pg_ref[...]
        err = x_ref[...] + n * pg - t_ref[...]
        lpart = jnp.sum(err * err, axis=0, keepdims=True)

        dy = err * (1.0 / D_MODEL)
        dy_ref[...] = dy
        gpart = jnp.sum(dy * n, axis=0, keepdims=True)
        dn = dy * pg
        dt = (r3 * (dn - n * jnp.mean(dn * n, axis=-1, keepdims=True))).astype(BF16)
        dt_ref[...] = dt
        dmg = _nt(dt, wout_ref[...])
        dpa = (dmg * sgm).astype(BF16)
        dpd = (dmg * sgd).astype(BF16)
        dpa_ref[...] = dpa
        dpd_ref[...] = dpd
        dp_ref[:, 0:1024] = (dmg * pa * sgm * (1.0 - sgm)).astype(BF16)
        dp_ref[:, 1024:2048] = (dmg * pd * sgd * (1.0 - sgd)).astype(BF16)
        dua = _nt(dpa, wpm_ref[...])
        dud = _nt(dpd, wpd_ref[...])
        dya_ref[...] = dua * sm
        dyd_ref[...] = dud * sd
        dp_ref[:, 2048:2560] = (dua * ya * szm * (1.0 + zm * (1.0 - szm))).astype(BF16)
        dp_ref[:, 2560:3072] = (dud * yd * szd * (1.0 + zd * (1.0 - szd))).astype(BF16)

        @pl.when(pl.program_id(0) == 0)
        def _():
            loss_ref[...] = lpart
            dgp_ref[...] = gpart

        @pl.when(pl.program_id(0) > 0)
        def _():
            loss_ref[...] += lpart
            dgp_ref[...] += gpart

    def rows(w):
        return pl.BlockSpec((tm, w), lambda i: (i, 0))

    def full(shape):
        return pl.BlockSpec(shape, lambda i: (0, 0))

    def sds(w, dt):
        return jax.ShapeDtypeStruct((SEQ, w), dt)

    return pl.pallas_call(
        body, name="tail", grid=(SEQ // tm,),
        in_specs=[rows(3072), rows(512), rows(512), rows(512), rows(512), rows(512), rows(512), rows(512),
                  rows(1024), rows(1024), full((512, 1024)), full((512, 1024)), full((1024, 1024)), full((1, 1024))],
        out_specs=[rows(3072), rows(1024), rows(1024), rows(1024), rows(512), rows(1024), rows(512), rows(1024),
                   rows(512), rows(512), rows(512), rows(512), full((1, 1024)), full((1, 1024))],
        out_shape=[sds(N_PAD, BF16), sds(1024, F32), sds(1024, BF16), sds(1024, BF16), sds(512, BF16),
                   sds(1024, BF16), sds(512, BF16), sds(1024, BF16), sds(512, F32), sds(512, F32),
                   sds(512, F32), sds(512, F32),
                   jax.ShapeDtypeStruct((1, 1024), F32), jax.ShapeDtypeStruct((1, 1024), F32)],
        compiler_params=_cparams(),
    )(p, ya, o_g[0], o_g[1], o_g[2], l_g[0], l_g[1], l_g[2], x, target, wpm, wpd, wout, post_g)


def _sum_parts(recv, own, me, tr, name):
    n, r, w = recv.shape
    if r % tr:
        return _sum_parts_cols(recv, own, me, name)
    own_spec = (pl.BlockSpec((tr, w), lambda i, me_ref: (i, 0)) if own.ndim == 2
                else pl.BlockSpec((None, tr, w), lambda i, me_ref: (me_ref[0], i, 0)))

    def body(me_ref, p_ref, own_ref, o_ref):
        mine = own_ref[...].astype(F32)
        acc = jnp.zeros((tr, w), F32)
        for s in range(n):
            acc = acc + jnp.where(me_ref[0] == s, mine, p_ref[s].astype(F32))
        o_ref[...] = acc

    return pl.pallas_call(
        body, name=name,
        grid_spec=pltpu.PrefetchScalarGridSpec(
            num_scalar_prefetch=1, grid=(r // tr,),
            in_specs=[pl.BlockSpec((n, tr, w), lambda i, me_ref: (0, i, 0)), own_spec],
            out_specs=pl.BlockSpec((tr, w), lambda i, me_ref: (i, 0))),
        out_shape=jax.ShapeDtypeStruct((r, w), F32),
    )(me.reshape(1), recv, own)


def _sum_parts_cols(recv, own, me, name):
    n, r, w = recv.shape
    tc = 128

    def body(me_ref, p_ref, own_ref, o_ref):
        mine = own_ref[...].astype(F32)
        acc = jnp.zeros((r, tc), F32)
        for s in range(n):
            acc = acc + jnp.where(me_ref[0] == s, mine, p_ref[s].astype(F32))
        o_ref[...] = acc

    return pl.pallas_call(
        body, name=name,
        grid_spec=pltpu.PrefetchScalarGridSpec(
            num_scalar_prefetch=1, grid=(w // tc,),
            in_specs=[pl.BlockSpec((n, r, tc), lambda i, me_ref: (0, 0, i)),
                      pl.BlockSpec((None, r, tc), lambda i, me_ref: (me_ref[0], 0, i))],
            out_specs=pl.BlockSpec((r, tc), lambda i, me_ref: (0, i))),
        out_shape=jax.ShapeDtypeStruct((r, w), F32),
    )(me.reshape(1), recv, own)


def _adamw(w, g, m, v, name):
    lead = w.shape[:-2]
    r, c = w.shape[-2:]
    tr = max([t for t in range(8, 257, 8) if r % t == 0], default=r)
    c1 = 1.0 - ADAM_B1 ** ADAM_STEP
    c2 = 1.0 - ADAM_B2 ** ADAM_STEP

    def body(w_ref, g_ref, m_ref, v_ref, d_ref, nm_ref, nv_ref):
        gv = g_ref[...]
        nm = ADAM_B1 * m_ref[...] + (1.0 - ADAM_B1) * gv
        nv = ADAM_B2 * v_ref[...] + (1.0 - ADAM_B2) * (gv * gv)
        nm_ref[...] = nm
        nv_ref[...] = nv
        d_ref[...] = -ADAM_LR * ((nm / c1) / (jnp.sqrt(nv / c2) + ADAM_EPS) + ADAM_WD * w_ref[...])

    zeros = (0,) * len(lead)
    spec = pl.BlockSpec((1,) * len(lead) + (tr, c), lambda i: zeros + (i, 0))
    sd = jax.ShapeDtypeStruct(w.shape, F32)
    return pl.pallas_call(
        body, name=name, grid=(r // tr,),
        in_specs=[spec] * 4, out_specs=[spec] * 3, out_shape=[sd] * 3,
    )(w, g, m, v)


def _adamw_in(w_t, m_t, v_t, own_half, swapped, core):
    r, c = SHARD_SHAPES[0]
    tr = max(t for t in range(8, 257, 8) if r % t == 0)
    c1 = 1.0 - ADAM_B1 ** ADAM_STEP
    c2 = 1.0 - ADAM_B2 ** ADAM_STEP

    def body(core_ref, w_ref, m_ref, v_ref, own_ref, sw_ref, d_ref, nm_ref, nv_ref, g_ref):
        own = own_ref[...]
        col_half = lax.broadcasted_iota(jnp.int32, (tr, c), 1) // (c // 2)
        gv = jnp.where(col_half == core_ref[0], jnp.concatenate([own, own], axis=1), sw_ref[...])
        g_ref[0] = gv
        nm = ADAM_B1 * m_ref[0] + (1.0 - ADAM_B1) * gv
        nv = ADAM_B2 * v_ref[0] + (1.0 - ADAM_B2) * (gv * gv)
        nm_ref[0] = nm
        nv_ref[0] = nv
        d_ref[0] = -ADAM_LR * ((nm / c1) / (jnp.sqrt(nv / c2) + ADAM_EPS) + ADAM_WD * w_ref[0])

    full = pl.BlockSpec((1, tr, c), lambda i, core_ref: (0, i, 0))
    sd = jax.ShapeDtypeStruct((1, r, c), F32)
    return pl.pallas_call(
        body, name="adamw_in",
        grid_spec=pltpu.PrefetchScalarGridSpec(
            num_scalar_prefetch=1, grid=(r // tr,),
            in_specs=[full, full, full, pl.BlockSpec((tr, c // 2), lambda i, core_ref: (i, 0)),
                      pl.BlockSpec((tr, c), lambda i, core_ref: (i, 0))],
            out_specs=[full] * 4),
        out_shape=[sd] * 4,
    )(core.reshape(1), w_t, m_t, v_t, own_half, swapped)


ANY = pl.BlockSpec(memory_space=pl.ANY)


def _my_place():
    return lax.axis_index("x"), lax.axis_index("y"), lax.axis_index("c")


HBM = pl.BlockSpec(memory_space=pltpu.HBM)
SEM = pl.BlockSpec(memory_space=pltpu.SEMAPHORE)
DATAFLOW = pltpu.SideEffectType.DATAFLOW_SIDE_EFFECTING


def _near_chips(x, y):
    return [(1 - x, y), (x, 1 - y)]


def _half(mi, hc):
    r, c = SHARD_SHAPES[mi]
    if mi == 0:
        return pl.ds(0, r), pl.ds(pl.multiple_of(hc * (c // 2), 128), c // 2)
    return pl.ds(pl.multiple_of(hc * (r // 2), 16), r // 2), pl.ds(0, c)


def _gather_copies(m_refs, land_refs, send_sems, recv_sems):
    x, y, c = _my_place()
    out, back = [], []
    for mi in range(N_MATS):
        rows, cols = _half(mi, c)
        for j, (cx, cy) in enumerate(_near_chips(x, y)):
            sems = dict(send_sem=send_sems.at[mi * 2 + j], recv_sem=recv_sems.at[mi * 2 + j],
                        device_id=(cx, cy, c), device_id_type=MESH)
            out.append(pltpu.make_async_remote_copy(src_ref=m_refs[mi].at[rows, cols],
                                                    dst_ref=land_refs[mi].at[2 * x + y, rows, cols], **sems))
            got = land_refs[mi].at[2 * cx + cy, rows, cols]
            back.append(pltpu.make_async_remote_copy(src_ref=got, dst_ref=got, **sems))
    return out, back


def _gather_start(mats, landing):
    n = N_MATS

    def body(*refs):
        out, _ = _gather_copies(refs[:n], refs[n:2 * n], refs[2 * n], refs[2 * n + 1])
        for cp in out:
            cp.start()
        refs[-1][...] = jnp.zeros_like(refs[-1])

    hbm = [pltpu.HBM(a.shape, a.dtype) for a in list(mats) + list(landing)]
    outs = pl.pallas_call(
        body, name="gather_start",
        out_shape=(pltpu.SemaphoreType.DMA((2 * n,)), pltpu.SemaphoreType.DMA((2 * n,)), *hbm,
                   jax.ShapeDtypeStruct((8, 128), F32)),
        in_specs=[HBM] * (2 * n), out_specs=(SEM, SEM, *[HBM] * (2 * n), pl.BlockSpec(memory_space=pltpu.VMEM)),
        input_output_aliases={i: 2 + i for i in range(2 * n)},
        compiler_params=pltpu.CompilerParams(has_side_effects=DATAFLOW),
    )(*[pltpu.with_memory_space_constraint(a, pltpu.HBM) for a in list(mats) + list(landing)])
    return outs[:-1], outs[-1]


def _gather_wait(handle, after):
    n = N_MATS

    def body(*refs):
        out, back = _gather_copies(refs[:n], refs[n:2 * n], refs[2 * n], refs[2 * n + 1])
        for cp, arrival in zip(out, back):
            cp.wait_send()
            arrival.wait_recv()

    bufs = handle[2:]
    after, after_specs = _after(after)
    res = pl.pallas_call(
        body, name="gather_wait", out_shape=tuple(pltpu.HBM(b.shape, b.dtype) for b in bufs),
        in_specs=[HBM] * (2 * n) + [SEM, SEM] + after_specs, out_specs=tuple([HBM] * (2 * n)),
        input_output_aliases={i: i for i in range(2 * n)},
        compiler_params=pltpu.CompilerParams(has_side_effects=DATAFLOW),
    )(*bufs, handle[0], handle[1], *after)
    return list(res[n:])


def _relay_share(gathered):
    n = N_MATS

    def body(*refs):
        out_refs = refs[n:2 * n]
        send_sems, recv_sems = refs[2 * n:]
        x, y, c = _my_place()
        sibling = (x, y, 1 - c)
        relayed = 2 * (x ^ (1 - c)) + (y ^ c)
        relay_to = (x ^ c, y ^ (1 - c), c)
        far = 2 * (1 - x) + (1 - y)
        near = [2 * (1 - x) + y, 2 * x + (1 - y)]

        def copy(k, mi, shard, hc, to):
            blk = out_refs[mi].at[(shard,) + _half(mi, hc)]
            return pltpu.make_async_remote_copy(src_ref=blk, dst_ref=blk, send_sem=send_sems.at[mi * 4 + k],
                                                recv_sem=recv_sems.at[mi * 4 + k], device_id=to, device_id_type=MESH)

        sends = []
        for mi in range(n):
            sends.append(copy(0, mi, relayed, c, relay_to))
            sends += [copy(1 + j, mi, near[j], c, sibling) for j in range(2)]
        for cp in sends:
            cp.start()
        for mi in range(n):
            copy(0, mi, far, c, relay_to).wait_recv()
            cp = copy(3, mi, far, c, sibling)
            cp.start()
            sends.append(cp)
        for mi in range(n):
            for j in range(2):
                copy(1 + j, mi, near[j], 1 - c, sibling).wait_recv()
            copy(3, mi, far, 1 - c, sibling).wait_recv()
        for cp in sends:
            cp.wait_send()

    return pl.pallas_call(
        body, name="relay_share",
        in_specs=[ANY] * n, out_specs=[ANY] * n,
        out_shape=[jax.ShapeDtypeStruct(g.shape, g.dtype) for g in gathered],
        input_output_aliases={i: i for i in range(n)},
        scratch_shapes=[pltpu.SemaphoreType.DMA((4 * n,)), pltpu.SemaphoreType.DMA((4 * n,))],
    )(*gathered)


def _peers(x, y, c):
    out = []
    for k in range(1, 8):
        px, py, pc = x ^ (k >> 2), y ^ ((k >> 1) & 1), c ^ (k & 1)
        out.append((k - 1, (px, py, pc), 4 * px + 2 * py + pc))
    return out


def _exchange_start(parts, name):
    n = len(parts)

    def body(*refs):
        p_refs, land_refs = refs[:n], refs[n:2 * n]
        send_sems, recv_sems, token = refs[2 * n], refs[2 * n + 1], refs[-1]
        x, y, c = _my_place()
        me = 4 * x + 2 * y + c
        for k, dev, peer in _peers(x, y, c):
            for mi in range(n):
                pltpu.make_async_remote_copy(
                    src_ref=p_refs[mi].at[peer], dst_ref=land_refs[mi].at[me], send_sem=send_sems.at[k * n + mi],
                    recv_sem=recv_sems.at[k * n + mi], device_id=dev, device_id_type=MESH).start()
        token[...] = jnp.zeros_like(token)

    hbm = [pltpu.HBM(p.shape, p.dtype) for p in parts]
    outs = pl.pallas_call(
        body, name=name + "_start",
        out_shape=(pltpu.SemaphoreType.DMA((7 * n,)), pltpu.SemaphoreType.DMA((7 * n,)), *hbm, *hbm,
                   jax.ShapeDtypeStruct((8, 128), F32)),
        in_specs=[HBM] * (2 * n), out_specs=(SEM, SEM, *[HBM] * (2 * n), pl.BlockSpec(memory_space=pltpu.VMEM)),
        input_output_aliases={i: 2 + i for i in range(2 * n)},
        compiler_params=pltpu.CompilerParams(has_side_effects=DATAFLOW),
    )(*[pltpu.with_memory_space_constraint(p, pltpu.HBM) for p in parts],
      *[pltpu.with_memory_space_constraint(lax.empty(p.shape, p.dtype), pltpu.HBM) for p in parts])
    return (name, outs[:-1]), outs[-1]


def _exchange_wait(handle, after):
    name, outs = handle
    n = (len(outs) - 2) // 2

    def body(*refs):
        p_refs, land_refs = refs[:n], refs[n:2 * n]
        send_sems, recv_sems = refs[2 * n], refs[2 * n + 1]
        x, y, c = _my_place()
        me = 4 * x + 2 * y + c
        for k, dev, peer in _peers(x, y, c):
            for mi in range(n):
                pltpu.make_async_remote_copy(
                    src_ref=p_refs[mi].at[peer], dst_ref=land_refs[mi].at[me], send_sem=send_sems.at[k * n + mi],
                    recv_sem=recv_sems.at[k * n + mi], device_id=dev, device_id_type=MESH).wait_send()
                slot = land_refs[mi].at[peer]
                pltpu.make_async_remote_copy(
                    src_ref=slot, dst_ref=slot, send_sem=send_sems.at[k * n + mi],
                    recv_sem=recv_sems.at[k * n + mi], device_id=dev, device_id_type=MESH).wait_recv()

    bufs = outs[2:]
    res = pl.pallas_call(
        body, name=name + "_wait", out_shape=tuple(pltpu.HBM(b.shape, b.dtype) for b in bufs),
        in_specs=[HBM] * (2 * n) + [SEM, SEM, ANY], out_specs=tuple([HBM] * (2 * n)),
        input_output_aliases={i: i for i in range(2 * n)},
        compiler_params=pltpu.CompilerParams(has_side_effects=DATAFLOW),
    )(*bufs, outs[0], outs[1], after)
    return list(res[n:])


def _swap_halves(halves, gvec):
    def place(ref, mi, hc):
        return ref.at[:, pl.ds(pl.multiple_of(hc * 512, 128), 512)] if mi == 0 else ref.at[hc]

    def body(*refs):
        g_refs, gv_ref = refs[:N_MATS], refs[N_MATS]
        out_refs, rg_ref = refs[N_MATS + 1:2 * N_MATS + 1], refs[2 * N_MATS + 1]
        send_sems, recv_sems = refs[2 * N_MATS + 2:]
        x, y, c = _my_place()
        me = 4 * x + 2 * y + c
        sends = []
        for mi in range(N_MATS):
            cp = pltpu.make_async_remote_copy(src_ref=g_refs[mi], dst_ref=place(out_refs[mi], mi, c), send_sem=send_sems.at[mi],
                                              recv_sem=recv_sems.at[mi], device_id=(x, y, 1 - c), device_id_type=MESH)
            cp.start()
            sends.append(cp)
        for k, dev, peer in _peers(x, y, c):
            cp = pltpu.make_async_remote_copy(src_ref=gv_ref, dst_ref=rg_ref.at[me], send_sem=send_sems.at[N_MATS + k],
                                              recv_sem=recv_sems.at[N_MATS + k], device_id=dev, device_id_type=MESH)
            cp.start()
            sends.append(cp)
        for mi in range(N_MATS):
            got = place(out_refs[mi], mi, 1 - c)
            pltpu.make_async_remote_copy(src_ref=got, dst_ref=got, send_sem=send_sems.at[mi], recv_sem=recv_sems.at[mi],
                                         device_id=(x, y, 1 - c), device_id_type=MESH).wait_recv()
        for k, dev, peer in _peers(x, y, c):
            got = rg_ref.at[peer]
            pltpu.make_async_remote_copy(src_ref=got, dst_ref=got, send_sem=send_sems.at[N_MATS + k],
                                         recv_sem=recv_sems.at[N_MATS + k], device_id=dev, device_id_type=MESH).wait_recv()
        for cp in sends:
            cp.wait_send()

    outs = pl.pallas_call(
        body, name="swap_halves",
        in_specs=[ANY] * (N_MATS + 1), out_specs=[ANY] * (N_MATS + 1),
        out_shape=[jax.ShapeDtypeStruct(SHARD_SHAPES[0], F32)]
        + [jax.ShapeDtypeStruct((2, r // 2, c), F32) for r, c in SHARD_SHAPES[1:]]
        + [jax.ShapeDtypeStruct((8, 8, N_GVEC), F32)],
        scratch_shapes=[pltpu.SemaphoreType.DMA((N_MATS + 7,)), pltpu.SemaphoreType.DMA((N_MATS + 7,))],
    )(*halves, gvec)
    return outs[:N_MATS], outs[N_MATS]


def _set_slot(arr, block, idx):
    return lax.dynamic_update_slice(arr, block[None], (idx,) + (0,) * block.ndim)


PAD_RUNS = ((6304, 8352, 0), (5280, 6304, COL_Z), (672, 5280, COL_QKV), (0, 640, COL_LAT), (640, 672, COL_LAT + 704))
W_IN_SHARD = 2088


def _full_weights(gathered):
    def cols(a):
        return jnp.concatenate([a[s] for s in range(4)], axis=1)

    w_uq, w_ukv, w_pm, w_pd = [cols(a) for a in gathered[1:5]]
    w_out = gathered[5].reshape(D_MODEL, D_MODEL)
    w_in_t = gathered[0].reshape(4 * W_IN_SHARD, D_MODEL)
    pieces, at = [], 0
    for lo, hi, pad_lo in sorted(PAD_RUNS, key=lambda t: t[2]):
        if pad_lo > at:
            pieces.append(jnp.zeros((pad_lo - at, D_MODEL), w_in_t.dtype))
        pieces.append(w_in_t[lo:hi])
        at = pad_lo + hi - lo
    pieces.append(jnp.zeros((N_PAD - at, D_MODEL), w_in_t.dtype))
    w_pad_t = jnp.concatenate(pieces, axis=0)
    z32 = jnp.zeros((Q_RANK, 32), w_uq.dtype)
    wuq_pad = jnp.concatenate([t for h in range(MLA_HEADS) for t in (w_uq[:, h * 96:(h + 1) * 96], z32)], axis=1)
    z64 = jnp.zeros((KV_RANK, 64), w_ukv.dtype)
    wk_pad = jnp.concatenate([t for h in range(MLA_HEADS) for t in (w_ukv[:, h * 128:h * 128 + 64], z64)], axis=1)
    wv = jnp.concatenate([w_ukv[:, h * 128 + 64:(h + 1) * 128] for h in range(MLA_HEADS)], axis=1)
    return w_pad_t.T, w_pad_t, wuq_pad, wk_pad, wv, w_pm, w_pd, w_out


W_IN_LAT = 672


def _grad_parts_in_early(dwt_early):
    def in_block(s, h):
        cols = slice(h * 512, (h + 1) * 512)
        out = []
        for lo, hi, pad_lo in sorted(PAD_RUNS):
            a_, b_ = max(lo, s * W_IN_SHARD), min(hi, (s + 1) * W_IN_SHARD)
            if a_ < b_:
                out.append(jnp.zeros((b_ - a_, 512), dwt_early.dtype) if pad_lo >= COL_LAT
                           else dwt_early[pad_lo + a_ - lo:pad_lo + b_ - lo, cols])
        return jnp.concatenate(out, axis=0)

    return jnp.stack([in_block(s, h) for s in range(4) for h in range(2)])


def _grad_parts_in_late(dwt_late):
    rows = jnp.concatenate([dwt_late[0:640], dwt_late[704:736]], axis=0)
    zero = jnp.zeros((W_IN_LAT, 512), dwt_late.dtype)
    return jnp.stack([rows[:, 0:512], rows[:, 512:1024]] + [zero] * 6)


def _col_blocks(m):
    r, c = m.shape[0] // 2, m.shape[1] // 4
    return jnp.stack([m[h * r:(h + 1) * r, s * c:(s + 1) * c] for s in range(4) for h in range(2)])


def _grad_parts_mla(dwuq_pad, dwk_pad, dwv):
    d_uq = jnp.concatenate([dwuq_pad[:, h * 128:h * 128 + 96] for h in range(MLA_HEADS)], axis=1)
    d_ukv = jnp.concatenate([t for h in range(MLA_HEADS) for t in (dwk_pad[:, h * 128:h * 128 + 64], dwv[:, h * 64:(h + 1) * 64])],
                            axis=1)
    return [_col_blocks(d_uq), _col_blocks(d_ukv)]


def _rope_tables(positions, token=None):
    pos = positions.reshape(SEQ).astype(F32)
    if token is not None:
        pos = pos + token[0, 0]
    lane = jnp.arange(128)

    def table(rot, first, period):
        inv = ROPE_THETA ** (-jnp.arange(0, rot, 2, dtype=F32) / rot)
        half = rot // 2
        off = lane % period - first
        in1, in2 = (off >= 0) & (off < half), (off >= half) & (off < rot)
        inv_lane = jnp.where(in1 | in2, inv[jnp.clip(off % half, 0, half - 1)], 0.0)
        sign = jnp.where(in1, -1.0, 1.0).astype(F32)
        ang = pos[:, None] * inv_lane[None, :]
        return jnp.cos(ang), jnp.sin(ang) * sign[None, :]

    return table(32, 64, 128), table(16, 0, 64)


class _Links:
    def __init__(self, mats, chip, me):
        landing = [_set_slot(lax.empty((4,) + m.shape, m.dtype), m, chip) for m in mats]
        self.gather, self.token = _gather_start(mats, landing)
        self.me, self.sent, self.handles, self.sums = me, {}, {}, {}

    def weights(self, after):
        return _relay_share(_gather_wait(self.gather, after))

    def send(self, blocks, name):
        self.sent[name] = blocks
        self.handles[name], token = _exchange_start(blocks, name)
        return token

    def collect(self, name, after, parts):
        recv = _exchange_wait(self.handles[name], after)
        for r, own, part in zip(recv, self.sent[name], parts):
            self.sums[part] = _sum_parts(r, own, self.me, 64, "sum_grad_" + part)
        return tuple(self.sums[part] for part in parts)


def _device_grads(x, positions, target, gains, links):
    pre_g, q_g, kv_g, post_g = gains
    (mc, ms), (dc, ds) = _rope_tables(positions, links.token)
    h = _prenorm_fwd(x, pre_g, links.token)
    w_pad, w_pad_t, wuq_pad, wk_pad, wv, w_pm, w_pd, w_out = _full_weights(links.weights((h, mc, ms, dc, ds)))

    p = _matmul(h, w_pad, "nn", F32, 1024, 1408, 1024, "in_proj")
    cqn, ckvn, q, k, v = _mla_prep_fwd(p, q_g, kv_g, wuq_pad, wk_pad, wv, mc, ms)
    ya, lse_m = _mla_flash_fwd(q, k, v)
    qkv = [_dil_prep_fwd(p, dc, ds, g) for g in range(3)]
    o_g, l_g = zip(*[_dil_attn_fwd2(qkv[g], g) for g in range(3)])
    (dp, dy, mg, dt, ua, dpa, ud, dpd, dya, dyd, yd, lse_d, loss_cols, dg_post) = _tail(
        p, ya, o_g, l_g, x, target, w_pm, w_pd, w_out, post_g)

    for g in range(3):
        dqkv = _dil_attn_bwd2(qkv[g], dyd, yd, lse_d, g)
        dp = _dil_prep_bwd(dp, dqkv, dc, ds, g)
    dw_early = _matmul(dp, h, "tn", BF16, 1536, 1024, 2048, "dw_in_early", a_cols=(0, COL_LAT // 1536))
    dwpm = _matmul(ua, dpa, "tn", BF16, 512, 1024, 512, "dw_proj_mla")
    dwpd = _matmul(ud, dpd, "tn", BF16, 512, 1024, 512, "dw_proj_dil")
    dwout = _matmul(mg, dt, "tn", BF16, 1024, 1024, 512, "dw_out")
    token = links.send([_grad_parts_in_early(dw_early), _col_blocks(dwpm), _col_blocks(dwpd),
                        dwout.reshape(8, 128, D_MODEL)], "exchange_early")

    dq, dk, dv = _mla_flash_bwd(q, k, v, ya, dya, lse_m, token)
    dp, dqb, dkb, dvb, dg_q, dg_kv = _mla_prep_bwd(dp, p, dq, dk, dv, q_g, kv_g, wuq_pad, wk_pad, wv, mc, ms)
    dwuq_pad = _matmul(cqn, dqb, "tn", BF16, Q_RANK, 1024, 512, "dw_uq")
    dwk_pad = _matmul(ckvn, dkb, "tn", BF16, KV_RANK, 1024, 512, "dw_k")
    dwv = _matmul(ckvn, dvb, "tn", BF16, KV_RANK, 512, 512, "dw_v")
    dw_late = _matmul(dp, h, "tn", BF16, N_LAT, 1024, 2048, "dw_in_late", a_cols=(COL_LAT // N_LAT, 1))
    token = links.send([_grad_parts_in_late(dw_late)] + _grad_parts_mla(dwuq_pad, dwk_pad, dwv), "exchange_late")
    early = links.collect("exchange_early", dw_late, ("in_early", "pm", "pd", "out"))

    grad_x, dg_pre = _dh_prenorm_bwd(dp, w_pad_t, x, dy, pre_g, (token,) + tuple(early))
    links.collect("exchange_late", grad_x, ("in_late", "uq", "ukv"))

    loss_part = jnp.pad((jnp.sum(loss_cols) * (0.5 / D_MODEL)).reshape(1, 1), ((0, 0), (0, N_GVEC - N_GAINS - 1)))
    gvec = jnp.concatenate([dg_pre, dg_q, dg_kv, dg_post, loss_part], axis=1)
    return grad_x, gvec


def kernel(x, positions, pre_norm_g, w_in, q_norm_g, w_uq, kv_norm_g, w_ukv, w_proj_mla, w_proj_dil, w_out, post_norm_g, loss_target, m_pre_norm_g, m_w_in, m_q_norm_g, m_w_uq, m_kv_norm_g, m_w_ukv, m_w_proj_mla, m_w_proj_dil, m_w_out, m_post_norm_g, v_pre_norm_g, v_w_in, v_q_norm_g, v_w_uq, v_kv_norm_g, v_w_ukv, v_w_proj_mla, v_w_proj_dil, v_w_out, v_post_norm_g):
    xi, yi, ci = _my_place()
    chip, me = 2 * xi + yi, 4 * xi + 2 * yi + ci
    mats = [jnp.swapaxes(w_in, 1, 2)] + [w_uq, w_ukv, w_proj_mla, w_proj_dil, w_out]
    mats = [w.reshape(w.shape[1:]).astype(BF16) for w in mats]
    links = _Links(mats, chip, me)
    gains = (pre_norm_g, q_norm_g, kv_norm_g, post_norm_g)
    grad_x, gvec = _device_grads(x[0], positions, loss_target[0], gains, links)

    sums = links.sums
    in_e = sums["in_early"]
    half_in = jnp.concatenate([in_e[:W_IN_LAT] + jnp.where(chip == 0, sums["in_late"], 0.0), in_e[W_IN_LAT:]], axis=0)
    halves = [half_in, sums["uq"], sums["ukv"], sums["pm"], sums["pd"], sums["out"]]
    gvec8 = jnp.pad(gvec, ((0, 7), (0, 0)))
    swapped, recv_gains = _swap_halves(halves, gvec8)
    g_gains = _sum_parts(recv_gains, gvec8, me, 8, "sum_gain_parts")[0:1]
    loss = g_gains[0, N_GAINS]
    sw = lambda a: jnp.swapaxes(a, 1, 2)
    d_in, m_in, v_in, g_in = [sw(o) for o in _adamw_in(sw(w_in), sw(m_w_in), sw(v_w_in), half_in, swapped[0], ci)]
    g_mats = [g_in] + [_set_slot(s, hf, ci).reshape((1,) + shp)
                       for s, hf, shp in zip(swapped[1:], halves[1:], SHARD_SHAPES[1:])]

    off = [0, 1024, 1408, 1664, 2688]
    g_gain = [g_gains[:, off[i]:off[i + 1]] for i in range(4)]
    grads = [g_gain[0], g_mats[0], g_gain[1], g_mats[1], g_gain[2], g_mats[2], g_mats[3], g_mats[4], g_mats[5], g_gain[3]]
    ws = [pre_norm_g, w_in, q_norm_g, w_uq, kv_norm_g, w_ukv, w_proj_mla, w_proj_dil, w_out, post_norm_g]
    ms = [m_pre_norm_g, m_w_in, m_q_norm_g, m_w_uq, m_kv_norm_g, m_w_ukv, m_w_proj_mla, m_w_proj_dil, m_w_out, m_post_norm_g]
    vs = [v_pre_norm_g, v_w_in, v_q_norm_g, v_w_uq, v_kv_norm_g, v_w_ukv, v_w_proj_mla, v_w_proj_dil, v_w_out, v_post_norm_g]
    deltas, new_m, new_v = [], [], []
    for i, (w, g, m, v) in enumerate(zip(ws, grads, ms, vs)):
        if w is w_in:
            d_, m_, v_ = d_in, m_in, v_in
        elif w.shape[-1] % 128 and w.shape[-2] % 128 == 0:
            g = jnp.swapaxes(g, 1, 2)
            grads[i] = jnp.swapaxes(g, 1, 2)
            d_, m_, v_ = [jnp.swapaxes(o, 1, 2) for o in
                          _adamw(jnp.swapaxes(w, 1, 2), g, jnp.swapaxes(m, 1, 2), jnp.swapaxes(v, 1, 2), f"adamw_{i}")]
        else:
            d_, m_, v_ = _adamw(w, g, m, v, f"adamw_{i}")
        deltas.append(d_)
        new_m.append(m_)
        new_v.append(v_)
    return (loss, grad_x.reshape(x.shape), *grads, *deltas, *new_m, *new_v)
```

```python
import jax
import jax.numpy as jnp
from jax import lax
from jax.experimental import pallas as pl
from jax.experimental.pallas import tpu as pltpu

F32 = jnp.float32
BF16 = jnp.bfloat16

SEQ = 4096
D_MODEL = 1024
EPS = 1e-6
ROPE_THETA = 500000.0
MLA_HEADS = 8
Q_RANK = 384
KV_RANK = 256
MLA_SCALE = 96.0 ** -0.5
MLA_ROPE_HALF = 16
DIL_DILATIONS = (1, 4, 16)
DIL_ROPE_HALF = 8
DIL_SCALE = 0.125
BAND = 128

N_LAT = 768
COL_Z, COL_QKV, COL_LAT = 2048, 3072, 7680
N_PAD = 8448


def _qkv_block(tq, g, pr):
    return COL_QKV // 128 + (g * 4 + pr) * 3 + tq

IN_SPLITS = (384, 256, 32, 4608, 512, 512, 1024, 1024)

SHARD_SHAPES = ((2088, 1024), (384, 192), (256, 256), (512, 256), (512, 256), (256, 1024))
N_MATS = len(SHARD_SHAPES)
N_GAINS = 2688
N_GVEC = N_GAINS + 128

ADAM_LR, ADAM_B1, ADAM_B2, ADAM_EPS, ADAM_WD, ADAM_STEP = 0.001, 0.9, 0.999, 1e-08, 0.01, 10

VMEM_LIMIT = 56 * 1024 * 1024
NEG = -1e30
MESH = pl.DeviceIdType.MESH


def _cparams(**kw):
    return pltpu.CompilerParams(vmem_limit_bytes=VMEM_LIMIT, **kw)


def _dot(a, b, dims):
    return lax.dot_general(a, b, (dims, ((), ())), preferred_element_type=F32)


def _nn(a, b):
    return _dot(a, b, ((1,), (0,)))


def _nt(a, b):
    return _dot(a, b, ((1,), (1,)))


def _tn(a, b):
    return _dot(a, b, ((0,), (0,)))


def _rope_lanes(shape, half, period, first):
    lane = lax.broadcasted_iota(jnp.int32, shape, len(shape) - 1) % period
    return (lane >= first) & (lane < first + half), (lane >= first + half) & (lane < first + 2 * half)


def _rope_fwd(x, c, s, half, lanes):
    x1, _ = lanes
    return x * c + jnp.where(x1, pltpu.roll(x, 128 - half, 1), pltpu.roll(x, half, 1)) * s


def _rope_bwd(g, c, s, half, lanes):
    x1, x2 = lanes
    gs = g * s
    return g * c + jnp.where(x2, pltpu.roll(gs, half, 1), jnp.where(x1, pltpu.roll(gs, 128 - half, 1), 0.0))


def _sigmoid(x):
    return 1.0 / (1.0 + jnp.exp(-x))


def _after(token):
    tokens = [t for t in (token if isinstance(token, (tuple, list)) else [token]) if t is not None]
    return tokens, [pl.BlockSpec(memory_space=pl.ANY)] * len(tokens)


def _matmul(a, b, mode, out_dtype, tm, tn, tk, name, token=None, b_cols=None, a_cols=None):
    after, after_specs = _after(token)
    if mode == "nn":
        (m, k), n = a.shape, b.shape[1]
        first = 0
        if b_cols is not None:
            first, n = b_cols[0], b_cols[1] * tn
        a_spec = pl.BlockSpec((tm, tk), lambda j, i, kk: (i, kk))
        b_spec = pl.BlockSpec((tk, tn), lambda j, i, kk: (kk, j + first))
        dot = _nn
    elif mode == "nt":
        (m, k), n = a.shape, b.shape[0]
        a_spec = pl.BlockSpec((tm, tk), lambda j, i, kk: (i, kk))
        b_spec = pl.BlockSpec((tn, tk), lambda j, i, kk: (j, kk))
        dot = _nt
    else:
        (k, m), n = a.shape, b.shape[1]
        first = 0
        if a_cols is not None:
            first, m = a_cols[0], a_cols[1] * tm
        a_spec = pl.BlockSpec((tk, tm), lambda j, i, kk: (kk, i + first))
        b_spec = pl.BlockSpec((tk, tn), lambda j, i, kk: (kk, j))
        dot = _tn
    assert m % tm == 0 and n % tn == 0 and k % tk == 0, (name, m, n, k, tm, tn, tk)
    nk = k // tk

    def body(a_ref, b_ref, *rest):
        o_ref, acc_ref = rest[-2:]
        kk = pl.program_id(2)
        part = dot(a_ref[...], b_ref[...])

        @pl.when(kk == 0)
        def _():
            acc_ref[...] = part

        @pl.when(kk > 0)
        def _():
            acc_ref[...] += part

        @pl.when(kk == nk - 1)
        def _():
            o_ref[...] = acc_ref[...].astype(o_ref.dtype)

    return pl.pallas_call(
        body, name=name, grid=(n // tn, m // tm, nk),
        in_specs=[a_spec, b_spec] + after_specs,
        out_specs=pl.BlockSpec((tm, tn), lambda j, i, kk: (i, j)),
        out_shape=jax.ShapeDtypeStruct((m, n), out_dtype),
        scratch_shapes=[pltpu.VMEM((tm, tn), F32)],
        compiler_params=_cparams(),
    )(a, b, *after)


def _prenorm_fwd(x, g, token=None):
    tm = 512
    after, after_specs = _after(token)

    def body(x_ref, g_ref, *rest):
        xv = x_ref[...]
        r = lax.rsqrt(jnp.mean(xv * xv, axis=-1, keepdims=True) + EPS)
        rest[-1][...] = (xv * r * g_ref[...]).astype(BF16)

    return pl.pallas_call(
        body, name="prenorm_fwd", grid=(SEQ // tm,),
        in_specs=[pl.BlockSpec((tm, D_MODEL), lambda i: (i, 0)), pl.BlockSpec((1, D_MODEL), lambda i: (0, 0))] + after_specs,
        out_specs=pl.BlockSpec((tm, D_MODEL), lambda i: (i, 0)),
        out_shape=jax.ShapeDtypeStruct((SEQ, D_MODEL), BF16),
    )(x, g, *after)


def _prenorm_bwd(x, dh, dy, g):
    tm = 512

    def body(x_ref, dh_ref, dy_ref, g_ref, gx_ref, dg_ref):
        xv = x_ref[...]
        r = lax.rsqrt(jnp.mean(xv * xv, axis=-1, keepdims=True) + EPS)
        n = xv * r
        dhv = dh_ref[...]
        dn = dhv * g_ref[...]
        gx_ref[...] = dy_ref[...] + r * (dn - n * jnp.mean(dn * n, axis=-1, keepdims=True))
        part = jnp.sum(dhv * n, axis=0, keepdims=True)

        @pl.when(pl.program_id(0) == 0)
        def _():
            dg_ref[...] = part

        @pl.when(pl.program_id(0) > 0)
        def _():
            dg_ref[...] += part

    row = pl.BlockSpec((tm, D_MODEL), lambda i: (i, 0))
    vec = pl.BlockSpec((1, D_MODEL), lambda i: (0, 0))
    return pl.pallas_call(
        body, name="prenorm_bwd", grid=(SEQ // tm,),
        in_specs=[row, row, row, vec], out_specs=[row, vec],
        out_shape=[jax.ShapeDtypeStruct((SEQ, D_MODEL), F32), jax.ShapeDtypeStruct((1, D_MODEL), F32)],
        compiler_params=_cparams(),
    )(x, dh, dy, g)


def _dh_prenorm_bwd(dp, w_pad_t, x, dy, g, token=None):
    tm, tk = 1024, 1408
    nk = N_PAD // tk
    after, after_specs = _after(token)

    def body(a_ref, b_ref, x_ref, dy_ref, g_ref, *rest):
        gx_ref, dg_ref, acc_ref = rest[-3:]
        i, kk = pl.program_id(0), pl.program_id(1)
        part = _nn(a_ref[...], b_ref[...])

        @pl.when(kk == 0)
        def _():
            acc_ref[...] = part

        @pl.when(kk > 0)
        def _():
            acc_ref[...] += part

        @pl.when(kk == nk - 1)
        def _():
            xv = x_ref[...]
            r = lax.rsqrt(jnp.mean(xv * xv, axis=-1, keepdims=True) + EPS)
            n = xv * r
            dhv = acc_ref[...]
            dn = dhv * g_ref[...]
            gx_ref[...] = dy_ref[...] + r * (dn - n * jnp.mean(dn * n, axis=-1, keepdims=True))
            cols = jnp.sum(dhv * n, axis=0, keepdims=True)

            @pl.when(i == 0)
            def _():
                dg_ref[...] = cols

            @pl.when(i > 0)
            def _():
                dg_ref[...] += cols

    row = pl.BlockSpec((tm, D_MODEL), lambda i, kk: (i, 0))
    vec = pl.BlockSpec((1, D_MODEL), lambda i, kk: (0, 0))
    return pl.pallas_call(
        body, name="dh_prenorm_bwd", grid=(SEQ // tm, nk),
        in_specs=[pl.BlockSpec((tm, tk), lambda i, kk: (i, kk)), pl.BlockSpec((tk, D_MODEL), lambda i, kk: (kk, 0)),
                  row, row, vec] + after_specs,
        out_specs=[row, vec],
        out_shape=[jax.ShapeDtypeStruct((SEQ, D_MODEL), F32), jax.ShapeDtypeStruct((1, D_MODEL), F32)],
        scratch_shapes=[pltpu.VMEM((tm, D_MODEL), F32)],
        compiler_params=_cparams(),
    )(dp, w_pad_t, x, dy, g, *after)


def _mla_prep_fwd(p, qg, kvg, wuq, wk, wv, rc, rs):
    tm = 512

    def body(lat_ref, qg_ref, kvg_ref, wuq_ref, wk_ref, wv_ref, c_ref, s_ref,
             cqn_ref, ckvn_ref, q_ref, k_ref, v_ref):
        c, s = c_ref[...], s_ref[...]
        lanes = _rope_lanes((tm, 128), MLA_ROPE_HALF, 128, 64)
        cq = lat_ref[:, 0:Q_RANK]
        r1 = lax.rsqrt(jnp.mean(cq * cq, axis=-1, keepdims=True) + EPS)
        cqn = (cq * r1 * qg_ref[...]).astype(BF16)
        cqn_ref[...] = cqn
        q = _nn(cqn, wuq_ref[...])
        for h in range(MLA_HEADS):
            sl = slice(h * 128, (h + 1) * 128)
            q_ref[:, sl] = (_rope_fwd(q[:, sl], c, s, MLA_ROPE_HALF, lanes) * MLA_SCALE).astype(BF16)
        ckv = lat_ref[:, Q_RANK:Q_RANK + KV_RANK]
        r2 = lax.rsqrt(jnp.mean(ckv * ckv, axis=-1, keepdims=True) + EPS)
        ckvn = (ckv * r2 * kvg_ref[...]).astype(BF16)
        ckvn_ref[...] = ckvn
        krr = _rope_fwd(lat_ref[:, Q_RANK + KV_RANK:N_LAT], c, s, MLA_ROPE_HALF, lanes)
        kn = _nn(ckvn, wk_ref[...])
        for h in range(MLA_HEADS):
            sl = slice(h * 128, (h + 1) * 128)
            k_ref[:, sl] = (kn[:, sl] + krr).astype(BF16)
        v_ref[...] = _nn(ckvn, wv_ref[...]).astype(BF16)

    def full(shape):
        return pl.BlockSpec(shape, lambda i: (0, 0))

    def rows(w):
        return pl.BlockSpec((tm, w), lambda i: (i, 0))

    return pl.pallas_call(
        body, name="mla_prep_fwd", grid=(SEQ // tm,),
        in_specs=[pl.BlockSpec((tm, N_LAT), lambda i: (i, COL_LAT // N_LAT)),
                  full((1, Q_RANK)), full((1, KV_RANK)), full((Q_RANK, 1024)), full((KV_RANK, 1024)),
                  full((KV_RANK, 512)), rows(128), rows(128)],
        out_specs=[rows(Q_RANK), rows(KV_RANK), rows(1024), rows(1024), rows(512)],
        out_shape=[jax.ShapeDtypeStruct((SEQ, Q_RANK), BF16), jax.ShapeDtypeStruct((SEQ, KV_RANK), BF16),
                   jax.ShapeDtypeStruct((SEQ, 1024), BF16), jax.ShapeDtypeStruct((SEQ, 1024), BF16),
                   jax.ShapeDtypeStruct((SEQ, 512), BF16)],
        compiler_params=_cparams(),
    )(p, qg, kvg, wuq, wk, wv, rc, rs)


def _mla_prep_bwd(dp_in, p, dq, dk, dv, qg, kvg, wuq, wk, wv, rc, rs):
    tm = 512

    def body(dp_any, lat_ref, dq_ref, dk_ref, dv_ref, qg_ref, kvg_ref, wuq_ref, wk_ref, wv_ref,
             c_ref, s_ref, dp_ref, dqb_ref, dkb_ref, dvb_ref, dgq_ref, dgkv_ref):
        del dp_any
        c, s = c_ref[...], s_ref[...]
        lanes = _rope_lanes((tm, 128), MLA_ROPE_HALF, 128, 64)
        lane = lax.broadcasted_iota(jnp.int32, (tm, 128), 1)
        dkr = jnp.zeros((tm, 128), F32)
        for h in range(MLA_HEADS):
            sl = slice(h * 128, (h + 1) * 128)
            dqb_ref[:, sl] = _rope_bwd(dq_ref[:, sl] * MLA_SCALE, c, s, MLA_ROPE_HALF, lanes).astype(BF16)
            dkh = dk_ref[:, sl]
            dkr = dkr + dkh
            dkb_ref[:, sl] = jnp.where(lane < 64, dkh, 0.0).astype(BF16)
        dkr = jnp.where((lane >= 64) & (lane < 96), dkr, 0.0)
        dkr = _rope_bwd(dkr, c, s, MLA_ROPE_HALF, lanes)
        dvb = dv_ref[...].astype(BF16)
        dvb_ref[...] = dvb

        cq = lat_ref[:, 0:Q_RANK]
        r1 = lax.rsqrt(jnp.mean(cq * cq, axis=-1, keepdims=True) + EPS)
        n1 = cq * r1
        dcqn = _nt(dqb_ref[...], wuq_ref[...])
        dn1 = dcqn * qg_ref[...]
        dcq = r1 * (dn1 - n1 * jnp.mean(dn1 * n1, axis=-1, keepdims=True))
        pq = jnp.sum(dcqn * n1, axis=0, keepdims=True)

        ckv = lat_ref[:, Q_RANK:Q_RANK + KV_RANK]
        r2 = lax.rsqrt(jnp.mean(ckv * ckv, axis=-1, keepdims=True) + EPS)
        n2 = ckv * r2
        dckvn = _nt(dkb_ref[...], wk_ref[...]) + _nt(dvb, wv_ref[...])
        dn2 = dckvn * kvg_ref[...]
        dckv = r2 * (dn2 - n2 * jnp.mean(dn2 * n2, axis=-1, keepdims=True))
        pkv = jnp.sum(dckvn * n2, axis=0, keepdims=True)

        dp_ref[:, 0:Q_RANK] = dcq.astype(BF16)
        dp_ref[:, Q_RANK:Q_RANK + KV_RANK] = dckv.astype(BF16)
        dp_ref[:, Q_RANK + KV_RANK:N_LAT] = dkr.astype(BF16)

        @pl.when(pl.program_id(0) == 0)
        def _():
            dgq_ref[...] = pq
            dgkv_ref[...] = pkv

        @pl.when(pl.program_id(0) > 0)
        def _():
            dgq_ref[...] += pq
            dgkv_ref[...] += pkv

    def full(shape):
        return pl.BlockSpec(shape, lambda i: (0, 0))

    def rows(w):
        return pl.BlockSpec((tm, w), lambda i: (i, 0))

    lat = pl.BlockSpec((tm, N_LAT), lambda i: (i, COL_LAT // N_LAT))
    return pl.pallas_call(
        body, name="mla_prep_bwd", grid=(SEQ // tm,),
        in_specs=[pl.BlockSpec(memory_space=pl.ANY), lat, rows(1024), rows(1024), rows(512),
                  full((1, Q_RANK)), full((1, KV_RANK)), full((Q_RANK, 1024)), full((KV_RANK, 1024)),
                  full((KV_RANK, 512)), rows(128), rows(128)],
        out_specs=[lat, rows(1024), rows(1024), rows(512), full((1, Q_RANK)), full((1, KV_RANK))],
        out_shape=[jax.ShapeDtypeStruct((SEQ, N_PAD), BF16), jax.ShapeDtypeStruct((SEQ, 1024), BF16),
                   jax.ShapeDtypeStruct((SEQ, 1024), BF16), jax.ShapeDtypeStruct((SEQ, 512), BF16),
                   jax.ShapeDtypeStruct((1, Q_RANK), F32), jax.ShapeDtypeStruct((1, KV_RANK), F32)],
        input_output_aliases={0: 0},
        compiler_params=_cparams(),
    )(dp_in, p, dq, dk, dv, qg, kvg, wuq, wk, wv, rc, rs)


FLASH_T = 1024


def _head_half(shape, hh):
    lane = lax.broadcasted_iota(jnp.int32, shape, 1)
    return (lane < 64) if hh == 0 else (lane >= 64)


def _diag_keep(nr, nk):
    row = lax.broadcasted_iota(jnp.int32, (nr, nk), 0)
    col = lax.broadcasted_iota(jnp.int32, (nr, nk), 1)
    return row + (nk - nr) >= col


def _tri_steps(nb, q_major):
    if q_major:
        pairs = [(i, kb) for i in range(nb) for kb in range(i + 1)]
    else:
        pairs = [(i, kb) for kb in range(nb) for i in range(kb, nb)]
    return jnp.asarray([p[0] for p in pairs], jnp.int32), jnp.asarray([p[1] for p in pairs], jnp.int32)


def _mla_flash_fwd(q, k, v):
    t = FLASH_T
    nb = SEQ // t
    qtab, ktab = _tri_steps(nb, True)

    def body(qi_ref, ki_ref, q_ref, k_ref, v_ref, o_ref, lse_ref, m_scr, l_scr, acc_scr):
        step = pl.program_id(1)
        i, kb = qi_ref[step], ki_ref[step]

        @pl.when(kb == 0)
        def _():
            m_scr[...] = jnp.full_like(m_scr, NEG)
            l_scr[...] = jnp.zeros_like(l_scr)
            acc_scr[...] = jnp.zeros_like(acc_scr)

        def update(r0, nr, nk, diagonal):
            rs = slice(r0, r0 + nr)
            vv = v_ref[0:nk, :]
            for hh in range(2):
                sl = slice(hh * 128, (hh + 1) * 128)
                s = _nt(q_ref[rs, sl], k_ref[0:nk, sl])
                if diagonal:
                    s = jnp.where(_diag_keep(nr, nk), s, NEG)
                m_prev = m_scr[hh, rs, :]
                m_new = jnp.maximum(m_prev, jnp.max(s, axis=-1, keepdims=True))
                pr = jnp.exp(s - jnp.tile(m_new, (1, nk // 128)))
                alpha = jnp.exp(m_prev - m_new)
                l_scr[hh, rs, :] = alpha * l_scr[hh, rs, :] + jnp.sum(pr, axis=-1, keepdims=True)
                acc_scr[hh, rs, :] = alpha * acc_scr[hh, rs, :] + _nn(pr.astype(BF16), vv)
                m_scr[hh, rs, :] = m_new

        @pl.when(kb < i)
        def _():
            update(0, t, t, False)

        @pl.when(kb == i)
        def _():
            update(0, t // 2, t // 2, True)
            update(t // 2, t // 2, t, True)
            o0 = acc_scr[0] / l_scr[0]
            o1 = acc_scr[1] / l_scr[1]
            o_ref[...] = jnp.where(_head_half((t, 128), 0), o0, o1)
            for hh in range(2):
                lse_ref[:, hh * 128:(hh + 1) * 128] = m_scr[hh] + jnp.log(l_scr[hh])

    grid_spec = pltpu.PrefetchScalarGridSpec(
        num_scalar_prefetch=2, grid=(4, qtab.shape[0]),
        in_specs=[pl.BlockSpec((t, 256), lambda j, s, qi, ki: (qi[s], j)),
                  pl.BlockSpec((t, 256), lambda j, s, qi, ki: (ki[s], j)),
                  pl.BlockSpec((t, 128), lambda j, s, qi, ki: (ki[s], j))],
        out_specs=[pl.BlockSpec((t, 128), lambda j, s, qi, ki: (qi[s], j)),
                   pl.BlockSpec((t, 256), lambda j, s, qi, ki: (qi[s], j))],
        scratch_shapes=[pltpu.VMEM((2, t, 128), F32), pltpu.VMEM((2, t, 128), F32), pltpu.VMEM((2, t, 128), F32)])
    return pl.pallas_call(
        body, name="mla_flash_fwd", grid_spec=grid_spec,
        out_shape=[jax.ShapeDtypeStruct((SEQ, 512), F32), jax.ShapeDtypeStruct((SEQ, 1024), F32)],
        compiler_params=_cparams(),
    )(qtab, ktab, q, k, v)


def _mla_flash_bwd(q, k, v, o, do, lse, token=None):
    t = FLASH_T
    nb = SEQ // t
    qtab, ktab = _tri_steps(nb, False)
    after, after_specs = _after(token)

    def body(qi_ref, ki_ref, q_ref, k_ref, v_ref, o_ref, do_ref, lse_ref, *rest):
        dq_ref, dk_ref, dv_ref, dk_scr, dv_scr = rest[-5:]
        step = pl.program_id(1)
        i, kb = qi_ref[step], ki_ref[step]

        @pl.when(step == 0)
        def _():
            dq_ref[...] = jnp.zeros_like(dq_ref)

        @pl.when(i == kb)
        def _():
            dk_scr[...] = jnp.zeros_like(dk_scr)
            dv_scr[...] = jnp.zeros_like(dv_scr)

        def update(r0, nr, nk, diagonal):
            rs = slice(r0, r0 + nr)
            vv = v_ref[0:nk, :]
            ov = o_ref[rs, :]
            dov = do_ref[rs, :]
            rows = pl.ds(pl.multiple_of(i * t + r0, t // 2), nr)
            for hh in range(2):
                sl = slice(hh * 128, (hh + 1) * 128)
                qh, kh = q_ref[rs, sl], k_ref[0:nk, sl]
                s = _nt(qh, kh)
                if diagonal:
                    s = jnp.where(_diag_keep(nr, nk), s, NEG)
                pr = jnp.exp(s - jnp.tile(lse_ref[rs, sl], (1, nk // 128)))
                dom = jnp.where(_head_half((nr, 128), hh), dov, 0.0)
                domb = dom.astype(BF16)
                dv_scr[0:nk, :] += _tn(pr.astype(BF16), domb)
                dpr = _nt(domb, vv)
                delta = jnp.sum(dom * ov, axis=-1, keepdims=True)
                ds = (pr * (dpr - delta)).astype(BF16)
                dq_ref[rows, sl] += _nn(ds, kh)
                dk_scr[hh, 0:nk, :] += _tn(ds, qh)

        @pl.when(i > kb)
        def _():
            update(0, t, t, False)

        @pl.when(i == kb)
        def _():
            update(0, t // 2, t // 2, True)
            update(t // 2, t // 2, t, True)

        @pl.when(i == nb - 1)
        def _():
            dk_ref[:, 0:128] = dk_scr[0]
            dk_ref[:, 128:256] = dk_scr[1]
            dv_ref[...] = dv_scr[...]

    qi_map = lambda j, s, qi, ki: (qi[s], j)
    ki_map = lambda j, s, qi, ki: (ki[s], j)
    grid_spec = pltpu.PrefetchScalarGridSpec(
        num_scalar_prefetch=2, grid=(4, qtab.shape[0]),
        in_specs=[pl.BlockSpec((t, 256), qi_map), pl.BlockSpec((t, 256), ki_map), pl.BlockSpec((t, 128), ki_map),
                  pl.BlockSpec((t, 128), qi_map), pl.BlockSpec((t, 128), qi_map), pl.BlockSpec((t, 256), qi_map)]
        + after_specs,
        out_specs=[pl.BlockSpec((SEQ, 256), lambda j, s, qi, ki: (0, j)), pl.BlockSpec((t, 256), ki_map),
                   pl.BlockSpec((t, 128), ki_map)],
        scratch_shapes=[pltpu.VMEM((2, t, 128), F32), pltpu.VMEM((t, 128), F32)])
    return pl.pallas_call(
        body, name="mla_flash_bwd", grid_spec=grid_spec,
        out_shape=[jax.ShapeDtypeStruct((SEQ, 1024), F32), jax.ShapeDtypeStruct((SEQ, 1024), F32),
                   jax.ShapeDtypeStruct((SEQ, 512), F32)],
        compiler_params=_cparams(),
    )(qtab, ktab, q, k, v, o, do, lse, *after)


DIL_UNROLL = 4


def _strided(start, size, d):
    return pl.ds(start, size) if d == 1 else pl.ds(start, size, stride=d)


def _dil_prep_fwd(p, rc, rs, g):
    d = DIL_DILATIONS[g]
    sub_len = SEQ // d
    ch = min(sub_len, 512)

    def body(p_ref, c_ref, s_ref, o_ref, x_scr):
        tq = pl.program_id(0)
        lanes = _rope_lanes((ch, 128), DIL_ROPE_HALF, 64, 0)
        o_ref[0, 0:BAND, :] = jnp.zeros((BAND, 128), BF16)

        @pl.when(tq < 2)
        def _():
            mult = jnp.where(tq == 0, DIL_SCALE, 1.0).astype(F32)
            for c0 in range(0, SEQ, ch):
                rows = pl.ds(c0, ch)
                x_scr[rows, :] = _rope_fwd(p_ref[rows, :], c_ref[rows, :] * mult, s_ref[rows, :] * mult, DIL_ROPE_HALF, lanes)

        def gather(src):
            for r in range(d):
                for c0 in range(0, sub_len, ch):
                    at = BAND + r * sub_len + c0
                    o_ref[0, at:at + ch, :] = src[_strided(r + c0 * d, ch, d), :].astype(BF16)

        @pl.when(tq < 2)
        def _():
            gather(x_scr)

        @pl.when(tq == 2)
        def _():
            gather(p_ref)

    tab = pl.BlockSpec((SEQ, 128), lambda tq, pr: (0, 0))
    return pl.pallas_call(
        body, name=f"dil_prep_fwd_g{g}", grid=(3, 4),
        in_specs=[pl.BlockSpec((SEQ, 128), lambda tq, pr: (0, _qkv_block(tq, g, pr))), tab, tab],
        out_specs=pl.BlockSpec((1, BAND + SEQ, 128), lambda tq, pr: (tq, 0, pr)),
        out_shape=jax.ShapeDtypeStruct((3, BAND + SEQ, 512), BF16),
        scratch_shapes=[pltpu.VMEM((SEQ, 128), F32)],
        compiler_params=_cparams(),
    )(p, rc, rs)


DIL_ST_FWD, DIL_ST_BWD = 1024, 2048


def _band_keep(g, b, t, nb):
    nbs = SEQ // DIL_DILATIONS[g] // BAND
    row = lax.broadcasted_iota(jnp.int32, (BAND, 2 * BAND), 0)
    col = lax.broadcasted_iota(jnp.int32, (BAND, 2 * BAND), 1)
    cur = (col >= BAND) & (row >= col - BAND)
    prev = (col < BAND) & (col >= row)
    if nbs >= nb:
        if b > 0:
            return cur | prev
        return cur | (prev & ((t * nb) % nbs != 0))
    return cur | prev if b % nbs else cur


def _dil_tok(g, b, t, nb):
    d = DIL_DILATIONS[g]
    nbs = SEQ // d // BAND
    gb = t * nb + b
    return _strided((gb % nbs) * BAND * d + gb // nbs, BAND, d)


def _dil_attn_fwd2(qkv, g):
    DIL_ST, DIL_NB = DIL_ST_FWD, DIL_ST_FWD // BAND

    def body(q_ref, k_ref, v_ref, o_ref, l_ref, s_scr, p_scr, o_scr):
        t = pl.program_id(1)
        base = t * DIL_ST
        half0 = _head_half((DIL_ST, 128), 0)
        lse_h = []
        for hh in range(2):
            half = _head_half((BAND, 128), hh)
            for b in range(DIL_NB):
                qv = q_ref[0, pl.ds(pl.multiple_of(base + (b + 1) * BAND, BAND), BAND), :]
                k2 = k_ref[0, pl.ds(pl.multiple_of(base + b * BAND, BAND), 2 * BAND), :]
                sb = _nt(jnp.where(half, qv, jnp.zeros_like(qv)), k2)
                s_scr[b * BAND:(b + 1) * BAND, :] = jnp.where(_band_keep(g, b, t, DIL_NB), sb, NEG)
            s = s_scr[...]
            m = jnp.max(s, axis=-1, keepdims=True)
            pr = jnp.exp(s - m)
            den = jnp.sum(pr, axis=-1, keepdims=True)
            p_scr[...] = pr.astype(BF16)
            for b in range(DIL_NB):
                v2 = v_ref[0, pl.ds(pl.multiple_of(base + b * BAND, BAND), 2 * BAND), :]
                o_scr[hh, b * BAND:(b + 1) * BAND, :] = _nn(p_scr[b * BAND:(b + 1) * BAND, :], v2)
            o_scr[hh] = o_scr[hh] / den
            lse_h.append(m + jnp.log(den))
        out = jnp.where(half0, o_scr[0], o_scr[1])
        lse = jnp.where(half0, lse_h[0], lse_h[1])
        for b in range(DIL_NB):
            tok = _dil_tok(g, b, t, DIL_NB)
            o_ref[tok, :] = out[b * BAND:(b + 1) * BAND, :]
            l_ref[tok, :] = lse[b * BAND:(b + 1) * BAND, :]

    def inp(tq):
        return pl.BlockSpec((1, BAND + SEQ, 128), lambda pr, t: (tq, 0, pr))

    out = pl.BlockSpec((SEQ, 128), lambda pr, t: (0, pr))
    return pl.pallas_call(
        body, name=f"dil_attn_fwd_g{g}", grid=(4, SEQ // DIL_ST),
        in_specs=[inp(0), inp(1), inp(2)], out_specs=[out, out],
        out_shape=[jax.ShapeDtypeStruct((SEQ, 512), F32), jax.ShapeDtypeStruct((SEQ, 512), F32)],
        scratch_shapes=[pltpu.VMEM((DIL_ST, 2 * BAND), F32), pltpu.VMEM((DIL_ST, 2 * BAND), BF16),
                        pltpu.VMEM((2, DIL_ST, 128), F32)],
        compiler_params=_cparams(),
    )(qkv, qkv, qkv)


def _dil_attn_bwd2(dp_in, qkv, dyd, yd, lse_all, rc, rs, g, token=None):
    d = DIL_DILATIONS[g]
    sub_len = SEQ // d
    DIL_ST, DIL_NB = DIL_ST_BWD, DIL_ST_BWD // BAND
    nst = SEQ // DIL_ST
    after, after_specs = _after(token)
    ch = 512

    def body(dp_any, q_ref, k_ref, v_ref, do_ref, y_ref, l_ref, c_ref, sn_ref, *rest):
        dp_ref, tok_scr, dk_scr, dv_scr, s_scr, dp_scr, p_scr, ds_scr, do_scr, y_scr, l_scr, dq_scr = rest[-12:]
        del dp_any
        t = pl.program_id(1)
        base = t * DIL_ST

        @pl.when(t == 0)
        def _():
            dk_scr[...] = jnp.zeros_like(dk_scr)
            dv_scr[...] = jnp.zeros_like(dv_scr)

        for b in range(DIL_NB):
            tok = _dil_tok(g, b, t, DIL_NB)
            do_scr[b * BAND:(b + 1) * BAND, :] = do_ref[tok, :]
            y_scr[b * BAND:(b + 1) * BAND, :] = y_ref[tok, :]
            l_scr[b * BAND:(b + 1) * BAND, :] = l_ref[tok, :]
        for hh in range(2):
            half = _head_half((BAND, 128), hh)
            half_st = _head_half((DIL_ST, 128), hh)
            dom = jnp.where(half_st, do_scr[...], 0.0)
            delta = jnp.sum(dom * y_scr[...], axis=-1, keepdims=True)
            lcol = jnp.max(jnp.where(half_st, l_scr[...], NEG), axis=-1, keepdims=True)
            for b in range(DIL_NB):
                rows = slice(b * BAND, (b + 1) * BAND)
                qv = q_ref[0, pl.ds(pl.multiple_of(base + (b + 1) * BAND, BAND), BAND), :]
                band = pl.ds(pl.multiple_of(base + b * BAND, BAND), 2 * BAND)
                sb = _nt(jnp.where(half, qv, jnp.zeros_like(qv)), k_ref[0, band, :])
                s_scr[rows, :] = jnp.where(_band_keep(g, b, t, DIL_NB), sb, NEG)
                dp_scr[rows, :] = _nt(dom[rows, :].astype(BF16), v_ref[0, band, :])
            pr = jnp.exp(s_scr[...] - lcol)
            p_scr[...] = pr.astype(BF16)
            ds_scr[...] = (pr * (dp_scr[...] - delta)).astype(BF16)
            for b in range(DIL_NB):
                rows = slice(b * BAND, (b + 1) * BAND)
                qv = q_ref[0, pl.ds(pl.multiple_of(base + (b + 1) * BAND, BAND), BAND), :]
                band = pl.ds(pl.multiple_of(base + b * BAND, BAND), 2 * BAND)
                dqb = jnp.where(half, _nn(ds_scr[rows, :], k_ref[0, band, :]), 0.0)
                if hh == 0:
                    dq_scr[rows, :] = dqb
                else:
                    dq_scr[rows, :] += dqb
                half2 = _head_half((2 * BAND, 128), hh)
                dk_scr[band, :] += jnp.where(half2, _tn(ds_scr[rows, :], qv), 0.0)
                dv_scr[band, :] += _tn(p_scr[rows, :], dom[rows, :].astype(BF16))
        for b in range(DIL_NB):
            tok_scr[pl.ds(0, 1), _dil_tok(g, b, t, DIL_NB), :] = dq_scr[b * BAND:(b + 1) * BAND, :][None]

        @pl.when(t == nst - 1)
        def _():
            for r in range(d):
                rows = _strided(r, sub_len, d)
                tok_scr[pl.ds(1, 1), rows, :] = dk_scr[BAND + r * sub_len:BAND + (r + 1) * sub_len, :][None]
                tok_scr[pl.ds(2, 1), rows, :] = dv_scr[BAND + r * sub_len:BAND + (r + 1) * sub_len, :][None]
            lanes = _rope_lanes((ch, 128), DIL_ROPE_HALF, 64, 0)
            for c0 in range(0, SEQ, ch):
                rows = slice(c0, c0 + ch)
                cv, sv = c_ref[rows, :], sn_ref[rows, :]
                dp_ref[rows, 0:128] = _rope_bwd(tok_scr[0, rows, :], cv * DIL_SCALE, sv * DIL_SCALE, DIL_ROPE_HALF, lanes).astype(BF16)
                dp_ref[rows, 128:256] = _rope_bwd(tok_scr[1, rows, :], cv, sv, DIL_ROPE_HALF, lanes).astype(BF16)
                dp_ref[rows, 256:384] = tok_scr[2, rows, :].astype(BF16)

    def inp(tq):
        return pl.BlockSpec((1, BAND + SEQ, 128), lambda pr, t: (tq, 0, pr))

    tok_spec = pl.BlockSpec((SEQ, 128), lambda pr, t: (0, pr))
    tab = pl.BlockSpec((SEQ, 128), lambda pr, t: (0, 0))
    st = (DIL_ST, 2 * BAND)
    return pl.pallas_call(
        body, name=f"dil_attn_bwd_g{g}", grid=(4, nst),
        in_specs=[pl.BlockSpec(memory_space=pl.ANY), inp(0), inp(1), inp(2), tok_spec, tok_spec, tok_spec, tab, tab]
        + after_specs,
        out_specs=pl.BlockSpec((SEQ, 384), lambda pr, t: (0, _qkv_block(0, g, pr) // 3)),
        out_shape=jax.ShapeDtypeStruct((SEQ, N_PAD), BF16),
        input_output_aliases={0: 0},
        scratch_shapes=[pltpu.VMEM((3, SEQ, 128), F32),
                        pltpu.VMEM((BAND + SEQ, 128), F32), pltpu.VMEM((BAND + SEQ, 128), F32),
                        pltpu.VMEM(st, F32), pltpu.VMEM(st, F32), pltpu.VMEM(st, BF16), pltpu.VMEM(st, BF16),
                        pltpu.VMEM((DIL_ST, 128), F32), pltpu.VMEM((DIL_ST, 128), F32), pltpu.VMEM((DIL_ST, 128), F32),
                        pltpu.VMEM((DIL_ST, 128), F32)],
        compiler_params=_cparams(),
    )(dp_in, qkv, qkv, qkv, dyd, yd, lse_all, rc, rs, *after)


def _band_masks():
    row = lax.broadcasted_iota(jnp.int32, (BAND, BAND), 0)
    col = lax.broadcasted_iota(jnp.int32, (BAND, BAND), 1)
    return row >= col, col >= row


def _dil_attn_fwd(qkv, g):
    d = DIL_DILATIONS[g]
    nbs = SEQ // d // BAND
    nblk = SEQ // BAND

    def body(q_ref, k_ref, v_ref, o_ref, l_ref):
        keep_c, keep_p = _band_masks()
        half0 = _head_half((BAND, 128), 0)

        def step(i, carry):
            cur = pl.ds(pl.multiple_of(i * BAND, BAND), BAND)
            prv = pl.ds(pl.multiple_of(jnp.maximum(i - 1, 0) * BAND, BAND), BAND)
            has_prev = (i % nbs) != 0
            qv = q_ref[0, cur, :]
            kc, kp = k_ref[0, cur, :], k_ref[0, prv, :]
            vc, vp = v_ref[0, cur, :], v_ref[0, prv, :]
            outs, lses = [], []
            for hh in range(2):
                qm = jnp.where(_head_half((BAND, 128), hh), qv, jnp.zeros_like(qv))
                sc = jnp.where(keep_c, _nt(qm, kc), NEG)
                sp = jnp.where(keep_p & has_prev, _nt(qm, kp), NEG)
                m = jnp.maximum(jnp.max(sc, axis=-1, keepdims=True), jnp.max(sp, axis=-1, keepdims=True))
                pc, pp = jnp.exp(sc - m), jnp.exp(sp - m)
                den = jnp.sum(pc, axis=-1, keepdims=True) + jnp.sum(pp, axis=-1, keepdims=True)
                o = (_nn(pc.astype(BF16), vc) + _nn(pp.astype(BF16), vp)) / den
                outs.append(o)
                lses.append(jnp.broadcast_to(m + jnp.log(den), (BAND, 128)))
            tok = _strided((i % nbs) * BAND * d + i // nbs, BAND, d)
            o_ref[tok, :] = jnp.where(half0, outs[0], outs[1])
            l_ref[tok, :] = jnp.where(half0, lses[0], lses[1])
            return carry

        lax.fori_loop(0, nblk, step, 0, unroll=DIL_UNROLL)

    def inp(tq):
        return pl.BlockSpec((1, SEQ, 128), lambda pr: (tq, 0, pr))

    out = pl.BlockSpec((SEQ, 128), lambda pr: (0, pr))
    return pl.pallas_call(
        body, name=f"dil_attn_fwd_g{g}", grid=(4,),
        in_specs=[inp(0), inp(1), inp(2)], out_specs=[out, out],
        out_shape=[jax.ShapeDtypeStruct((SEQ, 512), F32), jax.ShapeDtypeStruct((SEQ, 512), F32)],
        compiler_params=_cparams(),
    )(qkv, qkv, qkv)


def _dil_attn_bwd(qkv, dyd, yd, lse_all, g):
    d = DIL_DILATIONS[g]
    sub_len = SEQ // d
    nbs = sub_len // BAND
    nblk = SEQ // BAND

    def body(q_ref, k_ref, v_ref, do_ref, y_ref, l_ref, out_ref, dk_scr, dv_scr):
        keep_c, keep_p = _band_masks()
        dk_scr[...] = jnp.zeros_like(dk_scr)
        dv_scr[...] = jnp.zeros_like(dv_scr)

        def step(i, carry):
            cur = pl.ds(pl.multiple_of(i * BAND, BAND), BAND)
            prv = pl.ds(pl.multiple_of(jnp.maximum(i - 1, 0) * BAND, BAND), BAND)
            has_prev = (i % nbs) != 0
            tok = _strided((i % nbs) * BAND * d + i // nbs, BAND, d)
            qv = q_ref[0, cur, :]
            kc, kp = k_ref[0, cur, :], k_ref[0, prv, :]
            vc, vp = v_ref[0, cur, :], v_ref[0, prv, :]
            dov, yv, lv = do_ref[tok, :], y_ref[tok, :], l_ref[tok, :]
            dq = jnp.zeros((BAND, 128), F32)
            dkc = jnp.zeros((BAND, 128), F32)
            dkp = jnp.zeros((BAND, 128), F32)
            dvc = jnp.zeros((BAND, 128), F32)
            dvp = jnp.zeros((BAND, 128), F32)
            for hh in range(2):
                half = _head_half((BAND, 128), hh)
                qm = jnp.where(half, qv, jnp.zeros_like(qv))
                lcol = jnp.max(jnp.where(half, lv, NEG), axis=-1, keepdims=True)
                pc = jnp.exp(jnp.where(keep_c, _nt(qm, kc), NEG) - lcol)
                pp = jnp.exp(jnp.where(keep_p & has_prev, _nt(qm, kp), NEG) - lcol)
                dom = jnp.where(half, dov, 0.0)
                domb = dom.astype(BF16)
                delta = jnp.sum(dom * yv, axis=-1, keepdims=True)
                dsc = (pc * (_nt(domb, vc) - delta)).astype(BF16)
                dsp = (pp * (_nt(domb, vp) - delta)).astype(BF16)
                dvc = dvc + _tn(pc.astype(BF16), domb)
                dvp = dvp + _tn(pp.astype(BF16), domb)
                dq = dq + jnp.where(half, _nn(dsc, kc) + _nn(dsp, kp), 0.0)
                dkc = dkc + jnp.where(half, _tn(dsc, qv), 0.0)
                dkp = dkp + jnp.where(half, _tn(dsp, qv), 0.0)
            out_ref[pl.ds(0, 1), tok, :] = dq[None]
            dk_scr[cur, :] += dkc
            dk_scr[prv, :] += dkp
            dv_scr[cur, :] += dvc
            dv_scr[prv, :] += dvp
            return carry

        lax.fori_loop(0, nblk, step, 0, unroll=DIL_UNROLL)
        for r in range(d):
            rows = _strided(r, sub_len, d)
            out_ref[pl.ds(1, 1), rows, :] = dk_scr[r * sub_len:(r + 1) * sub_len, :][None]
            out_ref[pl.ds(2, 1), rows, :] = dv_scr[r * sub_len:(r + 1) * sub_len, :][None]

    def inp(tq):
        return pl.BlockSpec((1, SEQ, 128), lambda pr: (tq, 0, pr))

    tok_spec = pl.BlockSpec((SEQ, 128), lambda pr: (0, pr))
    return pl.pallas_call(
        body, name=f"dil_attn_bwd_g{g}", grid=(4,),
        in_specs=[inp(0), inp(1), inp(2), tok_spec, tok_spec, tok_spec],
        out_specs=pl.BlockSpec((3, SEQ, 128), lambda pr: (0, 0, pr)),
        out_shape=jax.ShapeDtypeStruct((3, SEQ, 512), F32),
        scratch_shapes=[pltpu.VMEM((SEQ, 128), F32), pltpu.VMEM((SEQ, 128), F32)],
        compiler_params=_cparams(),
    )(qkv, qkv, qkv, dyd, yd, lse_all)


def _dil_prep_bwd(dp_in, dqkv, rc, rs, g):
    tm = 1024

    def body(dp_any, g_ref, c_ref, s_ref, dp_ref):
        del dp_any
        tq = pl.program_id(0)

        @pl.when(tq == 2)
        def _():
            dp_ref[...] = g_ref[0].astype(BF16)

        @pl.when(tq < 2)
        def _():
            mult = jnp.where(tq == 0, DIL_SCALE, 1.0).astype(F32)
            lanes = _rope_lanes((tm, 128), DIL_ROPE_HALF, 64, 0)
            cv, sv = c_ref[...], s_ref[...] * mult
            cv = cv * mult
            for pr in range(4):
                gv = g_ref[0, :, pr * 128:(pr + 1) * 128]
                dp_ref[:, pr * 128:(pr + 1) * 128] = _rope_bwd(gv, cv, sv, DIL_ROPE_HALF, lanes).astype(BF16)

    tab = pl.BlockSpec((tm, 128), lambda tq, i: (i, 0))
    return pl.pallas_call(
        body, name=f"dil_prep_bwd_g{g}", grid=(3, SEQ // tm),
        in_specs=[pl.BlockSpec(memory_space=pl.ANY),
                  pl.BlockSpec((1, tm, 512), lambda tq, i: (tq, i, 0)), tab, tab],
        out_specs=pl.BlockSpec((tm, 512), lambda tq, i: (i, COL_QKV // 512 + tq * 3 + g)),
        out_shape=jax.ShapeDtypeStruct((SEQ, N_PAD), BF16),
        input_output_aliases={0: 0},
    )(dp_in, dqkv, rc, rs)


TAIL_T = 256


def _tail(p, ya, o_g, l_g, x, target, wpm, wpd, wout, post_g):
    tm = TAIL_T

    def body(pgz_ref, ya_ref, o0_ref, o1_ref, o2_ref, l0_ref, l1_ref, l2_ref, x_ref, t_ref,
             wpm_ref, wpd_ref, wout_ref, pg_ref,
             dp_ref, dy_ref, mg_ref, dt_ref, ua_ref, dpa_ref, ud_ref, dpd_ref, dya_ref, dyd_ref,
             yd_ref, lse_ref, loss_ref, dgp_ref):
        l0, l1, l2 = l0_ref[...], l1_ref[...], l2_ref[...]
        mx = jnp.maximum(jnp.maximum(l0, l1), l2)
        e0, e1, e2 = jnp.exp(l0 - mx), jnp.exp(l1 - mx), jnp.exp(l2 - mx)
        den = e0 + e1 + e2
        yd = (e0 * o0_ref[...] + e1 * o1_ref[...] + e2 * o2_ref[...]) / den
        yd_ref[...] = yd
        lse_ref[...] = mx + jnp.log(den)
        ya = ya_ref[...]

        gm, gd = pgz_ref[:, 0:1024], pgz_ref[:, 1024:2048]
        zm, zd = pgz_ref[:, 2048:2560], pgz_ref[:, 2560:3072]
        szm, szd = _sigmoid(zm), _sigmoid(zd)
        sm, sd = zm * szm, zd * szd
        ua = (ya * sm).astype(BF16)
        ud = (yd * sd).astype(BF16)
        ua_ref[...] = ua
        ud_ref[...] = ud
        pa = _nn(ua, wpm_ref[...])
        pd = _nn(ud, wpd_ref[...])
        sgm, sgd = _sigmoid(gm), _sigmoid(gd)
        mg = (sgm * pa + sgd * pd).astype(BF16)
        mg_ref[...] = mg
        t = _nn(mg, wout_ref[...])
        r3 = lax.rsqrt(jnp.mean(t * t, axis=-1, keepdims=True) + EPS)
        n = t * r3
        pg = pg_ref[...]
        err = x_ref[...] + n * pg - t_ref[...]
        lpart = jnp.sum(err * err, axis=0, keepdims=True)

        dy = err * (1.0 / D_MODEL)
        dy_ref[...] = dy
        gpart = jnp.sum(dy * n, axis=0, keepdims=True)
        dn = dy * pg
        dt = (r3 * (dn - n * jnp.mean(dn * n, axis=-1, keepdims=True))).astype(BF16)
        dt_ref[...] = dt
        dmg = _nt(dt, wout_ref[...])
        dpa = (dmg * sgm).astype(BF16)
        dpd = (dmg * sgd).astype(BF16)
        dpa_ref[...] = dpa
        dpd_ref[...] = dpd
        dp_ref[:, 0:1024] = (dmg * pa * sgm * (1.0 - sgm)).astype(BF16)
        dp_ref[:, 1024:2048] = (dmg * pd * sgd * (1.0 - sgd)).astype(BF16)
        dua = _nt(dpa, wpm_ref[...])
        dud = _nt(dpd, wpd_ref[...])
        dya_ref[...] = dua * sm
        dyd_ref[...] = dud * sd
        dp_ref[:, 2048:2560] = (dua * ya * szm * (1.0 + zm * (1.0 - szm))).astype(BF16)
        dp_ref[:, 2560:3072] = (dud * yd * szd * (1.0 + zd * (1.0 - szd))).astype(BF16)

        @pl.when(pl.program_id(0) == 0)
        def _():
            loss_ref[...] = lpart
            dgp_ref[...] = gpart

        @pl.when(pl.program_id(0) > 0)
        def _():
            loss_ref[...] += lpart
            dgp_ref[...] += gpart

    def rows(w):
        return pl.BlockSpec((tm, w), lambda i: (i, 0))

    def full(shape):
        return pl.BlockSpec(shape, lambda i: (0, 0))

    def sds(w, dt):
        return jax.ShapeDtypeStruct((SEQ, w), dt)

    return pl.pallas_call(
        body, name="tail", grid=(SEQ // tm,),
        in_specs=[rows(3072), rows(512), rows(512), rows(512), rows(512), rows(512), rows(512), rows(512),
                  rows(1024), rows(1024), full((512, 1024)), full((512, 1024)), full((1024, 1024)), full((1, 1024))],
        out_specs=[rows(3072), rows(1024), rows(1024), rows(1024), rows(512), rows(1024), rows(512), rows(1024),
                   rows(512), rows(512), rows(512), rows(512), full((1, 1024)), full((1, 1024))],
        out_shape=[sds(N_PAD, BF16), sds(1024, F32), sds(1024, BF16), sds(1024, BF16), sds(512, BF16),
                   sds(1024, BF16), sds(512, BF16), sds(1024, BF16), sds(512, F32), sds(512, F32),
                   sds(512, F32), sds(512, F32),
                   jax.ShapeDtypeStruct((1, 1024), F32), jax.ShapeDtypeStruct((1, 1024), F32)],
        compiler_params=_cparams(),
    )(p, ya, o_g[0], o_g[1], o_g[2], l_g[0], l_g[1], l_g[2], x, target, wpm, wpd, wout, post_g)


def _sum_parts(recv, own, me, tr, name):
    n, r, w = recv.shape
    if r % tr:
        return _sum_parts_cols(recv, own, me, name)
    own_spec = (pl.BlockSpec((tr, w), lambda i, me_ref: (i, 0)) if own.ndim == 2
                else pl.BlockSpec((None, tr, w), lambda i, me_ref: (me_ref[0], i, 0)))

    def body(me_ref, p_ref, own_ref, o_ref):
        mine = own_ref[...].astype(F32)
        acc = jnp.zeros((tr, w), F32)
        for s in range(n):
            acc = acc + jnp.where(me_ref[0] == s, mine, p_ref[s].astype(F32))
        o_ref[...] = acc

    return pl.pallas_call(
        body, name=name,
        grid_spec=pltpu.PrefetchScalarGridSpec(
            num_scalar_prefetch=1, grid=(r // tr,),
            in_specs=[pl.BlockSpec((n, tr, w), lambda i, me_ref: (0, i, 0)), own_spec],
            out_specs=pl.BlockSpec((tr, w), lambda i, me_ref: (i, 0))),
        out_shape=jax.ShapeDtypeStruct((r, w), F32),
    )(me.reshape(1), recv, own)


def _sum_parts_cols(recv, own, me, name):
    n, r, w = recv.shape
    tc = 128

    def body(me_ref, p_ref, own_ref, o_ref):
        mine = own_ref[...].astype(F32)
        acc = jnp.zeros((r, tc), F32)
        for s in range(n):
            acc = acc + jnp.where(me_ref[0] == s, mine, p_ref[s].astype(F32))
        o_ref[...] = acc

    return pl.pallas_call(
        body, name=name,
        grid_spec=pltpu.PrefetchScalarGridSpec(
            num_scalar_prefetch=1, grid=(w // tc,),
            in_specs=[pl.BlockSpec((n, r, tc), lambda i, me_ref: (0, 0, i)),
                      pl.BlockSpec((None, r, tc), lambda i, me_ref: (me_ref[0], 0, i))],
            out_specs=pl.BlockSpec((r, tc), lambda i, me_ref: (0, i))),
        out_shape=jax.ShapeDtypeStruct((r, w), F32),
    )(me.reshape(1), recv, own)


def _adamw(w, g, m, v, name):
    lead = w.shape[:-2]
    r, c = w.shape[-2:]
    tr = max([t for t in range(8, 257, 8) if r % t == 0], default=r)
    c1 = 1.0 - ADAM_B1 ** ADAM_STEP
    c2 = 1.0 - ADAM_B2 ** ADAM_STEP

    def body(w_ref, g_ref, m_ref, v_ref, d_ref, nm_ref, nv_ref):
        gv = g_ref[...]
        nm = ADAM_B1 * m_ref[...] + (1.0 - ADAM_B1) * gv
        nv = ADAM_B2 * v_ref[...] + (1.0 - ADAM_B2) * (gv * gv)
        nm_ref[...] = nm
        nv_ref[...] = nv
        d_ref[...] = -ADAM_LR * ((nm / c1) / (jnp.sqrt(nv / c2) + ADAM_EPS) + ADAM_WD * w_ref[...])

    zeros = (0,) * len(lead)
    spec = pl.BlockSpec((1,) * len(lead) + (tr, c), lambda i: zeros + (i, 0))
    sd = jax.ShapeDtypeStruct(w.shape, F32)
    return pl.pallas_call(
        body, name=name, grid=(r // tr,),
        in_specs=[spec] * 4, out_specs=[spec] * 3, out_shape=[sd] * 3,
    )(w, g, m, v)


def _adamw_in(w_t, m_t, v_t, own_half, swapped, core):
    r, c = SHARD_SHAPES[0]
    tr = max(t for t in range(8, 257, 8) if r % t == 0)
    c1 = 1.0 - ADAM_B1 ** ADAM_STEP
    c2 = 1.0 - ADAM_B2 ** ADAM_STEP

    def body(core_ref, w_ref, m_ref, v_ref, own_ref, sw_ref, d_ref, nm_ref, nv_ref, g_ref):
        own = own_ref[...]
        col_half = lax.broadcasted_iota(jnp.int32, (tr, c), 1) // (c // 2)
        gv = jnp.where(col_half == core_ref[0], jnp.concatenate([own, own], axis=1), sw_ref[...])
        g_ref[0] = gv
        nm = ADAM_B1 * m_ref[0] + (1.0 - ADAM_B1) * gv
        nv = ADAM_B2 * v_ref[0] + (1.0 - ADAM_B2) * (gv * gv)
        nm_ref[0] = nm
        nv_ref[0] = nv
        d_ref[0] = -ADAM_LR * ((nm / c1) / (jnp.sqrt(nv / c2) + ADAM_EPS) + ADAM_WD * w_ref[0])

    full = pl.BlockSpec((1, tr, c), lambda i, core_ref: (0, i, 0))
    sd = jax.ShapeDtypeStruct((1, r, c), F32)
    return pl.pallas_call(
        body, name="adamw_in",
        grid_spec=pltpu.PrefetchScalarGridSpec(
            num_scalar_prefetch=1, grid=(r // tr,),
            in_specs=[full, full, full, pl.BlockSpec((tr, c // 2), lambda i, core_ref: (i, 0)),
                      pl.BlockSpec((tr, c), lambda i, core_ref: (i, 0))],
            out_specs=[full] * 4),
        out_shape=[sd] * 4,
    )(core.reshape(1), w_t, m_t, v_t, own_half, swapped)


ANY = pl.BlockSpec(memory_space=pl.ANY)


def _my_place():
    return lax.axis_index("x"), lax.axis_index("y"), lax.axis_index("c")


HBM = pl.BlockSpec(memory_space=pltpu.HBM)
SEM = pl.BlockSpec(memory_space=pltpu.SEMAPHORE)
DATAFLOW = pltpu.SideEffectType.DATAFLOW_SIDE_EFFECTING


def _near_chips(x, y):
    return [(1 - x, y), (x, 1 - y)]


def _half(mi, hc):
    r, c = SHARD_SHAPES[mi]
    if mi == 0:
        return pl.ds(0, r), pl.ds(pl.multiple_of(hc * (c // 2), 128), c // 2)
    return pl.ds(pl.multiple_of(hc * (r // 2), 16), r // 2), pl.ds(0, c)


def _gather_copies(m_refs, land_refs, send_sems, recv_sems):
    x, y, c = _my_place()
    out, back = [], []
    for mi in range(N_MATS):
        rows, cols = _half(mi, c)
        for j, (cx, cy) in enumerate(_near_chips(x, y)):
            sems = dict(send_sem=send_sems.at[mi * 2 + j], recv_sem=recv_sems.at[mi * 2 + j],
                        device_id=(cx, cy, c), device_id_type=MESH)
            out.append(pltpu.make_async_remote_copy(src_ref=m_refs[mi].at[rows, cols],
                                                    dst_ref=land_refs[mi].at[2 * x + y, rows, cols], **sems))
            got = land_refs[mi].at[2 * cx + cy, rows, cols]
            back.append(pltpu.make_async_remote_copy(src_ref=got, dst_ref=got, **sems))
    return out, back


def _gather_start(mats, landing):
    n = N_MATS

    def body(*refs):
        out, _ = _gather_copies(refs[:n], refs[n:2 * n], refs[2 * n], refs[2 * n + 1])
        for cp in out:
            cp.start()
        refs[-1][...] = jnp.zeros_like(refs[-1])

    hbm = [pltpu.HBM(a.shape, a.dtype) for a in list(mats) + list(landing)]
    outs = pl.pallas_call(
        body, name="gather_start",
        out_shape=(pltpu.SemaphoreType.DMA((2 * n,)), pltpu.SemaphoreType.DMA((2 * n,)), *hbm,
                   jax.ShapeDtypeStruct((8, 128), F32)),
        in_specs=[HBM] * (2 * n), out_specs=(SEM, SEM, *[HBM] * (2 * n), pl.BlockSpec(memory_space=pltpu.VMEM)),
        input_output_aliases={i: 2 + i for i in range(2 * n)},
        compiler_params=pltpu.CompilerParams(has_side_effects=DATAFLOW),
    )(*[pltpu.with_memory_space_constraint(a, pltpu.HBM) for a in list(mats) + list(landing)])
    return outs[:-1], outs[-1]


def _gather_wait(handle, after):
    n = N_MATS

    def body(*refs):
        out, back = _gather_copies(refs[:n], refs[n:2 * n], refs[2 * n], refs[2 * n + 1])
        for cp, arrival in zip(out, back):
            cp.wait_send()
            arrival.wait_recv()

    bufs = handle[2:]
    after, after_specs = _after(after)
    res = pl.pallas_call(
        body, name="gather_wait", out_shape=tuple(pltpu.HBM(b.shape, b.dtype) for b in bufs),
        in_specs=[HBM] * (2 * n) + [SEM, SEM] + after_specs, out_specs=tuple([HBM] * (2 * n)),
        input_output_aliases={i: i for i in range(2 * n)},
        compiler_params=pltpu.CompilerParams(has_side_effects=DATAFLOW),
    )(*bufs, handle[0], handle[1], *after)
    return list(res[n:])


def _relay_share(gathered):
    n = N_MATS

    def body(*refs):
        out_refs = refs[n:2 * n]
        send_sems, recv_sems = refs[2 * n:]
        x, y, c = _my_place()
        sibling = (x, y, 1 - c)
        relayed = 2 * (x ^ (1 - c)) + (y ^ c)
        relay_to = (x ^ c, y ^ (1 - c), c)
        far = 2 * (1 - x) + (1 - y)
        near = [2 * (1 - x) + y, 2 * x + (1 - y)]

        def copy(k, mi, shard, hc, to):
            blk = out_refs[mi].at[(shard,) + _half(mi, hc)]
            return pltpu.make_async_remote_copy(src_ref=blk, dst_ref=blk, send_sem=send_sems.at[mi * 4 + k],
                                                recv_sem=recv_sems.at[mi * 4 + k], device_id=to, device_id_type=MESH)

        sends = []
        for mi in range(n):
            sends.append(copy(0, mi, relayed, c, relay_to))
            sends += [copy(1 + j, mi, near[j], c, sibling) for j in range(2)]
        for cp in sends:
            cp.start()
        for mi in range(n):
            copy(0, mi, far, c, relay_to).wait_recv()
            cp = copy(3, mi, far, c, sibling)
            cp.start()
            sends.append(cp)
        for mi in range(n):
            for j in range(2):
                copy(1 + j, mi, near[j], 1 - c, sibling).wait_recv()
            copy(3, mi, far, 1 - c, sibling).wait_recv()
        for cp in sends:
            cp.wait_send()

    return pl.pallas_call(
        body, name="relay_share",
        in_specs=[ANY] * n, out_specs=[ANY] * n,
        out_shape=[jax.ShapeDtypeStruct(g.shape, g.dtype) for g in gathered],
        input_output_aliases={i: i for i in range(n)},
        scratch_shapes=[pltpu.SemaphoreType.DMA((4 * n,)), pltpu.SemaphoreType.DMA((4 * n,))],
    )(*gathered)


def _peers(x, y, c):
    out = []
    for k in range(1, 8):
        px, py, pc = x ^ (k >> 2), y ^ ((k >> 1) & 1), c ^ (k & 1)
        out.append((k - 1, (px, py, pc), 4 * px + 2 * py + pc))
    return out


def _exchange_start(parts, name):
    n = len(parts)

    def body(*refs):
        p_refs, land_refs = refs[:n], refs[n:2 * n]
        send_sems, recv_sems, token = refs[2 * n], refs[2 * n + 1], refs[-1]
        x, y, c = _my_place()
        me = 4 * x + 2 * y + c
        for k, dev, peer in _peers(x, y, c):
            for mi in range(n):
                pltpu.make_async_remote_copy(
                    src_ref=p_refs[mi].at[peer], dst_ref=land_refs[mi].at[me], send_sem=send_sems.at[k * n + mi],
                    recv_sem=recv_sems.at[k * n + mi], device_id=dev, device_id_type=MESH).start()
        token[...] = jnp.zeros_like(token)

    hbm = [pltpu.HBM(p.shape, p.dtype) for p in parts]
    outs = pl.pallas_call(
        body, name=name + "_start",
        out_shape=(pltpu.SemaphoreType.DMA((7 * n,)), pltpu.SemaphoreType.DMA((7 * n,)), *hbm, *hbm,
                   jax.ShapeDtypeStruct((8, 128), F32)),
        in_specs=[HBM] * (2 * n), out_specs=(SEM, SEM, *[HBM] * (2 * n), pl.BlockSpec(memory_space=pltpu.VMEM)),
        input_output_aliases={i: 2 + i for i in range(2 * n)},
        compiler_params=pltpu.CompilerParams(has_side_effects=DATAFLOW),
    )(*[pltpu.with_memory_space_constraint(p, pltpu.HBM) for p in parts],
      *[pltpu.with_memory_space_constraint(lax.empty(p.shape, p.dtype), pltpu.HBM) for p in parts])
    return (name, outs[:-1]), outs[-1]


def _exchange_wait(handle, after):
    name, outs = handle
    n = (len(outs) - 2) // 2

    def body(*refs):
        p_refs, land_refs = refs[:n], refs[n:2 * n]
        send_sems, recv_sems = refs[2 * n], refs[2 * n + 1]
        x, y, c = _my_place()
        me = 4 * x + 2 * y + c
        for k, dev, peer in _peers(x, y, c):
            for mi in range(n):
                pltpu.make_async_remote_copy(
                    src_ref=p_refs[mi].at[peer], dst_ref=land_refs[mi].at[me], send_sem=send_sems.at[k * n + mi],
                    recv_sem=recv_sems.at[k * n + mi], device_id=dev, device_id_type=MESH).wait_send()
                slot = land_refs[mi].at[peer]
                pltpu.make_async_remote_copy(
                    src_ref=slot, dst_ref=slot, send_sem=send_sems.at[k * n + mi],
                    recv_sem=recv_sems.at[k * n + mi], device_id=dev, device_id_type=MESH).wait_recv()

    bufs = outs[2:]
    res = pl.pallas_call(
        body, name=name + "_wait", out_shape=tuple(pltpu.HBM(b.shape, b.dtype) for b in bufs),
        in_specs=[HBM] * (2 * n) + [SEM, SEM, ANY], out_specs=tuple([HBM] * (2 * n)),
        input_output_aliases={i: i for i in range(2 * n)},
        compiler_params=pltpu.CompilerParams(has_side_effects=DATAFLOW),
    )(*bufs, outs[0], outs[1], after)
    return list(res[n:])


def _swap_halves(halves, gvec):
    def place(ref, mi, hc):
        return ref.at[:, pl.ds(pl.multiple_of(hc * 512, 128), 512)] if mi == 0 else ref.at[hc]

    def body(*refs):
        g_refs, gv_ref = refs[:N_MATS], refs[N_MATS]
        out_refs, rg_ref = refs[N_MATS + 1:2 * N_MATS + 1], refs[2 * N_MATS + 1]
        send_sems, recv_sems = refs[2 * N_MATS + 2:]
        x, y, c = _my_place()
        me = 4 * x + 2 * y + c
        sends = []
        for mi in range(N_MATS):
            cp = pltpu.make_async_remote_copy(src_ref=g_refs[mi], dst_ref=place(out_refs[mi], mi, c), send_sem=send_sems.at[mi],
                                              recv_sem=recv_sems.at[mi], device_id=(x, y, 1 - c), device_id_type=MESH)
            cp.start()
            sends.append(cp)
        for k, dev, peer in _peers(x, y, c):
            cp = pltpu.make_async_remote_copy(src_ref=gv_ref, dst_ref=rg_ref.at[me], send_sem=send_sems.at[N_MATS + k],
                                              recv_sem=recv_sems.at[N_MATS + k], device_id=dev, device_id_type=MESH)
            cp.start()
            sends.append(cp)
        for mi in range(N_MATS):
            got = place(out_refs[mi], mi, 1 - c)
            pltpu.make_async_remote_copy(src_ref=got, dst_ref=got, send_sem=send_sems.at[mi], recv_sem=recv_sems.at[mi],
                                         device_id=(x, y, 1 - c), device_id_type=MESH).wait_recv()
        for k, dev, peer in _peers(x, y, c):
            got = rg_ref.at[peer]
            pltpu.make_async_remote_copy(src_ref=got, dst_ref=got, send_sem=send_sems.at[N_MATS + k],
                                         recv_sem=recv_sems.at[N_MATS + k], device_id=dev, device_id_type=MESH).wait_recv()
        for cp in sends:
            cp.wait_send()

    outs = pl.pallas_call(
        body, name="swap_halves",
        in_specs=[ANY] * (N_MATS + 1), out_specs=[ANY] * (N_MATS + 1),
        out_shape=[jax.ShapeDtypeStruct(SHARD_SHAPES[0], F32)]
        + [jax.ShapeDtypeStruct((2, r // 2, c), F32) for r, c in SHARD_SHAPES[1:]]
        + [jax.ShapeDtypeStruct((8, 8, N_GVEC), F32)],
        scratch_shapes=[pltpu.SemaphoreType.DMA((N_MATS + 7,)), pltpu.SemaphoreType.DMA((N_MATS + 7,))],
    )(*halves, gvec)
    return outs[:N_MATS], outs[N_MATS]


def _set_slot(arr, block, idx):
    return lax.dynamic_update_slice(arr, block[None], (idx,) + (0,) * block.ndim)


PAD_RUNS = (((6304, 8352, 0), (5280, 6304, COL_Z))
            + tuple((672 + ((tq * 3 + g) * 4 + pr) * 128, 672 + ((tq * 3 + g) * 4 + pr + 1) * 128, _qkv_block(tq, g, pr) * 128)
                    for tq in range(3) for g in range(3) for pr in range(4))
            + ((0, 640, COL_LAT), (640, 672, COL_LAT + 704)))
W_IN_SHARD = 2088


def _full_weights(gathered):
    def cols(a):
        return jnp.concatenate([a[s] for s in range(4)], axis=1)

    w_uq, w_ukv, w_pm, w_pd = [cols(a) for a in gathered[1:5]]
    w_out = gathered[5].reshape(D_MODEL, D_MODEL)
    w_in_t = gathered[0].reshape(4 * W_IN_SHARD, D_MODEL)
    pieces, at = [], 0
    for lo, hi, pad_lo in sorted(PAD_RUNS, key=lambda t: t[2]):
        if pad_lo > at:
            pieces.append(jnp.zeros((pad_lo - at, D_MODEL), w_in_t.dtype))
        pieces.append(w_in_t[lo:hi])
        at = pad_lo + hi - lo
    pieces.append(jnp.zeros((N_PAD - at, D_MODEL), w_in_t.dtype))
    w_pad_t = jnp.concatenate(pieces, axis=0)
    z32 = jnp.zeros((Q_RANK, 32), w_uq.dtype)
    wuq_pad = jnp.concatenate([t for h in range(MLA_HEADS) for t in (w_uq[:, h * 96:(h + 1) * 96], z32)], axis=1)
    z64 = jnp.zeros((KV_RANK, 64), w_ukv.dtype)
    wk_pad = jnp.concatenate([t for h in range(MLA_HEADS) for t in (w_ukv[:, h * 128:h * 128 + 64], z64)], axis=1)
    wv = jnp.concatenate([w_ukv[:, h * 128 + 64:(h + 1) * 128] for h in range(MLA_HEADS)], axis=1)
    return w_pad_t.T, w_pad_t, wuq_pad, wk_pad, wv, w_pm, w_pd, w_out


W_IN_LAT = 672


def _grad_parts_in_early(dwt_early):
    def in_block(s, h):
        cols = slice(h * 512, (h + 1) * 512)
        out = []
        for lo, hi, pad_lo in sorted(PAD_RUNS):
            a_, b_ = max(lo, s * W_IN_SHARD), min(hi, (s + 1) * W_IN_SHARD)
            if a_ < b_:
                out.append(jnp.zeros((b_ - a_, 512), dwt_early.dtype) if pad_lo >= COL_LAT
                           else dwt_early[pad_lo + a_ - lo:pad_lo + b_ - lo, cols])
        return jnp.concatenate(out, axis=0)

    return jnp.stack([in_block(s, h) for s in range(4) for h in range(2)])


def _grad_parts_in_late(dwt_late):
    rows = jnp.concatenate([dwt_late[0:640], dwt_late[704:736]], axis=0)
    zero = jnp.zeros((W_IN_LAT, 512), dwt_late.dtype)
    return jnp.stack([rows[:, 0:512], rows[:, 512:1024]] + [zero] * 6)


def _col_blocks(m):
    r, c = m.shape[0] // 2, m.shape[1] // 4
    return jnp.stack([m[h * r:(h + 1) * r, s * c:(s + 1) * c] for s in range(4) for h in range(2)])


def _grad_parts_mla(dwuq_pad, dwk_pad, dwv):
    d_uq = jnp.concatenate([dwuq_pad[:, h * 128:h * 128 + 96] for h in range(MLA_HEADS)], axis=1)
    d_ukv = jnp.concatenate([t for h in range(MLA_HEADS) for t in (dwk_pad[:, h * 128:h * 128 + 64], dwv[:, h * 64:(h + 1) * 64])],
                            axis=1)
    return [_col_blocks(d_uq), _col_blocks(d_ukv)]


def _rope_tables(positions, token=None):
    pos = positions.reshape(SEQ).astype(F32)
    if token is not None:
        pos = pos + token[0, 0]
    lane = jnp.arange(128)

    def table(rot, first, period):
        inv = ROPE_THETA ** (-jnp.arange(0, rot, 2, dtype=F32) / rot)
        half = rot // 2
        off = lane % period - first
        in1, in2 = (off >= 0) & (off < half), (off >= half) & (off < rot)
        inv_lane = jnp.where(in1 | in2, inv[jnp.clip(off % half, 0, half - 1)], 0.0)
        sign = jnp.where(in1, -1.0, 1.0).astype(F32)
        ang = pos[:, None] * inv_lane[None, :]
        return jnp.cos(ang), jnp.sin(ang) * sign[None, :]

    return table(32, 64, 128), table(16, 0, 64)


class _Links:
    def __init__(self, mats, chip, me):
        landing = [_set_slot(lax.empty((4,) + m.shape, m.dtype), m, chip) for m in mats]
        self.gather, self.token = _gather_start(mats, landing)
        self.me, self.sent, self.handles, self.sums = me, {}, {}, {}

    def weights(self, after):
        return _relay_share(_gather_wait(self.gather, after))

    def send(self, blocks, name):
        self.sent[name] = blocks
        self.handles[name], token = _exchange_start(blocks, name)
        return token

    def collect(self, name, after, parts):
        recv = _exchange_wait(self.handles[name], after)
        for r, own, part in zip(recv, self.sent[name], parts):
            self.sums[part] = _sum_parts(r, own, self.me, 64, "sum_grad_" + part)
        return tuple(self.sums[part] for part in parts)


def _device_grads(x, positions, target, gains, links):
    pre_g, q_g, kv_g, post_g = gains
    (mc, ms), (dc, ds) = _rope_tables(positions, links.token)
    h = _prenorm_fwd(x, pre_g, links.token)
    w_pad, w_pad_t, wuq_pad, wk_pad, wv, w_pm, w_pd, w_out = _full_weights(links.weights((h, mc, ms, dc, ds)))

    p = _matmul(h, w_pad, "nn", F32, 1024, 1408, 1024, "in_proj")
    cqn, ckvn, q, k, v = _mla_prep_fwd(p, q_g, kv_g, wuq_pad, wk_pad, wv, mc, ms)
    ya, lse_m = _mla_flash_fwd(q, k, v)
    qkv = [_dil_prep_fwd(p, dc, ds, g) for g in range(3)]
    o_g, l_g = zip(*[_dil_attn_fwd2(qkv[g], g) for g in range(3)])
    (dp, dy, mg, dt, ua, dpa, ud, dpd, dya, dyd, yd, lse_d, loss_cols, dg_post) = _tail(
        p, ya, o_g, l_g, x, target, w_pm, w_pd, w_out, post_g)

    for g in range(3):
        dp = _dil_attn_bwd2(dp, qkv[g], dyd, yd, lse_d, dc, ds, g)
    dw_early = _matmul(dp, h, "tn", BF16, 1536, 1024, 2048, "dw_in_early", a_cols=(0, COL_LAT // 1536))
    dwpm = _matmul(ua, dpa, "tn", BF16, 512, 1024, 512, "dw_proj_mla")
    dwpd = _matmul(ud, dpd, "tn", BF16, 512, 1024, 512, "dw_proj_dil")
    dwout = _matmul(mg, dt, "tn", BF16, 1024, 1024, 512, "dw_out")
    token = links.send([_grad_parts_in_early(dw_early), _col_blocks(dwpm), _col_blocks(dwpd),
                        dwout.reshape(8, 128, D_MODEL)], "exchange_early")

    dq, dk, dv = _mla_flash_bwd(q, k, v, ya, dya, lse_m, token)
    dp, dqb, dkb, dvb, dg_q, dg_kv = _mla_prep_bwd(dp, p, dq, dk, dv, q_g, kv_g, wuq_pad, wk_pad, wv, mc, ms)
    dwuq_pad = _matmul(cqn, dqb, "tn", BF16, Q_RANK, 1024, 512, "dw_uq")
    dwk_pad = _matmul(ckvn, dkb, "tn", BF16, KV_RANK, 1024, 512, "dw_k")
    dwv = _matmul(ckvn, dvb, "tn", BF16, KV_RANK, 512, 512, "dw_v")
    dw_late = _matmul(dp, h, "tn", BF16, N_LAT, 1024, 2048, "dw_in_late", a_cols=(COL_LAT // N_LAT, 1))
    token = links.send([_grad_parts_in_late(dw_late)] + _grad_parts_mla(dwuq_pad, dwk_pad, dwv), "exchange_late")
    early = links.collect("exchange_early", dw_late, ("in_early", "pm", "pd", "out"))

    grad_x, dg_pre = _dh_prenorm_bwd(dp, w_pad_t, x, dy, pre_g, (token,) + tuple(early))
    links.collect("exchange_late", grad_x, ("in_late", "uq", "ukv"))

    loss_part = jnp.pad((jnp.sum(loss_cols) * (0.5 / D_MODEL)).reshape(1, 1), ((0, 0), (0, N_GVEC - N_GAINS - 1)))
    gvec = jnp.concatenate([dg_pre, dg_q, dg_kv, dg_post, loss_part], axis=1)
    return grad_x, gvec


def kernel(x, positions, pre_norm_g, w_in, q_norm_g, w_uq, kv_norm_g, w_ukv, w_proj_mla, w_proj_dil, w_out, post_norm_g, loss_target, m_pre_norm_g, m_w_in, m_q_norm_g, m_w_uq, m_kv_norm_g, m_w_ukv, m_w_proj_mla, m_w_proj_dil, m_w_out, m_post_norm_g, v_pre_norm_g, v_w_in, v_q_norm_g, v_w_uq, v_kv_norm_g, v_w_ukv, v_w_proj_mla, v_w_proj_dil, v_w_out, v_post_norm_g):
    xi, yi, ci = _my_place()
    chip, me = 2 * xi + yi, 4 * xi + 2 * yi + ci
    mats = [jnp.swapaxes(w_in, 1, 2)] + [w_uq, w_ukv, w_proj_mla, w_proj_dil, w_out]
    mats = [w.reshape(w.shape[1:]).astype(BF16) for w in mats]
    links = _Links(mats, chip, me)
    gains = (pre_norm_g, q_norm_g, kv_norm_g, post_norm_g)
    grad_x, gvec = _device_grads(x[0], positions, loss_target[0], gains, links)

    sums = links.sums
    in_e = sums["in_early"]
    half_in = jnp.concatenate([in_e[:W_IN_LAT] + jnp.where(chip == 0, sums["in_late"], 0.0), in_e[W_IN_LAT:]], axis=0)
    halves = [half_in, sums["uq"], sums["ukv"], sums["pm"], sums["pd"], sums["out"]]
    gvec8 = jnp.pad(gvec, ((0, 7), (0, 0)))
    swapped, recv_gains = _swap_halves(halves, gvec8)
    g_gains = _sum_parts(recv_gains, gvec8, me, 8, "sum_gain_parts")[0:1]
    loss = g_gains[0, N_GAINS]
    sw = lambda a: jnp.swapaxes(a, 1, 2)
    d_in, m_in, v_in, g_in = [sw(o) for o in _adamw_in(sw(w_in), sw(m_w_in), sw(v_w_in), half_in, swapped[0], ci)]
    g_mats = [g_in] + [_set_slot(s, hf, ci).reshape((1,) + shp)
                       for s, hf, shp in zip(swapped[1:], halves[1:], SHARD_SHAPES[1:])]

    off = [0, 1024, 1408, 1664, 2688]
    g_gain = [g_gains[:, off[i]:off[i + 1]] for i in range(4)]
    grads = [g_gain[0], g_mats[0], g_gain[1], g_mats[1], g_gain[2], g_mats[2], g_mats[3], g_mats[4], g_mats[5], g_gain[3]]
    ws = [pre_norm_g, w_in, q_norm_g, w_uq, kv_norm_g, w_ukv, w_proj_mla, w_proj_dil, w_out, post_norm_g]
    ms = [m_pre_norm_g, m_w_in, m_q_norm_g, m_w_uq, m_kv_norm_g, m_w_ukv, m_w_proj_mla, m_w_proj_dil, m_w_out, m_post_norm_g]
    vs = [v_pre_norm_g, v_w_in, v_q_norm_g, v_w_uq, v_kv_norm_g, v_w_ukv, v_w_proj_mla, v_w_proj_dil, v_w_out, v_post_norm_g]
    deltas, new_m, new_v = [], [], []
    for i, (w, g, m, v) in enumerate(zip(ws, grads, ms, vs)):
        if w is w_in:
            d_, m_, v_ = d_in, m_in, v_in
        elif w.shape[-1] % 128 and w.shape[-2] % 128 == 0:
            g = jnp.swapaxes(g, 1, 2)
            grads[i] = jnp.swapaxes(g, 1, 2)
            d_, m_, v_ = [jnp.swapaxes(o, 1, 2) for o in
                          _adamw(jnp.swapaxes(w, 1, 2), g, jnp.swapaxes(m, 1, 2), jnp.swapaxes(v, 1, 2), f"adamw_{i}")]
        else:
            d_, m_, v_ = _adamw(w, g, m, v, f"adamw_{i}")
        deltas.append(d_)
        new_m.append(m_)
        new_v.append(v_)
    return (loss, grad_x.reshape(x.shape), *grads, *deltas, *new_m, *new_v)
```

```python
import jax
import jax.numpy as jnp
from jax import lax
from jax.experimental import pallas as pl
from jax.experimental.pallas import tpu as pltpu

F32 = jnp.float32
BF16 = jnp.bfloat16

SEQ = 4096
D_MODEL = 1024
EPS = 1e-6
ROPE_THETA = 500000.0
MLA_HEADS = 8
Q_RANK = 384
KV_RANK = 256
MLA_SCALE = 96.0 ** -0.5
MLA_ROPE_HALF = 16
DIL_DILATIONS = (1, 4, 16)
DIL_ROPE_HALF = 8
DIL_SCALE = 0.125
BAND = 128

N_LAT = 768
COL_Z, COL_QKV, COL_LAT = 2048, 3072, 7680
N_PAD = 8448


def _qkv_block(tq, g, pr):
    return COL_QKV // 128 + (g * 4 + pr) * 3 + tq

IN_SPLITS = (384, 256, 32, 4608, 512, 512, 1024, 1024)

SHARD_SHAPES = ((2088, 1024), (384, 192), (256, 256), (512, 256), (512, 256), (256, 1024))
N_MATS = len(SHARD_SHAPES)
N_GAINS = 2688
N_GVEC = N_GAINS + 128

ADAM_LR, ADAM_B1, ADAM_B2, ADAM_EPS, ADAM_WD, ADAM_STEP = 0.001, 0.9, 0.999, 1e-08, 0.01, 10

VMEM_LIMIT = 56 * 1024 * 1024
NEG = -1e30
MESH = pl.DeviceIdType.MESH


def _cparams(**kw):
    return pltpu.CompilerParams(vmem_limit_bytes=VMEM_LIMIT, **kw)


def _dot(a, b, dims):
    return lax.dot_general(a, b, (dims, ((), ())), preferred_element_type=F32)


def _nn(a, b):
    return _dot(a, b, ((1,), (0,)))


def _nt(a, b):
    return _dot(a, b, ((1,), (1,)))


def _tn(a, b):
    return _dot(a, b, ((0,), (0,)))


def _rope_lanes(shape, half, period, first):
    lane = lax.broadcasted_iota(jnp.int32, shape, len(shape) - 1) % period
    return (lane >= first) & (lane < first + half), (lane >= first + half) & (lane < first + 2 * half)


def _rope_fwd(x, c, s, half, lanes):
    x1, _ = lanes
    return x * c + jnp.where(x1, pltpu.roll(x, 128 - half, 1), pltpu.roll(x, half, 1)) * s


def _rope_bwd(g, c, s, half, lanes):
    x1, x2 = lanes
    gs = g * s
    return g * c + jnp.where(x2, pltpu.roll(gs, half, 1), jnp.where(x1, pltpu.roll(gs, 128 - half, 1), 0.0))


def _sigmoid(x):
    return 1.0 / (1.0 + jnp.exp(-x))


def _after(token):
    tokens = [t for t in (token if isinstance(token, (tuple, list)) else [token]) if t is not None]
    return tokens, [pl.BlockSpec(memory_space=pl.ANY)] * len(tokens)


def _matmul(a, b, mode, out_dtype, tm, tn, tk, name, token=None, b_cols=None, a_cols=None):
    after, after_specs = _after(token)
    if mode == "nn":
        (m, k), n = a.shape, b.shape[1]
        first = 0
        if b_cols is not None:
            first, n = b_cols[0], b_cols[1] * tn
        a_spec = pl.BlockSpec((tm, tk), lambda j, i, kk: (i, kk))
        b_spec = pl.BlockSpec((tk, tn), lambda j, i, kk: (kk, j + first))
        dot = _nn
    elif mode == "nt":
        (m, k), n = a.shape, b.shape[0]
        a_spec = pl.BlockSpec((tm, tk), lambda j, i, kk: (i, kk))
        b_spec = pl.BlockSpec((tn, tk), lambda j, i, kk: (j, kk))
        dot = _nt
    else:
        (k, m), n = a.shape, b.shape[1]
        first = 0
        if a_cols is not None:
            first, m = a_cols[0], a_cols[1] * tm
        a_spec = pl.BlockSpec((tk, tm), lambda j, i, kk: (kk, i + first))
        b_spec = pl.BlockSpec((tk, tn), lambda j, i, kk: (kk, j))
        dot = _tn
    assert m % tm == 0 and n % tn == 0 and k % tk == 0, (name, m, n, k, tm, tn, tk)
    nk = k // tk

    def body(a_ref, b_ref, *rest):
        o_ref, acc_ref = rest[-2:]
        kk = pl.program_id(2)
        part = dot(a_ref[...], b_ref[...])

        @pl.when(kk == 0)
        def _():
            acc_ref[...] = part

        @pl.when(kk > 0)
        def _():
            acc_ref[...] += part

        @pl.when(kk == nk - 1)
        def _():
            o_ref[...] = acc_ref[...].astype(o_ref.dtype)

    return pl.pallas_call(
        body, name=name, grid=(n // tn, m // tm, nk),
        in_specs=[a_spec, b_spec] + after_specs,
        out_specs=pl.BlockSpec((tm, tn), lambda j, i, kk: (i, j)),
        out_shape=jax.ShapeDtypeStruct((m, n), out_dtype),
        scratch_shapes=[pltpu.VMEM((tm, tn), F32)],
        compiler_params=_cparams(),
    )(a, b, *after)


def _prenorm_fwd(x, g, token=None):
    tm = 512
    after, after_specs = _after(token)

    def body(x_ref, g_ref, *rest):
        xv = x_ref[...]
        r = lax.rsqrt(jnp.mean(xv * xv, axis=-1, keepdims=True) + EPS)
        rest[-1][...] = (xv * r * g_ref[...]).astype(BF16)

    return pl.pallas_call(
        body, name="prenorm_fwd", grid=(SEQ // tm,),
        in_specs=[pl.BlockSpec((tm, D_MODEL), lambda i: (i, 0)), pl.BlockSpec((1, D_MODEL), lambda i: (0, 0))] + after_specs,
        out_specs=pl.BlockSpec((tm, D_MODEL), lambda i: (i, 0)),
        out_shape=jax.ShapeDtypeStruct((SEQ, D_MODEL), BF16),
    )(x, g, *after)


def _prenorm_bwd(x, dh, dy, g):
    tm = 512

    def body(x_ref, dh_ref, dy_ref, g_ref, gx_ref, dg_ref):
        xv = x_ref[...]
        r = lax.rsqrt(jnp.mean(xv * xv, axis=-1, keepdims=True) + EPS)
        n = xv * r
        dhv = dh_ref[...]
        dn = dhv * g_ref[...]
        gx_ref[...] = dy_ref[...] + r * (dn - n * jnp.mean(dn * n, axis=-1, keepdims=True))
        part = jnp.sum(dhv * n, axis=0, keepdims=True)

        @pl.when(pl.program_id(0) == 0)
        def _():
            dg_ref[...] = part

        @pl.when(pl.program_id(0) > 0)
        def _():
            dg_ref[...] += part

    row = pl.BlockSpec((tm, D_MODEL), lambda i: (i, 0))
    vec = pl.BlockSpec((1, D_MODEL), lambda i: (0, 0))
    return pl.pallas_call(
        body, name="prenorm_bwd", grid=(SEQ // tm,),
        in_specs=[row, row, row, vec], out_specs=[row, vec],
        out_shape=[jax.ShapeDtypeStruct((SEQ, D_MODEL), F32), jax.ShapeDtypeStruct((1, D_MODEL), F32)],
        compiler_params=_cparams(),
    )(x, dh, dy, g)


def _dh_prenorm_bwd(dp, w_pad_t, x, dy, g, token=None):
    tm, tk = 1024, 1408
    nk = N_PAD // tk
    after, after_specs = _after(token)

    def body(a_ref, b_ref, x_ref, dy_ref, g_ref, *rest):
        gx_ref, dg_ref, acc_ref = rest[-3:]
        i, kk = pl.program_id(0), pl.program_id(1)
        part = _nn(a_ref[...], b_ref[...])

        @pl.when(kk == 0)
        def _():
            acc_ref[...] = part

        @pl.when(kk > 0)
        def _():
            acc_ref[...] += part

        @pl.when(kk == nk - 1)
        def _():
            xv = x_ref[...]
            r = lax.rsqrt(jnp.mean(xv * xv, axis=-1, keepdims=True) + EPS)
            n = xv * r
            dhv = acc_ref[...]
            dn = dhv * g_ref[...]
            gx_ref[...] = dy_ref[...] + r * (dn - n * jnp.mean(dn * n, axis=-1, keepdims=True))
            cols = jnp.sum(dhv * n, axis=0, keepdims=True)

            @pl.when(i == 0)
            def _():
                dg_ref[...] = cols

            @pl.when(i > 0)
            def _():
                dg_ref[...] += cols

    row = pl.BlockSpec((tm, D_MODEL), lambda i, kk: (i, 0))
    vec = pl.BlockSpec((1, D_MODEL), lambda i, kk: (0, 0))
    return pl.pallas_call(
        body, name="dh_prenorm_bwd", grid=(SEQ // tm, nk),
        in_specs=[pl.BlockSpec((tm, tk), lambda i, kk: (i, kk)), pl.BlockSpec((tk, D_MODEL), lambda i, kk: (kk, 0)),
                  row, row, vec] + after_specs,
        out_specs=[row, vec],
        out_shape=[jax.ShapeDtypeStruct((SEQ, D_MODEL), F32), jax.ShapeDtypeStruct((1, D_MODEL), F32)],
        scratch_shapes=[pltpu.VMEM((tm, D_MODEL), F32)],
        compiler_params=_cparams(),
    )(dp, w_pad_t, x, dy, g, *after)


def _mla_prep_fwd(p, qg, kvg, wuq, wk, wv, rc, rs):
    tm = 512

    def body(lat_ref, qg_ref, kvg_ref, wuq_ref, wk_ref, wv_ref, c_ref, s_ref,
             cqn_ref, ckvn_ref, q_ref, k_ref, v_ref):
        c, s = c_ref[...], s_ref[...]
        lanes = _rope_lanes((tm, 128), MLA_ROPE_HALF, 128, 64)
        cq = lat_ref[:, 0:Q_RANK]
        r1 = lax.rsqrt(jnp.mean(cq * cq, axis=-1, keepdims=True) + EPS)
        cqn = (cq * r1 * qg_ref[...]).astype(BF16)
        cqn_ref[...] = cqn
        q = _nn(cqn, wuq_ref[...])
        for h in range(MLA_HEADS):
            sl = slice(h * 128, (h + 1) * 128)
            q_ref[:, sl] = (_rope_fwd(q[:, sl], c, s, MLA_ROPE_HALF, lanes) * MLA_SCALE).astype(BF16)
        ckv = lat_ref[:, Q_RANK:Q_RANK + KV_RANK]
        r2 = lax.rsqrt(jnp.mean(ckv * ckv, axis=-1, keepdims=True) + EPS)
        ckvn = (ckv * r2 * kvg_ref[...]).astype(BF16)
        ckvn_ref[...] = ckvn
        krr = _rope_fwd(lat_ref[:, Q_RANK + KV_RANK:N_LAT], c, s, MLA_ROPE_HALF, lanes)
        kn = _nn(ckvn, wk_ref[...])
        for h in range(MLA_HEADS):
            sl = slice(h * 128, (h + 1) * 128)
            k_ref[:, sl] = (kn[:, sl] + krr).astype(BF16)
        v_ref[...] = _nn(ckvn, wv_ref[...]).astype(BF16)

    def full(shape):
        return pl.BlockSpec(shape, lambda i: (0, 0))

    def rows(w):
        return pl.BlockSpec((tm, w), lambda i: (i, 0))

    return pl.pallas_call(
        body, name="mla_prep_fwd", grid=(SEQ // tm,),
        in_specs=[pl.BlockSpec((tm, N_LAT), lambda i: (i, COL_LAT // N_LAT)),
                  full((1, Q_RANK)), full((1, KV_RANK)), full((Q_RANK, 1024)), full((KV_RANK, 1024)),
                  full((KV_RANK, 512)), rows(128), rows(128)],
        out_specs=[rows(Q_RANK), rows(KV_RANK), rows(1024), rows(1024), rows(512)],
        out_shape=[jax.ShapeDtypeStruct((SEQ, Q_RANK), BF16), jax.ShapeDtypeStruct((SEQ, KV_RANK), BF16),
                   jax.ShapeDtypeStruct((SEQ, 1024), BF16), jax.ShapeDtypeStruct((SEQ, 1024), BF16),
                   jax.ShapeDtypeStruct((SEQ, 512), BF16)],
        compiler_params=_cparams(),
    )(p, qg, kvg, wuq, wk, wv, rc, rs)


def _mla_prep_bwd(dp_in, p, dq, dk, dv, qg, kvg, wuq, wk, wv, rc, rs):
    tm = 512

    def body(dp_any, lat_ref, dq_ref, dk_ref, dv_ref, qg_ref, kvg_ref, wuq_ref, wk_ref, wv_ref,
             c_ref, s_ref, dp_ref, dqb_ref, dkb_ref, dvb_ref, dgq_ref, dgkv_ref):
        del dp_any
        c, s = c_ref[...], s_ref[...]
        lanes = _rope_lanes((tm, 128), MLA_ROPE_HALF, 128, 64)
        lane = lax.broadcasted_iota(jnp.int32, (tm, 128), 1)
        dkr = jnp.zeros((tm, 128), F32)
        for h in range(MLA_HEADS):
            sl = slice(h * 128, (h + 1) * 128)
            dqb_ref[:, sl] = _rope_bwd(dq_ref[:, sl] * MLA_SCALE, c, s, MLA_ROPE_HALF, lanes).astype(BF16)
            dkh = dk_ref[:, sl]
            dkr = dkr + dkh
            dkb_ref[:, sl] = jnp.where(lane < 64, dkh, 0.0).astype(BF16)
        dkr = jnp.where((lane >= 64) & (lane < 96), dkr, 0.0)
        dkr = _rope_bwd(dkr, c, s, MLA_ROPE_HALF, lanes)
        dvb = dv_ref[...].astype(BF16)
        dvb_ref[...] = dvb

        cq = lat_ref[:, 0:Q_RANK]
        r1 = lax.rsqrt(jnp.mean(cq * cq, axis=-1, keepdims=True) + EPS)
        n1 = cq * r1
        dcqn = _nt(dqb_ref[...], wuq_ref[...])
        dn1 = dcqn * qg_ref[...]
        dcq = r1 * (dn1 - n1 * jnp.mean(dn1 * n1, axis=-1, keepdims=True))
        pq = jnp.sum(dcqn * n1, axis=0, keepdims=True)

        ckv = lat_ref[:, Q_RANK:Q_RANK + KV_RANK]
        r2 = lax.rsqrt(jnp.mean(ckv * ckv, axis=-1, keepdims=True) + EPS)
        n2 = ckv * r2
        dckvn = _nt(dkb_ref[...], wk_ref[...]) + _nt(dvb, wv_ref[...])
        dn2 = dckvn * kvg_ref[...]
        dckv = r2 * (dn2 - n2 * jnp.mean(dn2 * n2, axis=-1, keepdims=True))
        pkv = jnp.sum(dckvn * n2, axis=0, keepdims=True)

        dp_ref[:, 0:Q_RANK] = dcq.astype(BF16)
        dp_ref[:, Q_RANK:Q_RANK + KV_RANK] = dckv.astype(BF16)
        dp_ref[:, Q_RANK + KV_RANK:N_LAT] = dkr.astype(BF16)

        @pl.when(pl.program_id(0) == 0)
        def _():
            dgq_ref[...] = pq
            dgkv_ref[...] = pkv

        @pl.when(pl.program_id(0) > 0)
        def _():
            dgq_ref[...] += pq
            dgkv_ref[...] += pkv

    def full(shape):
        return pl.BlockSpec(shape, lambda i: (0, 0))

    def rows(w):
        return pl.BlockSpec((tm, w), lambda i: (i, 0))

    lat = pl.BlockSpec((tm, N_LAT), lambda i: (i, COL_LAT // N_LAT))
    return pl.pallas_call(
        body, name="mla_prep_bwd", grid=(SEQ // tm,),
        in_specs=[pl.BlockSpec(memory_space=pl.ANY), lat, rows(1024), rows(1024), rows(512),
                  full((1, Q_RANK)), full((1, KV_RANK)), full((Q_RANK, 1024)), full((KV_RANK, 1024)),
                  full((KV_RANK, 512)), rows(128), rows(128)],
        out_specs=[lat, rows(1024), rows(1024), rows(512), full((1, Q_RANK)), full((1, KV_RANK))],
        out_shape=[jax.ShapeDtypeStruct((SEQ, N_PAD), BF16), jax.ShapeDtypeStruct((SEQ, 1024), BF16),
                   jax.ShapeDtypeStruct((SEQ, 1024), BF16), jax.ShapeDtypeStruct((SEQ, 512), BF16),
                   jax.ShapeDtypeStruct((1, Q_RANK), F32), jax.ShapeDtypeStruct((1, KV_RANK), F32)],
        input_output_aliases={0: 0},
        compiler_params=_cparams(),
    )(dp_in, p, dq, dk, dv, qg, kvg, wuq, wk, wv, rc, rs)


FLASH_T = 1024


def _head_half(shape, hh):
    lane = lax.broadcasted_iota(jnp.int32, shape, 1)
    return (lane < 64) if hh == 0 else (lane >= 64)


def _diag_keep(nr, nk):
    row = lax.broadcasted_iota(jnp.int32, (nr, nk), 0)
    col = lax.broadcasted_iota(jnp.int32, (nr, nk), 1)
    return row + (nk - nr) >= col


def _tri_steps(nb, q_major):
    if q_major:
        pairs = [(i, kb) for i in range(nb) for kb in range(i + 1)]
    else:
        pairs = [(i, kb) for kb in range(nb) for i in range(kb, nb)]
    return jnp.asarray([p[0] for p in pairs], jnp.int32), jnp.asarray([p[1] for p in pairs], jnp.int32)


def _mla_flash_fwd(q, k, v):
    t = FLASH_T
    nb = SEQ // t
    qtab, ktab = _tri_steps(nb, True)

    def body(qi_ref, ki_ref, q_ref, k_ref, v_ref, o_ref, lse_ref, m_scr, l_scr, acc_scr):
        step = pl.program_id(1)
        i, kb = qi_ref[step], ki_ref[step]

        @pl.when(kb == 0)
        def _():
            m_scr[...] = jnp.full_like(m_scr, NEG)
            l_scr[...] = jnp.zeros_like(l_scr)
            acc_scr[...] = jnp.zeros_like(acc_scr)

        def update(r0, nr, nk, diagonal):
            rs = slice(r0, r0 + nr)
            vv = v_ref[0:nk, :]
            for hh in range(2):
                sl = slice(hh * 128, (hh + 1) * 128)
                s = _nt(q_ref[rs, sl], k_ref[0:nk, sl])
                if diagonal:
                    s = jnp.where(_diag_keep(nr, nk), s, NEG)
                m_prev = m_scr[hh, rs, :]
                m_new = jnp.maximum(m_prev, jnp.max(s, axis=-1, keepdims=True))
                pr = jnp.exp(s - jnp.tile(m_new, (1, nk // 128)))
                alpha = jnp.exp(m_prev - m_new)
                l_scr[hh, rs, :] = alpha * l_scr[hh, rs, :] + jnp.sum(pr, axis=-1, keepdims=True)
                acc_scr[hh, rs, :] = alpha * acc_scr[hh, rs, :] + _nn(pr.astype(BF16), vv)
                m_scr[hh, rs, :] = m_new

        @pl.when(kb < i)
        def _():
            update(0, t, t, False)

        @pl.when(kb == i)
        def _():
            update(0, t // 2, t // 2, True)
            update(t // 2, t // 2, t, True)
            o0 = acc_scr[0] / l_scr[0]
            o1 = acc_scr[1] / l_scr[1]
            o_ref[...] = jnp.where(_head_half((t, 128), 0), o0, o1)
            for hh in range(2):
                lse_ref[:, hh * 128:(hh + 1) * 128] = m_scr[hh] + jnp.log(l_scr[hh])

    grid_spec = pltpu.PrefetchScalarGridSpec(
        num_scalar_prefetch=2, grid=(4, qtab.shape[0]),
        in_specs=[pl.BlockSpec((t, 256), lambda j, s, qi, ki: (qi[s], j)),
                  pl.BlockSpec((t, 256), lambda j, s, qi, ki: (ki[s], j)),
                  pl.BlockSpec((t, 128), lambda j, s, qi, ki: (ki[s], j))],
        out_specs=[pl.BlockSpec((t, 128), lambda j, s, qi, ki: (qi[s], j)),
                   pl.BlockSpec((t, 256), lambda j, s, qi, ki: (qi[s], j))],
        scratch_shapes=[pltpu.VMEM((2, t, 128), F32), pltpu.VMEM((2, t, 128), F32), pltpu.VMEM((2, t, 128), F32)])
    return pl.pallas_call(
        body, name="mla_flash_fwd", grid_spec=grid_spec,
        out_shape=[jax.ShapeDtypeStruct((SEQ, 512), F32), jax.ShapeDtypeStruct((SEQ, 1024), F32)],
        compiler_params=_cparams(),
    )(qtab, ktab, q, k, v)


def _mla_flash_bwd(q, k, v, o, do, lse, token=None):
    t = FLASH_T
    nb = SEQ // t
    qtab, ktab = _tri_steps(nb, False)
    after, after_specs = _after(token)

    def body(qi_ref, ki_ref, q_ref, k_ref, v_ref, o_ref, do_ref, lse_ref, *rest):
        dq_ref, dk_ref, dv_ref, dk_scr, dv_scr = rest[-5:]
        step = pl.program_id(1)
        i, kb = qi_ref[step], ki_ref[step]

        @pl.when(step == 0)
        def _():
            dq_ref[...] = jnp.zeros_like(dq_ref)

        @pl.when(i == kb)
        def _():
            dk_scr[...] = jnp.zeros_like(dk_scr)
            dv_scr[...] = jnp.zeros_like(dv_scr)

        def update(r0, nr, nk, diagonal):
            rs = slice(r0, r0 + nr)
            vv = v_ref[0:nk, :]
            ov = o_ref[rs, :]
            dov = do_ref[rs, :]
            rows = pl.ds(pl.multiple_of(i * t + r0, t // 2), nr)
            for hh in range(2):
                sl = slice(hh * 128, (hh + 1) * 128)
                qh, kh = q_ref[rs, sl], k_ref[0:nk, sl]
                s = _nt(qh, kh)
                if diagonal:
                    s = jnp.where(_diag_keep(nr, nk), s, NEG)
                pr = jnp.exp(s - jnp.tile(lse_ref[rs, sl], (1, nk // 128)))
                dom = jnp.where(_head_half((nr, 128), hh), dov, 0.0)
                domb = dom.astype(BF16)
                dv_scr[0:nk, :] += _tn(pr.astype(BF16), domb)
                dpr = _nt(domb, vv)
                delta = jnp.sum(dom * ov, axis=-1, keepdims=True)
                ds = (pr * (dpr - delta)).astype(BF16)
                dq_ref[rows, sl] += _nn(ds, kh)
                dk_scr[hh, 0:nk, :] += _tn(ds, qh)

        @pl.when(i > kb)
        def _():
            update(0, t, t, False)

        @pl.when(i == kb)
        def _():
            update(0, t // 2, t // 2, True)
            update(t // 2, t // 2, t, True)

        @pl.when(i == nb - 1)
        def _():
            dk_ref[:, 0:128] = dk_scr[0]
            dk_ref[:, 128:256] = dk_scr[1]
            dv_ref[...] = dv_scr[...]

    qi_map = lambda j, s, qi, ki: (qi[s], j)
    ki_map = lambda j, s, qi, ki: (ki[s], j)
    grid_spec = pltpu.PrefetchScalarGridSpec(
        num_scalar_prefetch=2, grid=(4, qtab.shape[0]),
        in_specs=[pl.BlockSpec((t, 256), qi_map), pl.BlockSpec((t, 256), ki_map), pl.BlockSpec((t, 128), ki_map),
                  pl.BlockSpec((t, 128), qi_map), pl.BlockSpec((t, 128), qi_map), pl.BlockSpec((t, 256), qi_map)]
        + after_specs,
        out_specs=[pl.BlockSpec((SEQ, 256), lambda j, s, qi, ki: (0, j)), pl.BlockSpec((t, 256), ki_map),
                   pl.BlockSpec((t, 128), ki_map)],
        scratch_shapes=[pltpu.VMEM((2, t, 128), F32), pltpu.VMEM((t, 128), F32)])
    return pl.pallas_call(
        body, name="mla_flash_bwd", grid_spec=grid_spec,
        out_shape=[jax.ShapeDtypeStruct((SEQ, 1024), F32), jax.ShapeDtypeStruct((SEQ, 1024), F32),
                   jax.ShapeDtypeStruct((SEQ, 512), F32)],
        compiler_params=_cparams(),
    )(qtab, ktab, q, k, v, o, do, lse, *after)


DIL_UNROLL = 4


def _strided(start, size, d):
    return pl.ds(start, size) if d == 1 else pl.ds(start, size, stride=d)


def _dil_prep_fwd(p, rc, rs, g):
    d = DIL_DILATIONS[g]
    sub_len = SEQ // d
    ch = min(sub_len, 512)

    def body(p_ref, c_ref, s_ref, o_ref, x_scr):
        tq = pl.program_id(0)
        lanes = _rope_lanes((ch, 128), DIL_ROPE_HALF, 64, 0)
        o_ref[0, 0:BAND, :] = jnp.zeros((BAND, 128), BF16)

        @pl.when(tq < 2)
        def _():
            mult = jnp.where(tq == 0, DIL_SCALE, 1.0).astype(F32)
            for c0 in range(0, SEQ, ch):
                rows = pl.ds(c0, ch)
                x_scr[rows, :] = _rope_fwd(p_ref[rows, :], c_ref[rows, :] * mult, s_ref[rows, :] * mult, DIL_ROPE_HALF, lanes)

        def gather(src):
            for r in range(d):
                for c0 in range(0, sub_len, ch):
                    at = BAND + r * sub_len + c0
                    o_ref[0, at:at + ch, :] = src[_strided(r + c0 * d, ch, d), :].astype(BF16)

        @pl.when(tq < 2)
        def _():
            gather(x_scr)

        @pl.when(tq == 2)
        def _():
            gather(p_ref)

    tab = pl.BlockSpec((SEQ, 128), lambda tq, pr: (0, 0))
    return pl.pallas_call(
        body, name=f"dil_prep_fwd_g{g}", grid=(3, 4),
        in_specs=[pl.BlockSpec((SEQ, 128), lambda tq, pr: (0, _qkv_block(tq, g, pr))), tab, tab],
        out_specs=pl.BlockSpec((1, BAND + SEQ, 128), lambda tq, pr: (tq, 0, pr)),
        out_shape=jax.ShapeDtypeStruct((3, BAND + SEQ, 512), BF16),
        scratch_shapes=[pltpu.VMEM((SEQ, 128), F32)],
        compiler_params=_cparams(),
    )(p, rc, rs)


DIL_ST_FWD, DIL_ST_BWD = 1024, 2048


def _band_keep(g, b, t, nb):
    nbs = SEQ // DIL_DILATIONS[g] // BAND
    row = lax.broadcasted_iota(jnp.int32, (BAND, 2 * BAND), 0)
    col = lax.broadcasted_iota(jnp.int32, (BAND, 2 * BAND), 1)
    cur = (col >= BAND) & (row >= col - BAND)
    prev = (col < BAND) & (col >= row)
    if nbs >= nb:
        if b > 0:
            return cur | prev
        return cur | (prev & ((t * nb) % nbs != 0))
    return cur | prev if b % nbs else cur


def _dil_tok(g, b, t, nb):
    d = DIL_DILATIONS[g]
    nbs = SEQ // d // BAND
    gb = t * nb + b
    return _strided((gb % nbs) * BAND * d + gb // nbs, BAND, d)


def _dil_attn_fwd2(qkv, g):
    DIL_ST, DIL_NB = DIL_ST_FWD, DIL_ST_FWD // BAND

    def body(q_ref, k_ref, v_ref, o_ref, l_ref, s_scr, p_scr, o_scr):
        t = pl.program_id(1)
        base = t * DIL_ST
        half0 = _head_half((DIL_ST, 128), 0)
        lse_h = []
        for hh in range(2):
            half = _head_half((BAND, 128), hh)
            for b in range(DIL_NB):
                qv = q_ref[0, pl.ds(pl.multiple_of(base + (b + 1) * BAND, BAND), BAND), :]
                k2 = k_ref[0, pl.ds(pl.multiple_of(base + b * BAND, BAND), 2 * BAND), :]
                sb = _nt(jnp.where(half, qv, jnp.zeros_like(qv)), k2)
                s_scr[b * BAND:(b + 1) * BAND, :] = jnp.where(_band_keep(g, b, t, DIL_NB), sb, NEG)
            s = s_scr[...]
            m = jnp.max(s, axis=-1, keepdims=True)
            pr = jnp.exp(s - m)
            den = jnp.sum(pr, axis=-1, keepdims=True)
            p_scr[...] = pr.astype(BF16)
            for b in range(DIL_NB):
                v2 = v_ref[0, pl.ds(pl.multiple_of(base + b * BAND, BAND), 2 * BAND), :]
                o_scr[hh, b * BAND:(b + 1) * BAND, :] = _nn(p_scr[b * BAND:(b + 1) * BAND, :], v2)
            o_scr[hh] = o_scr[hh] / den
            lse_h.append(m + jnp.log(den))
        out = jnp.where(half0, o_scr[0], o_scr[1])
        lse = jnp.where(half0, lse_h[0], lse_h[1])
        for b in range(DIL_NB):
            tok = _dil_tok(g, b, t, DIL_NB)
            o_ref[tok, :] = out[b * BAND:(b + 1) * BAND, :]
            l_ref[tok, :] = lse[b * BAND:(b + 1) * BAND, :]

    def inp(tq):
        return pl.BlockSpec((1, BAND + SEQ, 128), lambda pr, t: (tq, 0, pr))

    out = pl.BlockSpec((SEQ, 128), lambda pr, t: (0, pr))
    return pl.pallas_call(
        body, name=f"dil_attn_fwd_g{g}", grid=(4, SEQ // DIL_ST),
        in_specs=[inp(0), inp(1), inp(2)], out_specs=[out, out],
        out_shape=[jax.ShapeDtypeStruct((SEQ, 512), F32), jax.ShapeDtypeStruct((SEQ, 512), F32)],
        scratch_shapes=[pltpu.VMEM((DIL_ST, 2 * BAND), F32), pltpu.VMEM((DIL_ST, 2 * BAND), BF16),
                        pltpu.VMEM((2, DIL_ST, 128), F32)],
        compiler_params=_cparams(),
    )(qkv, qkv, qkv)


def _dil_attn_bwd2(dp_in, qkv, dyd, yd, lse_all, rc, rs, g, token=None):
    d = DIL_DILATIONS[g]
    sub_len = SEQ // d
    DIL_ST, DIL_NB = DIL_ST_BWD, DIL_ST_BWD // BAND
    nst = SEQ // DIL_ST
    after, after_specs = _after(token)
    ch = 512

    def body(dp_any, q_ref, k_ref, v_ref, do_ref, y_ref, l_ref, c_ref, sn_ref, *rest):
        dp_ref, tok_scr, dk_scr, dv_scr, s_scr, dp_scr, p_scr, ds_scr, do_scr, y_scr, l_scr, dq_scr = rest[-12:]
        del dp_any
        t = pl.program_id(1)
        base = t * DIL_ST

        @pl.when(t == 0)
        def _():
            dk_scr[...] = jnp.zeros_like(dk_scr)
            dv_scr[...] = jnp.zeros_like(dv_scr)

        for b in range(DIL_NB):
            tok = _dil_tok(g, b, t, DIL_NB)
            do_scr[b * BAND:(b + 1) * BAND, :] = do_ref[tok, :]
            y_scr[b * BAND:(b + 1) * BAND, :] = y_ref[tok, :]
            l_scr[b * BAND:(b + 1) * BAND, :] = l_ref[tok, :]
        for hh in range(2):
            half = _head_half((BAND, 128), hh)
            half_st = _head_half((DIL_ST, 128), hh)
            dom = jnp.where(half_st, do_scr[...], 0.0)
            delta = jnp.sum(dom * y_scr[...], axis=-1, keepdims=True)
            lcol = jnp.max(jnp.where(half_st, l_scr[...], NEG), axis=-1, keepdims=True)
            for b in range(DIL_NB):
                rows = slice(b * BAND, (b + 1) * BAND)
                qv = q_ref[0, pl.ds(pl.multiple_of(base + (b + 1) * BAND, BAND), BAND), :]
                band = pl.ds(pl.multiple_of(base + b * BAND, BAND), 2 * BAND)
                sb = _nt(jnp.where(half, qv, jnp.zeros_like(qv)), k_ref[0, band, :])
                s_scr[rows, :] = jnp.where(_band_keep(g, b, t, DIL_NB), sb, NEG)
                dp_scr[rows, :] = _nt(dom[rows, :].astype(BF16), v_ref[0, band, :])
            pr = jnp.exp(s_scr[...] - lcol)
            p_scr[...] = pr.astype(BF16)
            ds_scr[...] = (pr * (dp_scr[...] - delta)).astype(BF16)
            for b in range(DIL_NB):
                rows = slice(b * BAND, (b + 1) * BAND)
                qv = q_ref[0, pl.ds(pl.multiple_of(base + (b + 1) * BAND, BAND), BAND), :]
                band = pl.ds(pl.multiple_of(base + b * BAND, BAND), 2 * BAND)
                dqb = jnp.where(half, _nn(ds_scr[rows, :], k_ref[0, band, :]), 0.0)
                if hh == 0:
                    dq_scr[rows, :] = dqb
                else:
                    dq_scr[rows, :] += dqb
                half2 = _head_half((2 * BAND, 128), hh)
                dk_scr[band, :] += jnp.where(half2, _tn(ds_scr[rows, :], qv), 0.0)
                dv_scr[band, :] += _tn(p_scr[rows, :], dom[rows, :].astype(BF16))
        for b in range(DIL_NB):
            tok_scr[pl.ds(0, 1), _dil_tok(g, b, t, DIL_NB), :] = dq_scr[b * BAND:(b + 1) * BAND, :][None]

        @pl.when(t == nst - 1)
        def _():
            for r in range(d):
                rows = _strided(r, sub_len, d)
                tok_scr[pl.ds(1, 1), rows, :] = dk_scr[BAND + r * sub_len:BAND + (r + 1) * sub_len, :][None]
                tok_scr[pl.ds(2, 1), rows, :] = dv_scr[BAND + r * sub_len:BAND + (r + 1) * sub_len, :][None]
            lanes = _rope_lanes((ch, 128), DIL_ROPE_HALF, 64, 0)
            for c0 in range(0, SEQ, ch):
                rows = slice(c0, c0 + ch)
                cv, sv = c_ref[rows, :], sn_ref[rows, :]
                dp_ref[rows, 0:128] = _rope_bwd(tok_scr[0, rows, :], cv * DIL_SCALE, sv * DIL_SCALE, DIL_ROPE_HALF, lanes).astype(BF16)
                dp_ref[rows, 128:256] = _rope_bwd(tok_scr[1, rows, :], cv, sv, DIL_ROPE_HALF, lanes).astype(BF16)
                dp_ref[rows, 256:384] = tok_scr[2, rows, :].astype(BF16)

    def inp(tq):
        return pl.BlockSpec((1, BAND + SEQ, 128), lambda pr, t: (tq, 0, pr))

    tok_spec = pl.BlockSpec((SEQ, 128), lambda pr, t: (0, pr))
    tab = pl.BlockSpec((SEQ, 128), lambda pr, t: (0, 0))
    st = (DIL_ST, 2 * BAND)
    return pl.pallas_call(
        body, name=f"dil_attn_bwd_g{g}", grid=(4, nst),
        in_specs=[pl.BlockSpec(memory_space=pl.ANY), inp(0), inp(1), inp(2), tok_spec, tok_spec, tok_spec, tab, tab]
        + after_specs,
        out_specs=pl.BlockSpec((SEQ, 384), lambda pr, t: (0, _qkv_block(0, g, pr) // 3)),
        out_shape=jax.ShapeDtypeStruct((SEQ, N_PAD), BF16),
        input_output_aliases={0: 0},
        scratch_shapes=[pltpu.VMEM((3, SEQ, 128), F32),
                        pltpu.VMEM((BAND + SEQ, 128), F32), pltpu.VMEM((BAND + SEQ, 128), F32),
                        pltpu.VMEM(st, F32), pltpu.VMEM(st, F32), pltpu.VMEM(st, BF16), pltpu.VMEM(st, BF16),
                        pltpu.VMEM((DIL_ST, 128), F32), pltpu.VMEM((DIL_ST, 128), F32), pltpu.VMEM((DIL_ST, 128), F32),
                        pltpu.VMEM((DIL_ST, 128), F32)],
        compiler_params=_cparams(),
    )(dp_in, qkv, qkv, qkv, dyd, yd, lse_all, rc, rs, *after)


def _band_masks():
    row = lax.broadcasted_iota(jnp.int32, (BAND, BAND), 0)
    col = lax.broadcasted_iota(jnp.int32, (BAND, BAND), 1)
    return row >= col, col >= row


def _dil_attn_fwd(qkv, g):
    d = DIL_DILATIONS[g]
    nbs = SEQ // d // BAND
    nblk = SEQ // BAND

    def body(q_ref, k_ref, v_ref, o_ref, l_ref):
        keep_c, keep_p = _band_masks()
        half0 = _head_half((BAND, 128), 0)

        def step(i, carry):
            cur = pl.ds(pl.multiple_of(i * BAND, BAND), BAND)
            prv = pl.ds(pl.multiple_of(jnp.maximum(i - 1, 0) * BAND, BAND), BAND)
            has_prev = (i % nbs) != 0
            qv = q_ref[0, cur, :]
            kc, kp = k_ref[0, cur, :], k_ref[0, prv, :]
            vc, vp = v_ref[0, cur, :], v_ref[0, prv, :]
            outs, lses = [], []
            for hh in range(2):
                qm = jnp.where(_head_half((BAND, 128), hh), qv, jnp.zeros_like(qv))
                sc = jnp.where(keep_c, _nt(qm, kc), NEG)
                sp = jnp.where(keep_p & has_prev, _nt(qm, kp), NEG)
                m = jnp.maximum(jnp.max(sc, axis=-1, keepdims=True), jnp.max(sp, axis=-1, keepdims=True))
                pc, pp = jnp.exp(sc - m), jnp.exp(sp - m)
                den = jnp.sum(pc, axis=-1, keepdims=True) + jnp.sum(pp, axis=-1, keepdims=True)
                o = (_nn(pc.astype(BF16), vc) + _nn(pp.astype(BF16), vp)) / den
                outs.append(o)
                lses.append(jnp.broadcast_to(m + jnp.log(den), (BAND, 128)))
            tok = _strided((i % nbs) * BAND * d + i // nbs, BAND, d)
            o_ref[tok, :] = jnp.where(half0, outs[0], outs[1])
            l_ref[tok, :] = jnp.where(half0, lses[0], lses[1])
            return carry

        lax.fori_loop(0, nblk, step, 0, unroll=DIL_UNROLL)

    def inp(tq):
        return pl.BlockSpec((1, SEQ, 128), lambda pr: (tq, 0, pr))

    out = pl.BlockSpec((SEQ, 128), lambda pr: (0, pr))
    return pl.pallas_call(
        body, name=f"dil_attn_fwd_g{g}", grid=(4,),
        in_specs=[inp(0), inp(1), inp(2)], out_specs=[out, out],
        out_shape=[jax.ShapeDtypeStruct((SEQ, 512), F32), jax.ShapeDtypeStruct((SEQ, 512), F32)],
        compiler_params=_cparams(),
    )(qkv, qkv, qkv)


def _dil_attn_bwd(qkv, dyd, yd, lse_all, g):
    d = DIL_DILATIONS[g]
    sub_len = SEQ // d
    nbs = sub_len // BAND
    nblk = SEQ // BAND

    def body(q_ref, k_ref, v_ref, do_ref, y_ref, l_ref, out_ref, dk_scr, dv_scr):
        keep_c, keep_p = _band_masks()
        dk_scr[...] = jnp.zeros_like(dk_scr)
        dv_scr[...] = jnp.zeros_like(dv_scr)

        def step(i, carry):
            cur = pl.ds(pl.multiple_of(i * BAND, BAND), BAND)
            prv = pl.ds(pl.multiple_of(jnp.maximum(i - 1, 0) * BAND, BAND), BAND)
            has_prev = (i % nbs) != 0
            tok = _strided((i % nbs) * BAND * d + i // nbs, BAND, d)
            qv = q_ref[0, cur, :]
            kc, kp = k_ref[0, cur, :], k_ref[0, prv, :]
            vc, vp = v_ref[0, cur, :], v_ref[0, prv, :]
            dov, yv, lv = do_ref[tok, :], y_ref[tok, :], l_ref[tok, :]
            dq = jnp.zeros((BAND, 128), F32)
            dkc = jnp.zeros((BAND, 128), F32)
            dkp = jnp.zeros((BAND, 128), F32)
            dvc = jnp.zeros((BAND, 128), F32)
            dvp = jnp.zeros((BAND, 128), F32)
            for hh in range(2):
                half = _head_half((BAND, 128), hh)
                qm = jnp.where(half, qv, jnp.zeros_like(qv))
                lcol = jnp.max(jnp.where(half, lv, NEG), axis=-1, keepdims=True)
                pc = jnp.exp(jnp.where(keep_c, _nt(qm, kc), NEG) - lcol)
                pp = jnp.exp(jnp.where(keep_p & has_prev, _nt(qm, kp), NEG) - lcol)
                dom = jnp.where(half, dov, 0.0)
                domb = dom.astype(BF16)
                delta = jnp.sum(dom * yv, axis=-1, keepdims=True)
                dsc = (pc * (_nt(domb, vc) - delta)).astype(BF16)
                dsp = (pp * (_nt(domb, vp) - delta)).astype(BF16)
                dvc = dvc + _tn(pc.astype(BF16), domb)
                dvp = dvp + _tn(pp.astype(BF16), domb)
                dq = dq + jnp.where(half, _nn(dsc, kc) + _nn(dsp, kp), 0.0)
                dkc = dkc + jnp.where(half, _tn(dsc, qv), 0.0)
                dkp = dkp + jnp.where(half, _tn(dsp, qv), 0.0)
            out_ref[pl.ds(0, 1), tok, :] = dq[None]
            dk_scr[cur, :] += dkc
            dk_scr[prv, :] += dkp
            dv_scr[cur, :] += dvc
            dv_scr[prv, :] += dvp
            return carry

        lax.fori_loop(0, nblk, step, 0, unroll=DIL_UNROLL)
        for r in range(d):
            rows = _strided(r, sub_len, d)
            out_ref[pl.ds(1, 1), rows, :] = dk_scr[r * sub_len:(r + 1) * sub_len, :][None]
            out_ref[pl.ds(2, 1), rows, :] = dv_scr[r * sub_len:(r + 1) * sub_len, :][None]

    def inp(tq):
        return pl.BlockSpec((1, SEQ, 128), lambda pr: (tq, 0, pr))

    tok_spec = pl.BlockSpec((SEQ, 128), lambda pr: (0, pr))
    return pl.pallas_call(
        body, name=f"dil_attn_bwd_g{g}", grid=(4,),
        in_specs=[inp(0), inp(1), inp(2), tok_spec, tok_spec, tok_spec],
        out_specs=pl.BlockSpec((3, SEQ, 128), lambda pr: (0, 0, pr)),
        out_shape=jax.ShapeDtypeStruct((3, SEQ, 512), F32),
        scratch_shapes=[pltpu.VMEM((SEQ, 128), F32), pltpu.VMEM((SEQ, 128), F32)],
        compiler_params=_cparams(),
    )(qkv, qkv, qkv, dyd, yd, lse_all)


def _dil_prep_bwd(dp_in, dqkv, rc, rs, g):
    tm = 1024

    def body(dp_any, g_ref, c_ref, s_ref, dp_ref):
        del dp_any
        tq = pl.program_id(0)

        @pl.when(tq == 2)
        def _():
            dp_ref[...] = g_ref[0].astype(BF16)

        @pl.when(tq < 2)
        def _():
            mult = jnp.where(tq == 0, DIL_SCALE, 1.0).astype(F32)
            lanes = _rope_lanes((tm, 128), DIL_ROPE_HALF, 64, 0)
            cv, sv = c_ref[...], s_ref[...] * mult
            cv = cv * mult
            for pr in range(4):
                gv = g_ref[0, :, pr * 128:(pr + 1) * 128]
                dp_ref[:, pr * 128:(pr + 1) * 128] = _rope_bwd(gv, cv, sv, DIL_ROPE_HALF, lanes).astype(BF16)

    tab = pl.BlockSpec((tm, 128), lambda tq, i: (i, 0))
    return pl.pallas_call(
        body, name=f"dil_prep_bwd_g{g}", grid=(3, SEQ // tm),
        in_specs=[pl.BlockSpec(memory_space=pl.ANY),
                  pl.BlockSpec((1, tm, 512), lambda tq, i: (tq, i, 0)), tab, tab],
        out_specs=pl.BlockSpec((tm, 512), lambda tq, i: (i, COL_QKV // 512 + tq * 3 + g)),
        out_shape=jax.ShapeDtypeStruct((SEQ, N_PAD), BF16),
        input_output_aliases={0: 0},
    )(dp_in, dqkv, rc, rs)


TAIL_T = 256


def _tail(p, ya, o_g, l_g, x, target, wpm, wpd, wout, post_g):
    tm = TAIL_T

    def body(pgz_ref, ya_ref, o0_ref, o1_ref, o2_ref, l0_ref, l1_ref, l2_ref, x_ref, t_ref,
             wpm_ref, wpd_ref, wout_ref, pg_ref,
             dp_ref, dy_ref, mg_ref, dt_ref, ua_ref, dpa_ref, ud_ref, dpd_ref, dya_ref, dyd_ref,
             yd_ref, lse_ref, loss_ref, dgp_ref):
        l0, l1, l2 = l0_ref[...], l1_ref[...], l2_ref[...]
        mx = jnp.maximum(jnp.maximum(l0, l1), l2)
        e0, e1, e2 = jnp.exp(l0 - mx), jnp.exp(l1 - mx), jnp.exp(l2 - mx)
        den = e0 + e1 + e2
        yd = (e0 * o0_ref[...] + e1 * o1_ref[...] + e2 * o2_ref[...]) / den
        yd_ref[...] = yd
        lse_ref[...] = mx + jnp.log(den)
        ya = ya_ref[...]

        gm, gd = pgz_ref[:, 0:1024], pgz_ref[:, 1024:2048]
        zm, zd = pgz_ref[:, 2048:2560], pgz_ref[:, 2560:3072]
        szm, szd = _sigmoid(zm), _sigmoid(zd)
        sm, sd = zm * szm, zd * szd
        ua = (ya * sm).astype(BF16)
        ud = (yd * sd).astype(BF16)
        ua_ref[...] = ua
        ud_ref[...] = ud
        pa = _nn(ua, wpm_ref[...])
        pd = _nn(ud, wpd_ref[...])
        sgm, sgd = _sigmoid(gm), _sigmoid(gd)
        mg = (sgm * pa + sgd * pd).astype(BF16)
        mg_ref[...] = mg
        t = _nn(mg, wout_ref[...])
        r3 = lax.rsqrt(jnp.mean(t * t, axis=-1, keepdims=True) + EPS)
        n = t * r3
        pg = pg_ref[...]
        err = x_ref[...] + n * pg - t_ref[...]
        lpart = jnp.sum(err * err, axis=0, keepdims=True)

        dy = err * (1.0 / D_MODEL)
        dy_ref[...] = dy
        gpart = jnp.sum(dy * n, axis=0, keepdims=True)
        dn = dy * pg
        dt = (r3 * (dn - n * jnp.mean(dn * n, axis=-1, keepdims=True))).astype(BF16)
        dt_ref[...] = dt
        dmg = _nt(dt, wout_ref[...])
        dpa = (dmg * sgm).astype(BF16)
        dpd = (dmg * sgd).astype(BF16)
        dpa_ref[...] = dpa
        dpd_ref[...] = dpd
        dp_ref[:, 0:1024] = (dmg * pa * sgm * (1.0 - sgm)).astype(BF16)
        dp_ref[:, 1024:2048] = (dmg * pd * sgd * (1.0 - sgd)).astype(BF16)
        dua = _nt(dpa, wpm_ref[...])
        dud = _nt(dpd, wpd_ref[...])
        dya_ref[...] = dua * sm
        dyd_ref[...] = dud * sd
        dp_ref[:, 2048:2560] = (dua * ya * szm * (1.0 + zm * (1.0 - szm))).astype(BF16)
        dp_ref[:, 2560:3072] = (dud * yd * szd * (1.0 + zd * (1.0 - szd))).astype(BF16)

        @pl.when(pl.program_id(0) == 0)
        def _():
            loss_ref[...] = lpart
            dgp_ref[...] = gpart

        @pl.when(pl.program_id(0) > 0)
        def _():
            loss_ref[...] += lpart
            dgp_ref[...] += gpart

    def rows(w):
        return pl.BlockSpec((tm, w), lambda i: (i, 0))

    def full(shape):
        return pl.BlockSpec(shape, lambda i: (0, 0))

    def sds(w, dt):
        return jax.ShapeDtypeStruct((SEQ, w), dt)

    return pl.pallas_call(
        body, name="tail", grid=(SEQ // tm,),
        in_specs=[rows(3072), rows(512), rows(512), rows(512), rows(512), rows(512), rows(512), rows(512),
                  rows(1024), rows(1024), full((512, 1024)), full((512, 1024)), full((1024, 1024)), full((1, 1024))],
        out_specs=[rows(3072), rows(1024), rows(1024), rows(1024), rows(512), rows(1024), rows(512), rows(1024),
                   rows(512), rows(512), rows(512), rows(512), full((1, 1024)), full((1, 1024))],
        out_shape=[sds(N_PAD, BF16), sds(1024, F32), sds(1024, BF16), sds(1024, BF16), sds(512, BF16),
                   sds(1024, BF16), sds(512, BF16), sds(1024, BF16), sds(512, F32), sds(512, F32),
                   sds(512, F32), sds(512, F32),
                   jax.ShapeDtypeStruct((1, 1024), F32), jax.ShapeDtypeStruct((1, 1024), F32)],
        compiler_params=_cparams(),
    )(p, ya, o_g[0], o_g[1], o_g[2], l_g[0], l_g[1], l_g[2], x, target, wpm, wpd, wout, post_g)


def _sum_parts(recv, own, me, tr, name):
    n, r, w = recv.shape
    if r % tr:
        return _sum_parts_cols(recv, own, me, name)
    own_spec = (pl.BlockSpec((tr, w), lambda i, me_ref: (i, 0)) if own.ndim == 2
                else pl.BlockSpec((None, tr, w), lambda i, me_ref: (me_ref[0], i, 0)))

    def body(me_ref, p_ref, own_ref, o_ref):
        mine = own_ref[...].astype(F32)
        acc = jnp.zeros((tr, w), F32)
        for s in range(n):
            acc = acc + jnp.where(me_ref[0] == s, mine, p_ref[s].astype(F32))
        o_ref[...] = acc

    return pl.pallas_call(
        body, name=name,
        grid_spec=pltpu.PrefetchScalarGridSpec(
            num_scalar_prefetch=1, grid=(r // tr,),
            in_specs=[pl.BlockSpec((n, tr, w), lambda i, me_ref: (0, i, 0)), own_spec],
            out_specs=pl.BlockSpec((tr, w), lambda i, me_ref: (i, 0))),
        out_shape=jax.ShapeDtypeStruct((r, w), F32),
    )(me.reshape(1), recv, own)


def _sum_parts_cols(recv, own, me, name):
    n, r, w = recv.shape
    tc = 128

    def body(me_ref, p_ref, own_ref, o_ref):
        mine = own_ref[...].astype(F32)
        acc = jnp.zeros((r, tc), F32)
        for s in range(n):
            acc = acc + jnp.where(me_ref[0] == s, mine, p_ref[s].astype(F32))
        o_ref[...] = acc

    return pl.pallas_call(
        body, name=name,
        grid_spec=pltpu.PrefetchScalarGridSpec(
            num_scalar_prefetch=1, grid=(w // tc,),
            in_specs=[pl.BlockSpec((n, r, tc), lambda i, me_ref: (0, 0, i)),
                      pl.BlockSpec((None, r, tc), lambda i, me_ref: (me_ref[0], 0, i))],
            out_specs=pl.BlockSpec((r, tc), lambda i, me_ref: (0, i))),
        out_shape=jax.ShapeDtypeStruct((r, w), F32),
    )(me.reshape(1), recv, own)


def _adamw(w, g, m, v, name):
    lead = w.shape[:-2]
    r, c = w.shape[-2:]
    tr = max([t for t in range(8, 257, 8) if r % t == 0], default=r)
    c1 = 1.0 - ADAM_B1 ** ADAM_STEP
    c2 = 1.0 - ADAM_B2 ** ADAM_STEP

    def body(w_ref, g_ref, m_ref, v_ref, d_ref, nm_ref, nv_ref):
        gv = g_ref[...]
        nm = ADAM_B1 * m_ref[...] + (1.0 - ADAM_B1) * gv
        nv = ADAM_B2 * v_ref[...] + (1.0 - ADAM_B2) * (gv * gv)
        nm_ref[...] = nm
        nv_ref[...] = nv
        d_ref[...] = -ADAM_LR * ((nm / c1) / (jnp.sqrt(nv / c2) + ADAM_EPS) + ADAM_WD * w_ref[...])

    zeros = (0,) * len(lead)
    spec = pl.BlockSpec((1,) * len(lead) + (tr, c), lambda i: zeros + (i, 0))
    sd = jax.ShapeDtypeStruct(w.shape, F32)
    return pl.pallas_call(
        body, name=name, grid=(r // tr,),
        in_specs=[spec] * 4, out_specs=[spec] * 3, out_shape=[sd] * 3,
    )(w, g, m, v)


def _adamw_in(w_t, m_t, v_t, own_half, swapped, core):
    r, c = SHARD_SHAPES[0]
    tr = max(t for t in range(8, 257, 8) if r % t == 0)
    c1 = 1.0 - ADAM_B1 ** ADAM_STEP
    c2 = 1.0 - ADAM_B2 ** ADAM_STEP

    def body(core_ref, w_ref, m_ref, v_ref, own_ref, sw_ref, d_ref, nm_ref, nv_ref, g_ref):
        own = own_ref[...]
        col_half = lax.broadcasted_iota(jnp.int32, (tr, c), 1) // (c // 2)
        gv = jnp.where(col_half == core_ref[0], jnp.concatenate([own, own], axis=1), sw_ref[...])
        g_ref[0] = gv
        nm = ADAM_B1 * m_ref[0] + (1.0 - ADAM_B1) * gv
        nv = ADAM_B2 * v_ref[0] + (1.0 - ADAM_B2) * (gv * gv)
        nm_ref[0] = nm
        nv_ref[0] = nv
        d_ref[0] = -ADAM_LR * ((nm / c1) / (jnp.sqrt(nv / c2) + ADAM_EPS) + ADAM_WD * w_ref[0])

    full = pl.BlockSpec((1, tr, c), lambda i, core_ref: (0, i, 0))
    sd = jax.ShapeDtypeStruct((1, r, c), F32)
    return pl.pallas_call(
        body, name="adamw_in",
        grid_spec=pltpu.PrefetchScalarGridSpec(
            num_scalar_prefetch=1, grid=(r // tr,),
            in_specs=[full, full, full, pl.BlockSpec((tr, c // 2), lambda i, core_ref: (i, 0)),
                      pl.BlockSpec((tr, c), lambda i, core_ref: (i, 0))],
            out_specs=[full] * 4),
        out_shape=[sd] * 4,
    )(core.reshape(1), w_t, m_t, v_t, own_half, swapped)


ANY = pl.BlockSpec(memory_space=pl.ANY)


def _my_place():
    return lax.axis_index("x"), lax.axis_index("y"), lax.axis_index("c")


HBM = pl.BlockSpec(memory_space=pltpu.HBM)
SEM = pl.BlockSpec(memory_space=pltpu.SEMAPHORE)
DATAFLOW = pltpu.SideEffectType.DATAFLOW_SIDE_EFFECTING


def _near_chips(x, y):
    return [(1 - x, y), (x, 1 - y)]


def _half(mi, hc):
    r, c = SHARD_SHAPES[mi]
    if mi == 0:
        return pl.ds(0, r), pl.ds(pl.multiple_of(hc * (c // 2), 128), c // 2)
    return pl.ds(pl.multiple_of(hc * (r // 2), 16), r // 2), pl.ds(0, c)


def _gather_copies(m_refs, land_refs, send_sems, recv_sems):
    x, y, c = _my_place()
    out, back = [], []
    for mi in range(N_MATS):
        rows, cols = _half(mi, c)
        for j, (cx, cy) in enumerate(_near_chips(x, y)):
            sems = dict(send_sem=send_sems.at[mi * 2 + j], recv_sem=recv_sems.at[mi * 2 + j],
                        device_id=(cx, cy, c), device_id_type=MESH)
            out.append(pltpu.make_async_remote_copy(src_ref=m_refs[mi].at[rows, cols],
                                                    dst_ref=land_refs[mi].at[2 * x + y, rows, cols], **sems))
            got = land_refs[mi].at[2 * cx + cy, rows, cols]
            back.append(pltpu.make_async_remote_copy(src_ref=got, dst_ref=got, **sems))
    return out, back


def _gather_start(mats, landing):
    n = N_MATS

    def body(*refs):
        out, _ = _gather_copies(refs[:n], refs[n:2 * n], refs[2 * n], refs[2 * n + 1])
        for cp in out:
            cp.start()
        refs[-1][...] = jnp.zeros_like(refs[-1])

    hbm = [pltpu.HBM(a.shape, a.dtype) for a in list(mats) + list(landing)]
    outs = pl.pallas_call(
        body, name="gather_start",
        out_shape=(pltpu.SemaphoreType.DMA((2 * n,)), pltpu.SemaphoreType.DMA((2 * n,)), *hbm,
                   jax.ShapeDtypeStruct((8, 128), F32)),
        in_specs=[HBM] * (2 * n), out_specs=(SEM, SEM, *[HBM] * (2 * n), pl.BlockSpec(memory_space=pltpu.VMEM)),
        input_output_aliases={i: 2 + i for i in range(2 * n)},
        compiler_params=pltpu.CompilerParams(has_side_effects=DATAFLOW),
    )(*[pltpu.with_memory_space_constraint(a, pltpu.HBM) for a in list(mats) + list(landing)])
    return outs[:-1], outs[-1]


def _gather_wait(handle, after):
    n = N_MATS

    def body(*refs):
        out, back = _gather_copies(refs[:n], refs[n:2 * n], refs[2 * n], refs[2 * n + 1])
        for cp, arrival in zip(out, back):
            cp.wait_send()
            arrival.wait_recv()

    bufs = handle[2:]
    after, after_specs = _after(after)
    res = pl.pallas_call(
        body, name="gather_wait", out_shape=tuple(pltpu.HBM(b.shape, b.dtype) for b in bufs),
        in_specs=[HBM] * (2 * n) + [SEM, SEM] + after_specs, out_specs=tuple([HBM] * (2 * n)),
        input_output_aliases={i: i for i in range(2 * n)},
        compiler_params=pltpu.CompilerParams(has_side_effects=DATAFLOW),
    )(*bufs, handle[0], handle[1], *after)
    return list(res[n:])


def _relay_share(gathered):
    n = N_MATS

    def body(*refs):
        out_refs = refs[n:2 * n]
        send_sems, recv_sems = refs[2 * n:]
        x, y, c = _my_place()
        sibling = (x, y, 1 - c)
        relayed = 2 * (x ^ (1 - c)) + (y ^ c)
        relay_to = (x ^ c, y ^ (1 - c), c)
        far = 2 * (1 - x) + (1 - y)
        near = [2 * (1 - x) + y, 2 * x + (1 - y)]

        def copy(k, mi, shard, hc, to):
            blk = out_refs[mi].at[(shard,) + _half(mi, hc)]
            return pltpu.make_async_remote_copy(src_ref=blk, dst_ref=blk, send_sem=send_sems.at[mi * 4 + k],
                                                recv_sem=recv_sems.at[mi * 4 + k], device_id=to, device_id_type=MESH)

        sends = []
        for mi in range(n):
            sends.append(copy(0, mi, relayed, c, relay_to))
            sends += [copy(1 + j, mi, near[j], c, sibling) for j in range(2)]
        for cp in sends:
            cp.start()
        for mi in range(n):
            copy(0, mi, far, c, relay_to).wait_recv()
            cp = copy(3, mi, far, c, sibling)
            cp.start()
            sends.append(cp)
        for mi in range(n):
            for j in range(2):
                copy(1 + j, mi, near[j], 1 - c, sibling).wait_recv()
            copy(3, mi, far, 1 - c, sibling).wait_recv()
        for cp in sends:
            cp.wait_send()

    return pl.pallas_call(
        body, name="relay_share",
        in_specs=[ANY] * n, out_specs=[ANY] * n,
        out_shape=[jax.ShapeDtypeStruct(g.shape, g.dtype) for g in gathered],
        input_output_aliases={i: i for i in range(n)},
        scratch_shapes=[pltpu.SemaphoreType.DMA((4 * n,)), pltpu.SemaphoreType.DMA((4 * n,))],
    )(*gathered)


def _peers(x, y, c):
    out = []
    for k in range(1, 8):
        px, py, pc = x ^ (k >> 2), y ^ ((k >> 1) & 1), c ^ (k & 1)
        out.append((k - 1, (px, py, pc), 4 * px + 2 * py + pc))
    return out


def _exchange_start(parts, name):
    n = len(parts)

    def body(*refs):
        p_refs, land_refs = refs[:n], refs[n:2 * n]
        send_sems, recv_sems, token = refs[2 * n], refs[2 * n + 1], refs[-1]
        x, y, c = _my_place()
        me = 4 * x + 2 * y + c
        for k, dev, peer in _peers(x, y, c):
            for mi in range(n):
                pltpu.make_async_remote_copy(
                    src_ref=p_refs[mi].at[peer], dst_ref=land_refs[mi].at[me], send_sem=send_sems.at[k * n + mi],
                    recv_sem=recv_sems.at[k * n + mi], device_id=dev, device_id_type=MESH).start()
        token[...] = jnp.zeros_like(token)

    hbm = [pltpu.HBM(p.shape, p.dtype) for p in parts]
    outs = pl.pallas_call(
        body, name=name + "_start",
        out_shape=(pltpu.SemaphoreType.DMA((7 * n,)), pltpu.SemaphoreType.DMA((7 * n,)), *hbm, *hbm,
                   jax.ShapeDtypeStruct((8, 128), F32)),
        in_specs=[HBM] * (2 * n), out_specs=(SEM, SEM, *[HBM] * (2 * n), pl.BlockSpec(memory_space=pltpu.VMEM)),
        input_output_aliases={i: 2 + i for i in range(2 * n)},
        compiler_params=pltpu.CompilerParams(has_side_effects=DATAFLOW),
    )(*[pltpu.with_memory_space_constraint(p, pltpu.HBM) for p in parts],
      *[pltpu.with_memory_space_constraint(lax.empty(p.shape, p.dtype), pltpu.HBM) for p in parts])
    return (name, outs[:-1]), outs[-1]


def _exchange_wait(handle, after):
    name, outs = handle
    n = (len(outs) - 2) // 2

    def body(*refs):
        p_refs, land_refs = refs[:n], refs[n:2 * n]
        send_sems, recv_sems = refs[2 * n], refs[2 * n + 1]
        x, y, c = _my_place()
        me = 4 * x + 2 * y + c
        for k, dev, peer in _peers(x, y, c):
            for mi in range(n):
                pltpu.make_async_remote_copy(
                    src_ref=p_refs[mi].at[peer], dst_ref=land_refs[mi].at[me], send_sem=send_sems.at[k * n + mi],
                    recv_sem=recv_sems.at[k * n + mi], device_id=dev, device_id_type=MESH).wait_send()
                slot = land_refs[mi].at[peer]
                pltpu.make_async_remote_copy(
                    src_ref=slot, dst_ref=slot, send_sem=send_sems.at[k * n + mi],
                    recv_sem=recv_sems.at[k * n + mi], device_id=dev, device_id_type=MESH).wait_recv()

    bufs = outs[2:]
    res = pl.pallas_call(
        body, name=name + "_wait", out_shape=tuple(pltpu.HBM(b.shape, b.dtype) for b in bufs),
        in_specs=[HBM] * (2 * n) + [SEM, SEM, ANY], out_specs=tuple([HBM] * (2 * n)),
        input_output_aliases={i: i for i in range(2 * n)},
        compiler_params=pltpu.CompilerParams(has_side_effects=DATAFLOW),
    )(*bufs, outs[0], outs[1], after)
    return list(res[n:])


def _swap_halves(halves, gvec):
    def place(ref, mi, hc):
        return ref.at[:, pl.ds(pl.multiple_of(hc * 512, 128), 512)] if mi == 0 else ref.at[hc]

    def body(*refs):
        g_refs, gv_ref = refs[:N_MATS], refs[N_MATS]
        out_refs, rg_ref = refs[N_MATS + 1:2 * N_MATS + 1], refs[2 * N_MATS + 1]
        send_sems, recv_sems = refs[2 * N_MATS + 2:]
        x, y, c = _my_place()
        me = 4 * x + 2 * y + c
        sends = []
        for mi in range(N_MATS):
            cp = pltpu.make_async_remote_copy(src_ref=g_refs[mi], dst_ref=place(out_refs[mi], mi, c), send_sem=send_sems.at[mi],
                                              recv_sem=recv_sems.at[mi], device_id=(x, y, 1 - c), device_id_type=MESH)
            cp.start()
            sends.append(cp)
        for k, dev, peer in _peers(x, y, c):
            cp = pltpu.make_async_remote_copy(src_ref=gv_ref, dst_ref=rg_ref.at[me], send_sem=send_sems.at[N_MATS + k],
                                              recv_sem=recv_sems.at[N_MATS + k], device_id=dev, device_id_type=MESH)
            cp.start()
            sends.append(cp)
        for mi in range(N_MATS):
            got = place(out_refs[mi], mi, 1 - c)
            pltpu.make_async_remote_copy(src_ref=got, dst_ref=got, send_sem=send_sems.at[mi], recv_sem=recv_sems.at[mi],
                                         device_id=(x, y, 1 - c), device_id_type=MESH).wait_recv()
        for k, dev, peer in _peers(x, y, c):
            got = rg_ref.at[peer]
            pltpu.make_async_remote_copy(src_ref=got, dst_ref=got, send_sem=send_sems.at[N_MATS + k],
                                         recv_sem=recv_sems.at[N_MATS + k], device_id=dev, device_id_type=MESH).wait_recv()
        for cp in sends:
            cp.wait_send()

    outs = pl.pallas_call(
        body, name="swap_halves",
        in_specs=[ANY] * (N_MATS + 1), out_specs=[ANY] * (N_MATS + 1),
        out_shape=[jax.ShapeDtypeStruct(SHARD_SHAPES[0], F32)]
        + [jax.ShapeDtypeStruct((2, r // 2, c), F32) for r, c in SHARD_SHAPES[1:]]
        + [jax.ShapeDtypeStruct((8, 8, N_GVEC), F32)],
        scratch_shapes=[pltpu.SemaphoreType.DMA((N_MATS + 7,)), pltpu.SemaphoreType.DMA((N_MATS + 7,))],
    )(*halves, gvec)
    return outs[:N_MATS], outs[N_MATS]


def _set_slot(arr, block, idx):
    return lax.dynamic_update_slice(arr, block[None], (idx,) + (0,) * block.ndim)


PAD_RUNS = ((6304, 8352, 0), (5280, 6304, COL_Z), (672, 5280, COL_QKV), (0, 640, COL_LAT), (640, 672, COL_LAT + 704))
N_QKV = COL_LAT - COL_QKV


def _qkv_rows_regroup(a, to_padded):
    if to_padded:
        a4 = a.reshape(3, 12, 128, a.shape[1])
        return jnp.stack([a4[0], a4[1], a4[2]], axis=1).reshape(a.shape)
    a4 = a.reshape(12, 3, 128, a.shape[1])
    return jnp.concatenate([a4[:, tq].reshape(N_QKV // 3, a.shape[1]) for tq in range(3)], axis=0)
W_IN_SHARD = 2088


def _full_weights(gathered):
    def cols(a):
        return jnp.concatenate([a[s] for s in range(4)], axis=1)

    w_uq, w_ukv, w_pm, w_pd = [cols(a) for a in gathered[1:5]]
    w_out = gathered[5].reshape(D_MODEL, D_MODEL)
    w_in_t = gathered[0].reshape(4 * W_IN_SHARD, D_MODEL)
    pieces, at = [], 0
    for lo, hi, pad_lo in sorted(PAD_RUNS, key=lambda t: t[2]):
        if pad_lo > at:
            pieces.append(jnp.zeros((pad_lo - at, D_MODEL), w_in_t.dtype))
        pieces.append(_qkv_rows_regroup(w_in_t[lo:hi], True) if pad_lo == COL_QKV else w_in_t[lo:hi])
        at = pad_lo + hi - lo
    pieces.append(jnp.zeros((N_PAD - at, D_MODEL), w_in_t.dtype))
    w_pad_t = jnp.concatenate(pieces, axis=0)
    z32 = jnp.zeros((Q_RANK, 32), w_uq.dtype)
    wuq_pad = jnp.concatenate([t for h in range(MLA_HEADS) for t in (w_uq[:, h * 96:(h + 1) * 96], z32)], axis=1)
    z64 = jnp.zeros((KV_RANK, 64), w_ukv.dtype)
    wk_pad = jnp.concatenate([t for h in range(MLA_HEADS) for t in (w_ukv[:, h * 128:h * 128 + 64], z64)], axis=1)
    wv = jnp.concatenate([w_ukv[:, h * 128 + 64:(h + 1) * 128] for h in range(MLA_HEADS)], axis=1)
    return w_pad_t.T, w_pad_t, wuq_pad, wk_pad, wv, w_pm, w_pd, w_out


W_IN_LAT = 672


def _grad_parts_in_early(dwt_early):
    dwt_early = jnp.concatenate([dwt_early[:COL_QKV], _qkv_rows_regroup(dwt_early[COL_QKV:COL_LAT], False)], axis=0)

    def in_block(s, h):
        cols = slice(h * 512, (h + 1) * 512)
        out = []
        for lo, hi, pad_lo in sorted(PAD_RUNS):
            a_, b_ = max(lo, s * W_IN_SHARD), min(hi, (s + 1) * W_IN_SHARD)
            if a_ < b_:
                out.append(jnp.zeros((b_ - a_, 512), dwt_early.dtype) if pad_lo >= COL_LAT
                           else dwt_early[pad_lo + a_ - lo:pad_lo + b_ - lo, cols])
        return jnp.concatenate(out, axis=0)

    return jnp.stack([in_block(s, h) for s in range(4) for h in range(2)])


def _grad_parts_in_late(dwt_late):
    rows = jnp.concatenate([dwt_late[0:640], dwt_late[704:736]], axis=0)
    zero = jnp.zeros((W_IN_LAT, 512), dwt_late.dtype)
    return jnp.stack([rows[:, 0:512], rows[:, 512:1024]] + [zero] * 6)


def _col_blocks(m):
    r, c = m.shape[0] // 2, m.shape[1] // 4
    return jnp.stack([m[h * r:(h + 1) * r, s * c:(s + 1) * c] for s in range(4) for h in range(2)])


def _grad_parts_mla(dwuq_pad, dwk_pad, dwv):
    d_uq = jnp.concatenate([dwuq_pad[:, h * 128:h * 128 + 96] for h in range(MLA_HEADS)], axis=1)
    d_ukv = jnp.concatenate([t for h in range(MLA_HEADS) for t in (dwk_pad[:, h * 128:h * 128 + 64], dwv[:, h * 64:(h + 1) * 64])],
                            axis=1)
    return [_col_blocks(d_uq), _col_blocks(d_ukv)]


def _rope_tables(positions, token=None):
    pos = positions.reshape(SEQ).astype(F32)
    if token is not None:
        pos = pos + token[0, 0]
    lane = jnp.arange(128)

    def table(rot, first, period):
        inv = ROPE_THETA ** (-jnp.arange(0, rot, 2, dtype=F32) / rot)
        half = rot // 2
        off = lane % period - first
        in1, in2 = (off >= 0) & (off < half), (off >= half) & (off < rot)
        inv_lane = jnp.where(in1 | in2, inv[jnp.clip(off % half, 0, half - 1)], 0.0)
        sign = jnp.where(in1, -1.0, 1.0).astype(F32)
        ang = pos[:, None] * inv_lane[None, :]
        return jnp.cos(ang), jnp.sin(ang) * sign[None, :]

    return table(32, 64, 128), table(16, 0, 64)


class _Links:
    def __init__(self, mats, chip, me):
        landing = [_set_slot(lax.empty((4,) + m.shape, m.dtype), m, chip) for m in mats]
        self.gather, self.token = _gather_start(mats, landing)
        self.me, self.sent, self.handles, self.sums = me, {}, {}, {}

    def weights(self, after):
        return _relay_share(_gather_wait(self.gather, after))

    def send(self, blocks, name):
        self.sent[name] = blocks
        self.handles[name], token = _exchange_start(blocks, name)
        return token

    def collect(self, name, after, parts):
        recv = _exchange_wait(self.handles[name], after)
        for r, own, part in zip(recv, self.sent[name], parts):
            self.sums[part] = _sum_parts(r, own, self.me, 64, "sum_grad_" + part)
        return tuple(self.sums[part] for part in parts)


def _device_grads(x, positions, target, gains, links):
    pre_g, q_g, kv_g, post_g = gains
    (mc, ms), (dc, ds) = _rope_tables(positions, links.token)
    h = _prenorm_fwd(x, pre_g, links.token)
    w_pad, w_pad_t, wuq_pad, wk_pad, wv, w_pm, w_pd, w_out = _full_weights(links.weights((h, mc, ms, dc, ds)))

    p = _matmul(h, w_pad, "nn", F32, 1024, 1408, 1024, "in_proj")
    cqn, ckvn, q, k, v = _mla_prep_fwd(p, q_g, kv_g, wuq_pad, wk_pad, wv, mc, ms)
    ya, lse_m = _mla_flash_fwd(q, k, v)
    qkv = [_dil_prep_fwd(p, dc, ds, g) for g in range(3)]
    o_g, l_g = zip(*[_dil_attn_fwd2(qkv[g], g) for g in range(3)])
    (dp, dy, mg, dt, ua, dpa, ud, dpd, dya, dyd, yd, lse_d, loss_cols, dg_post) = _tail(
        p, ya, o_g, l_g, x, target, w_pm, w_pd, w_out, post_g)

    for g in range(3):
        dp = _dil_attn_bwd2(dp, qkv[g], dyd, yd, lse_d, dc, ds, g)
    dw_early = _matmul(dp, h, "tn", BF16, 1536, 1024, 2048, "dw_in_early", a_cols=(0, COL_LAT // 1536))
    dwpm = _matmul(ua, dpa, "tn", BF16, 512, 1024, 512, "dw_proj_mla")
    dwpd = _matmul(ud, dpd, "tn", BF16, 512, 1024, 512, "dw_proj_dil")
    dwout = _matmul(mg, dt, "tn", BF16, 1024, 1024, 512, "dw_out")
    token = links.send([_grad_parts_in_early(dw_early), _col_blocks(dwpm), _col_blocks(dwpd),
                        dwout.reshape(8, 128, D_MODEL)], "exchange_early")

    dq, dk, dv = _mla_flash_bwd(q, k, v, ya, dya, lse_m, token)
    dp, dqb, dkb, dvb, dg_q, dg_kv = _mla_prep_bwd(dp, p, dq, dk, dv, q_g, kv_g, wuq_pad, wk_pad, wv, mc, ms)
    dwuq_pad = _matmul(cqn, dqb, "tn", BF16, Q_RANK, 1024, 512, "dw_uq")
    dwk_pad = _matmul(ckvn, dkb, "tn", BF16, KV_RANK, 1024, 512, "dw_k")
    dwv = _matmul(ckvn, dvb, "tn", BF16, KV_RANK, 512, 512, "dw_v")
    dw_late = _matmul(dp, h, "tn", BF16, N_LAT, 1024, 2048, "dw_in_late", a_cols=(COL_LAT // N_LAT, 1))
    token = links.send([_grad_parts_in_late(dw_late)] + _grad_parts_mla(dwuq_pad, dwk_pad, dwv), "exchange_late")
    early = links.collect("exchange_early", dw_late, ("in_early", "pm", "pd", "out"))

    grad_x, dg_pre = _dh_prenorm_bwd(dp, w_pad_t, x, dy, pre_g, (token,) + tuple(early))
    links.collect("exchange_late", grad_x, ("in_late", "uq", "ukv"))

    loss_part = jnp.pad((jnp.sum(loss_cols) * (0.5 / D_MODEL)).reshape(1, 1), ((0, 0), (0, N_GVEC - N_GAINS - 1)))
    gvec = jnp.concatenate([dg_pre, dg_q, dg_kv, dg_post, loss_part], axis=1)
    return grad_x, gvec


def kernel(x, positions, pre_norm_g, w_in, q_norm_g, w_uq, kv_norm_g, w_ukv, w_proj_mla, w_proj_dil, w_out, post_norm_g, loss_target, m_pre_norm_g, m_w_in, m_q_norm_g, m_w_uq, m_kv_norm_g, m_w_ukv, m_w_proj_mla, m_w_proj_dil, m_w_out, m_post_norm_g, v_pre_norm_g, v_w_in, v_q_norm_g, v_w_uq, v_kv_norm_g, v_w_ukv, v_w_proj_mla, v_w_proj_dil, v_w_out, v_post_norm_g):
    xi, yi, ci = _my_place()
    chip, me = 2 * xi + yi, 4 * xi + 2 * yi + ci
    mats = [jnp.swapaxes(w_in, 1, 2)] + [w_uq, w_ukv, w_proj_mla, w_proj_dil, w_out]
    mats = [w.reshape(w.shape[1:]).astype(BF16) for w in mats]
    links = _Links(mats, chip, me)
    gains = (pre_norm_g, q_norm_g, kv_norm_g, post_norm_g)
    grad_x, gvec = _device_grads(x[0], positions, loss_target[0], gains, links)

    sums = links.sums
    in_e = sums["in_early"]
    half_in = jnp.concatenate([in_e[:W_IN_LAT] + jnp.where(chip == 0, sums["in_late"], 0.0), in_e[W_IN_LAT:]], axis=0)
    halves = [half_in, sums["uq"], sums["ukv"], sums["pm"], sums["pd"], sums["out"]]
    gvec8 = jnp.pad(gvec, ((0, 7), (0, 0)))
    swapped, recv_gains = _swap_halves(halves, gvec8)
    g_gains = _sum_parts(recv_gains, gvec8, me, 8, "sum_gain_parts")[0:1]
    loss = g_gains[0, N_GAINS]
    sw = lambda a: jnp.swapaxes(a, 1, 2)
    d_in, m_in, v_in, g_in = [sw(o) for o in _adamw_in(sw(w_in), sw(m_w_in), sw(v_w_in), half_in, swapped[0], ci)]
    g_mats = [g_in] + [_set_slot(s, hf, ci).reshape((1,) + shp)
                       for s, hf, shp in zip(swapped[1:], halves[1:], SHARD_SHAPES[1:])]

    off = [0, 1024, 1408, 1664, 2688]
    g_gain = [g_gains[:, off[i]:off[i + 1]] for i in range(4)]
    grads = [g_gain[0], g_mats[0], g_gain[1], g_mats[1], g_gain[2], g_mats[2], g_mats[3], g_mats[4], g_mats[5], g_gain[3]]
    ws = [pre_norm_g, w_in, q_norm_g, w_uq, kv_norm_g, w_ukv, w_proj_mla, w_proj_dil, w_out, post_norm_g]
    ms = [m_pre_norm_g, m_w_in, m_q_norm_g, m_w_uq, m_kv_norm_g, m_w_ukv, m_w_proj_mla, m_w_proj_dil, m_w_out, m_post_norm_g]
    vs = [v_pre_norm_g, v_w_in, v_q_norm_g, v_w_uq, v_kv_norm_g, v_w_ukv, v_w_proj_mla, v_w_proj_dil, v_w_out, v_post_norm_g]
    deltas, new_m, new_v = [], [], []
    for i, (w, g, m, v) in enumerate(zip(ws, grads, ms, vs)):
        if w is w_in:
            d_, m_, v_ = d_in, m_in, v_in
        elif w.shape[-1] % 128 and w.shape[-2] % 128 == 0:
            g = jnp.swapaxes(g, 1, 2)
            grads[i] = jnp.swapaxes(g, 1, 2)
            d_, m_, v_ = [jnp.swapaxes(o, 1, 2) for o in
                          _adamw(jnp.swapaxes(w, 1, 2), g, jnp.swapaxes(m, 1, 2), jnp.swapaxes(v, 1, 2), f"adamw_{i}")]
        else:
            d_, m_, v_ = _adamw(w, g, m, v, f"adamw_{i}")
        deltas.append(d_)
        new_m.append(m_)
        new_v.append(v_)
    return (loss, grad_x.reshape(x.shape), *grads, *deltas, *new_m, *new_v)
```

```python
import jax
import jax.numpy as jnp
from jax import lax
from jax.experimental import pallas as pl
from jax.experimental.pallas import tpu as pltpu

F32 = jnp.float32
BF16 = jnp.bfloat16

SEQ = 4096
D_MODEL = 1024
EPS = 1e-6
ROPE_THETA = 500000.0
MLA_HEADS = 8
Q_RANK = 384
KV_RANK = 256
MLA_SCALE = 96.0 ** -0.5
MLA_ROPE_HALF = 16
DIL_DILATIONS = (1, 4, 16)
DIL_ROPE_HALF = 8
DIL_SCALE = 0.125
BAND = 128

N_LAT = 768
COL_Z, COL_QKV, COL_LAT = 2048, 3072, 7680
N_PAD = 8448


def _qkv_block(tq, g, pr):
    return COL_QKV // 128 + (g * 4 + pr) * 3 + tq

IN_SPLITS = (384, 256, 32, 4608, 512, 512, 1024, 1024)

SHARD_SHAPES = ((2088, 1024), (384, 192), (256, 256), (512, 256), (512, 256), (256, 1024))
N_MATS = len(SHARD_SHAPES)
N_GAINS = 2688
N_GVEC = N_GAINS + 128

ADAM_LR, ADAM_B1, ADAM_B2, ADAM_EPS, ADAM_WD, ADAM_STEP = 0.001, 0.9, 0.999, 1e-08, 0.01, 10

VMEM_LIMIT = 56 * 1024 * 1024
NEG = -1e30
MESH = pl.DeviceIdType.MESH


def _cparams(**kw):
    return pltpu.CompilerParams(vmem_limit_bytes=VMEM_LIMIT, **kw)


def _dot(a, b, dims):
    return lax.dot_general(a, b, (dims, ((), ())), preferred_element_type=F32)


def _nn(a, b):
    return _dot(a, b, ((1,), (0,)))


def _nt(a, b):
    return _dot(a, b, ((1,), (1,)))


def _tn(a, b):
    return _dot(a, b, ((0,), (0,)))


def _rope_lanes(shape, half, period, first):
    lane = lax.broadcasted_iota(jnp.int32, shape, len(shape) - 1) % period
    return (lane >= first) & (lane < first + half), (lane >= first + half) & (lane < first + 2 * half)


def _rope_fwd(x, c, s, half, lanes):
    x1, _ = lanes
    return x * c + jnp.where(x1, pltpu.roll(x, 128 - half, 1), pltpu.roll(x, half, 1)) * s


def _rope_bwd(g, c, s, half, lanes):
    x1, x2 = lanes
    gs = g * s
    return g * c + jnp.where(x2, pltpu.roll(gs, half, 1), jnp.where(x1, pltpu.roll(gs, 128 - half, 1), 0.0))


def _sigmoid(x):
    return 1.0 / (1.0 + jnp.exp(-x))


def _after(token):
    tokens = [t for t in (token if isinstance(token, (tuple, list)) else [token]) if t is not None]
    return tokens, [pl.BlockSpec(memory_space=pl.ANY)] * len(tokens)


def _matmul(a, b, mode, out_dtype, tm, tn, tk, name, token=None, b_cols=None, a_cols=None):
    after, after_specs = _after(token)
    if mode == "nn":
        (m, k), n = a.shape, b.shape[1]
        first = 0
        if b_cols is not None:
            first, n = b_cols[0], b_cols[1] * tn
        a_spec = pl.BlockSpec((tm, tk), lambda j, i, kk: (i, kk))
        b_spec = pl.BlockSpec((tk, tn), lambda j, i, kk: (kk, j + first))
        dot = _nn
    elif mode == "nt":
        (m, k), n = a.shape, b.shape[0]
        a_spec = pl.BlockSpec((tm, tk), lambda j, i, kk: (i, kk))
        b_spec = pl.BlockSpec((tn, tk), lambda j, i, kk: (j, kk))
        dot = _nt
    else:
        (k, m), n = a.shape, b.shape[1]
        first = 0
        if a_cols is not None:
            first, m = a_cols[0], a_cols[1] * tm
        a_spec = pl.BlockSpec((tk, tm), lambda j, i, kk: (kk, i + first))
        b_spec = pl.BlockSpec((tk, tn), lambda j, i, kk: (kk, j))
        dot = _tn
    assert m % tm == 0 and n % tn == 0 and k % tk == 0, (name, m, n, k, tm, tn, tk)
    nk = k // tk

    def body(a_ref, b_ref, *rest):
        o_ref, acc_ref = rest[-2:]
        kk = pl.program_id(2)
        part = dot(a_ref[...], b_ref[...])

        @pl.when(kk == 0)
        def _():
            acc_ref[...] = part

        @pl.when(kk > 0)
        def _():
            acc_ref[...] += part

        @pl.when(kk == nk - 1)
        def _():
            o_ref[...] = acc_ref[...].astype(o_ref.dtype)

    return pl.pallas_call(
        body, name=name, grid=(n // tn, m // tm, nk),
        in_specs=[a_spec, b_spec] + after_specs,
        out_specs=pl.BlockSpec((tm, tn), lambda j, i, kk: (i, j)),
        out_shape=jax.ShapeDtypeStruct((m, n), out_dtype),
        scratch_shapes=[pltpu.VMEM((tm, tn), F32)],
        compiler_params=_cparams(),
    )(a, b, *after)


def _prenorm_fwd(x, g, token=None):
    tm = 512
    after, after_specs = _after(token)

    def body(x_ref, g_ref, *rest):
        xv = x_ref[...]
        r = lax.rsqrt(jnp.mean(xv * xv, axis=-1, keepdims=True) + EPS)
        rest[-1][...] = (xv * r * g_ref[...]).astype(BF16)

    return pl.pallas_call(
        body, name="prenorm_fwd", grid=(SEQ // tm,),
        in_specs=[pl.BlockSpec((tm, D_MODEL), lambda i: (i, 0)), pl.BlockSpec((1, D_MODEL), lambda i: (0, 0))] + after_specs,
        out_specs=pl.BlockSpec((tm, D_MODEL), lambda i: (i, 0)),
        out_shape=jax.ShapeDtypeStruct((SEQ, D_MODEL), BF16),
    )(x, g, *after)


def _dh_prenorm_bwd(dp, w_pad_t, x, dy, g, token=None):
    tm, tk = 1024, 1408
    nk = N_PAD // tk
    after, after_specs = _after(token)

    def body(a_ref, b_ref, x_ref, dy_ref, g_ref, *rest):
        gx_ref, dg_ref, acc_ref = rest[-3:]
        i, kk = pl.program_id(0), pl.program_id(1)
        part = _nn(a_ref[...], b_ref[...])

        @pl.when(kk == 0)
        def _():
            acc_ref[...] = part

        @pl.when(kk > 0)
        def _():
            acc_ref[...] += part

        @pl.when(kk == nk - 1)
        def _():
            xv = x_ref[...]
            r = lax.rsqrt(jnp.mean(xv * xv, axis=-1, keepdims=True) + EPS)
            n = xv * r
            dhv = acc_ref[...]
            dn = dhv * g_ref[...]
            gx_ref[...] = dy_ref[...] + r * (dn - n * jnp.mean(dn * n, axis=-1, keepdims=True))
            cols = jnp.sum(dhv * n, axis=0, keepdims=True)

            @pl.when(i == 0)
            def _():
                dg_ref[...] = cols

            @pl.when(i > 0)
            def _():
                dg_ref[...] += cols

    row = pl.BlockSpec((tm, D_MODEL), lambda i, kk: (i, 0))
    vec = pl.BlockSpec((1, D_MODEL), lambda i, kk: (0, 0))
    return pl.pallas_call(
        body, name="dh_prenorm_bwd", grid=(SEQ // tm, nk),
        in_specs=[pl.BlockSpec((tm, tk), lambda i, kk: (i, kk)), pl.BlockSpec((tk, D_MODEL), lambda i, kk: (kk, 0)),
                  row, row, vec] + after_specs,
        out_specs=[row, vec],
        out_shape=[jax.ShapeDtypeStruct((SEQ, D_MODEL), F32), jax.ShapeDtypeStruct((1, D_MODEL), F32)],
        scratch_shapes=[pltpu.VMEM((tm, D_MODEL), F32)],
        compiler_params=_cparams(),
    )(dp, w_pad_t, x, dy, g, *after)


def _mla_prep_fwd(p, qg, kvg, wuq, wk, wv, rc, rs):
    tm = 512

    def body(lat_ref, qg_ref, kvg_ref, wuq_ref, wk_ref, wv_ref, c_ref, s_ref,
             cqn_ref, ckvn_ref, q_ref, k_ref, v_ref):
        c, s = c_ref[...], s_ref[...]
        lanes = _rope_lanes((tm, 128), MLA_ROPE_HALF, 128, 64)
        cq = lat_ref[:, 0:Q_RANK]
        r1 = lax.rsqrt(jnp.mean(cq * cq, axis=-1, keepdims=True) + EPS)
        cqn = (cq * r1 * qg_ref[...]).astype(BF16)
        cqn_ref[...] = cqn
        q = _nn(cqn, wuq_ref[...])
        for h in range(MLA_HEADS):
            sl = slice(h * 128, (h + 1) * 128)
            q_ref[:, sl] = (_rope_fwd(q[:, sl], c, s, MLA_ROPE_HALF, lanes) * MLA_SCALE).astype(BF16)
        ckv = lat_ref[:, Q_RANK:Q_RANK + KV_RANK]
        r2 = lax.rsqrt(jnp.mean(ckv * ckv, axis=-1, keepdims=True) + EPS)
        ckvn = (ckv * r2 * kvg_ref[...]).astype(BF16)
        ckvn_ref[...] = ckvn
        krr = _rope_fwd(lat_ref[:, Q_RANK + KV_RANK:N_LAT], c, s, MLA_ROPE_HALF, lanes)
        kn = _nn(ckvn, wk_ref[...])
        for h in range(MLA_HEADS):
            sl = slice(h * 128, (h + 1) * 128)
            k_ref[:, sl] = (kn[:, sl] + krr).astype(BF16)
        v_ref[...] = _nn(ckvn, wv_ref[...]).astype(BF16)

    def full(shape):
        return pl.BlockSpec(shape, lambda i: (0, 0))

    def rows(w):
        return pl.BlockSpec((tm, w), lambda i: (i, 0))

    return pl.pallas_call(
        body, name="mla_prep_fwd", grid=(SEQ // tm,),
        in_specs=[pl.BlockSpec((tm, N_LAT), lambda i: (i, COL_LAT // N_LAT)),
                  full((1, Q_RANK)), full((1, KV_RANK)), full((Q_RANK, 1024)), full((KV_RANK, 1024)),
                  full((KV_RANK, 512)), rows(128), rows(128)],
        out_specs=[rows(Q_RANK), rows(KV_RANK), rows(1024), rows(1024), rows(512)],
        out_shape=[jax.ShapeDtypeStruct((SEQ, Q_RANK), BF16), jax.ShapeDtypeStruct((SEQ, KV_RANK), BF16),
                   jax.ShapeDtypeStruct((SEQ, 1024), BF16), jax.ShapeDtypeStruct((SEQ, 1024), BF16),
                   jax.ShapeDtypeStruct((SEQ, 512), BF16)],
        compiler_params=_cparams(),
    )(p, qg, kvg, wuq, wk, wv, rc, rs)


def _mla_prep_bwd(dp_in, p, dq, dk, dv, qg, kvg, wuq, wk, wv, rc, rs):
    tm = 512

    def body(dp_any, lat_ref, dq_ref, dk_ref, dv_ref, qg_ref, kvg_ref, wuq_ref, wk_ref, wv_ref,
             c_ref, s_ref, dp_ref, dqb_ref, dkb_ref, dvb_ref, dgq_ref, dgkv_ref):
        del dp_any
        c, s = c_ref[...], s_ref[...]
        lanes = _rope_lanes((tm, 128), MLA_ROPE_HALF, 128, 64)
        lane = lax.broadcasted_iota(jnp.int32, (tm, 128), 1)
        dkr = jnp.zeros((tm, 128), F32)
        for h in range(MLA_HEADS):
            sl = slice(h * 128, (h + 1) * 128)
            dqb_ref[:, sl] = _rope_bwd(dq_ref[:, sl] * MLA_SCALE, c, s, MLA_ROPE_HALF, lanes).astype(BF16)
            dkh = dk_ref[:, sl]
            dkr = dkr + dkh
            dkb_ref[:, sl] = jnp.where(lane < 64, dkh, 0.0).astype(BF16)
        dkr = jnp.where((lane >= 64) & (lane < 96), dkr, 0.0)
        dkr = _rope_bwd(dkr, c, s, MLA_ROPE_HALF, lanes)
        dvb = dv_ref[...].astype(BF16)
        dvb_ref[...] = dvb

        cq = lat_ref[:, 0:Q_RANK]
        r1 = lax.rsqrt(jnp.mean(cq * cq, axis=-1, keepdims=True) + EPS)
        n1 = cq * r1
        dcqn = _nt(dqb_ref[...], wuq_ref[...])
        dn1 = dcqn * qg_ref[...]
        dcq = r1 * (dn1 - n1 * jnp.mean(dn1 * n1, axis=-1, keepdims=True))
        pq = jnp.sum(dcqn * n1, axis=0, keepdims=True)

        ckv = lat_ref[:, Q_RANK:Q_RANK + KV_RANK]
        r2 = lax.rsqrt(jnp.mean(ckv * ckv, axis=-1, keepdims=True) + EPS)
        n2 = ckv * r2
        dckvn = _nt(dkb_ref[...], wk_ref[...]) + _nt(dvb, wv_ref[...])
        dn2 = dckvn * kvg_ref[...]
        dckv = r2 * (dn2 - n2 * jnp.mean(dn2 * n2, axis=-1, keepdims=True))
        pkv = jnp.sum(dckvn * n2, axis=0, keepdims=True)

        dp_ref[:, 0:Q_RANK] = dcq.astype(BF16)
        dp_ref[:, Q_RANK:Q_RANK + KV_RANK] = dckv.astype(BF16)
        dp_ref[:, Q_RANK + KV_RANK:N_LAT] = dkr.astype(BF16)

        @pl.when(pl.program_id(0) == 0)
        def _():
            dgq_ref[...] = pq
            dgkv_ref[...] = pkv

        @pl.when(pl.program_id(0) > 0)
        def _():
            dgq_ref[...] += pq
            dgkv_ref[...] += pkv

    def full(shape):
        return pl.BlockSpec(shape, lambda i: (0, 0))

    def rows(w):
        return pl.BlockSpec((tm, w), lambda i: (i, 0))

    lat = pl.BlockSpec((tm, N_LAT), lambda i: (i, COL_LAT // N_LAT))
    return pl.pallas_call(
        body, name="mla_prep_bwd", grid=(SEQ // tm,),
        in_specs=[pl.BlockSpec(memory_space=pl.ANY), lat, rows(1024), rows(1024), rows(512),
                  full((1, Q_RANK)), full((1, KV_RANK)), full((Q_RANK, 1024)), full((KV_RANK, 1024)),
                  full((KV_RANK, 512)), rows(128), rows(128)],
        out_specs=[lat, rows(1024), rows(1024), rows(512), full((1, Q_RANK)), full((1, KV_RANK))],
        out_shape=[jax.ShapeDtypeStruct((SEQ, N_PAD), BF16), jax.ShapeDtypeStruct((SEQ, 1024), BF16),
                   jax.ShapeDtypeStruct((SEQ, 1024), BF16), jax.ShapeDtypeStruct((SEQ, 512), BF16),
                   jax.ShapeDtypeStruct((1, Q_RANK), F32), jax.ShapeDtypeStruct((1, KV_RANK), F32)],
        input_output_aliases={0: 0},
        compiler_params=_cparams(),
    )(dp_in, p, dq, dk, dv, qg, kvg, wuq, wk, wv, rc, rs)


FLASH_T = 1024


def _head_half(shape, hh):
    lane = lax.broadcasted_iota(jnp.int32, shape, 1)
    return (lane < 64) if hh == 0 else (lane >= 64)


def _diag_keep(nr, nk):
    row = lax.broadcasted_iota(jnp.int32, (nr, nk), 0)
    col = lax.broadcasted_iota(jnp.int32, (nr, nk), 1)
    return row + (nk - nr) >= col


def _tri_steps(nb, q_major):
    if q_major:
        pairs = [(i, kb) for i in range(nb) for kb in range(i + 1)]
    else:
        pairs = [(i, kb) for kb in range(nb) for i in range(kb, nb)]
    return jnp.asarray([p[0] for p in pairs], jnp.int32), jnp.asarray([p[1] for p in pairs], jnp.int32)


def _mla_flash_fwd(q, k, v):
    t = FLASH_T
    nb = SEQ // t
    qtab, ktab = _tri_steps(nb, True)

    def body(qi_ref, ki_ref, q_ref, k_ref, v_ref, o_ref, lse_ref, m_scr, l_scr, acc_scr):
        step = pl.program_id(1)
        i, kb = qi_ref[step], ki_ref[step]

        @pl.when(kb == 0)
        def _():
            m_scr[...] = jnp.full_like(m_scr, NEG)
            l_scr[...] = jnp.zeros_like(l_scr)
            acc_scr[...] = jnp.zeros_like(acc_scr)

        def update(r0, nr, nk, diagonal):
            rs = slice(r0, r0 + nr)
            vv = v_ref[0:nk, :]
            for hh in range(2):
                sl = slice(hh * 128, (hh + 1) * 128)
                s = _nt(q_ref[rs, sl], k_ref[0:nk, sl])
                if diagonal:
                    s = jnp.where(_diag_keep(nr, nk), s, NEG)
                m_prev = m_scr[hh, rs, :]
                m_new = jnp.maximum(m_prev, jnp.max(s, axis=-1, keepdims=True))
                pr = jnp.exp(s - jnp.tile(m_new, (1, nk // 128)))
                alpha = jnp.exp(m_prev - m_new)
                l_scr[hh, rs, :] = alpha * l_scr[hh, rs, :] + jnp.sum(pr, axis=-1, keepdims=True)
                acc_scr[hh, rs, :] = alpha * acc_scr[hh, rs, :] + _nn(pr.astype(BF16), vv)
                m_scr[hh, rs, :] = m_new

        @pl.when(kb < i)
        def _():
            update(0, t, t, False)

        @pl.when(kb == i)
        def _():
            update(0, t // 2, t // 2, True)
            update(t // 2, t // 2, t, True)
            o0 = acc_scr[0] / l_scr[0]
            o1 = acc_scr[1] / l_scr[1]
            o_ref[...] = jnp.where(_head_half((t, 128), 0), o0, o1)
            for hh in range(2):
                lse_ref[:, hh * 128:(hh + 1) * 128] = m_scr[hh] + jnp.log(l_scr[hh])

    grid_spec = pltpu.PrefetchScalarGridSpec(
        num_scalar_prefetch=2, grid=(4, qtab.shape[0]),
        in_specs=[pl.BlockSpec((t, 256), lambda j, s, qi, ki: (qi[s], j)),
                  pl.BlockSpec((t, 256), lambda j, s, qi, ki: (ki[s], j)),
                  pl.BlockSpec((t, 128), lambda j, s, qi, ki: (ki[s], j))],
        out_specs=[pl.BlockSpec((t, 128), lambda j, s, qi, ki: (qi[s], j)),
                   pl.BlockSpec((t, 256), lambda j, s, qi, ki: (qi[s], j))],
        scratch_shapes=[pltpu.VMEM((2, t, 128), F32), pltpu.VMEM((2, t, 128), F32), pltpu.VMEM((2, t, 128), F32)])
    return pl.pallas_call(
        body, name="mla_flash_fwd", grid_spec=grid_spec,
        out_shape=[jax.ShapeDtypeStruct((SEQ, 512), F32), jax.ShapeDtypeStruct((SEQ, 1024), F32)],
        compiler_params=_cparams(),
    )(qtab, ktab, q, k, v)


def _mla_flash_bwd(q, k, v, o, do, lse, token=None):
    t = FLASH_T
    nb = SEQ // t
    qtab, ktab = _tri_steps(nb, False)
    after, after_specs = _after(token)

    def body(qi_ref, ki_ref, q_ref, k_ref, v_ref, o_ref, do_ref, lse_ref, *rest):
        dq_ref, dk_ref, dv_ref, dk_scr, dv_scr = rest[-5:]
        step = pl.program_id(1)
        i, kb = qi_ref[step], ki_ref[step]

        @pl.when(step == 0)
        def _():
            dq_ref[...] = jnp.zeros_like(dq_ref)

        @pl.when(i == kb)
        def _():
            dk_scr[...] = jnp.zeros_like(dk_scr)
            dv_scr[...] = jnp.zeros_like(dv_scr)

        def update(r0, nr, nk, diagonal):
            rs = slice(r0, r0 + nr)
            vv = v_ref[0:nk, :]
            ov = o_ref[rs, :]
            dov = do_ref[rs, :]
            rows = pl.ds(pl.multiple_of(i * t + r0, t // 2), nr)
            for hh in range(2):
                sl = slice(hh * 128, (hh + 1) * 128)
                qh, kh = q_ref[rs, sl], k_ref[0:nk, sl]
                s = _nt(qh, kh)
                if diagonal:
                    s = jnp.where(_diag_keep(nr, nk), s, NEG)
                pr = jnp.exp(s - jnp.tile(lse_ref[rs, sl], (1, nk // 128)))
                dom = jnp.where(_head_half((nr, 128), hh), dov, 0.0)
                domb = dom.astype(BF16)
                dv_scr[0:nk, :] += _tn(pr.astype(BF16), domb)
                dpr = _nt(domb, vv)
                delta = jnp.sum(dom * ov, axis=-1, keepdims=True)
                ds = (pr * (dpr - delta)).astype(BF16)
                dq_ref[rows, sl] += _nn(ds, kh)
                dk_scr[hh, 0:nk, :] += _tn(ds, qh)

        @pl.when(i > kb)
        def _():
            update(0, t, t, False)

        @pl.when(i == kb)
        def _():
            update(0, t // 2, t // 2, True)
            update(t // 2, t // 2, t, True)

        @pl.when(i == nb - 1)
        def _():
            dk_ref[:, 0:128] = dk_scr[0]
            dk_ref[:, 128:256] = dk_scr[1]
            dv_ref[...] = dv_scr[...]

    qi_map = lambda j, s, qi, ki: (qi[s], j)
    ki_map = lambda j, s, qi, ki: (ki[s], j)
    grid_spec = pltpu.PrefetchScalarGridSpec(
        num_scalar_prefetch=2, grid=(4, qtab.shape[0]),
        in_specs=[pl.BlockSpec((t, 256), qi_map), pl.BlockSpec((t, 256), ki_map), pl.BlockSpec((t, 128), ki_map),
                  pl.BlockSpec((t, 128), qi_map), pl.BlockSpec((t, 128), qi_map), pl.BlockSpec((t, 256), qi_map)]
        + after_specs,
        out_specs=[pl.BlockSpec((SEQ, 256), lambda j, s, qi, ki: (0, j)), pl.BlockSpec((t, 256), ki_map),
                   pl.BlockSpec((t, 128), ki_map)],
        scratch_shapes=[pltpu.VMEM((2, t, 128), F32), pltpu.VMEM((t, 128), F32)])
    return pl.pallas_call(
        body, name="mla_flash_bwd", grid_spec=grid_spec,
        out_shape=[jax.ShapeDtypeStruct((SEQ, 1024), F32), jax.ShapeDtypeStruct((SEQ, 1024), F32),
                   jax.ShapeDtypeStruct((SEQ, 512), F32)],
        compiler_params=_cparams(),
    )(qtab, ktab, q, k, v, o, do, lse, *after)


def _strided(start, size, d):
    return pl.ds(start, size) if d == 1 else pl.ds(start, size, stride=d)


def _dil_prep_fwd(p, rc, rs, g):
    d = DIL_DILATIONS[g]
    sub_len = SEQ // d
    ch = min(sub_len, 512)

    def body(p_ref, c_ref, s_ref, o_ref, x_scr):
        tq = pl.program_id(0)
        lanes = _rope_lanes((ch, 128), DIL_ROPE_HALF, 64, 0)
        o_ref[0, 0:BAND, :] = jnp.zeros((BAND, 128), BF16)

        @pl.when(tq < 2)
        def _():
            mult = jnp.where(tq == 0, DIL_SCALE, 1.0).astype(F32)
            for c0 in range(0, SEQ, ch):
                rows = pl.ds(c0, ch)
                x_scr[rows, :] = _rope_fwd(p_ref[rows, :], c_ref[rows, :] * mult, s_ref[rows, :] * mult, DIL_ROPE_HALF, lanes)

        def gather(src):
            for r in range(d):
                for c0 in range(0, sub_len, ch):
                    at = BAND + r * sub_len + c0
                    o_ref[0, at:at + ch, :] = src[_strided(r + c0 * d, ch, d), :].astype(BF16)

        @pl.when(tq < 2)
        def _():
            gather(x_scr)

        @pl.when(tq == 2)
        def _():
            gather(p_ref)

    tab = pl.BlockSpec((SEQ, 128), lambda tq, pr: (0, 0))
    return pl.pallas_call(
        body, name=f"dil_prep_fwd_g{g}", grid=(3, 4),
        in_specs=[pl.BlockSpec((SEQ, 128), lambda tq, pr: (0, _qkv_block(tq, g, pr))), tab, tab],
        out_specs=pl.BlockSpec((1, BAND + SEQ, 128), lambda tq, pr: (tq, 0, pr)),
        out_shape=jax.ShapeDtypeStruct((3, BAND + SEQ, 512), BF16),
        scratch_shapes=[pltpu.VMEM((SEQ, 128), F32)],
        compiler_params=_cparams(),
    )(p, rc, rs)


DIL_ST_FWD, DIL_ST_BWD = 1024, 2048


def _band_keep(g, b, t, nb):
    nbs = SEQ // DIL_DILATIONS[g] // BAND
    row = lax.broadcasted_iota(jnp.int32, (BAND, 2 * BAND), 0)
    col = lax.broadcasted_iota(jnp.int32, (BAND, 2 * BAND), 1)
    cur = (col >= BAND) & (row >= col - BAND)
    prev = (col < BAND) & (col >= row)
    if nbs >= nb:
        if b > 0:
            return cur | prev
        return cur | (prev & ((t * nb) % nbs != 0))
    return cur | prev if b % nbs else cur


def _dil_tok(g, b, t, nb):
    d = DIL_DILATIONS[g]
    nbs = SEQ // d // BAND
    gb = t * nb + b
    return _strided((gb % nbs) * BAND * d + gb // nbs, BAND, d)


def _dil_attn_fwd(qkv, g):
    DIL_ST, DIL_NB = DIL_ST_FWD, DIL_ST_FWD // BAND

    def body(q_ref, k_ref, v_ref, o_ref, l_ref, s_scr, p_scr, o_scr):
        t = pl.program_id(1)
        base = t * DIL_ST
        half0 = _head_half((DIL_ST, 128), 0)
        lse_h = []
        for hh in range(2):
            half = _head_half((BAND, 128), hh)
            for b in range(DIL_NB):
                qv = q_ref[0, pl.ds(pl.multiple_of(base + (b + 1) * BAND, BAND), BAND), :]
                k2 = k_ref[0, pl.ds(pl.multiple_of(base + b * BAND, BAND), 2 * BAND), :]
                sb = _nt(jnp.where(half, qv, jnp.zeros_like(qv)), k2)
                s_scr[b * BAND:(b + 1) * BAND, :] = jnp.where(_band_keep(g, b, t, DIL_NB), sb, NEG)
            s = s_scr[...]
            m = jnp.max(s, axis=-1, keepdims=True)
            pr = jnp.exp(s - m)
            den = jnp.sum(pr, axis=-1, keepdims=True)
            p_scr[...] = pr.astype(BF16)
            for b in range(DIL_NB):
                v2 = v_ref[0, pl.ds(pl.multiple_of(base + b * BAND, BAND), 2 * BAND), :]
                o_scr[hh, b * BAND:(b + 1) * BAND, :] = _nn(p_scr[b * BAND:(b + 1) * BAND, :], v2)
            o_scr[hh] = o_scr[hh] / den
            lse_h.append(m + jnp.log(den))
        out = jnp.where(half0, o_scr[0], o_scr[1])
        lse = jnp.where(half0, lse_h[0], lse_h[1])
        for b in range(DIL_NB):
            tok = _dil_tok(g, b, t, DIL_NB)
            o_ref[tok, :] = out[b * BAND:(b + 1) * BAND, :]
            l_ref[tok, :] = lse[b * BAND:(b + 1) * BAND, :]

    def inp(tq):
        return pl.BlockSpec((1, BAND + SEQ, 128), lambda pr, t: (tq, 0, pr))

    out = pl.BlockSpec((SEQ, 128), lambda pr, t: (0, pr))
    return pl.pallas_call(
        body, name=f"dil_attn_fwd_g{g}", grid=(4, SEQ // DIL_ST),
        in_specs=[inp(0), inp(1), inp(2)], out_specs=[out, out],
        out_shape=[jax.ShapeDtypeStruct((SEQ, 512), F32), jax.ShapeDtypeStruct((SEQ, 512), F32)],
        scratch_shapes=[pltpu.VMEM((DIL_ST, 2 * BAND), F32), pltpu.VMEM((DIL_ST, 2 * BAND), BF16),
                        pltpu.VMEM((2, DIL_ST, 128), F32)],
        compiler_params=_cparams(),
    )(qkv, qkv, qkv)


def _dil_attn_bwd(dp_in, qkv, dyd, yd, lse_all, rc, rs, g, token=None):
    d = DIL_DILATIONS[g]
    sub_len = SEQ // d
    DIL_ST, DIL_NB = DIL_ST_BWD, DIL_ST_BWD // BAND
    nst = SEQ // DIL_ST
    after, after_specs = _after(token)
    ch = 512

    def body(dp_any, q_ref, k_ref, v_ref, do_ref, y_ref, l_ref, c_ref, sn_ref, *rest):
        dp_ref, tok_scr, dk_scr, dv_scr, s_scr, dp_scr, p_scr, ds_scr, do_scr, y_scr, l_scr, dq_scr = rest[-12:]
        del dp_any
        t = pl.program_id(1)
        base = t * DIL_ST

        @pl.when(t == 0)
        def _():
            dk_scr[...] = jnp.zeros_like(dk_scr)
            dv_scr[...] = jnp.zeros_like(dv_scr)

        for b in range(DIL_NB):
            tok = _dil_tok(g, b, t, DIL_NB)
            do_scr[b * BAND:(b + 1) * BAND, :] = do_ref[tok, :]
            y_scr[b * BAND:(b + 1) * BAND, :] = y_ref[tok, :]
            l_scr[b * BAND:(b + 1) * BAND, :] = l_ref[tok, :]
        for hh in range(2):
            half = _head_half((BAND, 128), hh)
            half_st = _head_half((DIL_ST, 128), hh)
            dom = jnp.where(half_st, do_scr[...], 0.0)
            delta = jnp.sum(dom * y_scr[...], axis=-1, keepdims=True)
            lcol = jnp.max(jnp.where(half_st, l_scr[...], NEG), axis=-1, keepdims=True)
            for b in range(DIL_NB):
                rows = slice(b * BAND, (b + 1) * BAND)
                qv = q_ref[0, pl.ds(pl.multiple_of(base + (b + 1) * BAND, BAND), BAND), :]
                band = pl.ds(pl.multiple_of(base + b * BAND, BAND), 2 * BAND)
                sb = _nt(jnp.where(half, qv, jnp.zeros_like(qv)), k_ref[0, band, :])
                s_scr[rows, :] = jnp.where(_band_keep(g, b, t, DIL_NB), sb, NEG)
                dp_scr[rows, :] = _nt(dom[rows, :].astype(BF16), v_ref[0, band, :])
            pr = jnp.exp(s_scr[...] - lcol)
            p_scr[...] = pr.astype(BF16)
            ds_scr[...] = (pr * (dp_scr[...] - delta)).astype(BF16)
            for b in range(DIL_NB):
                rows = slice(b * BAND, (b + 1) * BAND)
                qv = q_ref[0, pl.ds(pl.multiple_of(base + (b + 1) * BAND, BAND), BAND), :]
                band = pl.ds(pl.multiple_of(base + b * BAND, BAND), 2 * BAND)
                dqb = jnp.where(half, _nn(ds_scr[rows, :], k_ref[0, band, :]), 0.0)
                if hh == 0:
                    dq_scr[rows, :] = dqb
                else:
                    dq_scr[rows, :] += dqb
                half2 = _head_half((2 * BAND, 128), hh)
                dk_scr[band, :] += jnp.where(half2, _tn(ds_scr[rows, :], qv), 0.0)
                dv_scr[band, :] += _tn(p_scr[rows, :], dom[rows, :].astype(BF16))
        for b in range(DIL_NB):
            tok_scr[pl.ds(0, 1), _dil_tok(g, b, t, DIL_NB), :] = dq_scr[b * BAND:(b + 1) * BAND, :][None]

        @pl.when(t == nst - 1)
        def _():
            for r in range(d):
                rows = _strided(r, sub_len, d)
                tok_scr[pl.ds(1, 1), rows, :] = dk_scr[BAND + r * sub_len:BAND + (r + 1) * sub_len, :][None]
                tok_scr[pl.ds(2, 1), rows, :] = dv_scr[BAND + r * sub_len:BAND + (r + 1) * sub_len, :][None]
            lanes = _rope_lanes((ch, 128), DIL_ROPE_HALF, 64, 0)
            for c0 in range(0, SEQ, ch):
                rows = slice(c0, c0 + ch)
                cv, sv = c_ref[rows, :], sn_ref[rows, :]
                dp_ref[rows, 0:128] = _rope_bwd(tok_scr[0, rows, :], cv * DIL_SCALE, sv * DIL_SCALE, DIL_ROPE_HALF, lanes).astype(BF16)
                dp_ref[rows, 128:256] = _rope_bwd(tok_scr[1, rows, :], cv, sv, DIL_ROPE_HALF, lanes).astype(BF16)
                dp_ref[rows, 256:384] = tok_scr[2, rows, :].astype(BF16)

    def inp(tq):
        return pl.BlockSpec((1, BAND + SEQ, 128), lambda pr, t: (tq, 0, pr))

    tok_spec = pl.BlockSpec((SEQ, 128), lambda pr, t: (0, pr))
    tab = pl.BlockSpec((SEQ, 128), lambda pr, t: (0, 0))
    st = (DIL_ST, 2 * BAND)
    return pl.pallas_call(
        body, name=f"dil_attn_bwd_g{g}", grid=(4, nst),
        in_specs=[pl.BlockSpec(memory_space=pl.ANY), inp(0), inp(1), inp(2), tok_spec, tok_spec, tok_spec, tab, tab]
        + after_specs,
        out_specs=pl.BlockSpec((SEQ, 384), lambda pr, t: (0, _qkv_block(0, g, pr) // 3)),
        out_shape=jax.ShapeDtypeStruct((SEQ, N_PAD), BF16),
        input_output_aliases={0: 0},
        scratch_shapes=[pltpu.VMEM((3, SEQ, 128), F32),
                        pltpu.VMEM((BAND + SEQ, 128), F32), pltpu.VMEM((BAND + SEQ, 128), F32),
                        pltpu.VMEM(st, F32), pltpu.VMEM(st, F32), pltpu.VMEM(st, BF16), pltpu.VMEM(st, BF16),
                        pltpu.VMEM((DIL_ST, 128), F32), pltpu.VMEM((DIL_ST, 128), F32), pltpu.VMEM((DIL_ST, 128), F32),
                        pltpu.VMEM((DIL_ST, 128), F32)],
        compiler_params=_cparams(),
    )(dp_in, qkv, qkv, qkv, dyd, yd, lse_all, rc, rs, *after)


TAIL_T = 256


def _tail(p, ya, o_g, l_g, x, target, wpm, wpd, wout, post_g):
    tm = TAIL_T

    def body(pgz_ref, ya_ref, o0_ref, o1_ref, o2_ref, l0_ref, l1_ref, l2_ref, x_ref, t_ref,
             wpm_ref, wpd_ref, wout_ref, pg_ref,
             dp_ref, dy_ref, mg_ref, dt_ref, ua_ref, dpa_ref, ud_ref, dpd_ref, dya_ref, dyd_ref,
             yd_ref, lse_ref, loss_ref, dgp_ref):
        l0, l1, l2 = l0_ref[...], l1_ref[...], l2_ref[...]
        mx = jnp.maximum(jnp.maximum(l0, l1), l2)
        e0, e1, e2 = jnp.exp(l0 - mx), jnp.exp(l1 - mx), jnp.exp(l2 - mx)
        den = e0 + e1 + e2
        yd = (e0 * o0_ref[...] + e1 * o1_ref[...] + e2 * o2_ref[...]) / den
        yd_ref[...] = yd
        lse_ref[...] = mx + jnp.log(den)
        ya = ya_ref[...]

        gm, gd = pgz_ref[:, 0:1024], pgz_ref[:, 1024:2048]
        zm, zd = pgz_ref[:, 2048:2560], pgz_ref[:, 2560:3072]
        szm, szd = _sigmoid(zm), _sigmoid(zd)
        sm, sd = zm * szm, zd * szd
        ua = (ya * sm).astype(BF16)
        ud = (yd * sd).astype(BF16)
        ua_ref[...] = ua
        ud_ref[...] = ud
        pa = _nn(ua, wpm_ref[...])
        pd = _nn(ud, wpd_ref[...])
        sgm, sgd = _sigmoid(gm), _sigmoid(gd)
        mg = (sgm * pa + sgd * pd).astype(BF16)
        mg_ref[...] = mg
        t = _nn(mg, wout_ref[...])
        r3 = lax.rsqrt(jnp.mean(t * t, axis=-1, keepdims=True) + EPS)
        n = t * r3
        pg = pg_ref[...]
        err = x_ref[...] + n * pg - t_ref[...]
        lpart = jnp.sum(err * err, axis=0, keepdims=True)

        dy = err * (1.0 / D_MODEL)
        dy_ref[...] = dy
        gpart = jnp.sum(dy * n, axis=0, keepdims=True)
        dn = dy * pg
        dt = (r3 * (dn - n * jnp.mean(dn * n, axis=-1, keepdims=True))).astype(BF16)
        dt_ref[...] = dt
        dmg = _nt(dt, wout_ref[...])
        dpa = (dmg * sgm).astype(BF16)
        dpd = (dmg * sgd).astype(BF16)
        dpa_ref[...] = dpa
        dpd_ref[...] = dpd
        dp_ref[:, 0:1024] = (dmg * pa * sgm * (1.0 - sgm)).astype(BF16)
        dp_ref[:, 1024:2048] = (dmg * pd * sgd * (1.0 - sgd)).astype(BF16)
        dua = _nt(dpa, wpm_ref[...])
        dud = _nt(dpd, wpd_ref[...])
        dya_ref[...] = dua * sm
        dyd_ref[...] = dud * sd
        dp_ref[:, 2048:2560] = (dua * ya * szm * (1.0 + zm * (1.0 - szm))).astype(BF16)
        dp_ref[:, 2560:3072] = (dud * yd * szd * (1.0 + zd * (1.0 - szd))).astype(BF16)

        @pl.when(pl.program_id(0) == 0)
        def _():
            loss_ref[...] = lpart
            dgp_ref[...] = gpart

        @pl.when(pl.program_id(0) > 0)
        def _():
            loss_ref[...] += lpart
            dgp_ref[...] += gpart

    def rows(w):
        return pl.BlockSpec((tm, w), lambda i: (i, 0))

    def full(shape):
        return pl.BlockSpec(shape, lambda i: (0, 0))

    def sds(w, dt):
        return jax.ShapeDtypeStruct((SEQ, w), dt)

    return pl.pallas_call(
        body, name="tail", grid=(SEQ // tm,),
        in_specs=[rows(3072), rows(512), rows(512), rows(512), rows(512), rows(512), rows(512), rows(512),
                  rows(1024), rows(1024), full((512, 1024)), full((512, 1024)), full((1024, 1024)), full((1, 1024))],
        out_specs=[rows(3072), rows(1024), rows(1024), rows(1024), rows(512), rows(1024), rows(512), rows(1024),
                   rows(512), rows(512), rows(512), rows(512), full((1, 1024)), full((1, 1024))],
        out_shape=[sds(N_PAD, BF16), sds(1024, F32), sds(1024, BF16), sds(1024, BF16), sds(512, BF16),
                   sds(1024, BF16), sds(512, BF16), sds(1024, BF16), sds(512, F32), sds(512, F32),
                   sds(512, F32), sds(512, F32),
                   jax.ShapeDtypeStruct((1, 1024), F32), jax.ShapeDtypeStruct((1, 1024), F32)],
        compiler_params=_cparams(),
    )(p, ya, o_g[0], o_g[1], o_g[2], l_g[0], l_g[1], l_g[2], x, target, wpm, wpd, wout, post_g)


def _sum_parts(recv, own, me, tr, name):
    n, r, w = recv.shape
    if r % tr:
        return _sum_parts_cols(recv, own, me, name)
    own_spec = (pl.BlockSpec((tr, w), lambda i, me_ref: (i, 0)) if own.ndim == 2
                else pl.BlockSpec((None, tr, w), lambda i, me_ref: (me_ref[0], i, 0)))

    def body(me_ref, p_ref, own_ref, o_ref):
        mine = own_ref[...].astype(F32)
        acc = jnp.zeros((tr, w), F32)
        for s in range(n):
            acc = acc + jnp.where(me_ref[0] == s, mine, p_ref[s].astype(F32))
        o_ref[...] = acc

    return pl.pallas_call(
        body, name=name,
        grid_spec=pltpu.PrefetchScalarGridSpec(
            num_scalar_prefetch=1, grid=(r // tr,),
            in_specs=[pl.BlockSpec((n, tr, w), lambda i, me_ref: (0, i, 0)), own_spec],
            out_specs=pl.BlockSpec((tr, w), lambda i, me_ref: (i, 0))),
        out_shape=jax.ShapeDtypeStruct((r, w), F32),
    )(me.reshape(1), recv, own)


def _sum_parts_cols(recv, own, me, name):
    n, r, w = recv.shape
    tc = 128

    def body(me_ref, p_ref, own_ref, o_ref):
        mine = own_ref[...].astype(F32)
        acc = jnp.zeros((r, tc), F32)
        for s in range(n):
            acc = acc + jnp.where(me_ref[0] == s, mine, p_ref[s].astype(F32))
        o_ref[...] = acc

    return pl.pallas_call(
        body, name=name,
        grid_spec=pltpu.PrefetchScalarGridSpec(
            num_scalar_prefetch=1, grid=(w // tc,),
            in_specs=[pl.BlockSpec((n, r, tc), lambda i, me_ref: (0, 0, i)),
                      pl.BlockSpec((None, r, tc), lambda i, me_ref: (me_ref[0], 0, i))],
            out_specs=pl.BlockSpec((r, tc), lambda i, me_ref: (0, i))),
        out_shape=jax.ShapeDtypeStruct((r, w), F32),
    )(me.reshape(1), recv, own)


def _adamw(w, g, m, v, name):
    lead = w.shape[:-2]
    r, c = w.shape[-2:]
    tr = max([t for t in range(8, 257, 8) if r % t == 0], default=r)
    c1 = 1.0 - ADAM_B1 ** ADAM_STEP
    c2 = 1.0 - ADAM_B2 ** ADAM_STEP

    def body(w_ref, g_ref, m_ref, v_ref, d_ref, nm_ref, nv_ref):
        gv = g_ref[...]
        nm = ADAM_B1 * m_ref[...] + (1.0 - ADAM_B1) * gv
        nv = ADAM_B2 * v_ref[...] + (1.0 - ADAM_B2) * (gv * gv)
        nm_ref[...] = nm
        nv_ref[...] = nv
        d_ref[...] = -ADAM_LR * ((nm / c1) / (jnp.sqrt(nv / c2) + ADAM_EPS) + ADAM_WD * w_ref[...])

    zeros = (0,) * len(lead)
    spec = pl.BlockSpec((1,) * len(lead) + (tr, c), lambda i: zeros + (i, 0))
    sd = jax.ShapeDtypeStruct(w.shape, F32)
    return pl.pallas_call(
        body, name=name, grid=(r // tr,),
        in_specs=[spec] * 4, out_specs=[spec] * 3, out_shape=[sd] * 3,
    )(w, g, m, v)


def _adamw_in(w_t, m_t, v_t, own_half, swapped, core):
    r, c = SHARD_SHAPES[0]
    tr = max(t for t in range(8, 257, 8) if r % t == 0)
    c1 = 1.0 - ADAM_B1 ** ADAM_STEP
    c2 = 1.0 - ADAM_B2 ** ADAM_STEP

    def body(core_ref, w_ref, m_ref, v_ref, own_ref, sw_ref, d_ref, nm_ref, nv_ref, g_ref):
        own = own_ref[...]
        col_half = lax.broadcasted_iota(jnp.int32, (tr, c), 1) // (c // 2)
        gv = jnp.where(col_half == core_ref[0], jnp.concatenate([own, own], axis=1), sw_ref[...])
        g_ref[0] = gv
        nm = ADAM_B1 * m_ref[0] + (1.0 - ADAM_B1) * gv
        nv = ADAM_B2 * v_ref[0] + (1.0 - ADAM_B2) * (gv * gv)
        nm_ref[0] = nm
        nv_ref[0] = nv
        d_ref[0] = -ADAM_LR * ((nm / c1) / (jnp.sqrt(nv / c2) + ADAM_EPS) + ADAM_WD * w_ref[0])

    full = pl.BlockSpec((1, tr, c), lambda i, core_ref: (0, i, 0))
    sd = jax.ShapeDtypeStruct((1, r, c), F32)
    return pl.pallas_call(
        body, name="adamw_in",
        grid_spec=pltpu.PrefetchScalarGridSpec(
            num_scalar_prefetch=1, grid=(r // tr,),
            in_specs=[full, full, full, pl.BlockSpec((tr, c // 2), lambda i, core_ref: (i, 0)),
                      pl.BlockSpec((tr, c), lambda i, core_ref: (i, 0))],
            out_specs=[full] * 4),
        out_shape=[sd] * 4,
    )(core.reshape(1), w_t, m_t, v_t, own_half, swapped)


ANY = pl.BlockSpec(memory_space=pl.ANY)


def _my_place():
    return lax.axis_index("x"), lax.axis_index("y"), lax.axis_index("c")


HBM = pl.BlockSpec(memory_space=pltpu.HBM)
SEM = pl.BlockSpec(memory_space=pltpu.SEMAPHORE)
DATAFLOW = pltpu.SideEffectType.DATAFLOW_SIDE_EFFECTING


def _near_chips(x, y):
    return [(1 - x, y), (x, 1 - y)]


def _half(mi, hc):
    r, c = SHARD_SHAPES[mi]
    if mi == 0:
        return pl.ds(0, r), pl.ds(pl.multiple_of(hc * (c // 2), 128), c // 2)
    return pl.ds(pl.multiple_of(hc * (r // 2), 16), r // 2), pl.ds(0, c)


def _gather_copies(m_refs, land_refs, send_sems, recv_sems):
    x, y, c = _my_place()
    out, back = [], []
    for mi in range(N_MATS):
        rows, cols = _half(mi, c)
        for j, (cx, cy) in enumerate(_near_chips(x, y)):
            sems = dict(send_sem=send_sems.at[mi * 2 + j], recv_sem=recv_sems.at[mi * 2 + j],
                        device_id=(cx, cy, c), device_id_type=MESH)
            out.append(pltpu.make_async_remote_copy(src_ref=m_refs[mi].at[rows, cols],
                                                    dst_ref=land_refs[mi].at[2 * x + y, rows, cols], **sems))
            got = land_refs[mi].at[2 * cx + cy, rows, cols]
            back.append(pltpu.make_async_remote_copy(src_ref=got, dst_ref=got, **sems))
    return out, back


def _gather_start(mats, landing):
    n = N_MATS

    def body(*refs):
        out, _ = _gather_copies(refs[:n], refs[n:2 * n], refs[2 * n], refs[2 * n + 1])
        for cp in out:
            cp.start()
        refs[-1][...] = jnp.zeros_like(refs[-1])

    hbm = [pltpu.HBM(a.shape, a.dtype) for a in list(mats) + list(landing)]
    outs = pl.pallas_call(
        body, name="gather_start",
        out_shape=(pltpu.SemaphoreType.DMA((2 * n,)), pltpu.SemaphoreType.DMA((2 * n,)), *hbm,
                   jax.ShapeDtypeStruct((8, 128), F32)),
        in_specs=[HBM] * (2 * n), out_specs=(SEM, SEM, *[HBM] * (2 * n), pl.BlockSpec(memory_space=pltpu.VMEM)),
        input_output_aliases={i: 2 + i for i in range(2 * n)},
        compiler_params=pltpu.CompilerParams(has_side_effects=DATAFLOW),
    )(*[pltpu.with_memory_space_constraint(a, pltpu.HBM) for a in list(mats) + list(landing)])
    return outs[:-1], outs[-1]


def _gather_wait(handle, after):
    n = N_MATS

    def body(*refs):
        out, back = _gather_copies(refs[:n], refs[n:2 * n], refs[2 * n], refs[2 * n + 1])
        for cp, arrival in zip(out, back):
            cp.wait_send()
            arrival.wait_recv()

    bufs = handle[2:]
    after, after_specs = _after(after)
    res = pl.pallas_call(
        body, name="gather_wait", out_shape=tuple(pltpu.HBM(b.shape, b.dtype) for b in bufs),
        in_specs=[HBM] * (2 * n) + [SEM, SEM] + after_specs, out_specs=tuple([HBM] * (2 * n)),
        input_output_aliases={i: i for i in range(2 * n)},
        compiler_params=pltpu.CompilerParams(has_side_effects=DATAFLOW),
    )(*bufs, handle[0], handle[1], *after)
    return list(res[n:])


def _relay_share(gathered):
    n = N_MATS

    def body(*refs):
        out_refs = refs[n:2 * n]
        send_sems, recv_sems = refs[2 * n:]
        x, y, c = _my_place()
        sibling = (x, y, 1 - c)
        relayed = 2 * (x ^ (1 - c)) + (y ^ c)
        relay_to = (x ^ c, y ^ (1 - c), c)
        far = 2 * (1 - x) + (1 - y)
        near = [2 * (1 - x) + y, 2 * x + (1 - y)]

        def copy(k, mi, shard, hc, to):
            blk = out_refs[mi].at[(shard,) + _half(mi, hc)]
            return pltpu.make_async_remote_copy(src_ref=blk, dst_ref=blk, send_sem=send_sems.at[mi * 4 + k],
                                                recv_sem=recv_sems.at[mi * 4 + k], device_id=to, device_id_type=MESH)

        sends = []
        for mi in range(n):
            sends.append(copy(0, mi, relayed, c, relay_to))
            sends += [copy(1 + j, mi, near[j], c, sibling) for j in range(2)]
        for cp in sends:
            cp.start()
        for mi in range(n):
            copy(0, mi, far, c, relay_to).wait_recv()
            cp = copy(3, mi, far, c, sibling)
            cp.start()
            sends.append(cp)
        for mi in range(n):
            for j in range(2):
                copy(1 + j, mi, near[j], 1 - c, sibling).wait_recv()
            copy(3, mi, far, 1 - c, sibling).wait_recv()
        for cp in sends:
            cp.wait_send()

    return pl.pallas_call(
        body, name="relay_share",
        in_specs=[ANY] * n, out_specs=[ANY] * n,
        out_shape=[jax.ShapeDtypeStruct(g.shape, g.dtype) for g in gathered],
        input_output_aliases={i: i for i in range(n)},
        scratch_shapes=[pltpu.SemaphoreType.DMA((4 * n,)), pltpu.SemaphoreType.DMA((4 * n,))],
    )(*gathered)


def _peers(x, y, c):
    out = []
    for k in range(1, 8):
        px, py, pc = x ^ (k >> 2), y ^ ((k >> 1) & 1), c ^ (k & 1)
        out.append((k - 1, (px, py, pc), 4 * px + 2 * py + pc))
    return out


def _exchange_start(parts, name):
    n = len(parts)

    def body(*refs):
        p_refs, land_refs = refs[:n], refs[n:2 * n]
        send_sems, recv_sems, token = refs[2 * n], refs[2 * n + 1], refs[-1]
        x, y, c = _my_place()
        me = 4 * x + 2 * y + c
        for k, dev, peer in _peers(x, y, c):
            for mi in range(n):
                pltpu.make_async_remote_copy(
                    src_ref=p_refs[mi].at[peer], dst_ref=land_refs[mi].at[me], send_sem=send_sems.at[k * n + mi],
                    recv_sem=recv_sems.at[k * n + mi], device_id=dev, device_id_type=MESH).start()
        token[...] = jnp.zeros_like(token)

    hbm = [pltpu.HBM(p.shape, p.dtype) for p in parts]
    outs = pl.pallas_call(
        body, name=name + "_start",
        out_shape=(pltpu.SemaphoreType.DMA((7 * n,)), pltpu.SemaphoreType.DMA((7 * n,)), *hbm, *hbm,
                   jax.ShapeDtypeStruct((8, 128), F32)),
        in_specs=[HBM] * (2 * n), out_specs=(SEM, SEM, *[HBM] * (2 * n), pl.BlockSpec(memory_space=pltpu.VMEM)),
        input_output_aliases={i: 2 + i for i in range(2 * n)},
        compiler_params=pltpu.CompilerParams(has_side_effects=DATAFLOW),
    )(*[pltpu.with_memory_space_constraint(p, pltpu.HBM) for p in parts],
      *[pltpu.with_memory_space_constraint(lax.empty(p.shape, p.dtype), pltpu.HBM) for p in parts])
    return (name, outs[:-1]), outs[-1]


def _exchange_wait(handle, after):
    name, outs = handle
    n = (len(outs) - 2) // 2

    def body(*refs):
        p_refs, land_refs = refs[:n], refs[n:2 * n]
        send_sems, recv_sems = refs[2 * n], refs[2 * n + 1]
        x, y, c = _my_place()
        me = 4 * x + 2 * y + c
        for k, dev, peer in _peers(x, y, c):
            for mi in range(n):
                pltpu.make_async_remote_copy(
                    src_ref=p_refs[mi].at[peer], dst_ref=land_refs[mi].at[me], send_sem=send_sems.at[k * n + mi],
                    recv_sem=recv_sems.at[k * n + mi], device_id=dev, device_id_type=MESH).wait_send()
                slot = land_refs[mi].at[peer]
                pltpu.make_async_remote_copy(
                    src_ref=slot, dst_ref=slot, send_sem=send_sems.at[k * n + mi],
                    recv_sem=recv_sems.at[k * n + mi], device_id=dev, device_id_type=MESH).wait_recv()

    bufs = outs[2:]
    res = pl.pallas_call(
        body, name=name + "_wait", out_shape=tuple(pltpu.HBM(b.shape, b.dtype) for b in bufs),
        in_specs=[HBM] * (2 * n) + [SEM, SEM, ANY], out_specs=tuple([HBM] * (2 * n)),
        input_output_aliases={i: i for i in range(2 * n)},
        compiler_params=pltpu.CompilerParams(has_side_effects=DATAFLOW),
    )(*bufs, outs[0], outs[1], after)
    return list(res[n:])


def _swap_halves(halves, gvec):
    def place(ref, mi, hc):
        return ref.at[:, pl.ds(pl.multiple_of(hc * 512, 128), 512)] if mi == 0 else ref.at[hc]

    def body(*refs):
        g_refs, gv_ref = refs[:N_MATS], refs[N_MATS]
        out_refs, rg_ref = refs[N_MATS + 1:2 * N_MATS + 1], refs[2 * N_MATS + 1]
        send_sems, recv_sems = refs[2 * N_MATS + 2:]
        x, y, c = _my_place()
        me = 4 * x + 2 * y + c
        sends = []
        for mi in range(N_MATS):
            cp = pltpu.make_async_remote_copy(src_ref=g_refs[mi], dst_ref=place(out_refs[mi], mi, c), send_sem=send_sems.at[mi],
                                              recv_sem=recv_sems.at[mi], device_id=(x, y, 1 - c), device_id_type=MESH)
            cp.start()
            sends.append(cp)
        for k, dev, peer in _peers(x, y, c):
            cp = pltpu.make_async_remote_copy(src_ref=gv_ref, dst_ref=rg_ref.at[me], send_sem=send_sems.at[N_MATS + k],
                                              recv_sem=recv_sems.at[N_MATS + k], device_id=dev, device_id_type=MESH)
            cp.start()
            sends.append(cp)
        for mi in range(N_MATS):
            got = place(out_refs[mi], mi, 1 - c)
            pltpu.make_async_remote_copy(src_ref=got, dst_ref=got, send_sem=send_sems.at[mi], recv_sem=recv_sems.at[mi],
                                         device_id=(x, y, 1 - c), device_id_type=MESH).wait_recv()
        for k, dev, peer in _peers(x, y, c):
            got = rg_ref.at[peer]
            pltpu.make_async_remote_copy(src_ref=got, dst_ref=got, send_sem=send_sems.at[N_MATS + k],
                                         recv_sem=recv_sems.at[N_MATS + k], device_id=dev, device_id_type=MESH).wait_recv()
        for cp in sends:
            cp.wait_send()

    outs = pl.pallas_call(
        body, name="swap_halves",
        in_specs=[ANY] * (N_MATS + 1), out_specs=[ANY] * (N_MATS + 1),
        out_shape=[jax.ShapeDtypeStruct(SHARD_SHAPES[0], F32)]
        + [jax.ShapeDtypeStruct((2, r // 2, c), F32) for r, c in SHARD_SHAPES[1:]]
        + [jax.ShapeDtypeStruct((8, 8, N_GVEC), F32)],
        scratch_shapes=[pltpu.SemaphoreType.DMA((N_MATS + 7,)), pltpu.SemaphoreType.DMA((N_MATS + 7,))],
    )(*halves, gvec)
    return outs[:N_MATS], outs[N_MATS]


def _set_slot(arr, block, idx):
    return lax.dynamic_update_slice(arr, block[None], (idx,) + (0,) * block.ndim)


PAD_RUNS = ((6304, 8352, 0), (5280, 6304, COL_Z), (672, 5280, COL_QKV), (0, 640, COL_LAT), (640, 672, COL_LAT + 704))
N_QKV = COL_LAT - COL_QKV


def _qkv_rows_regroup(a, to_padded):
    if to_padded:
        a4 = a.reshape(3, 12, 128, a.shape[1])
        return jnp.stack([a4[0], a4[1], a4[2]], axis=1).reshape(a.shape)
    a4 = a.reshape(12, 3, 128, a.shape[1])
    return jnp.concatenate([a4[:, tq].reshape(N_QKV // 3, a.shape[1]) for tq in range(3)], axis=0)
W_IN_SHARD = 2088


def _full_weights(gathered):
    def cols(a):
        return jnp.concatenate([a[s] for s in range(4)], axis=1)

    w_uq, w_ukv, w_pm, w_pd = [cols(a) for a in gathered[1:5]]
    w_out = gathered[5].reshape(D_MODEL, D_MODEL)
    w_in_t = gathered[0].reshape(4 * W_IN_SHARD, D_MODEL)
    pieces, at = [], 0
    for lo, hi, pad_lo in sorted(PAD_RUNS, key=lambda t: t[2]):
        if pad_lo > at:
            pieces.append(jnp.zeros((pad_lo - at, D_MODEL), w_in_t.dtype))
        pieces.append(_qkv_rows_regroup(w_in_t[lo:hi], True) if pad_lo == COL_QKV else w_in_t[lo:hi])
        at = pad_lo + hi - lo
    pieces.append(jnp.zeros((N_PAD - at, D_MODEL), w_in_t.dtype))
    w_pad_t = jnp.concatenate(pieces, axis=0)
    z32 = jnp.zeros((Q_RANK, 32), w_uq.dtype)
    wuq_pad = jnp.concatenate([t for h in range(MLA_HEADS) for t in (w_uq[:, h * 96:(h + 1) * 96], z32)], axis=1)
    z64 = jnp.zeros((KV_RANK, 64), w_ukv.dtype)
    wk_pad = jnp.concatenate([t for h in range(MLA_HEADS) for t in (w_ukv[:, h * 128:h * 128 + 64], z64)], axis=1)
    wv = jnp.concatenate([w_ukv[:, h * 128 + 64:(h + 1) * 128] for h in range(MLA_HEADS)], axis=1)
    return w_pad_t.T, w_pad_t, wuq_pad, wk_pad, wv, w_pm, w_pd, w_out


W_IN_LAT = 672


def _grad_parts_in_early(dwt_early):
    dwt_early = jnp.concatenate([dwt_early[:COL_QKV], _qkv_rows_regroup(dwt_early[COL_QKV:COL_LAT], False)], axis=0)

    def in_block(s, h):
        cols = slice(h * 512, (h + 1) * 512)
        out = []
        for lo, hi, pad_lo in sorted(PAD_RUNS):
            a_, b_ = max(lo, s * W_IN_SHARD), min(hi, (s + 1) * W_IN_SHARD)
            if a_ < b_:
                out.append(jnp.zeros((b_ - a_, 512), dwt_early.dtype) if pad_lo >= COL_LAT
                           else dwt_early[pad_lo + a_ - lo:pad_lo + b_ - lo, cols])
        return jnp.concatenate(out, axis=0)

    return jnp.stack([in_block(s, h) for s in range(4) for h in range(2)])


def _grad_parts_in_late(dwt_late):
    rows = jnp.concatenate([dwt_late[0:640], dwt_late[704:736]], axis=0)
    zero = jnp.zeros((W_IN_LAT, 512), dwt_late.dtype)
    return jnp.stack([rows[:, 0:512], rows[:, 512:1024]] + [zero] * 6)


def _col_blocks(m):
    r, c = m.shape[0] // 2, m.shape[1] // 4
    return jnp.stack([m[h * r:(h + 1) * r, s * c:(s + 1) * c] for s in range(4) for h in range(2)])


def _grad_parts_mla(dwuq_pad, dwk_pad, dwv):
    d_uq = jnp.concatenate([dwuq_pad[:, h * 128:h * 128 + 96] for h in range(MLA_HEADS)], axis=1)
    d_ukv = jnp.concatenate([t for h in range(MLA_HEADS) for t in (dwk_pad[:, h * 128:h * 128 + 64], dwv[:, h * 64:(h + 1) * 64])],
                            axis=1)
    return [_col_blocks(d_uq), _col_blocks(d_ukv)]


def _rope_tables(positions, token=None):
    pos = positions.reshape(SEQ).astype(F32)
    if token is not None:
        pos = pos + token[0, 0]
    lane = jnp.arange(128)

    def table(rot, first, period):
        inv = ROPE_THETA ** (-jnp.arange(0, rot, 2, dtype=F32) / rot)
        half = rot // 2
        off = lane % period - first
        in1, in2 = (off >= 0) & (off < half), (off >= half) & (off < rot)
        inv_lane = jnp.where(in1 | in2, inv[jnp.clip(off % half, 0, half - 1)], 0.0)
        sign = jnp.where(in1, -1.0, 1.0).astype(F32)
        ang = pos[:, None] * inv_lane[None, :]
        return jnp.cos(ang), jnp.sin(ang) * sign[None, :]

    return table(32, 64, 128), table(16, 0, 64)


class _Links:
    def __init__(self, mats, chip, me):
        landing = [_set_slot(lax.empty((4,) + m.shape, m.dtype), m, chip) for m in mats]
        self.gather, self.token = _gather_start(mats, landing)
        self.me, self.sent, self.handles, self.sums = me, {}, {}, {}

    def weights(self, after):
        return _relay_share(_gather_wait(self.gather, after))

    def send(self, blocks, name):
        self.sent[name] = blocks
        self.handles[name], token = _exchange_start(blocks, name)
        return token

    def collect(self, name, after, parts):
        recv = _exchange_wait(self.handles[name], after)
        for r, own, part in zip(recv, self.sent[name], parts):
            self.sums[part] = _sum_parts(r, own, self.me, 64, "sum_grad_" + part)
        return tuple(self.sums[part] for part in parts)


def _device_grads(x, positions, target, gains, links):
    pre_g, q_g, kv_g, post_g = gains
    (mc, ms), (dc, ds) = _rope_tables(positions, links.token)
    h = _prenorm_fwd(x, pre_g, links.token)
    w_pad, w_pad_t, wuq_pad, wk_pad, wv, w_pm, w_pd, w_out = _full_weights(links.weights((h, mc, ms, dc, ds)))

    p = _matmul(h, w_pad, "nn", F32, 1024, 1408, 1024, "in_proj")
    cqn, ckvn, q, k, v = _mla_prep_fwd(p, q_g, kv_g, wuq_pad, wk_pad, wv, mc, ms)
    ya, lse_m = _mla_flash_fwd(q, k, v)
    qkv = [_dil_prep_fwd(p, dc, ds, g) for g in range(3)]
    o_g, l_g = zip(*[_dil_attn_fwd(qkv[g], g) for g in range(3)])
    (dp, dy, mg, dt, ua, dpa, ud, dpd, dya, dyd, yd, lse_d, loss_cols, dg_post) = _tail(
        p, ya, o_g, l_g, x, target, w_pm, w_pd, w_out, post_g)

    for g in range(3):
        dp = _dil_attn_bwd(dp, qkv[g], dyd, yd, lse_d, dc, ds, g)
    dw_early = _matmul(dp, h, "tn", BF16, 1536, 1024, 2048, "dw_in_early", a_cols=(0, COL_LAT // 1536))
    dwpm = _matmul(ua, dpa, "tn", BF16, 512, 1024, 2048, "dw_proj_mla")
    dwpd = _matmul(ud, dpd, "tn", BF16, 512, 1024, 2048, "dw_proj_dil")
    dwout = _matmul(mg, dt, "tn", BF16, 1024, 1024, 2048, "dw_out")
    token = links.send([_grad_parts_in_early(dw_early), _col_blocks(dwpm), _col_blocks(dwpd),
                        dwout.reshape(8, 128, D_MODEL)], "exchange_early")

    dq, dk, dv = _mla_flash_bwd(q, k, v, ya, dya, lse_m, token)
    dp, dqb, dkb, dvb, dg_q, dg_kv = _mla_prep_bwd(dp, p, dq, dk, dv, q_g, kv_g, wuq_pad, wk_pad, wv, mc, ms)
    dwuq_pad = _matmul(cqn, dqb, "tn", BF16, Q_RANK, 1024, 2048, "dw_uq")
    dwk_pad = _matmul(ckvn, dkb, "tn", BF16, KV_RANK, 1024, 2048, "dw_k")
    dwv = _matmul(ckvn, dvb, "tn", BF16, KV_RANK, 512, 2048, "dw_v")
    dw_late = _matmul(dp, h, "tn", BF16, N_LAT, 1024, 2048, "dw_in_late", a_cols=(COL_LAT // N_LAT, 1))
    token = links.send([_grad_parts_in_late(dw_late)] + _grad_parts_mla(dwuq_pad, dwk_pad, dwv), "exchange_late")
    early = links.collect("exchange_early", dw_late, ("in_early", "pm", "pd", "out"))

    grad_x, dg_pre = _dh_prenorm_bwd(dp, w_pad_t, x, dy, pre_g, (token,) + tuple(early))
    links.collect("exchange_late", grad_x, ("in_late", "uq", "ukv"))

    loss_part = jnp.pad((jnp.sum(loss_cols) * (0.5 / D_MODEL)).reshape(1, 1), ((0, 0), (0, N_GVEC - N_GAINS - 1)))
    gvec = jnp.concatenate([dg_pre, dg_q, dg_kv, dg_post, loss_part], axis=1)
    return grad_x, gvec


def kernel(x, positions, pre_norm_g, w_in, q_norm_g, w_uq, kv_norm_g, w_ukv, w_proj_mla, w_proj_dil, w_out, post_norm_g, loss_target, m_pre_norm_g, m_w_in, m_q_norm_g, m_w_uq, m_kv_norm_g, m_w_ukv, m_w_proj_mla, m_w_proj_dil, m_w_out, m_post_norm_g, v_pre_norm_g, v_w_in, v_q_norm_g, v_w_uq, v_kv_norm_g, v_w_ukv, v_w_proj_mla, v_w_proj_dil, v_w_out, v_post_norm_g):
    xi, yi, ci = _my_place()
    chip, me = 2 * xi + yi, 4 * xi + 2 * yi + ci
    mats = [jnp.swapaxes(w_in, 1, 2)] + [w_uq, w_ukv, w_proj_mla, w_proj_dil, w_out]
    mats = [w.reshape(w.shape[1:]).astype(BF16) for w in mats]
    links = _Links(mats, chip, me)
    gains = (pre_norm_g, q_norm_g, kv_norm_g, post_norm_g)
    grad_x, gvec = _device_grads(x[0], positions, loss_target[0], gains, links)

    sums = links.sums
    in_e = sums["in_early"]
    half_in = jnp.concatenate([in_e[:W_IN_LAT] + jnp.where(chip == 0, sums["in_late"], 0.0), in_e[W_IN_LAT:]], axis=0)
    halves = [half_in, sums["uq"], sums["ukv"], sums["pm"], sums["pd"], sums["out"]]
    gvec8 = jnp.pad(gvec, ((0, 7), (0, 0)))
    swapped, recv_gains = _swap_halves(halves, gvec8)
    g_gains = _sum_parts(recv_gains, gvec8, me, 8, "sum_gain_parts")[0:1]
    loss = g_gains[0, N_GAINS]
    sw = lambda a: jnp.swapaxes(a, 1, 2)
    d_in, m_in, v_in, g_in = [sw(o) for o in _adamw_in(sw(w_in), sw(m_w_in), sw(v_w_in), half_in, swapped[0], ci)]
    g_mats = [g_in] + [_set_slot(s, hf, ci).reshape((1,) + shp)
                       for s, hf, shp in zip(swapped[1:], halves[1:], SHARD_SHAPES[1:])]

    off = [0, 1024, 1408, 1664, 2688]
    g_gain = [g_gains[:, off[i]:off[i + 1]] for i in range(4)]
    grads = [g_gain[0], g_mats[0], g_gain[1], g_mats[1], g_gain[2], g_mats[2], g_mats[3], g_mats[4], g_mats[5], g_gain[3]]
    ws = [pre_norm_g, w_in, q_norm_g, w_uq, kv_norm_g, w_ukv, w_proj_mla, w_proj_dil, w_out, post_norm_g]
    ms = [m_pre_norm_g, m_w_in, m_q_norm_g, m_w_uq, m_kv_norm_g, m_w_ukv, m_w_proj_mla, m_w_proj_dil, m_w_out, m_post_norm_g]
    vs = [v_pre_norm_g, v_w_in, v_q_norm_g, v_w_uq, v_kv_norm_g, v_w_ukv, v_w_proj_mla, v_w_proj_dil, v_w_out, v_post_norm_g]
    deltas, new_m, new_v = [], [], []
    for i, (w, g, m, v) in enumerate(zip(ws, grads, ms, vs)):
        if w is w_in:
            d_, m_, v_ = d_in, m_in, v_in
        elif w.shape[-1] % 128 and w.shape[-2] % 128 == 0:
            g = jnp.swapaxes(g, 1, 2)
            grads[i] = jnp.swapaxes(g, 1, 2)
            d_, m_, v_ = [jnp.swapaxes(o, 1, 2) for o in
                          _adamw(jnp.swapaxes(w, 1, 2), g, jnp.swapaxes(m, 1, 2), jnp.swapaxes(v, 1, 2), f"adamw_{i}")]
        else:
            d_, m_, v_ = _adamw(w, g, m, v, f"adamw_{i}")
        deltas.append(d_)
        new_m.append(m_)
        new_v.append(v_)
    return (loss, grad_x.reshape(x.shape), *grads, *deltas, *new_m, *new_v)
```

```python
import jax
import jax.numpy as jnp
from jax import lax
from jax.experimental import pallas as pl
from jax.experimental.pallas import tpu as pltpu

F32 = jnp.float32
BF16 = jnp.bfloat16

SEQ = 4096
D_MODEL = 1024
EPS = 1e-6
ROPE_THETA = 500000.0
MLA_HEADS = 8
Q_RANK = 384
KV_RANK = 256
MLA_SCALE = 96.0 ** -0.5
MLA_ROPE_HALF = 16
DIL_DILATIONS = (1, 4, 16)
DIL_ROPE_HALF = 8
DIL_SCALE = 0.125
BAND = 128

N_LAT = 768
COL_Z, COL_QKV, COL_LAT = 2048, 3072, 7680
N_PAD = 8448


def _qkv_block(tq, g, pr):
    return COL_QKV // 128 + (g * 4 + pr) * 3 + tq

IN_SPLITS = (384, 256, 32, 4608, 512, 512, 1024, 1024)

SHARD_SHAPES = ((2088, 1024), (384, 192), (256, 256), (512, 256), (512, 256), (256, 1024))
N_MATS = len(SHARD_SHAPES)
N_GAINS = 2688
N_GVEC = N_GAINS + 128

ADAM_LR, ADAM_B1, ADAM_B2, ADAM_EPS, ADAM_WD, ADAM_STEP = 0.001, 0.9, 0.999, 1e-08, 0.01, 10

VMEM_LIMIT = 56 * 1024 * 1024
NEG = -1e30
MESH = pl.DeviceIdType.MESH


def _cparams(**kw):
    return pltpu.CompilerParams(vmem_limit_bytes=VMEM_LIMIT, **kw)


def _dot(a, b, dims):
    return lax.dot_general(a, b, (dims, ((), ())), preferred_element_type=F32)


def _nn(a, b):
    return _dot(a, b, ((1,), (0,)))


def _nt(a, b):
    return _dot(a, b, ((1,), (1,)))


def _tn(a, b):
    return _dot(a, b, ((0,), (0,)))


def _rope_lanes(shape, half, period, first):
    lane = lax.broadcasted_iota(jnp.int32, shape, len(shape) - 1) % period
    return (lane >= first) & (lane < first + half), (lane >= first + half) & (lane < first + 2 * half)


def _rope_fwd(x, c, s, half, lanes):
    x1, _ = lanes
    return x * c + jnp.where(x1, pltpu.roll(x, 128 - half, 1), pltpu.roll(x, half, 1)) * s


def _rope_bwd(g, c, s, half, lanes):
    x1, x2 = lanes
    gs = g * s
    return g * c + jnp.where(x2, pltpu.roll(gs, half, 1), jnp.where(x1, pltpu.roll(gs, 128 - half, 1), 0.0))


def _sigmoid(x):
    return 1.0 / (1.0 + jnp.exp(-x))


def _after(token):
    tokens = [t for t in (token if isinstance(token, (tuple, list)) else [token]) if t is not None]
    return tokens, [pl.BlockSpec(memory_space=pl.ANY)] * len(tokens)


def _matmul(a, b, mode, out_dtype, tm, tn, tk, name, token=None, b_cols=None, a_cols=None):
    after, after_specs = _after(token)
    if mode == "nn":
        (m, k), n = a.shape, b.shape[1]
        first = 0
        if b_cols is not None:
            first, n = b_cols[0], b_cols[1] * tn
        a_spec = pl.BlockSpec((tm, tk), lambda j, i, kk: (i, kk))
        b_spec = pl.BlockSpec((tk, tn), lambda j, i, kk: (kk, j + first))
        dot = _nn
    elif mode == "nt":
        (m, k), n = a.shape, b.shape[0]
        a_spec = pl.BlockSpec((tm, tk), lambda j, i, kk: (i, kk))
        b_spec = pl.BlockSpec((tn, tk), lambda j, i, kk: (j, kk))
        dot = _nt
    else:
        (k, m), n = a.shape, b.shape[1]
        first = 0
        if a_cols is not None:
            first, m = a_cols[0], a_cols[1] * tm
        a_spec = pl.BlockSpec((tk, tm), lambda j, i, kk: (kk, i + first))
        b_spec = pl.BlockSpec((tk, tn), lambda j, i, kk: (kk, j))
        dot = _tn
    assert m % tm == 0 and n % tn == 0 and k % tk == 0, (name, m, n, k, tm, tn, tk)
    nk = k // tk

    def body(a_ref, b_ref, *rest):
        o_ref, acc_ref = rest[-2:]
        kk = pl.program_id(2)
        part = dot(a_ref[...], b_ref[...])

        @pl.when(kk == 0)
        def _():
            acc_ref[...] = part

        @pl.when(kk > 0)
        def _():
            acc_ref[...] += part

        @pl.when(kk == nk - 1)
        def _():
            o_ref[...] = acc_ref[...].astype(o_ref.dtype)

    return pl.pallas_call(
        body, name=name, grid=(n // tn, m // tm, nk),
        in_specs=[a_spec, b_spec] + after_specs,
        out_specs=pl.BlockSpec((tm, tn), lambda j, i, kk: (i, j)),
        out_shape=jax.ShapeDtypeStruct((m, n), out_dtype),
        scratch_shapes=[pltpu.VMEM((tm, tn), F32)],
        compiler_params=_cparams(),
    )(a, b, *after)


def _prenorm_fwd(x, g, token=None):
    tm = 512
    after, after_specs = _after(token)

    def body(x_ref, g_ref, *rest):
        xv = x_ref[...]
        r = lax.rsqrt(jnp.mean(xv * xv, axis=-1, keepdims=True) + EPS)
        rest[-1][...] = (xv * r * g_ref[...]).astype(BF16)

    return pl.pallas_call(
        body, name="prenorm_fwd", grid=(SEQ // tm,),
        in_specs=[pl.BlockSpec((tm, D_MODEL), lambda i: (i, 0)), pl.BlockSpec((1, D_MODEL), lambda i: (0, 0))] + after_specs,
        out_specs=pl.BlockSpec((tm, D_MODEL), lambda i: (i, 0)),
        out_shape=jax.ShapeDtypeStruct((SEQ, D_MODEL), BF16),
    )(x, g, *after)


def _dh_prenorm_bwd(dp, w_pad_t, x, dy, g, token=None):
    tm, tk = 1024, 1408
    nk = N_PAD // tk
    after, after_specs = _after(token)

    def body(a_ref, b_ref, x_ref, dy_ref, g_ref, *rest):
        gx_ref, dg_ref, acc_ref = rest[-3:]
        i, kk = pl.program_id(0), pl.program_id(1)
        part = _nn(a_ref[...], b_ref[...])

        @pl.when(kk == 0)
        def _():
            acc_ref[...] = part

        @pl.when(kk > 0)
        def _():
            acc_ref[...] += part

        @pl.when(kk == nk - 1)
        def _():
            xv = x_ref[...]
            r = lax.rsqrt(jnp.mean(xv * xv, axis=-1, keepdims=True) + EPS)
            n = xv * r
            dhv = acc_ref[...]
            dn = dhv * g_ref[...]
            gx_ref[...] = dy_ref[...] + r * (dn - n * jnp.mean(dn * n, axis=-1, keepdims=True))
            cols = jnp.sum(dhv * n, axis=0, keepdims=True)

            @pl.when(i == 0)
            def _():
                dg_ref[...] = cols

            @pl.when(i > 0)
            def _():
                dg_ref[...] += cols

    row = pl.BlockSpec((tm, D_MODEL), lambda i, kk: (i, 0))
    vec = pl.BlockSpec((1, D_MODEL), lambda i, kk: (0, 0))
    return pl.pallas_call(
        body, name="dh_prenorm_bwd", grid=(SEQ // tm, nk),
        in_specs=[pl.BlockSpec((tm, tk), lambda i, kk: (i, kk)), pl.BlockSpec((tk, D_MODEL), lambda i, kk: (kk, 0)),
                  row, row, vec] + after_specs,
        out_specs=[row, vec],
        out_shape=[jax.ShapeDtypeStruct((SEQ, D_MODEL), F32), jax.ShapeDtypeStruct((1, D_MODEL), F32)],
        scratch_shapes=[pltpu.VMEM((tm, D_MODEL), F32)],
        compiler_params=_cparams(),
    )(dp, w_pad_t, x, dy, g, *after)


def _mla_prep_fwd(p, qg, kvg, wuq, wk, wv, rc, rs):
    tm = 512

    def body(lat_ref, qg_ref, kvg_ref, wuq_ref, wk_ref, wv_ref, c_ref, s_ref, q_ref, k_ref, v_ref):
        c, s = c_ref[...], s_ref[...]
        lanes = _rope_lanes((tm, 128), MLA_ROPE_HALF, 128, 64)
        cq = lat_ref[:, 0:Q_RANK]
        r1 = lax.rsqrt(jnp.mean(cq * cq, axis=-1, keepdims=True) + EPS)
        cqn = (cq * r1 * qg_ref[...]).astype(BF16)
        q = _nn(cqn, wuq_ref[...])
        for h in range(MLA_HEADS):
            sl = slice(h * 128, (h + 1) * 128)
            q_ref[:, sl] = (_rope_fwd(q[:, sl], c, s, MLA_ROPE_HALF, lanes) * MLA_SCALE).astype(BF16)
        ckv = lat_ref[:, Q_RANK:Q_RANK + KV_RANK]
        r2 = lax.rsqrt(jnp.mean(ckv * ckv, axis=-1, keepdims=True) + EPS)
        ckvn = (ckv * r2 * kvg_ref[...]).astype(BF16)
        krr = _rope_fwd(lat_ref[:, Q_RANK + KV_RANK:N_LAT], c, s, MLA_ROPE_HALF, lanes)
        kn = _nn(ckvn, wk_ref[...])
        for h in range(MLA_HEADS):
            sl = slice(h * 128, (h + 1) * 128)
            k_ref[:, sl] = (kn[:, sl] + krr).astype(BF16)
        v_ref[...] = _nn(ckvn, wv_ref[...]).astype(BF16)

    def full(shape):
        return pl.BlockSpec(shape, lambda i: (0, 0))

    def rows(w):
        return pl.BlockSpec((tm, w), lambda i: (i, 0))

    return pl.pallas_call(
        body, name="mla_prep_fwd", grid=(SEQ // tm,),
        in_specs=[pl.BlockSpec((tm, N_LAT), lambda i: (i, COL_LAT // N_LAT)),
                  full((1, Q_RANK)), full((1, KV_RANK)), full((Q_RANK, 1024)), full((KV_RANK, 1024)),
                  full((KV_RANK, 512)), rows(128), rows(128)],
        out_specs=[rows(1024), rows(1024), rows(512)],
        out_shape=[jax.ShapeDtypeStruct((SEQ, 1024), BF16), jax.ShapeDtypeStruct((SEQ, 1024), BF16),
                   jax.ShapeDtypeStruct((SEQ, 512), BF16)],
        compiler_params=_cparams(),
    )(p, qg, kvg, wuq, wk, wv, rc, rs)


def _mla_prep_bwd(dp_in, p, dq, dk, dv, qg, kvg, wuq, wk, wv, rc, rs):
    tm = 512

    def body(dp_any, lat_ref, dq_ref, dk_ref, dv_ref, qg_ref, kvg_ref, wuq_ref, wk_ref, wv_ref,
             c_ref, s_ref, dp_ref, dwuq_ref, dwk_ref, dwv_ref, dgq_ref, dgkv_ref, dqb_ref, dkb_ref):
        del dp_any
        c, s = c_ref[...], s_ref[...]
        lanes = _rope_lanes((tm, 128), MLA_ROPE_HALF, 128, 64)
        lane = lax.broadcasted_iota(jnp.int32, (tm, 128), 1)
        dkr = jnp.zeros((tm, 128), F32)
        for h in range(MLA_HEADS):
            sl = slice(h * 128, (h + 1) * 128)
            dqb_ref[:, sl] = _rope_bwd(dq_ref[:, sl] * MLA_SCALE, c, s, MLA_ROPE_HALF, lanes).astype(BF16)
            dkh = dk_ref[:, sl]
            dkr = dkr + dkh
            dkb_ref[:, sl] = jnp.where(lane < 64, dkh, 0.0).astype(BF16)
        dkr = jnp.where((lane >= 64) & (lane < 96), dkr, 0.0)
        dkr = _rope_bwd(dkr, c, s, MLA_ROPE_HALF, lanes)
        dvb = dv_ref[...].astype(BF16)

        cq = lat_ref[:, 0:Q_RANK]
        r1 = lax.rsqrt(jnp.mean(cq * cq, axis=-1, keepdims=True) + EPS)
        n1 = cq * r1
        dcqn = _nt(dqb_ref[...], wuq_ref[...])
        dn1 = dcqn * qg_ref[...]
        dcq = r1 * (dn1 - n1 * jnp.mean(dn1 * n1, axis=-1, keepdims=True))
        pq = jnp.sum(dcqn * n1, axis=0, keepdims=True)

        ckv = lat_ref[:, Q_RANK:Q_RANK + KV_RANK]
        r2 = lax.rsqrt(jnp.mean(ckv * ckv, axis=-1, keepdims=True) + EPS)
        n2 = ckv * r2
        dckvn = _nt(dkb_ref[...], wk_ref[...]) + _nt(dvb, wv_ref[...])
        dn2 = dckvn * kvg_ref[...]
        dckv = r2 * (dn2 - n2 * jnp.mean(dn2 * n2, axis=-1, keepdims=True))
        pkv = jnp.sum(dckvn * n2, axis=0, keepdims=True)
        cqn = (n1 * qg_ref[...]).astype(BF16)
        ckvn = (n2 * kvg_ref[...]).astype(BF16)
        wq, wk_, wv_ = _tn(cqn, dqb_ref[...]), _tn(ckvn, dkb_ref[...]), _tn(ckvn, dvb)

        dp_ref[:, 0:Q_RANK] = dcq.astype(BF16)
        dp_ref[:, Q_RANK:Q_RANK + KV_RANK] = dckv.astype(BF16)
        dp_ref[:, Q_RANK + KV_RANK:N_LAT] = dkr.astype(BF16)

        @pl.when(pl.program_id(0) == 0)
        def _():
            dgq_ref[...] = pq
            dgkv_ref[...] = pkv
            dwuq_ref[...] = wq
            dwk_ref[...] = wk_
            dwv_ref[...] = wv_

        @pl.when(pl.program_id(0) > 0)
        def _():
            dgq_ref[...] += pq
            dgkv_ref[...] += pkv
            dwuq_ref[...] += wq
            dwk_ref[...] += wk_
            dwv_ref[...] += wv_

    def full(shape):
        return pl.BlockSpec(shape, lambda i: (0, 0))

    def rows(w):
        return pl.BlockSpec((tm, w), lambda i: (i, 0))

    lat = pl.BlockSpec((tm, N_LAT), lambda i: (i, COL_LAT // N_LAT))
    return pl.pallas_call(
        body, name="mla_prep_bwd", grid=(SEQ // tm,),
        in_specs=[pl.BlockSpec(memory_space=pl.ANY), lat, rows(1024), rows(1024), rows(512),
                  full((1, Q_RANK)), full((1, KV_RANK)), full((Q_RANK, 1024)), full((KV_RANK, 1024)),
                  full((KV_RANK, 512)), rows(128), rows(128)],
        out_specs=[lat, full((Q_RANK, 1024)), full((KV_RANK, 1024)), full((KV_RANK, 512)),
                   full((1, Q_RANK)), full((1, KV_RANK))],
        out_shape=[jax.ShapeDtypeStruct((SEQ, N_PAD), BF16), jax.ShapeDtypeStruct((Q_RANK, 1024), F32),
                   jax.ShapeDtypeStruct((KV_RANK, 1024), F32), jax.ShapeDtypeStruct((KV_RANK, 512), F32),
                   jax.ShapeDtypeStruct((1, Q_RANK), F32), jax.ShapeDtypeStruct((1, KV_RANK), F32)],
        input_output_aliases={0: 0},
        scratch_shapes=[pltpu.VMEM((tm, 1024), BF16), pltpu.VMEM((tm, 1024), BF16)],
        compiler_params=_cparams(),
    )(dp_in, p, dq, dk, dv, qg, kvg, wuq, wk, wv, rc, rs)


FLASH_T = 1024


def _head_half(shape, hh):
    lane = lax.broadcasted_iota(jnp.int32, shape, 1)
    return (lane < 64) if hh == 0 else (lane >= 64)


def _diag_keep(nr, nk):
    row = lax.broadcasted_iota(jnp.int32, (nr, nk), 0)
    col = lax.broadcasted_iota(jnp.int32, (nr, nk), 1)
    return row + (nk - nr) >= col


def _tri_steps(nb, q_major):
    if q_major:
        pairs = [(i, kb) for i in range(nb) for kb in range(i + 1)]
    else:
        pairs = [(i, kb) for kb in range(nb) for i in range(kb, nb)]
    return jnp.asarray([p[0] for p in pairs], jnp.int32), jnp.asarray([p[1] for p in pairs], jnp.int32)


def _mla_flash_fwd(q, k, v):
    t = FLASH_T
    nb = SEQ // t
    qtab, ktab = _tri_steps(nb, True)

    def body(qi_ref, ki_ref, q_ref, k_ref, v_ref, o_ref, lse_ref, m_scr, l_scr, acc_scr):
        step = pl.program_id(1)
        i, kb = qi_ref[step], ki_ref[step]

        @pl.when(kb == 0)
        def _():
            m_scr[...] = jnp.full_like(m_scr, NEG)
            l_scr[...] = jnp.zeros_like(l_scr)
            acc_scr[...] = jnp.zeros_like(acc_scr)

        def update(r0, nr, nk, diagonal):
            rs = slice(r0, r0 + nr)
            vv = v_ref[0:nk, :]
            for hh in range(2):
                sl = slice(hh * 128, (hh + 1) * 128)
                s = _nt(q_ref[rs, sl], k_ref[0:nk, sl])
                if diagonal:
                    s = jnp.where(_diag_keep(nr, nk), s, NEG)
                m_prev = m_scr[hh, rs, :]
                m_new = jnp.maximum(m_prev, jnp.max(s, axis=-1, keepdims=True))
                pr = jnp.exp(s - jnp.tile(m_new, (1, nk // 128)))
                alpha = jnp.exp(m_prev - m_new)
                l_scr[hh, rs, :] = alpha * l_scr[hh, rs, :] + jnp.sum(pr, axis=-1, keepdims=True)
                acc_scr[hh, rs, :] = alpha * acc_scr[hh, rs, :] + _nn(pr.astype(BF16), vv)
                m_scr[hh, rs, :] = m_new

        @pl.when(kb < i)
        def _():
            update(0, t, t, False)

        @pl.when(kb == i)
        def _():
            update(0, t // 2, t // 2, True)
            update(t // 2, t // 2, t, True)
            o0 = acc_scr[0] / l_scr[0]
            o1 = acc_scr[1] / l_scr[1]
            o_ref[...] = jnp.where(_head_half((t, 128), 0), o0, o1)
            for hh in range(2):
                lse_ref[:, hh * 128:(hh + 1) * 128] = m_scr[hh] + jnp.log(l_scr[hh])

    grid_spec = pltpu.PrefetchScalarGridSpec(
        num_scalar_prefetch=2, grid=(4, qtab.shape[0]),
        in_specs=[pl.BlockSpec((t, 256), lambda j, s, qi, ki: (qi[s], j)),
                  pl.BlockSpec((t, 256), lambda j, s, qi, ki: (ki[s], j)),
                  pl.BlockSpec((t, 128), lambda j, s, qi, ki: (ki[s], j))],
        out_specs=[pl.BlockSpec((t, 128), lambda j, s, qi, ki: (qi[s], j)),
                   pl.BlockSpec((t, 256), lambda j, s, qi, ki: (qi[s], j))],
        scratch_shapes=[pltpu.VMEM((2, t, 128), F32), pltpu.VMEM((2, t, 128), F32), pltpu.VMEM((2, t, 128), F32)])
    return pl.pallas_call(
        body, name="mla_flash_fwd", grid_spec=grid_spec,
        out_shape=[jax.ShapeDtypeStruct((SEQ, 512), F32), jax.ShapeDtypeStruct((SEQ, 1024), F32)],
        compiler_params=_cparams(),
    )(qtab, ktab, q, k, v)


def _mla_flash_bwd(q, k, v, o, do, lse, token=None):
    t = FLASH_T
    nb = SEQ // t
    qtab, ktab = _tri_steps(nb, False)
    after, after_specs = _after(token)

    def body(qi_ref, ki_ref, q_ref, k_ref, v_ref, o_ref, do_ref, lse_ref, *rest):
        dq_ref, dk_ref, dv_ref, dk_scr, dv_scr = rest[-5:]
        step = pl.program_id(1)
        i, kb = qi_ref[step], ki_ref[step]

        @pl.when(step == 0)
        def _():
            dq_ref[...] = jnp.zeros_like(dq_ref)

        @pl.when(i == kb)
        def _():
            dk_scr[...] = jnp.zeros_like(dk_scr)
            dv_scr[...] = jnp.zeros_like(dv_scr)

        def update(r0, nr, nk, diagonal):
            rs = slice(r0, r0 + nr)
            vv = v_ref[0:nk, :]
            ov = o_ref[rs, :]
            dov = do_ref[rs, :]
            rows = pl.ds(pl.multiple_of(i * t + r0, t // 2), nr)
            for hh in range(2):
                sl = slice(hh * 128, (hh + 1) * 128)
                qh, kh = q_ref[rs, sl], k_ref[0:nk, sl]
                s = _nt(qh, kh)
                if diagonal:
                    s = jnp.where(_diag_keep(nr, nk), s, NEG)
                pr = jnp.exp(s - jnp.tile(lse_ref[rs, sl], (1, nk // 128)))
                dom = jnp.where(_head_half((nr, 128), hh), dov, 0.0)
                domb = dom.astype(BF16)
                dv_scr[0:nk, :] += _tn(pr.astype(BF16), domb)
                dpr = _nt(domb, vv)
                delta = jnp.sum(dom * ov, axis=-1, keepdims=True)
                ds = (pr * (dpr - delta)).astype(BF16)
                dq_ref[rows, sl] += _nn(ds, kh)
                dk_scr[hh, 0:nk, :] += _tn(ds, qh)

        @pl.when(i > kb)
        def _():
            update(0, t, t, False)

        @pl.when(i == kb)
        def _():
            update(0, t // 2, t // 2, True)
            update(t // 2, t // 2, t, True)

        @pl.when(i == nb - 1)
        def _():
            dk_ref[:, 0:128] = dk_scr[0]
            dk_ref[:, 128:256] = dk_scr[1]
            dv_ref[...] = dv_scr[...]

    qi_map = lambda j, s, qi, ki: (qi[s], j)
    ki_map = lambda j, s, qi, ki: (ki[s], j)
    grid_spec = pltpu.PrefetchScalarGridSpec(
        num_scalar_prefetch=2, grid=(4, qtab.shape[0]),
        in_specs=[pl.BlockSpec((t, 256), qi_map), pl.BlockSpec((t, 256), ki_map), pl.BlockSpec((t, 128), ki_map),
                  pl.BlockSpec((t, 128), qi_map), pl.BlockSpec((t, 128), qi_map), pl.BlockSpec((t, 256), qi_map)]
        + after_specs,
        out_specs=[pl.BlockSpec((SEQ, 256), lambda j, s, qi, ki: (0, j)), pl.BlockSpec((t, 256), ki_map),
                   pl.BlockSpec((t, 128), ki_map)],
        scratch_shapes=[pltpu.VMEM((2, t, 128), F32), pltpu.VMEM((t, 128), F32)])
    return pl.pallas_call(
        body, name="mla_flash_bwd", grid_spec=grid_spec,
        out_shape=[jax.ShapeDtypeStruct((SEQ, 1024), F32), jax.ShapeDtypeStruct((SEQ, 1024), F32),
                   jax.ShapeDtypeStruct((SEQ, 512), F32)],
        compiler_params=_cparams(),
    )(qtab, ktab, q, k, v, o, do, lse, *after)


def _strided(start, size, d):
    return pl.ds(start, size) if d == 1 else pl.ds(start, size, stride=d)


def _dil_prep_fwd(p, rc, rs, g):
    d = DIL_DILATIONS[g]
    sub_len = SEQ // d
    ch = min(sub_len, 512)

    def body(p_ref, c_ref, s_ref, o_ref, x_scr):
        tq = pl.program_id(0)
        lanes = _rope_lanes((ch, 128), DIL_ROPE_HALF, 64, 0)
        o_ref[0, 0:BAND, :] = jnp.zeros((BAND, 128), BF16)

        @pl.when(tq < 2)
        def _():
            mult = jnp.where(tq == 0, DIL_SCALE, 1.0).astype(F32)
            for c0 in range(0, SEQ, ch):
                rows = pl.ds(c0, ch)
                x_scr[rows, :] = _rope_fwd(p_ref[rows, :], c_ref[rows, :] * mult, s_ref[rows, :] * mult, DIL_ROPE_HALF, lanes)

        def gather(src):
            for r in range(d):
                for c0 in range(0, sub_len, ch):
                    at = BAND + r * sub_len + c0
                    o_ref[0, at:at + ch, :] = src[_strided(r + c0 * d, ch, d), :].astype(BF16)

        @pl.when(tq < 2)
        def _():
            gather(x_scr)

        @pl.when(tq == 2)
        def _():
            gather(p_ref)

    tab = pl.BlockSpec((SEQ, 128), lambda tq, pr: (0, 0))
    return pl.pallas_call(
        body, name=f"dil_prep_fwd_g{g}", grid=(3, 4),
        in_specs=[pl.BlockSpec((SEQ, 128), lambda tq, pr: (0, _qkv_block(tq, g, pr))), tab, tab],
        out_specs=pl.BlockSpec((1, BAND + SEQ, 128), lambda tq, pr: (tq, 0, pr)),
        out_shape=jax.ShapeDtypeStruct((3, BAND + SEQ, 512), BF16),
        scratch_shapes=[pltpu.VMEM((SEQ, 128), F32)],
        compiler_params=_cparams(),
    )(p, rc, rs)


DIL_ST_FWD, DIL_ST_BWD = 1024, 2048


def _band_keep(g, b, t, nb):
    nbs = SEQ // DIL_DILATIONS[g] // BAND
    row = lax.broadcasted_iota(jnp.int32, (BAND, 2 * BAND), 0)
    col = lax.broadcasted_iota(jnp.int32, (BAND, 2 * BAND), 1)
    cur = (col >= BAND) & (row >= col - BAND)
    prev = (col < BAND) & (col >= row)
    if nbs >= nb:
        if b > 0:
            return cur | prev
        return cur | (prev & ((t * nb) % nbs != 0))
    return cur | prev if b % nbs else cur


def _dil_tok(g, b, t, nb):
    d = DIL_DILATIONS[g]
    nbs = SEQ // d // BAND
    gb = t * nb + b
    return _strided((gb % nbs) * BAND * d + gb // nbs, BAND, d)


def _dil_attn_fwd(qkv, g):
    DIL_ST, DIL_NB = DIL_ST_FWD, DIL_ST_FWD // BAND

    def body(q_ref, k_ref, v_ref, o_ref, l_ref, s_scr, p_scr, o_scr):
        t = pl.program_id(1)
        base = t * DIL_ST
        half0 = _head_half((DIL_ST, 128), 0)
        lse_h = []
        for hh in range(2):
            half = _head_half((BAND, 128), hh)
            for b in range(DIL_NB):
                qv = q_ref[0, pl.ds(pl.multiple_of(base + (b + 1) * BAND, BAND), BAND), :]
                k2 = k_ref[0, pl.ds(pl.multiple_of(base + b * BAND, BAND), 2 * BAND), :]
                sb = _nt(jnp.where(half, qv, jnp.zeros_like(qv)), k2)
                s_scr[b * BAND:(b + 1) * BAND, :] = jnp.where(_band_keep(g, b, t, DIL_NB), sb, NEG)
            s = s_scr[...]
            m = jnp.max(s, axis=-1, keepdims=True)
            pr = jnp.exp(s - m)
            den = jnp.sum(pr, axis=-1, keepdims=True)
            p_scr[...] = pr.astype(BF16)
            for b in range(DIL_NB):
                v2 = v_ref[0, pl.ds(pl.multiple_of(base + b * BAND, BAND), 2 * BAND), :]
                o_scr[hh, b * BAND:(b + 1) * BAND, :] = _nn(p_scr[b * BAND:(b + 1) * BAND, :], v2)
            o_scr[hh] = o_scr[hh] / den
            lse_h.append(m + jnp.log(den))
        out = jnp.where(half0, o_scr[0], o_scr[1])
        lse = jnp.where(half0, lse_h[0], lse_h[1])
        for b in range(DIL_NB):
            tok = _dil_tok(g, b, t, DIL_NB)
            o_ref[tok, :] = out[b * BAND:(b + 1) * BAND, :]
            l_ref[tok, :] = lse[b * BAND:(b + 1) * BAND, :]

    def inp(tq):
        return pl.BlockSpec((1, BAND + SEQ, 128), lambda pr, t: (tq, 0, pr))

    out = pl.BlockSpec((SEQ, 128), lambda pr, t: (0, pr))
    return pl.pallas_call(
        body, name=f"dil_attn_fwd_g{g}", grid=(4, SEQ // DIL_ST),
        in_specs=[inp(0), inp(1), inp(2)], out_specs=[out, out],
        out_shape=[jax.ShapeDtypeStruct((SEQ, 512), F32), jax.ShapeDtypeStruct((SEQ, 512), F32)],
        scratch_shapes=[pltpu.VMEM((DIL_ST, 2 * BAND), F32), pltpu.VMEM((DIL_ST, 2 * BAND), BF16),
                        pltpu.VMEM((2, DIL_ST, 128), F32)],
        compiler_params=_cparams(),
    )(qkv, qkv, qkv)


def _dil_attn_bwd(dp_in, qkv, dyd, yd, lse_all, rc, rs, g, token=None):
    d = DIL_DILATIONS[g]
    sub_len = SEQ // d
    DIL_ST, DIL_NB = DIL_ST_BWD, DIL_ST_BWD // BAND
    nst = SEQ // DIL_ST
    after, after_specs = _after(token)
    ch = 512

    def body(dp_any, q_ref, k_ref, v_ref, do_ref, y_ref, l_ref, c_ref, sn_ref, *rest):
        dp_ref, tok_scr, dk_scr, dv_scr, s_scr, dp_scr, p_scr, ds_scr, do_scr, y_scr, l_scr, dq_scr = rest[-12:]
        del dp_any
        t = pl.program_id(1)
        base = t * DIL_ST

        @pl.when(t == 0)
        def _():
            dk_scr[...] = jnp.zeros_like(dk_scr)
            dv_scr[...] = jnp.zeros_like(dv_scr)

        for b in range(DIL_NB):
            tok = _dil_tok(g, b, t, DIL_NB)
            do_scr[b * BAND:(b + 1) * BAND, :] = do_ref[tok, :]
            y_scr[b * BAND:(b + 1) * BAND, :] = y_ref[tok, :]
            l_scr[b * BAND:(b + 1) * BAND, :] = l_ref[tok, :]
        for hh in range(2):
            half = _head_half((BAND, 128), hh)
            half_st = _head_half((DIL_ST, 128), hh)
            dom = jnp.where(half_st, do_scr[...], 0.0)
            delta = jnp.sum(dom * y_scr[...], axis=-1, keepdims=True)
            lcol = jnp.max(jnp.where(half_st, l_scr[...], NEG), axis=-1, keepdims=True)
            for b in range(DIL_NB):
                rows = slice(b * BAND, (b + 1) * BAND)
                qv = q_ref[0, pl.ds(pl.multiple_of(base + (b + 1) * BAND, BAND), BAND), :]
                band = pl.ds(pl.multiple_of(base + b * BAND, BAND), 2 * BAND)
                sb = _nt(jnp.where(half, qv, jnp.zeros_like(qv)), k_ref[0, band, :])
                s_scr[rows, :] = jnp.where(_band_keep(g, b, t, DIL_NB), sb, NEG)
                dp_scr[rows, :] = _nt(dom[rows, :].astype(BF16), v_ref[0, band, :])
            pr = jnp.exp(s_scr[...] - lcol)
            p_scr[...] = pr.astype(BF16)
            ds_scr[...] = (pr * (dp_scr[...] - delta)).astype(BF16)
            for b in range(DIL_NB):
                rows = slice(b * BAND, (b + 1) * BAND)
                qv = q_ref[0, pl.ds(pl.multiple_of(base + (b + 1) * BAND, BAND), BAND), :]
                band = pl.ds(pl.multiple_of(base + b * BAND, BAND), 2 * BAND)
                dqb = jnp.where(half, _nn(ds_scr[rows, :], k_ref[0, band, :]), 0.0)
                if hh == 0:
                    dq_scr[rows, :] = dqb
                else:
                    dq_scr[rows, :] += dqb
                half2 = _head_half((2 * BAND, 128), hh)
                dk_scr[band, :] += jnp.where(half2, _tn(ds_scr[rows, :], qv), 0.0)
                dv_scr[band, :] += _tn(p_scr[rows, :], dom[rows, :].astype(BF16))
        for b in range(DIL_NB):
            tok_scr[pl.ds(0, 1), _dil_tok(g, b, t, DIL_NB), :] = dq_scr[b * BAND:(b + 1) * BAND, :][None]

        @pl.when(t == nst - 1)
        def _():
            for r in range(d):
                rows = _strided(r, sub_len, d)
                tok_scr[pl.ds(1, 1), rows, :] = dk_scr[BAND + r * sub_len:BAND + (r + 1) * sub_len, :][None]
                tok_scr[pl.ds(2, 1), rows, :] = dv_scr[BAND + r * sub_len:BAND + (r + 1) * sub_len, :][None]
            lanes = _rope_lanes((ch, 128), DIL_ROPE_HALF, 64, 0)
            for c0 in range(0, SEQ, ch):
                rows = slice(c0, c0 + ch)
                cv, sv = c_ref[rows, :], sn_ref[rows, :]
                dp_ref[rows, 0:128] = _rope_bwd(tok_scr[0, rows, :], cv * DIL_SCALE, sv * DIL_SCALE, DIL_ROPE_HALF, lanes).astype(BF16)
                dp_ref[rows, 128:256] = _rope_bwd(tok_scr[1, rows, :], cv, sv, DIL_ROPE_HALF, lanes).astype(BF16)
                dp_ref[rows, 256:384] = tok_scr[2, rows, :].astype(BF16)

    def inp(tq):
        return pl.BlockSpec((1, BAND + SEQ, 128), lambda pr, t: (tq, 0, pr))

    tok_spec = pl.BlockSpec((SEQ, 128), lambda pr, t: (0, pr))
    tab = pl.BlockSpec((SEQ, 128), lambda pr, t: (0, 0))
    st = (DIL_ST, 2 * BAND)
    return pl.pallas_call(
        body, name=f"dil_attn_bwd_g{g}", grid=(4, nst),
        in_specs=[pl.BlockSpec(memory_space=pl.ANY), inp(0), inp(1), inp(2), tok_spec, tok_spec, tok_spec, tab, tab]
        + after_specs,
        out_specs=pl.BlockSpec((SEQ, 384), lambda pr, t: (0, _qkv_block(0, g, pr) // 3)),
        out_shape=jax.ShapeDtypeStruct((SEQ, N_PAD), BF16),
        input_output_aliases={0: 0},
        scratch_shapes=[pltpu.VMEM((3, SEQ, 128), F32),
                        pltpu.VMEM((BAND + SEQ, 128), F32), pltpu.VMEM((BAND + SEQ, 128), F32),
                        pltpu.VMEM(st, F32), pltpu.VMEM(st, F32), pltpu.VMEM(st, BF16), pltpu.VMEM(st, BF16),
                        pltpu.VMEM((DIL_ST, 128), F32), pltpu.VMEM((DIL_ST, 128), F32), pltpu.VMEM((DIL_ST, 128), F32),
                        pltpu.VMEM((DIL_ST, 128), F32)],
        compiler_params=_cparams(),
    )(dp_in, qkv, qkv, qkv, dyd, yd, lse_all, rc, rs, *after)


TAIL_T = 256


def _tail(p, ya, o_g, l_g, x, target, wpm, wpd, wout, post_g):
    tm = TAIL_T

    def body(pgz_ref, ya_ref, o0_ref, o1_ref, o2_ref, l0_ref, l1_ref, l2_ref, x_ref, t_ref,
             wpm_ref, wpd_ref, wout_ref, pg_ref,
             dp_ref, dy_ref, mg_ref, dt_ref, ua_ref, dpa_ref, ud_ref, dpd_ref, dya_ref, dyd_ref,
             yd_ref, lse_ref, loss_ref, dgp_ref):
        l0, l1, l2 = l0_ref[...], l1_ref[...], l2_ref[...]
        mx = jnp.maximum(jnp.maximum(l0, l1), l2)
        e0, e1, e2 = jnp.exp(l0 - mx), jnp.exp(l1 - mx), jnp.exp(l2 - mx)
        den = e0 + e1 + e2
        yd = (e0 * o0_ref[...] + e1 * o1_ref[...] + e2 * o2_ref[...]) / den
        yd_ref[...] = yd
        lse_ref[...] = mx + jnp.log(den)
        ya = ya_ref[...]

        gm, gd = pgz_ref[:, 0:1024], pgz_ref[:, 1024:2048]
        zm, zd = pgz_ref[:, 2048:2560], pgz_ref[:, 2560:3072]
        szm, szd = _sigmoid(zm), _sigmoid(zd)
        sm, sd = zm * szm, zd * szd
        ua = (ya * sm).astype(BF16)
        ud = (yd * sd).astype(BF16)
        ua_ref[...] = ua
        ud_ref[...] = ud
        pa = _nn(ua, wpm_ref[...])
        pd = _nn(ud, wpd_ref[...])
        sgm, sgd = _sigmoid(gm), _sigmoid(gd)
        mg = (sgm * pa + sgd * pd).astype(BF16)
        mg_ref[...] = mg
        t = _nn(mg, wout_ref[...])
        r3 = lax.rsqrt(jnp.mean(t * t, axis=-1, keepdims=True) + EPS)
        n = t * r3
        pg = pg_ref[...]
        err = x_ref[...] + n * pg - t_ref[...]
        lpart = jnp.sum(err * err, axis=0, keepdims=True)

        dy = err * (1.0 / D_MODEL)
        dy_ref[...] = dy
        gpart = jnp.sum(dy * n, axis=0, keepdims=True)
        dn = dy * pg
        dt = (r3 * (dn - n * jnp.mean(dn * n, axis=-1, keepdims=True))).astype(BF16)
        dt_ref[...] = dt
        dmg = _nt(dt, wout_ref[...])
        dpa = (dmg * sgm).astype(BF16)
        dpd = (dmg * sgd).astype(BF16)
        dpa_ref[...] = dpa
        dpd_ref[...] = dpd
        dp_ref[:, 0:1024] = (dmg * pa * sgm * (1.0 - sgm)).astype(BF16)
        dp_ref[:, 1024:2048] = (dmg * pd * sgd * (1.0 - sgd)).astype(BF16)
        dua = _nt(dpa, wpm_ref[...])
        dud = _nt(dpd, wpd_ref[...])
        dya_ref[...] = dua * sm
        dyd_ref[...] = dud * sd
        dp_ref[:, 2048:2560] = (dua * ya * szm * (1.0 + zm * (1.0 - szm))).astype(BF16)
        dp_ref[:, 2560:3072] = (dud * yd * szd * (1.0 + zd * (1.0 - szd))).astype(BF16)

        @pl.when(pl.program_id(0) == 0)
        def _():
            loss_ref[...] = lpart
            dgp_ref[...] = gpart

        @pl.when(pl.program_id(0) > 0)
        def _():
            loss_ref[...] += lpart
            dgp_ref[...] += gpart

    def rows(w):
        return pl.BlockSpec((tm, w), lambda i: (i, 0))

    def full(shape):
        return pl.BlockSpec(shape, lambda i: (0, 0))

    def sds(w, dt):
        return jax.ShapeDtypeStruct((SEQ, w), dt)

    return pl.pallas_call(
        body, name="tail", grid=(SEQ // tm,),
        in_specs=[rows(3072), rows(512), rows(512), rows(512), rows(512), rows(512), rows(512), rows(512),
                  rows(1024), rows(1024), full((512, 1024)), full((512, 1024)), full((1024, 1024)), full((1, 1024))],
        out_specs=[rows(3072), rows(1024), rows(1024), rows(1024), rows(512), rows(1024), rows(512), rows(1024),
                   rows(512), rows(512), rows(512), rows(512), full((1, 1024)), full((1, 1024))],
        out_shape=[sds(N_PAD, BF16), sds(1024, F32), sds(1024, BF16), sds(1024, BF16), sds(512, BF16),
                   sds(1024, BF16), sds(512, BF16), sds(1024, BF16), sds(512, F32), sds(512, F32),
                   sds(512, F32), sds(512, F32),
                   jax.ShapeDtypeStruct((1, 1024), F32), jax.ShapeDtypeStruct((1, 1024), F32)],
        compiler_params=_cparams(),
    )(p, ya, o_g[0], o_g[1], o_g[2], l_g[0], l_g[1], l_g[2], x, target, wpm, wpd, wout, post_g)


def _sum_parts(recv, own, me, tr, name):
    n, r, w = recv.shape
    if r % tr:
        return _sum_parts_cols(recv, own, me, name)
    own_spec = (pl.BlockSpec((tr, w), lambda i, me_ref: (i, 0)) if own.ndim == 2
                else pl.BlockSpec((None, tr, w), lambda i, me_ref: (me_ref[0], i, 0)))

    def body(me_ref, p_ref, own_ref, o_ref):
        mine = own_ref[...].astype(F32)
        acc = jnp.zeros((tr, w), F32)
        for s in range(n):
            acc = acc + jnp.where(me_ref[0] == s, mine, p_ref[s].astype(F32))
        o_ref[...] = acc

    return pl.pallas_call(
        body, name=name,
        grid_spec=pltpu.PrefetchScalarGridSpec(
            num_scalar_prefetch=1, grid=(r // tr,),
            in_specs=[pl.BlockSpec((n, tr, w), lambda i, me_ref: (0, i, 0)), own_spec],
            out_specs=pl.BlockSpec((tr, w), lambda i, me_ref: (i, 0))),
        out_shape=jax.ShapeDtypeStruct((r, w), F32),
    )(me.reshape(1), recv, own)


def _sum_parts_cols(recv, own, me, name):
    n, r, w = recv.shape
    tc = 128

    def body(me_ref, p_ref, own_ref, o_ref):
        mine = own_ref[...].astype(F32)
        acc = jnp.zeros((r, tc), F32)
        for s in range(n):
            acc = acc + jnp.where(me_ref[0] == s, mine, p_ref[s].astype(F32))
        o_ref[...] = acc

    return pl.pallas_call(
        body, name=name,
        grid_spec=pltpu.PrefetchScalarGridSpec(
            num_scalar_prefetch=1, grid=(w // tc,),
            in_specs=[pl.BlockSpec((n, r, tc), lambda i, me_ref: (0, 0, i)),
                      pl.BlockSpec((None, r, tc), lambda i, me_ref: (me_ref[0], 0, i))],
            out_specs=pl.BlockSpec((r, tc), lambda i, me_ref: (0, i))),
        out_shape=jax.ShapeDtypeStruct((r, w), F32),
    )(me.reshape(1), recv, own)


def _adamw(w, g, m, v, name):
    lead = w.shape[:-2]
    r, c = w.shape[-2:]
    tr = max([t for t in range(8, 257, 8) if r % t == 0], default=r)
    c1 = 1.0 - ADAM_B1 ** ADAM_STEP
    c2 = 1.0 - ADAM_B2 ** ADAM_STEP

    def body(w_ref, g_ref, m_ref, v_ref, d_ref, nm_ref, nv_ref):
        gv = g_ref[...]
        nm = ADAM_B1 * m_ref[...] + (1.0 - ADAM_B1) * gv
        nv = ADAM_B2 * v_ref[...] + (1.0 - ADAM_B2) * (gv * gv)
        nm_ref[...] = nm
        nv_ref[...] = nv
        d_ref[...] = -ADAM_LR * ((nm / c1) / (jnp.sqrt(nv / c2) + ADAM_EPS) + ADAM_WD * w_ref[...])

    zeros = (0,) * len(lead)
    spec = pl.BlockSpec((1,) * len(lead) + (tr, c), lambda i: zeros + (i, 0))
    sd = jax.ShapeDtypeStruct(w.shape, F32)
    return pl.pallas_call(
        body, name=name, grid=(r // tr,),
        in_specs=[spec] * 4, out_specs=[spec] * 3, out_shape=[sd] * 3,
    )(w, g, m, v)


def _adamw_in(w_t, m_t, v_t, own_half, swapped, core):
    r, c = SHARD_SHAPES[0]
    tr = max(t for t in range(8, 257, 8) if r % t == 0)
    c1 = 1.0 - ADAM_B1 ** ADAM_STEP
    c2 = 1.0 - ADAM_B2 ** ADAM_STEP

    def body(core_ref, w_ref, m_ref, v_ref, own_ref, sw_ref, d_ref, nm_ref, nv_ref, g_ref):
        own = own_ref[...]
        col_half = lax.broadcasted_iota(jnp.int32, (tr, c), 1) // (c // 2)
        gv = jnp.where(col_half == core_ref[0], jnp.concatenate([own, own], axis=1), sw_ref[...])
        g_ref[0] = gv
        nm = ADAM_B1 * m_ref[0] + (1.0 - ADAM_B1) * gv
        nv = ADAM_B2 * v_ref[0] + (1.0 - ADAM_B2) * (gv * gv)
        nm_ref[0] = nm
        nv_ref[0] = nv
        d_ref[0] = -ADAM_LR * ((nm / c1) / (jnp.sqrt(nv / c2) + ADAM_EPS) + ADAM_WD * w_ref[0])

    full = pl.BlockSpec((1, tr, c), lambda i, core_ref: (0, i, 0))
    sd = jax.ShapeDtypeStruct((1, r, c), F32)
    return pl.pallas_call(
        body, name="adamw_in",
        grid_spec=pltpu.PrefetchScalarGridSpec(
            num_scalar_prefetch=1, grid=(r // tr,),
            in_specs=[full, full, full, pl.BlockSpec((tr, c // 2), lambda i, core_ref: (i, 0)),
                      pl.BlockSpec((tr, c), lambda i, core_ref: (i, 0))],
            out_specs=[full] * 4),
        out_shape=[sd] * 4,
    )(core.reshape(1), w_t, m_t, v_t, own_half, swapped)


ANY = pl.BlockSpec(memory_space=pl.ANY)


def _my_place():
    return lax.axis_index("x"), lax.axis_index("y"), lax.axis_index("c")


HBM = pl.BlockSpec(memory_space=pltpu.HBM)
SEM = pl.BlockSpec(memory_space=pltpu.SEMAPHORE)
DATAFLOW = pltpu.SideEffectType.DATAFLOW_SIDE_EFFECTING


def _near_chips(x, y):
    return [(1 - x, y), (x, 1 - y)]


def _half(mi, hc):
    r, c = SHARD_SHAPES[mi]
    if mi == 0:
        return pl.ds(0, r), pl.ds(pl.multiple_of(hc * (c // 2), 128), c // 2)
    return pl.ds(pl.multiple_of(hc * (r // 2), 16), r // 2), pl.ds(0, c)


def _gather_copies(m_refs, land_refs, send_sems, recv_sems):
    x, y, c = _my_place()
    out, back = [], []
    for mi in range(N_MATS):
        rows, cols = _half(mi, c)
        for j, (cx, cy) in enumerate(_near_chips(x, y)):
            sems = dict(send_sem=send_sems.at[mi * 2 + j], recv_sem=recv_sems.at[mi * 2 + j],
                        device_id=(cx, cy, c), device_id_type=MESH)
            out.append(pltpu.make_async_remote_copy(src_ref=m_refs[mi].at[rows, cols],
                                                    dst_ref=land_refs[mi].at[2 * x + y, rows, cols], **sems))
            got = land_refs[mi].at[2 * cx + cy, rows, cols]
            back.append(pltpu.make_async_remote_copy(src_ref=got, dst_ref=got, **sems))
    return out, back


def _gather_start(mats, landing):
    n = N_MATS

    def body(*refs):
        out, _ = _gather_copies(refs[:n], refs[n:2 * n], refs[2 * n], refs[2 * n + 1])
        for cp in out:
            cp.start()
        refs[-1][...] = jnp.zeros_like(refs[-1])

    hbm = [pltpu.HBM(a.shape, a.dtype) for a in list(mats) + list(landing)]
    outs = pl.pallas_call(
        body, name="gather_start",
        out_shape=(pltpu.SemaphoreType.DMA((2 * n,)), pltpu.SemaphoreType.DMA((2 * n,)), *hbm,
                   jax.ShapeDtypeStruct((8, 128), F32)),
        in_specs=[HBM] * (2 * n), out_specs=(SEM, SEM, *[HBM] * (2 * n), pl.BlockSpec(memory_space=pltpu.VMEM)),
        input_output_aliases={i: 2 + i for i in range(2 * n)},
        compiler_params=pltpu.CompilerParams(has_side_effects=DATAFLOW),
    )(*[pltpu.with_memory_space_constraint(a, pltpu.HBM) for a in list(mats) + list(landing)])
    return outs[:-1], outs[-1]


def _gather_wait(handle, after):
    n = N_MATS

    def body(*refs):
        out, back = _gather_copies(refs[:n], refs[n:2 * n], refs[2 * n], refs[2 * n + 1])
        for cp, arrival in zip(out, back):
            cp.wait_send()
            arrival.wait_recv()

    bufs = handle[2:]
    after, after_specs = _after(after)
    res = pl.pallas_call(
        body, name="gather_wait", out_shape=tuple(pltpu.HBM(b.shape, b.dtype) for b in bufs),
        in_specs=[HBM] * (2 * n) + [SEM, SEM] + after_specs, out_specs=tuple([HBM] * (2 * n)),
        input_output_aliases={i: i for i in range(2 * n)},
        compiler_params=pltpu.CompilerParams(has_side_effects=DATAFLOW),
    )(*bufs, handle[0], handle[1], *after)
    return list(res[n:])


def _relay_share(gathered):
    n = N_MATS

    def body(*refs):
        out_refs = refs[n:2 * n]
        send_sems, recv_sems = refs[2 * n:]
        x, y, c = _my_place()
        sibling = (x, y, 1 - c)
        relayed = 2 * (x ^ (1 - c)) + (y ^ c)
        relay_to = (x ^ c, y ^ (1 - c), c)
        far = 2 * (1 - x) + (1 - y)
        near = [2 * (1 - x) + y, 2 * x + (1 - y)]

        def copy(k, mi, shard, hc, to):
            blk = out_refs[mi].at[(shard,) + _half(mi, hc)]
            return pltpu.make_async_remote_copy(src_ref=blk, dst_ref=blk, send_sem=send_sems.at[mi * 4 + k],
                                                recv_sem=recv_sems.at[mi * 4 + k], device_id=to, device_id_type=MESH)

        sends = []
        for mi in range(n):
            sends.append(copy(0, mi, relayed, c, relay_to))
            sends += [copy(1 + j, mi, near[j], c, sibling) for j in range(2)]
        for cp in sends:
            cp.start()
        for mi in range(n):
            copy(0, mi, far, c, relay_to).wait_recv()
            cp = copy(3, mi, far, c, sibling)
            cp.start()
            sends.append(cp)
        for mi in range(n):
            for j in range(2):
                copy(1 + j, mi, near[j], 1 - c, sibling).wait_recv()
            copy(3, mi, far, 1 - c, sibling).wait_recv()
        for cp in sends:
            cp.wait_send()

    return pl.pallas_call(
        body, name="relay_share",
        in_specs=[ANY] * n, out_specs=[ANY] * n,
        out_shape=[jax.ShapeDtypeStruct(g.shape, g.dtype) for g in gathered],
        input_output_aliases={i: i for i in range(n)},
        scratch_shapes=[pltpu.SemaphoreType.DMA((4 * n,)), pltpu.SemaphoreType.DMA((4 * n,))],
    )(*gathered)


def _peers(x, y, c):
    out = []
    for k in range(1, 8):
        px, py, pc = x ^ (k >> 2), y ^ ((k >> 1) & 1), c ^ (k & 1)
        out.append((k - 1, (px, py, pc), 4 * px + 2 * py + pc))
    return out


def _exchange_start(parts, name):
    n = len(parts)

    def body(*refs):
        p_refs, land_refs = refs[:n], refs[n:2 * n]
        send_sems, recv_sems, token = refs[2 * n], refs[2 * n + 1], refs[-1]
        x, y, c = _my_place()
        me = 4 * x + 2 * y + c
        for k, dev, peer in _peers(x, y, c):
            for mi in range(n):
                pltpu.make_async_remote_copy(
                    src_ref=p_refs[mi].at[peer], dst_ref=land_refs[mi].at[me], send_sem=send_sems.at[k * n + mi],
                    recv_sem=recv_sems.at[k * n + mi], device_id=dev, device_id_type=MESH).start()
        token[...] = jnp.zeros_like(token)

    hbm = [pltpu.HBM(p.shape, p.dtype) for p in parts]
    outs = pl.pallas_call(
        body, name=name + "_start",
        out_shape=(pltpu.SemaphoreType.DMA((7 * n,)), pltpu.SemaphoreType.DMA((7 * n,)), *hbm, *hbm,
                   jax.ShapeDtypeStruct((8, 128), F32)),
        in_specs=[HBM] * (2 * n), out_specs=(SEM, SEM, *[HBM] * (2 * n), pl.BlockSpec(memory_space=pltpu.VMEM)),
        input_output_aliases={i: 2 + i for i in range(2 * n)},
        compiler_params=pltpu.CompilerParams(has_side_effects=DATAFLOW),
    )(*[pltpu.with_memory_space_constraint(p, pltpu.HBM) for p in parts],
      *[pltpu.with_memory_space_constraint(lax.empty(p.shape, p.dtype), pltpu.HBM) for p in parts])
    return (name, outs[:-1]), outs[-1]


def _exchange_wait(handle, after):
    name, outs = handle
    n = (len(outs) - 2) // 2

    def body(*refs):
        p_refs, land_refs = refs[:n], refs[n:2 * n]
        send_sems, recv_sems = refs[2 * n], refs[2 * n + 1]
        x, y, c = _my_place()
        me = 4 * x + 2 * y + c
        for k, dev, peer in _peers(x, y, c):
            for mi in range(n):
                pltpu.make_async_remote_copy(
                    src_ref=p_refs[mi].at[peer], dst_ref=land_refs[mi].at[me], send_sem=send_sems.at[k * n + mi],
                    recv_sem=recv_sems.at[k * n + mi], device_id=dev, device_id_type=MESH).wait_send()
                slot = land_refs[mi].at[peer]
                pltpu.make_async_remote_copy(
                    src_ref=slot, dst_ref=slot, send_sem=send_sems.at[k * n + mi],
                    recv_sem=recv_sems.at[k * n + mi], device_id=dev, device_id_type=MESH).wait_recv()

    bufs = outs[2:]
    res = pl.pallas_call(
        body, name=name + "_wait", out_shape=tuple(pltpu.HBM(b.shape, b.dtype) for b in bufs),
        in_specs=[HBM] * (2 * n) + [SEM, SEM, ANY], out_specs=tuple([HBM] * (2 * n)),
        input_output_aliases={i: i for i in range(2 * n)},
        compiler_params=pltpu.CompilerParams(has_side_effects=DATAFLOW),
    )(*bufs, outs[0], outs[1], after)
    return list(res[n:])


def _swap_halves(halves, gvec):
    def place(ref, mi, hc):
        return ref.at[:, pl.ds(pl.multiple_of(hc * 512, 128), 512)] if mi == 0 else ref.at[hc]

    def body(*refs):
        g_refs, gv_ref = refs[:N_MATS], refs[N_MATS]
        out_refs, rg_ref = refs[N_MATS + 1:2 * N_MATS + 1], refs[2 * N_MATS + 1]
        send_sems, recv_sems = refs[2 * N_MATS + 2:]
        x, y, c = _my_place()
        me = 4 * x + 2 * y + c
        sends = []
        for mi in range(N_MATS):
            cp = pltpu.make_async_remote_copy(src_ref=g_refs[mi], dst_ref=place(out_refs[mi], mi, c), send_sem=send_sems.at[mi],
                                              recv_sem=recv_sems.at[mi], device_id=(x, y, 1 - c), device_id_type=MESH)
            cp.start()
            sends.append(cp)
        for k, dev, peer in _peers(x, y, c):
            cp = pltpu.make_async_remote_copy(src_ref=gv_ref, dst_ref=rg_ref.at[me], send_sem=send_sems.at[N_MATS + k],
                                              recv_sem=recv_sems.at[N_MATS + k], device_id=dev, device_id_type=MESH)
            cp.start()
            sends.append(cp)
        for mi in range(N_MATS):
            got = place(out_refs[mi], mi, 1 - c)
            pltpu.make_async_remote_copy(src_ref=got, dst_ref=got, send_sem=send_sems.at[mi], recv_sem=recv_sems.at[mi],
                                         device_id=(x, y, 1 - c), device_id_type=MESH).wait_recv()
        for k, dev, peer in _peers(x, y, c):
            got = rg_ref.at[peer]
            pltpu.make_async_remote_copy(src_ref=got, dst_ref=got, send_sem=send_sems.at[N_MATS + k],
                                         recv_sem=recv_sems.at[N_MATS + k], device_id=dev, device_id_type=MESH).wait_recv()
        for cp in sends:
            cp.wait_send()

    outs = pl.pallas_call(
        body, name="swap_halves",
        in_specs=[ANY] * (N_MATS + 1), out_specs=[ANY] * (N_MATS + 1),
        out_shape=[jax.ShapeDtypeStruct(SHARD_SHAPES[0], F32)]
        + [jax.ShapeDtypeStruct((2, r // 2, c), F32) for r, c in SHARD_SHAPES[1:]]
        + [jax.ShapeDtypeStruct((8, 8, N_GVEC), F32)],
        scratch_shapes=[pltpu.SemaphoreType.DMA((N_MATS + 7,)), pltpu.SemaphoreType.DMA((N_MATS + 7,))],
    )(*halves, gvec)
    return outs[:N_MATS], outs[N_MATS]


def _set_slot(arr, block, idx):
    return lax.dynamic_update_slice(arr, block[None], (idx,) + (0,) * block.ndim)


PAD_RUNS = ((6304, 8352, 0), (5280, 6304, COL_Z), (672, 5280, COL_QKV), (0, 640, COL_LAT), (640, 672, COL_LAT + 704))
N_QKV = COL_LAT - COL_QKV


def _qkv_rows_regroup(a, to_padded):
    if to_padded:
        a4 = a.reshape(3, 12, 128, a.shape[1])
        return jnp.stack([a4[0], a4[1], a4[2]], axis=1).reshape(a.shape)
    a4 = a.reshape(12, 3, 128, a.shape[1])
    return jnp.concatenate([a4[:, tq].reshape(N_QKV // 3, a.shape[1]) for tq in range(3)], axis=0)
W_IN_SHARD = 2088


def _full_weights(gathered):
    def cols(a):
        return jnp.concatenate([a[s] for s in range(4)], axis=1)

    w_uq, w_ukv, w_pm, w_pd = [cols(a) for a in gathered[1:5]]
    w_out = gathered[5].reshape(D_MODEL, D_MODEL)
    w_in_t = gathered[0].reshape(4 * W_IN_SHARD, D_MODEL)
    pieces, at = [], 0
    for lo, hi, pad_lo in sorted(PAD_RUNS, key=lambda t: t[2]):
        if pad_lo > at:
            pieces.append(jnp.zeros((pad_lo - at, D_MODEL), w_in_t.dtype))
        pieces.append(_qkv_rows_regroup(w_in_t[lo:hi], True) if pad_lo == COL_QKV else w_in_t[lo:hi])
        at = pad_lo + hi - lo
    pieces.append(jnp.zeros((N_PAD - at, D_MODEL), w_in_t.dtype))
    w_pad_t = jnp.concatenate(pieces, axis=0)
    z32 = jnp.zeros((Q_RANK, 32), w_uq.dtype)
    wuq_pad = jnp.concatenate([t for h in range(MLA_HEADS) for t in (w_uq[:, h * 96:(h + 1) * 96], z32)], axis=1)
    z64 = jnp.zeros((KV_RANK, 64), w_ukv.dtype)
    wk_pad = jnp.concatenate([t for h in range(MLA_HEADS) for t in (w_ukv[:, h * 128:h * 128 + 64], z64)], axis=1)
    wv = jnp.concatenate([w_ukv[:, h * 128 + 64:(h + 1) * 128] for h in range(MLA_HEADS)], axis=1)
    return w_pad_t.T, w_pad_t, wuq_pad, wk_pad, wv, w_pm, w_pd, w_out


W_IN_LAT = 672


def _grad_parts_in_early(dwt_early):
    dwt_early = jnp.concatenate([dwt_early[:COL_QKV], _qkv_rows_regroup(dwt_early[COL_QKV:COL_LAT], False)], axis=0)

    def in_block(s, h):
        cols = slice(h * 512, (h + 1) * 512)
        out = []
        for lo, hi, pad_lo in sorted(PAD_RUNS):
            a_, b_ = max(lo, s * W_IN_SHARD), min(hi, (s + 1) * W_IN_SHARD)
            if a_ < b_:
                out.append(jnp.zeros((b_ - a_, 512), dwt_early.dtype) if pad_lo >= COL_LAT
                           else dwt_early[pad_lo + a_ - lo:pad_lo + b_ - lo, cols])
        return jnp.concatenate(out, axis=0)

    return jnp.stack([in_block(s, h) for s in range(4) for h in range(2)])


def _grad_parts_in_late(dwt_late):
    rows = jnp.concatenate([dwt_late[0:640], dwt_late[704:736]], axis=0)
    zero = jnp.zeros((W_IN_LAT, 512), dwt_late.dtype)
    return jnp.stack([rows[:, 0:512], rows[:, 512:1024]] + [zero] * 6)


def _col_blocks(m):
    r, c = m.shape[0] // 2, m.shape[1] // 4
    return jnp.stack([m[h * r:(h + 1) * r, s * c:(s + 1) * c] for s in range(4) for h in range(2)])


def _grad_parts_mla(dwuq_pad, dwk_pad, dwv):
    d_uq = jnp.concatenate([dwuq_pad[:, h * 128:h * 128 + 96] for h in range(MLA_HEADS)], axis=1)
    d_ukv = jnp.concatenate([t for h in range(MLA_HEADS) for t in (dwk_pad[:, h * 128:h * 128 + 64], dwv[:, h * 64:(h + 1) * 64])],
                            axis=1)
    return [_col_blocks(d_uq.astype(BF16)), _col_blocks(d_ukv.astype(BF16))]


def _rope_tables(positions, token=None):
    pos = positions.reshape(SEQ).astype(F32)
    if token is not None:
        pos = pos + token[0, 0]
    lane = jnp.arange(128)

    def table(rot, first, period):
        inv = ROPE_THETA ** (-jnp.arange(0, rot, 2, dtype=F32) / rot)
        half = rot // 2
        off = lane % period - first
        in1, in2 = (off >= 0) & (off < half), (off >= half) & (off < rot)
        inv_lane = jnp.where(in1 | in2, inv[jnp.clip(off % half, 0, half - 1)], 0.0)
        sign = jnp.where(in1, -1.0, 1.0).astype(F32)
        ang = pos[:, None] * inv_lane[None, :]
        return jnp.cos(ang), jnp.sin(ang) * sign[None, :]

    return table(32, 64, 128), table(16, 0, 64)


class _Links:
    def __init__(self, mats, chip, me):
        landing = [_set_slot(lax.empty((4,) + m.shape, m.dtype), m, chip) for m in mats]
        self.gather, self.token = _gather_start(mats, landing)
        self.me, self.sent, self.handles, self.sums = me, {}, {}, {}

    def weights(self, after):
        return _relay_share(_gather_wait(self.gather, after))

    def send(self, blocks, name):
        self.sent[name] = blocks
        self.handles[name], token = _exchange_start(blocks, name)
        return token

    def collect(self, name, after, parts):
        recv = _exchange_wait(self.handles[name], after)
        for r, own, part in zip(recv, self.sent[name], parts):
            self.sums[part] = _sum_parts(r, own, self.me, 64, "sum_grad_" + part)
        return tuple(self.sums[part] for part in parts)


def _device_grads(x, positions, target, gains, links):
    pre_g, q_g, kv_g, post_g = gains
    (mc, ms), (dc, ds) = _rope_tables(positions, links.token)
    h = _prenorm_fwd(x, pre_g, links.token)
    w_pad, w_pad_t, wuq_pad, wk_pad, wv, w_pm, w_pd, w_out = _full_weights(links.weights((h, mc, ms, dc, ds)))

    p = _matmul(h, w_pad, "nn", F32, 1024, 1408, 1024, "in_proj")
    q, k, v = _mla_prep_fwd(p, q_g, kv_g, wuq_pad, wk_pad, wv, mc, ms)
    ya, lse_m = _mla_flash_fwd(q, k, v)
    qkv = [_dil_prep_fwd(p, dc, ds, g) for g in range(3)]
    o_g, l_g = zip(*[_dil_attn_fwd(qkv[g], g) for g in range(3)])
    (dp, dy, mg, dt, ua, dpa, ud, dpd, dya, dyd, yd, lse_d, loss_cols, dg_post) = _tail(
        p, ya, o_g, l_g, x, target, w_pm, w_pd, w_out, post_g)

    for g in range(3):
        dp = _dil_attn_bwd(dp, qkv[g], dyd, yd, lse_d, dc, ds, g)
    dw_early = _matmul(dp, h, "tn", BF16, 1536, 1024, 2048, "dw_in_early", a_cols=(0, COL_LAT // 1536))
    dwpm = _matmul(ua, dpa, "tn", BF16, 512, 1024, 2048, "dw_proj_mla")
    dwpd = _matmul(ud, dpd, "tn", BF16, 512, 1024, 2048, "dw_proj_dil")
    dwout = _matmul(mg, dt, "tn", BF16, 1024, 1024, 2048, "dw_out")
    token = links.send([_grad_parts_in_early(dw_early), _col_blocks(dwpm), _col_blocks(dwpd),
                        dwout.reshape(8, 128, D_MODEL)], "exchange_early")

    dq, dk, dv = _mla_flash_bwd(q, k, v, ya, dya, lse_m, token)
    dp, dwuq_pad, dwk_pad, dwv, dg_q, dg_kv = _mla_prep_bwd(dp, p, dq, dk, dv, q_g, kv_g, wuq_pad, wk_pad, wv, mc, ms)
    dw_late = _matmul(dp, h, "tn", BF16, N_LAT, 1024, 2048, "dw_in_late", a_cols=(COL_LAT // N_LAT, 1))
    token = links.send([_grad_parts_in_late(dw_late)] + _grad_parts_mla(dwuq_pad, dwk_pad, dwv), "exchange_late")
    early = links.collect("exchange_early", dw_late, ("in_early", "pm", "pd", "out"))

    grad_x, dg_pre = _dh_prenorm_bwd(dp, w_pad_t, x, dy, pre_g, (token,) + tuple(early))
    links.collect("exchange_late", grad_x, ("in_late", "uq", "ukv"))

    loss_part = jnp.pad((jnp.sum(loss_cols) * (0.5 / D_MODEL)).reshape(1, 1), ((0, 0), (0, N_GVEC - N_GAINS - 1)))
    gvec = jnp.concatenate([dg_pre, dg_q, dg_kv, dg_post, loss_part], axis=1)
    return grad_x, gvec


def kernel(x, positions, pre_norm_g, w_in, q_norm_g, w_uq, kv_norm_g, w_ukv, w_proj_mla, w_proj_dil, w_out, post_norm_g, loss_target, m_pre_norm_g, m_w_in, m_q_norm_g, m_w_uq, m_kv_norm_g, m_w_ukv, m_w_proj_mla, m_w_proj_dil, m_w_out, m_post_norm_g, v_pre_norm_g, v_w_in, v_q_norm_g, v_w_uq, v_kv_norm_g, v_w_ukv, v_w_proj_mla, v_w_proj_dil, v_w_out, v_post_norm_g):
    xi, yi, ci = _my_place()
    chip, me = 2 * xi + yi, 4 * xi + 2 * yi + ci
    mats = [jnp.swapaxes(w_in, 1, 2)] + [w_uq, w_ukv, w_proj_mla, w_proj_dil, w_out]
    mats = [w.reshape(w.shape[1:]).astype(BF16) for w in mats]
    links = _Links(mats, chip, me)
    gains = (pre_norm_g, q_norm_g, kv_norm_g, post_norm_g)
    grad_x, gvec = _device_grads(x[0], positions, loss_target[0], gains, links)

    sums = links.sums
    in_e = sums["in_early"]
    half_in = jnp.concatenate([in_e[:W_IN_LAT] + jnp.where(chip == 0, sums["in_late"], 0.0), in_e[W_IN_LAT:]], axis=0)
    halves = [half_in, sums["uq"], sums["ukv"], sums["pm"], sums["pd"], sums["out"]]
    gvec8 = jnp.pad(gvec, ((0, 7), (0, 0)))
    swapped, recv_gains = _swap_halves(halves, gvec8)
    g_gains = _sum_parts(recv_gains, gvec8, me, 8, "sum_gain_parts")[0:1]
    loss = g_gains[0, N_GAINS]
    sw = lambda a: jnp.swapaxes(a, 1, 2)
    d_in, m_in, v_in, g_in = [sw(o) for o in _adamw_in(sw(w_in), sw(m_w_in), sw(v_w_in), half_in, swapped[0], ci)]
    g_mats = [g_in] + [_set_slot(s, hf, ci).reshape((1,) + shp)
                       for s, hf, shp in zip(swapped[1:], halves[1:], SHARD_SHAPES[1:])]

    off = [0, 1024, 1408, 1664, 2688]
    g_gain = [g_gains[:, off[i]:off[i + 1]] for i in range(4)]
    grads = [g_gain[0], g_mats[0], g_gain[1], g_mats[1], g_gain[2], g_mats[2], g_mats[3], g_mats[4], g_mats[5], g_gain[3]]
    ws = [pre_norm_g, w_in, q_norm_g, w_uq, kv_norm_g, w_ukv, w_proj_mla, w_proj_dil, w_out, post_norm_g]
    ms = [m_pre_norm_g, m_w_in, m_q_norm_g, m_w_uq, m_kv_norm_g, m_w_ukv, m_w_proj_mla, m_w_proj_dil, m_w_out, m_post_norm_g]
    vs = [v_pre_norm_g, v_w_in, v_q_norm_g, v_w_uq, v_kv_norm_g, v_w_ukv, v_w_proj_mla, v_w_proj_dil, v_w_out, v_post_norm_g]
    deltas, new_m, new_v = [], [], []
    for i, (w, g, m, v) in enumerate(zip(ws, grads, ms, vs)):
        if w is w_in:
            d_, m_, v_ = d_in, m_in, v_in
        elif w.shape[-1] % 128 and w.shape[-2] % 128 == 0:
            g = jnp.swapaxes(g, 1, 2)
            grads[i] = jnp.swapaxes(g, 1, 2)
            d_, m_, v_ = [jnp.swapaxes(o, 1, 2) for o in
                          _adamw(jnp.swapaxes(w, 1, 2), g, jnp.swapaxes(m, 1, 2), jnp.swapaxes(v, 1, 2), f"adamw_{i}")]
        else:
            d_, m_, v_ = _adamw(w, g, m, v, f"adamw_{i}")
        deltas.append(d_)
        new_m.append(m_)
        new_v.append(v_)
    return (loss, grad_x.reshape(x.shape), *grads, *deltas, *new_m, *new_v)
```

```python
import jax
import jax.numpy as jnp
from jax import lax
from jax.experimental import pallas as pl
from jax.experimental.pallas import tpu as pltpu

F32 = jnp.float32
BF16 = jnp.bfloat16

SEQ = 4096
D_MODEL = 1024
EPS = 1e-6
ROPE_THETA = 500000.0
MLA_HEADS = 8
Q_RANK = 384
KV_RANK = 256
MLA_SCALE = 96.0 ** -0.5
MLA_ROPE_HALF = 16
DIL_DILATIONS = (1, 4, 16)
DIL_ROPE_HALF = 8
DIL_SCALE = 0.125
BAND = 128

N_LAT = 768
COL_Z, COL_QKV, COL_LAT = 2048, 3072, 7680
N_PAD = 8448


def _qkv_block(tq, g, pr):
    return COL_QKV // 128 + (g * 4 + pr) * 3 + tq

IN_SPLITS = (384, 256, 32, 4608, 512, 512, 1024, 1024)

SHARD_SHAPES = ((2088, 1024), (384, 192), (256, 256), (512, 256), (512, 256), (256, 1024))
N_MATS = len(SHARD_SHAPES)
N_GAINS = 2688
N_GVEC = N_GAINS + 128

ADAM_LR, ADAM_B1, ADAM_B2, ADAM_EPS, ADAM_WD, ADAM_STEP = 0.001, 0.9, 0.999, 1e-08, 0.01, 10

VMEM_LIMIT = 56 * 1024 * 1024
NEG = -1e30
MESH = pl.DeviceIdType.MESH


def _cparams(**kw):
    return pltpu.CompilerParams(vmem_limit_bytes=VMEM_LIMIT, **kw)


def _dot(a, b, dims):
    return lax.dot_general(a, b, (dims, ((), ())), preferred_element_type=F32)


def _nn(a, b):
    return _dot(a, b, ((1,), (0,)))


def _nt(a, b):
    return _dot(a, b, ((1,), (1,)))


def _tn(a, b):
    return _dot(a, b, ((0,), (0,)))


def _rope_lanes(shape, half, period, first):
    lane = lax.broadcasted_iota(jnp.int32, shape, len(shape) - 1) % period
    return (lane >= first) & (lane < first + half), (lane >= first + half) & (lane < first + 2 * half)


def _rope_fwd(x, c, s, half, lanes):
    x1, _ = lanes
    return x * c + jnp.where(x1, pltpu.roll(x, 128 - half, 1), pltpu.roll(x, half, 1)) * s


def _rope_bwd(g, c, s, half, lanes):
    x1, x2 = lanes
    gs = g * s
    return g * c + jnp.where(x2, pltpu.roll(gs, half, 1), jnp.where(x1, pltpu.roll(gs, 128 - half, 1), 0.0))


def _sigmoid(x):
    return 1.0 / (1.0 + jnp.exp(-x))


def _after(token):
    tokens = [t for t in (token if isinstance(token, (tuple, list)) else [token]) if t is not None]
    return tokens, [pl.BlockSpec(memory_space=pl.ANY)] * len(tokens)


def _matmul(a, b, mode, out_dtype, tm, tn, tk, name, token=None, b_cols=None, a_cols=None):
    after, after_specs = _after(token)
    if mode == "nn":
        (m, k), n = a.shape, b.shape[1]
        first = 0
        if b_cols is not None:
            first, n = b_cols[0], b_cols[1] * tn
        a_spec = pl.BlockSpec((tm, tk), lambda j, i, kk: (i, kk))
        b_spec = pl.BlockSpec((tk, tn), lambda j, i, kk: (kk, j + first))
        dot = _nn
    elif mode == "nt":
        (m, k), n = a.shape, b.shape[0]
        a_spec = pl.BlockSpec((tm, tk), lambda j, i, kk: (i, kk))
        b_spec = pl.BlockSpec((tn, tk), lambda j, i, kk: (j, kk))
        dot = _nt
    else:
        (k, m), n = a.shape, b.shape[1]
        first = 0
        if a_cols is not None:
            first, m = a_cols[0], a_cols[1] * tm
        a_spec = pl.BlockSpec((tk, tm), lambda j, i, kk: (kk, i + first))
        b_spec = pl.BlockSpec((tk, tn), lambda j, i, kk: (kk, j))
        dot = _tn
    assert m % tm == 0 and n % tn == 0 and k % tk == 0, (name, m, n, k, tm, tn, tk)
    nk = k // tk

    def body(a_ref, b_ref, *rest):
        o_ref, acc_ref = rest[-2:]
        kk = pl.program_id(2)
        part = dot(a_ref[...], b_ref[...])

        @pl.when(kk == 0)
        def _():
            acc_ref[...] = part

        @pl.when(kk > 0)
        def _():
            acc_ref[...] += part

        @pl.when(kk == nk - 1)
        def _():
            o_ref[...] = acc_ref[...].astype(o_ref.dtype)

    return pl.pallas_call(
        body, name=name, grid=(n // tn, m // tm, nk),
        in_specs=[a_spec, b_spec] + after_specs,
        out_specs=pl.BlockSpec((tm, tn), lambda j, i, kk: (i, j)),
        out_shape=jax.ShapeDtypeStruct((m, n), out_dtype),
        scratch_shapes=[pltpu.VMEM((tm, tn), F32)],
        compiler_params=_cparams(),
    )(a, b, *after)


def _prenorm_fwd(x, g, token=None):
    tm = 512
    after, after_specs = _after(token)

    def body(x_ref, g_ref, *rest):
        xv = x_ref[...]
        r = lax.rsqrt(jnp.mean(xv * xv, axis=-1, keepdims=True) + EPS)
        rest[-1][...] = (xv * r * g_ref[...]).astype(BF16)

    return pl.pallas_call(
        body, name="prenorm_fwd", grid=(SEQ // tm,),
        in_specs=[pl.BlockSpec((tm, D_MODEL), lambda i: (i, 0)), pl.BlockSpec((1, D_MODEL), lambda i: (0, 0))] + after_specs,
        out_specs=pl.BlockSpec((tm, D_MODEL), lambda i: (i, 0)),
        out_shape=jax.ShapeDtypeStruct((SEQ, D_MODEL), BF16),
    )(x, g, *after)


def _dh_prenorm_bwd(dp, w_pad_t, x, dy, g, token=None):
    tm, tk = 1024, 1408
    nk = N_PAD // tk
    after, after_specs = _after(token)

    def body(a_ref, b_ref, x_ref, dy_ref, g_ref, *rest):
        gx_ref, dg_ref, acc_ref = rest[-3:]
        i, kk = pl.program_id(0), pl.program_id(1)
        part = _nn(a_ref[...], b_ref[...])

        @pl.when(kk == 0)
        def _():
            acc_ref[...] = part

        @pl.when(kk > 0)
        def _():
            acc_ref[...] += part

        @pl.when(kk == nk - 1)
        def _():
            xv = x_ref[...]
            r = lax.rsqrt(jnp.mean(xv * xv, axis=-1, keepdims=True) + EPS)
            n = xv * r
            dhv = acc_ref[...]
            dn = dhv * g_ref[...]
            gx_ref[...] = dy_ref[...] + r * (dn - n * jnp.mean(dn * n, axis=-1, keepdims=True))
            cols = jnp.sum(dhv * n, axis=0, keepdims=True)

            @pl.when(i == 0)
            def _():
                dg_ref[...] = cols

            @pl.when(i > 0)
            def _():
                dg_ref[...] += cols

    row = pl.BlockSpec((tm, D_MODEL), lambda i, kk: (i, 0))
    vec = pl.BlockSpec((1, D_MODEL), lambda i, kk: (0, 0))
    return pl.pallas_call(
        body, name="dh_prenorm_bwd", grid=(SEQ // tm, nk),
        in_specs=[pl.BlockSpec((tm, tk), lambda i, kk: (i, kk)), pl.BlockSpec((tk, D_MODEL), lambda i, kk: (kk, 0)),
                  row, row, vec] + after_specs,
        out_specs=[row, vec],
        out_shape=[jax.ShapeDtypeStruct((SEQ, D_MODEL), F32), jax.ShapeDtypeStruct((1, D_MODEL), F32)],
        scratch_shapes=[pltpu.VMEM((tm, D_MODEL), F32)],
        compiler_params=_cparams(),
    )(dp, w_pad_t, x, dy, g, *after)


def _mla_prep_fwd(p, qg, kvg, wuq, wk, wv, rc, rs):
    tm = 512

    def body(lat_ref, qg_ref, kvg_ref, wuq_ref, wk_ref, wv_ref, c_ref, s_ref, q_ref, k_ref, v_ref):
        c, s = c_ref[...], s_ref[...]
        lanes = _rope_lanes((tm, 128), MLA_ROPE_HALF, 128, 64)
        cq = lat_ref[:, 0:Q_RANK]
        r1 = lax.rsqrt(jnp.mean(cq * cq, axis=-1, keepdims=True) + EPS)
        cqn = (cq * r1 * qg_ref[...]).astype(BF16)
        q = _nn(cqn, wuq_ref[...])
        for h in range(MLA_HEADS):
            sl = slice(h * 128, (h + 1) * 128)
            q_ref[:, sl] = (_rope_fwd(q[:, sl], c, s, MLA_ROPE_HALF, lanes) * MLA_SCALE).astype(BF16)
        ckv = lat_ref[:, Q_RANK:Q_RANK + KV_RANK]
        r2 = lax.rsqrt(jnp.mean(ckv * ckv, axis=-1, keepdims=True) + EPS)
        ckvn = (ckv * r2 * kvg_ref[...]).astype(BF16)
        krr = _rope_fwd(lat_ref[:, Q_RANK + KV_RANK:N_LAT], c, s, MLA_ROPE_HALF, lanes)
        kn = _nn(ckvn, wk_ref[...])
        for h in range(MLA_HEADS):
            sl = slice(h * 128, (h + 1) * 128)
            k_ref[:, sl] = (kn[:, sl] + krr).astype(BF16)
        v_ref[...] = _nn(ckvn, wv_ref[...]).astype(BF16)

    def full(shape):
        return pl.BlockSpec(shape, lambda i: (0, 0))

    def rows(w):
        return pl.BlockSpec((tm, w), lambda i: (i, 0))

    return pl.pallas_call(
        body, name="mla_prep_fwd", grid=(SEQ // tm,),
        in_specs=[pl.BlockSpec((tm, N_LAT), lambda i: (i, COL_LAT // N_LAT)),
                  full((1, Q_RANK)), full((1, KV_RANK)), full((Q_RANK, 1024)), full((KV_RANK, 1024)),
                  full((KV_RANK, 512)), rows(128), rows(128)],
        out_specs=[rows(1024), rows(1024), rows(512)],
        out_shape=[jax.ShapeDtypeStruct((SEQ, 1024), BF16), jax.ShapeDtypeStruct((SEQ, 1024), BF16),
                   jax.ShapeDtypeStruct((SEQ, 512), BF16)],
        compiler_params=_cparams(),
    )(p, qg, kvg, wuq, wk, wv, rc, rs)


def _mla_prep_bwd(dp_in, p, dq, dk, dv, qg, kvg, wuq, wk, wv, rc, rs):
    tm = 512

    def body(dp_any, lat_ref, dq_ref, dk_ref, dv_ref, qg_ref, kvg_ref, wuq_ref, wk_ref, wv_ref,
             c_ref, s_ref, dp_ref, dwuq_ref, dwk_ref, dwv_ref, dgq_ref, dgkv_ref, dqb_ref, dkb_ref):
        del dp_any
        c, s = c_ref[...], s_ref[...]
        lanes = _rope_lanes((tm, 128), MLA_ROPE_HALF, 128, 64)
        lane = lax.broadcasted_iota(jnp.int32, (tm, 128), 1)
        dkr = jnp.zeros((tm, 128), F32)
        for h in range(MLA_HEADS):
            sl = slice(h * 128, (h + 1) * 128)
            dqb_ref[:, sl] = _rope_bwd(dq_ref[:, sl] * MLA_SCALE, c, s, MLA_ROPE_HALF, lanes).astype(BF16)
            dkh = dk_ref[:, sl]
            dkr = dkr + dkh
            dkb_ref[:, sl] = jnp.where(lane < 64, dkh, 0.0).astype(BF16)
        dkr = jnp.where((lane >= 64) & (lane < 96), dkr, 0.0)
        dkr = _rope_bwd(dkr, c, s, MLA_ROPE_HALF, lanes)
        dvb = dv_ref[...].astype(BF16)

        cq = lat_ref[:, 0:Q_RANK]
        r1 = lax.rsqrt(jnp.mean(cq * cq, axis=-1, keepdims=True) + EPS)
        n1 = cq * r1
        dcqn = _nt(dqb_ref[...], wuq_ref[...])
        dn1 = dcqn * qg_ref[...]
        dcq = r1 * (dn1 - n1 * jnp.mean(dn1 * n1, axis=-1, keepdims=True))
        pq = jnp.sum(dcqn * n1, axis=0, keepdims=True)

        ckv = lat_ref[:, Q_RANK:Q_RANK + KV_RANK]
        r2 = lax.rsqrt(jnp.mean(ckv * ckv, axis=-1, keepdims=True) + EPS)
        n2 = ckv * r2
        dckvn = _nt(dkb_ref[...], wk_ref[...]) + _nt(dvb, wv_ref[...])
        dn2 = dckvn * kvg_ref[...]
        dckv = r2 * (dn2 - n2 * jnp.mean(dn2 * n2, axis=-1, keepdims=True))
        pkv = jnp.sum(dckvn * n2, axis=0, keepdims=True)
        cqn = (n1 * qg_ref[...]).astype(BF16)
        ckvn = (n2 * kvg_ref[...]).astype(BF16)
        wq, wk_, wv_ = _tn(cqn, dqb_ref[...]), _tn(ckvn, dkb_ref[...]), _tn(ckvn, dvb)

        dp_ref[:, 0:Q_RANK] = dcq.astype(BF16)
        dp_ref[:, Q_RANK:Q_RANK + KV_RANK] = dckv.astype(BF16)
        dp_ref[:, Q_RANK + KV_RANK:N_LAT] = dkr.astype(BF16)

        @pl.when(pl.program_id(0) == 0)
        def _():
            dgq_ref[...] = pq
            dgkv_ref[...] = pkv
            dwuq_ref[...] = wq
            dwk_ref[...] = wk_
            dwv_ref[...] = wv_

        @pl.when(pl.program_id(0) > 0)
        def _():
            dgq_ref[...] += pq
            dgkv_ref[...] += pkv
            dwuq_ref[...] += wq
            dwk_ref[...] += wk_
            dwv_ref[...] += wv_

    def full(shape):
        return pl.BlockSpec(shape, lambda i: (0, 0))

    def rows(w):
        return pl.BlockSpec((tm, w), lambda i: (i, 0))

    lat = pl.BlockSpec((tm, N_LAT), lambda i: (i, COL_LAT // N_LAT))
    return pl.pallas_call(
        body, name="mla_prep_bwd", grid=(SEQ // tm,),
        in_specs=[pl.BlockSpec(memory_space=pl.ANY), lat, rows(1024), rows(1024), rows(512),
                  full((1, Q_RANK)), full((1, KV_RANK)), full((Q_RANK, 1024)), full((KV_RANK, 1024)),
                  full((KV_RANK, 512)), rows(128), rows(128)],
        out_specs=[lat, full((Q_RANK, 1024)), full((KV_RANK, 1024)), full((KV_RANK, 512)),
                   full((1, Q_RANK)), full((1, KV_RANK))],
        out_shape=[jax.ShapeDtypeStruct((SEQ, N_PAD), BF16), jax.ShapeDtypeStruct((Q_RANK, 1024), F32),
                   jax.ShapeDtypeStruct((KV_RANK, 1024), F32), jax.ShapeDtypeStruct((KV_RANK, 512), F32),
                   jax.ShapeDtypeStruct((1, Q_RANK), F32), jax.ShapeDtypeStruct((1, KV_RANK), F32)],
        input_output_aliases={0: 0},
        scratch_shapes=[pltpu.VMEM((tm, 1024), BF16), pltpu.VMEM((tm, 1024), BF16)],
        compiler_params=_cparams(),
    )(dp_in, p, dq, dk, dv, qg, kvg, wuq, wk, wv, rc, rs)


FLASH_T = 1024


def _head_half(shape, hh):
    lane = lax.broadcasted_iota(jnp.int32, shape, 1)
    return (lane < 64) if hh == 0 else (lane >= 64)


def _diag_keep(nr, nk):
    row = lax.broadcasted_iota(jnp.int32, (nr, nk), 0)
    col = lax.broadcasted_iota(jnp.int32, (nr, nk), 1)
    return row + (nk - nr) >= col


def _tri_steps(nb, q_major):
    if q_major:
        pairs = [(i, kb) for i in range(nb) for kb in range(i + 1)]
    else:
        pairs = [(i, kb) for kb in range(nb) for i in range(kb, nb)]
    return jnp.asarray([p[0] for p in pairs], jnp.int32), jnp.asarray([p[1] for p in pairs], jnp.int32)


def _mla_flash_fwd(q, k, v):
    t = FLASH_T
    nb = SEQ // t
    qtab, ktab = _tri_steps(nb, True)

    def body(qi_ref, ki_ref, q_ref, k_ref, v_ref, o_ref, lse_ref, m_scr, l_scr, acc_scr):
        step = pl.program_id(1)
        i, kb = qi_ref[step], ki_ref[step]

        @pl.when(kb == 0)
        def _():
            m_scr[...] = jnp.full_like(m_scr, NEG)
            l_scr[...] = jnp.zeros_like(l_scr)
            acc_scr[...] = jnp.zeros_like(acc_scr)

        def update(r0, nr, nk, diagonal):
            rs = slice(r0, r0 + nr)
            vv = v_ref[0:nk, :]
            for hh in range(2):
                sl = slice(hh * 128, (hh + 1) * 128)
                s = _nt(q_ref[rs, sl], k_ref[0:nk, sl])
                if diagonal:
                    s = jnp.where(_diag_keep(nr, nk), s, NEG)
                m_prev = m_scr[hh, rs, :]
                m_new = jnp.maximum(m_prev, jnp.max(s, axis=-1, keepdims=True))
                pr = jnp.exp(s - jnp.tile(m_new, (1, nk // 128)))
                alpha = jnp.exp(m_prev - m_new)
                l_scr[hh, rs, :] = alpha * l_scr[hh, rs, :] + jnp.sum(pr, axis=-1, keepdims=True)
                acc_scr[hh, rs, :] = alpha * acc_scr[hh, rs, :] + _nn(pr.astype(BF16), vv)
                m_scr[hh, rs, :] = m_new

        @pl.when(kb < i)
        def _():
            update(0, t, t, False)

        @pl.when(kb == i)
        def _():
            update(0, t // 2, t // 2, True)
            update(t // 2, t // 2, t, True)
            o0 = acc_scr[0] / l_scr[0]
            o1 = acc_scr[1] / l_scr[1]
            o_ref[...] = jnp.where(_head_half((t, 128), 0), o0, o1)
            for hh in range(2):
                lse_ref[:, hh * 128:(hh + 1) * 128] = m_scr[hh] + jnp.log(l_scr[hh])

    grid_spec = pltpu.PrefetchScalarGridSpec(
        num_scalar_prefetch=2, grid=(4, qtab.shape[0]),
        in_specs=[pl.BlockSpec((t, 256), lambda j, s, qi, ki: (qi[s], j)),
                  pl.BlockSpec((t, 256), lambda j, s, qi, ki: (ki[s], j)),
                  pl.BlockSpec((t, 128), lambda j, s, qi, ki: (ki[s], j))],
        out_specs=[pl.BlockSpec((t, 128), lambda j, s, qi, ki: (qi[s], j)),
                   pl.BlockSpec((t, 256), lambda j, s, qi, ki: (qi[s], j))],
        scratch_shapes=[pltpu.VMEM((2, t, 128), F32), pltpu.VMEM((2, t, 128), F32), pltpu.VMEM((2, t, 128), F32)])
    return pl.pallas_call(
        body, name="mla_flash_fwd", grid_spec=grid_spec,
        out_shape=[jax.ShapeDtypeStruct((SEQ, 512), F32), jax.ShapeDtypeStruct((SEQ, 1024), F32)],
        compiler_params=_cparams(),
    )(qtab, ktab, q, k, v)


def _mla_flash_bwd(q, k, v, o, do, lse, token=None):
    t = FLASH_T
    nb = SEQ // t
    qtab, ktab = _tri_steps(nb, False)
    after, after_specs = _after(token)

    def body(qi_ref, ki_ref, q_ref, k_ref, v_ref, o_ref, do_ref, lse_ref, *rest):
        dq_ref, dk_ref, dv_ref, dk_scr, dv_scr = rest[-5:]
        step = pl.program_id(1)
        i, kb = qi_ref[step], ki_ref[step]

        @pl.when(step == 0)
        def _():
            dq_ref[...] = jnp.zeros_like(dq_ref)

        @pl.when(i == kb)
        def _():
            dk_scr[...] = jnp.zeros_like(dk_scr)
            dv_scr[...] = jnp.zeros_like(dv_scr)

        def update(r0, nr, nk, diagonal):
            rs = slice(r0, r0 + nr)
            vv = v_ref[0:nk, :]
            ov = o_ref[rs, :]
            dov = do_ref[rs, :]
            rows = pl.ds(pl.multiple_of(i * t + r0, t // 2), nr)
            for hh in range(2):
                sl = slice(hh * 128, (hh + 1) * 128)
                qh, kh = q_ref[rs, sl], k_ref[0:nk, sl]
                s = _nt(qh, kh)
                if diagonal:
                    s = jnp.where(_diag_keep(nr, nk), s, NEG)
                pr = jnp.exp(s - jnp.tile(lse_ref[rs, sl], (1, nk // 128)))
                dom = jnp.where(_head_half((nr, 128), hh), dov, 0.0)
                domb = dom.astype(BF16)
                dv_scr[0:nk, :] += _tn(pr.astype(BF16), domb)
                dpr = _nt(domb, vv)
                delta = jnp.sum(dom * ov, axis=-1, keepdims=True)
                ds = (pr * (dpr - delta)).astype(BF16)
                dq_ref[rows, sl] += _nn(ds, kh)
                dk_scr[hh, 0:nk, :] += _tn(ds, qh)

        @pl.when(i > kb)
        def _():
            update(0, t, t, False)

        @pl.when(i == kb)
        def _():
            update(0, t // 2, t // 2, True)
            update(t // 2, t // 2, t, True)

        @pl.when(i == nb - 1)
        def _():
            dk_ref[:, 0:128] = dk_scr[0]
            dk_ref[:, 128:256] = dk_scr[1]
            dv_ref[...] = dv_scr[...]

    qi_map = lambda j, s, qi, ki: (qi[s], j)
    ki_map = lambda j, s, qi, ki: (ki[s], j)
    grid_spec = pltpu.PrefetchScalarGridSpec(
        num_scalar_prefetch=2, grid=(4, qtab.shape[0]),
        in_specs=[pl.BlockSpec((t, 256), qi_map), pl.BlockSpec((t, 256), ki_map), pl.BlockSpec((t, 128), ki_map),
                  pl.BlockSpec((t, 128), qi_map), pl.BlockSpec((t, 128), qi_map), pl.BlockSpec((t, 256), qi_map)]
        + after_specs,
        out_specs=[pl.BlockSpec((SEQ, 256), lambda j, s, qi, ki: (0, j)), pl.BlockSpec((t, 256), ki_map),
                   pl.BlockSpec((t, 128), ki_map)],
        scratch_shapes=[pltpu.VMEM((2, t, 128), F32), pltpu.VMEM((t, 128), F32)])
    return pl.pallas_call(
        body, name="mla_flash_bwd", grid_spec=grid_spec,
        out_shape=[jax.ShapeDtypeStruct((SEQ, 1024), F32), jax.ShapeDtypeStruct((SEQ, 1024), F32),
                   jax.ShapeDtypeStruct((SEQ, 512), F32)],
        compiler_params=_cparams(),
    )(qtab, ktab, q, k, v, o, do, lse, *after)


def _strided(start, size, d):
    return pl.ds(start, size) if d == 1 else pl.ds(start, size, stride=d)


def _dil_prep_fwd(p, rc, rs, g):
    d = DIL_DILATIONS[g]
    sub_len = SEQ // d
    ch = min(sub_len, 512)

    def body(p_ref, c_ref, s_ref, o_ref, x_scr):
        tq = pl.program_id(0)
        lanes = _rope_lanes((ch, 128), DIL_ROPE_HALF, 64, 0)
        o_ref[0, 0:BAND, :] = jnp.zeros((BAND, 128), BF16)

        @pl.when(tq < 2)
        def _():
            mult = jnp.where(tq == 0, DIL_SCALE, 1.0).astype(F32)
            for c0 in range(0, SEQ, ch):
                rows = pl.ds(c0, ch)
                x_scr[rows, :] = _rope_fwd(p_ref[rows, :], c_ref[rows, :] * mult, s_ref[rows, :] * mult, DIL_ROPE_HALF, lanes)

        def gather(src):
            for r in range(d):
                for c0 in range(0, sub_len, ch):
                    at = BAND + r * sub_len + c0
                    o_ref[0, at:at + ch, :] = src[_strided(r + c0 * d, ch, d), :].astype(BF16)

        @pl.when(tq < 2)
        def _():
            gather(x_scr)

        @pl.when(tq == 2)
        def _():
            gather(p_ref)

    tab = pl.BlockSpec((SEQ, 128), lambda tq, pr: (0, 0))
    return pl.pallas_call(
        body, name=f"dil_prep_fwd_g{g}", grid=(3, 4),
        in_specs=[pl.BlockSpec((SEQ, 128), lambda tq, pr: (0, _qkv_block(tq, g, pr))), tab, tab],
        out_specs=pl.BlockSpec((1, BAND + SEQ, 128), lambda tq, pr: (tq, 0, pr)),
        out_shape=jax.ShapeDtypeStruct((3, BAND + SEQ, 512), BF16),
        scratch_shapes=[pltpu.VMEM((SEQ, 128), F32)],
        compiler_params=_cparams(),
    )(p, rc, rs)


DIL_ST_FWD, DIL_ST_BWD = 1024, 2048


def _band_keep(g, b, t, nb):
    nbs = SEQ // DIL_DILATIONS[g] // BAND
    row = lax.broadcasted_iota(jnp.int32, (BAND, 2 * BAND), 0)
    col = lax.broadcasted_iota(jnp.int32, (BAND, 2 * BAND), 1)
    cur = (col >= BAND) & (row >= col - BAND)
    prev = (col < BAND) & (col >= row)
    if nbs >= nb:
        if b > 0:
            return cur | prev
        return cur | (prev & ((t * nb) % nbs != 0))
    return cur | prev if b % nbs else cur


def _dil_tok(g, b, t, nb):
    d = DIL_DILATIONS[g]
    nbs = SEQ // d // BAND
    gb = t * nb + b
    return _strided((gb % nbs) * BAND * d + gb // nbs, BAND, d)


def _dil_attn_fwd(qkv, g):
    DIL_ST, DIL_NB = DIL_ST_FWD, DIL_ST_FWD // BAND

    def body(q_ref, k_ref, v_ref, o_ref, l_ref, s_scr, p_scr, o_scr):
        t = pl.program_id(1)
        base = t * DIL_ST
        half0 = _head_half((DIL_ST, 128), 0)
        lse_h = []
        for hh in range(2):
            half = _head_half((BAND, 128), hh)
            for b in range(DIL_NB):
                qv = q_ref[0, pl.ds(pl.multiple_of(base + (b + 1) * BAND, BAND), BAND), :]
                k2 = k_ref[0, pl.ds(pl.multiple_of(base + b * BAND, BAND), 2 * BAND), :]
                sb = _nt(jnp.where(half, qv, jnp.zeros_like(qv)), k2)
                s_scr[b * BAND:(b + 1) * BAND, :] = jnp.where(_band_keep(g, b, t, DIL_NB), sb, NEG)
            s = s_scr[...]
            m = jnp.max(s, axis=-1, keepdims=True)
            pr = jnp.exp(s - m)
            den = jnp.sum(pr, axis=-1, keepdims=True)
            p_scr[...] = pr.astype(BF16)
            for b in range(DIL_NB):
                v2 = v_ref[0, pl.ds(pl.multiple_of(base + b * BAND, BAND), 2 * BAND), :]
                o_scr[hh, b * BAND:(b + 1) * BAND, :] = _nn(p_scr[b * BAND:(b + 1) * BAND, :], v2)
            o_scr[hh] = o_scr[hh] / den
            lse_h.append(m + jnp.log(den))
        out = jnp.where(half0, o_scr[0], o_scr[1])
        lse = jnp.where(half0, lse_h[0], lse_h[1])
        for b in range(DIL_NB):
            tok = _dil_tok(g, b, t, DIL_NB)
            o_ref[tok, :] = out[b * BAND:(b + 1) * BAND, :]
            l_ref[tok, :] = lse[b * BAND:(b + 1) * BAND, :]

    def inp(tq):
        return pl.BlockSpec((1, BAND + SEQ, 128), lambda pr, t: (tq, 0, pr))

    out = pl.BlockSpec((SEQ, 128), lambda pr, t: (0, pr))
    return pl.pallas_call(
        body, name=f"dil_attn_fwd_g{g}", grid=(4, SEQ // DIL_ST),
        in_specs=[inp(0), inp(1), inp(2)], out_specs=[out, out],
        out_shape=[jax.ShapeDtypeStruct((SEQ, 512), F32), jax.ShapeDtypeStruct((SEQ, 512), F32)],
        scratch_shapes=[pltpu.VMEM((DIL_ST, 2 * BAND), F32), pltpu.VMEM((DIL_ST, 2 * BAND), BF16),
                        pltpu.VMEM((2, DIL_ST, 128), F32)],
        compiler_params=_cparams(),
    )(qkv, qkv, qkv)


def _dil_attn_bwd(dp_in, qkv, dyd, yd, lse_all, rc, rs, g, token=None):
    d = DIL_DILATIONS[g]
    sub_len = SEQ // d
    DIL_ST, DIL_NB = DIL_ST_BWD, DIL_ST_BWD // BAND
    nst = SEQ // DIL_ST
    after, after_specs = _after(token)
    ch = 512

    def body(dp_any, q_ref, k_ref, v_ref, do_ref, y_ref, l_ref, c_ref, sn_ref, *rest):
        dp_ref, tok_scr, dk_scr, dv_scr, s_scr, dp_scr, p_scr, ds_scr, do_scr, y_scr, l_scr, dq_scr = rest[-12:]
        del dp_any
        t = pl.program_id(1)
        base = t * DIL_ST

        @pl.when(t == 0)
        def _():
            dk_scr[...] = jnp.zeros_like(dk_scr)
            dv_scr[...] = jnp.zeros_like(dv_scr)

        for b in range(DIL_NB):
            tok = _dil_tok(g, b, t, DIL_NB)
            do_scr[b * BAND:(b + 1) * BAND, :] = do_ref[tok, :]
            y_scr[b * BAND:(b + 1) * BAND, :] = y_ref[tok, :]
            l_scr[b * BAND:(b + 1) * BAND, :] = l_ref[tok, :]
        for hh in range(2):
            half = _head_half((BAND, 128), hh)
            half_st = _head_half((DIL_ST, 128), hh)
            dom = jnp.where(half_st, do_scr[...], 0.0)
            delta = jnp.sum(dom * y_scr[...], axis=-1, keepdims=True)
            lcol = jnp.max(jnp.where(half_st, l_scr[...], NEG), axis=-1, keepdims=True)
            for b in range(DIL_NB):
                rows = slice(b * BAND, (b + 1) * BAND)
                qv = q_ref[0, pl.ds(pl.multiple_of(base + (b + 1) * BAND, BAND), BAND), :]
                band = pl.ds(pl.multiple_of(base + b * BAND, BAND), 2 * BAND)
                sb = _nt(jnp.where(half, qv, jnp.zeros_like(qv)), k_ref[0, band, :])
                s_scr[rows, :] = jnp.where(_band_keep(g, b, t, DIL_NB), sb, NEG)
                dp_scr[rows, :] = _nt(dom[rows, :].astype(BF16), v_ref[0, band, :])
            pr = jnp.exp(s_scr[...] - lcol)
            p_scr[...] = pr.astype(BF16)
            ds_scr[...] = (pr * (dp_scr[...] - delta)).astype(BF16)
            for b in range(DIL_NB):
                rows = slice(b * BAND, (b + 1) * BAND)
                qv = q_ref[0, pl.ds(pl.multiple_of(base + (b + 1) * BAND, BAND), BAND), :]
                band = pl.ds(pl.multiple_of(base + b * BAND, BAND), 2 * BAND)
                dqb = jnp.where(half, _nn(ds_scr[rows, :], k_ref[0, band, :]), 0.0)
                if hh == 0:
                    dq_scr[rows, :] = dqb
                else:
                    dq_scr[rows, :] += dqb
                half2 = _head_half((2 * BAND, 128), hh)
                dk_scr[band, :] += jnp.where(half2, _tn(ds_scr[rows, :], qv), 0.0)
                dv_scr[band, :] += _tn(p_scr[rows, :], dom[rows, :].astype(BF16))
        for b in range(DIL_NB):
            tok_scr[pl.ds(0, 1), _dil_tok(g, b, t, DIL_NB), :] = dq_scr[b * BAND:(b + 1) * BAND, :][None]

        @pl.when(t == nst - 1)
        def _():
            for r in range(d):
                rows = _strided(r, sub_len, d)
                tok_scr[pl.ds(1, 1), rows, :] = dk_scr[BAND + r * sub_len:BAND + (r + 1) * sub_len, :][None]
                tok_scr[pl.ds(2, 1), rows, :] = dv_scr[BAND + r * sub_len:BAND + (r + 1) * sub_len, :][None]
            lanes = _rope_lanes((ch, 128), DIL_ROPE_HALF, 64, 0)
            for c0 in range(0, SEQ, ch):
                rows = slice(c0, c0 + ch)
                cv, sv = c_ref[rows, :], sn_ref[rows, :]
                dp_ref[rows, 0:128] = _rope_bwd(tok_scr[0, rows, :], cv * DIL_SCALE, sv * DIL_SCALE, DIL_ROPE_HALF, lanes).astype(BF16)
                dp_ref[rows, 128:256] = _rope_bwd(tok_scr[1, rows, :], cv, sv, DIL_ROPE_HALF, lanes).astype(BF16)
                dp_ref[rows, 256:384] = tok_scr[2, rows, :].astype(BF16)

    def inp(tq):
        return pl.BlockSpec((1, BAND + SEQ, 128), lambda pr, t: (tq, 0, pr))

    tok_spec = pl.BlockSpec((SEQ, 128), lambda pr, t: (0, pr))
    tab = pl.BlockSpec((SEQ, 128), lambda pr, t: (0, 0))
    st = (DIL_ST, 2 * BAND)
    return pl.pallas_call(
        body, name=f"dil_attn_bwd_g{g}", grid=(4, nst),
        in_specs=[pl.BlockSpec(memory_space=pl.ANY), inp(0), inp(1), inp(2), tok_spec, tok_spec, tok_spec, tab, tab]
        + after_specs,
        out_specs=pl.BlockSpec((SEQ, 384), lambda pr, t: (0, _qkv_block(0, g, pr) // 3)),
        out_shape=jax.ShapeDtypeStruct((SEQ, N_PAD), BF16),
        input_output_aliases={0: 0},
        scratch_shapes=[pltpu.VMEM((3, SEQ, 128), F32),
                        pltpu.VMEM((BAND + SEQ, 128), F32), pltpu.VMEM((BAND + SEQ, 128), F32),
                        pltpu.VMEM(st, F32), pltpu.VMEM(st, F32), pltpu.VMEM(st, BF16), pltpu.VMEM(st, BF16),
                        pltpu.VMEM((DIL_ST, 128), F32), pltpu.VMEM((DIL_ST, 128), F32), pltpu.VMEM((DIL_ST, 128), F32),
                        pltpu.VMEM((DIL_ST, 128), F32)],
        compiler_params=_cparams(),
    )(dp_in, qkv, qkv, qkv, dyd, yd, lse_all, rc, rs, *after)


TAIL_T = 256


def _tail(p, ya, o_g, l_g, x, target, wpm, wpd, wout, post_g):
    tm = TAIL_T

    def body(pgz_ref, ya_ref, o0_ref, o1_ref, o2_ref, l0_ref, l1_ref, l2_ref, x_ref, t_ref,
             wpm_ref, wpd_ref, wout_ref, pg_ref,
             dp_ref, dy_ref, dya_ref, dyd_ref, yd_ref, lse_ref, loss_ref, dgp_ref, dwpm_ref, dwpd_ref, dwout_ref):
        l0, l1, l2 = l0_ref[...], l1_ref[...], l2_ref[...]
        mx = jnp.maximum(jnp.maximum(l0, l1), l2)
        e0, e1, e2 = jnp.exp(l0 - mx), jnp.exp(l1 - mx), jnp.exp(l2 - mx)
        den = e0 + e1 + e2
        yd = (e0 * o0_ref[...] + e1 * o1_ref[...] + e2 * o2_ref[...]) / den
        yd_ref[...] = yd
        lse_ref[...] = mx + jnp.log(den)
        ya = ya_ref[...]

        gm, gd = pgz_ref[:, 0:1024], pgz_ref[:, 1024:2048]
        zm, zd = pgz_ref[:, 2048:2560], pgz_ref[:, 2560:3072]
        szm, szd = _sigmoid(zm), _sigmoid(zd)
        sm, sd = zm * szm, zd * szd
        ua = (ya * sm).astype(BF16)
        ud = (yd * sd).astype(BF16)
        pa = _nn(ua, wpm_ref[...])
        pd = _nn(ud, wpd_ref[...])
        sgm, sgd = _sigmoid(gm), _sigmoid(gd)
        mg = (sgm * pa + sgd * pd).astype(BF16)
        t = _nn(mg, wout_ref[...])
        r3 = lax.rsqrt(jnp.mean(t * t, axis=-1, keepdims=True) + EPS)
        n = t * r3
        pg = pg_ref[...]
        err = x_ref[...] + n * pg - t_ref[...]
        lpart = jnp.sum(err * err, axis=0, keepdims=True)

        dy = err * (1.0 / D_MODEL)
        dy_ref[...] = dy
        gpart = jnp.sum(dy * n, axis=0, keepdims=True)
        dn = dy * pg
        dt = (r3 * (dn - n * jnp.mean(dn * n, axis=-1, keepdims=True))).astype(BF16)
        dmg = _nt(dt, wout_ref[...])
        dpa = (dmg * sgm).astype(BF16)
        dpd = (dmg * sgd).astype(BF16)
        dp_ref[:, 0:1024] = (dmg * pa * sgm * (1.0 - sgm)).astype(BF16)
        dp_ref[:, 1024:2048] = (dmg * pd * sgd * (1.0 - sgd)).astype(BF16)
        dua = _nt(dpa, wpm_ref[...])
        dud = _nt(dpd, wpd_ref[...])
        dya_ref[...] = dua * sm
        dyd_ref[...] = dud * sd
        dp_ref[:, 2048:2560] = (dua * ya * szm * (1.0 + zm * (1.0 - szm))).astype(BF16)
        dp_ref[:, 2560:3072] = (dud * yd * szd * (1.0 + zd * (1.0 - szd))).astype(BF16)

        wpm, wpd, wout = _tn(ua, dpa), _tn(ud, dpd), _tn(mg, dt)

        @pl.when(pl.program_id(0) == 0)
        def _():
            loss_ref[...] = lpart
            dgp_ref[...] = gpart
            dwpm_ref[...] = wpm
            dwpd_ref[...] = wpd
            dwout_ref[...] = wout

        @pl.when(pl.program_id(0) > 0)
        def _():
            loss_ref[...] += lpart
            dgp_ref[...] += gpart
            dwpm_ref[...] += wpm
            dwpd_ref[...] += wpd
            dwout_ref[...] += wout

    def rows(w):
        return pl.BlockSpec((tm, w), lambda i: (i, 0))

    def full(shape):
        return pl.BlockSpec(shape, lambda i: (0, 0))

    def sds(w, dt):
        return jax.ShapeDtypeStruct((SEQ, w), dt)

    return pl.pallas_call(
        body, name="tail", grid=(SEQ // tm,),
        in_specs=[rows(3072), rows(512), rows(512), rows(512), rows(512), rows(512), rows(512), rows(512),
                  rows(1024), rows(1024), full((512, 1024)), full((512, 1024)), full((1024, 1024)), full((1, 1024))],
        out_specs=[rows(3072), rows(1024), rows(512), rows(512), rows(512), rows(512), full((1, 1024)), full((1, 1024)),
                   full((512, 1024)), full((512, 1024)), full((1024, 1024))],
        out_shape=[sds(N_PAD, BF16), sds(1024, F32), sds(512, F32), sds(512, F32), sds(512, F32), sds(512, F32),
                   jax.ShapeDtypeStruct((1, 1024), F32), jax.ShapeDtypeStruct((1, 1024), F32),
                   jax.ShapeDtypeStruct((512, 1024), F32), jax.ShapeDtypeStruct((512, 1024), F32),
                   jax.ShapeDtypeStruct((1024, 1024), F32)],
        compiler_params=_cparams(),
    )(p, ya, o_g[0], o_g[1], o_g[2], l_g[0], l_g[1], l_g[2], x, target, wpm, wpd, wout, post_g)


def _sum_parts(recv, own, me, tr, name):
    n, r, w = recv.shape
    if r % tr:
        return _sum_parts_cols(recv, own, me, name)
    own_spec = (pl.BlockSpec((tr, w), lambda i, me_ref: (i, 0)) if own.ndim == 2
                else pl.BlockSpec((None, tr, w), lambda i, me_ref: (me_ref[0], i, 0)))

    def body(me_ref, p_ref, own_ref, o_ref):
        mine = own_ref[...].astype(F32)
        acc = jnp.zeros((tr, w), F32)
        for s in range(n):
            acc = acc + jnp.where(me_ref[0] == s, mine, p_ref[s].astype(F32))
        o_ref[...] = acc

    return pl.pallas_call(
        body, name=name,
        grid_spec=pltpu.PrefetchScalarGridSpec(
            num_scalar_prefetch=1, grid=(r // tr,),
            in_specs=[pl.BlockSpec((n, tr, w), lambda i, me_ref: (0, i, 0)), own_spec],
            out_specs=pl.BlockSpec((tr, w), lambda i, me_ref: (i, 0))),
        out_shape=jax.ShapeDtypeStruct((r, w), F32),
    )(me.reshape(1), recv, own)


def _sum_parts_cols(recv, own, me, name):
    n, r, w = recv.shape
    tc = 128

    def body(me_ref, p_ref, own_ref, o_ref):
        mine = own_ref[...].astype(F32)
        acc = jnp.zeros((r, tc), F32)
        for s in range(n):
            acc = acc + jnp.where(me_ref[0] == s, mine, p_ref[s].astype(F32))
        o_ref[...] = acc

    return pl.pallas_call(
        body, name=name,
        grid_spec=pltpu.PrefetchScalarGridSpec(
            num_scalar_prefetch=1, grid=(w // tc,),
            in_specs=[pl.BlockSpec((n, r, tc), lambda i, me_ref: (0, 0, i)),
                      pl.BlockSpec((None, r, tc), lambda i, me_ref: (me_ref[0], 0, i))],
            out_specs=pl.BlockSpec((r, tc), lambda i, me_ref: (0, i))),
        out_shape=jax.ShapeDtypeStruct((r, w), F32),
    )(me.reshape(1), recv, own)


def _adamw(w, g, m, v, name):
    lead = w.shape[:-2]
    r, c = w.shape[-2:]
    tr = max([t for t in range(8, 257, 8) if r % t == 0], default=r)
    c1 = 1.0 - ADAM_B1 ** ADAM_STEP
    c2 = 1.0 - ADAM_B2 ** ADAM_STEP

    def body(w_ref, g_ref, m_ref, v_ref, d_ref, nm_ref, nv_ref):
        gv = g_ref[...]
        nm = ADAM_B1 * m_ref[...] + (1.0 - ADAM_B1) * gv
        nv = ADAM_B2 * v_ref[...] + (1.0 - ADAM_B2) * (gv * gv)
        nm_ref[...] = nm
        nv_ref[...] = nv
        d_ref[...] = -ADAM_LR * ((nm / c1) / (jnp.sqrt(nv / c2) + ADAM_EPS) + ADAM_WD * w_ref[...])

    zeros = (0,) * len(lead)
    spec = pl.BlockSpec((1,) * len(lead) + (tr, c), lambda i: zeros + (i, 0))
    sd = jax.ShapeDtypeStruct(w.shape, F32)
    return pl.pallas_call(
        body, name=name, grid=(r // tr,),
        in_specs=[spec] * 4, out_specs=[spec] * 3, out_shape=[sd] * 3,
    )(w, g, m, v)


def _adamw_in(w_t, m_t, v_t, own_half, swapped, core):
    r, c = SHARD_SHAPES[0]
    tr = max(t for t in range(8, 257, 8) if r % t == 0)
    c1 = 1.0 - ADAM_B1 ** ADAM_STEP
    c2 = 1.0 - ADAM_B2 ** ADAM_STEP

    def body(core_ref, w_ref, m_ref, v_ref, own_ref, sw_ref, d_ref, nm_ref, nv_ref, g_ref):
        own = own_ref[...]
        col_half = lax.broadcasted_iota(jnp.int32, (tr, c), 1) // (c // 2)
        gv = jnp.where(col_half == core_ref[0], jnp.concatenate([own, own], axis=1), sw_ref[...])
        g_ref[0] = gv
        nm = ADAM_B1 * m_ref[0] + (1.0 - ADAM_B1) * gv
        nv = ADAM_B2 * v_ref[0] + (1.0 - ADAM_B2) * (gv * gv)
        nm_ref[0] = nm
        nv_ref[0] = nv
        d_ref[0] = -ADAM_LR * ((nm / c1) / (jnp.sqrt(nv / c2) + ADAM_EPS) + ADAM_WD * w_ref[0])

    full = pl.BlockSpec((1, tr, c), lambda i, core_ref: (0, i, 0))
    sd = jax.ShapeDtypeStruct((1, r, c), F32)
    return pl.pallas_call(
        body, name="adamw_in",
        grid_spec=pltpu.PrefetchScalarGridSpec(
            num_scalar_prefetch=1, grid=(r // tr,),
            in_specs=[full, full, full, pl.BlockSpec((tr, c // 2), lambda i, core_ref: (i, 0)),
                      pl.BlockSpec((tr, c), lambda i, core_ref: (i, 0))],
            out_specs=[full] * 4),
        out_shape=[sd] * 4,
    )(core.reshape(1), w_t, m_t, v_t, own_half, swapped)


ANY = pl.BlockSpec(memory_space=pl.ANY)


def _my_place():
    return lax.axis_index("x"), lax.axis_index("y"), lax.axis_index("c")


HBM = pl.BlockSpec(memory_space=pltpu.HBM)
SEM = pl.BlockSpec(memory_space=pltpu.SEMAPHORE)
DATAFLOW = pltpu.SideEffectType.DATAFLOW_SIDE_EFFECTING


def _near_chips(x, y):
    return [(1 - x, y), (x, 1 - y)]


def _half(mi, hc):
    r, c = SHARD_SHAPES[mi]
    if mi == 0:
        return pl.ds(0, r), pl.ds(pl.multiple_of(hc * (c // 2), 128), c // 2)
    return pl.ds(pl.multiple_of(hc * (r // 2), 16), r // 2), pl.ds(0, c)


def _gather_copies(m_refs, land_refs, send_sems, recv_sems):
    x, y, c = _my_place()
    out, back = [], []
    for mi in range(N_MATS):
        rows, cols = _half(mi, c)
        for j, (cx, cy) in enumerate(_near_chips(x, y)):
            sems = dict(send_sem=send_sems.at[mi * 2 + j], recv_sem=recv_sems.at[mi * 2 + j],
                        device_id=(cx, cy, c), device_id_type=MESH)
            out.append(pltpu.make_async_remote_copy(src_ref=m_refs[mi].at[rows, cols],
                                                    dst_ref=land_refs[mi].at[2 * x + y, rows, cols], **sems))
            got = land_refs[mi].at[2 * cx + cy, rows, cols]
            back.append(pltpu.make_async_remote_copy(src_ref=got, dst_ref=got, **sems))
    return out, back


def _gather_start(mats, landing):
    n = N_MATS

    def body(*refs):
        out, _ = _gather_copies(refs[:n], refs[n:2 * n], refs[2 * n], refs[2 * n + 1])
        for cp in out:
            cp.start()
        refs[-1][...] = jnp.zeros_like(refs[-1])

    hbm = [pltpu.HBM(a.shape, a.dtype) for a in list(mats) + list(landing)]
    outs = pl.pallas_call(
        body, name="gather_start",
        out_shape=(pltpu.SemaphoreType.DMA((2 * n,)), pltpu.SemaphoreType.DMA((2 * n,)), *hbm,
                   jax.ShapeDtypeStruct((8, 128), F32)),
        in_specs=[HBM] * (2 * n), out_specs=(SEM, SEM, *[HBM] * (2 * n), pl.BlockSpec(memory_space=pltpu.VMEM)),
        input_output_aliases={i: 2 + i for i in range(2 * n)},
        compiler_params=pltpu.CompilerParams(has_side_effects=DATAFLOW),
    )(*[pltpu.with_memory_space_constraint(a, pltpu.HBM) for a in list(mats) + list(landing)])
    return outs[:-1], outs[-1]


def _gather_wait(handle, after):
    n = N_MATS

    def body(*refs):
        out, back = _gather_copies(refs[:n], refs[n:2 * n], refs[2 * n], refs[2 * n + 1])
        for cp, arrival in zip(out, back):
            cp.wait_send()
            arrival.wait_recv()

    bufs = handle[2:]
    after, after_specs = _after(after)
    res = pl.pallas_call(
        body, name="gather_wait", out_shape=tuple(pltpu.HBM(b.shape, b.dtype) for b in bufs),
        in_specs=[HBM] * (2 * n) + [SEM, SEM] + after_specs, out_specs=tuple([HBM] * (2 * n)),
        input_output_aliases={i: i for i in range(2 * n)},
        compiler_params=pltpu.CompilerParams(has_side_effects=DATAFLOW),
    )(*bufs, handle[0], handle[1], *after)
    return list(res[n:])


def _relay_share(gathered):
    n = N_MATS

    def body(*refs):
        out_refs = refs[n:2 * n]
        send_sems, recv_sems = refs[2 * n:]
        x, y, c = _my_place()
        sibling = (x, y, 1 - c)
        relayed = 2 * (x ^ (1 - c)) + (y ^ c)
        relay_to = (x ^ c, y ^ (1 - c), c)
        far = 2 * (1 - x) + (1 - y)
        near = [2 * (1 - x) + y, 2 * x + (1 - y)]

        def copy(k, mi, shard, hc, to):
            blk = out_refs[mi].at[(shard,) + _half(mi, hc)]
            return pltpu.make_async_remote_copy(src_ref=blk, dst_ref=blk, send_sem=send_sems.at[mi * 4 + k],
                                                recv_sem=recv_sems.at[mi * 4 + k], device_id=to, device_id_type=MESH)

        sends = []
        for mi in range(n):
            sends.append(copy(0, mi, relayed, c, relay_to))
            sends += [copy(1 + j, mi, near[j], c, sibling) for j in range(2)]
        for cp in sends:
            cp.start()
        for mi in range(n):
            copy(0, mi, far, c, relay_to).wait_recv()
            cp = copy(3, mi, far, c, sibling)
            cp.start()
            sends.append(cp)
        for mi in range(n):
            for j in range(2):
                copy(1 + j, mi, near[j], 1 - c, sibling).wait_recv()
            copy(3, mi, far, 1 - c, sibling).wait_recv()
        for cp in sends:
            cp.wait_send()

    return pl.pallas_call(
        body, name="relay_share",
        in_specs=[ANY] * n, out_specs=[ANY] * n,
        out_shape=[jax.ShapeDtypeStruct(g.shape, g.dtype) for g in gathered],
        input_output_aliases={i: i for i in range(n)},
        scratch_shapes=[pltpu.SemaphoreType.DMA((4 * n,)), pltpu.SemaphoreType.DMA((4 * n,))],
    )(*gathered)


def _peers(x, y, c):
    out = []
    for k in range(1, 8):
        px, py, pc = x ^ (k >> 2), y ^ ((k >> 1) & 1), c ^ (k & 1)
        out.append((k - 1, (px, py, pc), 4 * px + 2 * py + pc))
    return out


def _exchange_start(parts, name):
    n = len(parts)

    def body(*refs):
        p_refs, land_refs = refs[:n], refs[n:2 * n]
        send_sems, recv_sems, token = refs[2 * n], refs[2 * n + 1], refs[-1]
        x, y, c = _my_place()
        me = 4 * x + 2 * y + c
        for k, dev, peer in _peers(x, y, c):
            for mi in range(n):
                pltpu.make_async_remote_copy(
                    src_ref=p_refs[mi].at[peer], dst_ref=land_refs[mi].at[me], send_sem=send_sems.at[k * n + mi],
                    recv_sem=recv_sems.at[k * n + mi], device_id=dev, device_id_type=MESH).start()
        token[...] = jnp.zeros_like(token)

    hbm = [pltpu.HBM(p.shape, p.dtype) for p in parts]
    outs = pl.pallas_call(
        body, name=name + "_start",
        out_shape=(pltpu.SemaphoreType.DMA((7 * n,)), pltpu.SemaphoreType.DMA((7 * n,)), *hbm, *hbm,
                   jax.ShapeDtypeStruct((8, 128), F32)),
        in_specs=[HBM] * (2 * n), out_specs=(SEM, SEM, *[HBM] * (2 * n), pl.BlockSpec(memory_space=pltpu.VMEM)),
        input_output_aliases={i: 2 + i for i in range(2 * n)},
        compiler_params=pltpu.CompilerParams(has_side_effects=DATAFLOW),
    )(*[pltpu.with_memory_space_constraint(p, pltpu.HBM) for p in parts],
      *[pltpu.with_memory_space_constraint(lax.empty(p.shape, p.dtype), pltpu.HBM) for p in parts])
    return (name, outs[:-1]), outs[-1]


def _exchange_wait(handle, after):
    name, outs = handle
    n = (len(outs) - 2) // 2

    def body(*refs):
        p_refs, land_refs = refs[:n], refs[n:2 * n]
        send_sems, recv_sems = refs[2 * n], refs[2 * n + 1]
        x, y, c = _my_place()
        me = 4 * x + 2 * y + c
        for k, dev, peer in _peers(x, y, c):
            for mi in range(n):
                pltpu.make_async_remote_copy(
                    src_ref=p_refs[mi].at[peer], dst_ref=land_refs[mi].at[me], send_sem=send_sems.at[k * n + mi],
                    recv_sem=recv_sems.at[k * n + mi], device_id=dev, device_id_type=MESH).wait_send()
                slot = land_refs[mi].at[peer]
                pltpu.make_async_remote_copy(
                    src_ref=slot, dst_ref=slot, send_sem=send_sems.at[k * n + mi],
                    recv_sem=recv_sems.at[k * n + mi], device_id=dev, device_id_type=MESH).wait_recv()

    bufs = outs[2:]
    res = pl.pallas_call(
        body, name=name + "_wait", out_shape=tuple(pltpu.HBM(b.shape, b.dtype) for b in bufs),
        in_specs=[HBM] * (2 * n) + [SEM, SEM, ANY], out_specs=tuple([HBM] * (2 * n)),
        input_output_aliases={i: i for i in range(2 * n)},
        compiler_params=pltpu.CompilerParams(has_side_effects=DATAFLOW),
    )(*bufs, outs[0], outs[1], after)
    return list(res[n:])


def _swap_halves(halves, gvec):
    def place(ref, mi, hc):
        return ref.at[:, pl.ds(pl.multiple_of(hc * 512, 128), 512)] if mi == 0 else ref.at[hc]

    def body(*refs):
        g_refs, gv_ref = refs[:N_MATS], refs[N_MATS]
        out_refs, rg_ref = refs[N_MATS + 1:2 * N_MATS + 1], refs[2 * N_MATS + 1]
        send_sems, recv_sems = refs[2 * N_MATS + 2:]
        x, y, c = _my_place()
        me = 4 * x + 2 * y + c
        sends = []
        for mi in range(N_MATS):
            cp = pltpu.make_async_remote_copy(src_ref=g_refs[mi], dst_ref=place(out_refs[mi], mi, c), send_sem=send_sems.at[mi],
                                              recv_sem=recv_sems.at[mi], device_id=(x, y, 1 - c), device_id_type=MESH)
            cp.start()
            sends.append(cp)
        for k, dev, peer in _peers(x, y, c):
            cp = pltpu.make_async_remote_copy(src_ref=gv_ref, dst_ref=rg_ref.at[me], send_sem=send_sems.at[N_MATS + k],
                                              recv_sem=recv_sems.at[N_MATS + k], device_id=dev, device_id_type=MESH)
            cp.start()
            sends.append(cp)
        for mi in range(N_MATS):
            got = place(out_refs[mi], mi, 1 - c)
            pltpu.make_async_remote_copy(src_ref=got, dst_ref=got, send_sem=send_sems.at[mi], recv_sem=recv_sems.at[mi],
                                         device_id=(x, y, 1 - c), device_id_type=MESH).wait_recv()
        for k, dev, peer in _peers(x, y, c):
            got = rg_ref.at[peer]
            pltpu.make_async_remote_copy(src_ref=got, dst_ref=got, send_sem=send_sems.at[N_MATS + k],
                                         recv_sem=recv_sems.at[N_MATS + k], device_id=dev, device_id_type=MESH).wait_recv()
        for cp in sends:
            cp.wait_send()

    outs = pl.pallas_call(
        body, name="swap_halves",
        in_specs=[ANY] * (N_MATS + 1), out_specs=[ANY] * (N_MATS + 1),
        out_shape=[jax.ShapeDtypeStruct(SHARD_SHAPES[0], F32)]
        + [jax.ShapeDtypeStruct((2, r // 2, c), F32) for r, c in SHARD_SHAPES[1:]]
        + [jax.ShapeDtypeStruct((8, 8, N_GVEC), F32)],
        scratch_shapes=[pltpu.SemaphoreType.DMA((N_MATS + 7,)), pltpu.SemaphoreType.DMA((N_MATS + 7,))],
    )(*halves, gvec)
    return outs[:N_MATS], outs[N_MATS]


def _set_slot(arr, block, idx):
    return lax.dynamic_update_slice(arr, block[None], (idx,) + (0,) * block.ndim)


PAD_RUNS = ((6304, 8352, 0), (5280, 6304, COL_Z), (672, 5280, COL_QKV), (0, 640, COL_LAT), (640, 672, COL_LAT + 704))
N_QKV = COL_LAT - COL_QKV


def _qkv_rows_regroup(a, to_padded):
    if to_padded:
        a4 = a.reshape(3, 12, 128, a.shape[1])
        return jnp.stack([a4[0], a4[1], a4[2]], axis=1).reshape(a.shape)
    a4 = a.reshape(12, 3, 128, a.shape[1])
    return jnp.concatenate([a4[:, tq].reshape(N_QKV // 3, a.shape[1]) for tq in range(3)], axis=0)
W_IN_SHARD = 2088


def _full_weights(gathered):
    def cols(a):
        return jnp.concatenate([a[s] for s in range(4)], axis=1)

    w_uq, w_ukv, w_pm, w_pd = [cols(a) for a in gathered[1:5]]
    w_out = gathered[5].reshape(D_MODEL, D_MODEL)
    w_in_t = gathered[0].reshape(4 * W_IN_SHARD, D_MODEL)
    pieces, at = [], 0
    for lo, hi, pad_lo in sorted(PAD_RUNS, key=lambda t: t[2]):
        if pad_lo > at:
            pieces.append(jnp.zeros((pad_lo - at, D_MODEL), w_in_t.dtype))
        pieces.append(_qkv_rows_regroup(w_in_t[lo:hi], True) if pad_lo == COL_QKV else w_in_t[lo:hi])
        at = pad_lo + hi - lo
    pieces.append(jnp.zeros((N_PAD - at, D_MODEL), w_in_t.dtype))
    w_pad_t = jnp.concatenate(pieces, axis=0)
    z32 = jnp.zeros((Q_RANK, 32), w_uq.dtype)
    wuq_pad = jnp.concatenate([t for h in range(MLA_HEADS) for t in (w_uq[:, h * 96:(h + 1) * 96], z32)], axis=1)
    z64 = jnp.zeros((KV_RANK, 64), w_ukv.dtype)
    wk_pad = jnp.concatenate([t for h in range(MLA_HEADS) for t in (w_ukv[:, h * 128:h * 128 + 64], z64)], axis=1)
    wv = jnp.concatenate([w_ukv[:, h * 128 + 64:(h + 1) * 128] for h in range(MLA_HEADS)], axis=1)
    return w_pad_t.T, w_pad_t, wuq_pad, wk_pad, wv, w_pm, w_pd, w_out


W_IN_LAT = 672


def _grad_parts_in_early(dwt_early):
    dwt_early = jnp.concatenate([dwt_early[:COL_QKV], _qkv_rows_regroup(dwt_early[COL_QKV:COL_LAT], False)], axis=0)

    def in_block(s, h):
        cols = slice(h * 512, (h + 1) * 512)
        out = []
        for lo, hi, pad_lo in sorted(PAD_RUNS):
            a_, b_ = max(lo, s * W_IN_SHARD), min(hi, (s + 1) * W_IN_SHARD)
            if a_ < b_:
                out.append(jnp.zeros((b_ - a_, 512), dwt_early.dtype) if pad_lo >= COL_LAT
                           else dwt_early[pad_lo + a_ - lo:pad_lo + b_ - lo, cols])
        return jnp.concatenate(out, axis=0)

    return jnp.stack([in_block(s, h) for s in range(4) for h in range(2)])


def _grad_parts_in_late(dwt_late):
    rows = jnp.concatenate([dwt_late[0:640], dwt_late[704:736]], axis=0)
    zero = jnp.zeros((W_IN_LAT, 512), dwt_late.dtype)
    return jnp.stack([rows[:, 0:512], rows[:, 512:1024]] + [zero] * 6)


def _col_blocks(m):
    r, c = m.shape[0] // 2, m.shape[1] // 4
    return jnp.stack([m[h * r:(h + 1) * r, s * c:(s + 1) * c] for s in range(4) for h in range(2)])


def _grad_parts_mla(dwuq_pad, dwk_pad, dwv):
    d_uq = jnp.concatenate([dwuq_pad[:, h * 128:h * 128 + 96] for h in range(MLA_HEADS)], axis=1)
    d_ukv = jnp.concatenate([t for h in range(MLA_HEADS) for t in (dwk_pad[:, h * 128:h * 128 + 64], dwv[:, h * 64:(h + 1) * 64])],
                            axis=1)
    return [_col_blocks(d_uq.astype(BF16)), _col_blocks(d_ukv.astype(BF16))]


def _rope_tables(positions, token=None):
    pos = positions.reshape(SEQ).astype(F32)
    if token is not None:
        pos = pos + token[0, 0]
    lane = jnp.arange(128)

    def table(rot, first, period):
        inv = ROPE_THETA ** (-jnp.arange(0, rot, 2, dtype=F32) / rot)
        half = rot // 2
        off = lane % period - first
        in1, in2 = (off >= 0) & (off < half), (off >= half) & (off < rot)
        inv_lane = jnp.where(in1 | in2, inv[jnp.clip(off % half, 0, half - 1)], 0.0)
        sign = jnp.where(in1, -1.0, 1.0).astype(F32)
        ang = pos[:, None] * inv_lane[None, :]
        return jnp.cos(ang), jnp.sin(ang) * sign[None, :]

    return table(32, 64, 128), table(16, 0, 64)


class _Links:
    def __init__(self, mats, chip, me):
        landing = [_set_slot(lax.empty((4,) + m.shape, m.dtype), m, chip) for m in mats]
        self.gather, self.token = _gather_start(mats, landing)
        self.me, self.sent, self.handles, self.sums = me, {}, {}, {}

    def weights(self, after):
        return _relay_share(_gather_wait(self.gather, after))

    def send(self, blocks, name):
        self.sent[name] = blocks
        self.handles[name], token = _exchange_start(blocks, name)
        return token

    def collect(self, name, after, parts):
        recv = _exchange_wait(self.handles[name], after)
        for r, own, part in zip(recv, self.sent[name], parts):
            self.sums[part] = _sum_parts(r, own, self.me, 64, "sum_grad_" + part)
        return tuple(self.sums[part] for part in parts)


def _device_grads(x, positions, target, gains, links):
    pre_g, q_g, kv_g, post_g = gains
    (mc, ms), (dc, ds) = _rope_tables(positions, links.token)
    h = _prenorm_fwd(x, pre_g, links.token)
    w_pad, w_pad_t, wuq_pad, wk_pad, wv, w_pm, w_pd, w_out = _full_weights(links.weights((h, mc, ms, dc, ds)))

    p = _matmul(h, w_pad, "nn", F32, 1024, 1408, 1024, "in_proj")
    q, k, v = _mla_prep_fwd(p, q_g, kv_g, wuq_pad, wk_pad, wv, mc, ms)
    ya, lse_m = _mla_flash_fwd(q, k, v)
    qkv = [_dil_prep_fwd(p, dc, ds, g) for g in range(3)]
    o_g, l_g = zip(*[_dil_attn_fwd(qkv[g], g) for g in range(3)])
    (dp, dy, dya, dyd, yd, lse_d, loss_cols, dg_post, dwpm, dwpd, dwout) = _tail(
        p, ya, o_g, l_g, x, target, w_pm, w_pd, w_out, post_g)

    for g in range(3):
        dp = _dil_attn_bwd(dp, qkv[g], dyd, yd, lse_d, dc, ds, g)
    dw_early = _matmul(dp, h, "tn", BF16, 1536, 1024, 2048, "dw_in_early", a_cols=(0, COL_LAT // 1536))
    token = links.send([_grad_parts_in_early(dw_early), _col_blocks(dwpm.astype(BF16)), _col_blocks(dwpd.astype(BF16)),
                        dwout.astype(BF16).reshape(8, 128, D_MODEL)], "exchange_early")

    dq, dk, dv = _mla_flash_bwd(q, k, v, ya, dya, lse_m, token)
    dp, dwuq_pad, dwk_pad, dwv, dg_q, dg_kv = _mla_prep_bwd(dp, p, dq, dk, dv, q_g, kv_g, wuq_pad, wk_pad, wv, mc, ms)
    dw_late = _matmul(dp, h, "tn", BF16, N_LAT, 1024, 2048, "dw_in_late", a_cols=(COL_LAT // N_LAT, 1))
    token = links.send([_grad_parts_in_late(dw_late)] + _grad_parts_mla(dwuq_pad, dwk_pad, dwv), "exchange_late")
    early = links.collect("exchange_early", dw_late, ("in_early", "pm", "pd", "out"))

    grad_x, dg_pre = _dh_prenorm_bwd(dp, w_pad_t, x, dy, pre_g, (token,) + tuple(early))
    links.collect("exchange_late", grad_x, ("in_late", "uq", "ukv"))

    loss_part = jnp.pad((jnp.sum(loss_cols) * (0.5 / D_MODEL)).reshape(1, 1), ((0, 0), (0, N_GVEC - N_GAINS - 1)))
    gvec = jnp.concatenate([dg_pre, dg_q, dg_kv, dg_post, loss_part], axis=1)
    return grad_x, gvec


def kernel(x, positions, pre_norm_g, w_in, q_norm_g, w_uq, kv_norm_g, w_ukv, w_proj_mla, w_proj_dil, w_out, post_norm_g, loss_target, m_pre_norm_g, m_w_in, m_q_norm_g, m_w_uq, m_kv_norm_g, m_w_ukv, m_w_proj_mla, m_w_proj_dil, m_w_out, m_post_norm_g, v_pre_norm_g, v_w_in, v_q_norm_g, v_w_uq, v_kv_norm_g, v_w_ukv, v_w_proj_mla, v_w_proj_dil, v_w_out, v_post_norm_g):
    xi, yi, ci = _my_place()
    chip, me = 2 * xi + yi, 4 * xi + 2 * yi + ci
    mats = [jnp.swapaxes(w_in, 1, 2)] + [w_uq, w_ukv, w_proj_mla, w_proj_dil, w_out]
    mats = [w.reshape(w.shape[1:]).astype(BF16) for w in mats]
    links = _Links(mats, chip, me)
    gains = (pre_norm_g, q_norm_g, kv_norm_g, post_norm_g)
    grad_x, gvec = _device_grads(x[0], positions, loss_target[0], gains, links)

    sums = links.sums
    in_e = sums["in_early"]
    half_in = jnp.concatenate([in_e[:W_IN_LAT] + jnp.where(chip == 0, sums["in_late"], 0.0), in_e[W_IN_LAT:]], axis=0)
    halves = [half_in, sums["uq"], sums["ukv"], sums["pm"], sums["pd"], sums["out"]]
    gvec8 = jnp.pad(gvec, ((0, 7), (0, 0)))
    swapped, recv_gains = _swap_halves(halves, gvec8)
    g_gains = _sum_parts(recv_gains, gvec8, me, 8, "sum_gain_parts")[0:1]
    loss = g_gains[0, N_GAINS]
    sw = lambda a: jnp.swapaxes(a, 1, 2)
    d_in, m_in, v_in, g_in = [sw(o) for o in _adamw_in(sw(w_in), sw(m_w_in), sw(v_w_in), half_in, swapped[0], ci)]
    g_mats = [g_in] + [_set_slot(s, hf, ci).reshape((1,) + shp)
                       for s, hf, shp in zip(swapped[1:], halves[1:], SHARD_SHAPES[1:])]

    off = [0, 1024, 1408, 1664, 2688]
    g_gain = [g_gains[:, off[i]:off[i + 1]] for i in range(4)]
    grads = [g_gain[0], g_mats[0], g_gain[1], g_mats[1], g_gain[2], g_mats[2], g_mats[3], g_mats[4], g_mats[5], g_gain[3]]
    ws = [pre_norm_g, w_in, q_norm_g, w_uq, kv_norm_g, w_ukv, w_proj_mla, w_proj_dil, w_out, post_norm_g]
    ms = [m_pre_norm_g, m_w_in, m_q_norm_g, m_w_uq, m_kv_norm_g, m_w_ukv, m_w_proj_mla, m_w_proj_dil, m_w_out, m_post_norm_g]
    vs = [v_pre_norm_g, v_w_in, v_q_norm_g, v_w_uq, v_kv_norm_g, v_w_ukv, v_w_proj_mla, v_w_proj_dil, v_w_out, v_post_norm_g]
    deltas, new_m, new_v = [], [], []
    for i, (w, g, m, v) in enumerate(zip(ws, grads, ms, vs)):
        if w is w_in:
            d_, m_, v_ = d_in, m_in, v_in
        elif w.shape[-1] % 128 and w.shape[-2] % 128 == 0:
            g = jnp.swapaxes(g, 1, 2)
            grads[i] = jnp.swapaxes(g, 1, 2)
            d_, m_, v_ = [jnp.swapaxes(o, 1, 2) for o in
                          _adamw(jnp.swapaxes(w, 1, 2), g, jnp.swapaxes(m, 1, 2), jnp.swapaxes(v, 1, 2), f"adamw_{i}")]
        else:
            d_, m_, v_ = _adamw(w, g, m, v, f"adamw_{i}")
        deltas.append(d_)
        new_m.append(m_)
        new_v.append(v_)
    return (loss, grad_x.reshape(x.shape), *grads, *deltas, *new_m, *new_v)
```

```python
import jax
import jax.numpy as jnp
from jax import lax
from jax.experimental import pallas as pl
from jax.experimental.pallas import tpu as pltpu

F32 = jnp.float32
BF16 = jnp.bfloat16

SEQ = 4096
D_MODEL = 1024
EPS = 1e-6
ROPE_THETA = 500000.0
MLA_HEADS = 8
Q_RANK = 384
KV_RANK = 256
MLA_SCALE = 96.0 ** -0.5
MLA_ROPE_HALF = 16
DIL_DILATIONS = (1, 4, 16)
DIL_ROPE_HALF = 8
DIL_SCALE = 0.125
BAND = 128

N_LAT = 768
COL_Z, COL_QKV, COL_LAT = 2048, 3072, 7680
N_PAD = 8448


def _qkv_block(tq, g, pr):
    return COL_QKV // 128 + (g * 4 + pr) * 3 + tq

IN_SPLITS = (384, 256, 32, 4608, 512, 512, 1024, 1024)

SHARD_SHAPES = ((2088, 1024), (384, 192), (256, 256), (512, 256), (512, 256), (256, 1024))
N_MATS = len(SHARD_SHAPES)
N_GAINS = 2688
N_GVEC = N_GAINS + 128

ADAM_LR, ADAM_B1, ADAM_B2, ADAM_EPS, ADAM_WD, ADAM_STEP = 0.001, 0.9, 0.999, 1e-08, 0.01, 10

VMEM_LIMIT = 56 * 1024 * 1024
NEG = -1e30
MESH = pl.DeviceIdType.MESH


def _cparams(**kw):
    return pltpu.CompilerParams(vmem_limit_bytes=VMEM_LIMIT, **kw)


def _dot(a, b, dims):
    return lax.dot_general(a, b, (dims, ((), ())), preferred_element_type=F32)


def _nn(a, b):
    return _dot(a, b, ((1,), (0,)))


def _nt(a, b):
    return _dot(a, b, ((1,), (1,)))


def _tn(a, b):
    return _dot(a, b, ((0,), (0,)))


def _rope_lanes(shape, half, period, first):
    lane = lax.broadcasted_iota(jnp.int32, shape, len(shape) - 1) % period
    return (lane >= first) & (lane < first + half), (lane >= first + half) & (lane < first + 2 * half)


def _rope_fwd(x, c, s, half, lanes):
    x1, _ = lanes
    return x * c + jnp.where(x1, pltpu.roll(x, 128 - half, 1), pltpu.roll(x, half, 1)) * s


def _rope_bwd(g, c, s, half, lanes):
    x1, x2 = lanes
    gs = g * s
    return g * c + jnp.where(x2, pltpu.roll(gs, half, 1), jnp.where(x1, pltpu.roll(gs, 128 - half, 1), 0.0))


def _sigmoid(x):
    return 1.0 / (1.0 + jnp.exp(-x))


def _after(token):
    tokens = [t for t in (token if isinstance(token, (tuple, list)) else [token]) if t is not None]
    return tokens, [pl.BlockSpec(memory_space=pl.ANY)] * len(tokens)


def _matmul(a, b, mode, out_dtype, tm, tn, tk, name, token=None, b_cols=None, a_cols=None, lane_blocks=False):
    after, after_specs = _after(token)
    if mode == "nn":
        (m, k), n = a.shape, b.shape[1]
        first = 0
        if b_cols is not None:
            first, n = b_cols[0], b_cols[1] * tn
        a_spec = pl.BlockSpec((tm, tk), lambda j, i, kk: (i, kk))
        b_spec = pl.BlockSpec((tk, tn), lambda j, i, kk: (kk, j + first))
        dot = _nn
    elif mode == "nt":
        (m, k), n = a.shape, b.shape[0]
        a_spec = pl.BlockSpec((tm, tk), lambda j, i, kk: (i, kk))
        b_spec = pl.BlockSpec((tn, tk), lambda j, i, kk: (j, kk))
        dot = _nt
    else:
        (k, m), n = a.shape, b.shape[1]
        first = 0
        if a_cols is not None:
            first, m = a_cols[0], a_cols[1] * tm
        a_spec = pl.BlockSpec((tk, tm), lambda j, i, kk: (kk, i + first))
        b_spec = pl.BlockSpec((tk, tn), lambda j, i, kk: (kk, j))
        dot = _tn
    assert m % tm == 0 and n % tn == 0 and k % tk == 0, (name, m, n, k, tm, tn, tk)
    nk = k // tk

    def body(a_ref, b_ref, *rest):
        o_ref, acc_ref = rest[-2:]
        kk = pl.program_id(2)
        part = dot(a_ref[...], b_ref[...])

        @pl.when(kk == 0)
        def _():
            acc_ref[...] = part

        @pl.when(kk > 0)
        def _():
            acc_ref[...] += part

        @pl.when(kk == nk - 1)
        def _():
            if lane_blocks:
                for blk in range(tn // 128):
                    o_ref[blk] = acc_ref[:, blk * 128:(blk + 1) * 128].astype(o_ref.dtype)
            else:
                o_ref[...] = acc_ref[...].astype(o_ref.dtype)

    if lane_blocks:
        out_spec = pl.BlockSpec((tn // 128, tm, 128), lambda j, i, kk: (j, i, 0))
        out_shape = jax.ShapeDtypeStruct((n // 128, m, 128), out_dtype)
    else:
        out_spec = pl.BlockSpec((tm, tn), lambda j, i, kk: (i, j))
        out_shape = jax.ShapeDtypeStruct((m, n), out_dtype)
    return pl.pallas_call(
        body, name=name, grid=(n // tn, m // tm, nk),
        in_specs=[a_spec, b_spec] + after_specs,
        out_specs=out_spec, out_shape=out_shape,
        scratch_shapes=[pltpu.VMEM((tm, tn), F32)],
        compiler_params=_cparams(),
    )(a, b, *after)


def _prenorm_fwd(x, g, token=None):
    tm = 512
    after, after_specs = _after(token)

    def body(x_ref, g_ref, *rest):
        xv = x_ref[...]
        r = lax.rsqrt(jnp.mean(xv * xv, axis=-1, keepdims=True) + EPS)
        rest[-1][...] = (xv * r * g_ref[...]).astype(BF16)

    return pl.pallas_call(
        body, name="prenorm_fwd", grid=(SEQ // tm,),
        in_specs=[pl.BlockSpec((tm, D_MODEL), lambda i: (i, 0)), pl.BlockSpec((1, D_MODEL), lambda i: (0, 0))] + after_specs,
        out_specs=pl.BlockSpec((tm, D_MODEL), lambda i: (i, 0)),
        out_shape=jax.ShapeDtypeStruct((SEQ, D_MODEL), BF16),
    )(x, g, *after)


def _dh_prenorm_bwd(dp, w_pad_t, x, dy, g, token=None):
    tm, tk = 1024, 1408
    nk = N_PAD // tk
    after, after_specs = _after(token)

    def body(a_ref, b_ref, x_ref, dy_ref, g_ref, *rest):
        gx_ref, dg_ref, acc_ref = rest[-3:]
        i, kk = pl.program_id(0), pl.program_id(1)
        part = _nn(a_ref[...], b_ref[...])

        @pl.when(kk == 0)
        def _():
            acc_ref[...] = part

        @pl.when(kk > 0)
        def _():
            acc_ref[...] += part

        @pl.when(kk == nk - 1)
        def _():
            xv = x_ref[...]
            r = lax.rsqrt(jnp.mean(xv * xv, axis=-1, keepdims=True) + EPS)
            n = xv * r
            dhv = acc_ref[...]
            dn = dhv * g_ref[...]
            gx_ref[...] = dy_ref[...] + r * (dn - n * jnp.mean(dn * n, axis=-1, keepdims=True))
            cols = jnp.sum(dhv * n, axis=0, keepdims=True)

            @pl.when(i == 0)
            def _():
                dg_ref[...] = cols

            @pl.when(i > 0)
            def _():
                dg_ref[...] += cols

    row = pl.BlockSpec((tm, D_MODEL), lambda i, kk: (i, 0))
    vec = pl.BlockSpec((1, D_MODEL), lambda i, kk: (0, 0))
    return pl.pallas_call(
        body, name="dh_prenorm_bwd", grid=(SEQ // tm, nk),
        in_specs=[pl.BlockSpec((tm, tk), lambda i, kk: (i, kk)), pl.BlockSpec((tk, D_MODEL), lambda i, kk: (kk, 0)),
                  row, row, vec] + after_specs,
        out_specs=[row, vec],
        out_shape=[jax.ShapeDtypeStruct((SEQ, D_MODEL), F32), jax.ShapeDtypeStruct((1, D_MODEL), F32)],
        scratch_shapes=[pltpu.VMEM((tm, D_MODEL), F32)],
        compiler_params=_cparams(),
    )(dp, w_pad_t, x, dy, g, *after)


def _mla_prep_fwd(p, qg, kvg, wuq, wk, wv, rc, rs):
    tm = 512

    def body(lat_ref, qg_ref, kvg_ref, wuq_ref, wk_ref, wv_ref, c_ref, s_ref, q_ref, k_ref, v_ref):
        c, s = c_ref[...], s_ref[...]
        lanes = _rope_lanes((tm, 128), MLA_ROPE_HALF, 128, 64)
        cq = lat_ref[:, 0:Q_RANK]
        r1 = lax.rsqrt(jnp.mean(cq * cq, axis=-1, keepdims=True) + EPS)
        cqn = (cq * r1 * qg_ref[...]).astype(BF16)
        q = _nn(cqn, wuq_ref[...])
        for h in range(MLA_HEADS):
            sl = slice(h * 128, (h + 1) * 128)
            q_ref[:, sl] = (_rope_fwd(q[:, sl], c, s, MLA_ROPE_HALF, lanes) * MLA_SCALE).astype(BF16)
        ckv = lat_ref[:, Q_RANK:Q_RANK + KV_RANK]
        r2 = lax.rsqrt(jnp.mean(ckv * ckv, axis=-1, keepdims=True) + EPS)
        ckvn = (ckv * r2 * kvg_ref[...]).astype(BF16)
        krr = _rope_fwd(lat_ref[:, Q_RANK + KV_RANK:N_LAT], c, s, MLA_ROPE_HALF, lanes)
        kn = _nn(ckvn, wk_ref[...])
        for h in range(MLA_HEADS):
            sl = slice(h * 128, (h + 1) * 128)
            k_ref[:, sl] = (kn[:, sl] + krr).astype(BF16)
        v_ref[...] = _nn(ckvn, wv_ref[...]).astype(BF16)

    def full(shape):
        return pl.BlockSpec(shape, lambda i: (0, 0))

    def rows(w):
        return pl.BlockSpec((tm, w), lambda i: (i, 0))

    return pl.pallas_call(
        body, name="mla_prep_fwd", grid=(SEQ // tm,),
        in_specs=[pl.BlockSpec((tm, N_LAT), lambda i: (i, 0)),
                  full((1, Q_RANK)), full((1, KV_RANK)), full((Q_RANK, 1024)), full((KV_RANK, 1024)),
                  full((KV_RANK, 512)), rows(128), rows(128)],
        out_specs=[rows(1024), rows(1024), rows(512)],
        out_shape=[jax.ShapeDtypeStruct((SEQ, 1024), BF16), jax.ShapeDtypeStruct((SEQ, 1024), BF16),
                   jax.ShapeDtypeStruct((SEQ, 512), BF16)],
        compiler_params=_cparams(),
    )(p, qg, kvg, wuq, wk, wv, rc, rs)


def _mla_prep_bwd(dp_in, p, dq, dk, dv, qg, kvg, wuq, wk, wv, rc, rs):
    tm = 512

    def body(dp_any, lat_ref, dq_ref, dk_ref, dv_ref, qg_ref, kvg_ref, wuq_ref, wk_ref, wv_ref,
             c_ref, s_ref, dp_ref, dwuq_ref, dwk_ref, dwv_ref, dgq_ref, dgkv_ref, dqb_ref, dkb_ref):
        del dp_any
        c, s = c_ref[...], s_ref[...]
        lanes = _rope_lanes((tm, 128), MLA_ROPE_HALF, 128, 64)
        lane = lax.broadcasted_iota(jnp.int32, (tm, 128), 1)
        dkr = jnp.zeros((tm, 128), F32)
        for h in range(MLA_HEADS):
            sl = slice(h * 128, (h + 1) * 128)
            dqb_ref[:, sl] = _rope_bwd(dq_ref[:, sl] * MLA_SCALE, c, s, MLA_ROPE_HALF, lanes).astype(BF16)
            dkh = dk_ref[:, sl]
            dkr = dkr + dkh
            dkb_ref[:, sl] = jnp.where(lane < 64, dkh, 0.0).astype(BF16)
        dkr = jnp.where((lane >= 64) & (lane < 96), dkr, 0.0)
        dkr = _rope_bwd(dkr, c, s, MLA_ROPE_HALF, lanes)
        dvb = dv_ref[...].astype(BF16)

        cq = lat_ref[:, 0:Q_RANK]
        r1 = lax.rsqrt(jnp.mean(cq * cq, axis=-1, keepdims=True) + EPS)
        n1 = cq * r1
        dcqn = _nt(dqb_ref[...], wuq_ref[...])
        dn1 = dcqn * qg_ref[...]
        dcq = r1 * (dn1 - n1 * jnp.mean(dn1 * n1, axis=-1, keepdims=True))
        pq = jnp.sum(dcqn * n1, axis=0, keepdims=True)

        ckv = lat_ref[:, Q_RANK:Q_RANK + KV_RANK]
        r2 = lax.rsqrt(jnp.mean(ckv * ckv, axis=-1, keepdims=True) + EPS)
        n2 = ckv * r2
        dckvn = _nt(dkb_ref[...], wk_ref[...]) + _nt(dvb, wv_ref[...])
        dn2 = dckvn * kvg_ref[...]
        dckv = r2 * (dn2 - n2 * jnp.mean(dn2 * n2, axis=-1, keepdims=True))
        pkv = jnp.sum(dckvn * n2, axis=0, keepdims=True)
        cqn = (n1 * qg_ref[...]).astype(BF16)
        ckvn = (n2 * kvg_ref[...]).astype(BF16)
        wq, wk_, wv_ = _tn(cqn, dqb_ref[...]), _tn(ckvn, dkb_ref[...]), _tn(ckvn, dvb)

        dp_ref[:, 0:Q_RANK] = dcq.astype(BF16)
        dp_ref[:, Q_RANK:Q_RANK + KV_RANK] = dckv.astype(BF16)
        dp_ref[:, Q_RANK + KV_RANK:N_LAT] = dkr.astype(BF16)

        @pl.when(pl.program_id(0) == 0)
        def _():
            dgq_ref[...] = pq
            dgkv_ref[...] = pkv
            dwuq_ref[...] = wq
            dwk_ref[...] = wk_
            dwv_ref[...] = wv_

        @pl.when(pl.program_id(0) > 0)
        def _():
            dgq_ref[...] += pq
            dgkv_ref[...] += pkv
            dwuq_ref[...] += wq
            dwk_ref[...] += wk_
            dwv_ref[...] += wv_

    def full(shape):
        return pl.BlockSpec(shape, lambda i: (0, 0))

    def rows(w):
        return pl.BlockSpec((tm, w), lambda i: (i, 0))

    lat = pl.BlockSpec((tm, N_LAT), lambda i: (i, 0))
    dlat = pl.BlockSpec((tm, N_LAT), lambda i: (i, COL_LAT // N_LAT))
    return pl.pallas_call(
        body, name="mla_prep_bwd", grid=(SEQ // tm,),
        in_specs=[pl.BlockSpec(memory_space=pl.ANY), lat, rows(1024), rows(1024), rows(512),
                  full((1, Q_RANK)), full((1, KV_RANK)), full((Q_RANK, 1024)), full((KV_RANK, 1024)),
                  full((KV_RANK, 512)), rows(128), rows(128)],
        out_specs=[dlat, full((Q_RANK, 1024)), full((KV_RANK, 1024)), full((KV_RANK, 512)),
                   full((1, Q_RANK)), full((1, KV_RANK))],
        out_shape=[jax.ShapeDtypeStruct((SEQ, N_PAD), BF16), jax.ShapeDtypeStruct((Q_RANK, 1024), F32),
                   jax.ShapeDtypeStruct((KV_RANK, 1024), F32), jax.ShapeDtypeStruct((KV_RANK, 512), F32),
                   jax.ShapeDtypeStruct((1, Q_RANK), F32), jax.ShapeDtypeStruct((1, KV_RANK), F32)],
        input_output_aliases={0: 0},
        scratch_shapes=[pltpu.VMEM((tm, 1024), BF16), pltpu.VMEM((tm, 1024), BF16)],
        compiler_params=_cparams(),
    )(dp_in, p, dq, dk, dv, qg, kvg, wuq, wk, wv, rc, rs)


FLASH_T = 1024


def _head_half(shape, hh):
    lane = lax.broadcasted_iota(jnp.int32, shape, 1)
    return (lane < 64) if hh == 0 else (lane >= 64)


def _diag_keep(nr, nk):
    row = lax.broadcasted_iota(jnp.int32, (nr, nk), 0)
    col = lax.broadcasted_iota(jnp.int32, (nr, nk), 1)
    return row + (nk - nr) >= col


def _tri_steps(nb, q_major):
    if q_major:
        pairs = [(i, kb) for i in range(nb) for kb in range(i + 1)]
    else:
        pairs = [(i, kb) for kb in range(nb) for i in range(kb, nb)]
    return jnp.asarray([p[0] for p in pairs], jnp.int32), jnp.asarray([p[1] for p in pairs], jnp.int32)


def _mla_flash_fwd(q, k, v):
    t = FLASH_T
    nb = SEQ // t
    qtab, ktab = _tri_steps(nb, True)

    def body(qi_ref, ki_ref, q_ref, k_ref, v_ref, o_ref, lse_ref, m_scr, l_scr, acc_scr):
        step = pl.program_id(1)
        i, kb = qi_ref[step], ki_ref[step]

        @pl.when(kb == 0)
        def _():
            m_scr[...] = jnp.full_like(m_scr, NEG)
            l_scr[...] = jnp.zeros_like(l_scr)
            acc_scr[...] = jnp.zeros_like(acc_scr)

        def update(r0, nr, nk, diagonal):
            rs = slice(r0, r0 + nr)
            vv = v_ref[0:nk, :]
            for hh in range(2):
                sl = slice(hh * 128, (hh + 1) * 128)
                s = _nt(q_ref[rs, sl], k_ref[0:nk, sl])
                if diagonal:
                    s = jnp.where(_diag_keep(nr, nk), s, NEG)
                m_prev = m_scr[hh, rs, :]
                m_new = jnp.maximum(m_prev, jnp.max(s, axis=-1, keepdims=True))
                pr = jnp.exp(s - jnp.tile(m_new, (1, nk // 128)))
                alpha = jnp.exp(m_prev - m_new)
                l_scr[hh, rs, :] = alpha * l_scr[hh, rs, :] + jnp.sum(pr, axis=-1, keepdims=True)
                acc_scr[hh, rs, :] = alpha * acc_scr[hh, rs, :] + _nn(pr.astype(BF16), vv)
                m_scr[hh, rs, :] = m_new

        @pl.when(kb < i)
        def _():
            update(0, t, t, False)

        @pl.when(kb == i)
        def _():
            update(0, t // 2, t // 2, True)
            update(t // 2, t // 2, t, True)
            o0 = acc_scr[0] / l_scr[0]
            o1 = acc_scr[1] / l_scr[1]
            o_ref[...] = jnp.where(_head_half((t, 128), 0), o0, o1)
            for hh in range(2):
                lse_ref[:, hh * 128:(hh + 1) * 128] = m_scr[hh] + jnp.log(l_scr[hh])

    grid_spec = pltpu.PrefetchScalarGridSpec(
        num_scalar_prefetch=2, grid=(4, qtab.shape[0]),
        in_specs=[pl.BlockSpec((t, 256), lambda j, s, qi, ki: (qi[s], j)),
                  pl.BlockSpec((t, 256), lambda j, s, qi, ki: (ki[s], j)),
                  pl.BlockSpec((t, 128), lambda j, s, qi, ki: (ki[s], j))],
        out_specs=[pl.BlockSpec((t, 128), lambda j, s, qi, ki: (qi[s], j)),
                   pl.BlockSpec((t, 256), lambda j, s, qi, ki: (qi[s], j))],
        scratch_shapes=[pltpu.VMEM((2, t, 128), F32), pltpu.VMEM((2, t, 128), F32), pltpu.VMEM((2, t, 128), F32)])
    return pl.pallas_call(
        body, name="mla_flash_fwd", grid_spec=grid_spec,
        out_shape=[jax.ShapeDtypeStruct((SEQ, 512), F32), jax.ShapeDtypeStruct((SEQ, 1024), F32)],
        compiler_params=_cparams(),
    )(qtab, ktab, q, k, v)


def _mla_flash_bwd(q, k, v, o, do, lse, token=None):
    t = FLASH_T
    nb = SEQ // t
    qtab, ktab = _tri_steps(nb, False)
    after, after_specs = _after(token)

    def body(qi_ref, ki_ref, q_ref, k_ref, v_ref, o_ref, do_ref, lse_ref, *rest):
        dq_ref, dk_ref, dv_ref, dk_scr, dv_scr = rest[-5:]
        step = pl.program_id(1)
        i, kb = qi_ref[step], ki_ref[step]

        @pl.when(step == 0)
        def _():
            dq_ref[...] = jnp.zeros_like(dq_ref)

        @pl.when(i == kb)
        def _():
            dk_scr[...] = jnp.zeros_like(dk_scr)
            dv_scr[...] = jnp.zeros_like(dv_scr)

        def update(r0, nr, nk, diagonal):
            rs = slice(r0, r0 + nr)
            vv = v_ref[0:nk, :]
            ov = o_ref[rs, :]
            dov = do_ref[rs, :]
            rows = pl.ds(pl.multiple_of(i * t + r0, t // 2), nr)
            for hh in range(2):
                sl = slice(hh * 128, (hh + 1) * 128)
                qh, kh = q_ref[rs, sl], k_ref[0:nk, sl]
                s = _nt(qh, kh)
                if diagonal:
                    s = jnp.where(_diag_keep(nr, nk), s, NEG)
                pr = jnp.exp(s - jnp.tile(lse_ref[rs, sl], (1, nk // 128)))
                dom = jnp.where(_head_half((nr, 128), hh), dov, 0.0)
                domb = dom.astype(BF16)
                dv_scr[0:nk, :] += _tn(pr.astype(BF16), domb)
                dpr = _nt(domb, vv)
                delta = jnp.sum(dom * ov, axis=-1, keepdims=True)
                ds = (pr * (dpr - delta)).astype(BF16)
                dq_ref[rows, sl] += _nn(ds, kh)
                dk_scr[hh, 0:nk, :] += _tn(ds, qh)

        @pl.when(i > kb)
        def _():
            update(0, t, t, False)

        @pl.when(i == kb)
        def _():
            update(0, t // 2, t // 2, True)
            update(t // 2, t // 2, t, True)

        @pl.when(i == nb - 1)
        def _():
            dk_ref[:, 0:128] = dk_scr[0]
            dk_ref[:, 128:256] = dk_scr[1]
            dv_ref[...] = dv_scr[...]

    qi_map = lambda j, s, qi, ki: (qi[s], j)
    ki_map = lambda j, s, qi, ki: (ki[s], j)
    grid_spec = pltpu.PrefetchScalarGridSpec(
        num_scalar_prefetch=2, grid=(4, qtab.shape[0]),
        in_specs=[pl.BlockSpec((t, 256), qi_map), pl.BlockSpec((t, 256), ki_map), pl.BlockSpec((t, 128), ki_map),
                  pl.BlockSpec((t, 128), qi_map), pl.BlockSpec((t, 128), qi_map), pl.BlockSpec((t, 256), qi_map)]
        + after_specs,
        out_specs=[pl.BlockSpec((SEQ, 256), lambda j, s, qi, ki: (0, j)), pl.BlockSpec((t, 256), ki_map),
                   pl.BlockSpec((t, 128), ki_map)],
        scratch_shapes=[pltpu.VMEM((2, t, 128), F32), pltpu.VMEM((t, 128), F32)])
    return pl.pallas_call(
        body, name="mla_flash_bwd", grid_spec=grid_spec,
        out_shape=[jax.ShapeDtypeStruct((SEQ, 1024), F32), jax.ShapeDtypeStruct((SEQ, 1024), F32),
                   jax.ShapeDtypeStruct((SEQ, 512), F32)],
        compiler_params=_cparams(),
    )(qtab, ktab, q, k, v, o, do, lse, *after)


def _strided(start, size, d):
    return pl.ds(start, size) if d == 1 else pl.ds(start, size, stride=d)


def _dil_prep_fwd(p, rc, rs, g):
    d = DIL_DILATIONS[g]
    sub_len = SEQ // d
    ch = min(sub_len, 512)

    def body(p_ref, c_ref, s_ref, o_ref, x_scr):
        tq = pl.program_id(0)
        lanes = _rope_lanes((ch, 128), DIL_ROPE_HALF, 64, 0)
        o_ref[0, 0, 0:BAND, :] = jnp.zeros((BAND, 128), BF16)

        @pl.when(tq < 2)
        def _():
            mult = jnp.where(tq == 0, DIL_SCALE, 1.0).astype(F32)
            for c0 in range(0, SEQ, ch):
                rows = pl.ds(c0, ch)
                x_scr[rows, :] = _rope_fwd(p_ref[rows, :], c_ref[rows, :] * mult, s_ref[rows, :] * mult, DIL_ROPE_HALF, lanes)

        def gather(take):
            for r in range(d):
                for c0 in range(0, sub_len, ch):
                    at = BAND + r * sub_len + c0
                    o_ref[0, 0, at:at + ch, :] = take(_strided(r + c0 * d, ch, d)).astype(BF16)

        @pl.when(tq < 2)
        def _():
            gather(lambda rows: x_scr[rows, :])

        @pl.when(tq == 2)
        def _():
            gather(lambda rows: p_ref[rows, :])

    tab = pl.BlockSpec((SEQ, 128), lambda tq, pr: (0, 0))
    return pl.pallas_call(
        body, name=f"dil_prep_fwd_g{g}", grid=(3, 4),
        in_specs=[pl.BlockSpec((None, SEQ, 128), lambda tq, pr: (_qkv_block(tq, g, pr) - COL_QKV // 128, 0, 0)), tab, tab],
        out_specs=pl.BlockSpec((1, 1, BAND + SEQ, 128), lambda tq, pr: (tq, pr, 0, 0)),
        out_shape=jax.ShapeDtypeStruct((3, 4, BAND + SEQ, 128), BF16),
        scratch_shapes=[pltpu.VMEM((SEQ, 128), F32)],
        compiler_params=_cparams(),
    )(p, rc, rs)


DIL_ST_FWD, DIL_ST_BWD = 1024, 2048


def _band_keep(g, b, t, nb):
    nbs = SEQ // DIL_DILATIONS[g] // BAND
    row = lax.broadcasted_iota(jnp.int32, (BAND, 2 * BAND), 0)
    col = lax.broadcasted_iota(jnp.int32, (BAND, 2 * BAND), 1)
    cur = (col >= BAND) & (row >= col - BAND)
    prev = (col < BAND) & (col >= row)
    if nbs >= nb:
        if b > 0:
            return cur | prev
        return cur | (prev & ((t * nb) % nbs != 0))
    return cur | prev if b % nbs else cur


def _dil_tok(g, b, t, nb):
    d = DIL_DILATIONS[g]
    nbs = SEQ // d // BAND
    gb = t * nb + b
    return _strided((gb % nbs) * BAND * d + gb // nbs, BAND, d)


def _dil_attn_fwd(qkv, g):
    DIL_ST, DIL_NB = DIL_ST_FWD, DIL_ST_FWD // BAND

    def body(q_ref, k_ref, v_ref, o_ref, l_ref, s_scr, p_scr, o_scr):
        t = pl.program_id(1)
        base = t * DIL_ST
        half0 = _head_half((DIL_ST, 128), 0)
        lse_h = []
        for hh in range(2):
            half = _head_half((BAND, 128), hh)
            for b in range(DIL_NB):
                qv = q_ref[0, 0, pl.ds(pl.multiple_of(base + (b + 1) * BAND, BAND), BAND), :]
                k2 = k_ref[0, 0, pl.ds(pl.multiple_of(base + b * BAND, BAND), 2 * BAND), :]
                sb = _nt(jnp.where(half, qv, jnp.zeros_like(qv)), k2)
                s_scr[b * BAND:(b + 1) * BAND, :] = jnp.where(_band_keep(g, b, t, DIL_NB), sb, NEG)
            s = s_scr[...]
            m = jnp.max(s, axis=-1, keepdims=True)
            pr = jnp.exp(s - m)
            den = jnp.sum(pr, axis=-1, keepdims=True)
            p_scr[...] = pr.astype(BF16)
            for b in range(DIL_NB):
                v2 = v_ref[0, 0, pl.ds(pl.multiple_of(base + b * BAND, BAND), 2 * BAND), :]
                o_scr[hh, b * BAND:(b + 1) * BAND, :] = _nn(p_scr[b * BAND:(b + 1) * BAND, :], v2)
            o_scr[hh] = o_scr[hh] / den
            lse_h.append(m + jnp.log(den))
        out = jnp.where(half0, o_scr[0], o_scr[1])
        lse = jnp.where(half0, lse_h[0], lse_h[1])
        for b in range(DIL_NB):
            tok = _dil_tok(g, b, t, DIL_NB)
            o_ref[tok, :] = out[b * BAND:(b + 1) * BAND, :]
            l_ref[tok, :] = lse[b * BAND:(b + 1) * BAND, :]

    def inp(tq):
        return pl.BlockSpec((1, 1, BAND + SEQ, 128), lambda pr, t: (tq, pr, 0, 0))

    out = pl.BlockSpec((SEQ, 128), lambda pr, t: (0, pr))
    return pl.pallas_call(
        body, name=f"dil_attn_fwd_g{g}", grid=(4, SEQ // DIL_ST),
        in_specs=[inp(0), inp(1), inp(2)], out_specs=[out, out],
        out_shape=[jax.ShapeDtypeStruct((SEQ, 512), F32), jax.ShapeDtypeStruct((SEQ, 512), F32)],
        scratch_shapes=[pltpu.VMEM((DIL_ST, 2 * BAND), F32), pltpu.VMEM((DIL_ST, 2 * BAND), BF16),
                        pltpu.VMEM((2, DIL_ST, 128), F32)],
        compiler_params=_cparams(),
    )(qkv, qkv, qkv)


def _dil_attn_bwd(dp_in, qkv, dyd, yd, lse_all, rc, rs, g, token=None):
    d = DIL_DILATIONS[g]
    sub_len = SEQ // d
    DIL_ST, DIL_NB = DIL_ST_BWD, DIL_ST_BWD // BAND
    nst = SEQ // DIL_ST
    after, after_specs = _after(token)
    ch = 512

    def body(dp_any, q_ref, k_ref, v_ref, do_ref, y_ref, l_ref, c_ref, sn_ref, *rest):
        dp_ref, tok_scr, dk_scr, dv_scr, s_scr, dp_scr, p_scr, ds_scr, do_scr, y_scr, l_scr, dq_scr = rest[-12:]
        del dp_any
        t = pl.program_id(1)
        base = t * DIL_ST

        @pl.when(t == 0)
        def _():
            dk_scr[...] = jnp.zeros_like(dk_scr)
            dv_scr[...] = jnp.zeros_like(dv_scr)

        for b in range(DIL_NB):
            tok = _dil_tok(g, b, t, DIL_NB)
            do_scr[b * BAND:(b + 1) * BAND, :] = do_ref[tok, :]
            y_scr[b * BAND:(b + 1) * BAND, :] = y_ref[tok, :]
            l_scr[b * BAND:(b + 1) * BAND, :] = l_ref[tok, :]
        for hh in range(2):
            half = _head_half((BAND, 128), hh)
            half_st = _head_half((DIL_ST, 128), hh)
            dom = jnp.where(half_st, do_scr[...], 0.0)
            delta = jnp.sum(dom * y_scr[...], axis=-1, keepdims=True)
            lcol = jnp.max(jnp.where(half_st, l_scr[...], NEG), axis=-1, keepdims=True)
            for b in range(DIL_NB):
                rows = slice(b * BAND, (b + 1) * BAND)
                qv = q_ref[0, 0, pl.ds(pl.multiple_of(base + (b + 1) * BAND, BAND), BAND), :]
                band = pl.ds(pl.multiple_of(base + b * BAND, BAND), 2 * BAND)
                sb = _nt(jnp.where(half, qv, jnp.zeros_like(qv)), k_ref[0, 0, band, :])
                s_scr[rows, :] = jnp.where(_band_keep(g, b, t, DIL_NB), sb, NEG)
                dp_scr[rows, :] = _nt(dom[rows, :].astype(BF16), v_ref[0, 0, band, :])
            pr = jnp.exp(s_scr[...] - lcol)
            p_scr[...] = pr.astype(BF16)
            ds_scr[...] = (pr * (dp_scr[...] - delta)).astype(BF16)
            for b in range(DIL_NB):
                rows = slice(b * BAND, (b + 1) * BAND)
                qv = q_ref[0, 0, pl.ds(pl.multiple_of(base + (b + 1) * BAND, BAND), BAND), :]
                band = pl.ds(pl.multiple_of(base + b * BAND, BAND), 2 * BAND)
                dqb = jnp.where(half, _nn(ds_scr[rows, :], k_ref[0, 0, band, :]), 0.0)
                if hh == 0:
                    dq_scr[rows, :] = dqb
                else:
                    dq_scr[rows, :] += dqb
                half2 = _head_half((2 * BAND, 128), hh)
                dk_scr[band, :] += jnp.where(half2, _tn(ds_scr[rows, :], qv), 0.0)
                dv_scr[band, :] += _tn(p_scr[rows, :], dom[rows, :].astype(BF16))
        for b in range(DIL_NB):
            tok_scr[pl.ds(0, 1), _dil_tok(g, b, t, DIL_NB), :] = dq_scr[b * BAND:(b + 1) * BAND, :][None]

        @pl.when(t == nst - 1)
        def _():
            for r in range(d):
                rows = _strided(r, sub_len, d)
                tok_scr[pl.ds(1, 1), rows, :] = dk_scr[BAND + r * sub_len:BAND + (r + 1) * sub_len, :][None]
                tok_scr[pl.ds(2, 1), rows, :] = dv_scr[BAND + r * sub_len:BAND + (r + 1) * sub_len, :][None]
            lanes = _rope_lanes((ch, 128), DIL_ROPE_HALF, 64, 0)
            for c0 in range(0, SEQ, ch):
                rows = slice(c0, c0 + ch)
                cv, sv = c_ref[rows, :], sn_ref[rows, :]
                dp_ref[rows, 0:128] = _rope_bwd(tok_scr[0, rows, :], cv * DIL_SCALE, sv * DIL_SCALE, DIL_ROPE_HALF, lanes).astype(BF16)
                dp_ref[rows, 128:256] = _rope_bwd(tok_scr[1, rows, :], cv, sv, DIL_ROPE_HALF, lanes).astype(BF16)
                dp_ref[rows, 256:384] = tok_scr[2, rows, :].astype(BF16)

    def inp(tq):
        return pl.BlockSpec((1, 1, BAND + SEQ, 128), lambda pr, t: (tq, pr, 0, 0))

    tok_spec = pl.BlockSpec((SEQ, 128), lambda pr, t: (0, pr))
    tab = pl.BlockSpec((SEQ, 128), lambda pr, t: (0, 0))
    st = (DIL_ST, 2 * BAND)
    return pl.pallas_call(
        body, name=f"dil_attn_bwd_g{g}", grid=(4, nst),
        in_specs=[pl.BlockSpec(memory_space=pl.ANY), inp(0), inp(1), inp(2), tok_spec, tok_spec, tok_spec, tab, tab]
        + after_specs,
        out_specs=pl.BlockSpec((SEQ, 384), lambda pr, t: (0, _qkv_block(0, g, pr) // 3)),
        out_shape=jax.ShapeDtypeStruct((SEQ, N_PAD), BF16),
        input_output_aliases={0: 0},
        scratch_shapes=[pltpu.VMEM((3, SEQ, 128), F32),
                        pltpu.VMEM((BAND + SEQ, 128), F32), pltpu.VMEM((BAND + SEQ, 128), F32),
                        pltpu.VMEM(st, F32), pltpu.VMEM(st, F32), pltpu.VMEM(st, BF16), pltpu.VMEM(st, BF16),
                        pltpu.VMEM((DIL_ST, 128), F32), pltpu.VMEM((DIL_ST, 128), F32), pltpu.VMEM((DIL_ST, 128), F32),
                        pltpu.VMEM((DIL_ST, 128), F32)],
        compiler_params=_cparams(),
    )(dp_in, qkv, qkv, qkv, dyd, yd, lse_all, rc, rs, *after)


TAIL_T = 256


def _tail(p, ya, o_g, l_g, x, target, wpm, wpd, wout, post_g):
    tm = TAIL_T

    def body(pgz_ref, ya_ref, o0_ref, o1_ref, o2_ref, l0_ref, l1_ref, l2_ref, x_ref, t_ref,
             wpm_ref, wpd_ref, wout_ref, pg_ref,
             dp_ref, dy_ref, dya_ref, dyd_ref, yd_ref, lse_ref, loss_ref, dgp_ref, dwpm_ref, dwpd_ref, dwout_ref):
        l0, l1, l2 = l0_ref[...], l1_ref[...], l2_ref[...]
        mx = jnp.maximum(jnp.maximum(l0, l1), l2)
        e0, e1, e2 = jnp.exp(l0 - mx), jnp.exp(l1 - mx), jnp.exp(l2 - mx)
        den = e0 + e1 + e2
        yd = (e0 * o0_ref[...] + e1 * o1_ref[...] + e2 * o2_ref[...]) / den
        yd_ref[...] = yd
        lse_ref[...] = mx + jnp.log(den)
        ya = ya_ref[...]

        gm, gd = pgz_ref[:, 0:1024], pgz_ref[:, 1024:2048]
        zm, zd = pgz_ref[:, 2048:2560], pgz_ref[:, 2560:3072]
        szm, szd = _sigmoid(zm), _sigmoid(zd)
        sm, sd = zm * szm, zd * szd
        ua = (ya * sm).astype(BF16)
        ud = (yd * sd).astype(BF16)
        pa = _nn(ua, wpm_ref[...])
        pd = _nn(ud, wpd_ref[...])
        sgm, sgd = _sigmoid(gm), _sigmoid(gd)
        mg = (sgm * pa + sgd * pd).astype(BF16)
        t = _nn(mg, wout_ref[...])
        r3 = lax.rsqrt(jnp.mean(t * t, axis=-1, keepdims=True) + EPS)
        n = t * r3
        pg = pg_ref[...]
        err = x_ref[...] + n * pg - t_ref[...]
        lpart = jnp.sum(err * err, axis=0, keepdims=True)

        dy = err * (1.0 / D_MODEL)
        dy_ref[...] = dy
        gpart = jnp.sum(dy * n, axis=0, keepdims=True)
        dn = dy * pg
        dt = (r3 * (dn - n * jnp.mean(dn * n, axis=-1, keepdims=True))).astype(BF16)
        dmg = _nt(dt, wout_ref[...])
        dpa = (dmg * sgm).astype(BF16)
        dpd = (dmg * sgd).astype(BF16)
        dp_ref[:, 0:1024] = (dmg * pa * sgm * (1.0 - sgm)).astype(BF16)
        dp_ref[:, 1024:2048] = (dmg * pd * sgd * (1.0 - sgd)).astype(BF16)
        dua = _nt(dpa, wpm_ref[...])
        dud = _nt(dpd, wpd_ref[...])
        dya_ref[...] = dua * sm
        dyd_ref[...] = dud * sd
        dp_ref[:, 2048:2560] = (dua * ya * szm * (1.0 + zm * (1.0 - szm))).astype(BF16)
        dp_ref[:, 2560:3072] = (dud * yd * szd * (1.0 + zd * (1.0 - szd))).astype(BF16)

        wpm, wpd, wout = _tn(ua, dpa), _tn(ud, dpd), _tn(mg, dt)

        @pl.when(pl.program_id(0) == 0)
        def _():
            loss_ref[...] = lpart
            dgp_ref[...] = gpart
            dwpm_ref[...] = wpm
            dwpd_ref[...] = wpd
            dwout_ref[...] = wout

        @pl.when(pl.program_id(0) > 0)
        def _():
            loss_ref[...] += lpart
            dgp_ref[...] += gpart
            dwpm_ref[...] += wpm
            dwpd_ref[...] += wpd
            dwout_ref[...] += wout

    def rows(w):
        return pl.BlockSpec((tm, w), lambda i: (i, 0))

    def full(shape):
        return pl.BlockSpec(shape, lambda i: (0, 0))

    def sds(w, dt):
        return jax.ShapeDtypeStruct((SEQ, w), dt)

    return pl.pallas_call(
        body, name="tail", grid=(SEQ // tm,),
        in_specs=[rows(3072), rows(512), rows(512), rows(512), rows(512), rows(512), rows(512), rows(512),
                  rows(1024), rows(1024), full((512, 1024)), full((512, 1024)), full((1024, 1024)), full((1, 1024))],
        out_specs=[rows(3072), rows(1024), rows(512), rows(512), rows(512), rows(512), full((1, 1024)), full((1, 1024)),
                   full((512, 1024)), full((512, 1024)), full((1024, 1024))],
        out_shape=[sds(N_PAD, BF16), sds(1024, F32), sds(512, F32), sds(512, F32), sds(512, F32), sds(512, F32),
                   jax.ShapeDtypeStruct((1, 1024), F32), jax.ShapeDtypeStruct((1, 1024), F32),
                   jax.ShapeDtypeStruct((512, 1024), F32), jax.ShapeDtypeStruct((512, 1024), F32),
                   jax.ShapeDtypeStruct((1024, 1024), F32)],
        compiler_params=_cparams(),
    )(p, ya, o_g[0], o_g[1], o_g[2], l_g[0], l_g[1], l_g[2], x, target, wpm, wpd, wout, post_g)


def _sum_parts(recv, own, me, tr, name):
    n, r, w = recv.shape
    if r % tr:
        return _sum_parts_cols(recv, own, me, name)
    own_spec = (pl.BlockSpec((tr, w), lambda i, me_ref: (i, 0)) if own.ndim == 2
                else pl.BlockSpec((None, tr, w), lambda i, me_ref: (me_ref[0], i, 0)))

    def body(me_ref, p_ref, own_ref, o_ref):
        mine = own_ref[...].astype(F32)
        acc = jnp.zeros((tr, w), F32)
        for s in range(n):
            acc = acc + jnp.where(me_ref[0] == s, mine, p_ref[s].astype(F32))
        o_ref[...] = acc

    return pl.pallas_call(
        body, name=name,
        grid_spec=pltpu.PrefetchScalarGridSpec(
            num_scalar_prefetch=1, grid=(r // tr,),
            in_specs=[pl.BlockSpec((n, tr, w), lambda i, me_ref: (0, i, 0)), own_spec],
            out_specs=pl.BlockSpec((tr, w), lambda i, me_ref: (i, 0))),
        out_shape=jax.ShapeDtypeStruct((r, w), F32),
    )(me.reshape(1), recv, own)


def _sum_parts_cols(recv, own, me, name):
    n, r, w = recv.shape
    tc = 128

    def body(me_ref, p_ref, own_ref, o_ref):
        mine = own_ref[...].astype(F32)
        acc = jnp.zeros((r, tc), F32)
        for s in range(n):
            acc = acc + jnp.where(me_ref[0] == s, mine, p_ref[s].astype(F32))
        o_ref[...] = acc

    return pl.pallas_call(
        body, name=name,
        grid_spec=pltpu.PrefetchScalarGridSpec(
            num_scalar_prefetch=1, grid=(w // tc,),
            in_specs=[pl.BlockSpec((n, r, tc), lambda i, me_ref: (0, 0, i)),
                      pl.BlockSpec((None, r, tc), lambda i, me_ref: (me_ref[0], 0, i))],
            out_specs=pl.BlockSpec((r, tc), lambda i, me_ref: (0, i))),
        out_shape=jax.ShapeDtypeStruct((r, w), F32),
    )(me.reshape(1), recv, own)


def _adamw(w, g, m, v, name):
    lead = w.shape[:-2]
    r, c = w.shape[-2:]
    tr = max([t for t in range(8, 257, 8) if r % t == 0], default=r)
    c1 = 1.0 - ADAM_B1 ** ADAM_STEP
    c2 = 1.0 - ADAM_B2 ** ADAM_STEP

    def body(w_ref, g_ref, m_ref, v_ref, d_ref, nm_ref, nv_ref):
        gv = g_ref[...]
        nm = ADAM_B1 * m_ref[...] + (1.0 - ADAM_B1) * gv
        nv = ADAM_B2 * v_ref[...] + (1.0 - ADAM_B2) * (gv * gv)
        nm_ref[...] = nm
        nv_ref[...] = nv
        d_ref[...] = -ADAM_LR * ((nm / c1) / (jnp.sqrt(nv / c2) + ADAM_EPS) + ADAM_WD * w_ref[...])

    zeros = (0,) * len(lead)
    spec = pl.BlockSpec((1,) * len(lead) + (tr, c), lambda i: zeros + (i, 0))
    sd = jax.ShapeDtypeStruct(w.shape, F32)
    return pl.pallas_call(
        body, name=name, grid=(r // tr,),
        in_specs=[spec] * 4, out_specs=[spec] * 3, out_shape=[sd] * 3,
    )(w, g, m, v)


def _adamw_in(w_t, m_t, v_t, own_half, swapped, core):
    r, c = SHARD_SHAPES[0]
    tr = max(t for t in range(8, 257, 8) if r % t == 0)
    c1 = 1.0 - ADAM_B1 ** ADAM_STEP
    c2 = 1.0 - ADAM_B2 ** ADAM_STEP

    def body(core_ref, w_ref, m_ref, v_ref, own_ref, sw_ref, d_ref, nm_ref, nv_ref, g_ref):
        own = own_ref[...]
        col_half = lax.broadcasted_iota(jnp.int32, (tr, c), 1) // (c // 2)
        gv = jnp.where(col_half == core_ref[0], jnp.concatenate([own, own], axis=1), sw_ref[...])
        g_ref[0] = gv
        nm = ADAM_B1 * m_ref[0] + (1.0 - ADAM_B1) * gv
        nv = ADAM_B2 * v_ref[0] + (1.0 - ADAM_B2) * (gv * gv)
        nm_ref[0] = nm
        nv_ref[0] = nv
        d_ref[0] = -ADAM_LR * ((nm / c1) / (jnp.sqrt(nv / c2) + ADAM_EPS) + ADAM_WD * w_ref[0])

    full = pl.BlockSpec((1, tr, c), lambda i, core_ref: (0, i, 0))
    sd = jax.ShapeDtypeStruct((1, r, c), F32)
    return pl.pallas_call(
        body, name="adamw_in",
        grid_spec=pltpu.PrefetchScalarGridSpec(
            num_scalar_prefetch=1, grid=(r // tr,),
            in_specs=[full, full, full, pl.BlockSpec((tr, c // 2), lambda i, core_ref: (i, 0)),
                      pl.BlockSpec((tr, c), lambda i, core_ref: (i, 0))],
            out_specs=[full] * 4),
        out_shape=[sd] * 4,
    )(core.reshape(1), w_t, m_t, v_t, own_half, swapped)


ANY = pl.BlockSpec(memory_space=pl.ANY)


def _my_place():
    return lax.axis_index("x"), lax.axis_index("y"), lax.axis_index("c")


HBM = pl.BlockSpec(memory_space=pltpu.HBM)
SEM = pl.BlockSpec(memory_space=pltpu.SEMAPHORE)
DATAFLOW = pltpu.SideEffectType.DATAFLOW_SIDE_EFFECTING


def _near_chips(x, y):
    return [(1 - x, y), (x, 1 - y)]


def _half(mi, hc):
    r, c = SHARD_SHAPES[mi]
    if mi == 0:
        return pl.ds(0, r), pl.ds(pl.multiple_of(hc * (c // 2), 128), c // 2)
    return pl.ds(pl.multiple_of(hc * (r // 2), 16), r // 2), pl.ds(0, c)


def _gather_copies(m_refs, land_refs, send_sems, recv_sems):
    x, y, c = _my_place()
    out, back = [], []
    for mi in range(N_MATS):
        rows, cols = _half(mi, c)
        for j, (cx, cy) in enumerate(_near_chips(x, y)):
            sems = dict(send_sem=send_sems.at[mi * 2 + j], recv_sem=recv_sems.at[mi * 2 + j],
                        device_id=(cx, cy, c), device_id_type=MESH)
            out.append(pltpu.make_async_remote_copy(src_ref=m_refs[mi].at[rows, cols],
                                                    dst_ref=land_refs[mi].at[2 * x + y, rows, cols], **sems))
            got = land_refs[mi].at[2 * cx + cy, rows, cols]
            back.append(pltpu.make_async_remote_copy(src_ref=got, dst_ref=got, **sems))
    return out, back


def _gather_start(mats, landing):
    n = N_MATS

    def body(*refs):
        out, _ = _gather_copies(refs[:n], refs[n:2 * n], refs[2 * n], refs[2 * n + 1])
        for cp in out:
            cp.start()
        refs[-1][...] = jnp.zeros_like(refs[-1])

    hbm = [pltpu.HBM(a.shape, a.dtype) for a in list(mats) + list(landing)]
    outs = pl.pallas_call(
        body, name="gather_start",
        out_shape=(pltpu.SemaphoreType.DMA((2 * n,)), pltpu.SemaphoreType.DMA((2 * n,)), *hbm,
                   jax.ShapeDtypeStruct((8, 128), F32)),
        in_specs=[HBM] * (2 * n), out_specs=(SEM, SEM, *[HBM] * (2 * n), pl.BlockSpec(memory_space=pltpu.VMEM)),
        input_output_aliases={i: 2 + i for i in range(2 * n)},
        compiler_params=pltpu.CompilerParams(has_side_effects=DATAFLOW),
    )(*[pltpu.with_memory_space_constraint(a, pltpu.HBM) for a in list(mats) + list(landing)])
    return outs[:-1], outs[-1]


def _gather_wait(handle, after):
    n = N_MATS

    def body(*refs):
        out, back = _gather_copies(refs[:n], refs[n:2 * n], refs[2 * n], refs[2 * n + 1])
        for cp, arrival in zip(out, back):
            cp.wait_send()
            arrival.wait_recv()

    bufs = handle[2:]
    after, after_specs = _after(after)
    res = pl.pallas_call(
        body, name="gather_wait", out_shape=tuple(pltpu.HBM(b.shape, b.dtype) for b in bufs),
        in_specs=[HBM] * (2 * n) + [SEM, SEM] + after_specs, out_specs=tuple([HBM] * (2 * n)),
        input_output_aliases={i: i for i in range(2 * n)},
        compiler_params=pltpu.CompilerParams(has_side_effects=DATAFLOW),
    )(*bufs, handle[0], handle[1], *after)
    return list(res[n:])


def _relay_share(gathered):
    n = N_MATS

    def body(*refs):
        out_refs = refs[n:2 * n]
        send_sems, recv_sems = refs[2 * n:]
        x, y, c = _my_place()
        sibling = (x, y, 1 - c)
        relayed = 2 * (x ^ (1 - c)) + (y ^ c)
        relay_to = (x ^ c, y ^ (1 - c), c)
        far = 2 * (1 - x) + (1 - y)
        near = [2 * (1 - x) + y, 2 * x + (1 - y)]

        def copy(k, mi, shard, hc, to):
            blk = out_refs[mi].at[(shard,) + _half(mi, hc)]
            return pltpu.make_async_remote_copy(src_ref=blk, dst_ref=blk, send_sem=send_sems.at[mi * 4 + k],
                                                recv_sem=recv_sems.at[mi * 4 + k], device_id=to, device_id_type=MESH)

        sends = []
        for mi in range(n):
            sends.append(copy(0, mi, relayed, c, relay_to))
            sends += [copy(1 + j, mi, near[j], c, sibling) for j in range(2)]
        for cp in sends:
            cp.start()
        for mi in range(n):
            copy(0, mi, far, c, relay_to).wait_recv()
            cp = copy(3, mi, far, c, sibling)
            cp.start()
            sends.append(cp)
        for mi in range(n):
            for j in range(2):
                copy(1 + j, mi, near[j], 1 - c, sibling).wait_recv()
            copy(3, mi, far, 1 - c, sibling).wait_recv()
        for cp in sends:
            cp.wait_send()

    return pl.pallas_call(
        body, name="relay_share",
        in_specs=[ANY] * n, out_specs=[ANY] * n,
        out_shape=[jax.ShapeDtypeStruct(g.shape, g.dtype) for g in gathered],
        input_output_aliases={i: i for i in range(n)},
        scratch_shapes=[pltpu.SemaphoreType.DMA((4 * n,)), pltpu.SemaphoreType.DMA((4 * n,))],
    )(*gathered)


def _peers(x, y, c):
    out = []
    for k in range(1, 8):
        px, py, pc = x ^ (k >> 2), y ^ ((k >> 1) & 1), c ^ (k & 1)
        out.append((k - 1, (px, py, pc), 4 * px + 2 * py + pc))
    return out


def _exchange_start(parts, name):
    n = len(parts)

    def body(*refs):
        p_refs, land_refs = refs[:n], refs[n:2 * n]
        send_sems, recv_sems, token = refs[2 * n], refs[2 * n + 1], refs[-1]
        x, y, c = _my_place()
        me = 4 * x + 2 * y + c
        for k, dev, peer in _peers(x, y, c):
            for mi in range(n):
                pltpu.make_async_remote_copy(
                    src_ref=p_refs[mi].at[peer], dst_ref=land_refs[mi].at[me], send_sem=send_sems.at[k * n + mi],
                    recv_sem=recv_sems.at[k * n + mi], device_id=dev, device_id_type=MESH).start()
        token[...] = jnp.zeros_like(token)

    hbm = [pltpu.HBM(p.shape, p.dtype) for p in parts]
    outs = pl.pallas_call(
        body, name=name + "_start",
        out_shape=(pltpu.SemaphoreType.DMA((7 * n,)), pltpu.SemaphoreType.DMA((7 * n,)), *hbm, *hbm,
                   jax.ShapeDtypeStruct((8, 128), F32)),
        in_specs=[HBM] * (2 * n), out_specs=(SEM, SEM, *[HBM] * (2 * n), pl.BlockSpec(memory_space=pltpu.VMEM)),
        input_output_aliases={i: 2 + i for i in range(2 * n)},
        compiler_params=pltpu.CompilerParams(has_side_effects=DATAFLOW),
    )(*[pltpu.with_memory_space_constraint(p, pltpu.HBM) for p in parts],
      *[pltpu.with_memory_space_constraint(lax.empty(p.shape, p.dtype), pltpu.HBM) for p in parts])
    return (name, outs[:-1]), outs[-1]


def _exchange_wait(handle, after):
    name, outs = handle
    n = (len(outs) - 2) // 2

    def body(*refs):
        p_refs, land_refs = refs[:n], refs[n:2 * n]
        send_sems, recv_sems = refs[2 * n], refs[2 * n + 1]
        x, y, c = _my_place()
        me = 4 * x + 2 * y + c
        for k, dev, peer in _peers(x, y, c):
            for mi in range(n):
                pltpu.make_async_remote_copy(
                    src_ref=p_refs[mi].at[peer], dst_ref=land_refs[mi].at[me], send_sem=send_sems.at[k * n + mi],
                    recv_sem=recv_sems.at[k * n + mi], device_id=dev, device_id_type=MESH).wait_send()
                slot = land_refs[mi].at[peer]
                pltpu.make_async_remote_copy(
                    src_ref=slot, dst_ref=slot, send_sem=send_sems.at[k * n + mi],
                    recv_sem=recv_sems.at[k * n + mi], device_id=dev, device_id_type=MESH).wait_recv()

    bufs = outs[2:]
    res = pl.pallas_call(
        body, name=name + "_wait", out_shape=tuple(pltpu.HBM(b.shape, b.dtype) for b in bufs),
        in_specs=[HBM] * (2 * n) + [SEM, SEM, ANY], out_specs=tuple([HBM] * (2 * n)),
        input_output_aliases={i: i for i in range(2 * n)},
        compiler_params=pltpu.CompilerParams(has_side_effects=DATAFLOW),
    )(*bufs, outs[0], outs[1], after)
    return list(res[n:])


def _swap_halves(halves, gvec):
    def place(ref, mi, hc):
        return ref.at[:, pl.ds(pl.multiple_of(hc * 512, 128), 512)] if mi == 0 else ref.at[hc]

    def body(*refs):
        g_refs, gv_ref = refs[:N_MATS], refs[N_MATS]
        out_refs, rg_ref = refs[N_MATS + 1:2 * N_MATS + 1], refs[2 * N_MATS + 1]
        send_sems, recv_sems = refs[2 * N_MATS + 2:]
        x, y, c = _my_place()
        me = 4 * x + 2 * y + c
        sends = []
        for mi in range(N_MATS):
            cp = pltpu.make_async_remote_copy(src_ref=g_refs[mi], dst_ref=place(out_refs[mi], mi, c), send_sem=send_sems.at[mi],
                                              recv_sem=recv_sems.at[mi], device_id=(x, y, 1 - c), device_id_type=MESH)
            cp.start()
            sends.append(cp)
        for k, dev, peer in _peers(x, y, c):
            cp = pltpu.make_async_remote_copy(src_ref=gv_ref, dst_ref=rg_ref.at[me], send_sem=send_sems.at[N_MATS + k],
                                              recv_sem=recv_sems.at[N_MATS + k], device_id=dev, device_id_type=MESH)
            cp.start()
            sends.append(cp)
        for mi in range(N_MATS):
            got = place(out_refs[mi], mi, 1 - c)
            pltpu.make_async_remote_copy(src_ref=got, dst_ref=got, send_sem=send_sems.at[mi], recv_sem=recv_sems.at[mi],
                                         device_id=(x, y, 1 - c), device_id_type=MESH).wait_recv()
        for k, dev, peer in _peers(x, y, c):
            got = rg_ref.at[peer]
            pltpu.make_async_remote_copy(src_ref=got, dst_ref=got, send_sem=send_sems.at[N_MATS + k],
                                         recv_sem=recv_sems.at[N_MATS + k], device_id=dev, device_id_type=MESH).wait_recv()
        for cp in sends:
            cp.wait_send()

    outs = pl.pallas_call(
        body, name="swap_halves",
        in_specs=[ANY] * (N_MATS + 1), out_specs=[ANY] * (N_MATS + 1),
        out_shape=[jax.ShapeDtypeStruct(SHARD_SHAPES[0], F32)]
        + [jax.ShapeDtypeStruct((2, r // 2, c), F32) for r, c in SHARD_SHAPES[1:]]
        + [jax.ShapeDtypeStruct((8, 8, N_GVEC), F32)],
        scratch_shapes=[pltpu.SemaphoreType.DMA((N_MATS + 7,)), pltpu.SemaphoreType.DMA((N_MATS + 7,))],
    )(*halves, gvec)
    return outs[:N_MATS], outs[N_MATS]


def _set_slot(arr, block, idx):
    return lax.dynamic_update_slice(arr, block[None], (idx,) + (0,) * block.ndim)


PAD_RUNS = ((6304, 8352, 0), (5280, 6304, COL_Z), (672, 5280, COL_QKV), (0, 640, COL_LAT), (640, 672, COL_LAT + 704))
N_QKV = COL_LAT - COL_QKV


def _qkv_rows_regroup(a, to_padded):
    if to_padded:
        a4 = a.reshape(3, 12, 128, a.shape[1])
        return jnp.stack([a4[0], a4[1], a4[2]], axis=1).reshape(a.shape)
    a4 = a.reshape(12, 3, 128, a.shape[1])
    return jnp.concatenate([a4[:, tq].reshape(N_QKV // 3, a.shape[1]) for tq in range(3)], axis=0)
W_IN_SHARD = 2088


def _full_weights(gathered):
    def cols(a):
        return jnp.concatenate([a[s] for s in range(4)], axis=1)

    w_uq, w_ukv, w_pm, w_pd = [cols(a) for a in gathered[1:5]]
    w_out = gathered[5].reshape(D_MODEL, D_MODEL)
    w_in_t = gathered[0].reshape(4 * W_IN_SHARD, D_MODEL)
    pieces, at = [], 0
    for lo, hi, pad_lo in sorted(PAD_RUNS, key=lambda t: t[2]):
        if pad_lo > at:
            pieces.append(jnp.zeros((pad_lo - at, D_MODEL), w_in_t.dtype))
        pieces.append(_qkv_rows_regroup(w_in_t[lo:hi], True) if pad_lo == COL_QKV else w_in_t[lo:hi])
        at = pad_lo + hi - lo
    pieces.append(jnp.zeros((N_PAD - at, D_MODEL), w_in_t.dtype))
    w_pad_t = jnp.concatenate(pieces, axis=0)
    z32 = jnp.zeros((Q_RANK, 32), w_uq.dtype)
    wuq_pad = jnp.concatenate([t for h in range(MLA_HEADS) for t in (w_uq[:, h * 96:(h + 1) * 96], z32)], axis=1)
    z64 = jnp.zeros((KV_RANK, 64), w_ukv.dtype)
    wk_pad = jnp.concatenate([t for h in range(MLA_HEADS) for t in (w_ukv[:, h * 128:h * 128 + 64], z64)], axis=1)
    wv = jnp.concatenate([w_ukv[:, h * 128 + 64:(h + 1) * 128] for h in range(MLA_HEADS)], axis=1)
    return w_pad_t.T, w_pad_t, wuq_pad, wk_pad, wv, w_pm, w_pd, w_out


W_IN_LAT = 672


def _grad_parts_in_early(dwt_early):
    dwt_early = jnp.concatenate([dwt_early[:COL_QKV], _qkv_rows_regroup(dwt_early[COL_QKV:COL_LAT], False)], axis=0)

    def in_block(s, h):
        cols = slice(h * 512, (h + 1) * 512)
        out = []
        for lo, hi, pad_lo in sorted(PAD_RUNS):
            a_, b_ = max(lo, s * W_IN_SHARD), min(hi, (s + 1) * W_IN_SHARD)
            if a_ < b_:
                out.append(jnp.zeros((b_ - a_, 512), dwt_early.dtype) if pad_lo >= COL_LAT
                           else dwt_early[pad_lo + a_ - lo:pad_lo + b_ - lo, cols])
        return jnp.concatenate(out, axis=0)

    return jnp.stack([in_block(s, h) for s in range(4) for h in range(2)])


def _grad_parts_in_late(dwt_late):
    rows = jnp.concatenate([dwt_late[0:640], dwt_late[704:736]], axis=0)
    zero = jnp.zeros((W_IN_LAT, 512), dwt_late.dtype)
    return jnp.stack([rows[:, 0:512], rows[:, 512:1024]] + [zero] * 6)


def _col_blocks(m):
    r, c = m.shape[0] // 2, m.shape[1] // 4
    return jnp.stack([m[h * r:(h + 1) * r, s * c:(s + 1) * c] for s in range(4) for h in range(2)])


def _grad_parts_mla(dwuq_pad, dwk_pad, dwv):
    d_uq = jnp.concatenate([dwuq_pad[:, h * 128:h * 128 + 96] for h in range(MLA_HEADS)], axis=1)
    d_ukv = jnp.concatenate([t for h in range(MLA_HEADS) for t in (dwk_pad[:, h * 128:h * 128 + 64], dwv[:, h * 64:(h + 1) * 64])],
                            axis=1)
    return [_col_blocks(d_uq.astype(BF16)), _col_blocks(d_ukv.astype(BF16))]


def _rope_tables(positions, token=None):
    pos = positions.reshape(SEQ).astype(F32)
    if token is not None:
        pos = pos + token[0, 0]
    lane = jnp.arange(128)

    def table(rot, first, period):
        inv = ROPE_THETA ** (-jnp.arange(0, rot, 2, dtype=F32) / rot)
        half = rot // 2
        off = lane % period - first
        in1, in2 = (off >= 0) & (off < half), (off >= half) & (off < rot)
        inv_lane = jnp.where(in1 | in2, inv[jnp.clip(off % half, 0, half - 1)], 0.0)
        sign = jnp.where(in1, -1.0, 1.0).astype(F32)
        ang = pos[:, None] * inv_lane[None, :]
        return jnp.cos(ang), jnp.sin(ang) * sign[None, :]

    return table(32, 64, 128), table(16, 0, 64)


class _Links:
    def __init__(self, mats, chip, me):
        landing = [_set_slot(lax.empty((4,) + m.shape, m.dtype), m, chip) for m in mats]
        self.gather, self.token = _gather_start(mats, landing)
        self.me, self.sent, self.handles, self.sums = me, {}, {}, {}

    def weights(self, after):
        return _relay_share(_gather_wait(self.gather, after))

    def send(self, blocks, name):
        self.sent[name] = blocks
        self.handles[name], token = _exchange_start(blocks, name)
        return token

    def collect(self, name, after, parts):
        recv = _exchange_wait(self.handles[name], after)
        for r, own, part in zip(recv, self.sent[name], parts):
            self.sums[part] = _sum_parts(r, own, self.me, 64, "sum_grad_" + part)
        return tuple(self.sums[part] for part in parts)


def _device_grads(x, positions, target, gains, links):
    pre_g, q_g, kv_g, post_g = gains
    (mc, ms), (dc, ds) = _rope_tables(positions, links.token)
    h = _prenorm_fwd(x, pre_g, links.token)
    w_pad, w_pad_t, wuq_pad, wk_pad, wv, w_pm, w_pd, w_out = _full_weights(links.weights((h, mc, ms, dc, ds)))

    p_gz = _matmul(h, w_pad, "nn", F32, 1024, 1536, 1024, "in_proj_gates", b_cols=(0, COL_QKV // 1536))
    p_qkv = _matmul(h, w_pad, "nn", F32, 1024, 1536, 1024, "in_proj_dilated", b_cols=(COL_QKV // 1536, N_QKV // 1536),
                    lane_blocks=True)
    p_lat = _matmul(h, w_pad, "nn", F32, 1024, N_LAT, 1024, "in_proj_latent", b_cols=(COL_LAT // N_LAT, 1))
    q, k, v = _mla_prep_fwd(p_lat, q_g, kv_g, wuq_pad, wk_pad, wv, mc, ms)
    ya, lse_m = _mla_flash_fwd(q, k, v)
    qkv = [_dil_prep_fwd(p_qkv, dc, ds, g) for g in range(3)]
    o_g, l_g = zip(*[_dil_attn_fwd(qkv[g], g) for g in range(3)])
    (dp, dy, dya, dyd, yd, lse_d, loss_cols, dg_post, dwpm, dwpd, dwout) = _tail(
        p_gz, ya, o_g, l_g, x, target, w_pm, w_pd, w_out, post_g)

    for g in range(3):
        dp = _dil_attn_bwd(dp, qkv[g], dyd, yd, lse_d, dc, ds, g)
    dw_early = _matmul(dp, h, "tn", BF16, 1536, 1024, 2048, "dw_in_early", a_cols=(0, COL_LAT // 1536))
    token = links.send([_grad_parts_in_early(dw_early), _col_blocks(dwpm.astype(BF16)), _col_blocks(dwpd.astype(BF16)),
                        dwout.astype(BF16).reshape(8, 128, D_MODEL)], "exchange_early")

    dq, dk, dv = _mla_flash_bwd(q, k, v, ya, dya, lse_m, token)
    dp, dwuq_pad, dwk_pad, dwv, dg_q, dg_kv = _mla_prep_bwd(dp, p_lat, dq, dk, dv, q_g, kv_g, wuq_pad, wk_pad, wv, mc, ms)
    dw_late = _matmul(dp, h, "tn", BF16, N_LAT, 1024, 2048, "dw_in_late", a_cols=(COL_LAT // N_LAT, 1))
    token = links.send([_grad_parts_in_late(dw_late)] + _grad_parts_mla(dwuq_pad, dwk_pad, dwv), "exchange_late")
    early = links.collect("exchange_early", dw_late, ("in_early", "pm", "pd", "out"))

    grad_x, dg_pre = _dh_prenorm_bwd(dp, w_pad_t, x, dy, pre_g, (token,) + tuple(early))
    links.collect("exchange_late", grad_x, ("in_late", "uq", "ukv"))

    loss_part = jnp.pad((jnp.sum(loss_cols) * (0.5 / D_MODEL)).reshape(1, 1), ((0, 0), (0, N_GVEC - N_GAINS - 1)))
    gvec = jnp.concatenate([dg_pre, dg_q, dg_kv, dg_post, loss_part], axis=1)
    return grad_x, gvec


def kernel(x, positions, pre_norm_g, w_in, q_norm_g, w_uq, kv_norm_g, w_ukv, w_proj_mla, w_proj_dil, w_out, post_norm_g, loss_target, m_pre_norm_g, m_w_in, m_q_norm_g, m_w_uq, m_kv_norm_g, m_w_ukv, m_w_proj_mla, m_w_proj_dil, m_w_out, m_post_norm_g, v_pre_norm_g, v_w_in, v_q_norm_g, v_w_uq, v_kv_norm_g, v_w_ukv, v_w_proj_mla, v_w_proj_dil, v_w_out, v_post_norm_g):
    xi, yi, ci = _my_place()
    chip, me = 2 * xi + yi, 4 * xi + 2 * yi + ci
    mats = [jnp.swapaxes(w_in, 1, 2)] + [w_uq, w_ukv, w_proj_mla, w_proj_dil, w_out]
    mats = [w.reshape(w.shape[1:]).astype(BF16) for w in mats]
    links = _Links(mats, chip, me)
    gains = (pre_norm_g, q_norm_g, kv_norm_g, post_norm_g)
    grad_x, gvec = _device_grads(x[0], positions, loss_target[0], gains, links)

    sums = links.sums
    in_e = sums["in_early"]
    half_in = jnp.concatenate([in_e[:W_IN_LAT] + jnp.where(chip == 0, sums["in_late"], 0.0), in_e[W_IN_LAT:]], axis=0)
    halves = [half_in, sums["uq"], sums["ukv"], sums["pm"], sums["pd"], sums["out"]]
    gvec8 = jnp.pad(gvec, ((0, 7), (0, 0)))
    swapped, recv_gains = _swap_halves(halves, gvec8)
    g_gains = _sum_parts(recv_gains, gvec8, me, 8, "sum_gain_parts")[0:1]
    loss = g_gains[0, N_GAINS]
    sw = lambda a: jnp.swapaxes(a, 1, 2)
    d_in, m_in, v_in, g_in = [sw(o) for o in _adamw_in(sw(w_in), sw(m_w_in), sw(v_w_in), half_in, swapped[0], ci)]
    g_mats = [g_in] + [_set_slot(s, hf, ci).reshape((1,) + shp)
                       for s, hf, shp in zip(swapped[1:], halves[1:], SHARD_SHAPES[1:])]

    off = [0, 1024, 1408, 1664, 2688]
    g_gain = [g_gains[:, off[i]:off[i + 1]] for i in range(4)]
    grads = [g_gain[0], g_mats[0], g_gain[1], g_mats[1], g_gain[2], g_mats[2], g_mats[3], g_mats[4], g_mats[5], g_gain[3]]
    ws = [pre_norm_g, w_in, q_norm_g, w_uq, kv_norm_g, w_ukv, w_proj_mla, w_proj_dil, w_out, post_norm_g]
    ms = [m_pre_norm_g, m_w_in, m_q_norm_g, m_w_uq, m_kv_norm_g, m_w_ukv, m_w_proj_mla, m_w_proj_dil, m_w_out, m_post_norm_g]
    vs = [v_pre_norm_g, v_w_in, v_q_norm_g, v_w_uq, v_kv_norm_g, v_w_ukv, v_w_proj_mla, v_w_proj_dil, v_w_out, v_post_norm_g]
    deltas, new_m, new_v = [], [], []
    for i, (w, g, m, v) in enumerate(zip(ws, grads, ms, vs)):
        if w is w_in:
            d_, m_, v_ = d_in, m_in, v_in
        elif w.shape[-1] % 128 and w.shape[-2] % 128 == 0:
            g = jnp.swapaxes(g, 1, 2)
            grads[i] = jnp.swapaxes(g, 1, 2)
            d_, m_, v_ = [jnp.swapaxes(o, 1, 2) for o in
                          _adamw(jnp.swapaxes(w, 1, 2), g, jnp.swapaxes(m, 1, 2), jnp.swapaxes(v, 1, 2), f"adamw_{i}")]
        else:
            d_, m_, v_ = _adamw(w, g, m, v, f"adamw_{i}")
        deltas.append(d_)
        new_m.append(m_)
        new_v.append(v_)
    return (loss, grad_x.reshape(x.shape), *grads, *deltas, *new_m, *new_v)
```

```python
import jax
import jax.numpy as jnp
from jax import lax
from jax.experimental import pallas as pl
from jax.experimental.pallas import tpu as pltpu

F32 = jnp.float32
BF16 = jnp.bfloat16

SEQ = 4096
D_MODEL = 1024
EPS = 1e-6
ROPE_THETA = 500000.0
MLA_HEADS = 8
Q_RANK = 384
KV_RANK = 256
MLA_SCALE = 96.0 ** -0.5
MLA_ROPE_HALF = 16
DIL_DILATIONS = (1, 4, 16)
DIL_ROPE_HALF = 8
DIL_SCALE = 0.125
BAND = 128

N_LAT = 768
COL_Z, COL_QKV, COL_LAT = 2048, 3072, 7680
N_PAD = 8448


def _qkv_block(tq, g, pr):
    return COL_QKV // 128 + (g * 4 + pr) * 3 + tq

IN_SPLITS = (384, 256, 32, 4608, 512, 512, 1024, 1024)

SHARD_SHAPES = ((2088, 1024), (384, 192), (256, 256), (512, 256), (512, 256), (256, 1024))
N_MATS = len(SHARD_SHAPES)
N_GAINS = 2688
N_GVEC = N_GAINS + 128

ADAM_LR, ADAM_B1, ADAM_B2, ADAM_EPS, ADAM_WD, ADAM_STEP = 0.001, 0.9, 0.999, 1e-08, 0.01, 10

VMEM_LIMIT = 56 * 1024 * 1024
NEG = -1e30
MESH = pl.DeviceIdType.MESH


def _cparams(**kw):
    return pltpu.CompilerParams(vmem_limit_bytes=VMEM_LIMIT, **kw)


def _dot(a, b, dims):
    return lax.dot_general(a, b, (dims, ((), ())), preferred_element_type=F32)


def _nn(a, b):
    return _dot(a, b, ((1,), (0,)))


def _nt(a, b):
    return _dot(a, b, ((1,), (1,)))


def _tn(a, b):
    return _dot(a, b, ((0,), (0,)))


def _rope_lanes(shape, half, period, first):
    lane = lax.broadcasted_iota(jnp.int32, shape, len(shape) - 1) % period
    return (lane >= first) & (lane < first + half), (lane >= first + half) & (lane < first + 2 * half)


def _rope_fwd(x, c, s, half, lanes):
    x1, _ = lanes
    return x * c + jnp.where(x1, pltpu.roll(x, 128 - half, 1), pltpu.roll(x, half, 1)) * s


def _rope_bwd(g, c, s, half, lanes):
    x1, x2 = lanes
    gs = g * s
    return g * c + jnp.where(x2, pltpu.roll(gs, half, 1), jnp.where(x1, pltpu.roll(gs, 128 - half, 1), 0.0))


def _sigmoid(x):
    return 1.0 / (1.0 + jnp.exp(-x))


def _after(token):
    tokens = [t for t in (token if isinstance(token, (tuple, list)) else [token]) if t is not None]
    return tokens, [pl.BlockSpec(memory_space=pl.ANY)] * len(tokens)


def _matmul(a, b, mode, out_dtype, tm, tn, tk, name, token=None, b_cols=None, a_cols=None, lane_blocks=False):
    after, after_specs = _after(token)
    if mode == "nn":
        (m, k), n = a.shape, b.shape[1]
        first = 0
        if b_cols is not None:
            first, n = b_cols[0], b_cols[1] * tn
        a_spec = pl.BlockSpec((tm, tk), lambda j, i, kk: (i, kk))
        b_spec = pl.BlockSpec((tk, tn), lambda j, i, kk: (kk, j + first))
        dot = _nn
    elif mode == "nt":
        (m, k), n = a.shape, b.shape[0]
        a_spec = pl.BlockSpec((tm, tk), lambda j, i, kk: (i, kk))
        b_spec = pl.BlockSpec((tn, tk), lambda j, i, kk: (j, kk))
        dot = _nt
    else:
        (k, m), n = a.shape, b.shape[1]
        first = 0
        if a_cols is not None:
            first, m = a_cols[0], a_cols[1] * tm
        a_spec = pl.BlockSpec((tk, tm), lambda j, i, kk: (kk, i + first))
        b_spec = pl.BlockSpec((tk, tn), lambda j, i, kk: (kk, j))
        dot = _tn
    assert m % tm == 0 and n % tn == 0 and k % tk == 0, (name, m, n, k, tm, tn, tk)
    nk = k // tk

    def body(a_ref, b_ref, *rest):
        o_ref, acc_ref = rest[-2:]
        kk = pl.program_id(2)
        part = dot(a_ref[...], b_ref[...])

        @pl.when(kk == 0)
        def _():
            acc_ref[...] = part

        @pl.when(kk > 0)
        def _():
            acc_ref[...] += part

        @pl.when(kk == nk - 1)
        def _():
            if lane_blocks:
                for blk in range(tn // 128):
                    o_ref[blk] = acc_ref[:, blk * 128:(blk + 1) * 128].astype(o_ref.dtype)
            else:
                o_ref[...] = acc_ref[...].astype(o_ref.dtype)

    if lane_blocks:
        out_spec = pl.BlockSpec((tn // 128, tm, 128), lambda j, i, kk: (j, i, 0))
        out_shape = jax.ShapeDtypeStruct((n // 128, m, 128), out_dtype)
    else:
        out_spec = pl.BlockSpec((tm, tn), lambda j, i, kk: (i, j))
        out_shape = jax.ShapeDtypeStruct((m, n), out_dtype)
    return pl.pallas_call(
        body, name=name, grid=(n // tn, m // tm, nk),
        in_specs=[a_spec, b_spec] + after_specs,
        out_specs=out_spec, out_shape=out_shape,
        scratch_shapes=[pltpu.VMEM((tm, tn), F32)],
        compiler_params=_cparams(),
    )(a, b, *after)


def _prenorm_fwd(x, g, token=None):
    tm = 512
    after, after_specs = _after(token)

    def body(x_ref, g_ref, *rest):
        xv = x_ref[...]
        r = lax.rsqrt(jnp.mean(xv * xv, axis=-1, keepdims=True) + EPS)
        rest[-1][...] = (xv * r * g_ref[...]).astype(BF16)

    return pl.pallas_call(
        body, name="prenorm_fwd", grid=(SEQ // tm,),
        in_specs=[pl.BlockSpec((tm, D_MODEL), lambda i: (i, 0)), pl.BlockSpec((1, D_MODEL), lambda i: (0, 0))] + after_specs,
        out_specs=pl.BlockSpec((tm, D_MODEL), lambda i: (i, 0)),
        out_shape=jax.ShapeDtypeStruct((SEQ, D_MODEL), BF16),
    )(x, g, *after)


def _dh_prenorm_bwd(dp, w_pad_t, x, dy, g, token=None):
    tm, tk = 1024, 1408
    nk = N_PAD // tk
    after, after_specs = _after(token)

    def body(a_ref, b_ref, x_ref, dy_ref, g_ref, *rest):
        gx_ref, dg_ref, acc_ref = rest[-3:]
        i, kk = pl.program_id(0), pl.program_id(1)
        part = _nn(a_ref[...], b_ref[...])

        @pl.when(kk == 0)
        def _():
            acc_ref[...] = part

        @pl.when(kk > 0)
        def _():
            acc_ref[...] += part

        @pl.when(kk == nk - 1)
        def _():
            xv = x_ref[...]
            r = lax.rsqrt(jnp.mean(xv * xv, axis=-1, keepdims=True) + EPS)
            n = xv * r
            dhv = acc_ref[...]
            dn = dhv * g_ref[...]
            gx_ref[...] = dy_ref[...] + r * (dn - n * jnp.mean(dn * n, axis=-1, keepdims=True))
            cols = jnp.sum(dhv * n, axis=0, keepdims=True)

            @pl.when(i == 0)
            def _():
                dg_ref[...] = cols

            @pl.when(i > 0)
            def _():
                dg_ref[...] += cols

    row = pl.BlockSpec((tm, D_MODEL), lambda i, kk: (i, 0))
    vec = pl.BlockSpec((1, D_MODEL), lambda i, kk: (0, 0))
    return pl.pallas_call(
        body, name="dh_prenorm_bwd", grid=(SEQ // tm, nk),
        in_specs=[pl.BlockSpec((tm, tk), lambda i, kk: (i, kk)), pl.BlockSpec((tk, D_MODEL), lambda i, kk: (kk, 0)),
                  row, row, vec] + after_specs,
        out_specs=[row, vec],
        out_shape=[jax.ShapeDtypeStruct((SEQ, D_MODEL), F32), jax.ShapeDtypeStruct((1, D_MODEL), F32)],
        scratch_shapes=[pltpu.VMEM((tm, D_MODEL), F32)],
        compiler_params=_cparams(),
    )(dp, w_pad_t, x, dy, g, *after)


def _mla_prep_fwd(p, qg, kvg, wuq, wk, wv, rc, rs):
    tm = 512

    def body(lat_ref, qg_ref, kvg_ref, wuq_ref, wk_ref, wv_ref, c_ref, s_ref, q_ref, k_ref, v_ref):
        c, s = c_ref[...], s_ref[...]
        lanes = _rope_lanes((tm, 128), MLA_ROPE_HALF, 128, 64)
        cq = lat_ref[:, 0:Q_RANK]
        r1 = lax.rsqrt(jnp.mean(cq * cq, axis=-1, keepdims=True) + EPS)
        cqn = (cq * r1 * qg_ref[...]).astype(BF16)
        q = _nn(cqn, wuq_ref[...])
        for h in range(MLA_HEADS):
            sl = slice(h * 128, (h + 1) * 128)
            q_ref[:, sl] = (_rope_fwd(q[:, sl], c, s, MLA_ROPE_HALF, lanes) * MLA_SCALE).astype(BF16)
        ckv = lat_ref[:, Q_RANK:Q_RANK + KV_RANK]
        r2 = lax.rsqrt(jnp.mean(ckv * ckv, axis=-1, keepdims=True) + EPS)
        ckvn = (ckv * r2 * kvg_ref[...]).astype(BF16)
        krr = _rope_fwd(lat_ref[:, Q_RANK + KV_RANK:N_LAT], c, s, MLA_ROPE_HALF, lanes)
        kn = _nn(ckvn, wk_ref[...])
        for h in range(MLA_HEADS):
            sl = slice(h * 128, (h + 1) * 128)
            k_ref[:, sl] = (kn[:, sl] + krr).astype(BF16)
        v_ref[...] = _nn(ckvn, wv_ref[...]).astype(BF16)

    def full(shape):
        return pl.BlockSpec(shape, lambda i: (0, 0))

    def rows(w):
        return pl.BlockSpec((tm, w), lambda i: (i, 0))

    return pl.pallas_call(
        body, name="mla_prep_fwd", grid=(SEQ // tm,),
        in_specs=[pl.BlockSpec((tm, N_LAT), lambda i: (i, 0)),
                  full((1, Q_RANK)), full((1, KV_RANK)), full((Q_RANK, 1024)), full((KV_RANK, 1024)),
                  full((KV_RANK, 512)), rows(128), rows(128)],
        out_specs=[rows(1024), rows(1024), rows(512)],
        out_shape=[jax.ShapeDtypeStruct((SEQ, 1024), BF16), jax.ShapeDtypeStruct((SEQ, 1024), BF16),
                   jax.ShapeDtypeStruct((SEQ, 512), BF16)],
        compiler_params=_cparams(),
    )(p, qg, kvg, wuq, wk, wv, rc, rs)


def _mla_prep_bwd(dp_in, p, dq, dk, dv, qg, kvg, wuq, wk, wv, rc, rs):
    tm = 512

    def body(dp_any, lat_ref, dq_ref, dk_ref, dv_ref, qg_ref, kvg_ref, wuq_ref, wk_ref, wv_ref,
             c_ref, s_ref, dp_ref, dwuq_ref, dwk_ref, dwv_ref, dgq_ref, dgkv_ref, dqb_ref, dkb_ref):
        del dp_any
        c, s = c_ref[...], s_ref[...]
        lanes = _rope_lanes((tm, 128), MLA_ROPE_HALF, 128, 64)
        lane = lax.broadcasted_iota(jnp.int32, (tm, 128), 1)
        dkr = jnp.zeros((tm, 128), F32)
        for h in range(MLA_HEADS):
            sl = slice(h * 128, (h + 1) * 128)
            dqb_ref[:, sl] = _rope_bwd(dq_ref[:, sl] * MLA_SCALE, c, s, MLA_ROPE_HALF, lanes).astype(BF16)
            dkh = dk_ref[:, sl]
            dkr = dkr + dkh
            dkb_ref[:, sl] = jnp.where(lane < 64, dkh, 0.0).astype(BF16)
        dkr = jnp.where((lane >= 64) & (lane < 96), dkr, 0.0)
        dkr = _rope_bwd(dkr, c, s, MLA_ROPE_HALF, lanes)
        dvb = dv_ref[...].astype(BF16)

        cq = lat_ref[:, 0:Q_RANK]
        r1 = lax.rsqrt(jnp.mean(cq * cq, axis=-1, keepdims=True) + EPS)
        n1 = cq * r1
        dcqn = _nt(dqb_ref[...], wuq_ref[...])
        dn1 = dcqn * qg_ref[...]
        dcq = r1 * (dn1 - n1 * jnp.mean(dn1 * n1, axis=-1, keepdims=True))
        pq = jnp.sum(dcqn * n1, axis=0, keepdims=True)

        ckv = lat_ref[:, Q_RANK:Q_RANK + KV_RANK]
        r2 = lax.rsqrt(jnp.mean(ckv * ckv, axis=-1, keepdims=True) + EPS)
        n2 = ckv * r2
        dckvn = _nt(dkb_ref[...], wk_ref[...]) + _nt(dvb, wv_ref[...])
        dn2 = dckvn * kvg_ref[...]
        dckv = r2 * (dn2 - n2 * jnp.mean(dn2 * n2, axis=-1, keepdims=True))
        pkv = jnp.sum(dckvn * n2, axis=0, keepdims=True)
        cqn = (n1 * qg_ref[...]).astype(BF16)
        ckvn = (n2 * kvg_ref[...]).astype(BF16)
        wq, wk_, wv_ = _tn(cqn, dqb_ref[...]), _tn(ckvn, dkb_ref[...]), _tn(ckvn, dvb)

        dp_ref[:, 0:Q_RANK] = dcq.astype(BF16)
        dp_ref[:, Q_RANK:Q_RANK + KV_RANK] = dckv.astype(BF16)
        dp_ref[:, Q_RANK + KV_RANK:N_LAT] = dkr.astype(BF16)

        @pl.when(pl.program_id(0) == 0)
        def _():
            dgq_ref[...] = pq
            dgkv_ref[...] = pkv
            dwuq_ref[...] = wq
            dwk_ref[...] = wk_
            dwv_ref[...] = wv_

        @pl.when(pl.program_id(0) > 0)
        def _():
            dgq_ref[...] += pq
            dgkv_ref[...] += pkv
            dwuq_ref[...] += wq
            dwk_ref[...] += wk_
            dwv_ref[...] += wv_

    def full(shape):
        return pl.BlockSpec(shape, lambda i: (0, 0))

    def rows(w):
        return pl.BlockSpec((tm, w), lambda i: (i, 0))

    lat = pl.BlockSpec((tm, N_LAT), lambda i: (i, 0))
    dlat = pl.BlockSpec((tm, N_LAT), lambda i: (i, COL_LAT // N_LAT))
    return pl.pallas_call(
        body, name="mla_prep_bwd", grid=(SEQ // tm,),
        in_specs=[pl.BlockSpec(memory_space=pl.ANY), lat, rows(1024), rows(1024), rows(512),
                  full((1, Q_RANK)), full((1, KV_RANK)), full((Q_RANK, 1024)), full((KV_RANK, 1024)),
                  full((KV_RANK, 512)), rows(128), rows(128)],
        out_specs=[dlat, full((Q_RANK, 1024)), full((KV_RANK, 1024)), full((KV_RANK, 512)),
                   full((1, Q_RANK)), full((1, KV_RANK))],
        out_shape=[jax.ShapeDtypeStruct((SEQ, N_PAD), BF16), jax.ShapeDtypeStruct((Q_RANK, 1024), F32),
                   jax.ShapeDtypeStruct((KV_RANK, 1024), F32), jax.ShapeDtypeStruct((KV_RANK, 512), F32),
                   jax.ShapeDtypeStruct((1, Q_RANK), F32), jax.ShapeDtypeStruct((1, KV_RANK), F32)],
        input_output_aliases={0: 0},
        scratch_shapes=[pltpu.VMEM((tm, 1024), BF16), pltpu.VMEM((tm, 1024), BF16)],
        compiler_params=_cparams(),
    )(dp_in, p, dq, dk, dv, qg, kvg, wuq, wk, wv, rc, rs)


FLASH_T = 1024


def _head_half(shape, hh):
    lane = lax.broadcasted_iota(jnp.int32, shape, 1)
    return (lane < 64) if hh == 0 else (lane >= 64)


def _diag_keep(nr, nk):
    row = lax.broadcasted_iota(jnp.int32, (nr, nk), 0)
    col = lax.broadcasted_iota(jnp.int32, (nr, nk), 1)
    return row + (nk - nr) >= col


def _tri_steps(nb, q_major):
    if q_major:
        pairs = [(i, kb) for i in range(nb) for kb in range(i + 1)]
    else:
        pairs = [(i, kb) for kb in range(nb) for i in range(kb, nb)]
    return jnp.asarray([p[0] for p in pairs], jnp.int32), jnp.asarray([p[1] for p in pairs], jnp.int32)


def _mla_flash_fwd(q, k, v):
    t = FLASH_T
    nb = SEQ // t
    qtab, ktab = _tri_steps(nb, True)

    def body(qi_ref, ki_ref, q_ref, k_ref, v_ref, o_ref, lse_ref, m_scr, l_scr, acc_scr):
        step = pl.program_id(1)
        i, kb = qi_ref[step], ki_ref[step]

        @pl.when(kb == 0)
        def _():
            m_scr[...] = jnp.full_like(m_scr, NEG)
            l_scr[...] = jnp.zeros_like(l_scr)
            acc_scr[...] = jnp.zeros_like(acc_scr)

        def update(r0, nr, nk, diagonal):
            rs = slice(r0, r0 + nr)
            vv = v_ref[0:nk, :]
            for hh in range(2):
                sl = slice(hh * 128, (hh + 1) * 128)
                s = _nt(q_ref[rs, sl], k_ref[0:nk, sl])
                if diagonal:
                    s = jnp.where(_diag_keep(nr, nk), s, NEG)
                m_prev = m_scr[hh, rs, :]
                m_new = jnp.maximum(m_prev, jnp.max(s, axis=-1, keepdims=True))
                pr = jnp.exp(s - jnp.tile(m_new, (1, nk // 128)))
                alpha = jnp.exp(m_prev - m_new)
                l_scr[hh, rs, :] = alpha * l_scr[hh, rs, :] + jnp.sum(pr, axis=-1, keepdims=True)
                acc_scr[hh, rs, :] = alpha * acc_scr[hh, rs, :] + _nn(pr.astype(BF16), vv)
                m_scr[hh, rs, :] = m_new

        @pl.when(kb < i)
        def _():
            update(0, t, t, False)

        @pl.when(kb == i)
        def _():
            update(0, t // 2, t // 2, True)
            update(t // 2, t // 2, t, True)
            o0 = acc_scr[0] / l_scr[0]
            o1 = acc_scr[1] / l_scr[1]
            o_ref[...] = jnp.where(_head_half((t, 128), 0), o0, o1)
            for hh in range(2):
                lse_ref[:, hh * 128:(hh + 1) * 128] = m_scr[hh] + jnp.log(l_scr[hh])

    grid_spec = pltpu.PrefetchScalarGridSpec(
        num_scalar_prefetch=2, grid=(4, qtab.shape[0]),
        in_specs=[pl.BlockSpec((t, 256), lambda j, s, qi, ki: (qi[s], j)),
                  pl.BlockSpec((t, 256), lambda j, s, qi, ki: (ki[s], j)),
                  pl.BlockSpec((t, 128), lambda j, s, qi, ki: (ki[s], j))],
        out_specs=[pl.BlockSpec((t, 128), lambda j, s, qi, ki: (qi[s], j)),
                   pl.BlockSpec((t, 256), lambda j, s, qi, ki: (qi[s], j))],
        scratch_shapes=[pltpu.VMEM((2, t, 128), F32), pltpu.VMEM((2, t, 128), F32), pltpu.VMEM((2, t, 128), F32)])
    return pl.pallas_call(
        body, name="mla_flash_fwd", grid_spec=grid_spec,
        out_shape=[jax.ShapeDtypeStruct((SEQ, 512), F32), jax.ShapeDtypeStruct((SEQ, 1024), F32)],
        compiler_params=_cparams(),
    )(qtab, ktab, q, k, v)


def _mla_flash_bwd(q, k, v, o, do, lse, token=None):
    t = FLASH_T
    nb = SEQ // t
    qtab, ktab = _tri_steps(nb, False)
    after, after_specs = _after(token)

    def body(qi_ref, ki_ref, q_ref, k_ref, v_ref, o_ref, do_ref, lse_ref, *rest):
        dq_ref, dk_ref, dv_ref, dk_scr, dv_scr = rest[-5:]
        step = pl.program_id(1)
        i, kb = qi_ref[step], ki_ref[step]

        @pl.when(step == 0)
        def _():
            dq_ref[...] = jnp.zeros_like(dq_ref)

        @pl.when(i == kb)
        def _():
            dk_scr[...] = jnp.zeros_like(dk_scr)
            dv_scr[...] = jnp.zeros_like(dv_scr)

        def update(r0, nr, nk, diagonal):
            rs = slice(r0, r0 + nr)
            vv = v_ref[0:nk, :]
            ov = o_ref[rs, :]
            dov = do_ref[rs, :]
            rows = pl.ds(pl.multiple_of(i * t + r0, t // 2), nr)
            for hh in range(2):
                sl = slice(hh * 128, (hh + 1) * 128)
                qh, kh = q_ref[rs, sl], k_ref[0:nk, sl]
                s = _nt(qh, kh)
                if diagonal:
                    s = jnp.where(_diag_keep(nr, nk), s, NEG)
                pr = jnp.exp(s - jnp.tile(lse_ref[rs, sl], (1, nk // 128)))
                dom = jnp.where(_head_half((nr, 128), hh), dov, 0.0)
                domb = dom.astype(BF16)
                dv_scr[0:nk, :] += _tn(pr.astype(BF16), domb)
                dpr = _nt(domb, vv)
                delta = jnp.sum(dom * ov, axis=-1, keepdims=True)
                ds = (pr * (dpr - delta)).astype(BF16)
                dq_ref[rows, sl] += _nn(ds, kh)
                dk_scr[hh, 0:nk, :] += _tn(ds, qh)

        @pl.when(i > kb)
        def _():
            update(0, t, t, False)

        @pl.when(i == kb)
        def _():
            update(0, t // 2, t // 2, True)
            update(t // 2, t // 2, t, True)

        @pl.when(i == nb - 1)
        def _():
            dk_ref[:, 0:128] = dk_scr[0]
            dk_ref[:, 128:256] = dk_scr[1]
            dv_ref[...] = dv_scr[...]

    qi_map = lambda j, s, qi, ki: (qi[s], j)
    ki_map = lambda j, s, qi, ki: (ki[s], j)
    grid_spec = pltpu.PrefetchScalarGridSpec(
        num_scalar_prefetch=2, grid=(4, qtab.shape[0]),
        in_specs=[pl.BlockSpec((t, 256), qi_map), pl.BlockSpec((t, 256), ki_map), pl.BlockSpec((t, 128), ki_map),
                  pl.BlockSpec((t, 128), qi_map), pl.BlockSpec((t, 128), qi_map), pl.BlockSpec((t, 256), qi_map)]
        + after_specs,
        out_specs=[pl.BlockSpec((SEQ, 256), lambda j, s, qi, ki: (0, j)), pl.BlockSpec((t, 256), ki_map),
                   pl.BlockSpec((t, 128), ki_map)],
        scratch_shapes=[pltpu.VMEM((2, t, 128), F32), pltpu.VMEM((t, 128), F32)])
    return pl.pallas_call(
        body, name="mla_flash_bwd", grid_spec=grid_spec,
        out_shape=[jax.ShapeDtypeStruct((SEQ, 1024), F32), jax.ShapeDtypeStruct((SEQ, 1024), F32),
                   jax.ShapeDtypeStruct((SEQ, 512), F32)],
        compiler_params=_cparams(),
    )(qtab, ktab, q, k, v, o, do, lse, *after)


def _strided(start, size, d):
    return pl.ds(start, size) if d == 1 else pl.ds(start, size, stride=d)


def _dil_prep_fwd(p, rc, rs, g):
    d = DIL_DILATIONS[g]
    sub_len = SEQ // d
    ch = min(sub_len, 512)

    def body(p_ref, c_ref, s_ref, o_ref, x_scr):
        tq = pl.program_id(0)
        lanes = _rope_lanes((ch, 128), DIL_ROPE_HALF, 64, 0)
        o_ref[0, 0, 0:BAND, :] = jnp.zeros((BAND, 128), BF16)

        @pl.when(tq < 2)
        def _():
            mult = jnp.where(tq == 0, DIL_SCALE, 1.0).astype(F32)
            for c0 in range(0, SEQ, ch):
                rows = pl.ds(c0, ch)
                x_scr[rows, :] = _rope_fwd(p_ref[rows, :], c_ref[rows, :] * mult, s_ref[rows, :] * mult, DIL_ROPE_HALF, lanes)

        def gather(take):
            for r in range(d):
                for c0 in range(0, sub_len, ch):
                    at = BAND + r * sub_len + c0
                    o_ref[0, 0, at:at + ch, :] = take(_strided(r + c0 * d, ch, d)).astype(BF16)

        @pl.when(tq < 2)
        def _():
            gather(lambda rows: x_scr[rows, :])

        @pl.when(tq == 2)
        def _():
            gather(lambda rows: p_ref[rows, :])

    tab = pl.BlockSpec((SEQ, 128), lambda tq, pr: (0, 0))
    return pl.pallas_call(
        body, name=f"dil_prep_fwd_g{g}", grid=(3, 4),
        in_specs=[pl.BlockSpec((None, SEQ, 128), lambda tq, pr: (_qkv_block(tq, g, pr) - COL_QKV // 128, 0, 0)), tab, tab],
        out_specs=pl.BlockSpec((1, 1, BAND + SEQ, 128), lambda tq, pr: (tq, pr, 0, 0)),
        out_shape=jax.ShapeDtypeStruct((3, 4, BAND + SEQ, 128), BF16),
        scratch_shapes=[pltpu.VMEM((SEQ, 128), F32)],
        compiler_params=_cparams(),
    )(p, rc, rs)


DIL_ST_FWD, DIL_ST_BWD = 1024, 2048


def _band_keep(g, b, t, nb):
    nbs = SEQ // DIL_DILATIONS[g] // BAND
    row = lax.broadcasted_iota(jnp.int32, (BAND, 2 * BAND), 0)
    col = lax.broadcasted_iota(jnp.int32, (BAND, 2 * BAND), 1)
    cur = (col >= BAND) & (row >= col - BAND)
    prev = (col < BAND) & (col >= row)
    if nbs >= nb:
        if b > 0:
            return cur | prev
        return cur | (prev & ((t * nb) % nbs != 0))
    return cur | prev if b % nbs else cur


def _dil_tok(g, b, t, nb):
    d = DIL_DILATIONS[g]
    nbs = SEQ // d // BAND
    gb = t * nb + b
    return _strided((gb % nbs) * BAND * d + gb // nbs, BAND, d)


def _dil_attn_fwd(qkv, g):
    DIL_ST, DIL_NB = DIL_ST_FWD, DIL_ST_FWD // BAND

    def body(q_ref, k_ref, v_ref, o_ref, l_ref, s_scr, p_scr, o_scr):
        t = pl.program_id(1)
        base = t * DIL_ST
        half0 = _head_half((DIL_ST, 128), 0)
        lse_h = []
        for hh in range(2):
            half = _head_half((BAND, 128), hh)
            for b in range(DIL_NB):
                qv = q_ref[0, 0, pl.ds(pl.multiple_of(base + (b + 1) * BAND, BAND), BAND), :]
                k2 = k_ref[0, 0, pl.ds(pl.multiple_of(base + b * BAND, BAND), 2 * BAND), :]
                sb = _nt(jnp.where(half, qv, jnp.zeros_like(qv)), k2)
                s_scr[b * BAND:(b + 1) * BAND, :] = jnp.where(_band_keep(g, b, t, DIL_NB), sb, NEG)
            s = s_scr[...]
            m = jnp.max(s, axis=-1, keepdims=True)
            pr = jnp.exp(s - m)
            den = jnp.sum(pr, axis=-1, keepdims=True)
            p_scr[...] = pr.astype(BF16)
            for b in range(DIL_NB):
                v2 = v_ref[0, 0, pl.ds(pl.multiple_of(base + b * BAND, BAND), 2 * BAND), :]
                o_scr[hh, b * BAND:(b + 1) * BAND, :] = _nn(p_scr[b * BAND:(b + 1) * BAND, :], v2)
            o_scr[hh] = o_scr[hh] / den
            lse_h.append(m + jnp.log(den))
        out = jnp.where(half0, o_scr[0], o_scr[1])
        lse = jnp.where(half0, lse_h[0], lse_h[1])
        for b in range(DIL_NB):
            tok = _dil_tok(g, b, t, DIL_NB)
            o_ref[tok, :] = out[b * BAND:(b + 1) * BAND, :]
            l_ref[tok, :] = lse[b * BAND:(b + 1) * BAND, :]

    def inp(tq):
        return pl.BlockSpec((1, 1, BAND + SEQ, 128), lambda pr, t: (tq, pr, 0, 0))

    out = pl.BlockSpec((SEQ, 128), lambda pr, t: (0, pr))
    return pl.pallas_call(
        body, name=f"dil_attn_fwd_g{g}", grid=(4, SEQ // DIL_ST),
        in_specs=[inp(0), inp(1), inp(2)], out_specs=[out, out],
        out_shape=[jax.ShapeDtypeStruct((SEQ, 512), F32), jax.ShapeDtypeStruct((SEQ, 512), F32)],
        scratch_shapes=[pltpu.VMEM((DIL_ST, 2 * BAND), F32), pltpu.VMEM((DIL_ST, 2 * BAND), BF16),
                        pltpu.VMEM((2, DIL_ST, 128), F32)],
        compiler_params=_cparams(),
    )(qkv, qkv, qkv)


def _dil_attn_bwd(dp_in, qkv, dyd, yd, lse_all, rc, rs, g, token=None):
    d = DIL_DILATIONS[g]
    sub_len = SEQ // d
    DIL_ST, DIL_NB = DIL_ST_BWD, DIL_ST_BWD // BAND
    nst = SEQ // DIL_ST
    after, after_specs = _after(token)
    ch = 512

    def body(dp_any, q_ref, k_ref, v_ref, do_ref, y_ref, l_ref, c_ref, sn_ref, *rest):
        dp_ref, tok_scr, dk_scr, dv_scr, s_scr, dp_scr, p_scr, ds_scr, do_scr, y_scr, l_scr, dq_scr = rest[-12:]
        del dp_any
        t = pl.program_id(1)
        base = t * DIL_ST

        @pl.when(t == 0)
        def _():
            dk_scr[...] = jnp.zeros_like(dk_scr)
            dv_scr[...] = jnp.zeros_like(dv_scr)

        for b in range(DIL_NB):
            tok = _dil_tok(g, b, t, DIL_NB)
            do_scr[b * BAND:(b + 1) * BAND, :] = do_ref[tok, :]
            y_scr[b * BAND:(b + 1) * BAND, :] = y_ref[tok, :]
            l_scr[b * BAND:(b + 1) * BAND, :] = l_ref[tok, :]
        for hh in range(2):
            half = _head_half((BAND, 128), hh)
            half_st = _head_half((DIL_ST, 128), hh)
            dom = jnp.where(half_st, do_scr[...], 0.0)
            delta = jnp.sum(dom * y_scr[...], axis=-1, keepdims=True)
            lcol = jnp.max(jnp.where(half_st, l_scr[...], NEG), axis=-1, keepdims=True)
            for b in range(DIL_NB):
                rows = slice(b * BAND, (b + 1) * BAND)
                qv = q_ref[0, 0, pl.ds(pl.multiple_of(base + (b + 1) * BAND, BAND), BAND), :]
                band = pl.ds(pl.multiple_of(base + b * BAND, BAND), 2 * BAND)
                sb = _nt(jnp.where(half, qv, jnp.zeros_like(qv)), k_ref[0, 0, band, :])
                s_scr[rows, :] = jnp.where(_band_keep(g, b, t, DIL_NB), sb, NEG)
                dp_scr[rows, :] = _nt(dom[rows, :].astype(BF16), v_ref[0, 0, band, :])
            pr = jnp.exp(s_scr[...] - lcol)
            p_scr[...] = pr.astype(BF16)
            ds_scr[...] = (pr * (dp_scr[...] - delta)).astype(BF16)
            for b in range(DIL_NB):
                rows = slice(b * BAND, (b + 1) * BAND)
                qv = q_ref[0, 0, pl.ds(pl.multiple_of(base + (b + 1) * BAND, BAND), BAND), :]
                band = pl.ds(pl.multiple_of(base + b * BAND, BAND), 2 * BAND)
                dqb = jnp.where(half, _nn(ds_scr[rows, :], k_ref[0, 0, band, :]), 0.0)
                if hh == 0:
                    dq_scr[rows, :] = dqb
                else:
                    dq_scr[rows, :] += dqb
                half2 = _head_half((2 * BAND, 128), hh)
                dk_scr[band, :] += jnp.where(half2, _tn(ds_scr[rows, :], qv), 0.0)
                dv_scr[band, :] += _tn(p_scr[rows, :], dom[rows, :].astype(BF16))
        for b in range(DIL_NB):
            tok_scr[pl.ds(0, 1), _dil_tok(g, b, t, DIL_NB), :] = dq_scr[b * BAND:(b + 1) * BAND, :][None]

        @pl.when(t == nst - 1)
        def _():
            for r in range(d):
                rows = _strided(r, sub_len, d)
                tok_scr[pl.ds(1, 1), rows, :] = dk_scr[BAND + r * sub_len:BAND + (r + 1) * sub_len, :][None]
                tok_scr[pl.ds(2, 1), rows, :] = dv_scr[BAND + r * sub_len:BAND + (r + 1) * sub_len, :][None]
            lanes = _rope_lanes((ch, 128), DIL_ROPE_HALF, 64, 0)
            for c0 in range(0, SEQ, ch):
                rows = slice(c0, c0 + ch)
                cv, sv = c_ref[rows, :], sn_ref[rows, :]
                dp_ref[rows, 0:128] = _rope_bwd(tok_scr[0, rows, :], cv * DIL_SCALE, sv * DIL_SCALE, DIL_ROPE_HALF, lanes).astype(BF16)
                dp_ref[rows, 128:256] = _rope_bwd(tok_scr[1, rows, :], cv, sv, DIL_ROPE_HALF, lanes).astype(BF16)
                dp_ref[rows, 256:384] = tok_scr[2, rows, :].astype(BF16)

    def inp(tq):
        return pl.BlockSpec((1, 1, BAND + SEQ, 128), lambda pr, t: (tq, pr, 0, 0))

    tok_spec = pl.BlockSpec((SEQ, 128), lambda pr, t: (0, pr))
    tab = pl.BlockSpec((SEQ, 128), lambda pr, t: (0, 0))
    st = (DIL_ST, 2 * BAND)
    return pl.pallas_call(
        body, name=f"dil_attn_bwd_g{g}", grid=(4, nst),
        in_specs=[pl.BlockSpec(memory_space=pl.ANY), inp(0), inp(1), inp(2), tok_spec, tok_spec, tok_spec, tab, tab]
        + after_specs,
        out_specs=pl.BlockSpec((SEQ, 384), lambda pr, t: (0, _qkv_block(0, g, pr) // 3)),
        out_shape=jax.ShapeDtypeStruct((SEQ, N_PAD), BF16),
        input_output_aliases={0: 0},
        scratch_shapes=[pltpu.VMEM((3, SEQ, 128), F32),
                        pltpu.VMEM((BAND + SEQ, 128), F32), pltpu.VMEM((BAND + SEQ, 128), F32),
                        pltpu.VMEM(st, F32), pltpu.VMEM(st, F32), pltpu.VMEM(st, BF16), pltpu.VMEM(st, BF16),
                        pltpu.VMEM((DIL_ST, 128), F32), pltpu.VMEM((DIL_ST, 128), F32), pltpu.VMEM((DIL_ST, 128), F32),
                        pltpu.VMEM((DIL_ST, 128), F32)],
        compiler_params=_cparams(),
    )(dp_in, qkv, qkv, qkv, dyd, yd, lse_all, rc, rs, *after)


TAIL_T = 256


def _tail(p, ya, o_g, l_g, x, target, wpm, wpd, wout, post_g):
    tm = TAIL_T

    def body(pgz_ref, ya_ref, o0_ref, o1_ref, o2_ref, l0_ref, l1_ref, l2_ref, x_ref, t_ref,
             wpm_ref, wpd_ref, wout_ref, pg_ref,
             dp_ref, dy_ref, dya_ref, dyd_ref, yd_ref, lse_ref, loss_ref, dgp_ref, dwpm_ref, dwpd_ref, dwout_ref):
        l0, l1, l2 = l0_ref[...], l1_ref[...], l2_ref[...]
        mx = jnp.maximum(jnp.maximum(l0, l1), l2)
        e0, e1, e2 = jnp.exp(l0 - mx), jnp.exp(l1 - mx), jnp.exp(l2 - mx)
        den = e0 + e1 + e2
        yd = (e0 * o0_ref[...] + e1 * o1_ref[...] + e2 * o2_ref[...]) / den
        yd_ref[...] = yd
        lse_ref[...] = mx + jnp.log(den)
        ya = ya_ref[...]

        gm, gd = pgz_ref[:, 0:1024], pgz_ref[:, 1024:2048]
        zm, zd = pgz_ref[:, 2048:2560], pgz_ref[:, 2560:3072]
        szm, szd = _sigmoid(zm), _sigmoid(zd)
        sm, sd = zm * szm, zd * szd
        ua = (ya * sm).astype(BF16)
        ud = (yd * sd).astype(BF16)
        pa = _nn(ua, wpm_ref[...])
        pd = _nn(ud, wpd_ref[...])
        sgm, sgd = _sigmoid(gm), _sigmoid(gd)
        mg = (sgm * pa + sgd * pd).astype(BF16)
        t = _nn(mg, wout_ref[...])
        r3 = lax.rsqrt(jnp.mean(t * t, axis=-1, keepdims=True) + EPS)
        n = t * r3
        pg = pg_ref[...]
        err = x_ref[...] + n * pg - t_ref[...]
        lpart = jnp.sum(err * err, axis=0, keepdims=True)

        dy = err * (1.0 / D_MODEL)
        dy_ref[...] = dy
        gpart = jnp.sum(dy * n, axis=0, keepdims=True)
        dn = dy * pg
        dt = (r3 * (dn - n * jnp.mean(dn * n, axis=-1, keepdims=True))).astype(BF16)
        dmg = _nt(dt, wout_ref[...])
        dpa = (dmg * sgm).astype(BF16)
        dpd = (dmg * sgd).astype(BF16)
        dp_ref[:, 0:1024] = (dmg * pa * sgm * (1.0 - sgm)).astype(BF16)
        dp_ref[:, 1024:2048] = (dmg * pd * sgd * (1.0 - sgd)).astype(BF16)
        dua = _nt(dpa, wpm_ref[...])
        dud = _nt(dpd, wpd_ref[...])
        dya_ref[...] = dua * sm
        dyd_ref[...] = dud * sd
        dp_ref[:, 2048:2560] = (dua * ya * szm * (1.0 + zm * (1.0 - szm))).astype(BF16)
        dp_ref[:, 2560:3072] = (dud * yd * szd * (1.0 + zd * (1.0 - szd))).astype(BF16)

        wpm, wpd, wout = _tn(ua, dpa), _tn(ud, dpd), _tn(mg, dt)

        @pl.when(pl.program_id(0) == 0)
        def _():
            loss_ref[...] = lpart
            dgp_ref[...] = gpart
            dwpm_ref[...] = wpm
            dwpd_ref[...] = wpd
            dwout_ref[...] = wout

        @pl.when(pl.program_id(0) > 0)
        def _():
            loss_ref[...] += lpart
            dgp_ref[...] += gpart
            dwpm_ref[...] += wpm
            dwpd_ref[...] += wpd
            dwout_ref[...] += wout

    def rows(w):
        return pl.BlockSpec((tm, w), lambda i: (i, 0))

    def full(shape):
        return pl.BlockSpec(shape, lambda i: (0, 0))

    def sds(w, dt):
        return jax.ShapeDtypeStruct((SEQ, w), dt)

    return pl.pallas_call(
        body, name="tail", grid=(SEQ // tm,),
        in_specs=[rows(3072), rows(512), rows(512), rows(512), rows(512), rows(512), rows(512), rows(512),
                  rows(1024), rows(1024), full((512, 1024)), full((512, 1024)), full((1024, 1024)), full((1, 1024))],
        out_specs=[rows(3072), rows(1024), rows(512), rows(512), rows(512), rows(512), full((1, 1024)), full((1, 1024)),
                   full((512, 1024)), full((512, 1024)), full((1024, 1024))],
        out_shape=[sds(N_PAD, BF16), sds(1024, F32), sds(512, F32), sds(512, F32), sds(512, F32), sds(512, F32),
                   jax.ShapeDtypeStruct((1, 1024), F32), jax.ShapeDtypeStruct((1, 1024), F32),
                   jax.ShapeDtypeStruct((512, 1024), F32), jax.ShapeDtypeStruct((512, 1024), F32),
                   jax.ShapeDtypeStruct((1024, 1024), F32)],
        compiler_params=_cparams(),
    )(p, ya, o_g[0], o_g[1], o_g[2], l_g[0], l_g[1], l_g[2], x, target, wpm, wpd, wout, post_g)


def _sum_parts(recv, own, me, tr, name):
    n, r, w = recv.shape
    if r % tr:
        return _sum_parts_cols(recv, own, me, name)
    own_spec = (pl.BlockSpec((tr, w), lambda i, me_ref: (i, 0)) if own.ndim == 2
                else pl.BlockSpec((None, tr, w), lambda i, me_ref: (me_ref[0], i, 0)))

    def body(me_ref, p_ref, own_ref, o_ref):
        mine = own_ref[...].astype(F32)
        acc = jnp.zeros((tr, w), F32)
        for s in range(n):
            acc = acc + jnp.where(me_ref[0] == s, mine, p_ref[s].astype(F32))
        o_ref[...] = acc

    return pl.pallas_call(
        body, name=name,
        grid_spec=pltpu.PrefetchScalarGridSpec(
            num_scalar_prefetch=1, grid=(r // tr,),
            in_specs=[pl.BlockSpec((n, tr, w), lambda i, me_ref: (0, i, 0)), own_spec],
            out_specs=pl.BlockSpec((tr, w), lambda i, me_ref: (i, 0))),
        out_shape=jax.ShapeDtypeStruct((r, w), F32),
    )(me.reshape(1), recv, own)


def _sum_parts_cols(recv, own, me, name):
    n, r, w = recv.shape
    tc = 128

    def body(me_ref, p_ref, own_ref, o_ref):
        mine = own_ref[...].astype(F32)
        acc = jnp.zeros((r, tc), F32)
        for s in range(n):
            acc = acc + jnp.where(me_ref[0] == s, mine, p_ref[s].astype(F32))
        o_ref[...] = acc

    return pl.pallas_call(
        body, name=name,
        grid_spec=pltpu.PrefetchScalarGridSpec(
            num_scalar_prefetch=1, grid=(w // tc,),
            in_specs=[pl.BlockSpec((n, r, tc), lambda i, me_ref: (0, 0, i)),
                      pl.BlockSpec((None, r, tc), lambda i, me_ref: (me_ref[0], 0, i))],
            out_specs=pl.BlockSpec((r, tc), lambda i, me_ref: (0, i))),
        out_shape=jax.ShapeDtypeStruct((r, w), F32),
    )(me.reshape(1), recv, own)


def _adamw(w, g, m, v, name):
    lead = w.shape[:-2]
    r, c = w.shape[-2:]
    tr = max([t for t in range(8, 257, 8) if r % t == 0], default=r)
    c1 = 1.0 - ADAM_B1 ** ADAM_STEP
    c2 = 1.0 - ADAM_B2 ** ADAM_STEP

    def body(w_ref, g_ref, m_ref, v_ref, d_ref, nm_ref, nv_ref):
        gv = g_ref[...]
        nm = ADAM_B1 * m_ref[...] + (1.0 - ADAM_B1) * gv
        nv = ADAM_B2 * v_ref[...] + (1.0 - ADAM_B2) * (gv * gv)
        nm_ref[...] = nm
        nv_ref[...] = nv
        d_ref[...] = -ADAM_LR * ((nm / c1) / (jnp.sqrt(nv / c2) + ADAM_EPS) + ADAM_WD * w_ref[...])

    zeros = (0,) * len(lead)
    spec = pl.BlockSpec((1,) * len(lead) + (tr, c), lambda i: zeros + (i, 0))
    sd = jax.ShapeDtypeStruct(w.shape, F32)
    return pl.pallas_call(
        body, name=name, grid=(r // tr,),
        in_specs=[spec] * 4, out_specs=[spec] * 3, out_shape=[sd] * 3,
    )(w, g, m, v)


def _adamw_in(w_t, m_t, v_t, own_half, swapped, core):
    r, c = SHARD_SHAPES[0]
    tr = max(t for t in range(8, 257, 8) if r % t == 0)
    c1 = 1.0 - ADAM_B1 ** ADAM_STEP
    c2 = 1.0 - ADAM_B2 ** ADAM_STEP

    def body(core_ref, w_ref, m_ref, v_ref, own_ref, sw_ref, d_ref, nm_ref, nv_ref, g_ref):
        own = own_ref[...]
        col_half = lax.broadcasted_iota(jnp.int32, (tr, c), 1) // (c // 2)
        gv = jnp.where(col_half == core_ref[0], jnp.concatenate([own, own], axis=1), sw_ref[...])
        g_ref[0] = gv
        nm = ADAM_B1 * m_ref[0] + (1.0 - ADAM_B1) * gv
        nv = ADAM_B2 * v_ref[0] + (1.0 - ADAM_B2) * (gv * gv)
        nm_ref[0] = nm
        nv_ref[0] = nv
        d_ref[0] = -ADAM_LR * ((nm / c1) / (jnp.sqrt(nv / c2) + ADAM_EPS) + ADAM_WD * w_ref[0])

    full = pl.BlockSpec((1, tr, c), lambda i, core_ref: (0, i, 0))
    sd = jax.ShapeDtypeStruct((1, r, c), F32)
    return pl.pallas_call(
        body, name="adamw_in",
        grid_spec=pltpu.PrefetchScalarGridSpec(
            num_scalar_prefetch=1, grid=(r // tr,),
            in_specs=[full, full, full, pl.BlockSpec((tr, c // 2), lambda i, core_ref: (i, 0)),
                      pl.BlockSpec((tr, c), lambda i, core_ref: (i, 0))],
            out_specs=[full] * 4),
        out_shape=[sd] * 4,
    )(core.reshape(1), w_t, m_t, v_t, own_half, swapped)


ANY = pl.BlockSpec(memory_space=pl.ANY)


def _my_place():
    return lax.axis_index("x"), lax.axis_index("y"), lax.axis_index("c")


HBM = pl.BlockSpec(memory_space=pltpu.HBM)
SEM = pl.BlockSpec(memory_space=pltpu.SEMAPHORE)
DATAFLOW = pltpu.SideEffectType.DATAFLOW_SIDE_EFFECTING


def _near_chips(x, y):
    return [(1 - x, y), (x, 1 - y)]


def _half(mi, hc):
    r, c = SHARD_SHAPES[mi]
    if mi == 0:
        return pl.ds(0, r), pl.ds(pl.multiple_of(hc * (c // 2), 128), c // 2)
    return pl.ds(pl.multiple_of(hc * (r // 2), 16), r // 2), pl.ds(0, c)


def _gather_copies(land_refs, send_sems, recv_sems):
    x, y, c = _my_place()
    out, back = [], []
    for mi in range(N_MATS):
        rows, cols = _half(mi, c)
        mine = land_refs[mi].at[2 * x + y, rows, cols]
        for j, (cx, cy) in enumerate(_near_chips(x, y)):
            sems = dict(send_sem=send_sems.at[mi * 2 + j], recv_sem=recv_sems.at[mi * 2 + j],
                        device_id=(cx, cy, c), device_id_type=MESH)
            out.append(pltpu.make_async_remote_copy(src_ref=mine, dst_ref=mine, **sems))
            got = land_refs[mi].at[2 * cx + cy, rows, cols]
            back.append(pltpu.make_async_remote_copy(src_ref=got, dst_ref=got, **sems))
    return out, back


def _gather_start(landing):
    n = N_MATS

    def body(*refs):
        out, _ = _gather_copies(refs[:n], refs[n], refs[n + 1])
        for cp in out:
            cp.start()
        refs[-1][...] = jnp.zeros_like(refs[-1])

    hbm = [pltpu.HBM(a.shape, a.dtype) for a in landing]
    outs = pl.pallas_call(
        body, name="gather_start",
        out_shape=(pltpu.SemaphoreType.DMA((2 * n,)), pltpu.SemaphoreType.DMA((2 * n,)), *hbm,
                   jax.ShapeDtypeStruct((8, 128), F32)),
        in_specs=[HBM] * n, out_specs=(SEM, SEM, *[HBM] * n, pl.BlockSpec(memory_space=pltpu.VMEM)),
        input_output_aliases={i: 2 + i for i in range(n)},
        compiler_params=pltpu.CompilerParams(has_side_effects=DATAFLOW),
    )(*[pltpu.with_memory_space_constraint(a, pltpu.HBM) for a in landing])
    return outs[:-1], outs[-1]


def _gather_wait(handle, after):
    n = N_MATS

    def body(*refs):
        out, back = _gather_copies(refs[:n], refs[n], refs[n + 1])
        for cp, arrival in zip(out, back):
            cp.wait_send()
            arrival.wait_recv()

    bufs = handle[2:]
    after, after_specs = _after(after)
    res = pl.pallas_call(
        body, name="gather_wait", out_shape=tuple(pltpu.HBM(b.shape, b.dtype) for b in bufs),
        in_specs=[HBM] * n + [SEM, SEM] + after_specs, out_specs=tuple([HBM] * n),
        input_output_aliases={i: i for i in range(n)},
        compiler_params=pltpu.CompilerParams(has_side_effects=DATAFLOW),
    )(*bufs, handle[0], handle[1], *after)
    return list(res)


def _relay_share(gathered):
    n = N_MATS

    def body(*refs):
        out_refs = refs[n:2 * n]
        send_sems, recv_sems = refs[2 * n:]
        x, y, c = _my_place()
        sibling = (x, y, 1 - c)
        relayed = 2 * (x ^ (1 - c)) + (y ^ c)
        relay_to = (x ^ c, y ^ (1 - c), c)
        far = 2 * (1 - x) + (1 - y)
        near = [2 * (1 - x) + y, 2 * x + (1 - y)]

        def copy(k, mi, shard, hc, to):
            blk = out_refs[mi].at[(shard,) + _half(mi, hc)]
            return pltpu.make_async_remote_copy(src_ref=blk, dst_ref=blk, send_sem=send_sems.at[mi * 4 + k],
                                                recv_sem=recv_sems.at[mi * 4 + k], device_id=to, device_id_type=MESH)

        sends = []
        for mi in range(n):
            sends.append(copy(0, mi, relayed, c, relay_to))
            sends += [copy(1 + j, mi, near[j], c, sibling) for j in range(2)]
        for cp in sends:
            cp.start()
        for mi in range(n):
            copy(0, mi, far, c, relay_to).wait_recv()
            cp = copy(3, mi, far, c, sibling)
            cp.start()
            sends.append(cp)
        for mi in range(n):
            for j in range(2):
                copy(1 + j, mi, near[j], 1 - c, sibling).wait_recv()
            copy(3, mi, far, 1 - c, sibling).wait_recv()
        for cp in sends:
            cp.wait_send()

    return pl.pallas_call(
        body, name="relay_share",
        in_specs=[ANY] * n, out_specs=[ANY] * n,
        out_shape=[jax.ShapeDtypeStruct(g.shape, g.dtype) for g in gathered],
        input_output_aliases={i: i for i in range(n)},
        scratch_shapes=[pltpu.SemaphoreType.DMA((4 * n,)), pltpu.SemaphoreType.DMA((4 * n,))],
    )(*gathered)


def _peers(x, y, c):
    out = []
    for k in range(1, 8):
        px, py, pc = x ^ (k >> 2), y ^ ((k >> 1) & 1), c ^ (k & 1)
        out.append((k - 1, (px, py, pc), 4 * px + 2 * py + pc))
    return out


def _exchange_start(parts, name):
    n = len(parts)

    def body(*refs):
        p_refs, land_refs = refs[:n], refs[n:2 * n]
        send_sems, recv_sems, token = refs[2 * n], refs[2 * n + 1], refs[-1]
        x, y, c = _my_place()
        me = 4 * x + 2 * y + c
        for k, dev, peer in _peers(x, y, c):
            for mi in range(n):
                pltpu.make_async_remote_copy(
                    src_ref=p_refs[mi].at[peer], dst_ref=land_refs[mi].at[me], send_sem=send_sems.at[k * n + mi],
                    recv_sem=recv_sems.at[k * n + mi], device_id=dev, device_id_type=MESH).start()
        token[...] = jnp.zeros_like(token)

    hbm = [pltpu.HBM(p.shape, p.dtype) for p in parts]
    outs = pl.pallas_call(
        body, name=name + "_start",
        out_shape=(pltpu.SemaphoreType.DMA((7 * n,)), pltpu.SemaphoreType.DMA((7 * n,)), *hbm, *hbm,
                   jax.ShapeDtypeStruct((8, 128), F32)),
        in_specs=[HBM] * (2 * n), out_specs=(SEM, SEM, *[HBM] * (2 * n), pl.BlockSpec(memory_space=pltpu.VMEM)),
        input_output_aliases={i: 2 + i for i in range(2 * n)},
        compiler_params=pltpu.CompilerParams(has_side_effects=DATAFLOW),
    )(*[pltpu.with_memory_space_constraint(p, pltpu.HBM) for p in parts],
      *[pltpu.with_memory_space_constraint(lax.empty(p.shape, p.dtype), pltpu.HBM) for p in parts])
    return (name, outs[:-1]), outs[-1]


def _exchange_wait(handle, after):
    name, outs = handle
    n = (len(outs) - 2) // 2

    def body(*refs):
        p_refs, land_refs = refs[:n], refs[n:2 * n]
        send_sems, recv_sems = refs[2 * n], refs[2 * n + 1]
        x, y, c = _my_place()
        me = 4 * x + 2 * y + c
        for k, dev, peer in _peers(x, y, c):
            for mi in range(n):
                pltpu.make_async_remote_copy(
                    src_ref=p_refs[mi].at[peer], dst_ref=land_refs[mi].at[me], send_sem=send_sems.at[k * n + mi],
                    recv_sem=recv_sems.at[k * n + mi], device_id=dev, device_id_type=MESH).wait_send()
                slot = land_refs[mi].at[peer]
                pltpu.make_async_remote_copy(
                    src_ref=slot, dst_ref=slot, send_sem=send_sems.at[k * n + mi],
                    recv_sem=recv_sems.at[k * n + mi], device_id=dev, device_id_type=MESH).wait_recv()

    bufs = outs[2:]
    res = pl.pallas_call(
        body, name=name + "_wait", out_shape=tuple(pltpu.HBM(b.shape, b.dtype) for b in bufs),
        in_specs=[HBM] * (2 * n) + [SEM, SEM, ANY], out_specs=tuple([HBM] * (2 * n)),
        input_output_aliases={i: i for i in range(2 * n)},
        compiler_params=pltpu.CompilerParams(has_side_effects=DATAFLOW),
    )(*bufs, outs[0], outs[1], after)
    return list(res[n:])


def _swap_halves(half_in, gvec):
    def body(g_ref, gv_ref, out_ref, rg_ref, send_sems, recv_sems):
        x, y, c = _my_place()
        me = 4 * x + 2 * y + c
        sibling = (x, y, 1 - c)

        def half(hc):
            return out_ref.at[:, pl.ds(pl.multiple_of(hc * 512, 128), 512)]

        sends = [pltpu.make_async_remote_copy(src_ref=g_ref, dst_ref=half(c), send_sem=send_sems.at[7],
                                              recv_sem=recv_sems.at[7], device_id=sibling, device_id_type=MESH)]
        for k, dev, peer in _peers(x, y, c):
            sends.append(pltpu.make_async_remote_copy(src_ref=gv_ref, dst_ref=rg_ref.at[me], send_sem=send_sems.at[k],
                                                      recv_sem=recv_sems.at[k], device_id=dev, device_id_type=MESH))
        for cp in sends:
            cp.start()
        got = half(1 - c)
        pltpu.make_async_remote_copy(src_ref=got, dst_ref=got, send_sem=send_sems.at[7], recv_sem=recv_sems.at[7],
                                     device_id=sibling, device_id_type=MESH).wait_recv()
        for k, dev, peer in _peers(x, y, c):
            got = rg_ref.at[peer]
            pltpu.make_async_remote_copy(src_ref=got, dst_ref=got, send_sem=send_sems.at[k], recv_sem=recv_sems.at[k],
                                         device_id=dev, device_id_type=MESH).wait_recv()
        for cp in sends:
            cp.wait_send()

    return pl.pallas_call(
        body, name="swap_halves",
        in_specs=[ANY, ANY], out_specs=[ANY, ANY],
        out_shape=[jax.ShapeDtypeStruct(SHARD_SHAPES[0], F32), jax.ShapeDtypeStruct((8, 8, N_GVEC), F32)],
        scratch_shapes=[pltpu.SemaphoreType.DMA((8,)), pltpu.SemaphoreType.DMA((8,))],
    )(half_in, gvec)


def _set_slot(arr, block, idx):
    return lax.dynamic_update_slice(arr, block[None], (idx,) + (0,) * block.ndim)


PAD_RUNS = ((6304, 8352, 0), (5280, 6304, COL_Z), (672, 5280, COL_QKV), (0, 640, COL_LAT), (640, 672, COL_LAT + 704))
N_QKV = COL_LAT - COL_QKV


def _qkv_rows_regroup(a, to_padded):
    if to_padded:
        a4 = a.reshape(3, 12, 128, a.shape[1])
        return jnp.stack([a4[0], a4[1], a4[2]], axis=1).reshape(a.shape)
    a4 = a.reshape(12, 3, 128, a.shape[1])
    return jnp.concatenate([a4[:, tq].reshape(N_QKV // 3, a.shape[1]) for tq in range(3)], axis=0)
W_IN_SHARD = 2088


def _full_weights(gathered):
    def cols(a):
        return jnp.concatenate([a[s] for s in range(4)], axis=1)

    w_uq, w_ukv, w_pm, w_pd = [cols(a) for a in gathered[1:5]]
    w_out = gathered[5].reshape(D_MODEL, D_MODEL)
    w_in_t = gathered[0].reshape(4 * W_IN_SHARD, D_MODEL)
    pieces, at = [], 0
    for lo, hi, pad_lo in sorted(PAD_RUNS, key=lambda t: t[2]):
        if pad_lo > at:
            pieces.append(jnp.zeros((pad_lo - at, D_MODEL), w_in_t.dtype))
        pieces.append(_qkv_rows_regroup(w_in_t[lo:hi], True) if pad_lo == COL_QKV else w_in_t[lo:hi])
        at = pad_lo + hi - lo
    pieces.append(jnp.zeros((N_PAD - at, D_MODEL), w_in_t.dtype))
    w_pad_t = jnp.concatenate(pieces, axis=0)
    z32 = jnp.zeros((Q_RANK, 32), w_uq.dtype)
    wuq_pad = jnp.concatenate([t for h in range(MLA_HEADS) for t in (w_uq[:, h * 96:(h + 1) * 96], z32)], axis=1)
    z64 = jnp.zeros((KV_RANK, 64), w_ukv.dtype)
    wk_pad = jnp.concatenate([t for h in range(MLA_HEADS) for t in (w_ukv[:, h * 128:h * 128 + 64], z64)], axis=1)
    wv = jnp.concatenate([w_ukv[:, h * 128 + 64:(h + 1) * 128] for h in range(MLA_HEADS)], axis=1)
    return w_pad_t.T, w_pad_t, wuq_pad, wk_pad, wv, w_pm, w_pd, w_out


W_IN_LAT = 672


def _grad_parts_in_early(dwt_early):
    dwt_early = jnp.concatenate([dwt_early[:COL_QKV], _qkv_rows_regroup(dwt_early[COL_QKV:COL_LAT], False)], axis=0)

    def in_block(s, h):
        cols = slice(h * 512, (h + 1) * 512)
        out = []
        for lo, hi, pad_lo in sorted(PAD_RUNS):
            a_, b_ = max(lo, s * W_IN_SHARD), min(hi, (s + 1) * W_IN_SHARD)
            if a_ < b_:
                out.append(jnp.zeros((b_ - a_, 512), dwt_early.dtype) if pad_lo >= COL_LAT
                           else dwt_early[pad_lo + a_ - lo:pad_lo + b_ - lo, cols])
        return jnp.concatenate(out, axis=0)

    return jnp.stack([in_block(s, h) for s in range(4) for h in range(2)])


def _grad_parts_in_late(dwt_late):
    rows = jnp.concatenate([dwt_late[0:640], dwt_late[704:736]], axis=0)
    zero = jnp.zeros((W_IN_LAT, 512), dwt_late.dtype)
    return jnp.stack([rows[:, 0:512], rows[:, 512:1024]] + [zero] * 6)


def _shard_blocks(m, axis=1):
    n = m.shape[axis] // 4
    cut = (lambda s: m[:, s * n:(s + 1) * n]) if axis == 1 else (lambda s: m[s * n:(s + 1) * n])
    return jnp.stack([cut(s) for s in range(4) for _ in range(2)])


def _grad_parts_mla(dwuq_pad, dwk_pad, dwv):
    d_uq = jnp.concatenate([dwuq_pad[:, h * 128:h * 128 + 96] for h in range(MLA_HEADS)], axis=1)
    d_ukv = jnp.concatenate([t for h in range(MLA_HEADS) for t in (dwk_pad[:, h * 128:h * 128 + 64], dwv[:, h * 64:(h + 1) * 64])],
                            axis=1)
    return [_shard_blocks(d_uq.astype(BF16)), _shard_blocks(d_ukv.astype(BF16))]


def _rope_tables(positions, token=None):
    pos = positions.reshape(SEQ).astype(F32)
    if token is not None:
        pos = pos + token[0, 0]
    lane = jnp.arange(128)

    def table(rot, first, period):
        inv = ROPE_THETA ** (-jnp.arange(0, rot, 2, dtype=F32) / rot)
        half = rot // 2
        off = lane % period - first
        in1, in2 = (off >= 0) & (off < half), (off >= half) & (off < rot)
        inv_lane = jnp.where(in1 | in2, inv[jnp.clip(off % half, 0, half - 1)], 0.0)
        sign = jnp.where(in1, -1.0, 1.0).astype(F32)
        ang = pos[:, None] * inv_lane[None, :]
        return jnp.cos(ang), jnp.sin(ang) * sign[None, :]

    return table(32, 64, 128), table(16, 0, 64)


class _Links:
    def __init__(self, mats, chip, me):
        landing = [_set_slot(lax.empty((4,) + m.shape, m.dtype), m, chip) for m in mats]
        self.gather, self.token = _gather_start(landing)
        self.me, self.sent, self.handles, self.sums = me, {}, {}, {}

    def weights(self, after):
        return _relay_share(_gather_wait(self.gather, after))

    def send(self, blocks, name):
        self.sent[name] = blocks
        self.handles[name], token = _exchange_start(blocks, name)
        return token

    def collect(self, name, after, parts):
        recv = _exchange_wait(self.handles[name], after)
        for r, own, part in zip(recv, self.sent[name], parts):
            self.sums[part] = _sum_parts(r, own, self.me, 64, "sum_grad_" + part)
        return tuple(self.sums[part] for part in parts)


def _device_grads(x, positions, target, gains, links):
    pre_g, q_g, kv_g, post_g = gains
    (mc, ms), (dc, ds) = _rope_tables(positions, links.token)
    h = _prenorm_fwd(x, pre_g, links.token)
    w_pad, w_pad_t, wuq_pad, wk_pad, wv, w_pm, w_pd, w_out = _full_weights(links.weights((h, mc, ms, dc, ds)))

    p_gz = _matmul(h, w_pad, "nn", F32, 1024, 1536, 1024, "in_proj_gates", b_cols=(0, COL_QKV // 1536))
    p_qkv = _matmul(h, w_pad, "nn", F32, 1024, 1536, 1024, "in_proj_dilated", b_cols=(COL_QKV // 1536, N_QKV // 1536),
                    lane_blocks=True)
    p_lat = _matmul(h, w_pad, "nn", F32, 1024, N_LAT, 1024, "in_proj_latent", b_cols=(COL_LAT // N_LAT, 1))
    q, k, v = _mla_prep_fwd(p_lat, q_g, kv_g, wuq_pad, wk_pad, wv, mc, ms)
    ya, lse_m = _mla_flash_fwd(q, k, v)
    qkv = [_dil_prep_fwd(p_qkv, dc, ds, g) for g in range(3)]
    o_g, l_g = zip(*[_dil_attn_fwd(qkv[g], g) for g in range(3)])
    (dp, dy, dya, dyd, yd, lse_d, loss_cols, dg_post, dwpm, dwpd, dwout) = _tail(
        p_gz, ya, o_g, l_g, x, target, w_pm, w_pd, w_out, post_g)

    for g in range(3):
        dp = _dil_attn_bwd(dp, qkv[g], dyd, yd, lse_d, dc, ds, g)
    dw_early = _matmul(dp, h, "tn", BF16, 1536, 1024, 2048, "dw_in_early", a_cols=(0, COL_LAT // 1536))
    token = links.send([_grad_parts_in_early(dw_early), _shard_blocks(dwpm.astype(BF16)), _shard_blocks(dwpd.astype(BF16)),
                        _shard_blocks(dwout.astype(BF16), axis=0)], "exchange_early")

    dq, dk, dv = _mla_flash_bwd(q, k, v, ya, dya, lse_m, token)
    dp, dwuq_pad, dwk_pad, dwv, dg_q, dg_kv = _mla_prep_bwd(dp, p_lat, dq, dk, dv, q_g, kv_g, wuq_pad, wk_pad, wv, mc, ms)
    dw_late = _matmul(dp, h, "tn", BF16, N_LAT, 1024, 2048, "dw_in_late", a_cols=(COL_LAT // N_LAT, 1))
    token = links.send([_grad_parts_in_late(dw_late)] + _grad_parts_mla(dwuq_pad, dwk_pad, dwv), "exchange_late")
    early = links.collect("exchange_early", dw_late, ("in_early", "pm", "pd", "out"))

    grad_x, dg_pre = _dh_prenorm_bwd(dp, w_pad_t, x, dy, pre_g, (token,) + tuple(early))
    links.collect("exchange_late", grad_x, ("in_late", "uq", "ukv"))

    loss_part = jnp.pad((jnp.sum(loss_cols) * (0.5 / D_MODEL)).reshape(1, 1), ((0, 0), (0, N_GVEC - N_GAINS - 1)))
    gvec = jnp.concatenate([dg_pre, dg_q, dg_kv, dg_post, loss_part], axis=1)
    return grad_x, gvec


def kernel(x, positions, pre_norm_g, w_in, q_norm_g, w_uq, kv_norm_g, w_ukv, w_proj_mla, w_proj_dil, w_out, post_norm_g, loss_target, m_pre_norm_g, m_w_in, m_q_norm_g, m_w_uq, m_kv_norm_g, m_w_ukv, m_w_proj_mla, m_w_proj_dil, m_w_out, m_post_norm_g, v_pre_norm_g, v_w_in, v_q_norm_g, v_w_uq, v_kv_norm_g, v_w_ukv, v_w_proj_mla, v_w_proj_dil, v_w_out, v_post_norm_g):
    xi, yi, ci = _my_place()
    chip, me = 2 * xi + yi, 4 * xi + 2 * yi + ci
    mats = [jnp.swapaxes(w_in, 1, 2)] + [w_uq, w_ukv, w_proj_mla, w_proj_dil, w_out]
    mats = [w.reshape(w.shape[1:]).astype(BF16) for w in mats]
    links = _Links(mats, chip, me)
    gains = (pre_norm_g, q_norm_g, kv_norm_g, post_norm_g)
    grad_x, gvec = _device_grads(x[0], positions, loss_target[0], gains, links)

    sums = links.sums
    in_e = sums["in_early"]
    half_in = jnp.concatenate([in_e[:W_IN_LAT] + jnp.where(chip == 0, sums["in_late"], 0.0), in_e[W_IN_LAT:]], axis=0)
    gvec8 = jnp.pad(gvec, ((0, 7), (0, 0)))
    swapped_in, recv_gains = _swap_halves(half_in, gvec8)
    g_gains = _sum_parts(recv_gains, gvec8, me, 8, "sum_gain_parts")[0:1]
    loss = g_gains[0, N_GAINS]
    sw = lambda a: jnp.swapaxes(a, 1, 2)
    d_in, m_in, v_in, g_in = [sw(o) for o in _adamw_in(sw(w_in), sw(m_w_in), sw(v_w_in), half_in, swapped_in, ci)]
    g_mats = [g_in] + [sums[part][None] for part in ("uq", "ukv", "pm", "pd", "out")]

    off = [0, 1024, 1408, 1664, 2688]
    g_gain = [g_gains[:, off[i]:off[i + 1]] for i in range(4)]
    grads = [g_gain[0], g_mats[0], g_gain[1], g_mats[1], g_gain[2], g_mats[2], g_mats[3], g_mats[4], g_mats[5], g_gain[3]]
    ws = [pre_norm_g, w_in, q_norm_g, w_uq, kv_norm_g, w_ukv, w_proj_mla, w_proj_dil, w_out, post_norm_g]
    ms = [m_pre_norm_g, m_w_in, m_q_norm_g, m_w_uq, m_kv_norm_g, m_w_ukv, m_w_proj_mla, m_w_proj_dil, m_w_out, m_post_norm_g]
    vs = [v_pre_norm_g, v_w_in, v_q_norm_g, v_w_uq, v_kv_norm_g, v_w_ukv, v_w_proj_mla, v_w_proj_dil, v_w_out, v_post_norm_g]
    deltas, new_m, new_v = [], [], []
    for i, (w, g, m, v) in enumerate(zip(ws, grads, ms, vs)):
        if w is w_in:
            d_, m_, v_ = d_in, m_in, v_in
        elif w.shape[-1] % 128 and w.shape[-2] % 128 == 0:
            g = jnp.swapaxes(g, 1, 2)
            grads[i] = jnp.swapaxes(g, 1, 2)
            d_, m_, v_ = [jnp.swapaxes(o, 1, 2) for o in
                          _adamw(jnp.swapaxes(w, 1, 2), g, jnp.swapaxes(m, 1, 2), jnp.swapaxes(v, 1, 2), f"adamw_{i}")]
        else:
            d_, m_, v_ = _adamw(w, g, m, v, f"adamw_{i}")
        deltas.append(d_)
        new_m.append(m_)
        new_v.append(v_)
    return (loss, grad_x.reshape(x.shape), *grads, *deltas, *new_m, *new_v)
```

```python
import jax
import jax.numpy as jnp
from jax import lax
from jax.experimental import pallas as pl
from jax.experimental.pallas import tpu as pltpu

F32 = jnp.float32
BF16 = jnp.bfloat16

SEQ = 4096
D_MODEL = 1024
EPS = 1e-6
ROPE_THETA = 500000.0
MLA_HEADS = 8
Q_RANK = 384
KV_RANK = 256
MLA_SCALE = 96.0 ** -0.5
MLA_ROPE_HALF = 16
DIL_DILATIONS = (1, 4, 16)
DIL_ROPE_HALF = 8
DIL_SCALE = 0.125
BAND = 128

N_LAT = 768
COL_Z, COL_QKV, COL_LAT = 2048, 3072, 7680
N_PAD = 8448


def _qkv_block(tq, g, pr):
    return COL_QKV // 128 + (g * 4 + pr) * 3 + tq

IN_SPLITS = (384, 256, 32, 4608, 512, 512, 1024, 1024)

SHARD_SHAPES = ((2088, 1024), (384, 192), (256, 256), (512, 256), (512, 256), (256, 1024))
N_MATS = len(SHARD_SHAPES)
N_GAINS = 2688
N_GVEC = N_GAINS + 128

ADAM_LR, ADAM_B1, ADAM_B2, ADAM_EPS, ADAM_WD, ADAM_STEP = 0.001, 0.9, 0.999, 1e-08, 0.01, 10

VMEM_LIMIT = 56 * 1024 * 1024
NEG = -1e30
MESH = pl.DeviceIdType.MESH


def _cparams(**kw):
    return pltpu.CompilerParams(vmem_limit_bytes=VMEM_LIMIT, **kw)


def _dot(a, b, dims):
    return lax.dot_general(a, b, (dims, ((), ())), preferred_element_type=F32)


def _nn(a, b):
    return _dot(a, b, ((1,), (0,)))


def _nt(a, b):
    return _dot(a, b, ((1,), (1,)))


def _tn(a, b):
    return _dot(a, b, ((0,), (0,)))


def _rope_lanes(shape, half, period, first):
    lane = lax.broadcasted_iota(jnp.int32, shape, len(shape) - 1) % period
    return (lane >= first) & (lane < first + half), (lane >= first + half) & (lane < first + 2 * half)


def _rope_fwd(x, c, s, half, lanes):
    x1, _ = lanes
    return x * c + jnp.where(x1, pltpu.roll(x, 128 - half, 1), pltpu.roll(x, half, 1)) * s


def _rope_bwd(g, c, s, half, lanes):
    x1, x2 = lanes
    gs = g * s
    return g * c + jnp.where(x2, pltpu.roll(gs, half, 1), jnp.where(x1, pltpu.roll(gs, 128 - half, 1), 0.0))


def _sigmoid(x):
    return 1.0 / (1.0 + jnp.exp(-x))


def _after(token):
    tokens = [t for t in (token if isinstance(token, (tuple, list)) else [token]) if t is not None]
    return tokens, [pl.BlockSpec(memory_space=pl.ANY)] * len(tokens)


def _matmul(a, b, mode, out_dtype, tm, tn, tk, name, token=None, b_cols=None, a_cols=None, lane_blocks=False):
    after, after_specs = _after(token)
    if mode == "nn":
        (m, k), n = a.shape, b.shape[1]
        first = 0
        if b_cols is not None:
            first, n = b_cols[0], b_cols[1] * tn
        a_spec = pl.BlockSpec((tm, tk), lambda j, i, kk: (i, kk))
        b_spec = pl.BlockSpec((tk, tn), lambda j, i, kk: (kk, j + first))
        dot = _nn
    elif mode == "nt":
        (m, k), n = a.shape, b.shape[0]
        first = 0
        if b_cols is not None:
            first, n = b_cols[0], b_cols[1] * tn
        a_spec = pl.BlockSpec((tm, tk), lambda j, i, kk: (i, kk))
        b_spec = pl.BlockSpec((tn, tk), lambda j, i, kk: (j + first, kk))
        dot = _nt
    else:
        (k, m), n = a.shape, b.shape[1]
        first = 0
        if a_cols is not None:
            first, m = a_cols[0], a_cols[1] * tm
        a_spec = pl.BlockSpec((tk, tm), lambda j, i, kk: (kk, i + first))
        b_spec = pl.BlockSpec((tk, tn), lambda j, i, kk: (kk, j))
        dot = _tn
    assert m % tm == 0 and n % tn == 0 and k % tk == 0, (name, m, n, k, tm, tn, tk)
    nk = k // tk

    def body(a_ref, b_ref, *rest):
        o_ref, acc_ref = rest[-2:]
        kk = pl.program_id(2)
        part = dot(a_ref[...], b_ref[...])

        @pl.when(kk == 0)
        def _():
            acc_ref[...] = part

        @pl.when(kk > 0)
        def _():
            acc_ref[...] += part

        @pl.when(kk == nk - 1)
        def _():
            if lane_blocks:
                for blk in range(tn // 128):
                    o_ref[blk] = acc_ref[:, blk * 128:(blk + 1) * 128].astype(o_ref.dtype)
            else:
                o_ref[...] = acc_ref[...].astype(o_ref.dtype)

    if lane_blocks:
        out_spec = pl.BlockSpec((tn // 128, tm, 128), lambda j, i, kk: (j, i, 0))
        out_shape = jax.ShapeDtypeStruct((n // 128, m, 128), out_dtype)
    else:
        out_spec = pl.BlockSpec((tm, tn), lambda j, i, kk: (i, j))
        out_shape = jax.ShapeDtypeStruct((m, n), out_dtype)
    return pl.pallas_call(
        body, name=name, grid=(n // tn, m // tm, nk),
        in_specs=[a_spec, b_spec] + after_specs,
        out_specs=out_spec, out_shape=out_shape,
        scratch_shapes=[pltpu.VMEM((tm, tn), F32)],
        compiler_params=_cparams(),
    )(a, b, *after)


def _prenorm_fwd(x, g, token=None):
    tm = 512
    after, after_specs = _after(token)

    def body(x_ref, g_ref, *rest):
        xv = x_ref[...]
        r = lax.rsqrt(jnp.mean(xv * xv, axis=-1, keepdims=True) + EPS)
        rest[-1][...] = (xv * r * g_ref[...]).astype(BF16)

    return pl.pallas_call(
        body, name="prenorm_fwd", grid=(SEQ // tm,),
        in_specs=[pl.BlockSpec((tm, D_MODEL), lambda i: (i, 0)), pl.BlockSpec((1, D_MODEL), lambda i: (0, 0))] + after_specs,
        out_specs=pl.BlockSpec((tm, D_MODEL), lambda i: (i, 0)),
        out_shape=jax.ShapeDtypeStruct((SEQ, D_MODEL), BF16),
    )(x, g, *after)


def _dh_prenorm_bwd(dp, w_pad_t, x, dy, g, token=None):
    tm, tk = 1024, 1408
    nk = N_PAD // tk
    after, after_specs = _after(token)

    def body(a_ref, b_ref, x_ref, dy_ref, g_ref, *rest):
        gx_ref, dg_ref, acc_ref = rest[-3:]
        i, kk = pl.program_id(0), pl.program_id(1)
        part = _nn(a_ref[...], b_ref[...])

        @pl.when(kk == 0)
        def _():
            acc_ref[...] = part

        @pl.when(kk > 0)
        def _():
            acc_ref[...] += part

        @pl.when(kk == nk - 1)
        def _():
            xv = x_ref[...]
            r = lax.rsqrt(jnp.mean(xv * xv, axis=-1, keepdims=True) + EPS)
            n = xv * r
            dhv = acc_ref[...]
            dn = dhv * g_ref[...]
            gx_ref[...] = dy_ref[...] + r * (dn - n * jnp.mean(dn * n, axis=-1, keepdims=True))
            cols = jnp.sum(dhv * n, axis=0, keepdims=True)

            @pl.when(i == 0)
            def _():
                dg_ref[...] = cols

            @pl.when(i > 0)
            def _():
                dg_ref[...] += cols

    row = pl.BlockSpec((tm, D_MODEL), lambda i, kk: (i, 0))
    vec = pl.BlockSpec((1, D_MODEL), lambda i, kk: (0, 0))
    return pl.pallas_call(
        body, name="dh_prenorm_bwd", grid=(SEQ // tm, nk),
        in_specs=[pl.BlockSpec((tm, tk), lambda i, kk: (i, kk)), pl.BlockSpec((tk, D_MODEL), lambda i, kk: (kk, 0)),
                  row, row, vec] + after_specs,
        out_specs=[row, vec],
        out_shape=[jax.ShapeDtypeStruct((SEQ, D_MODEL), F32), jax.ShapeDtypeStruct((1, D_MODEL), F32)],
        scratch_shapes=[pltpu.VMEM((tm, D_MODEL), F32)],
        compiler_params=_cparams(),
    )(dp, w_pad_t, x, dy, g, *after)


def _mla_prep_fwd(p, qg, kvg, wuq, wk, wv, rc, rs):
    tm = 512

    def body(lat_ref, qg_ref, kvg_ref, wuq_ref, wk_ref, wv_ref, c_ref, s_ref, q_ref, k_ref, v_ref):
        c, s = c_ref[...], s_ref[...]
        lanes = _rope_lanes((tm, 128), MLA_ROPE_HALF, 128, 64)
        cq = lat_ref[:, 0:Q_RANK]
        r1 = lax.rsqrt(jnp.mean(cq * cq, axis=-1, keepdims=True) + EPS)
        cqn = (cq * r1 * qg_ref[...]).astype(BF16)
        q = _nn(cqn, wuq_ref[...])
        for h in range(MLA_HEADS):
            sl = slice(h * 128, (h + 1) * 128)
            q_ref[:, sl] = (_rope_fwd(q[:, sl], c, s, MLA_ROPE_HALF, lanes) * MLA_SCALE).astype(BF16)
        ckv = lat_ref[:, Q_RANK:Q_RANK + KV_RANK]
        r2 = lax.rsqrt(jnp.mean(ckv * ckv, axis=-1, keepdims=True) + EPS)
        ckvn = (ckv * r2 * kvg_ref[...]).astype(BF16)
        krr = _rope_fwd(lat_ref[:, Q_RANK + KV_RANK:N_LAT], c, s, MLA_ROPE_HALF, lanes)
        kn = _nn(ckvn, wk_ref[...])
        for h in range(MLA_HEADS):
            sl = slice(h * 128, (h + 1) * 128)
            k_ref[:, sl] = (kn[:, sl] + krr).astype(BF16)
        v_ref[...] = _nn(ckvn, wv_ref[...]).astype(BF16)

    def full(shape):
        return pl.BlockSpec(shape, lambda i: (0, 0))

    def rows(w):
        return pl.BlockSpec((tm, w), lambda i: (i, 0))

    return pl.pallas_call(
        body, name="mla_prep_fwd", grid=(SEQ // tm,),
        in_specs=[pl.BlockSpec((tm, N_LAT), lambda i: (i, 0)),
                  full((1, Q_RANK)), full((1, KV_RANK)), full((Q_RANK, 1024)), full((KV_RANK, 1024)),
                  full((KV_RANK, 512)), rows(128), rows(128)],
        out_specs=[rows(1024), rows(1024), rows(512)],
        out_shape=[jax.ShapeDtypeStruct((SEQ, 1024), BF16), jax.ShapeDtypeStruct((SEQ, 1024), BF16),
                   jax.ShapeDtypeStruct((SEQ, 512), BF16)],
        compiler_params=_cparams(),
    )(p, qg, kvg, wuq, wk, wv, rc, rs)


def _mla_prep_bwd(dp_in, p, dq, dk, dv, qg, kvg, wuq, wk, wv, rc, rs):
    tm = 512

    def body(dp_any, lat_ref, dq_ref, dk_ref, dv_ref, qg_ref, kvg_ref, wuq_ref, wk_ref, wv_ref,
             c_ref, s_ref, dp_ref, dwuq_ref, dwk_ref, dwv_ref, dgq_ref, dgkv_ref, dqb_ref, dkb_ref):
        del dp_any
        c, s = c_ref[...], s_ref[...]
        lanes = _rope_lanes((tm, 128), MLA_ROPE_HALF, 128, 64)
        lane = lax.broadcasted_iota(jnp.int32, (tm, 128), 1)
        dkr = jnp.zeros((tm, 128), F32)
        for h in range(MLA_HEADS):
            sl = slice(h * 128, (h + 1) * 128)
            dqb_ref[:, sl] = _rope_bwd(dq_ref[:, sl] * MLA_SCALE, c, s, MLA_ROPE_HALF, lanes).astype(BF16)
            dkh = dk_ref[:, sl]
            dkr = dkr + dkh
            dkb_ref[:, sl] = jnp.where(lane < 64, dkh, 0.0).astype(BF16)
        dkr = jnp.where((lane >= 64) & (lane < 96), dkr, 0.0)
        dkr = _rope_bwd(dkr, c, s, MLA_ROPE_HALF, lanes)
        dvb = dv_ref[...].astype(BF16)

        cq = lat_ref[:, 0:Q_RANK]
        r1 = lax.rsqrt(jnp.mean(cq * cq, axis=-1, keepdims=True) + EPS)
        n1 = cq * r1
        dcqn = _nt(dqb_ref[...], wuq_ref[...])
        dn1 = dcqn * qg_ref[...]
        dcq = r1 * (dn1 - n1 * jnp.mean(dn1 * n1, axis=-1, keepdims=True))
        pq = jnp.sum(dcqn * n1, axis=0, keepdims=True)

        ckv = lat_ref[:, Q_RANK:Q_RANK + KV_RANK]
        r2 = lax.rsqrt(jnp.mean(ckv * ckv, axis=-1, keepdims=True) + EPS)
        n2 = ckv * r2
        dckvn = _nt(dkb_ref[...], wk_ref[...]) + _nt(dvb, wv_ref[...])
        dn2 = dckvn * kvg_ref[...]
        dckv = r2 * (dn2 - n2 * jnp.mean(dn2 * n2, axis=-1, keepdims=True))
        pkv = jnp.sum(dckvn * n2, axis=0, keepdims=True)
        cqn = (n1 * qg_ref[...]).astype(BF16)
        ckvn = (n2 * kvg_ref[...]).astype(BF16)
        wq, wk_, wv_ = _tn(cqn, dqb_ref[...]), _tn(ckvn, dkb_ref[...]), _tn(ckvn, dvb)

        dp_ref[:, 0:Q_RANK] = dcq.astype(BF16)
        dp_ref[:, Q_RANK:Q_RANK + KV_RANK] = dckv.astype(BF16)
        dp_ref[:, Q_RANK + KV_RANK:N_LAT] = dkr.astype(BF16)

        @pl.when(pl.program_id(0) == 0)
        def _():
            dgq_ref[...] = pq
            dgkv_ref[...] = pkv
            dwuq_ref[...] = wq
            dwk_ref[...] = wk_
            dwv_ref[...] = wv_

        @pl.when(pl.program_id(0) > 0)
        def _():
            dgq_ref[...] += pq
            dgkv_ref[...] += pkv
            dwuq_ref[...] += wq
            dwk_ref[...] += wk_
            dwv_ref[...] += wv_

    def full(shape):
        return pl.BlockSpec(shape, lambda i: (0, 0))

    def rows(w):
        return pl.BlockSpec((tm, w), lambda i: (i, 0))

    lat = pl.BlockSpec((tm, N_LAT), lambda i: (i, 0))
    dlat = pl.BlockSpec((tm, N_LAT), lambda i: (i, COL_LAT // N_LAT))
    return pl.pallas_call(
        body, name="mla_prep_bwd", grid=(SEQ // tm,),
        in_specs=[pl.BlockSpec(memory_space=pl.ANY), lat, rows(1024), rows(1024), rows(512),
                  full((1, Q_RANK)), full((1, KV_RANK)), full((Q_RANK, 1024)), full((KV_RANK, 1024)),
                  full((KV_RANK, 512)), rows(128), rows(128)],
        out_specs=[dlat, full((Q_RANK, 1024)), full((KV_RANK, 1024)), full((KV_RANK, 512)),
                   full((1, Q_RANK)), full((1, KV_RANK))],
        out_shape=[jax.ShapeDtypeStruct((SEQ, N_PAD), BF16), jax.ShapeDtypeStruct((Q_RANK, 1024), F32),
                   jax.ShapeDtypeStruct((KV_RANK, 1024), F32), jax.ShapeDtypeStruct((KV_RANK, 512), F32),
                   jax.ShapeDtypeStruct((1, Q_RANK), F32), jax.ShapeDtypeStruct((1, KV_RANK), F32)],
        input_output_aliases={0: 0},
        scratch_shapes=[pltpu.VMEM((tm, 1024), BF16), pltpu.VMEM((tm, 1024), BF16)],
        compiler_params=_cparams(),
    )(dp_in, p, dq, dk, dv, qg, kvg, wuq, wk, wv, rc, rs)


FLASH_T = 1024


def _head_half(shape, hh):
    lane = lax.broadcasted_iota(jnp.int32, shape, 1)
    return (lane < 64) if hh == 0 else (lane >= 64)


def _diag_keep(nr, nk):
    row = lax.broadcasted_iota(jnp.int32, (nr, nk), 0)
    col = lax.broadcasted_iota(jnp.int32, (nr, nk), 1)
    return row + (nk - nr) >= col


def _tri_steps(nb, q_major):
    if q_major:
        pairs = [(i, kb) for i in range(nb) for kb in range(i + 1)]
    else:
        pairs = [(i, kb) for kb in range(nb) for i in range(kb, nb)]
    return jnp.asarray([p[0] for p in pairs], jnp.int32), jnp.asarray([p[1] for p in pairs], jnp.int32)


def _mla_flash_fwd(q, k, v):
    t = FLASH_T
    nb = SEQ // t
    qtab, ktab = _tri_steps(nb, True)

    def body(qi_ref, ki_ref, q_ref, k_ref, v_ref, o_ref, lse_ref, m_scr, l_scr, acc_scr):
        step = pl.program_id(1)
        i, kb = qi_ref[step], ki_ref[step]

        @pl.when(kb == 0)
        def _():
            m_scr[...] = jnp.full_like(m_scr, NEG)
            l_scr[...] = jnp.zeros_like(l_scr)
            acc_scr[...] = jnp.zeros_like(acc_scr)

        def update(r0, nr, nk, diagonal):
            rs = slice(r0, r0 + nr)
            vv = v_ref[0:nk, :]
            for hh in range(2):
                sl = slice(hh * 128, (hh + 1) * 128)
                s = _nt(q_ref[rs, sl], k_ref[0:nk, sl])
                if diagonal:
                    s = jnp.where(_diag_keep(nr, nk), s, NEG)
                m_prev = m_scr[hh, rs, :]
                m_new = jnp.maximum(m_prev, jnp.max(s, axis=-1, keepdims=True))
                pr = jnp.exp(s - jnp.tile(m_new, (1, nk // 128)))
                alpha = jnp.exp(m_prev - m_new)
                l_scr[hh, rs, :] = alpha * l_scr[hh, rs, :] + jnp.sum(pr, axis=-1, keepdims=True)
                acc_scr[hh, rs, :] = alpha * acc_scr[hh, rs, :] + _nn(pr.astype(BF16), vv)
                m_scr[hh, rs, :] = m_new

        @pl.when(kb < i)
        def _():
            update(0, t, t, False)

        @pl.when(kb == i)
        def _():
            update(0, t // 2, t // 2, True)
            update(t // 2, t // 2, t, True)
            o0 = acc_scr[0] / l_scr[0]
            o1 = acc_scr[1] / l_scr[1]
            o_ref[...] = jnp.where(_head_half((t, 128), 0), o0, o1)
            for hh in range(2):
                lse_ref[:, hh * 128:(hh + 1) * 128] = m_scr[hh] + jnp.log(l_scr[hh])

    grid_spec = pltpu.PrefetchScalarGridSpec(
        num_scalar_prefetch=2, grid=(4, qtab.shape[0]),
        in_specs=[pl.BlockSpec((t, 256), lambda j, s, qi, ki: (qi[s], j)),
                  pl.BlockSpec((t, 256), lambda j, s, qi, ki: (ki[s], j)),
                  pl.BlockSpec((t, 128), lambda j, s, qi, ki: (ki[s], j))],
        out_specs=[pl.BlockSpec((t, 128), lambda j, s, qi, ki: (qi[s], j)),
                   pl.BlockSpec((t, 256), lambda j, s, qi, ki: (qi[s], j))],
        scratch_shapes=[pltpu.VMEM((2, t, 128), F32), pltpu.VMEM((2, t, 128), F32), pltpu.VMEM((2, t, 128), F32)])
    return pl.pallas_call(
        body, name="mla_flash_fwd", grid_spec=grid_spec,
        out_shape=[jax.ShapeDtypeStruct((SEQ, 512), F32), jax.ShapeDtypeStruct((SEQ, 1024), F32)],
        compiler_params=_cparams(),
    )(qtab, ktab, q, k, v)


def _mla_flash_bwd(q, k, v, o, do, lse, token=None):
    t = FLASH_T
    nb = SEQ // t
    qtab, ktab = _tri_steps(nb, False)
    after, after_specs = _after(token)

    def body(qi_ref, ki_ref, q_ref, k_ref, v_ref, o_ref, do_ref, lse_ref, *rest):
        dq_ref, dk_ref, dv_ref, dk_scr, dv_scr = rest[-5:]
        step = pl.program_id(1)
        i, kb = qi_ref[step], ki_ref[step]

        @pl.when(step == 0)
        def _():
            dq_ref[...] = jnp.zeros_like(dq_ref)

        @pl.when(i == kb)
        def _():
            dk_scr[...] = jnp.zeros_like(dk_scr)
            dv_scr[...] = jnp.zeros_like(dv_scr)

        def update(r0, nr, nk, diagonal):
            rs = slice(r0, r0 + nr)
            vv = v_ref[0:nk, :]
            ov = o_ref[rs, :]
            dov = do_ref[rs, :]
            rows = pl.ds(pl.multiple_of(i * t + r0, t // 2), nr)
            for hh in range(2):
                sl = slice(hh * 128, (hh + 1) * 128)
                qh, kh = q_ref[rs, sl], k_ref[0:nk, sl]
                s = _nt(qh, kh)
                if diagonal:
                    s = jnp.where(_diag_keep(nr, nk), s, NEG)
                pr = jnp.exp(s - jnp.tile(lse_ref[rs, sl], (1, nk // 128)))
                dom = jnp.where(_head_half((nr, 128), hh), dov, 0.0)
                domb = dom.astype(BF16)
                dv_scr[0:nk, :] += _tn(pr.astype(BF16), domb)
                dpr = _nt(domb, vv)
                delta = jnp.sum(dom * ov, axis=-1, keepdims=True)
                ds = (pr * (dpr - delta)).astype(BF16)
                dq_ref[rows, sl] += _nn(ds, kh)
                dk_scr[hh, 0:nk, :] += _tn(ds, qh)

        @pl.when(i > kb)
        def _():
            update(0, t, t, False)

        @pl.when(i == kb)
        def _():
            update(0, t // 2, t // 2, True)
            update(t // 2, t // 2, t, True)

        @pl.when(i == nb - 1)
        def _():
            dk_ref[:, 0:128] = dk_scr[0]
            dk_ref[:, 128:256] = dk_scr[1]
            dv_ref[...] = dv_scr[...]

    qi_map = lambda j, s, qi, ki: (qi[s], j)
    ki_map = lambda j, s, qi, ki: (ki[s], j)
    grid_spec = pltpu.PrefetchScalarGridSpec(
        num_scalar_prefetch=2, grid=(4, qtab.shape[0]),
        in_specs=[pl.BlockSpec((t, 256), qi_map), pl.BlockSpec((t, 256), ki_map), pl.BlockSpec((t, 128), ki_map),
                  pl.BlockSpec((t, 128), qi_map), pl.BlockSpec((t, 128), qi_map), pl.BlockSpec((t, 256), qi_map)]
        + after_specs,
        out_specs=[pl.BlockSpec((SEQ, 256), lambda j, s, qi, ki: (0, j)), pl.BlockSpec((t, 256), ki_map),
                   pl.BlockSpec((t, 128), ki_map)],
        scratch_shapes=[pltpu.VMEM((2, t, 128), F32), pltpu.VMEM((t, 128), F32)])
    return pl.pallas_call(
        body, name="mla_flash_bwd", grid_spec=grid_spec,
        out_shape=[jax.ShapeDtypeStruct((SEQ, 1024), F32), jax.ShapeDtypeStruct((SEQ, 1024), F32),
                   jax.ShapeDtypeStruct((SEQ, 512), F32)],
        compiler_params=_cparams(),
    )(qtab, ktab, q, k, v, o, do, lse, *after)


def _strided(start, size, d):
    return pl.ds(start, size) if d == 1 else pl.ds(start, size, stride=d)


def _dil_prep_fwd(p, rc, rs, g):
    d = DIL_DILATIONS[g]
    sub_len = SEQ // d
    ch = min(sub_len, 512)

    def body(p_ref, c_ref, s_ref, o_ref, x_scr):
        tq = pl.program_id(0)
        lanes = _rope_lanes((ch, 128), DIL_ROPE_HALF, 64, 0)
        o_ref[0, 0, 0:BAND, :] = jnp.zeros((BAND, 128), BF16)

        @pl.when(tq < 2)
        def _():
            mult = jnp.where(tq == 0, DIL_SCALE, 1.0).astype(F32)
            for c0 in range(0, SEQ, ch):
                rows = pl.ds(c0, ch)
                x_scr[rows, :] = _rope_fwd(p_ref[rows, :], c_ref[rows, :] * mult, s_ref[rows, :] * mult, DIL_ROPE_HALF, lanes)

        def gather(take):
            for r in range(d):
                for c0 in range(0, sub_len, ch):
                    at = BAND + r * sub_len + c0
                    o_ref[0, 0, at:at + ch, :] = take(_strided(r + c0 * d, ch, d)).astype(BF16)

        @pl.when(tq < 2)
        def _():
            gather(lambda rows: x_scr[rows, :])

        @pl.when(tq == 2)
        def _():
            gather(lambda rows: p_ref[rows, :])

    tab = pl.BlockSpec((SEQ, 128), lambda tq, pr: (0, 0))
    return pl.pallas_call(
        body, name=f"dil_prep_fwd_g{g}", grid=(3, 4),
        in_specs=[pl.BlockSpec((None, SEQ, 128), lambda tq, pr: (_qkv_block(tq, g, pr) - COL_QKV // 128, 0, 0)), tab, tab],
        out_specs=pl.BlockSpec((1, 1, BAND + SEQ, 128), lambda tq, pr: (tq, pr, 0, 0)),
        out_shape=jax.ShapeDtypeStruct((3, 4, BAND + SEQ, 128), BF16),
        scratch_shapes=[pltpu.VMEM((SEQ, 128), F32)],
        compiler_params=_cparams(),
    )(p, rc, rs)


DIL_ST_FWD, DIL_ST_BWD = 1024, 2048


def _band_keep(g, b, t, nb):
    nbs = SEQ // DIL_DILATIONS[g] // BAND
    row = lax.broadcasted_iota(jnp.int32, (BAND, 2 * BAND), 0)
    col = lax.broadcasted_iota(jnp.int32, (BAND, 2 * BAND), 1)
    cur = (col >= BAND) & (row >= col - BAND)
    prev = (col < BAND) & (col >= row)
    if nbs >= nb:
        if b > 0:
            return cur | prev
        return cur | (prev & ((t * nb) % nbs != 0))
    return cur | prev if b % nbs else cur


def _dil_tok(g, b, t, nb):
    d = DIL_DILATIONS[g]
    nbs = SEQ // d // BAND
    gb = t * nb + b
    return _strided((gb % nbs) * BAND * d + gb // nbs, BAND, d)


def _dil_attn_fwd(qkv, g):
    DIL_ST, DIL_NB = DIL_ST_FWD, DIL_ST_FWD // BAND

    def body(q_ref, k_ref, v_ref, o_ref, l_ref, s_scr, p_scr, o_scr):
        t = pl.program_id(1)
        base = t * DIL_ST
        half0 = _head_half((DIL_ST, 128), 0)
        lse_h = []
        for hh in range(2):
            half = _head_half((BAND, 128), hh)
            for b in range(DIL_NB):
                qv = q_ref[0, 0, pl.ds(pl.multiple_of(base + (b + 1) * BAND, BAND), BAND), :]
                k2 = k_ref[0, 0, pl.ds(pl.multiple_of(base + b * BAND, BAND), 2 * BAND), :]
                sb = _nt(jnp.where(half, qv, jnp.zeros_like(qv)), k2)
                s_scr[b * BAND:(b + 1) * BAND, :] = jnp.where(_band_keep(g, b, t, DIL_NB), sb, NEG)
            s = s_scr[...]
            m = jnp.max(s, axis=-1, keepdims=True)
            pr = jnp.exp(s - m)
            den = jnp.sum(pr, axis=-1, keepdims=True)
            p_scr[...] = pr.astype(BF16)
            for b in range(DIL_NB):
                v2 = v_ref[0, 0, pl.ds(pl.multiple_of(base + b * BAND, BAND), 2 * BAND), :]
                o_scr[hh, b * BAND:(b + 1) * BAND, :] = _nn(p_scr[b * BAND:(b + 1) * BAND, :], v2)
            o_scr[hh] = o_scr[hh] / den
            lse_h.append(m + jnp.log(den))
        out = jnp.where(half0, o_scr[0], o_scr[1])
        lse = jnp.where(half0, lse_h[0], lse_h[1])
        for b in range(DIL_NB):
            tok = _dil_tok(g, b, t, DIL_NB)
            o_ref[tok, :] = out[b * BAND:(b + 1) * BAND, :]
            l_ref[tok, :] = lse[b * BAND:(b + 1) * BAND, :]

    def inp(tq):
        return pl.BlockSpec((1, 1, BAND + SEQ, 128), lambda pr, t: (tq, pr, 0, 0))

    out = pl.BlockSpec((SEQ, 128), lambda pr, t: (0, pr))
    return pl.pallas_call(
        body, name=f"dil_attn_fwd_g{g}", grid=(4, SEQ // DIL_ST),
        in_specs=[inp(0), inp(1), inp(2)], out_specs=[out, out],
        out_shape=[jax.ShapeDtypeStruct((SEQ, 512), F32), jax.ShapeDtypeStruct((SEQ, 512), F32)],
        scratch_shapes=[pltpu.VMEM((DIL_ST, 2 * BAND), F32), pltpu.VMEM((DIL_ST, 2 * BAND), BF16),
                        pltpu.VMEM((2, DIL_ST, 128), F32)],
        compiler_params=_cparams(),
    )(qkv, qkv, qkv)


def _dil_attn_bwd(dp_in, qkv, dyd, yd, lse_all, rc, rs, g, token=None):
    d = DIL_DILATIONS[g]
    sub_len = SEQ // d
    DIL_ST, DIL_NB = DIL_ST_BWD, DIL_ST_BWD // BAND
    nst = SEQ // DIL_ST
    after, after_specs = _after(token)
    ch = 512

    def body(dp_any, q_ref, k_ref, v_ref, do_ref, y_ref, l_ref, c_ref, sn_ref, *rest):
        dp_ref, tok_scr, dk_scr, dv_scr, s_scr, dp_scr, p_scr, ds_scr, do_scr, y_scr, l_scr, dq_scr = rest[-12:]
        del dp_any
        t = pl.program_id(1)
        base = t * DIL_ST

        @pl.when(t == 0)
        def _():
            dk_scr[...] = jnp.zeros_like(dk_scr)
            dv_scr[...] = jnp.zeros_like(dv_scr)

        for b in range(DIL_NB):
            tok = _dil_tok(g, b, t, DIL_NB)
            do_scr[b * BAND:(b + 1) * BAND, :] = do_ref[tok, :]
            y_scr[b * BAND:(b + 1) * BAND, :] = y_ref[tok, :]
            l_scr[b * BAND:(b + 1) * BAND, :] = l_ref[tok, :]
        for hh in range(2):
            half = _head_half((BAND, 128), hh)
            half_st = _head_half((DIL_ST, 128), hh)
            dom = jnp.where(half_st, do_scr[...], 0.0)
            delta = jnp.sum(dom * y_scr[...], axis=-1, keepdims=True)
            lcol = jnp.max(jnp.where(half_st, l_scr[...], NEG), axis=-1, keepdims=True)
            for b in range(DIL_NB):
                rows = slice(b * BAND, (b + 1) * BAND)
                qv = q_ref[0, 0, pl.ds(pl.multiple_of(base + (b + 1) * BAND, BAND), BAND), :]
                band = pl.ds(pl.multiple_of(base + b * BAND, BAND), 2 * BAND)
                sb = _nt(jnp.where(half, qv, jnp.zeros_like(qv)), k_ref[0, 0, band, :])
                s_scr[rows, :] = jnp.where(_band_keep(g, b, t, DIL_NB), sb, NEG)
                dp_scr[rows, :] = _nt(dom[rows, :].astype(BF16), v_ref[0, 0, band, :])
            pr = jnp.exp(s_scr[...] - lcol)
            p_scr[...] = pr.astype(BF16)
            ds_scr[...] = (pr * (dp_scr[...] - delta)).astype(BF16)
            for b in range(DIL_NB):
                rows = slice(b * BAND, (b + 1) * BAND)
                qv = q_ref[0, 0, pl.ds(pl.multiple_of(base + (b + 1) * BAND, BAND), BAND), :]
                band = pl.ds(pl.multiple_of(base + b * BAND, BAND), 2 * BAND)
                dqb = jnp.where(half, _nn(ds_scr[rows, :], k_ref[0, 0, band, :]), 0.0)
                if hh == 0:
                    dq_scr[rows, :] = dqb
                else:
                    dq_scr[rows, :] += dqb
                half2 = _head_half((2 * BAND, 128), hh)
                dk_scr[band, :] += jnp.where(half2, _tn(ds_scr[rows, :], qv), 0.0)
                dv_scr[band, :] += _tn(p_scr[rows, :], dom[rows, :].astype(BF16))
        for b in range(DIL_NB):
            tok_scr[pl.ds(0, 1), _dil_tok(g, b, t, DIL_NB), :] = dq_scr[b * BAND:(b + 1) * BAND, :][None]

        @pl.when(t == nst - 1)
        def _():
            for r in range(d):
                rows = _strided(r, sub_len, d)
                tok_scr[pl.ds(1, 1), rows, :] = dk_scr[BAND + r * sub_len:BAND + (r + 1) * sub_len, :][None]
                tok_scr[pl.ds(2, 1), rows, :] = dv_scr[BAND + r * sub_len:BAND + (r + 1) * sub_len, :][None]
            lanes = _rope_lanes((ch, 128), DIL_ROPE_HALF, 64, 0)
            for c0 in range(0, SEQ, ch):
                rows = slice(c0, c0 + ch)
                cv, sv = c_ref[rows, :], sn_ref[rows, :]
                dp_ref[rows, 0:128] = _rope_bwd(tok_scr[0, rows, :], cv * DIL_SCALE, sv * DIL_SCALE, DIL_ROPE_HALF, lanes).astype(BF16)
                dp_ref[rows, 128:256] = _rope_bwd(tok_scr[1, rows, :], cv, sv, DIL_ROPE_HALF, lanes).astype(BF16)
                dp_ref[rows, 256:384] = tok_scr[2, rows, :].astype(BF16)

    def inp(tq):
        return pl.BlockSpec((1, 1, BAND + SEQ, 128), lambda pr, t: (tq, pr, 0, 0))

    tok_spec = pl.BlockSpec((SEQ, 128), lambda pr, t: (0, pr))
    tab = pl.BlockSpec((SEQ, 128), lambda pr, t: (0, 0))
    st = (DIL_ST, 2 * BAND)
    return pl.pallas_call(
        body, name=f"dil_attn_bwd_g{g}", grid=(4, nst),
        in_specs=[pl.BlockSpec(memory_space=pl.ANY), inp(0), inp(1), inp(2), tok_spec, tok_spec, tok_spec, tab, tab]
        + after_specs,
        out_specs=pl.BlockSpec((SEQ, 384), lambda pr, t: (0, _qkv_block(0, g, pr) // 3)),
        out_shape=jax.ShapeDtypeStruct((SEQ, N_PAD), BF16),
        input_output_aliases={0: 0},
        scratch_shapes=[pltpu.VMEM((3, SEQ, 128), F32),
                        pltpu.VMEM((BAND + SEQ, 128), F32), pltpu.VMEM((BAND + SEQ, 128), F32),
                        pltpu.VMEM(st, F32), pltpu.VMEM(st, F32), pltpu.VMEM(st, BF16), pltpu.VMEM(st, BF16),
                        pltpu.VMEM((DIL_ST, 128), F32), pltpu.VMEM((DIL_ST, 128), F32), pltpu.VMEM((DIL_ST, 128), F32),
                        pltpu.VMEM((DIL_ST, 128), F32)],
        compiler_params=_cparams(),
    )(dp_in, qkv, qkv, qkv, dyd, yd, lse_all, rc, rs, *after)


TAIL_T = 256


def _tail(p, ya, o_g, l_g, x, target, wpm, wpd, wout, post_g):
    tm = TAIL_T

    def body(pgz_ref, ya_ref, o0_ref, o1_ref, o2_ref, l0_ref, l1_ref, l2_ref, x_ref, t_ref,
             wpm_ref, wpd_ref, wout_ref, pg_ref,
             dp_ref, dy_ref, dya_ref, dyd_ref, yd_ref, lse_ref, loss_ref, dgp_ref, dwpm_ref, dwpd_ref, dwout_ref):
        l0, l1, l2 = l0_ref[...], l1_ref[...], l2_ref[...]
        mx = jnp.maximum(jnp.maximum(l0, l1), l2)
        e0, e1, e2 = jnp.exp(l0 - mx), jnp.exp(l1 - mx), jnp.exp(l2 - mx)
        den = e0 + e1 + e2
        yd = (e0 * o0_ref[...] + e1 * o1_ref[...] + e2 * o2_ref[...]) / den
        yd_ref[...] = yd
        lse_ref[...] = mx + jnp.log(den)
        ya = ya_ref[...]

        gm, gd = pgz_ref[:, 0:1024], pgz_ref[:, 1024:2048]
        zm, zd = pgz_ref[:, 2048:2560], pgz_ref[:, 2560:3072]
        szm, szd = _sigmoid(zm), _sigmoid(zd)
        sm, sd = zm * szm, zd * szd
        ua = (ya * sm).astype(BF16)
        ud = (yd * sd).astype(BF16)
        pa = _nn(ua, wpm_ref[...])
        pd = _nn(ud, wpd_ref[...])
        sgm, sgd = _sigmoid(gm), _sigmoid(gd)
        mg = (sgm * pa + sgd * pd).astype(BF16)
        t = _nn(mg, wout_ref[...])
        r3 = lax.rsqrt(jnp.mean(t * t, axis=-1, keepdims=True) + EPS)
        n = t * r3
        pg = pg_ref[...]
        err = x_ref[...] + n * pg - t_ref[...]
        lpart = jnp.sum(err * err, axis=0, keepdims=True)

        dy = err * (1.0 / D_MODEL)
        dy_ref[...] = dy
        gpart = jnp.sum(dy * n, axis=0, keepdims=True)
        dn = dy * pg
        dt = (r3 * (dn - n * jnp.mean(dn * n, axis=-1, keepdims=True))).astype(BF16)
        dmg = _nt(dt, wout_ref[...])
        dpa = (dmg * sgm).astype(BF16)
        dpd = (dmg * sgd).astype(BF16)
        dp_ref[:, 0:1024] = (dmg * pa * sgm * (1.0 - sgm)).astype(BF16)
        dp_ref[:, 1024:2048] = (dmg * pd * sgd * (1.0 - sgd)).astype(BF16)
        dua = _nt(dpa, wpm_ref[...])
        dud = _nt(dpd, wpd_ref[...])
        dya_ref[...] = dua * sm
        dyd_ref[...] = dud * sd
        dp_ref[:, 2048:2560] = (dua * ya * szm * (1.0 + zm * (1.0 - szm))).astype(BF16)
        dp_ref[:, 2560:3072] = (dud * yd * szd * (1.0 + zd * (1.0 - szd))).astype(BF16)

        wpm, wpd, wout = _tn(ua, dpa), _tn(ud, dpd), _tn(mg, dt)

        @pl.when(pl.program_id(0) == 0)
        def _():
            loss_ref[...] = lpart
            dgp_ref[...] = gpart
            dwpm_ref[...] = wpm
            dwpd_ref[...] = wpd
            dwout_ref[...] = wout

        @pl.when(pl.program_id(0) > 0)
        def _():
            loss_ref[...] += lpart
            dgp_ref[...] += gpart
            dwpm_ref[...] += wpm
            dwpd_ref[...] += wpd
            dwout_ref[...] += wout

    def rows(w):
        return pl.BlockSpec((tm, w), lambda i: (i, 0))

    def full(shape):
        return pl.BlockSpec(shape, lambda i: (0, 0))

    def sds(w, dt):
        return jax.ShapeDtypeStruct((SEQ, w), dt)

    return pl.pallas_call(
        body, name="tail", grid=(SEQ // tm,),
        in_specs=[rows(3072), rows(512), rows(512), rows(512), rows(512), rows(512), rows(512), rows(512),
                  rows(1024), rows(1024), full((512, 1024)), full((512, 1024)), full((1024, 1024)), full((1, 1024))],
        out_specs=[rows(3072), rows(1024), rows(512), rows(512), rows(512), rows(512), full((1, 1024)), full((1, 1024)),
                   full((512, 1024)), full((512, 1024)), full((1024, 1024))],
        out_shape=[sds(N_PAD, BF16), sds(1024, F32), sds(512, F32), sds(512, F32), sds(512, F32), sds(512, F32),
                   jax.ShapeDtypeStruct((1, 1024), F32), jax.ShapeDtypeStruct((1, 1024), F32),
                   jax.ShapeDtypeStruct((512, 1024), F32), jax.ShapeDtypeStruct((512, 1024), F32),
                   jax.ShapeDtypeStruct((1024, 1024), F32)],
        compiler_params=_cparams(),
    )(p, ya, o_g[0], o_g[1], o_g[2], l_g[0], l_g[1], l_g[2], x, target, wpm, wpd, wout, post_g)


def _sum_parts(recv, own, me, tr, name):
    n, r, w = recv.shape
    if r % tr:
        return _sum_parts_cols(recv, own, me, name)
    own_spec = (pl.BlockSpec((tr, w), lambda i, me_ref: (i, 0)) if own.ndim == 2
                else pl.BlockSpec((None, tr, w), lambda i, me_ref: (me_ref[0], i, 0)))

    def body(me_ref, p_ref, own_ref, o_ref):
        mine = own_ref[...].astype(F32)
        acc = jnp.zeros((tr, w), F32)
        for s in range(n):
            acc = acc + jnp.where(me_ref[0] == s, mine, p_ref[s].astype(F32))
        o_ref[...] = acc

    return pl.pallas_call(
        body, name=name,
        grid_spec=pltpu.PrefetchScalarGridSpec(
            num_scalar_prefetch=1, grid=(r // tr,),
            in_specs=[pl.BlockSpec((n, tr, w), lambda i, me_ref: (0, i, 0)), own_spec],
            out_specs=pl.BlockSpec((tr, w), lambda i, me_ref: (i, 0))),
        out_shape=jax.ShapeDtypeStruct((r, w), F32),
    )(me.reshape(1), recv, own)


def _sum_parts_cols(recv, own, me, name):
    n, r, w = recv.shape
    tc = 128

    def body(me_ref, p_ref, own_ref, o_ref):
        mine = own_ref[...].astype(F32)
        acc = jnp.zeros((r, tc), F32)
        for s in range(n):
            acc = acc + jnp.where(me_ref[0] == s, mine, p_ref[s].astype(F32))
        o_ref[...] = acc

    return pl.pallas_call(
        body, name=name,
        grid_spec=pltpu.PrefetchScalarGridSpec(
            num_scalar_prefetch=1, grid=(w // tc,),
            in_specs=[pl.BlockSpec((n, r, tc), lambda i, me_ref: (0, 0, i)),
                      pl.BlockSpec((None, r, tc), lambda i, me_ref: (me_ref[0], 0, i))],
            out_specs=pl.BlockSpec((r, tc), lambda i, me_ref: (0, i))),
        out_shape=jax.ShapeDtypeStruct((r, w), F32),
    )(me.reshape(1), recv, own)


def _adamw(w, g, m, v, name):
    lead = w.shape[:-2]
    r, c = w.shape[-2:]
    tr = max([t for t in range(8, 257, 8) if r % t == 0], default=r)
    c1 = 1.0 - ADAM_B1 ** ADAM_STEP
    c2 = 1.0 - ADAM_B2 ** ADAM_STEP

    def body(w_ref, g_ref, m_ref, v_ref, d_ref, nm_ref, nv_ref):
        gv = g_ref[...]
        nm = ADAM_B1 * m_ref[...] + (1.0 - ADAM_B1) * gv
        nv = ADAM_B2 * v_ref[...] + (1.0 - ADAM_B2) * (gv * gv)
        nm_ref[...] = nm
        nv_ref[...] = nv
        d_ref[...] = -ADAM_LR * ((nm / c1) / (jnp.sqrt(nv / c2) + ADAM_EPS) + ADAM_WD * w_ref[...])

    zeros = (0,) * len(lead)
    spec = pl.BlockSpec((1,) * len(lead) + (tr, c), lambda i: zeros + (i, 0))
    sd = jax.ShapeDtypeStruct(w.shape, F32)
    return pl.pallas_call(
        body, name=name, grid=(r // tr,),
        in_specs=[spec] * 4, out_specs=[spec] * 3, out_shape=[sd] * 3,
    )(w, g, m, v)


def _adamw_in(w_t, m_t, v_t, own_half, swapped, core):
    r, c = SHARD_SHAPES[0]
    tr = max(t for t in range(8, 257, 8) if r % t == 0)
    c1 = 1.0 - ADAM_B1 ** ADAM_STEP
    c2 = 1.0 - ADAM_B2 ** ADAM_STEP

    def body(core_ref, w_ref, m_ref, v_ref, own_ref, sw_ref, d_ref, nm_ref, nv_ref, g_ref):
        own = own_ref[...]
        col_half = lax.broadcasted_iota(jnp.int32, (tr, c), 1) // (c // 2)
        gv = jnp.where(col_half == core_ref[0], jnp.concatenate([own, own], axis=1), sw_ref[...])
        g_ref[0] = gv
        nm = ADAM_B1 * m_ref[0] + (1.0 - ADAM_B1) * gv
        nv = ADAM_B2 * v_ref[0] + (1.0 - ADAM_B2) * (gv * gv)
        nm_ref[0] = nm
        nv_ref[0] = nv
        d_ref[0] = -ADAM_LR * ((nm / c1) / (jnp.sqrt(nv / c2) + ADAM_EPS) + ADAM_WD * w_ref[0])

    full = pl.BlockSpec((1, tr, c), lambda i, core_ref: (0, i, 0))
    sd = jax.ShapeDtypeStruct((1, r, c), F32)
    return pl.pallas_call(
        body, name="adamw_in",
        grid_spec=pltpu.PrefetchScalarGridSpec(
            num_scalar_prefetch=1, grid=(r // tr,),
            in_specs=[full, full, full, pl.BlockSpec((tr, c // 2), lambda i, core_ref: (i, 0)),
                      pl.BlockSpec((tr, c), lambda i, core_ref: (i, 0))],
            out_specs=[full] * 4),
        out_shape=[sd] * 4,
    )(core.reshape(1), w_t, m_t, v_t, own_half, swapped)


ANY = pl.BlockSpec(memory_space=pl.ANY)


def _my_place():
    return lax.axis_index("x"), lax.axis_index("y"), lax.axis_index("c")


HBM = pl.BlockSpec(memory_space=pltpu.HBM)
SEM = pl.BlockSpec(memory_space=pltpu.SEMAPHORE)
DATAFLOW = pltpu.SideEffectType.DATAFLOW_SIDE_EFFECTING


def _near_chips(x, y):
    return [(1 - x, y), (x, 1 - y)]


def _half(mi, hc):
    r, c = SHARD_SHAPES[mi]
    if mi == 0:
        return pl.ds(0, r), pl.ds(pl.multiple_of(hc * (c // 2), 128), c // 2)
    return pl.ds(pl.multiple_of(hc * (r // 2), 16), r // 2), pl.ds(0, c)


def _gather_copies(land_refs, send_sems, recv_sems):
    x, y, c = _my_place()
    out, back = [], []
    for mi in range(N_MATS):
        rows, cols = _half(mi, c)
        mine = land_refs[mi].at[2 * x + y, rows, cols]
        for j, (cx, cy) in enumerate(_near_chips(x, y)):
            sems = dict(send_sem=send_sems.at[mi * 2 + j], recv_sem=recv_sems.at[mi * 2 + j],
                        device_id=(cx, cy, c), device_id_type=MESH)
            out.append(pltpu.make_async_remote_copy(src_ref=mine, dst_ref=mine, **sems))
            got = land_refs[mi].at[2 * cx + cy, rows, cols]
            back.append(pltpu.make_async_remote_copy(src_ref=got, dst_ref=got, **sems))
    return out, back


def _gather_start(landing):
    n = N_MATS

    def body(*refs):
        out, _ = _gather_copies(refs[:n], refs[n], refs[n + 1])
        for cp in out:
            cp.start()
        refs[-1][...] = jnp.zeros_like(refs[-1])

    hbm = [pltpu.HBM(a.shape, a.dtype) for a in landing]
    outs = pl.pallas_call(
        body, name="gather_start",
        out_shape=(pltpu.SemaphoreType.DMA((2 * n,)), pltpu.SemaphoreType.DMA((2 * n,)), *hbm,
                   jax.ShapeDtypeStruct((8, 128), F32)),
        in_specs=[HBM] * n, out_specs=(SEM, SEM, *[HBM] * n, pl.BlockSpec(memory_space=pltpu.VMEM)),
        input_output_aliases={i: 2 + i for i in range(n)},
        compiler_params=pltpu.CompilerParams(has_side_effects=DATAFLOW),
    )(*[pltpu.with_memory_space_constraint(a, pltpu.HBM) for a in landing])
    return outs[:-1], outs[-1]


def _gather_wait(handle, after):
    n = N_MATS

    def body(*refs):
        out, back = _gather_copies(refs[:n], refs[n], refs[n + 1])
        for cp, arrival in zip(out, back):
            cp.wait_send()
            arrival.wait_recv()

    bufs = handle[2:]
    after, after_specs = _after(after)
    res = pl.pallas_call(
        body, name="gather_wait", out_shape=tuple(pltpu.HBM(b.shape, b.dtype) for b in bufs),
        in_specs=[HBM] * n + [SEM, SEM] + after_specs, out_specs=tuple([HBM] * n),
        input_output_aliases={i: i for i in range(n)},
        compiler_params=pltpu.CompilerParams(has_side_effects=DATAFLOW),
    )(*bufs, handle[0], handle[1], *after)
    return list(res)


def _relay_share(gathered):
    n = N_MATS

    def body(*refs):
        out_refs = refs[n:2 * n]
        send_sems, recv_sems = refs[2 * n:]
        x, y, c = _my_place()
        sibling = (x, y, 1 - c)
        relayed = 2 * (x ^ (1 - c)) + (y ^ c)
        relay_to = (x ^ c, y ^ (1 - c), c)
        far = 2 * (1 - x) + (1 - y)
        near = [2 * (1 - x) + y, 2 * x + (1 - y)]

        def copy(k, mi, shard, hc, to):
            blk = out_refs[mi].at[(shard,) + _half(mi, hc)]
            return pltpu.make_async_remote_copy(src_ref=blk, dst_ref=blk, send_sem=send_sems.at[mi * 4 + k],
                                                recv_sem=recv_sems.at[mi * 4 + k], device_id=to, device_id_type=MESH)

        sends = []
        for mi in range(n):
            sends.append(copy(0, mi, relayed, c, relay_to))
            sends += [copy(1 + j, mi, near[j], c, sibling) for j in range(2)]
        for cp in sends:
            cp.start()
        for mi in range(n):
            copy(0, mi, far, c, relay_to).wait_recv()
            cp = copy(3, mi, far, c, sibling)
            cp.start()
            sends.append(cp)
        for mi in range(n):
            for j in range(2):
                copy(1 + j, mi, near[j], 1 - c, sibling).wait_recv()
            copy(3, mi, far, 1 - c, sibling).wait_recv()
        for cp in sends:
            cp.wait_send()

    return pl.pallas_call(
        body, name="relay_share",
        in_specs=[ANY] * n, out_specs=[ANY] * n,
        out_shape=[jax.ShapeDtypeStruct(g.shape, g.dtype) for g in gathered],
        input_output_aliases={i: i for i in range(n)},
        scratch_shapes=[pltpu.SemaphoreType.DMA((4 * n,)), pltpu.SemaphoreType.DMA((4 * n,))],
    )(*gathered)


def _peers(x, y, c):
    out = []
    for k in range(1, 8):
        px, py, pc = x ^ (k >> 2), y ^ ((k >> 1) & 1), c ^ (k & 1)
        out.append((k - 1, (px, py, pc), 4 * px + 2 * py + pc))
    return out


def _exchange_start(parts, name):
    n = len(parts)

    def body(*refs):
        p_refs, land_refs = refs[:n], refs[n:2 * n]
        send_sems, recv_sems, token = refs[2 * n], refs[2 * n + 1], refs[-1]
        x, y, c = _my_place()
        me = 4 * x + 2 * y + c
        for k, dev, peer in _peers(x, y, c):
            for mi in range(n):
                pltpu.make_async_remote_copy(
                    src_ref=p_refs[mi].at[peer], dst_ref=land_refs[mi].at[me], send_sem=send_sems.at[k * n + mi],
                    recv_sem=recv_sems.at[k * n + mi], device_id=dev, device_id_type=MESH).start()
        token[...] = jnp.zeros_like(token)

    hbm = [pltpu.HBM(p.shape, p.dtype) for p in parts]
    outs = pl.pallas_call(
        body, name=name + "_start",
        out_shape=(pltpu.SemaphoreType.DMA((7 * n,)), pltpu.SemaphoreType.DMA((7 * n,)), *hbm, *hbm,
                   jax.ShapeDtypeStruct((8, 128), F32)),
        in_specs=[HBM] * (2 * n), out_specs=(SEM, SEM, *[HBM] * (2 * n), pl.BlockSpec(memory_space=pltpu.VMEM)),
        input_output_aliases={i: 2 + i for i in range(2 * n)},
        compiler_params=pltpu.CompilerParams(has_side_effects=DATAFLOW),
    )(*[pltpu.with_memory_space_constraint(p, pltpu.HBM) for p in parts],
      *[pltpu.with_memory_space_constraint(lax.empty(p.shape, p.dtype), pltpu.HBM) for p in parts])
    return (name, outs[:-1]), outs[-1]


def _exchange_wait(handle, after):
    name, outs = handle
    n = (len(outs) - 2) // 2

    def body(*refs):
        p_refs, land_refs = refs[:n], refs[n:2 * n]
        send_sems, recv_sems = refs[2 * n], refs[2 * n + 1]
        x, y, c = _my_place()
        me = 4 * x + 2 * y + c
        for k, dev, peer in _peers(x, y, c):
            for mi in range(n):
                pltpu.make_async_remote_copy(
                    src_ref=p_refs[mi].at[peer], dst_ref=land_refs[mi].at[me], send_sem=send_sems.at[k * n + mi],
                    recv_sem=recv_sems.at[k * n + mi], device_id=dev, device_id_type=MESH).wait_send()
                slot = land_refs[mi].at[peer]
                pltpu.make_async_remote_copy(
                    src_ref=slot, dst_ref=slot, send_sem=send_sems.at[k * n + mi],
                    recv_sem=recv_sems.at[k * n + mi], device_id=dev, device_id_type=MESH).wait_recv()

    bufs = outs[2:]
    res = pl.pallas_call(
        body, name=name + "_wait", out_shape=tuple(pltpu.HBM(b.shape, b.dtype) for b in bufs),
        in_specs=[HBM] * (2 * n) + [SEM, SEM, ANY], out_specs=tuple([HBM] * (2 * n)),
        input_output_aliases={i: i for i in range(2 * n)},
        compiler_params=pltpu.CompilerParams(has_side_effects=DATAFLOW),
    )(*bufs, outs[0], outs[1], after)
    return list(res[n:])


def _swap_halves(half_in, gvec):
    def body(g_ref, gv_ref, out_ref, rg_ref, send_sems, recv_sems):
        x, y, c = _my_place()
        me = 4 * x + 2 * y + c
        sibling = (x, y, 1 - c)

        def half(hc):
            return out_ref.at[:, pl.ds(pl.multiple_of(hc * 512, 128), 512)]

        sends = [pltpu.make_async_remote_copy(src_ref=g_ref, dst_ref=half(c), send_sem=send_sems.at[7],
                                              recv_sem=recv_sems.at[7], device_id=sibling, device_id_type=MESH)]
        for k, dev, peer in _peers(x, y, c):
            sends.append(pltpu.make_async_remote_copy(src_ref=gv_ref, dst_ref=rg_ref.at[me], send_sem=send_sems.at[k],
                                                      recv_sem=recv_sems.at[k], device_id=dev, device_id_type=MESH))
        for cp in sends:
            cp.start()
        got = half(1 - c)
        pltpu.make_async_remote_copy(src_ref=got, dst_ref=got, send_sem=send_sems.at[7], recv_sem=recv_sems.at[7],
                                     device_id=sibling, device_id_type=MESH).wait_recv()
        for k, dev, peer in _peers(x, y, c):
            got = rg_ref.at[peer]
            pltpu.make_async_remote_copy(src_ref=got, dst_ref=got, send_sem=send_sems.at[k], recv_sem=recv_sems.at[k],
                                         device_id=dev, device_id_type=MESH).wait_recv()
        for cp in sends:
            cp.wait_send()

    return pl.pallas_call(
        body, name="swap_halves",
        in_specs=[ANY, ANY], out_specs=[ANY, ANY],
        out_shape=[jax.ShapeDtypeStruct(SHARD_SHAPES[0], F32), jax.ShapeDtypeStruct((8, 8, N_GVEC), F32)],
        scratch_shapes=[pltpu.SemaphoreType.DMA((8,)), pltpu.SemaphoreType.DMA((8,))],
    )(half_in, gvec)


def _set_slot(arr, block, idx):
    return lax.dynamic_update_slice(arr, block[None], (idx,) + (0,) * block.ndim)


PAD_RUNS = ((6304, 8352, 0), (5280, 6304, COL_Z), (672, 5280, COL_QKV), (0, 640, COL_LAT), (640, 672, COL_LAT + 704))
N_QKV = COL_LAT - COL_QKV


def _qkv_rows_regroup(a, to_padded):
    if to_padded:
        a4 = a.reshape(3, 12, 128, a.shape[1])
        return jnp.stack([a4[0], a4[1], a4[2]], axis=1).reshape(a.shape)
    a4 = a.reshape(12, 3, 128, a.shape[1])
    return jnp.concatenate([a4[:, tq].reshape(N_QKV // 3, a.shape[1]) for tq in range(3)], axis=0)
W_IN_SHARD = 2088


def _full_weights(gathered):
    def cols(a):
        return jnp.concatenate([a[s] for s in range(4)], axis=1)

    w_uq, w_ukv, w_pm, w_pd = [cols(a) for a in gathered[1:5]]
    w_out = gathered[5].reshape(D_MODEL, D_MODEL)
    w_in_t = gathered[0].reshape(4 * W_IN_SHARD, D_MODEL)
    pieces, at = [], 0
    for lo, hi, pad_lo in sorted(PAD_RUNS, key=lambda t: t[2]):
        if pad_lo > at:
            pieces.append(jnp.zeros((pad_lo - at, D_MODEL), w_in_t.dtype))
        pieces.append(_qkv_rows_regroup(w_in_t[lo:hi], True) if pad_lo == COL_QKV else w_in_t[lo:hi])
        at = pad_lo + hi - lo
    pieces.append(jnp.zeros((N_PAD - at, D_MODEL), w_in_t.dtype))
    w_pad_t = jnp.concatenate(pieces, axis=0)
    z32 = jnp.zeros((Q_RANK, 32), w_uq.dtype)
    wuq_pad = jnp.concatenate([t for h in range(MLA_HEADS) for t in (w_uq[:, h * 96:(h + 1) * 96], z32)], axis=1)
    z64 = jnp.zeros((KV_RANK, 64), w_ukv.dtype)
    wk_pad = jnp.concatenate([t for h in range(MLA_HEADS) for t in (w_ukv[:, h * 128:h * 128 + 64], z64)], axis=1)
    wv = jnp.concatenate([w_ukv[:, h * 128 + 64:(h + 1) * 128] for h in range(MLA_HEADS)], axis=1)
    return w_pad_t, wuq_pad, wk_pad, wv, w_pm, w_pd, w_out


W_IN_LAT = 672


def _grad_parts_in_early(dwt_early):
    dwt_early = jnp.concatenate([dwt_early[:COL_QKV], _qkv_rows_regroup(dwt_early[COL_QKV:COL_LAT], False)], axis=0)

    def in_block(s, h):
        cols = slice(h * 512, (h + 1) * 512)
        out = []
        for lo, hi, pad_lo in sorted(PAD_RUNS):
            a_, b_ = max(lo, s * W_IN_SHARD), min(hi, (s + 1) * W_IN_SHARD)
            if a_ < b_:
                out.append(jnp.zeros((b_ - a_, 512), dwt_early.dtype) if pad_lo >= COL_LAT
                           else dwt_early[pad_lo + a_ - lo:pad_lo + b_ - lo, cols])
        return jnp.concatenate(out, axis=0)

    return jnp.stack([in_block(s, h) for s in range(4) for h in range(2)])


def _grad_parts_in_late(dwt_late):
    rows = jnp.concatenate([dwt_late[0:640], dwt_late[704:736]], axis=0)
    zero = jnp.zeros((W_IN_LAT, 512), dwt_late.dtype)
    return jnp.stack([rows[:, 0:512], rows[:, 512:1024]] + [zero] * 6)


def _shard_blocks(m, axis=1):
    n = m.shape[axis] // 4
    cut = (lambda s: m[:, s * n:(s + 1) * n]) if axis == 1 else (lambda s: m[s * n:(s + 1) * n])
    return jnp.stack([cut(s) for s in range(4) for _ in range(2)])


def _grad_parts_mla(dwuq_pad, dwk_pad, dwv):
    d_uq = jnp.concatenate([dwuq_pad[:, h * 128:h * 128 + 96] for h in range(MLA_HEADS)], axis=1)
    d_ukv = jnp.concatenate([t for h in range(MLA_HEADS) for t in (dwk_pad[:, h * 128:h * 128 + 64], dwv[:, h * 64:(h + 1) * 64])],
                            axis=1)
    return [_shard_blocks(d_uq.astype(BF16)), _shard_blocks(d_ukv.astype(BF16))]


def _rope_tables(positions, token=None):
    pos = positions.reshape(SEQ).astype(F32)
    if token is not None:
        pos = pos + token[0, 0]
    lane = jnp.arange(128)

    def table(rot, first, period):
        inv = ROPE_THETA ** (-jnp.arange(0, rot, 2, dtype=F32) / rot)
        half = rot // 2
        off = lane % period - first
        in1, in2 = (off >= 0) & (off < half), (off >= half) & (off < rot)
        inv_lane = jnp.where(in1 | in2, inv[jnp.clip(off % half, 0, half - 1)], 0.0)
        sign = jnp.where(in1, -1.0, 1.0).astype(F32)
        ang = pos[:, None] * inv_lane[None, :]
        return jnp.cos(ang), jnp.sin(ang) * sign[None, :]

    return table(32, 64, 128), table(16, 0, 64)


class _Links:
    def __init__(self, mats, chip, me):
        landing = [_set_slot(lax.empty((4,) + m.shape, m.dtype), m, chip) for m in mats]
        self.gather, self.token = _gather_start(landing)
        self.me, self.sent, self.handles, self.sums = me, {}, {}, {}

    def weights(self, after):
        return _relay_share(_gather_wait(self.gather, after))

    def send(self, blocks, name):
        self.sent[name] = blocks
        self.handles[name], token = _exchange_start(blocks, name)
        return token

    def collect(self, name, after, parts):
        recv = _exchange_wait(self.handles[name], after)
        for r, own, part in zip(recv, self.sent[name], parts):
            self.sums[part] = _sum_parts(r, own, self.me, 64, "sum_grad_" + part)
        return tuple(self.sums[part] for part in parts)


def _device_grads(x, positions, target, gains, links):
    pre_g, q_g, kv_g, post_g = gains
    (mc, ms), (dc, ds) = _rope_tables(positions, links.token)
    h = _prenorm_fwd(x, pre_g, links.token)
    w_pad_t, wuq_pad, wk_pad, wv, w_pm, w_pd, w_out = _full_weights(links.weights((h, mc, ms, dc, ds)))

    p_gz = _matmul(h, w_pad_t, "nt", F32, 1024, 1536, 1024, "in_proj_gates", b_cols=(0, COL_QKV // 1536))
    p_qkv = _matmul(h, w_pad_t, "nt", F32, 1024, 1536, 1024, "in_proj_dilated", b_cols=(COL_QKV // 1536, N_QKV // 1536),
                    lane_blocks=True)
    p_lat = _matmul(h, w_pad_t, "nt", F32, 1024, N_LAT, 1024, "in_proj_latent", b_cols=(COL_LAT // N_LAT, 1))
    q, k, v = _mla_prep_fwd(p_lat, q_g, kv_g, wuq_pad, wk_pad, wv, mc, ms)
    ya, lse_m = _mla_flash_fwd(q, k, v)
    qkv = [_dil_prep_fwd(p_qkv, dc, ds, g) for g in range(3)]
    o_g, l_g = zip(*[_dil_attn_fwd(qkv[g], g) for g in range(3)])
    (dp, dy, dya, dyd, yd, lse_d, loss_cols, dg_post, dwpm, dwpd, dwout) = _tail(
        p_gz, ya, o_g, l_g, x, target, w_pm, w_pd, w_out, post_g)

    for g in range(3):
        dp = _dil_attn_bwd(dp, qkv[g], dyd, yd, lse_d, dc, ds, g)
    dw_early = _matmul(dp, h, "tn", BF16, 1536, 1024, 2048, "dw_in_early", a_cols=(0, COL_LAT // 1536))
    token = links.send([_grad_parts_in_early(dw_early), _shard_blocks(dwpm.astype(BF16)), _shard_blocks(dwpd.astype(BF16)),
                        _shard_blocks(dwout.astype(BF16), axis=0)], "exchange_early")

    dq, dk, dv = _mla_flash_bwd(q, k, v, ya, dya, lse_m, token)
    dp, dwuq_pad, dwk_pad, dwv, dg_q, dg_kv = _mla_prep_bwd(dp, p_lat, dq, dk, dv, q_g, kv_g, wuq_pad, wk_pad, wv, mc, ms)
    dw_late = _matmul(dp, h, "tn", BF16, N_LAT, 1024, 2048, "dw_in_late", a_cols=(COL_LAT // N_LAT, 1))
    token = links.send([_grad_parts_in_late(dw_late)] + _grad_parts_mla(dwuq_pad, dwk_pad, dwv), "exchange_late")
    early = links.collect("exchange_early", dw_late, ("in_early", "pm", "pd", "out"))

    grad_x, dg_pre = _dh_prenorm_bwd(dp, w_pad_t, x, dy, pre_g, (token,) + tuple(early))
    links.collect("exchange_late", grad_x, ("in_late", "uq", "ukv"))

    loss_part = jnp.pad((jnp.sum(loss_cols) * (0.5 / D_MODEL)).reshape(1, 1), ((0, 0), (0, N_GVEC - N_GAINS - 1)))
    gvec = jnp.concatenate([dg_pre, dg_q, dg_kv, dg_post, loss_part], axis=1)
    return grad_x, gvec


def kernel(x, positions, pre_norm_g, w_in, q_norm_g, w_uq, kv_norm_g, w_ukv, w_proj_mla, w_proj_dil, w_out, post_norm_g, loss_target, m_pre_norm_g, m_w_in, m_q_norm_g, m_w_uq, m_kv_norm_g, m_w_ukv, m_w_proj_mla, m_w_proj_dil, m_w_out, m_post_norm_g, v_pre_norm_g, v_w_in, v_q_norm_g, v_w_uq, v_kv_norm_g, v_w_ukv, v_w_proj_mla, v_w_proj_dil, v_w_out, v_post_norm_g):
    xi, yi, ci = _my_place()
    chip, me = 2 * xi + yi, 4 * xi + 2 * yi + ci
    mats = [jnp.swapaxes(w_in, 1, 2)] + [w_uq, w_ukv, w_proj_mla, w_proj_dil, w_out]
    mats = [w.reshape(w.shape[1:]).astype(BF16) for w in mats]
    links = _Links(mats, chip, me)
    gains = (pre_norm_g, q_norm_g, kv_norm_g, post_norm_g)
    grad_x, gvec = _device_grads(x[0], positions, loss_target[0], gains, links)

    sums = links.sums
    in_e = sums["in_early"]
    half_in = jnp.concatenate([in_e[:W_IN_LAT] + jnp.where(chip == 0, sums["in_late"], 0.0), in_e[W_IN_LAT:]], axis=0)
    gvec8 = jnp.pad(gvec, ((0, 7), (0, 0)))
    swapped_in, recv_gains = _swap_halves(half_in, gvec8)
    g_gains = _sum_parts(recv_gains, gvec8, me, 8, "sum_gain_parts")[0:1]
    loss = g_gains[0, N_GAINS]
    sw = lambda a: jnp.swapaxes(a, 1, 2)
    d_in, m_in, v_in, g_in = [sw(o) for o in _adamw_in(sw(w_in), sw(m_w_in), sw(v_w_in), half_in, swapped_in, ci)]
    g_mats = [g_in] + [sums[part][None] for part in ("uq", "ukv", "pm", "pd", "out")]

    off = [0, 1024, 1408, 1664, 2688]
    g_gain = [g_gains[:, off[i]:off[i + 1]] for i in range(4)]
    grads = [g_gain[0], g_mats[0], g_gain[1], g_mats[1], g_gain[2], g_mats[2], g_mats[3], g_mats[4], g_mats[5], g_gain[3]]
    ws = [pre_norm_g, w_in, q_norm_g, w_uq, kv_norm_g, w_ukv, w_proj_mla, w_proj_dil, w_out, post_norm_g]
    ms = [m_pre_norm_g, m_w_in, m_q_norm_g, m_w_uq, m_kv_norm_g, m_w_ukv, m_w_proj_mla, m_w_proj_dil, m_w_out, m_post_norm_g]
    vs = [v_pre_norm_g, v_w_in, v_q_norm_g, v_w_uq, v_kv_norm_g, v_w_ukv, v_w_proj_mla, v_w_proj_dil, v_w_out, v_post_norm_g]
    deltas, new_m, new_v = [], [], []
    for i, (w, g, m, v) in enumerate(zip(ws, grads, ms, vs)):
        if w is w_in:
            d_, m_, v_ = d_in, m_in, v_in
        elif w.shape[-1] % 128 and w.shape[-2] % 128 == 0:
            g = jnp.swapaxes(g, 1, 2)
            grads[i] = jnp.swapaxes(g, 1, 2)
            d_, m_, v_ = [jnp.swapaxes(o, 1, 2) for o in
                          _adamw(jnp.swapaxes(w, 1, 2), g, jnp.swapaxes(m, 1, 2), jnp.swapaxes(v, 1, 2), f"adamw_{i}")]
        else:
            d_, m_, v_ = _adamw(w, g, m, v, f"adamw_{i}")
        deltas.append(d_)
        new_m.append(m_)
        new_v.append(v_)
    return (loss, grad_x.reshape(x.shape), *grads, *deltas, *new_m, *new_v)
```

```python
import jax
import jax.numpy as jnp
from jax import lax
from jax.experimental import pallas as pl
from jax.experimental.pallas import tpu as pltpu

F32 = jnp.float32
BF16 = jnp.bfloat16

SEQ = 4096
D_MODEL = 1024
EPS = 1e-6
ROPE_THETA = 500000.0
MLA_HEADS = 8
Q_RANK = 384
KV_RANK = 256
MLA_SCALE = 96.0 ** -0.5
MLA_ROPE_HALF = 16
DIL_DILATIONS = (1, 4, 16)
DIL_ROPE_HALF = 8
DIL_SCALE = 0.125
BAND = 128

N_LAT = 768
COL_Z, COL_QKV, COL_LAT = 2048, 3072, 7680
N_PAD = 8448


def _qkv_block(tq, g, pr):
    return COL_QKV // 128 + (g * 4 + pr) * 3 + tq

IN_SPLITS = (384, 256, 32, 4608, 512, 512, 1024, 1024)

SHARD_SHAPES = ((2088, 1024), (384, 192), (256, 256), (512, 256), (512, 256), (256, 1024))
N_MATS = len(SHARD_SHAPES)
N_GAINS = 2688
N_GVEC = N_GAINS + 128

ADAM_LR, ADAM_B1, ADAM_B2, ADAM_EPS, ADAM_WD, ADAM_STEP = 0.001, 0.9, 0.999, 1e-08, 0.01, 10

VMEM_LIMIT = 56 * 1024 * 1024
NEG = -1e30
MESH = pl.DeviceIdType.MESH


def _cparams(**kw):
    return pltpu.CompilerParams(vmem_limit_bytes=VMEM_LIMIT, **kw)


def _dot(a, b, dims):
    return lax.dot_general(a, b, (dims, ((), ())), preferred_element_type=F32)


def _nn(a, b):
    return _dot(a, b, ((1,), (0,)))


def _nt(a, b):
    return _dot(a, b, ((1,), (1,)))


def _tn(a, b):
    return _dot(a, b, ((0,), (0,)))


def _rope_lanes(shape, half, period, first):
    lane = lax.broadcasted_iota(jnp.int32, shape, len(shape) - 1) % period
    return (lane >= first) & (lane < first + half), (lane >= first + half) & (lane < first + 2 * half)


def _rope_fwd(x, c, s, half, lanes):
    x1, _ = lanes
    return x * c + jnp.where(x1, pltpu.roll(x, 128 - half, 1), pltpu.roll(x, half, 1)) * s


def _rope_bwd(g, c, s, half, lanes):
    x1, x2 = lanes
    gs = g * s
    return g * c + jnp.where(x2, pltpu.roll(gs, half, 1), jnp.where(x1, pltpu.roll(gs, 128 - half, 1), 0.0))


def _sigmoid(x):
    return 1.0 / (1.0 + jnp.exp(-x))


def _after(token):
    tokens = [t for t in (token if isinstance(token, (tuple, list)) else [token]) if t is not None]
    return tokens, [pl.BlockSpec(memory_space=pl.ANY)] * len(tokens)


def _matmul(a, b, mode, out_dtype, tm, tn, tk, name, token=None, b_cols=None, a_cols=None, lane_blocks=False):
    after, after_specs = _after(token)
    if mode == "nn":
        (m, k), n = a.shape, b.shape[1]
        first = 0
        if b_cols is not None:
            first, n = b_cols[0], b_cols[1] * tn
        a_spec = pl.BlockSpec((tm, tk), lambda j, i, kk: (i, kk))
        b_spec = pl.BlockSpec((tk, tn), lambda j, i, kk: (kk, j + first))
        dot = _nn
    elif mode == "nt":
        (m, k), n = a.shape, b.shape[0]
        first = 0
        if b_cols is not None:
            first, n = b_cols[0], b_cols[1] * tn
        a_spec = pl.BlockSpec((tm, tk), lambda j, i, kk: (i, kk))
        b_spec = pl.BlockSpec((tn, tk), lambda j, i, kk: (j + first, kk))
        dot = _nt
    else:
        (k, m), n = a.shape, b.shape[1]
        first = 0
        if a_cols is not None:
            first, m = a_cols[0], a_cols[1] * tm
        a_spec = pl.BlockSpec((tk, tm), lambda j, i, kk: (kk, i + first))
        b_spec = pl.BlockSpec((tk, tn), lambda j, i, kk: (kk, j))
        dot = _tn
    assert m % tm == 0 and n % tn == 0 and k % tk == 0, (name, m, n, k, tm, tn, tk)
    nk = k // tk

    def body(a_ref, b_ref, *rest):
        o_ref, acc_ref = rest[-2:]
        kk = pl.program_id(2)
        part = dot(a_ref[...], b_ref[...])

        @pl.when(kk == 0)
        def _():
            acc_ref[...] = part

        @pl.when(kk > 0)
        def _():
            acc_ref[...] += part

        @pl.when(kk == nk - 1)
        def _():
            if lane_blocks:
                for blk in range(tn // 128):
                    o_ref[blk] = acc_ref[:, blk * 128:(blk + 1) * 128].astype(o_ref.dtype)
            else:
                o_ref[...] = acc_ref[...].astype(o_ref.dtype)

    if lane_blocks:
        out_spec = pl.BlockSpec((tn // 128, tm, 128), lambda j, i, kk: (j, i, 0))
        out_shape = jax.ShapeDtypeStruct((n // 128, m, 128), out_dtype)
    else:
        out_spec = pl.BlockSpec((tm, tn), lambda j, i, kk: (i, j))
        out_shape = jax.ShapeDtypeStruct((m, n), out_dtype)
    return pl.pallas_call(
        body, name=name, grid=(n // tn, m // tm, nk),
        in_specs=[a_spec, b_spec] + after_specs,
        out_specs=out_spec, out_shape=out_shape,
        scratch_shapes=[pltpu.VMEM((tm, tn), F32)],
        compiler_params=_cparams(),
    )(a, b, *after)


def _prenorm_fwd(x, g, token=None):
    tm = 512
    after, after_specs = _after(token)

    def body(x_ref, g_ref, *rest):
        xv = x_ref[...]
        r = lax.rsqrt(jnp.mean(xv * xv, axis=-1, keepdims=True) + EPS)
        rest[-1][...] = (xv * r * g_ref[...]).astype(BF16)

    return pl.pallas_call(
        body, name="prenorm_fwd", grid=(SEQ // tm,),
        in_specs=[pl.BlockSpec((tm, D_MODEL), lambda i: (i, 0)), pl.BlockSpec((1, D_MODEL), lambda i: (0, 0))] + after_specs,
        out_specs=pl.BlockSpec((tm, D_MODEL), lambda i: (i, 0)),
        out_shape=jax.ShapeDtypeStruct((SEQ, D_MODEL), BF16),
    )(x, g, *after)


def _dh_prenorm_bwd(dp, w_pad_t, x, dy, g, token=None):
    tm, tk = 1024, 1408
    nk = N_PAD // tk
    after, after_specs = _after(token)

    def body(a_ref, b_ref, x_ref, dy_ref, g_ref, *rest):
        gx_ref, dg_ref, acc_ref = rest[-3:]
        i, kk = pl.program_id(0), pl.program_id(1)
        part = _nn(a_ref[...], b_ref[...])

        @pl.when(kk == 0)
        def _():
            acc_ref[...] = part

        @pl.when(kk > 0)
        def _():
            acc_ref[...] += part

        @pl.when(kk == nk - 1)
        def _():
            xv = x_ref[...]
            r = lax.rsqrt(jnp.mean(xv * xv, axis=-1, keepdims=True) + EPS)
            n = xv * r
            dhv = acc_ref[...]
            dn = dhv * g_ref[...]
            gx_ref[...] = dy_ref[...] + r * (dn - n * jnp.mean(dn * n, axis=-1, keepdims=True))
            cols = jnp.sum(dhv * n, axis=0, keepdims=True)

            @pl.when(i == 0)
            def _():
                dg_ref[...] = cols

            @pl.when(i > 0)
            def _():
                dg_ref[...] += cols

    row = pl.BlockSpec((tm, D_MODEL), lambda i, kk: (i, 0))
    vec = pl.BlockSpec((1, D_MODEL), lambda i, kk: (0, 0))
    return pl.pallas_call(
        body, name="dh_prenorm_bwd", grid=(SEQ // tm, nk),
        in_specs=[pl.BlockSpec((tm, tk), lambda i, kk: (i, kk)), pl.BlockSpec((tk, D_MODEL), lambda i, kk: (kk, 0)),
                  row, row, vec] + after_specs,
        out_specs=[row, vec],
        out_shape=[jax.ShapeDtypeStruct((SEQ, D_MODEL), F32), jax.ShapeDtypeStruct((1, D_MODEL), F32)],
        scratch_shapes=[pltpu.VMEM((tm, D_MODEL), F32)],
        compiler_params=_cparams(),
    )(dp, w_pad_t, x, dy, g, *after)


def _mla_prep_fwd(p, qg, kvg, wuq, wk, wv, rc, rs):
    tm = 512

    def body(lat_ref, qg_ref, kvg_ref, wuq_ref, wk_ref, wv_ref, c_ref, s_ref, q_ref, k_ref, v_ref):
        c, s = c_ref[...], s_ref[...]
        lanes = _rope_lanes((tm, 128), MLA_ROPE_HALF, 128, 64)
        cq = lat_ref[:, 0:Q_RANK]
        r1 = lax.rsqrt(jnp.mean(cq * cq, axis=-1, keepdims=True) + EPS)
        cqn = (cq * r1 * qg_ref[...]).astype(BF16)
        q = _nn(cqn, wuq_ref[...])
        for h in range(MLA_HEADS):
            sl = slice(h * 128, (h + 1) * 128)
            q_ref[:, sl] = (_rope_fwd(q[:, sl], c, s, MLA_ROPE_HALF, lanes) * MLA_SCALE).astype(BF16)
        ckv = lat_ref[:, Q_RANK:Q_RANK + KV_RANK]
        r2 = lax.rsqrt(jnp.mean(ckv * ckv, axis=-1, keepdims=True) + EPS)
        ckvn = (ckv * r2 * kvg_ref[...]).astype(BF16)
        krr = _rope_fwd(lat_ref[:, Q_RANK + KV_RANK:N_LAT], c, s, MLA_ROPE_HALF, lanes)
        kn = _nn(ckvn, wk_ref[...])
        for h in range(MLA_HEADS):
            sl = slice(h * 128, (h + 1) * 128)
            k_ref[:, sl] = (kn[:, sl] + krr).astype(BF16)
        v_ref[...] = _nn(ckvn, wv_ref[...]).astype(BF16)

    def full(shape):
        return pl.BlockSpec(shape, lambda i: (0, 0))

    def rows(w):
        return pl.BlockSpec((tm, w), lambda i: (i, 0))

    return pl.pallas_call(
        body, name="mla_prep_fwd", grid=(SEQ // tm,),
        in_specs=[pl.BlockSpec((tm, N_LAT), lambda i: (i, 0)),
                  full((1, Q_RANK)), full((1, KV_RANK)), full((Q_RANK, 1024)), full((KV_RANK, 1024)),
                  full((KV_RANK, 512)), rows(128), rows(128)],
        out_specs=[rows(1024), rows(1024), rows(512)],
        out_shape=[jax.ShapeDtypeStruct((SEQ, 1024), BF16), jax.ShapeDtypeStruct((SEQ, 1024), BF16),
                   jax.ShapeDtypeStruct((SEQ, 512), BF16)],
        compiler_params=_cparams(),
    )(p, qg, kvg, wuq, wk, wv, rc, rs)


def _mla_prep_bwd(dp_in, p, dq, dk, dv, qg, kvg, wuq, wk, wv, rc, rs):
    tm = 512

    def body(dp_any, lat_ref, dq_ref, dk_ref, dv_ref, qg_ref, kvg_ref, wuq_ref, wk_ref, wv_ref,
             c_ref, s_ref, dp_ref, dwuq_ref, dwk_ref, dwv_ref, dgq_ref, dgkv_ref, dqb_ref, dkb_ref):
        del dp_any
        c, s = c_ref[...], s_ref[...]
        lanes = _rope_lanes((tm, 128), MLA_ROPE_HALF, 128, 64)
        lane = lax.broadcasted_iota(jnp.int32, (tm, 128), 1)
        dkr = jnp.zeros((tm, 128), F32)
        for h in range(MLA_HEADS):
            sl = slice(h * 128, (h + 1) * 128)
            dqb_ref[:, sl] = _rope_bwd(dq_ref[:, sl] * MLA_SCALE, c, s, MLA_ROPE_HALF, lanes).astype(BF16)
            dkh = dk_ref[:, sl]
            dkr = dkr + dkh
            dkb_ref[:, sl] = jnp.where(lane < 64, dkh, 0.0).astype(BF16)
        dkr = jnp.where((lane >= 64) & (lane < 96), dkr, 0.0)
        dkr = _rope_bwd(dkr, c, s, MLA_ROPE_HALF, lanes)
        dvb = dv_ref[...].astype(BF16)

        cq = lat_ref[:, 0:Q_RANK]
        r1 = lax.rsqrt(jnp.mean(cq * cq, axis=-1, keepdims=True) + EPS)
        n1 = cq * r1
        dcqn = _nt(dqb_ref[...], wuq_ref[...])
        dn1 = dcqn * qg_ref[...]
        dcq = r1 * (dn1 - n1 * jnp.mean(dn1 * n1, axis=-1, keepdims=True))
        pq = jnp.sum(dcqn * n1, axis=0, keepdims=True)

        ckv = lat_ref[:, Q_RANK:Q_RANK + KV_RANK]
        r2 = lax.rsqrt(jnp.mean(ckv * ckv, axis=-1, keepdims=True) + EPS)
        n2 = ckv * r2
        dckvn = _nt(dkb_ref[...], wk_ref[...]) + _nt(dvb, wv_ref[...])
        dn2 = dckvn * kvg_ref[...]
        dckv = r2 * (dn2 - n2 * jnp.mean(dn2 * n2, axis=-1, keepdims=True))
        pkv = jnp.sum(dckvn * n2, axis=0, keepdims=True)
        cqn = (n1 * qg_ref[...]).astype(BF16)
        ckvn = (n2 * kvg_ref[...]).astype(BF16)
        wq, wk_, wv_ = _tn(cqn, dqb_ref[...]), _tn(ckvn, dkb_ref[...]), _tn(ckvn, dvb)

        dp_ref[:, 0:Q_RANK] = dcq.astype(BF16)
        dp_ref[:, Q_RANK:Q_RANK + KV_RANK] = dckv.astype(BF16)
        dp_ref[:, Q_RANK + KV_RANK:N_LAT] = dkr.astype(BF16)

        @pl.when(pl.program_id(0) == 0)
        def _():
            dgq_ref[...] = pq
            dgkv_ref[...] = pkv
            dwuq_ref[...] = wq
            dwk_ref[...] = wk_
            dwv_ref[...] = wv_

        @pl.when(pl.program_id(0) > 0)
        def _():
            dgq_ref[...] += pq
            dgkv_ref[...] += pkv
            dwuq_ref[...] += wq
            dwk_ref[...] += wk_
            dwv_ref[...] += wv_

    def full(shape):
        return pl.BlockSpec(shape, lambda i: (0, 0))

    def rows(w):
        return pl.BlockSpec((tm, w), lambda i: (i, 0))

    lat = pl.BlockSpec((tm, N_LAT), lambda i: (i, 0))
    dlat = pl.BlockSpec((tm, N_LAT), lambda i: (i, COL_LAT // N_LAT))
    return pl.pallas_call(
        body, name="mla_prep_bwd", grid=(SEQ // tm,),
        in_specs=[pl.BlockSpec(memory_space=pl.ANY), lat, rows(1024), rows(1024), rows(512),
                  full((1, Q_RANK)), full((1, KV_RANK)), full((Q_RANK, 1024)), full((KV_RANK, 1024)),
                  full((KV_RANK, 512)), rows(128), rows(128)],
        out_specs=[dlat, full((Q_RANK, 1024)), full((KV_RANK, 1024)), full((KV_RANK, 512)),
                   full((1, Q_RANK)), full((1, KV_RANK))],
        out_shape=[jax.ShapeDtypeStruct((SEQ, N_PAD), BF16), jax.ShapeDtypeStruct((Q_RANK, 1024), F32),
                   jax.ShapeDtypeStruct((KV_RANK, 1024), F32), jax.ShapeDtypeStruct((KV_RANK, 512), F32),
                   jax.ShapeDtypeStruct((1, Q_RANK), F32), jax.ShapeDtypeStruct((1, KV_RANK), F32)],
        input_output_aliases={0: 0},
        scratch_shapes=[pltpu.VMEM((tm, 1024), BF16), pltpu.VMEM((tm, 1024), BF16)],
        compiler_params=_cparams(),
    )(dp_in, p, dq, dk, dv, qg, kvg, wuq, wk, wv, rc, rs)


FLASH_T = 1024


def _head_half(shape, hh):
    lane = lax.broadcasted_iota(jnp.int32, shape, 1)
    return (lane < 64) if hh == 0 else (lane >= 64)


def _diag_keep(nr, nk):
    row = lax.broadcasted_iota(jnp.int32, (nr, nk), 0)
    col = lax.broadcasted_iota(jnp.int32, (nr, nk), 1)
    return row + (nk - nr) >= col


def _tri_steps(nb, q_major):
    if q_major:
        pairs = [(i, kb) for i in range(nb) for kb in range(i + 1)]
    else:
        pairs = [(i, kb) for kb in range(nb) for i in range(kb, nb)]
    return jnp.asarray([p[0] for p in pairs], jnp.int32), jnp.asarray([p[1] for p in pairs], jnp.int32)


def _mla_flash_fwd(q, k, v):
    t = FLASH_T
    nb = SEQ // t
    qtab, ktab = _tri_steps(nb, True)

    def body(qi_ref, ki_ref, q_ref, k_ref, v_ref, o_ref, lse_ref, m_scr, l_scr, acc_scr):
        step = pl.program_id(1)
        i, kb = qi_ref[step], ki_ref[step]

        @pl.when(kb == 0)
        def _():
            m_scr[...] = jnp.full_like(m_scr, NEG)
            l_scr[...] = jnp.zeros_like(l_scr)
            acc_scr[...] = jnp.zeros_like(acc_scr)

        def update(r0, nr, nk, diagonal):
            rs = slice(r0, r0 + nr)
            vv = v_ref[0:nk, :]
            for hh in range(2):
                sl = slice(hh * 128, (hh + 1) * 128)
                s = _nt(q_ref[rs, sl], k_ref[0:nk, sl])
                if diagonal:
                    s = jnp.where(_diag_keep(nr, nk), s, NEG)
                m_prev = m_scr[hh, rs, :]
                m_new = jnp.maximum(m_prev, jnp.max(s, axis=-1, keepdims=True))
                pr = jnp.exp(s - jnp.tile(m_new, (1, nk // 128)))
                alpha = jnp.exp(m_prev - m_new)
                l_scr[hh, rs, :] = alpha * l_scr[hh, rs, :] + jnp.sum(pr, axis=-1, keepdims=True)
                acc_scr[hh, rs, :] = alpha * acc_scr[hh, rs, :] + _nn(pr.astype(BF16), vv)
                m_scr[hh, rs, :] = m_new

        @pl.when(kb < i)
        def _():
            update(0, t, t, False)

        @pl.when(kb == i)
        def _():
            update(0, t // 2, t // 2, True)
            update(t // 2, t // 2, t, True)
            o0 = acc_scr[0] / l_scr[0]
            o1 = acc_scr[1] / l_scr[1]
            o_ref[...] = jnp.where(_head_half((t, 128), 0), o0, o1)
            for hh in range(2):
                lse_ref[:, hh * 128:(hh + 1) * 128] = m_scr[hh] + jnp.log(l_scr[hh])

    grid_spec = pltpu.PrefetchScalarGridSpec(
        num_scalar_prefetch=2, grid=(4, qtab.shape[0]),
        in_specs=[pl.BlockSpec((t, 256), lambda j, s, qi, ki: (qi[s], j)),
                  pl.BlockSpec((t, 256), lambda j, s, qi, ki: (ki[s], j)),
                  pl.BlockSpec((t, 128), lambda j, s, qi, ki: (ki[s], j))],
        out_specs=[pl.BlockSpec((t, 128), lambda j, s, qi, ki: (qi[s], j)),
                   pl.BlockSpec((t, 256), lambda j, s, qi, ki: (qi[s], j))],
        scratch_shapes=[pltpu.VMEM((2, t, 128), F32), pltpu.VMEM((2, t, 128), F32), pltpu.VMEM((2, t, 128), F32)])
    return pl.pallas_call(
        body, name="mla_flash_fwd", grid_spec=grid_spec,
        out_shape=[jax.ShapeDtypeStruct((SEQ, 512), F32), jax.ShapeDtypeStruct((SEQ, 1024), F32)],
        compiler_params=_cparams(),
    )(qtab, ktab, q, k, v)


def _mla_flash_bwd(q, k, v, o, do, lse, token=None):
    t = FLASH_T
    nb = SEQ // t
    qtab, ktab = _tri_steps(nb, False)
    after, after_specs = _after(token)

    def body(qi_ref, ki_ref, q_ref, k_ref, v_ref, o_ref, do_ref, lse_ref, *rest):
        dq_ref, dk_ref, dv_ref, dk_scr, dv_scr = rest[-5:]
        step = pl.program_id(1)
        i, kb = qi_ref[step], ki_ref[step]

        @pl.when(step == 0)
        def _():
            dq_ref[...] = jnp.zeros_like(dq_ref)

        @pl.when(i == kb)
        def _():
            dk_scr[...] = jnp.zeros_like(dk_scr)
            dv_scr[...] = jnp.zeros_like(dv_scr)

        def update(r0, nr, nk, diagonal):
            rs = slice(r0, r0 + nr)
            vv = v_ref[0:nk, :]
            ov = o_ref[rs, :]
            dov = do_ref[rs, :]
            rows = pl.ds(pl.multiple_of(i * t + r0, t // 2), nr)
            for hh in range(2):
                sl = slice(hh * 128, (hh + 1) * 128)
                qh, kh = q_ref[rs, sl], k_ref[0:nk, sl]
                s = _nt(qh, kh)
                if diagonal:
                    s = jnp.where(_diag_keep(nr, nk), s, NEG)
                pr = jnp.exp(s - jnp.tile(lse_ref[rs, sl], (1, nk // 128)))
                dom = jnp.where(_head_half((nr, 128), hh), dov, 0.0)
                domb = dom.astype(BF16)
                dv_scr[0:nk, :] += _tn(pr.astype(BF16), domb)
                dpr = _nt(domb, vv)
                delta = jnp.sum(dom * ov, axis=-1, keepdims=True)
                ds = (pr * (dpr - delta)).astype(BF16)
                dq_ref[rows, sl] += _nn(ds, kh)
                dk_scr[hh, 0:nk, :] += _tn(ds, qh)

        @pl.when(i > kb)
        def _():
            update(0, t, t, False)

        @pl.when(i == kb)
        def _():
            update(0, t // 2, t // 2, True)
            update(t // 2, t // 2, t, True)

        @pl.when(i == nb - 1)
        def _():
            dk_ref[:, 0:128] = dk_scr[0]
            dk_ref[:, 128:256] = dk_scr[1]
            dv_ref[...] = dv_scr[...]

    qi_map = lambda j, s, qi, ki: (qi[s], j)
    ki_map = lambda j, s, qi, ki: (ki[s], j)
    grid_spec = pltpu.PrefetchScalarGridSpec(
        num_scalar_prefetch=2, grid=(4, qtab.shape[0]),
        in_specs=[pl.BlockSpec((t, 256), qi_map), pl.BlockSpec((t, 256), ki_map), pl.BlockSpec((t, 128), ki_map),
                  pl.BlockSpec((t, 128), qi_map), pl.BlockSpec((t, 128), qi_map), pl.BlockSpec((t, 256), qi_map)]
        + after_specs,
        out_specs=[pl.BlockSpec((SEQ, 256), lambda j, s, qi, ki: (0, j)), pl.BlockSpec((t, 256), ki_map),
                   pl.BlockSpec((t, 128), ki_map)],
        scratch_shapes=[pltpu.VMEM((2, t, 128), F32), pltpu.VMEM((t, 128), F32)])
    return pl.pallas_call(
        body, name="mla_flash_bwd", grid_spec=grid_spec,
        out_shape=[jax.ShapeDtypeStruct((SEQ, 1024), F32), jax.ShapeDtypeStruct((SEQ, 1024), F32),
                   jax.ShapeDtypeStruct((SEQ, 512), F32)],
        compiler_params=_cparams(),
    )(qtab, ktab, q, k, v, o, do, lse, *after)


def _strided(start, size, d):
    return pl.ds(start, size) if d == 1 else pl.ds(start, size, stride=d)


def _dil_prep_fwd(p, rc, rs, g):
    d = DIL_DILATIONS[g]
    sub_len = SEQ // d
    ch = min(sub_len, 512)

    def body(p_ref, c_ref, s_ref, o_ref, x_scr):
        tq = pl.program_id(0)
        lanes = _rope_lanes((ch, 128), DIL_ROPE_HALF, 64, 0)
        o_ref[0, 0, 0:BAND, :] = jnp.zeros((BAND, 128), BF16)

        @pl.when(tq < 2)
        def _():
            mult = jnp.where(tq == 0, DIL_SCALE, 1.0).astype(F32)
            for c0 in range(0, SEQ, ch):
                rows = pl.ds(c0, ch)
                x_scr[rows, :] = _rope_fwd(p_ref[rows, :], c_ref[rows, :] * mult, s_ref[rows, :] * mult, DIL_ROPE_HALF, lanes)

        def gather(take):
            for r in range(d):
                for c0 in range(0, sub_len, ch):
                    at = BAND + r * sub_len + c0
                    o_ref[0, 0, at:at + ch, :] = take(_strided(r + c0 * d, ch, d)).astype(BF16)

        @pl.when(tq < 2)
        def _():
            gather(lambda rows: x_scr[rows, :])

        @pl.when(tq == 2)
        def _():
            gather(lambda rows: p_ref[rows, :])

    tab = pl.BlockSpec((SEQ, 128), lambda tq, pr: (0, 0))
    return pl.pallas_call(
        body, name=f"dil_prep_fwd_g{g}", grid=(3, 4),
        in_specs=[pl.BlockSpec((None, SEQ, 128), lambda tq, pr: (_qkv_block(tq, g, pr) - COL_QKV // 128, 0, 0)), tab, tab],
        out_specs=pl.BlockSpec((1, 1, BAND + SEQ, 128), lambda tq, pr: (tq, pr, 0, 0)),
        out_shape=jax.ShapeDtypeStruct((3, 4, BAND + SEQ, 128), BF16),
        scratch_shapes=[pltpu.VMEM((SEQ, 128), F32)],
        compiler_params=_cparams(),
    )(p, rc, rs)


DIL_ST_FWD, DIL_ST_BWD = 1024, 2048


def _band_keep(g, b, t, nb):
    nbs = SEQ // DIL_DILATIONS[g] // BAND
    row = lax.broadcasted_iota(jnp.int32, (BAND, 2 * BAND), 0)
    col = lax.broadcasted_iota(jnp.int32, (BAND, 2 * BAND), 1)
    cur = (col >= BAND) & (row >= col - BAND)
    prev = (col < BAND) & (col >= row)
    if nbs >= nb:
        if b > 0:
            return cur | prev
        return cur | (prev & ((t * nb) % nbs != 0))
    return cur | prev if b % nbs else cur


def _dil_tok(g, b, t, nb):
    d = DIL_DILATIONS[g]
    nbs = SEQ // d // BAND
    gb = t * nb + b
    return _strided((gb % nbs) * BAND * d + gb // nbs, BAND, d)


def _dil_attn_fwd(qkv, g):
    DIL_ST, DIL_NB = DIL_ST_FWD, DIL_ST_FWD // BAND

    def body(q_ref, k_ref, v_ref, o_ref, l_ref, s_scr, p_scr, o_scr):
        t = pl.program_id(1)
        base = t * DIL_ST
        half0 = _head_half((DIL_ST, 128), 0)
        lse_h = []
        for hh in range(2):
            half = _head_half((BAND, 128), hh)
            for b in range(DIL_NB):
                qv = q_ref[0, 0, pl.ds(pl.multiple_of(base + (b + 1) * BAND, BAND), BAND), :]
                k2 = k_ref[0, 0, pl.ds(pl.multiple_of(base + b * BAND, BAND), 2 * BAND), :]
                sb = _nt(jnp.where(half, qv, jnp.zeros_like(qv)), k2)
                s_scr[b * BAND:(b + 1) * BAND, :] = jnp.where(_band_keep(g, b, t, DIL_NB), sb, NEG)
            s = s_scr[...]
            m = jnp.max(s, axis=-1, keepdims=True)
            pr = jnp.exp(s - m)
            den = jnp.sum(pr, axis=-1, keepdims=True)
            p_scr[...] = pr.astype(BF16)
            for b in range(DIL_NB):
                v2 = v_ref[0, 0, pl.ds(pl.multiple_of(base + b * BAND, BAND), 2 * BAND), :]
                o_scr[hh, b * BAND:(b + 1) * BAND, :] = _nn(p_scr[b * BAND:(b + 1) * BAND, :], v2)
            o_scr[hh] = o_scr[hh] / den
            lse_h.append(m + jnp.log(den))
        out = jnp.where(half0, o_scr[0], o_scr[1])
        lse = jnp.where(half0, lse_h[0], lse_h[1])
        for b in range(DIL_NB):
            tok = _dil_tok(g, b, t, DIL_NB)
            o_ref[tok, :] = out[b * BAND:(b + 1) * BAND, :]
            l_ref[tok, :] = lse[b * BAND:(b + 1) * BAND, :]

    def inp(tq):
        return pl.BlockSpec((1, 1, BAND + SEQ, 128), lambda pr, t: (tq, pr, 0, 0))

    out = pl.BlockSpec((SEQ, 128), lambda pr, t: (0, pr))
    return pl.pallas_call(
        body, name=f"dil_attn_fwd_g{g}", grid=(4, SEQ // DIL_ST),
        in_specs=[inp(0), inp(1), inp(2)], out_specs=[out, out],
        out_shape=[jax.ShapeDtypeStruct((SEQ, 512), F32), jax.ShapeDtypeStruct((SEQ, 512), F32)],
        scratch_shapes=[pltpu.VMEM((DIL_ST, 2 * BAND), F32), pltpu.VMEM((DIL_ST, 2 * BAND), BF16),
                        pltpu.VMEM((2, DIL_ST, 128), F32)],
        compiler_params=_cparams(),
    )(qkv, qkv, qkv)


def _dil_attn_bwd(dp_in, qkv, dyd, yd, lse_all, rc, rs, g, token=None):
    d = DIL_DILATIONS[g]
    sub_len = SEQ // d
    DIL_ST, DIL_NB = DIL_ST_BWD, DIL_ST_BWD // BAND
    nst = SEQ // DIL_ST
    after, after_specs = _after(token)
    ch = 512

    def body(dp_any, q_ref, k_ref, v_ref, do_ref, y_ref, l_ref, c_ref, sn_ref, *rest):
        dp_ref, tok_scr, dk_scr, dv_scr, s_scr, dp_scr, p_scr, ds_scr, do_scr, y_scr, l_scr, dq_scr = rest[-12:]
        del dp_any
        t = pl.program_id(1)
        base = t * DIL_ST

        @pl.when(t == 0)
        def _():
            dk_scr[...] = jnp.zeros_like(dk_scr)
            dv_scr[...] = jnp.zeros_like(dv_scr)

        for b in range(DIL_NB):
            tok = _dil_tok(g, b, t, DIL_NB)
            do_scr[b * BAND:(b + 1) * BAND, :] = do_ref[tok, :]
            y_scr[b * BAND:(b + 1) * BAND, :] = y_ref[tok, :]
            l_scr[b * BAND:(b + 1) * BAND, :] = l_ref[tok, :]
        for hh in range(2):
            half = _head_half((BAND, 128), hh)
            half_st = _head_half((DIL_ST, 128), hh)
            dom = jnp.where(half_st, do_scr[...], 0.0)
            delta = jnp.sum(dom * y_scr[...], axis=-1, keepdims=True)
            lcol = jnp.max(jnp.where(half_st, l_scr[...], NEG), axis=-1, keepdims=True)
            for b in range(DIL_NB):
                rows = slice(b * BAND, (b + 1) * BAND)
                qv = q_ref[0, 0, pl.ds(pl.multiple_of(base + (b + 1) * BAND, BAND), BAND), :]
                band = pl.ds(pl.multiple_of(base + b * BAND, BAND), 2 * BAND)
                sb = _nt(jnp.where(half, qv, jnp.zeros_like(qv)), k_ref[0, 0, band, :])
                s_scr[rows, :] = jnp.where(_band_keep(g, b, t, DIL_NB), sb, NEG)
                dp_scr[rows, :] = _nt(dom[rows, :].astype(BF16), v_ref[0, 0, band, :])
            pr = jnp.exp(s_scr[...] - lcol)
            p_scr[...] = pr.astype(BF16)
            ds_scr[...] = (pr * (dp_scr[...] - delta)).astype(BF16)
            for b in range(DIL_NB):
                rows = slice(b * BAND, (b + 1) * BAND)
                qv = q_ref[0, 0, pl.ds(pl.multiple_of(base + (b + 1) * BAND, BAND), BAND), :]
                band = pl.ds(pl.multiple_of(base + b * BAND, BAND), 2 * BAND)
                dqb = jnp.where(half, _nn(ds_scr[rows, :], k_ref[0, 0, band, :]), 0.0)
                if hh == 0:
                    dq_scr[rows, :] = dqb
                else:
                    dq_scr[rows, :] += dqb
                half2 = _head_half((2 * BAND, 128), hh)
                dk_scr[band, :] += jnp.where(half2, _tn(ds_scr[rows, :], qv), 0.0)
                dv_scr[band, :] += _tn(p_scr[rows, :], dom[rows, :].astype(BF16))
        for b in range(DIL_NB):
            tok_scr[pl.ds(0, 1), _dil_tok(g, b, t, DIL_NB), :] = dq_scr[b * BAND:(b + 1) * BAND, :][None]

        @pl.when(t == nst - 1)
        def _():
            for r in range(d):
                rows = _strided(r, sub_len, d)
                tok_scr[pl.ds(1, 1), rows, :] = dk_scr[BAND + r * sub_len:BAND + (r + 1) * sub_len, :][None]
                tok_scr[pl.ds(2, 1), rows, :] = dv_scr[BAND + r * sub_len:BAND + (r + 1) * sub_len, :][None]
            lanes = _rope_lanes((ch, 128), DIL_ROPE_HALF, 64, 0)
            for c0 in range(0, SEQ, ch):
                rows = slice(c0, c0 + ch)
                cv, sv = c_ref[rows, :], sn_ref[rows, :]
                dp_ref[rows, 0:128] = _rope_bwd(tok_scr[0, rows, :], cv * DIL_SCALE, sv * DIL_SCALE, DIL_ROPE_HALF, lanes).astype(BF16)
                dp_ref[rows, 128:256] = _rope_bwd(tok_scr[1, rows, :], cv, sv, DIL_ROPE_HALF, lanes).astype(BF16)
                dp_ref[rows, 256:384] = tok_scr[2, rows, :].astype(BF16)

    def inp(tq):
        return pl.BlockSpec((1, 1, BAND + SEQ, 128), lambda pr, t: (tq, pr, 0, 0))

    tok_spec = pl.BlockSpec((SEQ, 128), lambda pr, t: (0, pr))
    tab = pl.BlockSpec((SEQ, 128), lambda pr, t: (0, 0))
    st = (DIL_ST, 2 * BAND)
    return pl.pallas_call(
        body, name=f"dil_attn_bwd_g{g}", grid=(4, nst),
        in_specs=[pl.BlockSpec(memory_space=pl.ANY), inp(0), inp(1), inp(2), tok_spec, tok_spec, tok_spec, tab, tab]
        + after_specs,
        out_specs=pl.BlockSpec((SEQ, 384), lambda pr, t: (0, _qkv_block(0, g, pr) // 3)),
        out_shape=jax.ShapeDtypeStruct((SEQ, N_PAD), BF16),
        input_output_aliases={0: 0},
        scratch_shapes=[pltpu.VMEM((3, SEQ, 128), F32),
                        pltpu.VMEM((BAND + SEQ, 128), F32), pltpu.VMEM((BAND + SEQ, 128), F32),
                        pltpu.VMEM(st, F32), pltpu.VMEM(st, F32), pltpu.VMEM(st, BF16), pltpu.VMEM(st, BF16),
                        pltpu.VMEM((DIL_ST, 128), F32), pltpu.VMEM((DIL_ST, 128), F32), pltpu.VMEM((DIL_ST, 128), F32),
                        pltpu.VMEM((DIL_ST, 128), F32)],
        compiler_params=_cparams(),
    )(dp_in, qkv, qkv, qkv, dyd, yd, lse_all, rc, rs, *after)


TAIL_T = 256


def _tail(p, ya, o_g, l_g, x, target, wpm, wpd, wout, post_g):
    tm = TAIL_T

    def body(pgz_ref, ya_ref, o0_ref, o1_ref, o2_ref, l0_ref, l1_ref, l2_ref, x_ref, t_ref,
             wpm_ref, wpd_ref, wout_ref, pg_ref,
             dp_ref, dy_ref, dya_ref, dyd_ref, yd_ref, lse_ref, loss_ref, dgp_ref, dwpm_ref, dwpd_ref, dwout_ref):
        l0, l1, l2 = l0_ref[...], l1_ref[...], l2_ref[...]
        mx = jnp.maximum(jnp.maximum(l0, l1), l2)
        e0, e1, e2 = jnp.exp(l0 - mx), jnp.exp(l1 - mx), jnp.exp(l2 - mx)
        den = e0 + e1 + e2
        yd = (e0 * o0_ref[...] + e1 * o1_ref[...] + e2 * o2_ref[...]) / den
        yd_ref[...] = yd
        lse_ref[...] = mx + jnp.log(den)
        ya = ya_ref[...]

        gm, gd = pgz_ref[:, 0:1024], pgz_ref[:, 1024:2048]
        zm, zd = pgz_ref[:, 2048:2560], pgz_ref[:, 2560:3072]
        szm, szd = _sigmoid(zm), _sigmoid(zd)
        sm, sd = zm * szm, zd * szd
        ua = (ya * sm).astype(BF16)
        ud = (yd * sd).astype(BF16)
        pa = _nn(ua, wpm_ref[...])
        pd = _nn(ud, wpd_ref[...])
        sgm, sgd = _sigmoid(gm), _sigmoid(gd)
        mg = (sgm * pa + sgd * pd).astype(BF16)
        t = _nn(mg, wout_ref[...])
        r3 = lax.rsqrt(jnp.mean(t * t, axis=-1, keepdims=True) + EPS)
        n = t * r3
        pg = pg_ref[...]
        err = x_ref[...] + n * pg - t_ref[...]
        lpart = jnp.sum(err * err, axis=0, keepdims=True)

        dy = err * (1.0 / D_MODEL)
        dy_ref[...] = dy
        gpart = jnp.sum(dy * n, axis=0, keepdims=True)
        dn = dy * pg
        dt = (r3 * (dn - n * jnp.mean(dn * n, axis=-1, keepdims=True))).astype(BF16)
        dmg = _nt(dt, wout_ref[...])
        dpa = (dmg * sgm).astype(BF16)
        dpd = (dmg * sgd).astype(BF16)
        dp_ref[:, 0:1024] = (dmg * pa * sgm * (1.0 - sgm)).astype(BF16)
        dp_ref[:, 1024:2048] = (dmg * pd * sgd * (1.0 - sgd)).astype(BF16)
        dua = _nt(dpa, wpm_ref[...])
        dud = _nt(dpd, wpd_ref[...])
        dya_ref[...] = dua * sm
        dyd_ref[...] = dud * sd
        dp_ref[:, 2048:2560] = (dua * ya * szm * (1.0 + zm * (1.0 - szm))).astype(BF16)
        dp_ref[:, 2560:3072] = (dud * yd * szd * (1.0 + zd * (1.0 - szd))).astype(BF16)

        wpm, wpd, wout = _tn(ua, dpa), _tn(ud, dpd), _tn(mg, dt)

        @pl.when(pl.program_id(0) == 0)
        def _():
            loss_ref[...] = lpart
            dgp_ref[...] = gpart
            dwpm_ref[...] = wpm
            dwpd_ref[...] = wpd
            dwout_ref[...] = wout

        @pl.when(pl.program_id(0) > 0)
        def _():
            loss_ref[...] += lpart
            dgp_ref[...] += gpart
            dwpm_ref[...] += wpm
            dwpd_ref[...] += wpd
            dwout_ref[...] += wout

    def rows(w):
        return pl.BlockSpec((tm, w), lambda i: (i, 0))

    def full(shape):
        return pl.BlockSpec(shape, lambda i: (0, 0))

    def sds(w, dt):
        return jax.ShapeDtypeStruct((SEQ, w), dt)

    return pl.pallas_call(
        body, name="tail", grid=(SEQ // tm,),
        in_specs=[rows(3072), rows(512), rows(512), rows(512), rows(512), rows(512), rows(512), rows(512),
                  rows(1024), rows(1024), full((512, 1024)), full((512, 1024)), full((1024, 1024)), full((1, 1024))],
        out_specs=[rows(3072), rows(1024), rows(512), rows(512), rows(512), rows(512), full((1, 1024)), full((1, 1024)),
                   full((512, 1024)), full((512, 1024)), full((1024, 1024))],
        out_shape=[sds(N_PAD, BF16), sds(1024, F32), sds(512, F32), sds(512, F32), sds(512, F32), sds(512, F32),
                   jax.ShapeDtypeStruct((1, 1024), F32), jax.ShapeDtypeStruct((1, 1024), F32),
                   jax.ShapeDtypeStruct((512, 1024), F32), jax.ShapeDtypeStruct((512, 1024), F32),
                   jax.ShapeDtypeStruct((1024, 1024), F32)],
        compiler_params=_cparams(),
    )(p, ya, o_g[0], o_g[1], o_g[2], l_g[0], l_g[1], l_g[2], x, target, wpm, wpd, wout, post_g)


def _sum_parts(recv, own, me, tr, name):
    n, r, w = recv.shape
    if r % tr:
        return _sum_parts_cols(recv, own, me, name)
    own_spec = (pl.BlockSpec((tr, w), lambda i, me_ref: (i, 0)) if own.ndim == 2
                else pl.BlockSpec((None, tr, w), lambda i, me_ref: (me_ref[0], i, 0)))

    def body(me_ref, p_ref, own_ref, o_ref):
        mine = own_ref[...].astype(F32)
        acc = jnp.zeros((tr, w), F32)
        for s in range(n):
            acc = acc + jnp.where(me_ref[0] == s, mine, p_ref[s].astype(F32))
        o_ref[...] = acc

    return pl.pallas_call(
        body, name=name,
        grid_spec=pltpu.PrefetchScalarGridSpec(
            num_scalar_prefetch=1, grid=(r // tr,),
            in_specs=[pl.BlockSpec((n, tr, w), lambda i, me_ref: (0, i, 0)), own_spec],
            out_specs=pl.BlockSpec((tr, w), lambda i, me_ref: (i, 0))),
        out_shape=jax.ShapeDtypeStruct((r, w), F32),
    )(me.reshape(1), recv, own)


def _sum_parts_cols(recv, own, me, name):
    n, r, w = recv.shape
    tc = 128

    def body(me_ref, p_ref, own_ref, o_ref):
        mine = own_ref[...].astype(F32)
        acc = jnp.zeros((r, tc), F32)
        for s in range(n):
            acc = acc + jnp.where(me_ref[0] == s, mine, p_ref[s].astype(F32))
        o_ref[...] = acc

    return pl.pallas_call(
        body, name=name,
        grid_spec=pltpu.PrefetchScalarGridSpec(
            num_scalar_prefetch=1, grid=(w // tc,),
            in_specs=[pl.BlockSpec((n, r, tc), lambda i, me_ref: (0, 0, i)),
                      pl.BlockSpec((None, r, tc), lambda i, me_ref: (me_ref[0], 0, i))],
            out_specs=pl.BlockSpec((r, tc), lambda i, me_ref: (0, i))),
        out_shape=jax.ShapeDtypeStruct((r, w), F32),
    )(me.reshape(1), recv, own)


def _adamw(w, g, m, v, name):
    lead = w.shape[:-2]
    r, c = w.shape[-2:]
    tr = max([t for t in range(8, 257, 8) if r % t == 0], default=r)
    c1 = 1.0 - ADAM_B1 ** ADAM_STEP
    c2 = 1.0 - ADAM_B2 ** ADAM_STEP

    def body(w_ref, g_ref, m_ref, v_ref, d_ref, nm_ref, nv_ref):
        gv = g_ref[...]
        nm = ADAM_B1 * m_ref[...] + (1.0 - ADAM_B1) * gv
        nv = ADAM_B2 * v_ref[...] + (1.0 - ADAM_B2) * (gv * gv)
        nm_ref[...] = nm
        nv_ref[...] = nv
        d_ref[...] = -ADAM_LR * ((nm / c1) / (jnp.sqrt(nv / c2) + ADAM_EPS) + ADAM_WD * w_ref[...])

    zeros = (0,) * len(lead)
    spec = pl.BlockSpec((1,) * len(lead) + (tr, c), lambda i: zeros + (i, 0))
    sd = jax.ShapeDtypeStruct(w.shape, F32)
    return pl.pallas_call(
        body, name=name, grid=(r // tr,),
        in_specs=[spec] * 4, out_specs=[spec] * 3, out_shape=[sd] * 3,
    )(w, g, m, v)


def _adamw_recv(w, m, v, recv, own, me, name):
    n, r, c = recv.shape
    tr = 128
    c1 = 1.0 - ADAM_B1 ** ADAM_STEP
    c2 = 1.0 - ADAM_B2 ** ADAM_STEP

    def body(me_ref, w_ref, m_ref, v_ref, p_ref, own_ref, d_ref, nm_ref, nv_ref, g_ref):
        mine = own_ref[...].astype(F32)
        gv = jnp.zeros((tr, c), F32)
        for s in range(n):
            gv = gv + jnp.where(me_ref[0] == s, mine, p_ref[s].astype(F32))
        g_ref[0] = gv
        nm = ADAM_B1 * m_ref[0] + (1.0 - ADAM_B1) * gv
        nv = ADAM_B2 * v_ref[0] + (1.0 - ADAM_B2) * (gv * gv)
        nm_ref[0] = nm
        nv_ref[0] = nv
        d_ref[0] = -ADAM_LR * ((nm / c1) / (jnp.sqrt(nv / c2) + ADAM_EPS) + ADAM_WD * w_ref[0])

    full = pl.BlockSpec((1, tr, c), lambda i, me_ref: (0, i, 0))
    sd = jax.ShapeDtypeStruct((1, r, c), F32)
    return pl.pallas_call(
        body, name=name,
        grid_spec=pltpu.PrefetchScalarGridSpec(
            num_scalar_prefetch=1, grid=(r // tr,),
            in_specs=[full, full, full, pl.BlockSpec((n, tr, c), lambda i, me_ref: (0, i, 0)),
                      pl.BlockSpec((None, tr, c), lambda i, me_ref: (me_ref[0], i, 0))],
            out_specs=[full] * 4),
        out_shape=[sd] * 4,
    )(me.reshape(1), w, m, v, recv, own)


def _adamw_in(w_t, m_t, v_t, own_half, swapped, core):
    r, c = SHARD_SHAPES[0]
    tr = max(t for t in range(8, 257, 8) if r % t == 0)
    c1 = 1.0 - ADAM_B1 ** ADAM_STEP
    c2 = 1.0 - ADAM_B2 ** ADAM_STEP

    def body(core_ref, w_ref, m_ref, v_ref, own_ref, sw_ref, d_ref, nm_ref, nv_ref, g_ref):
        own = own_ref[...]
        col_half = lax.broadcasted_iota(jnp.int32, (tr, c), 1) // (c // 2)
        gv = jnp.where(col_half == core_ref[0], jnp.concatenate([own, own], axis=1), sw_ref[...])
        g_ref[0] = gv
        nm = ADAM_B1 * m_ref[0] + (1.0 - ADAM_B1) * gv
        nv = ADAM_B2 * v_ref[0] + (1.0 - ADAM_B2) * (gv * gv)
        nm_ref[0] = nm
        nv_ref[0] = nv
        d_ref[0] = -ADAM_LR * ((nm / c1) / (jnp.sqrt(nv / c2) + ADAM_EPS) + ADAM_WD * w_ref[0])

    full = pl.BlockSpec((1, tr, c), lambda i, core_ref: (0, i, 0))
    sd = jax.ShapeDtypeStruct((1, r, c), F32)
    return pl.pallas_call(
        body, name="adamw_in",
        grid_spec=pltpu.PrefetchScalarGridSpec(
            num_scalar_prefetch=1, grid=(r // tr,),
            in_specs=[full, full, full, pl.BlockSpec((tr, c // 2), lambda i, core_ref: (i, 0)),
                      pl.BlockSpec((tr, c), lambda i, core_ref: (i, 0))],
            out_specs=[full] * 4),
        out_shape=[sd] * 4,
    )(core.reshape(1), w_t, m_t, v_t, own_half, swapped)


ANY = pl.BlockSpec(memory_space=pl.ANY)


def _my_place():
    return lax.axis_index("x"), lax.axis_index("y"), lax.axis_index("c")


HBM = pl.BlockSpec(memory_space=pltpu.HBM)
SEM = pl.BlockSpec(memory_space=pltpu.SEMAPHORE)
DATAFLOW = pltpu.SideEffectType.DATAFLOW_SIDE_EFFECTING


def _near_chips(x, y):
    return [(1 - x, y), (x, 1 - y)]


def _half(mi, hc):
    r, c = SHARD_SHAPES[mi]
    if mi == 0:
        return pl.ds(0, r), pl.ds(pl.multiple_of(hc * (c // 2), 128), c // 2)
    return pl.ds(pl.multiple_of(hc * (r // 2), 16), r // 2), pl.ds(0, c)


def _gather_copies(land_refs, send_sems, recv_sems):
    x, y, c = _my_place()
    out, back = [], []
    for mi in range(N_MATS):
        rows, cols = _half(mi, c)
        mine = land_refs[mi].at[2 * x + y, rows, cols]
        for j, (cx, cy) in enumerate(_near_chips(x, y)):
            sems = dict(send_sem=send_sems.at[mi * 2 + j], recv_sem=recv_sems.at[mi * 2 + j],
                        device_id=(cx, cy, c), device_id_type=MESH)
            out.append(pltpu.make_async_remote_copy(src_ref=mine, dst_ref=mine, **sems))
            got = land_refs[mi].at[2 * cx + cy, rows, cols]
            back.append(pltpu.make_async_remote_copy(src_ref=got, dst_ref=got, **sems))
    return out, back


def _gather_start(landing):
    n = N_MATS

    def body(*refs):
        out, _ = _gather_copies(refs[:n], refs[n], refs[n + 1])
        for cp in out:
            cp.start()
        refs[-1][...] = jnp.zeros_like(refs[-1])

    hbm = [pltpu.HBM(a.shape, a.dtype) for a in landing]
    outs = pl.pallas_call(
        body, name="gather_start",
        out_shape=(pltpu.SemaphoreType.DMA((2 * n,)), pltpu.SemaphoreType.DMA((2 * n,)), *hbm,
                   jax.ShapeDtypeStruct((8, 128), F32)),
        in_specs=[HBM] * n, out_specs=(SEM, SEM, *[HBM] * n, pl.BlockSpec(memory_space=pltpu.VMEM)),
        input_output_aliases={i: 2 + i for i in range(n)},
        compiler_params=pltpu.CompilerParams(has_side_effects=DATAFLOW),
    )(*[pltpu.with_memory_space_constraint(a, pltpu.HBM) for a in landing])
    return outs[:-1], outs[-1]


def _gather_wait(handle, after):
    n = N_MATS

    def body(*refs):
        out, back = _gather_copies(refs[:n], refs[n], refs[n + 1])
        for cp, arrival in zip(out, back):
            cp.wait_send()
            arrival.wait_recv()

    bufs = handle[2:]
    after, after_specs = _after(after)
    res = pl.pallas_call(
        body, name="gather_wait", out_shape=tuple(pltpu.HBM(b.shape, b.dtype) for b in bufs),
        in_specs=[HBM] * n + [SEM, SEM] + after_specs, out_specs=tuple([HBM] * n),
        input_output_aliases={i: i for i in range(n)},
        compiler_params=pltpu.CompilerParams(has_side_effects=DATAFLOW),
    )(*bufs, handle[0], handle[1], *after)
    return list(res)


def _relay_share(gathered):
    n = N_MATS

    def body(*refs):
        out_refs = refs[n:2 * n]
        send_sems, recv_sems = refs[2 * n:]
        x, y, c = _my_place()
        sibling = (x, y, 1 - c)
        relayed = 2 * (x ^ (1 - c)) + (y ^ c)
        relay_to = (x ^ c, y ^ (1 - c), c)
        far = 2 * (1 - x) + (1 - y)
        near = [2 * (1 - x) + y, 2 * x + (1 - y)]

        def copy(k, mi, shard, hc, to):
            blk = out_refs[mi].at[(shard,) + _half(mi, hc)]
            return pltpu.make_async_remote_copy(src_ref=blk, dst_ref=blk, send_sem=send_sems.at[mi * 4 + k],
                                                recv_sem=recv_sems.at[mi * 4 + k], device_id=to, device_id_type=MESH)

        sends = []
        for mi in range(n):
            sends.append(copy(0, mi, relayed, c, relay_to))
            sends += [copy(1 + j, mi, near[j], c, sibling) for j in range(2)]
        for cp in sends:
            cp.start()
        for mi in range(n):
            copy(0, mi, far, c, relay_to).wait_recv()
            cp = copy(3, mi, far, c, sibling)
            cp.start()
            sends.append(cp)
        for mi in range(n):
            for j in range(2):
                copy(1 + j, mi, near[j], 1 - c, sibling).wait_recv()
            copy(3, mi, far, 1 - c, sibling).wait_recv()
        for cp in sends:
            cp.wait_send()

    return pl.pallas_call(
        body, name="relay_share",
        in_specs=[ANY] * n, out_specs=[ANY] * n,
        out_shape=[jax.ShapeDtypeStruct(g.shape, g.dtype) for g in gathered],
        input_output_aliases={i: i for i in range(n)},
        scratch_shapes=[pltpu.SemaphoreType.DMA((4 * n,)), pltpu.SemaphoreType.DMA((4 * n,))],
    )(*gathered)


def _peers(x, y, c):
    out = []
    for k in range(1, 8):
        px, py, pc = x ^ (k >> 2), y ^ ((k >> 1) & 1), c ^ (k & 1)
        out.append((k - 1, (px, py, pc), 4 * px + 2 * py + pc))
    return out


def _exchange_start(parts, name):
    n = len(parts)

    def body(*refs):
        p_refs, land_refs = refs[:n], refs[n:2 * n]
        send_sems, recv_sems, token = refs[2 * n], refs[2 * n + 1], refs[-1]
        x, y, c = _my_place()
        me = 4 * x + 2 * y + c
        for k, dev, peer in _peers(x, y, c):
            for mi in range(n):
                pltpu.make_async_remote_copy(
                    src_ref=p_refs[mi].at[peer], dst_ref=land_refs[mi].at[me], send_sem=send_sems.at[k * n + mi],
                    recv_sem=recv_sems.at[k * n + mi], device_id=dev, device_id_type=MESH).start()
        token[...] = jnp.zeros_like(token)

    hbm = [pltpu.HBM(p.shape, p.dtype) for p in parts]
    outs = pl.pallas_call(
        body, name=name + "_start",
        out_shape=(pltpu.SemaphoreType.DMA((7 * n,)), pltpu.SemaphoreType.DMA((7 * n,)), *hbm, *hbm,
                   jax.ShapeDtypeStruct((8, 128), F32)),
        in_specs=[HBM] * (2 * n), out_specs=(SEM, SEM, *[HBM] * (2 * n), pl.BlockSpec(memory_space=pltpu.VMEM)),
        input_output_aliases={i: 2 + i for i in range(2 * n)},
        compiler_params=pltpu.CompilerParams(has_side_effects=DATAFLOW),
    )(*[pltpu.with_memory_space_constraint(p, pltpu.HBM) for p in parts],
      *[pltpu.with_memory_space_constraint(lax.empty(p.shape, p.dtype), pltpu.HBM) for p in parts])
    return (name, outs[:-1]), outs[-1]


def _exchange_wait(handle, after):
    name, outs = handle
    n = (len(outs) - 2) // 2

    def body(*refs):
        p_refs, land_refs = refs[:n], refs[n:2 * n]
        send_sems, recv_sems = refs[2 * n], refs[2 * n + 1]
        x, y, c = _my_place()
        me = 4 * x + 2 * y + c
        for k, dev, peer in _peers(x, y, c):
            for mi in range(n):
                pltpu.make_async_remote_copy(
                    src_ref=p_refs[mi].at[peer], dst_ref=land_refs[mi].at[me], send_sem=send_sems.at[k * n + mi],
                    recv_sem=recv_sems.at[k * n + mi], device_id=dev, device_id_type=MESH).wait_send()
                slot = land_refs[mi].at[peer]
                pltpu.make_async_remote_copy(
                    src_ref=slot, dst_ref=slot, send_sem=send_sems.at[k * n + mi],
                    recv_sem=recv_sems.at[k * n + mi], device_id=dev, device_id_type=MESH).wait_recv()

    bufs = outs[2:]
    res = pl.pallas_call(
        body, name=name + "_wait", out_shape=tuple(pltpu.HBM(b.shape, b.dtype) for b in bufs),
        in_specs=[HBM] * (2 * n) + [SEM, SEM, ANY], out_specs=tuple([HBM] * (2 * n)),
        input_output_aliases={i: i for i in range(2 * n)},
        compiler_params=pltpu.CompilerParams(has_side_effects=DATAFLOW),
    )(*bufs, outs[0], outs[1], after)
    return list(res[n:])


def _swap_halves(half_in, gvec):
    def body(g_ref, gv_ref, out_ref, rg_ref, send_sems, recv_sems):
        x, y, c = _my_place()
        me = 4 * x + 2 * y + c
        sibling = (x, y, 1 - c)

        def half(hc):
            return out_ref.at[:, pl.ds(pl.multiple_of(hc * 512, 128), 512)]

        sends = [pltpu.make_async_remote_copy(src_ref=g_ref, dst_ref=half(c), send_sem=send_sems.at[7],
                                              recv_sem=recv_sems.at[7], device_id=sibling, device_id_type=MESH)]
        for k, dev, peer in _peers(x, y, c):
            sends.append(pltpu.make_async_remote_copy(src_ref=gv_ref, dst_ref=rg_ref.at[me], send_sem=send_sems.at[k],
                                                      recv_sem=recv_sems.at[k], device_id=dev, device_id_type=MESH))
        for cp in sends:
            cp.start()
        got = half(1 - c)
        pltpu.make_async_remote_copy(src_ref=got, dst_ref=got, send_sem=send_sems.at[7], recv_sem=recv_sems.at[7],
                                     device_id=sibling, device_id_type=MESH).wait_recv()
        for k, dev, peer in _peers(x, y, c):
            got = rg_ref.at[peer]
            pltpu.make_async_remote_copy(src_ref=got, dst_ref=got, send_sem=send_sems.at[k], recv_sem=recv_sems.at[k],
                                         device_id=dev, device_id_type=MESH).wait_recv()
        for cp in sends:
            cp.wait_send()

    return pl.pallas_call(
        body, name="swap_halves",
        in_specs=[ANY, ANY], out_specs=[ANY, ANY],
        out_shape=[jax.ShapeDtypeStruct(SHARD_SHAPES[0], F32), jax.ShapeDtypeStruct((8, 8, N_GVEC), F32)],
        scratch_shapes=[pltpu.SemaphoreType.DMA((8,)), pltpu.SemaphoreType.DMA((8,))],
    )(half_in, gvec)


def _set_slot(arr, block, idx):
    return lax.dynamic_update_slice(arr, block[None], (idx,) + (0,) * block.ndim)


PAD_RUNS = ((6304, 8352, 0), (5280, 6304, COL_Z), (672, 5280, COL_QKV), (0, 640, COL_LAT), (640, 672, COL_LAT + 704))
N_QKV = COL_LAT - COL_QKV


def _qkv_rows_regroup(a, to_padded):
    if to_padded:
        a4 = a.reshape(3, 12, 128, a.shape[1])
        return jnp.stack([a4[0], a4[1], a4[2]], axis=1).reshape(a.shape)
    a4 = a.reshape(12, 3, 128, a.shape[1])
    return jnp.concatenate([a4[:, tq].reshape(N_QKV // 3, a.shape[1]) for tq in range(3)], axis=0)
W_IN_SHARD = 2088


def _full_weights(gathered):
    def cols(a):
        return jnp.concatenate([a[s] for s in range(4)], axis=1)

    w_uq, w_ukv, w_pm, w_pd = [cols(a) for a in gathered[1:5]]
    w_out = gathered[5].reshape(D_MODEL, D_MODEL)
    w_in_t = gathered[0].reshape(4 * W_IN_SHARD, D_MODEL)
    pieces, at = [], 0
    for lo, hi, pad_lo in sorted(PAD_RUNS, key=lambda t: t[2]):
        if pad_lo > at:
            pieces.append(jnp.zeros((pad_lo - at, D_MODEL), w_in_t.dtype))
        pieces.append(_qkv_rows_regroup(w_in_t[lo:hi], True) if pad_lo == COL_QKV else w_in_t[lo:hi])
        at = pad_lo + hi - lo
    pieces.append(jnp.zeros((N_PAD - at, D_MODEL), w_in_t.dtype))
    w_pad_t = jnp.concatenate(pieces, axis=0)
    z32 = jnp.zeros((Q_RANK, 32), w_uq.dtype)
    wuq_pad = jnp.concatenate([t for h in range(MLA_HEADS) for t in (w_uq[:, h * 96:(h + 1) * 96], z32)], axis=1)
    z64 = jnp.zeros((KV_RANK, 64), w_ukv.dtype)
    wk_pad = jnp.concatenate([t for h in range(MLA_HEADS) for t in (w_ukv[:, h * 128:h * 128 + 64], z64)], axis=1)
    wv = jnp.concatenate([w_ukv[:, h * 128 + 64:(h + 1) * 128] for h in range(MLA_HEADS)], axis=1)
    return w_pad_t, wuq_pad, wk_pad, wv, w_pm, w_pd, w_out


W_IN_LAT = 672


def _grad_parts_in_early(dwt_early):
    dwt_early = jnp.concatenate([dwt_early[:COL_QKV], _qkv_rows_regroup(dwt_early[COL_QKV:COL_LAT], False)], axis=0)

    def in_block(s, h):
        cols = slice(h * 512, (h + 1) * 512)
        out = []
        for lo, hi, pad_lo in sorted(PAD_RUNS):
            a_, b_ = max(lo, s * W_IN_SHARD), min(hi, (s + 1) * W_IN_SHARD)
            if a_ < b_:
                out.append(jnp.zeros((b_ - a_, 512), dwt_early.dtype) if pad_lo >= COL_LAT
                           else dwt_early[pad_lo + a_ - lo:pad_lo + b_ - lo, cols])
        return jnp.concatenate(out, axis=0)

    return jnp.stack([in_block(s, h) for s in range(4) for h in range(2)])


def _grad_parts_in_late(dwt_late):
    rows = jnp.concatenate([dwt_late[0:640], dwt_late[704:736]], axis=0)
    zero = jnp.zeros((W_IN_LAT, 512), dwt_late.dtype)
    return jnp.stack([rows[:, 0:512], rows[:, 512:1024]] + [zero] * 6)


def _shard_blocks(m, axis=1):
    n = m.shape[axis] // 4
    cut = (lambda s: m[:, s * n:(s + 1) * n]) if axis == 1 else (lambda s: m[s * n:(s + 1) * n])
    return jnp.stack([cut(s) for s in range(4) for _ in range(2)])


def _grad_parts_mla(dwuq_pad, dwk_pad, dwv):
    d_uq = jnp.concatenate([dwuq_pad[:, h * 128:h * 128 + 96] for h in range(MLA_HEADS)], axis=1)
    d_ukv = jnp.concatenate([t for h in range(MLA_HEADS) for t in (dwk_pad[:, h * 128:h * 128 + 64], dwv[:, h * 64:(h + 1) * 64])],
                            axis=1)
    return [_shard_blocks(d_uq.astype(BF16)), _shard_blocks(d_ukv.astype(BF16))]


def _rope_tables(positions, token=None):
    pos = positions.reshape(SEQ).astype(F32)
    if token is not None:
        pos = pos + token[0, 0]
    lane = jnp.arange(128)

    def table(rot, first, period):
        inv = ROPE_THETA ** (-jnp.arange(0, rot, 2, dtype=F32) / rot)
        half = rot // 2
        off = lane % period - first
        in1, in2 = (off >= 0) & (off < half), (off >= half) & (off < rot)
        inv_lane = jnp.where(in1 | in2, inv[jnp.clip(off % half, 0, half - 1)], 0.0)
        sign = jnp.where(in1, -1.0, 1.0).astype(F32)
        ang = pos[:, None] * inv_lane[None, :]
        return jnp.cos(ang), jnp.sin(ang) * sign[None, :]

    return table(32, 64, 128), table(16, 0, 64)


class _Links:
    def __init__(self, mats, chip, me):
        landing = [_set_slot(lax.empty((4,) + m.shape, m.dtype), m, chip) for m in mats]
        self.gather, self.token = _gather_start(landing)
        self.me, self.sent, self.handles, self.sums, self.raw = me, {}, {}, {}, {}

    def weights(self, after):
        return _relay_share(_gather_wait(self.gather, after))

    def send(self, blocks, name):
        self.sent[name] = blocks
        self.handles[name], token = _exchange_start(blocks, name)
        return token

    def collect(self, name, after, parts):
        recv = _exchange_wait(self.handles[name], after)
        for r, own, part in zip(recv, self.sent[name], parts):
            if part.startswith("in_"):
                self.sums[part] = _sum_parts(r, own, self.me, 64, "sum_grad_" + part)
            else:
                self.raw[part] = (r, own)
        return tuple(self.sums[part] for part in parts if part in self.sums)


def _device_grads(x, positions, target, gains, links):
    pre_g, q_g, kv_g, post_g = gains
    (mc, ms), (dc, ds) = _rope_tables(positions, links.token)
    h = _prenorm_fwd(x, pre_g, links.token)
    w_pad_t, wuq_pad, wk_pad, wv, w_pm, w_pd, w_out = _full_weights(links.weights((h, mc, ms, dc, ds)))

    p_gz = _matmul(h, w_pad_t, "nt", F32, 1024, 1536, 1024, "in_proj_gates", b_cols=(0, COL_QKV // 1536))
    p_qkv = _matmul(h, w_pad_t, "nt", F32, 1024, 1536, 1024, "in_proj_dilated", b_cols=(COL_QKV // 1536, N_QKV // 1536),
                    lane_blocks=True)
    p_lat = _matmul(h, w_pad_t, "nt", F32, 1024, N_LAT, 1024, "in_proj_latent", b_cols=(COL_LAT // N_LAT, 1))
    q, k, v = _mla_prep_fwd(p_lat, q_g, kv_g, wuq_pad, wk_pad, wv, mc, ms)
    ya, lse_m = _mla_flash_fwd(q, k, v)
    qkv = [_dil_prep_fwd(p_qkv, dc, ds, g) for g in range(3)]
    o_g, l_g = zip(*[_dil_attn_fwd(qkv[g], g) for g in range(3)])
    (dp, dy, dya, dyd, yd, lse_d, loss_cols, dg_post, dwpm, dwpd, dwout) = _tail(
        p_gz, ya, o_g, l_g, x, target, w_pm, w_pd, w_out, post_g)

    for g in range(3):
        dp = _dil_attn_bwd(dp, qkv[g], dyd, yd, lse_d, dc, ds, g)
    dw_early = _matmul(dp, h, "tn", BF16, 1536, 1024, 2048, "dw_in_early", a_cols=(0, COL_LAT // 1536))
    token = links.send([_grad_parts_in_early(dw_early), _shard_blocks(dwpm.astype(BF16)), _shard_blocks(dwpd.astype(BF16)),
                        _shard_blocks(dwout.astype(BF16), axis=0)], "exchange_early")

    dq, dk, dv = _mla_flash_bwd(q, k, v, ya, dya, lse_m, token)
    dp, dwuq_pad, dwk_pad, dwv, dg_q, dg_kv = _mla_prep_bwd(dp, p_lat, dq, dk, dv, q_g, kv_g, wuq_pad, wk_pad, wv, mc, ms)
    dw_late = _matmul(dp, h, "tn", BF16, N_LAT, 1024, 2048, "dw_in_late", a_cols=(COL_LAT // N_LAT, 1))
    token = links.send([_grad_parts_in_late(dw_late)] + _grad_parts_mla(dwuq_pad, dwk_pad, dwv), "exchange_late")
    early = links.collect("exchange_early", dw_late, ("in_early", "pm", "pd", "out"))

    grad_x, dg_pre = _dh_prenorm_bwd(dp, w_pad_t, x, dy, pre_g, (token,) + tuple(early))
    links.collect("exchange_late", grad_x, ("in_late", "uq", "ukv"))

    loss_part = jnp.pad((jnp.sum(loss_cols) * (0.5 / D_MODEL)).reshape(1, 1), ((0, 0), (0, N_GVEC - N_GAINS - 1)))
    gvec = jnp.concatenate([dg_pre, dg_q, dg_kv, dg_post, loss_part], axis=1)
    return grad_x, gvec


def kernel(x, positions, pre_norm_g, w_in, q_norm_g, w_uq, kv_norm_g, w_ukv, w_proj_mla, w_proj_dil, w_out, post_norm_g, loss_target, m_pre_norm_g, m_w_in, m_q_norm_g, m_w_uq, m_kv_norm_g, m_w_ukv, m_w_proj_mla, m_w_proj_dil, m_w_out, m_post_norm_g, v_pre_norm_g, v_w_in, v_q_norm_g, v_w_uq, v_kv_norm_g, v_w_ukv, v_w_proj_mla, v_w_proj_dil, v_w_out, v_post_norm_g):
    xi, yi, ci = _my_place()
    chip, me = 2 * xi + yi, 4 * xi + 2 * yi + ci
    mats = [jnp.swapaxes(w_in, 1, 2)] + [w_uq, w_ukv, w_proj_mla, w_proj_dil, w_out]
    mats = [w.reshape(w.shape[1:]).astype(BF16) for w in mats]
    links = _Links(mats, chip, me)
    gains = (pre_norm_g, q_norm_g, kv_norm_g, post_norm_g)
    grad_x, gvec = _device_grads(x[0], positions, loss_target[0], gains, links)

    sums = links.sums
    in_e = sums["in_early"]
    half_in = jnp.concatenate([in_e[:W_IN_LAT] + jnp.where(chip == 0, sums["in_late"], 0.0), in_e[W_IN_LAT:]], axis=0)
    gvec8 = jnp.pad(gvec, ((0, 7), (0, 0)))
    swapped_in, recv_gains = _swap_halves(half_in, gvec8)
    g_gains = _sum_parts(recv_gains, gvec8, me, 8, "sum_gain_parts")[0:1]
    loss = g_gains[0, N_GAINS]
    sw = lambda a: jnp.swapaxes(a, 1, 2)
    d_in, m_in, v_in, g_in = [sw(o) for o in _adamw_in(sw(w_in), sw(m_w_in), sw(v_w_in), half_in, swapped_in, ci)]
    off = [0, 1024, 1408, 1664, 2688]
    g_gain = [g_gains[:, off[i]:off[i + 1]] for i in range(4)]
    ws = [pre_norm_g, w_in, q_norm_g, w_uq, kv_norm_g, w_ukv, w_proj_mla, w_proj_dil, w_out, post_norm_g]
    ms = [m_pre_norm_g, m_w_in, m_q_norm_g, m_w_uq, m_kv_norm_g, m_w_ukv, m_w_proj_mla, m_w_proj_dil, m_w_out, m_post_norm_g]
    vs = [v_pre_norm_g, v_w_in, v_q_norm_g, v_w_uq, v_kv_norm_g, v_w_ukv, v_w_proj_mla, v_w_proj_dil, v_w_out, v_post_norm_g]
    part_of = [None, "in", None, "uq", None, "ukv", "pm", "pd", "out", None]
    gain_of = iter(g_gain)
    grads, deltas, new_m, new_v = [], [], [], []
    for i, (w, m, v, part) in enumerate(zip(ws, ms, vs, part_of)):
        if part == "in":
            d_, m_, v_, g = d_in, m_in, v_in, g_in
        elif part is not None:
            d_, m_, v_, g = _adamw_recv(w, m, v, *links.raw[part], me, f"adamw_{i}")
        else:
            g = next(gain_of)
            d_, m_, v_ = _adamw(w, g, m, v, f"adamw_{i}")
        grads.append(g)
        deltas.append(d_)
        new_m.append(m_)
        new_v.append(v_)
    return (loss, grad_x.reshape(x.shape), *grads, *deltas, *new_m, *new_v)
```

```python
import jax
import jax.numpy as jnp
from jax import lax
from jax.experimental import pallas as pl
from jax.experimental.pallas import tpu as pltpu

F32 = jnp.float32
BF16 = jnp.bfloat16

SEQ = 4096
D_MODEL = 1024
EPS = 1e-6
ROPE_THETA = 500000.0
MLA_HEADS = 8
Q_RANK = 384
KV_RANK = 256
MLA_SCALE = 96.0 ** -0.5
MLA_ROPE_HALF = 16
DIL_DILATIONS = (1, 4, 16)
DIL_ROPE_HALF = 8
DIL_SCALE = 0.125
BAND = 128

N_LAT = 768
COL_Z, COL_QKV, COL_LAT = 2048, 3072, 7680
N_PAD = 8448


def _qkv_block(tq, g, pr):
    return COL_QKV // 128 + (g * 4 + pr) * 3 + tq

IN_SPLITS = (384, 256, 32, 4608, 512, 512, 1024, 1024)

SHARD_SHAPES = ((2088, 1024), (384, 192), (256, 256), (512, 256), (512, 256), (256, 1024))
N_MATS = len(SHARD_SHAPES)
N_GAINS = 2688
N_GVEC = N_GAINS + 128

ADAM_LR, ADAM_B1, ADAM_B2, ADAM_EPS, ADAM_WD, ADAM_STEP = 0.001, 0.9, 0.999, 1e-08, 0.01, 10

VMEM_LIMIT = 56 * 1024 * 1024
NEG = -1e30
MESH = pl.DeviceIdType.MESH


def _cparams(**kw):
    return pltpu.CompilerParams(vmem_limit_bytes=VMEM_LIMIT, **kw)


def _dot(a, b, dims):
    return lax.dot_general(a, b, (dims, ((), ())), preferred_element_type=F32)


def _nn(a, b):
    return _dot(a, b, ((1,), (0,)))


def _nt(a, b):
    return _dot(a, b, ((1,), (1,)))


def _tn(a, b):
    return _dot(a, b, ((0,), (0,)))


def _rope_lanes(shape, half, period, first):
    lane = lax.broadcasted_iota(jnp.int32, shape, len(shape) - 1) % period
    return (lane >= first) & (lane < first + half), (lane >= first + half) & (lane < first + 2 * half)


def _rope_fwd(x, c, s, half, lanes):
    x1, _ = lanes
    return x * c + jnp.where(x1, pltpu.roll(x, 128 - half, 1), pltpu.roll(x, half, 1)) * s


def _rope_bwd(g, c, s, half, lanes):
    x1, x2 = lanes
    gs = g * s
    return g * c + jnp.where(x2, pltpu.roll(gs, half, 1), jnp.where(x1, pltpu.roll(gs, 128 - half, 1), 0.0))


def _sigmoid(x):
    return 1.0 / (1.0 + jnp.exp(-x))


def _after(token):
    tokens = [t for t in (token if isinstance(token, (tuple, list)) else [token]) if t is not None]
    return tokens, [pl.BlockSpec(memory_space=pl.ANY)] * len(tokens)


def _matmul(a, b, mode, out_dtype, tm, tn, tk, name, token=None, b_cols=None, a_cols=None, lane_blocks=False):
    after, after_specs = _after(token)
    if mode == "nn":
        (m, k), n = a.shape, b.shape[1]
        first = 0
        if b_cols is not None:
            first, n = b_cols[0], b_cols[1] * tn
        a_spec = pl.BlockSpec((tm, tk), lambda j, i, kk: (i, kk))
        b_spec = pl.BlockSpec((tk, tn), lambda j, i, kk: (kk, j + first))
        dot = _nn
    elif mode == "nt":
        (m, k), n = a.shape, b.shape[0]
        first = 0
        if b_cols is not None:
            first, n = b_cols[0], b_cols[1] * tn
        a_spec = pl.BlockSpec((tm, tk), lambda j, i, kk: (i, kk))
        b_spec = pl.BlockSpec((tn, tk), lambda j, i, kk: (j + first, kk))
        dot = _nt
    else:
        (k, m), n = a.shape, b.shape[1]
        first = 0
        if a_cols is not None:
            first, m = a_cols[0], a_cols[1] * tm
        a_spec = pl.BlockSpec((tk, tm), lambda j, i, kk: (kk, i + first))
        b_spec = pl.BlockSpec((tk, tn), lambda j, i, kk: (kk, j))
        dot = _tn
    assert m % tm == 0 and n % tn == 0 and k % tk == 0, (name, m, n, k, tm, tn, tk)
    nk = k // tk

    def body(a_ref, b_ref, *rest):
        o_ref, acc_ref = rest[-2:]
        kk = pl.program_id(2)
        part = dot(a_ref[...], b_ref[...])

        @pl.when(kk == 0)
        def _():
            acc_ref[...] = part

        @pl.when(kk > 0)
        def _():
            acc_ref[...] += part

        @pl.when(kk == nk - 1)
        def _():
            if lane_blocks:
                for blk in range(tn // 128):
                    o_ref[blk] = acc_ref[:, blk * 128:(blk + 1) * 128].astype(o_ref.dtype)
            else:
                o_ref[...] = acc_ref[...].astype(o_ref.dtype)

    if lane_blocks:
        out_spec = pl.BlockSpec((tn // 128, tm, 128), lambda j, i, kk: (j, i, 0))
        out_shape = jax.ShapeDtypeStruct((n // 128, m, 128), out_dtype)
    else:
        out_spec = pl.BlockSpec((tm, tn), lambda j, i, kk: (i, j))
        out_shape = jax.ShapeDtypeStruct((m, n), out_dtype)
    return pl.pallas_call(
        body, name=name, grid=(n // tn, m // tm, nk),
        in_specs=[a_spec, b_spec] + after_specs,
        out_specs=out_spec, out_shape=out_shape,
        scratch_shapes=[pltpu.VMEM((tm, tn), F32)],
        compiler_params=_cparams(),
    )(a, b, *after)


def _prenorm_fwd(x, g, token=None):
    tm = 512
    after, after_specs = _after(token)

    def body(x_ref, g_ref, *rest):
        xv = x_ref[...]
        r = lax.rsqrt(jnp.mean(xv * xv, axis=-1, keepdims=True) + EPS)
        rest[-1][...] = (xv * r * g_ref[...]).astype(BF16)

    return pl.pallas_call(
        body, name="prenorm_fwd", grid=(SEQ // tm,),
        in_specs=[pl.BlockSpec((tm, D_MODEL), lambda i: (i, 0)), pl.BlockSpec((1, D_MODEL), lambda i: (0, 0))] + after_specs,
        out_specs=pl.BlockSpec((tm, D_MODEL), lambda i: (i, 0)),
        out_shape=jax.ShapeDtypeStruct((SEQ, D_MODEL), BF16),
    )(x, g, *after)


def _dh_prenorm_bwd(dp, w_pad_t, x, dy, g, token=None):
    tm, tk = 1024, 1408
    nk = N_PAD // tk
    after, after_specs = _after(token)

    def body(a_ref, b_ref, x_ref, dy_ref, g_ref, *rest):
        gx_ref, dg_ref, acc_ref = rest[-3:]
        i, kk = pl.program_id(0), pl.program_id(1)
        part = _nn(a_ref[...], b_ref[...])

        @pl.when(kk == 0)
        def _():
            acc_ref[...] = part

        @pl.when(kk > 0)
        def _():
            acc_ref[...] += part

        @pl.when(kk == nk - 1)
        def _():
            xv = x_ref[...]
            r = lax.rsqrt(jnp.mean(xv * xv, axis=-1, keepdims=True) + EPS)
            n = xv * r
            dhv = acc_ref[...]
            dn = dhv * g_ref[...]
            gx_ref[...] = dy_ref[...] + r * (dn - n * jnp.mean(dn * n, axis=-1, keepdims=True))
            cols = jnp.sum(dhv * n, axis=0, keepdims=True)

            @pl.when(i == 0)
            def _():
                dg_ref[...] = cols

            @pl.when(i > 0)
            def _():
                dg_ref[...] += cols

    row = pl.BlockSpec((tm, D_MODEL), lambda i, kk: (i, 0))
    vec = pl.BlockSpec((1, D_MODEL), lambda i, kk: (0, 0))
    return pl.pallas_call(
        body, name="dh_prenorm_bwd", grid=(SEQ // tm, nk),
        in_specs=[pl.BlockSpec((tm, tk), lambda i, kk: (i, kk)), pl.BlockSpec((tk, D_MODEL), lambda i, kk: (kk, 0)),
                  row, row, vec] + after_specs,
        out_specs=[row, vec],
        out_shape=[jax.ShapeDtypeStruct((SEQ, D_MODEL), F32), jax.ShapeDtypeStruct((1, D_MODEL), F32)],
        scratch_shapes=[pltpu.VMEM((tm, D_MODEL), F32)],
        compiler_params=_cparams(),
    )(dp, w_pad_t, x, dy, g, *after)


def _mla_prep_fwd(p, qg, kvg, wuq, wk, wv, rc, rs):
    tm = 1024

    def body(lat_ref, qg_ref, kvg_ref, wuq_ref, wk_ref, wv_ref, c_ref, s_ref, q_ref, k_ref, v_ref):
        c, s = c_ref[...], s_ref[...]
        lanes = _rope_lanes((tm, 128), MLA_ROPE_HALF, 128, 64)
        cq = lat_ref[:, 0:Q_RANK]
        r1 = lax.rsqrt(jnp.mean(cq * cq, axis=-1, keepdims=True) + EPS)
        cqn = (cq * r1 * qg_ref[...]).astype(BF16)
        q = _nn(cqn, wuq_ref[...])
        for h in range(MLA_HEADS):
            sl = slice(h * 128, (h + 1) * 128)
            q_ref[:, sl] = (_rope_fwd(q[:, sl], c, s, MLA_ROPE_HALF, lanes) * MLA_SCALE).astype(BF16)
        ckv = lat_ref[:, Q_RANK:Q_RANK + KV_RANK]
        r2 = lax.rsqrt(jnp.mean(ckv * ckv, axis=-1, keepdims=True) + EPS)
        ckvn = (ckv * r2 * kvg_ref[...]).astype(BF16)
        krr = _rope_fwd(lat_ref[:, Q_RANK + KV_RANK:N_LAT], c, s, MLA_ROPE_HALF, lanes)
        kn = _nn(ckvn, wk_ref[...])
        for h in range(MLA_HEADS):
            sl = slice(h * 128, (h + 1) * 128)
            k_ref[:, sl] = (kn[:, sl] + krr).astype(BF16)
        v_ref[...] = _nn(ckvn, wv_ref[...]).astype(BF16)

    def full(shape):
        return pl.BlockSpec(shape, lambda i: (0, 0))

    def rows(w):
        return pl.BlockSpec((tm, w), lambda i: (i, 0))

    return pl.pallas_call(
        body, name="mla_prep_fwd", grid=(SEQ // tm,),
        in_specs=[pl.BlockSpec((tm, N_LAT), lambda i: (i, 0)),
                  full((1, Q_RANK)), full((1, KV_RANK)), full((Q_RANK, 1024)), full((KV_RANK, 1024)),
                  full((KV_RANK, 512)), rows(128), rows(128)],
        out_specs=[rows(1024), rows(1024), rows(512)],
        out_shape=[jax.ShapeDtypeStruct((SEQ, 1024), BF16), jax.ShapeDtypeStruct((SEQ, 1024), BF16),
                   jax.ShapeDtypeStruct((SEQ, 512), BF16)],
        compiler_params=_cparams(),
    )(p, qg, kvg, wuq, wk, wv, rc, rs)


def _mla_prep_bwd(dp_in, p, dq, dk, dv, qg, kvg, wuq, wk, wv, rc, rs):
    tm = 1024

    def body(dp_any, lat_ref, dq_ref, dk_ref, dv_ref, qg_ref, kvg_ref, wuq_ref, wk_ref, wv_ref,
             c_ref, s_ref, dp_ref, dwuq_ref, dwk_ref, dwv_ref, dgq_ref, dgkv_ref, dqb_ref, dkb_ref):
        del dp_any
        c, s = c_ref[...], s_ref[...]
        lanes = _rope_lanes((tm, 128), MLA_ROPE_HALF, 128, 64)
        lane = lax.broadcasted_iota(jnp.int32, (tm, 128), 1)
        dkr = jnp.zeros((tm, 128), F32)
        for h in range(MLA_HEADS):
            sl = slice(h * 128, (h + 1) * 128)
            dqb_ref[:, sl] = _rope_bwd(dq_ref[:, sl] * MLA_SCALE, c, s, MLA_ROPE_HALF, lanes).astype(BF16)
            dkh = dk_ref[:, sl]
            dkr = dkr + dkh
            dkb_ref[:, sl] = jnp.where(lane < 64, dkh, 0.0).astype(BF16)
        dkr = jnp.where((lane >= 64) & (lane < 96), dkr, 0.0)
        dkr = _rope_bwd(dkr, c, s, MLA_ROPE_HALF, lanes)
        dvb = dv_ref[...].astype(BF16)

        cq = lat_ref[:, 0:Q_RANK]
        r1 = lax.rsqrt(jnp.mean(cq * cq, axis=-1, keepdims=True) + EPS)
        n1 = cq * r1
        dcqn = _nt(dqb_ref[...], wuq_ref[...])
        dn1 = dcqn * qg_ref[...]
        dcq = r1 * (dn1 - n1 * jnp.mean(dn1 * n1, axis=-1, keepdims=True))
        pq = jnp.sum(dcqn * n1, axis=0, keepdims=True)

        ckv = lat_ref[:, Q_RANK:Q_RANK + KV_RANK]
        r2 = lax.rsqrt(jnp.mean(ckv * ckv, axis=-1, keepdims=True) + EPS)
        n2 = ckv * r2
        dckvn = _nt(dkb_ref[...], wk_ref[...]) + _nt(dvb, wv_ref[...])
        dn2 = dckvn * kvg_ref[...]
        dckv = r2 * (dn2 - n2 * jnp.mean(dn2 * n2, axis=-1, keepdims=True))
        pkv = jnp.sum(dckvn * n2, axis=0, keepdims=True)
        cqn = (n1 * qg_ref[...]).astype(BF16)
        ckvn = (n2 * kvg_ref[...]).astype(BF16)
        wq, wk_, wv_ = _tn(cqn, dqb_ref[...]), _tn(ckvn, dkb_ref[...]), _tn(ckvn, dvb)

        dp_ref[:, 0:Q_RANK] = dcq.astype(BF16)
        dp_ref[:, Q_RANK:Q_RANK + KV_RANK] = dckv.astype(BF16)
        dp_ref[:, Q_RANK + KV_RANK:N_LAT] = dkr.astype(BF16)

        @pl.when(pl.program_id(0) == 0)
        def _():
            dgq_ref[...] = pq
            dgkv_ref[...] = pkv
            dwuq_ref[...] = wq
            dwk_ref[...] = wk_
            dwv_ref[...] = wv_

        @pl.when(pl.program_id(0) > 0)
        def _():
            dgq_ref[...] += pq
            dgkv_ref[...] += pkv
            dwuq_ref[...] += wq
            dwk_ref[...] += wk_
            dwv_ref[...] += wv_

    def full(shape):
        return pl.BlockSpec(shape, lambda i: (0, 0))

    def rows(w):
        return pl.BlockSpec((tm, w), lambda i: (i, 0))

    lat = pl.BlockSpec((tm, N_LAT), lambda i: (i, 0))
    dlat = pl.BlockSpec((tm, N_LAT), lambda i: (i, COL_LAT // N_LAT))
    return pl.pallas_call(
        body, name="mla_prep_bwd", grid=(SEQ // tm,),
        in_specs=[pl.BlockSpec(memory_space=pl.ANY), lat, rows(1024), rows(1024), rows(512),
                  full((1, Q_RANK)), full((1, KV_RANK)), full((Q_RANK, 1024)), full((KV_RANK, 1024)),
                  full((KV_RANK, 512)), rows(128), rows(128)],
        out_specs=[dlat, full((Q_RANK, 1024)), full((KV_RANK, 1024)), full((KV_RANK, 512)),
                   full((1, Q_RANK)), full((1, KV_RANK))],
        out_shape=[jax.ShapeDtypeStruct((SEQ, N_PAD), BF16), jax.ShapeDtypeStruct((Q_RANK, 1024), F32),
                   jax.ShapeDtypeStruct((KV_RANK, 1024), F32), jax.ShapeDtypeStruct((KV_RANK, 512), F32),
                   jax.ShapeDtypeStruct((1, Q_RANK), F32), jax.ShapeDtypeStruct((1, KV_RANK), F32)],
        input_output_aliases={0: 0},
        scratch_shapes=[pltpu.VMEM((tm, 1024), BF16), pltpu.VMEM((tm, 1024), BF16)],
        compiler_params=_cparams(),
    )(dp_in, p, dq, dk, dv, qg, kvg, wuq, wk, wv, rc, rs)


FLASH_T = 1024


def _head_half(shape, hh):
    lane = lax.broadcasted_iota(jnp.int32, shape, 1)
    return (lane < 64) if hh == 0 else (lane >= 64)


def _diag_keep(nr, nk):
    row = lax.broadcasted_iota(jnp.int32, (nr, nk), 0)
    col = lax.broadcasted_iota(jnp.int32, (nr, nk), 1)
    return row + (nk - nr) >= col


def _tri_steps(nb, q_major):
    if q_major:
        pairs = [(i, kb) for i in range(nb) for kb in range(i + 1)]
    else:
        pairs = [(i, kb) for kb in range(nb) for i in range(kb, nb)]
    return jnp.asarray([p[0] for p in pairs], jnp.int32), jnp.asarray([p[1] for p in pairs], jnp.int32)


def _mla_flash_fwd(q, k, v):
    t = FLASH_T
    nb = SEQ // t
    qtab, ktab = _tri_steps(nb, True)

    def body(qi_ref, ki_ref, q_ref, k_ref, v_ref, o_ref, lse_ref, m_scr, l_scr, acc_scr):
        step = pl.program_id(1)
        i, kb = qi_ref[step], ki_ref[step]

        @pl.when(kb == 0)
        def _():
            m_scr[...] = jnp.full_like(m_scr, NEG)
            l_scr[...] = jnp.zeros_like(l_scr)
            acc_scr[...] = jnp.zeros_like(acc_scr)

        def update(r0, nr, nk, diagonal):
            rs = slice(r0, r0 + nr)
            vv = v_ref[0:nk, :]
            for hh in range(2):
                sl = slice(hh * 128, (hh + 1) * 128)
                s = _nt(q_ref[rs, sl], k_ref[0:nk, sl])
                if diagonal:
                    s = jnp.where(_diag_keep(nr, nk), s, NEG)
                m_prev = m_scr[hh, rs, :]
                m_new = jnp.maximum(m_prev, jnp.max(s, axis=-1, keepdims=True))
                pr = jnp.exp(s - jnp.tile(m_new, (1, nk // 128)))
                alpha = jnp.exp(m_prev - m_new)
                l_scr[hh, rs, :] = alpha * l_scr[hh, rs, :] + jnp.sum(pr, axis=-1, keepdims=True)
                acc_scr[hh, rs, :] = alpha * acc_scr[hh, rs, :] + _nn(pr.astype(BF16), vv)
                m_scr[hh, rs, :] = m_new

        @pl.when(kb < i)
        def _():
            update(0, t, t, False)

        @pl.when(kb == i)
        def _():
            update(0, t // 2, t // 2, True)
            update(t // 2, t // 2, t, True)
            o0 = acc_scr[0] / l_scr[0]
            o1 = acc_scr[1] / l_scr[1]
            o_ref[...] = jnp.where(_head_half((t, 128), 0), o0, o1)
            for hh in range(2):
                lse_ref[:, hh * 128:(hh + 1) * 128] = m_scr[hh] + jnp.log(l_scr[hh])

    grid_spec = pltpu.PrefetchScalarGridSpec(
        num_scalar_prefetch=2, grid=(4, qtab.shape[0]),
        in_specs=[pl.BlockSpec((t, 256), lambda j, s, qi, ki: (qi[s], j)),
                  pl.BlockSpec((t, 256), lambda j, s, qi, ki: (ki[s], j)),
                  pl.BlockSpec((t, 128), lambda j, s, qi, ki: (ki[s], j))],
        out_specs=[pl.BlockSpec((t, 128), lambda j, s, qi, ki: (qi[s], j)),
                   pl.BlockSpec((t, 256), lambda j, s, qi, ki: (qi[s], j))],
        scratch_shapes=[pltpu.VMEM((2, t, 128), F32), pltpu.VMEM((2, t, 128), F32), pltpu.VMEM((2, t, 128), F32)])
    return pl.pallas_call(
        body, name="mla_flash_fwd", grid_spec=grid_spec,
        out_shape=[jax.ShapeDtypeStruct((SEQ, 512), F32), jax.ShapeDtypeStruct((SEQ, 1024), F32)],
        compiler_params=_cparams(),
    )(qtab, ktab, q, k, v)


def _mla_flash_bwd(q, k, v, o, do, lse, token=None):
    t = FLASH_T
    nb = SEQ // t
    qtab, ktab = _tri_steps(nb, False)
    after, after_specs = _after(token)

    def body(qi_ref, ki_ref, q_ref, k_ref, v_ref, o_ref, do_ref, lse_ref, *rest):
        dq_ref, dk_ref, dv_ref, dk_scr, dv_scr = rest[-5:]
        step = pl.program_id(1)
        i, kb = qi_ref[step], ki_ref[step]

        @pl.when(step == 0)
        def _():
            dq_ref[...] = jnp.zeros_like(dq_ref)

        @pl.when(i == kb)
        def _():
            dk_scr[...] = jnp.zeros_like(dk_scr)
            dv_scr[...] = jnp.zeros_like(dv_scr)

        def update(r0, nr, nk, diagonal):
            rs = slice(r0, r0 + nr)
            vv = v_ref[0:nk, :]
            ov = o_ref[rs, :]
            dov = do_ref[rs, :]
            rows = pl.ds(pl.multiple_of(i * t + r0, t // 2), nr)
            for hh in range(2):
                sl = slice(hh * 128, (hh + 1) * 128)
                qh, kh = q_ref[rs, sl], k_ref[0:nk, sl]
                s = _nt(qh, kh)
                if diagonal:
                    s = jnp.where(_diag_keep(nr, nk), s, NEG)
                pr = jnp.exp(s - jnp.tile(lse_ref[rs, sl], (1, nk // 128)))
                dom = jnp.where(_head_half((nr, 128), hh), dov, 0.0)
                domb = dom.astype(BF16)
                dv_scr[0:nk, :] += _tn(pr.astype(BF16), domb)
                dpr = _nt(domb, vv)
                delta = jnp.sum(dom * ov, axis=-1, keepdims=True)
                ds = (pr * (dpr - delta)).astype(BF16)
                dq_ref[rows, sl] += _nn(ds, kh)
                dk_scr[hh, 0:nk, :] += _tn(ds, qh)

        @pl.when(i > kb)
        def _():
            update(0, t, t, False)

        @pl.when(i == kb)
        def _():
            update(0, t // 2, t // 2, True)
            update(t // 2, t // 2, t, True)

        @pl.when(i == nb - 1)
        def _():
            dk_ref[:, 0:128] = dk_scr[0]
            dk_ref[:, 128:256] = dk_scr[1]
            dv_ref[...] = dv_scr[...]

    qi_map = lambda j, s, qi, ki: (qi[s], j)
    ki_map = lambda j, s, qi, ki: (ki[s], j)
    grid_spec = pltpu.PrefetchScalarGridSpec(
        num_scalar_prefetch=2, grid=(4, qtab.shape[0]),
        in_specs=[pl.BlockSpec((t, 256), qi_map), pl.BlockSpec((t, 256), ki_map), pl.BlockSpec((t, 128), ki_map),
                  pl.BlockSpec((t, 128), qi_map), pl.BlockSpec((t, 128), qi_map), pl.BlockSpec((t, 256), qi_map)]
        + after_specs,
        out_specs=[pl.BlockSpec((SEQ, 256), lambda j, s, qi, ki: (0, j)), pl.BlockSpec((t, 256), ki_map),
                   pl.BlockSpec((t, 128), ki_map)],
        scratch_shapes=[pltpu.VMEM((2, t, 128), F32), pltpu.VMEM((t, 128), F32)])
    return pl.pallas_call(
        body, name="mla_flash_bwd", grid_spec=grid_spec,
        out_shape=[jax.ShapeDtypeStruct((SEQ, 1024), F32), jax.ShapeDtypeStruct((SEQ, 1024), F32),
                   jax.ShapeDtypeStruct((SEQ, 512), F32)],
        compiler_params=_cparams(),
    )(qtab, ktab, q, k, v, o, do, lse, *after)


def _strided(start, size, d):
    return pl.ds(start, size) if d == 1 else pl.ds(start, size, stride=d)


def _dil_prep_fwd(p, rc, rs, g):
    d = DIL_DILATIONS[g]
    sub_len = SEQ // d
    ch = min(sub_len, 512)

    def body(p_ref, c_ref, s_ref, o_ref, x_scr):
        tq = pl.program_id(0)
        lanes = _rope_lanes((ch, 128), DIL_ROPE_HALF, 64, 0)
        o_ref[0, 0, 0:BAND, :] = jnp.zeros((BAND, 128), BF16)

        @pl.when(tq < 2)
        def _():
            mult = jnp.where(tq == 0, DIL_SCALE, 1.0).astype(F32)
            for c0 in range(0, SEQ, ch):
                rows = pl.ds(c0, ch)
                x_scr[rows, :] = _rope_fwd(p_ref[rows, :], c_ref[rows, :] * mult, s_ref[rows, :] * mult, DIL_ROPE_HALF, lanes)

        def gather(take):
            for r in range(d):
                for c0 in range(0, sub_len, ch):
                    at = BAND + r * sub_len + c0
                    o_ref[0, 0, at:at + ch, :] = take(_strided(r + c0 * d, ch, d)).astype(BF16)

        @pl.when(tq < 2)
        def _():
            gather(lambda rows: x_scr[rows, :])

        @pl.when(tq == 2)
        def _():
            gather(lambda rows: p_ref[rows, :])

    tab = pl.BlockSpec((SEQ, 128), lambda tq, pr: (0, 0))
    return pl.pallas_call(
        body, name=f"dil_prep_fwd_g{g}", grid=(3, 4),
        in_specs=[pl.BlockSpec((None, SEQ, 128), lambda tq, pr: (_qkv_block(tq, g, pr) - COL_QKV // 128, 0, 0)), tab, tab],
        out_specs=pl.BlockSpec((1, 1, BAND + SEQ, 128), lambda tq, pr: (tq, pr, 0, 0)),
        out_shape=jax.ShapeDtypeStruct((3, 4, BAND + SEQ, 128), BF16),
        scratch_shapes=[pltpu.VMEM((SEQ, 128), F32)],
        compiler_params=_cparams(),
    )(p, rc, rs)


DIL_ST_FWD, DIL_ST_BWD = 1024, 2048


def _band_keep(g, b, t, nb):
    nbs = SEQ // DIL_DILATIONS[g] // BAND
    row = lax.broadcasted_iota(jnp.int32, (BAND, 2 * BAND), 0)
    col = lax.broadcasted_iota(jnp.int32, (BAND, 2 * BAND), 1)
    cur = (col >= BAND) & (row >= col - BAND)
    prev = (col < BAND) & (col >= row)
    if nbs >= nb:
        if b > 0:
            return cur | prev
        return cur | (prev & ((t * nb) % nbs != 0))
    return cur | prev if b % nbs else cur


def _dil_tok(g, b, t, nb):
    d = DIL_DILATIONS[g]
    nbs = SEQ // d // BAND
    gb = t * nb + b
    return _strided((gb % nbs) * BAND * d + gb // nbs, BAND, d)


def _dil_attn_fwd(qkv, g):
    DIL_ST, DIL_NB = DIL_ST_FWD, DIL_ST_FWD // BAND

    def body(q_ref, k_ref, v_ref, o_ref, l_ref, s_scr, p_scr, o_scr):
        t = pl.program_id(1)
        base = t * DIL_ST
        half0 = _head_half((DIL_ST, 128), 0)
        lse_h = []
        for hh in range(2):
            half = _head_half((BAND, 128), hh)
            for b in range(DIL_NB):
                qv = q_ref[0, 0, pl.ds(pl.multiple_of(base + (b + 1) * BAND, BAND), BAND), :]
                k2 = k_ref[0, 0, pl.ds(pl.multiple_of(base + b * BAND, BAND), 2 * BAND), :]
                sb = _nt(jnp.where(half, qv, jnp.zeros_like(qv)), k2)
                s_scr[b * BAND:(b + 1) * BAND, :] = jnp.where(_band_keep(g, b, t, DIL_NB), sb, NEG)
            s = s_scr[...]
            m = jnp.max(s, axis=-1, keepdims=True)
            pr = jnp.exp(s - m)
            den = jnp.sum(pr, axis=-1, keepdims=True)
            p_scr[...] = pr.astype(BF16)
            for b in range(DIL_NB):
                v2 = v_ref[0, 0, pl.ds(pl.multiple_of(base + b * BAND, BAND), 2 * BAND), :]
                o_scr[hh, b * BAND:(b + 1) * BAND, :] = _nn(p_scr[b * BAND:(b + 1) * BAND, :], v2)
            o_scr[hh] = o_scr[hh] / den
            lse_h.append(m + jnp.log(den))
        out = jnp.where(half0, o_scr[0], o_scr[1])
        lse = jnp.where(half0, lse_h[0], lse_h[1])
        for b in range(DIL_NB):
            tok = _dil_tok(g, b, t, DIL_NB)
            o_ref[tok, :] = out[b * BAND:(b + 1) * BAND, :]
            l_ref[tok, :] = lse[b * BAND:(b + 1) * BAND, :]

    def inp(tq):
        return pl.BlockSpec((1, 1, BAND + SEQ, 128), lambda pr, t: (tq, pr, 0, 0))

    out = pl.BlockSpec((SEQ, 128), lambda pr, t: (0, pr))
    return pl.pallas_call(
        body, name=f"dil_attn_fwd_g{g}", grid=(4, SEQ // DIL_ST),
        in_specs=[inp(0), inp(1), inp(2)], out_specs=[out, out],
        out_shape=[jax.ShapeDtypeStruct((SEQ, 512), F32), jax.ShapeDtypeStruct((SEQ, 512), F32)],
        scratch_shapes=[pltpu.VMEM((DIL_ST, 2 * BAND), F32), pltpu.VMEM((DIL_ST, 2 * BAND), BF16),
                        pltpu.VMEM((2, DIL_ST, 128), F32)],
        compiler_params=_cparams(),
    )(qkv, qkv, qkv)


def _dil_attn_bwd(dp_in, qkv, dyd, yd, lse_all, rc, rs, g, token=None):
    d = DIL_DILATIONS[g]
    sub_len = SEQ // d
    DIL_ST, DIL_NB = DIL_ST_BWD, DIL_ST_BWD // BAND
    nst = SEQ // DIL_ST
    after, after_specs = _after(token)
    ch = 512

    def body(dp_any, q_ref, k_ref, v_ref, do_ref, y_ref, l_ref, c_ref, sn_ref, *rest):
        dp_ref, tok_scr, dk_scr, dv_scr, s_scr, dp_scr, p_scr, ds_scr, do_scr, y_scr, l_scr, dq_scr = rest[-12:]
        del dp_any
        t = pl.program_id(1)
        base = t * DIL_ST

        @pl.when(t == 0)
        def _():
            dk_scr[...] = jnp.zeros_like(dk_scr)
            dv_scr[...] = jnp.zeros_like(dv_scr)

        for b in range(DIL_NB):
            tok = _dil_tok(g, b, t, DIL_NB)
            do_scr[b * BAND:(b + 1) * BAND, :] = do_ref[tok, :]
            y_scr[b * BAND:(b + 1) * BAND, :] = y_ref[tok, :]
            l_scr[b * BAND:(b + 1) * BAND, :] = l_ref[tok, :]
        for hh in range(2):
            half = _head_half((BAND, 128), hh)
            half_st = _head_half((DIL_ST, 128), hh)
            dom = jnp.where(half_st, do_scr[...], 0.0)
            delta = jnp.sum(dom * y_scr[...], axis=-1, keepdims=True)
            lcol = jnp.max(jnp.where(half_st, l_scr[...], NEG), axis=-1, keepdims=True)
            for b in range(DIL_NB):
                rows = slice(b * BAND, (b + 1) * BAND)
                qv = q_ref[0, 0, pl.ds(pl.multiple_of(base + (b + 1) * BAND, BAND), BAND), :]
                band = pl.ds(pl.multiple_of(base + b * BAND, BAND), 2 * BAND)
                sb = _nt(jnp.where(half, qv, jnp.zeros_like(qv)), k_ref[0, 0, band, :])
                s_scr[rows, :] = jnp.where(_band_keep(g, b, t, DIL_NB), sb, NEG)
                dp_scr[rows, :] = _nt(dom[rows, :].astype(BF16), v_ref[0, 0, band, :])
            pr = jnp.exp(s_scr[...] - lcol)
            p_scr[...] = pr.astype(BF16)
            ds_scr[...] = (pr * (dp_scr[...] - delta)).astype(BF16)
            for b in range(DIL_NB):
                rows = slice(b * BAND, (b + 1) * BAND)
                qv = q_ref[0, 0, pl.ds(pl.multiple_of(base + (b + 1) * BAND, BAND), BAND), :]
                band = pl.ds(pl.multiple_of(base + b * BAND, BAND), 2 * BAND)
                dqb = jnp.where(half, _nn(ds_scr[rows, :], k_ref[0, 0, band, :]), 0.0)
                if hh == 0:
                    dq_scr[rows, :] = dqb
                else:
                    dq_scr[rows, :] += dqb
                half2 = _head_half((2 * BAND, 128), hh)
                dk_scr[band, :] += jnp.where(half2, _tn(ds_scr[rows, :], qv), 0.0)
                dv_scr[band, :] += _tn(p_scr[rows, :], dom[rows, :].astype(BF16))
        for b in range(DIL_NB):
            tok_scr[pl.ds(0, 1), _dil_tok(g, b, t, DIL_NB), :] = dq_scr[b * BAND:(b + 1) * BAND, :][None]

        @pl.when(t == nst - 1)
        def _():
            for r in range(d):
                rows = _strided(r, sub_len, d)
                tok_scr[pl.ds(1, 1), rows, :] = dk_scr[BAND + r * sub_len:BAND + (r + 1) * sub_len, :][None]
                tok_scr[pl.ds(2, 1), rows, :] = dv_scr[BAND + r * sub_len:BAND + (r + 1) * sub_len, :][None]
            lanes = _rope_lanes((ch, 128), DIL_ROPE_HALF, 64, 0)
            for c0 in range(0, SEQ, ch):
                rows = slice(c0, c0 + ch)
                cv, sv = c_ref[rows, :], sn_ref[rows, :]
                dp_ref[rows, 0:128] = _rope_bwd(tok_scr[0, rows, :], cv * DIL_SCALE, sv * DIL_SCALE, DIL_ROPE_HALF, lanes).astype(BF16)
                dp_ref[rows, 128:256] = _rope_bwd(tok_scr[1, rows, :], cv, sv, DIL_ROPE_HALF, lanes).astype(BF16)
                dp_ref[rows, 256:384] = tok_scr[2, rows, :].astype(BF16)

    def inp(tq):
        return pl.BlockSpec((1, 1, BAND + SEQ, 128), lambda pr, t: (tq, pr, 0, 0))

    tok_spec = pl.BlockSpec((SEQ, 128), lambda pr, t: (0, pr))
    tab = pl.BlockSpec((SEQ, 128), lambda pr, t: (0, 0))
    st = (DIL_ST, 2 * BAND)
    return pl.pallas_call(
        body, name=f"dil_attn_bwd_g{g}", grid=(4, nst),
        in_specs=[pl.BlockSpec(memory_space=pl.ANY), inp(0), inp(1), inp(2), tok_spec, tok_spec, tok_spec, tab, tab]
        + after_specs,
        out_specs=pl.BlockSpec((SEQ, 384), lambda pr, t: (0, _qkv_block(0, g, pr) // 3)),
        out_shape=jax.ShapeDtypeStruct((SEQ, N_PAD), BF16),
        input_output_aliases={0: 0},
        scratch_shapes=[pltpu.VMEM((3, SEQ, 128), F32),
                        pltpu.VMEM((BAND + SEQ, 128), F32), pltpu.VMEM((BAND + SEQ, 128), F32),
                        pltpu.VMEM(st, F32), pltpu.VMEM(st, F32), pltpu.VMEM(st, BF16), pltpu.VMEM(st, BF16),
                        pltpu.VMEM((DIL_ST, 128), F32), pltpu.VMEM((DIL_ST, 128), F32), pltpu.VMEM((DIL_ST, 128), F32),
                        pltpu.VMEM((DIL_ST, 128), F32)],
        compiler_params=_cparams(),
    )(dp_in, qkv, qkv, qkv, dyd, yd, lse_all, rc, rs, *after)


TAIL_T = 256


def _tail(p, ya, o_g, l_g, x, target, wpm, wpd, wout, post_g):
    tm = TAIL_T

    def body(pgz_ref, ya_ref, o0_ref, o1_ref, o2_ref, l0_ref, l1_ref, l2_ref, x_ref, t_ref,
             wpm_ref, wpd_ref, wout_ref, pg_ref,
             dp_ref, dy_ref, dya_ref, dyd_ref, yd_ref, lse_ref, loss_ref, dgp_ref, dwpm_ref, dwpd_ref, dwout_ref):
        l0, l1, l2 = l0_ref[...], l1_ref[...], l2_ref[...]
        mx = jnp.maximum(jnp.maximum(l0, l1), l2)
        e0, e1, e2 = jnp.exp(l0 - mx), jnp.exp(l1 - mx), jnp.exp(l2 - mx)
        den = e0 + e1 + e2
        yd = (e0 * o0_ref[...] + e1 * o1_ref[...] + e2 * o2_ref[...]) / den
        yd_ref[...] = yd
        lse_ref[...] = mx + jnp.log(den)
        ya = ya_ref[...]

        gm, gd = pgz_ref[:, 0:1024], pgz_ref[:, 1024:2048]
        zm, zd = pgz_ref[:, 2048:2560], pgz_ref[:, 2560:3072]
        szm, szd = _sigmoid(zm), _sigmoid(zd)
        sm, sd = zm * szm, zd * szd
        ua = (ya * sm).astype(BF16)
        ud = (yd * sd).astype(BF16)
        pa = _nn(ua, wpm_ref[...])
        pd = _nn(ud, wpd_ref[...])
        sgm, sgd = _sigmoid(gm), _sigmoid(gd)
        mg = (sgm * pa + sgd * pd).astype(BF16)
        t = _nn(mg, wout_ref[...])
        r3 = lax.rsqrt(jnp.mean(t * t, axis=-1, keepdims=True) + EPS)
        n = t * r3
        pg = pg_ref[...]
        err = x_ref[...] + n * pg - t_ref[...]
        lpart = jnp.sum(err * err, axis=0, keepdims=True)

        dy = err * (1.0 / D_MODEL)
        dy_ref[...] = dy
        gpart = jnp.sum(dy * n, axis=0, keepdims=True)
        dn = dy * pg
        dt = (r3 * (dn - n * jnp.mean(dn * n, axis=-1, keepdims=True))).astype(BF16)
        dmg = _nt(dt, wout_ref[...])
        dpa = (dmg * sgm).astype(BF16)
        dpd = (dmg * sgd).astype(BF16)
        dp_ref[:, 0:1024] = (dmg * pa * sgm * (1.0 - sgm)).astype(BF16)
        dp_ref[:, 1024:2048] = (dmg * pd * sgd * (1.0 - sgd)).astype(BF16)
        dua = _nt(dpa, wpm_ref[...])
        dud = _nt(dpd, wpd_ref[...])
        dya_ref[...] = dua * sm
        dyd_ref[...] = dud * sd
        dp_ref[:, 2048:2560] = (dua * ya * szm * (1.0 + zm * (1.0 - szm))).astype(BF16)
        dp_ref[:, 2560:3072] = (dud * yd * szd * (1.0 + zd * (1.0 - szd))).astype(BF16)

        wpm, wpd, wout = _tn(ua, dpa), _tn(ud, dpd), _tn(mg, dt)

        @pl.when(pl.program_id(0) == 0)
        def _():
            loss_ref[...] = lpart
            dgp_ref[...] = gpart
            dwpm_ref[...] = wpm
            dwpd_ref[...] = wpd
            dwout_ref[...] = wout

        @pl.when(pl.program_id(0) > 0)
        def _():
            loss_ref[...] += lpart
            dgp_ref[...] += gpart
            dwpm_ref[...] += wpm
            dwpd_ref[...] += wpd
            dwout_ref[...] += wout

    def rows(w):
        return pl.BlockSpec((tm, w), lambda i: (i, 0))

    def full(shape):
        return pl.BlockSpec(shape, lambda i: (0, 0))

    def sds(w, dt):
        return jax.ShapeDtypeStruct((SEQ, w), dt)

    return pl.pallas_call(
        body, name="tail", grid=(SEQ // tm,),
        in_specs=[rows(3072), rows(512), rows(512), rows(512), rows(512), rows(512), rows(512), rows(512),
                  rows(1024), rows(1024), full((512, 1024)), full((512, 1024)), full((1024, 1024)), full((1, 1024))],
        out_specs=[rows(3072), rows(1024), rows(512), rows(512), rows(512), rows(512), full((1, 1024)), full((1, 1024)),
                   full((512, 1024)), full((512, 1024)), full((1024, 1024))],
        out_shape=[sds(N_PAD, BF16), sds(1024, F32), sds(512, F32), sds(512, F32), sds(512, F32), sds(512, F32),
                   jax.ShapeDtypeStruct((1, 1024), F32), jax.ShapeDtypeStruct((1, 1024), F32),
                   jax.ShapeDtypeStruct((512, 1024), F32), jax.ShapeDtypeStruct((512, 1024), F32),
                   jax.ShapeDtypeStruct((1024, 1024), F32)],
        compiler_params=_cparams(),
    )(p, ya, o_g[0], o_g[1], o_g[2], l_g[0], l_g[1], l_g[2], x, target, wpm, wpd, wout, post_g)


def _sum_parts(recv, own, me, tr, name):
    n, r, w = recv.shape
    if r % tr:
        return _sum_parts_cols(recv, own, me, name)
    own_spec = (pl.BlockSpec((tr, w), lambda i, me_ref: (i, 0)) if own.ndim == 2
                else pl.BlockSpec((None, tr, w), lambda i, me_ref: (me_ref[0], i, 0)))

    def body(me_ref, p_ref, own_ref, o_ref):
        mine = own_ref[...].astype(F32)
        acc = jnp.zeros((tr, w), F32)
        for s in range(n):
            acc = acc + jnp.where(me_ref[0] == s, mine, p_ref[s].astype(F32))
        o_ref[...] = acc

    return pl.pallas_call(
        body, name=name,
        grid_spec=pltpu.PrefetchScalarGridSpec(
            num_scalar_prefetch=1, grid=(r // tr,),
            in_specs=[pl.BlockSpec((n, tr, w), lambda i, me_ref: (0, i, 0)), own_spec],
            out_specs=pl.BlockSpec((tr, w), lambda i, me_ref: (i, 0))),
        out_shape=jax.ShapeDtypeStruct((r, w), F32),
    )(me.reshape(1), recv, own)


def _sum_parts_cols(recv, own, me, name):
    n, r, w = recv.shape
    tc = 128

    def body(me_ref, p_ref, own_ref, o_ref):
        mine = own_ref[...].astype(F32)
        acc = jnp.zeros((r, tc), F32)
        for s in range(n):
            acc = acc + jnp.where(me_ref[0] == s, mine, p_ref[s].astype(F32))
        o_ref[...] = acc

    return pl.pallas_call(
        body, name=name,
        grid_spec=pltpu.PrefetchScalarGridSpec(
            num_scalar_prefetch=1, grid=(w // tc,),
            in_specs=[pl.BlockSpec((n, r, tc), lambda i, me_ref: (0, 0, i)),
                      pl.BlockSpec((None, r, tc), lambda i, me_ref: (me_ref[0], 0, i))],
            out_specs=pl.BlockSpec((r, tc), lambda i, me_ref: (0, i))),
        out_shape=jax.ShapeDtypeStruct((r, w), F32),
    )(me.reshape(1), recv, own)


def _adamw(w, g, m, v, name):
    lead = w.shape[:-2]
    r, c = w.shape[-2:]
    tr = max([t for t in range(8, 257, 8) if r % t == 0], default=r)
    c1 = 1.0 - ADAM_B1 ** ADAM_STEP
    c2 = 1.0 - ADAM_B2 ** ADAM_STEP

    def body(w_ref, g_ref, m_ref, v_ref, d_ref, nm_ref, nv_ref):
        gv = g_ref[...]
        nm = ADAM_B1 * m_ref[...] + (1.0 - ADAM_B1) * gv
        nv = ADAM_B2 * v_ref[...] + (1.0 - ADAM_B2) * (gv * gv)
        nm_ref[...] = nm
        nv_ref[...] = nv
        d_ref[...] = -ADAM_LR * ((nm / c1) / (jnp.sqrt(nv / c2) + ADAM_EPS) + ADAM_WD * w_ref[...])

    zeros = (0,) * len(lead)
    spec = pl.BlockSpec((1,) * len(lead) + (tr, c), lambda i: zeros + (i, 0))
    sd = jax.ShapeDtypeStruct(w.shape, F32)
    return pl.pallas_call(
        body, name=name, grid=(r // tr,),
        in_specs=[spec] * 4, out_specs=[spec] * 3, out_shape=[sd] * 3,
    )(w, g, m, v)


def _adamw_recv(w, m, v, recv, own, me, name):
    n, r, c = recv.shape
    tr = 128
    c1 = 1.0 - ADAM_B1 ** ADAM_STEP
    c2 = 1.0 - ADAM_B2 ** ADAM_STEP

    def body(me_ref, w_ref, m_ref, v_ref, p_ref, own_ref, d_ref, nm_ref, nv_ref, g_ref):
        mine = own_ref[...].astype(F32)
        gv = jnp.zeros((tr, c), F32)
        for s in range(n):
            gv = gv + jnp.where(me_ref[0] == s, mine, p_ref[s].astype(F32))
        g_ref[0] = gv
        nm = ADAM_B1 * m_ref[0] + (1.0 - ADAM_B1) * gv
        nv = ADAM_B2 * v_ref[0] + (1.0 - ADAM_B2) * (gv * gv)
        nm_ref[0] = nm
        nv_ref[0] = nv
        d_ref[0] = -ADAM_LR * ((nm / c1) / (jnp.sqrt(nv / c2) + ADAM_EPS) + ADAM_WD * w_ref[0])

    full = pl.BlockSpec((1, tr, c), lambda i, me_ref: (0, i, 0))
    sd = jax.ShapeDtypeStruct((1, r, c), F32)
    return pl.pallas_call(
        body, name=name,
        grid_spec=pltpu.PrefetchScalarGridSpec(
            num_scalar_prefetch=1, grid=(r // tr,),
            in_specs=[full, full, full, pl.BlockSpec((n, tr, c), lambda i, me_ref: (0, i, 0)),
                      pl.BlockSpec((None, tr, c), lambda i, me_ref: (me_ref[0], i, 0))],
            out_specs=[full] * 4),
        out_shape=[sd] * 4,
    )(me.reshape(1), w, m, v, recv, own)


def _adamw_in(w_t, m_t, v_t, own_half, swapped, core):
    r, c = SHARD_SHAPES[0]
    tr = max(t for t in range(8, 257, 8) if r % t == 0)
    c1 = 1.0 - ADAM_B1 ** ADAM_STEP
    c2 = 1.0 - ADAM_B2 ** ADAM_STEP

    def body(core_ref, w_ref, m_ref, v_ref, own_ref, sw_ref, d_ref, nm_ref, nv_ref, g_ref):
        own = own_ref[...]
        col_half = lax.broadcasted_iota(jnp.int32, (tr, c), 1) // (c // 2)
        gv = jnp.where(col_half == core_ref[0], jnp.concatenate([own, own], axis=1), sw_ref[...])
        g_ref[0] = gv
        nm = ADAM_B1 * m_ref[0] + (1.0 - ADAM_B1) * gv
        nv = ADAM_B2 * v_ref[0] + (1.0 - ADAM_B2) * (gv * gv)
        nm_ref[0] = nm
        nv_ref[0] = nv
        d_ref[0] = -ADAM_LR * ((nm / c1) / (jnp.sqrt(nv / c2) + ADAM_EPS) + ADAM_WD * w_ref[0])

    full = pl.BlockSpec((1, tr, c), lambda i, core_ref: (0, i, 0))
    sd = jax.ShapeDtypeStruct((1, r, c), F32)
    return pl.pallas_call(
        body, name="adamw_in",
        grid_spec=pltpu.PrefetchScalarGridSpec(
            num_scalar_prefetch=1, grid=(r // tr,),
            in_specs=[full, full, full, pl.BlockSpec((tr, c // 2), lambda i, core_ref: (i, 0)),
                      pl.BlockSpec((tr, c), lambda i, core_ref: (i, 0))],
            out_specs=[full] * 4),
        out_shape=[sd] * 4,
    )(core.reshape(1), w_t, m_t, v_t, own_half, swapped)


ANY = pl.BlockSpec(memory_space=pl.ANY)


def _my_place():
    return lax.axis_index("x"), lax.axis_index("y"), lax.axis_index("c")


HBM = pl.BlockSpec(memory_space=pltpu.HBM)
SEM = pl.BlockSpec(memory_space=pltpu.SEMAPHORE)
DATAFLOW = pltpu.SideEffectType.DATAFLOW_SIDE_EFFECTING


def _near_chips(x, y):
    return [(1 - x, y), (x, 1 - y)]


def _half(mi, hc):
    r, c = SHARD_SHAPES[mi]
    if mi == 0:
        return pl.ds(0, r), pl.ds(pl.multiple_of(hc * (c // 2), 128), c // 2)
    return pl.ds(pl.multiple_of(hc * (r // 2), 16), r // 2), pl.ds(0, c)


def _gather_copies(land_refs, send_sems, recv_sems):
    x, y, c = _my_place()
    out, back = [], []
    for mi in range(N_MATS):
        rows, cols = _half(mi, c)
        mine = land_refs[mi].at[2 * x + y, rows, cols]
        for j, (cx, cy) in enumerate(_near_chips(x, y)):
            sems = dict(send_sem=send_sems.at[mi * 2 + j], recv_sem=recv_sems.at[mi * 2 + j],
                        device_id=(cx, cy, c), device_id_type=MESH)
            out.append(pltpu.make_async_remote_copy(src_ref=mine, dst_ref=mine, **sems))
            got = land_refs[mi].at[2 * cx + cy, rows, cols]
            back.append(pltpu.make_async_remote_copy(src_ref=got, dst_ref=got, **sems))
    return out, back


def _gather_start(landing):
    n = N_MATS

    def body(*refs):
        out, _ = _gather_copies(refs[:n], refs[n], refs[n + 1])
        for cp in out:
            cp.start()
        refs[-1][...] = jnp.zeros_like(refs[-1])

    hbm = [pltpu.HBM(a.shape, a.dtype) for a in landing]
    outs = pl.pallas_call(
        body, name="gather_start",
        out_shape=(pltpu.SemaphoreType.DMA((2 * n,)), pltpu.SemaphoreType.DMA((2 * n,)), *hbm,
                   jax.ShapeDtypeStruct((8, 128), F32)),
        in_specs=[HBM] * n, out_specs=(SEM, SEM, *[HBM] * n, pl.BlockSpec(memory_space=pltpu.VMEM)),
        input_output_aliases={i: 2 + i for i in range(n)},
        compiler_params=pltpu.CompilerParams(has_side_effects=DATAFLOW),
    )(*[pltpu.with_memory_space_constraint(a, pltpu.HBM) for a in landing])
    return outs[:-1], outs[-1]


def _gather_wait(handle, after):
    n = N_MATS

    def body(*refs):
        out, back = _gather_copies(refs[:n], refs[n], refs[n + 1])
        for cp, arrival in zip(out, back):
            cp.wait_send()
            arrival.wait_recv()

    bufs = handle[2:]
    after, after_specs = _after(after)
    res = pl.pallas_call(
        body, name="gather_wait", out_shape=tuple(pltpu.HBM(b.shape, b.dtype) for b in bufs),
        in_specs=[HBM] * n + [SEM, SEM] + after_specs, out_specs=tuple([HBM] * n),
        input_output_aliases={i: i for i in range(n)},
        compiler_params=pltpu.CompilerParams(has_side_effects=DATAFLOW),
    )(*bufs, handle[0], handle[1], *after)
    return list(res)


def _relay_share(gathered):
    n = N_MATS

    def body(*refs):
        out_refs = refs[n:2 * n]
        send_sems, recv_sems = refs[2 * n:]
        x, y, c = _my_place()
        sibling = (x, y, 1 - c)
        relayed = 2 * (x ^ (1 - c)) + (y ^ c)
        relay_to = (x ^ c, y ^ (1 - c), c)
        far = 2 * (1 - x) + (1 - y)
        near = [2 * (1 - x) + y, 2 * x + (1 - y)]

        def copy(k, mi, shard, hc, to):
            blk = out_refs[mi].at[(shard,) + _half(mi, hc)]
            return pltpu.make_async_remote_copy(src_ref=blk, dst_ref=blk, send_sem=send_sems.at[mi * 4 + k],
                                                recv_sem=recv_sems.at[mi * 4 + k], device_id=to, device_id_type=MESH)

        sends = []
        for mi in range(n):
            sends.append(copy(0, mi, relayed, c, relay_to))
            sends += [copy(1 + j, mi, near[j], c, sibling) for j in range(2)]
        for cp in sends:
            cp.start()
        for mi in range(n):
            copy(0, mi, far, c, relay_to).wait_recv()
            cp = copy(3, mi, far, c, sibling)
            cp.start()
            sends.append(cp)
        for mi in range(n):
            for j in range(2):
                copy(1 + j, mi, near[j], 1 - c, sibling).wait_recv()
            copy(3, mi, far, 1 - c, sibling).wait_recv()
        for cp in sends:
            cp.wait_send()

    return pl.pallas_call(
        body, name="relay_share",
        in_specs=[ANY] * n, out_specs=[ANY] * n,
        out_shape=[jax.ShapeDtypeStruct(g.shape, g.dtype) for g in gathered],
        input_output_aliases={i: i for i in range(n)},
        scratch_shapes=[pltpu.SemaphoreType.DMA((4 * n,)), pltpu.SemaphoreType.DMA((4 * n,))],
    )(*gathered)


def _peers(x, y, c):
    out = []
    for k in range(1, 8):
        px, py, pc = x ^ (k >> 2), y ^ ((k >> 1) & 1), c ^ (k & 1)
        out.append((k - 1, (px, py, pc), 4 * px + 2 * py + pc))
    return out


def _exchange_start(parts, name):
    n = len(parts)

    def body(*refs):
        p_refs, land_refs = refs[:n], refs[n:2 * n]
        send_sems, recv_sems, token = refs[2 * n], refs[2 * n + 1], refs[-1]
        x, y, c = _my_place()
        me = 4 * x + 2 * y + c
        for k, dev, peer in _peers(x, y, c):
            for mi in range(n):
                pltpu.make_async_remote_copy(
                    src_ref=p_refs[mi].at[peer], dst_ref=land_refs[mi].at[me], send_sem=send_sems.at[k * n + mi],
                    recv_sem=recv_sems.at[k * n + mi], device_id=dev, device_id_type=MESH).start()
        token[...] = jnp.zeros_like(token)

    hbm = [pltpu.HBM(p.shape, p.dtype) for p in parts]
    outs = pl.pallas_call(
        body, name=name + "_start",
        out_shape=(pltpu.SemaphoreType.DMA((7 * n,)), pltpu.SemaphoreType.DMA((7 * n,)), *hbm, *hbm,
                   jax.ShapeDtypeStruct((8, 128), F32)),
        in_specs=[HBM] * (2 * n), out_specs=(SEM, SEM, *[HBM] * (2 * n), pl.BlockSpec(memory_space=pltpu.VMEM)),
        input_output_aliases={i: 2 + i for i in range(2 * n)},
        compiler_params=pltpu.CompilerParams(has_side_effects=DATAFLOW),
    )(*[pltpu.with_memory_space_constraint(p, pltpu.HBM) for p in parts],
      *[pltpu.with_memory_space_constraint(lax.empty(p.shape, p.dtype), pltpu.HBM) for p in parts])
    return (name, outs[:-1]), outs[-1]


def _exchange_wait(handle, after):
    name, outs = handle
    n = (len(outs) - 2) // 2

    def body(*refs):
        p_refs, land_refs = refs[:n], refs[n:2 * n]
        send_sems, recv_sems = refs[2 * n], refs[2 * n + 1]
        x, y, c = _my_place()
        me = 4 * x + 2 * y + c
        for k, dev, peer in _peers(x, y, c):
            for mi in range(n):
                pltpu.make_async_remote_copy(
                    src_ref=p_refs[mi].at[peer], dst_ref=land_refs[mi].at[me], send_sem=send_sems.at[k * n + mi],
                    recv_sem=recv_sems.at[k * n + mi], device_id=dev, device_id_type=MESH).wait_send()
                slot = land_refs[mi].at[peer]
                pltpu.make_async_remote_copy(
                    src_ref=slot, dst_ref=slot, send_sem=send_sems.at[k * n + mi],
                    recv_sem=recv_sems.at[k * n + mi], device_id=dev, device_id_type=MESH).wait_recv()

    bufs = outs[2:]
    res = pl.pallas_call(
        body, name=name + "_wait", out_shape=tuple(pltpu.HBM(b.shape, b.dtype) for b in bufs),
        in_specs=[HBM] * (2 * n) + [SEM, SEM, ANY], out_specs=tuple([HBM] * (2 * n)),
        input_output_aliases={i: i for i in range(2 * n)},
        compiler_params=pltpu.CompilerParams(has_side_effects=DATAFLOW),
    )(*bufs, outs[0], outs[1], after)
    return list(res[n:])


def _swap_halves(half_in, gvec):
    def body(g_ref, gv_ref, out_ref, rg_ref, send_sems, recv_sems):
        x, y, c = _my_place()
        me = 4 * x + 2 * y + c
        sibling = (x, y, 1 - c)

        def half(hc):
            return out_ref.at[:, pl.ds(pl.multiple_of(hc * 512, 128), 512)]

        sends = [pltpu.make_async_remote_copy(src_ref=g_ref, dst_ref=half(c), send_sem=send_sems.at[7],
                                              recv_sem=recv_sems.at[7], device_id=sibling, device_id_type=MESH)]
        for k, dev, peer in _peers(x, y, c):
            sends.append(pltpu.make_async_remote_copy(src_ref=gv_ref, dst_ref=rg_ref.at[me], send_sem=send_sems.at[k],
                                                      recv_sem=recv_sems.at[k], device_id=dev, device_id_type=MESH))
        for cp in sends:
            cp.start()
        got = half(1 - c)
        pltpu.make_async_remote_copy(src_ref=got, dst_ref=got, send_sem=send_sems.at[7], recv_sem=recv_sems.at[7],
                                     device_id=sibling, device_id_type=MESH).wait_recv()
        for k, dev, peer in _peers(x, y, c):
            got = rg_ref.at[peer]
            pltpu.make_async_remote_copy(src_ref=got, dst_ref=got, send_sem=send_sems.at[k], recv_sem=recv_sems.at[k],
                                         device_id=dev, device_id_type=MESH).wait_recv()
        for cp in sends:
            cp.wait_send()

    return pl.pallas_call(
        body, name="swap_halves",
        in_specs=[ANY, ANY], out_specs=[ANY, ANY],
        out_shape=[jax.ShapeDtypeStruct(SHARD_SHAPES[0], F32), jax.ShapeDtypeStruct((8, 8, N_GVEC), F32)],
        scratch_shapes=[pltpu.SemaphoreType.DMA((8,)), pltpu.SemaphoreType.DMA((8,))],
    )(half_in, gvec)


def _set_slot(arr, block, idx):
    return lax.dynamic_update_slice(arr, block[None], (idx,) + (0,) * block.ndim)


PAD_RUNS = ((6304, 8352, 0), (5280, 6304, COL_Z), (672, 5280, COL_QKV), (0, 640, COL_LAT), (640, 672, COL_LAT + 704))
N_QKV = COL_LAT - COL_QKV


def _qkv_rows_regroup(a, to_padded):
    if to_padded:
        a4 = a.reshape(3, 12, 128, a.shape[1])
        return jnp.stack([a4[0], a4[1], a4[2]], axis=1).reshape(a.shape)
    a4 = a.reshape(12, 3, 128, a.shape[1])
    return jnp.concatenate([a4[:, tq].reshape(N_QKV // 3, a.shape[1]) for tq in range(3)], axis=0)
W_IN_SHARD = 2088


def _full_weights(gathered):
    def cols(a):
        return jnp.concatenate([a[s] for s in range(4)], axis=1)

    w_uq, w_ukv, w_pm, w_pd = [cols(a) for a in gathered[1:5]]
    w_out = gathered[5].reshape(D_MODEL, D_MODEL)
    w_in_t = gathered[0].reshape(4 * W_IN_SHARD, D_MODEL)
    pieces, at = [], 0
    for lo, hi, pad_lo in sorted(PAD_RUNS, key=lambda t: t[2]):
        if pad_lo > at:
            pieces.append(jnp.zeros((pad_lo - at, D_MODEL), w_in_t.dtype))
        pieces.append(_qkv_rows_regroup(w_in_t[lo:hi], True) if pad_lo == COL_QKV else w_in_t[lo:hi])
        at = pad_lo + hi - lo
    pieces.append(jnp.zeros((N_PAD - at, D_MODEL), w_in_t.dtype))
    w_pad_t = jnp.concatenate(pieces, axis=0)
    z32 = jnp.zeros((Q_RANK, 32), w_uq.dtype)
    wuq_pad = jnp.concatenate([t for h in range(MLA_HEADS) for t in (w_uq[:, h * 96:(h + 1) * 96], z32)], axis=1)
    z64 = jnp.zeros((KV_RANK, 64), w_ukv.dtype)
    wk_pad = jnp.concatenate([t for h in range(MLA_HEADS) for t in (w_ukv[:, h * 128:h * 128 + 64], z64)], axis=1)
    wv = jnp.concatenate([w_ukv[:, h * 128 + 64:(h + 1) * 128] for h in range(MLA_HEADS)], axis=1)
    return w_pad_t, wuq_pad, wk_pad, wv, w_pm, w_pd, w_out


W_IN_LAT = 672


def _grad_parts_in_early(dwt_early):
    dwt_early = jnp.concatenate([dwt_early[:COL_QKV], _qkv_rows_regroup(dwt_early[COL_QKV:COL_LAT], False)], axis=0)

    def in_block(s, h):
        cols = slice(h * 512, (h + 1) * 512)
        out = []
        for lo, hi, pad_lo in sorted(PAD_RUNS):
            a_, b_ = max(lo, s * W_IN_SHARD), min(hi, (s + 1) * W_IN_SHARD)
            if a_ < b_:
                out.append(jnp.zeros((b_ - a_, 512), dwt_early.dtype) if pad_lo >= COL_LAT
                           else dwt_early[pad_lo + a_ - lo:pad_lo + b_ - lo, cols])
        return jnp.concatenate(out, axis=0)

    return jnp.stack([in_block(s, h) for s in range(4) for h in range(2)])


def _grad_parts_in_late(dwt_late):
    rows = jnp.concatenate([dwt_late[0:640], dwt_late[704:736]], axis=0)
    zero = jnp.zeros((W_IN_LAT, 512), dwt_late.dtype)
    return jnp.stack([rows[:, 0:512], rows[:, 512:1024]] + [zero] * 6)


def _shard_blocks(m, axis=1):
    n = m.shape[axis] // 4
    cut = (lambda s: m[:, s * n:(s + 1) * n]) if axis == 1 else (lambda s: m[s * n:(s + 1) * n])
    return jnp.stack([cut(s) for s in range(4) for _ in range(2)])


def _grad_parts_mla(dwuq_pad, dwk_pad, dwv):
    d_uq = jnp.concatenate([dwuq_pad[:, h * 128:h * 128 + 96] for h in range(MLA_HEADS)], axis=1)
    d_ukv = jnp.concatenate([t for h in range(MLA_HEADS) for t in (dwk_pad[:, h * 128:h * 128 + 64], dwv[:, h * 64:(h + 1) * 64])],
                            axis=1)
    return [_shard_blocks(d_uq.astype(BF16)), _shard_blocks(d_ukv.astype(BF16))]


def _rope_tables(positions, token=None):
    pos = positions.reshape(SEQ).astype(F32)
    if token is not None:
        pos = pos + token[0, 0]
    lane = jnp.arange(128)

    def table(rot, first, period):
        inv = ROPE_THETA ** (-jnp.arange(0, rot, 2, dtype=F32) / rot)
        half = rot // 2
        off = lane % period - first
        in1, in2 = (off >= 0) & (off < half), (off >= half) & (off < rot)
        inv_lane = jnp.where(in1 | in2, inv[jnp.clip(off % half, 0, half - 1)], 0.0)
        sign = jnp.where(in1, -1.0, 1.0).astype(F32)
        ang = pos[:, None] * inv_lane[None, :]
        return jnp.cos(ang), jnp.sin(ang) * sign[None, :]

    return table(32, 64, 128), table(16, 0, 64)


class _Links:
    def __init__(self, mats, chip, me):
        landing = [_set_slot(lax.empty((4,) + m.shape, m.dtype), m, chip) for m in mats]
        self.gather, self.token = _gather_start(landing)
        self.me, self.sent, self.handles, self.sums, self.raw = me, {}, {}, {}, {}

    def weights(self, after):
        return _relay_share(_gather_wait(self.gather, after))

    def send(self, blocks, name):
        self.sent[name] = blocks
        self.handles[name], token = _exchange_start(blocks, name)
        return token

    def collect(self, name, after, parts):
        recv = _exchange_wait(self.handles[name], after)
        for r, own, part in zip(recv, self.sent[name], parts):
            if part.startswith("in_"):
                self.sums[part] = _sum_parts(r, own, self.me, 64, "sum_grad_" + part)
            else:
                self.raw[part] = (r, own)
        return tuple(self.sums[part] for part in parts if part in self.sums)


def _device_grads(x, positions, target, gains, links):
    pre_g, q_g, kv_g, post_g = gains
    (mc, ms), (dc, ds) = _rope_tables(positions, links.token)
    h = _prenorm_fwd(x, pre_g, links.token)
    w_pad_t, wuq_pad, wk_pad, wv, w_pm, w_pd, w_out = _full_weights(links.weights((h, mc, ms, dc, ds)))

    p_gz = _matmul(h, w_pad_t, "nt", F32, 1024, 1536, 1024, "in_proj_gates", b_cols=(0, COL_QKV // 1536))
    p_qkv = _matmul(h, w_pad_t, "nt", F32, 1024, 1536, 1024, "in_proj_dilated", b_cols=(COL_QKV // 1536, N_QKV // 1536),
                    lane_blocks=True)
    p_lat = _matmul(h, w_pad_t, "nt", F32, 1024, N_LAT, 1024, "in_proj_latent", b_cols=(COL_LAT // N_LAT, 1))
    q, k, v = _mla_prep_fwd(p_lat, q_g, kv_g, wuq_pad, wk_pad, wv, mc, ms)
    ya, lse_m = _mla_flash_fwd(q, k, v)
    qkv = [_dil_prep_fwd(p_qkv, dc, ds, g) for g in range(3)]
    o_g, l_g = zip(*[_dil_attn_fwd(qkv[g], g) for g in range(3)])
    (dp, dy, dya, dyd, yd, lse_d, loss_cols, dg_post, dwpm, dwpd, dwout) = _tail(
        p_gz, ya, o_g, l_g, x, target, w_pm, w_pd, w_out, post_g)

    for g in range(3):
        dp = _dil_attn_bwd(dp, qkv[g], dyd, yd, lse_d, dc, ds, g)
    dw_early = _matmul(dp, h, "tn", BF16, 1536, 1024, 2048, "dw_in_early", a_cols=(0, COL_LAT // 1536))
    token = links.send([_grad_parts_in_early(dw_early), _shard_blocks(dwpm.astype(BF16)), _shard_blocks(dwpd.astype(BF16)),
                        _shard_blocks(dwout.astype(BF16), axis=0)], "exchange_early")

    dq, dk, dv = _mla_flash_bwd(q, k, v, ya, dya, lse_m, token)
    dp, dwuq_pad, dwk_pad, dwv, dg_q, dg_kv = _mla_prep_bwd(dp, p_lat, dq, dk, dv, q_g, kv_g, wuq_pad, wk_pad, wv, mc, ms)
    dw_late = _matmul(dp, h, "tn", BF16, N_LAT, 1024, 2048, "dw_in_late", a_cols=(COL_LAT // N_LAT, 1))
    token = links.send([_grad_parts_in_late(dw_late)] + _grad_parts_mla(dwuq_pad, dwk_pad, dwv), "exchange_late")
    early = links.collect("exchange_early", dw_late, ("in_early", "pm", "pd", "out"))

    grad_x, dg_pre = _dh_prenorm_bwd(dp, w_pad_t, x, dy, pre_g, (token,) + tuple(early))
    links.collect("exchange_late", grad_x, ("in_late", "uq", "ukv"))

    loss_part = jnp.pad((jnp.sum(loss_cols) * (0.5 / D_MODEL)).reshape(1, 1), ((0, 0), (0, N_GVEC - N_GAINS - 1)))
    gvec = jnp.concatenate([dg_pre, dg_q, dg_kv, dg_post, loss_part], axis=1)
    return grad_x, gvec


def kernel(x, positions, pre_norm_g, w_in, q_norm_g, w_uq, kv_norm_g, w_ukv, w_proj_mla, w_proj_dil, w_out, post_norm_g, loss_target, m_pre_norm_g, m_w_in, m_q_norm_g, m_w_uq, m_kv_norm_g, m_w_ukv, m_w_proj_mla, m_w_proj_dil, m_w_out, m_post_norm_g, v_pre_norm_g, v_w_in, v_q_norm_g, v_w_uq, v_kv_norm_g, v_w_ukv, v_w_proj_mla, v_w_proj_dil, v_w_out, v_post_norm_g):
    xi, yi, ci = _my_place()
    chip, me = 2 * xi + yi, 4 * xi + 2 * yi + ci
    mats = [jnp.swapaxes(w_in, 1, 2)] + [w_uq, w_ukv, w_proj_mla, w_proj_dil, w_out]
    mats = [w.reshape(w.shape[1:]).astype(BF16) for w in mats]
    links = _Links(mats, chip, me)
    gains = (pre_norm_g, q_norm_g, kv_norm_g, post_norm_g)
    grad_x, gvec = _device_grads(x[0], positions, loss_target[0], gains, links)

    sums = links.sums
    in_e = sums["in_early"]
    half_in = jnp.concatenate([in_e[:W_IN_LAT] + jnp.where(chip == 0, sums["in_late"], 0.0), in_e[W_IN_LAT:]], axis=0)
    gvec8 = jnp.pad(gvec, ((0, 7), (0, 0)))
    swapped_in, recv_gains = _swap_halves(half_in, gvec8)
    g_gains = _sum_parts(recv_gains, gvec8, me, 8, "sum_gain_parts")[0:1]
    loss = g_gains[0, N_GAINS]
    sw = lambda a: jnp.swapaxes(a, 1, 2)
    d_in, m_in, v_in, g_in = [sw(o) for o in _adamw_in(sw(w_in), sw(m_w_in), sw(v_w_in), half_in, swapped_in, ci)]
    off = [0, 1024, 1408, 1664, 2688]
    g_gain = [g_gains[:, off[i]:off[i + 1]] for i in range(4)]
    ws = [pre_norm_g, w_in, q_norm_g, w_uq, kv_norm_g, w_ukv, w_proj_mla, w_proj_dil, w_out, post_norm_g]
    ms = [m_pre_norm_g, m_w_in, m_q_norm_g, m_w_uq, m_kv_norm_g, m_w_ukv, m_w_proj_mla, m_w_proj_dil, m_w_out, m_post_norm_g]
    vs = [v_pre_norm_g, v_w_in, v_q_norm_g, v_w_uq, v_kv_norm_g, v_w_ukv, v_w_proj_mla, v_w_proj_dil, v_w_out, v_post_norm_g]
    part_of = [None, "in", None, "uq", None, "ukv", "pm", "pd", "out", None]
    gain_of = iter(g_gain)
    grads, deltas, new_m, new_v = [], [], [], []
    for i, (w, m, v, part) in enumerate(zip(ws, ms, vs, part_of)):
        if part == "in":
            d_, m_, v_, g = d_in, m_in, v_in, g_in
        elif part is not None:
            d_, m_, v_, g = _adamw_recv(w, m, v, *links.raw[part], me, f"adamw_{i}")
        else:
            g = next(gain_of)
            d_, m_, v_ = _adamw(w, g, m, v, f"adamw_{i}")
        grads.append(g)
        deltas.append(d_)
        new_m.append(m_)
        new_v.append(v_)
    return (loss, grad_x.reshape(x.shape), *grads, *deltas, *new_m, *new_v)
```

```python
import jax
import jax.numpy as jnp
from jax import lax
from jax.experimental import pallas as pl
from jax.experimental.pallas import tpu as pltpu

F32 = jnp.float32
BF16 = jnp.bfloat16

SEQ = 4096
D_MODEL = 1024
EPS = 1e-6
ROPE_THETA = 500000.0
MLA_HEADS = 8
Q_RANK = 384
KV_RANK = 256
MLA_SCALE = 96.0 ** -0.5
MLA_ROPE_HALF = 16
DIL_DILATIONS = (1, 4, 16)
DIL_ROPE_HALF = 8
DIL_SCALE = 0.125
BAND = 128

N_LAT = 768
COL_Z, COL_QKV, COL_LAT = 2048, 3072, 7680
N_PAD = 8448


def _qkv_block(tq, g, pr):
    return COL_QKV // 128 + (g * 4 + pr) * 3 + tq

IN_SPLITS = (384, 256, 32, 4608, 512, 512, 1024, 1024)

SHARD_SHAPES = ((2088, 1024), (384, 192), (256, 256), (512, 256), (512, 256), (256, 1024))
N_MATS = len(SHARD_SHAPES)
N_GAINS = 2688
N_GVEC = N_GAINS + 128

ADAM_LR, ADAM_B1, ADAM_B2, ADAM_EPS, ADAM_WD, ADAM_STEP = 0.001, 0.9, 0.999, 1e-08, 0.01, 10

VMEM_LIMIT = 56 * 1024 * 1024
NEG = -1e30
MESH = pl.DeviceIdType.MESH


def _cparams(**kw):
    return pltpu.CompilerParams(vmem_limit_bytes=VMEM_LIMIT, **kw)


def _dot(a, b, dims):
    return lax.dot_general(a, b, (dims, ((), ())), preferred_element_type=F32)


def _nn(a, b):
    return _dot(a, b, ((1,), (0,)))


def _nt(a, b):
    return _dot(a, b, ((1,), (1,)))


def _tn(a, b):
    return _dot(a, b, ((0,), (0,)))


def _rope_lanes(shape, half, period, first):
    lane = lax.broadcasted_iota(jnp.int32, shape, len(shape) - 1) % period
    return (lane >= first) & (lane < first + half), (lane >= first + half) & (lane < first + 2 * half)


def _rope_fwd(x, c, s, half, lanes):
    x1, _ = lanes
    return x * c + jnp.where(x1, pltpu.roll(x, 128 - half, 1), pltpu.roll(x, half, 1)) * s


def _rope_bwd(g, c, s, half, lanes):
    x1, x2 = lanes
    gs = g * s
    return g * c + jnp.where(x2, pltpu.roll(gs, half, 1), jnp.where(x1, pltpu.roll(gs, 128 - half, 1), 0.0))


def _sigmoid(x):
    return 1.0 / (1.0 + jnp.exp(-x))


def _after(token):
    tokens = [t for t in (token if isinstance(token, (tuple, list)) else [token]) if t is not None]
    return tokens, [pl.BlockSpec(memory_space=pl.ANY)] * len(tokens)


def _matmul(a, b, mode, out_dtype, tm, tn, tk, name, token=None, b_cols=None, a_cols=None, lane_blocks=False):
    after, after_specs = _after(token)
    if mode == "nn":
        (m, k), n = a.shape, b.shape[1]
        first = 0
        if b_cols is not None:
            first, n = b_cols[0], b_cols[1] * tn
        a_spec = pl.BlockSpec((tm, tk), lambda j, i, kk: (i, kk))
        b_spec = pl.BlockSpec((tk, tn), lambda j, i, kk: (kk, j + first))
        dot = _nn
    elif mode == "nt":
        (m, k), n = a.shape, b.shape[0]
        first = 0
        if b_cols is not None:
            first, n = b_cols[0], b_cols[1] * tn
        a_spec = pl.BlockSpec((tm, tk), lambda j, i, kk: (i, kk))
        b_spec = pl.BlockSpec((tn, tk), lambda j, i, kk: (j + first, kk))
        dot = _nt
    else:
        (k, m), n = a.shape, b.shape[1]
        first = 0
        if a_cols is not None:
            first, m = a_cols[0], a_cols[1] * tm
        a_spec = pl.BlockSpec((tk, tm), lambda j, i, kk: (kk, i + first))
        b_spec = pl.BlockSpec((tk, tn), lambda j, i, kk: (kk, j))
        dot = _tn
    assert m % tm == 0 and n % tn == 0 and k % tk == 0, (name, m, n, k, tm, tn, tk)
    nk = k // tk

    def body(a_ref, b_ref, *rest):
        o_ref, acc_ref = rest[-2:]
        kk = pl.program_id(2)
        part = dot(a_ref[...], b_ref[...])

        @pl.when(kk == 0)
        def _():
            acc_ref[...] = part

        @pl.when(kk > 0)
        def _():
            acc_ref[...] += part

        @pl.when(kk == nk - 1)
        def _():
            if lane_blocks:
                for blk in range(tn // 128):
                    o_ref[blk] = acc_ref[:, blk * 128:(blk + 1) * 128].astype(o_ref.dtype)
            else:
                o_ref[...] = acc_ref[...].astype(o_ref.dtype)

    if lane_blocks:
        out_spec = pl.BlockSpec((tn // 128, tm, 128), lambda j, i, kk: (j, i, 0))
        out_shape = jax.ShapeDtypeStruct((n // 128, m, 128), out_dtype)
    else:
        out_spec = pl.BlockSpec((tm, tn), lambda j, i, kk: (i, j))
        out_shape = jax.ShapeDtypeStruct((m, n), out_dtype)
    return pl.pallas_call(
        body, name=name, grid=(n // tn, m // tm, nk),
        in_specs=[a_spec, b_spec] + after_specs,
        out_specs=out_spec, out_shape=out_shape,
        scratch_shapes=[pltpu.VMEM((tm, tn), F32)],
        compiler_params=_cparams(),
    )(a, b, *after)


def _prenorm_fwd(x, g, token=None):
    tm = 512
    after, after_specs = _after(token)

    def body(x_ref, g_ref, *rest):
        xv = x_ref[...]
        r = lax.rsqrt(jnp.mean(xv * xv, axis=-1, keepdims=True) + EPS)
        rest[-1][...] = (xv * r * g_ref[...]).astype(BF16)

    return pl.pallas_call(
        body, name="prenorm_fwd", grid=(SEQ // tm,),
        in_specs=[pl.BlockSpec((tm, D_MODEL), lambda i: (i, 0)), pl.BlockSpec((1, D_MODEL), lambda i: (0, 0))] + after_specs,
        out_specs=pl.BlockSpec((tm, D_MODEL), lambda i: (i, 0)),
        out_shape=jax.ShapeDtypeStruct((SEQ, D_MODEL), BF16),
    )(x, g, *after)


def _dh_prenorm_bwd(dp, w_pad_t, x, dy, g, token=None):
    tm, tk = 1024, 1408
    nk = N_PAD // tk
    after, after_specs = _after(token)

    def body(a_ref, b_ref, x_ref, dy_ref, g_ref, *rest):
        gx_ref, dg_ref, acc_ref = rest[-3:]
        i, kk = pl.program_id(0), pl.program_id(1)
        part = _nn(a_ref[...], b_ref[...])

        @pl.when(kk == 0)
        def _():
            acc_ref[...] = part

        @pl.when(kk > 0)
        def _():
            acc_ref[...] += part

        @pl.when(kk == nk - 1)
        def _():
            xv = x_ref[...]
            r = lax.rsqrt(jnp.mean(xv * xv, axis=-1, keepdims=True) + EPS)
            n = xv * r
            dhv = acc_ref[...]
            dn = dhv * g_ref[...]
            gx_ref[...] = dy_ref[...] + r * (dn - n * jnp.mean(dn * n, axis=-1, keepdims=True))
            cols = jnp.sum(dhv * n, axis=0, keepdims=True)

            @pl.when(i == 0)
            def _():
                dg_ref[...] = cols

            @pl.when(i > 0)
            def _():
                dg_ref[...] += cols

    row = pl.BlockSpec((tm, D_MODEL), lambda i, kk: (i, 0))
    vec = pl.BlockSpec((1, D_MODEL), lambda i, kk: (0, 0))
    return pl.pallas_call(
        body, name="dh_prenorm_bwd", grid=(SEQ // tm, nk),
        in_specs=[pl.BlockSpec((tm, tk), lambda i, kk: (i, kk)), pl.BlockSpec((tk, D_MODEL), lambda i, kk: (kk, 0)),
                  row, row, vec] + after_specs,
        out_specs=[row, vec],
        out_shape=[jax.ShapeDtypeStruct((SEQ, D_MODEL), F32), jax.ShapeDtypeStruct((1, D_MODEL), F32)],
        scratch_shapes=[pltpu.VMEM((tm, D_MODEL), F32)],
        compiler_params=_cparams(),
    )(dp, w_pad_t, x, dy, g, *after)


def _mla_prep_fwd(p, qg, kvg, wuq, wk, wv, rc, rs):
    tm = 1024

    def body(lat_ref, qg_ref, kvg_ref, wuq_ref, wk_ref, wv_ref, c_ref, s_ref, q_ref, k_ref, v_ref):
        c, s = c_ref[...], s_ref[...]
        lanes = _rope_lanes((tm, 128), MLA_ROPE_HALF, 128, 64)
        cq = lat_ref[:, 0:Q_RANK]
        r1 = lax.rsqrt(jnp.mean(cq * cq, axis=-1, keepdims=True) + EPS)
        cqn = (cq * r1 * qg_ref[...]).astype(BF16)
        q = _nn(cqn, wuq_ref[...])
        for h in range(MLA_HEADS):
            sl = slice(h * 128, (h + 1) * 128)
            q_ref[:, sl] = (_rope_fwd(q[:, sl], c, s, MLA_ROPE_HALF, lanes) * MLA_SCALE).astype(BF16)
        ckv = lat_ref[:, Q_RANK:Q_RANK + KV_RANK]
        r2 = lax.rsqrt(jnp.mean(ckv * ckv, axis=-1, keepdims=True) + EPS)
        ckvn = (ckv * r2 * kvg_ref[...]).astype(BF16)
        krr = _rope_fwd(lat_ref[:, Q_RANK + KV_RANK:N_LAT], c, s, MLA_ROPE_HALF, lanes)
        kn = _nn(ckvn, wk_ref[...])
        for h in range(MLA_HEADS):
            sl = slice(h * 128, (h + 1) * 128)
            k_ref[:, sl] = (kn[:, sl] + krr).astype(BF16)
        v_ref[...] = _nn(ckvn, wv_ref[...]).astype(BF16)

    def full(shape):
        return pl.BlockSpec(shape, lambda i: (0, 0))

    def rows(w):
        return pl.BlockSpec((tm, w), lambda i: (i, 0))

    return pl.pallas_call(
        body, name="mla_prep_fwd", grid=(SEQ // tm,),
        in_specs=[pl.BlockSpec((tm, N_LAT), lambda i: (i, 0)),
                  full((1, Q_RANK)), full((1, KV_RANK)), full((Q_RANK, 1024)), full((KV_RANK, 1024)),
                  full((KV_RANK, 512)), rows(128), rows(128)],
        out_specs=[rows(1024), rows(1024), rows(512)],
        out_shape=[jax.ShapeDtypeStruct((SEQ, 1024), BF16), jax.ShapeDtypeStruct((SEQ, 1024), BF16),
                   jax.ShapeDtypeStruct((SEQ, 512), BF16)],
        compiler_params=_cparams(),
    )(p, qg, kvg, wuq, wk, wv, rc, rs)


def _mla_prep_bwd(dp_in, p, dq, dk, dv, qg, kvg, wuq, wk, wv, rc, rs):
    tm = 1024

    def body(dp_any, lat_ref, dq_ref, dk_ref, dv_ref, qg_ref, kvg_ref, wuq_ref, wk_ref, wv_ref,
             c_ref, s_ref, dp_ref, dwuq_ref, dwk_ref, dwv_ref, dgq_ref, dgkv_ref, dqb_ref, dkb_ref):
        del dp_any
        c, s = c_ref[...], s_ref[...]
        lanes = _rope_lanes((tm, 128), MLA_ROPE_HALF, 128, 64)
        lane = lax.broadcasted_iota(jnp.int32, (tm, 128), 1)
        dkr = jnp.zeros((tm, 128), F32)
        for h in range(MLA_HEADS):
            sl = slice(h * 128, (h + 1) * 128)
            dqb_ref[:, sl] = _rope_bwd(dq_ref[:, sl] * MLA_SCALE, c, s, MLA_ROPE_HALF, lanes).astype(BF16)
            dkh = dk_ref[:, sl]
            dkr = dkr + dkh
            dkb_ref[:, sl] = jnp.where(lane < 64, dkh, 0.0).astype(BF16)
        dkr = jnp.where((lane >= 64) & (lane < 96), dkr, 0.0)
        dkr = _rope_bwd(dkr, c, s, MLA_ROPE_HALF, lanes)
        dvb = dv_ref[...].astype(BF16)

        cq = lat_ref[:, 0:Q_RANK]
        r1 = lax.rsqrt(jnp.mean(cq * cq, axis=-1, keepdims=True) + EPS)
        n1 = cq * r1
        dcqn = _nt(dqb_ref[...], wuq_ref[...])
        dn1 = dcqn * qg_ref[...]
        dcq = r1 * (dn1 - n1 * jnp.mean(dn1 * n1, axis=-1, keepdims=True))
        pq = jnp.sum(dcqn * n1, axis=0, keepdims=True)

        ckv = lat_ref[:, Q_RANK:Q_RANK + KV_RANK]
        r2 = lax.rsqrt(jnp.mean(ckv * ckv, axis=-1, keepdims=True) + EPS)
        n2 = ckv * r2
        dckvn = _nt(dkb_ref[...], wk_ref[...]) + _nt(dvb, wv_ref[...])
        dn2 = dckvn * kvg_ref[...]
        dckv = r2 * (dn2 - n2 * jnp.mean(dn2 * n2, axis=-1, keepdims=True))
        pkv = jnp.sum(dckvn * n2, axis=0, keepdims=True)
        cqn = (n1 * qg_ref[...]).astype(BF16)
        ckvn = (n2 * kvg_ref[...]).astype(BF16)
        wq, wk_, wv_ = _tn(cqn, dqb_ref[...]), _tn(ckvn, dkb_ref[...]), _tn(ckvn, dvb)

        dp_ref[:, 0:Q_RANK] = dcq.astype(BF16)
        dp_ref[:, Q_RANK:Q_RANK + KV_RANK] = dckv.astype(BF16)
        dp_ref[:, Q_RANK + KV_RANK:N_LAT] = dkr.astype(BF16)

        @pl.when(pl.program_id(0) == 0)
        def _():
            dgq_ref[...] = pq
            dgkv_ref[...] = pkv
            dwuq_ref[...] = wq
            dwk_ref[...] = wk_
            dwv_ref[...] = wv_

        @pl.when(pl.program_id(0) > 0)
        def _():
            dgq_ref[...] += pq
            dgkv_ref[...] += pkv
            dwuq_ref[...] += wq
            dwk_ref[...] += wk_
            dwv_ref[...] += wv_

    def full(shape):
        return pl.BlockSpec(shape, lambda i: (0, 0))

    def rows(w):
        return pl.BlockSpec((tm, w), lambda i: (i, 0))

    lat = pl.BlockSpec((tm, N_LAT), lambda i: (i, 0))
    dlat = pl.BlockSpec((tm, N_LAT), lambda i: (i, COL_LAT // N_LAT))
    return pl.pallas_call(
        body, name="mla_prep_bwd", grid=(SEQ // tm,),
        in_specs=[pl.BlockSpec(memory_space=pl.ANY), lat, rows(1024), rows(1024), rows(512),
                  full((1, Q_RANK)), full((1, KV_RANK)), full((Q_RANK, 1024)), full((KV_RANK, 1024)),
                  full((KV_RANK, 512)), rows(128), rows(128)],
        out_specs=[dlat, full((Q_RANK, 1024)), full((KV_RANK, 1024)), full((KV_RANK, 512)),
                   full((1, Q_RANK)), full((1, KV_RANK))],
        out_shape=[jax.ShapeDtypeStruct((SEQ, N_PAD), BF16), jax.ShapeDtypeStruct((Q_RANK, 1024), F32),
                   jax.ShapeDtypeStruct((KV_RANK, 1024), F32), jax.ShapeDtypeStruct((KV_RANK, 512), F32),
                   jax.ShapeDtypeStruct((1, Q_RANK), F32), jax.ShapeDtypeStruct((1, KV_RANK), F32)],
        input_output_aliases={0: 0},
        scratch_shapes=[pltpu.VMEM((tm, 1024), BF16), pltpu.VMEM((tm, 1024), BF16)],
        compiler_params=_cparams(),
    )(dp_in, p, dq, dk, dv, qg, kvg, wuq, wk, wv, rc, rs)


FLASH_T = 1024


def _head_half(shape, hh):
    lane = lax.broadcasted_iota(jnp.int32, shape, 1)
    return (lane < 64) if hh == 0 else (lane >= 64)


def _diag_keep(nr, nk):
    row = lax.broadcasted_iota(jnp.int32, (nr, nk), 0)
    col = lax.broadcasted_iota(jnp.int32, (nr, nk), 1)
    return row + (nk - nr) >= col


def _tri_steps(nb, q_major):
    if q_major:
        pairs = [(i, kb) for i in range(nb) for kb in range(i + 1)]
    else:
        pairs = [(i, kb) for kb in range(nb) for i in range(kb, nb)]
    return jnp.asarray([p[0] for p in pairs], jnp.int32), jnp.asarray([p[1] for p in pairs], jnp.int32)


def _mla_flash_fwd(q, k, v):
    t = FLASH_T
    nb = SEQ // t
    qtab, ktab = _tri_steps(nb, True)

    def body(qi_ref, ki_ref, q_ref, k_ref, v_ref, o_ref, lse_ref, m_scr, l_scr, acc_scr):
        step = pl.program_id(1)
        i, kb = qi_ref[step], ki_ref[step]

        @pl.when(kb == 0)
        def _():
            m_scr[...] = jnp.full_like(m_scr, NEG)
            l_scr[...] = jnp.zeros_like(l_scr)
            acc_scr[...] = jnp.zeros_like(acc_scr)

        def update(r0, nr, nk, diagonal):
            rs = slice(r0, r0 + nr)
            vv = v_ref[0:nk, :]
            for hh in range(2):
                sl = slice(hh * 128, (hh + 1) * 128)
                s = _nt(q_ref[rs, sl], k_ref[0:nk, sl])
                if diagonal:
                    s = jnp.where(_diag_keep(nr, nk), s, NEG)
                m_prev = m_scr[hh, rs, :]
                m_new = jnp.maximum(m_prev, jnp.max(s, axis=-1, keepdims=True))
                pr = jnp.exp(s - jnp.tile(m_new, (1, nk // 128)))
                alpha = jnp.exp(m_prev - m_new)
                l_scr[hh, rs, :] = alpha * l_scr[hh, rs, :] + jnp.sum(pr, axis=-1, keepdims=True)
                acc_scr[hh, rs, :] = alpha * acc_scr[hh, rs, :] + _nn(pr.astype(BF16), vv)
                m_scr[hh, rs, :] = m_new

        @pl.when(kb < i)
        def _():
            update(0, t, t, False)

        @pl.when(kb == i)
        def _():
            update(0, t // 2, t // 2, True)
            update(t // 2, t // 2, t, True)
            o0 = acc_scr[0] / l_scr[0]
            o1 = acc_scr[1] / l_scr[1]
            o_ref[...] = jnp.where(_head_half((t, 128), 0), o0, o1)
            for hh in range(2):
                lse_ref[:, hh * 128:(hh + 1) * 128] = m_scr[hh] + jnp.log(l_scr[hh])

    grid_spec = pltpu.PrefetchScalarGridSpec(
        num_scalar_prefetch=2, grid=(4, qtab.shape[0]),
        in_specs=[pl.BlockSpec((t, 256), lambda j, s, qi, ki: (qi[s], j)),
                  pl.BlockSpec((t, 256), lambda j, s, qi, ki: (ki[s], j)),
                  pl.BlockSpec((t, 128), lambda j, s, qi, ki: (ki[s], j))],
        out_specs=[pl.BlockSpec((t, 128), lambda j, s, qi, ki: (qi[s], j)),
                   pl.BlockSpec((t, 256), lambda j, s, qi, ki: (qi[s], j))],
        scratch_shapes=[pltpu.VMEM((2, t, 128), F32), pltpu.VMEM((2, t, 128), F32), pltpu.VMEM((2, t, 128), F32)])
    return pl.pallas_call(
        body, name="mla_flash_fwd", grid_spec=grid_spec,
        out_shape=[jax.ShapeDtypeStruct((SEQ, 512), F32), jax.ShapeDtypeStruct((SEQ, 1024), F32)],
        compiler_params=_cparams(),
    )(qtab, ktab, q, k, v)


def _mla_flash_bwd(q, k, v, o, do, lse, token=None):
    t = FLASH_T
    nb = SEQ // t
    qtab, ktab = _tri_steps(nb, False)
    after, after_specs = _after(token)

    def body(qi_ref, ki_ref, q_ref, k_ref, v_ref, o_ref, do_ref, lse_ref, *rest):
        dq_ref, dk_ref, dv_ref, dk_scr, dv_scr = rest[-5:]
        step = pl.program_id(1)
        i, kb = qi_ref[step], ki_ref[step]

        @pl.when(step == 0)
        def _():
            dq_ref[...] = jnp.zeros_like(dq_ref)

        @pl.when(i == kb)
        def _():
            dk_scr[...] = jnp.zeros_like(dk_scr)
            dv_scr[...] = jnp.zeros_like(dv_scr)

        def update(r0, nr, nk, diagonal):
            rs = slice(r0, r0 + nr)
            vv = v_ref[0:nk, :]
            ov = o_ref[rs, :]
            dov = do_ref[rs, :]
            rows = pl.ds(pl.multiple_of(i * t + r0, t // 2), nr)
            for hh in range(2):
                sl = slice(hh * 128, (hh + 1) * 128)
                qh, kh = q_ref[rs, sl], k_ref[0:nk, sl]
                s = _nt(qh, kh)
                if diagonal:
                    s = jnp.where(_diag_keep(nr, nk), s, NEG)
                pr = jnp.exp(s - jnp.tile(lse_ref[rs, sl], (1, nk // 128)))
                dom = jnp.where(_head_half((nr, 128), hh), dov, 0.0)
                domb = dom.astype(BF16)
                dv_scr[0:nk, :] += _tn(pr.astype(BF16), domb)
                dpr = _nt(domb, vv)
                delta = jnp.sum(dom * ov, axis=-1, keepdims=True)
                ds = (pr * (dpr - delta)).astype(BF16)
                dq_ref[rows, sl] += _nn(ds, kh)
                dk_scr[hh, 0:nk, :] += _tn(ds, qh)

        @pl.when(i > kb)
        def _():
            update(0, t, t, False)

        @pl.when(i == kb)
        def _():
            update(0, t // 2, t // 2, True)
            update(t // 2, t // 2, t, True)

        @pl.when(i == nb - 1)
        def _():
            dk_ref[:, 0:128] = dk_scr[0]
            dk_ref[:, 128:256] = dk_scr[1]
            dv_ref[...] = dv_scr[...]

    qi_map = lambda j, s, qi, ki: (qi[s], j)
    ki_map = lambda j, s, qi, ki: (ki[s], j)
    grid_spec = pltpu.PrefetchScalarGridSpec(
        num_scalar_prefetch=2, grid=(4, qtab.shape[0]),
        in_specs=[pl.BlockSpec((t, 256), qi_map), pl.BlockSpec((t, 256), ki_map), pl.BlockSpec((t, 128), ki_map),
                  pl.BlockSpec((t, 128), qi_map), pl.BlockSpec((t, 128), qi_map), pl.BlockSpec((t, 256), qi_map)]
        + after_specs,
        out_specs=[pl.BlockSpec((SEQ, 256), lambda j, s, qi, ki: (0, j)), pl.BlockSpec((t, 256), ki_map),
                   pl.BlockSpec((t, 128), ki_map)],
        scratch_shapes=[pltpu.VMEM((2, t, 128), F32), pltpu.VMEM((t, 128), F32)])
    return pl.pallas_call(
        body, name="mla_flash_bwd", grid_spec=grid_spec,
        out_shape=[jax.ShapeDtypeStruct((SEQ, 1024), F32), jax.ShapeDtypeStruct((SEQ, 1024), F32),
                   jax.ShapeDtypeStruct((SEQ, 512), F32)],
        compiler_params=_cparams(),
    )(qtab, ktab, q, k, v, o, do, lse, *after)


def _strided(start, size, d):
    return pl.ds(start, size) if d == 1 else pl.ds(start, size, stride=d)


def _dil_prep_fwd(p, rc, rs, g):
    d = DIL_DILATIONS[g]
    sub_len = SEQ // d
    ch = min(sub_len, 512)

    def body(p_ref, c_ref, s_ref, o_ref, x_scr):
        tq = pl.program_id(0)
        lanes = _rope_lanes((ch, 128), DIL_ROPE_HALF, 64, 0)
        o_ref[0, 0, 0:BAND, :] = jnp.zeros((BAND, 128), BF16)

        @pl.when(tq < 2)
        def _():
            mult = jnp.where(tq == 0, DIL_SCALE, 1.0).astype(F32)
            for c0 in range(0, SEQ, ch):
                rows = pl.ds(c0, ch)
                x_scr[rows, :] = _rope_fwd(p_ref[rows, :], c_ref[rows, :] * mult, s_ref[rows, :] * mult, DIL_ROPE_HALF, lanes)

        def gather(take):
            for r in range(d):
                for c0 in range(0, sub_len, ch):
                    at = BAND + r * sub_len + c0
                    o_ref[0, 0, at:at + ch, :] = take(_strided(r + c0 * d, ch, d)).astype(BF16)

        @pl.when(tq < 2)
        def _():
            gather(lambda rows: x_scr[rows, :])

        @pl.when(tq == 2)
        def _():
            gather(lambda rows: p_ref[rows, :])

    tab = pl.BlockSpec((SEQ, 128), lambda tq, pr: (0, 0))
    return pl.pallas_call(
        body, name=f"dil_prep_fwd_g{g}", grid=(3, 4),
        in_specs=[pl.BlockSpec((None, SEQ, 128), lambda tq, pr: (_qkv_block(tq, g, pr) - COL_QKV // 128, 0, 0)), tab, tab],
        out_specs=pl.BlockSpec((1, 1, BAND + SEQ, 128), lambda tq, pr: (tq, pr, 0, 0)),
        out_shape=jax.ShapeDtypeStruct((3, 4, BAND + SEQ, 128), BF16),
        scratch_shapes=[pltpu.VMEM((SEQ, 128), F32)],
        compiler_params=_cparams(),
    )(p, rc, rs)


DIL_ST_FWD, DIL_ST_BWD = 1024, 2048


def _band_keep(g, b, t, nb):
    nbs = SEQ // DIL_DILATIONS[g] // BAND
    row = lax.broadcasted_iota(jnp.int32, (BAND, 2 * BAND), 0)
    col = lax.broadcasted_iota(jnp.int32, (BAND, 2 * BAND), 1)
    cur = (col >= BAND) & (row >= col - BAND)
    prev = (col < BAND) & (col >= row)
    if nbs >= nb:
        if b > 0:
            return cur | prev
        return cur | (prev & ((t * nb) % nbs != 0))
    return cur | prev if b % nbs else cur


def _dil_tok(g, b, t, nb):
    d = DIL_DILATIONS[g]
    nbs = SEQ // d // BAND
    gb = t * nb + b
    return _strided((gb % nbs) * BAND * d + gb // nbs, BAND, d)


def _dil_attn_fwd(qkv, g):
    DIL_ST, DIL_NB = DIL_ST_FWD, DIL_ST_FWD // BAND

    def body(q_ref, k_ref, v_ref, o_ref, l_ref, s_scr, p_scr, o_scr):
        t = pl.program_id(1)
        base = t * DIL_ST
        half0 = _head_half((DIL_ST, 128), 0)
        lse_h = []
        for hh in range(2):
            half = _head_half((BAND, 128), hh)
            for b in range(DIL_NB):
                qv = q_ref[0, 0, pl.ds(pl.multiple_of(base + (b + 1) * BAND, BAND), BAND), :]
                k2 = k_ref[0, 0, pl.ds(pl.multiple_of(base + b * BAND, BAND), 2 * BAND), :]
                sb = _nt(jnp.where(half, qv, jnp.zeros_like(qv)), k2)
                s_scr[b * BAND:(b + 1) * BAND, :] = jnp.where(_band_keep(g, b, t, DIL_NB), sb, NEG)
            s = s_scr[...]
            m = jnp.max(s, axis=-1, keepdims=True)
            pr = jnp.exp(s - m)
            den = jnp.sum(pr, axis=-1, keepdims=True)
            p_scr[...] = pr.astype(BF16)
            for b in range(DIL_NB):
                v2 = v_ref[0, 0, pl.ds(pl.multiple_of(base + b * BAND, BAND), 2 * BAND), :]
                o_scr[hh, b * BAND:(b + 1) * BAND, :] = _nn(p_scr[b * BAND:(b + 1) * BAND, :], v2)
            o_scr[hh] = o_scr[hh] / den
            lse_h.append(m + jnp.log(den))
        out = jnp.where(half0, o_scr[0], o_scr[1])
        lse = jnp.where(half0, lse_h[0], lse_h[1])
        for b in range(DIL_NB):
            tok = _dil_tok(g, b, t, DIL_NB)
            o_ref[tok, :] = out[b * BAND:(b + 1) * BAND, :]
            l_ref[tok, :] = lse[b * BAND:(b + 1) * BAND, :]

    def inp(tq):
        return pl.BlockSpec((1, 1, BAND + SEQ, 128), lambda pr, t: (tq, pr, 0, 0))

    out = pl.BlockSpec((SEQ, 128), lambda pr, t: (0, pr))
    return pl.pallas_call(
        body, name=f"dil_attn_fwd_g{g}", grid=(4, SEQ // DIL_ST),
        in_specs=[inp(0), inp(1), inp(2)], out_specs=[out, out],
        out_shape=[jax.ShapeDtypeStruct((SEQ, 512), F32), jax.ShapeDtypeStruct((SEQ, 512), F32)],
        scratch_shapes=[pltpu.VMEM((DIL_ST, 2 * BAND), F32), pltpu.VMEM((DIL_ST, 2 * BAND), BF16),
                        pltpu.VMEM((2, DIL_ST, 128), F32)],
        compiler_params=_cparams(),
    )(qkv, qkv, qkv)


def _dil_attn_bwd(dp_in, qkv, dyd, yd, lse_all, rc, rs, g, token=None):
    d = DIL_DILATIONS[g]
    sub_len = SEQ // d
    DIL_ST, DIL_NB = DIL_ST_BWD, DIL_ST_BWD // BAND
    nst = SEQ // DIL_ST
    after, after_specs = _after(token)
    ch = 512

    def body(dp_any, q_ref, k_ref, v_ref, do_ref, y_ref, l_ref, c_ref, sn_ref, *rest):
        dp_ref, tok_scr, dk_scr, dv_scr, s_scr, dp_scr, p_scr, ds_scr, do_scr, y_scr, l_scr, dq_scr = rest[-12:]
        del dp_any
        t = pl.program_id(1)
        base = t * DIL_ST

        @pl.when(t == 0)
        def _():
            dk_scr[...] = jnp.zeros_like(dk_scr)
            dv_scr[...] = jnp.zeros_like(dv_scr)

        for b in range(DIL_NB):
            tok = _dil_tok(g, b, t, DIL_NB)
            do_scr[b * BAND:(b + 1) * BAND, :] = do_ref[tok, :]
            y_scr[b * BAND:(b + 1) * BAND, :] = y_ref[tok, :]
            l_scr[b * BAND:(b + 1) * BAND, :] = l_ref[tok, :]
        for hh in range(2):
            half = _head_half((BAND, 128), hh)
            half_st = _head_half((DIL_ST, 128), hh)
            dom = jnp.where(half_st, do_scr[...], 0.0)
            delta = jnp.sum(dom * y_scr[...], axis=-1, keepdims=True)
            lcol = jnp.max(jnp.where(half_st, l_scr[...], NEG), axis=-1, keepdims=True)
            for b in range(DIL_NB):
                rows = slice(b * BAND, (b + 1) * BAND)
                qv = q_ref[0, 0, pl.ds(pl.multiple_of(base + (b + 1) * BAND, BAND), BAND), :]
                band = pl.ds(pl.multiple_of(base + b * BAND, BAND), 2 * BAND)
                sb = _nt(jnp.where(half, qv, jnp.zeros_like(qv)), k_ref[0, 0, band, :])
                s_scr[rows, :] = jnp.where(_band_keep(g, b, t, DIL_NB), sb, NEG)
                dp_scr[rows, :] = _nt(dom[rows, :].astype(BF16), v_ref[0, 0, band, :])
            pr = jnp.exp(s_scr[...] - lcol)
            p_scr[...] = pr.astype(BF16)
            ds_scr[...] = (pr * (dp_scr[...] - delta)).astype(BF16)
            for b in range(DIL_NB):
                rows = slice(b * BAND, (b + 1) * BAND)
                qv = q_ref[0, 0, pl.ds(pl.multiple_of(base + (b + 1) * BAND, BAND), BAND), :]
                band = pl.ds(pl.multiple_of(base + b * BAND, BAND), 2 * BAND)
                dqb = jnp.where(half, _nn(ds_scr[rows, :], k_ref[0, 0, band, :]), 0.0)
                if hh == 0:
                    dq_scr[rows, :] = dqb
                else:
                    dq_scr[rows, :] += dqb
                half2 = _head_half((2 * BAND, 128), hh)
                dk_scr[band, :] += jnp.where(half2, _tn(ds_scr[rows, :], qv), 0.0)
                dv_scr[band, :] += _tn(p_scr[rows, :], dom[rows, :].astype(BF16))
        for b in range(DIL_NB):
            tok_scr[pl.ds(0, 1), _dil_tok(g, b, t, DIL_NB), :] = dq_scr[b * BAND:(b + 1) * BAND, :][None]

        @pl.when(t == nst - 1)
        def _():
            for r in range(d):
                rows = _strided(r, sub_len, d)
                tok_scr[pl.ds(1, 1), rows, :] = dk_scr[BAND + r * sub_len:BAND + (r + 1) * sub_len, :][None]
                tok_scr[pl.ds(2, 1), rows, :] = dv_scr[BAND + r * sub_len:BAND + (r + 1) * sub_len, :][None]
            lanes = _rope_lanes((ch, 128), DIL_ROPE_HALF, 64, 0)
            for c0 in range(0, SEQ, ch):
                rows = slice(c0, c0 + ch)
                cv, sv = c_ref[rows, :], sn_ref[rows, :]
                dp_ref[rows, 0:128] = _rope_bwd(tok_scr[0, rows, :], cv * DIL_SCALE, sv * DIL_SCALE, DIL_ROPE_HALF, lanes).astype(BF16)
                dp_ref[rows, 128:256] = _rope_bwd(tok_scr[1, rows, :], cv, sv, DIL_ROPE_HALF, lanes).astype(BF16)
                dp_ref[rows, 256:384] = tok_scr[2, rows, :].astype(BF16)

    def inp(tq):
        return pl.BlockSpec((1, 1, BAND + SEQ, 128), lambda pr, t: (tq, pr, 0, 0))

    tok_spec = pl.BlockSpec((SEQ, 128), lambda pr, t: (0, pr))
    tab = pl.BlockSpec((SEQ, 128), lambda pr, t: (0, 0))
    st = (DIL_ST, 2 * BAND)
    return pl.pallas_call(
        body, name=f"dil_attn_bwd_g{g}", grid=(4, nst),
        in_specs=[pl.BlockSpec(memory_space=pl.ANY), inp(0), inp(1), inp(2), tok_spec, tok_spec, tok_spec, tab, tab]
        + after_specs,
        out_specs=pl.BlockSpec((SEQ, 384), lambda pr, t: (0, _qkv_block(0, g, pr) // 3)),
        out_shape=jax.ShapeDtypeStruct((SEQ, N_PAD), BF16),
        input_output_aliases={0: 0},
        scratch_shapes=[pltpu.VMEM((3, SEQ, 128), F32),
                        pltpu.VMEM((BAND + SEQ, 128), F32), pltpu.VMEM((BAND + SEQ, 128), F32),
                        pltpu.VMEM(st, F32), pltpu.VMEM(st, F32), pltpu.VMEM(st, BF16), pltpu.VMEM(st, BF16),
                        pltpu.VMEM((DIL_ST, 128), F32), pltpu.VMEM((DIL_ST, 128), F32), pltpu.VMEM((DIL_ST, 128), F32),
                        pltpu.VMEM((DIL_ST, 128), F32)],
        compiler_params=_cparams(),
    )(dp_in, qkv, qkv, qkv, dyd, yd, lse_all, rc, rs, *after)


TAIL_T = 256


def _tail(p, ya, o_g, l_g, x, target, wpm, wpd, wout, post_g):
    tm = TAIL_T

    def body(pgz_ref, ya_ref, o0_ref, o1_ref, o2_ref, l0_ref, l1_ref, l2_ref, x_ref, t_ref,
             wpm_ref, wpd_ref, wout_ref, pg_ref,
             dp_ref, dy_ref, dya_ref, dyd_ref, yd_ref, lse_ref, loss_ref, dgp_ref, dwpm_ref, dwpd_ref, dwout_ref):
        l0, l1, l2 = l0_ref[...], l1_ref[...], l2_ref[...]
        mx = jnp.maximum(jnp.maximum(l0, l1), l2)
        e0, e1, e2 = jnp.exp(l0 - mx), jnp.exp(l1 - mx), jnp.exp(l2 - mx)
        den = e0 + e1 + e2
        yd = (e0 * o0_ref[...] + e1 * o1_ref[...] + e2 * o2_ref[...]) / den
        yd_ref[...] = yd
        lse_ref[...] = mx + jnp.log(den)
        ya = ya_ref[...]

        gm, gd = pgz_ref[:, 0:1024], pgz_ref[:, 1024:2048]
        zm, zd = pgz_ref[:, 2048:2560], pgz_ref[:, 2560:3072]
        szm, szd = _sigmoid(zm), _sigmoid(zd)
        sm, sd = zm * szm, zd * szd
        ua = (ya * sm).astype(BF16)
        ud = (yd * sd).astype(BF16)
        pa = _nn(ua, wpm_ref[...])
        pd = _nn(ud, wpd_ref[...])
        sgm, sgd = _sigmoid(gm), _sigmoid(gd)
        mg = (sgm * pa + sgd * pd).astype(BF16)
        t = _nn(mg, wout_ref[...])
        r3 = lax.rsqrt(jnp.mean(t * t, axis=-1, keepdims=True) + EPS)
        n = t * r3
        pg = pg_ref[...]
        err = x_ref[...] + n * pg - t_ref[...]
        lpart = jnp.sum(err * err, axis=0, keepdims=True)

        dy = err * (1.0 / D_MODEL)
        dy_ref[...] = dy
        gpart = jnp.sum(dy * n, axis=0, keepdims=True)
        dn = dy * pg
        dt = (r3 * (dn - n * jnp.mean(dn * n, axis=-1, keepdims=True))).astype(BF16)
        dmg = _nt(dt, wout_ref[...])
        dpa = (dmg * sgm).astype(BF16)
        dpd = (dmg * sgd).astype(BF16)
        dp_ref[:, 0:1024] = (dmg * pa * sgm * (1.0 - sgm)).astype(BF16)
        dp_ref[:, 1024:2048] = (dmg * pd * sgd * (1.0 - sgd)).astype(BF16)
        dua = _nt(dpa, wpm_ref[...])
        dud = _nt(dpd, wpd_ref[...])
        dya_ref[...] = dua * sm
        dyd_ref[...] = dud * sd
        dp_ref[:, 2048:2560] = (dua * ya * szm * (1.0 + zm * (1.0 - szm))).astype(BF16)
        dp_ref[:, 2560:3072] = (dud * yd * szd * (1.0 + zd * (1.0 - szd))).astype(BF16)

        wpm, wpd, wout = _tn(ua, dpa), _tn(ud, dpd), _tn(mg, dt)

        @pl.when(pl.program_id(0) == 0)
        def _():
            loss_ref[...] = lpart
            dgp_ref[...] = gpart
            dwpm_ref[...] = wpm
            dwpd_ref[...] = wpd
            dwout_ref[...] = wout

        @pl.when(pl.program_id(0) > 0)
        def _():
            loss_ref[...] += lpart
            dgp_ref[...] += gpart
            dwpm_ref[...] += wpm
            dwpd_ref[...] += wpd
            dwout_ref[...] += wout

    def rows(w):
        return pl.BlockSpec((tm, w), lambda i: (i, 0))

    def full(shape):
        return pl.BlockSpec(shape, lambda i: (0, 0))

    def sds(w, dt):
        return jax.ShapeDtypeStruct((SEQ, w), dt)

    return pl.pallas_call(
        body, name="tail", grid=(SEQ // tm,),
        in_specs=[rows(3072), rows(512), rows(512), rows(512), rows(512), rows(512), rows(512), rows(512),
                  rows(1024), rows(1024), full((512, 1024)), full((512, 1024)), full((1024, 1024)), full((1, 1024))],
        out_specs=[rows(3072), rows(1024), rows(512), rows(512), rows(512), rows(512), full((1, 1024)), full((1, 1024)),
                   full((512, 1024)), full((512, 1024)), full((1024, 1024))],
        out_shape=[sds(N_PAD, BF16), sds(1024, F32), sds(512, F32), sds(512, F32), sds(512, F32), sds(512, F32),
                   jax.ShapeDtypeStruct((1, 1024), F32), jax.ShapeDtypeStruct((1, 1024), F32),
                   jax.ShapeDtypeStruct((512, 1024), F32), jax.ShapeDtypeStruct((512, 1024), F32),
                   jax.ShapeDtypeStruct((1024, 1024), F32)],
        compiler_params=_cparams(),
    )(p, ya, o_g[0], o_g[1], o_g[2], l_g[0], l_g[1], l_g[2], x, target, wpm, wpd, wout, post_g)


def _sum_parts(recv, own, me, tr, name):
    n, r, w = recv.shape
    if r % tr:
        return _sum_parts_cols(recv, own, me, name)
    own_spec = (pl.BlockSpec((tr, w), lambda i, me_ref: (i, 0)) if own.ndim == 2
                else pl.BlockSpec((None, tr, w), lambda i, me_ref: (me_ref[0], i, 0)))

    def body(me_ref, p_ref, own_ref, o_ref):
        mine = own_ref[...].astype(F32)
        acc = jnp.zeros((tr, w), F32)
        for s in range(n):
            acc = acc + jnp.where(me_ref[0] == s, mine, p_ref[s].astype(F32))
        o_ref[...] = acc

    return pl.pallas_call(
        body, name=name,
        grid_spec=pltpu.PrefetchScalarGridSpec(
            num_scalar_prefetch=1, grid=(r // tr,),
            in_specs=[pl.BlockSpec((n, tr, w), lambda i, me_ref: (0, i, 0)), own_spec],
            out_specs=pl.BlockSpec((tr, w), lambda i, me_ref: (i, 0))),
        out_shape=jax.ShapeDtypeStruct((r, w), F32),
    )(me.reshape(1), recv, own)


def _sum_parts_cols(recv, own, me, name):
    n, r, w = recv.shape
    tc = 128

    def body(me_ref, p_ref, own_ref, o_ref):
        mine = own_ref[...].astype(F32)
        acc = jnp.zeros((r, tc), F32)
        for s in range(n):
            acc = acc + jnp.where(me_ref[0] == s, mine, p_ref[s].astype(F32))
        o_ref[...] = acc

    return pl.pallas_call(
        body, name=name,
        grid_spec=pltpu.PrefetchScalarGridSpec(
            num_scalar_prefetch=1, grid=(w // tc,),
            in_specs=[pl.BlockSpec((n, r, tc), lambda i, me_ref: (0, 0, i)),
                      pl.BlockSpec((None, r, tc), lambda i, me_ref: (me_ref[0], 0, i))],
            out_specs=pl.BlockSpec((r, tc), lambda i, me_ref: (0, i))),
        out_shape=jax.ShapeDtypeStruct((r, w), F32),
    )(me.reshape(1), recv, own)


def _adamw(w, g, m, v, name):
    lead = w.shape[:-2]
    r, c = w.shape[-2:]
    tr = max([t for t in range(8, 257, 8) if r % t == 0], default=r)
    c1 = 1.0 - ADAM_B1 ** ADAM_STEP
    c2 = 1.0 - ADAM_B2 ** ADAM_STEP

    def body(w_ref, g_ref, m_ref, v_ref, d_ref, nm_ref, nv_ref):
        gv = g_ref[...]
        nm = ADAM_B1 * m_ref[...] + (1.0 - ADAM_B1) * gv
        nv = ADAM_B2 * v_ref[...] + (1.0 - ADAM_B2) * (gv * gv)
        nm_ref[...] = nm
        nv_ref[...] = nv
        d_ref[...] = -ADAM_LR * ((nm / c1) / (jnp.sqrt(nv / c2) + ADAM_EPS) + ADAM_WD * w_ref[...])

    zeros = (0,) * len(lead)
    spec = pl.BlockSpec((1,) * len(lead) + (tr, c), lambda i: zeros + (i, 0))
    sd = jax.ShapeDtypeStruct(w.shape, F32)
    return pl.pallas_call(
        body, name=name, grid=(r // tr,),
        in_specs=[spec] * 4, out_specs=[spec] * 3, out_shape=[sd] * 3,
    )(w, g, m, v)


def _adamw_recv(w, m, v, recv, own, me, name):
    n, r, c = recv.shape
    tr = 128
    c1 = 1.0 - ADAM_B1 ** ADAM_STEP
    c2 = 1.0 - ADAM_B2 ** ADAM_STEP

    def body(me_ref, w_ref, m_ref, v_ref, p_ref, own_ref, d_ref, nm_ref, nv_ref, g_ref):
        mine = own_ref[...].astype(F32)
        gv = jnp.zeros((tr, c), F32)
        for s in range(n):
            gv = gv + jnp.where(me_ref[0] == s, mine, p_ref[s].astype(F32))
        g_ref[0] = gv
        nm = ADAM_B1 * m_ref[0] + (1.0 - ADAM_B1) * gv
        nv = ADAM_B2 * v_ref[0] + (1.0 - ADAM_B2) * (gv * gv)
        nm_ref[0] = nm
        nv_ref[0] = nv
        d_ref[0] = -ADAM_LR * ((nm / c1) / (jnp.sqrt(nv / c2) + ADAM_EPS) + ADAM_WD * w_ref[0])

    full = pl.BlockSpec((1, tr, c), lambda i, me_ref: (0, i, 0))
    sd = jax.ShapeDtypeStruct((1, r, c), F32)
    return pl.pallas_call(
        body, name=name,
        grid_spec=pltpu.PrefetchScalarGridSpec(
            num_scalar_prefetch=1, grid=(r // tr,),
            in_specs=[full, full, full, pl.BlockSpec((n, tr, c), lambda i, me_ref: (0, i, 0)),
                      pl.BlockSpec((None, tr, c), lambda i, me_ref: (me_ref[0], i, 0))],
            out_specs=[full] * 4),
        out_shape=[sd] * 4,
    )(me.reshape(1), w, m, v, recv, own)


def _adamw_in(w_t, m_t, v_t, own_half, swapped, core):
    r, c = SHARD_SHAPES[0]
    tr = max(t for t in range(8, 257, 8) if r % t == 0)
    c1 = 1.0 - ADAM_B1 ** ADAM_STEP
    c2 = 1.0 - ADAM_B2 ** ADAM_STEP

    def body(core_ref, w_ref, m_ref, v_ref, own_ref, sw_ref, d_ref, nm_ref, nv_ref, g_ref):
        own = own_ref[...]
        col_half = lax.broadcasted_iota(jnp.int32, (tr, c), 1) // (c // 2)
        gv = jnp.where(col_half == core_ref[0], jnp.concatenate([own, own], axis=1), sw_ref[...])
        g_ref[0] = gv
        nm = ADAM_B1 * m_ref[0] + (1.0 - ADAM_B1) * gv
        nv = ADAM_B2 * v_ref[0] + (1.0 - ADAM_B2) * (gv * gv)
        nm_ref[0] = nm
        nv_ref[0] = nv
        d_ref[0] = -ADAM_LR * ((nm / c1) / (jnp.sqrt(nv / c2) + ADAM_EPS) + ADAM_WD * w_ref[0])

    full = pl.BlockSpec((1, tr, c), lambda i, core_ref: (0, i, 0))
    sd = jax.ShapeDtypeStruct((1, r, c), F32)
    return pl.pallas_call(
        body, name="adamw_in",
        grid_spec=pltpu.PrefetchScalarGridSpec(
            num_scalar_prefetch=1, grid=(r // tr,),
            in_specs=[full, full, full, pl.BlockSpec((tr, c // 2), lambda i, core_ref: (i, 0)),
                      pl.BlockSpec((tr, c), lambda i, core_ref: (i, 0))],
            out_specs=[full] * 4),
        out_shape=[sd] * 4,
    )(core.reshape(1), w_t, m_t, v_t, own_half, swapped)


ANY = pl.BlockSpec(memory_space=pl.ANY)


def _my_place():
    return lax.axis_index("x"), lax.axis_index("y"), lax.axis_index("c")


HBM = pl.BlockSpec(memory_space=pltpu.HBM)
SEM = pl.BlockSpec(memory_space=pltpu.SEMAPHORE)
DATAFLOW = pltpu.SideEffectType.DATAFLOW_SIDE_EFFECTING


def _near_chips(x, y):
    return [(1 - x, y), (x, 1 - y)]


def _half(mi, hc):
    r, c = SHARD_SHAPES[mi]
    if mi == 0:
        return pl.ds(0, r), pl.ds(pl.multiple_of(hc * (c // 2), 128), c // 2)
    return pl.ds(pl.multiple_of(hc * (r // 2), 16), r // 2), pl.ds(0, c)


def _gather_copies(land_refs, send_sems, recv_sems):
    x, y, c = _my_place()
    out, back = [], []
    for mi in range(N_MATS):
        rows, cols = _half(mi, c)
        mine = land_refs[mi].at[2 * x + y, rows, cols]
        for j, (cx, cy) in enumerate(_near_chips(x, y)):
            sems = dict(send_sem=send_sems.at[mi * 2 + j], recv_sem=recv_sems.at[mi * 2 + j],
                        device_id=(cx, cy, c), device_id_type=MESH)
            out.append(pltpu.make_async_remote_copy(src_ref=mine, dst_ref=mine, **sems))
            got = land_refs[mi].at[2 * cx + cy, rows, cols]
            back.append(pltpu.make_async_remote_copy(src_ref=got, dst_ref=got, **sems))
    return out, back


def _gather_start(landing):
    n = N_MATS

    def body(*refs):
        out, _ = _gather_copies(refs[:n], refs[n], refs[n + 1])
        for cp in out:
            cp.start()
        refs[-1][...] = jnp.zeros_like(refs[-1])

    hbm = [pltpu.HBM(a.shape, a.dtype) for a in landing]
    outs = pl.pallas_call(
        body, name="gather_start",
        out_shape=(pltpu.SemaphoreType.DMA((2 * n,)), pltpu.SemaphoreType.DMA((2 * n,)), *hbm,
                   jax.ShapeDtypeStruct((8, 128), F32)),
        in_specs=[HBM] * n, out_specs=(SEM, SEM, *[HBM] * n, pl.BlockSpec(memory_space=pltpu.VMEM)),
        input_output_aliases={i: 2 + i for i in range(n)},
        compiler_params=pltpu.CompilerParams(has_side_effects=DATAFLOW),
    )(*[pltpu.with_memory_space_constraint(a, pltpu.HBM) for a in landing])
    return outs[:-1], outs[-1]


def _gather_wait(handle, after):
    n = N_MATS

    def body(*refs):
        out, back = _gather_copies(refs[:n], refs[n], refs[n + 1])
        for cp, arrival in zip(out, back):
            cp.wait_send()
            arrival.wait_recv()

    bufs = handle[2:]
    after, after_specs = _after(after)
    res = pl.pallas_call(
        body, name="gather_wait", out_shape=tuple(pltpu.HBM(b.shape, b.dtype) for b in bufs),
        in_specs=[HBM] * n + [SEM, SEM] + after_specs, out_specs=tuple([HBM] * n),
        input_output_aliases={i: i for i in range(n)},
        compiler_params=pltpu.CompilerParams(has_side_effects=DATAFLOW),
    )(*bufs, handle[0], handle[1], *after)
    return list(res)


def _relay_share(gathered):
    n = N_MATS

    def body(*refs):
        out_refs = refs[n:2 * n]
        send_sems, recv_sems = refs[2 * n:]
        x, y, c = _my_place()
        sibling = (x, y, 1 - c)
        relayed = 2 * (x ^ (1 - c)) + (y ^ c)
        relay_to = (x ^ c, y ^ (1 - c), c)
        far = 2 * (1 - x) + (1 - y)
        near = [2 * (1 - x) + y, 2 * x + (1 - y)]

        def copy(k, mi, shard, hc, to):
            blk = out_refs[mi].at[(shard,) + _half(mi, hc)]
            return pltpu.make_async_remote_copy(src_ref=blk, dst_ref=blk, send_sem=send_sems.at[mi * 4 + k],
                                                recv_sem=recv_sems.at[mi * 4 + k], device_id=to, device_id_type=MESH)

        sends = []
        for mi in range(n):
            sends.append(copy(0, mi, relayed, c, relay_to))
            sends += [copy(1 + j, mi, near[j], c, sibling) for j in range(2)]
        for cp in sends:
            cp.start()
        for mi in range(n):
            copy(0, mi, far, c, relay_to).wait_recv()
            cp = copy(3, mi, far, c, sibling)
            cp.start()
            sends.append(cp)
        for mi in range(n):
            for j in range(2):
                copy(1 + j, mi, near[j], 1 - c, sibling).wait_recv()
            copy(3, mi, far, 1 - c, sibling).wait_recv()
        for cp in sends:
            cp.wait_send()

    return pl.pallas_call(
        body, name="relay_share",
        in_specs=[ANY] * n, out_specs=[ANY] * n,
        out_shape=[jax.ShapeDtypeStruct(g.shape, g.dtype) for g in gathered],
        input_output_aliases={i: i for i in range(n)},
        scratch_shapes=[pltpu.SemaphoreType.DMA((4 * n,)), pltpu.SemaphoreType.DMA((4 * n,))],
    )(*gathered)


def _peers(x, y, c):
    out = []
    for k in range(1, 8):
        px, py, pc = x ^ (k >> 2), y ^ ((k >> 1) & 1), c ^ (k & 1)
        out.append((k - 1, (px, py, pc), 4 * px + 2 * py + pc))
    return out


def _exchange_start(parts, name):
    n = len(parts)

    def body(*refs):
        p_refs, land_refs = refs[:n], refs[n:2 * n]
        send_sems, recv_sems, token = refs[2 * n], refs[2 * n + 1], refs[-1]
        x, y, c = _my_place()
        me = 4 * x + 2 * y + c
        for k, dev, peer in _peers(x, y, c):
            for mi in range(n):
                pltpu.make_async_remote_copy(
                    src_ref=p_refs[mi].at[peer], dst_ref=land_refs[mi].at[me], send_sem=send_sems.at[k * n + mi],
                    recv_sem=recv_sems.at[k * n + mi], device_id=dev, device_id_type=MESH).start()
        token[...] = jnp.zeros_like(token)

    hbm = [pltpu.HBM(p.shape, p.dtype) for p in parts]
    outs = pl.pallas_call(
        body, name=name + "_start",
        out_shape=(pltpu.SemaphoreType.DMA((7 * n,)), pltpu.SemaphoreType.DMA((7 * n,)), *hbm, *hbm,
                   jax.ShapeDtypeStruct((8, 128), F32)),
        in_specs=[HBM] * (2 * n), out_specs=(SEM, SEM, *[HBM] * (2 * n), pl.BlockSpec(memory_space=pltpu.VMEM)),
        input_output_aliases={i: 2 + i for i in range(2 * n)},
        compiler_params=pltpu.CompilerParams(has_side_effects=DATAFLOW),
    )(*[pltpu.with_memory_space_constraint(p, pltpu.HBM) for p in parts],
      *[pltpu.with_memory_space_constraint(lax.empty(p.shape, p.dtype), pltpu.HBM) for p in parts])
    return (name, outs[:-1]), outs[-1]


def _exchange_wait(handle, after):
    name, outs = handle
    n = (len(outs) - 2) // 2

    def body(*refs):
        p_refs, land_refs = refs[:n], refs[n:2 * n]
        send_sems, recv_sems = refs[2 * n], refs[2 * n + 1]
        x, y, c = _my_place()
        me = 4 * x + 2 * y + c
        for k, dev, peer in _peers(x, y, c):
            for mi in range(n):
                pltpu.make_async_remote_copy(
                    src_ref=p_refs[mi].at[peer], dst_ref=land_refs[mi].at[me], send_sem=send_sems.at[k * n + mi],
                    recv_sem=recv_sems.at[k * n + mi], device_id=dev, device_id_type=MESH).wait_send()
                slot = land_refs[mi].at[peer]
                pltpu.make_async_remote_copy(
                    src_ref=slot, dst_ref=slot, send_sem=send_sems.at[k * n + mi],
                    recv_sem=recv_sems.at[k * n + mi], device_id=dev, device_id_type=MESH).wait_recv()

    bufs = outs[2:]
    res = pl.pallas_call(
        body, name=name + "_wait", out_shape=tuple(pltpu.HBM(b.shape, b.dtype) for b in bufs),
        in_specs=[HBM] * (2 * n) + [SEM, SEM, ANY], out_specs=tuple([HBM] * (2 * n)),
        input_output_aliases={i: i for i in range(2 * n)},
        compiler_params=pltpu.CompilerParams(has_side_effects=DATAFLOW),
    )(*bufs, outs[0], outs[1], after)
    return list(res[:n]), list(res[n:])


def _swap_halves(half_in, gvec):
    def body(g_ref, gv_ref, out_ref, rg_ref, send_sems, recv_sems):
        x, y, c = _my_place()
        me = 4 * x + 2 * y + c
        sibling = (x, y, 1 - c)

        def half(hc):
            return out_ref.at[:, pl.ds(pl.multiple_of(hc * 512, 128), 512)]

        sends = [pltpu.make_async_remote_copy(src_ref=g_ref, dst_ref=half(c), send_sem=send_sems.at[7],
                                              recv_sem=recv_sems.at[7], device_id=sibling, device_id_type=MESH)]
        for k, dev, peer in _peers(x, y, c):
            sends.append(pltpu.make_async_remote_copy(src_ref=gv_ref, dst_ref=rg_ref.at[me], send_sem=send_sems.at[k],
                                                      recv_sem=recv_sems.at[k], device_id=dev, device_id_type=MESH))
        for cp in sends:
            cp.start()
        got = half(1 - c)
        pltpu.make_async_remote_copy(src_ref=got, dst_ref=got, send_sem=send_sems.at[7], recv_sem=recv_sems.at[7],
                                     device_id=sibling, device_id_type=MESH).wait_recv()
        for k, dev, peer in _peers(x, y, c):
            got = rg_ref.at[peer]
            pltpu.make_async_remote_copy(src_ref=got, dst_ref=got, send_sem=send_sems.at[k], recv_sem=recv_sems.at[k],
                                         device_id=dev, device_id_type=MESH).wait_recv()
        for cp in sends:
            cp.wait_send()

    return pl.pallas_call(
        body, name="swap_halves",
        in_specs=[ANY, ANY], out_specs=[ANY, ANY],
        out_shape=[jax.ShapeDtypeStruct(SHARD_SHAPES[0], F32), jax.ShapeDtypeStruct((8, 8, N_GVEC), F32)],
        scratch_shapes=[pltpu.SemaphoreType.DMA((8,)), pltpu.SemaphoreType.DMA((8,))],
    )(half_in, gvec)


def _set_slot(arr, block, idx):
    return lax.dynamic_update_slice(arr, block[None], (idx,) + (0,) * block.ndim)


PAD_RUNS = ((6304, 8352, 0), (5280, 6304, COL_Z), (672, 5280, COL_QKV), (0, 640, COL_LAT), (640, 672, COL_LAT + 704))
N_QKV = COL_LAT - COL_QKV


def _qkv_rows_regroup(a, to_padded):
    if to_padded:
        a4 = a.reshape(3, 12, 128, a.shape[1])
        return jnp.stack([a4[0], a4[1], a4[2]], axis=1).reshape(a.shape)
    a4 = a.reshape(12, 3, 128, a.shape[1])
    return jnp.concatenate([a4[:, tq].reshape(N_QKV // 3, a.shape[1]) for tq in range(3)], axis=0)
W_IN_SHARD = 2088


def _full_weights(gathered):
    def cols(a):
        return jnp.concatenate([a[s] for s in range(4)], axis=1)

    w_uq, w_ukv, w_pm, w_pd = [cols(a) for a in gathered[1:5]]
    w_out = gathered[5].reshape(D_MODEL, D_MODEL)
    w_in_t = gathered[0].reshape(4 * W_IN_SHARD, D_MODEL)
    pieces, at = [], 0
    for lo, hi, pad_lo in sorted(PAD_RUNS, key=lambda t: t[2]):
        if pad_lo > at:
            pieces.append(jnp.zeros((pad_lo - at, D_MODEL), w_in_t.dtype))
        pieces.append(_qkv_rows_regroup(w_in_t[lo:hi], True) if pad_lo == COL_QKV else w_in_t[lo:hi])
        at = pad_lo + hi - lo
    pieces.append(jnp.zeros((N_PAD - at, D_MODEL), w_in_t.dtype))
    w_pad_t = jnp.concatenate(pieces, axis=0)
    z32 = jnp.zeros((Q_RANK, 32), w_uq.dtype)
    wuq_pad = jnp.concatenate([t for h in range(MLA_HEADS) for t in (w_uq[:, h * 96:(h + 1) * 96], z32)], axis=1)
    z64 = jnp.zeros((KV_RANK, 64), w_ukv.dtype)
    wk_pad = jnp.concatenate([t for h in range(MLA_HEADS) for t in (w_ukv[:, h * 128:h * 128 + 64], z64)], axis=1)
    wv = jnp.concatenate([w_ukv[:, h * 128 + 64:(h + 1) * 128] for h in range(MLA_HEADS)], axis=1)
    return w_pad_t, wuq_pad, wk_pad, wv, w_pm, w_pd, w_out


W_IN_LAT = 672


def _grad_parts_in_early(dwt_early):
    dwt_early = jnp.concatenate([dwt_early[:COL_QKV], _qkv_rows_regroup(dwt_early[COL_QKV:COL_LAT], False)], axis=0)

    def in_block(s, h):
        cols = slice(h * 512, (h + 1) * 512)
        out = []
        for lo, hi, pad_lo in sorted(PAD_RUNS):
            a_, b_ = max(lo, s * W_IN_SHARD), min(hi, (s + 1) * W_IN_SHARD)
            if a_ < b_:
                out.append(jnp.zeros((b_ - a_, 512), dwt_early.dtype) if pad_lo >= COL_LAT
                           else dwt_early[pad_lo + a_ - lo:pad_lo + b_ - lo, cols])
        return jnp.concatenate(out, axis=0)

    return jnp.stack([in_block(s, h) for s in range(4) for h in range(2)])


def _grad_parts_in_late(dwt_late):
    rows = jnp.concatenate([dwt_late[0:640], dwt_late[704:736]], axis=0)
    zero = jnp.zeros((W_IN_LAT, 512), dwt_late.dtype)
    return jnp.stack([rows[:, 0:512], rows[:, 512:1024]] + [zero] * 6)


def _shard_blocks(m, axis=1):
    n = m.shape[axis] // 4
    cut = (lambda s: m[:, s * n:(s + 1) * n]) if axis == 1 else (lambda s: m[s * n:(s + 1) * n])
    return jnp.stack([cut(s) for s in range(4) for _ in range(2)])


def _grad_parts_mla(dwuq_pad, dwk_pad, dwv):
    d_uq = jnp.concatenate([dwuq_pad[:, h * 128:h * 128 + 96] for h in range(MLA_HEADS)], axis=1)
    d_ukv = jnp.concatenate([t for h in range(MLA_HEADS) for t in (dwk_pad[:, h * 128:h * 128 + 64], dwv[:, h * 64:(h + 1) * 64])],
                            axis=1)
    return [_shard_blocks(d_uq.astype(BF16)), _shard_blocks(d_ukv.astype(BF16))]


def _rope_tables(positions, token=None):
    pos = positions.reshape(SEQ).astype(F32)
    if token is not None:
        pos = pos + token[0, 0]
    lane = jnp.arange(128)

    def table(rot, first, period):
        inv = ROPE_THETA ** (-jnp.arange(0, rot, 2, dtype=F32) / rot)
        half = rot // 2
        off = lane % period - first
        in1, in2 = (off >= 0) & (off < half), (off >= half) & (off < rot)
        inv_lane = jnp.where(in1 | in2, inv[jnp.clip(off % half, 0, half - 1)], 0.0)
        sign = jnp.where(in1, -1.0, 1.0).astype(F32)
        ang = pos[:, None] * inv_lane[None, :]
        return jnp.cos(ang), jnp.sin(ang) * sign[None, :]

    return table(32, 64, 128), table(16, 0, 64)


class _Links:
    def __init__(self, mats, chip, me):
        landing = [_set_slot(lax.empty((4,) + m.shape, m.dtype), m, chip) for m in mats]
        self.gather, self.token = _gather_start(landing)
        self.me, self.handles, self.sums, self.raw = me, {}, {}, {}

    def weights(self, after):
        return _relay_share(_gather_wait(self.gather, after))

    def send(self, blocks, name):
        self.handles[name], token = _exchange_start(blocks, name)
        return token

    def collect(self, name, after, parts):
        sent, recv = _exchange_wait(self.handles[name], after)
        for r, own, part in zip(recv, sent, parts):
            if part.startswith("in_"):
                self.sums[part] = _sum_parts(r, own, self.me, 64, "sum_grad_" + part)
            else:
                self.raw[part] = (r, own)
        return tuple(self.sums[part] for part in parts if part in self.sums)


def _device_grads(x, positions, target, gains, links):
    pre_g, q_g, kv_g, post_g = gains
    (mc, ms), (dc, ds) = _rope_tables(positions, links.token)
    h = _prenorm_fwd(x, pre_g, links.token)
    w_pad_t, wuq_pad, wk_pad, wv, w_pm, w_pd, w_out = _full_weights(links.weights((h, mc, ms, dc, ds)))

    p_gz = _matmul(h, w_pad_t, "nt", F32, 1024, 1536, 1024, "in_proj_gates", b_cols=(0, COL_QKV // 1536))
    p_qkv = _matmul(h, w_pad_t, "nt", F32, 1024, 1536, 1024, "in_proj_dilated", b_cols=(COL_QKV // 1536, N_QKV // 1536),
                    lane_blocks=True)
    p_lat = _matmul(h, w_pad_t, "nt", F32, 1024, N_LAT, 1024, "in_proj_latent", b_cols=(COL_LAT // N_LAT, 1))
    q, k, v = _mla_prep_fwd(p_lat, q_g, kv_g, wuq_pad, wk_pad, wv, mc, ms)
    ya, lse_m = _mla_flash_fwd(q, k, v)
    qkv = [_dil_prep_fwd(p_qkv, dc, ds, g) for g in range(3)]
    o_g, l_g = zip(*[_dil_attn_fwd(qkv[g], g) for g in range(3)])
    (dp, dy, dya, dyd, yd, lse_d, loss_cols, dg_post, dwpm, dwpd, dwout) = _tail(
        p_gz, ya, o_g, l_g, x, target, w_pm, w_pd, w_out, post_g)

    for g in range(3):
        dp = _dil_attn_bwd(dp, qkv[g], dyd, yd, lse_d, dc, ds, g)
    dw_early = _matmul(dp, h, "tn", BF16, 1536, 1024, 2048, "dw_in_early", a_cols=(0, COL_LAT // 1536))
    token = links.send([_grad_parts_in_early(dw_early), _shard_blocks(dwpm.astype(BF16)), _shard_blocks(dwpd.astype(BF16)),
                        _shard_blocks(dwout.astype(BF16), axis=0)], "exchange_early")

    dq, dk, dv = _mla_flash_bwd(q, k, v, ya, dya, lse_m, token)
    dp, dwuq_pad, dwk_pad, dwv, dg_q, dg_kv = _mla_prep_bwd(dp, p_lat, dq, dk, dv, q_g, kv_g, wuq_pad, wk_pad, wv, mc, ms)
    dw_late = _matmul(dp, h, "tn", BF16, N_LAT, 1024, 2048, "dw_in_late", a_cols=(COL_LAT // N_LAT, 1))
    token = links.send([_grad_parts_in_late(dw_late)] + _grad_parts_mla(dwuq_pad, dwk_pad, dwv), "exchange_late")
    early = links.collect("exchange_early", dw_late, ("in_early", "pm", "pd", "out"))

    grad_x, dg_pre = _dh_prenorm_bwd(dp, w_pad_t, x, dy, pre_g, (token,) + tuple(early))
    links.collect("exchange_late", grad_x, ("in_late", "uq", "ukv"))

    loss_part = jnp.pad((jnp.sum(loss_cols) * (0.5 / D_MODEL)).reshape(1, 1), ((0, 0), (0, N_GVEC - N_GAINS - 1)))
    gvec = jnp.concatenate([dg_pre, dg_q, dg_kv, dg_post, loss_part], axis=1)
    return grad_x, gvec


def kernel(x, positions, pre_norm_g, w_in, q_norm_g, w_uq, kv_norm_g, w_ukv, w_proj_mla, w_proj_dil, w_out, post_norm_g, loss_target, m_pre_norm_g, m_w_in, m_q_norm_g, m_w_uq, m_kv_norm_g, m_w_ukv, m_w_proj_mla, m_w_proj_dil, m_w_out, m_post_norm_g, v_pre_norm_g, v_w_in, v_q_norm_g, v_w_uq, v_kv_norm_g, v_w_ukv, v_w_proj_mla, v_w_proj_dil, v_w_out, v_post_norm_g):
    xi, yi, ci = _my_place()
    chip, me = 2 * xi + yi, 4 * xi + 2 * yi + ci
    mats = [jnp.swapaxes(w_in, 1, 2)] + [w_uq, w_ukv, w_proj_mla, w_proj_dil, w_out]
    mats = [w.reshape(w.shape[1:]).astype(BF16) for w in mats]
    links = _Links(mats, chip, me)
    gains = (pre_norm_g, q_norm_g, kv_norm_g, post_norm_g)
    grad_x, gvec = _device_grads(x[0], positions, loss_target[0], gains, links)

    sums = links.sums
    in_e = sums["in_early"]
    half_in = jnp.concatenate([in_e[:W_IN_LAT] + jnp.where(chip == 0, sums["in_late"], 0.0), in_e[W_IN_LAT:]], axis=0)
    gvec8 = jnp.pad(gvec, ((0, 7), (0, 0)))
    swapped_in, recv_gains = _swap_halves(half_in, gvec8)
    g_gains = _sum_parts(recv_gains, gvec8, me, 8, "sum_gain_parts")[0:1]
    loss = g_gains[0, N_GAINS]
    sw = lambda a: jnp.swapaxes(a, 1, 2)
    d_in, m_in, v_in, g_in = [sw(o) for o in _adamw_in(sw(w_in), sw(m_w_in), sw(v_w_in), half_in, swapped_in, ci)]
    off = [0, 1024, 1408, 1664, 2688]
    g_gain = [g_gains[:, off[i]:off[i + 1]] for i in range(4)]
    ws = [pre_norm_g, w_in, q_norm_g, w_uq, kv_norm_g, w_ukv, w_proj_mla, w_proj_dil, w_out, post_norm_g]
    ms = [m_pre_norm_g, m_w_in, m_q_norm_g, m_w_uq, m_kv_norm_g, m_w_ukv, m_w_proj_mla, m_w_proj_dil, m_w_out, m_post_norm_g]
    vs = [v_pre_norm_g, v_w_in, v_q_norm_g, v_w_uq, v_kv_norm_g, v_w_ukv, v_w_proj_mla, v_w_proj_dil, v_w_out, v_post_norm_g]
    part_of = [None, "in", None, "uq", None, "ukv", "pm", "pd", "out", None]
    gain_of = iter(g_gain)
    grads, deltas, new_m, new_v = [], [], [], []
    for i, (w, m, v, part) in enumerate(zip(ws, ms, vs, part_of)):
        if part == "in":
            d_, m_, v_, g = d_in, m_in, v_in, g_in
        elif part is not None:
            d_, m_, v_, g = _adamw_recv(w, m, v, *links.raw[part], me, f"adamw_{i}")
        else:
            g = next(gain_of)
            d_, m_, v_ = _adamw(w, g, m, v, f"adamw_{i}")
        grads.append(g)
        deltas.append(d_)
        new_m.append(m_)
        new_v.append(v_)
    return (loss, grad_x.reshape(x.shape), *grads, *deltas, *new_m, *new_v)
```

```python
import jax
import jax.numpy as jnp
from jax import lax
from jax.experimental import pallas as pl
from jax.experimental.pallas import tpu as pltpu

F32 = jnp.float32
BF16 = jnp.bfloat16

SEQ = 4096
D_MODEL = 1024
EPS = 1e-6
ROPE_THETA = 500000.0
MLA_HEADS = 8
Q_RANK = 384
KV_RANK = 256
MLA_SCALE = 96.0 ** -0.5
MLA_ROPE_HALF = 16
DIL_DILATIONS = (1, 4, 16)
DIL_ROPE_HALF = 8
DIL_SCALE = 0.125
BAND = 128

N_LAT = 768
COL_Z, COL_QKV, COL_LAT = 2048, 3072, 7680
N_PAD = 8448


def _qkv_block(tq, g, pr):
    return COL_QKV // 128 + (g * 4 + pr) * 3 + tq

IN_SPLITS = (384, 256, 32, 4608, 512, 512, 1024, 1024)

SHARD_SHAPES = ((2088, 1024), (384, 192), (256, 256), (512, 256), (512, 256), (256, 1024))
N_MATS = len(SHARD_SHAPES)
N_GAINS = 2688
N_GVEC = N_GAINS + 128

ADAM_LR, ADAM_B1, ADAM_B2, ADAM_EPS, ADAM_WD, ADAM_STEP = 0.001, 0.9, 0.999, 1e-08, 0.01, 10

VMEM_LIMIT = 56 * 1024 * 1024
NEG = -1e30
MESH = pl.DeviceIdType.MESH


def _cparams(**kw):
    return pltpu.CompilerParams(vmem_limit_bytes=VMEM_LIMIT, **kw)


def _dot(a, b, dims):
    return lax.dot_general(a, b, (dims, ((), ())), preferred_element_type=F32)


def _nn(a, b):
    return _dot(a, b, ((1,), (0,)))


def _nt(a, b):
    return _dot(a, b, ((1,), (1,)))


def _tn(a, b):
    return _dot(a, b, ((0,), (0,)))


def _rope_lanes(shape, half, period, first):
    lane = lax.broadcasted_iota(jnp.int32, shape, len(shape) - 1) % period
    return (lane >= first) & (lane < first + half), (lane >= first + half) & (lane < first + 2 * half)


def _rope_fwd(x, c, s, half, lanes):
    x1, _ = lanes
    return x * c + jnp.where(x1, pltpu.roll(x, 128 - half, 1), pltpu.roll(x, half, 1)) * s


def _rope_bwd(g, c, s, half, lanes):
    x1, x2 = lanes
    gs = g * s
    return g * c + jnp.where(x2, pltpu.roll(gs, half, 1), jnp.where(x1, pltpu.roll(gs, 128 - half, 1), 0.0))


def _sigmoid(x):
    return 1.0 / (1.0 + jnp.exp(-x))


def _after(token):
    tokens = [t for t in (token if isinstance(token, (tuple, list)) else [token]) if t is not None]
    return tokens, [pl.BlockSpec(memory_space=pl.ANY)] * len(tokens)


def _matmul(a, b, mode, out_dtype, tm, tn, tk, name, token=None, b_cols=None, a_cols=None, lane_blocks=False):
    after, after_specs = _after(token)
    if mode == "nn":
        (m, k), n = a.shape, b.shape[1]
        first = 0
        if b_cols is not None:
            first, n = b_cols[0], b_cols[1] * tn
        a_spec = pl.BlockSpec((tm, tk), lambda j, i, kk: (i, kk))
        b_spec = pl.BlockSpec((tk, tn), lambda j, i, kk: (kk, j + first))
        dot = _nn
    elif mode == "nt":
        (m, k), n = a.shape, b.shape[0]
        first = 0
        if b_cols is not None:
            first, n = b_cols[0], b_cols[1] * tn
        a_spec = pl.BlockSpec((tm, tk), lambda j, i, kk: (i, kk))
        b_spec = pl.BlockSpec((tn, tk), lambda j, i, kk: (j + first, kk))
        dot = _nt
    else:
        (k, m), n = a.shape, b.shape[1]
        first = 0
        if a_cols is not None:
            first, m = a_cols[0], a_cols[1] * tm
        a_spec = pl.BlockSpec((tk, tm), lambda j, i, kk: (kk, i + first))
        b_spec = pl.BlockSpec((tk, tn), lambda j, i, kk: (kk, j))
        dot = _tn
    assert m % tm == 0 and n % tn == 0 and k % tk == 0, (name, m, n, k, tm, tn, tk)
    nk = k // tk

    def body(a_ref, b_ref, *rest):
        o_ref, acc_ref = rest[-2:]
        kk = pl.program_id(2)
        part = dot(a_ref[...], b_ref[...])

        @pl.when(kk == 0)
        def _():
            acc_ref[...] = part

        @pl.when(kk > 0)
        def _():
            acc_ref[...] += part

        @pl.when(kk == nk - 1)
        def _():
            if lane_blocks:
                for blk in range(tn // 128):
                    o_ref[blk] = acc_ref[:, blk * 128:(blk + 1) * 128].astype(o_ref.dtype)
            else:
                o_ref[...] = acc_ref[...].astype(o_ref.dtype)

    if lane_blocks:
        out_spec = pl.BlockSpec((tn // 128, tm, 128), lambda j, i, kk: (j, i, 0))
        out_shape = jax.ShapeDtypeStruct((n // 128, m, 128), out_dtype)
    else:
        out_spec = pl.BlockSpec((tm, tn), lambda j, i, kk: (i, j))
        out_shape = jax.ShapeDtypeStruct((m, n), out_dtype)
    return pl.pallas_call(
        body, name=name, grid=(n // tn, m // tm, nk),
        in_specs=[a_spec, b_spec] + after_specs,
        out_specs=out_spec, out_shape=out_shape,
        scratch_shapes=[pltpu.VMEM((tm, tn), F32)],
        compiler_params=_cparams(),
    )(a, b, *after)


def _prenorm_fwd(x, g, token=None):
    tm = 512
    after, after_specs = _after(token)

    def body(x_ref, g_ref, *rest):
        xv = x_ref[...]
        r = lax.rsqrt(jnp.mean(xv * xv, axis=-1, keepdims=True) + EPS)
        rest[-1][...] = (xv * r * g_ref[...]).astype(BF16)

    return pl.pallas_call(
        body, name="prenorm_fwd", grid=(SEQ // tm,),
        in_specs=[pl.BlockSpec((tm, D_MODEL), lambda i: (i, 0)), pl.BlockSpec((1, D_MODEL), lambda i: (0, 0))] + after_specs,
        out_specs=pl.BlockSpec((tm, D_MODEL), lambda i: (i, 0)),
        out_shape=jax.ShapeDtypeStruct((SEQ, D_MODEL), BF16),
    )(x, g, *after)


def _dh_prenorm_bwd(dp, w_pad_t, x, dy, g, token=None):
    tm, tk = 1024, 1408
    nk = N_PAD // tk
    after, after_specs = _after(token)

    def body(a_ref, b_ref, x_ref, dy_ref, g_ref, *rest):
        gx_ref, dg_ref, acc_ref = rest[-3:]
        i, kk = pl.program_id(0), pl.program_id(1)
        part = _nn(a_ref[...], b_ref[...])

        @pl.when(kk == 0)
        def _():
            acc_ref[...] = part

        @pl.when(kk > 0)
        def _():
            acc_ref[...] += part

        @pl.when(kk == nk - 1)
        def _():
            xv = x_ref[...]
            r = lax.rsqrt(jnp.mean(xv * xv, axis=-1, keepdims=True) + EPS)
            n = xv * r
            dhv = acc_ref[...]
            dn = dhv * g_ref[...]
            gx_ref[...] = dy_ref[...] + r * (dn - n * jnp.mean(dn * n, axis=-1, keepdims=True))
            cols = jnp.sum(dhv * n, axis=0, keepdims=True)

            @pl.when(i == 0)
            def _():
                dg_ref[...] = cols

            @pl.when(i > 0)
            def _():
                dg_ref[...] += cols

    row = pl.BlockSpec((tm, D_MODEL), lambda i, kk: (i, 0))
    vec = pl.BlockSpec((1, D_MODEL), lambda i, kk: (0, 0))
    return pl.pallas_call(
        body, name="dh_prenorm_bwd", grid=(SEQ // tm, nk),
        in_specs=[pl.BlockSpec((tm, tk), lambda i, kk: (i, kk)), pl.BlockSpec((tk, D_MODEL), lambda i, kk: (kk, 0)),
                  row, row, vec] + after_specs,
        out_specs=[row, vec],
        out_shape=[jax.ShapeDtypeStruct((SEQ, D_MODEL), F32), jax.ShapeDtypeStruct((1, D_MODEL), F32)],
        scratch_shapes=[pltpu.VMEM((tm, D_MODEL), F32)],
        compiler_params=_cparams(),
    )(dp, w_pad_t, x, dy, g, *after)


def _mla_prep_fwd(p, qg, kvg, wuq, wk, wv, rc, rs):
    tm = 512

    def body(lat_ref, qg_ref, kvg_ref, wuq_ref, wk_ref, wv_ref, c_ref, s_ref, q_ref, k_ref, v_ref):
        c, s = c_ref[...], s_ref[...]
        lanes = _rope_lanes((tm, 128), MLA_ROPE_HALF, 128, 64)
        cq = lat_ref[:, 0:Q_RANK]
        r1 = lax.rsqrt(jnp.mean(cq * cq, axis=-1, keepdims=True) + EPS)
        cqn = (cq * r1 * qg_ref[...]).astype(BF16)
        q = _nn(cqn, wuq_ref[...])
        for h in range(MLA_HEADS):
            sl = slice(h * 128, (h + 1) * 128)
            q_ref[:, sl] = (_rope_fwd(q[:, sl], c, s, MLA_ROPE_HALF, lanes) * MLA_SCALE).astype(BF16)
        ckv = lat_ref[:, Q_RANK:Q_RANK + KV_RANK]
        r2 = lax.rsqrt(jnp.mean(ckv * ckv, axis=-1, keepdims=True) + EPS)
        ckvn = (ckv * r2 * kvg_ref[...]).astype(BF16)
        krr = _rope_fwd(lat_ref[:, Q_RANK + KV_RANK:N_LAT], c, s, MLA_ROPE_HALF, lanes)
        kn = _nn(ckvn, wk_ref[...])
        for h in range(MLA_HEADS):
            sl = slice(h * 128, (h + 1) * 128)
            k_ref[:, sl] = (kn[:, sl] + krr).astype(BF16)
        v_ref[...] = _nn(ckvn, wv_ref[...]).astype(BF16)

    def full(shape):
        return pl.BlockSpec(shape, lambda i: (0, 0))

    def rows(w):
        return pl.BlockSpec((tm, w), lambda i: (i, 0))

    return pl.pallas_call(
        body, name="mla_prep_fwd", grid=(SEQ // tm,),
        in_specs=[pl.BlockSpec((tm, N_LAT), lambda i: (i, 0)),
                  full((1, Q_RANK)), full((1, KV_RANK)), full((Q_RANK, 1024)), full((KV_RANK, 1024)),
                  full((KV_RANK, 512)), rows(128), rows(128)],
        out_specs=[rows(1024), rows(1024), rows(512)],
        out_shape=[jax.ShapeDtypeStruct((SEQ, 1024), BF16), jax.ShapeDtypeStruct((SEQ, 1024), BF16),
                   jax.ShapeDtypeStruct((SEQ, 512), BF16)],
        compiler_params=_cparams(),
    )(p, qg, kvg, wuq, wk, wv, rc, rs)


def _mla_prep_bwd(dp_in, p, dq, dk, dv, qg, kvg, wuq, wk, wv, rc, rs):
    tm = 512

    def body(dp_any, lat_ref, dq_ref, dk_ref, dv_ref, qg_ref, kvg_ref, wuq_ref, wk_ref, wv_ref,
             c_ref, s_ref, dp_ref, dwuq_ref, dwk_ref, dwv_ref, dgq_ref, dgkv_ref, dqb_ref, dkb_ref):
        del dp_any
        c, s = c_ref[...], s_ref[...]
        lanes = _rope_lanes((tm, 128), MLA_ROPE_HALF, 128, 64)
        lane = lax.broadcasted_iota(jnp.int32, (tm, 128), 1)
        dkr = jnp.zeros((tm, 128), F32)
        for h in range(MLA_HEADS):
            sl = slice(h * 128, (h + 1) * 128)
            dqb_ref[:, sl] = _rope_bwd(dq_ref[:, sl] * MLA_SCALE, c, s, MLA_ROPE_HALF, lanes).astype(BF16)
            dkh = dk_ref[:, sl]
            dkr = dkr + dkh
            dkb_ref[:, sl] = jnp.where(lane < 64, dkh, 0.0).astype(BF16)
        dkr = jnp.where((lane >= 64) & (lane < 96), dkr, 0.0)
        dkr = _rope_bwd(dkr, c, s, MLA_ROPE_HALF, lanes)
        dvb = dv_ref[...].astype(BF16)

        cq = lat_ref[:, 0:Q_RANK]
        r1 = lax.rsqrt(jnp.mean(cq * cq, axis=-1, keepdims=True) + EPS)
        n1 = cq * r1
        dcqn = _nt(dqb_ref[...], wuq_ref[...])
        dn1 = dcqn * qg_ref[...]
        dcq = r1 * (dn1 - n1 * jnp.mean(dn1 * n1, axis=-1, keepdims=True))
        pq = jnp.sum(dcqn * n1, axis=0, keepdims=True)

        ckv = lat_ref[:, Q_RANK:Q_RANK + KV_RANK]
        r2 = lax.rsqrt(jnp.mean(ckv * ckv, axis=-1, keepdims=True) + EPS)
        n2 = ckv * r2
        dckvn = _nt(dkb_ref[...], wk_ref[...]) + _nt(dvb, wv_ref[...])
        dn2 = dckvn * kvg_ref[...]
        dckv = r2 * (dn2 - n2 * jnp.mean(dn2 * n2, axis=-1, keepdims=True))
        pkv = jnp.sum(dckvn * n2, axis=0, keepdims=True)
        cqn = (n1 * qg_ref[...]).astype(BF16)
        ckvn = (n2 * kvg_ref[...]).astype(BF16)
        wq, wk_, wv_ = _tn(cqn, dqb_ref[...]), _tn(ckvn, dkb_ref[...]), _tn(ckvn, dvb)

        dp_ref[:, 0:Q_RANK] = dcq.astype(BF16)
        dp_ref[:, Q_RANK:Q_RANK + KV_RANK] = dckv.astype(BF16)
        dp_ref[:, Q_RANK + KV_RANK:N_LAT] = dkr.astype(BF16)

        @pl.when(pl.program_id(0) == 0)
        def _():
            dgq_ref[...] = pq
            dgkv_ref[...] = pkv
            dwuq_ref[...] = wq
            dwk_ref[...] = wk_
            dwv_ref[...] = wv_

        @pl.when(pl.program_id(0) > 0)
        def _():
            dgq_ref[...] += pq
            dgkv_ref[...] += pkv
            dwuq_ref[...] += wq
            dwk_ref[...] += wk_
            dwv_ref[...] += wv_

    def full(shape):
        return pl.BlockSpec(shape, lambda i: (0, 0))

    def rows(w):
        return pl.BlockSpec((tm, w), lambda i: (i, 0))

    lat = pl.BlockSpec((tm, N_LAT), lambda i: (i, 0))
    dlat = pl.BlockSpec((tm, N_LAT), lambda i: (i, COL_LAT // N_LAT))
    return pl.pallas_call(
        body, name="mla_prep_bwd", grid=(SEQ // tm,),
        in_specs=[pl.BlockSpec(memory_space=pl.ANY), lat, rows(1024), rows(1024), rows(512),
                  full((1, Q_RANK)), full((1, KV_RANK)), full((Q_RANK, 1024)), full((KV_RANK, 1024)),
                  full((KV_RANK, 512)), rows(128), rows(128)],
        out_specs=[dlat, full((Q_RANK, 1024)), full((KV_RANK, 1024)), full((KV_RANK, 512)),
                   full((1, Q_RANK)), full((1, KV_RANK))],
        out_shape=[jax.ShapeDtypeStruct((SEQ, N_PAD), BF16), jax.ShapeDtypeStruct((Q_RANK, 1024), F32),
                   jax.ShapeDtypeStruct((KV_RANK, 1024), F32), jax.ShapeDtypeStruct((KV_RANK, 512), F32),
                   jax.ShapeDtypeStruct((1, Q_RANK), F32), jax.ShapeDtypeStruct((1, KV_RANK), F32)],
        input_output_aliases={0: 0},
        scratch_shapes=[pltpu.VMEM((tm, 1024), BF16), pltpu.VMEM((tm, 1024), BF16)],
        compiler_params=_cparams(),
    )(dp_in, p, dq, dk, dv, qg, kvg, wuq, wk, wv, rc, rs)


FLASH_T = 1024


def _head_half(shape, hh):
    lane = lax.broadcasted_iota(jnp.int32, shape, 1)
    return (lane < 64) if hh == 0 else (lane >= 64)


def _diag_keep(nr, nk):
    row = lax.broadcasted_iota(jnp.int32, (nr, nk), 0)
    col = lax.broadcasted_iota(jnp.int32, (nr, nk), 1)
    return row + (nk - nr) >= col


def _tri_steps(nb, q_major):
    if q_major:
        pairs = [(i, kb) for i in range(nb) for kb in range(i + 1)]
    else:
        pairs = [(i, kb) for kb in range(nb) for i in range(kb, nb)]
    return jnp.asarray([p[0] for p in pairs], jnp.int32), jnp.asarray([p[1] for p in pairs], jnp.int32)


def _mla_flash_fwd(q, k, v):
    t = FLASH_T
    nb = SEQ // t
    qtab, ktab = _tri_steps(nb, True)

    def body(qi_ref, ki_ref, q_ref, k_ref, v_ref, o_ref, lse_ref, m_scr, l_scr, acc_scr):
        step = pl.program_id(1)
        i, kb = qi_ref[step], ki_ref[step]

        @pl.when(kb == 0)
        def _():
            m_scr[...] = jnp.full_like(m_scr, NEG)
            l_scr[...] = jnp.zeros_like(l_scr)
            acc_scr[...] = jnp.zeros_like(acc_scr)

        def update(r0, nr, nk, diagonal):
            rs = slice(r0, r0 + nr)
            vv = v_ref[0:nk, :]
            for hh in range(2):
                sl = slice(hh * 128, (hh + 1) * 128)
                s = _nt(q_ref[rs, sl], k_ref[0:nk, sl])
                if diagonal:
                    s = jnp.where(_diag_keep(nr, nk), s, NEG)
                m_prev = m_scr[hh, rs, :]
                m_new = jnp.maximum(m_prev, jnp.max(s, axis=-1, keepdims=True))
                pr = jnp.exp(s - jnp.tile(m_new, (1, nk // 128)))
                alpha = jnp.exp(m_prev - m_new)
                l_scr[hh, rs, :] = alpha * l_scr[hh, rs, :] + jnp.sum(pr, axis=-1, keepdims=True)
                acc_scr[hh, rs, :] = alpha * acc_scr[hh, rs, :] + _nn(pr.astype(BF16), vv)
                m_scr[hh, rs, :] = m_new

        @pl.when(kb < i)
        def _():
            update(0, t, t, False)

        @pl.when(kb == i)
        def _():
            update(0, t // 2, t // 2, True)
            update(t // 2, t // 2, t, True)
            o0 = acc_scr[0] / l_scr[0]
            o1 = acc_scr[1] / l_scr[1]
            o_ref[...] = jnp.where(_head_half((t, 128), 0), o0, o1)
            for hh in range(2):
                lse_ref[:, hh * 128:(hh + 1) * 128] = m_scr[hh] + jnp.log(l_scr[hh])

    grid_spec = pltpu.PrefetchScalarGridSpec(
        num_scalar_prefetch=2, grid=(4, qtab.shape[0]),
        in_specs=[pl.BlockSpec((t, 256), lambda j, s, qi, ki: (qi[s], j)),
                  pl.BlockSpec((t, 256), lambda j, s, qi, ki: (ki[s], j)),
                  pl.BlockSpec((t, 128), lambda j, s, qi, ki: (ki[s], j))],
        out_specs=[pl.BlockSpec((t, 128), lambda j, s, qi, ki: (qi[s], j)),
                   pl.BlockSpec((t, 256), lambda j, s, qi, ki: (qi[s], j))],
        scratch_shapes=[pltpu.VMEM((2, t, 128), F32), pltpu.VMEM((2, t, 128), F32), pltpu.VMEM((2, t, 128), F32)])
    return pl.pallas_call(
        body, name="mla_flash_fwd", grid_spec=grid_spec,
        out_shape=[jax.ShapeDtypeStruct((SEQ, 512), F32), jax.ShapeDtypeStruct((SEQ, 1024), F32)],
        compiler_params=_cparams(),
    )(qtab, ktab, q, k, v)


def _mla_flash_bwd(q, k, v, o, do, lse, token=None):
    t = FLASH_T
    nb = SEQ // t
    qtab, ktab = _tri_steps(nb, False)
    after, after_specs = _after(token)

    def body(qi_ref, ki_ref, q_ref, k_ref, v_ref, o_ref, do_ref, lse_ref, *rest):
        dq_ref, dk_ref, dv_ref, dk_scr, dv_scr = rest[-5:]
        step = pl.program_id(1)
        i, kb = qi_ref[step], ki_ref[step]

        @pl.when(step == 0)
        def _():
            dq_ref[...] = jnp.zeros_like(dq_ref)

        @pl.when(i == kb)
        def _():
            dk_scr[...] = jnp.zeros_like(dk_scr)
            dv_scr[...] = jnp.zeros_like(dv_scr)

        def update(r0, nr, nk, diagonal):
            rs = slice(r0, r0 + nr)
            vv = v_ref[0:nk, :]
            ov = o_ref[rs, :]
            dov = do_ref[rs, :]
            rows = pl.ds(pl.multiple_of(i * t + r0, t // 2), nr)
            for hh in range(2):
                sl = slice(hh * 128, (hh + 1) * 128)
                qh, kh = q_ref[rs, sl], k_ref[0:nk, sl]
                s = _nt(qh, kh)
                if diagonal:
                    s = jnp.where(_diag_keep(nr, nk), s, NEG)
                pr = jnp.exp(s - jnp.tile(lse_ref[rs, sl], (1, nk // 128)))
                dom = jnp.where(_head_half((nr, 128), hh), dov, 0.0)
                domb = dom.astype(BF16)
                dv_scr[0:nk, :] += _tn(pr.astype(BF16), domb)
                dpr = _nt(domb, vv)
                delta = jnp.sum(dom * ov, axis=-1, keepdims=True)
                ds = (pr * (dpr - delta)).astype(BF16)
                dq_ref[rows, sl] += _nn(ds, kh)
                dk_scr[hh, 0:nk, :] += _tn(ds, qh)

        @pl.when(i > kb)
        def _():
            update(0, t, t, False)

        @pl.when(i == kb)
        def _():
            update(0, t // 2, t // 2, True)
            update(t // 2, t // 2, t, True)

        @pl.when(i == nb - 1)
        def _():
            dk_ref[:, 0:128] = dk_scr[0]
            dk_ref[:, 128:256] = dk_scr[1]
            dv_ref[...] = dv_scr[...]

    qi_map = lambda j, s, qi, ki: (qi[s], j)
    ki_map = lambda j, s, qi, ki: (ki[s], j)
    grid_spec = pltpu.PrefetchScalarGridSpec(
        num_scalar_prefetch=2, grid=(4, qtab.shape[0]),
        in_specs=[pl.BlockSpec((t, 256), qi_map), pl.BlockSpec((t, 256), ki_map), pl.BlockSpec((t, 128), ki_map),
                  pl.BlockSpec((t, 128), qi_map), pl.BlockSpec((t, 128), qi_map), pl.BlockSpec((t, 256), qi_map)]
        + after_specs,
        out_specs=[pl.BlockSpec((SEQ, 256), lambda j, s, qi, ki: (0, j)), pl.BlockSpec((t, 256), ki_map),
                   pl.BlockSpec((t, 128), ki_map)],
        scratch_shapes=[pltpu.VMEM((2, t, 128), F32), pltpu.VMEM((t, 128), F32)])
    return pl.pallas_call(
        body, name="mla_flash_bwd", grid_spec=grid_spec,
        out_shape=[jax.ShapeDtypeStruct((SEQ, 1024), F32), jax.ShapeDtypeStruct((SEQ, 1024), F32),
                   jax.ShapeDtypeStruct((SEQ, 512), F32)],
        compiler_params=_cparams(),
    )(qtab, ktab, q, k, v, o, do, lse, *after)


def _strided(start, size, d):
    return pl.ds(start, size) if d == 1 else pl.ds(start, size, stride=d)


DIL_ST_FWD, DIL_ST_BWD = 1024, 2048


def _band_keep(g, b, t, nb):
    nbs = SEQ // DIL_DILATIONS[g] // BAND
    row = lax.broadcasted_iota(jnp.int32, (BAND, 2 * BAND), 0)
    col = lax.broadcasted_iota(jnp.int32, (BAND, 2 * BAND), 1)
    cur = (col >= BAND) & (row >= col - BAND)
    prev = (col < BAND) & (col >= row)
    if nbs >= nb:
        if b > 0:
            return cur | prev
        return cur | (prev & ((t * nb) % nbs != 0))
    return cur | prev if b % nbs else cur


def _dil_tok(g, b, t, nb):
    d = DIL_DILATIONS[g]
    nbs = SEQ // d // BAND
    gb = t * nb + b
    return _strided((gb % nbs) * BAND * d + gb // nbs, BAND, d)


def _dil_attn_fwd(p_qkv, rc, rs, g):
    d = DIL_DILATIONS[g]
    sub_len = SEQ // d
    ch = min(sub_len, 512)
    DIL_ST, DIL_NB = DIL_ST_FWD, DIL_ST_FWD // BAND

    def body(p_ref, c_ref, sn_ref, o_ref, l_ref, qkv_ref, x_scr, s_scr, p_scr, o_scr):
        t = pl.program_id(1)

        @pl.when(t == 0)
        def _():
            lanes = _rope_lanes((ch, 128), DIL_ROPE_HALF, 64, 0)
            for tq in range(3):
                qkv_ref[tq, 0, 0:BAND, :] = jnp.zeros((BAND, 128), BF16)
                mult = DIL_SCALE if tq == 0 else 1.0
                for c0 in range(0, SEQ, ch):
                    rows = pl.ds(c0, ch)
                    xv = p_ref[tq, rows, :]
                    x_scr[rows, :] = xv if tq == 2 else _rope_fwd(xv, c_ref[rows, :] * mult, sn_ref[rows, :] * mult,
                                                                   DIL_ROPE_HALF, lanes)
                for r in range(d):
                    for c0 in range(0, sub_len, ch):
                        at = BAND + r * sub_len + c0
                        qkv_ref[tq, 0, at:at + ch, :] = x_scr[_strided(r + c0 * d, ch, d), :].astype(BF16)

        base = t * DIL_ST
        half0 = _head_half((DIL_ST, 128), 0)
        lse_h = []
        for hh in range(2):
            half = _head_half((BAND, 128), hh)
            for b in range(DIL_NB):
                qv = qkv_ref[0, 0, pl.ds(pl.multiple_of(base + (b + 1) * BAND, BAND), BAND), :]
                k2 = qkv_ref[1, 0, pl.ds(pl.multiple_of(base + b * BAND, BAND), 2 * BAND), :]
                sb = _nt(jnp.where(half, qv, jnp.zeros_like(qv)), k2)
                s_scr[b * BAND:(b + 1) * BAND, :] = jnp.where(_band_keep(g, b, t, DIL_NB), sb, NEG)
            s = s_scr[...]
            m = jnp.max(s, axis=-1, keepdims=True)
            pr = jnp.exp(s - m)
            den = jnp.sum(pr, axis=-1, keepdims=True)
            p_scr[...] = pr.astype(BF16)
            for b in range(DIL_NB):
                v2 = qkv_ref[2, 0, pl.ds(pl.multiple_of(base + b * BAND, BAND), 2 * BAND), :]
                o_scr[hh, b * BAND:(b + 1) * BAND, :] = _nn(p_scr[b * BAND:(b + 1) * BAND, :], v2)
            o_scr[hh] = o_scr[hh] / den
            lse_h.append(m + jnp.log(den))
        out = jnp.where(half0, o_scr[0], o_scr[1])
        lse = jnp.where(half0, lse_h[0], lse_h[1])
        for b in range(DIL_NB):
            tok = _dil_tok(g, b, t, DIL_NB)
            o_ref[tok, :] = out[b * BAND:(b + 1) * BAND, :]
            l_ref[tok, :] = lse[b * BAND:(b + 1) * BAND, :]

    tab = pl.BlockSpec((SEQ, 128), lambda pr, t: (0, 0))
    out = pl.BlockSpec((SEQ, 128), lambda pr, t: (0, pr))
    return pl.pallas_call(
        body, name=f"dil_attn_fwd_g{g}", grid=(4, SEQ // DIL_ST),
        in_specs=[pl.BlockSpec((None, 3, SEQ, 128), lambda pr, t: (g * 4 + pr, 0, 0, 0)), tab, tab],
        out_specs=[out, out, pl.BlockSpec((3, 1, BAND + SEQ, 128), lambda pr, t: (0, pr, 0, 0))],
        out_shape=[jax.ShapeDtypeStruct((SEQ, 512), F32), jax.ShapeDtypeStruct((SEQ, 512), F32),
                   jax.ShapeDtypeStruct((3, 4, BAND + SEQ, 128), BF16)],
        scratch_shapes=[pltpu.VMEM((SEQ, 128), F32), pltpu.VMEM((DIL_ST, 2 * BAND), F32),
                        pltpu.VMEM((DIL_ST, 2 * BAND), BF16), pltpu.VMEM((2, DIL_ST, 128), F32)],
        compiler_params=_cparams(),
    )(p_qkv.reshape(12, 3, SEQ, 128), rc, rs)


def _dil_attn_bwd(dp_in, qkv, dyd, yd, lse_all, rc, rs, g, token=None):
    d = DIL_DILATIONS[g]
    sub_len = SEQ // d
    DIL_ST, DIL_NB = DIL_ST_BWD, DIL_ST_BWD // BAND
    nst = SEQ // DIL_ST
    after, after_specs = _after(token)
    ch = 512

    def body(dp_any, q_ref, k_ref, v_ref, do_ref, y_ref, l_ref, c_ref, sn_ref, *rest):
        dp_ref, tok_scr, dk_scr, dv_scr, s_scr, dp_scr, p_scr, ds_scr, do_scr, y_scr, l_scr, dq_scr = rest[-12:]
        del dp_any
        t = pl.program_id(1)
        base = t * DIL_ST

        @pl.when(t == 0)
        def _():
            dk_scr[...] = jnp.zeros_like(dk_scr)
            dv_scr[...] = jnp.zeros_like(dv_scr)

        for b in range(DIL_NB):
            tok = _dil_tok(g, b, t, DIL_NB)
            do_scr[b * BAND:(b + 1) * BAND, :] = do_ref[tok, :]
            y_scr[b * BAND:(b + 1) * BAND, :] = y_ref[tok, :]
            l_scr[b * BAND:(b + 1) * BAND, :] = l_ref[tok, :]
        for hh in range(2):
            half = _head_half((BAND, 128), hh)
            half_st = _head_half((DIL_ST, 128), hh)
            dom = jnp.where(half_st, do_scr[...], 0.0)
            delta = jnp.sum(dom * y_scr[...], axis=-1, keepdims=True)
            lcol = jnp.max(jnp.where(half_st, l_scr[...], NEG), axis=-1, keepdims=True)
            for b in range(DIL_NB):
                rows = slice(b * BAND, (b + 1) * BAND)
                qv = q_ref[0, 0, pl.ds(pl.multiple_of(base + (b + 1) * BAND, BAND), BAND), :]
                band = pl.ds(pl.multiple_of(base + b * BAND, BAND), 2 * BAND)
                sb = _nt(jnp.where(half, qv, jnp.zeros_like(qv)), k_ref[0, 0, band, :])
                s_scr[rows, :] = jnp.where(_band_keep(g, b, t, DIL_NB), sb, NEG)
                dp_scr[rows, :] = _nt(dom[rows, :].astype(BF16), v_ref[0, 0, band, :])
            pr = jnp.exp(s_scr[...] - lcol)
            p_scr[...] = pr.astype(BF16)
            ds_scr[...] = (pr * (dp_scr[...] - delta)).astype(BF16)
            for b in range(DIL_NB):
                rows = slice(b * BAND, (b + 1) * BAND)
                qv = q_ref[0, 0, pl.ds(pl.multiple_of(base + (b + 1) * BAND, BAND), BAND), :]
                band = pl.ds(pl.multiple_of(base + b * BAND, BAND), 2 * BAND)
                dqb = jnp.where(half, _nn(ds_scr[rows, :], k_ref[0, 0, band, :]), 0.0)
                if hh == 0:
                    dq_scr[rows, :] = dqb
                else:
                    dq_scr[rows, :] += dqb
                half2 = _head_half((2 * BAND, 128), hh)
                dk_scr[band, :] += jnp.where(half2, _tn(ds_scr[rows, :], qv), 0.0)
                dv_scr[band, :] += _tn(p_scr[rows, :], dom[rows, :].astype(BF16))
        for b in range(DIL_NB):
            tok_scr[pl.ds(0, 1), _dil_tok(g, b, t, DIL_NB), :] = dq_scr[b * BAND:(b + 1) * BAND, :][None]

        @pl.when(t == nst - 1)
        def _():
            for r in range(d):
                rows = _strided(r, sub_len, d)
                tok_scr[pl.ds(1, 1), rows, :] = dk_scr[BAND + r * sub_len:BAND + (r + 1) * sub_len, :][None]
                tok_scr[pl.ds(2, 1), rows, :] = dv_scr[BAND + r * sub_len:BAND + (r + 1) * sub_len, :][None]
            lanes = _rope_lanes((ch, 128), DIL_ROPE_HALF, 64, 0)
            for c0 in range(0, SEQ, ch):
                rows = slice(c0, c0 + ch)
                cv, sv = c_ref[rows, :], sn_ref[rows, :]
                dp_ref[rows, 0:128] = _rope_bwd(tok_scr[0, rows, :], cv * DIL_SCALE, sv * DIL_SCALE, DIL_ROPE_HALF, lanes).astype(BF16)
                dp_ref[rows, 128:256] = _rope_bwd(tok_scr[1, rows, :], cv, sv, DIL_ROPE_HALF, lanes).astype(BF16)
                dp_ref[rows, 256:384] = tok_scr[2, rows, :].astype(BF16)

    def inp(tq):
        return pl.BlockSpec((1, 1, BAND + SEQ, 128), lambda pr, t: (tq, pr, 0, 0))

    tok_spec = pl.BlockSpec((SEQ, 128), lambda pr, t: (0, pr))
    tab = pl.BlockSpec((SEQ, 128), lambda pr, t: (0, 0))
    st = (DIL_ST, 2 * BAND)
    return pl.pallas_call(
        body, name=f"dil_attn_bwd_g{g}", grid=(4, nst),
        in_specs=[pl.BlockSpec(memory_space=pl.ANY), inp(0), inp(1), inp(2), tok_spec, tok_spec, tok_spec, tab, tab]
        + after_specs,
        out_specs=pl.BlockSpec((SEQ, 384), lambda pr, t: (0, _qkv_block(0, g, pr) // 3)),
        out_shape=jax.ShapeDtypeStruct((SEQ, N_PAD), BF16),
        input_output_aliases={0: 0},
        scratch_shapes=[pltpu.VMEM((3, SEQ, 128), F32),
                        pltpu.VMEM((BAND + SEQ, 128), F32), pltpu.VMEM((BAND + SEQ, 128), F32),
                        pltpu.VMEM(st, F32), pltpu.VMEM(st, F32), pltpu.VMEM(st, BF16), pltpu.VMEM(st, BF16),
                        pltpu.VMEM((DIL_ST, 128), F32), pltpu.VMEM((DIL_ST, 128), F32), pltpu.VMEM((DIL_ST, 128), F32),
                        pltpu.VMEM((DIL_ST, 128), F32)],
        compiler_params=_cparams(),
    )(dp_in, qkv, qkv, qkv, dyd, yd, lse_all, rc, rs, *after)


TAIL_T = 256


def _tail(p, ya, o_g, l_g, x, target, wpm, wpd, wout, post_g):
    tm = TAIL_T

    def body(pgz_ref, ya_ref, o0_ref, o1_ref, o2_ref, l0_ref, l1_ref, l2_ref, x_ref, t_ref,
             wpm_ref, wpd_ref, wout_ref, pg_ref,
             dp_ref, dy_ref, dya_ref, dyd_ref, yd_ref, lse_ref, loss_ref, dgp_ref, dwpm_ref, dwpd_ref, dwout_ref):
        l0, l1, l2 = l0_ref[...], l1_ref[...], l2_ref[...]
        mx = jnp.maximum(jnp.maximum(l0, l1), l2)
        e0, e1, e2 = jnp.exp(l0 - mx), jnp.exp(l1 - mx), jnp.exp(l2 - mx)
        den = e0 + e1 + e2
        yd = (e0 * o0_ref[...] + e1 * o1_ref[...] + e2 * o2_ref[...]) / den
        yd_ref[...] = yd
        lse_ref[...] = mx + jnp.log(den)
        ya = ya_ref[...]

        gm, gd = pgz_ref[:, 0:1024], pgz_ref[:, 1024:2048]
        zm, zd = pgz_ref[:, 2048:2560], pgz_ref[:, 2560:3072]
        szm, szd = _sigmoid(zm), _sigmoid(zd)
        sm, sd = zm * szm, zd * szd
        ua = (ya * sm).astype(BF16)
        ud = (yd * sd).astype(BF16)
        pa = _nn(ua, wpm_ref[...])
        pd = _nn(ud, wpd_ref[...])
        sgm, sgd = _sigmoid(gm), _sigmoid(gd)
        mg = (sgm * pa + sgd * pd).astype(BF16)
        t = _nn(mg, wout_ref[...])
        r3 = lax.rsqrt(jnp.mean(t * t, axis=-1, keepdims=True) + EPS)
        n = t * r3
        pg = pg_ref[...]
        err = x_ref[...] + n * pg - t_ref[...]
        lpart = jnp.sum(err * err, axis=0, keepdims=True)

        dy = err * (1.0 / D_MODEL)
        dy_ref[...] = dy
        gpart = jnp.sum(dy * n, axis=0, keepdims=True)
        dn = dy * pg
        dt = (r3 * (dn - n * jnp.mean(dn * n, axis=-1, keepdims=True))).astype(BF16)
        dmg = _nt(dt, wout_ref[...])
        dpa = (dmg * sgm).astype(BF16)
        dpd = (dmg * sgd).astype(BF16)
        dp_ref[:, 0:1024] = (dmg * pa * sgm * (1.0 - sgm)).astype(BF16)
        dp_ref[:, 1024:2048] = (dmg * pd * sgd * (1.0 - sgd)).astype(BF16)
        dua = _nt(dpa, wpm_ref[...])
        dud = _nt(dpd, wpd_ref[...])
        dya_ref[...] = dua * sm
        dyd_ref[...] = dud * sd
        dp_ref[:, 2048:2560] = (dua * ya * szm * (1.0 + zm * (1.0 - szm))).astype(BF16)
        dp_ref[:, 2560:3072] = (dud * yd * szd * (1.0 + zd * (1.0 - szd))).astype(BF16)

        wpm, wpd, wout = _tn(ua, dpa), _tn(ud, dpd), _tn(mg, dt)

        @pl.when(pl.program_id(0) == 0)
        def _():
            loss_ref[...] = lpart
            dgp_ref[...] = gpart
            dwpm_ref[...] = wpm
            dwpd_ref[...] = wpd
            dwout_ref[...] = wout

        @pl.when(pl.program_id(0) > 0)
        def _():
            loss_ref[...] += lpart
            dgp_ref[...] += gpart
            dwpm_ref[...] += wpm
            dwpd_ref[...] += wpd
            dwout_ref[...] += wout

    def rows(w):
        return pl.BlockSpec((tm, w), lambda i: (i, 0))

    def full(shape):
        return pl.BlockSpec(shape, lambda i: (0, 0))

    def sds(w, dt):
        return jax.ShapeDtypeStruct((SEQ, w), dt)

    return pl.pallas_call(
        body, name="tail", grid=(SEQ // tm,),
        in_specs=[rows(3072), rows(512), rows(512), rows(512), rows(512), rows(512), rows(512), rows(512),
                  rows(1024), rows(1024), full((512, 1024)), full((512, 1024)), full((1024, 1024)), full((1, 1024))],
        out_specs=[rows(3072), rows(1024), rows(512), rows(512), rows(512), rows(512), full((1, 1024)), full((1, 1024)),
                   full((512, 1024)), full((512, 1024)), full((1024, 1024))],
        out_shape=[sds(N_PAD, BF16), sds(1024, F32), sds(512, F32), sds(512, F32), sds(512, F32), sds(512, F32),
                   jax.ShapeDtypeStruct((1, 1024), F32), jax.ShapeDtypeStruct((1, 1024), F32),
                   jax.ShapeDtypeStruct((512, 1024), F32), jax.ShapeDtypeStruct((512, 1024), F32),
                   jax.ShapeDtypeStruct((1024, 1024), F32)],
        compiler_params=_cparams(),
    )(p, ya, o_g[0], o_g[1], o_g[2], l_g[0], l_g[1], l_g[2], x, target, wpm, wpd, wout, post_g)


def _sum_parts(recv, own, me, tr, name):
    n, r, w = recv.shape
    if r % tr:
        return _sum_parts_cols(recv, own, me, name)
    own_spec = (pl.BlockSpec((tr, w), lambda i, me_ref: (i, 0)) if own.ndim == 2
                else pl.BlockSpec((None, tr, w), lambda i, me_ref: (me_ref[0], i, 0)))

    def body(me_ref, p_ref, own_ref, o_ref):
        mine = own_ref[...].astype(F32)
        acc = jnp.zeros((tr, w), F32)
        for s in range(n):
            acc = acc + jnp.where(me_ref[0] == s, mine, p_ref[s].astype(F32))
        o_ref[...] = acc

    return pl.pallas_call(
        body, name=name,
        grid_spec=pltpu.PrefetchScalarGridSpec(
            num_scalar_prefetch=1, grid=(r // tr,),
            in_specs=[pl.BlockSpec((n, tr, w), lambda i, me_ref: (0, i, 0)), own_spec],
            out_specs=pl.BlockSpec((tr, w), lambda i, me_ref: (i, 0))),
        out_shape=jax.ShapeDtypeStruct((r, w), F32),
    )(me.reshape(1), recv, own)


def _sum_parts_cols(recv, own, me, name):
    n, r, w = recv.shape
    tc = 128

    def body(me_ref, p_ref, own_ref, o_ref):
        mine = own_ref[...].astype(F32)
        acc = jnp.zeros((r, tc), F32)
        for s in range(n):
            acc = acc + jnp.where(me_ref[0] == s, mine, p_ref[s].astype(F32))
        o_ref[...] = acc

    return pl.pallas_call(
        body, name=name,
        grid_spec=pltpu.PrefetchScalarGridSpec(
            num_scalar_prefetch=1, grid=(w // tc,),
            in_specs=[pl.BlockSpec((n, r, tc), lambda i, me_ref: (0, 0, i)),
                      pl.BlockSpec((None, r, tc), lambda i, me_ref: (me_ref[0], 0, i))],
            out_specs=pl.BlockSpec((r, tc), lambda i, me_ref: (0, i))),
        out_shape=jax.ShapeDtypeStruct((r, w), F32),
    )(me.reshape(1), recv, own)


def _adamw(w, g, m, v, name):
    lead = w.shape[:-2]
    r, c = w.shape[-2:]
    tr = max([t for t in range(8, 257, 8) if r % t == 0], default=r)
    c1 = 1.0 - ADAM_B1 ** ADAM_STEP
    c2 = 1.0 - ADAM_B2 ** ADAM_STEP

    def body(w_ref, g_ref, m_ref, v_ref, d_ref, nm_ref, nv_ref):
        gv = g_ref[...]
        nm = ADAM_B1 * m_ref[...] + (1.0 - ADAM_B1) * gv
        nv = ADAM_B2 * v_ref[...] + (1.0 - ADAM_B2) * (gv * gv)
        nm_ref[...] = nm
        nv_ref[...] = nv
        d_ref[...] = -ADAM_LR * ((nm / c1) / (jnp.sqrt(nv / c2) + ADAM_EPS) + ADAM_WD * w_ref[...])

    zeros = (0,) * len(lead)
    spec = pl.BlockSpec((1,) * len(lead) + (tr, c), lambda i: zeros + (i, 0))
    sd = jax.ShapeDtypeStruct(w.shape, F32)
    return pl.pallas_call(
        body, name=name, grid=(r // tr,),
        in_specs=[spec] * 4, out_specs=[spec] * 3, out_shape=[sd] * 3,
    )(w, g, m, v)


def _adamw_recv(w, m, v, recv, own, me, name):
    n, r, c = recv.shape
    tr = 128
    c1 = 1.0 - ADAM_B1 ** ADAM_STEP
    c2 = 1.0 - ADAM_B2 ** ADAM_STEP

    def body(me_ref, w_ref, m_ref, v_ref, p_ref, own_ref, d_ref, nm_ref, nv_ref, g_ref):
        mine = own_ref[...].astype(F32)
        gv = jnp.zeros((tr, c), F32)
        for s in range(n):
            gv = gv + jnp.where(me_ref[0] == s, mine, p_ref[s].astype(F32))
        g_ref[0] = gv
        nm = ADAM_B1 * m_ref[0] + (1.0 - ADAM_B1) * gv
        nv = ADAM_B2 * v_ref[0] + (1.0 - ADAM_B2) * (gv * gv)
        nm_ref[0] = nm
        nv_ref[0] = nv
        d_ref[0] = -ADAM_LR * ((nm / c1) / (jnp.sqrt(nv / c2) + ADAM_EPS) + ADAM_WD * w_ref[0])

    full = pl.BlockSpec((1, tr, c), lambda i, me_ref: (0, i, 0))
    sd = jax.ShapeDtypeStruct((1, r, c), F32)
    return pl.pallas_call(
        body, name=name,
        grid_spec=pltpu.PrefetchScalarGridSpec(
            num_scalar_prefetch=1, grid=(r // tr,),
            in_specs=[full, full, full, pl.BlockSpec((n, tr, c), lambda i, me_ref: (0, i, 0)),
                      pl.BlockSpec((None, tr, c), lambda i, me_ref: (me_ref[0], i, 0))],
            out_specs=[full] * 4),
        out_shape=[sd] * 4,
    )(me.reshape(1), w, m, v, recv, own)


def _adamw_in(w_t, m_t, v_t, own_half, swapped, core):
    r, c = SHARD_SHAPES[0]
    tr = max(t for t in range(8, 257, 8) if r % t == 0)
    c1 = 1.0 - ADAM_B1 ** ADAM_STEP
    c2 = 1.0 - ADAM_B2 ** ADAM_STEP

    def body(core_ref, w_ref, m_ref, v_ref, own_ref, sw_ref, d_ref, nm_ref, nv_ref, g_ref):
        own = own_ref[...]
        col_half = lax.broadcasted_iota(jnp.int32, (tr, c), 1) // (c // 2)
        gv = jnp.where(col_half == core_ref[0], jnp.concatenate([own, own], axis=1), sw_ref[...])
        g_ref[0] = gv
        nm = ADAM_B1 * m_ref[0] + (1.0 - ADAM_B1) * gv
        nv = ADAM_B2 * v_ref[0] + (1.0 - ADAM_B2) * (gv * gv)
        nm_ref[0] = nm
        nv_ref[0] = nv
        d_ref[0] = -ADAM_LR * ((nm / c1) / (jnp.sqrt(nv / c2) + ADAM_EPS) + ADAM_WD * w_ref[0])

    full = pl.BlockSpec((1, tr, c), lambda i, core_ref: (0, i, 0))
    sd = jax.ShapeDtypeStruct((1, r, c), F32)
    return pl.pallas_call(
        body, name="adamw_in",
        grid_spec=pltpu.PrefetchScalarGridSpec(
            num_scalar_prefetch=1, grid=(r // tr,),
            in_specs=[full, full, full, pl.BlockSpec((tr, c // 2), lambda i, core_ref: (i, 0)),
                      pl.BlockSpec((tr, c), lambda i, core_ref: (i, 0))],
            out_specs=[full] * 4),
        out_shape=[sd] * 4,
    )(core.reshape(1), w_t, m_t, v_t, own_half, swapped)


ANY = pl.BlockSpec(memory_space=pl.ANY)


def _my_place():
    return lax.axis_index("x"), lax.axis_index("y"), lax.axis_index("c")


HBM = pl.BlockSpec(memory_space=pltpu.HBM)
SEM = pl.BlockSpec(memory_space=pltpu.SEMAPHORE)
DATAFLOW = pltpu.SideEffectType.DATAFLOW_SIDE_EFFECTING


def _near_chips(x, y):
    return [(1 - x, y), (x, 1 - y)]


def _half(mi, hc):
    r, c = SHARD_SHAPES[mi]
    if mi == 0:
        return pl.ds(0, r), pl.ds(pl.multiple_of(hc * (c // 2), 128), c // 2)
    return pl.ds(pl.multiple_of(hc * (r // 2), 16), r // 2), pl.ds(0, c)


def _gather_copies(land_refs, send_sems, recv_sems):
    x, y, c = _my_place()
    out, back = [], []
    for mi in range(N_MATS):
        rows, cols = _half(mi, c)
        mine = land_refs[mi].at[2 * x + y, rows, cols]
        for j, (cx, cy) in enumerate(_near_chips(x, y)):
            sems = dict(send_sem=send_sems.at[mi * 2 + j], recv_sem=recv_sems.at[mi * 2 + j],
                        device_id=(cx, cy, c), device_id_type=MESH)
            out.append(pltpu.make_async_remote_copy(src_ref=mine, dst_ref=mine, **sems))
            got = land_refs[mi].at[2 * cx + cy, rows, cols]
            back.append(pltpu.make_async_remote_copy(src_ref=got, dst_ref=got, **sems))
    return out, back


def _gather_start(landing):
    n = N_MATS

    def body(*refs):
        out, _ = _gather_copies(refs[:n], refs[n], refs[n + 1])
        for cp in out:
            cp.start()
        refs[-1][...] = jnp.zeros_like(refs[-1])

    hbm = [pltpu.HBM(a.shape, a.dtype) for a in landing]
    outs = pl.pallas_call(
        body, name="gather_start",
        out_shape=(pltpu.SemaphoreType.DMA((2 * n,)), pltpu.SemaphoreType.DMA((2 * n,)), *hbm,
                   jax.ShapeDtypeStruct((8, 128), F32)),
        in_specs=[HBM] * n, out_specs=(SEM, SEM, *[HBM] * n, pl.BlockSpec(memory_space=pltpu.VMEM)),
        input_output_aliases={i: 2 + i for i in range(n)},
        compiler_params=pltpu.CompilerParams(has_side_effects=DATAFLOW),
    )(*[pltpu.with_memory_space_constraint(a, pltpu.HBM) for a in landing])
    return outs[:-1], outs[-1]


def _gather_wait(handle, after):
    n = N_MATS

    def body(*refs):
        out, back = _gather_copies(refs[:n], refs[n], refs[n + 1])
        for cp, arrival in zip(out, back):
            cp.wait_send()
            arrival.wait_recv()

    bufs = handle[2:]
    after, after_specs = _after(after)
    res = pl.pallas_call(
        body, name="gather_wait", out_shape=tuple(pltpu.HBM(b.shape, b.dtype) for b in bufs),
        in_specs=[HBM] * n + [SEM, SEM] + after_specs, out_specs=tuple([HBM] * n),
        input_output_aliases={i: i for i in range(n)},
        compiler_params=pltpu.CompilerParams(has_side_effects=DATAFLOW),
    )(*bufs, handle[0], handle[1], *after)
    return list(res)


def _relay_share(gathered):
    n = N_MATS

    def body(*refs):
        out_refs = refs[n:2 * n]
        send_sems, recv_sems = refs[2 * n:]
        x, y, c = _my_place()
        sibling = (x, y, 1 - c)
        relayed = 2 * (x ^ (1 - c)) + (y ^ c)
        relay_to = (x ^ c, y ^ (1 - c), c)
        far = 2 * (1 - x) + (1 - y)
        near = [2 * (1 - x) + y, 2 * x + (1 - y)]

        def copy(k, mi, shard, hc, to):
            blk = out_refs[mi].at[(shard,) + _half(mi, hc)]
            return pltpu.make_async_remote_copy(src_ref=blk, dst_ref=blk, send_sem=send_sems.at[mi * 4 + k],
                                                recv_sem=recv_sems.at[mi * 4 + k], device_id=to, device_id_type=MESH)

        sends = []
        for mi in range(n):
            sends.append(copy(0, mi, relayed, c, relay_to))
            sends += [copy(1 + j, mi, near[j], c, sibling) for j in range(2)]
        for cp in sends:
            cp.start()
        for mi in range(n):
            copy(0, mi, far, c, relay_to).wait_recv()
            cp = copy(3, mi, far, c, sibling)
            cp.start()
            sends.append(cp)
        for mi in range(n):
            for j in range(2):
                copy(1 + j, mi, near[j], 1 - c, sibling).wait_recv()
            copy(3, mi, far, 1 - c, sibling).wait_recv()
        for cp in sends:
            cp.wait_send()

    return pl.pallas_call(
        body, name="relay_share",
        in_specs=[ANY] * n, out_specs=[ANY] * n,
        out_shape=[jax.ShapeDtypeStruct(g.shape, g.dtype) for g in gathered],
        input_output_aliases={i: i for i in range(n)},
        scratch_shapes=[pltpu.SemaphoreType.DMA((4 * n,)), pltpu.SemaphoreType.DMA((4 * n,))],
    )(*gathered)


def _peers(x, y, c):
    out = []
    for k in range(1, 8):
        px, py, pc = x ^ (k >> 2), y ^ ((k >> 1) & 1), c ^ (k & 1)
        out.append((k - 1, (px, py, pc), 4 * px + 2 * py + pc))
    return out


def _exchange_start(parts, name):
    n = len(parts)

    def body(*refs):
        p_refs, land_refs = refs[:n], refs[n:2 * n]
        send_sems, recv_sems, token = refs[2 * n], refs[2 * n + 1], refs[-1]
        x, y, c = _my_place()
        me = 4 * x + 2 * y + c
        for k, dev, peer in _peers(x, y, c):
            for mi in range(n):
                pltpu.make_async_remote_copy(
                    src_ref=p_refs[mi].at[peer], dst_ref=land_refs[mi].at[me], send_sem=send_sems.at[k * n + mi],
                    recv_sem=recv_sems.at[k * n + mi], device_id=dev, device_id_type=MESH).start()
        token[...] = jnp.zeros_like(token)

    hbm = [pltpu.HBM(p.shape, p.dtype) for p in parts]
    outs = pl.pallas_call(
        body, name=name + "_start",
        out_shape=(pltpu.SemaphoreType.DMA((7 * n,)), pltpu.SemaphoreType.DMA((7 * n,)), *hbm, *hbm,
                   jax.ShapeDtypeStruct((8, 128), F32)),
        in_specs=[HBM] * (2 * n), out_specs=(SEM, SEM, *[HBM] * (2 * n), pl.BlockSpec(memory_space=pltpu.VMEM)),
        input_output_aliases={i: 2 + i for i in range(2 * n)},
        compiler_params=pltpu.CompilerParams(has_side_effects=DATAFLOW),
    )(*[pltpu.with_memory_space_constraint(p, pltpu.HBM) for p in parts],
      *[pltpu.with_memory_space_constraint(lax.empty(p.shape, p.dtype), pltpu.HBM) for p in parts])
    return (name, outs[:-1]), outs[-1]


def _exchange_wait(handle, after):
    name, outs = handle
    n = (len(outs) - 2) // 2

    def body(*refs):
        p_refs, land_refs = refs[:n], refs[n:2 * n]
        send_sems, recv_sems = refs[2 * n], refs[2 * n + 1]
        x, y, c = _my_place()
        me = 4 * x + 2 * y + c
        for k, dev, peer in _peers(x, y, c):
            for mi in range(n):
                pltpu.make_async_remote_copy(
                    src_ref=p_refs[mi].at[peer], dst_ref=land_refs[mi].at[me], send_sem=send_sems.at[k * n + mi],
                    recv_sem=recv_sems.at[k * n + mi], device_id=dev, device_id_type=MESH).wait_send()
                slot = land_refs[mi].at[peer]
                pltpu.make_async_remote_copy(
                    src_ref=slot, dst_ref=slot, send_sem=send_sems.at[k * n + mi],
                    recv_sem=recv_sems.at[k * n + mi], device_id=dev, device_id_type=MESH).wait_recv()

    bufs = outs[2:]
    res = pl.pallas_call(
        body, name=name + "_wait", out_shape=tuple(pltpu.HBM(b.shape, b.dtype) for b in bufs),
        in_specs=[HBM] * (2 * n) + [SEM, SEM, ANY], out_specs=tuple([HBM] * (2 * n)),
        input_output_aliases={i: i for i in range(2 * n)},
        compiler_params=pltpu.CompilerParams(has_side_effects=DATAFLOW),
    )(*bufs, outs[0], outs[1], after)
    return list(res[n:])


def _swap_halves(half_in, gvec):
    def body(g_ref, gv_ref, out_ref, rg_ref, send_sems, recv_sems):
        x, y, c = _my_place()
        me = 4 * x + 2 * y + c
        sibling = (x, y, 1 - c)

        def half(hc):
            return out_ref.at[:, pl.ds(pl.multiple_of(hc * 512, 128), 512)]

        sends = [pltpu.make_async_remote_copy(src_ref=g_ref, dst_ref=half(c), send_sem=send_sems.at[7],
                                              recv_sem=recv_sems.at[7], device_id=sibling, device_id_type=MESH)]
        for k, dev, peer in _peers(x, y, c):
            sends.append(pltpu.make_async_remote_copy(src_ref=gv_ref, dst_ref=rg_ref.at[me], send_sem=send_sems.at[k],
                                                      recv_sem=recv_sems.at[k], device_id=dev, device_id_type=MESH))
        for cp in sends:
            cp.start()
        got = half(1 - c)
        pltpu.make_async_remote_copy(src_ref=got, dst_ref=got, send_sem=send_sems.at[7], recv_sem=recv_sems.at[7],
                                     device_id=sibling, device_id_type=MESH).wait_recv()
        for k, dev, peer in _peers(x, y, c):
            got = rg_ref.at[peer]
            pltpu.make_async_remote_copy(src_ref=got, dst_ref=got, send_sem=send_sems.at[k], recv_sem=recv_sems.at[k],
                                         device_id=dev, device_id_type=MESH).wait_recv()
        for cp in sends:
            cp.wait_send()

    return pl.pallas_call(
        body, name="swap_halves",
        in_specs=[ANY, ANY], out_specs=[ANY, ANY],
        out_shape=[jax.ShapeDtypeStruct(SHARD_SHAPES[0], F32), jax.ShapeDtypeStruct((8, 8, N_GVEC), F32)],
        scratch_shapes=[pltpu.SemaphoreType.DMA((8,)), pltpu.SemaphoreType.DMA((8,))],
    )(half_in, gvec)


def _set_slot(arr, block, idx):
    return lax.dynamic_update_slice(arr, block[None], (idx,) + (0,) * block.ndim)


PAD_RUNS = ((6304, 8352, 0), (5280, 6304, COL_Z), (672, 5280, COL_QKV), (0, 640, COL_LAT), (640, 672, COL_LAT + 704))
N_QKV = COL_LAT - COL_QKV


def _qkv_rows_regroup(a, to_padded):
    if to_padded:
        a4 = a.reshape(3, 12, 128, a.shape[1])
        return jnp.stack([a4[0], a4[1], a4[2]], axis=1).reshape(a.shape)
    a4 = a.reshape(12, 3, 128, a.shape[1])
    return jnp.concatenate([a4[:, tq].reshape(N_QKV // 3, a.shape[1]) for tq in range(3)], axis=0)
W_IN_SHARD = 2088


def _full_weights(gathered):
    def cols(a):
        return jnp.concatenate([a[s] for s in range(4)], axis=1)

    w_uq, w_ukv, w_pm, w_pd = [cols(a) for a in gathered[1:5]]
    w_out = gathered[5].reshape(D_MODEL, D_MODEL)
    w_in_t = gathered[0].reshape(4 * W_IN_SHARD, D_MODEL)
    pieces, at = [], 0
    for lo, hi, pad_lo in sorted(PAD_RUNS, key=lambda t: t[2]):
        if pad_lo > at:
            pieces.append(jnp.zeros((pad_lo - at, D_MODEL), w_in_t.dtype))
        pieces.append(_qkv_rows_regroup(w_in_t[lo:hi], True) if pad_lo == COL_QKV else w_in_t[lo:hi])
        at = pad_lo + hi - lo
    pieces.append(jnp.zeros((N_PAD - at, D_MODEL), w_in_t.dtype))
    w_pad_t = jnp.concatenate(pieces, axis=0)
    z32 = jnp.zeros((Q_RANK, 32), w_uq.dtype)
    wuq_pad = jnp.concatenate([t for h in range(MLA_HEADS) for t in (w_uq[:, h * 96:(h + 1) * 96], z32)], axis=1)
    z64 = jnp.zeros((KV_RANK, 64), w_ukv.dtype)
    wk_pad = jnp.concatenate([t for h in range(MLA_HEADS) for t in (w_ukv[:, h * 128:h * 128 + 64], z64)], axis=1)
    wv = jnp.concatenate([w_ukv[:, h * 128 + 64:(h + 1) * 128] for h in range(MLA_HEADS)], axis=1)
    return w_pad_t, wuq_pad, wk_pad, wv, w_pm, w_pd, w_out


W_IN_LAT = 672


def _grad_parts_in_early(dwt_early):
    dwt_early = jnp.concatenate([dwt_early[:COL_QKV], _qkv_rows_regroup(dwt_early[COL_QKV:COL_LAT], False)], axis=0)

    def in_block(s, h):
        cols = slice(h * 512, (h + 1) * 512)
        out = []
        for lo, hi, pad_lo in sorted(PAD_RUNS):
            a_, b_ = max(lo, s * W_IN_SHARD), min(hi, (s + 1) * W_IN_SHARD)
            if a_ < b_:
                out.append(jnp.zeros((b_ - a_, 512), dwt_early.dtype) if pad_lo >= COL_LAT
                           else dwt_early[pad_lo + a_ - lo:pad_lo + b_ - lo, cols])
        return jnp.concatenate(out, axis=0)

    return jnp.stack([in_block(s, h) for s in range(4) for h in range(2)])


def _grad_parts_in_late(dwt_late):
    rows = jnp.concatenate([dwt_late[0:640], dwt_late[704:736]], axis=0)
    zero = jnp.zeros((W_IN_LAT, 512), dwt_late.dtype)
    return jnp.stack([rows[:, 0:512], rows[:, 512:1024]] + [zero] * 6)


def _shard_blocks(m, axis=1):
    n = m.shape[axis] // 4
    cut = (lambda s: m[:, s * n:(s + 1) * n]) if axis == 1 else (lambda s: m[s * n:(s + 1) * n])
    return jnp.stack([cut(s) for s in range(4) for _ in range(2)])


def _grad_parts_mla(dwuq_pad, dwk_pad, dwv):
    d_uq = jnp.concatenate([dwuq_pad[:, h * 128:h * 128 + 96] for h in range(MLA_HEADS)], axis=1)
    d_ukv = jnp.concatenate([t for h in range(MLA_HEADS) for t in (dwk_pad[:, h * 128:h * 128 + 64], dwv[:, h * 64:(h + 1) * 64])],
                            axis=1)
    return [_shard_blocks(d_uq.astype(BF16)), _shard_blocks(d_ukv.astype(BF16))]


def _rope_tables(positions, token=None):
    pos = positions.reshape(SEQ).astype(F32)
    if token is not None:
        pos = pos + token[0, 0]
    lane = jnp.arange(128)

    def table(rot, first, period):
        inv = ROPE_THETA ** (-jnp.arange(0, rot, 2, dtype=F32) / rot)
        half = rot // 2
        off = lane % period - first
        in1, in2 = (off >= 0) & (off < half), (off >= half) & (off < rot)
        inv_lane = jnp.where(in1 | in2, inv[jnp.clip(off % half, 0, half - 1)], 0.0)
        sign = jnp.where(in1, -1.0, 1.0).astype(F32)
        ang = pos[:, None] * inv_lane[None, :]
        return jnp.cos(ang), jnp.sin(ang) * sign[None, :]

    return table(32, 64, 128), table(16, 0, 64)


class _Links:
    def __init__(self, mats, chip, me):
        landing = [_set_slot(lax.empty((4,) + m.shape, m.dtype), m, chip) for m in mats]
        self.gather, self.token = _gather_start(landing)
        self.me, self.sent, self.handles, self.sums, self.raw = me, {}, {}, {}, {}

    def weights(self, after):
        return _relay_share(_gather_wait(self.gather, after))

    def send(self, blocks, name):
        self.sent[name] = blocks
        self.handles[name], token = _exchange_start(blocks, name)
        return token

    def collect(self, name, after, parts):
        recv = _exchange_wait(self.handles[name], after)
        for r, own, part in zip(recv, self.sent[name], parts):
            if part.startswith("in_"):
                self.sums[part] = _sum_parts(r, own, self.me, 64, "sum_grad_" + part)
            else:
                self.raw[part] = (r, own)
        return tuple(self.sums[part] for part in parts if part in self.sums)


def _device_grads(x, positions, target, gains, links):
    pre_g, q_g, kv_g, post_g = gains
    (mc, ms), (dc, ds) = _rope_tables(positions, links.token)
    h = _prenorm_fwd(x, pre_g, links.token)
    w_pad_t, wuq_pad, wk_pad, wv, w_pm, w_pd, w_out = _full_weights(links.weights((h, mc, ms, dc, ds)))

    p_gz = _matmul(h, w_pad_t, "nt", F32, 1024, 1536, 1024, "in_proj_gates", b_cols=(0, COL_QKV // 1536))
    p_qkv = _matmul(h, w_pad_t, "nt", F32, 1024, 1536, 1024, "in_proj_dilated", b_cols=(COL_QKV // 1536, N_QKV // 1536),
                    lane_blocks=True)
    p_lat = _matmul(h, w_pad_t, "nt", F32, 1024, N_LAT, 1024, "in_proj_latent", b_cols=(COL_LAT // N_LAT, 1))
    q, k, v = _mla_prep_fwd(p_lat, q_g, kv_g, wuq_pad, wk_pad, wv, mc, ms)
    ya, lse_m = _mla_flash_fwd(q, k, v)
    o_g, l_g, qkv = zip(*[_dil_attn_fwd(p_qkv, dc, ds, g) for g in range(3)])
    (dp, dy, dya, dyd, yd, lse_d, loss_cols, dg_post, dwpm, dwpd, dwout) = _tail(
        p_gz, ya, o_g, l_g, x, target, w_pm, w_pd, w_out, post_g)

    for g in range(3):
        dp = _dil_attn_bwd(dp, qkv[g], dyd, yd, lse_d, dc, ds, g)
    dw_early = _matmul(dp, h, "tn", BF16, 1536, 1024, 2048, "dw_in_early", a_cols=(0, COL_LAT // 1536))
    token = links.send([_grad_parts_in_early(dw_early), _shard_blocks(dwpm.astype(BF16)), _shard_blocks(dwpd.astype(BF16)),
                        _shard_blocks(dwout.astype(BF16), axis=0)], "exchange_early")

    dq, dk, dv = _mla_flash_bwd(q, k, v, ya, dya, lse_m, token)
    dp, dwuq_pad, dwk_pad, dwv, dg_q, dg_kv = _mla_prep_bwd(dp, p_lat, dq, dk, dv, q_g, kv_g, wuq_pad, wk_pad, wv, mc, ms)
    dw_late = _matmul(dp, h, "tn", BF16, N_LAT, 1024, 2048, "dw_in_late", a_cols=(COL_LAT // N_LAT, 1))
    token = links.send([_grad_parts_in_late(dw_late)] + _grad_parts_mla(dwuq_pad, dwk_pad, dwv), "exchange_late")
    early = links.collect("exchange_early", dw_late, ("in_early", "pm", "pd", "out"))

    grad_x, dg_pre = _dh_prenorm_bwd(dp, w_pad_t, x, dy, pre_g, (token,) + tuple(early))
    links.collect("exchange_late", grad_x, ("in_late", "uq", "ukv"))

    loss_part = jnp.pad((jnp.sum(loss_cols) * (0.5 / D_MODEL)).reshape(1, 1), ((0, 0), (0, N_GVEC - N_GAINS - 1)))
    gvec = jnp.concatenate([dg_pre, dg_q, dg_kv, dg_post, loss_part], axis=1)
    return grad_x, gvec


def kernel(x, positions, pre_norm_g, w_in, q_norm_g, w_uq, kv_norm_g, w_ukv, w_proj_mla, w_proj_dil, w_out, post_norm_g, loss_target, m_pre_norm_g, m_w_in, m_q_norm_g, m_w_uq, m_kv_norm_g, m_w_ukv, m_w_proj_mla, m_w_proj_dil, m_w_out, m_post_norm_g, v_pre_norm_g, v_w_in, v_q_norm_g, v_w_uq, v_kv_norm_g, v_w_ukv, v_w_proj_mla, v_w_proj_dil, v_w_out, v_post_norm_g):
    xi, yi, ci = _my_place()
    chip, me = 2 * xi + yi, 4 * xi + 2 * yi + ci
    mats = [jnp.swapaxes(w_in, 1, 2)] + [w_uq, w_ukv, w_proj_mla, w_proj_dil, w_out]
    mats = [w.reshape(w.shape[1:]).astype(BF16) for w in mats]
    links = _Links(mats, chip, me)
    gains = (pre_norm_g, q_norm_g, kv_norm_g, post_norm_g)
    grad_x, gvec = _device_grads(x[0], positions, loss_target[0], gains, links)

    sums = links.sums
    in_e = sums["in_early"]
    half_in = jnp.concatenate([in_e[:W_IN_LAT] + jnp.where(chip == 0, sums["in_late"], 0.0), in_e[W_IN_LAT:]], axis=0)
    gvec8 = jnp.pad(gvec, ((0, 7), (0, 0)))
    swapped_in, recv_gains = _swap_halves(half_in, gvec8)
    g_gains = _sum_parts(recv_gains, gvec8, me, 8, "sum_gain_parts")[0:1]
    loss = g_gains[0, N_GAINS]
    sw = lambda a: jnp.swapaxes(a, 1, 2)
    d_in, m_in, v_in, g_in = [sw(o) for o in _adamw_in(sw(w_in), sw(m_w_in), sw(v_w_in), half_in, swapped_in, ci)]
    off = [0, 1024, 1408, 1664, 2688]
    g_gain = [g_gains[:, off[i]:off[i + 1]] for i in range(4)]
    ws = [pre_norm_g, w_in, q_norm_g, w_uq, kv_norm_g, w_ukv, w_proj_mla, w_proj_dil, w_out, post_norm_g]
    ms = [m_pre_norm_g, m_w_in, m_q_norm_g, m_w_uq, m_kv_norm_g, m_w_ukv, m_w_proj_mla, m_w_proj_dil, m_w_out, m_post_norm_g]
    vs = [v_pre_norm_g, v_w_in, v_q_norm_g, v_w_uq, v_kv_norm_g, v_w_ukv, v_w_proj_mla, v_w_proj_dil, v_w_out, v_post_norm_g]
    part_of = [None, "in", None, "uq", None, "ukv", "pm", "pd", "out", None]
    gain_of = iter(g_gain)
    grads, deltas, new_m, new_v = [], [], [], []
    for i, (w, m, v, part) in enumerate(zip(ws, ms, vs, part_of)):
        if part == "in":
            d_, m_, v_, g = d_in, m_in, v_in, g_in
        elif part is not None:
            d_, m_, v_, g = _adamw_recv(w, m, v, *links.raw[part], me, f"adamw_{i}")
        else:
            g = next(gain_of)
            d_, m_, v_ = _adamw(w, g, m, v, f"adamw_{i}")
        grads.append(g)
        deltas.append(d_)
        new_m.append(m_)
        new_v.append(v_)
    return (loss, grad_x.reshape(x.shape), *grads, *deltas, *new_m, *new_v)
```

```python
import jax
import jax.numpy as jnp
from jax import lax
from jax.experimental import pallas as pl
from jax.experimental.pallas import tpu as pltpu

F32 = jnp.float32
BF16 = jnp.bfloat16

SEQ = 4096
D_MODEL = 1024
EPS = 1e-6
ROPE_THETA = 500000.0
MLA_HEADS = 8
Q_RANK = 384
KV_RANK = 256
MLA_SCALE = 96.0 ** -0.5
MLA_ROPE_HALF = 16
DIL_DILATIONS = (1, 4, 16)
DIL_ROPE_HALF = 8
DIL_SCALE = 0.125
BAND = 128

N_LAT = 768
COL_Z, COL_QKV, COL_LAT = 2048, 3072, 7680
N_PAD = 8448


def _qkv_block(tq, g, pr):
    return COL_QKV // 128 + (g * 4 + pr) * 3 + tq

IN_SPLITS = (384, 256, 32, 4608, 512, 512, 1024, 1024)

SHARD_SHAPES = ((2088, 1024), (384, 192), (256, 256), (512, 256), (512, 256), (256, 1024))
N_MATS = len(SHARD_SHAPES)
N_GAINS = 2688
N_GVEC = N_GAINS + 128

ADAM_LR, ADAM_B1, ADAM_B2, ADAM_EPS, ADAM_WD, ADAM_STEP = 0.001, 0.9, 0.999, 1e-08, 0.01, 10

VMEM_LIMIT = 56 * 1024 * 1024
NEG = -1e30
MESH = pl.DeviceIdType.MESH


def _cparams(**kw):
    return pltpu.CompilerParams(vmem_limit_bytes=VMEM_LIMIT, **kw)


def _dot(a, b, dims):
    return lax.dot_general(a, b, (dims, ((), ())), preferred_element_type=F32)


def _nn(a, b):
    return _dot(a, b, ((1,), (0,)))


def _nt(a, b):
    return _dot(a, b, ((1,), (1,)))


def _tn(a, b):
    return _dot(a, b, ((0,), (0,)))


def _rope_lanes(shape, half, period, first):
    lane = lax.broadcasted_iota(jnp.int32, shape, len(shape) - 1) % period
    return (lane >= first) & (lane < first + half), (lane >= first + half) & (lane < first + 2 * half)


def _rope_fwd(x, c, s, half, lanes):
    x1, _ = lanes
    return x * c + jnp.where(x1, pltpu.roll(x, 128 - half, 1), pltpu.roll(x, half, 1)) * s


def _rope_bwd(g, c, s, half, lanes):
    x1, x2 = lanes
    gs = g * s
    return g * c + jnp.where(x2, pltpu.roll(gs, half, 1), jnp.where(x1, pltpu.roll(gs, 128 - half, 1), 0.0))


def _sigmoid(x):
    return 1.0 / (1.0 + jnp.exp(-x))


def _after(token):
    tokens = [t for t in (token if isinstance(token, (tuple, list)) else [token]) if t is not None]
    return tokens, [pl.BlockSpec(memory_space=pl.ANY)] * len(tokens)


def _matmul(a, b, mode, out_dtype, tm, tn, tk, name, token=None, b_cols=None, a_cols=None, lane_blocks=False):
    after, after_specs = _after(token)
    if mode == "nn":
        (m, k), n = a.shape, b.shape[1]
        first = 0
        if b_cols is not None:
            first, n = b_cols[0], b_cols[1] * tn
        a_spec = pl.BlockSpec((tm, tk), lambda j, i, kk: (i, kk))
        b_spec = pl.BlockSpec((tk, tn), lambda j, i, kk: (kk, j + first))
        dot = _nn
    elif mode == "nt":
        (m, k), n = a.shape, b.shape[0]
        first = 0
        if b_cols is not None:
            first, n = b_cols[0], b_cols[1] * tn
        a_spec = pl.BlockSpec((tm, tk), lambda j, i, kk: (i, kk))
        b_spec = pl.BlockSpec((tn, tk), lambda j, i, kk: (j + first, kk))
        dot = _nt
    else:
        (k, m), n = a.shape, b.shape[1]
        first = 0
        if a_cols is not None:
            first, m = a_cols[0], a_cols[1] * tm
        a_spec = pl.BlockSpec((tk, tm), lambda j, i, kk: (kk, i + first))
        b_spec = pl.BlockSpec((tk, tn), lambda j, i, kk: (kk, j))
        dot = _tn
    assert m % tm == 0 and n % tn == 0 and k % tk == 0, (name, m, n, k, tm, tn, tk)
    nk = k // tk

    def body(a_ref, b_ref, *rest):
        o_ref, acc_ref = rest[-2:]
        kk = pl.program_id(2)
        part = dot(a_ref[...], b_ref[...])

        @pl.when(kk == 0)
        def _():
            acc_ref[...] = part

        @pl.when(kk > 0)
        def _():
            acc_ref[...] += part

        @pl.when(kk == nk - 1)
        def _():
            if lane_blocks:
                for blk in range(tn // 128):
                    o_ref[blk] = acc_ref[:, blk * 128:(blk + 1) * 128].astype(o_ref.dtype)
            else:
                o_ref[...] = acc_ref[...].astype(o_ref.dtype)

    if lane_blocks:
        out_spec = pl.BlockSpec((tn // 128, tm, 128), lambda j, i, kk: (j, i, 0))
        out_shape = jax.ShapeDtypeStruct((n // 128, m, 128), out_dtype)
    else:
        out_spec = pl.BlockSpec((tm, tn), lambda j, i, kk: (i, j))
        out_shape = jax.ShapeDtypeStruct((m, n), out_dtype)
    return pl.pallas_call(
        body, name=name, grid=(n // tn, m // tm, nk),
        in_specs=[a_spec, b_spec] + after_specs,
        out_specs=out_spec, out_shape=out_shape,
        scratch_shapes=[pltpu.VMEM((tm, tn), F32)],
        compiler_params=_cparams(),
    )(a, b, *after)


def _prenorm_fwd(x, g, token=None):
    tm = 512
    after, after_specs = _after(token)

    def body(x_ref, g_ref, *rest):
        xv = x_ref[...]
        r = lax.rsqrt(jnp.mean(xv * xv, axis=-1, keepdims=True) + EPS)
        rest[-1][...] = (xv * r * g_ref[...]).astype(BF16)

    return pl.pallas_call(
        body, name="prenorm_fwd", grid=(SEQ // tm,),
        in_specs=[pl.BlockSpec((tm, D_MODEL), lambda i: (i, 0)), pl.BlockSpec((1, D_MODEL), lambda i: (0, 0))] + after_specs,
        out_specs=pl.BlockSpec((tm, D_MODEL), lambda i: (i, 0)),
        out_shape=jax.ShapeDtypeStruct((SEQ, D_MODEL), BF16),
    )(x, g, *after)


def _dh_prenorm_bwd(dp, w_pad_t, x, dy, g, token=None):
    tm, tk = 1024, 1408
    nk = N_PAD // tk
    after, after_specs = _after(token)

    def body(a_ref, b_ref, x_ref, dy_ref, g_ref, *rest):
        gx_ref, dg_ref, acc_ref = rest[-3:]
        i, kk = pl.program_id(0), pl.program_id(1)
        part = _nn(a_ref[...], b_ref[...])

        @pl.when(kk == 0)
        def _():
            acc_ref[...] = part

        @pl.when(kk > 0)
        def _():
            acc_ref[...] += part

        @pl.when(kk == nk - 1)
        def _():
            xv = x_ref[...]
            r = lax.rsqrt(jnp.mean(xv * xv, axis=-1, keepdims=True) + EPS)
            n = xv * r
            dhv = acc_ref[...]
            dn = dhv * g_ref[...]
            gx_ref[...] = dy_ref[...] + r * (dn - n * jnp.mean(dn * n, axis=-1, keepdims=True))
            cols = jnp.sum(dhv * n, axis=0, keepdims=True)

            @pl.when(i == 0)
            def _():
                dg_ref[...] = cols

            @pl.when(i > 0)
            def _():
                dg_ref[...] += cols

    row = pl.BlockSpec((tm, D_MODEL), lambda i, kk: (i, 0))
    vec = pl.BlockSpec((1, D_MODEL), lambda i, kk: (0, 0))
    return pl.pallas_call(
        body, name="dh_prenorm_bwd", grid=(SEQ // tm, nk),
        in_specs=[pl.BlockSpec((tm, tk), lambda i, kk: (i, kk)), pl.BlockSpec((tk, D_MODEL), lambda i, kk: (kk, 0)),
                  row, row, vec] + after_specs,
        out_specs=[row, vec],
        out_shape=[jax.ShapeDtypeStruct((SEQ, D_MODEL), F32), jax.ShapeDtypeStruct((1, D_MODEL), F32)],
        scratch_shapes=[pltpu.VMEM((tm, D_MODEL), F32)],
        compiler_params=_cparams(),
    )(dp, w_pad_t, x, dy, g, *after)


def _mla_prep_fwd(p, qg, kvg, wuq, wk, wv, rc, rs):
    tm = 512

    def body(lat_ref, qg_ref, kvg_ref, wuq_ref, wk_ref, wv_ref, c_ref, s_ref, q_ref, k_ref, v_ref):
        c, s = c_ref[...], s_ref[...]
        lanes = _rope_lanes((tm, 128), MLA_ROPE_HALF, 128, 64)
        cq = lat_ref[:, 0:Q_RANK]
        r1 = lax.rsqrt(jnp.mean(cq * cq, axis=-1, keepdims=True) + EPS)
        cqn = (cq * r1 * qg_ref[...]).astype(BF16)
        q = _nn(cqn, wuq_ref[...])
        for h in range(MLA_HEADS):
            sl = slice(h * 128, (h + 1) * 128)
            q_ref[:, sl] = (_rope_fwd(q[:, sl], c, s, MLA_ROPE_HALF, lanes) * MLA_SCALE).astype(BF16)
        ckv = lat_ref[:, Q_RANK:Q_RANK + KV_RANK]
        r2 = lax.rsqrt(jnp.mean(ckv * ckv, axis=-1, keepdims=True) + EPS)
        ckvn = (ckv * r2 * kvg_ref[...]).astype(BF16)
        krr = _rope_fwd(lat_ref[:, Q_RANK + KV_RANK:N_LAT], c, s, MLA_ROPE_HALF, lanes)
        kn = _nn(ckvn, wk_ref[...])
        for h in range(MLA_HEADS):
            sl = slice(h * 128, (h + 1) * 128)
            k_ref[:, sl] = (kn[:, sl] + krr).astype(BF16)
        v_ref[...] = _nn(ckvn, wv_ref[...]).astype(BF16)

    def full(shape):
        return pl.BlockSpec(shape, lambda i: (0, 0))

    def rows(w):
        return pl.BlockSpec((tm, w), lambda i: (i, 0))

    return pl.pallas_call(
        body, name="mla_prep_fwd", grid=(SEQ // tm,),
        in_specs=[pl.BlockSpec((tm, N_LAT), lambda i: (i, 0)),
                  full((1, Q_RANK)), full((1, KV_RANK)), full((Q_RANK, 1024)), full((KV_RANK, 1024)),
                  full((KV_RANK, 512)), rows(128), rows(128)],
        out_specs=[rows(1024), rows(1024), rows(512)],
        out_shape=[jax.ShapeDtypeStruct((SEQ, 1024), BF16), jax.ShapeDtypeStruct((SEQ, 1024), BF16),
                   jax.ShapeDtypeStruct((SEQ, 512), BF16)],
        compiler_params=_cparams(),
    )(p, qg, kvg, wuq, wk, wv, rc, rs)


def _mla_prep_bwd(dp_in, p, dq, dk, dv, qg, kvg, wuq, wk, wv, rc, rs):
    tm = 512

    def body(dp_any, lat_ref, dq_ref, dk_ref, dv_ref, qg_ref, kvg_ref, wuq_ref, wk_ref, wv_ref,
             c_ref, s_ref, dp_ref, dwuq_ref, dwk_ref, dwv_ref, dgq_ref, dgkv_ref, dqb_ref, dkb_ref):
        del dp_any
        c, s = c_ref[...], s_ref[...]
        lanes = _rope_lanes((tm, 128), MLA_ROPE_HALF, 128, 64)
        lane = lax.broadcasted_iota(jnp.int32, (tm, 128), 1)
        dkr = jnp.zeros((tm, 128), F32)
        for h in range(MLA_HEADS):
            sl = slice(h * 128, (h + 1) * 128)
            dqb_ref[:, sl] = _rope_bwd(dq_ref[:, sl] * MLA_SCALE, c, s, MLA_ROPE_HALF, lanes).astype(BF16)
            dkh = dk_ref[:, sl]
            dkr = dkr + dkh
            dkb_ref[:, sl] = jnp.where(lane < 64, dkh, 0.0).astype(BF16)
        dkr = jnp.where((lane >= 64) & (lane < 96), dkr, 0.0)
        dkr = _rope_bwd(dkr, c, s, MLA_ROPE_HALF, lanes)
        dvb = dv_ref[...].astype(BF16)

        cq = lat_ref[:, 0:Q_RANK]
        r1 = lax.rsqrt(jnp.mean(cq * cq, axis=-1, keepdims=True) + EPS)
        n1 = cq * r1
        dcqn = _nt(dqb_ref[...], wuq_ref[...])
        dn1 = dcqn * qg_ref[...]
        dcq = r1 * (dn1 - n1 * jnp.mean(dn1 * n1, axis=-1, keepdims=True))
        pq = jnp.sum(dcqn * n1, axis=0, keepdims=True)

        ckv = lat_ref[:, Q_RANK:Q_RANK + KV_RANK]
        r2 = lax.rsqrt(jnp.mean(ckv * ckv, axis=-1, keepdims=True) + EPS)
        n2 = ckv * r2
        dckvn = _nt(dkb_ref[...], wk_ref[...]) + _nt(dvb, wv_ref[...])
        dn2 = dckvn * kvg_ref[...]
        dckv = r2 * (dn2 - n2 * jnp.mean(dn2 * n2, axis=-1, keepdims=True))
        pkv = jnp.sum(dckvn * n2, axis=0, keepdims=True)
        cqn = (n1 * qg_ref[...]).astype(BF16)
        ckvn = (n2 * kvg_ref[...]).astype(BF16)
        wq, wk_, wv_ = _tn(cqn, dqb_ref[...]), _tn(ckvn, dkb_ref[...]), _tn(ckvn, dvb)

        dp_ref[:, 0:Q_RANK] = dcq.astype(BF16)
        dp_ref[:, Q_RANK:Q_RANK + KV_RANK] = dckv.astype(BF16)
        dp_ref[:, Q_RANK + KV_RANK:N_LAT] = dkr.astype(BF16)

        @pl.when(pl.program_id(0) == 0)
        def _():
            dgq_ref[...] = pq
            dgkv_ref[...] = pkv
            dwuq_ref[...] = wq
            dwk_ref[...] = wk_
            dwv_ref[...] = wv_

        @pl.when(pl.program_id(0) > 0)
        def _():
            dgq_ref[...] += pq
            dgkv_ref[...] += pkv
            dwuq_ref[...] += wq
            dwk_ref[...] += wk_
            dwv_ref[...] += wv_

    def full(shape):
        return pl.BlockSpec(shape, lambda i: (0, 0))

    def rows(w):
        return pl.BlockSpec((tm, w), lambda i: (i, 0))

    lat = pl.BlockSpec((tm, N_LAT), lambda i: (i, 0))
    dlat = pl.BlockSpec((tm, N_LAT), lambda i: (i, COL_LAT // N_LAT))
    return pl.pallas_call(
        body, name="mla_prep_bwd", grid=(SEQ // tm,),
        in_specs=[pl.BlockSpec(memory_space=pl.ANY), lat, rows(1024), rows(1024), rows(512),
                  full((1, Q_RANK)), full((1, KV_RANK)), full((Q_RANK, 1024)), full((KV_RANK, 1024)),
                  full((KV_RANK, 512)), rows(128), rows(128)],
        out_specs=[dlat, full((Q_RANK, 1024)), full((KV_RANK, 1024)), full((KV_RANK, 512)),
                   full((1, Q_RANK)), full((1, KV_RANK))],
        out_shape=[jax.ShapeDtypeStruct((SEQ, N_PAD), BF16), jax.ShapeDtypeStruct((Q_RANK, 1024), F32),
                   jax.ShapeDtypeStruct((KV_RANK, 1024), F32), jax.ShapeDtypeStruct((KV_RANK, 512), F32),
                   jax.ShapeDtypeStruct((1, Q_RANK), F32), jax.ShapeDtypeStruct((1, KV_RANK), F32)],
        input_output_aliases={0: 0},
        scratch_shapes=[pltpu.VMEM((tm, 1024), BF16), pltpu.VMEM((tm, 1024), BF16)],
        compiler_params=_cparams(),
    )(dp_in, p, dq, dk, dv, qg, kvg, wuq, wk, wv, rc, rs)


FLASH_T = 1024


def _head_half(shape, hh):
    lane = lax.broadcasted_iota(jnp.int32, shape, 1)
    return (lane < 64) if hh == 0 else (lane >= 64)


def _diag_keep(nr, nk):
    row = lax.broadcasted_iota(jnp.int32, (nr, nk), 0)
    col = lax.broadcasted_iota(jnp.int32, (nr, nk), 1)
    return row + (nk - nr) >= col


def _tri_steps(nb, q_major):
    if q_major:
        pairs = [(i, kb) for i in range(nb) for kb in range(i + 1)]
    else:
        pairs = [(i, kb) for kb in range(nb) for i in range(kb, nb)]
    return jnp.asarray([p[0] for p in pairs], jnp.int32), jnp.asarray([p[1] for p in pairs], jnp.int32)


def _mla_flash_fwd(q, k, v):
    t = FLASH_T
    nb = SEQ // t
    qtab, ktab = _tri_steps(nb, True)

    def body(qi_ref, ki_ref, q_ref, k_ref, v_ref, o_ref, lse_ref, m_scr, l_scr, acc_scr):
        step = pl.program_id(1)
        i, kb = qi_ref[step], ki_ref[step]

        @pl.when(kb == 0)
        def _():
            m_scr[...] = jnp.full_like(m_scr, NEG)
            l_scr[...] = jnp.zeros_like(l_scr)
            acc_scr[...] = jnp.zeros_like(acc_scr)

        def update(r0, nr, nk, diagonal):
            rs = slice(r0, r0 + nr)
            vv = v_ref[0:nk, :]
            for hh in range(2):
                sl = slice(hh * 128, (hh + 1) * 128)
                s = _nt(q_ref[rs, sl], k_ref[0:nk, sl])
                if diagonal:
                    s = jnp.where(_diag_keep(nr, nk), s, NEG)
                m_prev = m_scr[hh, rs, :]
                m_new = jnp.maximum(m_prev, jnp.max(s, axis=-1, keepdims=True))
                pr = jnp.exp(s - jnp.tile(m_new, (1, nk // 128)))
                alpha = jnp.exp(m_prev - m_new)
                l_scr[hh, rs, :] = alpha * l_scr[hh, rs, :] + jnp.sum(pr, axis=-1, keepdims=True)
                acc_scr[hh, rs, :] = alpha * acc_scr[hh, rs, :] + _nn(pr.astype(BF16), vv)
                m_scr[hh, rs, :] = m_new

        @pl.when(kb < i)
        def _():
            update(0, t, t, False)

        @pl.when(kb == i)
        def _():
            update(0, t // 2, t // 2, True)
            update(t // 2, t // 2, t, True)
            o0 = acc_scr[0] / l_scr[0]
            o1 = acc_scr[1] / l_scr[1]
            o_ref[...] = jnp.where(_head_half((t, 128), 0), o0, o1)
            for hh in range(2):
                lse_ref[:, hh * 128:(hh + 1) * 128] = m_scr[hh] + jnp.log(l_scr[hh])

    grid_spec = pltpu.PrefetchScalarGridSpec(
        num_scalar_prefetch=2, grid=(4, qtab.shape[0]),
        in_specs=[pl.BlockSpec((t, 256), lambda j, s, qi, ki: (qi[s], j)),
                  pl.BlockSpec((t, 256), lambda j, s, qi, ki: (ki[s], j)),
                  pl.BlockSpec((t, 128), lambda j, s, qi, ki: (ki[s], j))],
        out_specs=[pl.BlockSpec((t, 128), lambda j, s, qi, ki: (qi[s], j)),
                   pl.BlockSpec((t, 256), lambda j, s, qi, ki: (qi[s], j))],
        scratch_shapes=[pltpu.VMEM((2, t, 128), F32), pltpu.VMEM((2, t, 128), F32), pltpu.VMEM((2, t, 128), F32)])
    return pl.pallas_call(
        body, name="mla_flash_fwd", grid_spec=grid_spec,
        out_shape=[jax.ShapeDtypeStruct((SEQ, 512), F32), jax.ShapeDtypeStruct((SEQ, 1024), F32)],
        compiler_params=_cparams(),
    )(qtab, ktab, q, k, v)


def _mla_flash_bwd(q, k, v, o, do, lse, token=None):
    t = FLASH_T
    nb = SEQ // t
    qtab, ktab = _tri_steps(nb, False)
    after, after_specs = _after(token)

    def body(qi_ref, ki_ref, q_ref, k_ref, v_ref, o_ref, do_ref, lse_ref, *rest):
        dq_ref, dk_ref, dv_ref, dk_scr, dv_scr = rest[-5:]
        step = pl.program_id(1)
        i, kb = qi_ref[step], ki_ref[step]

        @pl.when(step == 0)
        def _():
            dq_ref[...] = jnp.zeros_like(dq_ref)

        @pl.when(i == kb)
        def _():
            dk_scr[...] = jnp.zeros_like(dk_scr)
            dv_scr[...] = jnp.zeros_like(dv_scr)

        def update(r0, nr, nk, diagonal):
            rs = slice(r0, r0 + nr)
            vv = v_ref[0:nk, :]
            ov = o_ref[rs, :]
            dov = do_ref[rs, :]
            rows = pl.ds(pl.multiple_of(i * t + r0, t // 2), nr)
            for hh in range(2):
                sl = slice(hh * 128, (hh + 1) * 128)
                qh, kh = q_ref[rs, sl], k_ref[0:nk, sl]
                s = _nt(qh, kh)
                if diagonal:
                    s = jnp.where(_diag_keep(nr, nk), s, NEG)
                pr = jnp.exp(s - jnp.tile(lse_ref[rs, sl], (1, nk // 128)))
                dom = jnp.where(_head_half((nr, 128), hh), dov, 0.0)
                domb = dom.astype(BF16)
                dv_scr[0:nk, :] += _tn(pr.astype(BF16), domb)
                dpr = _nt(domb, vv)
                delta = jnp.sum(dom * ov, axis=-1, keepdims=True)
                ds = (pr * (dpr - delta)).astype(BF16)
                dq_ref[rows, sl] += _nn(ds, kh)
                dk_scr[hh, 0:nk, :] += _tn(ds, qh)

        @pl.when(i > kb)
        def _():
            update(0, t, t, False)

        @pl.when(i == kb)
        def _():
            update(0, t // 2, t // 2, True)
            update(t // 2, t // 2, t, True)

        @pl.when(i == nb - 1)
        def _():
            dk_ref[:, 0:128] = dk_scr[0]
            dk_ref[:, 128:256] = dk_scr[1]
            dv_ref[...] = dv_scr[...]

    qi_map = lambda j, s, qi, ki: (qi[s], j)
    ki_map = lambda j, s, qi, ki: (ki[s], j)
    grid_spec = pltpu.PrefetchScalarGridSpec(
        num_scalar_prefetch=2, grid=(4, qtab.shape[0]),
        in_specs=[pl.BlockSpec((t, 256), qi_map), pl.BlockSpec((t, 256), ki_map), pl.BlockSpec((t, 128), ki_map),
                  pl.BlockSpec((t, 128), qi_map), pl.BlockSpec((t, 128), qi_map), pl.BlockSpec((t, 256), qi_map)]
        + after_specs,
        out_specs=[pl.BlockSpec((SEQ, 256), lambda j, s, qi, ki: (0, j)), pl.BlockSpec((t, 256), ki_map),
                   pl.BlockSpec((t, 128), ki_map)],
        scratch_shapes=[pltpu.VMEM((2, t, 128), F32), pltpu.VMEM((t, 128), F32)])
    return pl.pallas_call(
        body, name="mla_flash_bwd", grid_spec=grid_spec,
        out_shape=[jax.ShapeDtypeStruct((SEQ, 1024), F32), jax.ShapeDtypeStruct((SEQ, 1024), F32),
                   jax.ShapeDtypeStruct((SEQ, 512), F32)],
        compiler_params=_cparams(),
    )(qtab, ktab, q, k, v, o, do, lse, *after)


def _strided(start, size, d):
    return pl.ds(start, size) if d == 1 else pl.ds(start, size, stride=d)


DIL_ST_FWD, DIL_ST_BWD = 1024, 2048


def _band_keep(g, b, t, nb):
    nbs = SEQ // DIL_DILATIONS[g] // BAND
    row = lax.broadcasted_iota(jnp.int32, (BAND, 2 * BAND), 0)
    col = lax.broadcasted_iota(jnp.int32, (BAND, 2 * BAND), 1)
    cur = (col >= BAND) & (row >= col - BAND)
    prev = (col < BAND) & (col >= row)
    if nbs >= nb:
        if b > 0:
            return cur | prev
        return cur | (prev & ((t * nb) % nbs != 0))
    return cur | prev if b % nbs else cur


def _dil_tok(g, b, t, nb):
    d = DIL_DILATIONS[g]
    nbs = SEQ // d // BAND
    gb = t * nb + b
    return _strided((gb % nbs) * BAND * d + gb // nbs, BAND, d)


def _dil_attn_fwd(p_qkv, rc, rs, g):
    d = DIL_DILATIONS[g]
    sub_len = SEQ // d
    ch = min(sub_len, 512)
    DIL_ST, DIL_NB = DIL_ST_FWD, DIL_ST_FWD // BAND

    def body(p_ref, c_ref, sn_ref, o_ref, l_ref, qkv_ref, x_scr, s_scr, p_scr, o_scr):
        t = pl.program_id(1)

        @pl.when(t == 0)
        def _():
            lanes = _rope_lanes((ch, 128), DIL_ROPE_HALF, 64, 0)
            for tq in range(3):
                qkv_ref[tq, 0, 0:BAND, :] = jnp.zeros((BAND, 128), BF16)
                mult = DIL_SCALE if tq == 0 else 1.0
                for c0 in range(0, SEQ, ch):
                    rows = pl.ds(c0, ch)
                    xv = p_ref[tq, rows, :].astype(F32)
                    x_scr[rows, :] = xv if tq == 2 else _rope_fwd(xv, c_ref[rows, :] * mult, sn_ref[rows, :] * mult,
                                                                   DIL_ROPE_HALF, lanes)
                for r in range(d):
                    for c0 in range(0, sub_len, ch):
                        at = BAND + r * sub_len + c0
                        qkv_ref[tq, 0, at:at + ch, :] = x_scr[_strided(r + c0 * d, ch, d), :].astype(BF16)

        base = t * DIL_ST
        half0 = _head_half((DIL_ST, 128), 0)
        lse_h = []
        for hh in range(2):
            half = _head_half((BAND, 128), hh)
            for b in range(DIL_NB):
                qv = qkv_ref[0, 0, pl.ds(pl.multiple_of(base + (b + 1) * BAND, BAND), BAND), :]
                k2 = qkv_ref[1, 0, pl.ds(pl.multiple_of(base + b * BAND, BAND), 2 * BAND), :]
                sb = _nt(jnp.where(half, qv, jnp.zeros_like(qv)), k2)
                s_scr[b * BAND:(b + 1) * BAND, :] = jnp.where(_band_keep(g, b, t, DIL_NB), sb, NEG)
            s = s_scr[...]
            m = jnp.max(s, axis=-1, keepdims=True)
            pr = jnp.exp(s - m)
            den = jnp.sum(pr, axis=-1, keepdims=True)
            p_scr[...] = pr.astype(BF16)
            for b in range(DIL_NB):
                v2 = qkv_ref[2, 0, pl.ds(pl.multiple_of(base + b * BAND, BAND), 2 * BAND), :]
                o_scr[hh, b * BAND:(b + 1) * BAND, :] = _nn(p_scr[b * BAND:(b + 1) * BAND, :], v2)
            o_scr[hh] = o_scr[hh] / den
            lse_h.append(m + jnp.log(den))
        out = jnp.where(half0, o_scr[0], o_scr[1])
        lse = jnp.where(half0, lse_h[0], lse_h[1])
        for b in range(DIL_NB):
            tok = _dil_tok(g, b, t, DIL_NB)
            o_ref[tok, :] = out[b * BAND:(b + 1) * BAND, :]
            l_ref[tok, :] = lse[b * BAND:(b + 1) * BAND, :]

    tab = pl.BlockSpec((SEQ, 128), lambda pr, t: (0, 0))
    out = pl.BlockSpec((SEQ, 128), lambda pr, t: (0, pr))
    return pl.pallas_call(
        body, name=f"dil_attn_fwd_g{g}", grid=(4, SEQ // DIL_ST),
        in_specs=[pl.BlockSpec((None, 3, SEQ, 128), lambda pr, t: (g * 4 + pr, 0, 0, 0)), tab, tab],
        out_specs=[out, out, pl.BlockSpec((3, 1, BAND + SEQ, 128), lambda pr, t: (0, pr, 0, 0))],
        out_shape=[jax.ShapeDtypeStruct((SEQ, 512), F32), jax.ShapeDtypeStruct((SEQ, 512), F32),
                   jax.ShapeDtypeStruct((3, 4, BAND + SEQ, 128), BF16)],
        scratch_shapes=[pltpu.VMEM((SEQ, 128), F32), pltpu.VMEM((DIL_ST, 2 * BAND), F32),
                        pltpu.VMEM((DIL_ST, 2 * BAND), BF16), pltpu.VMEM((2, DIL_ST, 128), F32)],
        compiler_params=_cparams(),
    )(p_qkv.reshape(12, 3, SEQ, 128), rc, rs)


def _dil_attn_bwd(dp_in, qkv, dyd, yd, lse_all, rc, rs, g, token=None):
    d = DIL_DILATIONS[g]
    sub_len = SEQ // d
    DIL_ST, DIL_NB = DIL_ST_BWD, DIL_ST_BWD // BAND
    nst = SEQ // DIL_ST
    after, after_specs = _after(token)
    ch = 512

    def body(dp_any, q_ref, k_ref, v_ref, do_ref, y_ref, l_ref, c_ref, sn_ref, *rest):
        dp_ref, tok_scr, dk_scr, dv_scr, s_scr, dp_scr, p_scr, ds_scr, do_scr, y_scr, l_scr, dq_scr = rest[-12:]
        del dp_any
        t = pl.program_id(1)
        base = t * DIL_ST

        @pl.when(t == 0)
        def _():
            dk_scr[...] = jnp.zeros_like(dk_scr)
            dv_scr[...] = jnp.zeros_like(dv_scr)

        for b in range(DIL_NB):
            tok = _dil_tok(g, b, t, DIL_NB)
            do_scr[b * BAND:(b + 1) * BAND, :] = do_ref[tok, :]
            y_scr[b * BAND:(b + 1) * BAND, :] = y_ref[tok, :]
            l_scr[b * BAND:(b + 1) * BAND, :] = l_ref[tok, :]
        for hh in range(2):
            half = _head_half((BAND, 128), hh)
            half_st = _head_half((DIL_ST, 128), hh)
            dom = jnp.where(half_st, do_scr[...], 0.0)
            delta = jnp.sum(dom * y_scr[...], axis=-1, keepdims=True)
            lcol = jnp.max(jnp.where(half_st, l_scr[...], NEG), axis=-1, keepdims=True)
            for b in range(DIL_NB):
                rows = slice(b * BAND, (b + 1) * BAND)
                qv = q_ref[0, 0, pl.ds(pl.multiple_of(base + (b + 1) * BAND, BAND), BAND), :]
                band = pl.ds(pl.multiple_of(base + b * BAND, BAND), 2 * BAND)
                sb = _nt(jnp.where(half, qv, jnp.zeros_like(qv)), k_ref[0, 0, band, :])
                s_scr[rows, :] = jnp.where(_band_keep(g, b, t, DIL_NB), sb, NEG)
                dp_scr[rows, :] = _nt(dom[rows, :].astype(BF16), v_ref[0, 0, band, :])
            pr = jnp.exp(s_scr[...] - lcol)
            p_scr[...] = pr.astype(BF16)
            ds_scr[...] = (pr * (dp_scr[...] - delta)).astype(BF16)
            for b in range(DIL_NB):
                rows = slice(b * BAND, (b + 1) * BAND)
                qv = q_ref[0, 0, pl.ds(pl.multiple_of(base + (b + 1) * BAND, BAND), BAND), :]
                band = pl.ds(pl.multiple_of(base + b * BAND, BAND), 2 * BAND)
                dqb = jnp.where(half, _nn(ds_scr[rows, :], k_ref[0, 0, band, :]), 0.0)
                if hh == 0:
                    dq_scr[rows, :] = dqb
                else:
                    dq_scr[rows, :] += dqb
                half2 = _head_half((2 * BAND, 128), hh)
                dk_scr[band, :] += jnp.where(half2, _tn(ds_scr[rows, :], qv), 0.0)
                dv_scr[band, :] += _tn(p_scr[rows, :], dom[rows, :].astype(BF16))
        for b in range(DIL_NB):
            tok_scr[pl.ds(0, 1), _dil_tok(g, b, t, DIL_NB), :] = dq_scr[b * BAND:(b + 1) * BAND, :][None]

        @pl.when(t == nst - 1)
        def _():
            for r in range(d):
                rows = _strided(r, sub_len, d)
                tok_scr[pl.ds(1, 1), rows, :] = dk_scr[BAND + r * sub_len:BAND + (r + 1) * sub_len, :][None]
                tok_scr[pl.ds(2, 1), rows, :] = dv_scr[BAND + r * sub_len:BAND + (r + 1) * sub_len, :][None]
            lanes = _rope_lanes((ch, 128), DIL_ROPE_HALF, 64, 0)
            for c0 in range(0, SEQ, ch):
                rows = slice(c0, c0 + ch)
                cv, sv = c_ref[rows, :], sn_ref[rows, :]
                dp_ref[rows, 0:128] = _rope_bwd(tok_scr[0, rows, :], cv * DIL_SCALE, sv * DIL_SCALE, DIL_ROPE_HALF, lanes).astype(BF16)
                dp_ref[rows, 128:256] = _rope_bwd(tok_scr[1, rows, :], cv, sv, DIL_ROPE_HALF, lanes).astype(BF16)
                dp_ref[rows, 256:384] = tok_scr[2, rows, :].astype(BF16)

    def inp(tq):
        return pl.BlockSpec((1, 1, BAND + SEQ, 128), lambda pr, t: (tq, pr, 0, 0))

    tok_spec = pl.BlockSpec((SEQ, 128), lambda pr, t: (0, pr))
    tab = pl.BlockSpec((SEQ, 128), lambda pr, t: (0, 0))
    st = (DIL_ST, 2 * BAND)
    return pl.pallas_call(
        body, name=f"dil_attn_bwd_g{g}", grid=(4, nst),
        in_specs=[pl.BlockSpec(memory_space=pl.ANY), inp(0), inp(1), inp(2), tok_spec, tok_spec, tok_spec, tab, tab]
        + after_specs,
        out_specs=pl.BlockSpec((SEQ, 384), lambda pr, t: (0, _qkv_block(0, g, pr) // 3)),
        out_shape=jax.ShapeDtypeStruct((SEQ, N_PAD), BF16),
        input_output_aliases={0: 0},
        scratch_shapes=[pltpu.VMEM((3, SEQ, 128), F32),
                        pltpu.VMEM((BAND + SEQ, 128), F32), pltpu.VMEM((BAND + SEQ, 128), F32),
                        pltpu.VMEM(st, F32), pltpu.VMEM(st, F32), pltpu.VMEM(st, BF16), pltpu.VMEM(st, BF16),
                        pltpu.VMEM((DIL_ST, 128), F32), pltpu.VMEM((DIL_ST, 128), F32), pltpu.VMEM((DIL_ST, 128), F32),
                        pltpu.VMEM((DIL_ST, 128), F32)],
        compiler_params=_cparams(),
    )(dp_in, qkv, qkv, qkv, dyd, yd, lse_all, rc, rs, *after)


TAIL_T = 256


def _tail(p, ya, o_g, l_g, x, target, wpm, wpd, wout, post_g):
    tm = TAIL_T

    def body(pgz_ref, ya_ref, o0_ref, o1_ref, o2_ref, l0_ref, l1_ref, l2_ref, x_ref, t_ref,
             wpm_ref, wpd_ref, wout_ref, pg_ref,
             dp_ref, dy_ref, dya_ref, dyd_ref, yd_ref, lse_ref, loss_ref, dgp_ref, dwpm_ref, dwpd_ref, dwout_ref):
        l0, l1, l2 = l0_ref[...], l1_ref[...], l2_ref[...]
        mx = jnp.maximum(jnp.maximum(l0, l1), l2)
        e0, e1, e2 = jnp.exp(l0 - mx), jnp.exp(l1 - mx), jnp.exp(l2 - mx)
        den = e0 + e1 + e2
        yd = (e0 * o0_ref[...] + e1 * o1_ref[...] + e2 * o2_ref[...]) / den
        yd_ref[...] = yd
        lse_ref[...] = mx + jnp.log(den)
        ya = ya_ref[...]

        gm, gd = pgz_ref[:, 0:1024], pgz_ref[:, 1024:2048]
        zm, zd = pgz_ref[:, 2048:2560], pgz_ref[:, 2560:3072]
        szm, szd = _sigmoid(zm), _sigmoid(zd)
        sm, sd = zm * szm, zd * szd
        ua = (ya * sm).astype(BF16)
        ud = (yd * sd).astype(BF16)
        pa = _nn(ua, wpm_ref[...])
        pd = _nn(ud, wpd_ref[...])
        sgm, sgd = _sigmoid(gm), _sigmoid(gd)
        mg = (sgm * pa + sgd * pd).astype(BF16)
        t = _nn(mg, wout_ref[...])
        r3 = lax.rsqrt(jnp.mean(t * t, axis=-1, keepdims=True) + EPS)
        n = t * r3
        pg = pg_ref[...]
        err = x_ref[...] + n * pg - t_ref[...]
        lpart = jnp.sum(err * err, axis=0, keepdims=True)

        dy = err * (1.0 / D_MODEL)
        dy_ref[...] = dy
        gpart = jnp.sum(dy * n, axis=0, keepdims=True)
        dn = dy * pg
        dt = (r3 * (dn - n * jnp.mean(dn * n, axis=-1, keepdims=True))).astype(BF16)
        dmg = _nt(dt, wout_ref[...])
        dpa = (dmg * sgm).astype(BF16)
        dpd = (dmg * sgd).astype(BF16)
        dp_ref[:, 0:1024] = (dmg * pa * sgm * (1.0 - sgm)).astype(BF16)
        dp_ref[:, 1024:2048] = (dmg * pd * sgd * (1.0 - sgd)).astype(BF16)
        dua = _nt(dpa, wpm_ref[...])
        dud = _nt(dpd, wpd_ref[...])
        dya_ref[...] = dua * sm
        dyd_ref[...] = dud * sd
        dp_ref[:, 2048:2560] = (dua * ya * szm * (1.0 + zm * (1.0 - szm))).astype(BF16)
        dp_ref[:, 2560:3072] = (dud * yd * szd * (1.0 + zd * (1.0 - szd))).astype(BF16)

        wpm, wpd, wout = _tn(ua, dpa), _tn(ud, dpd), _tn(mg, dt)

        @pl.when(pl.program_id(0) == 0)
        def _():
            loss_ref[...] = lpart
            dgp_ref[...] = gpart
            dwpm_ref[...] = wpm
            dwpd_ref[...] = wpd
            dwout_ref[...] = wout

        @pl.when(pl.program_id(0) > 0)
        def _():
            loss_ref[...] += lpart
            dgp_ref[...] += gpart
            dwpm_ref[...] += wpm
            dwpd_ref[...] += wpd
            dwout_ref[...] += wout

    def rows(w):
        return pl.BlockSpec((tm, w), lambda i: (i, 0))

    def full(shape):
        return pl.BlockSpec(shape, lambda i: (0, 0))

    def sds(w, dt):
        return jax.ShapeDtypeStruct((SEQ, w), dt)

    return pl.pallas_call(
        body, name="tail", grid=(SEQ // tm,),
        in_specs=[rows(3072), rows(512), rows(512), rows(512), rows(512), rows(512), rows(512), rows(512),
                  rows(1024), rows(1024), full((512, 1024)), full((512, 1024)), full((1024, 1024)), full((1, 1024))],
        out_specs=[rows(3072), rows(1024), rows(512), rows(512), rows(512), rows(512), full((1, 1024)), full((1, 1024)),
                   full((512, 1024)), full((512, 1024)), full((1024, 1024))],
        out_shape=[sds(N_PAD, BF16), sds(1024, F32), sds(512, F32), sds(512, F32), sds(512, F32), sds(512, F32),
                   jax.ShapeDtypeStruct((1, 1024), F32), jax.ShapeDtypeStruct((1, 1024), F32),
                   jax.ShapeDtypeStruct((512, 1024), F32), jax.ShapeDtypeStruct((512, 1024), F32),
                   jax.ShapeDtypeStruct((1024, 1024), F32)],
        compiler_params=_cparams(),
    )(p, ya, o_g[0], o_g[1], o_g[2], l_g[0], l_g[1], l_g[2], x, target, wpm, wpd, wout, post_g)


def _sum_parts(recv, own, me, tr, name):
    n, r, w = recv.shape
    if r % tr:
        return _sum_parts_cols(recv, own, me, name)
    own_spec = (pl.BlockSpec((tr, w), lambda i, me_ref: (i, 0)) if own.ndim == 2
                else pl.BlockSpec((None, tr, w), lambda i, me_ref: (me_ref[0], i, 0)))

    def body(me_ref, p_ref, own_ref, o_ref):
        mine = own_ref[...].astype(F32)
        acc = jnp.zeros((tr, w), F32)
        for s in range(n):
            acc = acc + jnp.where(me_ref[0] == s, mine, p_ref[s].astype(F32))
        o_ref[...] = acc

    return pl.pallas_call(
        body, name=name,
        grid_spec=pltpu.PrefetchScalarGridSpec(
            num_scalar_prefetch=1, grid=(r // tr,),
            in_specs=[pl.BlockSpec((n, tr, w), lambda i, me_ref: (0, i, 0)), own_spec],
            out_specs=pl.BlockSpec((tr, w), lambda i, me_ref: (i, 0))),
        out_shape=jax.ShapeDtypeStruct((r, w), F32),
    )(me.reshape(1), recv, own)


def _sum_parts_cols(recv, own, me, name):
    n, r, w = recv.shape
    tc = 128

    def body(me_ref, p_ref, own_ref, o_ref):
        mine = own_ref[...].astype(F32)
        acc = jnp.zeros((r, tc), F32)
        for s in range(n):
            acc = acc + jnp.where(me_ref[0] == s, mine, p_ref[s].astype(F32))
        o_ref[...] = acc

    return pl.pallas_call(
        body, name=name,
        grid_spec=pltpu.PrefetchScalarGridSpec(
            num_scalar_prefetch=1, grid=(w // tc,),
            in_specs=[pl.BlockSpec((n, r, tc), lambda i, me_ref: (0, 0, i)),
                      pl.BlockSpec((None, r, tc), lambda i, me_ref: (me_ref[0], 0, i))],
            out_specs=pl.BlockSpec((r, tc), lambda i, me_ref: (0, i))),
        out_shape=jax.ShapeDtypeStruct((r, w), F32),
    )(me.reshape(1), recv, own)


def _adamw(w, g, m, v, name):
    lead = w.shape[:-2]
    r, c = w.shape[-2:]
    tr = max([t for t in range(8, 257, 8) if r % t == 0], default=r)
    c1 = 1.0 - ADAM_B1 ** ADAM_STEP
    c2 = 1.0 - ADAM_B2 ** ADAM_STEP

    def body(w_ref, g_ref, m_ref, v_ref, d_ref, nm_ref, nv_ref):
        gv = g_ref[...]
        nm = ADAM_B1 * m_ref[...] + (1.0 - ADAM_B1) * gv
        nv = ADAM_B2 * v_ref[...] + (1.0 - ADAM_B2) * (gv * gv)
        nm_ref[...] = nm
        nv_ref[...] = nv
        d_ref[...] = -ADAM_LR * ((nm / c1) / (jnp.sqrt(nv / c2) + ADAM_EPS) + ADAM_WD * w_ref[...])

    zeros = (0,) * len(lead)
    spec = pl.BlockSpec((1,) * len(lead) + (tr, c), lambda i: zeros + (i, 0))
    sd = jax.ShapeDtypeStruct(w.shape, F32)
    return pl.pallas_call(
        body, name=name, grid=(r // tr,),
        in_specs=[spec] * 4, out_specs=[spec] * 3, out_shape=[sd] * 3,
    )(w, g, m, v)


def _adamw_recv(w, m, v, recv, own, me, name):
    n, r, c = recv.shape
    tr = 128
    c1 = 1.0 - ADAM_B1 ** ADAM_STEP
    c2 = 1.0 - ADAM_B2 ** ADAM_STEP

    def body(me_ref, w_ref, m_ref, v_ref, p_ref, own_ref, d_ref, nm_ref, nv_ref, g_ref):
        mine = own_ref[...].astype(F32)
        gv = jnp.zeros((tr, c), F32)
        for s in range(n):
            gv = gv + jnp.where(me_ref[0] == s, mine, p_ref[s].astype(F32))
        g_ref[0] = gv
        nm = ADAM_B1 * m_ref[0] + (1.0 - ADAM_B1) * gv
        nv = ADAM_B2 * v_ref[0] + (1.0 - ADAM_B2) * (gv * gv)
        nm_ref[0] = nm
        nv_ref[0] = nv
        d_ref[0] = -ADAM_LR * ((nm / c1) / (jnp.sqrt(nv / c2) + ADAM_EPS) + ADAM_WD * w_ref[0])

    full = pl.BlockSpec((1, tr, c), lambda i, me_ref: (0, i, 0))
    sd = jax.ShapeDtypeStruct((1, r, c), F32)
    return pl.pallas_call(
        body, name=name,
        grid_spec=pltpu.PrefetchScalarGridSpec(
            num_scalar_prefetch=1, grid=(r // tr,),
            in_specs=[full, full, full, pl.BlockSpec((n, tr, c), lambda i, me_ref: (0, i, 0)),
                      pl.BlockSpec((None, tr, c), lambda i, me_ref: (me_ref[0], i, 0))],
            out_specs=[full] * 4),
        out_shape=[sd] * 4,
    )(me.reshape(1), w, m, v, recv, own)


def _adamw_in(w_t, m_t, v_t, own_half, swapped, core):
    r, c = SHARD_SHAPES[0]
    tr = max(t for t in range(8, 257, 8) if r % t == 0)
    c1 = 1.0 - ADAM_B1 ** ADAM_STEP
    c2 = 1.0 - ADAM_B2 ** ADAM_STEP

    def body(core_ref, w_ref, m_ref, v_ref, own_ref, sw_ref, d_ref, nm_ref, nv_ref, g_ref):
        own = own_ref[...]
        col_half = lax.broadcasted_iota(jnp.int32, (tr, c), 1) // (c // 2)
        gv = jnp.where(col_half == core_ref[0], jnp.concatenate([own, own], axis=1), sw_ref[...])
        g_ref[0] = gv
        nm = ADAM_B1 * m_ref[0] + (1.0 - ADAM_B1) * gv
        nv = ADAM_B2 * v_ref[0] + (1.0 - ADAM_B2) * (gv * gv)
        nm_ref[0] = nm
        nv_ref[0] = nv
        d_ref[0] = -ADAM_LR * ((nm / c1) / (jnp.sqrt(nv / c2) + ADAM_EPS) + ADAM_WD * w_ref[0])

    full = pl.BlockSpec((1, tr, c), lambda i, core_ref: (0, i, 0))
    sd = jax.ShapeDtypeStruct((1, r, c), F32)
    return pl.pallas_call(
        body, name="adamw_in",
        grid_spec=pltpu.PrefetchScalarGridSpec(
            num_scalar_prefetch=1, grid=(r // tr,),
            in_specs=[full, full, full, pl.BlockSpec((tr, c // 2), lambda i, core_ref: (i, 0)),
                      pl.BlockSpec((tr, c), lambda i, core_ref: (i, 0))],
            out_specs=[full] * 4),
        out_shape=[sd] * 4,
    )(core.reshape(1), w_t, m_t, v_t, own_half, swapped)


ANY = pl.BlockSpec(memory_space=pl.ANY)


def _my_place():
    return lax.axis_index("x"), lax.axis_index("y"), lax.axis_index("c")


HBM = pl.BlockSpec(memory_space=pltpu.HBM)
SEM = pl.BlockSpec(memory_space=pltpu.SEMAPHORE)
DATAFLOW = pltpu.SideEffectType.DATAFLOW_SIDE_EFFECTING


def _near_chips(x, y):
    return [(1 - x, y), (x, 1 - y)]


def _half(mi, hc):
    r, c = SHARD_SHAPES[mi]
    if mi == 0:
        return pl.ds(0, r), pl.ds(pl.multiple_of(hc * (c // 2), 128), c // 2)
    return pl.ds(pl.multiple_of(hc * (r // 2), 16), r // 2), pl.ds(0, c)


def _gather_copies(land_refs, send_sems, recv_sems):
    x, y, c = _my_place()
    out, back = [], []
    for mi in range(N_MATS):
        rows, cols = _half(mi, c)
        mine = land_refs[mi].at[2 * x + y, rows, cols]
        for j, (cx, cy) in enumerate(_near_chips(x, y)):
            sems = dict(send_sem=send_sems.at[mi * 2 + j], recv_sem=recv_sems.at[mi * 2 + j],
                        device_id=(cx, cy, c), device_id_type=MESH)
            out.append(pltpu.make_async_remote_copy(src_ref=mine, dst_ref=mine, **sems))
            got = land_refs[mi].at[2 * cx + cy, rows, cols]
            back.append(pltpu.make_async_remote_copy(src_ref=got, dst_ref=got, **sems))
    return out, back


def _gather_start(landing):
    n = N_MATS

    def body(*refs):
        out, _ = _gather_copies(refs[:n], refs[n], refs[n + 1])
        for cp in out:
            cp.start()
        refs[-1][...] = jnp.zeros_like(refs[-1])

    hbm = [pltpu.HBM(a.shape, a.dtype) for a in landing]
    outs = pl.pallas_call(
        body, name="gather_start",
        out_shape=(pltpu.SemaphoreType.DMA((2 * n,)), pltpu.SemaphoreType.DMA((2 * n,)), *hbm,
                   jax.ShapeDtypeStruct((8, 128), F32)),
        in_specs=[HBM] * n, out_specs=(SEM, SEM, *[HBM] * n, pl.BlockSpec(memory_space=pltpu.VMEM)),
        input_output_aliases={i: 2 + i for i in range(n)},
        compiler_params=pltpu.CompilerParams(has_side_effects=DATAFLOW),
    )(*[pltpu.with_memory_space_constraint(a, pltpu.HBM) for a in landing])
    return outs[:-1], outs[-1]


def _gather_wait(handle, after):
    n = N_MATS

    def body(*refs):
        out, back = _gather_copies(refs[:n], refs[n], refs[n + 1])
        for cp, arrival in zip(out, back):
            cp.wait_send()
            arrival.wait_recv()

    bufs = handle[2:]
    after, after_specs = _after(after)
    res = pl.pallas_call(
        body, name="gather_wait", out_shape=tuple(pltpu.HBM(b.shape, b.dtype) for b in bufs),
        in_specs=[HBM] * n + [SEM, SEM] + after_specs, out_specs=tuple([HBM] * n),
        input_output_aliases={i: i for i in range(n)},
        compiler_params=pltpu.CompilerParams(has_side_effects=DATAFLOW),
    )(*bufs, handle[0], handle[1], *after)
    return list(res)


def _relay_share(gathered):
    n = N_MATS

    def body(*refs):
        out_refs = refs[n:2 * n]
        send_sems, recv_sems = refs[2 * n:]
        x, y, c = _my_place()
        sibling = (x, y, 1 - c)
        relayed = 2 * (x ^ (1 - c)) + (y ^ c)
        relay_to = (x ^ c, y ^ (1 - c), c)
        far = 2 * (1 - x) + (1 - y)
        near = [2 * (1 - x) + y, 2 * x + (1 - y)]

        def copy(k, mi, shard, hc, to):
            blk = out_refs[mi].at[(shard,) + _half(mi, hc)]
            return pltpu.make_async_remote_copy(src_ref=blk, dst_ref=blk, send_sem=send_sems.at[mi * 4 + k],
                                                recv_sem=recv_sems.at[mi * 4 + k], device_id=to, device_id_type=MESH)

        sends = []
        for mi in range(n):
            sends.append(copy(0, mi, relayed, c, relay_to))
            sends += [copy(1 + j, mi, near[j], c, sibling) for j in range(2)]
        for cp in sends:
            cp.start()
        for mi in range(n):
            copy(0, mi, far, c, relay_to).wait_recv()
            cp = copy(3, mi, far, c, sibling)
            cp.start()
            sends.append(cp)
        for mi in range(n):
            for j in range(2):
                copy(1 + j, mi, near[j], 1 - c, sibling).wait_recv()
            copy(3, mi, far, 1 - c, sibling).wait_recv()
        for cp in sends:
            cp.wait_send()

    return pl.pallas_call(
        body, name="relay_share",
        in_specs=[ANY] * n, out_specs=[ANY] * n,
        out_shape=[jax.ShapeDtypeStruct(g.shape, g.dtype) for g in gathered],
        input_output_aliases={i: i for i in range(n)},
        scratch_shapes=[pltpu.SemaphoreType.DMA((4 * n,)), pltpu.SemaphoreType.DMA((4 * n,))],
    )(*gathered)


def _peers(x, y, c):
    out = []
    for k in range(1, 8):
        px, py, pc = x ^ (k >> 2), y ^ ((k >> 1) & 1), c ^ (k & 1)
        out.append((k - 1, (px, py, pc), 4 * px + 2 * py + pc))
    return out


def _exchange_start(parts, name):
    n = len(parts)

    def body(*refs):
        p_refs, land_refs = refs[:n], refs[n:2 * n]
        send_sems, recv_sems, token = refs[2 * n], refs[2 * n + 1], refs[-1]
        x, y, c = _my_place()
        me = 4 * x + 2 * y + c
        for k, dev, peer in _peers(x, y, c):
            for mi in range(n):
                pltpu.make_async_remote_copy(
                    src_ref=p_refs[mi].at[peer], dst_ref=land_refs[mi].at[me], send_sem=send_sems.at[k * n + mi],
                    recv_sem=recv_sems.at[k * n + mi], device_id=dev, device_id_type=MESH).start()
        token[...] = jnp.zeros_like(token)

    hbm = [pltpu.HBM(p.shape, p.dtype) for p in parts]
    outs = pl.pallas_call(
        body, name=name + "_start",
        out_shape=(pltpu.SemaphoreType.DMA((7 * n,)), pltpu.SemaphoreType.DMA((7 * n,)), *hbm, *hbm,
                   jax.ShapeDtypeStruct((8, 128), F32)),
        in_specs=[HBM] * (2 * n), out_specs=(SEM, SEM, *[HBM] * (2 * n), pl.BlockSpec(memory_space=pltpu.VMEM)),
        input_output_aliases={i: 2 + i for i in range(2 * n)},
        compiler_params=pltpu.CompilerParams(has_side_effects=DATAFLOW),
    )(*[pltpu.with_memory_space_constraint(p, pltpu.HBM) for p in parts],
      *[pltpu.with_memory_space_constraint(lax.empty(p.shape, p.dtype), pltpu.HBM) for p in parts])
    return (name, outs[:-1]), outs[-1]


def _exchange_wait(handle, after):
    name, outs = handle
    n = (len(outs) - 2) // 2

    def body(*refs):
        p_refs, land_refs = refs[:n], refs[n:2 * n]
        send_sems, recv_sems = refs[2 * n], refs[2 * n + 1]
        x, y, c = _my_place()
        me = 4 * x + 2 * y + c
        for k, dev, peer in _peers(x, y, c):
            for mi in range(n):
                pltpu.make_async_remote_copy(
                    src_ref=p_refs[mi].at[peer], dst_ref=land_refs[mi].at[me], send_sem=send_sems.at[k * n + mi],
                    recv_sem=recv_sems.at[k * n + mi], device_id=dev, device_id_type=MESH).wait_send()
                slot = land_refs[mi].at[peer]
                pltpu.make_async_remote_copy(
                    src_ref=slot, dst_ref=slot, send_sem=send_sems.at[k * n + mi],
                    recv_sem=recv_sems.at[k * n + mi], device_id=dev, device_id_type=MESH).wait_recv()

    bufs = outs[2:]
    res = pl.pallas_call(
        body, name=name + "_wait", out_shape=tuple(pltpu.HBM(b.shape, b.dtype) for b in bufs),
        in_specs=[HBM] * (2 * n) + [SEM, SEM, ANY], out_specs=tuple([HBM] * (2 * n)),
        input_output_aliases={i: i for i in range(2 * n)},
        compiler_params=pltpu.CompilerParams(has_side_effects=DATAFLOW),
    )(*bufs, outs[0], outs[1], after)
    return list(res[n:])


def _swap_halves(half_in, gvec):
    def body(g_ref, gv_ref, out_ref, rg_ref, send_sems, recv_sems):
        x, y, c = _my_place()
        me = 4 * x + 2 * y + c
        sibling = (x, y, 1 - c)

        def half(hc):
            return out_ref.at[:, pl.ds(pl.multiple_of(hc * 512, 128), 512)]

        sends = [pltpu.make_async_remote_copy(src_ref=g_ref, dst_ref=half(c), send_sem=send_sems.at[7],
                                              recv_sem=recv_sems.at[7], device_id=sibling, device_id_type=MESH)]
        for k, dev, peer in _peers(x, y, c):
            sends.append(pltpu.make_async_remote_copy(src_ref=gv_ref, dst_ref=rg_ref.at[me], send_sem=send_sems.at[k],
                                                      recv_sem=recv_sems.at[k], device_id=dev, device_id_type=MESH))
        for cp in sends:
            cp.start()
        got = half(1 - c)
        pltpu.make_async_remote_copy(src_ref=got, dst_ref=got, send_sem=send_sems.at[7], recv_sem=recv_sems.at[7],
                                     device_id=sibling, device_id_type=MESH).wait_recv()
        for k, dev, peer in _peers(x, y, c):
            got = rg_ref.at[peer]
            pltpu.make_async_remote_copy(src_ref=got, dst_ref=got, send_sem=send_sems.at[k], recv_sem=recv_sems.at[k],
                                         device_id=dev, device_id_type=MESH).wait_recv()
        for cp in sends:
            cp.wait_send()

    return pl.pallas_call(
        body, name="swap_halves",
        in_specs=[ANY, ANY], out_specs=[ANY, ANY],
        out_shape=[jax.ShapeDtypeStruct(SHARD_SHAPES[0], F32), jax.ShapeDtypeStruct((8, 8, N_GVEC), F32)],
        scratch_shapes=[pltpu.SemaphoreType.DMA((8,)), pltpu.SemaphoreType.DMA((8,))],
    )(half_in, gvec)


def _set_slot(arr, block, idx):
    return lax.dynamic_update_slice(arr, block[None], (idx,) + (0,) * block.ndim)


PAD_RUNS = ((6304, 8352, 0), (5280, 6304, COL_Z), (672, 5280, COL_QKV), (0, 640, COL_LAT), (640, 672, COL_LAT + 704))
N_QKV = COL_LAT - COL_QKV


def _qkv_rows_regroup(a, to_padded):
    if to_padded:
        a4 = a.reshape(3, 12, 128, a.shape[1])
        return jnp.stack([a4[0], a4[1], a4[2]], axis=1).reshape(a.shape)
    a4 = a.reshape(12, 3, 128, a.shape[1])
    return jnp.concatenate([a4[:, tq].reshape(N_QKV // 3, a.shape[1]) for tq in range(3)], axis=0)
W_IN_SHARD = 2088


def _full_weights(gathered):
    def cols(a):
        return jnp.concatenate([a[s] for s in range(4)], axis=1)

    w_uq, w_ukv, w_pm, w_pd = [cols(a) for a in gathered[1:5]]
    w_out = gathered[5].reshape(D_MODEL, D_MODEL)
    w_in_t = gathered[0].reshape(4 * W_IN_SHARD, D_MODEL)
    pieces, at = [], 0
    for lo, hi, pad_lo in sorted(PAD_RUNS, key=lambda t: t[2]):
        if pad_lo > at:
            pieces.append(jnp.zeros((pad_lo - at, D_MODEL), w_in_t.dtype))
        pieces.append(_qkv_rows_regroup(w_in_t[lo:hi], True) if pad_lo == COL_QKV else w_in_t[lo:hi])
        at = pad_lo + hi - lo
    pieces.append(jnp.zeros((N_PAD - at, D_MODEL), w_in_t.dtype))
    w_pad_t = jnp.concatenate(pieces, axis=0)
    z32 = jnp.zeros((Q_RANK, 32), w_uq.dtype)
    wuq_pad = jnp.concatenate([t for h in range(MLA_HEADS) for t in (w_uq[:, h * 96:(h + 1) * 96], z32)], axis=1)
    z64 = jnp.zeros((KV_RANK, 64), w_ukv.dtype)
    wk_pad = jnp.concatenate([t for h in range(MLA_HEADS) for t in (w_ukv[:, h * 128:h * 128 + 64], z64)], axis=1)
    wv = jnp.concatenate([w_ukv[:, h * 128 + 64:(h + 1) * 128] for h in range(MLA_HEADS)], axis=1)
    return w_pad_t, wuq_pad, wk_pad, wv, w_pm, w_pd, w_out


W_IN_LAT = 672


def _grad_parts_in_early(dwt_early):
    dwt_early = jnp.concatenate([dwt_early[:COL_QKV], _qkv_rows_regroup(dwt_early[COL_QKV:COL_LAT], False)], axis=0)

    def in_block(s, h):
        cols = slice(h * 512, (h + 1) * 512)
        out = []
        for lo, hi, pad_lo in sorted(PAD_RUNS):
            a_, b_ = max(lo, s * W_IN_SHARD), min(hi, (s + 1) * W_IN_SHARD)
            if a_ < b_:
                out.append(jnp.zeros((b_ - a_, 512), dwt_early.dtype) if pad_lo >= COL_LAT
                           else dwt_early[pad_lo + a_ - lo:pad_lo + b_ - lo, cols])
        return jnp.concatenate(out, axis=0)

    return jnp.stack([in_block(s, h) for s in range(4) for h in range(2)])


def _grad_parts_in_late(dwt_late):
    rows = jnp.concatenate([dwt_late[0:640], dwt_late[704:736]], axis=0)
    zero = jnp.zeros((W_IN_LAT, 512), dwt_late.dtype)
    return jnp.stack([rows[:, 0:512], rows[:, 512:1024]] + [zero] * 6)


def _shard_blocks(m, axis=1):
    n = m.shape[axis] // 4
    cut = (lambda s: m[:, s * n:(s + 1) * n]) if axis == 1 else (lambda s: m[s * n:(s + 1) * n])
    return jnp.stack([cut(s) for s in range(4) for _ in range(2)])


def _grad_parts_mla(dwuq_pad, dwk_pad, dwv):
    d_uq = jnp.concatenate([dwuq_pad[:, h * 128:h * 128 + 96] for h in range(MLA_HEADS)], axis=1)
    d_ukv = jnp.concatenate([t for h in range(MLA_HEADS) for t in (dwk_pad[:, h * 128:h * 128 + 64], dwv[:, h * 64:(h + 1) * 64])],
                            axis=1)
    return [_shard_blocks(d_uq.astype(BF16)), _shard_blocks(d_ukv.astype(BF16))]


def _rope_tables(positions, token=None):
    pos = positions.reshape(SEQ).astype(F32)
    if token is not None:
        pos = pos + token[0, 0]
    lane = jnp.arange(128)

    def table(rot, first, period):
        inv = ROPE_THETA ** (-jnp.arange(0, rot, 2, dtype=F32) / rot)
        half = rot // 2
        off = lane % period - first
        in1, in2 = (off >= 0) & (off < half), (off >= half) & (off < rot)
        inv_lane = jnp.where(in1 | in2, inv[jnp.clip(off % half, 0, half - 1)], 0.0)
        sign = jnp.where(in1, -1.0, 1.0).astype(F32)
        ang = pos[:, None] * inv_lane[None, :]
        return jnp.cos(ang), jnp.sin(ang) * sign[None, :]

    return table(32, 64, 128), table(16, 0, 64)


class _Links:
    def __init__(self, mats, chip, me):
        landing = [_set_slot(lax.empty((4,) + m.shape, m.dtype), m, chip) for m in mats]
        self.gather, self.token = _gather_start(landing)
        self.me, self.sent, self.handles, self.sums, self.raw = me, {}, {}, {}, {}

    def weights(self, after):
        return _relay_share(_gather_wait(self.gather, after))

    def send(self, blocks, name):
        self.sent[name] = blocks
        self.handles[name], token = _exchange_start(blocks, name)
        return token

    def collect(self, name, after, parts):
        recv = _exchange_wait(self.handles[name], after)
        for r, own, part in zip(recv, self.sent[name], parts):
            if part.startswith("in_"):
                self.sums[part] = _sum_parts(r, own, self.me, 64, "sum_grad_" + part)
            else:
                self.raw[part] = (r, own)
        return tuple(self.sums[part] for part in parts if part in self.sums)


def _device_grads(x, positions, target, gains, links):
    pre_g, q_g, kv_g, post_g = gains
    (mc, ms), (dc, ds) = _rope_tables(positions, links.token)
    h = _prenorm_fwd(x, pre_g, links.token)
    w_pad_t, wuq_pad, wk_pad, wv, w_pm, w_pd, w_out = _full_weights(links.weights((h, mc, ms, dc, ds)))

    p_gz = _matmul(h, w_pad_t, "nt", F32, 1024, 1536, 1024, "in_proj_gates", b_cols=(0, COL_QKV // 1536))
    p_qkv = _matmul(h, w_pad_t, "nt", BF16, 1024, 1536, 1024, "in_proj_dilated", b_cols=(COL_QKV // 1536, N_QKV // 1536),
                    lane_blocks=True)
    p_lat = _matmul(h, w_pad_t, "nt", F32, 1024, N_LAT, 1024, "in_proj_latent", b_cols=(COL_LAT // N_LAT, 1))
    q, k, v = _mla_prep_fwd(p_lat, q_g, kv_g, wuq_pad, wk_pad, wv, mc, ms)
    ya, lse_m = _mla_flash_fwd(q, k, v)
    o_g, l_g, qkv = zip(*[_dil_attn_fwd(p_qkv, dc, ds, g) for g in range(3)])
    (dp, dy, dya, dyd, yd, lse_d, loss_cols, dg_post, dwpm, dwpd, dwout) = _tail(
        p_gz, ya, o_g, l_g, x, target, w_pm, w_pd, w_out, post_g)

    for g in range(3):
        dp = _dil_attn_bwd(dp, qkv[g], dyd, yd, lse_d, dc, ds, g)
    dw_early = _matmul(dp, h, "tn", BF16, 1536, 1024, 2048, "dw_in_early", a_cols=(0, COL_LAT // 1536))
    token = links.send([_grad_parts_in_early(dw_early), _shard_blocks(dwpm.astype(BF16)), _shard_blocks(dwpd.astype(BF16)),
                        _shard_blocks(dwout.astype(BF16), axis=0)], "exchange_early")

    dq, dk, dv = _mla_flash_bwd(q, k, v, ya, dya, lse_m, token)
    dp, dwuq_pad, dwk_pad, dwv, dg_q, dg_kv = _mla_prep_bwd(dp, p_lat, dq, dk, dv, q_g, kv_g, wuq_pad, wk_pad, wv, mc, ms)
    dw_late = _matmul(dp, h, "tn", BF16, N_LAT, 1024, 2048, "dw_in_late", a_cols=(COL_LAT // N_LAT, 1))
    token = links.send([_grad_parts_in_late(dw_late)] + _grad_parts_mla(dwuq_pad, dwk_pad, dwv), "exchange_late")
    early = links.collect("exchange_early", dw_late, ("in_early", "pm", "pd", "out"))

    grad_x, dg_pre = _dh_prenorm_bwd(dp, w_pad_t, x, dy, pre_g, (token,) + tuple(early))
    links.collect("exchange_late", grad_x, ("in_late", "uq", "ukv"))

    loss_part = jnp.pad((jnp.sum(loss_cols) * (0.5 / D_MODEL)).reshape(1, 1), ((0, 0), (0, N_GVEC - N_GAINS - 1)))
    gvec = jnp.concatenate([dg_pre, dg_q, dg_kv, dg_post, loss_part], axis=1)
    return grad_x, gvec


def kernel(x, positions, pre_norm_g, w_in, q_norm_g, w_uq, kv_norm_g, w_ukv, w_proj_mla, w_proj_dil, w_out, post_norm_g, loss_target, m_pre_norm_g, m_w_in, m_q_norm_g, m_w_uq, m_kv_norm_g, m_w_ukv, m_w_proj_mla, m_w_proj_dil, m_w_out, m_post_norm_g, v_pre_norm_g, v_w_in, v_q_norm_g, v_w_uq, v_kv_norm_g, v_w_ukv, v_w_proj_mla, v_w_proj_dil, v_w_out, v_post_norm_g):
    xi, yi, ci = _my_place()
    chip, me = 2 * xi + yi, 4 * xi + 2 * yi + ci
    mats = [jnp.swapaxes(w_in, 1, 2)] + [w_uq, w_ukv, w_proj_mla, w_proj_dil, w_out]
    mats = [w.reshape(w.shape[1:]).astype(BF16) for w in mats]
    links = _Links(mats, chip, me)
    gains = (pre_norm_g, q_norm_g, kv_norm_g, post_norm_g)
    grad_x, gvec = _device_grads(x[0], positions, loss_target[0], gains, links)

    sums = links.sums
    in_e = sums["in_early"]
    half_in = jnp.concatenate([in_e[:W_IN_LAT] + jnp.where(chip == 0, sums["in_late"], 0.0), in_e[W_IN_LAT:]], axis=0)
    gvec8 = jnp.pad(gvec, ((0, 7), (0, 0)))
    swapped_in, recv_gains = _swap_halves(half_in, gvec8)
    g_gains = _sum_parts(recv_gains, gvec8, me, 8, "sum_gain_parts")[0:1]
    loss = g_gains[0, N_GAINS]
    sw = lambda a: jnp.swapaxes(a, 1, 2)
    d_in, m_in, v_in, g_in = [sw(o) for o in _adamw_in(sw(w_in), sw(m_w_in), sw(v_w_in), half_in, swapped_in, ci)]
    off = [0, 1024, 1408, 1664, 2688]
    g_gain = [g_gains[:, off[i]:off[i + 1]] for i in range(4)]
    ws = [pre_norm_g, w_in, q_norm_g, w_uq, kv_norm_g, w_ukv, w_proj_mla, w_proj_dil, w_out, post_norm_g]
    ms = [m_pre_norm_g, m_w_in, m_q_norm_g, m_w_uq, m_kv_norm_g, m_w_ukv, m_w_proj_mla, m_w_proj_dil, m_w_out, m_post_norm_g]
    vs = [v_pre_norm_g, v_w_in, v_q_norm_g, v_w_uq, v_kv_norm_g, v_w_ukv, v_w_proj_mla, v_w_proj_dil, v_w_out, v_post_norm_g]
    part_of = [None, "in", None, "uq", None, "ukv", "pm", "pd", "out", None]
    gain_of = iter(g_gain)
    grads, deltas, new_m, new_v = [], [], [], []
    for i, (w, m, v, part) in enumerate(zip(ws, ms, vs, part_of)):
        if part == "in":
            d_, m_, v_, g = d_in, m_in, v_in, g_in
        elif part is not None:
            d_, m_, v_, g = _adamw_recv(w, m, v, *links.raw[part], me, f"adamw_{i}")
        else:
            g = next(gain_of)
            d_, m_, v_ = _adamw(w, g, m, v, f"adamw_{i}")
        grads.append(g)
        deltas.append(d_)
        new_m.append(m_)
        new_v.append(v_)
    return (loss, grad_x.reshape(x.shape), *grads, *deltas, *new_m, *new_v)
```

```python
import jax
import jax.numpy as jnp
from jax import lax
from jax.experimental import pallas as pl
from jax.experimental.pallas import tpu as pltpu

F32 = jnp.float32
BF16 = jnp.bfloat16

SEQ = 4096
D_MODEL = 1024
EPS = 1e-6
ROPE_THETA = 500000.0
MLA_HEADS = 8
Q_RANK = 384
KV_RANK = 256
MLA_SCALE = 96.0 ** -0.5
MLA_ROPE_HALF = 16
DIL_DILATIONS = (1, 4, 16)
DIL_ROPE_HALF = 8
DIL_SCALE = 0.125
BAND = 128

N_LAT = 768
COL_Z, COL_QKV, COL_LAT = 2048, 3072, 7680
N_PAD = 8448


def _qkv_block(tq, g, pr):
    return COL_QKV // 128 + (g * 4 + pr) * 3 + tq

IN_SPLITS = (384, 256, 32, 4608, 512, 512, 1024, 1024)

SHARD_SHAPES = ((2088, 1024), (384, 192), (256, 256), (512, 256), (512, 256), (256, 1024))
N_MATS = len(SHARD_SHAPES)
N_GAINS = 2688
N_GVEC = N_GAINS + 128

ADAM_LR, ADAM_B1, ADAM_B2, ADAM_EPS, ADAM_WD, ADAM_STEP = 0.001, 0.9, 0.999, 1e-08, 0.01, 10

VMEM_LIMIT = 56 * 1024 * 1024
NEG = -1e30
MESH = pl.DeviceIdType.MESH


def _cparams(**kw):
    return pltpu.CompilerParams(vmem_limit_bytes=VMEM_LIMIT, **kw)


def _dot(a, b, dims):
    return lax.dot_general(a, b, (dims, ((), ())), preferred_element_type=F32)


def _nn(a, b):
    return _dot(a, b, ((1,), (0,)))


def _nt(a, b):
    return _dot(a, b, ((1,), (1,)))


def _tn(a, b):
    return _dot(a, b, ((0,), (0,)))


def _rope_lanes(shape, half, period, first):
    lane = lax.broadcasted_iota(jnp.int32, shape, len(shape) - 1) % period
    return (lane >= first) & (lane < first + half), (lane >= first + half) & (lane < first + 2 * half)


def _rope_fwd(x, c, s, half, lanes):
    x1, _ = lanes
    return x * c + jnp.where(x1, pltpu.roll(x, 128 - half, 1), pltpu.roll(x, half, 1)) * s


def _rope_bwd(g, c, s, half, lanes):
    x1, x2 = lanes
    gs = g * s
    return g * c + jnp.where(x2, pltpu.roll(gs, half, 1), jnp.where(x1, pltpu.roll(gs, 128 - half, 1), 0.0))


def _sigmoid(x):
    return 1.0 / (1.0 + jnp.exp(-x))


def _after(token):
    tokens = [t for t in (token if isinstance(token, (tuple, list)) else [token]) if t is not None]
    return tokens, [pl.BlockSpec(memory_space=pl.ANY)] * len(tokens)


def _matmul(a, b, mode, out_dtype, tm, tn, tk, name, token=None, b_cols=None, a_cols=None, lane_blocks=False):
    after, after_specs = _after(token)
    if mode == "nn":
        (m, k), n = a.shape, b.shape[1]
        first = 0
        if b_cols is not None:
            first, n = b_cols[0], b_cols[1] * tn
        a_spec = pl.BlockSpec((tm, tk), lambda j, i, kk: (i, kk))
        b_spec = pl.BlockSpec((tk, tn), lambda j, i, kk: (kk, j + first))
        dot = _nn
    elif mode == "nt":
        (m, k), n = a.shape, b.shape[0]
        first = 0
        if b_cols is not None:
            first, n = b_cols[0], b_cols[1] * tn
        a_spec = pl.BlockSpec((tm, tk), lambda j, i, kk: (i, kk))
        b_spec = pl.BlockSpec((tn, tk), lambda j, i, kk: (j + first, kk))
        dot = _nt
    else:
        (k, m), n = a.shape, b.shape[1]
        first = 0
        if a_cols is not None:
            first, m = a_cols[0], a_cols[1] * tm
        a_spec = pl.BlockSpec((tk, tm), lambda j, i, kk: (kk, i + first))
        b_spec = pl.BlockSpec((tk, tn), lambda j, i, kk: (kk, j))
        dot = _tn
    assert m % tm == 0 and n % tn == 0 and k % tk == 0, (name, m, n, k, tm, tn, tk)
    nk = k // tk

    def body(a_ref, b_ref, *rest):
        o_ref, acc_ref = rest[-2:]
        kk = pl.program_id(2)
        part = dot(a_ref[...], b_ref[...])

        @pl.when(kk == 0)
        def _():
            acc_ref[...] = part

        @pl.when(kk > 0)
        def _():
            acc_ref[...] += part

        @pl.when(kk == nk - 1)
        def _():
            if lane_blocks:
                for blk in range(tn // 128):
                    o_ref[blk] = acc_ref[:, blk * 128:(blk + 1) * 128].astype(o_ref.dtype)
            else:
                o_ref[...] = acc_ref[...].astype(o_ref.dtype)

    if lane_blocks:
        out_spec = pl.BlockSpec((tn // 128, tm, 128), lambda j, i, kk: (j, i, 0))
        out_shape = jax.ShapeDtypeStruct((n // 128, m, 128), out_dtype)
    else:
        out_spec = pl.BlockSpec((tm, tn), lambda j, i, kk: (i, j))
        out_shape = jax.ShapeDtypeStruct((m, n), out_dtype)
    return pl.pallas_call(
        body, name=name, grid=(n // tn, m // tm, nk),
        in_specs=[a_spec, b_spec] + after_specs,
        out_specs=out_spec, out_shape=out_shape,
        scratch_shapes=[pltpu.VMEM((tm, tn), F32)],
        compiler_params=_cparams(),
    )(a, b, *after)


def _prenorm_fwd(x, g, token=None):
    tm = 512
    after, after_specs = _after(token)

    def body(x_ref, g_ref, *rest):
        xv = x_ref[...]
        r = lax.rsqrt(jnp.mean(xv * xv, axis=-1, keepdims=True) + EPS)
        rest[-1][...] = (xv * r * g_ref[...]).astype(BF16)

    return pl.pallas_call(
        body, name="prenorm_fwd", grid=(SEQ // tm,),
        in_specs=[pl.BlockSpec((tm, D_MODEL), lambda i: (i, 0)), pl.BlockSpec((1, D_MODEL), lambda i: (0, 0))] + after_specs,
        out_specs=pl.BlockSpec((tm, D_MODEL), lambda i: (i, 0)),
        out_shape=jax.ShapeDtypeStruct((SEQ, D_MODEL), BF16),
    )(x, g, *after)


def _dh_prenorm_bwd(dp, w_pad_t, x, dy, g, token=None):
    tm, tk = 1024, 1408
    nk = N_PAD // tk
    after, after_specs = _after(token)

    def body(a_ref, b_ref, x_ref, dy_ref, g_ref, *rest):
        gx_ref, dg_ref, acc_ref = rest[-3:]
        i, kk = pl.program_id(0), pl.program_id(1)
        part = _nn(a_ref[...], b_ref[...])

        @pl.when(kk == 0)
        def _():
            acc_ref[...] = part

        @pl.when(kk > 0)
        def _():
            acc_ref[...] += part

        @pl.when(kk == nk - 1)
        def _():
            xv = x_ref[...]
            r = lax.rsqrt(jnp.mean(xv * xv, axis=-1, keepdims=True) + EPS)
            n = xv * r
            dhv = acc_ref[...]
            dn = dhv * g_ref[...]
            gx_ref[...] = dy_ref[...] + r * (dn - n * jnp.mean(dn * n, axis=-1, keepdims=True))
            cols = jnp.sum(dhv * n, axis=0, keepdims=True)

            @pl.when(i == 0)
            def _():
                dg_ref[...] = cols

            @pl.when(i > 0)
            def _():
                dg_ref[...] += cols

    row = pl.BlockSpec((tm, D_MODEL), lambda i, kk: (i, 0))
    vec = pl.BlockSpec((1, D_MODEL), lambda i, kk: (0, 0))
    return pl.pallas_call(
        body, name="dh_prenorm_bwd", grid=(SEQ // tm, nk),
        in_specs=[pl.BlockSpec((tm, tk), lambda i, kk: (i, kk)), pl.BlockSpec((tk, D_MODEL), lambda i, kk: (kk, 0)),
                  row, row, vec] + after_specs,
        out_specs=[row, vec],
        out_shape=[jax.ShapeDtypeStruct((SEQ, D_MODEL), F32), jax.ShapeDtypeStruct((1, D_MODEL), F32)],
        scratch_shapes=[pltpu.VMEM((tm, D_MODEL), F32)],
        compiler_params=_cparams(),
    )(dp, w_pad_t, x, dy, g, *after)


def _mla_prep_fwd(p, qg, kvg, wuq, wk, wv, rc, rs):
    tm = 512

    def body(lat_ref, qg_ref, kvg_ref, wuq_ref, wk_ref, wv_ref, c_ref, s_ref, q_ref, k_ref, v_ref):
        c, s = c_ref[...], s_ref[...]
        lanes = _rope_lanes((tm, 128), MLA_ROPE_HALF, 128, 64)
        cq = lat_ref[:, 0:Q_RANK]
        r1 = lax.rsqrt(jnp.mean(cq * cq, axis=-1, keepdims=True) + EPS)
        cqn = (cq * r1 * qg_ref[...]).astype(BF16)
        q = _nn(cqn, wuq_ref[...])
        for h in range(MLA_HEADS):
            sl = slice(h * 128, (h + 1) * 128)
            q_ref[:, sl] = (_rope_fwd(q[:, sl], c, s, MLA_ROPE_HALF, lanes) * MLA_SCALE).astype(BF16)
        ckv = lat_ref[:, Q_RANK:Q_RANK + KV_RANK]
        r2 = lax.rsqrt(jnp.mean(ckv * ckv, axis=-1, keepdims=True) + EPS)
        ckvn = (ckv * r2 * kvg_ref[...]).astype(BF16)
        krr = _rope_fwd(lat_ref[:, Q_RANK + KV_RANK:N_LAT], c, s, MLA_ROPE_HALF, lanes)
        kn = _nn(ckvn, wk_ref[...])
        for h in range(MLA_HEADS):
            sl = slice(h * 128, (h + 1) * 128)
            k_ref[:, sl] = (kn[:, sl] + krr).astype(BF16)
        v_ref[...] = _nn(ckvn, wv_ref[...]).astype(BF16)

    def full(shape):
        return pl.BlockSpec(shape, lambda i: (0, 0))

    def rows(w):
        return pl.BlockSpec((tm, w), lambda i: (i, 0))

    return pl.pallas_call(
        body, name="mla_prep_fwd", grid=(SEQ // tm,),
        in_specs=[pl.BlockSpec((tm, N_LAT), lambda i: (i, 0)),
                  full((1, Q_RANK)), full((1, KV_RANK)), full((Q_RANK, 1024)), full((KV_RANK, 1024)),
                  full((KV_RANK, 512)), rows(128), rows(128)],
        out_specs=[rows(1024), rows(1024), rows(512)],
        out_shape=[jax.ShapeDtypeStruct((SEQ, 1024), BF16), jax.ShapeDtypeStruct((SEQ, 1024), BF16),
                   jax.ShapeDtypeStruct((SEQ, 512), BF16)],
        compiler_params=_cparams(),
    )(p, qg, kvg, wuq, wk, wv, rc, rs)


def _mla_prep_bwd(dp_in, p, dq, dk, dv, qg, kvg, wuq, wk, wv, rc, rs):
    tm = 512

    def body(dp_any, lat_ref, dq_ref, dk_ref, dv_ref, qg_ref, kvg_ref, wuq_ref, wk_ref, wv_ref,
             c_ref, s_ref, dp_ref, dwuq_ref, dwk_ref, dwv_ref, dgq_ref, dgkv_ref, dqb_ref, dkb_ref):
        del dp_any
        c, s = c_ref[...], s_ref[...]
        lanes = _rope_lanes((tm, 128), MLA_ROPE_HALF, 128, 64)
        lane = lax.broadcasted_iota(jnp.int32, (tm, 128), 1)
        dkr = jnp.zeros((tm, 128), F32)
        for h in range(MLA_HEADS):
            sl = slice(h * 128, (h + 1) * 128)
            dqb_ref[:, sl] = _rope_bwd(dq_ref[:, sl] * MLA_SCALE, c, s, MLA_ROPE_HALF, lanes).astype(BF16)
            dkh = dk_ref[:, sl]
            dkr = dkr + dkh
            dkb_ref[:, sl] = jnp.where(lane < 64, dkh, 0.0).astype(BF16)
        dkr = jnp.where((lane >= 64) & (lane < 96), dkr, 0.0)
        dkr = _rope_bwd(dkr, c, s, MLA_ROPE_HALF, lanes)
        dvb = dv_ref[...].astype(BF16)

        cq = lat_ref[:, 0:Q_RANK]
        r1 = lax.rsqrt(jnp.mean(cq * cq, axis=-1, keepdims=True) + EPS)
        n1 = cq * r1
        dcqn = _nt(dqb_ref[...], wuq_ref[...])
        dn1 = dcqn * qg_ref[...]
        dcq = r1 * (dn1 - n1 * jnp.mean(dn1 * n1, axis=-1, keepdims=True))
        pq = jnp.sum(dcqn * n1, axis=0, keepdims=True)

        ckv = lat_ref[:, Q_RANK:Q_RANK + KV_RANK]
        r2 = lax.rsqrt(jnp.mean(ckv * ckv, axis=-1, keepdims=True) + EPS)
        n2 = ckv * r2
        dckvn = _nt(dkb_ref[...], wk_ref[...]) + _nt(dvb, wv_ref[...])
        dn2 = dckvn * kvg_ref[...]
        dckv = r2 * (dn2 - n2 * jnp.mean(dn2 * n2, axis=-1, keepdims=True))
        pkv = jnp.sum(dckvn * n2, axis=0, keepdims=True)
        cqn = (n1 * qg_ref[...]).astype(BF16)
        ckvn = (n2 * kvg_ref[...]).astype(BF16)
        wq, wk_, wv_ = _tn(cqn, dqb_ref[...]), _tn(ckvn, dkb_ref[...]), _tn(ckvn, dvb)

        dp_ref[:, 0:Q_RANK] = dcq.astype(BF16)
        dp_ref[:, Q_RANK:Q_RANK + KV_RANK] = dckv.astype(BF16)
        dp_ref[:, Q_RANK + KV_RANK:N_LAT] = dkr.astype(BF16)

        @pl.when(pl.program_id(0) == 0)
        def _():
            dgq_ref[...] = pq
            dgkv_ref[...] = pkv
            dwuq_ref[...] = wq
            dwk_ref[...] = wk_
            dwv_ref[...] = wv_

        @pl.when(pl.program_id(0) > 0)
        def _():
            dgq_ref[...] += pq
            dgkv_ref[...] += pkv
            dwuq_ref[...] += wq
            dwk_ref[...] += wk_
            dwv_ref[...] += wv_

    def full(shape):
        return pl.BlockSpec(shape, lambda i: (0, 0))

    def rows(w):
        return pl.BlockSpec((tm, w), lambda i: (i, 0))

    lat = pl.BlockSpec((tm, N_LAT), lambda i: (i, 0))
    dlat = pl.BlockSpec((tm, N_LAT), lambda i: (i, COL_LAT // N_LAT))
    return pl.pallas_call(
        body, name="mla_prep_bwd", grid=(SEQ // tm,),
        in_specs=[pl.BlockSpec(memory_space=pl.ANY), lat, rows(1024), rows(1024), rows(512),
                  full((1, Q_RANK)), full((1, KV_RANK)), full((Q_RANK, 1024)), full((KV_RANK, 1024)),
                  full((KV_RANK, 512)), rows(128), rows(128)],
        out_specs=[dlat, full((Q_RANK, 1024)), full((KV_RANK, 1024)), full((KV_RANK, 512)),
                   full((1, Q_RANK)), full((1, KV_RANK))],
        out_shape=[jax.ShapeDtypeStruct((SEQ, N_PAD), BF16), jax.ShapeDtypeStruct((Q_RANK, 1024), F32),
                   jax.ShapeDtypeStruct((KV_RANK, 1024), F32), jax.ShapeDtypeStruct((KV_RANK, 512), F32),
                   jax.ShapeDtypeStruct((1, Q_RANK), F32), jax.ShapeDtypeStruct((1, KV_RANK), F32)],
        input_output_aliases={0: 0},
        scratch_shapes=[pltpu.VMEM((tm, 1024), BF16), pltpu.VMEM((tm, 1024), BF16)],
        compiler_params=_cparams(),
    )(dp_in, p, dq, dk, dv, qg, kvg, wuq, wk, wv, rc, rs)


FLASH_T = 1024


def _head_half(shape, hh):
    lane = lax.broadcasted_iota(jnp.int32, shape, 1)
    return (lane < 64) if hh == 0 else (lane >= 64)


def _diag_keep(nr, nk):
    row = lax.broadcasted_iota(jnp.int32, (nr, nk), 0)
    col = lax.broadcasted_iota(jnp.int32, (nr, nk), 1)
    return row + (nk - nr) >= col


def _tri_steps(nb, q_major):
    if q_major:
        pairs = [(i, kb) for i in range(nb) for kb in range(i + 1)]
    else:
        pairs = [(i, kb) for kb in range(nb) for i in range(kb, nb)]
    return jnp.asarray([p[0] for p in pairs], jnp.int32), jnp.asarray([p[1] for p in pairs], jnp.int32)


def _mla_flash_fwd(q, k, v):
    t = FLASH_T
    nb = SEQ // t
    qtab, ktab = _tri_steps(nb, True)

    def body(qi_ref, ki_ref, q_ref, k_ref, v_ref, o_ref, lse_ref, m_scr, l_scr, acc_scr):
        step = pl.program_id(1)
        i, kb = qi_ref[step], ki_ref[step]

        @pl.when(kb == 0)
        def _():
            m_scr[...] = jnp.full_like(m_scr, NEG)
            l_scr[...] = jnp.zeros_like(l_scr)
            acc_scr[...] = jnp.zeros_like(acc_scr)

        def update(r0, nr, nk, diagonal):
            rs = slice(r0, r0 + nr)
            vv = v_ref[0:nk, :]
            for hh in range(2):
                sl = slice(hh * 128, (hh + 1) * 128)
                s = _nt(q_ref[rs, sl], k_ref[0:nk, sl])
                if diagonal:
                    s = jnp.where(_diag_keep(nr, nk), s, NEG)
                m_prev = m_scr[hh, rs, :]
                m_new = jnp.maximum(m_prev, jnp.max(s, axis=-1, keepdims=True))
                pr = jnp.exp(s - jnp.tile(m_new, (1, nk // 128)))
                alpha = jnp.exp(m_prev - m_new)
                l_scr[hh, rs, :] = alpha * l_scr[hh, rs, :] + jnp.sum(pr, axis=-1, keepdims=True)
                acc_scr[hh, rs, :] = alpha * acc_scr[hh, rs, :] + _nn(pr.astype(BF16), vv)
                m_scr[hh, rs, :] = m_new

        @pl.when(kb < i)
        def _():
            update(0, t, t, False)

        @pl.when(kb == i)
        def _():
            update(0, t // 2, t // 2, True)
            update(t // 2, t // 2, t, True)
            o0 = acc_scr[0] / l_scr[0]
            o1 = acc_scr[1] / l_scr[1]
            o_ref[...] = jnp.where(_head_half((t, 128), 0), o0, o1)
            for hh in range(2):
                lse_ref[:, hh * 128:(hh + 1) * 128] = m_scr[hh] + jnp.log(l_scr[hh])

    grid_spec = pltpu.PrefetchScalarGridSpec(
        num_scalar_prefetch=2, grid=(4, qtab.shape[0]),
        in_specs=[pl.BlockSpec((t, 256), lambda j, s, qi, ki: (qi[s], j)),
                  pl.BlockSpec((t, 256), lambda j, s, qi, ki: (ki[s], j)),
                  pl.BlockSpec((t, 128), lambda j, s, qi, ki: (ki[s], j))],
        out_specs=[pl.BlockSpec((t, 128), lambda j, s, qi, ki: (qi[s], j)),
                   pl.BlockSpec((t, 256), lambda j, s, qi, ki: (qi[s], j))],
        scratch_shapes=[pltpu.VMEM((2, t, 128), F32), pltpu.VMEM((2, t, 128), F32), pltpu.VMEM((2, t, 128), F32)])
    return pl.pallas_call(
        body, name="mla_flash_fwd", grid_spec=grid_spec,
        out_shape=[jax.ShapeDtypeStruct((SEQ, 512), F32), jax.ShapeDtypeStruct((SEQ, 1024), F32)],
        compiler_params=_cparams(),
    )(qtab, ktab, q, k, v)


def _mla_flash_bwd(q, k, v, o, do, lse, token=None):
    t = FLASH_T
    nb = SEQ // t
    qtab, ktab = _tri_steps(nb, False)
    after, after_specs = _after(token)

    def body(qi_ref, ki_ref, q_ref, k_ref, v_ref, o_ref, do_ref, lse_ref, *rest):
        dq_ref, dk_ref, dv_ref, dk_scr, dv_scr = rest[-5:]
        step = pl.program_id(1)
        i, kb = qi_ref[step], ki_ref[step]

        @pl.when(step == 0)
        def _():
            dq_ref[...] = jnp.zeros_like(dq_ref)

        @pl.when(i == kb)
        def _():
            dk_scr[...] = jnp.zeros_like(dk_scr)
            dv_scr[...] = jnp.zeros_like(dv_scr)

        def update(r0, nr, nk, diagonal):
            rs = slice(r0, r0 + nr)
            vv = v_ref[0:nk, :]
            ov = o_ref[rs, :]
            dov = do_ref[rs, :]
            rows = pl.ds(pl.multiple_of(i * t + r0, t // 2), nr)
            for hh in range(2):
                sl = slice(hh * 128, (hh + 1) * 128)
                qh, kh = q_ref[rs, sl], k_ref[0:nk, sl]
                s = _nt(qh, kh)
                if diagonal:
                    s = jnp.where(_diag_keep(nr, nk), s, NEG)
                pr = jnp.exp(s - jnp.tile(lse_ref[rs, sl], (1, nk // 128)))
                dom = jnp.where(_head_half((nr, 128), hh), dov, 0.0)
                domb = dom.astype(BF16)
                dv_scr[0:nk, :] += _tn(pr.astype(BF16), domb)
                dpr = _nt(domb, vv)
                delta = jnp.sum(dom * ov, axis=-1, keepdims=True)
                ds = (pr * (dpr - delta)).astype(BF16)
                dq_ref[rows, sl] += _nn(ds, kh)
                dk_scr[hh, 0:nk, :] += _tn(ds, qh)

        @pl.when(i > kb)
        def _():
            update(0, t, t, False)

        @pl.when(i == kb)
        def _():
            update(0, t // 2, t // 2, True)
            update(t // 2, t // 2, t, True)

        @pl.when(i == nb - 1)
        def _():
            dk_ref[:, 0:128] = dk_scr[0]
            dk_ref[:, 128:256] = dk_scr[1]
            dv_ref[...] = dv_scr[...]

    qi_map = lambda j, s, qi, ki: (qi[s], j)
    ki_map = lambda j, s, qi, ki: (ki[s], j)
    grid_spec = pltpu.PrefetchScalarGridSpec(
        num_scalar_prefetch=2, grid=(4, qtab.shape[0]),
        in_specs=[pl.BlockSpec((t, 256), qi_map), pl.BlockSpec((t, 256), ki_map), pl.BlockSpec((t, 128), ki_map),
                  pl.BlockSpec((t, 128), qi_map), pl.BlockSpec((t, 128), qi_map), pl.BlockSpec((t, 256), qi_map)]
        + after_specs,
        out_specs=[pl.BlockSpec((SEQ, 256), lambda j, s, qi, ki: (0, j)), pl.BlockSpec((t, 256), ki_map),
                   pl.BlockSpec((t, 128), ki_map)],
        scratch_shapes=[pltpu.VMEM((2, t, 128), F32), pltpu.VMEM((t, 128), F32)])
    return pl.pallas_call(
        body, name="mla_flash_bwd", grid_spec=grid_spec,
        out_shape=[jax.ShapeDtypeStruct((SEQ, 1024), F32), jax.ShapeDtypeStruct((SEQ, 1024), F32),
                   jax.ShapeDtypeStruct((SEQ, 512), F32)],
        compiler_params=_cparams(),
    )(qtab, ktab, q, k, v, o, do, lse, *after)


def _strided(start, size, d):
    return pl.ds(start, size) if d == 1 else pl.ds(start, size, stride=d)


DIL_ST_FWD, DIL_ST_BWD = 1024, 2048


def _band_keep(g, b, t, nb):
    nbs = SEQ // DIL_DILATIONS[g] // BAND
    row = lax.broadcasted_iota(jnp.int32, (BAND, 2 * BAND), 0)
    col = lax.broadcasted_iota(jnp.int32, (BAND, 2 * BAND), 1)
    cur = (col >= BAND) & (row >= col - BAND)
    prev = (col < BAND) & (col >= row)
    if nbs >= nb:
        if b > 0:
            return cur | prev
        return cur | (prev & ((t * nb) % nbs != 0))
    return cur | prev if b % nbs else cur


def _dil_tok(g, b, t, nb):
    d = DIL_DILATIONS[g]
    nbs = SEQ // d // BAND
    gb = t * nb + b
    return _strided((gb % nbs) * BAND * d + gb // nbs, BAND, d)


def _dil_attn_fwd(p_qkv, rc, rs, g):
    d = DIL_DILATIONS[g]
    sub_len = SEQ // d
    ch = min(sub_len, 512)
    DIL_ST, DIL_NB = DIL_ST_FWD, DIL_ST_FWD // BAND

    def body(p_ref, c_ref, sn_ref, o_ref, l_ref, qkv_ref, x_scr, s_scr, p_scr, o_scr):
        t = pl.program_id(1)

        @pl.when(t == 0)
        def _():
            lanes = _rope_lanes((ch, 128), DIL_ROPE_HALF, 64, 0)
            for tq in range(3):
                qkv_ref[tq, 0, 0:BAND, :] = jnp.zeros((BAND, 128), BF16)
                mult = DIL_SCALE if tq == 0 else 1.0
                for c0 in range(0, SEQ, ch):
                    rows = pl.ds(c0, ch)
                    xv = p_ref[tq, rows, :].astype(F32)
                    x_scr[rows, :] = xv if tq == 2 else _rope_fwd(xv, c_ref[rows, :] * mult, sn_ref[rows, :] * mult,
                                                                   DIL_ROPE_HALF, lanes)
                for r in range(d):
                    for c0 in range(0, sub_len, ch):
                        at = BAND + r * sub_len + c0
                        qkv_ref[tq, 0, at:at + ch, :] = x_scr[_strided(r + c0 * d, ch, d), :].astype(BF16)

        base = t * DIL_ST
        half0 = _head_half((DIL_ST, 128), 0)
        lse_h = []
        for hh in range(2):
            half = _head_half((BAND, 128), hh)
            for b in range(DIL_NB):
                qv = qkv_ref[0, 0, pl.ds(pl.multiple_of(base + (b + 1) * BAND, BAND), BAND), :]
                k2 = qkv_ref[1, 0, pl.ds(pl.multiple_of(base + b * BAND, BAND), 2 * BAND), :]
                sb = _nt(jnp.where(half, qv, jnp.zeros_like(qv)), k2)
                s_scr[b * BAND:(b + 1) * BAND, :] = jnp.where(_band_keep(g, b, t, DIL_NB), sb, NEG)
            s = s_scr[...]
            m = jnp.max(s, axis=-1, keepdims=True)
            pr = jnp.exp(s - m)
            den = jnp.sum(pr, axis=-1, keepdims=True)
            p_scr[...] = pr.astype(BF16)
            for b in range(DIL_NB):
                v2 = qkv_ref[2, 0, pl.ds(pl.multiple_of(base + b * BAND, BAND), 2 * BAND), :]
                o_scr[hh, b * BAND:(b + 1) * BAND, :] = _nn(p_scr[b * BAND:(b + 1) * BAND, :], v2)
            o_scr[hh] = o_scr[hh] / den
            lse_h.append(m + jnp.log(den))
        out = jnp.where(half0, o_scr[0], o_scr[1])
        lse = jnp.where(half0, lse_h[0], lse_h[1])
        for b in range(DIL_NB):
            tok = _dil_tok(g, b, t, DIL_NB)
            o_ref[tok, :] = out[b * BAND:(b + 1) * BAND, :]
            l_ref[tok, :] = lse[b * BAND:(b + 1) * BAND, :]

    tab = pl.BlockSpec((SEQ, 128), lambda pr, t: (0, 0))
    out = pl.BlockSpec((SEQ, 128), lambda pr, t: (0, pr))
    return pl.pallas_call(
        body, name=f"dil_attn_fwd_g{g}", grid=(4, SEQ // DIL_ST),
        in_specs=[pl.BlockSpec((None, 3, SEQ, 128), lambda pr, t: (g * 4 + pr, 0, 0, 0)), tab, tab],
        out_specs=[out, out, pl.BlockSpec((3, 1, BAND + SEQ, 128), lambda pr, t: (0, pr, 0, 0))],
        out_shape=[jax.ShapeDtypeStruct((SEQ, 512), F32), jax.ShapeDtypeStruct((SEQ, 512), F32),
                   jax.ShapeDtypeStruct((3, 4, BAND + SEQ, 128), BF16)],
        scratch_shapes=[pltpu.VMEM((SEQ, 128), F32), pltpu.VMEM((DIL_ST, 2 * BAND), F32),
                        pltpu.VMEM((DIL_ST, 2 * BAND), BF16), pltpu.VMEM((2, DIL_ST, 128), F32)],
        compiler_params=_cparams(),
    )(p_qkv.reshape(12, 3, SEQ, 128), rc, rs)


def _dil_attn_bwd(dp_in, qkv, dyd, yd, lse_all, rc, rs, g, token=None):
    d = DIL_DILATIONS[g]
    sub_len = SEQ // d
    DIL_ST, DIL_NB = DIL_ST_BWD, DIL_ST_BWD // BAND
    nst = SEQ // DIL_ST
    after, after_specs = _after(token)
    ch = 512

    def body(dp_any, q_ref, k_ref, v_ref, do_ref, y_ref, l_ref, c_ref, sn_ref, *rest):
        dp_ref, tok_scr, dk_scr, dv_scr, s_scr, dp_scr, p_scr, ds_scr, do_scr, y_scr, l_scr, dq_scr = rest[-12:]
        del dp_any
        t = pl.program_id(1)
        base = t * DIL_ST

        @pl.when(t == 0)
        def _():
            dk_scr[...] = jnp.zeros_like(dk_scr)
            dv_scr[...] = jnp.zeros_like(dv_scr)

        for b in range(DIL_NB):
            tok = _dil_tok(g, b, t, DIL_NB)
            do_scr[b * BAND:(b + 1) * BAND, :] = do_ref[tok, :]
            y_scr[b * BAND:(b + 1) * BAND, :] = y_ref[tok, :]
            l_scr[b * BAND:(b + 1) * BAND, :] = l_ref[tok, :]
        for hh in range(2):
            half = _head_half((BAND, 128), hh)
            half_st = _head_half((DIL_ST, 128), hh)
            dom = jnp.where(half_st, do_scr[...], 0.0)
            delta = jnp.sum(dom * y_scr[...], axis=-1, keepdims=True)
            lcol = jnp.max(jnp.where(half_st, l_scr[...], NEG), axis=-1, keepdims=True)
            for b in range(DIL_NB):
                rows = slice(b * BAND, (b + 1) * BAND)
                qv = q_ref[0, 0, pl.ds(pl.multiple_of(base + (b + 1) * BAND, BAND), BAND), :]
                band = pl.ds(pl.multiple_of(base + b * BAND, BAND), 2 * BAND)
                sb = _nt(jnp.where(half, qv, jnp.zeros_like(qv)), k_ref[0, 0, band, :])
                s_scr[rows, :] = jnp.where(_band_keep(g, b, t, DIL_NB), sb, NEG)
                dp_scr[rows, :] = _nt(dom[rows, :].astype(BF16), v_ref[0, 0, band, :])
            pr = jnp.exp(s_scr[...] - lcol)
            p_scr[...] = pr.astype(BF16)
            ds_scr[...] = (pr * (dp_scr[...] - delta)).astype(BF16)
            for b in range(DIL_NB):
                rows = slice(b * BAND, (b + 1) * BAND)
                qv = q_ref[0, 0, pl.ds(pl.multiple_of(base + (b + 1) * BAND, BAND), BAND), :]
                band = pl.ds(pl.multiple_of(base + b * BAND, BAND), 2 * BAND)
                dqb = jnp.where(half, _nn(ds_scr[rows, :], k_ref[0, 0, band, :]), 0.0)
                if hh == 0:
                    dq_scr[rows, :] = dqb
                else:
                    dq_scr[rows, :] += dqb
                half2 = _head_half((2 * BAND, 128), hh)
                dk_scr[band, :] += jnp.where(half2, _tn(ds_scr[rows, :], qv), 0.0)
                dv_scr[band, :] += _tn(p_scr[rows, :], dom[rows, :].astype(BF16))
        for b in range(DIL_NB):
            tok_scr[pl.ds(0, 1), _dil_tok(g, b, t, DIL_NB), :] = dq_scr[b * BAND:(b + 1) * BAND, :][None]

        @pl.when(t == nst - 1)
        def _():
            for r in range(d):
                rows = _strided(r, sub_len, d)
                tok_scr[pl.ds(1, 1), rows, :] = dk_scr[BAND + r * sub_len:BAND + (r + 1) * sub_len, :][None]
                tok_scr[pl.ds(2, 1), rows, :] = dv_scr[BAND + r * sub_len:BAND + (r + 1) * sub_len, :][None]
            lanes = _rope_lanes((ch, 128), DIL_ROPE_HALF, 64, 0)
            for c0 in range(0, SEQ, ch):
                rows = slice(c0, c0 + ch)
                cv, sv = c_ref[rows, :], sn_ref[rows, :]
                dp_ref[rows, 0:128] = _rope_bwd(tok_scr[0, rows, :], cv * DIL_SCALE, sv * DIL_SCALE, DIL_ROPE_HALF, lanes).astype(BF16)
                dp_ref[rows, 128:256] = _rope_bwd(tok_scr[1, rows, :], cv, sv, DIL_ROPE_HALF, lanes).astype(BF16)
                dp_ref[rows, 256:384] = tok_scr[2, rows, :].astype(BF16)

    def inp(tq):
        return pl.BlockSpec((1, 1, BAND + SEQ, 128), lambda pr, t: (tq, pr, 0, 0))

    tok_spec = pl.BlockSpec((SEQ, 128), lambda pr, t: (0, pr))
    tab = pl.BlockSpec((SEQ, 128), lambda pr, t: (0, 0))
    st = (DIL_ST, 2 * BAND)
    return pl.pallas_call(
        body, name=f"dil_attn_bwd_g{g}", grid=(4, nst),
        in_specs=[pl.BlockSpec(memory_space=pl.ANY), inp(0), inp(1), inp(2), tok_spec, tok_spec, tok_spec, tab, tab]
        + after_specs,
        out_specs=pl.BlockSpec((SEQ, 384), lambda pr, t: (0, _qkv_block(0, g, pr) // 3)),
        out_shape=jax.ShapeDtypeStruct((SEQ, N_PAD), BF16),
        input_output_aliases={0: 0},
        scratch_shapes=[pltpu.VMEM((3, SEQ, 128), F32),
                        pltpu.VMEM((BAND + SEQ, 128), F32), pltpu.VMEM((BAND + SEQ, 128), F32),
                        pltpu.VMEM(st, F32), pltpu.VMEM(st, F32), pltpu.VMEM(st, BF16), pltpu.VMEM(st, BF16),
                        pltpu.VMEM((DIL_ST, 128), F32), pltpu.VMEM((DIL_ST, 128), F32), pltpu.VMEM((DIL_ST, 128), F32),
                        pltpu.VMEM((DIL_ST, 128), F32)],
        compiler_params=_cparams(),
    )(dp_in, qkv, qkv, qkv, dyd, yd, lse_all, rc, rs, *after)


TAIL_T = 256


def _tail(p, ya, o_g, l_g, x, target, wpm, wpd, wout, post_g):
    tm = TAIL_T

    def body(pgz_ref, ya_ref, o0_ref, o1_ref, o2_ref, l0_ref, l1_ref, l2_ref, x_ref, t_ref,
             wpm_ref, wpd_ref, wout_ref, pg_ref,
             dp_ref, dy_ref, dya_ref, dyd_ref, yd_ref, lse_ref, loss_ref, dgp_ref, dwpm_ref, dwpd_ref, dwout_ref):
        l0, l1, l2 = l0_ref[...], l1_ref[...], l2_ref[...]
        mx = jnp.maximum(jnp.maximum(l0, l1), l2)
        e0, e1, e2 = jnp.exp(l0 - mx), jnp.exp(l1 - mx), jnp.exp(l2 - mx)
        den = e0 + e1 + e2
        yd = (e0 * o0_ref[...] + e1 * o1_ref[...] + e2 * o2_ref[...]) / den
        yd_ref[...] = yd
        lse_ref[...] = mx + jnp.log(den)
        ya = ya_ref[...]

        gm, gd = pgz_ref[:, 0:1024].astype(F32), pgz_ref[:, 1024:2048].astype(F32)
        zm, zd = pgz_ref[:, 2048:2560].astype(F32), pgz_ref[:, 2560:3072].astype(F32)
        szm, szd = _sigmoid(zm), _sigmoid(zd)
        sm, sd = zm * szm, zd * szd
        ua = (ya * sm).astype(BF16)
        ud = (yd * sd).astype(BF16)
        pa = _nn(ua, wpm_ref[...])
        pd = _nn(ud, wpd_ref[...])
        sgm, sgd = _sigmoid(gm), _sigmoid(gd)
        mg = (sgm * pa + sgd * pd).astype(BF16)
        t = _nn(mg, wout_ref[...])
        r3 = lax.rsqrt(jnp.mean(t * t, axis=-1, keepdims=True) + EPS)
        n = t * r3
        pg = pg_ref[...]
        err = x_ref[...] + n * pg - t_ref[...]
        lpart = jnp.sum(err * err, axis=0, keepdims=True)

        dy = err * (1.0 / D_MODEL)
        dy_ref[...] = dy
        gpart = jnp.sum(dy * n, axis=0, keepdims=True)
        dn = dy * pg
        dt = (r3 * (dn - n * jnp.mean(dn * n, axis=-1, keepdims=True))).astype(BF16)
        dmg = _nt(dt, wout_ref[...])
        dpa = (dmg * sgm).astype(BF16)
        dpd = (dmg * sgd).astype(BF16)
        dp_ref[:, 0:1024] = (dmg * pa * sgm * (1.0 - sgm)).astype(BF16)
        dp_ref[:, 1024:2048] = (dmg * pd * sgd * (1.0 - sgd)).astype(BF16)
        dua = _nt(dpa, wpm_ref[...])
        dud = _nt(dpd, wpd_ref[...])
        dya_ref[...] = dua * sm
        dyd_ref[...] = dud * sd
        dp_ref[:, 2048:2560] = (dua * ya * szm * (1.0 + zm * (1.0 - szm))).astype(BF16)
        dp_ref[:, 2560:3072] = (dud * yd * szd * (1.0 + zd * (1.0 - szd))).astype(BF16)

        wpm, wpd, wout = _tn(ua, dpa), _tn(ud, dpd), _tn(mg, dt)

        @pl.when(pl.program_id(0) == 0)
        def _():
            loss_ref[...] = lpart
            dgp_ref[...] = gpart
            dwpm_ref[...] = wpm
            dwpd_ref[...] = wpd
            dwout_ref[...] = wout

        @pl.when(pl.program_id(0) > 0)
        def _():
            loss_ref[...] += lpart
            dgp_ref[...] += gpart
            dwpm_ref[...] += wpm
            dwpd_ref[...] += wpd
            dwout_ref[...] += wout

    def rows(w):
        return pl.BlockSpec((tm, w), lambda i: (i, 0))

    def full(shape):
        return pl.BlockSpec(shape, lambda i: (0, 0))

    def sds(w, dt):
        return jax.ShapeDtypeStruct((SEQ, w), dt)

    return pl.pallas_call(
        body, name="tail", grid=(SEQ // tm,),
        in_specs=[rows(3072), rows(512), rows(512), rows(512), rows(512), rows(512), rows(512), rows(512),
                  rows(1024), rows(1024), full((512, 1024)), full((512, 1024)), full((1024, 1024)), full((1, 1024))],
        out_specs=[rows(3072), rows(1024), rows(512), rows(512), rows(512), rows(512), full((1, 1024)), full((1, 1024)),
                   full((512, 1024)), full((512, 1024)), full((1024, 1024))],
        out_shape=[sds(N_PAD, BF16), sds(1024, F32), sds(512, F32), sds(512, F32), sds(512, F32), sds(512, F32),
                   jax.ShapeDtypeStruct((1, 1024), F32), jax.ShapeDtypeStruct((1, 1024), F32),
                   jax.ShapeDtypeStruct((512, 1024), F32), jax.ShapeDtypeStruct((512, 1024), F32),
                   jax.ShapeDtypeStruct((1024, 1024), F32)],
        compiler_params=_cparams(),
    )(p, ya, o_g[0], o_g[1], o_g[2], l_g[0], l_g[1], l_g[2], x, target, wpm, wpd, wout, post_g)


def _sum_parts(recv, own, me, tr, name):
    n, r, w = recv.shape
    if r % tr:
        return _sum_parts_cols(recv, own, me, name)
    own_spec = (pl.BlockSpec((tr, w), lambda i, me_ref: (i, 0)) if own.ndim == 2
                else pl.BlockSpec((None, tr, w), lambda i, me_ref: (me_ref[0], i, 0)))

    def body(me_ref, p_ref, own_ref, o_ref):
        mine = own_ref[...].astype(F32)
        acc = jnp.zeros((tr, w), F32)
        for s in range(n):
            acc = acc + jnp.where(me_ref[0] == s, mine, p_ref[s].astype(F32))
        o_ref[...] = acc

    return pl.pallas_call(
        body, name=name,
        grid_spec=pltpu.PrefetchScalarGridSpec(
            num_scalar_prefetch=1, grid=(r // tr,),
            in_specs=[pl.BlockSpec((n, tr, w), lambda i, me_ref: (0, i, 0)), own_spec],
            out_specs=pl.BlockSpec((tr, w), lambda i, me_ref: (i, 0))),
        out_shape=jax.ShapeDtypeStruct((r, w), F32),
    )(me.reshape(1), recv, own)


def _sum_parts_cols(recv, own, me, name):
    n, r, w = recv.shape
    tc = 128

    def body(me_ref, p_ref, own_ref, o_ref):
        mine = own_ref[...].astype(F32)
        acc = jnp.zeros((r, tc), F32)
        for s in range(n):
            acc = acc + jnp.where(me_ref[0] == s, mine, p_ref[s].astype(F32))
        o_ref[...] = acc

    return pl.pallas_call(
        body, name=name,
        grid_spec=pltpu.PrefetchScalarGridSpec(
            num_scalar_prefetch=1, grid=(w // tc,),
            in_specs=[pl.BlockSpec((n, r, tc), lambda i, me_ref: (0, 0, i)),
                      pl.BlockSpec((None, r, tc), lambda i, me_ref: (me_ref[0], 0, i))],
            out_specs=pl.BlockSpec((r, tc), lambda i, me_ref: (0, i))),
        out_shape=jax.ShapeDtypeStruct((r, w), F32),
    )(me.reshape(1), recv, own)


def _adamw(w, g, m, v, name):
    lead = w.shape[:-2]
    r, c = w.shape[-2:]
    tr = max([t for t in range(8, 257, 8) if r % t == 0], default=r)
    c1 = 1.0 - ADAM_B1 ** ADAM_STEP
    c2 = 1.0 - ADAM_B2 ** ADAM_STEP

    def body(w_ref, g_ref, m_ref, v_ref, d_ref, nm_ref, nv_ref):
        gv = g_ref[...]
        nm = ADAM_B1 * m_ref[...] + (1.0 - ADAM_B1) * gv
        nv = ADAM_B2 * v_ref[...] + (1.0 - ADAM_B2) * (gv * gv)
        nm_ref[...] = nm
        nv_ref[...] = nv
        d_ref[...] = -ADAM_LR * ((nm / c1) / (jnp.sqrt(nv / c2) + ADAM_EPS) + ADAM_WD * w_ref[...])

    zeros = (0,) * len(lead)
    spec = pl.BlockSpec((1,) * len(lead) + (tr, c), lambda i: zeros + (i, 0))
    sd = jax.ShapeDtypeStruct(w.shape, F32)
    return pl.pallas_call(
        body, name=name, grid=(r // tr,),
        in_specs=[spec] * 4, out_specs=[spec] * 3, out_shape=[sd] * 3,
    )(w, g, m, v)


def _adamw_recv(w, m, v, recv, own, me, name):
    n, r, c = recv.shape
    tr = 128
    c1 = 1.0 - ADAM_B1 ** ADAM_STEP
    c2 = 1.0 - ADAM_B2 ** ADAM_STEP

    def body(me_ref, w_ref, m_ref, v_ref, p_ref, own_ref, d_ref, nm_ref, nv_ref, g_ref):
        mine = own_ref[...].astype(F32)
        gv = jnp.zeros((tr, c), F32)
        for s in range(n):
            gv = gv + jnp.where(me_ref[0] == s, mine, p_ref[s].astype(F32))
        g_ref[0] = gv
        nm = ADAM_B1 * m_ref[0] + (1.0 - ADAM_B1) * gv
        nv = ADAM_B2 * v_ref[0] + (1.0 - ADAM_B2) * (gv * gv)
        nm_ref[0] = nm
        nv_ref[0] = nv
        d_ref[0] = -ADAM_LR * ((nm / c1) / (jnp.sqrt(nv / c2) + ADAM_EPS) + ADAM_WD * w_ref[0])

    full = pl.BlockSpec((1, tr, c), lambda i, me_ref: (0, i, 0))
    sd = jax.ShapeDtypeStruct((1, r, c), F32)
    return pl.pallas_call(
        body, name=name,
        grid_spec=pltpu.PrefetchScalarGridSpec(
            num_scalar_prefetch=1, grid=(r // tr,),
            in_specs=[full, full, full, pl.BlockSpec((n, tr, c), lambda i, me_ref: (0, i, 0)),
                      pl.BlockSpec((None, tr, c), lambda i, me_ref: (me_ref[0], i, 0))],
            out_specs=[full] * 4),
        out_shape=[sd] * 4,
    )(me.reshape(1), w, m, v, recv, own)


def _adamw_in(w_t, m_t, v_t, own_half, swapped, core):
    r, c = SHARD_SHAPES[0]
    tr = max(t for t in range(8, 257, 8) if r % t == 0)
    c1 = 1.0 - ADAM_B1 ** ADAM_STEP
    c2 = 1.0 - ADAM_B2 ** ADAM_STEP

    def body(core_ref, w_ref, m_ref, v_ref, own_ref, sw_ref, d_ref, nm_ref, nv_ref, g_ref):
        own = own_ref[...]
        col_half = lax.broadcasted_iota(jnp.int32, (tr, c), 1) // (c // 2)
        gv = jnp.where(col_half == core_ref[0], jnp.concatenate([own, own], axis=1), sw_ref[...])
        g_ref[0] = gv
        nm = ADAM_B1 * m_ref[0] + (1.0 - ADAM_B1) * gv
        nv = ADAM_B2 * v_ref[0] + (1.0 - ADAM_B2) * (gv * gv)
        nm_ref[0] = nm
        nv_ref[0] = nv
        d_ref[0] = -ADAM_LR * ((nm / c1) / (jnp.sqrt(nv / c2) + ADAM_EPS) + ADAM_WD * w_ref[0])

    full = pl.BlockSpec((1, tr, c), lambda i, core_ref: (0, i, 0))
    sd = jax.ShapeDtypeStruct((1, r, c), F32)
    return pl.pallas_call(
        body, name="adamw_in",
        grid_spec=pltpu.PrefetchScalarGridSpec(
            num_scalar_prefetch=1, grid=(r // tr,),
            in_specs=[full, full, full, pl.BlockSpec((tr, c // 2), lambda i, core_ref: (i, 0)),
                      pl.BlockSpec((tr, c), lambda i, core_ref: (i, 0))],
            out_specs=[full] * 4),
        out_shape=[sd] * 4,
    )(core.reshape(1), w_t, m_t, v_t, own_half, swapped)


ANY = pl.BlockSpec(memory_space=pl.ANY)


def _my_place():
    return lax.axis_index("x"), lax.axis_index("y"), lax.axis_index("c")


HBM = pl.BlockSpec(memory_space=pltpu.HBM)
SEM = pl.BlockSpec(memory_space=pltpu.SEMAPHORE)
DATAFLOW = pltpu.SideEffectType.DATAFLOW_SIDE_EFFECTING


def _near_chips(x, y):
    return [(1 - x, y), (x, 1 - y)]


def _half(mi, hc):
    r, c = SHARD_SHAPES[mi]
    if mi == 0:
        return pl.ds(0, r), pl.ds(pl.multiple_of(hc * (c // 2), 128), c // 2)
    return pl.ds(pl.multiple_of(hc * (r // 2), 16), r // 2), pl.ds(0, c)


def _gather_copies(land_refs, send_sems, recv_sems):
    x, y, c = _my_place()
    out, back = [], []
    for mi in range(N_MATS):
        rows, cols = _half(mi, c)
        mine = land_refs[mi].at[2 * x + y, rows, cols]
        for j, (cx, cy) in enumerate(_near_chips(x, y)):
            sems = dict(send_sem=send_sems.at[mi * 2 + j], recv_sem=recv_sems.at[mi * 2 + j],
                        device_id=(cx, cy, c), device_id_type=MESH)
            out.append(pltpu.make_async_remote_copy(src_ref=mine, dst_ref=mine, **sems))
            got = land_refs[mi].at[2 * cx + cy, rows, cols]
            back.append(pltpu.make_async_remote_copy(src_ref=got, dst_ref=got, **sems))
    return out, back


def _gather_start(landing):
    n = N_MATS

    def body(*refs):
        out, _ = _gather_copies(refs[:n], refs[n], refs[n + 1])
        for cp in out:
            cp.start()
        refs[-1][...] = jnp.zeros_like(refs[-1])

    hbm = [pltpu.HBM(a.shape, a.dtype) for a in landing]
    outs = pl.pallas_call(
        body, name="gather_start",
        out_shape=(pltpu.SemaphoreType.DMA((2 * n,)), pltpu.SemaphoreType.DMA((2 * n,)), *hbm,
                   jax.ShapeDtypeStruct((8, 128), F32)),
        in_specs=[HBM] * n, out_specs=(SEM, SEM, *[HBM] * n, pl.BlockSpec(memory_space=pltpu.VMEM)),
        input_output_aliases={i: 2 + i for i in range(n)},
        compiler_params=pltpu.CompilerParams(has_side_effects=DATAFLOW),
    )(*[pltpu.with_memory_space_constraint(a, pltpu.HBM) for a in landing])
    return outs[:-1], outs[-1]


def _gather_wait(handle, after):
    n = N_MATS

    def body(*refs):
        out, back = _gather_copies(refs[:n], refs[n], refs[n + 1])
        for cp, arrival in zip(out, back):
            cp.wait_send()
            arrival.wait_recv()

    bufs = handle[2:]
    after, after_specs = _after(after)
    res = pl.pallas_call(
        body, name="gather_wait", out_shape=tuple(pltpu.HBM(b.shape, b.dtype) for b in bufs),
        in_specs=[HBM] * n + [SEM, SEM] + after_specs, out_specs=tuple([HBM] * n),
        input_output_aliases={i: i for i in range(n)},
        compiler_params=pltpu.CompilerParams(has_side_effects=DATAFLOW),
    )(*bufs, handle[0], handle[1], *after)
    return list(res)


def _relay_share(gathered):
    n = N_MATS

    def body(*refs):
        out_refs = refs[n:2 * n]
        send_sems, recv_sems = refs[2 * n:]
        x, y, c = _my_place()
        sibling = (x, y, 1 - c)
        relayed = 2 * (x ^ (1 - c)) + (y ^ c)
        relay_to = (x ^ c, y ^ (1 - c), c)
        far = 2 * (1 - x) + (1 - y)
        near = [2 * (1 - x) + y, 2 * x + (1 - y)]

        def copy(k, mi, shard, hc, to):
            blk = out_refs[mi].at[(shard,) + _half(mi, hc)]
            return pltpu.make_async_remote_copy(src_ref=blk, dst_ref=blk, send_sem=send_sems.at[mi * 4 + k],
                                                recv_sem=recv_sems.at[mi * 4 + k], device_id=to, device_id_type=MESH)

        sends = []
        for mi in range(n):
            sends.append(copy(0, mi, relayed, c, relay_to))
            sends += [copy(1 + j, mi, near[j], c, sibling) for j in range(2)]
        for cp in sends:
            cp.start()
        for mi in range(n):
            copy(0, mi, far, c, relay_to).wait_recv()
            cp = copy(3, mi, far, c, sibling)
            cp.start()
            sends.append(cp)
        for mi in range(n):
            for j in range(2):
                copy(1 + j, mi, near[j], 1 - c, sibling).wait_recv()
            copy(3, mi, far, 1 - c, sibling).wait_recv()
        for cp in sends:
            cp.wait_send()

    return pl.pallas_call(
        body, name="relay_share",
        in_specs=[ANY] * n, out_specs=[ANY] * n,
        out_shape=[jax.ShapeDtypeStruct(g.shape, g.dtype) for g in gathered],
        input_output_aliases={i: i for i in range(n)},
        scratch_shapes=[pltpu.SemaphoreType.DMA((4 * n,)), pltpu.SemaphoreType.DMA((4 * n,))],
    )(*gathered)


def _peers(x, y, c):
    out = []
    for k in range(1, 8):
        px, py, pc = x ^ (k >> 2), y ^ ((k >> 1) & 1), c ^ (k & 1)
        out.append((k - 1, (px, py, pc), 4 * px + 2 * py + pc))
    return out


def _exchange_start(parts, name):
    n = len(parts)

    def body(*refs):
        p_refs, land_refs = refs[:n], refs[n:2 * n]
        send_sems, recv_sems, token = refs[2 * n], refs[2 * n + 1], refs[-1]
        x, y, c = _my_place()
        me = 4 * x + 2 * y + c
        for k, dev, peer in _peers(x, y, c):
            for mi in range(n):
                pltpu.make_async_remote_copy(
                    src_ref=p_refs[mi].at[peer], dst_ref=land_refs[mi].at[me], send_sem=send_sems.at[k * n + mi],
                    recv_sem=recv_sems.at[k * n + mi], device_id=dev, device_id_type=MESH).start()
        token[...] = jnp.zeros_like(token)

    hbm = [pltpu.HBM(p.shape, p.dtype) for p in parts]
    outs = pl.pallas_call(
        body, name=name + "_start",
        out_shape=(pltpu.SemaphoreType.DMA((7 * n,)), pltpu.SemaphoreType.DMA((7 * n,)), *hbm, *hbm,
                   jax.ShapeDtypeStruct((8, 128), F32)),
        in_specs=[HBM] * (2 * n), out_specs=(SEM, SEM, *[HBM] * (2 * n), pl.BlockSpec(memory_space=pltpu.VMEM)),
        input_output_aliases={i: 2 + i for i in range(2 * n)},
        compiler_params=pltpu.CompilerParams(has_side_effects=DATAFLOW),
    )(*[pltpu.with_memory_space_constraint(p, pltpu.HBM) for p in parts],
      *[pltpu.with_memory_space_constraint(lax.empty(p.shape, p.dtype), pltpu.HBM) for p in parts])
    return (name, outs[:-1]), outs[-1]


def _exchange_wait(handle, after):
    name, outs = handle
    n = (len(outs) - 2) // 2

    def body(*refs):
        p_refs, land_refs = refs[:n], refs[n:2 * n]
        send_sems, recv_sems = refs[2 * n], refs[2 * n + 1]
        x, y, c = _my_place()
        me = 4 * x + 2 * y + c
        for k, dev, peer in _peers(x, y, c):
            for mi in range(n):
                pltpu.make_async_remote_copy(
                    src_ref=p_refs[mi].at[peer], dst_ref=land_refs[mi].at[me], send_sem=send_sems.at[k * n + mi],
                    recv_sem=recv_sems.at[k * n + mi], device_id=dev, device_id_type=MESH).wait_send()
                slot = land_refs[mi].at[peer]
                pltpu.make_async_remote_copy(
                    src_ref=slot, dst_ref=slot, send_sem=send_sems.at[k * n + mi],
                    recv_sem=recv_sems.at[k * n + mi], device_id=dev, device_id_type=MESH).wait_recv()

    bufs = outs[2:]
    res = pl.pallas_call(
        body, name=name + "_wait", out_shape=tuple(pltpu.HBM(b.shape, b.dtype) for b in bufs),
        in_specs=[HBM] * (2 * n) + [SEM, SEM, ANY], out_specs=tuple([HBM] * (2 * n)),
        input_output_aliases={i: i for i in range(2 * n)},
        compiler_params=pltpu.CompilerParams(has_side_effects=DATAFLOW),
    )(*bufs, outs[0], outs[1], after)
    return list(res[n:])


def _swap_halves(half_in, gvec):
    def body(g_ref, gv_ref, out_ref, rg_ref, send_sems, recv_sems):
        x, y, c = _my_place()
        me = 4 * x + 2 * y + c
        sibling = (x, y, 1 - c)

        def half(hc):
            return out_ref.at[:, pl.ds(pl.multiple_of(hc * 512, 128), 512)]

        sends = [pltpu.make_async_remote_copy(src_ref=g_ref, dst_ref=half(c), send_sem=send_sems.at[7],
                                              recv_sem=recv_sems.at[7], device_id=sibling, device_id_type=MESH)]
        for k, dev, peer in _peers(x, y, c):
            sends.append(pltpu.make_async_remote_copy(src_ref=gv_ref, dst_ref=rg_ref.at[me], send_sem=send_sems.at[k],
                                                      recv_sem=recv_sems.at[k], device_id=dev, device_id_type=MESH))
        for cp in sends:
            cp.start()
        got = half(1 - c)
        pltpu.make_async_remote_copy(src_ref=got, dst_ref=got, send_sem=send_sems.at[7], recv_sem=recv_sems.at[7],
                                     device_id=sibling, device_id_type=MESH).wait_recv()
        for k, dev, peer in _peers(x, y, c):
            got = rg_ref.at[peer]
            pltpu.make_async_remote_copy(src_ref=got, dst_ref=got, send_sem=send_sems.at[k], recv_sem=recv_sems.at[k],
                                         device_id=dev, device_id_type=MESH).wait_recv()
        for cp in sends:
            cp.wait_send()

    return pl.pallas_call(
        body, name="swap_halves",
        in_specs=[ANY, ANY], out_specs=[ANY, ANY],
        out_shape=[jax.ShapeDtypeStruct(SHARD_SHAPES[0], F32), jax.ShapeDtypeStruct((8, 8, N_GVEC), F32)],
        scratch_shapes=[pltpu.SemaphoreType.DMA((8,)), pltpu.SemaphoreType.DMA((8,))],
    )(half_in, gvec)


def _set_slot(arr, block, idx):
    return lax.dynamic_update_slice(arr, block[None], (idx,) + (0,) * block.ndim)


PAD_RUNS = ((6304, 8352, 0), (5280, 6304, COL_Z), (672, 5280, COL_QKV), (0, 640, COL_LAT), (640, 672, COL_LAT + 704))
N_QKV = COL_LAT - COL_QKV


def _qkv_rows_regroup(a, to_padded):
    if to_padded:
        a4 = a.reshape(3, 12, 128, a.shape[1])
        return jnp.stack([a4[0], a4[1], a4[2]], axis=1).reshape(a.shape)
    a4 = a.reshape(12, 3, 128, a.shape[1])
    return jnp.concatenate([a4[:, tq].reshape(N_QKV // 3, a.shape[1]) for tq in range(3)], axis=0)
W_IN_SHARD = 2088


def _full_weights(gathered):
    def cols(a):
        return jnp.concatenate([a[s] for s in range(4)], axis=1)

    w_uq, w_ukv, w_pm, w_pd = [cols(a) for a in gathered[1:5]]
    w_out = gathered[5].reshape(D_MODEL, D_MODEL)
    w_in_t = gathered[0].reshape(4 * W_IN_SHARD, D_MODEL)
    pieces, at = [], 0
    for lo, hi, pad_lo in sorted(PAD_RUNS, key=lambda t: t[2]):
        if pad_lo > at:
            pieces.append(jnp.zeros((pad_lo - at, D_MODEL), w_in_t.dtype))
        pieces.append(_qkv_rows_regroup(w_in_t[lo:hi], True) if pad_lo == COL_QKV else w_in_t[lo:hi])
        at = pad_lo + hi - lo
    pieces.append(jnp.zeros((N_PAD - at, D_MODEL), w_in_t.dtype))
    w_pad_t = jnp.concatenate(pieces, axis=0)
    z32 = jnp.zeros((Q_RANK, 32), w_uq.dtype)
    wuq_pad = jnp.concatenate([t for h in range(MLA_HEADS) for t in (w_uq[:, h * 96:(h + 1) * 96], z32)], axis=1)
    z64 = jnp.zeros((KV_RANK, 64), w_ukv.dtype)
    wk_pad = jnp.concatenate([t for h in range(MLA_HEADS) for t in (w_ukv[:, h * 128:h * 128 + 64], z64)], axis=1)
    wv = jnp.concatenate([w_ukv[:, h * 128 + 64:(h + 1) * 128] for h in range(MLA_HEADS)], axis=1)
    return w_pad_t, wuq_pad, wk_pad, wv, w_pm, w_pd, w_out


W_IN_LAT = 672


def _grad_parts_in_early(dwt_early):
    dwt_early = jnp.concatenate([dwt_early[:COL_QKV], _qkv_rows_regroup(dwt_early[COL_QKV:COL_LAT], False)], axis=0)

    def in_block(s, h):
        cols = slice(h * 512, (h + 1) * 512)
        out = []
        for lo, hi, pad_lo in sorted(PAD_RUNS):
            a_, b_ = max(lo, s * W_IN_SHARD), min(hi, (s + 1) * W_IN_SHARD)
            if a_ < b_:
                out.append(jnp.zeros((b_ - a_, 512), dwt_early.dtype) if pad_lo >= COL_LAT
                           else dwt_early[pad_lo + a_ - lo:pad_lo + b_ - lo, cols])
        return jnp.concatenate(out, axis=0)

    return jnp.stack([in_block(s, h) for s in range(4) for h in range(2)])


def _grad_parts_in_late(dwt_late):
    rows = jnp.concatenate([dwt_late[0:640], dwt_late[704:736]], axis=0)
    zero = jnp.zeros((W_IN_LAT, 512), dwt_late.dtype)
    return jnp.stack([rows[:, 0:512], rows[:, 512:1024]] + [zero] * 6)


def _shard_blocks(m, axis=1):
    n = m.shape[axis] // 4
    cut = (lambda s: m[:, s * n:(s + 1) * n]) if axis == 1 else (lambda s: m[s * n:(s + 1) * n])
    return jnp.stack([cut(s) for s in range(4) for _ in range(2)])


def _grad_parts_mla(dwuq_pad, dwk_pad, dwv):
    d_uq = jnp.concatenate([dwuq_pad[:, h * 128:h * 128 + 96] for h in range(MLA_HEADS)], axis=1)
    d_ukv = jnp.concatenate([t for h in range(MLA_HEADS) for t in (dwk_pad[:, h * 128:h * 128 + 64], dwv[:, h * 64:(h + 1) * 64])],
                            axis=1)
    return [_shard_blocks(d_uq.astype(BF16)), _shard_blocks(d_ukv.astype(BF16))]


def _rope_tables(positions, token=None):
    pos = positions.reshape(SEQ).astype(F32)
    if token is not None:
        pos = pos + token[0, 0]
    lane = jnp.arange(128)

    def table(rot, first, period):
        inv = ROPE_THETA ** (-jnp.arange(0, rot, 2, dtype=F32) / rot)
        half = rot // 2
        off = lane % period - first
        in1, in2 = (off >= 0) & (off < half), (off >= half) & (off < rot)
        inv_lane = jnp.where(in1 | in2, inv[jnp.clip(off % half, 0, half - 1)], 0.0)
        sign = jnp.where(in1, -1.0, 1.0).astype(F32)
        ang = pos[:, None] * inv_lane[None, :]
        return jnp.cos(ang), jnp.sin(ang) * sign[None, :]

    return table(32, 64, 128), table(16, 0, 64)


class _Links:
    def __init__(self, mats, chip, me):
        landing = [_set_slot(lax.empty((4,) + m.shape, m.dtype), m, chip) for m in mats]
        self.gather, self.token = _gather_start(landing)
        self.me, self.sent, self.handles, self.sums, self.raw = me, {}, {}, {}, {}

    def weights(self, after):
        return _relay_share(_gather_wait(self.gather, after))

    def send(self, blocks, name):
        self.sent[name] = blocks
        self.handles[name], token = _exchange_start(blocks, name)
        return token

    def collect(self, name, after, parts):
        recv = _exchange_wait(self.handles[name], after)
        for r, own, part in zip(recv, self.sent[name], parts):
            if part.startswith("in_"):
                self.sums[part] = _sum_parts(r, own, self.me, 64, "sum_grad_" + part)
            else:
                self.raw[part] = (r, own)
        return tuple(self.sums[part] for part in parts if part in self.sums)


def _device_grads(x, positions, target, gains, links):
    pre_g, q_g, kv_g, post_g = gains
    (mc, ms), (dc, ds) = _rope_tables(positions, links.token)
    h = _prenorm_fwd(x, pre_g, links.token)
    w_pad_t, wuq_pad, wk_pad, wv, w_pm, w_pd, w_out = _full_weights(links.weights((h, mc, ms, dc, ds)))

    p_gz = _matmul(h, w_pad_t, "nt", BF16, 1024, 1536, 1024, "in_proj_gates", b_cols=(0, COL_QKV // 1536))
    p_qkv = _matmul(h, w_pad_t, "nt", BF16, 1024, 1536, 1024, "in_proj_dilated", b_cols=(COL_QKV // 1536, N_QKV // 1536),
                    lane_blocks=True)
    p_lat = _matmul(h, w_pad_t, "nt", F32, 1024, N_LAT, 1024, "in_proj_latent", b_cols=(COL_LAT // N_LAT, 1))
    q, k, v = _mla_prep_fwd(p_lat, q_g, kv_g, wuq_pad, wk_pad, wv, mc, ms)
    ya, lse_m = _mla_flash_fwd(q, k, v)
    o_g, l_g, qkv = zip(*[_dil_attn_fwd(p_qkv, dc, ds, g) for g in range(3)])
    (dp, dy, dya, dyd, yd, lse_d, loss_cols, dg_post, dwpm, dwpd, dwout) = _tail(
        p_gz, ya, o_g, l_g, x, target, w_pm, w_pd, w_out, post_g)

    for g in range(3):
        dp = _dil_attn_bwd(dp, qkv[g], dyd, yd, lse_d, dc, ds, g)
    dw_early = _matmul(dp, h, "tn", BF16, 1536, 1024, 2048, "dw_in_early", a_cols=(0, COL_LAT // 1536))
    token = links.send([_grad_parts_in_early(dw_early), _shard_blocks(dwpm.astype(BF16)), _shard_blocks(dwpd.astype(BF16)),
                        _shard_blocks(dwout.astype(BF16), axis=0)], "exchange_early")

    dq, dk, dv = _mla_flash_bwd(q, k, v, ya, dya, lse_m, token)
    dp, dwuq_pad, dwk_pad, dwv, dg_q, dg_kv = _mla_prep_bwd(dp, p_lat, dq, dk, dv, q_g, kv_g, wuq_pad, wk_pad, wv, mc, ms)
    dw_late = _matmul(dp, h, "tn", BF16, N_LAT, 1024, 2048, "dw_in_late", a_cols=(COL_LAT // N_LAT, 1))
    token = links.send([_grad_parts_in_late(dw_late)] + _grad_parts_mla(dwuq_pad, dwk_pad, dwv), "exchange_late")
    early = links.collect("exchange_early", dw_late, ("in_early", "pm", "pd", "out"))

    grad_x, dg_pre = _dh_prenorm_bwd(dp, w_pad_t, x, dy, pre_g, (token,) + tuple(early))
    links.collect("exchange_late", grad_x, ("in_late", "uq", "ukv"))

    loss_part = jnp.pad((jnp.sum(loss_cols) * (0.5 / D_MODEL)).reshape(1, 1), ((0, 0), (0, N_GVEC - N_GAINS - 1)))
    gvec = jnp.concatenate([dg_pre, dg_q, dg_kv, dg_post, loss_part], axis=1)
    return grad_x, gvec


def kernel(x, positions, pre_norm_g, w_in, q_norm_g, w_uq, kv_norm_g, w_ukv, w_proj_mla, w_proj_dil, w_out, post_norm_g, loss_target, m_pre_norm_g, m_w_in, m_q_norm_g, m_w_uq, m_kv_norm_g, m_w_ukv, m_w_proj_mla, m_w_proj_dil, m_w_out, m_post_norm_g, v_pre_norm_g, v_w_in, v_q_norm_g, v_w_uq, v_kv_norm_g, v_w_ukv, v_w_proj_mla, v_w_proj_dil, v_w_out, v_post_norm_g):
    xi, yi, ci = _my_place()
    chip, me = 2 * xi + yi, 4 * xi + 2 * yi + ci
    mats = [jnp.swapaxes(w_in, 1, 2)] + [w_uq, w_ukv, w_proj_mla, w_proj_dil, w_out]
    mats = [w.reshape(w.shape[1:]).astype(BF16) for w in mats]
    links = _Links(mats, chip, me)
    gains = (pre_norm_g, q_norm_g, kv_norm_g, post_norm_g)
    grad_x, gvec = _device_grads(x[0], positions, loss_target[0], gains, links)

    sums = links.sums
    in_e = sums["in_early"]
    half_in = jnp.concatenate([in_e[:W_IN_LAT] + jnp.where(chip == 0, sums["in_late"], 0.0), in_e[W_IN_LAT:]], axis=0)
    gvec8 = jnp.pad(gvec, ((0, 7), (0, 0)))
    swapped_in, recv_gains = _swap_halves(half_in, gvec8)
    g_gains = _sum_parts(recv_gains, gvec8, me, 8, "sum_gain_parts")[0:1]
    loss = g_gains[0, N_GAINS]
    sw = lambda a: jnp.swapaxes(a, 1, 2)
    d_in, m_in, v_in, g_in = [sw(o) for o in _adamw_in(sw(w_in), sw(m_w_in), sw(v_w_in), half_in, swapped_in, ci)]
    off = [0, 1024, 1408, 1664, 2688]
    g_gain = [g_gains[:, off[i]:off[i + 1]] for i in range(4)]
    ws = [pre_norm_g, w_in, q_norm_g, w_uq, kv_norm_g, w_ukv, w_proj_mla, w_proj_dil, w_out, post_norm_g]
    ms = [m_pre_norm_g, m_w_in, m_q_norm_g, m_w_uq, m_kv_norm_g, m_w_ukv, m_w_proj_mla, m_w_proj_dil, m_w_out, m_post_norm_g]
    vs = [v_pre_norm_g, v_w_in, v_q_norm_g, v_w_uq, v_kv_norm_g, v_w_ukv, v_w_proj_mla, v_w_proj_dil, v_w_out, v_post_norm_g]
    part_of = [None, "in", None, "uq", None, "ukv", "pm", "pd", "out", None]
    gain_of = iter(g_gain)
    grads, deltas, new_m, new_v = [], [], [], []
    for i, (w, m, v, part) in enumerate(zip(ws, ms, vs, part_of)):
        if part == "in":
            d_, m_, v_, g = d_in, m_in, v_in, g_in
        elif part is not None:
            d_, m_, v_, g = _adamw_recv(w, m, v, *links.raw[part], me, f"adamw_{i}")
        else:
            g = next(gain_of)
            d_, m_, v_ = _adamw(w, g, m, v, f"adamw_{i}")
        grads.append(g)
        deltas.append(d_)
        new_m.append(m_)
        new_v.append(v_)
    return (loss, grad_x.reshape(x.shape), *grads, *deltas, *new_m, *new_v)
```

```python
import jax
import jax.numpy as jnp
from jax import lax
from jax.experimental import pallas as pl
from jax.experimental.pallas import tpu as pltpu

F32 = jnp.float32
BF16 = jnp.bfloat16

SEQ = 4096
D_MODEL = 1024
EPS = 1e-6
ROPE_THETA = 500000.0
MLA_HEADS = 8
Q_RANK = 384
KV_RANK = 256
MLA_SCALE = 96.0 ** -0.5
MLA_ROPE_HALF = 16
DIL_DILATIONS = (1, 4, 16)
DIL_ROPE_HALF = 8
DIL_SCALE = 0.125
BAND = 128

N_LAT = 768
COL_Z, COL_QKV, COL_LAT = 2048, 3072, 7680
N_PAD = 8448


def _qkv_block(tq, g, pr):
    return COL_QKV // 128 + (g * 4 + pr) * 3 + tq

IN_SPLITS = (384, 256, 32, 4608, 512, 512, 1024, 1024)

SHARD_SHAPES = ((2088, 1024), (384, 192), (256, 256), (512, 256), (512, 256), (256, 1024))
N_MATS = len(SHARD_SHAPES)
N_GAINS = 2688
N_GVEC = N_GAINS + 128

ADAM_LR, ADAM_B1, ADAM_B2, ADAM_EPS, ADAM_WD, ADAM_STEP = 0.001, 0.9, 0.999, 1e-08, 0.01, 10

VMEM_LIMIT = 56 * 1024 * 1024
NEG = -1e30
MESH = pl.DeviceIdType.MESH


def _cparams(**kw):
    return pltpu.CompilerParams(vmem_limit_bytes=VMEM_LIMIT, **kw)


def _dot(a, b, dims):
    return lax.dot_general(a, b, (dims, ((), ())), preferred_element_type=F32)


def _nn(a, b):
    return _dot(a, b, ((1,), (0,)))


def _nt(a, b):
    return _dot(a, b, ((1,), (1,)))


def _tn(a, b):
    return _dot(a, b, ((0,), (0,)))


def _rope_lanes(shape, half, period, first):
    lane = lax.broadcasted_iota(jnp.int32, shape, len(shape) - 1) % period
    return (lane >= first) & (lane < first + half), (lane >= first + half) & (lane < first + 2 * half)


def _rope_fwd(x, c, s, half, lanes):
    x1, _ = lanes
    return x * c + jnp.where(x1, pltpu.roll(x, 128 - half, 1), pltpu.roll(x, half, 1)) * s


def _rope_bwd(g, c, s, half, lanes):
    x1, x2 = lanes
    gs = g * s
    return g * c + jnp.where(x2, pltpu.roll(gs, half, 1), jnp.where(x1, pltpu.roll(gs, 128 - half, 1), 0.0))


def _sigmoid(x):
    return 1.0 / (1.0 + jnp.exp(-x))


def _after(token):
    tokens = [t for t in (token if isinstance(token, (tuple, list)) else [token]) if t is not None]
    return tokens, [pl.BlockSpec(memory_space=pl.ANY)] * len(tokens)


def _matmul(a, b, mode, out_dtype, tm, tn, tk, name, token=None, b_cols=None, a_cols=None, lane_blocks=False):
    after, after_specs = _after(token)
    if mode == "nn":
        (m, k), n = a.shape, b.shape[1]
        first = 0
        if b_cols is not None:
            first, n = b_cols[0], b_cols[1] * tn
        a_spec = pl.BlockSpec((tm, tk), lambda j, i, kk: (i, kk))
        b_spec = pl.BlockSpec((tk, tn), lambda j, i, kk: (kk, j + first))
        dot = _nn
    elif mode == "nt":
        (m, k), n = a.shape, b.shape[0]
        first = 0
        if b_cols is not None:
            first, n = b_cols[0], b_cols[1] * tn
        a_spec = pl.BlockSpec((tm, tk), lambda j, i, kk: (i, kk))
        b_spec = pl.BlockSpec((tn, tk), lambda j, i, kk: (j + first, kk))
        dot = _nt
    else:
        (k, m), n = a.shape, b.shape[1]
        first = 0
        if a_cols is not None:
            first, m = a_cols[0], a_cols[1] * tm
        a_spec = pl.BlockSpec((tk, tm), lambda j, i, kk: (kk, i + first))
        b_spec = pl.BlockSpec((tk, tn), lambda j, i, kk: (kk, j))
        dot = _tn
    assert m % tm == 0 and n % tn == 0 and k % tk == 0, (name, m, n, k, tm, tn, tk)
    nk = k // tk

    def write(o_ref, val):
        if lane_blocks:
            for blk in range(tn // 128):
                o_ref[blk] = val[:, blk * 128:(blk + 1) * 128].astype(o_ref.dtype)
        else:
            o_ref[...] = val.astype(o_ref.dtype)

    def body_single(a_ref, b_ref, *rest):
        write(rest[-1], dot(a_ref[...], b_ref[...]))

    def body_accumulate(a_ref, b_ref, *rest):
        o_ref, acc_ref = rest[-2:]
        kk = pl.program_id(2)
        part = dot(a_ref[...], b_ref[...])

        @pl.when(kk == 0)
        def _():
            acc_ref[...] = part

        @pl.when(kk > 0)
        def _():
            acc_ref[...] += part

        @pl.when(kk == nk - 1)
        def _():
            write(o_ref, acc_ref[...])

    if lane_blocks:
        out_spec = pl.BlockSpec((tn // 128, tm, 128), lambda j, i, kk: (j, i, 0))
        out_shape = jax.ShapeDtypeStruct((n // 128, m, 128), out_dtype)
    else:
        out_spec = pl.BlockSpec((tm, tn), lambda j, i, kk: (i, j))
        out_shape = jax.ShapeDtypeStruct((m, n), out_dtype)
    return pl.pallas_call(
        body_single if nk == 1 else body_accumulate, name=name, grid=(n // tn, m // tm, nk),
        in_specs=[a_spec, b_spec] + after_specs,
        out_specs=out_spec, out_shape=out_shape,
        scratch_shapes=[] if nk == 1 else [pltpu.VMEM((tm, tn), F32)],
        compiler_params=_cparams(),
    )(a, b, *after)


def _prenorm_fwd(x, g, token=None):
    tm = 512
    after, after_specs = _after(token)

    def body(x_ref, g_ref, *rest):
        xv = x_ref[...]
        r = lax.rsqrt(jnp.mean(xv * xv, axis=-1, keepdims=True) + EPS)
        rest[-1][...] = (xv * r * g_ref[...]).astype(BF16)

    return pl.pallas_call(
        body, name="prenorm_fwd", grid=(SEQ // tm,),
        in_specs=[pl.BlockSpec((tm, D_MODEL), lambda i: (i, 0)), pl.BlockSpec((1, D_MODEL), lambda i: (0, 0))] + after_specs,
        out_specs=pl.BlockSpec((tm, D_MODEL), lambda i: (i, 0)),
        out_shape=jax.ShapeDtypeStruct((SEQ, D_MODEL), BF16),
    )(x, g, *after)


def _dh_prenorm_bwd(dp, w_pad_t, x, dy, g, token=None):
    tm, tk = 1024, 1408
    nk = N_PAD // tk
    after, after_specs = _after(token)

    def body(a_ref, b_ref, x_ref, dy_ref, g_ref, *rest):
        gx_ref, dg_ref, acc_ref = rest[-3:]
        i, kk = pl.program_id(0), pl.program_id(1)
        part = _nn(a_ref[...], b_ref[...])

        @pl.when(kk == 0)
        def _():
            acc_ref[...] = part

        @pl.when(kk > 0)
        def _():
            acc_ref[...] += part

        @pl.when(kk == nk - 1)
        def _():
            xv = x_ref[...]
            r = lax.rsqrt(jnp.mean(xv * xv, axis=-1, keepdims=True) + EPS)
            n = xv * r
            dhv = acc_ref[...]
            dn = dhv * g_ref[...]
            gx_ref[...] = dy_ref[...] + r * (dn - n * jnp.mean(dn * n, axis=-1, keepdims=True))
            cols = jnp.sum(dhv * n, axis=0, keepdims=True)

            @pl.when(i == 0)
            def _():
                dg_ref[...] = cols

            @pl.when(i > 0)
            def _():
                dg_ref[...] += cols

    row = pl.BlockSpec((tm, D_MODEL), lambda i, kk: (i, 0))
    vec = pl.BlockSpec((1, D_MODEL), lambda i, kk: (0, 0))
    return pl.pallas_call(
        body, name="dh_prenorm_bwd", grid=(SEQ // tm, nk),
        in_specs=[pl.BlockSpec((tm, tk), lambda i, kk: (i, kk)), pl.BlockSpec((tk, D_MODEL), lambda i, kk: (kk, 0)),
                  row, row, vec] + after_specs,
        out_specs=[row, vec],
        out_shape=[jax.ShapeDtypeStruct((SEQ, D_MODEL), F32), jax.ShapeDtypeStruct((1, D_MODEL), F32)],
        scratch_shapes=[pltpu.VMEM((tm, D_MODEL), F32)],
        compiler_params=_cparams(),
    )(dp, w_pad_t, x, dy, g, *after)


def _mla_prep_fwd(p, qg, kvg, wuq, wk, wv, rc, rs):
    tm = 512

    def body(lat_ref, qg_ref, kvg_ref, wuq_ref, wk_ref, wv_ref, c_ref, s_ref, q_ref, k_ref, v_ref):
        c, s = c_ref[...], s_ref[...]
        lanes = _rope_lanes((tm, 128), MLA_ROPE_HALF, 128, 64)
        cq = lat_ref[:, 0:Q_RANK]
        r1 = lax.rsqrt(jnp.mean(cq * cq, axis=-1, keepdims=True) + EPS)
        cqn = (cq * r1 * qg_ref[...]).astype(BF16)
        q = _nn(cqn, wuq_ref[...])
        for h in range(MLA_HEADS):
            sl = slice(h * 128, (h + 1) * 128)
            q_ref[:, sl] = (_rope_fwd(q[:, sl], c, s, MLA_ROPE_HALF, lanes) * MLA_SCALE).astype(BF16)
        ckv = lat_ref[:, Q_RANK:Q_RANK + KV_RANK]
        r2 = lax.rsqrt(jnp.mean(ckv * ckv, axis=-1, keepdims=True) + EPS)
        ckvn = (ckv * r2 * kvg_ref[...]).astype(BF16)
        krr = _rope_fwd(lat_ref[:, Q_RANK + KV_RANK:N_LAT], c, s, MLA_ROPE_HALF, lanes)
        kn = _nn(ckvn, wk_ref[...])
        for h in range(MLA_HEADS):
            sl = slice(h * 128, (h + 1) * 128)
            k_ref[:, sl] = (kn[:, sl] + krr).astype(BF16)
        v_ref[...] = _nn(ckvn, wv_ref[...]).astype(BF16)

    def full(shape):
        return pl.BlockSpec(shape, lambda i: (0, 0))

    def rows(w):
        return pl.BlockSpec((tm, w), lambda i: (i, 0))

    return pl.pallas_call(
        body, name="mla_prep_fwd", grid=(SEQ // tm,),
        in_specs=[pl.BlockSpec((tm, N_LAT), lambda i: (i, 0)),
                  full((1, Q_RANK)), full((1, KV_RANK)), full((Q_RANK, 1024)), full((KV_RANK, 1024)),
                  full((KV_RANK, 512)), rows(128), rows(128)],
        out_specs=[rows(1024), rows(1024), rows(512)],
        out_shape=[jax.ShapeDtypeStruct((SEQ, 1024), BF16), jax.ShapeDtypeStruct((SEQ, 1024), BF16),
                   jax.ShapeDtypeStruct((SEQ, 512), BF16)],
        compiler_params=_cparams(),
    )(p, qg, kvg, wuq, wk, wv, rc, rs)


def _mla_prep_bwd(dp_in, p, dq, dk, dv, qg, kvg, wuq, wk, wv, rc, rs):
    tm = 512

    def body(dp_any, lat_ref, dq_ref, dk_ref, dv_ref, qg_ref, kvg_ref, wuq_ref, wk_ref, wv_ref,
             c_ref, s_ref, dp_ref, dwuq_ref, dwk_ref, dwv_ref, dgq_ref, dgkv_ref, dqb_ref, dkb_ref):
        del dp_any
        c, s = c_ref[...], s_ref[...]
        lanes = _rope_lanes((tm, 128), MLA_ROPE_HALF, 128, 64)
        lane = lax.broadcasted_iota(jnp.int32, (tm, 128), 1)
        dkr = jnp.zeros((tm, 128), F32)
        for h in range(MLA_HEADS):
            sl = slice(h * 128, (h + 1) * 128)
            dqb_ref[:, sl] = _rope_bwd(dq_ref[:, sl] * MLA_SCALE, c, s, MLA_ROPE_HALF, lanes).astype(BF16)
            dkh = dk_ref[:, sl]
            dkr = dkr + dkh
            dkb_ref[:, sl] = jnp.where(lane < 64, dkh, 0.0).astype(BF16)
        dkr = jnp.where((lane >= 64) & (lane < 96), dkr, 0.0)
        dkr = _rope_bwd(dkr, c, s, MLA_ROPE_HALF, lanes)
        dvb = dv_ref[...].astype(BF16)

        cq = lat_ref[:, 0:Q_RANK]
        r1 = lax.rsqrt(jnp.mean(cq * cq, axis=-1, keepdims=True) + EPS)
        n1 = cq * r1
        dcqn = _nt(dqb_ref[...], wuq_ref[...])
        dn1 = dcqn * qg_ref[...]
        dcq = r1 * (dn1 - n1 * jnp.mean(dn1 * n1, axis=-1, keepdims=True))
        pq = jnp.sum(dcqn * n1, axis=0, keepdims=True)

        ckv = lat_ref[:, Q_RANK:Q_RANK + KV_RANK]
        r2 = lax.rsqrt(jnp.mean(ckv * ckv, axis=-1, keepdims=True) + EPS)
        n2 = ckv * r2
        dckvn = _nt(dkb_ref[...], wk_ref[...]) + _nt(dvb, wv_ref[...])
        dn2 = dckvn * kvg_ref[...]
        dckv = r2 * (dn2 - n2 * jnp.mean(dn2 * n2, axis=-1, keepdims=True))
        pkv = jnp.sum(dckvn * n2, axis=0, keepdims=True)
        cqn = (n1 * qg_ref[...]).astype(BF16)
        ckvn = (n2 * kvg_ref[...]).astype(BF16)
        wq, wk_, wv_ = _tn(cqn, dqb_ref[...]), _tn(ckvn, dkb_ref[...]), _tn(ckvn, dvb)

        dp_ref[:, 0:Q_RANK] = dcq.astype(BF16)
        dp_ref[:, Q_RANK:Q_RANK + KV_RANK] = dckv.astype(BF16)
        dp_ref[:, Q_RANK + KV_RANK:N_LAT] = dkr.astype(BF16)

        @pl.when(pl.program_id(0) == 0)
        def _():
            dgq_ref[...] = pq
            dgkv_ref[...] = pkv
            dwuq_ref[...] = wq
            dwk_ref[...] = wk_
            dwv_ref[...] = wv_

        @pl.when(pl.program_id(0) > 0)
        def _():
            dgq_ref[...] += pq
            dgkv_ref[...] += pkv
            dwuq_ref[...] += wq
            dwk_ref[...] += wk_
            dwv_ref[...] += wv_

    def full(shape):
        return pl.BlockSpec(shape, lambda i: (0, 0))

    def rows(w):
        return pl.BlockSpec((tm, w), lambda i: (i, 0))

    lat = pl.BlockSpec((tm, N_LAT), lambda i: (i, 0))
    dlat = pl.BlockSpec((tm, N_LAT), lambda i: (i, COL_LAT // N_LAT))
    return pl.pallas_call(
        body, name="mla_prep_bwd", grid=(SEQ // tm,),
        in_specs=[pl.BlockSpec(memory_space=pl.ANY), lat, rows(1024), rows(1024), rows(512),
                  full((1, Q_RANK)), full((1, KV_RANK)), full((Q_RANK, 1024)), full((KV_RANK, 1024)),
                  full((KV_RANK, 512)), rows(128), rows(128)],
        out_specs=[dlat, full((Q_RANK, 1024)), full((KV_RANK, 1024)), full((KV_RANK, 512)),
                   full((1, Q_RANK)), full((1, KV_RANK))],
        out_shape=[jax.ShapeDtypeStruct((SEQ, N_PAD), BF16), jax.ShapeDtypeStruct((Q_RANK, 1024), F32),
                   jax.ShapeDtypeStruct((KV_RANK, 1024), F32), jax.ShapeDtypeStruct((KV_RANK, 512), F32),
                   jax.ShapeDtypeStruct((1, Q_RANK), F32), jax.ShapeDtypeStruct((1, KV_RANK), F32)],
        input_output_aliases={0: 0},
        scratch_shapes=[pltpu.VMEM((tm, 1024), BF16), pltpu.VMEM((tm, 1024), BF16)],
        compiler_params=_cparams(),
    )(dp_in, p, dq, dk, dv, qg, kvg, wuq, wk, wv, rc, rs)


FLASH_T = 1024


def _head_half(shape, hh):
    lane = lax.broadcasted_iota(jnp.int32, shape, 1)
    return (lane < 64) if hh == 0 else (lane >= 64)


def _diag_keep(nr, nk):
    row = lax.broadcasted_iota(jnp.int32, (nr, nk), 0)
    col = lax.broadcasted_iota(jnp.int32, (nr, nk), 1)
    return row + (nk - nr) >= col


def _tri_steps(nb, q_major):
    if q_major:
        pairs = [(i, kb) for i in range(nb) for kb in range(i + 1)]
    else:
        pairs = [(i, kb) for kb in range(nb) for i in range(kb, nb)]
    return jnp.asarray([p[0] for p in pairs], jnp.int32), jnp.asarray([p[1] for p in pairs], jnp.int32)


def _mla_flash_fwd(q, k, v):
    t = FLASH_T
    nb = SEQ // t
    qtab, ktab = _tri_steps(nb, True)

    def body(qi_ref, ki_ref, q_ref, k_ref, v_ref, o_ref, lse_ref, m_scr, l_scr, acc_scr):
        step = pl.program_id(1)
        i, kb = qi_ref[step], ki_ref[step]

        @pl.when(kb == 0)
        def _():
            m_scr[...] = jnp.full_like(m_scr, NEG)
            l_scr[...] = jnp.zeros_like(l_scr)
            acc_scr[...] = jnp.zeros_like(acc_scr)

        def update(r0, nr, nk, diagonal):
            rs = slice(r0, r0 + nr)
            vv = v_ref[0:nk, :]
            for hh in range(2):
                sl = slice(hh * 128, (hh + 1) * 128)
                s = _nt(q_ref[rs, sl], k_ref[0:nk, sl])
                if diagonal:
                    s = jnp.where(_diag_keep(nr, nk), s, NEG)
                m_prev = m_scr[hh, rs, :]
                m_new = jnp.maximum(m_prev, jnp.max(s, axis=-1, keepdims=True))
                pr = jnp.exp(s - jnp.tile(m_new, (1, nk // 128)))
                alpha = jnp.exp(m_prev - m_new)
                l_scr[hh, rs, :] = alpha * l_scr[hh, rs, :] + jnp.sum(pr, axis=-1, keepdims=True)
                acc_scr[hh, rs, :] = alpha * acc_scr[hh, rs, :] + _nn(pr.astype(BF16), vv)
                m_scr[hh, rs, :] = m_new

        @pl.when(kb < i)
        def _():
            update(0, t, t, False)

        @pl.when(kb == i)
        def _():
            update(0, t // 2, t // 2, True)
            update(t // 2, t // 2, t, True)
            o0 = acc_scr[0] / l_scr[0]
            o1 = acc_scr[1] / l_scr[1]
            o_ref[...] = jnp.where(_head_half((t, 128), 0), o0, o1)
            for hh in range(2):
                lse_ref[:, hh * 128:(hh + 1) * 128] = m_scr[hh] + jnp.log(l_scr[hh])

    grid_spec = pltpu.PrefetchScalarGridSpec(
        num_scalar_prefetch=2, grid=(4, qtab.shape[0]),
        in_specs=[pl.BlockSpec((t, 256), lambda j, s, qi, ki: (qi[s], j)),
                  pl.BlockSpec((t, 256), lambda j, s, qi, ki: (ki[s], j)),
                  pl.BlockSpec((t, 128), lambda j, s, qi, ki: (ki[s], j))],
        out_specs=[pl.BlockSpec((t, 128), lambda j, s, qi, ki: (qi[s], j)),
                   pl.BlockSpec((t, 256), lambda j, s, qi, ki: (qi[s], j))],
        scratch_shapes=[pltpu.VMEM((2, t, 128), F32), pltpu.VMEM((2, t, 128), F32), pltpu.VMEM((2, t, 128), F32)])
    return pl.pallas_call(
        body, name="mla_flash_fwd", grid_spec=grid_spec,
        out_shape=[jax.ShapeDtypeStruct((SEQ, 512), F32), jax.ShapeDtypeStruct((SEQ, 1024), F32)],
        compiler_params=_cparams(),
    )(qtab, ktab, q, k, v)


def _mla_flash_bwd(q, k, v, o, do, lse, token=None):
    t = FLASH_T
    nb = SEQ // t
    qtab, ktab = _tri_steps(nb, False)
    after, after_specs = _after(token)

    def body(qi_ref, ki_ref, q_ref, k_ref, v_ref, o_ref, do_ref, lse_ref, *rest):
        dq_ref, dk_ref, dv_ref, dk_scr, dv_scr = rest[-5:]
        step = pl.program_id(1)
        i, kb = qi_ref[step], ki_ref[step]

        @pl.when(step == 0)
        def _():
            dq_ref[...] = jnp.zeros_like(dq_ref)

        @pl.when(i == kb)
        def _():
            dk_scr[...] = jnp.zeros_like(dk_scr)
            dv_scr[...] = jnp.zeros_like(dv_scr)

        def update(r0, nr, nk, diagonal):
            rs = slice(r0, r0 + nr)
            vv = v_ref[0:nk, :]
            ov = o_ref[rs, :]
            dov = do_ref[rs, :]
            rows = pl.ds(pl.multiple_of(i * t + r0, t // 2), nr)
            for hh in range(2):
                sl = slice(hh * 128, (hh + 1) * 128)
                qh, kh = q_ref[rs, sl], k_ref[0:nk, sl]
                s = _nt(qh, kh)
                if diagonal:
                    s = jnp.where(_diag_keep(nr, nk), s, NEG)
                pr = jnp.exp(s - jnp.tile(lse_ref[rs, sl], (1, nk // 128)))
                dom = jnp.where(_head_half((nr, 128), hh), dov, 0.0)
                domb = dom.astype(BF16)
                dv_scr[0:nk, :] += _tn(pr.astype(BF16), domb)
                dpr = _nt(domb, vv)
                delta = jnp.sum(dom * ov, axis=-1, keepdims=True)
                ds = (pr * (dpr - delta)).astype(BF16)
                dq_ref[rows, sl] += _nn(ds, kh)
                dk_scr[hh, 0:nk, :] += _tn(ds, qh)

        @pl.when(i > kb)
        def _():
            update(0, t, t, False)

        @pl.when(i == kb)
        def _():
            update(0, t // 2, t // 2, True)
            update(t // 2, t // 2, t, True)

        @pl.when(i == nb - 1)
        def _():
            dk_ref[:, 0:128] = dk_scr[0]
            dk_ref[:, 128:256] = dk_scr[1]
            dv_ref[...] = dv_scr[...]

    qi_map = lambda j, s, qi, ki: (qi[s], j)
    ki_map = lambda j, s, qi, ki: (ki[s], j)
    grid_spec = pltpu.PrefetchScalarGridSpec(
        num_scalar_prefetch=2, grid=(4, qtab.shape[0]),
        in_specs=[pl.BlockSpec((t, 256), qi_map), pl.BlockSpec((t, 256), ki_map), pl.BlockSpec((t, 128), ki_map),
                  pl.BlockSpec((t, 128), qi_map), pl.BlockSpec((t, 128), qi_map), pl.BlockSpec((t, 256), qi_map)]
        + after_specs,
        out_specs=[pl.BlockSpec((SEQ, 256), lambda j, s, qi, ki: (0, j)), pl.BlockSpec((t, 256), ki_map),
                   pl.BlockSpec((t, 128), ki_map)],
        scratch_shapes=[pltpu.VMEM((2, t, 128), F32), pltpu.VMEM((t, 128), F32)])
    return pl.pallas_call(
        body, name="mla_flash_bwd", grid_spec=grid_spec,
        out_shape=[jax.ShapeDtypeStruct((SEQ, 1024), F32), jax.ShapeDtypeStruct((SEQ, 1024), F32),
                   jax.ShapeDtypeStruct((SEQ, 512), F32)],
        compiler_params=_cparams(),
    )(qtab, ktab, q, k, v, o, do, lse, *after)


def _strided(start, size, d):
    return pl.ds(start, size) if d == 1 else pl.ds(start, size, stride=d)


DIL_ST_FWD, DIL_ST_BWD = 1024, 2048


def _band_keep(g, b, t, nb):
    nbs = SEQ // DIL_DILATIONS[g] // BAND
    row = lax.broadcasted_iota(jnp.int32, (BAND, 2 * BAND), 0)
    col = lax.broadcasted_iota(jnp.int32, (BAND, 2 * BAND), 1)
    cur = (col >= BAND) & (row >= col - BAND)
    prev = (col < BAND) & (col >= row)
    if nbs >= nb:
        if b > 0:
            return cur | prev
        return cur | (prev & ((t * nb) % nbs != 0))
    return cur | prev if b % nbs else cur


def _dil_tok(g, b, t, nb):
    d = DIL_DILATIONS[g]
    nbs = SEQ // d // BAND
    gb = t * nb + b
    return _strided((gb % nbs) * BAND * d + gb // nbs, BAND, d)


def _dil_attn_fwd(p_qkv, rc, rs, g):
    d = DIL_DILATIONS[g]
    sub_len = SEQ // d
    ch = min(sub_len, 512)
    DIL_ST, DIL_NB = DIL_ST_FWD, DIL_ST_FWD // BAND

    def body(p_ref, c_ref, sn_ref, o_ref, l_ref, qkv_ref, x_scr, s_scr, p_scr, o_scr):
        t = pl.program_id(1)

        @pl.when(t == 0)
        def _():
            lanes = _rope_lanes((ch, 128), DIL_ROPE_HALF, 64, 0)
            for tq in range(3):
                qkv_ref[tq, 0, 0:BAND, :] = jnp.zeros((BAND, 128), BF16)
                mult = DIL_SCALE if tq == 0 else 1.0
                for c0 in range(0, SEQ, ch):
                    rows = pl.ds(c0, ch)
                    xv = p_ref[tq, rows, :].astype(F32)
                    x_scr[rows, :] = xv if tq == 2 else _rope_fwd(xv, c_ref[rows, :] * mult, sn_ref[rows, :] * mult,
                                                                   DIL_ROPE_HALF, lanes)
                for r in range(d):
                    for c0 in range(0, sub_len, ch):
                        at = BAND + r * sub_len + c0
                        qkv_ref[tq, 0, at:at + ch, :] = x_scr[_strided(r + c0 * d, ch, d), :].astype(BF16)

        base = t * DIL_ST
        half0 = _head_half((DIL_ST, 128), 0)
        lse_h = []
        for hh in range(2):
            half = _head_half((BAND, 128), hh)
            for b in range(DIL_NB):
                qv = qkv_ref[0, 0, pl.ds(pl.multiple_of(base + (b + 1) * BAND, BAND), BAND), :]
                k2 = qkv_ref[1, 0, pl.ds(pl.multiple_of(base + b * BAND, BAND), 2 * BAND), :]
                sb = _nt(jnp.where(half, qv, jnp.zeros_like(qv)), k2)
                s_scr[b * BAND:(b + 1) * BAND, :] = jnp.where(_band_keep(g, b, t, DIL_NB), sb, NEG)
            s = s_scr[...]
            m = jnp.max(s, axis=-1, keepdims=True)
            pr = jnp.exp(s - m)
            den = jnp.sum(pr, axis=-1, keepdims=True)
            p_scr[...] = pr.astype(BF16)
            for b in range(DIL_NB):
                v2 = qkv_ref[2, 0, pl.ds(pl.multiple_of(base + b * BAND, BAND), 2 * BAND), :]
                o_scr[hh, b * BAND:(b + 1) * BAND, :] = _nn(p_scr[b * BAND:(b + 1) * BAND, :], v2)
            o_scr[hh] = o_scr[hh] / den
            lse_h.append(m + jnp.log(den))
        out = jnp.where(half0, o_scr[0], o_scr[1])
        lse = jnp.where(half0, lse_h[0], lse_h[1])
        for b in range(DIL_NB):
            tok = _dil_tok(g, b, t, DIL_NB)
            o_ref[tok, :] = out[b * BAND:(b + 1) * BAND, :]
            l_ref[tok, :] = lse[b * BAND:(b + 1) * BAND, :]

    tab = pl.BlockSpec((SEQ, 128), lambda pr, t: (0, 0))
    out = pl.BlockSpec((SEQ, 128), lambda pr, t: (0, pr))
    return pl.pallas_call(
        body, name=f"dil_attn_fwd_g{g}", grid=(4, SEQ // DIL_ST),
        in_specs=[pl.BlockSpec((None, 3, SEQ, 128), lambda pr, t: (g * 4 + pr, 0, 0, 0)), tab, tab],
        out_specs=[out, out, pl.BlockSpec((3, 1, BAND + SEQ, 128), lambda pr, t: (0, pr, 0, 0))],
        out_shape=[jax.ShapeDtypeStruct((SEQ, 512), F32), jax.ShapeDtypeStruct((SEQ, 512), F32),
                   jax.ShapeDtypeStruct((3, 4, BAND + SEQ, 128), BF16)],
        scratch_shapes=[pltpu.VMEM((SEQ, 128), F32), pltpu.VMEM((DIL_ST, 2 * BAND), F32),
                        pltpu.VMEM((DIL_ST, 2 * BAND), BF16), pltpu.VMEM((2, DIL_ST, 128), F32)],
        compiler_params=_cparams(),
    )(p_qkv.reshape(12, 3, SEQ, 128), rc, rs)


def _dil_attn_bwd(dp_in, qkv, dyd, yd, lse_all, rc, rs, g, token=None):
    d = DIL_DILATIONS[g]
    sub_len = SEQ // d
    DIL_ST, DIL_NB = DIL_ST_BWD, DIL_ST_BWD // BAND
    nst = SEQ // DIL_ST
    after, after_specs = _after(token)
    ch = 512

    def body(dp_any, q_ref, k_ref, v_ref, do_ref, y_ref, l_ref, c_ref, sn_ref, *rest):
        dp_ref, tok_scr, dk_scr, dv_scr, s_scr, dp_scr, p_scr, ds_scr, do_scr, y_scr, l_scr, dq_scr = rest[-12:]
        del dp_any
        t = pl.program_id(1)
        base = t * DIL_ST

        @pl.when(t == 0)
        def _():
            dk_scr[...] = jnp.zeros_like(dk_scr)
            dv_scr[...] = jnp.zeros_like(dv_scr)

        for b in range(DIL_NB):
            tok = _dil_tok(g, b, t, DIL_NB)
            do_scr[b * BAND:(b + 1) * BAND, :] = do_ref[tok, :]
            y_scr[b * BAND:(b + 1) * BAND, :] = y_ref[tok, :]
            l_scr[b * BAND:(b + 1) * BAND, :] = l_ref[tok, :]
        for hh in range(2):
            half = _head_half((BAND, 128), hh)
            half_st = _head_half((DIL_ST, 128), hh)
            dom = jnp.where(half_st, do_scr[...], 0.0)
            delta = jnp.sum(dom * y_scr[...], axis=-1, keepdims=True)
            lcol = jnp.max(jnp.where(half_st, l_scr[...], NEG), axis=-1, keepdims=True)
            for b in range(DIL_NB):
                rows = slice(b * BAND, (b + 1) * BAND)
                qv = q_ref[0, 0, pl.ds(pl.multiple_of(base + (b + 1) * BAND, BAND), BAND), :]
                band = pl.ds(pl.multiple_of(base + b * BAND, BAND), 2 * BAND)
                sb = _nt(jnp.where(half, qv, jnp.zeros_like(qv)), k_ref[0, 0, band, :])
                s_scr[rows, :] = jnp.where(_band_keep(g, b, t, DIL_NB), sb, NEG)
                dp_scr[rows, :] = _nt(dom[rows, :].astype(BF16), v_ref[0, 0, band, :])
            pr = jnp.exp(s_scr[...] - lcol)
            p_scr[...] = pr.astype(BF16)
            ds_scr[...] = (pr * (dp_scr[...] - delta)).astype(BF16)
            for b in range(DIL_NB):
                rows = slice(b * BAND, (b + 1) * BAND)
                qv = q_ref[0, 0, pl.ds(pl.multiple_of(base + (b + 1) * BAND, BAND), BAND), :]
                band = pl.ds(pl.multiple_of(base + b * BAND, BAND), 2 * BAND)
                dqb = jnp.where(half, _nn(ds_scr[rows, :], k_ref[0, 0, band, :]), 0.0)
                if hh == 0:
                    dq_scr[rows, :] = dqb
                else:
                    dq_scr[rows, :] += dqb
                half2 = _head_half((2 * BAND, 128), hh)
                dk_scr[band, :] += jnp.where(half2, _tn(ds_scr[rows, :], qv), 0.0)
                dv_scr[band, :] += _tn(p_scr[rows, :], dom[rows, :].astype(BF16))
        for b in range(DIL_NB):
            tok_scr[pl.ds(0, 1), _dil_tok(g, b, t, DIL_NB), :] = dq_scr[b * BAND:(b + 1) * BAND, :][None]

        @pl.when(t == nst - 1)
        def _():
            for r in range(d):
                rows = _strided(r, sub_len, d)
                tok_scr[pl.ds(1, 1), rows, :] = dk_scr[BAND + r * sub_len:BAND + (r + 1) * sub_len, :][None]
                tok_scr[pl.ds(2, 1), rows, :] = dv_scr[BAND + r * sub_len:BAND + (r + 1) * sub_len, :][None]
            lanes = _rope_lanes((ch, 128), DIL_ROPE_HALF, 64, 0)
            for c0 in range(0, SEQ, ch):
                rows = slice(c0, c0 + ch)
                cv, sv = c_ref[rows, :], sn_ref[rows, :]
                dp_ref[rows, 0:128] = _rope_bwd(tok_scr[0, rows, :], cv * DIL_SCALE, sv * DIL_SCALE, DIL_ROPE_HALF, lanes).astype(BF16)
                dp_ref[rows, 128:256] = _rope_bwd(tok_scr[1, rows, :], cv, sv, DIL_ROPE_HALF, lanes).astype(BF16)
                dp_ref[rows, 256:384] = tok_scr[2, rows, :].astype(BF16)

    def inp(tq):
        return pl.BlockSpec((1, 1, BAND + SEQ, 128), lambda pr, t: (tq, pr, 0, 0))

    tok_spec = pl.BlockSpec((SEQ, 128), lambda pr, t: (0, pr))
    tab = pl.BlockSpec((SEQ, 128), lambda pr, t: (0, 0))
    st = (DIL_ST, 2 * BAND)
    return pl.pallas_call(
        body, name=f"dil_attn_bwd_g{g}", grid=(4, nst),
        in_specs=[pl.BlockSpec(memory_space=pl.ANY), inp(0), inp(1), inp(2), tok_spec, tok_spec, tok_spec, tab, tab]
        + after_specs,
        out_specs=pl.BlockSpec((SEQ, 384), lambda pr, t: (0, _qkv_block(0, g, pr) // 3)),
        out_shape=jax.ShapeDtypeStruct((SEQ, N_PAD), BF16),
        input_output_aliases={0: 0},
        scratch_shapes=[pltpu.VMEM((3, SEQ, 128), F32),
                        pltpu.VMEM((BAND + SEQ, 128), F32), pltpu.VMEM((BAND + SEQ, 128), F32),
                        pltpu.VMEM(st, F32), pltpu.VMEM(st, F32), pltpu.VMEM(st, BF16), pltpu.VMEM(st, BF16),
                        pltpu.VMEM((DIL_ST, 128), F32), pltpu.VMEM((DIL_ST, 128), F32), pltpu.VMEM((DIL_ST, 128), F32),
                        pltpu.VMEM((DIL_ST, 128), F32)],
        compiler_params=_cparams(),
    )(dp_in, qkv, qkv, qkv, dyd, yd, lse_all, rc, rs, *after)


TAIL_T = 256


def _tail(p, ya, o_g, l_g, x, target, wpm, wpd, wout, post_g):
    tm = TAIL_T

    def body(pgz_ref, ya_ref, o0_ref, o1_ref, o2_ref, l0_ref, l1_ref, l2_ref, x_ref, t_ref,
             wpm_ref, wpd_ref, wout_ref, pg_ref,
             dp_ref, dy_ref, dya_ref, dyd_ref, yd_ref, lse_ref, loss_ref, dgp_ref, dwpm_ref, dwpd_ref, dwout_ref):
        l0, l1, l2 = l0_ref[...], l1_ref[...], l2_ref[...]
        mx = jnp.maximum(jnp.maximum(l0, l1), l2)
        e0, e1, e2 = jnp.exp(l0 - mx), jnp.exp(l1 - mx), jnp.exp(l2 - mx)
        den = e0 + e1 + e2
        yd = (e0 * o0_ref[...] + e1 * o1_ref[...] + e2 * o2_ref[...]) / den
        yd_ref[...] = yd
        lse_ref[...] = mx + jnp.log(den)
        ya = ya_ref[...]

        gm, gd = pgz_ref[:, 0:1024], pgz_ref[:, 1024:2048]
        zm, zd = pgz_ref[:, 2048:2560], pgz_ref[:, 2560:3072]
        szm, szd = _sigmoid(zm), _sigmoid(zd)
        sm, sd = zm * szm, zd * szd
        ua = (ya * sm).astype(BF16)
        ud = (yd * sd).astype(BF16)
        pa = _nn(ua, wpm_ref[...])
        pd = _nn(ud, wpd_ref[...])
        sgm, sgd = _sigmoid(gm), _sigmoid(gd)
        mg = (sgm * pa + sgd * pd).astype(BF16)
        t = _nn(mg, wout_ref[...])
        r3 = lax.rsqrt(jnp.mean(t * t, axis=-1, keepdims=True) + EPS)
        n = t * r3
        pg = pg_ref[...]
        err = x_ref[...] + n * pg - t_ref[...]
        lpart = jnp.sum(err * err, axis=0, keepdims=True)

        dy = err * (1.0 / D_MODEL)
        dy_ref[...] = dy
        gpart = jnp.sum(dy * n, axis=0, keepdims=True)
        dn = dy * pg
        dt = (r3 * (dn - n * jnp.mean(dn * n, axis=-1, keepdims=True))).astype(BF16)
        dmg = _nt(dt, wout_ref[...])
        dpa = (dmg * sgm).astype(BF16)
        dpd = (dmg * sgd).astype(BF16)
        dp_ref[:, 0:1024] = (dmg * pa * sgm * (1.0 - sgm)).astype(BF16)
        dp_ref[:, 1024:2048] = (dmg * pd * sgd * (1.0 - sgd)).astype(BF16)
        dua = _nt(dpa, wpm_ref[...])
        dud = _nt(dpd, wpd_ref[...])
        dya_ref[...] = dua * sm
        dyd_ref[...] = dud * sd
        dp_ref[:, 2048:2560] = (dua * ya * szm * (1.0 + zm * (1.0 - szm))).astype(BF16)
        dp_ref[:, 2560:3072] = (dud * yd * szd * (1.0 + zd * (1.0 - szd))).astype(BF16)

        wpm, wpd, wout = _tn(ua, dpa), _tn(ud, dpd), _tn(mg, dt)

        @pl.when(pl.program_id(0) == 0)
        def _():
            loss_ref[...] = lpart
            dgp_ref[...] = gpart
            dwpm_ref[...] = wpm
            dwpd_ref[...] = wpd
            dwout_ref[...] = wout

        @pl.when(pl.program_id(0) > 0)
        def _():
            loss_ref[...] += lpart
            dgp_ref[...] += gpart
            dwpm_ref[...] += wpm
            dwpd_ref[...] += wpd
            dwout_ref[...] += wout

    def rows(w):
        return pl.BlockSpec((tm, w), lambda i: (i, 0))

    def full(shape):
        return pl.BlockSpec(shape, lambda i: (0, 0))

    def sds(w, dt):
        return jax.ShapeDtypeStruct((SEQ, w), dt)

    return pl.pallas_call(
        body, name="tail", grid=(SEQ // tm,),
        in_specs=[rows(3072), rows(512), rows(512), rows(512), rows(512), rows(512), rows(512), rows(512),
                  rows(1024), rows(1024), full((512, 1024)), full((512, 1024)), full((1024, 1024)), full((1, 1024))],
        out_specs=[rows(3072), rows(1024), rows(512), rows(512), rows(512), rows(512), full((1, 1024)), full((1, 1024)),
                   full((512, 1024)), full((512, 1024)), full((1024, 1024))],
        out_shape=[sds(N_PAD, BF16), sds(1024, F32), sds(512, F32), sds(512, F32), sds(512, F32), sds(512, F32),
                   jax.ShapeDtypeStruct((1, 1024), F32), jax.ShapeDtypeStruct((1, 1024), F32),
                   jax.ShapeDtypeStruct((512, 1024), F32), jax.ShapeDtypeStruct((512, 1024), F32),
                   jax.ShapeDtypeStruct((1024, 1024), F32)],
        compiler_params=_cparams(),
    )(p, ya, o_g[0], o_g[1], o_g[2], l_g[0], l_g[1], l_g[2], x, target, wpm, wpd, wout, post_g)


def _sum_parts(recv, own, me, tr, name):
    n, r, w = recv.shape
    if r % tr:
        return _sum_parts_cols(recv, own, me, name)
    own_spec = (pl.BlockSpec((tr, w), lambda i, me_ref: (i, 0)) if own.ndim == 2
                else pl.BlockSpec((None, tr, w), lambda i, me_ref: (me_ref[0], i, 0)))

    def body(me_ref, p_ref, own_ref, o_ref):
        mine = own_ref[...].astype(F32)
        acc = jnp.zeros((tr, w), F32)
        for s in range(n):
            acc = acc + jnp.where(me_ref[0] == s, mine, p_ref[s].astype(F32))
        o_ref[...] = acc

    return pl.pallas_call(
        body, name=name,
        grid_spec=pltpu.PrefetchScalarGridSpec(
            num_scalar_prefetch=1, grid=(r // tr,),
            in_specs=[pl.BlockSpec((n, tr, w), lambda i, me_ref: (0, i, 0)), own_spec],
            out_specs=pl.BlockSpec((tr, w), lambda i, me_ref: (i, 0))),
        out_shape=jax.ShapeDtypeStruct((r, w), F32),
    )(me.reshape(1), recv, own)


def _sum_parts_cols(recv, own, me, name):
    n, r, w = recv.shape
    tc = 128

    def body(me_ref, p_ref, own_ref, o_ref):
        mine = own_ref[...].astype(F32)
        acc = jnp.zeros((r, tc), F32)
        for s in range(n):
            acc = acc + jnp.where(me_ref[0] == s, mine, p_ref[s].astype(F32))
        o_ref[...] = acc

    return pl.pallas_call(
        body, name=name,
        grid_spec=pltpu.PrefetchScalarGridSpec(
            num_scalar_prefetch=1, grid=(w // tc,),
            in_specs=[pl.BlockSpec((n, r, tc), lambda i, me_ref: (0, 0, i)),
                      pl.BlockSpec((None, r, tc), lambda i, me_ref: (me_ref[0], 0, i))],
            out_specs=pl.BlockSpec((r, tc), lambda i, me_ref: (0, i))),
        out_shape=jax.ShapeDtypeStruct((r, w), F32),
    )(me.reshape(1), recv, own)


def _adamw(w, g, m, v, name):
    lead = w.shape[:-2]
    r, c = w.shape[-2:]
    tr = max([t for t in range(8, 257, 8) if r % t == 0], default=r)
    c1 = 1.0 - ADAM_B1 ** ADAM_STEP
    c2 = 1.0 - ADAM_B2 ** ADAM_STEP

    def body(w_ref, g_ref, m_ref, v_ref, d_ref, nm_ref, nv_ref):
        gv = g_ref[...]
        nm = ADAM_B1 * m_ref[...] + (1.0 - ADAM_B1) * gv
        nv = ADAM_B2 * v_ref[...] + (1.0 - ADAM_B2) * (gv * gv)
        nm_ref[...] = nm
        nv_ref[...] = nv
        d_ref[...] = -ADAM_LR * ((nm / c1) / (jnp.sqrt(nv / c2) + ADAM_EPS) + ADAM_WD * w_ref[...])

    zeros = (0,) * len(lead)
    spec = pl.BlockSpec((1,) * len(lead) + (tr, c), lambda i: zeros + (i, 0))
    sd = jax.ShapeDtypeStruct(w.shape, F32)
    return pl.pallas_call(
        body, name=name, grid=(r // tr,),
        in_specs=[spec] * 4, out_specs=[spec] * 3, out_shape=[sd] * 3,
    )(w, g, m, v)


def _adamw_recv(w, m, v, recv, own, me, name):
    n, r, c = recv.shape
    tr = 128
    c1 = 1.0 - ADAM_B1 ** ADAM_STEP
    c2 = 1.0 - ADAM_B2 ** ADAM_STEP

    def body(me_ref, w_ref, m_ref, v_ref, p_ref, own_ref, d_ref, nm_ref, nv_ref, g_ref):
        mine = own_ref[...].astype(F32)
        gv = jnp.zeros((tr, c), F32)
        for s in range(n):
            gv = gv + jnp.where(me_ref[0] == s, mine, p_ref[s].astype(F32))
        g_ref[0] = gv
        nm = ADAM_B1 * m_ref[0] + (1.0 - ADAM_B1) * gv
        nv = ADAM_B2 * v_ref[0] + (1.0 - ADAM_B2) * (gv * gv)
        nm_ref[0] = nm
        nv_ref[0] = nv
        d_ref[0] = -ADAM_LR * ((nm / c1) / (jnp.sqrt(nv / c2) + ADAM_EPS) + ADAM_WD * w_ref[0])

    full = pl.BlockSpec((1, tr, c), lambda i, me_ref: (0, i, 0))
    sd = jax.ShapeDtypeStruct((1, r, c), F32)
    return pl.pallas_call(
        body, name=name,
        grid_spec=pltpu.PrefetchScalarGridSpec(
            num_scalar_prefetch=1, grid=(r // tr,),
            in_specs=[full, full, full, pl.BlockSpec((n, tr, c), lambda i, me_ref: (0, i, 0)),
                      pl.BlockSpec((None, tr, c), lambda i, me_ref: (me_ref[0], i, 0))],
            out_specs=[full] * 4),
        out_shape=[sd] * 4,
    )(me.reshape(1), w, m, v, recv, own)


def _adamw_in(w_t, m_t, v_t, own_half, swapped, core):
    r, c = SHARD_SHAPES[0]
    tr = max(t for t in range(8, 257, 8) if r % t == 0)
    c1 = 1.0 - ADAM_B1 ** ADAM_STEP
    c2 = 1.0 - ADAM_B2 ** ADAM_STEP

    def body(core_ref, w_ref, m_ref, v_ref, own_ref, sw_ref, d_ref, nm_ref, nv_ref, g_ref):
        own = own_ref[...]
        col_half = lax.broadcasted_iota(jnp.int32, (tr, c), 1) // (c // 2)
        gv = jnp.where(col_half == core_ref[0], jnp.concatenate([own, own], axis=1), sw_ref[...])
        g_ref[0] = gv
        nm = ADAM_B1 * m_ref[0] + (1.0 - ADAM_B1) * gv
        nv = ADAM_B2 * v_ref[0] + (1.0 - ADAM_B2) * (gv * gv)
        nm_ref[0] = nm
        nv_ref[0] = nv
        d_ref[0] = -ADAM_LR * ((nm / c1) / (jnp.sqrt(nv / c2) + ADAM_EPS) + ADAM_WD * w_ref[0])

    full = pl.BlockSpec((1, tr, c), lambda i, core_ref: (0, i, 0))
    sd = jax.ShapeDtypeStruct((1, r, c), F32)
    return pl.pallas_call(
        body, name="adamw_in",
        grid_spec=pltpu.PrefetchScalarGridSpec(
            num_scalar_prefetch=1, grid=(r // tr,),
            in_specs=[full, full, full, pl.BlockSpec((tr, c // 2), lambda i, core_ref: (i, 0)),
                      pl.BlockSpec((tr, c), lambda i, core_ref: (i, 0))],
            out_specs=[full] * 4),
        out_shape=[sd] * 4,
    )(core.reshape(1), w_t, m_t, v_t, own_half, swapped)


ANY = pl.BlockSpec(memory_space=pl.ANY)


def _my_place():
    return lax.axis_index("x"), lax.axis_index("y"), lax.axis_index("c")


HBM = pl.BlockSpec(memory_space=pltpu.HBM)
SEM = pl.BlockSpec(memory_space=pltpu.SEMAPHORE)
DATAFLOW = pltpu.SideEffectType.DATAFLOW_SIDE_EFFECTING


def _near_chips(x, y):
    return [(1 - x, y), (x, 1 - y)]


def _half(mi, hc):
    r, c = SHARD_SHAPES[mi]
    if mi == 0:
        return pl.ds(0, r), pl.ds(pl.multiple_of(hc * (c // 2), 128), c // 2)
    return pl.ds(pl.multiple_of(hc * (r // 2), 16), r // 2), pl.ds(0, c)


def _gather_copies(land_refs, send_sems, recv_sems):
    x, y, c = _my_place()
    out, back = [], []
    for mi in range(N_MATS):
        rows, cols = _half(mi, c)
        mine = land_refs[mi].at[2 * x + y, rows, cols]
        for j, (cx, cy) in enumerate(_near_chips(x, y)):
            sems = dict(send_sem=send_sems.at[mi * 2 + j], recv_sem=recv_sems.at[mi * 2 + j],
                        device_id=(cx, cy, c), device_id_type=MESH)
            out.append(pltpu.make_async_remote_copy(src_ref=mine, dst_ref=mine, **sems))
            got = land_refs[mi].at[2 * cx + cy, rows, cols]
            back.append(pltpu.make_async_remote_copy(src_ref=got, dst_ref=got, **sems))
    return out, back


def _gather_start(landing):
    n = N_MATS

    def body(*refs):
        out, _ = _gather_copies(refs[:n], refs[n], refs[n + 1])
        for cp in out:
            cp.start()
        refs[-1][...] = jnp.zeros_like(refs[-1])

    hbm = [pltpu.HBM(a.shape, a.dtype) for a in landing]
    outs = pl.pallas_call(
        body, name="gather_start",
        out_shape=(pltpu.SemaphoreType.DMA((2 * n,)), pltpu.SemaphoreType.DMA((2 * n,)), *hbm,
                   jax.ShapeDtypeStruct((8, 128), F32)),
        in_specs=[HBM] * n, out_specs=(SEM, SEM, *[HBM] * n, pl.BlockSpec(memory_space=pltpu.VMEM)),
        input_output_aliases={i: 2 + i for i in range(n)},
        compiler_params=pltpu.CompilerParams(has_side_effects=DATAFLOW),
    )(*[pltpu.with_memory_space_constraint(a, pltpu.HBM) for a in landing])
    return outs[:-1], outs[-1]


def _gather_wait(handle, after):
    n = N_MATS

    def body(*refs):
        out, back = _gather_copies(refs[:n], refs[n], refs[n + 1])
        for cp, arrival in zip(out, back):
            cp.wait_send()
            arrival.wait_recv()

    bufs = handle[2:]
    after, after_specs = _after(after)
    res = pl.pallas_call(
        body, name="gather_wait", out_shape=tuple(pltpu.HBM(b.shape, b.dtype) for b in bufs),
        in_specs=[HBM] * n + [SEM, SEM] + after_specs, out_specs=tuple([HBM] * n),
        input_output_aliases={i: i for i in range(n)},
        compiler_params=pltpu.CompilerParams(has_side_effects=DATAFLOW),
    )(*bufs, handle[0], handle[1], *after)
    return list(res)


def _relay_share(gathered):
    n = N_MATS

    def body(*refs):
        out_refs = refs[n:2 * n]
        send_sems, recv_sems = refs[2 * n:]
        x, y, c = _my_place()
        sibling = (x, y, 1 - c)
        relayed = 2 * (x ^ (1 - c)) + (y ^ c)
        relay_to = (x ^ c, y ^ (1 - c), c)
        far = 2 * (1 - x) + (1 - y)
        near = [2 * (1 - x) + y, 2 * x + (1 - y)]

        def copy(k, mi, shard, hc, to):
            blk = out_refs[mi].at[(shard,) + _half(mi, hc)]
            return pltpu.make_async_remote_copy(src_ref=blk, dst_ref=blk, send_sem=send_sems.at[mi * 4 + k],
                                                recv_sem=recv_sems.at[mi * 4 + k], device_id=to, device_id_type=MESH)

        sends = []
        for mi in range(n):
            sends.append(copy(0, mi, relayed, c, relay_to))
            sends += [copy(1 + j, mi, near[j], c, sibling) for j in range(2)]
        for cp in sends:
            cp.start()
        for mi in range(n):
            copy(0, mi, far, c, relay_to).wait_recv()
            cp = copy(3, mi, far, c, sibling)
            cp.start()
            sends.append(cp)
        for mi in range(n):
            for j in range(2):
                copy(1 + j, mi, near[j], 1 - c, sibling).wait_recv()
            copy(3, mi, far, 1 - c, sibling).wait_recv()
        for cp in sends:
            cp.wait_send()

    return pl.pallas_call(
        body, name="relay_share",
        in_specs=[ANY] * n, out_specs=[ANY] * n,
        out_shape=[jax.ShapeDtypeStruct(g.shape, g.dtype) for g in gathered],
        input_output_aliases={i: i for i in range(n)},
        scratch_shapes=[pltpu.SemaphoreType.DMA((4 * n,)), pltpu.SemaphoreType.DMA((4 * n,))],
    )(*gathered)


def _peers(x, y, c):
    out = []
    for k in range(1, 8):
        px, py, pc = x ^ (k >> 2), y ^ ((k >> 1) & 1), c ^ (k & 1)
        out.append((k - 1, (px, py, pc), 4 * px + 2 * py + pc))
    return out


def _exchange_start(parts, name):
    n = len(parts)

    def body(*refs):
        p_refs, land_refs = refs[:n], refs[n:2 * n]
        send_sems, recv_sems, token = refs[2 * n], refs[2 * n + 1], refs[-1]
        x, y, c = _my_place()
        me = 4 * x + 2 * y + c
        for k, dev, peer in _peers(x, y, c):
            for mi in range(n):
                pltpu.make_async_remote_copy(
                    src_ref=p_refs[mi].at[peer], dst_ref=land_refs[mi].at[me], send_sem=send_sems.at[k * n + mi],
                    recv_sem=recv_sems.at[k * n + mi], device_id=dev, device_id_type=MESH).start()
        token[...] = jnp.zeros_like(token)

    hbm = [pltpu.HBM(p.shape, p.dtype) for p in parts]
    outs = pl.pallas_call(
        body, name=name + "_start",
        out_shape=(pltpu.SemaphoreType.DMA((7 * n,)), pltpu.SemaphoreType.DMA((7 * n,)), *hbm, *hbm,
                   jax.ShapeDtypeStruct((8, 128), F32)),
        in_specs=[HBM] * (2 * n), out_specs=(SEM, SEM, *[HBM] * (2 * n), pl.BlockSpec(memory_space=pltpu.VMEM)),
        input_output_aliases={i: 2 + i for i in range(2 * n)},
        compiler_params=pltpu.CompilerParams(has_side_effects=DATAFLOW),
    )(*[pltpu.with_memory_space_constraint(p, pltpu.HBM) for p in parts],
      *[pltpu.with_memory_space_constraint(lax.empty(p.shape, p.dtype), pltpu.HBM) for p in parts])
    return (name, outs[:-1]), outs[-1]


def _exchange_wait(handle, after):
    name, outs = handle
    n = (len(outs) - 2) // 2

    def body(*refs):
        p_refs, land_refs = refs[:n], refs[n:2 * n]
        send_sems, recv_sems = refs[2 * n], refs[2 * n + 1]
        x, y, c = _my_place()
        me = 4 * x + 2 * y + c
        for k, dev, peer in _peers(x, y, c):
            for mi in range(n):
                pltpu.make_async_remote_copy(
                    src_ref=p_refs[mi].at[peer], dst_ref=land_refs[mi].at[me], send_sem=send_sems.at[k * n + mi],
                    recv_sem=recv_sems.at[k * n + mi], device_id=dev, device_id_type=MESH).wait_send()
                slot = land_refs[mi].at[peer]
                pltpu.make_async_remote_copy(
                    src_ref=slot, dst_ref=slot, send_sem=send_sems.at[k * n + mi],
                    recv_sem=recv_sems.at[k * n + mi], device_id=dev, device_id_type=MESH).wait_recv()

    bufs = outs[2:]
    res = pl.pallas_call(
        body, name=name + "_wait", out_shape=tuple(pltpu.HBM(b.shape, b.dtype) for b in bufs),
        in_specs=[HBM] * (2 * n) + [SEM, SEM, ANY], out_specs=tuple([HBM] * (2 * n)),
        input_output_aliases={i: i for i in range(2 * n)},
        compiler_params=pltpu.CompilerParams(has_side_effects=DATAFLOW),
    )(*bufs, outs[0], outs[1], after)
    return list(res[n:])


def _swap_halves(half_in, gvec):
    def body(g_ref, gv_ref, out_ref, rg_ref, send_sems, recv_sems):
        x, y, c = _my_place()
        me = 4 * x + 2 * y + c
        sibling = (x, y, 1 - c)

        def half(hc):
            return out_ref.at[:, pl.ds(pl.multiple_of(hc * 512, 128), 512)]

        sends = [pltpu.make_async_remote_copy(src_ref=g_ref, dst_ref=half(c), send_sem=send_sems.at[7],
                                              recv_sem=recv_sems.at[7], device_id=sibling, device_id_type=MESH)]
        for k, dev, peer in _peers(x, y, c):
            sends.append(pltpu.make_async_remote_copy(src_ref=gv_ref, dst_ref=rg_ref.at[me], send_sem=send_sems.at[k],
                                                      recv_sem=recv_sems.at[k], device_id=dev, device_id_type=MESH))
        for cp in sends:
            cp.start()
        got = half(1 - c)
        pltpu.make_async_remote_copy(src_ref=got, dst_ref=got, send_sem=send_sems.at[7], recv_sem=recv_sems.at[7],
                                     device_id=sibling, device_id_type=MESH).wait_recv()
        for k, dev, peer in _peers(x, y, c):
            got = rg_ref.at[peer]
            pltpu.make_async_remote_copy(src_ref=got, dst_ref=got, send_sem=send_sems.at[k], recv_sem=recv_sems.at[k],
                                         device_id=dev, device_id_type=MESH).wait_recv()
        for cp in sends:
            cp.wait_send()

    return pl.pallas_call(
        body, name="swap_halves",
        in_specs=[ANY, ANY], out_specs=[ANY, ANY],
        out_shape=[jax.ShapeDtypeStruct(SHARD_SHAPES[0], F32), jax.ShapeDtypeStruct((8, 8, N_GVEC), F32)],
        scratch_shapes=[pltpu.SemaphoreType.DMA((8,)), pltpu.SemaphoreType.DMA((8,))],
    )(half_in, gvec)


def _set_slot(arr, block, idx):
    return lax.dynamic_update_slice(arr, block[None], (idx,) + (0,) * block.ndim)


PAD_RUNS = ((6304, 8352, 0), (5280, 6304, COL_Z), (672, 5280, COL_QKV), (0, 640, COL_LAT), (640, 672, COL_LAT + 704))
N_QKV = COL_LAT - COL_QKV


def _qkv_rows_regroup(a, to_padded):
    if to_padded:
        a4 = a.reshape(3, 12, 128, a.shape[1])
        return jnp.stack([a4[0], a4[1], a4[2]], axis=1).reshape(a.shape)
    a4 = a.reshape(12, 3, 128, a.shape[1])
    return jnp.concatenate([a4[:, tq].reshape(N_QKV // 3, a.shape[1]) for tq in range(3)], axis=0)
W_IN_SHARD = 2088


def _full_weights(gathered):
    def cols(a):
        return jnp.concatenate([a[s] for s in range(4)], axis=1)

    w_uq, w_ukv, w_pm, w_pd = [cols(a) for a in gathered[1:5]]
    w_out = gathered[5].reshape(D_MODEL, D_MODEL)
    w_in_t = gathered[0].reshape(4 * W_IN_SHARD, D_MODEL)
    pieces, at = [], 0
    for lo, hi, pad_lo in sorted(PAD_RUNS, key=lambda t: t[2]):
        if pad_lo > at:
            pieces.append(jnp.zeros((pad_lo - at, D_MODEL), w_in_t.dtype))
        pieces.append(_qkv_rows_regroup(w_in_t[lo:hi], True) if pad_lo == COL_QKV else w_in_t[lo:hi])
        at = pad_lo + hi - lo
    pieces.append(jnp.zeros((N_PAD - at, D_MODEL), w_in_t.dtype))
    w_pad_t = jnp.concatenate(pieces, axis=0)
    z32 = jnp.zeros((Q_RANK, 32), w_uq.dtype)
    wuq_pad = jnp.concatenate([t for h in range(MLA_HEADS) for t in (w_uq[:, h * 96:(h + 1) * 96], z32)], axis=1)
    z64 = jnp.zeros((KV_RANK, 64), w_ukv.dtype)
    wk_pad = jnp.concatenate([t for h in range(MLA_HEADS) for t in (w_ukv[:, h * 128:h * 128 + 64], z64)], axis=1)
    wv = jnp.concatenate([w_ukv[:, h * 128 + 64:(h + 1) * 128] for h in range(MLA_HEADS)], axis=1)
    return w_pad_t, wuq_pad, wk_pad, wv, w_pm, w_pd, w_out


W_IN_LAT = 672


def _grad_parts_in_early(dwt_early):
    dwt_early = jnp.concatenate([dwt_early[:COL_QKV], _qkv_rows_regroup(dwt_early[COL_QKV:COL_LAT], False)], axis=0)

    def in_block(s, h):
        cols = slice(h * 512, (h + 1) * 512)
        out = []
        for lo, hi, pad_lo in sorted(PAD_RUNS):
            a_, b_ = max(lo, s * W_IN_SHARD), min(hi, (s + 1) * W_IN_SHARD)
            if a_ < b_:
                out.append(jnp.zeros((b_ - a_, 512), dwt_early.dtype) if pad_lo >= COL_LAT
                           else dwt_early[pad_lo + a_ - lo:pad_lo + b_ - lo, cols])
        return jnp.concatenate(out, axis=0)

    return jnp.stack([in_block(s, h) for s in range(4) for h in range(2)])


def _grad_parts_in_late(dwt_late):
    rows = jnp.concatenate([dwt_late[0:640], dwt_late[704:736]], axis=0)
    zero = jnp.zeros((W_IN_LAT, 512), dwt_late.dtype)
    return jnp.stack([rows[:, 0:512], rows[:, 512:1024]] + [zero] * 6)


def _shard_blocks(m, axis=1):
    n = m.shape[axis] // 4
    cut = (lambda s: m[:, s * n:(s + 1) * n]) if axis == 1 else (lambda s: m[s * n:(s + 1) * n])
    return jnp.stack([cut(s) for s in range(4) for _ in range(2)])


def _grad_parts_mla(dwuq_pad, dwk_pad, dwv):
    d_uq = jnp.concatenate([dwuq_pad[:, h * 128:h * 128 + 96] for h in range(MLA_HEADS)], axis=1)
    d_ukv = jnp.concatenate([t for h in range(MLA_HEADS) for t in (dwk_pad[:, h * 128:h * 128 + 64], dwv[:, h * 64:(h + 1) * 64])],
                            axis=1)
    return [_shard_blocks(d_uq.astype(BF16)), _shard_blocks(d_ukv.astype(BF16))]


def _rope_tables(positions, token=None):
    pos = positions.reshape(SEQ).astype(F32)
    if token is not None:
        pos = pos + token[0, 0]
    lane = jnp.arange(128)

    def table(rot, first, period):
        inv = ROPE_THETA ** (-jnp.arange(0, rot, 2, dtype=F32) / rot)
        half = rot // 2
        off = lane % period - first
        in1, in2 = (off >= 0) & (off < half), (off >= half) & (off < rot)
        inv_lane = jnp.where(in1 | in2, inv[jnp.clip(off % half, 0, half - 1)], 0.0)
        sign = jnp.where(in1, -1.0, 1.0).astype(F32)
        ang = pos[:, None] * inv_lane[None, :]
        return jnp.cos(ang), jnp.sin(ang) * sign[None, :]

    return table(32, 64, 128), table(16, 0, 64)


class _Links:
    def __init__(self, mats, chip, me):
        landing = [_set_slot(lax.empty((4,) + m.shape, m.dtype), m, chip) for m in mats]
        self.gather, self.token = _gather_start(landing)
        self.me, self.sent, self.handles, self.sums, self.raw = me, {}, {}, {}, {}

    def weights(self, after):
        return _relay_share(_gather_wait(self.gather, after))

    def send(self, blocks, name):
        self.sent[name] = blocks
        self.handles[name], token = _exchange_start(blocks, name)
        return token

    def collect(self, name, after, parts):
        recv = _exchange_wait(self.handles[name], after)
        for r, own, part in zip(recv, self.sent[name], parts):
            if part.startswith("in_"):
                self.sums[part] = _sum_parts(r, own, self.me, 64, "sum_grad_" + part)
            else:
                self.raw[part] = (r, own)
        return tuple(self.sums[part] for part in parts if part in self.sums)


def _device_grads(x, positions, target, gains, links):
    pre_g, q_g, kv_g, post_g = gains
    (mc, ms), (dc, ds) = _rope_tables(positions, links.token)
    h = _prenorm_fwd(x, pre_g, links.token)
    w_pad_t, wuq_pad, wk_pad, wv, w_pm, w_pd, w_out = _full_weights(links.weights((h, mc, ms, dc, ds)))

    p_gz = _matmul(h, w_pad_t, "nt", F32, 1024, 1536, 1024, "in_proj_gates", b_cols=(0, COL_QKV // 1536))
    p_qkv = _matmul(h, w_pad_t, "nt", BF16, 1024, 1536, 1024, "in_proj_dilated", b_cols=(COL_QKV // 1536, N_QKV // 1536),
                    lane_blocks=True)
    p_lat = _matmul(h, w_pad_t, "nt", F32, 1024, N_LAT, 1024, "in_proj_latent", b_cols=(COL_LAT // N_LAT, 1))
    q, k, v = _mla_prep_fwd(p_lat, q_g, kv_g, wuq_pad, wk_pad, wv, mc, ms)
    ya, lse_m = _mla_flash_fwd(q, k, v)
    o_g, l_g, qkv = zip(*[_dil_attn_fwd(p_qkv, dc, ds, g) for g in range(3)])
    (dp, dy, dya, dyd, yd, lse_d, loss_cols, dg_post, dwpm, dwpd, dwout) = _tail(
        p_gz, ya, o_g, l_g, x, target, w_pm, w_pd, w_out, post_g)

    for g in range(3):
        dp = _dil_attn_bwd(dp, qkv[g], dyd, yd, lse_d, dc, ds, g)
    dw_early = _matmul(dp, h, "tn", BF16, 1536, 1024, 2048, "dw_in_early", a_cols=(0, COL_LAT // 1536))
    token = links.send([_grad_parts_in_early(dw_early), _shard_blocks(dwpm.astype(BF16)), _shard_blocks(dwpd.astype(BF16)),
                        _shard_blocks(dwout.astype(BF16), axis=0)], "exchange_early")

    dq, dk, dv = _mla_flash_bwd(q, k, v, ya, dya, lse_m, token)
    dp, dwuq_pad, dwk_pad, dwv, dg_q, dg_kv = _mla_prep_bwd(dp, p_lat, dq, dk, dv, q_g, kv_g, wuq_pad, wk_pad, wv, mc, ms)
    dw_late = _matmul(dp, h, "tn", BF16, N_LAT, 1024, 2048, "dw_in_late", a_cols=(COL_LAT // N_LAT, 1))
    token = links.send([_grad_parts_in_late(dw_late)] + _grad_parts_mla(dwuq_pad, dwk_pad, dwv), "exchange_late")
    early = links.collect("exchange_early", dw_late, ("in_early", "pm", "pd", "out"))

    grad_x, dg_pre = _dh_prenorm_bwd(dp, w_pad_t, x, dy, pre_g, (token,) + tuple(early))
    links.collect("exchange_late", grad_x, ("in_late", "uq", "ukv"))

    loss_part = jnp.pad((jnp.sum(loss_cols) * (0.5 / D_MODEL)).reshape(1, 1), ((0, 0), (0, N_GVEC - N_GAINS - 1)))
    gvec = jnp.concatenate([dg_pre, dg_q, dg_kv, dg_post, loss_part], axis=1)
    return grad_x, gvec


def kernel(x, positions, pre_norm_g, w_in, q_norm_g, w_uq, kv_norm_g, w_ukv, w_proj_mla, w_proj_dil, w_out, post_norm_g, loss_target, m_pre_norm_g, m_w_in, m_q_norm_g, m_w_uq, m_kv_norm_g, m_w_ukv, m_w_proj_mla, m_w_proj_dil, m_w_out, m_post_norm_g, v_pre_norm_g, v_w_in, v_q_norm_g, v_w_uq, v_kv_norm_g, v_w_ukv, v_w_proj_mla, v_w_proj_dil, v_w_out, v_post_norm_g):
    xi, yi, ci = _my_place()
    chip, me = 2 * xi + yi, 4 * xi + 2 * yi + ci
    mats = [jnp.swapaxes(w_in, 1, 2)] + [w_uq, w_ukv, w_proj_mla, w_proj_dil, w_out]
    mats = [w.reshape(w.shape[1:]).astype(BF16) for w in mats]
    links = _Links(mats, chip, me)
    gains = (pre_norm_g, q_norm_g, kv_norm_g, post_norm_g)
    grad_x, gvec = _device_grads(x[0], positions, loss_target[0], gains, links)

    sums = links.sums
    in_e = sums["in_early"]
    half_in = jnp.concatenate([in_e[:W_IN_LAT] + jnp.where(chip == 0, sums["in_late"], 0.0), in_e[W_IN_LAT:]], axis=0)
    gvec8 = jnp.pad(gvec, ((0, 7), (0, 0)))
    swapped_in, recv_gains = _swap_halves(half_in, gvec8)
    g_gains = _sum_parts(recv_gains, gvec8, me, 8, "sum_gain_parts")[0:1]
    loss = g_gains[0, N_GAINS]
    sw = lambda a: jnp.swapaxes(a, 1, 2)
    d_in, m_in, v_in, g_in = [sw(o) for o in _adamw_in(sw(w_in), sw(m_w_in), sw(v_w_in), half_in, swapped_in, ci)]
    off = [0, 1024, 1408, 1664, 2688]
    g_gain = [g_gains[:, off[i]:off[i + 1]] for i in range(4)]
    ws = [pre_norm_g, w_in, q_norm_g, w_uq, kv_norm_g, w_ukv, w_proj_mla, w_proj_dil, w_out, post_norm_g]
    ms = [m_pre_norm_g, m_w_in, m_q_norm_g, m_w_uq, m_kv_norm_g, m_w_ukv, m_w_proj_mla, m_w_proj_dil, m_w_out, m_post_norm_g]
    vs = [v_pre_norm_g, v_w_in, v_q_norm_g, v_w_uq, v_kv_norm_g, v_w_ukv, v_w_proj_mla, v_w_proj_dil, v_w_out, v_post_norm_g]
    part_of = [None, "in", None, "uq", None, "ukv", "pm", "pd", "out", None]
    gain_of = iter(g_gain)
    grads, deltas, new_m, new_v = [], [], [], []
    for i, (w, m, v, part) in enumerate(zip(ws, ms, vs, part_of)):
        if part == "in":
            d_, m_, v_, g = d_in, m_in, v_in, g_in
        elif part is not None:
            d_, m_, v_, g = _adamw_recv(w, m, v, *links.raw[part], me, f"adamw_{i}")
        else:
            g = next(gain_of)
            d_, m_, v_ = _adamw(w, g, m, v, f"adamw_{i}")
        grads.append(g)
        deltas.append(d_)
        new_m.append(m_)
        new_v.append(v_)
    return (loss, grad_x.reshape(x.shape), *grads, *deltas, *new_m, *new_v)
```

```python
import jax
import jax.numpy as jnp
from jax import lax
from jax.experimental import pallas as pl
from jax.experimental.pallas import tpu as pltpu

F32 = jnp.float32
BF16 = jnp.bfloat16

SEQ = 4096
D_MODEL = 1024
EPS = 1e-6
ROPE_THETA = 500000.0
MLA_HEADS = 8
Q_RANK = 384
KV_RANK = 256
MLA_SCALE = 96.0 ** -0.5
MLA_ROPE_HALF = 16
DIL_DILATIONS = (1, 4, 16)
DIL_ROPE_HALF = 8
DIL_SCALE = 0.125
BAND = 128

N_LAT = 768
COL_Z, COL_QKV, COL_LAT = 2048, 3072, 7680
N_PAD = 8448


def _qkv_block(tq, g, pr):
    return COL_QKV // 128 + (g * 4 + pr) * 3 + tq

IN_SPLITS = (384, 256, 32, 4608, 512, 512, 1024, 1024)

SHARD_SHAPES = ((2088, 1024), (384, 192), (256, 256), (512, 256), (512, 256), (256, 1024))
N_MATS = len(SHARD_SHAPES)
N_GAINS = 2688
N_GVEC = N_GAINS + 128

ADAM_LR, ADAM_B1, ADAM_B2, ADAM_EPS, ADAM_WD, ADAM_STEP = 0.001, 0.9, 0.999, 1e-08, 0.01, 10

VMEM_LIMIT = 56 * 1024 * 1024
NEG = -1e30
MESH = pl.DeviceIdType.MESH


def _cparams(**kw):
    return pltpu.CompilerParams(vmem_limit_bytes=VMEM_LIMIT, **kw)


def _dot(a, b, dims):
    return lax.dot_general(a, b, (dims, ((), ())), preferred_element_type=F32)


def _nn(a, b):
    return _dot(a, b, ((1,), (0,)))


def _nt(a, b):
    return _dot(a, b, ((1,), (1,)))


def _tn(a, b):
    return _dot(a, b, ((0,), (0,)))


def _rope_lanes(shape, half, period, first):
    lane = lax.broadcasted_iota(jnp.int32, shape, len(shape) - 1) % period
    return (lane >= first) & (lane < first + half), (lane >= first + half) & (lane < first + 2 * half)


def _rope_fwd(x, c, s, half, lanes):
    x1, _ = lanes
    return x * c + jnp.where(x1, pltpu.roll(x, 128 - half, 1), pltpu.roll(x, half, 1)) * s


def _rope_bwd(g, c, s, half, lanes):
    x1, x2 = lanes
    gs = g * s
    return g * c + jnp.where(x2, pltpu.roll(gs, half, 1), jnp.where(x1, pltpu.roll(gs, 128 - half, 1), 0.0))


def _sigmoid(x):
    return 1.0 / (1.0 + jnp.exp(-x))


def _after(token):
    tokens = [t for t in (token if isinstance(token, (tuple, list)) else [token]) if t is not None]
    return tokens, [pl.BlockSpec(memory_space=pl.ANY)] * len(tokens)


def _matmul(a, b, mode, out_dtype, tm, tn, tk, name, token=None, b_cols=None, a_cols=None, lane_blocks=False):
    after, after_specs = _after(token)
    if mode == "nn":
        (m, k), n = a.shape, b.shape[1]
        first = 0
        if b_cols is not None:
            first, n = b_cols[0], b_cols[1] * tn
        a_spec = pl.BlockSpec((tm, tk), lambda j, i, kk: (i, kk))
        b_spec = pl.BlockSpec((tk, tn), lambda j, i, kk: (kk, j + first))
        dot = _nn
    elif mode == "nt":
        (m, k), n = a.shape, b.shape[0]
        first = 0
        if b_cols is not None:
            first, n = b_cols[0], b_cols[1] * tn
        a_spec = pl.BlockSpec((tm, tk), lambda j, i, kk: (i, kk))
        b_spec = pl.BlockSpec((tn, tk), lambda j, i, kk: (j + first, kk))
        dot = _nt
    else:
        (k, m), n = a.shape, b.shape[1]
        first = 0
        if a_cols is not None:
            first, m = a_cols[0], a_cols[1] * tm
        a_spec = pl.BlockSpec((tk, tm), lambda j, i, kk: (kk, i + first))
        b_spec = pl.BlockSpec((tk, tn), lambda j, i, kk: (kk, j))
        dot = _tn
    assert m % tm == 0 and n % tn == 0 and k % tk == 0, (name, m, n, k, tm, tn, tk)
    nk = k // tk

    def write(o_ref, val):
        if lane_blocks:
            for blk in range(tn // 128):
                o_ref[blk] = val[:, blk * 128:(blk + 1) * 128].astype(o_ref.dtype)
        else:
            o_ref[...] = val.astype(o_ref.dtype)

    def body_single(a_ref, b_ref, *rest):
        write(rest[-1], dot(a_ref[...], b_ref[...]))

    def body_accumulate(a_ref, b_ref, *rest):
        o_ref, acc_ref = rest[-2:]
        kk = pl.program_id(2)
        part = dot(a_ref[...], b_ref[...])

        @pl.when(kk == 0)
        def _():
            acc_ref[...] = part

        @pl.when(kk > 0)
        def _():
            acc_ref[...] += part

        @pl.when(kk == nk - 1)
        def _():
            write(o_ref, acc_ref[...])

    if lane_blocks:
        out_spec = pl.BlockSpec((tn // 128, tm, 128), lambda j, i, kk: (j, i, 0))
        out_shape = jax.ShapeDtypeStruct((n // 128, m, 128), out_dtype)
    else:
        out_spec = pl.BlockSpec((tm, tn), lambda j, i, kk: (i, j))
        out_shape = jax.ShapeDtypeStruct((m, n), out_dtype)
    return pl.pallas_call(
        body_single if nk == 1 else body_accumulate, name=name, grid=(n // tn, m // tm, nk),
        in_specs=[a_spec, b_spec] + after_specs,
        out_specs=out_spec, out_shape=out_shape,
        scratch_shapes=[] if nk == 1 else [pltpu.VMEM((tm, tn), F32)],
        compiler_params=_cparams(),
    )(a, b, *after)


def _prenorm_fwd(x, g, token=None):
    tm = 512
    after, after_specs = _after(token)

    def body(x_ref, g_ref, *rest):
        xv = x_ref[...]
        r = lax.rsqrt(jnp.mean(xv * xv, axis=-1, keepdims=True) + EPS)
        rest[-1][...] = (xv * r * g_ref[...]).astype(BF16)

    return pl.pallas_call(
        body, name="prenorm_fwd", grid=(SEQ // tm,),
        in_specs=[pl.BlockSpec((tm, D_MODEL), lambda i: (i, 0)), pl.BlockSpec((1, D_MODEL), lambda i: (0, 0))] + after_specs,
        out_specs=pl.BlockSpec((tm, D_MODEL), lambda i: (i, 0)),
        out_shape=jax.ShapeDtypeStruct((SEQ, D_MODEL), BF16),
    )(x, g, *after)


def _dh_prenorm_bwd(dp, w_pad_t, x, dy, g, token=None):
    tm, tk = 1024, 1408
    nk = N_PAD // tk
    after, after_specs = _after(token)

    def body(a_ref, b_ref, x_ref, dy_ref, g_ref, *rest):
        gx_ref, dg_ref, acc_ref = rest[-3:]
        i, kk = pl.program_id(0), pl.program_id(1)
        part = _nn(a_ref[...], b_ref[...])

        @pl.when(kk == 0)
        def _():
            acc_ref[...] = part

        @pl.when(kk > 0)
        def _():
            acc_ref[...] += part

        @pl.when(kk == nk - 1)
        def _():
            xv = x_ref[...]
            r = lax.rsqrt(jnp.mean(xv * xv, axis=-1, keepdims=True) + EPS)
            n = xv * r
            dhv = acc_ref[...]
            dn = dhv * g_ref[...]
            gx_ref[...] = dy_ref[...] + r * (dn - n * jnp.mean(dn * n, axis=-1, keepdims=True))
            cols = jnp.sum(dhv * n, axis=0, keepdims=True)

            @pl.when(i == 0)
            def _():
                dg_ref[...] = cols

            @pl.when(i > 0)
            def _():
                dg_ref[...] += cols

    row = pl.BlockSpec((tm, D_MODEL), lambda i, kk: (i, 0))
    vec = pl.BlockSpec((1, D_MODEL), lambda i, kk: (0, 0))
    return pl.pallas_call(
        body, name="dh_prenorm_bwd", grid=(SEQ // tm, nk),
        in_specs=[pl.BlockSpec((tm, tk), lambda i, kk: (i, kk)), pl.BlockSpec((tk, D_MODEL), lambda i, kk: (kk, 0)),
                  row, row, vec] + after_specs,
        out_specs=[row, vec],
        out_shape=[jax.ShapeDtypeStruct((SEQ, D_MODEL), F32), jax.ShapeDtypeStruct((1, D_MODEL), F32)],
        scratch_shapes=[pltpu.VMEM((tm, D_MODEL), F32)],
        compiler_params=_cparams(),
    )(dp, w_pad_t, x, dy, g, *after)


def _mla_prep_fwd(p, qg, kvg, wuq, wk, wv, rc, rs):
    tm = 512

    def body(lat_ref, qg_ref, kvg_ref, wuq_ref, wk_ref, wv_ref, c_ref, s_ref, q_ref, k_ref, v_ref):
        c, s = c_ref[...], s_ref[...]
        lanes = _rope_lanes((tm, 128), MLA_ROPE_HALF, 128, 64)
        cq = lat_ref[:, 0:Q_RANK]
        r1 = lax.rsqrt(jnp.mean(cq * cq, axis=-1, keepdims=True) + EPS)
        cqn = (cq * r1 * qg_ref[...]).astype(BF16)
        q = _nn(cqn, wuq_ref[...])
        for h in range(MLA_HEADS):
            sl = slice(h * 128, (h + 1) * 128)
            q_ref[:, sl] = (_rope_fwd(q[:, sl], c, s, MLA_ROPE_HALF, lanes) * MLA_SCALE).astype(BF16)
        ckv = lat_ref[:, Q_RANK:Q_RANK + KV_RANK]
        r2 = lax.rsqrt(jnp.mean(ckv * ckv, axis=-1, keepdims=True) + EPS)
        ckvn = (ckv * r2 * kvg_ref[...]).astype(BF16)
        krr = _rope_fwd(lat_ref[:, Q_RANK + KV_RANK:N_LAT], c, s, MLA_ROPE_HALF, lanes)
        kn = _nn(ckvn, wk_ref[...])
        for h in range(MLA_HEADS):
            sl = slice(h * 128, (h + 1) * 128)
            k_ref[:, sl] = (kn[:, sl] + krr).astype(BF16)
        v_ref[...] = _nn(ckvn, wv_ref[...]).astype(BF16)

    def full(shape):
        return pl.BlockSpec(shape, lambda i: (0, 0))

    def rows(w):
        return pl.BlockSpec((tm, w), lambda i: (i, 0))

    return pl.pallas_call(
        body, name="mla_prep_fwd", grid=(SEQ // tm,),
        in_specs=[pl.BlockSpec((tm, N_LAT), lambda i: (i, 0)),
                  full((1, Q_RANK)), full((1, KV_RANK)), full((Q_RANK, 1024)), full((KV_RANK, 1024)),
                  full((KV_RANK, 512)), rows(128), rows(128)],
        out_specs=[rows(1024), rows(1024), rows(512)],
        out_shape=[jax.ShapeDtypeStruct((SEQ, 1024), BF16), jax.ShapeDtypeStruct((SEQ, 1024), BF16),
                   jax.ShapeDtypeStruct((SEQ, 512), BF16)],
        compiler_params=_cparams(),
    )(p, qg, kvg, wuq, wk, wv, rc, rs)


def _mla_prep_bwd(dp_in, p, dq, dk, dv, qg, kvg, wuq, wk, wv, rc, rs):
    tm = 512

    def body(dp_any, lat_ref, dq_ref, dk_ref, dv_ref, qg_ref, kvg_ref, wuq_ref, wk_ref, wv_ref,
             c_ref, s_ref, dp_ref, dwuq_ref, dwk_ref, dwv_ref, dgq_ref, dgkv_ref, dqb_ref, dkb_ref):
        del dp_any
        c, s = c_ref[...], s_ref[...]
        lanes = _rope_lanes((tm, 128), MLA_ROPE_HALF, 128, 64)
        lane = lax.broadcasted_iota(jnp.int32, (tm, 128), 1)
        dkr = jnp.zeros((tm, 128), F32)
        for h in range(MLA_HEADS):
            sl = slice(h * 128, (h + 1) * 128)
            dqb_ref[:, sl] = _rope_bwd(dq_ref[:, sl] * MLA_SCALE, c, s, MLA_ROPE_HALF, lanes).astype(BF16)
            dkh = dk_ref[:, sl]
            dkr = dkr + dkh
            dkb_ref[:, sl] = jnp.where(lane < 64, dkh, 0.0).astype(BF16)
        dkr = jnp.where((lane >= 64) & (lane < 96), dkr, 0.0)
        dkr = _rope_bwd(dkr, c, s, MLA_ROPE_HALF, lanes)
        dvb = dv_ref[...].astype(BF16)

        cq = lat_ref[:, 0:Q_RANK]
        r1 = lax.rsqrt(jnp.mean(cq * cq, axis=-1, keepdims=True) + EPS)
        n1 = cq * r1
        dcqn = _nt(dqb_ref[...], wuq_ref[...])
        dn1 = dcqn * qg_ref[...]
        dcq = r1 * (dn1 - n1 * jnp.mean(dn1 * n1, axis=-1, keepdims=True))
        pq = jnp.sum(dcqn * n1, axis=0, keepdims=True)

        ckv = lat_ref[:, Q_RANK:Q_RANK + KV_RANK]
        r2 = lax.rsqrt(jnp.mean(ckv * ckv, axis=-1, keepdims=True) + EPS)
        n2 = ckv * r2
        dckvn = _nt(dkb_ref[...], wk_ref[...]) + _nt(dvb, wv_ref[...])
        dn2 = dckvn * kvg_ref[...]
        dckv = r2 * (dn2 - n2 * jnp.mean(dn2 * n2, axis=-1, keepdims=True))
        pkv = jnp.sum(dckvn * n2, axis=0, keepdims=True)
        cqn = (n1 * qg_ref[...]).astype(BF16)
        ckvn = (n2 * kvg_ref[...]).astype(BF16)
        wq, wk_, wv_ = _tn(cqn, dqb_ref[...]), _tn(ckvn, dkb_ref[...]), _tn(ckvn, dvb)

        dp_ref[:, 0:Q_RANK] = dcq.astype(BF16)
        dp_ref[:, Q_RANK:Q_RANK + KV_RANK] = dckv.astype(BF16)
        dp_ref[:, Q_RANK + KV_RANK:N_LAT] = dkr.astype(BF16)

        @pl.when(pl.program_id(0) == 0)
        def _():
            dgq_ref[...] = pq
            dgkv_ref[...] = pkv
            dwuq_ref[...] = wq
            dwk_ref[...] = wk_
            dwv_ref[...] = wv_

        @pl.when(pl.program_id(0) > 0)
        def _():
            dgq_ref[...] += pq
            dgkv_ref[...] += pkv
            dwuq_ref[...] += wq
            dwk_ref[...] += wk_
            dwv_ref[...] += wv_

    def full(shape):
        return pl.BlockSpec(shape, lambda i: (0, 0))

    def rows(w):
        return pl.BlockSpec((tm, w), lambda i: (i, 0))

    lat = pl.BlockSpec((tm, N_LAT), lambda i: (i, 0))
    dlat = pl.BlockSpec((tm, N_LAT), lambda i: (i, COL_LAT // N_LAT))
    return pl.pallas_call(
        body, name="mla_prep_bwd", grid=(SEQ // tm,),
        in_specs=[pl.BlockSpec(memory_space=pl.ANY), lat, rows(1024), rows(1024), rows(512),
                  full((1, Q_RANK)), full((1, KV_RANK)), full((Q_RANK, 1024)), full((KV_RANK, 1024)),
                  full((KV_RANK, 512)), rows(128), rows(128)],
        out_specs=[dlat, full((Q_RANK, 1024)), full((KV_RANK, 1024)), full((KV_RANK, 512)),
                   full((1, Q_RANK)), full((1, KV_RANK))],
        out_shape=[jax.ShapeDtypeStruct((SEQ, N_PAD), BF16), jax.ShapeDtypeStruct((Q_RANK, 1024), F32),
                   jax.ShapeDtypeStruct((KV_RANK, 1024), F32), jax.ShapeDtypeStruct((KV_RANK, 512), F32),
                   jax.ShapeDtypeStruct((1, Q_RANK), F32), jax.ShapeDtypeStruct((1, KV_RANK), F32)],
        input_output_aliases={0: 0},
        scratch_shapes=[pltpu.VMEM((tm, 1024), BF16), pltpu.VMEM((tm, 1024), BF16)],
        compiler_params=_cparams(),
    )(dp_in, p, dq, dk, dv, qg, kvg, wuq, wk, wv, rc, rs)


FLASH_T = 1024


def _head_half(shape, hh):
    lane = lax.broadcasted_iota(jnp.int32, shape, 1)
    return (lane < 64) if hh == 0 else (lane >= 64)


def _diag_keep(nr, nk):
    row = lax.broadcasted_iota(jnp.int32, (nr, nk), 0)
    col = lax.broadcasted_iota(jnp.int32, (nr, nk), 1)
    return row + (nk - nr) >= col


def _tri_steps(nb, q_major):
    if q_major:
        pairs = [(i, kb) for i in range(nb) for kb in range(i + 1)]
    else:
        pairs = [(i, kb) for kb in range(nb) for i in range(kb, nb)]
    return jnp.asarray([p[0] for p in pairs], jnp.int32), jnp.asarray([p[1] for p in pairs], jnp.int32)


def _mla_flash_fwd(q, k, v):
    t = FLASH_T
    nb = SEQ // t
    qtab, ktab = _tri_steps(nb, True)

    def body(qi_ref, ki_ref, q_ref, k_ref, v_ref, o_ref, lse_ref, m_scr, l_scr, acc_scr):
        step = pl.program_id(1)
        i, kb = qi_ref[step], ki_ref[step]

        @pl.when(kb == 0)
        def _():
            m_scr[...] = jnp.full_like(m_scr, NEG)
            l_scr[...] = jnp.zeros_like(l_scr)
            acc_scr[...] = jnp.zeros_like(acc_scr)

        def update(r0, nr, nk, diagonal):
            rs = slice(r0, r0 + nr)
            vv = v_ref[0:nk, :]
            for hh in range(2):
                sl = slice(hh * 128, (hh + 1) * 128)
                s = _nt(q_ref[rs, sl], k_ref[0:nk, sl])
                if diagonal:
                    s = jnp.where(_diag_keep(nr, nk), s, NEG)
                m_prev = m_scr[hh, rs, :]
                m_new = jnp.maximum(m_prev, jnp.max(s, axis=-1, keepdims=True))
                pr = jnp.exp(s - jnp.tile(m_new, (1, nk // 128)))
                alpha = jnp.exp(m_prev - m_new)
                l_scr[hh, rs, :] = alpha * l_scr[hh, rs, :] + jnp.sum(pr, axis=-1, keepdims=True)
                acc_scr[hh, rs, :] = alpha * acc_scr[hh, rs, :] + _nn(pr.astype(BF16), vv)
                m_scr[hh, rs, :] = m_new

        @pl.when(kb < i)
        def _():
            update(0, t, t, False)

        @pl.when(kb == i)
        def _():
            update(0, t // 2, t // 2, True)
            update(t // 2, t // 2, t, True)
            o0 = acc_scr[0] / l_scr[0]
            o1 = acc_scr[1] / l_scr[1]
            o_ref[...] = jnp.where(_head_half((t, 128), 0), o0, o1)
            for hh in range(2):
                lse_ref[:, hh * 128:(hh + 1) * 128] = m_scr[hh] + jnp.log(l_scr[hh])

    grid_spec = pltpu.PrefetchScalarGridSpec(
        num_scalar_prefetch=2, grid=(4, qtab.shape[0]),
        in_specs=[pl.BlockSpec((t, 256), lambda j, s, qi, ki: (qi[s], j)),
                  pl.BlockSpec((t, 256), lambda j, s, qi, ki: (ki[s], j)),
                  pl.BlockSpec((t, 128), lambda j, s, qi, ki: (ki[s], j))],
        out_specs=[pl.BlockSpec((t, 128), lambda j, s, qi, ki: (qi[s], j)),
                   pl.BlockSpec((t, 256), lambda j, s, qi, ki: (qi[s], j))],
        scratch_shapes=[pltpu.VMEM((2, t, 128), F32), pltpu.VMEM((2, t, 128), F32), pltpu.VMEM((2, t, 128), F32)])
    return pl.pallas_call(
        body, name="mla_flash_fwd", grid_spec=grid_spec,
        out_shape=[jax.ShapeDtypeStruct((SEQ, 512), F32), jax.ShapeDtypeStruct((SEQ, 1024), F32)],
        compiler_params=_cparams(),
    )(qtab, ktab, q, k, v)


def _mla_flash_bwd(q, k, v, o, do, lse, token=None):
    t = FLASH_T
    nb = SEQ // t
    qtab, ktab = _tri_steps(nb, False)
    after, after_specs = _after(token)

    def body(qi_ref, ki_ref, q_ref, k_ref, v_ref, o_ref, do_ref, lse_ref, *rest):
        dq_ref, dk_ref, dv_ref, dk_scr, dv_scr = rest[-5:]
        step = pl.program_id(1)
        i, kb = qi_ref[step], ki_ref[step]

        @pl.when(step == 0)
        def _():
            dq_ref[...] = jnp.zeros_like(dq_ref)

        @pl.when(i == kb)
        def _():
            dk_scr[...] = jnp.zeros_like(dk_scr)
            dv_scr[...] = jnp.zeros_like(dv_scr)

        def update(r0, nr, nk, diagonal):
            rs = slice(r0, r0 + nr)
            vv = v_ref[0:nk, :]
            ov = o_ref[rs, :]
            dov = do_ref[rs, :]
            rows = pl.ds(pl.multiple_of(i * t + r0, t // 2), nr)
            for hh in range(2):
                sl = slice(hh * 128, (hh + 1) * 128)
                qh, kh = q_ref[rs, sl], k_ref[0:nk, sl]
                s = _nt(qh, kh)
                if diagonal:
                    s = jnp.where(_diag_keep(nr, nk), s, NEG)
                pr = jnp.exp(s - jnp.tile(lse_ref[rs, sl], (1, nk // 128)))
                dom = jnp.where(_head_half((nr, 128), hh), dov, 0.0)
                domb = dom.astype(BF16)
                dv_scr[0:nk, :] += _tn(pr.astype(BF16), domb)
                dpr = _nt(domb, vv)
                delta = jnp.sum(dom * ov, axis=-1, keepdims=True)
                ds = (pr * (dpr - delta)).astype(BF16)
                dq_ref[rows, sl] += _nn(ds, kh)
                dk_scr[hh, 0:nk, :] += _tn(ds, qh)

        @pl.when(i > kb)
        def _():
            update(0, t, t, False)

        @pl.when(i == kb)
        def _():
            update(0, t // 2, t // 2, True)
            update(t // 2, t // 2, t, True)

        @pl.when(i == nb - 1)
        def _():
            dk_ref[:, 0:128] = dk_scr[0]
            dk_ref[:, 128:256] = dk_scr[1]
            dv_ref[...] = dv_scr[...]

    qi_map = lambda j, s, qi, ki: (qi[s], j)
    ki_map = lambda j, s, qi, ki: (ki[s], j)
    grid_spec = pltpu.PrefetchScalarGridSpec(
        num_scalar_prefetch=2, grid=(4, qtab.shape[0]),
        in_specs=[pl.BlockSpec((t, 256), qi_map), pl.BlockSpec((t, 256), ki_map), pl.BlockSpec((t, 128), ki_map),
                  pl.BlockSpec((t, 128), qi_map), pl.BlockSpec((t, 128), qi_map), pl.BlockSpec((t, 256), qi_map)]
        + after_specs,
        out_specs=[pl.BlockSpec((SEQ, 256), lambda j, s, qi, ki: (0, j)), pl.BlockSpec((t, 256), ki_map),
                   pl.BlockSpec((t, 128), ki_map)],
        scratch_shapes=[pltpu.VMEM((2, t, 128), F32), pltpu.VMEM((t, 128), F32)])
    return pl.pallas_call(
        body, name="mla_flash_bwd", grid_spec=grid_spec,
        out_shape=[jax.ShapeDtypeStruct((SEQ, 1024), F32), jax.ShapeDtypeStruct((SEQ, 1024), F32),
                   jax.ShapeDtypeStruct((SEQ, 512), F32)],
        compiler_params=_cparams(),
    )(qtab, ktab, q, k, v, o, do, lse, *after)


def _strided(start, size, d):
    return pl.ds(start, size) if d == 1 else pl.ds(start, size, stride=d)


DIL_ST_FWD, DIL_ST_BWD = 2048, 2048


def _band_keep(g, b, t, nb):
    nbs = SEQ // DIL_DILATIONS[g] // BAND
    row = lax.broadcasted_iota(jnp.int32, (BAND, 2 * BAND), 0)
    col = lax.broadcasted_iota(jnp.int32, (BAND, 2 * BAND), 1)
    cur = (col >= BAND) & (row >= col - BAND)
    prev = (col < BAND) & (col >= row)
    if nbs >= nb:
        if b > 0:
            return cur | prev
        return cur | (prev & ((t * nb) % nbs != 0))
    return cur | prev if b % nbs else cur


def _dil_tok(g, b, t, nb):
    d = DIL_DILATIONS[g]
    nbs = SEQ // d // BAND
    gb = t * nb + b
    return _strided((gb % nbs) * BAND * d + gb // nbs, BAND, d)


def _dil_attn_fwd(p_qkv, rc, rs, g):
    d = DIL_DILATIONS[g]
    sub_len = SEQ // d
    ch = min(sub_len, 512)
    DIL_ST, DIL_NB = DIL_ST_FWD, DIL_ST_FWD // BAND

    def body(p_ref, c_ref, sn_ref, o_ref, l_ref, qkv_ref, x_scr, s_scr, p_scr, o_scr):
        t = pl.program_id(1)

        @pl.when(t == 0)
        def _():
            lanes = _rope_lanes((ch, 128), DIL_ROPE_HALF, 64, 0)
            for tq in range(3):
                qkv_ref[tq, 0, 0:BAND, :] = jnp.zeros((BAND, 128), BF16)
                mult = DIL_SCALE if tq == 0 else 1.0
                for c0 in range(0, SEQ, ch):
                    rows = pl.ds(c0, ch)
                    xv = p_ref[tq, rows, :].astype(F32)
                    x_scr[rows, :] = xv if tq == 2 else _rope_fwd(xv, c_ref[rows, :] * mult, sn_ref[rows, :] * mult,
                                                                   DIL_ROPE_HALF, lanes)
                for r in range(d):
                    for c0 in range(0, sub_len, ch):
                        at = BAND + r * sub_len + c0
                        qkv_ref[tq, 0, at:at + ch, :] = x_scr[_strided(r + c0 * d, ch, d), :].astype(BF16)

        base = t * DIL_ST
        half0 = _head_half((DIL_ST, 128), 0)
        lse_h = []
        for hh in range(2):
            half = _head_half((BAND, 128), hh)
            for b in range(DIL_NB):
                qv = qkv_ref[0, 0, pl.ds(pl.multiple_of(base + (b + 1) * BAND, BAND), BAND), :]
                k2 = qkv_ref[1, 0, pl.ds(pl.multiple_of(base + b * BAND, BAND), 2 * BAND), :]
                sb = _nt(jnp.where(half, qv, jnp.zeros_like(qv)), k2)
                s_scr[b * BAND:(b + 1) * BAND, :] = jnp.where(_band_keep(g, b, t, DIL_NB), sb, NEG)
            s = s_scr[...]
            m = jnp.max(s, axis=-1, keepdims=True)
            pr = jnp.exp(s - m)
            den = jnp.sum(pr, axis=-1, keepdims=True)
            p_scr[...] = pr.astype(BF16)
            for b in range(DIL_NB):
                v2 = qkv_ref[2, 0, pl.ds(pl.multiple_of(base + b * BAND, BAND), 2 * BAND), :]
                o_scr[hh, b * BAND:(b + 1) * BAND, :] = _nn(p_scr[b * BAND:(b + 1) * BAND, :], v2)
            o_scr[hh] = o_scr[hh] / den
            lse_h.append(m + jnp.log(den))
        out = jnp.where(half0, o_scr[0], o_scr[1])
        lse = jnp.where(half0, lse_h[0], lse_h[1])
        for b in range(DIL_NB):
            tok = _dil_tok(g, b, t, DIL_NB)
            o_ref[tok, :] = out[b * BAND:(b + 1) * BAND, :]
            l_ref[tok, :] = lse[b * BAND:(b + 1) * BAND, :]

    tab = pl.BlockSpec((SEQ, 128), lambda pr, t: (0, 0))
    out = pl.BlockSpec((SEQ, 128), lambda pr, t: (0, pr))
    return pl.pallas_call(
        body, name=f"dil_attn_fwd_g{g}", grid=(4, SEQ // DIL_ST),
        in_specs=[pl.BlockSpec((None, 3, SEQ, 128), lambda pr, t: (g * 4 + pr, 0, 0, 0)), tab, tab],
        out_specs=[out, out, pl.BlockSpec((3, 1, BAND + SEQ, 128), lambda pr, t: (0, pr, 0, 0))],
        out_shape=[jax.ShapeDtypeStruct((SEQ, 512), F32), jax.ShapeDtypeStruct((SEQ, 512), F32),
                   jax.ShapeDtypeStruct((3, 4, BAND + SEQ, 128), BF16)],
        scratch_shapes=[pltpu.VMEM((SEQ, 128), F32), pltpu.VMEM((DIL_ST, 2 * BAND), F32),
                        pltpu.VMEM((DIL_ST, 2 * BAND), BF16), pltpu.VMEM((2, DIL_ST, 128), F32)],
        compiler_params=_cparams(),
    )(p_qkv.reshape(12, 3, SEQ, 128), rc, rs)


def _dil_attn_bwd(dp_in, qkv, dyd, yd, lse_all, rc, rs, g, token=None):
    d = DIL_DILATIONS[g]
    sub_len = SEQ // d
    DIL_ST, DIL_NB = DIL_ST_BWD, DIL_ST_BWD // BAND
    nst = SEQ // DIL_ST
    after, after_specs = _after(token)
    ch = 512

    def body(dp_any, q_ref, k_ref, v_ref, do_ref, y_ref, l_ref, c_ref, sn_ref, *rest):
        dp_ref, tok_scr, dk_scr, dv_scr, s_scr, dp_scr, p_scr, ds_scr, do_scr, y_scr, l_scr, dq_scr = rest[-12:]
        del dp_any
        t = pl.program_id(1)
        base = t * DIL_ST

        @pl.when(t == 0)
        def _():
            dk_scr[...] = jnp.zeros_like(dk_scr)
            dv_scr[...] = jnp.zeros_like(dv_scr)

        for b in range(DIL_NB):
            tok = _dil_tok(g, b, t, DIL_NB)
            do_scr[b * BAND:(b + 1) * BAND, :] = do_ref[tok, :]
            y_scr[b * BAND:(b + 1) * BAND, :] = y_ref[tok, :]
            l_scr[b * BAND:(b + 1) * BAND, :] = l_ref[tok, :]
        for hh in range(2):
            half = _head_half((BAND, 128), hh)
            half_st = _head_half((DIL_ST, 128), hh)
            dom = jnp.where(half_st, do_scr[...], 0.0)
            delta = jnp.sum(dom * y_scr[...], axis=-1, keepdims=True)
            lcol = jnp.max(jnp.where(half_st, l_scr[...], NEG), axis=-1, keepdims=True)
            for b in range(DIL_NB):
                rows = slice(b * BAND, (b + 1) * BAND)
                qv = q_ref[0, 0, pl.ds(pl.multiple_of(base + (b + 1) * BAND, BAND), BAND), :]
                band = pl.ds(pl.multiple_of(base + b * BAND, BAND), 2 * BAND)
                sb = _nt(jnp.where(half, qv, jnp.zeros_like(qv)), k_ref[0, 0, band, :])
                s_scr[rows, :] = jnp.where(_band_keep(g, b, t, DIL_NB), sb, NEG)
                dp_scr[rows, :] = _nt(dom[rows, :].astype(BF16), v_ref[0, 0, band, :])
            pr = jnp.exp(s_scr[...] - lcol)
            p_scr[...] = pr.astype(BF16)
            ds_scr[...] = (pr * (dp_scr[...] - delta)).astype(BF16)
            for b in range(DIL_NB):
                rows = slice(b * BAND, (b + 1) * BAND)
                qv = q_ref[0, 0, pl.ds(pl.multiple_of(base + (b + 1) * BAND, BAND), BAND), :]
                band = pl.ds(pl.multiple_of(base + b * BAND, BAND), 2 * BAND)
                dqb = jnp.where(half, _nn(ds_scr[rows, :], k_ref[0, 0, band, :]), 0.0)
                if hh == 0:
                    dq_scr[rows, :] = dqb
                else:
                    dq_scr[rows, :] += dqb
                half2 = _head_half((2 * BAND, 128), hh)
                dk_scr[band, :] += jnp.where(half2, _tn(ds_scr[rows, :], qv), 0.0)
                dv_scr[band, :] += _tn(p_scr[rows, :], dom[rows, :].astype(BF16))
        for b in range(DIL_NB):
            tok_scr[pl.ds(0, 1), _dil_tok(g, b, t, DIL_NB), :] = dq_scr[b * BAND:(b + 1) * BAND, :][None]

        @pl.when(t == nst - 1)
        def _():
            for r in range(d):
                rows = _strided(r, sub_len, d)
                tok_scr[pl.ds(1, 1), rows, :] = dk_scr[BAND + r * sub_len:BAND + (r + 1) * sub_len, :][None]
                tok_scr[pl.ds(2, 1), rows, :] = dv_scr[BAND + r * sub_len:BAND + (r + 1) * sub_len, :][None]
            lanes = _rope_lanes((ch, 128), DIL_ROPE_HALF, 64, 0)
            for c0 in range(0, SEQ, ch):
                rows = slice(c0, c0 + ch)
                cv, sv = c_ref[rows, :], sn_ref[rows, :]
                dp_ref[rows, 0:128] = _rope_bwd(tok_scr[0, rows, :], cv * DIL_SCALE, sv * DIL_SCALE, DIL_ROPE_HALF, lanes).astype(BF16)
                dp_ref[rows, 128:256] = _rope_bwd(tok_scr[1, rows, :], cv, sv, DIL_ROPE_HALF, lanes).astype(BF16)
                dp_ref[rows, 256:384] = tok_scr[2, rows, :].astype(BF16)

    def inp(tq):
        return pl.BlockSpec((1, 1, BAND + SEQ, 128), lambda pr, t: (tq, pr, 0, 0))

    tok_spec = pl.BlockSpec((SEQ, 128), lambda pr, t: (0, pr))
    tab = pl.BlockSpec((SEQ, 128), lambda pr, t: (0, 0))
    st = (DIL_ST, 2 * BAND)
    return pl.pallas_call(
        body, name=f"dil_attn_bwd_g{g}", grid=(4, nst),
        in_specs=[pl.BlockSpec(memory_space=pl.ANY), inp(0), inp(1), inp(2), tok_spec, tok_spec, tok_spec, tab, tab]
        + after_specs,
        out_specs=pl.BlockSpec((SEQ, 384), lambda pr, t: (0, _qkv_block(0, g, pr) // 3)),
        out_shape=jax.ShapeDtypeStruct((SEQ, N_PAD), BF16),
        input_output_aliases={0: 0},
        scratch_shapes=[pltpu.VMEM((3, SEQ, 128), F32),
                        pltpu.VMEM((BAND + SEQ, 128), F32), pltpu.VMEM((BAND + SEQ, 128), F32),
                        pltpu.VMEM(st, F32), pltpu.VMEM(st, F32), pltpu.VMEM(st, BF16), pltpu.VMEM(st, BF16),
                        pltpu.VMEM((DIL_ST, 128), F32), pltpu.VMEM((DIL_ST, 128), F32), pltpu.VMEM((DIL_ST, 128), F32),
                        pltpu.VMEM((DIL_ST, 128), F32)],
        compiler_params=_cparams(),
    )(dp_in, qkv, qkv, qkv, dyd, yd, lse_all, rc, rs, *after)


TAIL_T = 256


def _tail(p, ya, o_g, l_g, x, target, wpm, wpd, wout, post_g):
    tm = TAIL_T

    def body(pgz_ref, ya_ref, o0_ref, o1_ref, o2_ref, l0_ref, l1_ref, l2_ref, x_ref, t_ref,
             wpm_ref, wpd_ref, wout_ref, pg_ref,
             dp_ref, dy_ref, dya_ref, dyd_ref, yd_ref, lse_ref, loss_ref, dgp_ref, dwpm_ref, dwpd_ref, dwout_ref):
        l0, l1, l2 = l0_ref[...], l1_ref[...], l2_ref[...]
        mx = jnp.maximum(jnp.maximum(l0, l1), l2)
        e0, e1, e2 = jnp.exp(l0 - mx), jnp.exp(l1 - mx), jnp.exp(l2 - mx)
        den = e0 + e1 + e2
        yd = (e0 * o0_ref[...] + e1 * o1_ref[...] + e2 * o2_ref[...]) / den
        yd_ref[...] = yd
        lse_ref[...] = mx + jnp.log(den)
        ya = ya_ref[...]

        gm, gd = pgz_ref[:, 0:1024], pgz_ref[:, 1024:2048]
        zm, zd = pgz_ref[:, 2048:2560], pgz_ref[:, 2560:3072]
        szm, szd = _sigmoid(zm), _sigmoid(zd)
        sm, sd = zm * szm, zd * szd
        ua = (ya * sm).astype(BF16)
        ud = (yd * sd).astype(BF16)
        pa = _nn(ua, wpm_ref[...])
        pd = _nn(ud, wpd_ref[...])
        sgm, sgd = _sigmoid(gm), _sigmoid(gd)
        mg = (sgm * pa + sgd * pd).astype(BF16)
        t = _nn(mg, wout_ref[...])
        r3 = lax.rsqrt(jnp.mean(t * t, axis=-1, keepdims=True) + EPS)
        n = t * r3
        pg = pg_ref[...]
        err = x_ref[...] + n * pg - t_ref[...]
        lpart = jnp.sum(err * err, axis=0, keepdims=True)

        dy = err * (1.0 / D_MODEL)
        dy_ref[...] = dy
        gpart = jnp.sum(dy * n, axis=0, keepdims=True)
        dn = dy * pg
        dt = (r3 * (dn - n * jnp.mean(dn * n, axis=-1, keepdims=True))).astype(BF16)
        dmg = _nt(dt, wout_ref[...])
        dpa = (dmg * sgm).astype(BF16)
        dpd = (dmg * sgd).astype(BF16)
        dp_ref[:, 0:1024] = (dmg * pa * sgm * (1.0 - sgm)).astype(BF16)
        dp_ref[:, 1024:2048] = (dmg * pd * sgd * (1.0 - sgd)).astype(BF16)
        dua = _nt(dpa, wpm_ref[...])
        dud = _nt(dpd, wpd_ref[...])
        dya_ref[...] = dua * sm
        dyd_ref[...] = dud * sd
        dp_ref[:, 2048:2560] = (dua * ya * szm * (1.0 + zm * (1.0 - szm))).astype(BF16)
        dp_ref[:, 2560:3072] = (dud * yd * szd * (1.0 + zd * (1.0 - szd))).astype(BF16)

        wpm, wpd, wout = _tn(ua, dpa), _tn(ud, dpd), _tn(mg, dt)

        @pl.when(pl.program_id(0) == 0)
        def _():
            loss_ref[...] = lpart
            dgp_ref[...] = gpart
            dwpm_ref[...] = wpm
            dwpd_ref[...] = wpd
            dwout_ref[...] = wout

        @pl.when(pl.program_id(0) > 0)
        def _():
            loss_ref[...] += lpart
            dgp_ref[...] += gpart
            dwpm_ref[...] += wpm
            dwpd_ref[...] += wpd
            dwout_ref[...] += wout

    def rows(w):
        return pl.BlockSpec((tm, w), lambda i: (i, 0))

    def full(shape):
        return pl.BlockSpec(shape, lambda i: (0, 0))

    def sds(w, dt):
        return jax.ShapeDtypeStruct((SEQ, w), dt)

    return pl.pallas_call(
        body, name="tail", grid=(SEQ // tm,),
        in_specs=[rows(3072), rows(512), rows(512), rows(512), rows(512), rows(512), rows(512), rows(512),
                  rows(1024), rows(1024), full((512, 1024)), full((512, 1024)), full((1024, 1024)), full((1, 1024))],
        out_specs=[rows(3072), rows(1024), rows(512), rows(512), rows(512), rows(512), full((1, 1024)), full((1, 1024)),
                   full((512, 1024)), full((512, 1024)), full((1024, 1024))],
        out_shape=[sds(N_PAD, BF16), sds(1024, F32), sds(512, F32), sds(512, F32), sds(512, F32), sds(512, F32),
                   jax.ShapeDtypeStruct((1, 1024), F32), jax.ShapeDtypeStruct((1, 1024), F32),
                   jax.ShapeDtypeStruct((512, 1024), F32), jax.ShapeDtypeStruct((512, 1024), F32),
                   jax.ShapeDtypeStruct((1024, 1024), F32)],
        compiler_params=_cparams(),
    )(p, ya, o_g[0], o_g[1], o_g[2], l_g[0], l_g[1], l_g[2], x, target, wpm, wpd, wout, post_g)


def _sum_parts(recv, own, me, tr, name):
    n, r, w = recv.shape
    if r % tr:
        return _sum_parts_cols(recv, own, me, name)
    own_spec = (pl.BlockSpec((tr, w), lambda i, me_ref: (i, 0)) if own.ndim == 2
                else pl.BlockSpec((None, tr, w), lambda i, me_ref: (me_ref[0], i, 0)))

    def body(me_ref, p_ref, own_ref, o_ref):
        mine = own_ref[...].astype(F32)
        acc = jnp.zeros((tr, w), F32)
        for s in range(n):
            acc = acc + jnp.where(me_ref[0] == s, mine, p_ref[s].astype(F32))
        o_ref[...] = acc

    return pl.pallas_call(
        body, name=name,
        grid_spec=pltpu.PrefetchScalarGridSpec(
            num_scalar_prefetch=1, grid=(r // tr,),
            in_specs=[pl.BlockSpec((n, tr, w), lambda i, me_ref: (0, i, 0)), own_spec],
            out_specs=pl.BlockSpec((tr, w), lambda i, me_ref: (i, 0))),
        out_shape=jax.ShapeDtypeStruct((r, w), F32),
    )(me.reshape(1), recv, own)


def _sum_parts_cols(recv, own, me, name):
    n, r, w = recv.shape
    tc = 128

    def body(me_ref, p_ref, own_ref, o_ref):
        mine = own_ref[...].astype(F32)
        acc = jnp.zeros((r, tc), F32)
        for s in range(n):
            acc = acc + jnp.where(me_ref[0] == s, mine, p_ref[s].astype(F32))
        o_ref[...] = acc

    return pl.pallas_call(
        body, name=name,
        grid_spec=pltpu.PrefetchScalarGridSpec(
            num_scalar_prefetch=1, grid=(w // tc,),
            in_specs=[pl.BlockSpec((n, r, tc), lambda i, me_ref: (0, 0, i)),
                      pl.BlockSpec((None, r, tc), lambda i, me_ref: (me_ref[0], 0, i))],
            out_specs=pl.BlockSpec((r, tc), lambda i, me_ref: (0, i))),
        out_shape=jax.ShapeDtypeStruct((r, w), F32),
    )(me.reshape(1), recv, own)


def _adamw(w, g, m, v, name):
    lead = w.shape[:-2]
    r, c = w.shape[-2:]
    tr = max([t for t in range(8, 257, 8) if r % t == 0], default=r)
    c1 = 1.0 - ADAM_B1 ** ADAM_STEP
    c2 = 1.0 - ADAM_B2 ** ADAM_STEP

    def body(w_ref, g_ref, m_ref, v_ref, d_ref, nm_ref, nv_ref):
        gv = g_ref[...]
        nm = ADAM_B1 * m_ref[...] + (1.0 - ADAM_B1) * gv
        nv = ADAM_B2 * v_ref[...] + (1.0 - ADAM_B2) * (gv * gv)
        nm_ref[...] = nm
        nv_ref[...] = nv
        d_ref[...] = -ADAM_LR * ((nm / c1) / (jnp.sqrt(nv / c2) + ADAM_EPS) + ADAM_WD * w_ref[...])

    zeros = (0,) * len(lead)
    spec = pl.BlockSpec((1,) * len(lead) + (tr, c), lambda i: zeros + (i, 0))
    sd = jax.ShapeDtypeStruct(w.shape, F32)
    return pl.pallas_call(
        body, name=name, grid=(r // tr,),
        in_specs=[spec] * 4, out_specs=[spec] * 3, out_shape=[sd] * 3,
    )(w, g, m, v)


def _adamw_recv(w, m, v, recv, own, me, name):
    n, r, c = recv.shape
    tr = 128
    c1 = 1.0 - ADAM_B1 ** ADAM_STEP
    c2 = 1.0 - ADAM_B2 ** ADAM_STEP

    def body(me_ref, w_ref, m_ref, v_ref, p_ref, own_ref, d_ref, nm_ref, nv_ref, g_ref):
        mine = own_ref[...].astype(F32)
        gv = jnp.zeros((tr, c), F32)
        for s in range(n):
            gv = gv + jnp.where(me_ref[0] == s, mine, p_ref[s].astype(F32))
        g_ref[0] = gv
        nm = ADAM_B1 * m_ref[0] + (1.0 - ADAM_B1) * gv
        nv = ADAM_B2 * v_ref[0] + (1.0 - ADAM_B2) * (gv * gv)
        nm_ref[0] = nm
        nv_ref[0] = nv
        d_ref[0] = -ADAM_LR * ((nm / c1) / (jnp.sqrt(nv / c2) + ADAM_EPS) + ADAM_WD * w_ref[0])

    full = pl.BlockSpec((1, tr, c), lambda i, me_ref: (0, i, 0))
    sd = jax.ShapeDtypeStruct((1, r, c), F32)
    return pl.pallas_call(
        body, name=name,
        grid_spec=pltpu.PrefetchScalarGridSpec(
            num_scalar_prefetch=1, grid=(r // tr,),
            in_specs=[full, full, full, pl.BlockSpec((n, tr, c), lambda i, me_ref: (0, i, 0)),
                      pl.BlockSpec((None, tr, c), lambda i, me_ref: (me_ref[0], i, 0))],
            out_specs=[full] * 4),
        out_shape=[sd] * 4,
    )(me.reshape(1), w, m, v, recv, own)


def _adamw_in(w_t, m_t, v_t, own_half, swapped, core):
    r, c = SHARD_SHAPES[0]
    tr = max(t for t in range(8, 257, 8) if r % t == 0)
    c1 = 1.0 - ADAM_B1 ** ADAM_STEP
    c2 = 1.0 - ADAM_B2 ** ADAM_STEP

    def body(core_ref, w_ref, m_ref, v_ref, own_ref, sw_ref, d_ref, nm_ref, nv_ref, g_ref):
        own = own_ref[...]
        col_half = lax.broadcasted_iota(jnp.int32, (tr, c), 1) // (c // 2)
        gv = jnp.where(col_half == core_ref[0], jnp.concatenate([own, own], axis=1), sw_ref[...])
        g_ref[0] = gv
        nm = ADAM_B1 * m_ref[0] + (1.0 - ADAM_B1) * gv
        nv = ADAM_B2 * v_ref[0] + (1.0 - ADAM_B2) * (gv * gv)
        nm_ref[0] = nm
        nv_ref[0] = nv
        d_ref[0] = -ADAM_LR * ((nm / c1) / (jnp.sqrt(nv / c2) + ADAM_EPS) + ADAM_WD * w_ref[0])

    full = pl.BlockSpec((1, tr, c), lambda i, core_ref: (0, i, 0))
    sd = jax.ShapeDtypeStruct((1, r, c), F32)
    return pl.pallas_call(
        body, name="adamw_in",
        grid_spec=pltpu.PrefetchScalarGridSpec(
            num_scalar_prefetch=1, grid=(r // tr,),
            in_specs=[full, full, full, pl.BlockSpec((tr, c // 2), lambda i, core_ref: (i, 0)),
                      pl.BlockSpec((tr, c), lambda i, core_ref: (i, 0))],
            out_specs=[full] * 4),
        out_shape=[sd] * 4,
    )(core.reshape(1), w_t, m_t, v_t, own_half, swapped)


ANY = pl.BlockSpec(memory_space=pl.ANY)


def _my_place():
    return lax.axis_index("x"), lax.axis_index("y"), lax.axis_index("c")


HBM = pl.BlockSpec(memory_space=pltpu.HBM)
SEM = pl.BlockSpec(memory_space=pltpu.SEMAPHORE)
DATAFLOW = pltpu.SideEffectType.DATAFLOW_SIDE_EFFECTING


def _near_chips(x, y):
    return [(1 - x, y), (x, 1 - y)]


def _half(mi, hc):
    r, c = SHARD_SHAPES[mi]
    if mi == 0:
        return pl.ds(0, r), pl.ds(pl.multiple_of(hc * (c // 2), 128), c // 2)
    return pl.ds(pl.multiple_of(hc * (r // 2), 16), r // 2), pl.ds(0, c)


def _gather_copies(land_refs, send_sems, recv_sems):
    x, y, c = _my_place()
    out, back = [], []
    for mi in range(N_MATS):
        rows, cols = _half(mi, c)
        mine = land_refs[mi].at[2 * x + y, rows, cols]
        for j, (cx, cy) in enumerate(_near_chips(x, y)):
            sems = dict(send_sem=send_sems.at[mi * 2 + j], recv_sem=recv_sems.at[mi * 2 + j],
                        device_id=(cx, cy, c), device_id_type=MESH)
            out.append(pltpu.make_async_remote_copy(src_ref=mine, dst_ref=mine, **sems))
            got = land_refs[mi].at[2 * cx + cy, rows, cols]
            back.append(pltpu.make_async_remote_copy(src_ref=got, dst_ref=got, **sems))
    return out, back


def _gather_start(landing):
    n = N_MATS

    def body(*refs):
        out, _ = _gather_copies(refs[:n], refs[n], refs[n + 1])
        for cp in out:
            cp.start()
        refs[-1][...] = jnp.zeros_like(refs[-1])

    hbm = [pltpu.HBM(a.shape, a.dtype) for a in landing]
    outs = pl.pallas_call(
        body, name="gather_start",
        out_shape=(pltpu.SemaphoreType.DMA((2 * n,)), pltpu.SemaphoreType.DMA((2 * n,)), *hbm,
                   jax.ShapeDtypeStruct((8, 128), F32)),
        in_specs=[HBM] * n, out_specs=(SEM, SEM, *[HBM] * n, pl.BlockSpec(memory_space=pltpu.VMEM)),
        input_output_aliases={i: 2 + i for i in range(n)},
        compiler_params=pltpu.CompilerParams(has_side_effects=DATAFLOW),
    )(*[pltpu.with_memory_space_constraint(a, pltpu.HBM) for a in landing])
    return outs[:-1], outs[-1]


def _gather_wait(handle, after):
    n = N_MATS

    def body(*refs):
        out, back = _gather_copies(refs[:n], refs[n], refs[n + 1])
        for cp, arrival in zip(out, back):
            cp.wait_send()
            arrival.wait_recv()

    bufs = handle[2:]
    after, after_specs = _after(after)
    res = pl.pallas_call(
        body, name="gather_wait", out_shape=tuple(pltpu.HBM(b.shape, b.dtype) for b in bufs),
        in_specs=[HBM] * n + [SEM, SEM] + after_specs, out_specs=tuple([HBM] * n),
        input_output_aliases={i: i for i in range(n)},
        compiler_params=pltpu.CompilerParams(has_side_effects=DATAFLOW),
    )(*bufs, handle[0], handle[1], *after)
    return list(res)


def _relay_share(gathered):
    n = N_MATS

    def body(*refs):
        out_refs = refs[n:2 * n]
        send_sems, recv_sems = refs[2 * n:]
        x, y, c = _my_place()
        sibling = (x, y, 1 - c)
        relayed = 2 * (x ^ (1 - c)) + (y ^ c)
        relay_to = (x ^ c, y ^ (1 - c), c)
        far = 2 * (1 - x) + (1 - y)
        near = [2 * (1 - x) + y, 2 * x + (1 - y)]

        def copy(k, mi, shard, hc, to):
            blk = out_refs[mi].at[(shard,) + _half(mi, hc)]
            return pltpu.make_async_remote_copy(src_ref=blk, dst_ref=blk, send_sem=send_sems.at[mi * 4 + k],
                                                recv_sem=recv_sems.at[mi * 4 + k], device_id=to, device_id_type=MESH)

        sends = []
        for mi in range(n):
            sends.append(copy(0, mi, relayed, c, relay_to))
            sends += [copy(1 + j, mi, near[j], c, sibling) for j in range(2)]
        for cp in sends:
            cp.start()
        for mi in range(n):
            copy(0, mi, far, c, relay_to).wait_recv()
            cp = copy(3, mi, far, c, sibling)
            cp.start()
            sends.append(cp)
        for mi in range(n):
            for j in range(2):
                copy(1 + j, mi, near[j], 1 - c, sibling).wait_recv()
            copy(3, mi, far, 1 - c, sibling).wait_recv()
        for cp in sends:
            cp.wait_send()

    return pl.pallas_call(
        body, name="relay_share",
        in_specs=[ANY] * n, out_specs=[ANY] * n,
        out_shape=[jax.ShapeDtypeStruct(g.shape, g.dtype) for g in gathered],
        input_output_aliases={i: i for i in range(n)},
        scratch_shapes=[pltpu.SemaphoreType.DMA((4 * n,)), pltpu.SemaphoreType.DMA((4 * n,))],
    )(*gathered)


def _peers(x, y, c):
    out = []
    for k in range(1, 8):
        px, py, pc = x ^ (k >> 2), y ^ ((k >> 1) & 1), c ^ (k & 1)
        out.append((k - 1, (px, py, pc), 4 * px + 2 * py + pc))
    return out


def _exchange_start(parts, name):
    n = len(parts)

    def body(*refs):
        p_refs, land_refs = refs[:n], refs[n:2 * n]
        send_sems, recv_sems, token = refs[2 * n], refs[2 * n + 1], refs[-1]
        x, y, c = _my_place()
        me = 4 * x + 2 * y + c
        for k, dev, peer in _peers(x, y, c):
            for mi in range(n):
                pltpu.make_async_remote_copy(
                    src_ref=p_refs[mi].at[peer], dst_ref=land_refs[mi].at[me], send_sem=send_sems.at[k * n + mi],
                    recv_sem=recv_sems.at[k * n + mi], device_id=dev, device_id_type=MESH).start()
        token[...] = jnp.zeros_like(token)

    hbm = [pltpu.HBM(p.shape, p.dtype) for p in parts]
    outs = pl.pallas_call(
        body, name=name + "_start",
        out_shape=(pltpu.SemaphoreType.DMA((7 * n,)), pltpu.SemaphoreType.DMA((7 * n,)), *hbm, *hbm,
                   jax.ShapeDtypeStruct((8, 128), F32)),
        in_specs=[HBM] * (2 * n), out_specs=(SEM, SEM, *[HBM] * (2 * n), pl.BlockSpec(memory_space=pltpu.VMEM)),
        input_output_aliases={i: 2 + i for i in range(2 * n)},
        compiler_params=pltpu.CompilerParams(has_side_effects=DATAFLOW),
    )(*[pltpu.with_memory_space_constraint(p, pltpu.HBM) for p in parts],
      *[pltpu.with_memory_space_constraint(lax.empty(p.shape, p.dtype), pltpu.HBM) for p in parts])
    return (name, outs[:-1]), outs[-1]


def _exchange_wait(handle, after):
    name, outs = handle
    n = (len(outs) - 2) // 2

    def body(*refs):
        p_refs, land_refs = refs[:n], refs[n:2 * n]
        send_sems, recv_sems = refs[2 * n], refs[2 * n + 1]
        x, y, c = _my_place()
        me = 4 * x + 2 * y + c
        for k, dev, peer in _peers(x, y, c):
            for mi in range(n):
                pltpu.make_async_remote_copy(
                    src_ref=p_refs[mi].at[peer], dst_ref=land_refs[mi].at[me], send_sem=send_sems.at[k * n + mi],
                    recv_sem=recv_sems.at[k * n + mi], device_id=dev, device_id_type=MESH).wait_send()
                slot = land_refs[mi].at[peer]
                pltpu.make_async_remote_copy(
                    src_ref=slot, dst_ref=slot, send_sem=send_sems.at[k * n + mi],
                    recv_sem=recv_sems.at[k * n + mi], device_id=dev, device_id_type=MESH).wait_recv()

    bufs = outs[2:]
    res = pl.pallas_call(
        body, name=name + "_wait", out_shape=tuple(pltpu.HBM(b.shape, b.dtype) for b in bufs),
        in_specs=[HBM] * (2 * n) + [SEM, SEM, ANY], out_specs=tuple([HBM] * (2 * n)),
        input_output_aliases={i: i for i in range(2 * n)},
        compiler_params=pltpu.CompilerParams(has_side_effects=DATAFLOW),
    )(*bufs, outs[0], outs[1], after)
    return list(res[n:])


def _swap_halves(half_in, gvec):
    def body(g_ref, gv_ref, out_ref, rg_ref, send_sems, recv_sems):
        x, y, c = _my_place()
        me = 4 * x + 2 * y + c
        sibling = (x, y, 1 - c)

        def half(hc):
            return out_ref.at[:, pl.ds(pl.multiple_of(hc * 512, 128), 512)]

        sends = [pltpu.make_async_remote_copy(src_ref=g_ref, dst_ref=half(c), send_sem=send_sems.at[7],
                                              recv_sem=recv_sems.at[7], device_id=sibling, device_id_type=MESH)]
        for k, dev, peer in _peers(x, y, c):
            sends.append(pltpu.make_async_remote_copy(src_ref=gv_ref, dst_ref=rg_ref.at[me], send_sem=send_sems.at[k],
                                                      recv_sem=recv_sems.at[k], device_id=dev, device_id_type=MESH))
        for cp in sends:
            cp.start()
        got = half(1 - c)
        pltpu.make_async_remote_copy(src_ref=got, dst_ref=got, send_sem=send_sems.at[7], recv_sem=recv_sems.at[7],
                                     device_id=sibling, device_id_type=MESH).wait_recv()
        for k, dev, peer in _peers(x, y, c):
            got = rg_ref.at[peer]
            pltpu.make_async_remote_copy(src_ref=got, dst_ref=got, send_sem=send_sems.at[k], recv_sem=recv_sems.at[k],
                                         device_id=dev, device_id_type=MESH).wait_recv()
        for cp in sends:
            cp.wait_send()

    return pl.pallas_call(
        body, name="swap_halves",
        in_specs=[ANY, ANY], out_specs=[ANY, ANY],
        out_shape=[jax.ShapeDtypeStruct(SHARD_SHAPES[0], F32), jax.ShapeDtypeStruct((8, 8, N_GVEC), F32)],
        scratch_shapes=[pltpu.SemaphoreType.DMA((8,)), pltpu.SemaphoreType.DMA((8,))],
    )(half_in, gvec)


def _set_slot(arr, block, idx):
    return lax.dynamic_update_slice(arr, block[None], (idx,) + (0,) * block.ndim)


PAD_RUNS = ((6304, 8352, 0), (5280, 6304, COL_Z), (672, 5280, COL_QKV), (0, 640, COL_LAT), (640, 672, COL_LAT + 704))
N_QKV = COL_LAT - COL_QKV


def _qkv_rows_regroup(a, to_padded):
    if to_padded:
        a4 = a.reshape(3, 12, 128, a.shape[1])
        return jnp.stack([a4[0], a4[1], a4[2]], axis=1).reshape(a.shape)
    a4 = a.reshape(12, 3, 128, a.shape[1])
    return jnp.concatenate([a4[:, tq].reshape(N_QKV // 3, a.shape[1]) for tq in range(3)], axis=0)
W_IN_SHARD = 2088


def _full_weights(gathered):
    def cols(a):
        return jnp.concatenate([a[s] for s in range(4)], axis=1)

    w_uq, w_ukv, w_pm, w_pd = [cols(a) for a in gathered[1:5]]
    w_out = gathered[5].reshape(D_MODEL, D_MODEL)
    w_in_t = gathered[0].reshape(4 * W_IN_SHARD, D_MODEL)
    pieces, at = [], 0
    for lo, hi, pad_lo in sorted(PAD_RUNS, key=lambda t: t[2]):
        if pad_lo > at:
            pieces.append(jnp.zeros((pad_lo - at, D_MODEL), w_in_t.dtype))
        pieces.append(_qkv_rows_regroup(w_in_t[lo:hi], True) if pad_lo == COL_QKV else w_in_t[lo:hi])
        at = pad_lo + hi - lo
    pieces.append(jnp.zeros((N_PAD - at, D_MODEL), w_in_t.dtype))
    w_pad_t = jnp.concatenate(pieces, axis=0)
    z32 = jnp.zeros((Q_RANK, 32), w_uq.dtype)
    wuq_pad = jnp.concatenate([t for h in range(MLA_HEADS) for t in (w_uq[:, h * 96:(h + 1) * 96], z32)], axis=1)
    z64 = jnp.zeros((KV_RANK, 64), w_ukv.dtype)
    wk_pad = jnp.concatenate([t for h in range(MLA_HEADS) for t in (w_ukv[:, h * 128:h * 128 + 64], z64)], axis=1)
    wv = jnp.concatenate([w_ukv[:, h * 128 + 64:(h + 1) * 128] for h in range(MLA_HEADS)], axis=1)
    return w_pad_t, wuq_pad, wk_pad, wv, w_pm, w_pd, w_out


W_IN_LAT = 672


def _grad_parts_in_early(dwt_early):
    dwt_early = jnp.concatenate([dwt_early[:COL_QKV], _qkv_rows_regroup(dwt_early[COL_QKV:COL_LAT], False)], axis=0)

    def in_block(s, h):
        cols = slice(h * 512, (h + 1) * 512)
        out = []
        for lo, hi, pad_lo in sorted(PAD_RUNS):
            a_, b_ = max(lo, s * W_IN_SHARD), min(hi, (s + 1) * W_IN_SHARD)
            if a_ < b_:
                out.append(jnp.zeros((b_ - a_, 512), dwt_early.dtype) if pad_lo >= COL_LAT
                           else dwt_early[pad_lo + a_ - lo:pad_lo + b_ - lo, cols])
        return jnp.concatenate(out, axis=0)

    return jnp.stack([in_block(s, h) for s in range(4) for h in range(2)])


def _grad_parts_in_late(dwt_late):
    rows = jnp.concatenate([dwt_late[0:640], dwt_late[704:736]], axis=0)
    zero = jnp.zeros((W_IN_LAT, 512), dwt_late.dtype)
    return jnp.stack([rows[:, 0:512], rows[:, 512:1024]] + [zero] * 6)


def _shard_blocks(m, axis=1):
    n = m.shape[axis] // 4
    cut = (lambda s: m[:, s * n:(s + 1) * n]) if axis == 1 else (lambda s: m[s * n:(s + 1) * n])
    return jnp.stack([cut(s) for s in range(4) for _ in range(2)])


def _grad_parts_mla(dwuq_pad, dwk_pad, dwv):
    d_uq = jnp.concatenate([dwuq_pad[:, h * 128:h * 128 + 96] for h in range(MLA_HEADS)], axis=1)
    d_ukv = jnp.concatenate([t for h in range(MLA_HEADS) for t in (dwk_pad[:, h * 128:h * 128 + 64], dwv[:, h * 64:(h + 1) * 64])],
                            axis=1)
    return [_shard_blocks(d_uq.astype(BF16)), _shard_blocks(d_ukv.astype(BF16))]


def _rope_tables(positions, token=None):
    pos = positions.reshape(SEQ).astype(F32)
    if token is not None:
        pos = pos + token[0, 0]
    lane = jnp.arange(128)

    def table(rot, first, period):
        inv = ROPE_THETA ** (-jnp.arange(0, rot, 2, dtype=F32) / rot)
        half = rot // 2
        off = lane % period - first
        in1, in2 = (off >= 0) & (off < half), (off >= half) & (off < rot)
        inv_lane = jnp.where(in1 | in2, inv[jnp.clip(off % half, 0, half - 1)], 0.0)
        sign = jnp.where(in1, -1.0, 1.0).astype(F32)
        ang = pos[:, None] * inv_lane[None, :]
        return jnp.cos(ang), jnp.sin(ang) * sign[None, :]

    return table(32, 64, 128), table(16, 0, 64)


class _Links:
    def __init__(self, mats, chip, me):
        landing = [_set_slot(lax.empty((4,) + m.shape, m.dtype), m, chip) for m in mats]
        self.gather, self.token = _gather_start(landing)
        self.me, self.sent, self.handles, self.sums, self.raw = me, {}, {}, {}, {}

    def weights(self, after):
        return _relay_share(_gather_wait(self.gather, after))

    def send(self, blocks, name):
        self.sent[name] = blocks
        self.handles[name], token = _exchange_start(blocks, name)
        return token

    def collect(self, name, after, parts):
        recv = _exchange_wait(self.handles[name], after)
        for r, own, part in zip(recv, self.sent[name], parts):
            if part.startswith("in_"):
                self.sums[part] = _sum_parts(r, own, self.me, 64, "sum_grad_" + part)
            else:
                self.raw[part] = (r, own)
        return tuple(self.sums[part] for part in parts if part in self.sums)


def _device_grads(x, positions, target, gains, links):
    pre_g, q_g, kv_g, post_g = gains
    (mc, ms), (dc, ds) = _rope_tables(positions, links.token)
    h = _prenorm_fwd(x, pre_g, links.token)
    w_pad_t, wuq_pad, wk_pad, wv, w_pm, w_pd, w_out = _full_weights(links.weights((h, mc, ms, dc, ds)))

    p_gz = _matmul(h, w_pad_t, "nt", F32, 1024, 1536, 1024, "in_proj_gates", b_cols=(0, COL_QKV // 1536))
    p_qkv = _matmul(h, w_pad_t, "nt", BF16, 1024, 1536, 1024, "in_proj_dilated", b_cols=(COL_QKV // 1536, N_QKV // 1536),
                    lane_blocks=True)
    p_lat = _matmul(h, w_pad_t, "nt", F32, 1024, N_LAT, 1024, "in_proj_latent", b_cols=(COL_LAT // N_LAT, 1))
    q, k, v = _mla_prep_fwd(p_lat, q_g, kv_g, wuq_pad, wk_pad, wv, mc, ms)
    ya, lse_m = _mla_flash_fwd(q, k, v)
    o_g, l_g, qkv = zip(*[_dil_attn_fwd(p_qkv, dc, ds, g) for g in range(3)])
    (dp, dy, dya, dyd, yd, lse_d, loss_cols, dg_post, dwpm, dwpd, dwout) = _tail(
        p_gz, ya, o_g, l_g, x, target, w_pm, w_pd, w_out, post_g)

    for g in range(3):
        dp = _dil_attn_bwd(dp, qkv[g], dyd, yd, lse_d, dc, ds, g)
    dw_early = _matmul(dp, h, "tn", BF16, 1536, 1024, 2048, "dw_in_early", a_cols=(0, COL_LAT // 1536))
    token = links.send([_grad_parts_in_early(dw_early), _shard_blocks(dwpm.astype(BF16)), _shard_blocks(dwpd.astype(BF16)),
                        _shard_blocks(dwout.astype(BF16), axis=0)], "exchange_early")

    dq, dk, dv = _mla_flash_bwd(q, k, v, ya, dya, lse_m, token)
    dp, dwuq_pad, dwk_pad, dwv, dg_q, dg_kv = _mla_prep_bwd(dp, p_lat, dq, dk, dv, q_g, kv_g, wuq_pad, wk_pad, wv, mc, ms)
    dw_late = _matmul(dp, h, "tn", BF16, N_LAT, 1024, 2048, "dw_in_late", a_cols=(COL_LAT // N_LAT, 1))
    token = links.send([_grad_parts_in_late(dw_late)] + _grad_parts_mla(dwuq_pad, dwk_pad, dwv), "exchange_late")
    early = links.collect("exchange_early", dw_late, ("in_early", "pm", "pd", "out"))

    grad_x, dg_pre = _dh_prenorm_bwd(dp, w_pad_t, x, dy, pre_g, (token,) + tuple(early))
    links.collect("exchange_late", grad_x, ("in_late", "uq", "ukv"))

    loss_part = jnp.pad((jnp.sum(loss_cols) * (0.5 / D_MODEL)).reshape(1, 1), ((0, 0), (0, N_GVEC - N_GAINS - 1)))
    gvec = jnp.concatenate([dg_pre, dg_q, dg_kv, dg_post, loss_part], axis=1)
    return grad_x, gvec


def kernel(x, positions, pre_norm_g, w_in, q_norm_g, w_uq, kv_norm_g, w_ukv, w_proj_mla, w_proj_dil, w_out, post_norm_g, loss_target, m_pre_norm_g, m_w_in, m_q_norm_g, m_w_uq, m_kv_norm_g, m_w_ukv, m_w_proj_mla, m_w_proj_dil, m_w_out, m_post_norm_g, v_pre_norm_g, v_w_in, v_q_norm_g, v_w_uq, v_kv_norm_g, v_w_ukv, v_w_proj_mla, v_w_proj_dil, v_w_out, v_post_norm_g):
    xi, yi, ci = _my_place()
    chip, me = 2 * xi + yi, 4 * xi + 2 * yi + ci
    mats = [jnp.swapaxes(w_in, 1, 2)] + [w_uq, w_ukv, w_proj_mla, w_proj_dil, w_out]
    mats = [w.reshape(w.shape[1:]).astype(BF16) for w in mats]
    links = _Links(mats, chip, me)
    gains = (pre_norm_g, q_norm_g, kv_norm_g, post_norm_g)
    grad_x, gvec = _device_grads(x[0], positions, loss_target[0], gains, links)

    sums = links.sums
    in_e = sums["in_early"]
    half_in = jnp.concatenate([in_e[:W_IN_LAT] + jnp.where(chip == 0, sums["in_late"], 0.0), in_e[W_IN_LAT:]], axis=0)
    gvec8 = jnp.pad(gvec, ((0, 7), (0, 0)))
    swapped_in, recv_gains = _swap_halves(half_in, gvec8)
    g_gains = _sum_parts(recv_gains, gvec8, me, 8, "sum_gain_parts")[0:1]
    loss = g_gains[0, N_GAINS]
    sw = lambda a: jnp.swapaxes(a, 1, 2)
    d_in, m_in, v_in, g_in = [sw(o) for o in _adamw_in(sw(w_in), sw(m_w_in), sw(v_w_in), half_in, swapped_in, ci)]
    off = [0, 1024, 1408, 1664, 2688]
    g_gain = [g_gains[:, off[i]:off[i + 1]] for i in range(4)]
    ws = [pre_norm_g, w_in, q_norm_g, w_uq, kv_norm_g, w_ukv, w_proj_mla, w_proj_dil, w_out, post_norm_g]
    ms = [m_pre_norm_g, m_w_in, m_q_norm_g, m_w_uq, m_kv_norm_g, m_w_ukv, m_w_proj_mla, m_w_proj_dil, m_w_out, m_post_norm_g]
    vs = [v_pre_norm_g, v_w_in, v_q_norm_g, v_w_uq, v_kv_norm_g, v_w_ukv, v_w_proj_mla, v_w_proj_dil, v_w_out, v_post_norm_g]
    part_of = [None, "in", None, "uq", None, "ukv", "pm", "pd", "out", None]
    gain_of = iter(g_gain)
    grads, deltas, new_m, new_v = [], [], [], []
    for i, (w, m, v, part) in enumerate(zip(ws, ms, vs, part_of)):
        if part == "in":
            d_, m_, v_, g = d_in, m_in, v_in, g_in
        elif part is not None:
            d_, m_, v_, g = _adamw_recv(w, m, v, *links.raw[part], me, f"adamw_{i}")
        else:
            g = next(gain_of)
            d_, m_, v_ = _adamw(w, g, m, v, f"adamw_{i}")
        grads.append(g)
        deltas.append(d_)
        new_m.append(m_)
        new_v.append(v_)
    return (loss, grad_x.reshape(x.shape), *grads, *deltas, *new_m, *new_v)
```

```python
import jax
import jax.numpy as jnp
from jax import lax
from jax.experimental import pallas as pl
from jax.experimental.pallas import tpu as pltpu

F32 = jnp.float32
BF16 = jnp.bfloat16

SEQ = 4096
D_MODEL = 1024
EPS = 1e-6
ROPE_THETA = 500000.0
MLA_HEADS = 8
Q_RANK = 384
KV_RANK = 256
MLA_SCALE = 96.0 ** -0.5
MLA_ROPE_HALF = 16
DIL_DILATIONS = (1, 4, 16)
DIL_ROPE_HALF = 8
DIL_SCALE = 0.125
BAND = 128

N_LAT = 768
COL_Z, COL_QKV, COL_LAT = 2048, 3072, 7680
N_PAD = 8448


def _qkv_block(tq, g, pr):
    return COL_QKV // 128 + (g * 4 + pr) * 3 + tq

IN_SPLITS = (384, 256, 32, 4608, 512, 512, 1024, 1024)

SHARD_SHAPES = ((2088, 1024), (384, 192), (256, 256), (512, 256), (512, 256), (256, 1024))
N_MATS = len(SHARD_SHAPES)
N_GAINS = 2688
N_GVEC = N_GAINS + 128

ADAM_LR, ADAM_B1, ADAM_B2, ADAM_EPS, ADAM_WD, ADAM_STEP = 0.001, 0.9, 0.999, 1e-08, 0.01, 10

VMEM_LIMIT = 56 * 1024 * 1024
NEG = -1e30
MESH = pl.DeviceIdType.MESH


def _cparams(**kw):
    return pltpu.CompilerParams(vmem_limit_bytes=VMEM_LIMIT, **kw)


def _dot(a, b, dims):
    return lax.dot_general(a, b, (dims, ((), ())), preferred_element_type=F32)


def _nn(a, b):
    return _dot(a, b, ((1,), (0,)))


def _nt(a, b):
    return _dot(a, b, ((1,), (1,)))


def _tn(a, b):
    return _dot(a, b, ((0,), (0,)))


def _rope_lanes(shape, half, period, first):
    lane = lax.broadcasted_iota(jnp.int32, shape, len(shape) - 1) % period
    return (lane >= first) & (lane < first + half), (lane >= first + half) & (lane < first + 2 * half)


def _rope_fwd(x, c, s, half, lanes):
    x1, _ = lanes
    return x * c + jnp.where(x1, pltpu.roll(x, 128 - half, 1), pltpu.roll(x, half, 1)) * s


def _rope_bwd(g, c, s, half, lanes):
    x1, x2 = lanes
    gs = g * s
    return g * c + jnp.where(x2, pltpu.roll(gs, half, 1), jnp.where(x1, pltpu.roll(gs, 128 - half, 1), 0.0))


def _sigmoid(x):
    return 1.0 / (1.0 + jnp.exp(-x))


def _after(token):
    tokens = [t for t in (token if isinstance(token, (tuple, list)) else [token]) if t is not None]
    return tokens, [pl.BlockSpec(memory_space=pl.ANY)] * len(tokens)


def _matmul(a, b, mode, out_dtype, tm, tn, tk, name, token=None, b_cols=None, a_cols=None, lane_blocks=False):
    after, after_specs = _after(token)
    if mode == "nn":
        (m, k), n = a.shape, b.shape[1]
        first = 0
        if b_cols is not None:
            first, n = b_cols[0], b_cols[1] * tn
        a_spec = pl.BlockSpec((tm, tk), lambda j, i, kk: (i, kk))
        b_spec = pl.BlockSpec((tk, tn), lambda j, i, kk: (kk, j + first))
        dot = _nn
    elif mode == "nt":
        (m, k), n = a.shape, b.shape[0]
        first = 0
        if b_cols is not None:
            first, n = b_cols[0], b_cols[1] * tn
        a_spec = pl.BlockSpec((tm, tk), lambda j, i, kk: (i, kk))
        b_spec = pl.BlockSpec((tn, tk), lambda j, i, kk: (j + first, kk))
        dot = _nt
    else:
        (k, m), n = a.shape, b.shape[1]
        first = 0
        if a_cols is not None:
            first, m = a_cols[0], a_cols[1] * tm
        a_spec = pl.BlockSpec((tk, tm), lambda j, i, kk: (kk, i + first))
        b_spec = pl.BlockSpec((tk, tn), lambda j, i, kk: (kk, j))
        dot = _tn
    assert m % tm == 0 and n % tn == 0 and k % tk == 0, (name, m, n, k, tm, tn, tk)
    nk = k // tk

    def write(o_ref, val):
        if lane_blocks:
            for blk in range(tn // 128):
                o_ref[blk] = val[:, blk * 128:(blk + 1) * 128].astype(o_ref.dtype)
        else:
            o_ref[...] = val.astype(o_ref.dtype)

    def body_single(a_ref, b_ref, *rest):
        write(rest[-1], dot(a_ref[...], b_ref[...]))

    def body_accumulate(a_ref, b_ref, *rest):
        o_ref, acc_ref = rest[-2:]
        kk = pl.program_id(2)
        part = dot(a_ref[...], b_ref[...])

        @pl.when(kk == 0)
        def _():
            acc_ref[...] = part

        @pl.when(kk > 0)
        def _():
            acc_ref[...] += part

        @pl.when(kk == nk - 1)
        def _():
            write(o_ref, acc_ref[...])

    if lane_blocks:
        out_spec = pl.BlockSpec((tn // 128, tm, 128), lambda j, i, kk: (j, i, 0))
        out_shape = jax.ShapeDtypeStruct((n // 128, m, 128), out_dtype)
    else:
        out_spec = pl.BlockSpec((tm, tn), lambda j, i, kk: (i, j))
        out_shape = jax.ShapeDtypeStruct((m, n), out_dtype)
    return pl.pallas_call(
        body_single if nk == 1 else body_accumulate, name=name, grid=(n // tn, m // tm, nk),
        in_specs=[a_spec, b_spec] + after_specs,
        out_specs=out_spec, out_shape=out_shape,
        scratch_shapes=[] if nk == 1 else [pltpu.VMEM((tm, tn), F32)],
        compiler_params=_cparams(),
    )(a, b, *after)


def _prenorm_fwd(x, g, token=None):
    tm = 512
    after, after_specs = _after(token)

    def body(x_ref, g_ref, *rest):
        xv = x_ref[...]
        r = lax.rsqrt(jnp.mean(xv * xv, axis=-1, keepdims=True) + EPS)
        rest[-1][...] = (xv * r * g_ref[...]).astype(BF16)

    return pl.pallas_call(
        body, name="prenorm_fwd", grid=(SEQ // tm,),
        in_specs=[pl.BlockSpec((tm, D_MODEL), lambda i: (i, 0)), pl.BlockSpec((1, D_MODEL), lambda i: (0, 0))] + after_specs,
        out_specs=pl.BlockSpec((tm, D_MODEL), lambda i: (i, 0)),
        out_shape=jax.ShapeDtypeStruct((SEQ, D_MODEL), BF16),
    )(x, g, *after)


def _dh_prenorm_bwd(dp, w_pad_t, x, dy, g, token=None):
    tm, tk = 1024, 1408
    nk = N_PAD // tk
    after, after_specs = _after(token)

    def body(a_ref, b_ref, x_ref, dy_ref, g_ref, *rest):
        gx_ref, dg_ref, acc_ref = rest[-3:]
        i, kk = pl.program_id(0), pl.program_id(1)
        part = _nn(a_ref[...], b_ref[...])

        @pl.when(kk == 0)
        def _():
            acc_ref[...] = part

        @pl.when(kk > 0)
        def _():
            acc_ref[...] += part

        @pl.when(kk == nk - 1)
        def _():
            xv = x_ref[...]
            r = lax.rsqrt(jnp.mean(xv * xv, axis=-1, keepdims=True) + EPS)
            n = xv * r
            dhv = acc_ref[...]
            dn = dhv * g_ref[...]
            gx_ref[...] = dy_ref[...] + r * (dn - n * jnp.mean(dn * n, axis=-1, keepdims=True))
            cols = jnp.sum(dhv * n, axis=0, keepdims=True)

            @pl.when(i == 0)
            def _():
                dg_ref[...] = cols

            @pl.when(i > 0)
            def _():
                dg_ref[...] += cols

    row = pl.BlockSpec((tm, D_MODEL), lambda i, kk: (i, 0))
    vec = pl.BlockSpec((1, D_MODEL), lambda i, kk: (0, 0))
    return pl.pallas_call(
        body, name="dh_prenorm_bwd", grid=(SEQ // tm, nk),
        in_specs=[pl.BlockSpec((tm, tk), lambda i, kk: (i, kk)), pl.BlockSpec((tk, D_MODEL), lambda i, kk: (kk, 0)),
                  row, row, vec] + after_specs,
        out_specs=[row, vec],
        out_shape=[jax.ShapeDtypeStruct((SEQ, D_MODEL), F32), jax.ShapeDtypeStruct((1, D_MODEL), F32)],
        scratch_shapes=[pltpu.VMEM((tm, D_MODEL), F32)],
        compiler_params=_cparams(),
    )(dp, w_pad_t, x, dy, g, *after)


def _mla_prep_fwd(p, qg, kvg, wuq, wk, wv, rc, rs):
    tm = 512

    def body(lat_ref, qg_ref, kvg_ref, wuq_ref, wk_ref, wv_ref, c_ref, s_ref, q_ref, k_ref, v_ref):
        c, s = c_ref[...], s_ref[...]
        lanes = _rope_lanes((tm, 128), MLA_ROPE_HALF, 128, 64)
        cq = lat_ref[:, 0:Q_RANK]
        r1 = lax.rsqrt(jnp.mean(cq * cq, axis=-1, keepdims=True) + EPS)
        cqn = (cq * r1 * qg_ref[...]).astype(BF16)
        q = _nn(cqn, wuq_ref[...])
        for h in range(MLA_HEADS):
            sl = slice(h * 128, (h + 1) * 128)
            q_ref[:, sl] = (_rope_fwd(q[:, sl], c, s, MLA_ROPE_HALF, lanes) * MLA_SCALE).astype(BF16)
        ckv = lat_ref[:, Q_RANK:Q_RANK + KV_RANK]
        r2 = lax.rsqrt(jnp.mean(ckv * ckv, axis=-1, keepdims=True) + EPS)
        ckvn = (ckv * r2 * kvg_ref[...]).astype(BF16)
        krr = _rope_fwd(lat_ref[:, Q_RANK + KV_RANK:N_LAT], c, s, MLA_ROPE_HALF, lanes)
        kn = _nn(ckvn, wk_ref[...])
        for h in range(MLA_HEADS):
            sl = slice(h * 128, (h + 1) * 128)
            k_ref[:, sl] = (kn[:, sl] + krr).astype(BF16)
        v_ref[...] = _nn(ckvn, wv_ref[...]).astype(BF16)

    def full(shape):
        return pl.BlockSpec(shape, lambda i: (0, 0))

    def rows(w):
        return pl.BlockSpec((tm, w), lambda i: (i, 0))

    return pl.pallas_call(
        body, name="mla_prep_fwd", grid=(SEQ // tm,),
        in_specs=[pl.BlockSpec((tm, N_LAT), lambda i: (i, 0)),
                  full((1, Q_RANK)), full((1, KV_RANK)), full((Q_RANK, 1024)), full((KV_RANK, 1024)),
                  full((KV_RANK, 512)), rows(128), rows(128)],
        out_specs=[rows(1024), rows(1024), rows(512)],
        out_shape=[jax.ShapeDtypeStruct((SEQ, 1024), BF16), jax.ShapeDtypeStruct((SEQ, 1024), BF16),
                   jax.ShapeDtypeStruct((SEQ, 512), BF16)],
        compiler_params=_cparams(),
    )(p, qg, kvg, wuq, wk, wv, rc, rs)


def _mla_prep_bwd(dp_in, p, dq, dk, dv, qg, kvg, wuq, wk, wv, rc, rs):
    tm = 512

    def body(dp_any, lat_ref, dq_ref, dk_ref, dv_ref, qg_ref, kvg_ref, wuq_ref, wk_ref, wv_ref,
             c_ref, s_ref, dp_ref, dwuq_ref, dwk_ref, dwv_ref, dgq_ref, dgkv_ref, dqb_ref, dkb_ref):
        del dp_any
        c, s = c_ref[...], s_ref[...]
        lanes = _rope_lanes((tm, 128), MLA_ROPE_HALF, 128, 64)
        lane = lax.broadcasted_iota(jnp.int32, (tm, 128), 1)
        dkr = jnp.zeros((tm, 128), F32)
        for h in range(MLA_HEADS):
            sl = slice(h * 128, (h + 1) * 128)
            dqb_ref[:, sl] = _rope_bwd(dq_ref[:, sl] * MLA_SCALE, c, s, MLA_ROPE_HALF, lanes).astype(BF16)
            dkh = dk_ref[:, sl]
            dkr = dkr + dkh
            dkb_ref[:, sl] = jnp.where(lane < 64, dkh, 0.0).astype(BF16)
        dkr = jnp.where((lane >= 64) & (lane < 96), dkr, 0.0)
        dkr = _rope_bwd(dkr, c, s, MLA_ROPE_HALF, lanes)
        dvb = dv_ref[...].astype(BF16)

        cq = lat_ref[:, 0:Q_RANK]
        r1 = lax.rsqrt(jnp.mean(cq * cq, axis=-1, keepdims=True) + EPS)
        n1 = cq * r1
        dcqn = _nt(dqb_ref[...], wuq_ref[...])
        dn1 = dcqn * qg_ref[...]
        dcq = r1 * (dn1 - n1 * jnp.mean(dn1 * n1, axis=-1, keepdims=True))
        pq = jnp.sum(dcqn * n1, axis=0, keepdims=True)

        ckv = lat_ref[:, Q_RANK:Q_RANK + KV_RANK]
        r2 = lax.rsqrt(jnp.mean(ckv * ckv, axis=-1, keepdims=True) + EPS)
        n2 = ckv * r2
        dckvn = _nt(dkb_ref[...], wk_ref[...]) + _nt(dvb, wv_ref[...])
        dn2 = dckvn * kvg_ref[...]
        dckv = r2 * (dn2 - n2 * jnp.mean(dn2 * n2, axis=-1, keepdims=True))
        pkv = jnp.sum(dckvn * n2, axis=0, keepdims=True)
        cqn = (n1 * qg_ref[...]).astype(BF16)
        ckvn = (n2 * kvg_ref[...]).astype(BF16)
        wq, wk_, wv_ = _tn(cqn, dqb_ref[...]), _tn(ckvn, dkb_ref[...]), _tn(ckvn, dvb)

        dp_ref[:, 0:Q_RANK] = dcq.astype(BF16)
        dp_ref[:, Q_RANK:Q_RANK + KV_RANK] = dckv.astype(BF16)
        dp_ref[:, Q_RANK + KV_RANK:N_LAT] = dkr.astype(BF16)

        @pl.when(pl.program_id(0) == 0)
        def _():
            dgq_ref[...] = pq
            dgkv_ref[...] = pkv
            dwuq_ref[...] = wq
            dwk_ref[...] = wk_
            dwv_ref[...] = wv_

        @pl.when(pl.program_id(0) > 0)
        def _():
            dgq_ref[...] += pq
            dgkv_ref[...] += pkv
            dwuq_ref[...] += wq
            dwk_ref[...] += wk_
            dwv_ref[...] += wv_

    def full(shape):
        return pl.BlockSpec(shape, lambda i: (0, 0))

    def rows(w):
        return pl.BlockSpec((tm, w), lambda i: (i, 0))

    lat = pl.BlockSpec((tm, N_LAT), lambda i: (i, 0))
    dlat = pl.BlockSpec((tm, N_LAT), lambda i: (i, COL_LAT // N_LAT))
    return pl.pallas_call(
        body, name="mla_prep_bwd", grid=(SEQ // tm,),
        in_specs=[pl.BlockSpec(memory_space=pl.ANY), lat, rows(1024), rows(1024), rows(512),
                  full((1, Q_RANK)), full((1, KV_RANK)), full((Q_RANK, 1024)), full((KV_RANK, 1024)),
                  full((KV_RANK, 512)), rows(128), rows(128)],
        out_specs=[dlat, full((Q_RANK, 1024)), full((KV_RANK, 1024)), full((KV_RANK, 512)),
                   full((1, Q_RANK)), full((1, KV_RANK))],
        out_shape=[jax.ShapeDtypeStruct((SEQ, N_PAD), BF16), jax.ShapeDtypeStruct((Q_RANK, 1024), F32),
                   jax.ShapeDtypeStruct((KV_RANK, 1024), F32), jax.ShapeDtypeStruct((KV_RANK, 512), F32),
                   jax.ShapeDtypeStruct((1, Q_RANK), F32), jax.ShapeDtypeStruct((1, KV_RANK), F32)],
        input_output_aliases={0: 0},
        scratch_shapes=[pltpu.VMEM((tm, 1024), BF16), pltpu.VMEM((tm, 1024), BF16)],
        compiler_params=_cparams(),
    )(dp_in, p, dq, dk, dv, qg, kvg, wuq, wk, wv, rc, rs)


FLASH_T = 1024


def _head_half(shape, hh):
    lane = lax.broadcasted_iota(jnp.int32, shape, 1)
    return (lane < 64) if hh == 0 else (lane >= 64)


def _diag_keep(nr, nk):
    row = lax.broadcasted_iota(jnp.int32, (nr, nk), 0)
    col = lax.broadcasted_iota(jnp.int32, (nr, nk), 1)
    return row + (nk - nr) >= col


def _tri_steps(nb, q_major):
    if q_major:
        pairs = [(i, kb) for i in range(nb) for kb in range(i + 1)]
    else:
        pairs = [(i, kb) for kb in range(nb) for i in range(kb, nb)]
    return jnp.asarray([p[0] for p in pairs], jnp.int32), jnp.asarray([p[1] for p in pairs], jnp.int32)


def _mla_flash_fwd(q, k, v):
    t = FLASH_T
    nb = SEQ // t
    qtab, ktab = _tri_steps(nb, True)

    def body(qi_ref, ki_ref, q_ref, k_ref, v_ref, o_ref, lse_ref, m_scr, l_scr, acc_scr):
        step = pl.program_id(1)
        i, kb = qi_ref[step], ki_ref[step]

        @pl.when(kb == 0)
        def _():
            m_scr[...] = jnp.full_like(m_scr, NEG)
            l_scr[...] = jnp.zeros_like(l_scr)
            acc_scr[...] = jnp.zeros_like(acc_scr)

        def update(r0, nr, nk, diagonal):
            rs = slice(r0, r0 + nr)
            vv = v_ref[0:nk, :]
            for hh in range(2):
                sl = slice(hh * 128, (hh + 1) * 128)
                s = _nt(q_ref[rs, sl], k_ref[0:nk, sl])
                if diagonal:
                    s = jnp.where(_diag_keep(nr, nk), s, NEG)
                m_prev = m_scr[hh, rs, :]
                m_new = jnp.maximum(m_prev, jnp.max(s, axis=-1, keepdims=True))
                pr = jnp.exp(s - jnp.tile(m_new, (1, nk // 128)))
                alpha = jnp.exp(m_prev - m_new)
                l_scr[hh, rs, :] = alpha * l_scr[hh, rs, :] + jnp.sum(pr, axis=-1, keepdims=True)
                acc_scr[hh, rs, :] = alpha * acc_scr[hh, rs, :] + _nn(pr.astype(BF16), vv)
                m_scr[hh, rs, :] = m_new

        @pl.when(kb < i)
        def _():
            update(0, t, t, False)

        @pl.when(kb == i)
        def _():
            update(0, t // 2, t // 2, True)
            update(t // 2, t // 2, t, True)
            o0 = acc_scr[0] / l_scr[0]
            o1 = acc_scr[1] / l_scr[1]
            o_ref[...] = jnp.where(_head_half((t, 128), 0), o0, o1)
            for hh in range(2):
                lse_ref[:, hh * 128:(hh + 1) * 128] = m_scr[hh] + jnp.log(l_scr[hh])

    grid_spec = pltpu.PrefetchScalarGridSpec(
        num_scalar_prefetch=2, grid=(4, qtab.shape[0]),
        in_specs=[pl.BlockSpec((t, 256), lambda j, s, qi, ki: (qi[s], j)),
                  pl.BlockSpec((t, 256), lambda j, s, qi, ki: (ki[s], j)),
                  pl.BlockSpec((t, 128), lambda j, s, qi, ki: (ki[s], j))],
        out_specs=[pl.BlockSpec((t, 128), lambda j, s, qi, ki: (qi[s], j)),
                   pl.BlockSpec((t, 256), lambda j, s, qi, ki: (qi[s], j))],
        scratch_shapes=[pltpu.VMEM((2, t, 128), F32), pltpu.VMEM((2, t, 128), F32), pltpu.VMEM((2, t, 128), F32)])
    return pl.pallas_call(
        body, name="mla_flash_fwd", grid_spec=grid_spec,
        out_shape=[jax.ShapeDtypeStruct((SEQ, 512), F32), jax.ShapeDtypeStruct((SEQ, 1024), F32)],
        compiler_params=_cparams(),
    )(qtab, ktab, q, k, v)


def _mla_flash_bwd(q, k, v, o, do, lse, token=None):
    t = FLASH_T
    nb = SEQ // t
    qtab, ktab = _tri_steps(nb, False)
    after, after_specs = _after(token)

    def body(qi_ref, ki_ref, q_ref, k_ref, v_ref, o_ref, do_ref, lse_ref, *rest):
        dq_ref, dk_ref, dv_ref, dk_scr, dv_scr = rest[-5:]
        step = pl.program_id(1)
        i, kb = qi_ref[step], ki_ref[step]

        @pl.when(step == 0)
        def _():
            dq_ref[...] = jnp.zeros_like(dq_ref)

        @pl.when(i == kb)
        def _():
            dk_scr[...] = jnp.zeros_like(dk_scr)
            dv_scr[...] = jnp.zeros_like(dv_scr)

        def update(r0, nr, nk, diagonal):
            rs = slice(r0, r0 + nr)
            vv = v_ref[0:nk, :]
            ov = o_ref[rs, :]
            dov = do_ref[rs, :]
            rows = pl.ds(pl.multiple_of(i * t + r0, t // 2), nr)
            for hh in range(2):
                sl = slice(hh * 128, (hh + 1) * 128)
                qh, kh = q_ref[rs, sl], k_ref[0:nk, sl]
                s = _nt(qh, kh)
                if diagonal:
                    s = jnp.where(_diag_keep(nr, nk), s, NEG)
                pr = jnp.exp(s - jnp.tile(lse_ref[rs, sl], (1, nk // 128)))
                dom = jnp.where(_head_half((nr, 128), hh), dov, 0.0)
                domb = dom.astype(BF16)
                dv_scr[0:nk, :] += _tn(pr.astype(BF16), domb)
                dpr = _nt(domb, vv)
                delta = jnp.sum(dom * ov, axis=-1, keepdims=True)
                ds = (pr * (dpr - delta)).astype(BF16)
                dq_ref[rows, sl] += _nn(ds, kh)
                dk_scr[hh, 0:nk, :] += _tn(ds, qh)

        @pl.when(i > kb)
        def _():
            update(0, t, t, False)

        @pl.when(i == kb)
        def _():
            update(0, t // 2, t // 2, True)
            update(t // 2, t // 2, t, True)

        @pl.when(i == nb - 1)
        def _():
            dk_ref[:, 0:128] = dk_scr[0]
            dk_ref[:, 128:256] = dk_scr[1]
            dv_ref[...] = dv_scr[...]

    qi_map = lambda j, s, qi, ki: (qi[s], j)
    ki_map = lambda j, s, qi, ki: (ki[s], j)
    grid_spec = pltpu.PrefetchScalarGridSpec(
        num_scalar_prefetch=2, grid=(4, qtab.shape[0]),
        in_specs=[pl.BlockSpec((t, 256), qi_map), pl.BlockSpec((t, 256), ki_map), pl.BlockSpec((t, 128), ki_map),
                  pl.BlockSpec((t, 128), qi_map), pl.BlockSpec((t, 128), qi_map), pl.BlockSpec((t, 256), qi_map)]
        + after_specs,
        out_specs=[pl.BlockSpec((SEQ, 256), lambda j, s, qi, ki: (0, j)), pl.BlockSpec((t, 256), ki_map),
                   pl.BlockSpec((t, 128), ki_map)],
        scratch_shapes=[pltpu.VMEM((2, t, 128), F32), pltpu.VMEM((t, 128), F32)])
    return pl.pallas_call(
        body, name="mla_flash_bwd", grid_spec=grid_spec,
        out_shape=[jax.ShapeDtypeStruct((SEQ, 1024), F32), jax.ShapeDtypeStruct((SEQ, 1024), F32),
                   jax.ShapeDtypeStruct((SEQ, 512), F32)],
        compiler_params=_cparams(),
    )(qtab, ktab, q, k, v, o, do, lse, *after)


def _strided(start, size, d):
    return pl.ds(start, size) if d == 1 else pl.ds(start, size, stride=d)


DIL_ST_FWD, DIL_ST_BWD = 4096, 2048


def _band_keep(g, b, t, nb):
    nbs = SEQ // DIL_DILATIONS[g] // BAND
    row = lax.broadcasted_iota(jnp.int32, (BAND, 2 * BAND), 0)
    col = lax.broadcasted_iota(jnp.int32, (BAND, 2 * BAND), 1)
    cur = (col >= BAND) & (row >= col - BAND)
    prev = (col < BAND) & (col >= row)
    if nbs >= nb:
        if b > 0:
            return cur | prev
        return cur | (prev & ((t * nb) % nbs != 0))
    return cur | prev if b % nbs else cur


def _dil_tok(g, b, t, nb):
    d = DIL_DILATIONS[g]
    nbs = SEQ // d // BAND
    gb = t * nb + b
    return _strided((gb % nbs) * BAND * d + gb // nbs, BAND, d)


def _dil_attn_fwd(p_qkv, rc, rs, g):
    d = DIL_DILATIONS[g]
    sub_len = SEQ // d
    ch = min(sub_len, 512)
    DIL_ST, DIL_NB = DIL_ST_FWD, DIL_ST_FWD // BAND

    def body(p_ref, c_ref, sn_ref, o_ref, l_ref, qkv_ref, x_scr, s_scr, p_scr, o_scr):
        t = pl.program_id(1)

        @pl.when(t == 0)
        def _():
            lanes = _rope_lanes((ch, 128), DIL_ROPE_HALF, 64, 0)
            for tq in range(3):
                qkv_ref[tq, 0, 0:BAND, :] = jnp.zeros((BAND, 128), BF16)
                mult = DIL_SCALE if tq == 0 else 1.0
                for c0 in range(0, SEQ, ch):
                    rows = pl.ds(c0, ch)
                    xv = p_ref[tq, rows, :].astype(F32)
                    x_scr[rows, :] = xv if tq == 2 else _rope_fwd(xv, c_ref[rows, :] * mult, sn_ref[rows, :] * mult,
                                                                   DIL_ROPE_HALF, lanes)
                for r in range(d):
                    for c0 in range(0, sub_len, ch):
                        at = BAND + r * sub_len + c0
                        qkv_ref[tq, 0, at:at + ch, :] = x_scr[_strided(r + c0 * d, ch, d), :].astype(BF16)

        base = t * DIL_ST
        half0 = _head_half((DIL_ST, 128), 0)
        lse_h = []
        for hh in range(2):
            half = _head_half((BAND, 128), hh)
            for b in range(DIL_NB):
                qv = qkv_ref[0, 0, pl.ds(pl.multiple_of(base + (b + 1) * BAND, BAND), BAND), :]
                k2 = qkv_ref[1, 0, pl.ds(pl.multiple_of(base + b * BAND, BAND), 2 * BAND), :]
                sb = _nt(jnp.where(half, qv, jnp.zeros_like(qv)), k2)
                s_scr[b * BAND:(b + 1) * BAND, :] = jnp.where(_band_keep(g, b, t, DIL_NB), sb, NEG)
            s = s_scr[...]
            m = jnp.max(s, axis=-1, keepdims=True)
            pr = jnp.exp(s - m)
            den = jnp.sum(pr, axis=-1, keepdims=True)
            p_scr[...] = pr.astype(BF16)
            for b in range(DIL_NB):
                v2 = qkv_ref[2, 0, pl.ds(pl.multiple_of(base + b * BAND, BAND), 2 * BAND), :]
                o_scr[hh, b * BAND:(b + 1) * BAND, :] = _nn(p_scr[b * BAND:(b + 1) * BAND, :], v2)
            o_scr[hh] = o_scr[hh] / den
            lse_h.append(m + jnp.log(den))
        out = jnp.where(half0, o_scr[0], o_scr[1])
        lse = jnp.where(half0, lse_h[0], lse_h[1])
        for b in range(DIL_NB):
            tok = _dil_tok(g, b, t, DIL_NB)
            o_ref[tok, :] = out[b * BAND:(b + 1) * BAND, :]
            l_ref[tok, :] = lse[b * BAND:(b + 1) * BAND, :]

    tab = pl.BlockSpec((SEQ, 128), lambda pr, t: (0, 0))
    out = pl.BlockSpec((SEQ, 128), lambda pr, t: (0, pr))
    return pl.pallas_call(
        body, name=f"dil_attn_fwd_g{g}", grid=(4, SEQ // DIL_ST),
        in_specs=[pl.BlockSpec((None, 3, SEQ, 128), lambda pr, t: (g * 4 + pr, 0, 0, 0)), tab, tab],
        out_specs=[out, out, pl.BlockSpec((3, 1, BAND + SEQ, 128), lambda pr, t: (0, pr, 0, 0))],
        out_shape=[jax.ShapeDtypeStruct((SEQ, 512), F32), jax.ShapeDtypeStruct((SEQ, 512), F32),
                   jax.ShapeDtypeStruct((3, 4, BAND + SEQ, 128), BF16)],
        scratch_shapes=[pltpu.VMEM((SEQ, 128), F32), pltpu.VMEM((DIL_ST, 2 * BAND), F32),
                        pltpu.VMEM((DIL_ST, 2 * BAND), BF16), pltpu.VMEM((2, DIL_ST, 128), F32)],
        compiler_params=_cparams(),
    )(p_qkv.reshape(12, 3, SEQ, 128), rc, rs)


def _dil_attn_bwd(dp_in, qkv, dyd, yd, lse_all, rc, rs, g, token=None):
    d = DIL_DILATIONS[g]
    sub_len = SEQ // d
    DIL_ST, DIL_NB = DIL_ST_BWD, DIL_ST_BWD // BAND
    nst = SEQ // DIL_ST
    after, after_specs = _after(token)
    ch = 512

    def body(dp_any, q_ref, k_ref, v_ref, do_ref, y_ref, l_ref, c_ref, sn_ref, *rest):
        dp_ref, tok_scr, dk_scr, dv_scr, s_scr, dp_scr, p_scr, ds_scr, do_scr, y_scr, l_scr, dq_scr = rest[-12:]
        del dp_any
        t = pl.program_id(1)
        base = t * DIL_ST

        @pl.when(t == 0)
        def _():
            dk_scr[...] = jnp.zeros_like(dk_scr)
            dv_scr[...] = jnp.zeros_like(dv_scr)

        for b in range(DIL_NB):
            tok = _dil_tok(g, b, t, DIL_NB)
            do_scr[b * BAND:(b + 1) * BAND, :] = do_ref[tok, :]
            y_scr[b * BAND:(b + 1) * BAND, :] = y_ref[tok, :]
            l_scr[b * BAND:(b + 1) * BAND, :] = l_ref[tok, :]
        for hh in range(2):
            half = _head_half((BAND, 128), hh)
            half_st = _head_half((DIL_ST, 128), hh)
            dom = jnp.where(half_st, do_scr[...], 0.0)
            delta = jnp.sum(dom * y_scr[...], axis=-1, keepdims=True)
            lcol = jnp.max(jnp.where(half_st, l_scr[...], NEG), axis=-1, keepdims=True)
            for b in range(DIL_NB):
                rows = slice(b * BAND, (b + 1) * BAND)
                qv = q_ref[0, 0, pl.ds(pl.multiple_of(base + (b + 1) * BAND, BAND), BAND), :]
                band = pl.ds(pl.multiple_of(base + b * BAND, BAND), 2 * BAND)
                sb = _nt(jnp.where(half, qv, jnp.zeros_like(qv)), k_ref[0, 0, band, :])
                s_scr[rows, :] = jnp.where(_band_keep(g, b, t, DIL_NB), sb, NEG)
                dp_scr[rows, :] = _nt(dom[rows, :].astype(BF16), v_ref[0, 0, band, :])
            pr = jnp.exp(s_scr[...] - lcol)
            p_scr[...] = pr.astype(BF16)
            ds_scr[...] = (pr * (dp_scr[...] - delta)).astype(BF16)
            for b in range(DIL_NB):
                rows = slice(b * BAND, (b + 1) * BAND)
                qv = q_ref[0, 0, pl.ds(pl.multiple_of(base + (b + 1) * BAND, BAND), BAND), :]
                band = pl.ds(pl.multiple_of(base + b * BAND, BAND), 2 * BAND)
                dqb = jnp.where(half, _nn(ds_scr[rows, :], k_ref[0, 0, band, :]), 0.0)
                if hh == 0:
                    dq_scr[rows, :] = dqb
                else:
                    dq_scr[rows, :] += dqb
                half2 = _head_half((2 * BAND, 128), hh)
                dk_scr[band, :] += jnp.where(half2, _tn(ds_scr[rows, :], qv), 0.0)
                dv_scr[band, :] += _tn(p_scr[rows, :], dom[rows, :].astype(BF16))
        for b in range(DIL_NB):
            tok_scr[pl.ds(0, 1), _dil_tok(g, b, t, DIL_NB), :] = dq_scr[b * BAND:(b + 1) * BAND, :][None]

        @pl.when(t == nst - 1)
        def _():
            for r in range(d):
                rows = _strided(r, sub_len, d)
                tok_scr[pl.ds(1, 1), rows, :] = dk_scr[BAND + r * sub_len:BAND + (r + 1) * sub_len, :][None]
                tok_scr[pl.ds(2, 1), rows, :] = dv_scr[BAND + r * sub_len:BAND + (r + 1) * sub_len, :][None]
            lanes = _rope_lanes((ch, 128), DIL_ROPE_HALF, 64, 0)
            for c0 in range(0, SEQ, ch):
                rows = slice(c0, c0 + ch)
                cv, sv = c_ref[rows, :], sn_ref[rows, :]
                dp_ref[rows, 0:128] = _rope_bwd(tok_scr[0, rows, :], cv * DIL_SCALE, sv * DIL_SCALE, DIL_ROPE_HALF, lanes).astype(BF16)
                dp_ref[rows, 128:256] = _rope_bwd(tok_scr[1, rows, :], cv, sv, DIL_ROPE_HALF, lanes).astype(BF16)
                dp_ref[rows, 256:384] = tok_scr[2, rows, :].astype(BF16)

    def inp(tq):
        return pl.BlockSpec((1, 1, BAND + SEQ, 128), lambda pr, t: (tq, pr, 0, 0))

    tok_spec = pl.BlockSpec((SEQ, 128), lambda pr, t: (0, pr))
    tab = pl.BlockSpec((SEQ, 128), lambda pr, t: (0, 0))
    st = (DIL_ST, 2 * BAND)
    return pl.pallas_call(
        body, name=f"dil_attn_bwd_g{g}", grid=(4, nst),
        in_specs=[pl.BlockSpec(memory_space=pl.ANY), inp(0), inp(1), inp(2), tok_spec, tok_spec, tok_spec, tab, tab]
        + after_specs,
        out_specs=pl.BlockSpec((SEQ, 384), lambda pr, t: (0, _qkv_block(0, g, pr) // 3)),
        out_shape=jax.ShapeDtypeStruct((SEQ, N_PAD), BF16),
        input_output_aliases={0: 0},
        scratch_shapes=[pltpu.VMEM((3, SEQ, 128), F32),
                        pltpu.VMEM((BAND + SEQ, 128), F32), pltpu.VMEM((BAND + SEQ, 128), F32),
                        pltpu.VMEM(st, F32), pltpu.VMEM(st, F32), pltpu.VMEM(st, BF16), pltpu.VMEM(st, BF16),
                        pltpu.VMEM((DIL_ST, 128), F32), pltpu.VMEM((DIL_ST, 128), F32), pltpu.VMEM((DIL_ST, 128), F32),
                        pltpu.VMEM((DIL_ST, 128), F32)],
        compiler_params=_cparams(),
    )(dp_in, qkv, qkv, qkv, dyd, yd, lse_all, rc, rs, *after)


TAIL_T = 256


def _tail(p, ya, o_g, l_g, x, target, wpm, wpd, wout, post_g):
    tm = TAIL_T

    def body(pgz_ref, ya_ref, o0_ref, o1_ref, o2_ref, l0_ref, l1_ref, l2_ref, x_ref, t_ref,
             wpm_ref, wpd_ref, wout_ref, pg_ref,
             dp_ref, dy_ref, dya_ref, dyd_ref, yd_ref, lse_ref, loss_ref, dgp_ref, dwpm_ref, dwpd_ref, dwout_ref):
        l0, l1, l2 = l0_ref[...], l1_ref[...], l2_ref[...]
        mx = jnp.maximum(jnp.maximum(l0, l1), l2)
        e0, e1, e2 = jnp.exp(l0 - mx), jnp.exp(l1 - mx), jnp.exp(l2 - mx)
        den = e0 + e1 + e2
        yd = (e0 * o0_ref[...] + e1 * o1_ref[...] + e2 * o2_ref[...]) / den
        yd_ref[...] = yd
        lse_ref[...] = mx + jnp.log(den)
        ya = ya_ref[...]

        gm, gd = pgz_ref[:, 0:1024], pgz_ref[:, 1024:2048]
        zm, zd = pgz_ref[:, 2048:2560], pgz_ref[:, 2560:3072]
        szm, szd = _sigmoid(zm), _sigmoid(zd)
        sm, sd = zm * szm, zd * szd
        ua = (ya * sm).astype(BF16)
        ud = (yd * sd).astype(BF16)
        pa = _nn(ua, wpm_ref[...])
        pd = _nn(ud, wpd_ref[...])
        sgm, sgd = _sigmoid(gm), _sigmoid(gd)
        mg = (sgm * pa + sgd * pd).astype(BF16)
        t = _nn(mg, wout_ref[...])
        r3 = lax.rsqrt(jnp.mean(t * t, axis=-1, keepdims=True) + EPS)
        n = t * r3
        pg = pg_ref[...]
        err = x_ref[...] + n * pg - t_ref[...]
        lpart = jnp.sum(err * err, axis=0, keepdims=True)

        dy = err * (1.0 / D_MODEL)
        dy_ref[...] = dy
        gpart = jnp.sum(dy * n, axis=0, keepdims=True)
        dn = dy * pg
        dt = (r3 * (dn - n * jnp.mean(dn * n, axis=-1, keepdims=True))).astype(BF16)
        dmg = _nt(dt, wout_ref[...])
        dpa = (dmg * sgm).astype(BF16)
        dpd = (dmg * sgd).astype(BF16)
        dp_ref[:, 0:1024] = (dmg * pa * sgm * (1.0 - sgm)).astype(BF16)
        dp_ref[:, 1024:2048] = (dmg * pd * sgd * (1.0 - sgd)).astype(BF16)
        dua = _nt(dpa, wpm_ref[...])
        dud = _nt(dpd, wpd_ref[...])
        dya_ref[...] = dua * sm
        dyd_ref[...] = dud * sd
        dp_ref[:, 2048:2560] = (dua * ya * szm * (1.0 + zm * (1.0 - szm))).astype(BF16)
        dp_ref[:, 2560:3072] = (dud * yd * szd * (1.0 + zd * (1.0 - szd))).astype(BF16)

        wpm, wpd, wout = _tn(ua, dpa), _tn(ud, dpd), _tn(mg, dt)

        @pl.when(pl.program_id(0) == 0)
        def _():
            loss_ref[...] = lpart
            dgp_ref[...] = gpart
            dwpm_ref[...] = wpm
            dwpd_ref[...] = wpd
            dwout_ref[...] = wout

        @pl.when(pl.program_id(0) > 0)
        def _():
            loss_ref[...] += lpart
            dgp_ref[...] += gpart
            dwpm_ref[...] += wpm
            dwpd_ref[...] += wpd
            dwout_ref[...] += wout

    def rows(w):
        return pl.BlockSpec((tm, w), lambda i: (i, 0))

    def full(shape):
        return pl.BlockSpec(shape, lambda i: (0, 0))

    def sds(w, dt):
        return jax.ShapeDtypeStruct((SEQ, w), dt)

    return pl.pallas_call(
        body, name="tail", grid=(SEQ // tm,),
        in_specs=[rows(3072), rows(512), rows(512), rows(512), rows(512), rows(512), rows(512), rows(512),
                  rows(1024), rows(1024), full((512, 1024)), full((512, 1024)), full((1024, 1024)), full((1, 1024))],
        out_specs=[rows(3072), rows(1024), rows(512), rows(512), rows(512), rows(512), full((1, 1024)), full((1, 1024)),
                   full((512, 1024)), full((512, 1024)), full((1024, 1024))],
        out_shape=[sds(N_PAD, BF16), sds(1024, F32), sds(512, F32), sds(512, F32), sds(512, F32), sds(512, F32),
                   jax.ShapeDtypeStruct((1, 1024), F32), jax.ShapeDtypeStruct((1, 1024), F32),
                   jax.ShapeDtypeStruct((512, 1024), F32), jax.ShapeDtypeStruct((512, 1024), F32),
                   jax.ShapeDtypeStruct((1024, 1024), F32)],
        compiler_params=_cparams(),
    )(p, ya, o_g[0], o_g[1], o_g[2], l_g[0], l_g[1], l_g[2], x, target, wpm, wpd, wout, post_g)


def _sum_parts(recv, own, me, tr, name):
    n, r, w = recv.shape
    if r % tr:
        return _sum_parts_cols(recv, own, me, name)
    own_spec = (pl.BlockSpec((tr, w), lambda i, me_ref: (i, 0)) if own.ndim == 2
                else pl.BlockSpec((None, tr, w), lambda i, me_ref: (me_ref[0], i, 0)))

    def body(me_ref, p_ref, own_ref, o_ref):
        mine = own_ref[...].astype(F32)
        acc = jnp.zeros((tr, w), F32)
        for s in range(n):
            acc = acc + jnp.where(me_ref[0] == s, mine, p_ref[s].astype(F32))
        o_ref[...] = acc

    return pl.pallas_call(
        body, name=name,
        grid_spec=pltpu.PrefetchScalarGridSpec(
            num_scalar_prefetch=1, grid=(r // tr,),
            in_specs=[pl.BlockSpec((n, tr, w), lambda i, me_ref: (0, i, 0)), own_spec],
            out_specs=pl.BlockSpec((tr, w), lambda i, me_ref: (i, 0))),
        out_shape=jax.ShapeDtypeStruct((r, w), F32),
    )(me.reshape(1), recv, own)


def _sum_parts_cols(recv, own, me, name):
    n, r, w = recv.shape
    tc = 128

    def body(me_ref, p_ref, own_ref, o_ref):
        mine = own_ref[...].astype(F32)
        acc = jnp.zeros((r, tc), F32)
        for s in range(n):
            acc = acc + jnp.where(me_ref[0] == s, mine, p_ref[s].astype(F32))
        o_ref[...] = acc

    return pl.pallas_call(
        body, name=name,
        grid_spec=pltpu.PrefetchScalarGridSpec(
            num_scalar_prefetch=1, grid=(w // tc,),
            in_specs=[pl.BlockSpec((n, r, tc), lambda i, me_ref: (0, 0, i)),
                      pl.BlockSpec((None, r, tc), lambda i, me_ref: (me_ref[0], 0, i))],
            out_specs=pl.BlockSpec((r, tc), lambda i, me_ref: (0, i))),
        out_shape=jax.ShapeDtypeStruct((r, w), F32),
    )(me.reshape(1), recv, own)


def _adamw(w, g, m, v, name):
    lead = w.shape[:-2]
    r, c = w.shape[-2:]
    tr = max([t for t in range(8, 257, 8) if r % t == 0], default=r)
    c1 = 1.0 - ADAM_B1 ** ADAM_STEP
    c2 = 1.0 - ADAM_B2 ** ADAM_STEP

    def body(w_ref, g_ref, m_ref, v_ref, d_ref, nm_ref, nv_ref):
        gv = g_ref[...]
        nm = ADAM_B1 * m_ref[...] + (1.0 - ADAM_B1) * gv
        nv = ADAM_B2 * v_ref[...] + (1.0 - ADAM_B2) * (gv * gv)
        nm_ref[...] = nm
        nv_ref[...] = nv
        d_ref[...] = -ADAM_LR * ((nm / c1) / (jnp.sqrt(nv / c2) + ADAM_EPS) + ADAM_WD * w_ref[...])

    zeros = (0,) * len(lead)
    spec = pl.BlockSpec((1,) * len(lead) + (tr, c), lambda i: zeros + (i, 0))
    sd = jax.ShapeDtypeStruct(w.shape, F32)
    return pl.pallas_call(
        body, name=name, grid=(r // tr,),
        in_specs=[spec] * 4, out_specs=[spec] * 3, out_shape=[sd] * 3,
    )(w, g, m, v)


def _adamw_recv(w, m, v, recv, own, me, name):
    n, r, c = recv.shape
    tr = 128
    c1 = 1.0 - ADAM_B1 ** ADAM_STEP
    c2 = 1.0 - ADAM_B2 ** ADAM_STEP

    def body(me_ref, w_ref, m_ref, v_ref, p_ref, own_ref, d_ref, nm_ref, nv_ref, g_ref):
        mine = own_ref[...].astype(F32)
        gv = jnp.zeros((tr, c), F32)
        for s in range(n):
            gv = gv + jnp.where(me_ref[0] == s, mine, p_ref[s].astype(F32))
        g_ref[0] = gv
        nm = ADAM_B1 * m_ref[0] + (1.0 - ADAM_B1) * gv
        nv = ADAM_B2 * v_ref[0] + (1.0 - ADAM_B2) * (gv * gv)
        nm_ref[0] = nm
        nv_ref[0] = nv
        d_ref[0] = -ADAM_LR * ((nm / c1) / (jnp.sqrt(nv / c2) + ADAM_EPS) + ADAM_WD * w_ref[0])

    full = pl.BlockSpec((1, tr, c), lambda i, me_ref: (0, i, 0))
    sd = jax.ShapeDtypeStruct((1, r, c), F32)
    return pl.pallas_call(
        body, name=name,
        grid_spec=pltpu.PrefetchScalarGridSpec(
            num_scalar_prefetch=1, grid=(r // tr,),
            in_specs=[full, full, full, pl.BlockSpec((n, tr, c), lambda i, me_ref: (0, i, 0)),
                      pl.BlockSpec((None, tr, c), lambda i, me_ref: (me_ref[0], i, 0))],
            out_specs=[full] * 4),
        out_shape=[sd] * 4,
    )(me.reshape(1), w, m, v, recv, own)


def _adamw_in(w_t, m_t, v_t, own_half, swapped, core):
    r, c = SHARD_SHAPES[0]
    tr = max(t for t in range(8, 257, 8) if r % t == 0)
    c1 = 1.0 - ADAM_B1 ** ADAM_STEP
    c2 = 1.0 - ADAM_B2 ** ADAM_STEP

    def body(core_ref, w_ref, m_ref, v_ref, own_ref, sw_ref, d_ref, nm_ref, nv_ref, g_ref):
        own = own_ref[...]
        col_half = lax.broadcasted_iota(jnp.int32, (tr, c), 1) // (c // 2)
        gv = jnp.where(col_half == core_ref[0], jnp.concatenate([own, own], axis=1), sw_ref[...])
        g_ref[0] = gv
        nm = ADAM_B1 * m_ref[0] + (1.0 - ADAM_B1) * gv
        nv = ADAM_B2 * v_ref[0] + (1.0 - ADAM_B2) * (gv * gv)
        nm_ref[0] = nm
        nv_ref[0] = nv
        d_ref[0] = -ADAM_LR * ((nm / c1) / (jnp.sqrt(nv / c2) + ADAM_EPS) + ADAM_WD * w_ref[0])

    full = pl.BlockSpec((1, tr, c), lambda i, core_ref: (0, i, 0))
    sd = jax.ShapeDtypeStruct((1, r, c), F32)
    return pl.pallas_call(
        body, name="adamw_in",
        grid_spec=pltpu.PrefetchScalarGridSpec(
            num_scalar_prefetch=1, grid=(r // tr,),
            in_specs=[full, full, full, pl.BlockSpec((tr, c // 2), lambda i, core_ref: (i, 0)),
                      pl.BlockSpec((tr, c), lambda i, core_ref: (i, 0))],
            out_specs=[full] * 4),
        out_shape=[sd] * 4,
    )(core.reshape(1), w_t, m_t, v_t, own_half, swapped)


ANY = pl.BlockSpec(memory_space=pl.ANY)


def _my_place():
    return lax.axis_index("x"), lax.axis_index("y"), lax.axis_index("c")


HBM = pl.BlockSpec(memory_space=pltpu.HBM)
SEM = pl.BlockSpec(memory_space=pltpu.SEMAPHORE)
DATAFLOW = pltpu.SideEffectType.DATAFLOW_SIDE_EFFECTING


def _near_chips(x, y):
    return [(1 - x, y), (x, 1 - y)]


def _half(mi, hc):
    r, c = SHARD_SHAPES[mi]
    if mi == 0:
        return pl.ds(0, r), pl.ds(pl.multiple_of(hc * (c // 2), 128), c // 2)
    return pl.ds(pl.multiple_of(hc * (r // 2), 16), r // 2), pl.ds(0, c)


def _gather_copies(land_refs, send_sems, recv_sems):
    x, y, c = _my_place()
    out, back = [], []
    for mi in range(N_MATS):
        rows, cols = _half(mi, c)
        mine = land_refs[mi].at[2 * x + y, rows, cols]
        for j, (cx, cy) in enumerate(_near_chips(x, y)):
            sems = dict(send_sem=send_sems.at[mi * 2 + j], recv_sem=recv_sems.at[mi * 2 + j],
                        device_id=(cx, cy, c), device_id_type=MESH)
            out.append(pltpu.make_async_remote_copy(src_ref=mine, dst_ref=mine, **sems))
            got = land_refs[mi].at[2 * cx + cy, rows, cols]
            back.append(pltpu.make_async_remote_copy(src_ref=got, dst_ref=got, **sems))
    return out, back


def _gather_start(landing):
    n = N_MATS

    def body(*refs):
        out, _ = _gather_copies(refs[:n], refs[n], refs[n + 1])
        for cp in out:
            cp.start()
        refs[-1][...] = jnp.zeros_like(refs[-1])

    hbm = [pltpu.HBM(a.shape, a.dtype) for a in landing]
    outs = pl.pallas_call(
        body, name="gather_start",
        out_shape=(pltpu.SemaphoreType.DMA((2 * n,)), pltpu.SemaphoreType.DMA((2 * n,)), *hbm,
                   jax.ShapeDtypeStruct((8, 128), F32)),
        in_specs=[HBM] * n, out_specs=(SEM, SEM, *[HBM] * n, pl.BlockSpec(memory_space=pltpu.VMEM)),
        input_output_aliases={i: 2 + i for i in range(n)},
        compiler_params=pltpu.CompilerParams(has_side_effects=DATAFLOW),
    )(*[pltpu.with_memory_space_constraint(a, pltpu.HBM) for a in landing])
    return outs[:-1], outs[-1]


def _gather_wait(handle, after):
    n = N_MATS

    def body(*refs):
        out, back = _gather_copies(refs[:n], refs[n], refs[n + 1])
        for cp, arrival in zip(out, back):
            cp.wait_send()
            arrival.wait_recv()

    bufs = handle[2:]
    after, after_specs = _after(after)
    res = pl.pallas_call(
        body, name="gather_wait", out_shape=tuple(pltpu.HBM(b.shape, b.dtype) for b in bufs),
        in_specs=[HBM] * n + [SEM, SEM] + after_specs, out_specs=tuple([HBM] * n),
        input_output_aliases={i: i for i in range(n)},
        compiler_params=pltpu.CompilerParams(has_side_effects=DATAFLOW),
    )(*bufs, handle[0], handle[1], *after)
    return list(res)


def _relay_share(gathered):
    n = N_MATS

    def body(*refs):
        out_refs = refs[n:2 * n]
        send_sems, recv_sems = refs[2 * n:]
        x, y, c = _my_place()
        sibling = (x, y, 1 - c)
        relayed = 2 * (x ^ (1 - c)) + (y ^ c)
        relay_to = (x ^ c, y ^ (1 - c), c)
        far = 2 * (1 - x) + (1 - y)
        near = [2 * (1 - x) + y, 2 * x + (1 - y)]

        def copy(k, mi, shard, hc, to):
            blk = out_refs[mi].at[(shard,) + _half(mi, hc)]
            return pltpu.make_async_remote_copy(src_ref=blk, dst_ref=blk, send_sem=send_sems.at[mi * 4 + k],
                                                recv_sem=recv_sems.at[mi * 4 + k], device_id=to, device_id_type=MESH)

        sends = []
        for mi in range(n):
            sends.append(copy(0, mi, relayed, c, relay_to))
            sends += [copy(1 + j, mi, near[j], c, sibling) for j in range(2)]
        for cp in sends:
            cp.start()
        for mi in range(n):
            copy(0, mi, far, c, relay_to).wait_recv()
            cp = copy(3, mi, far, c, sibling)
            cp.start()
            sends.append(cp)
        for mi in range(n):
            for j in range(2):
                copy(1 + j, mi, near[j], 1 - c, sibling).wait_recv()
            copy(3, mi, far, 1 - c, sibling).wait_recv()
        for cp in sends:
            cp.wait_send()

    return pl.pallas_call(
        body, name="relay_share",
        in_specs=[ANY] * n, out_specs=[ANY] * n,
        out_shape=[jax.ShapeDtypeStruct(g.shape, g.dtype) for g in gathered],
        input_output_aliases={i: i for i in range(n)},
        scratch_shapes=[pltpu.SemaphoreType.DMA((4 * n,)), pltpu.SemaphoreType.DMA((4 * n,))],
    )(*gathered)


def _peers(x, y, c):
    out = []
    for k in range(1, 8):
        px, py, pc = x ^ (k >> 2), y ^ ((k >> 1) & 1), c ^ (k & 1)
        out.append((k - 1, (px, py, pc), 4 * px + 2 * py + pc))
    return out


def _exchange_start(parts, name):
    n = len(parts)

    def body(*refs):
        p_refs, land_refs = refs[:n], refs[n:2 * n]
        send_sems, recv_sems, token = refs[2 * n], refs[2 * n + 1], refs[-1]
        x, y, c = _my_place()
        me = 4 * x + 2 * y + c
        for k, dev, peer in _peers(x, y, c):
            for mi in range(n):
                pltpu.make_async_remote_copy(
                    src_ref=p_refs[mi].at[peer], dst_ref=land_refs[mi].at[me], send_sem=send_sems.at[k * n + mi],
                    recv_sem=recv_sems.at[k * n + mi], device_id=dev, device_id_type=MESH).start()
        token[...] = jnp.zeros_like(token)

    hbm = [pltpu.HBM(p.shape, p.dtype) for p in parts]
    outs = pl.pallas_call(
        body, name=name + "_start",
        out_shape=(pltpu.SemaphoreType.DMA((7 * n,)), pltpu.SemaphoreType.DMA((7 * n,)), *hbm, *hbm,
                   jax.ShapeDtypeStruct((8, 128), F32)),
        in_specs=[HBM] * (2 * n), out_specs=(SEM, SEM, *[HBM] * (2 * n), pl.BlockSpec(memory_space=pltpu.VMEM)),
        input_output_aliases={i: 2 + i for i in range(2 * n)},
        compiler_params=pltpu.CompilerParams(has_side_effects=DATAFLOW),
    )(*[pltpu.with_memory_space_constraint(p, pltpu.HBM) for p in parts],
      *[pltpu.with_memory_space_constraint(lax.empty(p.shape, p.dtype), pltpu.HBM) for p in parts])
    return (name, outs[:-1]), outs[-1]


def _exchange_wait(handle, after):
    name, outs = handle
    n = (len(outs) - 2) // 2

    def body(*refs):
        p_refs, land_refs = refs[:n], refs[n:2 * n]
        send_sems, recv_sems = refs[2 * n], refs[2 * n + 1]
        x, y, c = _my_place()
        me = 4 * x + 2 * y + c
        for k, dev, peer in _peers(x, y, c):
            for mi in range(n):
                pltpu.make_async_remote_copy(
                    src_ref=p_refs[mi].at[peer], dst_ref=land_refs[mi].at[me], send_sem=send_sems.at[k * n + mi],
                    recv_sem=recv_sems.at[k * n + mi], device_id=dev, device_id_type=MESH).wait_send()
                slot = land_refs[mi].at[peer]
                pltpu.make_async_remote_copy(
                    src_ref=slot, dst_ref=slot, send_sem=send_sems.at[k * n + mi],
                    recv_sem=recv_sems.at[k * n + mi], device_id=dev, device_id_type=MESH).wait_recv()

    bufs = outs[2:]
    res = pl.pallas_call(
        body, name=name + "_wait", out_shape=tuple(pltpu.HBM(b.shape, b.dtype) for b in bufs),
        in_specs=[HBM] * (2 * n) + [SEM, SEM, ANY], out_specs=tuple([HBM] * (2 * n)),
        input_output_aliases={i: i for i in range(2 * n)},
        compiler_params=pltpu.CompilerParams(has_side_effects=DATAFLOW),
    )(*bufs, outs[0], outs[1], after)
    return list(res[n:])


def _swap_halves(half_in, gvec):
    def body(g_ref, gv_ref, out_ref, rg_ref, send_sems, recv_sems):
        x, y, c = _my_place()
        me = 4 * x + 2 * y + c
        sibling = (x, y, 1 - c)

        def half(hc):
            return out_ref.at[:, pl.ds(pl.multiple_of(hc * 512, 128), 512)]

        sends = [pltpu.make_async_remote_copy(src_ref=g_ref, dst_ref=half(c), send_sem=send_sems.at[7],
                                              recv_sem=recv_sems.at[7], device_id=sibling, device_id_type=MESH)]
        for k, dev, peer in _peers(x, y, c):
            sends.append(pltpu.make_async_remote_copy(src_ref=gv_ref, dst_ref=rg_ref.at[me], send_sem=send_sems.at[k],
                                                      recv_sem=recv_sems.at[k], device_id=dev, device_id_type=MESH))
        for cp in sends:
            cp.start()
        got = half(1 - c)
        pltpu.make_async_remote_copy(src_ref=got, dst_ref=got, send_sem=send_sems.at[7], recv_sem=recv_sems.at[7],
                                     device_id=sibling, device_id_type=MESH).wait_recv()
        for k, dev, peer in _peers(x, y, c):
            got = rg_ref.at[peer]
            pltpu.make_async_remote_copy(src_ref=got, dst_ref=got, send_sem=send_sems.at[k], recv_sem=recv_sems.at[k],
                                         device_id=dev, device_id_type=MESH).wait_recv()
        for cp in sends:
            cp.wait_send()

    return pl.pallas_call(
        body, name="swap_halves",
        in_specs=[ANY, ANY], out_specs=[ANY, ANY],
        out_shape=[jax.ShapeDtypeStruct(SHARD_SHAPES[0], F32), jax.ShapeDtypeStruct((8, 8, N_GVEC), F32)],
        scratch_shapes=[pltpu.SemaphoreType.DMA((8,)), pltpu.SemaphoreType.DMA((8,))],
    )(half_in, gvec)


def _set_slot(arr, block, idx):
    return lax.dynamic_update_slice(arr, block[None], (idx,) + (0,) * block.ndim)


PAD_RUNS = ((6304, 8352, 0), (5280, 6304, COL_Z), (672, 5280, COL_QKV), (0, 640, COL_LAT), (640, 672, COL_LAT + 704))
N_QKV = COL_LAT - COL_QKV


def _qkv_rows_regroup(a, to_padded):
    if to_padded:
        a4 = a.reshape(3, 12, 128, a.shape[1])
        return jnp.stack([a4[0], a4[1], a4[2]], axis=1).reshape(a.shape)
    a4 = a.reshape(12, 3, 128, a.shape[1])
    return jnp.concatenate([a4[:, tq].reshape(N_QKV // 3, a.shape[1]) for tq in range(3)], axis=0)
W_IN_SHARD = 2088


def _full_weights(gathered):
    def cols(a):
        return jnp.concatenate([a[s] for s in range(4)], axis=1)

    w_uq, w_ukv, w_pm, w_pd = [cols(a) for a in gathered[1:5]]
    w_out = gathered[5].reshape(D_MODEL, D_MODEL)
    w_in_t = gathered[0].reshape(4 * W_IN_SHARD, D_MODEL)
    pieces, at = [], 0
    for lo, hi, pad_lo in sorted(PAD_RUNS, key=lambda t: t[2]):
        if pad_lo > at:
            pieces.append(jnp.zeros((pad_lo - at, D_MODEL), w_in_t.dtype))
        pieces.append(_qkv_rows_regroup(w_in_t[lo:hi], True) if pad_lo == COL_QKV else w_in_t[lo:hi])
        at = pad_lo + hi - lo
    pieces.append(jnp.zeros((N_PAD - at, D_MODEL), w_in_t.dtype))
    w_pad_t = jnp.concatenate(pieces, axis=0)
    z32 = jnp.zeros((Q_RANK, 32), w_uq.dtype)
    wuq_pad = jnp.concatenate([t for h in range(MLA_HEADS) for t in (w_uq[:, h * 96:(h + 1) * 96], z32)], axis=1)
    z64 = jnp.zeros((KV_RANK, 64), w_ukv.dtype)
    wk_pad = jnp.concatenate([t for h in range(MLA_HEADS) for t in (w_ukv[:, h * 128:h * 128 + 64], z64)], axis=1)
    wv = jnp.concatenate([w_ukv[:, h * 128 + 64:(h + 1) * 128] for h in range(MLA_HEADS)], axis=1)
    return w_pad_t, wuq_pad, wk_pad, wv, w_pm, w_pd, w_out


W_IN_LAT = 672


def _grad_parts_in_early(dwt_early):
    dwt_early = jnp.concatenate([dwt_early[:COL_QKV], _qkv_rows_regroup(dwt_early[COL_QKV:COL_LAT], False)], axis=0)

    def in_block(s, h):
        cols = slice(h * 512, (h + 1) * 512)
        out = []
        for lo, hi, pad_lo in sorted(PAD_RUNS):
            a_, b_ = max(lo, s * W_IN_SHARD), min(hi, (s + 1) * W_IN_SHARD)
            if a_ < b_:
                out.append(jnp.zeros((b_ - a_, 512), dwt_early.dtype) if pad_lo >= COL_LAT
                           else dwt_early[pad_lo + a_ - lo:pad_lo + b_ - lo, cols])
        return jnp.concatenate(out, axis=0)

    return jnp.stack([in_block(s, h) for s in range(4) for h in range(2)])


def _grad_parts_in_late(dwt_late):
    rows = jnp.concatenate([dwt_late[0:640], dwt_late[704:736]], axis=0)
    zero = jnp.zeros((W_IN_LAT, 512), dwt_late.dtype)
    return jnp.stack([rows[:, 0:512], rows[:, 512:1024]] + [zero] * 6)


def _shard_blocks(m, axis=1):
    n = m.shape[axis] // 4
    cut = (lambda s: m[:, s * n:(s + 1) * n]) if axis == 1 else (lambda s: m[s * n:(s + 1) * n])
    return jnp.stack([cut(s) for s in range(4) for _ in range(2)])


def _grad_parts_mla(dwuq_pad, dwk_pad, dwv):
    d_uq = jnp.concatenate([dwuq_pad[:, h * 128:h * 128 + 96] for h in range(MLA_HEADS)], axis=1)
    d_ukv = jnp.concatenate([t for h in range(MLA_HEADS) for t in (dwk_pad[:, h * 128:h * 128 + 64], dwv[:, h * 64:(h + 1) * 64])],
                            axis=1)
    return [_shard_blocks(d_uq.astype(BF16)), _shard_blocks(d_ukv.astype(BF16))]


def _rope_tables(positions, token=None):
    pos = positions.reshape(SEQ).astype(F32)
    if token is not None:
        pos = pos + token[0, 0]
    lane = jnp.arange(128)

    def table(rot, first, period):
        inv = ROPE_THETA ** (-jnp.arange(0, rot, 2, dtype=F32) / rot)
        half = rot // 2
        off = lane % period - first
        in1, in2 = (off >= 0) & (off < half), (off >= half) & (off < rot)
        inv_lane = jnp.where(in1 | in2, inv[jnp.clip(off % half, 0, half - 1)], 0.0)
        sign = jnp.where(in1, -1.0, 1.0).astype(F32)
        ang = pos[:, None] * inv_lane[None, :]
        return jnp.cos(ang), jnp.sin(ang) * sign[None, :]

    return table(32, 64, 128), table(16, 0, 64)


class _Links:
    def __init__(self, mats, chip, me):
        landing = [_set_slot(lax.empty((4,) + m.shape, m.dtype), m, chip) for m in mats]
        self.gather, self.token = _gather_start(landing)
        self.me, self.sent, self.handles, self.sums, self.raw = me, {}, {}, {}, {}

    def weights(self, after):
        return _relay_share(_gather_wait(self.gather, after))

    def send(self, blocks, name):
        self.sent[name] = blocks
        self.handles[name], token = _exchange_start(blocks, name)
        return token

    def collect(self, name, after, parts):
        recv = _exchange_wait(self.handles[name], after)
        for r, own, part in zip(recv, self.sent[name], parts):
            if part.startswith("in_"):
                self.sums[part] = _sum_parts(r, own, self.me, 64, "sum_grad_" + part)
            else:
                self.raw[part] = (r, own)
        return tuple(self.sums[part] for part in parts if part in self.sums)


def _device_grads(x, positions, target, gains, links):
    pre_g, q_g, kv_g, post_g = gains
    (mc, ms), (dc, ds) = _rope_tables(positions, links.token)
    h = _prenorm_fwd(x, pre_g, links.token)
    w_pad_t, wuq_pad, wk_pad, wv, w_pm, w_pd, w_out = _full_weights(links.weights((h, mc, ms, dc, ds)))

    p_gz = _matmul(h, w_pad_t, "nt", F32, 1024, 1536, 1024, "in_proj_gates", b_cols=(0, COL_QKV // 1536))
    p_qkv = _matmul(h, w_pad_t, "nt", BF16, 1024, 1536, 1024, "in_proj_dilated", b_cols=(COL_QKV // 1536, N_QKV // 1536),
                    lane_blocks=True)
    p_lat = _matmul(h, w_pad_t, "nt", F32, 1024, N_LAT, 1024, "in_proj_latent", b_cols=(COL_LAT // N_LAT, 1))
    q, k, v = _mla_prep_fwd(p_lat, q_g, kv_g, wuq_pad, wk_pad, wv, mc, ms)
    ya, lse_m = _mla_flash_fwd(q, k, v)
    o_g, l_g, qkv = zip(*[_dil_attn_fwd(p_qkv, dc, ds, g) for g in range(3)])
    (dp, dy, dya, dyd, yd, lse_d, loss_cols, dg_post, dwpm, dwpd, dwout) = _tail(
        p_gz, ya, o_g, l_g, x, target, w_pm, w_pd, w_out, post_g)

    for g in range(3):
        dp = _dil_attn_bwd(dp, qkv[g], dyd, yd, lse_d, dc, ds, g)
    dw_early = _matmul(dp, h, "tn", BF16, 1536, 1024, 2048, "dw_in_early", a_cols=(0, COL_LAT // 1536))
    token = links.send([_grad_parts_in_early(dw_early), _shard_blocks(dwpm.astype(BF16)), _shard_blocks(dwpd.astype(BF16)),
                        _shard_blocks(dwout.astype(BF16), axis=0)], "exchange_early")

    dq, dk, dv = _mla_flash_bwd(q, k, v, ya, dya, lse_m, token)
    dp, dwuq_pad, dwk_pad, dwv, dg_q, dg_kv = _mla_prep_bwd(dp, p_lat, dq, dk, dv, q_g, kv_g, wuq_pad, wk_pad, wv, mc, ms)
    dw_late = _matmul(dp, h, "tn", BF16, N_LAT, 1024, 2048, "dw_in_late", a_cols=(COL_LAT // N_LAT, 1))
    token = links.send([_grad_parts_in_late(dw_late)] + _grad_parts_mla(dwuq_pad, dwk_pad, dwv), "exchange_late")
    early = links.collect("exchange_early", dw_late, ("in_early", "pm", "pd", "out"))

    grad_x, dg_pre = _dh_prenorm_bwd(dp, w_pad_t, x, dy, pre_g, (token,) + tuple(early))
    links.collect("exchange_late", grad_x, ("in_late", "uq", "ukv"))

    loss_part = jnp.pad((jnp.sum(loss_cols) * (0.5 / D_MODEL)).reshape(1, 1), ((0, 0), (0, N_GVEC - N_GAINS - 1)))
    gvec = jnp.concatenate([dg_pre, dg_q, dg_kv, dg_post, loss_part], axis=1)
    return grad_x, gvec


def kernel(x, positions, pre_norm_g, w_in, q_norm_g, w_uq, kv_norm_g, w_ukv, w_proj_mla, w_proj_dil, w_out, post_norm_g, loss_target, m_pre_norm_g, m_w_in, m_q_norm_g, m_w_uq, m_kv_norm_g, m_w_ukv, m_w_proj_mla, m_w_proj_dil, m_w_out, m_post_norm_g, v_pre_norm_g, v_w_in, v_q_norm_g, v_w_uq, v_kv_norm_g, v_w_ukv, v_w_proj_mla, v_w_proj_dil, v_w_out, v_post_norm_g):
    xi, yi, ci = _my_place()
    chip, me = 2 * xi + yi, 4 * xi + 2 * yi + ci
    mats = [jnp.swapaxes(w_in, 1, 2)] + [w_uq, w_ukv, w_proj_mla, w_proj_dil, w_out]
    mats = [w.reshape(w.shape[1:]).astype(BF16) for w in mats]
    links = _Links(mats, chip, me)
    gains = (pre_norm_g, q_norm_g, kv_norm_g, post_norm_g)
    grad_x, gvec = _device_grads(x[0], positions, loss_target[0], gains, links)

    sums = links.sums
    in_e = sums["in_early"]
    half_in = jnp.concatenate([in_e[:W_IN_LAT] + jnp.where(chip == 0, sums["in_late"], 0.0), in_e[W_IN_LAT:]], axis=0)
    gvec8 = jnp.pad(gvec, ((0, 7), (0, 0)))
    swapped_in, recv_gains = _swap_halves(half_in, gvec8)
    g_gains = _sum_parts(recv_gains, gvec8, me, 8, "sum_gain_parts")[0:1]
    loss = g_gains[0, N_GAINS]
    sw = lambda a: jnp.swapaxes(a, 1, 2)
    d_in, m_in, v_in, g_in = [sw(o) for o in _adamw_in(sw(w_in), sw(m_w_in), sw(v_w_in), half_in, swapped_in, ci)]
    off = [0, 1024, 1408, 1664, 2688]
    g_gain = [g_gains[:, off[i]:off[i + 1]] for i in range(4)]
    ws = [pre_norm_g, w_in, q_norm_g, w_uq, kv_norm_g, w_ukv, w_proj_mla, w_proj_dil, w_out, post_norm_g]
    ms = [m_pre_norm_g, m_w_in, m_q_norm_g, m_w_uq, m_kv_norm_g, m_w_ukv, m_w_proj_mla, m_w_proj_dil, m_w_out, m_post_norm_g]
    vs = [v_pre_norm_g, v_w_in, v_q_norm_g, v_w_uq, v_kv_norm_g, v_w_ukv, v_w_proj_mla, v_w_proj_dil, v_w_out, v_post_norm_g]
    part_of = [None, "in", None, "uq", None, "ukv", "pm", "pd", "out", None]
    gain_of = iter(g_gain)
    grads, deltas, new_m, new_v = [], [], [], []
    for i, (w, m, v, part) in enumerate(zip(ws, ms, vs, part_of)):
        if part == "in":
            d_, m_, v_, g = d_in, m_in, v_in, g_in
        elif part is not None:
            d_, m_, v_, g = _adamw_recv(w, m, v, *links.raw[part], me, f"adamw_{i}")
        else:
            g = next(gain_of)
            d_, m_, v_ = _adamw(w, g, m, v, f"adamw_{i}")
        grads.append(g)
        deltas.append(d_)
        new_m.append(m_)
        new_v.append(v_)
    return (loss, grad_x.reshape(x.shape), *grads, *deltas, *new_m, *new_v)
```

```python
import jax
import jax.numpy as jnp
from jax import lax
from jax.experimental import pallas as pl
from jax.experimental.pallas import tpu as pltpu

F32 = jnp.float32
BF16 = jnp.bfloat16

SEQ = 4096
D_MODEL = 1024
EPS = 1e-6
ROPE_THETA = 500000.0
MLA_HEADS = 8
Q_RANK = 384
KV_RANK = 256
MLA_SCALE = 96.0 ** -0.5
MLA_ROPE_HALF = 16
DIL_DILATIONS = (1, 4, 16)
DIL_ROPE_HALF = 8
DIL_SCALE = 0.125
BAND = 128

N_LAT = 768
COL_Z, COL_QKV, COL_LAT = 2048, 3072, 7680
N_PAD = 8448


def _qkv_block(tq, g, pr):
    return COL_QKV // 128 + (g * 4 + pr) * 3 + tq

IN_SPLITS = (384, 256, 32, 4608, 512, 512, 1024, 1024)

SHARD_SHAPES = ((2088, 1024), (384, 192), (256, 256), (512, 256), (512, 256), (256, 1024))
N_MATS = len(SHARD_SHAPES)
N_GAINS = 2688
N_GVEC = N_GAINS + 128

ADAM_LR, ADAM_B1, ADAM_B2, ADAM_EPS, ADAM_WD, ADAM_STEP = 0.001, 0.9, 0.999, 1e-08, 0.01, 10

VMEM_LIMIT = 56 * 1024 * 1024
NEG = -1e30
MESH = pl.DeviceIdType.MESH


def _cparams(**kw):
    return pltpu.CompilerParams(vmem_limit_bytes=VMEM_LIMIT, **kw)


def _dot(a, b, dims):
    return lax.dot_general(a, b, (dims, ((), ())), preferred_element_type=F32)


def _nn(a, b):
    return _dot(a, b, ((1,), (0,)))


def _nt(a, b):
    return _dot(a, b, ((1,), (1,)))


def _tn(a, b):
    return _dot(a, b, ((0,), (0,)))


def _rope_lanes(shape, half, period, first):
    lane = lax.broadcasted_iota(jnp.int32, shape, len(shape) - 1) % period
    return (lane >= first) & (lane < first + half), (lane >= first + half) & (lane < first + 2 * half)


def _rope_fwd(x, c, s, half, lanes):
    x1, _ = lanes
    return x * c + jnp.where(x1, pltpu.roll(x, 128 - half, 1), pltpu.roll(x, half, 1)) * s


def _rope_bwd(g, c, s, half, lanes):
    x1, x2 = lanes
    gs = g * s
    return g * c + jnp.where(x2, pltpu.roll(gs, half, 1), jnp.where(x1, pltpu.roll(gs, 128 - half, 1), 0.0))


def _sigmoid(x):
    return 1.0 / (1.0 + jnp.exp(-x))


def _after(token):
    tokens = [t for t in (token if isinstance(token, (tuple, list)) else [token]) if t is not None]
    return tokens, [pl.BlockSpec(memory_space=pl.ANY)] * len(tokens)


def _matmul(a, b, mode, out_dtype, tm, tn, tk, name, token=None, b_cols=None, a_cols=None, lane_blocks=False):
    after, after_specs = _after(token)
    if mode == "nn":
        (m, k), n = a.shape, b.shape[1]
        first = 0
        if b_cols is not None:
            first, n = b_cols[0], b_cols[1] * tn
        a_spec = pl.BlockSpec((tm, tk), lambda j, i, kk: (i, kk))
        b_spec = pl.BlockSpec((tk, tn), lambda j, i, kk: (kk, j + first))
        dot = _nn
    elif mode == "nt":
        (m, k), n = a.shape, b.shape[0]
        first = 0
        if b_cols is not None:
            first, n = b_cols[0], b_cols[1] * tn
        a_spec = pl.BlockSpec((tm, tk), lambda j, i, kk: (i, kk))
        b_spec = pl.BlockSpec((tn, tk), lambda j, i, kk: (j + first, kk))
        dot = _nt
    else:
        (k, m), n = a.shape, b.shape[1]
        first = 0
        if a_cols is not None:
            first, m = a_cols[0], a_cols[1] * tm
        a_spec = pl.BlockSpec((tk, tm), lambda j, i, kk: (kk, i + first))
        b_spec = pl.BlockSpec((tk, tn), lambda j, i, kk: (kk, j))
        dot = _tn
    assert m % tm == 0 and n % tn == 0 and k % tk == 0, (name, m, n, k, tm, tn, tk)
    nk = k // tk

    def write(o_ref, val):
        if lane_blocks:
            for blk in range(tn // 128):
                o_ref[blk] = val[:, blk * 128:(blk + 1) * 128].astype(o_ref.dtype)
        else:
            o_ref[...] = val.astype(o_ref.dtype)

    def body_single(a_ref, b_ref, *rest):
        write(rest[-1], dot(a_ref[...], b_ref[...]))

    def body_accumulate(a_ref, b_ref, *rest):
        o_ref, acc_ref = rest[-2:]
        kk = pl.program_id(2)
        part = dot(a_ref[...], b_ref[...])

        @pl.when(kk == 0)
        def _():
            acc_ref[...] = part

        @pl.when(kk > 0)
        def _():
            acc_ref[...] += part

        @pl.when(kk == nk - 1)
        def _():
            write(o_ref, acc_ref[...])

    if lane_blocks:
        out_spec = pl.BlockSpec((tn // 128, tm, 128), lambda j, i, kk: (j, i, 0))
        out_shape = jax.ShapeDtypeStruct((n // 128, m, 128), out_dtype)
    else:
        out_spec = pl.BlockSpec((tm, tn), lambda j, i, kk: (i, j))
        out_shape = jax.ShapeDtypeStruct((m, n), out_dtype)
    return pl.pallas_call(
        body_single if nk == 1 else body_accumulate, name=name, grid=(n // tn, m // tm, nk),
        in_specs=[a_spec, b_spec] + after_specs,
        out_specs=out_spec, out_shape=out_shape,
        scratch_shapes=[] if nk == 1 else [pltpu.VMEM((tm, tn), F32)],
        compiler_params=_cparams(),
    )(a, b, *after)


def _prenorm_fwd(x, g, token=None):
    tm = 512
    after, after_specs = _after(token)

    def body(x_ref, g_ref, *rest):
        xv = x_ref[...]
        r = lax.rsqrt(jnp.mean(xv * xv, axis=-1, keepdims=True) + EPS)
        rest[-1][...] = (xv * r * g_ref[...]).astype(BF16)

    return pl.pallas_call(
        body, name="prenorm_fwd", grid=(SEQ // tm,),
        in_specs=[pl.BlockSpec((tm, D_MODEL), lambda i: (i, 0)), pl.BlockSpec((1, D_MODEL), lambda i: (0, 0))] + after_specs,
        out_specs=pl.BlockSpec((tm, D_MODEL), lambda i: (i, 0)),
        out_shape=jax.ShapeDtypeStruct((SEQ, D_MODEL), BF16),
    )(x, g, *after)


def _dh_prenorm_bwd(dp, w_pad_t, x, dy, g, token=None):
    tm, tk = 1024, 1408
    nk = N_PAD // tk
    after, after_specs = _after(token)

    def body(a_ref, b_ref, x_ref, dy_ref, g_ref, *rest):
        gx_ref, dg_ref, acc_ref = rest[-3:]
        i, kk = pl.program_id(0), pl.program_id(1)
        part = _nn(a_ref[...], b_ref[...])

        @pl.when(kk == 0)
        def _():
            acc_ref[...] = part

        @pl.when(kk > 0)
        def _():
            acc_ref[...] += part

        @pl.when(kk == nk - 1)
        def _():
            xv = x_ref[...]
            r = lax.rsqrt(jnp.mean(xv * xv, axis=-1, keepdims=True) + EPS)
            n = xv * r
            dhv = acc_ref[...]
            dn = dhv * g_ref[...]
            gx_ref[...] = dy_ref[...] + r * (dn - n * jnp.mean(dn * n, axis=-1, keepdims=True))
            cols = jnp.sum(dhv * n, axis=0, keepdims=True)

            @pl.when(i == 0)
            def _():
                dg_ref[...] = cols

            @pl.when(i > 0)
            def _():
                dg_ref[...] += cols

    row = pl.BlockSpec((tm, D_MODEL), lambda i, kk: (i, 0))
    vec = pl.BlockSpec((1, D_MODEL), lambda i, kk: (0, 0))
    return pl.pallas_call(
        body, name="dh_prenorm_bwd", grid=(SEQ // tm, nk),
        in_specs=[pl.BlockSpec((tm, tk), lambda i, kk: (i, kk)), pl.BlockSpec((tk, D_MODEL), lambda i, kk: (kk, 0)),
                  row, row, vec] + after_specs,
        out_specs=[row, vec],
        out_shape=[jax.ShapeDtypeStruct((SEQ, D_MODEL), F32), jax.ShapeDtypeStruct((1, D_MODEL), F32)],
        scratch_shapes=[pltpu.VMEM((tm, D_MODEL), F32)],
        compiler_params=_cparams(),
    )(dp, w_pad_t, x, dy, g, *after)


def _mla_prep_fwd(p, qg, kvg, wuq, wk, wv, rc, rs):
    tm = 512

    def body(lat_ref, qg_ref, kvg_ref, wuq_ref, wk_ref, wv_ref, c_ref, s_ref, q_ref, k_ref, v_ref):
        c, s = c_ref[...], s_ref[...]
        lanes = _rope_lanes((tm, 128), MLA_ROPE_HALF, 128, 64)
        cq = lat_ref[:, 0:Q_RANK]
        r1 = lax.rsqrt(jnp.mean(cq * cq, axis=-1, keepdims=True) + EPS)
        cqn = (cq * r1 * qg_ref[...]).astype(BF16)
        q = _nn(cqn, wuq_ref[...])
        for h in range(MLA_HEADS):
            sl = slice(h * 128, (h + 1) * 128)
            q_ref[:, sl] = (_rope_fwd(q[:, sl], c, s, MLA_ROPE_HALF, lanes) * MLA_SCALE).astype(BF16)
        ckv = lat_ref[:, Q_RANK:Q_RANK + KV_RANK]
        r2 = lax.rsqrt(jnp.mean(ckv * ckv, axis=-1, keepdims=True) + EPS)
        ckvn = (ckv * r2 * kvg_ref[...]).astype(BF16)
        krr = _rope_fwd(lat_ref[:, Q_RANK + KV_RANK:N_LAT], c, s, MLA_ROPE_HALF, lanes)
        kn = _nn(ckvn, wk_ref[...])
        for h in range(MLA_HEADS):
            sl = slice(h * 128, (h + 1) * 128)
            k_ref[:, sl] = (kn[:, sl] + krr).astype(BF16)
        v_ref[...] = _nn(ckvn, wv_ref[...]).astype(BF16)

    def full(shape):
        return pl.BlockSpec(shape, lambda i: (0, 0))

    def rows(w):
        return pl.BlockSpec((tm, w), lambda i: (i, 0))

    return pl.pallas_call(
        body, name="mla_prep_fwd", grid=(SEQ // tm,),
        in_specs=[pl.BlockSpec((tm, N_LAT), lambda i: (i, 0)),
                  full((1, Q_RANK)), full((1, KV_RANK)), full((Q_RANK, 1024)), full((KV_RANK, 1024)),
                  full((KV_RANK, 512)), rows(128), rows(128)],
        out_specs=[rows(1024), rows(1024), rows(512)],
        out_shape=[jax.ShapeDtypeStruct((SEQ, 1024), BF16), jax.ShapeDtypeStruct((SEQ, 1024), BF16),
                   jax.ShapeDtypeStruct((SEQ, 512), BF16)],
        compiler_params=_cparams(),
    )(p, qg, kvg, wuq, wk, wv, rc, rs)


def _mla_prep_bwd(dp_in, p, dq, dk, dv, qg, kvg, wuq, wk, wv, rc, rs):
    tm = 512

    def body(dp_any, lat_ref, dq_ref, dk_ref, dv_ref, qg_ref, kvg_ref, wuq_ref, wk_ref, wv_ref,
             c_ref, s_ref, dp_ref, dwuq_ref, dwk_ref, dwv_ref, dgq_ref, dgkv_ref, dqb_ref, dkb_ref):
        del dp_any
        c, s = c_ref[...], s_ref[...]
        lanes = _rope_lanes((tm, 128), MLA_ROPE_HALF, 128, 64)
        lane = lax.broadcasted_iota(jnp.int32, (tm, 128), 1)
        dkr = jnp.zeros((tm, 128), F32)
        for h in range(MLA_HEADS):
            sl = slice(h * 128, (h + 1) * 128)
            dqb_ref[:, sl] = _rope_bwd(dq_ref[:, sl] * MLA_SCALE, c, s, MLA_ROPE_HALF, lanes).astype(BF16)
            dkh = dk_ref[:, sl]
            dkr = dkr + dkh
            dkb_ref[:, sl] = jnp.where(lane < 64, dkh, 0.0).astype(BF16)
        dkr = jnp.where((lane >= 64) & (lane < 96), dkr, 0.0)
        dkr = _rope_bwd(dkr, c, s, MLA_ROPE_HALF, lanes)
        dvb = dv_ref[...].astype(BF16)

        cq = lat_ref[:, 0:Q_RANK]
        r1 = lax.rsqrt(jnp.mean(cq * cq, axis=-1, keepdims=True) + EPS)
        n1 = cq * r1
        dcqn = _nt(dqb_ref[...], wuq_ref[...])
        dn1 = dcqn * qg_ref[...]
        dcq = r1 * (dn1 - n1 * jnp.mean(dn1 * n1, axis=-1, keepdims=True))
        pq = jnp.sum(dcqn * n1, axis=0, keepdims=True)

        ckv = lat_ref[:, Q_RANK:Q_RANK + KV_RANK]
        r2 = lax.rsqrt(jnp.mean(ckv * ckv, axis=-1, keepdims=True) + EPS)
        n2 = ckv * r2
        dckvn = _nt(dkb_ref[...], wk_ref[...]) + _nt(dvb, wv_ref[...])
        dn2 = dckvn * kvg_ref[...]
        dckv = r2 * (dn2 - n2 * jnp.mean(dn2 * n2, axis=-1, keepdims=True))
        pkv = jnp.sum(dckvn * n2, axis=0, keepdims=True)
        cqn = (n1 * qg_ref[...]).astype(BF16)
        ckvn = (n2 * kvg_ref[...]).astype(BF16)
        wq, wk_, wv_ = _tn(cqn, dqb_ref[...]), _tn(ckvn, dkb_ref[...]), _tn(ckvn, dvb)

        dp_ref[:, 0:Q_RANK] = dcq.astype(BF16)
        dp_ref[:, Q_RANK:Q_RANK + KV_RANK] = dckv.astype(BF16)
        dp_ref[:, Q_RANK + KV_RANK:N_LAT] = dkr.astype(BF16)

        @pl.when(pl.program_id(0) == 0)
        def _():
            dgq_ref[...] = pq
            dgkv_ref[...] = pkv
            dwuq_ref[...] = wq
            dwk_ref[...] = wk_
            dwv_ref[...] = wv_

        @pl.when(pl.program_id(0) > 0)
        def _():
            dgq_ref[...] += pq
            dgkv_ref[...] += pkv
            dwuq_ref[...] += wq
            dwk_ref[...] += wk_
            dwv_ref[...] += wv_

    def full(shape):
        return pl.BlockSpec(shape, lambda i: (0, 0))

    def rows(w):
        return pl.BlockSpec((tm, w), lambda i: (i, 0))

    lat = pl.BlockSpec((tm, N_LAT), lambda i: (i, 0))
    dlat = pl.BlockSpec((tm, N_LAT), lambda i: (i, COL_LAT // N_LAT))
    return pl.pallas_call(
        body, name="mla_prep_bwd", grid=(SEQ // tm,),
        in_specs=[pl.BlockSpec(memory_space=pl.ANY), lat, rows(1024), rows(1024), rows(512),
                  full((1, Q_RANK)), full((1, KV_RANK)), full((Q_RANK, 1024)), full((KV_RANK, 1024)),
                  full((KV_RANK, 512)), rows(128), rows(128)],
        out_specs=[dlat, full((Q_RANK, 1024)), full((KV_RANK, 1024)), full((KV_RANK, 512)),
                   full((1, Q_RANK)), full((1, KV_RANK))],
        out_shape=[jax.ShapeDtypeStruct((SEQ, N_PAD), BF16), jax.ShapeDtypeStruct((Q_RANK, 1024), F32),
                   jax.ShapeDtypeStruct((KV_RANK, 1024), F32), jax.ShapeDtypeStruct((KV_RANK, 512), F32),
                   jax.ShapeDtypeStruct((1, Q_RANK), F32), jax.ShapeDtypeStruct((1, KV_RANK), F32)],
        input_output_aliases={0: 0},
        scratch_shapes=[pltpu.VMEM((tm, 1024), BF16), pltpu.VMEM((tm, 1024), BF16)],
        compiler_params=_cparams(),
    )(dp_in, p, dq, dk, dv, qg, kvg, wuq, wk, wv, rc, rs)


FLASH_T = 1024


def _head_half(shape, hh):
    lane = lax.broadcasted_iota(jnp.int32, shape, 1)
    return (lane < 64) if hh == 0 else (lane >= 64)


def _diag_keep(nr, nk):
    row = lax.broadcasted_iota(jnp.int32, (nr, nk), 0)
    col = lax.broadcasted_iota(jnp.int32, (nr, nk), 1)
    return row + (nk - nr) >= col


def _tri_steps(nb, q_major):
    if q_major:
        pairs = [(i, kb) for i in range(nb) for kb in range(i + 1)]
    else:
        pairs = [(i, kb) for kb in range(nb) for i in range(kb, nb)]
    return jnp.asarray([p[0] for p in pairs], jnp.int32), jnp.asarray([p[1] for p in pairs], jnp.int32)


def _mla_flash_fwd(q, k, v):
    t = FLASH_T
    nb = SEQ // t
    qtab, ktab = _tri_steps(nb, True)

    def body(qi_ref, ki_ref, q_ref, k_ref, v_ref, o_ref, lse_ref, m_scr, l_scr, acc_scr):
        step = pl.program_id(1)
        i, kb = qi_ref[step], ki_ref[step]

        @pl.when(kb == 0)
        def _():
            m_scr[...] = jnp.full_like(m_scr, NEG)
            l_scr[...] = jnp.zeros_like(l_scr)
            acc_scr[...] = jnp.zeros_like(acc_scr)

        def update(r0, nr, nk, diagonal):
            rs = slice(r0, r0 + nr)
            vv = v_ref[0:nk, :]
            for hh in range(2):
                sl = slice(hh * 128, (hh + 1) * 128)
                s = _nt(q_ref[rs, sl], k_ref[0:nk, sl])
                if diagonal:
                    s = jnp.where(_diag_keep(nr, nk), s, NEG)
                m_prev = m_scr[hh, rs, :]
                m_new = jnp.maximum(m_prev, jnp.max(s, axis=-1, keepdims=True))
                pr = jnp.exp(s - jnp.tile(m_new, (1, nk // 128)))
                alpha = jnp.exp(m_prev - m_new)
                l_scr[hh, rs, :] = alpha * l_scr[hh, rs, :] + jnp.sum(pr, axis=-1, keepdims=True)
                acc_scr[hh, rs, :] = alpha * acc_scr[hh, rs, :] + _nn(pr.astype(BF16), vv)
                m_scr[hh, rs, :] = m_new

        @pl.when(kb < i)
        def _():
            update(0, t, t, False)

        @pl.when(kb == i)
        def _():
            update(0, t // 2, t // 2, True)
            update(t // 2, t // 2, t, True)
            o0 = acc_scr[0] / l_scr[0]
            o1 = acc_scr[1] / l_scr[1]
            o_ref[...] = jnp.where(_head_half((t, 128), 0), o0, o1)
            for hh in range(2):
                lse_ref[:, hh * 128:(hh + 1) * 128] = m_scr[hh] + jnp.log(l_scr[hh])

    grid_spec = pltpu.PrefetchScalarGridSpec(
        num_scalar_prefetch=2, grid=(4, qtab.shape[0]),
        in_specs=[pl.BlockSpec((t, 256), lambda j, s, qi, ki: (qi[s], j)),
                  pl.BlockSpec((t, 256), lambda j, s, qi, ki: (ki[s], j)),
                  pl.BlockSpec((t, 128), lambda j, s, qi, ki: (ki[s], j))],
        out_specs=[pl.BlockSpec((t, 128), lambda j, s, qi, ki: (qi[s], j)),
                   pl.BlockSpec((t, 256), lambda j, s, qi, ki: (qi[s], j))],
        scratch_shapes=[pltpu.VMEM((2, t, 128), F32), pltpu.VMEM((2, t, 128), F32), pltpu.VMEM((2, t, 128), F32)])
    return pl.pallas_call(
        body, name="mla_flash_fwd", grid_spec=grid_spec,
        out_shape=[jax.ShapeDtypeStruct((SEQ, 512), F32), jax.ShapeDtypeStruct((SEQ, 1024), F32)],
        compiler_params=_cparams(),
    )(qtab, ktab, q, k, v)


def _mla_flash_bwd(q, k, v, o, do, lse, token=None):
    t = FLASH_T
    nb = SEQ // t
    qtab, ktab = _tri_steps(nb, False)
    after, after_specs = _after(token)

    def body(qi_ref, ki_ref, q_ref, k_ref, v_ref, o_ref, do_ref, lse_ref, *rest):
        dq_ref, dk_ref, dv_ref, dk_scr, dv_scr = rest[-5:]
        step = pl.program_id(1)
        i, kb = qi_ref[step], ki_ref[step]

        @pl.when(step == 0)
        def _():
            dq_ref[...] = jnp.zeros_like(dq_ref)

        @pl.when(i == kb)
        def _():
            dk_scr[...] = jnp.zeros_like(dk_scr)
            dv_scr[...] = jnp.zeros_like(dv_scr)

        def update(r0, nr, nk, diagonal):
            rs = slice(r0, r0 + nr)
            vv = v_ref[0:nk, :]
            ov = o_ref[rs, :]
            dov = do_ref[rs, :]
            rows = pl.ds(pl.multiple_of(i * t + r0, t // 2), nr)
            for hh in range(2):
                sl = slice(hh * 128, (hh + 1) * 128)
                qh, kh = q_ref[rs, sl], k_ref[0:nk, sl]
                s = _nt(qh, kh)
                if diagonal:
                    s = jnp.where(_diag_keep(nr, nk), s, NEG)
                pr = jnp.exp(s - jnp.tile(lse_ref[rs, sl], (1, nk // 128)))
                dom = jnp.where(_head_half((nr, 128), hh), dov, 0.0)
                domb = dom.astype(BF16)
                dv_scr[0:nk, :] += _tn(pr.astype(BF16), domb)
                dpr = _nt(domb, vv)
                delta = jnp.sum(dom * ov, axis=-1, keepdims=True)
                ds = (pr * (dpr - delta)).astype(BF16)
                dq_ref[rows, sl] += _nn(ds, kh)
                dk_scr[hh, 0:nk, :] += _tn(ds, qh)

        @pl.when(i > kb)
        def _():
            update(0, t, t, False)

        @pl.when(i == kb)
        def _():
            update(0, t // 2, t // 2, True)
            update(t // 2, t // 2, t, True)

        @pl.when(i == nb - 1)
        def _():
            dk_ref[:, 0:128] = dk_scr[0]
            dk_ref[:, 128:256] = dk_scr[1]
            dv_ref[...] = dv_scr[...]

    qi_map = lambda j, s, qi, ki: (qi[s], j)
    ki_map = lambda j, s, qi, ki: (ki[s], j)
    grid_spec = pltpu.PrefetchScalarGridSpec(
        num_scalar_prefetch=2, grid=(4, qtab.shape[0]),
        in_specs=[pl.BlockSpec((t, 256), qi_map), pl.BlockSpec((t, 256), ki_map), pl.BlockSpec((t, 128), ki_map),
                  pl.BlockSpec((t, 128), qi_map), pl.BlockSpec((t, 128), qi_map), pl.BlockSpec((t, 256), qi_map)]
        + after_specs,
        out_specs=[pl.BlockSpec((SEQ, 256), lambda j, s, qi, ki: (0, j)), pl.BlockSpec((t, 256), ki_map),
                   pl.BlockSpec((t, 128), ki_map)],
        scratch_shapes=[pltpu.VMEM((2, t, 128), F32), pltpu.VMEM((t, 128), F32)])
    return pl.pallas_call(
        body, name="mla_flash_bwd", grid_spec=grid_spec,
        out_shape=[jax.ShapeDtypeStruct((SEQ, 1024), F32), jax.ShapeDtypeStruct((SEQ, 1024), F32),
                   jax.ShapeDtypeStruct((SEQ, 512), F32)],
        compiler_params=_cparams(),
    )(qtab, ktab, q, k, v, o, do, lse, *after)


def _strided(start, size, d):
    return pl.ds(start, size) if d == 1 else pl.ds(start, size, stride=d)


DIL_ST_FWD, DIL_ST_BWD = 4096, 4096


def _band_keep(g, b, t, nb):
    nbs = SEQ // DIL_DILATIONS[g] // BAND
    row = lax.broadcasted_iota(jnp.int32, (BAND, 2 * BAND), 0)
    col = lax.broadcasted_iota(jnp.int32, (BAND, 2 * BAND), 1)
    cur = (col >= BAND) & (row >= col - BAND)
    prev = (col < BAND) & (col >= row)
    if nbs >= nb:
        if b > 0:
            return cur | prev
        return cur | (prev & ((t * nb) % nbs != 0))
    return cur | prev if b % nbs else cur


def _dil_tok(g, b, t, nb):
    d = DIL_DILATIONS[g]
    nbs = SEQ // d // BAND
    gb = t * nb + b
    return _strided((gb % nbs) * BAND * d + gb // nbs, BAND, d)


def _dil_attn_fwd(p_qkv, rc, rs, g):
    d = DIL_DILATIONS[g]
    sub_len = SEQ // d
    ch = min(sub_len, 512)
    DIL_ST, DIL_NB = DIL_ST_FWD, DIL_ST_FWD // BAND

    def body(p_ref, c_ref, sn_ref, o_ref, l_ref, qkv_ref, x_scr, s_scr, p_scr, o_scr):
        t = pl.program_id(1)

        @pl.when(t == 0)
        def _():
            lanes = _rope_lanes((ch, 128), DIL_ROPE_HALF, 64, 0)
            for tq in range(3):
                qkv_ref[tq, 0, 0:BAND, :] = jnp.zeros((BAND, 128), BF16)
                mult = DIL_SCALE if tq == 0 else 1.0
                for c0 in range(0, SEQ, ch):
                    rows = pl.ds(c0, ch)
                    xv = p_ref[tq, rows, :].astype(F32)
                    x_scr[rows, :] = xv if tq == 2 else _rope_fwd(xv, c_ref[rows, :] * mult, sn_ref[rows, :] * mult,
                                                                   DIL_ROPE_HALF, lanes)
                for r in range(d):
                    for c0 in range(0, sub_len, ch):
                        at = BAND + r * sub_len + c0
                        qkv_ref[tq, 0, at:at + ch, :] = x_scr[_strided(r + c0 * d, ch, d), :].astype(BF16)

        base = t * DIL_ST
        half0 = _head_half((DIL_ST, 128), 0)
        lse_h = []
        for hh in range(2):
            half = _head_half((BAND, 128), hh)
            for b in range(DIL_NB):
                qv = qkv_ref[0, 0, pl.ds(pl.multiple_of(base + (b + 1) * BAND, BAND), BAND), :]
                k2 = qkv_ref[1, 0, pl.ds(pl.multiple_of(base + b * BAND, BAND), 2 * BAND), :]
                sb = _nt(jnp.where(half, qv, jnp.zeros_like(qv)), k2)
                s_scr[b * BAND:(b + 1) * BAND, :] = jnp.where(_band_keep(g, b, t, DIL_NB), sb, NEG)
            s = s_scr[...]
            m = jnp.max(s, axis=-1, keepdims=True)
            pr = jnp.exp(s - m)
            den = jnp.sum(pr, axis=-1, keepdims=True)
            p_scr[...] = pr.astype(BF16)
            for b in range(DIL_NB):
                v2 = qkv_ref[2, 0, pl.ds(pl.multiple_of(base + b * BAND, BAND), 2 * BAND), :]
                o_scr[hh, b * BAND:(b + 1) * BAND, :] = _nn(p_scr[b * BAND:(b + 1) * BAND, :], v2)
            o_scr[hh] = o_scr[hh] / den
            lse_h.append(m + jnp.log(den))
        out = jnp.where(half0, o_scr[0], o_scr[1])
        lse = jnp.where(half0, lse_h[0], lse_h[1])
        for b in range(DIL_NB):
            tok = _dil_tok(g, b, t, DIL_NB)
            o_ref[tok, :] = out[b * BAND:(b + 1) * BAND, :]
            l_ref[tok, :] = lse[b * BAND:(b + 1) * BAND, :]

    tab = pl.BlockSpec((SEQ, 128), lambda pr, t: (0, 0))
    out = pl.BlockSpec((SEQ, 128), lambda pr, t: (0, pr))
    return pl.pallas_call(
        body, name=f"dil_attn_fwd_g{g}", grid=(4, SEQ // DIL_ST),
        in_specs=[pl.BlockSpec((None, 3, SEQ, 128), lambda pr, t: (g * 4 + pr, 0, 0, 0)), tab, tab],
        out_specs=[out, out, pl.BlockSpec((3, 1, BAND + SEQ, 128), lambda pr, t: (0, pr, 0, 0))],
        out_shape=[jax.ShapeDtypeStruct((SEQ, 512), F32), jax.ShapeDtypeStruct((SEQ, 512), F32),
                   jax.ShapeDtypeStruct((3, 4, BAND + SEQ, 128), BF16)],
        scratch_shapes=[pltpu.VMEM((SEQ, 128), F32), pltpu.VMEM((DIL_ST, 2 * BAND), F32),
                        pltpu.VMEM((DIL_ST, 2 * BAND), BF16), pltpu.VMEM((2, DIL_ST, 128), F32)],
        compiler_params=_cparams(),
    )(p_qkv.reshape(12, 3, SEQ, 128), rc, rs)


def _dil_attn_bwd(dp_in, qkv, dyd, yd, lse_all, rc, rs, g, token=None):
    d = DIL_DILATIONS[g]
    sub_len = SEQ // d
    DIL_ST, DIL_NB = DIL_ST_BWD, DIL_ST_BWD // BAND
    nst = SEQ // DIL_ST
    after, after_specs = _after(token)
    ch = 512

    def body(dp_any, q_ref, k_ref, v_ref, do_ref, y_ref, l_ref, c_ref, sn_ref, *rest):
        dp_ref, tok_scr, dk_scr, dv_scr, s_scr, dp_scr, p_scr, ds_scr, do_scr, y_scr, l_scr, dq_scr = rest[-12:]
        del dp_any
        t = pl.program_id(1)
        base = t * DIL_ST

        @pl.when(t == 0)
        def _():
            dk_scr[...] = jnp.zeros_like(dk_scr)
            dv_scr[...] = jnp.zeros_like(dv_scr)

        for b in range(DIL_NB):
            tok = _dil_tok(g, b, t, DIL_NB)
            do_scr[b * BAND:(b + 1) * BAND, :] = do_ref[tok, :]
            y_scr[b * BAND:(b + 1) * BAND, :] = y_ref[tok, :]
            l_scr[b * BAND:(b + 1) * BAND, :] = l_ref[tok, :]
        for hh in range(2):
            half = _head_half((BAND, 128), hh)
            half_st = _head_half((DIL_ST, 128), hh)
            dom = jnp.where(half_st, do_scr[...], 0.0)
            delta = jnp.sum(dom * y_scr[...], axis=-1, keepdims=True)
            lcol = jnp.max(jnp.where(half_st, l_scr[...], NEG), axis=-1, keepdims=True)
            for b in range(DIL_NB):
                rows = slice(b * BAND, (b + 1) * BAND)
                qv = q_ref[0, 0, pl.ds(pl.multiple_of(base + (b + 1) * BAND, BAND), BAND), :]
                band = pl.ds(pl.multiple_of(base + b * BAND, BAND), 2 * BAND)
                sb = _nt(jnp.where(half, qv, jnp.zeros_like(qv)), k_ref[0, 0, band, :])
                s_scr[rows, :] = jnp.where(_band_keep(g, b, t, DIL_NB), sb, NEG)
                dp_scr[rows, :] = _nt(dom[rows, :].astype(BF16), v_ref[0, 0, band, :])
            pr = jnp.exp(s_scr[...] - lcol)
            p_scr[...] = pr.astype(BF16)
            ds_scr[...] = (pr * (dp_scr[...] - delta)).astype(BF16)
            for b in range(DIL_NB):
                rows = slice(b * BAND, (b + 1) * BAND)
                qv = q_ref[0, 0, pl.ds(pl.multiple_of(base + (b + 1) * BAND, BAND), BAND), :]
                band = pl.ds(pl.multiple_of(base + b * BAND, BAND), 2 * BAND)
                dqb = jnp.where(half, _nn(ds_scr[rows, :], k_ref[0, 0, band, :]), 0.0)
                if hh == 0:
                    dq_scr[rows, :] = dqb
                else:
                    dq_scr[rows, :] += dqb
                half2 = _head_half((2 * BAND, 128), hh)
                dk_scr[band, :] += jnp.where(half2, _tn(ds_scr[rows, :], qv), 0.0)
                dv_scr[band, :] += _tn(p_scr[rows, :], dom[rows, :].astype(BF16))
        for b in range(DIL_NB):
            tok_scr[pl.ds(0, 1), _dil_tok(g, b, t, DIL_NB), :] = dq_scr[b * BAND:(b + 1) * BAND, :][None]

        @pl.when(t == nst - 1)
        def _():
            for r in range(d):
                rows = _strided(r, sub_len, d)
                tok_scr[pl.ds(1, 1), rows, :] = dk_scr[BAND + r * sub_len:BAND + (r + 1) * sub_len, :][None]
                tok_scr[pl.ds(2, 1), rows, :] = dv_scr[BAND + r * sub_len:BAND + (r + 1) * sub_len, :][None]
            lanes = _rope_lanes((ch, 128), DIL_ROPE_HALF, 64, 0)
            for c0 in range(0, SEQ, ch):
                rows = slice(c0, c0 + ch)
                cv, sv = c_ref[rows, :], sn_ref[rows, :]
                dp_ref[rows, 0:128] = _rope_bwd(tok_scr[0, rows, :], cv * DIL_SCALE, sv * DIL_SCALE, DIL_ROPE_HALF, lanes).astype(BF16)
                dp_ref[rows, 128:256] = _rope_bwd(tok_scr[1, rows, :], cv, sv, DIL_ROPE_HALF, lanes).astype(BF16)
                dp_ref[rows, 256:384] = tok_scr[2, rows, :].astype(BF16)

    def inp(tq):
        return pl.BlockSpec((1, 1, BAND + SEQ, 128), lambda pr, t: (tq, pr, 0, 0))

    tok_spec = pl.BlockSpec((SEQ, 128), lambda pr, t: (0, pr))
    tab = pl.BlockSpec((SEQ, 128), lambda pr, t: (0, 0))
    st = (DIL_ST, 2 * BAND)
    return pl.pallas_call(
        body, name=f"dil_attn_bwd_g{g}", grid=(4, nst),
        in_specs=[pl.BlockSpec(memory_space=pl.ANY), inp(0), inp(1), inp(2), tok_spec, tok_spec, tok_spec, tab, tab]
        + after_specs,
        out_specs=pl.BlockSpec((SEQ, 384), lambda pr, t: (0, _qkv_block(0, g, pr) // 3)),
        out_shape=jax.ShapeDtypeStruct((SEQ, N_PAD), BF16),
        input_output_aliases={0: 0},
        scratch_shapes=[pltpu.VMEM((3, SEQ, 128), F32),
                        pltpu.VMEM((BAND + SEQ, 128), F32), pltpu.VMEM((BAND + SEQ, 128), F32),
                        pltpu.VMEM(st, F32), pltpu.VMEM(st, F32), pltpu.VMEM(st, BF16), pltpu.VMEM(st, BF16),
                        pltpu.VMEM((DIL_ST, 128), F32), pltpu.VMEM((DIL_ST, 128), F32), pltpu.VMEM((DIL_ST, 128), F32),
                        pltpu.VMEM((DIL_ST, 128), F32)],
        compiler_params=_cparams(),
    )(dp_in, qkv, qkv, qkv, dyd, yd, lse_all, rc, rs, *after)


TAIL_T = 256


def _tail(p, ya, o_g, l_g, x, target, wpm, wpd, wout, post_g):
    tm = TAIL_T

    def body(pgz_ref, ya_ref, o0_ref, o1_ref, o2_ref, l0_ref, l1_ref, l2_ref, x_ref, t_ref,
             wpm_ref, wpd_ref, wout_ref, pg_ref,
             dp_ref, dy_ref, dya_ref, dyd_ref, yd_ref, lse_ref, loss_ref, dgp_ref, dwpm_ref, dwpd_ref, dwout_ref):
        l0, l1, l2 = l0_ref[...], l1_ref[...], l2_ref[...]
        mx = jnp.maximum(jnp.maximum(l0, l1), l2)
        e0, e1, e2 = jnp.exp(l0 - mx), jnp.exp(l1 - mx), jnp.exp(l2 - mx)
        den = e0 + e1 + e2
        yd = (e0 * o0_ref[...] + e1 * o1_ref[...] + e2 * o2_ref[...]) / den
        yd_ref[...] = yd
        lse_ref[...] = mx + jnp.log(den)
        ya = ya_ref[...]

        gm, gd = pgz_ref[:, 0:1024], pgz_ref[:, 1024:2048]
        zm, zd = pgz_ref[:, 2048:2560], pgz_ref[:, 2560:3072]
        szm, szd = _sigmoid(zm), _sigmoid(zd)
        sm, sd = zm * szm, zd * szd
        ua = (ya * sm).astype(BF16)
        ud = (yd * sd).astype(BF16)
        pa = _nn(ua, wpm_ref[...])
        pd = _nn(ud, wpd_ref[...])
        sgm, sgd = _sigmoid(gm), _sigmoid(gd)
        mg = (sgm * pa + sgd * pd).astype(BF16)
        t = _nn(mg, wout_ref[...])
        r3 = lax.rsqrt(jnp.mean(t * t, axis=-1, keepdims=True) + EPS)
        n = t * r3
        pg = pg_ref[...]
        err = x_ref[...] + n * pg - t_ref[...]
        lpart = jnp.sum(err * err, axis=0, keepdims=True)

        dy = err * (1.0 / D_MODEL)
        dy_ref[...] = dy
        gpart = jnp.sum(dy * n, axis=0, keepdims=True)
        dn = dy * pg
        dt = (r3 * (dn - n * jnp.mean(dn * n, axis=-1, keepdims=True))).astype(BF16)
        dmg = _nt(dt, wout_ref[...])
        dpa = (dmg * sgm).astype(BF16)
        dpd = (dmg * sgd).astype(BF16)
        dp_ref[:, 0:1024] = (dmg * pa * sgm * (1.0 - sgm)).astype(BF16)
        dp_ref[:, 1024:2048] = (dmg * pd * sgd * (1.0 - sgd)).astype(BF16)
        dua = _nt(dpa, wpm_ref[...])
        dud = _nt(dpd, wpd_ref[...])
        dya_ref[...] = dua * sm
        dyd_ref[...] = dud * sd
        dp_ref[:, 2048:2560] = (dua * ya * szm * (1.0 + zm * (1.0 - szm))).astype(BF16)
        dp_ref[:, 2560:3072] = (dud * yd * szd * (1.0 + zd * (1.0 - szd))).astype(BF16)

        wpm, wpd, wout = _tn(ua, dpa), _tn(ud, dpd), _tn(mg, dt)

        @pl.when(pl.program_id(0) == 0)
        def _():
            loss_ref[...] = lpart
            dgp_ref[...] = gpart
            dwpm_ref[...] = wpm
            dwpd_ref[...] = wpd
            dwout_ref[...] = wout

        @pl.when(pl.program_id(0) > 0)
        def _():
            loss_ref[...] += lpart
            dgp_ref[...] += gpart
            dwpm_ref[...] += wpm
            dwpd_ref[...] += wpd
            dwout_ref[...] += wout

    def rows(w):
        return pl.BlockSpec((tm, w), lambda i: (i, 0))

    def full(shape):
        return pl.BlockSpec(shape, lambda i: (0, 0))

    def sds(w, dt):
        return jax.ShapeDtypeStruct((SEQ, w), dt)

    return pl.pallas_call(
        body, name="tail", grid=(SEQ // tm,),
        in_specs=[rows(3072), rows(512), rows(512), rows(512), rows(512), rows(512), rows(512), rows(512),
                  rows(1024), rows(1024), full((512, 1024)), full((512, 1024)), full((1024, 1024)), full((1, 1024))],
        out_specs=[rows(3072), rows(1024), rows(512), rows(512), rows(512), rows(512), full((1, 1024)), full((1, 1024)),
                   full((512, 1024)), full((512, 1024)), full((1024, 1024))],
        out_shape=[sds(N_PAD, BF16), sds(1024, F32), sds(512, F32), sds(512, F32), sds(512, F32), sds(512, F32),
                   jax.ShapeDtypeStruct((1, 1024), F32), jax.ShapeDtypeStruct((1, 1024), F32),
                   jax.ShapeDtypeStruct((512, 1024), F32), jax.ShapeDtypeStruct((512, 1024), F32),
                   jax.ShapeDtypeStruct((1024, 1024), F32)],
        compiler_params=_cparams(),
    )(p, ya, o_g[0], o_g[1], o_g[2], l_g[0], l_g[1], l_g[2], x, target, wpm, wpd, wout, post_g)


def _sum_parts(recv, own, me, tr, name):
    n, r, w = recv.shape
    if r % tr:
        return _sum_parts_cols(recv, own, me, name)
    own_spec = (pl.BlockSpec((tr, w), lambda i, me_ref: (i, 0)) if own.ndim == 2
                else pl.BlockSpec((None, tr, w), lambda i, me_ref: (me_ref[0], i, 0)))

    def body(me_ref, p_ref, own_ref, o_ref):
        mine = own_ref[...].astype(F32)
        acc = jnp.zeros((tr, w), F32)
        for s in range(n):
            acc = acc + jnp.where(me_ref[0] == s, mine, p_ref[s].astype(F32))
        o_ref[...] = acc

    return pl.pallas_call(
        body, name=name,
        grid_spec=pltpu.PrefetchScalarGridSpec(
            num_scalar_prefetch=1, grid=(r // tr,),
            in_specs=[pl.BlockSpec((n, tr, w), lambda i, me_ref: (0, i, 0)), own_spec],
            out_specs=pl.BlockSpec((tr, w), lambda i, me_ref: (i, 0))),
        out_shape=jax.ShapeDtypeStruct((r, w), F32),
    )(me.reshape(1), recv, own)


def _sum_parts_cols(recv, own, me, name):
    n, r, w = recv.shape
    tc = 128

    def body(me_ref, p_ref, own_ref, o_ref):
        mine = own_ref[...].astype(F32)
        acc = jnp.zeros((r, tc), F32)
        for s in range(n):
            acc = acc + jnp.where(me_ref[0] == s, mine, p_ref[s].astype(F32))
        o_ref[...] = acc

    return pl.pallas_call(
        body, name=name,
        grid_spec=pltpu.PrefetchScalarGridSpec(
            num_scalar_prefetch=1, grid=(w // tc,),
            in_specs=[pl.BlockSpec((n, r, tc), lambda i, me_ref: (0, 0, i)),
                      pl.BlockSpec((None, r, tc), lambda i, me_ref: (me_ref[0], 0, i))],
            out_specs=pl.BlockSpec((r, tc), lambda i, me_ref: (0, i))),
        out_shape=jax.ShapeDtypeStruct((r, w), F32),
    )(me.reshape(1), recv, own)


def _adamw(w, g, m, v, name):
    lead = w.shape[:-2]
    r, c = w.shape[-2:]
    tr = max([t for t in range(8, 257, 8) if r % t == 0], default=r)
    c1 = 1.0 - ADAM_B1 ** ADAM_STEP
    c2 = 1.0 - ADAM_B2 ** ADAM_STEP

    def body(w_ref, g_ref, m_ref, v_ref, d_ref, nm_ref, nv_ref):
        gv = g_ref[...]
        nm = ADAM_B1 * m_ref[...] + (1.0 - ADAM_B1) * gv
        nv = ADAM_B2 * v_ref[...] + (1.0 - ADAM_B2) * (gv * gv)
        nm_ref[...] = nm
        nv_ref[...] = nv
        d_ref[...] = -ADAM_LR * ((nm / c1) / (jnp.sqrt(nv / c2) + ADAM_EPS) + ADAM_WD * w_ref[...])

    zeros = (0,) * len(lead)
    spec = pl.BlockSpec((1,) * len(lead) + (tr, c), lambda i: zeros + (i, 0))
    sd = jax.ShapeDtypeStruct(w.shape, F32)
    return pl.pallas_call(
        body, name=name, grid=(r // tr,),
        in_specs=[spec] * 4, out_specs=[spec] * 3, out_shape=[sd] * 3,
    )(w, g, m, v)


def _adamw_recv(w, m, v, recv, own, me, name):
    n, r, c = recv.shape
    tr = 128
    c1 = 1.0 - ADAM_B1 ** ADAM_STEP
    c2 = 1.0 - ADAM_B2 ** ADAM_STEP

    def body(me_ref, w_ref, m_ref, v_ref, p_ref, own_ref, d_ref, nm_ref, nv_ref, g_ref):
        mine = own_ref[...].astype(F32)
        gv = jnp.zeros((tr, c), F32)
        for s in range(n):
            gv = gv + jnp.where(me_ref[0] == s, mine, p_ref[s].astype(F32))
        g_ref[0] = gv
        nm = ADAM_B1 * m_ref[0] + (1.0 - ADAM_B1) * gv
        nv = ADAM_B2 * v_ref[0] + (1.0 - ADAM_B2) * (gv * gv)
        nm_ref[0] = nm
        nv_ref[0] = nv
        d_ref[0] = -ADAM_LR * ((nm / c1) / (jnp.sqrt(nv / c2) + ADAM_EPS) + ADAM_WD * w_ref[0])

    full = pl.BlockSpec((1, tr, c), lambda i, me_ref: (0, i, 0))
    sd = jax.ShapeDtypeStruct((1, r, c), F32)
    return pl.pallas_call(
        body, name=name,
        grid_spec=pltpu.PrefetchScalarGridSpec(
            num_scalar_prefetch=1, grid=(r // tr,),
            in_specs=[full, full, full, pl.BlockSpec((n, tr, c), lambda i, me_ref: (0, i, 0)),
                      pl.BlockSpec((None, tr, c), lambda i, me_ref: (me_ref[0], i, 0))],
            out_specs=[full] * 4),
        out_shape=[sd] * 4,
    )(me.reshape(1), w, m, v, recv, own)


def _adamw_in(w_t, m_t, v_t, own_half, swapped, core):
    r, c = SHARD_SHAPES[0]
    tr = max(t for t in range(8, 257, 8) if r % t == 0)
    c1 = 1.0 - ADAM_B1 ** ADAM_STEP
    c2 = 1.0 - ADAM_B2 ** ADAM_STEP

    def body(core_ref, w_ref, m_ref, v_ref, own_ref, sw_ref, d_ref, nm_ref, nv_ref, g_ref):
        own = own_ref[...]
        col_half = lax.broadcasted_iota(jnp.int32, (tr, c), 1) // (c // 2)
        gv = jnp.where(col_half == core_ref[0], jnp.concatenate([own, own], axis=1), sw_ref[...])
        g_ref[0] = gv
        nm = ADAM_B1 * m_ref[0] + (1.0 - ADAM_B1) * gv
        nv = ADAM_B2 * v_ref[0] + (1.0 - ADAM_B2) * (gv * gv)
        nm_ref[0] = nm
        nv_ref[0] = nv
        d_ref[0] = -ADAM_LR * ((nm / c1) / (jnp.sqrt(nv / c2) + ADAM_EPS) + ADAM_WD * w_ref[0])

    full = pl.BlockSpec((1, tr, c), lambda i, core_ref: (0, i, 0))
    sd = jax.ShapeDtypeStruct((1, r, c), F32)
    return pl.pallas_call(
        body, name="adamw_in",
        grid_spec=pltpu.PrefetchScalarGridSpec(
            num_scalar_prefetch=1, grid=(r // tr,),
            in_specs=[full, full, full, pl.BlockSpec((tr, c // 2), lambda i, core_ref: (i, 0)),
                      pl.BlockSpec((tr, c), lambda i, core_ref: (i, 0))],
            out_specs=[full] * 4),
        out_shape=[sd] * 4,
    )(core.reshape(1), w_t, m_t, v_t, own_half, swapped)


ANY = pl.BlockSpec(memory_space=pl.ANY)


def _my_place():
    return lax.axis_index("x"), lax.axis_index("y"), lax.axis_index("c")


HBM = pl.BlockSpec(memory_space=pltpu.HBM)
SEM = pl.BlockSpec(memory_space=pltpu.SEMAPHORE)
DATAFLOW = pltpu.SideEffectType.DATAFLOW_SIDE_EFFECTING


def _near_chips(x, y):
    return [(1 - x, y), (x, 1 - y)]


def _half(mi, hc):
    r, c = SHARD_SHAPES[mi]
    if mi == 0:
        return pl.ds(0, r), pl.ds(pl.multiple_of(hc * (c // 2), 128), c // 2)
    return pl.ds(pl.multiple_of(hc * (r // 2), 16), r // 2), pl.ds(0, c)


def _gather_copies(land_refs, send_sems, recv_sems):
    x, y, c = _my_place()
    out, back = [], []
    for mi in range(N_MATS):
        rows, cols = _half(mi, c)
        mine = land_refs[mi].at[2 * x + y, rows, cols]
        for j, (cx, cy) in enumerate(_near_chips(x, y)):
            sems = dict(send_sem=send_sems.at[mi * 2 + j], recv_sem=recv_sems.at[mi * 2 + j],
                        device_id=(cx, cy, c), device_id_type=MESH)
            out.append(pltpu.make_async_remote_copy(src_ref=mine, dst_ref=mine, **sems))
            got = land_refs[mi].at[2 * cx + cy, rows, cols]
            back.append(pltpu.make_async_remote_copy(src_ref=got, dst_ref=got, **sems))
    return out, back


def _gather_start(landing):
    n = N_MATS

    def body(*refs):
        out, _ = _gather_copies(refs[:n], refs[n], refs[n + 1])
        for cp in out:
            cp.start()
        refs[-1][...] = jnp.zeros_like(refs[-1])

    hbm = [pltpu.HBM(a.shape, a.dtype) for a in landing]
    outs = pl.pallas_call(
        body, name="gather_start",
        out_shape=(pltpu.SemaphoreType.DMA((2 * n,)), pltpu.SemaphoreType.DMA((2 * n,)), *hbm,
                   jax.ShapeDtypeStruct((8, 128), F32)),
        in_specs=[HBM] * n, out_specs=(SEM, SEM, *[HBM] * n, pl.BlockSpec(memory_space=pltpu.VMEM)),
        input_output_aliases={i: 2 + i for i in range(n)},
        compiler_params=pltpu.CompilerParams(has_side_effects=DATAFLOW),
    )(*[pltpu.with_memory_space_constraint(a, pltpu.HBM) for a in landing])
    return outs[:-1], outs[-1]


def _gather_wait(handle, after):
    n = N_MATS

    def body(*refs):
        out, back = _gather_copies(refs[:n], refs[n], refs[n + 1])
        for cp, arrival in zip(out, back):
            cp.wait_send()
            arrival.wait_recv()

    bufs = handle[2:]
    after, after_specs = _after(after)
    res = pl.pallas_call(
        body, name="gather_wait", out_shape=tuple(pltpu.HBM(b.shape, b.dtype) for b in bufs),
        in_specs=[HBM] * n + [SEM, SEM] + after_specs, out_specs=tuple([HBM] * n),
        input_output_aliases={i: i for i in range(n)},
        compiler_params=pltpu.CompilerParams(has_side_effects=DATAFLOW),
    )(*bufs, handle[0], handle[1], *after)
    return list(res)


def _relay_share(gathered):
    n = N_MATS

    def body(*refs):
        out_refs = refs[n:2 * n]
        send_sems, recv_sems = refs[2 * n:]
        x, y, c = _my_place()
        sibling = (x, y, 1 - c)
        relayed = 2 * (x ^ (1 - c)) + (y ^ c)
        relay_to = (x ^ c, y ^ (1 - c), c)
        far = 2 * (1 - x) + (1 - y)
        near = [2 * (1 - x) + y, 2 * x + (1 - y)]

        def copy(k, mi, shard, hc, to):
            blk = out_refs[mi].at[(shard,) + _half(mi, hc)]
            return pltpu.make_async_remote_copy(src_ref=blk, dst_ref=blk, send_sem=send_sems.at[mi * 4 + k],
                                                recv_sem=recv_sems.at[mi * 4 + k], device_id=to, device_id_type=MESH)

        sends = []
        for mi in range(n):
            sends.append(copy(0, mi, relayed, c, relay_to))
            sends += [copy(1 + j, mi, near[j], c, sibling) for j in range(2)]
        for cp in sends:
            cp.start()
        for mi in range(n):
            copy(0, mi, far, c, relay_to).wait_recv()
            cp = copy(3, mi, far, c, sibling)
            cp.start()
            sends.append(cp)
        for mi in range(n):
            for j in range(2):
                copy(1 + j, mi, near[j], 1 - c, sibling).wait_recv()
            copy(3, mi, far, 1 - c, sibling).wait_recv()
        for cp in sends:
            cp.wait_send()

    return pl.pallas_call(
        body, name="relay_share",
        in_specs=[ANY] * n, out_specs=[ANY] * n,
        out_shape=[jax.ShapeDtypeStruct(g.shape, g.dtype) for g in gathered],
        input_output_aliases={i: i for i in range(n)},
        scratch_shapes=[pltpu.SemaphoreType.DMA((4 * n,)), pltpu.SemaphoreType.DMA((4 * n,))],
    )(*gathered)


def _peers(x, y, c):
    out = []
    for k in range(1, 8):
        px, py, pc = x ^ (k >> 2), y ^ ((k >> 1) & 1), c ^ (k & 1)
        out.append((k - 1, (px, py, pc), 4 * px + 2 * py + pc))
    return out


def _exchange_start(parts, name):
    n = len(parts)

    def body(*refs):
        p_refs, land_refs = refs[:n], refs[n:2 * n]
        send_sems, recv_sems, token = refs[2 * n], refs[2 * n + 1], refs[-1]
        x, y, c = _my_place()
        me = 4 * x + 2 * y + c
        for k, dev, peer in _peers(x, y, c):
            for mi in range(n):
                pltpu.make_async_remote_copy(
                    src_ref=p_refs[mi].at[peer], dst_ref=land_refs[mi].at[me], send_sem=send_sems.at[k * n + mi],
                    recv_sem=recv_sems.at[k * n + mi], device_id=dev, device_id_type=MESH).start()
        token[...] = jnp.zeros_like(token)

    hbm = [pltpu.HBM(p.shape, p.dtype) for p in parts]
    outs = pl.pallas_call(
        body, name=name + "_start",
        out_shape=(pltpu.SemaphoreType.DMA((7 * n,)), pltpu.SemaphoreType.DMA((7 * n,)), *hbm, *hbm,
                   jax.ShapeDtypeStruct((8, 128), F32)),
        in_specs=[HBM] * (2 * n), out_specs=(SEM, SEM, *[HBM] * (2 * n), pl.BlockSpec(memory_space=pltpu.VMEM)),
        input_output_aliases={i: 2 + i for i in range(2 * n)},
        compiler_params=pltpu.CompilerParams(has_side_effects=DATAFLOW),
    )(*[pltpu.with_memory_space_constraint(p, pltpu.HBM) for p in parts],
      *[pltpu.with_memory_space_constraint(lax.empty(p.shape, p.dtype), pltpu.HBM) for p in parts])
    return (name, outs[:-1]), outs[-1]


def _exchange_wait(handle, after):
    name, outs = handle
    n = (len(outs) - 2) // 2

    def body(*refs):
        p_refs, land_refs = refs[:n], refs[n:2 * n]
        send_sems, recv_sems = refs[2 * n], refs[2 * n + 1]
        x, y, c = _my_place()
        me = 4 * x + 2 * y + c
        for k, dev, peer in _peers(x, y, c):
            for mi in range(n):
                pltpu.make_async_remote_copy(
                    src_ref=p_refs[mi].at[peer], dst_ref=land_refs[mi].at[me], send_sem=send_sems.at[k * n + mi],
                    recv_sem=recv_sems.at[k * n + mi], device_id=dev, device_id_type=MESH).wait_send()
                slot = land_refs[mi].at[peer]
                pltpu.make_async_remote_copy(
                    src_ref=slot, dst_ref=slot, send_sem=send_sems.at[k * n + mi],
                    recv_sem=recv_sems.at[k * n + mi], device_id=dev, device_id_type=MESH).wait_recv()

    bufs = outs[2:]
    res = pl.pallas_call(
        body, name=name + "_wait", out_shape=tuple(pltpu.HBM(b.shape, b.dtype) for b in bufs),
        in_specs=[HBM] * (2 * n) + [SEM, SEM, ANY], out_specs=tuple([HBM] * (2 * n)),
        input_output_aliases={i: i for i in range(2 * n)},
        compiler_params=pltpu.CompilerParams(has_side_effects=DATAFLOW),
    )(*bufs, outs[0], outs[1], after)
    return list(res[n:])


def _swap_halves(half_in, gvec):
    def body(g_ref, gv_ref, out_ref, rg_ref, send_sems, recv_sems):
        x, y, c = _my_place()
        me = 4 * x + 2 * y + c
        sibling = (x, y, 1 - c)

        def half(hc):
            return out_ref.at[:, pl.ds(pl.multiple_of(hc * 512, 128), 512)]

        sends = [pltpu.make_async_remote_copy(src_ref=g_ref, dst_ref=half(c), send_sem=send_sems.at[7],
                                              recv_sem=recv_sems.at[7], device_id=sibling, device_id_type=MESH)]
        for k, dev, peer in _peers(x, y, c):
            sends.append(pltpu.make_async_remote_copy(src_ref=gv_ref, dst_ref=rg_ref.at[me], send_sem=send_sems.at[k],
                                                      recv_sem=recv_sems.at[k], device_id=dev, device_id_type=MESH))
        for cp in sends:
            cp.start()
        got = half(1 - c)
        pltpu.make_async_remote_copy(src_ref=got, dst_ref=got, send_sem=send_sems.at[7], recv_sem=recv_sems.at[7],
                                     device_id=sibling, device_id_type=MESH).wait_recv()
        for k, dev, peer in _peers(x, y, c):
            got = rg_ref.at[peer]
            pltpu.make_async_remote_copy(src_ref=got, dst_ref=got, send_sem=send_sems.at[k], recv_sem=recv_sems.at[k],
                                         device_id=dev, device_id_type=MESH).wait_recv()
        for cp in sends:
            cp.wait_send()

    return pl.pallas_call(
        body, name="swap_halves",
        in_specs=[ANY, ANY], out_specs=[ANY, ANY],
        out_shape=[jax.ShapeDtypeStruct(SHARD_SHAPES[0], F32), jax.ShapeDtypeStruct((8, 8, N_GVEC), F32)],
        scratch_shapes=[pltpu.SemaphoreType.DMA((8,)), pltpu.SemaphoreType.DMA((8,))],
    )(half_in, gvec)


def _set_slot(arr, block, idx):
    return lax.dynamic_update_slice(arr, block[None], (idx,) + (0,) * block.ndim)


PAD_RUNS = ((6304, 8352, 0), (5280, 6304, COL_Z), (672, 5280, COL_QKV), (0, 640, COL_LAT), (640, 672, COL_LAT + 704))
N_QKV = COL_LAT - COL_QKV


def _qkv_rows_regroup(a, to_padded):
    if to_padded:
        a4 = a.reshape(3, 12, 128, a.shape[1])
        return jnp.stack([a4[0], a4[1], a4[2]], axis=1).reshape(a.shape)
    a4 = a.reshape(12, 3, 128, a.shape[1])
    return jnp.concatenate([a4[:, tq].reshape(N_QKV // 3, a.shape[1]) for tq in range(3)], axis=0)
W_IN_SHARD = 2088


def _full_weights(gathered):
    def cols(a):
        return jnp.concatenate([a[s] for s in range(4)], axis=1)

    w_uq, w_ukv, w_pm, w_pd = [cols(a) for a in gathered[1:5]]
    w_out = gathered[5].reshape(D_MODEL, D_MODEL)
    w_in_t = gathered[0].reshape(4 * W_IN_SHARD, D_MODEL)
    pieces, at = [], 0
    for lo, hi, pad_lo in sorted(PAD_RUNS, key=lambda t: t[2]):
        if pad_lo > at:
            pieces.append(jnp.zeros((pad_lo - at, D_MODEL), w_in_t.dtype))
        pieces.append(_qkv_rows_regroup(w_in_t[lo:hi], True) if pad_lo == COL_QKV else w_in_t[lo:hi])
        at = pad_lo + hi - lo
    pieces.append(jnp.zeros((N_PAD - at, D_MODEL), w_in_t.dtype))
    w_pad_t = jnp.concatenate(pieces, axis=0)
    z32 = jnp.zeros((Q_RANK, 32), w_uq.dtype)
    wuq_pad = jnp.concatenate([t for h in range(MLA_HEADS) for t in (w_uq[:, h * 96:(h + 1) * 96], z32)], axis=1)
    z64 = jnp.zeros((KV_RANK, 64), w_ukv.dtype)
    wk_pad = jnp.concatenate([t for h in range(MLA_HEADS) for t in (w_ukv[:, h * 128:h * 128 + 64], z64)], axis=1)
    wv = jnp.concatenate([w_ukv[:, h * 128 + 64:(h + 1) * 128] for h in range(MLA_HEADS)], axis=1)
    return w_pad_t, wuq_pad, wk_pad, wv, w_pm, w_pd, w_out


W_IN_LAT = 672


def _grad_parts_in_early(dwt_early):
    dwt_early = jnp.concatenate([dwt_early[:COL_QKV], _qkv_rows_regroup(dwt_early[COL_QKV:COL_LAT], False)], axis=0)

    def in_block(s, h):
        cols = slice(h * 512, (h + 1) * 512)
        out = []
        for lo, hi, pad_lo in sorted(PAD_RUNS):
            a_, b_ = max(lo, s * W_IN_SHARD), min(hi, (s + 1) * W_IN_SHARD)
            if a_ < b_:
                out.append(jnp.zeros((b_ - a_, 512), dwt_early.dtype) if pad_lo >= COL_LAT
                           else dwt_early[pad_lo + a_ - lo:pad_lo + b_ - lo, cols])
        return jnp.concatenate(out, axis=0)

    return jnp.stack([in_block(s, h) for s in range(4) for h in range(2)])


def _grad_parts_in_late(dwt_late):
    rows = jnp.concatenate([dwt_late[0:640], dwt_late[704:736]], axis=0)
    zero = jnp.zeros((W_IN_LAT, 512), dwt_late.dtype)
    return jnp.stack([rows[:, 0:512], rows[:, 512:1024]] + [zero] * 6)


def _shard_blocks(m, axis=1):
    n = m.shape[axis] // 4
    cut = (lambda s: m[:, s * n:(s + 1) * n]) if axis == 1 else (lambda s: m[s * n:(s + 1) * n])
    return jnp.stack([cut(s) for s in range(4) for _ in range(2)])


def _grad_parts_mla(dwuq_pad, dwk_pad, dwv):
    d_uq = jnp.concatenate([dwuq_pad[:, h * 128:h * 128 + 96] for h in range(MLA_HEADS)], axis=1)
    d_ukv = jnp.concatenate([t for h in range(MLA_HEADS) for t in (dwk_pad[:, h * 128:h * 128 + 64], dwv[:, h * 64:(h + 1) * 64])],
                            axis=1)
    return [_shard_blocks(d_uq.astype(BF16)), _shard_blocks(d_ukv.astype(BF16))]


def _rope_tables(positions, token=None):
    pos = positions.reshape(SEQ).astype(F32)
    if token is not None:
        pos = pos + token[0, 0]
    lane = jnp.arange(128)

    def table(rot, first, period):
        inv = ROPE_THETA ** (-jnp.arange(0, rot, 2, dtype=F32) / rot)
        half = rot // 2
        off = lane % period - first
        in1, in2 = (off >= 0) & (off < half), (off >= half) & (off < rot)
        inv_lane = jnp.where(in1 | in2, inv[jnp.clip(off % half, 0, half - 1)], 0.0)
        sign = jnp.where(in1, -1.0, 1.0).astype(F32)
        ang = pos[:, None] * inv_lane[None, :]
        return jnp.cos(ang), jnp.sin(ang) * sign[None, :]

    return table(32, 64, 128), table(16, 0, 64)


class _Links:
    def __init__(self, mats, chip, me):
        landing = [_set_slot(lax.empty((4,) + m.shape, m.dtype), m, chip) for m in mats]
        self.gather, self.token = _gather_start(landing)
        self.me, self.sent, self.handles, self.sums, self.raw = me, {}, {}, {}, {}

    def weights(self, after):
        return _relay_share(_gather_wait(self.gather, after))

    def send(self, blocks, name):
        self.sent[name] = blocks
        self.handles[name], token = _exchange_start(blocks, name)
        return token

    def collect(self, name, after, parts):
        recv = _exchange_wait(self.handles[name], after)
        for r, own, part in zip(recv, self.sent[name], parts):
            if part.startswith("in_"):
                self.sums[part] = _sum_parts(r, own, self.me, 64, "sum_grad_" + part)
            else:
                self.raw[part] = (r, own)
        return tuple(self.sums[part] for part in parts if part in self.sums)


def _device_grads(x, positions, target, gains, links):
    pre_g, q_g, kv_g, post_g = gains
    (mc, ms), (dc, ds) = _rope_tables(positions, links.token)
    h = _prenorm_fwd(x, pre_g, links.token)
    w_pad_t, wuq_pad, wk_pad, wv, w_pm, w_pd, w_out = _full_weights(links.weights((h, mc, ms, dc, ds)))

    p_gz = _matmul(h, w_pad_t, "nt", F32, 1024, 1536, 1024, "in_proj_gates", b_cols=(0, COL_QKV // 1536))
    p_qkv = _matmul(h, w_pad_t, "nt", BF16, 1024, 1536, 1024, "in_proj_dilated", b_cols=(COL_QKV // 1536, N_QKV // 1536),
                    lane_blocks=True)
    p_lat = _matmul(h, w_pad_t, "nt", F32, 1024, N_LAT, 1024, "in_proj_latent", b_cols=(COL_LAT // N_LAT, 1))
    q, k, v = _mla_prep_fwd(p_lat, q_g, kv_g, wuq_pad, wk_pad, wv, mc, ms)
    ya, lse_m = _mla_flash_fwd(q, k, v)
    o_g, l_g, qkv = zip(*[_dil_attn_fwd(p_qkv, dc, ds, g) for g in range(3)])
    (dp, dy, dya, dyd, yd, lse_d, loss_cols, dg_post, dwpm, dwpd, dwout) = _tail(
        p_gz, ya, o_g, l_g, x, target, w_pm, w_pd, w_out, post_g)

    for g in range(3):
        dp = _dil_attn_bwd(dp, qkv[g], dyd, yd, lse_d, dc, ds, g)
    dw_early = _matmul(dp, h, "tn", BF16, 1536, 1024, 2048, "dw_in_early", a_cols=(0, COL_LAT // 1536))
    token = links.send([_grad_parts_in_early(dw_early), _shard_blocks(dwpm.astype(BF16)), _shard_blocks(dwpd.astype(BF16)),
                        _shard_blocks(dwout.astype(BF16), axis=0)], "exchange_early")

    dq, dk, dv = _mla_flash_bwd(q, k, v, ya, dya, lse_m, token)
    dp, dwuq_pad, dwk_pad, dwv, dg_q, dg_kv = _mla_prep_bwd(dp, p_lat, dq, dk, dv, q_g, kv_g, wuq_pad, wk_pad, wv, mc, ms)
    dw_late = _matmul(dp, h, "tn", BF16, N_LAT, 1024, 2048, "dw_in_late", a_cols=(COL_LAT // N_LAT, 1))
    token = links.send([_grad_parts_in_late(dw_late)] + _grad_parts_mla(dwuq_pad, dwk_pad, dwv), "exchange_late")
    early = links.collect("exchange_early", dw_late, ("in_early", "pm", "pd", "out"))

    grad_x, dg_pre = _dh_prenorm_bwd(dp, w_pad_t, x, dy, pre_g, (token,) + tuple(early))
    links.collect("exchange_late", grad_x, ("in_late", "uq", "ukv"))

    loss_part = jnp.pad((jnp.sum(loss_cols) * (0.5 / D_MODEL)).reshape(1, 1), ((0, 0), (0, N_GVEC - N_GAINS - 1)))
    gvec = jnp.concatenate([dg_pre, dg_q, dg_kv, dg_post, loss_part], axis=1)
    return grad_x, gvec


def kernel(x, positions, pre_norm_g, w_in, q_norm_g, w_uq, kv_norm_g, w_ukv, w_proj_mla, w_proj_dil, w_out, post_norm_g, loss_target, m_pre_norm_g, m_w_in, m_q_norm_g, m_w_uq, m_kv_norm_g, m_w_ukv, m_w_proj_mla, m_w_proj_dil, m_w_out, m_post_norm_g, v_pre_norm_g, v_w_in, v_q_norm_g, v_w_uq, v_kv_norm_g, v_w_ukv, v_w_proj_mla, v_w_proj_dil, v_w_out, v_post_norm_g):
    xi, yi, ci = _my_place()
    chip, me = 2 * xi + yi, 4 * xi + 2 * yi + ci
    mats = [jnp.swapaxes(w_in, 1, 2)] + [w_uq, w_ukv, w_proj_mla, w_proj_dil, w_out]
    mats = [w.reshape(w.shape[1:]).astype(BF16) for w in mats]
    links = _Links(mats, chip, me)
    gains = (pre_norm_g, q_norm_g, kv_norm_g, post_norm_g)
    grad_x, gvec = _device_grads(x[0], positions, loss_target[0], gains, links)

    sums = links.sums
    in_e = sums["in_early"]
    half_in = jnp.concatenate([in_e[:W_IN_LAT] + jnp.where(chip == 0, sums["in_late"], 0.0), in_e[W_IN_LAT:]], axis=0)
    gvec8 = jnp.pad(gvec, ((0, 7), (0, 0)))
    swapped_in, recv_gains = _swap_halves(half_in, gvec8)
    g_gains = _sum_parts(recv_gains, gvec8, me, 8, "sum_gain_parts")[0:1]
    loss = g_gains[0, N_GAINS]
    sw = lambda a: jnp.swapaxes(a, 1, 2)
    d_in, m_in, v_in, g_in = [sw(o) for o in _adamw_in(sw(w_in), sw(m_w_in), sw(v_w_in), half_in, swapped_in, ci)]
    off = [0, 1024, 1408, 1664, 2688]
    g_gain = [g_gains[:, off[i]:off[i + 1]] for i in range(4)]
    ws = [pre_norm_g, w_in, q_norm_g, w_uq, kv_norm_g, w_ukv, w_proj_mla, w_proj_dil, w_out, post_norm_g]
    ms = [m_pre_norm_g, m_w_in, m_q_norm_g, m_w_uq, m_kv_norm_g, m_w_ukv, m_w_proj_mla, m_w_proj_dil, m_w_out, m_post_norm_g]
    vs = [v_pre_norm_g, v_w_in, v_q_norm_g, v_w_uq, v_kv_norm_g, v_w_ukv, v_w_proj_mla, v_w_proj_dil, v_w_out, v_post_norm_g]
    part_of = [None, "in", None, "uq", None, "ukv", "pm", "pd", "out", None]
    gain_of = iter(g_gain)
    grads, deltas, new_m, new_v = [], [], [], []
    for i, (w, m, v, part) in enumerate(zip(ws, ms, vs, part_of)):
        if part == "in":
            d_, m_, v_, g = d_in, m_in, v_in, g_in
        elif part is not None:
            d_, m_, v_, g = _adamw_recv(w, m, v, *links.raw[part], me, f"adamw_{i}")
        else:
            g = next(gain_of)
            d_, m_, v_ = _adamw(w, g, m, v, f"adamw_{i}")
        grads.append(g)
        deltas.append(d_)
        new_m.append(m_)
        new_v.append(v_)
    return (loss, grad_x.reshape(x.shape), *grads, *deltas, *new_m, *new_v)
```

```python
import jax
import jax.numpy as jnp
from jax import lax
from jax.experimental import pallas as pl
from jax.experimental.pallas import tpu as pltpu

F32 = jnp.float32
BF16 = jnp.bfloat16

SEQ = 4096
D_MODEL = 1024
EPS = 1e-6
ROPE_THETA = 500000.0
MLA_HEADS = 8
Q_RANK = 384
KV_RANK = 256
MLA_SCALE = 96.0 ** -0.5
MLA_ROPE_HALF = 16
DIL_DILATIONS = (1, 4, 16)
DIL_ROPE_HALF = 8
DIL_SCALE = 0.125
BAND = 128

N_LAT = 768
COL_Z, COL_QKV, COL_LAT = 2048, 3072, 7680
N_PAD = 8448


def _qkv_block(tq, g, pr):
    return COL_QKV // 128 + (g * 4 + pr) * 3 + tq

IN_SPLITS = (384, 256, 32, 4608, 512, 512, 1024, 1024)

SHARD_SHAPES = ((2088, 1024), (384, 192), (256, 256), (512, 256), (512, 256), (256, 1024))
N_MATS = len(SHARD_SHAPES)
N_GAINS = 2688
N_GVEC = N_GAINS + 128

ADAM_LR, ADAM_B1, ADAM_B2, ADAM_EPS, ADAM_WD, ADAM_STEP = 0.001, 0.9, 0.999, 1e-08, 0.01, 10

VMEM_LIMIT = 56 * 1024 * 1024
NEG = -1e30
MESH = pl.DeviceIdType.MESH


def _cparams(**kw):
    return pltpu.CompilerParams(vmem_limit_bytes=VMEM_LIMIT, **kw)


def _dot(a, b, dims):
    return lax.dot_general(a, b, (dims, ((), ())), preferred_element_type=F32)


def _nn(a, b):
    return _dot(a, b, ((1,), (0,)))


def _nt(a, b):
    return _dot(a, b, ((1,), (1,)))


def _tn(a, b):
    return _dot(a, b, ((0,), (0,)))


def _rope_lanes(shape, half, period, first):
    lane = lax.broadcasted_iota(jnp.int32, shape, len(shape) - 1) % period
    return (lane >= first) & (lane < first + half), (lane >= first + half) & (lane < first + 2 * half)


def _rope_fwd(x, c, s, half, lanes):
    x1, _ = lanes
    return x * c + jnp.where(x1, pltpu.roll(x, 128 - half, 1), pltpu.roll(x, half, 1)) * s


def _rope_bwd(g, c, s, half, lanes):
    x1, x2 = lanes
    gs = g * s
    return g * c + jnp.where(x2, pltpu.roll(gs, half, 1), jnp.where(x1, pltpu.roll(gs, 128 - half, 1), 0.0))


def _sigmoid(x):
    return 1.0 / (1.0 + jnp.exp(-x))


def _after(token):
    tokens = [t for t in (token if isinstance(token, (tuple, list)) else [token]) if t is not None]
    return tokens, [pl.BlockSpec(memory_space=pl.ANY)] * len(tokens)


def _matmul(a, b, mode, out_dtype, tm, tn, tk, name, token=None, b_cols=None, a_cols=None, lane_blocks=False):
    after, after_specs = _after(token)
    if mode == "nn":
        (m, k), n = a.shape, b.shape[1]
        first = 0
        if b_cols is not None:
            first, n = b_cols[0], b_cols[1] * tn
        a_spec = pl.BlockSpec((tm, tk), lambda j, i, kk: (i, kk))
        b_spec = pl.BlockSpec((tk, tn), lambda j, i, kk: (kk, j + first))
        dot = _nn
    elif mode == "nt":
        (m, k), n = a.shape, b.shape[0]
        first = 0
        if b_cols is not None:
            first, n = b_cols[0], b_cols[1] * tn
        a_spec = pl.BlockSpec((tm, tk), lambda j, i, kk: (i, kk))
        b_spec = pl.BlockSpec((tn, tk), lambda j, i, kk: (j + first, kk))
        dot = _nt
    else:
        (k, m), n = a.shape, b.shape[1]
        first = 0
        if a_cols is not None:
            first, m = a_cols[0], a_cols[1] * tm
        a_spec = pl.BlockSpec((tk, tm), lambda j, i, kk: (kk, i + first))
        b_spec = pl.BlockSpec((tk, tn), lambda j, i, kk: (kk, j))
        dot = _tn
    assert m % tm == 0 and n % tn == 0 and k % tk == 0, (name, m, n, k, tm, tn, tk)
    nk = k // tk

    def write(o_ref, val):
        if lane_blocks:
            for blk in range(tn // 128):
                o_ref[blk] = val[:, blk * 128:(blk + 1) * 128].astype(o_ref.dtype)
        else:
            o_ref[...] = val.astype(o_ref.dtype)

    def body_single(a_ref, b_ref, *rest):
        write(rest[-1], dot(a_ref[...], b_ref[...]))

    def body_accumulate(a_ref, b_ref, *rest):
        o_ref, acc_ref = rest[-2:]
        kk = pl.program_id(2)
        part = dot(a_ref[...], b_ref[...])

        @pl.when(kk == 0)
        def _():
            acc_ref[...] = part

        @pl.when(kk > 0)
        def _():
            acc_ref[...] += part

        @pl.when(kk == nk - 1)
        def _():
            write(o_ref, acc_ref[...])

    if lane_blocks:
        out_spec = pl.BlockSpec((tn // 128, tm, 128), lambda j, i, kk: (j, i, 0))
        out_shape = jax.ShapeDtypeStruct((n // 128, m, 128), out_dtype)
    else:
        out_spec = pl.BlockSpec((tm, tn), lambda j, i, kk: (i, j))
        out_shape = jax.ShapeDtypeStruct((m, n), out_dtype)
    return pl.pallas_call(
        body_single if nk == 1 else body_accumulate, name=name, grid=(n // tn, m // tm, nk),
        in_specs=[a_spec, b_spec] + after_specs,
        out_specs=out_spec, out_shape=out_shape,
        scratch_shapes=[] if nk == 1 else [pltpu.VMEM((tm, tn), F32)],
        compiler_params=_cparams(),
    )(a, b, *after)


def _prenorm_fwd(x, g, token=None):
    tm = 512
    after, after_specs = _after(token)

    def body(x_ref, g_ref, *rest):
        xv = x_ref[...]
        r = lax.rsqrt(jnp.mean(xv * xv, axis=-1, keepdims=True) + EPS)
        rest[-1][...] = (xv * r * g_ref[...]).astype(BF16)

    return pl.pallas_call(
        body, name="prenorm_fwd", grid=(SEQ // tm,),
        in_specs=[pl.BlockSpec((tm, D_MODEL), lambda i: (i, 0)), pl.BlockSpec((1, D_MODEL), lambda i: (0, 0))] + after_specs,
        out_specs=pl.BlockSpec((tm, D_MODEL), lambda i: (i, 0)),
        out_shape=jax.ShapeDtypeStruct((SEQ, D_MODEL), BF16),
    )(x, g, *after)


def _dh_prenorm_bwd(dp, w_pad_t, x, dy, g, token=None):
    tm, tk = 1024, 1408
    nk = N_PAD // tk
    after, after_specs = _after(token)

    nbuf, steps = 3, (SEQ // tm) * nk

    def body(a_hbm, b_hbm, x_ref, dy_ref, g_ref, *rest):
        gx_ref, dg_ref, acc_ref, a_buf, b_buf, sem = rest[-6:]
        i, kk = pl.program_id(0), pl.program_id(1)
        s = i * nk + kk

        def copies(step):
            slot, ii, k2 = step % nbuf, step // nk, step % nk
            return (pltpu.make_async_copy(a_hbm.at[pl.ds(ii * tm, tm), pl.ds(k2 * tk, tk)], a_buf.at[slot],
                                          sem.at[0, slot]),
                    pltpu.make_async_copy(b_hbm.at[pl.ds(k2 * tk, tk), :], b_buf.at[slot], sem.at[1, slot]))

        @pl.when(s == 0)
        def _():
            for first in range(nbuf - 1):
                for c in copies(first):
                    c.start()

        @pl.when(s + nbuf - 1 < steps)
        def _():
            for c in copies(s + nbuf - 1):
                c.start()

        for c in copies(s):
            c.wait()
        slot = s % nbuf
        part = _nn(a_buf[slot], b_buf[slot])

        @pl.when(kk == 0)
        def _():
            acc_ref[...] = part

        @pl.when(kk > 0)
        def _():
            acc_ref[...] += part

        @pl.when(kk == nk - 1)
        def _():
            xv = x_ref[...]
            r = lax.rsqrt(jnp.mean(xv * xv, axis=-1, keepdims=True) + EPS)
            n = xv * r
            dhv = acc_ref[...]
            dn = dhv * g_ref[...]
            gx_ref[...] = dy_ref[...] + r * (dn - n * jnp.mean(dn * n, axis=-1, keepdims=True))
            cols = jnp.sum(dhv * n, axis=0, keepdims=True)

            @pl.when(i == 0)
            def _():
                dg_ref[...] = cols

            @pl.when(i > 0)
            def _():
                dg_ref[...] += cols

    row = pl.BlockSpec((tm, D_MODEL), lambda i, kk: (i, 0))
    vec = pl.BlockSpec((1, D_MODEL), lambda i, kk: (0, 0))
    return pl.pallas_call(
        body, name="dh_prenorm_bwd", grid=(SEQ // tm, nk),
        in_specs=[pl.BlockSpec(memory_space=pl.ANY), pl.BlockSpec(memory_space=pl.ANY), row, row, vec] + after_specs,
        out_specs=[row, vec],
        out_shape=[jax.ShapeDtypeStruct((SEQ, D_MODEL), F32), jax.ShapeDtypeStruct((1, D_MODEL), F32)],
        scratch_shapes=[pltpu.VMEM((tm, D_MODEL), F32), pltpu.VMEM((nbuf, tm, tk), BF16),
                        pltpu.VMEM((nbuf, tk, D_MODEL), BF16), pltpu.SemaphoreType.DMA((2, nbuf))],
        compiler_params=_cparams(dimension_semantics=("arbitrary", "arbitrary")),
    )(dp, w_pad_t, x, dy, g, *after)


def _mla_prep_fwd(p, qg, kvg, wuq, wk, wv, rc, rs):
    tm = 512

    def body(lat_ref, qg_ref, kvg_ref, wuq_ref, wk_ref, wv_ref, c_ref, s_ref, q_ref, k_ref, v_ref):
        c, s = c_ref[...], s_ref[...]
        lanes = _rope_lanes((tm, 128), MLA_ROPE_HALF, 128, 64)
        cq = lat_ref[:, 0:Q_RANK]
        r1 = lax.rsqrt(jnp.mean(cq * cq, axis=-1, keepdims=True) + EPS)
        cqn = (cq * r1 * qg_ref[...]).astype(BF16)
        q = _nn(cqn, wuq_ref[...])
        for h in range(MLA_HEADS):
            sl = slice(h * 128, (h + 1) * 128)
            q_ref[:, sl] = (_rope_fwd(q[:, sl], c, s, MLA_ROPE_HALF, lanes) * MLA_SCALE).astype(BF16)
        ckv = lat_ref[:, Q_RANK:Q_RANK + KV_RANK]
        r2 = lax.rsqrt(jnp.mean(ckv * ckv, axis=-1, keepdims=True) + EPS)
        ckvn = (ckv * r2 * kvg_ref[...]).astype(BF16)
        krr = _rope_fwd(lat_ref[:, Q_RANK + KV_RANK:N_LAT], c, s, MLA_ROPE_HALF, lanes)
        kn = _nn(ckvn, wk_ref[...])
        for h in range(MLA_HEADS):
            sl = slice(h * 128, (h + 1) * 128)
            k_ref[:, sl] = (kn[:, sl] + krr).astype(BF16)
        v_ref[...] = _nn(ckvn, wv_ref[...]).astype(BF16)

    def full(shape):
        return pl.BlockSpec(shape, lambda i: (0, 0))

    def rows(w):
        return pl.BlockSpec((tm, w), lambda i: (i, 0))

    return pl.pallas_call(
        body, name="mla_prep_fwd", grid=(SEQ // tm,),
        in_specs=[pl.BlockSpec((tm, N_LAT), lambda i: (i, 0)),
                  full((1, Q_RANK)), full((1, KV_RANK)), full((Q_RANK, 1024)), full((KV_RANK, 1024)),
                  full((KV_RANK, 512)), rows(128), rows(128)],
        out_specs=[rows(1024), rows(1024), rows(512)],
        out_shape=[jax.ShapeDtypeStruct((SEQ, 1024), BF16), jax.ShapeDtypeStruct((SEQ, 1024), BF16),
                   jax.ShapeDtypeStruct((SEQ, 512), BF16)],
        compiler_params=_cparams(),
    )(p, qg, kvg, wuq, wk, wv, rc, rs)


def _mla_prep_bwd(dp_in, p, dq, dk, dv, qg, kvg, wuq, wk, wv, rc, rs):
    tm = 512

    def body(dp_any, lat_ref, dq_ref, dk_ref, dv_ref, qg_ref, kvg_ref, wuq_ref, wk_ref, wv_ref,
             c_ref, s_ref, dp_ref, dwuq_ref, dwk_ref, dwv_ref, dgq_ref, dgkv_ref, dqb_ref, dkb_ref):
        del dp_any
        c, s = c_ref[...], s_ref[...]
        lanes = _rope_lanes((tm, 128), MLA_ROPE_HALF, 128, 64)
        lane = lax.broadcasted_iota(jnp.int32, (tm, 128), 1)
        dkr = jnp.zeros((tm, 128), F32)
        for h in range(MLA_HEADS):
            sl = slice(h * 128, (h + 1) * 128)
            dqb_ref[:, sl] = _rope_bwd(dq_ref[:, sl] * MLA_SCALE, c, s, MLA_ROPE_HALF, lanes).astype(BF16)
            dkh = dk_ref[:, sl]
            dkr = dkr + dkh
            dkb_ref[:, sl] = jnp.where(lane < 64, dkh, 0.0).astype(BF16)
        dkr = jnp.where((lane >= 64) & (lane < 96), dkr, 0.0)
        dkr = _rope_bwd(dkr, c, s, MLA_ROPE_HALF, lanes)
        dvb = dv_ref[...].astype(BF16)

        cq = lat_ref[:, 0:Q_RANK]
        r1 = lax.rsqrt(jnp.mean(cq * cq, axis=-1, keepdims=True) + EPS)
        n1 = cq * r1
        dcqn = _nt(dqb_ref[...], wuq_ref[...])
        dn1 = dcqn * qg_ref[...]
        dcq = r1 * (dn1 - n1 * jnp.mean(dn1 * n1, axis=-1, keepdims=True))
        pq = jnp.sum(dcqn * n1, axis=0, keepdims=True)

        ckv = lat_ref[:, Q_RANK:Q_RANK + KV_RANK]
        r2 = lax.rsqrt(jnp.mean(ckv * ckv, axis=-1, keepdims=True) + EPS)
        n2 = ckv * r2
        dckvn = _nt(dkb_ref[...], wk_ref[...]) + _nt(dvb, wv_ref[...])
        dn2 = dckvn * kvg_ref[...]
        dckv = r2 * (dn2 - n2 * jnp.mean(dn2 * n2, axis=-1, keepdims=True))
        pkv = jnp.sum(dckvn * n2, axis=0, keepdims=True)
        cqn = (n1 * qg_ref[...]).astype(BF16)
        ckvn = (n2 * kvg_ref[...]).astype(BF16)
        wq, wk_, wv_ = _tn(cqn, dqb_ref[...]), _tn(ckvn, dkb_ref[...]), _tn(ckvn, dvb)

        dp_ref[:, 0:Q_RANK] = dcq.astype(BF16)
        dp_ref[:, Q_RANK:Q_RANK + KV_RANK] = dckv.astype(BF16)
        dp_ref[:, Q_RANK + KV_RANK:N_LAT] = dkr.astype(BF16)

        @pl.when(pl.program_id(0) == 0)
        def _():
            dgq_ref[...] = pq
            dgkv_ref[...] = pkv
            dwuq_ref[...] = wq
            dwk_ref[...] = wk_
            dwv_ref[...] = wv_

        @pl.when(pl.program_id(0) > 0)
        def _():
            dgq_ref[...] += pq
            dgkv_ref[...] += pkv
            dwuq_ref[...] += wq
            dwk_ref[...] += wk_
            dwv_ref[...] += wv_

    def full(shape):
        return pl.BlockSpec(shape, lambda i: (0, 0))

    def rows(w):
        return pl.BlockSpec((tm, w), lambda i: (i, 0))

    lat = pl.BlockSpec((tm, N_LAT), lambda i: (i, 0))
    dlat = pl.BlockSpec((tm, N_LAT), lambda i: (i, COL_LAT // N_LAT))
    return pl.pallas_call(
        body, name="mla_prep_bwd", grid=(SEQ // tm,),
        in_specs=[pl.BlockSpec(memory_space=pl.ANY), lat, rows(1024), rows(1024), rows(512),
                  full((1, Q_RANK)), full((1, KV_RANK)), full((Q_RANK, 1024)), full((KV_RANK, 1024)),
                  full((KV_RANK, 512)), rows(128), rows(128)],
        out_specs=[dlat, full((Q_RANK, 1024)), full((KV_RANK, 1024)), full((KV_RANK, 512)),
                   full((1, Q_RANK)), full((1, KV_RANK))],
        out_shape=[jax.ShapeDtypeStruct((SEQ, N_PAD), BF16), jax.ShapeDtypeStruct((Q_RANK, 1024), F32),
                   jax.ShapeDtypeStruct((KV_RANK, 1024), F32), jax.ShapeDtypeStruct((KV_RANK, 512), F32),
                   jax.ShapeDtypeStruct((1, Q_RANK), F32), jax.ShapeDtypeStruct((1, KV_RANK), F32)],
        input_output_aliases={0: 0},
        scratch_shapes=[pltpu.VMEM((tm, 1024), BF16), pltpu.VMEM((tm, 1024), BF16)],
        compiler_params=_cparams(),
    )(dp_in, p, dq, dk, dv, qg, kvg, wuq, wk, wv, rc, rs)


FLASH_T = 1024


def _head_half(shape, hh):
    lane = lax.broadcasted_iota(jnp.int32, shape, 1)
    return (lane < 64) if hh == 0 else (lane >= 64)


def _diag_keep(nr, nk):
    row = lax.broadcasted_iota(jnp.int32, (nr, nk), 0)
    col = lax.broadcasted_iota(jnp.int32, (nr, nk), 1)
    return row + (nk - nr) >= col


def _tri_steps(nb, q_major):
    if q_major:
        pairs = [(i, kb) for i in range(nb) for kb in range(i + 1)]
    else:
        pairs = [(i, kb) for kb in range(nb) for i in range(kb, nb)]
    return jnp.asarray([p[0] for p in pairs], jnp.int32), jnp.asarray([p[1] for p in pairs], jnp.int32)


def _mla_flash_fwd(q, k, v):
    t = FLASH_T
    nb = SEQ // t
    qtab, ktab = _tri_steps(nb, True)

    def body(qi_ref, ki_ref, q_ref, k_ref, v_ref, o_ref, lse_ref, m_scr, l_scr, acc_scr):
        step = pl.program_id(1)
        i, kb = qi_ref[step], ki_ref[step]

        @pl.when(kb == 0)
        def _():
            m_scr[...] = jnp.full_like(m_scr, NEG)
            l_scr[...] = jnp.zeros_like(l_scr)
            acc_scr[...] = jnp.zeros_like(acc_scr)

        def update(r0, nr, nk, diagonal):
            rs = slice(r0, r0 + nr)
            vv = v_ref[0:nk, :]
            for hh in range(2):
                sl = slice(hh * 128, (hh + 1) * 128)
                s = _nt(q_ref[rs, sl], k_ref[0:nk, sl])
                if diagonal:
                    s = jnp.where(_diag_keep(nr, nk), s, NEG)
                m_prev = m_scr[hh, rs, :]
                m_new = jnp.maximum(m_prev, jnp.max(s, axis=-1, keepdims=True))
                pr = jnp.exp(s - jnp.tile(m_new, (1, nk // 128)))
                alpha = jnp.exp(m_prev - m_new)
                l_scr[hh, rs, :] = alpha * l_scr[hh, rs, :] + jnp.sum(pr, axis=-1, keepdims=True)
                acc_scr[hh, rs, :] = alpha * acc_scr[hh, rs, :] + _nn(pr.astype(BF16), vv)
                m_scr[hh, rs, :] = m_new

        @pl.when(kb < i)
        def _():
            update(0, t, t, False)

        @pl.when(kb == i)
        def _():
            update(0, t // 2, t // 2, True)
            update(t // 2, t // 2, t, True)
            o0 = acc_scr[0] / l_scr[0]
            o1 = acc_scr[1] / l_scr[1]
            o_ref[...] = jnp.where(_head_half((t, 128), 0), o0, o1)
            for hh in range(2):
                lse_ref[:, hh * 128:(hh + 1) * 128] = m_scr[hh] + jnp.log(l_scr[hh])

    grid_spec = pltpu.PrefetchScalarGridSpec(
        num_scalar_prefetch=2, grid=(4, qtab.shape[0]),
        in_specs=[pl.BlockSpec((t, 256), lambda j, s, qi, ki: (qi[s], j)),
                  pl.BlockSpec((t, 256), lambda j, s, qi, ki: (ki[s], j)),
                  pl.BlockSpec((t, 128), lambda j, s, qi, ki: (ki[s], j))],
        out_specs=[pl.BlockSpec((t, 128), lambda j, s, qi, ki: (qi[s], j)),
                   pl.BlockSpec((t, 256), lambda j, s, qi, ki: (qi[s], j))],
        scratch_shapes=[pltpu.VMEM((2, t, 128), F32), pltpu.VMEM((2, t, 128), F32), pltpu.VMEM((2, t, 128), F32)])
    return pl.pallas_call(
        body, name="mla_flash_fwd", grid_spec=grid_spec,
        out_shape=[jax.ShapeDtypeStruct((SEQ, 512), F32), jax.ShapeDtypeStruct((SEQ, 1024), F32)],
        compiler_params=_cparams(),
    )(qtab, ktab, q, k, v)


def _mla_flash_bwd(q, k, v, o, do, lse, token=None):
    t = FLASH_T
    nb = SEQ // t
    qtab, ktab = _tri_steps(nb, False)
    after, after_specs = _after(token)

    def body(qi_ref, ki_ref, q_ref, k_ref, v_ref, o_ref, do_ref, lse_ref, *rest):
        dq_ref, dk_ref, dv_ref, dk_scr, dv_scr = rest[-5:]
        step = pl.program_id(1)
        i, kb = qi_ref[step], ki_ref[step]

        @pl.when(step == 0)
        def _():
            dq_ref[...] = jnp.zeros_like(dq_ref)

        @pl.when(i == kb)
        def _():
            dk_scr[...] = jnp.zeros_like(dk_scr)
            dv_scr[...] = jnp.zeros_like(dv_scr)

        def update(r0, nr, nk, diagonal):
            rs = slice(r0, r0 + nr)
            vv = v_ref[0:nk, :]
            ov = o_ref[rs, :]
            dov = do_ref[rs, :]
            rows = pl.ds(pl.multiple_of(i * t + r0, t // 2), nr)
            for hh in range(2):
                sl = slice(hh * 128, (hh + 1) * 128)
                qh, kh = q_ref[rs, sl], k_ref[0:nk, sl]
                s = _nt(qh, kh)
                if diagonal:
                    s = jnp.where(_diag_keep(nr, nk), s, NEG)
                pr = jnp.exp(s - jnp.tile(lse_ref[rs, sl], (1, nk // 128)))
                dom = jnp.where(_head_half((nr, 128), hh), dov, 0.0)
                domb = dom.astype(BF16)
                dv_scr[0:nk, :] += _tn(pr.astype(BF16), domb)
                dpr = _nt(domb, vv)
                delta = jnp.sum(dom * ov, axis=-1, keepdims=True)
                ds = (pr * (dpr - delta)).astype(BF16)
                dq_ref[rows, sl] += _nn(ds, kh)
                dk_scr[hh, 0:nk, :] += _tn(ds, qh)

        @pl.when(i > kb)
        def _():
            update(0, t, t, False)

        @pl.when(i == kb)
        def _():
            update(0, t // 2, t // 2, True)
            update(t // 2, t // 2, t, True)

        @pl.when(i == nb - 1)
        def _():
            dk_ref[:, 0:128] = dk_scr[0]
            dk_ref[:, 128:256] = dk_scr[1]
            dv_ref[...] = dv_scr[...]

    qi_map = lambda j, s, qi, ki: (qi[s], j)
    ki_map = lambda j, s, qi, ki: (ki[s], j)
    grid_spec = pltpu.PrefetchScalarGridSpec(
        num_scalar_prefetch=2, grid=(4, qtab.shape[0]),
        in_specs=[pl.BlockSpec((t, 256), qi_map), pl.BlockSpec((t, 256), ki_map), pl.BlockSpec((t, 128), ki_map),
                  pl.BlockSpec((t, 128), qi_map), pl.BlockSpec((t, 128), qi_map), pl.BlockSpec((t, 256), qi_map)]
        + after_specs,
        out_specs=[pl.BlockSpec((SEQ, 256), lambda j, s, qi, ki: (0, j)), pl.BlockSpec((t, 256), ki_map),
                   pl.BlockSpec((t, 128), ki_map)],
        scratch_shapes=[pltpu.VMEM((2, t, 128), F32), pltpu.VMEM((t, 128), F32)])
    return pl.pallas_call(
        body, name="mla_flash_bwd", grid_spec=grid_spec,
        out_shape=[jax.ShapeDtypeStruct((SEQ, 1024), F32), jax.ShapeDtypeStruct((SEQ, 1024), F32),
                   jax.ShapeDtypeStruct((SEQ, 512), F32)],
        compiler_params=_cparams(),
    )(qtab, ktab, q, k, v, o, do, lse, *after)


def _strided(start, size, d):
    return pl.ds(start, size) if d == 1 else pl.ds(start, size, stride=d)


DIL_ST_FWD, DIL_ST_BWD = 4096, 4096


def _band_keep(g, b, t, nb):
    nbs = SEQ // DIL_DILATIONS[g] // BAND
    row = lax.broadcasted_iota(jnp.int32, (BAND, 2 * BAND), 0)
    col = lax.broadcasted_iota(jnp.int32, (BAND, 2 * BAND), 1)
    cur = (col >= BAND) & (row >= col - BAND)
    prev = (col < BAND) & (col >= row)
    if nbs >= nb:
        if b > 0:
            return cur | prev
        return cur | (prev & ((t * nb) % nbs != 0))
    return cur | prev if b % nbs else cur


def _dil_tok(g, b, t, nb):
    d = DIL_DILATIONS[g]
    nbs = SEQ // d // BAND
    gb = t * nb + b
    return _strided((gb % nbs) * BAND * d + gb // nbs, BAND, d)


def _dil_attn_fwd(p_qkv, rc, rs, g):
    d = DIL_DILATIONS[g]
    sub_len = SEQ // d
    ch = min(sub_len, 512)
    DIL_ST, DIL_NB = DIL_ST_FWD, DIL_ST_FWD // BAND

    def body(p_ref, c_ref, sn_ref, o_ref, l_ref, qkv_ref, x_scr, s_scr, p_scr, o_scr):
        t = pl.program_id(1)

        @pl.when(t == 0)
        def _():
            lanes = _rope_lanes((ch, 128), DIL_ROPE_HALF, 64, 0)
            for tq in range(3):
                qkv_ref[tq, 0, 0:BAND, :] = jnp.zeros((BAND, 128), BF16)
                mult = DIL_SCALE if tq == 0 else 1.0
                for c0 in range(0, SEQ, ch):
                    rows = pl.ds(c0, ch)
                    xv = p_ref[tq, rows, :].astype(F32)
                    x_scr[rows, :] = xv if tq == 2 else _rope_fwd(xv, c_ref[rows, :] * mult, sn_ref[rows, :] * mult,
                                                                   DIL_ROPE_HALF, lanes)
                for r in range(d):
                    for c0 in range(0, sub_len, ch):
                        at = BAND + r * sub_len + c0
                        qkv_ref[tq, 0, at:at + ch, :] = x_scr[_strided(r + c0 * d, ch, d), :].astype(BF16)

        base = t * DIL_ST
        half0 = _head_half((DIL_ST, 128), 0)
        lse_h = []
        for hh in range(2):
            half = _head_half((BAND, 128), hh)
            for b in range(DIL_NB):
                qv = qkv_ref[0, 0, pl.ds(pl.multiple_of(base + (b + 1) * BAND, BAND), BAND), :]
                k2 = qkv_ref[1, 0, pl.ds(pl.multiple_of(base + b * BAND, BAND), 2 * BAND), :]
                sb = _nt(jnp.where(half, qv, jnp.zeros_like(qv)), k2)
                s_scr[b * BAND:(b + 1) * BAND, :] = jnp.where(_band_keep(g, b, t, DIL_NB), sb, NEG)
            s = s_scr[...]
            m = jnp.max(s, axis=-1, keepdims=True)
            pr = jnp.exp(s - m)
            den = jnp.sum(pr, axis=-1, keepdims=True)
            p_scr[...] = pr.astype(BF16)
            for b in range(DIL_NB):
                v2 = qkv_ref[2, 0, pl.ds(pl.multiple_of(base + b * BAND, BAND), 2 * BAND), :]
                o_scr[hh, b * BAND:(b + 1) * BAND, :] = _nn(p_scr[b * BAND:(b + 1) * BAND, :], v2)
            o_scr[hh] = o_scr[hh] / den
            lse_h.append(m + jnp.log(den))
        out = jnp.where(half0, o_scr[0], o_scr[1])
        lse = jnp.where(half0, lse_h[0], lse_h[1])
        for b in range(DIL_NB):
            tok = _dil_tok(g, b, t, DIL_NB)
            o_ref[tok, :] = out[b * BAND:(b + 1) * BAND, :]
            l_ref[tok, :] = lse[b * BAND:(b + 1) * BAND, :]

    tab = pl.BlockSpec((SEQ, 128), lambda pr, t: (0, 0))
    out = pl.BlockSpec((SEQ, 128), lambda pr, t: (0, pr))
    return pl.pallas_call(
        body, name=f"dil_attn_fwd_g{g}", grid=(4, SEQ // DIL_ST),
        in_specs=[pl.BlockSpec((None, 3, SEQ, 128), lambda pr, t: (g * 4 + pr, 0, 0, 0)), tab, tab],
        out_specs=[out, out, pl.BlockSpec((3, 1, BAND + SEQ, 128), lambda pr, t: (0, pr, 0, 0))],
        out_shape=[jax.ShapeDtypeStruct((SEQ, 512), F32), jax.ShapeDtypeStruct((SEQ, 512), F32),
                   jax.ShapeDtypeStruct((3, 4, BAND + SEQ, 128), BF16)],
        scratch_shapes=[pltpu.VMEM((SEQ, 128), F32), pltpu.VMEM((DIL_ST, 2 * BAND), F32),
                        pltpu.VMEM((DIL_ST, 2 * BAND), BF16), pltpu.VMEM((2, DIL_ST, 128), F32)],
        compiler_params=_cparams(),
    )(p_qkv.reshape(12, 3, SEQ, 128), rc, rs)


def _dil_attn_bwd(dp_in, qkv, dyd, yd, lse_all, rc, rs, g, token=None):
    d = DIL_DILATIONS[g]
    sub_len = SEQ // d
    DIL_ST, DIL_NB = DIL_ST_BWD, DIL_ST_BWD // BAND
    nst = SEQ // DIL_ST
    after, after_specs = _after(token)
    ch = 512

    def body(dp_any, q_ref, k_ref, v_ref, do_ref, y_ref, l_ref, c_ref, sn_ref, *rest):
        dp_ref, tok_scr, dk_scr, dv_scr, s_scr, dp_scr, p_scr, ds_scr, do_scr, y_scr, l_scr, dq_scr = rest[-12:]
        del dp_any
        t = pl.program_id(1)
        base = t * DIL_ST

        @pl.when(t == 0)
        def _():
            dk_scr[...] = jnp.zeros_like(dk_scr)
            dv_scr[...] = jnp.zeros_like(dv_scr)

        for b in range(DIL_NB):
            tok = _dil_tok(g, b, t, DIL_NB)
            do_scr[b * BAND:(b + 1) * BAND, :] = do_ref[tok, :]
            y_scr[b * BAND:(b + 1) * BAND, :] = y_ref[tok, :]
            l_scr[b * BAND:(b + 1) * BAND, :] = l_ref[tok, :]
        for hh in range(2):
            half = _head_half((BAND, 128), hh)
            half_st = _head_half((DIL_ST, 128), hh)
            dom = jnp.where(half_st, do_scr[...], 0.0)
            delta = jnp.sum(dom * y_scr[...], axis=-1, keepdims=True)
            lcol = jnp.max(jnp.where(half_st, l_scr[...], NEG), axis=-1, keepdims=True)
            for b in range(DIL_NB):
                rows = slice(b * BAND, (b + 1) * BAND)
                qv = q_ref[0, 0, pl.ds(pl.multiple_of(base + (b + 1) * BAND, BAND), BAND), :]
                band = pl.ds(pl.multiple_of(base + b * BAND, BAND), 2 * BAND)
                sb = _nt(jnp.where(half, qv, jnp.zeros_like(qv)), k_ref[0, 0, band, :])
                s_scr[rows, :] = jnp.where(_band_keep(g, b, t, DIL_NB), sb, NEG)
                dp_scr[rows, :] = _nt(dom[rows, :].astype(BF16), v_ref[0, 0, band, :])
            pr = jnp.exp(s_scr[...] - lcol)
            p_scr[...] = pr.astype(BF16)
            ds_scr[...] = (pr * (dp_scr[...] - delta)).astype(BF16)
            for b in range(DIL_NB):
                rows = slice(b * BAND, (b + 1) * BAND)
                qv = q_ref[0, 0, pl.ds(pl.multiple_of(base + (b + 1) * BAND, BAND), BAND), :]
                band = pl.ds(pl.multiple_of(base + b * BAND, BAND), 2 * BAND)
                dqb = jnp.where(half, _nn(ds_scr[rows, :], k_ref[0, 0, band, :]), 0.0)
                if hh == 0:
                    dq_scr[rows, :] = dqb
                else:
                    dq_scr[rows, :] += dqb
                half2 = _head_half((2 * BAND, 128), hh)
                dk_scr[band, :] += jnp.where(half2, _tn(ds_scr[rows, :], qv), 0.0)
                dv_scr[band, :] += _tn(p_scr[rows, :], dom[rows, :].astype(BF16))
        for b in range(DIL_NB):
            tok_scr[pl.ds(0, 1), _dil_tok(g, b, t, DIL_NB), :] = dq_scr[b * BAND:(b + 1) * BAND, :][None]

        @pl.when(t == nst - 1)
        def _():
            for r in range(d):
                rows = _strided(r, sub_len, d)
                tok_scr[pl.ds(1, 1), rows, :] = dk_scr[BAND + r * sub_len:BAND + (r + 1) * sub_len, :][None]
                tok_scr[pl.ds(2, 1), rows, :] = dv_scr[BAND + r * sub_len:BAND + (r + 1) * sub_len, :][None]
            lanes = _rope_lanes((ch, 128), DIL_ROPE_HALF, 64, 0)
            for c0 in range(0, SEQ, ch):
                rows = slice(c0, c0 + ch)
                cv, sv = c_ref[rows, :], sn_ref[rows, :]
                dp_ref[rows, 0:128] = _rope_bwd(tok_scr[0, rows, :], cv * DIL_SCALE, sv * DIL_SCALE, DIL_ROPE_HALF, lanes).astype(BF16)
                dp_ref[rows, 128:256] = _rope_bwd(tok_scr[1, rows, :], cv, sv, DIL_ROPE_HALF, lanes).astype(BF16)
                dp_ref[rows, 256:384] = tok_scr[2, rows, :].astype(BF16)

    def inp(tq):
        return pl.BlockSpec((1, 1, BAND + SEQ, 128), lambda pr, t: (tq, pr, 0, 0))

    tok_spec = pl.BlockSpec((SEQ, 128), lambda pr, t: (0, pr))
    tab = pl.BlockSpec((SEQ, 128), lambda pr, t: (0, 0))
    st = (DIL_ST, 2 * BAND)
    return pl.pallas_call(
        body, name=f"dil_attn_bwd_g{g}", grid=(4, nst),
        in_specs=[pl.BlockSpec(memory_space=pl.ANY), inp(0), inp(1), inp(2), tok_spec, tok_spec, tok_spec, tab, tab]
        + after_specs,
        out_specs=pl.BlockSpec((SEQ, 384), lambda pr, t: (0, _qkv_block(0, g, pr) // 3)),
        out_shape=jax.ShapeDtypeStruct((SEQ, N_PAD), BF16),
        input_output_aliases={0: 0},
        scratch_shapes=[pltpu.VMEM((3, SEQ, 128), F32),
                        pltpu.VMEM((BAND + SEQ, 128), F32), pltpu.VMEM((BAND + SEQ, 128), F32),
                        pltpu.VMEM(st, F32), pltpu.VMEM(st, F32), pltpu.VMEM(st, BF16), pltpu.VMEM(st, BF16),
                        pltpu.VMEM((DIL_ST, 128), F32), pltpu.VMEM((DIL_ST, 128), F32), pltpu.VMEM((DIL_ST, 128), F32),
                        pltpu.VMEM((DIL_ST, 128), F32)],
        compiler_params=_cparams(),
    )(dp_in, qkv, qkv, qkv, dyd, yd, lse_all, rc, rs, *after)


TAIL_T = 256


def _tail(p, ya, o_g, l_g, x, target, wpm, wpd, wout, post_g):
    tm = TAIL_T

    def body(pgz_ref, ya_ref, o0_ref, o1_ref, o2_ref, l0_ref, l1_ref, l2_ref, x_ref, t_ref,
             wpm_ref, wpd_ref, wout_ref, pg_ref,
             dp_ref, dy_ref, dya_ref, dyd_ref, yd_ref, lse_ref, loss_ref, dgp_ref, dwpm_ref, dwpd_ref, dwout_ref):
        l0, l1, l2 = l0_ref[...], l1_ref[...], l2_ref[...]
        mx = jnp.maximum(jnp.maximum(l0, l1), l2)
        e0, e1, e2 = jnp.exp(l0 - mx), jnp.exp(l1 - mx), jnp.exp(l2 - mx)
        den = e0 + e1 + e2
        yd = (e0 * o0_ref[...] + e1 * o1_ref[...] + e2 * o2_ref[...]) / den
        yd_ref[...] = yd
        lse_ref[...] = mx + jnp.log(den)
        ya = ya_ref[...]

        gm, gd = pgz_ref[:, 0:1024], pgz_ref[:, 1024:2048]
        zm, zd = pgz_ref[:, 2048:2560], pgz_ref[:, 2560:3072]
        szm, szd = _sigmoid(zm), _sigmoid(zd)
        sm, sd = zm * szm, zd * szd
        ua = (ya * sm).astype(BF16)
        ud = (yd * sd).astype(BF16)
        pa = _nn(ua, wpm_ref[...])
        pd = _nn(ud, wpd_ref[...])
        sgm, sgd = _sigmoid(gm), _sigmoid(gd)
        mg = (sgm * pa + sgd * pd).astype(BF16)
        t = _nn(mg, wout_ref[...])
        r3 = lax.rsqrt(jnp.mean(t * t, axis=-1, keepdims=True) + EPS)
        n = t * r3
        pg = pg_ref[...]
        err = x_ref[...] + n * pg - t_ref[...]
        lpart = jnp.sum(err * err, axis=0, keepdims=True)

        dy = err * (1.0 / D_MODEL)
        dy_ref[...] = dy
        gpart = jnp.sum(dy * n, axis=0, keepdims=True)
        dn = dy * pg
        dt = (r3 * (dn - n * jnp.mean(dn * n, axis=-1, keepdims=True))).astype(BF16)
        dmg = _nt(dt, wout_ref[...])
        dpa = (dmg * sgm).astype(BF16)
        dpd = (dmg * sgd).astype(BF16)
        dp_ref[:, 0:1024] = (dmg * pa * sgm * (1.0 - sgm)).astype(BF16)
        dp_ref[:, 1024:2048] = (dmg * pd * sgd * (1.0 - sgd)).astype(BF16)
        dua = _nt(dpa, wpm_ref[...])
        dud = _nt(dpd, wpd_ref[...])
        dya_ref[...] = dua * sm
        dyd_ref[...] = dud * sd
        dp_ref[:, 2048:2560] = (dua * ya * szm * (1.0 + zm * (1.0 - szm))).astype(BF16)
        dp_ref[:, 2560:3072] = (dud * yd * szd * (1.0 + zd * (1.0 - szd))).astype(BF16)

        wpm, wpd, wout = _tn(ua, dpa), _tn(ud, dpd), _tn(mg, dt)

        @pl.when(pl.program_id(0) == 0)
        def _():
            loss_ref[...] = lpart
            dgp_ref[...] = gpart
            dwpm_ref[...] = wpm
            dwpd_ref[...] = wpd
            dwout_ref[...] = wout

        @pl.when(pl.program_id(0) > 0)
        def _():
            loss_ref[...] += lpart
            dgp_ref[...] += gpart
            dwpm_ref[...] += wpm
            dwpd_ref[...] += wpd
            dwout_ref[...] += wout

    def rows(w):
        return pl.BlockSpec((tm, w), lambda i: (i, 0))

    def full(shape):
        return pl.BlockSpec(shape, lambda i: (0, 0))

    def sds(w, dt):
        return jax.ShapeDtypeStruct((SEQ, w), dt)

    return pl.pallas_call(
        body, name="tail", grid=(SEQ // tm,),
        in_specs=[rows(3072), rows(512), rows(512), rows(512), rows(512), rows(512), rows(512), rows(512),
                  rows(1024), rows(1024), full((512, 1024)), full((512, 1024)), full((1024, 1024)), full((1, 1024))],
        out_specs=[rows(3072), rows(1024), rows(512), rows(512), rows(512), rows(512), full((1, 1024)), full((1, 1024)),
                   full((512, 1024)), full((512, 1024)), full((1024, 1024))],
        out_shape=[sds(N_PAD, BF16), sds(1024, F32), sds(512, F32), sds(512, F32), sds(512, F32), sds(512, F32),
                   jax.ShapeDtypeStruct((1, 1024), F32), jax.ShapeDtypeStruct((1, 1024), F32),
                   jax.ShapeDtypeStruct((512, 1024), F32), jax.ShapeDtypeStruct((512, 1024), F32),
                   jax.ShapeDtypeStruct((1024, 1024), F32)],
        compiler_params=_cparams(),
    )(p, ya, o_g[0], o_g[1], o_g[2], l_g[0], l_g[1], l_g[2], x, target, wpm, wpd, wout, post_g)


def _sum_parts(recv, own, me, tr, name):
    n, r, w = recv.shape
    if r % tr:
        return _sum_parts_cols(recv, own, me, name)
    own_spec = (pl.BlockSpec((tr, w), lambda i, me_ref: (i, 0)) if own.ndim == 2
                else pl.BlockSpec((None, tr, w), lambda i, me_ref: (me_ref[0], i, 0)))

    def body(me_ref, p_ref, own_ref, o_ref):
        mine = own_ref[...].astype(F32)
        acc = jnp.zeros((tr, w), F32)
        for s in range(n):
            acc = acc + jnp.where(me_ref[0] == s, mine, p_ref[s].astype(F32))
        o_ref[...] = acc

    return pl.pallas_call(
        body, name=name,
        grid_spec=pltpu.PrefetchScalarGridSpec(
            num_scalar_prefetch=1, grid=(r // tr,),
            in_specs=[pl.BlockSpec((n, tr, w), lambda i, me_ref: (0, i, 0)), own_spec],
            out_specs=pl.BlockSpec((tr, w), lambda i, me_ref: (i, 0))),
        out_shape=jax.ShapeDtypeStruct((r, w), F32),
    )(me.reshape(1), recv, own)


def _sum_parts_cols(recv, own, me, name):
    n, r, w = recv.shape
    tc = 128

    def body(me_ref, p_ref, own_ref, o_ref):
        mine = own_ref[...].astype(F32)
        acc = jnp.zeros((r, tc), F32)
        for s in range(n):
            acc = acc + jnp.where(me_ref[0] == s, mine, p_ref[s].astype(F32))
        o_ref[...] = acc

    return pl.pallas_call(
        body, name=name,
        grid_spec=pltpu.PrefetchScalarGridSpec(
            num_scalar_prefetch=1, grid=(w // tc,),
            in_specs=[pl.BlockSpec((n, r, tc), lambda i, me_ref: (0, 0, i)),
                      pl.BlockSpec((None, r, tc), lambda i, me_ref: (me_ref[0], 0, i))],
            out_specs=pl.BlockSpec((r, tc), lambda i, me_ref: (0, i))),
        out_shape=jax.ShapeDtypeStruct((r, w), F32),
    )(me.reshape(1), recv, own)


def _adamw(w, g, m, v, name):
    lead = w.shape[:-2]
    r, c = w.shape[-2:]
    tr = max([t for t in range(8, 257, 8) if r % t == 0], default=r)
    c1 = 1.0 - ADAM_B1 ** ADAM_STEP
    c2 = 1.0 - ADAM_B2 ** ADAM_STEP

    def body(w_ref, g_ref, m_ref, v_ref, d_ref, nm_ref, nv_ref):
        gv = g_ref[...]
        nm = ADAM_B1 * m_ref[...] + (1.0 - ADAM_B1) * gv
        nv = ADAM_B2 * v_ref[...] + (1.0 - ADAM_B2) * (gv * gv)
        nm_ref[...] = nm
        nv_ref[...] = nv
        d_ref[...] = -ADAM_LR * ((nm / c1) / (jnp.sqrt(nv / c2) + ADAM_EPS) + ADAM_WD * w_ref[...])

    zeros = (0,) * len(lead)
    spec = pl.BlockSpec((1,) * len(lead) + (tr, c), lambda i: zeros + (i, 0))
    sd = jax.ShapeDtypeStruct(w.shape, F32)
    return pl.pallas_call(
        body, name=name, grid=(r // tr,),
        in_specs=[spec] * 4, out_specs=[spec] * 3, out_shape=[sd] * 3,
    )(w, g, m, v)


def _adamw_recv(w, m, v, recv, own, me, name):
    n, r, c = recv.shape
    tr = 128
    c1 = 1.0 - ADAM_B1 ** ADAM_STEP
    c2 = 1.0 - ADAM_B2 ** ADAM_STEP

    def body(me_ref, w_ref, m_ref, v_ref, p_ref, own_ref, d_ref, nm_ref, nv_ref, g_ref):
        mine = own_ref[...].astype(F32)
        gv = jnp.zeros((tr, c), F32)
        for s in range(n):
            gv = gv + jnp.where(me_ref[0] == s, mine, p_ref[s].astype(F32))
        g_ref[0] = gv
        nm = ADAM_B1 * m_ref[0] + (1.0 - ADAM_B1) * gv
        nv = ADAM_B2 * v_ref[0] + (1.0 - ADAM_B2) * (gv * gv)
        nm_ref[0] = nm
        nv_ref[0] = nv
        d_ref[0] = -ADAM_LR * ((nm / c1) / (jnp.sqrt(nv / c2) + ADAM_EPS) + ADAM_WD * w_ref[0])

    full = pl.BlockSpec((1, tr, c), lambda i, me_ref: (0, i, 0))
    sd = jax.ShapeDtypeStruct((1, r, c), F32)
    return pl.pallas_call(
        body, name=name,
        grid_spec=pltpu.PrefetchScalarGridSpec(
            num_scalar_prefetch=1, grid=(r // tr,),
            in_specs=[full, full, full, pl.BlockSpec((n, tr, c), lambda i, me_ref: (0, i, 0)),
                      pl.BlockSpec((None, tr, c), lambda i, me_ref: (me_ref[0], i, 0))],
            out_specs=[full] * 4),
        out_shape=[sd] * 4,
    )(me.reshape(1), w, m, v, recv, own)


def _adamw_in(w_t, m_t, v_t, own_half, swapped, core):
    r, c = SHARD_SHAPES[0]
    tr = max(t for t in range(8, 257, 8) if r % t == 0)
    c1 = 1.0 - ADAM_B1 ** ADAM_STEP
    c2 = 1.0 - ADAM_B2 ** ADAM_STEP

    def body(core_ref, w_ref, m_ref, v_ref, own_ref, sw_ref, d_ref, nm_ref, nv_ref, g_ref):
        own = own_ref[...]
        col_half = lax.broadcasted_iota(jnp.int32, (tr, c), 1) // (c // 2)
        gv = jnp.where(col_half == core_ref[0], jnp.concatenate([own, own], axis=1), sw_ref[...])
        g_ref[0] = gv
        nm = ADAM_B1 * m_ref[0] + (1.0 - ADAM_B1) * gv
        nv = ADAM_B2 * v_ref[0] + (1.0 - ADAM_B2) * (gv * gv)
        nm_ref[0] = nm
        nv_ref[0] = nv
        d_ref[0] = -ADAM_LR * ((nm / c1) / (jnp.sqrt(nv / c2) + ADAM_EPS) + ADAM_WD * w_ref[0])

    full = pl.BlockSpec((1, tr, c), lambda i, core_ref: (0, i, 0))
    sd = jax.ShapeDtypeStruct((1, r, c), F32)
    return pl.pallas_call(
        body, name="adamw_in",
        grid_spec=pltpu.PrefetchScalarGridSpec(
            num_scalar_prefetch=1, grid=(r // tr,),
            in_specs=[full, full, full, pl.BlockSpec((tr, c // 2), lambda i, core_ref: (i, 0)),
                      pl.BlockSpec((tr, c), lambda i, core_ref: (i, 0))],
            out_specs=[full] * 4),
        out_shape=[sd] * 4,
    )(core.reshape(1), w_t, m_t, v_t, own_half, swapped)


ANY = pl.BlockSpec(memory_space=pl.ANY)


def _my_place():
    return lax.axis_index("x"), lax.axis_index("y"), lax.axis_index("c")


HBM = pl.BlockSpec(memory_space=pltpu.HBM)
SEM = pl.BlockSpec(memory_space=pltpu.SEMAPHORE)
DATAFLOW = pltpu.SideEffectType.DATAFLOW_SIDE_EFFECTING


def _near_chips(x, y):
    return [(1 - x, y), (x, 1 - y)]


def _half(mi, hc):
    r, c = SHARD_SHAPES[mi]
    if mi == 0:
        return pl.ds(0, r), pl.ds(pl.multiple_of(hc * (c // 2), 128), c // 2)
    return pl.ds(pl.multiple_of(hc * (r // 2), 16), r // 2), pl.ds(0, c)


def _gather_copies(land_refs, send_sems, recv_sems):
    x, y, c = _my_place()
    out, back = [], []
    for mi in range(N_MATS):
        rows, cols = _half(mi, c)
        mine = land_refs[mi].at[2 * x + y, rows, cols]
        for j, (cx, cy) in enumerate(_near_chips(x, y)):
            sems = dict(send_sem=send_sems.at[mi * 2 + j], recv_sem=recv_sems.at[mi * 2 + j],
                        device_id=(cx, cy, c), device_id_type=MESH)
            out.append(pltpu.make_async_remote_copy(src_ref=mine, dst_ref=mine, **sems))
            got = land_refs[mi].at[2 * cx + cy, rows, cols]
            back.append(pltpu.make_async_remote_copy(src_ref=got, dst_ref=got, **sems))
    return out, back


def _gather_start(landing):
    n = N_MATS

    def body(*refs):
        out, _ = _gather_copies(refs[:n], refs[n], refs[n + 1])
        for cp in out:
            cp.start()
        refs[-1][...] = jnp.zeros_like(refs[-1])

    hbm = [pltpu.HBM(a.shape, a.dtype) for a in landing]
    outs = pl.pallas_call(
        body, name="gather_start",
        out_shape=(pltpu.SemaphoreType.DMA((2 * n,)), pltpu.SemaphoreType.DMA((2 * n,)), *hbm,
                   jax.ShapeDtypeStruct((8, 128), F32)),
        in_specs=[HBM] * n, out_specs=(SEM, SEM, *[HBM] * n, pl.BlockSpec(memory_space=pltpu.VMEM)),
        input_output_aliases={i: 2 + i for i in range(n)},
        compiler_params=pltpu.CompilerParams(has_side_effects=DATAFLOW),
    )(*[pltpu.with_memory_space_constraint(a, pltpu.HBM) for a in landing])
    return outs[:-1], outs[-1]


def _gather_wait(handle, after):
    n = N_MATS

    def body(*refs):
        out, back = _gather_copies(refs[:n], refs[n], refs[n + 1])
        for cp, arrival in zip(out, back):
            cp.wait_send()
            arrival.wait_recv()

    bufs = handle[2:]
    after, after_specs = _after(after)
    res = pl.pallas_call(
        body, name="gather_wait", out_shape=tuple(pltpu.HBM(b.shape, b.dtype) for b in bufs),
        in_specs=[HBM] * n + [SEM, SEM] + after_specs, out_specs=tuple([HBM] * n),
        input_output_aliases={i: i for i in range(n)},
        compiler_params=pltpu.CompilerParams(has_side_effects=DATAFLOW),
    )(*bufs, handle[0], handle[1], *after)
    return list(res)


def _relay_share(gathered):
    n = N_MATS

    def body(*refs):
        out_refs = refs[n:2 * n]
        send_sems, recv_sems = refs[2 * n:]
        x, y, c = _my_place()
        sibling = (x, y, 1 - c)
        relayed = 2 * (x ^ (1 - c)) + (y ^ c)
        relay_to = (x ^ c, y ^ (1 - c), c)
        far = 2 * (1 - x) + (1 - y)
        near = [2 * (1 - x) + y, 2 * x + (1 - y)]

        def copy(k, mi, shard, hc, to):
            blk = out_refs[mi].at[(shard,) + _half(mi, hc)]
            return pltpu.make_async_remote_copy(src_ref=blk, dst_ref=blk, send_sem=send_sems.at[mi * 4 + k],
                                                recv_sem=recv_sems.at[mi * 4 + k], device_id=to, device_id_type=MESH)

        sends = []
        for mi in range(n):
            sends.append(copy(0, mi, relayed, c, relay_to))
            sends += [copy(1 + j, mi, near[j], c, sibling) for j in range(2)]
        for cp in sends:
            cp.start()
        for mi in range(n):
            copy(0, mi, far, c, relay_to).wait_recv()
            cp = copy(3, mi, far, c, sibling)
            cp.start()
            sends.append(cp)
        for mi in range(n):
            for j in range(2):
                copy(1 + j, mi, near[j], 1 - c, sibling).wait_recv()
            copy(3, mi, far, 1 - c, sibling).wait_recv()
        for cp in sends:
            cp.wait_send()

    return pl.pallas_call(
        body, name="relay_share",
        in_specs=[ANY] * n, out_specs=[ANY] * n,
        out_shape=[jax.ShapeDtypeStruct(g.shape, g.dtype) for g in gathered],
        input_output_aliases={i: i for i in range(n)},
        scratch_shapes=[pltpu.SemaphoreType.DMA((4 * n,)), pltpu.SemaphoreType.DMA((4 * n,))],
    )(*gathered)


def _peers(x, y, c):
    out = []
    for k in range(1, 8):
        px, py, pc = x ^ (k >> 2), y ^ ((k >> 1) & 1), c ^ (k & 1)
        out.append((k - 1, (px, py, pc), 4 * px + 2 * py + pc))
    return out


def _exchange_start(parts, name):
    n = len(parts)

    def body(*refs):
        p_refs, land_refs = refs[:n], refs[n:2 * n]
        send_sems, recv_sems, token = refs[2 * n], refs[2 * n + 1], refs[-1]
        x, y, c = _my_place()
        me = 4 * x + 2 * y + c
        for k, dev, peer in _peers(x, y, c):
            for mi in range(n):
                pltpu.make_async_remote_copy(
                    src_ref=p_refs[mi].at[peer], dst_ref=land_refs[mi].at[me], send_sem=send_sems.at[k * n + mi],
                    recv_sem=recv_sems.at[k * n + mi], device_id=dev, device_id_type=MESH).start()
        token[...] = jnp.zeros_like(token)

    hbm = [pltpu.HBM(p.shape, p.dtype) for p in parts]
    outs = pl.pallas_call(
        body, name=name + "_start",
        out_shape=(pltpu.SemaphoreType.DMA((7 * n,)), pltpu.SemaphoreType.DMA((7 * n,)), *hbm, *hbm,
                   jax.ShapeDtypeStruct((8, 128), F32)),
        in_specs=[HBM] * (2 * n), out_specs=(SEM, SEM, *[HBM] * (2 * n), pl.BlockSpec(memory_space=pltpu.VMEM)),
        input_output_aliases={i: 2 + i for i in range(2 * n)},
        compiler_params=pltpu.CompilerParams(has_side_effects=DATAFLOW),
    )(*[pltpu.with_memory_space_constraint(p, pltpu.HBM) for p in parts],
      *[pltpu.with_memory_space_constraint(lax.empty(p.shape, p.dtype), pltpu.HBM) for p in parts])
    return (name, outs[:-1]), outs[-1]


def _exchange_wait(handle, after):
    name, outs = handle
    n = (len(outs) - 2) // 2

    def body(*refs):
        p_refs, land_refs = refs[:n], refs[n:2 * n]
        send_sems, recv_sems = refs[2 * n], refs[2 * n + 1]
        x, y, c = _my_place()
        me = 4 * x + 2 * y + c
        for k, dev, peer in _peers(x, y, c):
            for mi in range(n):
                pltpu.make_async_remote_copy(
                    src_ref=p_refs[mi].at[peer], dst_ref=land_refs[mi].at[me], send_sem=send_sems.at[k * n + mi],
                    recv_sem=recv_sems.at[k * n + mi], device_id=dev, device_id_type=MESH).wait_send()
                slot = land_refs[mi].at[peer]
                pltpu.make_async_remote_copy(
                    src_ref=slot, dst_ref=slot, send_sem=send_sems.at[k * n + mi],
                    recv_sem=recv_sems.at[k * n + mi], device_id=dev, device_id_type=MESH).wait_recv()

    bufs = outs[2:]
    res = pl.pallas_call(
        body, name=name + "_wait", out_shape=tuple(pltpu.HBM(b.shape, b.dtype) for b in bufs),
        in_specs=[HBM] * (2 * n) + [SEM, SEM, ANY], out_specs=tuple([HBM] * (2 * n)),
        input_output_aliases={i: i for i in range(2 * n)},
        compiler_params=pltpu.CompilerParams(has_side_effects=DATAFLOW),
    )(*bufs, outs[0], outs[1], after)
    return list(res[n:])


def _swap_halves(half_in, gvec):
    def body(g_ref, gv_ref, out_ref, rg_ref, send_sems, recv_sems):
        x, y, c = _my_place()
        me = 4 * x + 2 * y + c
        sibling = (x, y, 1 - c)

        def half(hc):
            return out_ref.at[:, pl.ds(pl.multiple_of(hc * 512, 128), 512)]

        sends = [pltpu.make_async_remote_copy(src_ref=g_ref, dst_ref=half(c), send_sem=send_sems.at[7],
                                              recv_sem=recv_sems.at[7], device_id=sibling, device_id_type=MESH)]
        for k, dev, peer in _peers(x, y, c):
            sends.append(pltpu.make_async_remote_copy(src_ref=gv_ref, dst_ref=rg_ref.at[me], send_sem=send_sems.at[k],
                                                      recv_sem=recv_sems.at[k], device_id=dev, device_id_type=MESH))
        for cp in sends:
            cp.start()
        got = half(1 - c)
        pltpu.make_async_remote_copy(src_ref=got, dst_ref=got, send_sem=send_sems.at[7], recv_sem=recv_sems.at[7],
                                     device_id=sibling, device_id_type=MESH).wait_recv()
        for k, dev, peer in _peers(x, y, c):
            got = rg_ref.at[peer]
            pltpu.make_async_remote_copy(src_ref=got, dst_ref=got, send_sem=send_sems.at[k], recv_sem=recv_sems.at[k],
                                         device_id=dev, device_id_type=MESH).wait_recv()
        for cp in sends:
            cp.wait_send()

    return pl.pallas_call(
        body, name="swap_halves",
        in_specs=[ANY, ANY], out_specs=[ANY, ANY],
        out_shape=[jax.ShapeDtypeStruct(SHARD_SHAPES[0], F32), jax.ShapeDtypeStruct((8, 8, N_GVEC), F32)],
        scratch_shapes=[pltpu.SemaphoreType.DMA((8,)), pltpu.SemaphoreType.DMA((8,))],
    )(half_in, gvec)


def _set_slot(arr, block, idx):
    return lax.dynamic_update_slice(arr, block[None], (idx,) + (0,) * block.ndim)


PAD_RUNS = ((6304, 8352, 0), (5280, 6304, COL_Z), (672, 5280, COL_QKV), (0, 640, COL_LAT), (640, 672, COL_LAT + 704))
N_QKV = COL_LAT - COL_QKV


def _qkv_rows_regroup(a, to_padded):
    if to_padded:
        a4 = a.reshape(3, 12, 128, a.shape[1])
        return jnp.stack([a4[0], a4[1], a4[2]], axis=1).reshape(a.shape)
    a4 = a.reshape(12, 3, 128, a.shape[1])
    return jnp.concatenate([a4[:, tq].reshape(N_QKV // 3, a.shape[1]) for tq in range(3)], axis=0)
W_IN_SHARD = 2088


def _full_weights(gathered):
    def cols(a):
        return jnp.concatenate([a[s] for s in range(4)], axis=1)

    w_uq, w_ukv, w_pm, w_pd = [cols(a) for a in gathered[1:5]]
    w_out = gathered[5].reshape(D_MODEL, D_MODEL)
    w_in_t = gathered[0].reshape(4 * W_IN_SHARD, D_MODEL)
    pieces, at = [], 0
    for lo, hi, pad_lo in sorted(PAD_RUNS, key=lambda t: t[2]):
        if pad_lo > at:
            pieces.append(jnp.zeros((pad_lo - at, D_MODEL), w_in_t.dtype))
        pieces.append(_qkv_rows_regroup(w_in_t[lo:hi], True) if pad_lo == COL_QKV else w_in_t[lo:hi])
        at = pad_lo + hi - lo
    pieces.append(jnp.zeros((N_PAD - at, D_MODEL), w_in_t.dtype))
    w_pad_t = jnp.concatenate(pieces, axis=0)
    z32 = jnp.zeros((Q_RANK, 32), w_uq.dtype)
    wuq_pad = jnp.concatenate([t for h in range(MLA_HEADS) for t in (w_uq[:, h * 96:(h + 1) * 96], z32)], axis=1)
    z64 = jnp.zeros((KV_RANK, 64), w_ukv.dtype)
    wk_pad = jnp.concatenate([t for h in range(MLA_HEADS) for t in (w_ukv[:, h * 128:h * 128 + 64], z64)], axis=1)
    wv = jnp.concatenate([w_ukv[:, h * 128 + 64:(h + 1) * 128] for h in range(MLA_HEADS)], axis=1)
    return w_pad_t, wuq_pad, wk_pad, wv, w_pm, w_pd, w_out


W_IN_LAT = 672


def _grad_parts_in_early(dwt_early):
    dwt_early = jnp.concatenate([dwt_early[:COL_QKV], _qkv_rows_regroup(dwt_early[COL_QKV:COL_LAT], False)], axis=0)

    def in_block(s, h):
        cols = slice(h * 512, (h + 1) * 512)
        out = []
        for lo, hi, pad_lo in sorted(PAD_RUNS):
            a_, b_ = max(lo, s * W_IN_SHARD), min(hi, (s + 1) * W_IN_SHARD)
            if a_ < b_:
                out.append(jnp.zeros((b_ - a_, 512), dwt_early.dtype) if pad_lo >= COL_LAT
                           else dwt_early[pad_lo + a_ - lo:pad_lo + b_ - lo, cols])
        return jnp.concatenate(out, axis=0)

    return jnp.stack([in_block(s, h) for s in range(4) for h in range(2)])


def _grad_parts_in_late(dwt_late):
    rows = jnp.concatenate([dwt_late[0:640], dwt_late[704:736]], axis=0)
    zero = jnp.zeros((W_IN_LAT, 512), dwt_late.dtype)
    return jnp.stack([rows[:, 0:512], rows[:, 512:1024]] + [zero] * 6)


def _shard_blocks(m, axis=1):
    n = m.shape[axis] // 4
    cut = (lambda s: m[:, s * n:(s + 1) * n]) if axis == 1 else (lambda s: m[s * n:(s + 1) * n])
    return jnp.stack([cut(s) for s in range(4) for _ in range(2)])


def _grad_parts_mla(dwuq_pad, dwk_pad, dwv):
    d_uq = jnp.concatenate([dwuq_pad[:, h * 128:h * 128 + 96] for h in range(MLA_HEADS)], axis=1)
    d_ukv = jnp.concatenate([t for h in range(MLA_HEADS) for t in (dwk_pad[:, h * 128:h * 128 + 64], dwv[:, h * 64:(h + 1) * 64])],
                            axis=1)
    return [_shard_blocks(d_uq.astype(BF16)), _shard_blocks(d_ukv.astype(BF16))]


def _rope_tables(positions, token=None):
    pos = positions.reshape(SEQ).astype(F32)
    if token is not None:
        pos = pos + token[0, 0]
    lane = jnp.arange(128)

    def table(rot, first, period):
        inv = ROPE_THETA ** (-jnp.arange(0, rot, 2, dtype=F32) / rot)
        half = rot // 2
        off = lane % period - first
        in1, in2 = (off >= 0) & (off < half), (off >= half) & (off < rot)
        inv_lane = jnp.where(in1 | in2, inv[jnp.clip(off % half, 0, half - 1)], 0.0)
        sign = jnp.where(in1, -1.0, 1.0).astype(F32)
        ang = pos[:, None] * inv_lane[None, :]
        return jnp.cos(ang), jnp.sin(ang) * sign[None, :]

    return table(32, 64, 128), table(16, 0, 64)


class _Links:
    def __init__(self, mats, chip, me):
        landing = [_set_slot(lax.empty((4,) + m.shape, m.dtype), m, chip) for m in mats]
        self.gather, self.token = _gather_start(landing)
        self.me, self.sent, self.handles, self.sums, self.raw = me, {}, {}, {}, {}

    def weights(self, after):
        return _relay_share(_gather_wait(self.gather, after))

    def send(self, blocks, name):
        self.sent[name] = blocks
        self.handles[name], token = _exchange_start(blocks, name)
        return token

    def collect(self, name, after, parts):
        recv = _exchange_wait(self.handles[name], after)
        for r, own, part in zip(recv, self.sent[name], parts):
            if part.startswith("in_"):
                self.sums[part] = _sum_parts(r, own, self.me, 64, "sum_grad_" + part)
            else:
                self.raw[part] = (r, own)
        return tuple(self.sums[part] for part in parts if part in self.sums)


def _device_grads(x, positions, target, gains, links):
    pre_g, q_g, kv_g, post_g = gains
    (mc, ms), (dc, ds) = _rope_tables(positions, links.token)
    h = _prenorm_fwd(x, pre_g, links.token)
    w_pad_t, wuq_pad, wk_pad, wv, w_pm, w_pd, w_out = _full_weights(links.weights((h, mc, ms, dc, ds)))

    p_gz = _matmul(h, w_pad_t, "nt", F32, 1024, 1536, 1024, "in_proj_gates", b_cols=(0, COL_QKV // 1536))
    p_qkv = _matmul(h, w_pad_t, "nt", BF16, 1024, 1536, 1024, "in_proj_dilated", b_cols=(COL_QKV // 1536, N_QKV // 1536),
                    lane_blocks=True)
    p_lat = _matmul(h, w_pad_t, "nt", F32, 1024, N_LAT, 1024, "in_proj_latent", b_cols=(COL_LAT // N_LAT, 1))
    q, k, v = _mla_prep_fwd(p_lat, q_g, kv_g, wuq_pad, wk_pad, wv, mc, ms)
    ya, lse_m = _mla_flash_fwd(q, k, v)
    o_g, l_g, qkv = zip(*[_dil_attn_fwd(p_qkv, dc, ds, g) for g in range(3)])
    (dp, dy, dya, dyd, yd, lse_d, loss_cols, dg_post, dwpm, dwpd, dwout) = _tail(
        p_gz, ya, o_g, l_g, x, target, w_pm, w_pd, w_out, post_g)

    for g in range(3):
        dp = _dil_attn_bwd(dp, qkv[g], dyd, yd, lse_d, dc, ds, g)
    dw_early = _matmul(dp, h, "tn", BF16, 1536, 1024, 2048, "dw_in_early", a_cols=(0, COL_LAT // 1536))
    token = links.send([_grad_parts_in_early(dw_early), _shard_blocks(dwpm.astype(BF16)), _shard_blocks(dwpd.astype(BF16)),
                        _shard_blocks(dwout.astype(BF16), axis=0)], "exchange_early")

    dq, dk, dv = _mla_flash_bwd(q, k, v, ya, dya, lse_m, token)
    dp, dwuq_pad, dwk_pad, dwv, dg_q, dg_kv = _mla_prep_bwd(dp, p_lat, dq, dk, dv, q_g, kv_g, wuq_pad, wk_pad, wv, mc, ms)
    dw_late = _matmul(dp, h, "tn", BF16, N_LAT, 1024, 2048, "dw_in_late", a_cols=(COL_LAT // N_LAT, 1))
    token = links.send([_grad_parts_in_late(dw_late)] + _grad_parts_mla(dwuq_pad, dwk_pad, dwv), "exchange_late")
    early = links.collect("exchange_early", dw_late, ("in_early", "pm", "pd", "out"))

    grad_x, dg_pre = _dh_prenorm_bwd(dp, w_pad_t, x, dy, pre_g, (token,) + tuple(early))
    links.collect("exchange_late", grad_x, ("in_late", "uq", "ukv"))

    loss_part = jnp.pad((jnp.sum(loss_cols) * (0.5 / D_MODEL)).reshape(1, 1), ((0, 0), (0, N_GVEC - N_GAINS - 1)))
    gvec = jnp.concatenate([dg_pre, dg_q, dg_kv, dg_post, loss_part], axis=1)
    return grad_x, gvec


def kernel(x, positions, pre_norm_g, w_in, q_norm_g, w_uq, kv_norm_g, w_ukv, w_proj_mla, w_proj_dil, w_out, post_norm_g, loss_target, m_pre_norm_g, m_w_in, m_q_norm_g, m_w_uq, m_kv_norm_g, m_w_ukv, m_w_proj_mla, m_w_proj_dil, m_w_out, m_post_norm_g, v_pre_norm_g, v_w_in, v_q_norm_g, v_w_uq, v_kv_norm_g, v_w_ukv, v_w_proj_mla, v_w_proj_dil, v_w_out, v_post_norm_g):
    xi, yi, ci = _my_place()
    chip, me = 2 * xi + yi, 4 * xi + 2 * yi + ci
    mats = [jnp.swapaxes(w_in, 1, 2)] + [w_uq, w_ukv, w_proj_mla, w_proj_dil, w_out]
    mats = [w.reshape(w.shape[1:]).astype(BF16) for w in mats]
    links = _Links(mats, chip, me)
    gains = (pre_norm_g, q_norm_g, kv_norm_g, post_norm_g)
    grad_x, gvec = _device_grads(x[0], positions, loss_target[0], gains, links)

    sums = links.sums
    in_e = sums["in_early"]
    half_in = jnp.concatenate([in_e[:W_IN_LAT] + jnp.where(chip == 0, sums["in_late"], 0.0), in_e[W_IN_LAT:]], axis=0)
    gvec8 = jnp.pad(gvec, ((0, 7), (0, 0)))
    swapped_in, recv_gains = _swap_halves(half_in, gvec8)
    g_gains = _sum_parts(recv_gains, gvec8, me, 8, "sum_gain_parts")[0:1]
    loss = g_gains[0, N_GAINS]
    sw = lambda a: jnp.swapaxes(a, 1, 2)
    d_in, m_in, v_in, g_in = [sw(o) for o in _adamw_in(sw(w_in), sw(m_w_in), sw(v_w_in), half_in, swapped_in, ci)]
    off = [0, 1024, 1408, 1664, 2688]
    g_gain = [g_gains[:, off[i]:off[i + 1]] for i in range(4)]
    ws = [pre_norm_g, w_in, q_norm_g, w_uq, kv_norm_g, w_ukv, w_proj_mla, w_proj_dil, w_out, post_norm_g]
    ms = [m_pre_norm_g, m_w_in, m_q_norm_g, m_w_uq, m_kv_norm_g, m_w_ukv, m_w_proj_mla, m_w_proj_dil, m_w_out, m_post_norm_g]
    vs = [v_pre_norm_g, v_w_in, v_q_norm_g, v_w_uq, v_kv_norm_g, v_w_ukv, v_w_proj_mla, v_w_proj_dil, v_w_out, v_post_norm_g]
    part_of = [None, "in", None, "uq", None, "ukv", "pm", "pd", "out", None]
    gain_of = iter(g_gain)
    grads, deltas, new_m, new_v = [], [], [], []
    for i, (w, m, v, part) in enumerate(zip(ws, ms, vs, part_of)):
        if part == "in":
            d_, m_, v_, g = d_in, m_in, v_in, g_in
        elif part is not None:
            d_, m_, v_, g = _adamw_recv(w, m, v, *links.raw[part], me, f"adamw_{i}")
        else:
            g = next(gain_of)
            d_, m_, v_ = _adamw(w, g, m, v, f"adamw_{i}")
        grads.append(g)
        deltas.append(d_)
        new_m.append(m_)
        new_v.append(v_)
    return (loss, grad_x.reshape(x.shape), *grads, *deltas, *new_m, *new_v)
```
